```python
import math
import jax, jax.numpy as jnp
from jax import lax
import numpy as np

D_MODEL = 1024
BATCH = 16
SEQ = 2048
DEPTH = 2

N_EVEN = (DEPTH + 1) // 2
N_ODD = DEPTH // 2

A_WIDTH = D_MODEL // 2
A_EXPAND = 128
A_HEADS = A_WIDTH // A_EXPAND
A_DK = A_EXPAND
A_DV = A_WIDTH // A_HEADS
A_CHUNK = 64
B_WIDTH = D_MODEL - A_WIDTH
B_GROUPS = 4
B_GROUP_DIM = B_WIDTH // B_GROUPS
B_CHUNK = 128
C_HEADS = 16
C_HEAD_DIM = D_MODEL // C_HEADS
C_ROT_DIM = C_HEAD_DIM // 4
ROPE_THETA = 500000.0
C_BRANCHES = ((128, 1), (512, 4), (2048, 16))
C_BLOCK = 128
D_FF = 4 * D_MODEL
EPS = 1e-6

EVEN_IN = 4 * A_WIDTH + 2 * B_WIDTH
ODD_IN = 3 * D_MODEL

kernel_name = "hybrid_hgrn2_gmlp_dilated_attn"

F32 = jnp.float32


def rmsnorm(x, g):
    xf = x.astype(F32)
    y = xf * lax.rsqrt(jnp.mean(xf * xf, axis=-1, keepdims=True) + EPS)
    return (y * g.astype(F32)).astype(x.dtype)


def layernorm(x, g, b):
    xf = x.astype(F32)
    mu = jnp.mean(xf, axis=-1, keepdims=True)
    var = jnp.mean(jnp.square(xf - mu), axis=-1, keepdims=True)
    return ((xf - mu) * lax.rsqrt(var + EPS) * g.astype(F32) + b.astype(F32)).astype(x.dtype)


def hgrn2_mix(q, f_logit, i, g, lb, norm_g):
    b_, s_, _ = q.shape
    n_chunks = s_ // A_CHUNK
    f = lb[None, None, :] + (1.0 - lb[None, None, :]) * jax.nn.sigmoid(f_logit.astype(F32))
    k = 1.0 - f
    logf = jnp.log(f)
    qf = jax.nn.silu(q.astype(F32))

    def chunks(t, d):
        return t.reshape(b_, n_chunks, A_CHUNK, A_HEADS, d).transpose(1, 0, 3, 2, 4)

    xs = (chunks(qf, A_DK), chunks(k, A_DK), chunks(i.astype(F32), A_DV), chunks(logf, A_DK))
    causal = np.tril(np.ones((A_CHUNK, A_CHUNK), dtype=bool))

    def step(state, inp):
        qc, kc, vc, lfc = inp
        G = jnp.cumsum(lfc, axis=2)
        o_inter = jnp.einsum('bhtk,bhkv->bhtv', qc * jnp.exp(G), state)
        diff = G[:, :, :, None, :] - G[:, :, None, :, :]
        decay = jnp.exp(jnp.where(causal[None, None, :, :, None], diff, -jnp.inf))
        scores = jnp.einsum('bhtk,bhsk,bhtsk->bhts', qc, kc, decay)
        o_intra = jnp.einsum('bhts,bhsv->bhtv', scores, vc)
        G_last = G[:, :, -1:, :]
        new_state = jnp.exp(G_last[:, :, 0, :])[..., None] * state + jnp.einsum(
            'bhsk,bhsv->bhkv', kc * jnp.exp(G_last - G), vc)
        return new_state, o_inter + o_intra

    state0 = jnp.zeros((b_, A_HEADS, A_DK, A_DV), F32)
    _, o = lax.scan(step, state0, xs)
    o = o.transpose(1, 0, 3, 2, 4).reshape(b_, s_, A_HEADS, A_DV)
    o = rmsnorm(o, norm_g.reshape(A_HEADS, A_DV))
    o = o.reshape(b_, s_, A_WIDTH) * jax.nn.silu(g.astype(F32))
    return o.astype(q.dtype)


def chunk_gmlp_mix(u, v, ln_g, ln_b, w_s, b_s):
    b_, s_, _ = u.shape
    v = layernorm(v, ln_g, ln_b)
    vb = v.reshape(b_, s_ // B_CHUNK, B_CHUNK, B_GROUPS, B_GROUP_DIM)
    tril = np.tril(np.ones((B_CHUNK, B_CHUNK), dtype=bool))
    w = jnp.where(tril[None], w_s, jnp.zeros_like(w_s))
    mixed = jnp.einsum('gts,bnsgc->bntgc', w, vb) + b_s.T[None, None, :, :, None]
    return u * mixed.reshape(b_, s_, B_WIDTH).astype(u.dtype)


def rope_partial(x, pos):
    half = C_ROT_DIM // 2
    inv = ROPE_THETA ** (-jnp.arange(half, dtype=F32) / half)
    ang = pos[..., None].astype(F32) * inv
    cos, sin = jnp.cos(ang)[:, :, None, :], jnp.sin(ang)[:, :, None, :]
    xf = x.astype(F32)
    x1, x2, xp = xf[..., :half], xf[..., half:C_ROT_DIM], xf[..., C_ROT_DIM:]
    return jnp.concatenate([x1 * cos - x2 * sin, x1 * sin + x2 * cos, xp], axis=-1)


def dilated_branch(q, k, v, window, dilation):
    b_, h_, s_, dh = q.shape
    L = s_ // dilation
    W = window // dilation
    qb_len = min(C_BLOCK, L)
    n_blk = L // qb_len

    def sub(t):
        return t.reshape(b_, h_, L, dilation, dh).transpose(0, 1, 3, 2, 4)

    qs = sub(q).reshape(b_, h_, dilation, n_blk, qb_len, dh)
    pad = ((0, 0), (0, 0), (0, 0), (W, 0), (0, 0))
    kp, vp = jnp.pad(sub(k), pad), jnp.pad(sub(v), pad)
    idx = np.arange(n_blk)[:, None] * qb_len + np.arange(qb_len + W)[None, :]
    kb = jnp.take(kp, idx, axis=3)
    vb = jnp.take(vp, idx, axis=3)
    q_pos = np.arange(n_blk)[:, None] * qb_len + np.arange(qb_len)[None, :]
    k_pos = idx - W
    dist = q_pos[:, :, None] - k_pos[:, None, :]
    mask = (dist >= 0) & (dist <= W) & (k_pos[:, None, :] >= 0)
    s = jnp.einsum('bhrnqd,bhrnkd->bhrnqk', qs, kb) * (1.0 / math.sqrt(dh))
    s = jnp.where(mask, s, -jnp.inf)
    m = jnp.max(s, axis=-1, keepdims=True)
    p = jnp.exp(s - m)
    den = jnp.sum(p, axis=-1, keepdims=True)
    o = jnp.einsum('bhrnqk,bhrnkd->bhrnqd', p, vb) / den

    def unsub(t):
        c = t.shape[-1]
        return t.reshape(b_, h_, dilation, L, c).transpose(0, 1, 3, 2, 4).reshape(b_, h_, s_, c)

    return unsub(o), unsub(m), unsub(den)


def dilated_attention_mix(h, w_in, pos):
    b_, s_, _ = h.shape
    qkv = h @ w_in
    q, k, v = jnp.split(qkv, 3, axis=-1)
    q = rope_partial(q.reshape(b_, s_, C_HEADS, C_HEAD_DIM), pos).transpose(0, 2, 1, 3)
    k = rope_partial(k.reshape(b_, s_, C_HEADS, C_HEAD_DIM), pos).transpose(0, 2, 1, 3)
    v = v.reshape(b_, s_, C_HEADS, C_HEAD_DIM).transpose(0, 2, 1, 3).astype(F32)
    outs, maxes, dens = [], [], []
    for window, dilation in C_BRANCHES:
        o_g, m_g, d_g = dilated_branch(q, k, v, window, dilation)
        outs.append(o_g); maxes.append(m_g); dens.append(d_g)
    m_all = jnp.max(jnp.stack(maxes, 0), axis=0)
    weights = [d_g * jnp.exp(m_g - m_all) for m_g, d_g in zip(maxes, dens)]
    o = sum(w_g * o_g for w_g, o_g in zip(weights, outs)) / sum(weights)
    return o.transpose(0, 2, 1, 3).reshape(b_, s_, D_MODEL).astype(h.dtype)


def _fwd_setup_inputs(seed: int = 0) -> dict:
    key = jax.random.key(seed)
    ks = jax.random.split(key, 20)

    def nrm(k, shape, scale):
        return jax.random.normal(k, shape, F32) * scale

    def gain(k, shape):
        return 1.0 + 0.05 * jax.random.normal(k, shape, F32)

    return {
        'x': nrm(ks[0], (BATCH, SEQ, D_MODEL), 1.0),
        'positions': jnp.broadcast_to(jnp.arange(SEQ, dtype=jnp.int32), (BATCH, SEQ)),
        'norm_mix_pre': gain(ks[1], (DEPTH, D_MODEL)),
        'norm_mix_post': gain(ks[2], (DEPTH, D_MODEL)),
        'norm_ffn_pre': gain(ks[3], (DEPTH, D_MODEL)),
        'norm_ffn_post': gain(ks[4], (DEPTH, D_MODEL)),
        'w_in_even': nrm(ks[5], (N_EVEN, D_MODEL, EVEN_IN), D_MODEL ** -0.5),
        'lb_table': nrm(ks[6], (DEPTH + 1, A_WIDTH), 0.5),
        'a_norm': gain(ks[7], (N_EVEN, A_WIDTH)),
        'b_ln_g': gain(ks[8], (N_EVEN, B_WIDTH)),
        'b_ln_b': nrm(ks[9], (N_EVEN, B_WIDTH), 0.02),
        'b_ws': nrm(ks[10], (N_EVEN, B_GROUPS, B_CHUNK, B_CHUNK), B_CHUNK ** -0.5),
        'b_bias': 1.0 + nrm(ks[11], (N_EVEN, B_GROUPS, B_CHUNK), 0.1),
        'w_out_even': nrm(ks[12], (N_EVEN, A_WIDTH + B_WIDTH, D_MODEL), (A_WIDTH + B_WIDTH) ** -0.5),
        'w_in_odd': nrm(ks[13], (N_ODD, D_MODEL, ODD_IN), D_MODEL ** -0.5),
        'w_out_odd': nrm(ks[14], (N_ODD, D_MODEL, D_MODEL), D_MODEL ** -0.5),
        'w_ff1': nrm(ks[15], (DEPTH, D_MODEL, D_FF), D_MODEL ** -0.5),
        'w_ff2': nrm(ks[16], (DEPTH, D_FF, D_MODEL), D_FF ** -0.5),
    }


def _fwd_reference(x, positions, norm_mix_pre, norm_mix_post, norm_ffn_pre, norm_ffn_post,
              w_in_even, lb_table, a_norm, b_ln_g, b_ln_b, b_ws, b_bias, w_out_even,
              w_in_odd, w_out_odd, w_ff1, w_ff2):
    lb_all = jnp.cumsum(jax.nn.softmax(lb_table.astype(F32), axis=0), axis=0)
    splits = [A_WIDTH, 2 * A_WIDTH, 3 * A_WIDTH, 4 * A_WIDTH, 4 * A_WIDTH + B_WIDTH]
    for l in range(DEPTH):
        h = rmsnorm(x, norm_mix_pre[l])
        if l % 2 == 0:
            e = l // 2
            proj = h @ w_in_even[e]
            qa, fa, ia, ga, ub, vb = jnp.split(proj, splits, axis=-1)
            oa = hgrn2_mix(qa, fa, ia, ga, lb_all[l], a_norm[e])
            ob = chunk_gmlp_mix(jax.nn.gelu(ub), jax.nn.gelu(vb), b_ln_g[e], b_ln_b[e], b_ws[e], b_bias[e])
            mix = jnp.concatenate([oa, ob], axis=-1) @ w_out_even[e]
        else:
            o = l // 2
            mix = dilated_attention_mix(h, w_in_odd[o], positions) @ w_out_odd[o]
        x = x + rmsnorm(mix, norm_mix_post[l])
        h = rmsnorm(x, norm_ffn_pre[l])
        y = jnp.square(jax.nn.relu(h @ w_ff1[l])) @ w_ff2[l]
        x = x + rmsnorm(y, norm_ffn_post[l])
    return x


import jax as _jax
import jax.numpy as _jnp

TWIN_FORMAT = 'train_step'
FWD_PARAMS = ['x', 'positions', 'norm_mix_pre', 'norm_mix_post', 'norm_ffn_pre', 'norm_ffn_post', 'w_in_even', 'lb_table', 'a_norm', 'b_ln_g', 'b_ln_b', 'b_ws', 'b_bias', 'w_out_even', 'w_in_odd', 'w_out_odd', 'w_ff1', 'w_ff2']
TWIN_WEIGHTS = ['norm_mix_pre', 'norm_mix_post', 'norm_ffn_pre', 'norm_ffn_post', 'w_in_even', 'lb_table', 'a_norm', 'b_ln_g', 'b_ln_b', 'b_ws', 'b_bias', 'w_out_even', 'w_in_odd', 'w_out_odd', 'w_ff1', 'w_ff2']
TWIN_DIFF_INPUT = 'x'
TWIN_INPUTS = ['x', 'positions', 'norm_mix_pre', 'norm_mix_post', 'norm_ffn_pre', 'norm_ffn_post', 'w_in_even', 'lb_table', 'a_norm', 'b_ln_g', 'b_ln_b', 'b_ws', 'b_bias', 'w_out_even', 'w_in_odd', 'w_out_odd', 'w_ff1', 'w_ff2', 'loss_target', 'm_norm_mix_pre', 'm_norm_mix_post', 'm_norm_ffn_pre', 'm_norm_ffn_post', 'm_w_in_even', 'm_lb_table', 'm_a_norm', 'm_b_ln_g', 'm_b_ln_b', 'm_b_ws', 'm_b_bias', 'm_w_out_even', 'm_w_in_odd', 'm_w_out_odd', 'm_w_ff1', 'm_w_ff2', 'v_norm_mix_pre', 'v_norm_mix_post', 'v_norm_ffn_pre', 'v_norm_ffn_post', 'v_w_in_even', 'v_lb_table', 'v_a_norm', 'v_b_ln_g', 'v_b_ln_b', 'v_b_ws', 'v_b_bias', 'v_w_out_even', 'v_w_in_odd', 'v_w_out_odd', 'v_w_ff1', 'v_w_ff2']
TWIN_OUTPUTS = ['loss', 'grad_x', 'grad_norm_mix_pre', 'grad_norm_mix_post', 'grad_norm_ffn_pre', 'grad_norm_ffn_post', 'grad_w_in_even', 'grad_lb_table', 'grad_a_norm', 'grad_b_ln_g', 'grad_b_ln_b', 'grad_b_ws', 'grad_b_bias', 'grad_w_out_even', 'grad_w_in_odd', 'grad_w_out_odd', 'grad_w_ff1', 'grad_w_ff2', 'delta_norm_mix_pre', 'delta_norm_mix_post', 'delta_norm_ffn_pre', 'delta_norm_ffn_post', 'delta_w_in_even', 'delta_lb_table', 'delta_a_norm', 'delta_b_ln_g', 'delta_b_ln_b', 'delta_b_ws', 'delta_b_bias', 'delta_w_out_even', 'delta_w_in_odd', 'delta_w_out_odd', 'delta_w_ff1', 'delta_w_ff2', 'new_m_norm_mix_pre', 'new_m_norm_mix_post', 'new_m_norm_ffn_pre', 'new_m_norm_ffn_post', 'new_m_w_in_even', 'new_m_lb_table', 'new_m_a_norm', 'new_m_b_ln_g', 'new_m_b_ln_b', 'new_m_b_ws', 'new_m_b_bias', 'new_m_w_out_even', 'new_m_w_in_odd', 'new_m_w_out_odd', 'new_m_w_ff1', 'new_m_w_ff2', 'new_v_norm_mix_pre', 'new_v_norm_mix_post', 'new_v_norm_ffn_pre', 'new_v_norm_ffn_post', 'new_v_w_in_even', 'new_v_lb_table', 'new_v_a_norm', 'new_v_b_ln_g', 'new_v_b_ln_b', 'new_v_b_ws', 'new_v_b_bias', 'new_v_w_out_even', 'new_v_w_in_odd', 'new_v_w_out_odd', 'new_v_w_ff1', 'new_v_w_ff2']
TWIN_LEAF_KINDS = {'loss': 'loss', 'grad_x': 'grad_x', 'grad_norm_mix_pre': 'grad_w', 'grad_norm_mix_post': 'grad_w', 'grad_norm_ffn_pre': 'grad_w', 'grad_norm_ffn_post': 'grad_w', 'grad_w_in_even': 'grad_w', 'grad_lb_table': 'grad_w', 'grad_a_norm': 'grad_w', 'grad_b_ln_g': 'grad_w', 'grad_b_ln_b': 'grad_w', 'grad_b_ws': 'grad_w', 'grad_b_bias': 'grad_w', 'grad_w_out_even': 'grad_w', 'grad_w_in_odd': 'grad_w', 'grad_w_out_odd': 'grad_w', 'grad_w_ff1': 'grad_w', 'grad_w_ff2': 'grad_w', 'delta_norm_mix_pre': 'delta_w', 'delta_norm_mix_post': 'delta_w', 'delta_norm_ffn_pre': 'delta_w', 'delta_norm_ffn_post': 'delta_w', 'delta_w_in_even': 'delta_w', 'delta_lb_table': 'delta_w', 'delta_a_norm': 'delta_w', 'delta_b_ln_g': 'delta_w', 'delta_b_ln_b': 'delta_w', 'delta_b_ws': 'delta_w', 'delta_b_bias': 'delta_w', 'delta_w_out_even': 'delta_w', 'delta_w_in_odd': 'delta_w', 'delta_w_out_odd': 'delta_w', 'delta_w_ff1': 'delta_w', 'delta_w_ff2': 'delta_w', 'new_m_norm_mix_pre': 'new_m', 'new_m_norm_mix_post': 'new_m', 'new_m_norm_ffn_pre': 'new_m', 'new_m_norm_ffn_post': 'new_m', 'new_m_w_in_even': 'new_m', 'new_m_lb_table': 'new_m', 'new_m_a_norm': 'new_m', 'new_m_b_ln_g': 'new_m', 'new_m_b_ln_b': 'new_m', 'new_m_b_ws': 'new_m', 'new_m_b_bias': 'new_m', 'new_m_w_out_even': 'new_m', 'new_m_w_in_odd': 'new_m', 'new_m_w_out_odd': 'new_m', 'new_m_w_ff1': 'new_m', 'new_m_w_ff2': 'new_m', 'new_v_norm_mix_pre': 'new_v', 'new_v_norm_mix_post': 'new_v', 'new_v_norm_ffn_pre': 'new_v', 'new_v_norm_ffn_post': 'new_v', 'new_v_w_in_even': 'new_v', 'new_v_lb_table': 'new_v', 'new_v_a_norm': 'new_v', 'new_v_b_ln_g': 'new_v', 'new_v_b_ln_b': 'new_v', 'new_v_b_ws': 'new_v', 'new_v_b_bias': 'new_v', 'new_v_w_out_even': 'new_v', 'new_v_w_in_odd': 'new_v', 'new_v_w_out_odd': 'new_v', 'new_v_w_ff1': 'new_v', 'new_v_w_ff2': 'new_v'}


def _forward(args):
    return _fwd_reference(*[args[k] for k in FWD_PARAMS])


def _output_shape():
    out = _jax.eval_shape(lambda: _forward(_fwd_setup_inputs(0)))
    return out.shape, out.dtype

N_MICROBATCH = 1
ADAM_LR = 0.001
ADAM_B1 = 0.9
ADAM_B2 = 0.999
ADAM_EPS = 1e-08
ADAM_WD = 0.01
ADAM_STEP = 10
PER_EXAMPLE_BATCH_AXIS = {'x': 0, 'positions': 0, 'loss_target': 0}
SHARED_INPUTS = []
_WEIGHT_DTYPES = {'norm_mix_pre': _jnp.float32, 'norm_mix_post': _jnp.float32, 'norm_ffn_pre': _jnp.float32, 'norm_ffn_post': _jnp.float32, 'w_in_even': _jnp.float32, 'lb_table': _jnp.float32, 'a_norm': _jnp.float32, 'b_ln_g': _jnp.float32, 'b_ln_b': _jnp.float32, 'b_ws': _jnp.float32, 'b_bias': _jnp.float32, 'w_out_even': _jnp.float32, 'w_in_odd': _jnp.float32, 'w_out_odd': _jnp.float32, 'w_ff1': _jnp.float32, 'w_ff2': _jnp.float32}
MOMENT_SCALE = {'norm_mix_pre': 1.779014e+01, 'norm_mix_post': 4.517317e+01, 'norm_ffn_pre': 1.079608e+01, 'norm_ffn_post': 4.008407e+01, 'w_in_even': 8.073796e-01, 'lb_table': 3.046394e-02, 'a_norm': 1.284801e+00, 'b_ln_g': 6.622178e-01, 'b_ln_b': 1.042267e+00, 'b_ws': 5.022308e-01, 'b_bias': 1.384754e+00, 'w_out_even': 1.886907e+01, 'w_in_odd': 1.344109e+01, 'w_out_odd': 2.307916e+01, 'w_ff1': 5.424466e+00, 'w_ff2': 2.110391e+01}


def _to_microbatches(a, axis):
    t = _jnp.moveaxis(a, axis, 0)
    t = t.reshape((N_MICROBATCH, t.shape[0] // N_MICROBATCH) + t.shape[1:])
    return _jnp.moveaxis(t, 1, axis + 1)


def setup_inputs(seed: int = 0) -> dict:
    inp = _fwd_setup_inputs(seed)
    key = _jax.random.fold_in(_jax.random.key(seed), 7919)
    shape, _ = _output_shape()
    out = dict(inp)
    out["loss_target"] = _jax.random.normal(_jax.random.fold_in(key, 0), shape, _jnp.float32)
    for i, name in enumerate(TWIN_WEIGHTS):
        w = inp[name].astype(_jnp.float32)
        if MOMENT_SCALE is None:
            s = _jnp.sqrt(_jnp.mean(_jnp.square(w)) + 1e-30)
        else:
            s = MOMENT_SCALE[name]
        km, kv = _jax.random.split(_jax.random.fold_in(key, i + 1))
        out[name] = w
        out["m_" + name] = s * _jax.random.normal(km, w.shape, _jnp.float32)
        out["v_" + name] = (s * s) * _jax.random.uniform(kv, w.shape, _jnp.float32, 0.5, 1.5)
    if N_MICROBATCH > 1:
        for name, axis in PER_EXAMPLE_BATCH_AXIS.items():
            out[name] = _to_microbatches(out[name], axis)
    return {'x': out['x'], 'positions': out['positions'], 'norm_mix_pre': out['norm_mix_pre'], 'norm_mix_post': out['norm_mix_post'], 'norm_ffn_pre': out['norm_ffn_pre'], 'norm_ffn_post': out['norm_ffn_post'], 'w_in_even': out['w_in_even'], 'lb_table': out['lb_table'], 'a_norm': out['a_norm'], 'b_ln_g': out['b_ln_g'], 'b_ln_b': out['b_ln_b'], 'b_ws': out['b_ws'], 'b_bias': out['b_bias'], 'w_out_even': out['w_out_even'], 'w_in_odd': out['w_in_odd'], 'w_out_odd': out['w_out_odd'], 'w_ff1': out['w_ff1'], 'w_ff2': out['w_ff2'], 'loss_target': out['loss_target'], 'm_norm_mix_pre': out['m_norm_mix_pre'], 'm_norm_mix_post': out['m_norm_mix_post'], 'm_norm_ffn_pre': out['m_norm_ffn_pre'], 'm_norm_ffn_post': out['m_norm_ffn_post'], 'm_w_in_even': out['m_w_in_even'], 'm_lb_table': out['m_lb_table'], 'm_a_norm': out['m_a_norm'], 'm_b_ln_g': out['m_b_ln_g'], 'm_b_ln_b': out['m_b_ln_b'], 'm_b_ws': out['m_b_ws'], 'm_b_bias': out['m_b_bias'], 'm_w_out_even': out['m_w_out_even'], 'm_w_in_odd': out['m_w_in_odd'], 'm_w_out_odd': out['m_w_out_odd'], 'm_w_ff1': out['m_w_ff1'], 'm_w_ff2': out['m_w_ff2'], 'v_norm_mix_pre': out['v_norm_mix_pre'], 'v_norm_mix_post': out['v_norm_mix_post'], 'v_norm_ffn_pre': out['v_norm_ffn_pre'], 'v_norm_ffn_post': out['v_norm_ffn_post'], 'v_w_in_even': out['v_w_in_even'], 'v_lb_table': out['v_lb_table'], 'v_a_norm': out['v_a_norm'], 'v_b_ln_g': out['v_b_ln_g'], 'v_b_ln_b': out['v_b_ln_b'], 'v_b_ws': out['v_b_ws'], 'v_b_bias': out['v_b_bias'], 'v_w_out_even': out['v_w_out_even'], 'v_w_in_odd': out['v_w_in_odd'], 'v_w_out_odd': out['v_w_out_odd'], 'v_w_ff1': out['v_w_ff1'], 'v_w_ff2': out['v_w_ff2']}


def _loss(weights, diff, rest, loss_target):
    with _jax.named_scope("forward"):
        args = {**rest, TWIN_DIFF_INPUT: diff, **{k: w.astype(_WEIGHT_DTYPES[k]) for k, w in weights.items()}}
        y = _forward(args)
    with _jax.named_scope("loss_head"):
        err = _jnp.square(y.astype(_jnp.float32) - loss_target)
        return 0.5 * _jnp.sum(_jnp.mean(err, axis=-1)) if err.ndim else 0.5 * err


def _adamw(w, g, m, v):
    m = ADAM_B1 * m + (1.0 - ADAM_B1) * g
    v = ADAM_B2 * v + (1.0 - ADAM_B2) * _jnp.square(g)
    m_hat = m / (1.0 - ADAM_B1 ** ADAM_STEP)
    v_hat = v / (1.0 - ADAM_B2 ** ADAM_STEP)
    delta = -ADAM_LR * (m_hat / (_jnp.sqrt(v_hat) + ADAM_EPS) + ADAM_WD * w)
    return delta, m, v


def reference(x, positions, norm_mix_pre, norm_mix_post, norm_ffn_pre, norm_ffn_post, w_in_even, lb_table, a_norm, b_ln_g, b_ln_b, b_ws, b_bias, w_out_even, w_in_odd, w_out_odd, w_ff1, w_ff2, loss_target, m_norm_mix_pre, m_norm_mix_post, m_norm_ffn_pre, m_norm_ffn_post, m_w_in_even, m_lb_table, m_a_norm, m_b_ln_g, m_b_ln_b, m_b_ws, m_b_bias, m_w_out_even, m_w_in_odd, m_w_out_odd, m_w_ff1, m_w_ff2, v_norm_mix_pre, v_norm_mix_post, v_norm_ffn_pre, v_norm_ffn_post, v_w_in_even, v_lb_table, v_a_norm, v_b_ln_g, v_b_ln_b, v_b_ws, v_b_bias, v_w_out_even, v_w_in_odd, v_w_out_odd, v_w_ff1, v_w_ff2):
    given = dict(x=x, positions=positions, norm_mix_pre=norm_mix_pre, norm_mix_post=norm_mix_post, norm_ffn_pre=norm_ffn_pre, norm_ffn_post=norm_ffn_post, w_in_even=w_in_even, lb_table=lb_table, a_norm=a_norm, b_ln_g=b_ln_g, b_ln_b=b_ln_b, b_ws=b_ws, b_bias=b_bias, w_out_even=w_out_even, w_in_odd=w_in_odd, w_out_odd=w_out_odd, w_ff1=w_ff1, w_ff2=w_ff2, loss_target=loss_target, m_norm_mix_pre=m_norm_mix_pre, m_norm_mix_post=m_norm_mix_post, m_norm_ffn_pre=m_norm_ffn_pre, m_norm_ffn_post=m_norm_ffn_post, m_w_in_even=m_w_in_even, m_lb_table=m_lb_table, m_a_norm=m_a_norm, m_b_ln_g=m_b_ln_g, m_b_ln_b=m_b_ln_b, m_b_ws=m_b_ws, m_b_bias=m_b_bias, m_w_out_even=m_w_out_even, m_w_in_odd=m_w_in_odd, m_w_out_odd=m_w_out_odd, m_w_ff1=m_w_ff1, m_w_ff2=m_w_ff2, v_norm_mix_pre=v_norm_mix_pre, v_norm_mix_post=v_norm_mix_post, v_norm_ffn_pre=v_norm_ffn_pre, v_norm_ffn_post=v_norm_ffn_post, v_w_in_even=v_w_in_even, v_lb_table=v_lb_table, v_a_norm=v_a_norm, v_b_ln_g=v_b_ln_g, v_b_ln_b=v_b_ln_b, v_b_ws=v_b_ws, v_b_bias=v_b_bias, v_w_out_even=v_w_out_even, v_w_in_odd=v_w_in_odd, v_w_out_odd=v_w_out_odd, v_w_ff1=v_w_ff1, v_w_ff2=v_w_ff2)
    weights = {n: given[n] for n in TWIN_WEIGHTS}
    shared = {n: given[n] for n in SHARED_INPUTS}
    per_example = {n: given[n] for n in ['x', 'positions']}
    grad_fn = _jax.value_and_grad(_loss, argnums=(0, 1))

    def one_microbatch(ex, loss_target):
        ex = dict(ex)
        diff = ex.pop(TWIN_DIFF_INPUT)
        return grad_fn(weights, diff, {**shared, **ex}, loss_target)

    if N_MICROBATCH == 1:
        loss, (grad_w, grad_x) = one_microbatch(per_example, given["loss_target"])
    else:
        def body(carry, xs):
            loss_sum, grad_sum = carry
            l_k, (gw_k, gx_k) = one_microbatch(xs[0], xs[1])
            with _jax.named_scope("update"):
                return (loss_sum + l_k, _jax.tree.map(_jnp.add, grad_sum, gw_k)), gx_k

        init = (_jnp.zeros((), _jnp.float32), _jax.tree.map(_jnp.zeros_like, weights))
        (loss, grad_w), grad_x = _jax.lax.scan(body, init, (per_example, given["loss_target"]))
    with _jax.named_scope("update"):
        delta_w, new_m, new_v = {}, {}, {}
        for n in TWIN_WEIGHTS:
            delta_w[n], new_m[n], new_v[n] = _adamw(weights[n], grad_w[n], given["m_" + n], given["v_" + n])
    return (loss, grad_x, *[grad_w[n] for n in TWIN_WEIGHTS], *[delta_w[n] for n in TWIN_WEIGHTS],
            *[new_m[n] for n in TWIN_WEIGHTS], *[new_v[n] for n in TWIN_WEIGHTS])
```

```python
import functools
import math

import jax
import jax.numpy as jnp
from jax import lax
from jax.experimental import pallas as pl
from jax.experimental.pallas import tpu as pltpu

F32 = jnp.float32
BF16 = jnp.bfloat16
MESH = pl.DeviceIdType.MESH

D_MODEL = 1024
SEQ = 2048
D_FF = 4096
N_CHIPS = 4
A_WIDTH = 512
A_HEADS = 4
A_DK = 128
A_CHUNK = 64
A_SUB = 16
B_WIDTH = 512
B_GROUPS = 4
B_CHUNK = 128
C_HEADS = 16
C_HEAD_DIM = 64
C_ROT_HALF = 8
C_BLOCK = 128
C_DILATIONS = (1, 4, 16)
ROPE_THETA = 500000.0
EPS = 1e-6
ADAM_LR = 0.001
ADAM_B1 = 0.9
ADAM_B2 = 0.999
ADAM_EPS = 1e-08
ADAM_WD = 0.01
ADAM_STEP = 10

ROW_TILE = 512
VMEM_LIMIT = 56 * 1024 * 1024
NEG_BIG = -1e30


def _params(sem=None):
    return pltpu.CompilerParams(dimension_semantics=sem, vmem_limit_bytes=VMEM_LIMIT)


def _dot(a, b):
    return jnp.dot(a, b, preferred_element_type=F32)


def _dot_nt(a, b):
    return lax.dot_general(a, b, (((1,), (1,)), ((), ())), preferred_element_type=F32)


def _dot_tn(a, b):
    return lax.dot_general(a, b, (((0,), (0,)), ((), ())), preferred_element_type=F32)


def _rms(x, g):
    r = lax.rsqrt(jnp.mean(x * x, axis=-1, keepdims=True) + EPS)
    return x * r * g


def _rms_bwd(x, g, dy):
    r = lax.rsqrt(jnp.mean(x * x, axis=-1, keepdims=True) + EPS)
    xh = x * r
    dg = jnp.sum(dy * xh, axis=0, keepdims=True)
    dxh = dy * g
    dx = r * (dxh - xh * jnp.mean(dxh * xh, axis=-1, keepdims=True))
    return dx, dg


def _accumulate(ref, val, first):
    @pl.when(first)
    def _():
        ref[...] = val

    @pl.when(jnp.logical_not(first))
    def _():
        ref[...] += val


def norm_matmul(x, g, wg, name):
    t, d = x.shape
    nl = wg.shape[2]

    def body(x_ref, g_ref, w_ref, o_ref, h_ref):
        @pl.when(pl.program_id(1) == 0)
        def _():
            h_ref[...] = _rms(x_ref[...], g_ref[...]).astype(BF16)

        o_ref[...] = _dot(h_ref[...], w_ref[...])

    return pl.pallas_call(
        body, name=name, grid=(t // ROW_TILE, N_CHIPS),
        in_specs=[pl.BlockSpec((ROW_TILE, d), lambda i, c: (i, 0)),
                  pl.BlockSpec((1, d), lambda i, c: (0, 0)),
                  pl.BlockSpec((None, d, nl), lambda i, c: (c, 0, 0))],
        out_specs=[pl.BlockSpec((ROW_TILE, nl), lambda i, c: (i, c)),
                   pl.BlockSpec((ROW_TILE, d), lambda i, c: (i, 0))],
        out_shape=[jax.ShapeDtypeStruct((t, N_CHIPS * nl), F32), jax.ShapeDtypeStruct((t, d), BF16)],
        compiler_params=_params(("arbitrary", "arbitrary")),
    )(x, g, wg)


def norm_matmul_bwd(dproj, wg, x, g, dres, name):
    t, d = x.shape
    nl = wg.shape[2]

    def body(dp_ref, w_ref, x_ref, g_ref, dres_ref, dx_ref, dg_ref, acc):
        i, c = pl.program_id(0), pl.program_id(1)
        part = _dot_nt(dp_ref[...].astype(BF16), w_ref[...])
        _accumulate(acc, part, c == 0)

        @pl.when(c == N_CHIPS - 1)
        def _():
            dx, dg = _rms_bwd(x_ref[...], g_ref[...], acc[...])
            dx_ref[...] = dres_ref[...] + dx
            _accumulate(dg_ref, dg, i == 0)

    return pl.pallas_call(
        body, name=name, grid=(t // ROW_TILE, N_CHIPS),
        in_specs=[pl.BlockSpec((ROW_TILE, nl), lambda i, c: (i, c)),
                  pl.BlockSpec((None, d, nl), lambda i, c: (c, 0, 0)),
                  pl.BlockSpec((ROW_TILE, d), lambda i, c: (i, 0)),
                  pl.BlockSpec((1, d), lambda i, c: (0, 0)),
                  pl.BlockSpec((ROW_TILE, d), lambda i, c: (i, 0))],
        out_specs=[pl.BlockSpec((ROW_TILE, d), lambda i, c: (i, 0)),
                   pl.BlockSpec((1, d), lambda i, c: (0, 0))],
        out_shape=[jax.ShapeDtypeStruct((t, d), F32), jax.ShapeDtypeStruct((1, d), F32)],
        scratch_shapes=[pltpu.VMEM((ROW_TILE, d), F32)],
        compiler_params=_params(("arbitrary", "arbitrary")),
    )(dproj, wg, x, g, dres)


def out_proj(a, wg, x, g, name):
    t, d = x.shape
    kl = wg.shape[1]

    def body(a_ref, w_ref, x_ref, g_ref, mix_ref, xo_ref):
        acc = _dot(a_ref[:, 0:kl], w_ref[0])
        for c in range(1, N_CHIPS):
            acc += _dot(a_ref[:, c * kl:(c + 1) * kl], w_ref[c])
        mix_ref[...] = acc
        xo_ref[...] = x_ref[...] + _rms(acc, g_ref[...])

    row = pl.BlockSpec((ROW_TILE, d), lambda i: (i, 0))
    return pl.pallas_call(
        body, name=name, grid=(t // ROW_TILE,),
        in_specs=[row, pl.BlockSpec((N_CHIPS, kl, d), lambda i: (0, 0, 0)), row,
                  pl.BlockSpec((1, d), lambda i: (0, 0))],
        out_specs=[row, row],
        out_shape=[jax.ShapeDtypeStruct((t, d), F32), jax.ShapeDtypeStruct((t, d), F32)],
        compiler_params=_params(("arbitrary",)),
    )(a, wg, x, g)


def out_proj_bwd(dxo, mix, g, wg, name):
    t, d = mix.shape
    kl = wg.shape[1]

    def body(dxo_ref, mix_ref, g_ref, w_ref, dmix_ref, da_ref, dg_ref):
        dmix, dg = _rms_bwd(mix_ref[...], g_ref[...], dxo_ref[...])
        dmb = dmix.astype(BF16)
        dmix_ref[...] = dmb
        for c in range(N_CHIPS):
            da_ref[:, c * kl:(c + 1) * kl] = _dot_nt(dmb, w_ref[c])
        _accumulate(dg_ref, dg, pl.program_id(0) == 0)

    row = pl.BlockSpec((ROW_TILE, d), lambda i: (i, 0))
    vec = pl.BlockSpec((1, d), lambda i: (0, 0))
    return pl.pallas_call(
        body, name=name, grid=(t // ROW_TILE,),
        in_specs=[row, row, vec, pl.BlockSpec((N_CHIPS, kl, d), lambda i: (0, 0, 0))],
        out_specs=[row, row, vec],
        out_shape=[jax.ShapeDtypeStruct((t, d), BF16), jax.ShapeDtypeStruct((t, d), F32),
                   jax.ShapeDtypeStruct((1, d), F32)],
        compiler_params=_params(("arbitrary",)),
    )(dxo, mix, g, wg)


def ffn_fwd(x, gpre, w1g, w2g, gpost, layer, name):
    t, d = x.shape
    hc = w1g.shape[3]

    def body(x_ref, gpre_ref, w1_ref, w2_ref, gpost_ref, xo_ref, h_ref, a_ref, y_ref, acc):
        c = pl.program_id(1)

        @pl.when(c == 0)
        def _():
            h_ref[...] = _rms(x_ref[...], gpre_ref[...]).astype(BF16)

        a = _dot(h_ref[...], w1_ref[...])
        a_ref[...] = a.astype(BF16)
        r = jnp.square(jnp.maximum(a, 0.0)).astype(BF16)
        _accumulate(acc, _dot(r, w2_ref[...]), c == 0)

        @pl.when(c == N_CHIPS - 1)
        def _():
            y = acc[...]
            y_ref[...] = y
            xo_ref[...] = x_ref[...] + _rms(y, gpost_ref[...])

    row = pl.BlockSpec((ROW_TILE, d), lambda i, c: (i, 0))
    vec = pl.BlockSpec((1, d), lambda i, c: (0, 0))
    return pl.pallas_call(
        body, name=name, grid=(t // ROW_TILE, N_CHIPS),
        in_specs=[row, vec,
                  pl.BlockSpec((None, None, d, hc), lambda i, c: (c, layer, 0, 0)),
                  pl.BlockSpec((None, None, hc, d), lambda i, c: (c, layer, 0, 0)), vec],
        out_specs=[row, row, pl.BlockSpec((ROW_TILE, hc), lambda i, c: (i, c)), row],
        out_shape=[jax.ShapeDtypeStruct((t, d), F32), jax.ShapeDtypeStruct((t, d), BF16),
                   jax.ShapeDtypeStruct((t, N_CHIPS * hc), BF16), jax.ShapeDtypeStruct((t, d), F32)],
        scratch_shapes=[pltpu.VMEM((ROW_TILE, d), F32)],
        compiler_params=_params(("arbitrary", "arbitrary")),
    )(x, gpre, w1g, w2g, gpost)


def ffn_bwd(dxo, x, y, a, gpre, gpost, w1g, w2g, layer, name):
    t, d = x.shape
    hc = w1g.shape[3]

    def body(dxo_ref, x_ref, y_ref, a_ref, gpre_ref, gpost_ref, w1_ref, w2_ref,
             dxi_ref, dy_ref, da_ref, dgpre_ref, dgpost_ref, acc):
        i, c = pl.program_id(0), pl.program_id(1)

        @pl.when(c == 0)
        def _():
            dy, dg = _rms_bwd(y_ref[...], gpost_ref[...], dxo_ref[...])
            dy_ref[...] = dy.astype(BF16)
            _accumulate(dgpost_ref, dg, i == 0)

        dr = _dot_nt(dy_ref[...], w2_ref[...])
        da = (dr * (2.0 * jnp.maximum(a_ref[...].astype(F32), 0.0))).astype(BF16)
        da_ref[...] = da
        _accumulate(acc, _dot_nt(da, w1_ref[...]), c == 0)

        @pl.when(c == N_CHIPS - 1)
        def _():
            dx, dg = _rms_bwd(x_ref[...], gpre_ref[...], acc[...])
            dxi_ref[...] = dxo_ref[...] + dx
            _accumulate(dgpre_ref, dg, i == 0)

    row = pl.BlockSpec((ROW_TILE, d), lambda i, c: (i, 0))
    vec = pl.BlockSpec((1, d), lambda i, c: (0, 0))
    hid = pl.BlockSpec((ROW_TILE, hc), lambda i, c: (i, c))
    return pl.pallas_call(
        body, name=name, grid=(t // ROW_TILE, N_CHIPS),
        in_specs=[row, row, row, hid, vec, vec,
                  pl.BlockSpec((None, None, d, hc), lambda i, c: (c, layer, 0, 0)),
                  pl.BlockSpec((None, None, hc, d), lambda i, c: (c, layer, 0, 0))],
        out_specs=[row, row, hid, vec, vec],
        out_shape=[jax.ShapeDtypeStruct((t, d), F32), jax.ShapeDtypeStruct((t, d), BF16),
                   jax.ShapeDtypeStruct((t, N_CHIPS * hc), BF16),
                   jax.ShapeDtypeStruct((1, d), F32), jax.ShapeDtypeStruct((1, d), F32)],
        scratch_shapes=[pltpu.VMEM((ROW_TILE, d), F32)],
        compiler_params=_params(("arbitrary", "arbitrary")),
    )(dxo, x, y, a, gpre, gpost, w1g, w2g)


def weight_grad(a, b, chunked, bk, bn, relu2, name, layer=None, into=None):
    t = a.shape[0]
    a_on = chunked == "a"
    n_steps = t // ROW_TILE

    def body(a_ref, b_ref, *rest):
        o_ref, acc = rest[-2], rest[-1]
        s = pl.program_id(1)
        av = a_ref[...]
        if relu2:
            av = jnp.square(jnp.maximum(av.astype(F32), 0.0))
        _accumulate(acc, _dot_tn(av.astype(BF16), b_ref[...].astype(BF16)), s == 0)

        @pl.when(s == n_steps - 1)
        def _():
            o_ref[...] = acc[...].astype(BF16)

    in_specs = [pl.BlockSpec((ROW_TILE, bk), (lambda c, s: (s, c)) if a_on else (lambda c, s: (s, 0))),
                pl.BlockSpec((ROW_TILE, bn), (lambda c, s: (s, 0)) if a_on else (lambda c, s: (s, c)))]
    args = [a, b]
    aliases = {}
    if layer is None:
        out_spec = pl.BlockSpec((None, bk, bn), lambda c, s: (c, 0, 0))
        out_shape = jax.ShapeDtypeStruct((N_CHIPS, bk, bn), BF16)
    else:
        out_spec = pl.BlockSpec((None, None, bk, bn), lambda c, s: (c, layer, 0, 0))
        out_shape = jax.ShapeDtypeStruct((N_CHIPS, 2, bk, bn), BF16)
        if into is not None:
            in_specs.append(pl.BlockSpec(memory_space=pl.ANY))
            args.append(into)
            aliases = {2: 0}
    return pl.pallas_call(
        body, name=name, grid=(N_CHIPS, n_steps),
        in_specs=in_specs, out_specs=out_spec, out_shape=out_shape,
        scratch_shapes=[pltpu.VMEM((bk, bn), F32)],
        input_output_aliases=aliases,
        compiler_params=_params(("arbitrary", "arbitrary")),
    )(*args)


def loss_grad(xf, target, name):
    t, d = xf.shape

    def body(x_ref, t_ref, dy_ref, l_ref):
        e = x_ref[...] - t_ref[...]
        dy_ref[...] = e * (1.0 / d)
        part = jnp.sum(jnp.sum(e * e, axis=-1, keepdims=True), axis=0, keepdims=True) * (0.5 / d)
        _accumulate(l_ref, part, pl.program_id(0) == 0)

    row = pl.BlockSpec((ROW_TILE, d), lambda i: (i, 0))
    return pl.pallas_call(
        body, name=name, grid=(t // ROW_TILE,),
        in_specs=[row, row],
        out_specs=[row, pl.BlockSpec((1, 1), lambda i: (0, 0))],
        out_shape=[jax.ShapeDtypeStruct((t, d), F32), jax.ShapeDtypeStruct((1, 1), F32)],
        compiler_params=_params(("arbitrary",)),
    )(xf, target)


def _hgrn2_chunk(st, qs, fls, ivs, gls, l0, l1, l2, ng):
    nsub = len(qs)
    mx = jnp.maximum(jnp.maximum(l0, l1), l2)
    e0, e1, e2 = jnp.exp(l0 - mx), jnp.exp(l1 - mx), jnp.exp(l2 - mx)
    lb = e0 / (e0 + e1 + e2)
    rows = lax.broadcasted_iota(jnp.int32, (A_SUB, A_SUB), 0)
    cols = lax.broadcasted_iota(jnp.int32, (A_SUB, A_SUB), 1)
    tri = (rows >= cols).astype(F32)
    keep = (lax.broadcasted_iota(jnp.int32, (A_SUB, A_SUB, A_DK), 0)
            >= lax.broadcasted_iota(jnp.int32, (A_SUB, A_SUB, A_DK), 1))
    base = jnp.zeros_like(l0)
    bases, gs, ks, qfs = [], [], [], []
    for i in range(nsub):
        f = lb + (1.0 - lb) * jax.nn.sigmoid(fls[i])
        logf = jnp.log(f)
        bases.append(base)
        gs.append(base + jnp.dot(tri, logf, precision=lax.Precision.HIGHEST, preferred_element_type=F32))
        base = base + jnp.sum(logf, axis=0, keepdims=True)
        ks.append(1.0 - f)
        qfs.append(jax.nn.silu(qs[i]))
    g_last = base
    stb = st.astype(BF16)
    outs = []
    for i in range(nsub):
        o = _dot_nt((qfs[i] * jnp.exp(gs[i])).astype(BF16), stb)
        if i > 0:
            qt = (qfs[i] * jnp.exp(gs[i] - bases[i])).astype(BF16)
            kk = jnp.concatenate([ks[j] * jnp.exp(bases[i] - gs[j]) for j in range(i)], axis=0).astype(BF16)
            vv = jnp.concatenate(ivs[:i], axis=0).astype(BF16)
            o = o + _dot(_dot_nt(qt, kk).astype(BF16), vv)
        dec = jnp.exp(jnp.where(keep, gs[i][:, None, :] - gs[i][None, :, :], NEG_BIG))
        s_diag = jnp.sum(qfs[i][:, None, :] * ks[i][None, :, :] * dec, axis=-1)
        o = o + _dot(s_diag.astype(BF16), ivs[i].astype(BF16))
        o = o * lax.rsqrt(jnp.mean(o * o, axis=-1, keepdims=True) + EPS) * ng
        outs.append(o * jax.nn.silu(gls[i]))
    kdec = jnp.concatenate([ks[j] * jnp.exp(g_last - gs[j]) for j in range(nsub)], axis=0).astype(BF16)
    vall = jnp.concatenate(ivs, axis=0).astype(BF16)
    new_st = st * jnp.exp(g_last) + _dot_tn(vall, kdec)
    return new_st, outs


def _sub_blocks(ref, head):
    lanes = slice(head * A_DK, (head + 1) * A_DK)
    return [ref[i * A_SUB:(i + 1) * A_SUB, lanes] for i in range(A_CHUNK // A_SUB)]


def hgrn2_fwd(proj, lb_table, a_norm, batch, name):
    t = proj.shape[0]
    n_chunks = t // batch // A_CHUNK
    nblk = A_WIDTH // A_DK

    def body(q_ref, f_ref, i_ref, g_ref, lb_ref, ng_ref, o_ref, st_ref, st):
        @pl.when(pl.program_id(1) == 0)
        def _():
            st[...] = jnp.zeros_like(st)

        st_ref[...] = st[...]
        for h in range(A_HEADS):
            lanes = slice(h * A_DK, (h + 1) * A_DK)
            new_st, outs = _hgrn2_chunk(
                st[h], _sub_blocks(q_ref, h), _sub_blocks(f_ref, h), _sub_blocks(i_ref, h), _sub_blocks(g_ref, h),
                lb_ref[0:1, lanes], lb_ref[1:2, lanes], lb_ref[2:3, lanes], ng_ref[:, lanes])
            st[h] = new_st
            for i, o in enumerate(outs):
                o_ref[i * A_SUB:(i + 1) * A_SUB, lanes] = o.astype(BF16)

    def part(k):
        return pl.BlockSpec((A_CHUNK, A_WIDTH), lambda b, n: (b * n_chunks + n, k))

    return pl.pallas_call(
        body, name=name, grid=(batch, n_chunks),
        in_specs=[part(0), part(1), part(2), part(3),
                  pl.BlockSpec((3, A_WIDTH), lambda b, n: (0, 0)), pl.BlockSpec((1, A_WIDTH), lambda b, n: (0, 0))],
        out_specs=[pl.BlockSpec((A_CHUNK, A_WIDTH), lambda b, n: (b * n_chunks + n, 0)),
                   pl.BlockSpec((None, A_HEADS, A_DK, A_DK), lambda b, n: (b * n_chunks + n, 0, 0, 0))],
        out_shape=[jax.ShapeDtypeStruct((t, A_WIDTH), BF16),
                   jax.ShapeDtypeStruct((t // A_CHUNK, A_HEADS, A_DK, A_DK), F32)],
        scratch_shapes=[pltpu.VMEM((A_HEADS, A_DK, A_DK), F32)],
        compiler_params=_params(("arbitrary", "arbitrary")),
    )(proj, proj, proj, proj, lb_table, a_norm)


def hgrn2_bwd(proj, states, lb_table, a_norm, do, batch, name):
    t = proj.shape[0]
    n_chunks = t // batch // A_CHUNK

    def body(q_ref, f_ref, i_ref, g_ref, st_ref, lb_ref, ng_ref, do_ref, dp_ref, dlb_ref, dng_ref, dst):
        @pl.when(jnp.logical_and(pl.program_id(0) == 0, pl.program_id(1) == 0))
        def _():
            dlb_ref[...] = jnp.zeros_like(dlb_ref)
            dng_ref[...] = jnp.zeros_like(dng_ref)

        @pl.when(pl.program_id(1) == 0)
        def _():
            dst[...] = jnp.zeros_like(dst)

        for h in range(A_HEADS):
            lanes = slice(h * A_DK, (h + 1) * A_DK)
            _, vjp = jax.vjp(
                _hgrn2_chunk, st_ref[h], _sub_blocks(q_ref, h), _sub_blocks(f_ref, h), _sub_blocks(i_ref, h),
                _sub_blocks(g_ref, h), lb_ref[0:1, lanes], lb_ref[1:2, lanes], lb_ref[2:3, lanes], ng_ref[:, lanes])
            douts = [x.astype(F32) for x in _sub_blocks(do_ref, h)]
            d_st, dqs, dfs, dis, dgs, dl0, dl1, dl2, dng = vjp((dst[h], douts))
            dst[h] = d_st
            for k, parts in enumerate((dqs, dfs, dis, dgs)):
                for i in range(A_CHUNK // A_SUB):
                    dp_ref[i * A_SUB:(i + 1) * A_SUB, k * A_WIDTH + h * A_DK:k * A_WIDTH + (h + 1) * A_DK] = parts[i]
            for row, val in enumerate((dl0, dl1, dl2)):
                dlb_ref[row:row + 1, lanes] += val
            dng_ref[:, lanes] += dng

    def rev(b, n):
        return b * n_chunks + (n_chunks - 1 - n)

    def part(k):
        return pl.BlockSpec((A_CHUNK, A_WIDTH), lambda b, n: (rev(b, n), k))

    const3 = pl.BlockSpec((3, A_WIDTH), lambda b, n: (0, 0))
    const1 = pl.BlockSpec((1, A_WIDTH), lambda b, n: (0, 0))
    return pl.pallas_call(
        body, name=name, grid=(batch, n_chunks),
        in_specs=[part(0), part(1), part(2), part(3),
                  pl.BlockSpec((None, A_HEADS, A_DK, A_DK), lambda b, n: (rev(b, n), 0, 0, 0)),
                  const3, const1, part(0)],
        out_specs=[pl.BlockSpec((A_CHUNK, 4 * A_WIDTH), lambda b, n: (rev(b, n), 0)), const3, const1],
        out_shape=[jax.ShapeDtypeStruct((t, 4 * A_WIDTH + 2 * B_WIDTH), F32),
                   jax.ShapeDtypeStruct((3, A_WIDTH), F32), jax.ShapeDtypeStruct((1, A_WIDTH), F32)],
        scratch_shapes=[pltpu.VMEM((A_HEADS, A_DK, A_DK), F32)],
        compiler_params=_params(("arbitrary", "arbitrary")),
    )(proj, proj, proj, proj, states, lb_table, a_norm, do)


B_GDIM = B_WIDTH // B_GROUPS
B_ROWS = 512


def _gmlp_chunk(ubs, vbs, lngs, lnbs, ws, bcols):
    vs = [jax.nn.gelu(v) for v in vbs]
    mu = sum(jnp.sum(v, axis=-1, keepdims=True) for v in vs) * (1.0 / B_WIDTH)
    var = sum(jnp.sum(jnp.square(v - mu), axis=-1, keepdims=True) for v in vs) * (1.0 / B_WIDTH)
    rstd = lax.rsqrt(var + EPS)
    tril = (lax.broadcasted_iota(jnp.int32, (B_CHUNK, B_CHUNK), 0)
            >= lax.broadcasted_iota(jnp.int32, (B_CHUNK, B_CHUNK), 1))
    outs = []
    for g in range(B_GROUPS):
        vn = (vs[g] - mu) * rstd * lngs[g] + lnbs[g]
        w = jnp.where(tril, ws[g], 0.0).astype(BF16)
        outs.append(jax.nn.gelu(ubs[g]) * (_dot(w, vn.astype(BF16)) + bcols[g]))
    return outs


def _gmlp_args(u_ref, v_ref, lng_ref, lnb_ref, w_ref, bt_ref, rows):
    def groups(ref):
        return [ref[rows, g * B_GDIM:(g + 1) * B_GDIM] for g in range(B_GROUPS)]

    def vec(ref):
        return [ref[:, g * B_GDIM:(g + 1) * B_GDIM] for g in range(B_GROUPS)]

    return (groups(u_ref), groups(v_ref), vec(lng_ref), vec(lnb_ref),
            [w_ref[g] for g in range(B_GROUPS)], [bt_ref[:, g:g + 1] for g in range(B_GROUPS)])


def gmlp_fwd(proj, oa, ln_g, ln_b, w, bias_t, name):
    t = proj.shape[0]

    def body(u_ref, v_ref, oa_ref, lng_ref, lnb_ref, w_ref, bt_ref, o_ref):
        o_ref[:, 0:A_WIDTH] = oa_ref[...]
        for n in range(B_ROWS // B_CHUNK):
            rows = slice(n * B_CHUNK, (n + 1) * B_CHUNK)
            outs = _gmlp_chunk(*_gmlp_args(u_ref, v_ref, lng_ref, lnb_ref, w_ref, bt_ref, rows))
            for g, o in enumerate(outs):
                o_ref[rows, A_WIDTH + g * B_GDIM:A_WIDTH + (g + 1) * B_GDIM] = o.astype(BF16)

    vec = pl.BlockSpec((1, B_WIDTH), lambda i: (0, 0))
    return pl.pallas_call(
        body, name=name, grid=(t // B_ROWS,),
        in_specs=[pl.BlockSpec((B_ROWS, B_WIDTH), lambda i: (i, 4)), pl.BlockSpec((B_ROWS, B_WIDTH), lambda i: (i, 5)),
                  pl.BlockSpec((B_ROWS, A_WIDTH), lambda i: (i, 0)), vec, vec,
                  pl.BlockSpec((B_GROUPS, B_CHUNK, B_CHUNK), lambda i: (0, 0, 0)),
                  pl.BlockSpec((B_CHUNK, B_GROUPS), lambda i: (0, 0))],
        out_specs=pl.BlockSpec((B_ROWS, A_WIDTH + B_WIDTH), lambda i: (i, 0)),
        out_shape=jax.ShapeDtypeStruct((t, A_WIDTH + B_WIDTH), BF16),
        compiler_params=_params(("arbitrary",)),
    )(proj, proj, oa, ln_g, ln_b, w, bias_t)


def gmlp_bwd(proj, dmixin, ln_g, ln_b, w, bias_t, dproj, name):
    t = proj.shape[0]

    def body(u_ref, v_ref, do_ref, lng_ref, lnb_ref, w_ref, bt_ref, dp_in_ref,
             dp_ref, dlng_ref, dlnb_ref, dw_ref, dbt_ref):
        del dp_in_ref

        @pl.when(pl.program_id(0) == 0)
        def _():
            for ref in (dlng_ref, dlnb_ref, dw_ref, dbt_ref):
                ref[...] = jnp.zeros_like(ref)

        for n in range(B_ROWS // B_CHUNK):
            rows = slice(n * B_CHUNK, (n + 1) * B_CHUNK)
            _, vjp = jax.vjp(_gmlp_chunk, *_gmlp_args(u_ref, v_ref, lng_ref, lnb_ref, w_ref, bt_ref, rows))
            douts = [do_ref[rows, g * B_GDIM:(g + 1) * B_GDIM] for g in range(B_GROUPS)]
            dus, dvs, dlngs, dlnbs, dws, dbs = vjp(douts)
            for g in range(B_GROUPS):
                lanes = slice(g * B_GDIM, (g + 1) * B_GDIM)
                dp_ref[rows, lanes] = dus[g]
                dp_ref[rows, B_WIDTH + g * B_GDIM:B_WIDTH + (g + 1) * B_GDIM] = dvs[g]
                dlng_ref[:, lanes] += dlngs[g]
                dlnb_ref[:, lanes] += dlnbs[g]
                dw_ref[g] += dws[g]
                dbt_ref[:, g:g + 1] += dbs[g]

    vec = pl.BlockSpec((1, B_WIDTH), lambda i: (0, 0))
    wspec = pl.BlockSpec((B_GROUPS, B_CHUNK, B_CHUNK), lambda i: (0, 0, 0))
    bspec = pl.BlockSpec((B_CHUNK, B_GROUPS), lambda i: (0, 0))
    return pl.pallas_call(
        body, name=name, grid=(t // B_ROWS,),
        in_specs=[pl.BlockSpec((B_ROWS, B_WIDTH), lambda i: (i, 4)), pl.BlockSpec((B_ROWS, B_WIDTH), lambda i: (i, 5)),
                  pl.BlockSpec((B_ROWS, B_WIDTH), lambda i: (i, 1)), vec, vec, wspec, bspec,
                  pl.BlockSpec(memory_space=pl.ANY)],
        out_specs=[pl.BlockSpec((B_ROWS, 2 * B_WIDTH), lambda i: (i, 2)), vec, vec, wspec, bspec],
        out_shape=[jax.ShapeDtypeStruct(dproj.shape, F32), jax.ShapeDtypeStruct((1, B_WIDTH), F32),
                   jax.ShapeDtypeStruct((1, B_WIDTH), F32), jax.ShapeDtypeStruct((B_GROUPS, B_CHUNK, B_CHUNK), F32),
                   jax.ShapeDtypeStruct((B_CHUNK, B_GROUPS), F32)],
        input_output_aliases={7: 0},
        compiler_params=_params(("arbitrary",)),
    )(proj, proj, dmixin, ln_g, ln_b, w, bias_t, dproj)


C_PAIR = 2 * C_HEAD_DIM
C_PAIRS = C_HEADS // 2
C_SCALE = 1.0 / math.sqrt(C_HEAD_DIM)
C_ROT_DIM = 2 * C_ROT_HALF
ROPE_ROWS = 1024


def rope_tables(pos_col, name):
    t = pos_col.shape[0]

    def body(p_ref, c_ref, a_ref, b_ref):
        lane = jnp.bitwise_and(lax.broadcasted_iota(jnp.int32, (1, C_PAIR), 1), C_HEAD_DIM - 1)
        j = jnp.bitwise_and(lane, C_ROT_HALF - 1).astype(F32)
        inv = jnp.exp(j * (-math.log(ROPE_THETA) / C_ROT_HALF))
        ang = p_ref[...].astype(F32) * inv
        cos, sin = jnp.cos(ang), jnp.sin(ang)
        c_ref[...] = jnp.where(lane < C_ROT_DIM, cos, 1.0)
        a_ref[...] = jnp.where(lane < C_ROT_HALF, -sin, 0.0)
        b_ref[...] = jnp.where(jnp.logical_and(lane >= C_ROT_HALF, lane < C_ROT_DIM), sin, 0.0)

    tab = pl.BlockSpec((ROPE_ROWS, C_PAIR), lambda i: (i, 0))
    return pl.pallas_call(
        body, name=name, grid=(t // ROPE_ROWS,),
        in_specs=[pl.BlockSpec((ROPE_ROWS, 1), lambda i: (i, 0))],
        out_specs=[tab, tab, tab],
        out_shape=[jax.ShapeDtypeStruct((t, C_PAIR), F32)] * 3,
        compiler_params=_params(("arbitrary",)),
    )(pos_col)


def _rope(x, c, a, b):
    return x * c + pltpu.roll(x, C_PAIR - C_ROT_HALF, 1) * a + pltpu.roll(x, C_ROT_HALF, 1) * b


def _rope_t(d, c, a, b):
    return d * c + pltpu.roll(d * a, C_ROT_HALF, 1) + pltpu.roll(d * b, C_PAIR - C_ROT_HALF, 1)


def _attn_rows(idx, dil):
    nblk = SEQ // dil // C_BLOCK
    r, n = idx // nblk, idx % nblk
    start = r + dil * C_BLOCK * n
    prev = r + dil * C_BLOCK * jnp.maximum(n - 1, 0)
    if dil == 1:
        return pl.ds(pl.multiple_of(start, C_BLOCK), C_BLOCK), pl.ds(pl.multiple_of(prev, C_BLOCK), C_BLOCK), n > 0
    return pl.ds(start, C_BLOCK, stride=dil), pl.ds(prev, C_BLOCK, stride=dil), n > 0


def _head_masks():
    low = lax.broadcasted_iota(jnp.int32, (1, C_PAIR), 1) < C_HEAD_DIM
    return low, jnp.logical_not(low)


def _attn_masks(has_prev):
    i = lax.broadcasted_iota(jnp.int32, (C_BLOCK, C_BLOCK), 0)
    j = lax.broadcasted_iota(jnp.int32, (C_BLOCK, C_BLOCK), 1)
    return j <= i, jnp.logical_and(j >= i, has_prev)


def attn_fwd(qkv, cos_t, sin_a, sin_b, batch, name):
    t = qkv.shape[0]
    nbr = len(C_DILATIONS)

    def body(q_ref, k_ref, v_ref, c_ref, a_ref, b_ref, o_ref, l_ref, qs, ks, *stats):
        acc, mm, dd = stats[0:nbr], stats[nbr:2 * nbr], stats[2 * nbr:3 * nbr]
        c, a, b = c_ref[...], a_ref[...], b_ref[...]
        qs[...] = _rope(q_ref[...], c, a, b) * C_SCALE
        ks[...] = _rope(k_ref[...], c, a, b)
        for bi, dil in enumerate(C_DILATIONS):
            def block(idx, carry, bi=bi, dil=dil):
                rows, prev, has_prev = _attn_rows(idx, dil)
                m_own, m_prev = _attn_masks(has_prev)
                q_full = qs[rows, :]
                k_own, k_prev = ks[rows, :].astype(BF16), ks[prev, :].astype(BF16)
                v_own, v_prev = v_ref[rows, :], v_ref[prev, :]
                pv = m_full = den_full = None
                for hm in _head_masks():
                    qb = jnp.where(hm, q_full, 0.0).astype(BF16)
                    s_own = jnp.where(m_own, _dot_nt(qb, k_own), NEG_BIG)
                    s_prev = jnp.where(m_prev, _dot_nt(qb, k_prev), NEG_BIG)
                    m = jnp.maximum(jnp.max(s_own, axis=-1, keepdims=True), jnp.max(s_prev, axis=-1, keepdims=True))
                    p_own, p_prev = jnp.exp(s_own - m), jnp.exp(s_prev - m)
                    den = jnp.sum(p_own, axis=-1, keepdims=True) + jnp.sum(p_prev, axis=-1, keepdims=True)
                    pv_h = (_dot(p_own.astype(BF16), jnp.where(hm, v_own, 0.0).astype(BF16))
                            + _dot(p_prev.astype(BF16), jnp.where(hm, v_prev, 0.0).astype(BF16)))
                    if pv is None:
                        pv = pv_h
                        m_full = jnp.broadcast_to(m, (C_BLOCK, C_PAIR))
                        den_full = jnp.broadcast_to(den, (C_BLOCK, C_PAIR))
                    else:
                        pv = pv + pv_h
                        m_full = jnp.where(hm, m, m_full)
                        den_full = jnp.where(hm, den, den_full)
                acc[bi][rows, :] = pv
                mm[bi][rows, :] = m_full
                dd[bi][rows, :] = den_full
                return carry

            lax.fori_loop(0, SEQ // C_BLOCK, block, 0)
        step = 256
        for r0 in range(0, SEQ, step):
            rr = slice(r0, r0 + step)
            ms = [mm[g][rr, :] for g in range(nbr)]
            m_all = functools.reduce(jnp.maximum, ms)
            ws = [jnp.exp(m - m_all) for m in ms]
            num = sum(acc[g][rr, :] * ws[g] for g in range(nbr))
            den = sum(dd[g][rr, :] * ws[g] for g in range(nbr))
            o_ref[rr, :] = (num / den).astype(BF16)
            l_ref[rr, :] = m_all + jnp.log(den)

    def col(k):
        return pl.BlockSpec((SEQ, C_PAIR), lambda b, p: (b, k * C_PAIRS + p))

    tab = pl.BlockSpec((SEQ, C_PAIR), lambda b, p: (b, 0))
    return pl.pallas_call(
        body, name=name, grid=(batch, C_PAIRS),
        in_specs=[col(0), col(1), col(2), tab, tab, tab],
        out_specs=[col(0), col(0)],
        out_shape=[jax.ShapeDtypeStruct((t, D_MODEL), BF16), jax.ShapeDtypeStruct((t, D_MODEL), F32)],
        scratch_shapes=[pltpu.VMEM((SEQ, C_PAIR), F32)] * (2 + 3 * nbr),
        compiler_params=_params(("arbitrary", "arbitrary")),
    )(qkv, qkv, qkv, cos_t, sin_a, sin_b)


def attn_bwd(qkv, cos_t, sin_a, sin_b, o, lse, do, batch, name):
    t = qkv.shape[0]

    def body(q_ref, k_ref, v_ref, c_ref, a_ref, b_ref, o_ref, l_ref, do_ref, dq_ref, dk_ref, dv_ref,
             qs, ks, dqs, dks, dvs, dlt):
        c, a, b = c_ref[...], a_ref[...], b_ref[...]
        qs[...] = _rope(q_ref[...], c, a, b) * C_SCALE
        ks[...] = _rope(k_ref[...], c, a, b)
        prod = do_ref[...] * o_ref[...].astype(F32)
        low = lax.broadcasted_iota(jnp.int32, (1, C_PAIR), 1) < C_HEAD_DIM
        s_low = jnp.sum(jnp.where(low, prod, 0.0), axis=-1, keepdims=True)
        s_all = jnp.sum(prod, axis=-1, keepdims=True)
        dlt[...] = jnp.where(low, s_low, s_all - s_low)
        dqs[...] = jnp.zeros_like(dqs)
        dks[...] = jnp.zeros_like(dks)
        dvs[...] = jnp.zeros_like(dvs)
        for dil in C_DILATIONS:
            def block(idx, carry, dil=dil):
                rows, prev, has_prev = _attn_rows(idx, dil)
                m_own, m_prev = _attn_masks(has_prev)
                q_full, do_full = qs[rows, :], do_ref[rows, :]
                k_own, k_prev = ks[rows, :], ks[prev, :]
                v_own, v_prev = v_ref[rows, :].astype(BF16), v_ref[prev, :].astype(BF16)
                l_full, d_full = l_ref[rows, :], dlt[rows, :]
                dq = dk_own = dk_prev = dv_own = dv_prev = 0.0
                for hh, hm in enumerate(_head_masks()):
                    qb = jnp.where(hm, q_full, 0.0).astype(BF16)
                    dob = jnp.where(hm, do_full, 0.0).astype(BF16)
                    lrow = l_full[:, hh * C_HEAD_DIM:hh * C_HEAD_DIM + 1]
                    drow = d_full[:, hh * C_HEAD_DIM:hh * C_HEAD_DIM + 1]
                    p_own = jnp.exp(jnp.where(m_own, _dot_nt(qb, k_own.astype(BF16)), NEG_BIG) - lrow)
                    p_prev = jnp.exp(jnp.where(m_prev, _dot_nt(qb, k_prev.astype(BF16)), NEG_BIG) - lrow)
                    ds_own = (p_own * (_dot_nt(dob, v_own) - drow)).astype(BF16)
                    ds_prev = (p_prev * (_dot_nt(dob, v_prev) - drow)).astype(BF16)
                    dq = dq + (_dot(ds_own, jnp.where(hm, k_own, 0.0).astype(BF16))
                               + _dot(ds_prev, jnp.where(hm, k_prev, 0.0).astype(BF16)))
                    dk_own = dk_own + _dot_tn(ds_own, qb)
                    dk_prev = dk_prev + _dot_tn(ds_prev, qb)
                    dv_own = dv_own + _dot_tn(p_own.astype(BF16), dob)
                    dv_prev = dv_prev + _dot_tn(p_prev.astype(BF16), dob)
                dqs[rows, :] += dq
                dks[rows, :] += dk_own
                dvs[rows, :] += dv_own
                dks[prev, :] += dk_prev
                dvs[prev, :] += dv_prev
                return carry

            lax.fori_loop(0, SEQ // C_BLOCK, block, 0)
        dq_ref[...] = _rope_t(dqs[...] * C_SCALE, c, a, b)
        dk_ref[...] = _rope_t(dks[...], c, a, b)
        dv_ref[...] = dvs[...]

    def col(k):
        return pl.BlockSpec((SEQ, C_PAIR), lambda b, p: (b, k * C_PAIRS + p))

    tab = pl.BlockSpec((SEQ, C_PAIR), lambda b, p: (b, 0))
    out = jax.ShapeDtypeStruct((t, D_MODEL), F32)
    return pl.pallas_call(
        body, name=name, grid=(batch, C_PAIRS),
        in_specs=[col(0), col(1), col(2), tab, tab, tab, col(0), col(0), col(0)],
        out_specs=[col(0), col(0), col(0)],
        out_shape=[out, out, out],
        scratch_shapes=[pltpu.VMEM((SEQ, C_PAIR), F32)] * 6,
        compiler_params=_params(("arbitrary", "arbitrary")),
    )(qkv, qkv, qkv, cos_t, sin_a, sin_b, o, lse, do)


N_DEV = 8
ANY = pl.BlockSpec(memory_space=pl.ANY)


def _place():
    x, y, c = lax.axis_index("x"), lax.axis_index("y"), lax.axis_index("c")
    return x, y, c, [(1 - x, y), (x, 1 - y), (1 - x, 1 - y)]


def gather_weights(shards, name):
    n = len(shards)

    def body(*refs):
        ins, outs = refs[:n], refs[n:2 * n]
        send_sems, recv_sems, local_sems = refs[2 * n:]
        x, y, c, chips = _place()
        me = 2 * x + y
        copies = []
        for a in range(n):
            own = pltpu.make_async_copy(ins[a], outs[a].at[me], local_sems.at[a])
            own.start()
            copies.append(own)
        sends = []
        for a in range(n):
            for j, (px, py) in enumerate(chips):
                cp = pltpu.make_async_remote_copy(
                    src_ref=ins[a], dst_ref=outs[a].at[me], send_sem=send_sems.at[3 * a + j],
                    recv_sem=recv_sems.at[3 * a + j], device_id=(px, py, c), device_id_type=MESH)
                cp.start()
                sends.append(cp)
        for a in range(n):
            for j, (px, py) in enumerate(chips):
                pltpu.make_async_remote_copy(
                    src_ref=ins[a], dst_ref=outs[a].at[2 * px + py], send_sem=send_sems.at[3 * a + j],
                    recv_sem=recv_sems.at[3 * a + j], device_id=(px, py, c), device_id_type=MESH).wait_recv()
        for cp in sends:
            cp.wait_send()
        for cp in copies:
            cp.wait()

    return pl.pallas_call(
        body, name=name,
        in_specs=[ANY] * n, out_specs=[ANY] * n,
        out_shape=[jax.ShapeDtypeStruct((N_CHIPS,) + s.shape, s.dtype) for s in shards],
        scratch_shapes=[pltpu.SemaphoreType.DMA((3 * n,)), pltpu.SemaphoreType.DMA((3 * n,)),
                        pltpu.SemaphoreType.DMA((n,))],
    )(*shards)


def scatter_grads(grads, name):
    n = len(grads)

    def body(*refs):
        ins, outs = refs[:n], refs[n:2 * n]
        send_sems, recv_sems, local_sems = refs[2 * n:]
        x, y, c, chips = _place()
        me = 2 * x + y
        copies = []
        for a in range(n):
            own = pltpu.make_async_copy(ins[a].at[me], outs[a].at[3], local_sems.at[a])
            own.start()
            copies.append(own)
        sends = []
        for a in range(n):
            for j, (px, py) in enumerate(chips):
                cp = pltpu.make_async_remote_copy(
                    src_ref=ins[a].at[2 * px + py], dst_ref=outs[a].at[j], send_sem=send_sems.at[3 * a + j],
                    recv_sem=recv_sems.at[3 * a + j], device_id=(px, py, c), device_id_type=MESH)
                cp.start()
                sends.append(cp)
        for cp in sends:
            cp.wait_recv()
        for cp in sends:
            cp.wait_send()
        for cp in copies:
            cp.wait()

    return pl.pallas_call(
        body, name=name,
        in_specs=[ANY] * n, out_specs=[ANY] * n,
        out_shape=[jax.ShapeDtypeStruct(g.shape, g.dtype) for g in grads],
        scratch_shapes=[pltpu.SemaphoreType.DMA((3 * n,)), pltpu.SemaphoreType.DMA((3 * n,)),
                        pltpu.SemaphoreType.DMA((n,))],
    )(*grads)


def sibling_swap(arrays, name):
    n = len(arrays)

    def body(*refs):
        ins, outs = refs[:n], refs[n:2 * n]
        send_sems, recv_sems = refs[2 * n:]
        x, y, c, _ = _place()
        sends = []
        for a in range(n):
            cp = pltpu.make_async_remote_copy(
                src_ref=ins[a], dst_ref=outs[a], send_sem=send_sems.at[a], recv_sem=recv_sems.at[a],
                device_id=(x, y, 1 - c), device_id_type=MESH)
            cp.start()
            sends.append(cp)
        for cp in sends:
            cp.wait_recv()
        for cp in sends:
            cp.wait_send()

    return pl.pallas_call(
        body, name=name,
        in_specs=[ANY] * n, out_specs=[ANY] * n,
        out_shape=[jax.ShapeDtypeStruct(s.shape, s.dtype) for s in arrays],
        scratch_shapes=[pltpu.SemaphoreType.DMA((n,)), pltpu.SemaphoreType.DMA((n,))],
    )(*arrays)


def allreduce_small(slab, name):
    rows, lanes = slab.shape

    def body(x_ref, out_ref, gath, send_sems, recv_sems, local_sem):
        x, y, c, chips = _place()
        me, sibling = (x, y, c), (x, y, 1 - c)

        def slot(px, py, pc):
            return gath.at[4 * px + 2 * py + pc]

        def copy(k, block, to, src=None):
            return pltpu.make_async_remote_copy(
                src_ref=slot(*block) if src is None else src, dst_ref=slot(*block),
                send_sem=send_sems.at[k], recv_sem=recv_sems.at[k], device_id=to, device_id_type=MESH)

        mine = pltpu.make_async_copy(x_ref, slot(*me), local_sem)
        mine.start()
        first = [copy(0, me, sibling, src=x_ref)]
        first += [copy(1 + j, me, (*chip, c), src=x_ref) for j, chip in enumerate(chips)]
        for cp in first:
            cp.start()
        passed = [copy(4 + j, (*chip, c), sibling) for j, chip in enumerate(chips)]
        for j, chip in enumerate(chips):
            copy(1 + j, (*chip, c), me).wait_recv()
            passed[j].start()
        copy(0, sibling, me).wait_recv()
        for j, chip in enumerate(chips):
            copy(4 + j, (*chip, 1 - c), me).wait_recv()
        for cp in first + passed:
            cp.wait_send()
        mine.wait()
        total = gath[0]
        for d in range(1, N_DEV):
            total = total + gath[d]
        out_ref[...] = total

    return pl.pallas_call(
        body, name=name,
        in_specs=[pl.BlockSpec(memory_space=pltpu.VMEM)],
        out_specs=pl.BlockSpec(memory_space=pltpu.VMEM),
        out_shape=jax.ShapeDtypeStruct((rows, lanes), F32),
        scratch_shapes=[pltpu.VMEM((N_DEV, rows, lanes), F32),
                        pltpu.SemaphoreType.DMA((7,)), pltpu.SemaphoreType.DMA((7,)), pltpu.SemaphoreType.DMA],
    )(slab)


ELT_ROWS = 512


def reduce_slabs(r, name):
    _, rows, cols = r.shape

    def body(r_ref, o_ref):
        o_ref[...] = ((r_ref[3].astype(F32) + r_ref[0].astype(F32)) + r_ref[1].astype(F32)) + r_ref[2].astype(F32)

    return pl.pallas_call(
        body, name=name, grid=(rows // min(rows, ELT_ROWS),),
        in_specs=[pl.BlockSpec((N_CHIPS, min(rows, ELT_ROWS), cols), lambda i: (0, i, 0))],
        out_specs=pl.BlockSpec((min(rows, ELT_ROWS), cols), lambda i: (i, 0)),
        out_shape=jax.ShapeDtypeStruct((rows, cols), F32),
        compiler_params=_params(("arbitrary",)),
    )(r)


def _adamw(w, g, m, v):
    m = ADAM_B1 * m + (1.0 - ADAM_B1) * g
    v = ADAM_B2 * v + (1.0 - ADAM_B2) * jnp.square(g)
    m_hat = m / (1.0 - ADAM_B1 ** ADAM_STEP)
    v_hat = v / (1.0 - ADAM_B2 ** ADAM_STEP)
    delta = -ADAM_LR * (m_hat / (jnp.sqrt(v_hat) + ADAM_EPS) + ADAM_WD * w)
    return delta, m, v


def adamw_big(w, s_mine, s_sibling, m, v, name):
    rows, cols = w.shape

    def body(w_ref, a_ref, b_ref, m_ref, v_ref, g_out, d_out, m_out, v_out):
        g = a_ref[...] + b_ref[...]
        g_out[...] = g
        d_out[...], m_out[...], v_out[...] = _adamw(w_ref[...], g, m_ref[...], v_ref[...])

    blk = pl.BlockSpec((min(rows, ELT_ROWS), cols), lambda i: (i, 0))
    out = jax.ShapeDtypeStruct((rows, cols), F32)
    return pl.pallas_call(
        body, name=name, grid=(rows // min(rows, ELT_ROWS),),
        in_specs=[blk] * 5, out_specs=[blk] * 4, out_shape=[out] * 4,
        compiler_params=_params(("arbitrary",)),
    )(w, s_mine, s_sibling, m, v)


def adamw_small(ws, gs, ms, vs, name):
    n = len(ws)

    def body(*refs):
        w_refs, g_refs, m_refs, v_refs = (refs[k * n:(k + 1) * n] for k in range(4))
        d_out, m_out, v_out = (refs[(4 + k) * n:(5 + k) * n] for k in range(3))
        for i in range(n):
            d_out[i][...], m_out[i][...], v_out[i][...] = _adamw(
                w_refs[i][...], g_refs[i][...], m_refs[i][...], v_refs[i][...])

    outs = [jax.ShapeDtypeStruct(w.shape, F32) for w in ws]
    res = pl.pallas_call(body, name=name, out_shape=outs * 3)(*ws, *gs, *ms, *vs)
    return res[:n], res[n:2 * n], res[2 * n:]


SLAB_LANES = 128
SLAB_ROW_ALIGN = 8


def _pack(parts):
    flat = jnp.concatenate([p.reshape(-1) for p in parts])
    rows = -(-flat.shape[0] // (SLAB_LANES * SLAB_ROW_ALIGN)) * SLAB_ROW_ALIGN
    flat = jnp.pad(flat, (0, rows * SLAB_LANES - flat.shape[0]))
    return flat.reshape(rows, SLAB_LANES)


def _unpack(slab, shapes):
    flat = slab.reshape(-1)
    out, pos = [], 0
    for s in shapes:
        size = math.prod(s)
        out.append(flat[pos:pos + size].reshape(s))
        pos += size
    return out


def kernel(x, positions, norm_mix_pre, norm_mix_post, norm_ffn_pre, norm_ffn_post, w_in_even, lb_table, a_norm, b_ln_g, b_ln_b, b_ws, b_bias, w_out_even, w_in_odd, w_out_odd, w_ff1, w_ff2, loss_target, m_norm_mix_pre, m_norm_mix_post, m_norm_ffn_pre, m_norm_ffn_post, m_w_in_even, m_lb_table, m_a_norm, m_b_ln_g, m_b_ln_b, m_b_ws, m_b_bias, m_w_out_even, m_w_in_odd, m_w_out_odd, m_w_ff1, m_w_ff2, v_norm_mix_pre, v_norm_mix_post, v_norm_ffn_pre, v_norm_ffn_post, v_w_in_even, v_lb_table, v_a_norm, v_b_ln_g, v_b_ln_b, v_b_ws, v_b_bias, v_w_out_even, v_w_in_odd, v_w_out_odd, v_w_ff1, v_w_ff2):
    batch = x.shape[0]
    t = batch * SEQ
    d = D_MODEL
    x0 = x.reshape(t, d)
    target = loss_target.reshape(t, d)

    win_e, wout_e, win_o, wout_o, w1, w2 = gather_weights(
        [w_in_even[0].astype(BF16), w_out_even[0].astype(BF16), w_in_odd[0].astype(BF16),
         w_out_odd[0].astype(BF16), w_ff1.astype(BF16), w_ff2.astype(BF16)], "gather_weights")

    def gain(p, layer):
        return p[layer:layer + 1]

    bias_t = b_bias[0].T
    proj, h0 = norm_matmul(x0, gain(norm_mix_pre, 0), win_e, "in_proj_even")
    oa, states = hgrn2_fwd(proj, lb_table, a_norm, batch, "hgrn2_fwd")
    mixin = gmlp_fwd(proj, oa, b_ln_g, b_ln_b, b_ws[0], bias_t, "gmlp_fwd")
    mix0, x1 = out_proj(mixin, wout_e, x0, gain(norm_mix_post, 0), "out_proj_even")
    x2, hf0, a0, y0 = ffn_fwd(x1, gain(norm_ffn_pre, 0), w1, w2, gain(norm_ffn_post, 0), 0, "ffn_fwd_0")
    qkv, h1 = norm_matmul(x2, gain(norm_mix_pre, 1), win_o, "in_proj_odd")
    cos_t, sin_a, sin_b = rope_tables(positions.reshape(t, 1), "rope_tables")
    ao, lse = attn_fwd(qkv, cos_t, sin_a, sin_b, batch, "attn_fwd")
    mix1, x3 = out_proj(ao, wout_o, x2, gain(norm_mix_post, 1), "out_proj_odd")
    x4, hf1, a1, y1 = ffn_fwd(x3, gain(norm_ffn_pre, 1), w1, w2, gain(norm_ffn_post, 1), 1, "ffn_fwd_1")
    dx4, loss_part = loss_grad(x4, target, "loss_grad")
    loss = lax.psum(loss_part[0, 0], ("x", "y", "c"))

    dx3, dy1, da1, dg_fpre1, dg_fpost1 = ffn_bwd(
        dx4, x3, y1, a1, gain(norm_ffn_pre, 1), gain(norm_ffn_post, 1), w1, w2, 1, "ffn_bwd_1")
    g_w1 = weight_grad(hf1, da1, "b", d, D_FF // N_CHIPS, False, "wgrad_ff1_1", layer=1)
    g_w2 = weight_grad(a1, dy1, "a", D_FF // N_CHIPS, d, True, "wgrad_ff2_1", layer=1)
    dmix1, dao, dg_mpost1 = out_proj_bwd(dx3, mix1, gain(norm_mix_post, 1), wout_o, "out_proj_bwd_odd")
    g_wout_o = weight_grad(ao, dmix1, "a", d // N_CHIPS, d, False, "wgrad_out_odd")
    dq, dk, dv = attn_bwd(qkv, cos_t, sin_a, sin_b, ao, lse, dao, batch, "attn_bwd")
    dqkv = jnp.concatenate([dq, dk, dv], axis=1)
    dx2, dg_mpre1 = norm_matmul_bwd(dqkv, win_o, x2, gain(norm_mix_pre, 1), dx3, "in_proj_bwd_odd")
    g_win_o = weight_grad(h1, dqkv, "b", d, 3 * d // N_CHIPS, False, "wgrad_in_odd")
    dx1, dy0, da0, dg_fpre0, dg_fpost0 = ffn_bwd(
        dx2, x1, y0, a0, gain(norm_ffn_pre, 0), gain(norm_ffn_post, 0), w1, w2, 0, "ffn_bwd_0")
    g_w1 = weight_grad(hf0, da0, "b", d, D_FF // N_CHIPS, False, "wgrad_ff1_0", layer=0, into=g_w1)
    g_w2 = weight_grad(a0, dy0, "a", D_FF // N_CHIPS, d, True, "wgrad_ff2_0", layer=0, into=g_w2)
    dmix0, dmixin, dg_mpost0 = out_proj_bwd(dx1, mix0, gain(norm_mix_post, 0), wout_e, "out_proj_bwd_even")
    g_wout_e = weight_grad(mixin, dmix0, "a", d // N_CHIPS, d, False, "wgrad_out_even")
    dproj, d_lb, d_anorm = hgrn2_bwd(proj, states, lb_table, a_norm, dmixin, batch, "hgrn2_bwd")
    dproj, d_lng, d_lnb, d_ws, d_bias_t = gmlp_bwd(proj, dmixin, b_ln_g, b_ln_b, b_ws[0], bias_t, dproj, "gmlp_bwd")
    dx0, dg_mpre0 = norm_matmul_bwd(dproj, win_e, x0, gain(norm_mix_pre, 0), dx1, "in_proj_bwd_even")
    g_win_e = weight_grad(h0, dproj, "b", d, 3 * d // N_CHIPS, False, "wgrad_in_even")
    grad_x = dx0.reshape(x.shape)

    big_w = [w_in_even, w_out_even, w_in_odd, w_out_odd, w_ff1, w_ff2]
    big_m = [m_w_in_even, m_w_out_even, m_w_in_odd, m_w_out_odd, m_w_ff1, m_w_ff2]
    big_v = [v_w_in_even, v_w_out_even, v_w_in_odd, v_w_out_odd, v_w_ff1, v_w_ff2]
    received = scatter_grads([g_win_e, g_wout_e, g_win_o, g_wout_o, g_w1, g_w2], "scatter_grads")
    sums = [reduce_slabs(r.reshape(N_CHIPS, -1, r.shape[-1]), "reduce_slabs_%d" % i) for i, r in enumerate(received)]
    sibling = sibling_swap(sums, "sibling_swap")
    big = []
    for i, (w, m, v) in enumerate(zip(big_w, big_m, big_v)):
        two_d = (-1, w.shape[-1])
        res = adamw_big(w.reshape(two_d), sums[i], sibling[i], m.reshape(two_d), v.reshape(two_d), "adamw_big_%d" % i)
        big.append([r.reshape(w.shape) for r in res])

    small_w = [norm_mix_pre, norm_mix_post, norm_ffn_pre, norm_ffn_post, lb_table, a_norm, b_ln_g, b_ln_b, b_ws, b_bias]
    small_m = [m_norm_mix_pre, m_norm_mix_post, m_norm_ffn_pre, m_norm_ffn_post, m_lb_table, m_a_norm, m_b_ln_g,
               m_b_ln_b, m_b_ws, m_b_bias]
    small_v = [v_norm_mix_pre, v_norm_mix_post, v_norm_ffn_pre, v_norm_ffn_post, v_lb_table, v_a_norm, v_b_ln_g,
               v_b_ln_b, v_b_ws, v_b_bias]
    partial = [jnp.concatenate([dg_mpre0, dg_mpre1]), jnp.concatenate([dg_mpost0, dg_mpost1]),
               jnp.concatenate([dg_fpre0, dg_fpre1]), jnp.concatenate([dg_fpost0, dg_fpost1]),
               d_lb, d_anorm, d_lng, d_lnb, d_ws[None], d_bias_t.T[None]]
    small_g = _unpack(allreduce_small(_pack(partial), "allreduce_small"), [w.shape for w in small_w])
    small_d, small_nm, small_nv = adamw_small(small_w, small_g, small_m, small_v, "adamw_small")

    order = ["norm_mix_pre", "norm_mix_post", "norm_ffn_pre", "norm_ffn_post", "w_in_even", "lb_table", "a_norm",
             "b_ln_g", "b_ln_b", "b_ws", "b_bias", "w_out_even", "w_in_odd", "w_out_odd", "w_ff1", "w_ff2"]
    small_names = ["norm_mix_pre", "norm_mix_post", "norm_ffn_pre", "norm_ffn_post", "lb_table", "a_norm",
                   "b_ln_g", "b_ln_b", "b_ws", "b_bias"]
    big_names = ["w_in_even", "w_out_even", "w_in_odd", "w_out_odd", "w_ff1", "w_ff2"]
    grads, deltas, new_m, new_v = {}, {}, {}, {}
    for i, nm in enumerate(small_names):
        grads[nm], deltas[nm], new_m[nm], new_v[nm] = small_g[i], small_d[i], small_nm[i], small_nv[i]
    for i, nm in enumerate(big_names):
        grads[nm], deltas[nm], new_m[nm], new_v[nm] = big[i]
    return (loss, grad_x, *[grads[n] for n in order], *[deltas[n] for n in order],
            *[new_m[n] for n in order], *[new_v[n] for n in order])
```

```python
import functools
import math

import jax
import jax.numpy as jnp
from jax import lax
from jax.experimental import pallas as pl
from jax.experimental.pallas import tpu as pltpu

F32 = jnp.float32
BF16 = jnp.bfloat16
MESH = pl.DeviceIdType.MESH

D_MODEL = 1024
SEQ = 2048
D_FF = 4096
N_CHIPS = 4
A_WIDTH = 512
A_HEADS = 4
A_DK = 128
A_CHUNK = 64
A_SUB = 16
B_WIDTH = 512
B_GROUPS = 4
B_CHUNK = 128
C_HEADS = 16
C_HEAD_DIM = 64
C_ROT_HALF = 8
C_BLOCK = 128
C_DILATIONS = (1, 4, 16)
ROPE_THETA = 500000.0
EPS = 1e-6
ADAM_LR = 0.001
ADAM_B1 = 0.9
ADAM_B2 = 0.999
ADAM_EPS = 1e-08
ADAM_WD = 0.01
ADAM_STEP = 10

ROW_TILE = 512
VMEM_LIMIT = 56 * 1024 * 1024
NEG_BIG = -1e30


def _params(sem=None):
    return pltpu.CompilerParams(dimension_semantics=sem, vmem_limit_bytes=VMEM_LIMIT)


def _dot(a, b):
    return jnp.dot(a, b, preferred_element_type=F32)


def _dot_nt(a, b):
    return lax.dot_general(a, b, (((1,), (1,)), ((), ())), preferred_element_type=F32)


def _dot_tn(a, b):
    return lax.dot_general(a, b, (((0,), (0,)), ((), ())), preferred_element_type=F32)


def _rms(x, g):
    r = lax.rsqrt(jnp.mean(x * x, axis=-1, keepdims=True) + EPS)
    return x * r * g


def _rms_bwd(x, g, dy):
    r = lax.rsqrt(jnp.mean(x * x, axis=-1, keepdims=True) + EPS)
    xh = x * r
    dg = jnp.sum(dy * xh, axis=0, keepdims=True)
    dxh = dy * g
    dx = r * (dxh - xh * jnp.mean(dxh * xh, axis=-1, keepdims=True))
    return dx, dg


def _accumulate(ref, val, first):
    @pl.when(first)
    def _():
        ref[...] = val

    @pl.when(jnp.logical_not(first))
    def _():
        ref[...] += val


N_DEV = 8
ANY = pl.BlockSpec(memory_space=pl.ANY)


def _place():
    x, y, c = lax.axis_index("x"), lax.axis_index("y"), lax.axis_index("c")
    return x, y, c, [(1 - x, y), (x, 1 - y), (1 - x, 1 - y)]


class _Exchange:
    def __init__(self, kind, arrays):
        self.kind, self.arrays, self.n = kind, list(arrays), len(arrays)
        if kind == "gather":
            self.out_shape = [jax.ShapeDtypeStruct((N_CHIPS,) + a.shape, a.dtype) for a in self.arrays]
        else:
            self.out_shape = [jax.ShapeDtypeStruct(a.shape, a.dtype) for a in self.arrays]
        self.scratch = [pltpu.SemaphoreType.DMA((3 * self.n,)), pltpu.SemaphoreType.DMA((3 * self.n,)),
                        pltpu.SemaphoreType.DMA((self.n,))]

    def _copies(self, ins, outs, sems):
        send_sems, recv_sems, local_sems = sems
        x, y, c, chips = _place()
        me = 2 * x + y
        local, remote = [], []
        for a in range(self.n):
            if self.kind == "gather":
                local.append(pltpu.make_async_copy(ins[a], outs[a].at[me], local_sems.at[a]))
            else:
                local.append(pltpu.make_async_copy(ins[a].at[me], outs[a].at[3], local_sems.at[a]))
            for j, (px, py) in enumerate(chips):
                if self.kind == "gather":
                    src, dst, landed = ins[a], outs[a].at[me], outs[a].at[2 * px + py]
                else:
                    src, dst, landed = ins[a].at[2 * px + py], outs[a].at[j], outs[a].at[j]
                send = pltpu.make_async_remote_copy(
                    src_ref=src, dst_ref=dst, send_sem=send_sems.at[3 * a + j], recv_sem=recv_sems.at[3 * a + j],
                    device_id=(px, py, c), device_id_type=MESH)
                recv = pltpu.make_async_remote_copy(
                    src_ref=src, dst_ref=landed, send_sem=send_sems.at[3 * a + j], recv_sem=recv_sems.at[3 * a + j],
                    device_id=(px, py, c), device_id_type=MESH)
                remote.append((send, recv))
        return local, remote

    def start(self, ins, outs, sems):
        local, remote = self._copies(ins, outs, sems)
        for cp in local:
            cp.start()
        for send, _ in remote:
            send.start()

    def finish(self, ins, outs, sems):
        local, remote = self._copies(ins, outs, sems)
        for _, recv in remote:
            recv.wait_recv()
        for send, _ in remote:
            send.wait_send()
        for cp in local:
            cp.wait()


def _call(body, *, name, grid, in_specs, out_specs, out_shape, args, scratch_shapes=(), aliases=None, exchange=None):
    if exchange is None:
        return pl.pallas_call(
            body, name=name, grid=grid, in_specs=in_specs, out_specs=out_specs, out_shape=out_shape,
            scratch_shapes=list(scratch_shapes), input_output_aliases=aliases or {},
            compiler_params=_params(("arbitrary",) * len(grid)))(*args)
    n_in, n_out, n_scr, n_ex = len(in_specs), len(out_specs), len(scratch_shapes), exchange.n
    steps = grid

    def wrapped(*refs):
        ins, refs = refs[:n_in], refs[n_in:]
        ex_in, refs = refs[:n_ex], refs[n_ex:]
        outs, refs = refs[:n_out], refs[n_out:]
        ex_out, refs = refs[:n_ex], refs[n_ex:]
        scr, sems = refs[:n_scr], refs[n_scr:]
        first = functools.reduce(jnp.logical_and, [pl.program_id(k) == 0 for k in range(len(steps))])
        last = functools.reduce(jnp.logical_and, [pl.program_id(k) == steps[k] - 1 for k in range(len(steps))])

        @pl.when(first)
        def _():
            exchange.start(ex_in, ex_out, sems)

        body(*ins, *outs, *scr)

        @pl.when(last)
        def _():
            exchange.finish(ex_in, ex_out, sems)

    return pl.pallas_call(
        wrapped, name=name, grid=grid,
        in_specs=list(in_specs) + [ANY] * n_ex, out_specs=list(out_specs) + [ANY] * n_ex,
        out_shape=list(out_shape) + exchange.out_shape,
        scratch_shapes=list(scratch_shapes) + exchange.scratch, input_output_aliases=aliases or {},
        compiler_params=_params(("arbitrary",) * len(grid)))(*args, *exchange.arrays)


def exchange_alone(exchange, name):
    def body(*refs):
        n = exchange.n
        exchange.start(refs[:n], refs[n:2 * n], refs[2 * n:])
        exchange.finish(refs[:n], refs[n:2 * n], refs[2 * n:])

    return pl.pallas_call(
        body, name=name, in_specs=[ANY] * exchange.n, out_specs=[ANY] * exchange.n,
        out_shape=exchange.out_shape, scratch_shapes=exchange.scratch)(*exchange.arrays)


def norm_matmul(x, g, wg, name, exchange=None):
    t, d = x.shape
    nl = wg.shape[2]

    def body(x_ref, g_ref, w_ref, o_ref, h_ref):
        @pl.when(pl.program_id(1) == 0)
        def _():
            h_ref[...] = _rms(x_ref[...], g_ref[...]).astype(BF16)

        o_ref[...] = _dot(h_ref[...], w_ref[...])

    return _call(
        body, name=name, grid=(t // ROW_TILE, N_CHIPS),
        in_specs=[pl.BlockSpec((ROW_TILE, d), lambda i, c: (i, 0)),
                  pl.BlockSpec((1, d), lambda i, c: (0, 0)),
                  pl.BlockSpec((None, d, nl), lambda i, c: (c, 0, 0))],
        out_specs=[pl.BlockSpec((ROW_TILE, nl), lambda i, c: (i, c)),
                   pl.BlockSpec((ROW_TILE, d), lambda i, c: (i, 0))],
        out_shape=[jax.ShapeDtypeStruct((t, N_CHIPS * nl), F32), jax.ShapeDtypeStruct((t, d), BF16)],
        args=(x, g, wg), exchange=exchange)


def norm_matmul_bwd(dproj, wg, x, g, dres, name):
    t, d = x.shape
    nl = wg.shape[2]

    def body(dp_ref, w_ref, x_ref, g_ref, dres_ref, dx_ref, dg_ref, acc):
        i, c = pl.program_id(0), pl.program_id(1)
        part = _dot_nt(dp_ref[...].astype(BF16), w_ref[...])
        _accumulate(acc, part, c == 0)

        @pl.when(c == N_CHIPS - 1)
        def _():
            dx, dg = _rms_bwd(x_ref[...], g_ref[...], acc[...])
            dx_ref[...] = dres_ref[...] + dx
            _accumulate(dg_ref, dg, i == 0)

    return pl.pallas_call(
        body, name=name, grid=(t // ROW_TILE, N_CHIPS),
        in_specs=[pl.BlockSpec((ROW_TILE, nl), lambda i, c: (i, c)),
                  pl.BlockSpec((None, d, nl), lambda i, c: (c, 0, 0)),
                  pl.BlockSpec((ROW_TILE, d), lambda i, c: (i, 0)),
                  pl.BlockSpec((1, d), lambda i, c: (0, 0)),
                  pl.BlockSpec((ROW_TILE, d), lambda i, c: (i, 0))],
        out_specs=[pl.BlockSpec((ROW_TILE, d), lambda i, c: (i, 0)),
                   pl.BlockSpec((1, d), lambda i, c: (0, 0))],
        out_shape=[jax.ShapeDtypeStruct((t, d), F32), jax.ShapeDtypeStruct((1, d), F32)],
        scratch_shapes=[pltpu.VMEM((ROW_TILE, d), F32)],
        compiler_params=_params(("arbitrary", "arbitrary")),
    )(dproj, wg, x, g, dres)


def out_proj(a, wg, x, g, name):
    t, d = x.shape
    kl = wg.shape[1]

    def body(a_ref, w_ref, x_ref, g_ref, mix_ref, xo_ref):
        acc = _dot(a_ref[:, 0:kl], w_ref[0])
        for c in range(1, N_CHIPS):
            acc += _dot(a_ref[:, c * kl:(c + 1) * kl], w_ref[c])
        mix_ref[...] = acc
        xo_ref[...] = x_ref[...] + _rms(acc, g_ref[...])

    row = pl.BlockSpec((ROW_TILE, d), lambda i: (i, 0))
    return pl.pallas_call(
        body, name=name, grid=(t // ROW_TILE,),
        in_specs=[row, pl.BlockSpec((N_CHIPS, kl, d), lambda i: (0, 0, 0)), row,
                  pl.BlockSpec((1, d), lambda i: (0, 0))],
        out_specs=[row, row],
        out_shape=[jax.ShapeDtypeStruct((t, d), F32), jax.ShapeDtypeStruct((t, d), F32)],
        compiler_params=_params(("arbitrary",)),
    )(a, wg, x, g)


def out_proj_bwd(dxo, mix, g, wg, name):
    t, d = mix.shape
    kl = wg.shape[1]

    def body(dxo_ref, mix_ref, g_ref, w_ref, dmix_ref, da_ref, dg_ref):
        dmix, dg = _rms_bwd(mix_ref[...], g_ref[...], dxo_ref[...])
        dmb = dmix.astype(BF16)
        dmix_ref[...] = dmb
        for c in range(N_CHIPS):
            da_ref[:, c * kl:(c + 1) * kl] = _dot_nt(dmb, w_ref[c])
        _accumulate(dg_ref, dg, pl.program_id(0) == 0)

    row = pl.BlockSpec((ROW_TILE, d), lambda i: (i, 0))
    vec = pl.BlockSpec((1, d), lambda i: (0, 0))
    return pl.pallas_call(
        body, name=name, grid=(t // ROW_TILE,),
        in_specs=[row, row, vec, pl.BlockSpec((N_CHIPS, kl, d), lambda i: (0, 0, 0))],
        out_specs=[row, row, vec],
        out_shape=[jax.ShapeDtypeStruct((t, d), BF16), jax.ShapeDtypeStruct((t, d), F32),
                   jax.ShapeDtypeStruct((1, d), F32)],
        compiler_params=_params(("arbitrary",)),
    )(dxo, mix, g, wg)


def ffn_fwd(x, gpre, w1g, w2g, gpost, name, exchange=None):
    t, d = x.shape
    hc = w1g.shape[2]

    def body(x_ref, gpre_ref, w1_ref, w2_ref, gpost_ref, xo_ref, h_ref, a_ref, y_ref, acc):
        c = pl.program_id(1)

        @pl.when(c == 0)
        def _():
            h_ref[...] = _rms(x_ref[...], gpre_ref[...]).astype(BF16)

        a = _dot(h_ref[...], w1_ref[...])
        a_ref[...] = a.astype(BF16)
        r = jnp.square(jnp.maximum(a, 0.0)).astype(BF16)
        _accumulate(acc, _dot(r, w2_ref[...]), c == 0)

        @pl.when(c == N_CHIPS - 1)
        def _():
            y = acc[...]
            y_ref[...] = y
            xo_ref[...] = x_ref[...] + _rms(y, gpost_ref[...])

    row = pl.BlockSpec((ROW_TILE, d), lambda i, c: (i, 0))
    vec = pl.BlockSpec((1, d), lambda i, c: (0, 0))
    return _call(
        body, name=name, grid=(t // ROW_TILE, N_CHIPS),
        in_specs=[row, vec,
                  pl.BlockSpec((None, d, hc), lambda i, c: (c, 0, 0)),
                  pl.BlockSpec((None, hc, d), lambda i, c: (c, 0, 0)), vec],
        out_specs=[row, row, pl.BlockSpec((ROW_TILE, hc), lambda i, c: (i, c)), row],
        out_shape=[jax.ShapeDtypeStruct((t, d), F32), jax.ShapeDtypeStruct((t, d), BF16),
                   jax.ShapeDtypeStruct((t, N_CHIPS * hc), BF16), jax.ShapeDtypeStruct((t, d), F32)],
        scratch_shapes=[pltpu.VMEM((ROW_TILE, d), F32)],
        args=(x, gpre, w1g, w2g, gpost), exchange=exchange)


def ffn_bwd(dxo, x, y, a, gpre, gpost, w1g, w2g, name, exchange=None):
    t, d = x.shape
    hc = w1g.shape[2]

    def body(dxo_ref, x_ref, y_ref, a_ref, gpre_ref, gpost_ref, w1_ref, w2_ref,
             dxi_ref, dy_ref, da_ref, dgpre_ref, dgpost_ref, acc):
        i, c = pl.program_id(0), pl.program_id(1)

        @pl.when(c == 0)
        def _():
            dy, dg = _rms_bwd(y_ref[...], gpost_ref[...], dxo_ref[...])
            dy_ref[...] = dy.astype(BF16)
            _accumulate(dgpost_ref, dg, i == 0)

        dr = _dot_nt(dy_ref[...], w2_ref[...])
        da = (dr * (2.0 * jnp.maximum(a_ref[...].astype(F32), 0.0))).astype(BF16)
        da_ref[...] = da
        _accumulate(acc, _dot_nt(da, w1_ref[...]), c == 0)

        @pl.when(c == N_CHIPS - 1)
        def _():
            dx, dg = _rms_bwd(x_ref[...], gpre_ref[...], acc[...])
            dxi_ref[...] = dxo_ref[...] + dx
            _accumulate(dgpre_ref, dg, i == 0)

    row = pl.BlockSpec((ROW_TILE, d), lambda i, c: (i, 0))
    vec = pl.BlockSpec((1, d), lambda i, c: (0, 0))
    hid = pl.BlockSpec((ROW_TILE, hc), lambda i, c: (i, c))
    return _call(
        body, name=name, grid=(t // ROW_TILE, N_CHIPS),
        in_specs=[row, row, row, hid, vec, vec,
                  pl.BlockSpec((None, d, hc), lambda i, c: (c, 0, 0)),
                  pl.BlockSpec((None, hc, d), lambda i, c: (c, 0, 0))],
        out_specs=[row, row, hid, vec, vec],
        out_shape=[jax.ShapeDtypeStruct((t, d), F32), jax.ShapeDtypeStruct((t, d), BF16),
                   jax.ShapeDtypeStruct((t, N_CHIPS * hc), BF16),
                   jax.ShapeDtypeStruct((1, d), F32), jax.ShapeDtypeStruct((1, d), F32)],
        scratch_shapes=[pltpu.VMEM((ROW_TILE, d), F32)],
        args=(dxo, x, y, a, gpre, gpost, w1g, w2g), exchange=exchange)


def weight_grad(a, b, chunked, bk, bn, relu2, name):
    t = a.shape[0]
    a_on = chunked == "a"
    n_steps = t // ROW_TILE

    def body(a_ref, b_ref, o_ref, acc):
        s = pl.program_id(1)
        av = a_ref[...]
        if relu2:
            av = jnp.square(jnp.maximum(av.astype(F32), 0.0))
        _accumulate(acc, _dot_tn(av.astype(BF16), b_ref[...].astype(BF16)), s == 0)

        @pl.when(s == n_steps - 1)
        def _():
            o_ref[...] = acc[...].astype(BF16)

    return pl.pallas_call(
        body, name=name, grid=(N_CHIPS, n_steps),
        in_specs=[pl.BlockSpec((ROW_TILE, bk), (lambda c, s: (s, c)) if a_on else (lambda c, s: (s, 0))),
                  pl.BlockSpec((ROW_TILE, bn), (lambda c, s: (s, 0)) if a_on else (lambda c, s: (s, c)))],
        out_specs=pl.BlockSpec((None, bk, bn), lambda c, s: (c, 0, 0)),
        out_shape=jax.ShapeDtypeStruct((N_CHIPS, bk, bn), BF16),
        scratch_shapes=[pltpu.VMEM((bk, bn), F32)],
        compiler_params=_params(("arbitrary", "arbitrary")),
    )(a, b)


def loss_grad(xf, target, name):
    t, d = xf.shape

    def body(x_ref, t_ref, dy_ref, l_ref):
        e = x_ref[...] - t_ref[...]
        dy_ref[...] = e * (1.0 / d)
        part = jnp.sum(jnp.sum(e * e, axis=-1, keepdims=True), axis=0, keepdims=True) * (0.5 / d)
        _accumulate(l_ref, part, pl.program_id(0) == 0)

    row = pl.BlockSpec((ROW_TILE, d), lambda i: (i, 0))
    return pl.pallas_call(
        body, name=name, grid=(t // ROW_TILE,),
        in_specs=[row, row],
        out_specs=[row, pl.BlockSpec((1, 1), lambda i: (0, 0))],
        out_shape=[jax.ShapeDtypeStruct((t, d), F32), jax.ShapeDtypeStruct((1, 1), F32)],
        compiler_params=_params(("arbitrary",)),
    )(xf, target)


def _hgrn2_chunk(st, qs, fls, ivs, gls, l0, l1, l2, ng):
    nsub = len(qs)
    mx = jnp.maximum(jnp.maximum(l0, l1), l2)
    e0, e1, e2 = jnp.exp(l0 - mx), jnp.exp(l1 - mx), jnp.exp(l2 - mx)
    lb = e0 / (e0 + e1 + e2)
    rows = lax.broadcasted_iota(jnp.int32, (A_SUB, A_SUB), 0)
    cols = lax.broadcasted_iota(jnp.int32, (A_SUB, A_SUB), 1)
    tri = (rows >= cols).astype(F32)
    keep = (lax.broadcasted_iota(jnp.int32, (A_SUB, A_SUB, A_DK), 0)
            >= lax.broadcasted_iota(jnp.int32, (A_SUB, A_SUB, A_DK), 1))
    base = jnp.zeros_like(l0)
    bases, gs, ks, qfs = [], [], [], []
    for i in range(nsub):
        f = lb + (1.0 - lb) * jax.nn.sigmoid(fls[i])
        logf = jnp.log(f)
        bases.append(base)
        gs.append(base + jnp.dot(tri, logf, precision=lax.Precision.HIGHEST, preferred_element_type=F32))
        base = base + jnp.sum(logf, axis=0, keepdims=True)
        ks.append(1.0 - f)
        qfs.append(jax.nn.silu(qs[i]))
    g_last = base
    stb = st.astype(BF16)
    outs = []
    for i in range(nsub):
        o = _dot_nt((qfs[i] * jnp.exp(gs[i])).astype(BF16), stb)
        if i > 0:
            qt = (qfs[i] * jnp.exp(gs[i] - bases[i])).astype(BF16)
            kk = jnp.concatenate([ks[j] * jnp.exp(bases[i] - gs[j]) for j in range(i)], axis=0).astype(BF16)
            vv = jnp.concatenate(ivs[:i], axis=0).astype(BF16)
            o = o + _dot(_dot_nt(qt, kk).astype(BF16), vv)
        dec = jnp.exp(jnp.where(keep, gs[i][:, None, :] - gs[i][None, :, :], NEG_BIG))
        s_diag = jnp.sum(qfs[i][:, None, :] * ks[i][None, :, :] * dec, axis=-1)
        o = o + _dot(s_diag.astype(BF16), ivs[i].astype(BF16))
        o = o * lax.rsqrt(jnp.mean(o * o, axis=-1, keepdims=True) + EPS) * ng
        outs.append(o * jax.nn.silu(gls[i]))
    kdec = jnp.concatenate([ks[j] * jnp.exp(g_last - gs[j]) for j in range(nsub)], axis=0).astype(BF16)
    vall = jnp.concatenate(ivs, axis=0).astype(BF16)
    new_st = st * jnp.exp(g_last) + _dot_tn(vall, kdec)
    return new_st, outs


def _sub_blocks(ref, head):
    lanes = slice(head * A_DK, (head + 1) * A_DK)
    return [ref[i * A_SUB:(i + 1) * A_SUB, lanes] for i in range(A_CHUNK // A_SUB)]


def hgrn2_fwd(proj, lb_table, a_norm, batch, name, exchange=None):
    t = proj.shape[0]
    n_chunks = t // batch // A_CHUNK
    nblk = A_WIDTH // A_DK

    def body(q_ref, f_ref, i_ref, g_ref, lb_ref, ng_ref, o_ref, st_ref, st):
        @pl.when(pl.program_id(1) == 0)
        def _():
            st[...] = jnp.zeros_like(st)

        st_ref[...] = st[...]
        for h in range(A_HEADS):
            lanes = slice(h * A_DK, (h + 1) * A_DK)
            new_st, outs = _hgrn2_chunk(
                st[h], _sub_blocks(q_ref, h), _sub_blocks(f_ref, h), _sub_blocks(i_ref, h), _sub_blocks(g_ref, h),
                lb_ref[0:1, lanes], lb_ref[1:2, lanes], lb_ref[2:3, lanes], ng_ref[:, lanes])
            st[h] = new_st
            for i, o in enumerate(outs):
                o_ref[i * A_SUB:(i + 1) * A_SUB, lanes] = o.astype(BF16)

    def part(k):
        return pl.BlockSpec((A_CHUNK, A_WIDTH), lambda b, n: (b * n_chunks + n, k))

    return _call(
        body, name=name, grid=(batch, n_chunks),
        in_specs=[part(0), part(1), part(2), part(3),
                  pl.BlockSpec((3, A_WIDTH), lambda b, n: (0, 0)), pl.BlockSpec((1, A_WIDTH), lambda b, n: (0, 0))],
        out_specs=[pl.BlockSpec((A_CHUNK, A_WIDTH), lambda b, n: (b * n_chunks + n, 0)),
                   pl.BlockSpec((None, A_HEADS, A_DK, A_DK), lambda b, n: (b * n_chunks + n, 0, 0, 0))],
        out_shape=[jax.ShapeDtypeStruct((t, A_WIDTH), BF16),
                   jax.ShapeDtypeStruct((t // A_CHUNK, A_HEADS, A_DK, A_DK), F32)],
        scratch_shapes=[pltpu.VMEM((A_HEADS, A_DK, A_DK), F32)],
        args=(proj, proj, proj, proj, lb_table, a_norm), exchange=exchange)


def hgrn2_bwd(proj, states, lb_table, a_norm, do, batch, name, exchange=None):
    t = proj.shape[0]
    n_chunks = t // batch // A_CHUNK

    def body(q_ref, f_ref, i_ref, g_ref, st_ref, lb_ref, ng_ref, do_ref, dp_ref, dlb_ref, dng_ref, dst):
        @pl.when(jnp.logical_and(pl.program_id(0) == 0, pl.program_id(1) == 0))
        def _():
            dlb_ref[...] = jnp.zeros_like(dlb_ref)
            dng_ref[...] = jnp.zeros_like(dng_ref)

        @pl.when(pl.program_id(1) == 0)
        def _():
            dst[...] = jnp.zeros_like(dst)

        for h in range(A_HEADS):
            lanes = slice(h * A_DK, (h + 1) * A_DK)
            _, vjp = jax.vjp(
                _hgrn2_chunk, st_ref[h], _sub_blocks(q_ref, h), _sub_blocks(f_ref, h), _sub_blocks(i_ref, h),
                _sub_blocks(g_ref, h), lb_ref[0:1, lanes], lb_ref[1:2, lanes], lb_ref[2:3, lanes], ng_ref[:, lanes])
            douts = [x.astype(F32) for x in _sub_blocks(do_ref, h)]
            d_st, dqs, dfs, dis, dgs, dl0, dl1, dl2, dng = vjp((dst[h], douts))
            dst[h] = d_st
            for k, parts in enumerate((dqs, dfs, dis, dgs)):
                for i in range(A_CHUNK // A_SUB):
                    dp_ref[i * A_SUB:(i + 1) * A_SUB, k * A_WIDTH + h * A_DK:k * A_WIDTH + (h + 1) * A_DK] = parts[i]
            for row, val in enumerate((dl0, dl1, dl2)):
                dlb_ref[row:row + 1, lanes] += val
            dng_ref[:, lanes] += dng

    def rev(b, n):
        return b * n_chunks + (n_chunks - 1 - n)

    def part(k):
        return pl.BlockSpec((A_CHUNK, A_WIDTH), lambda b, n: (rev(b, n), k))

    const3 = pl.BlockSpec((3, A_WIDTH), lambda b, n: (0, 0))
    const1 = pl.BlockSpec((1, A_WIDTH), lambda b, n: (0, 0))
    return _call(
        body, name=name, grid=(batch, n_chunks),
        in_specs=[part(0), part(1), part(2), part(3),
                  pl.BlockSpec((None, A_HEADS, A_DK, A_DK), lambda b, n: (rev(b, n), 0, 0, 0)),
                  const3, const1, part(0)],
        out_specs=[pl.BlockSpec((A_CHUNK, 4 * A_WIDTH), lambda b, n: (rev(b, n), 0)), const3, const1],
        out_shape=[jax.ShapeDtypeStruct((t, 4 * A_WIDTH + 2 * B_WIDTH), F32),
                   jax.ShapeDtypeStruct((3, A_WIDTH), F32), jax.ShapeDtypeStruct((1, A_WIDTH), F32)],
        scratch_shapes=[pltpu.VMEM((A_HEADS, A_DK, A_DK), F32)],
        args=(proj, proj, proj, proj, states, lb_table, a_norm, do), exchange=exchange)


B_GDIM = B_WIDTH // B_GROUPS
B_ROWS = 512


def _gmlp_chunk(ubs, vbs, lngs, lnbs, ws, bcols):
    vs = [jax.nn.gelu(v) for v in vbs]
    mu = sum(jnp.sum(v, axis=-1, keepdims=True) for v in vs) * (1.0 / B_WIDTH)
    var = sum(jnp.sum(jnp.square(v - mu), axis=-1, keepdims=True) for v in vs) * (1.0 / B_WIDTH)
    rstd = lax.rsqrt(var + EPS)
    tril = (lax.broadcasted_iota(jnp.int32, (B_CHUNK, B_CHUNK), 0)
            >= lax.broadcasted_iota(jnp.int32, (B_CHUNK, B_CHUNK), 1))
    outs = []
    for g in range(B_GROUPS):
        vn = (vs[g] - mu) * rstd * lngs[g] + lnbs[g]
        w = jnp.where(tril, ws[g], 0.0).astype(BF16)
        outs.append(jax.nn.gelu(ubs[g]) * (_dot(w, vn.astype(BF16)) + bcols[g]))
    return outs


def _gmlp_args(u_ref, v_ref, lng_ref, lnb_ref, w_ref, bt_ref, rows):
    def groups(ref):
        return [ref[rows, g * B_GDIM:(g + 1) * B_GDIM] for g in range(B_GROUPS)]

    def vec(ref):
        return [ref[:, g * B_GDIM:(g + 1) * B_GDIM] for g in range(B_GROUPS)]

    return (groups(u_ref), groups(v_ref), vec(lng_ref), vec(lnb_ref),
            [w_ref[g] for g in range(B_GROUPS)], [bt_ref[:, g:g + 1] for g in range(B_GROUPS)])


def gmlp_fwd(proj, oa, ln_g, ln_b, w, bias_t, name):
    t = proj.shape[0]

    def body(u_ref, v_ref, oa_ref, lng_ref, lnb_ref, w_ref, bt_ref, o_ref):
        o_ref[:, 0:A_WIDTH] = oa_ref[...]
        for n in range(B_ROWS // B_CHUNK):
            rows = slice(n * B_CHUNK, (n + 1) * B_CHUNK)
            outs = _gmlp_chunk(*_gmlp_args(u_ref, v_ref, lng_ref, lnb_ref, w_ref, bt_ref, rows))
            for g, o in enumerate(outs):
                o_ref[rows, A_WIDTH + g * B_GDIM:A_WIDTH + (g + 1) * B_GDIM] = o.astype(BF16)

    vec = pl.BlockSpec((1, B_WIDTH), lambda i: (0, 0))
    return pl.pallas_call(
        body, name=name, grid=(t // B_ROWS,),
        in_specs=[pl.BlockSpec((B_ROWS, B_WIDTH), lambda i: (i, 4)), pl.BlockSpec((B_ROWS, B_WIDTH), lambda i: (i, 5)),
                  pl.BlockSpec((B_ROWS, A_WIDTH), lambda i: (i, 0)), vec, vec,
                  pl.BlockSpec((B_GROUPS, B_CHUNK, B_CHUNK), lambda i: (0, 0, 0)),
                  pl.BlockSpec((B_CHUNK, B_GROUPS), lambda i: (0, 0))],
        out_specs=pl.BlockSpec((B_ROWS, A_WIDTH + B_WIDTH), lambda i: (i, 0)),
        out_shape=jax.ShapeDtypeStruct((t, A_WIDTH + B_WIDTH), BF16),
        compiler_params=_params(("arbitrary",)),
    )(proj, proj, oa, ln_g, ln_b, w, bias_t)


def gmlp_bwd(proj, dmixin, ln_g, ln_b, w, bias_t, dproj, name):
    t = proj.shape[0]

    def body(u_ref, v_ref, do_ref, lng_ref, lnb_ref, w_ref, bt_ref, dp_in_ref,
             dp_ref, dlng_ref, dlnb_ref, dw_ref, dbt_ref):
        del dp_in_ref

        @pl.when(pl.program_id(0) == 0)
        def _():
            for ref in (dlng_ref, dlnb_ref, dw_ref, dbt_ref):
                ref[...] = jnp.zeros_like(ref)

        for n in range(B_ROWS // B_CHUNK):
            rows = slice(n * B_CHUNK, (n + 1) * B_CHUNK)
            _, vjp = jax.vjp(_gmlp_chunk, *_gmlp_args(u_ref, v_ref, lng_ref, lnb_ref, w_ref, bt_ref, rows))
            douts = [do_ref[rows, g * B_GDIM:(g + 1) * B_GDIM] for g in range(B_GROUPS)]
            dus, dvs, dlngs, dlnbs, dws, dbs = vjp(douts)
            for g in range(B_GROUPS):
                lanes = slice(g * B_GDIM, (g + 1) * B_GDIM)
                dp_ref[rows, lanes] = dus[g]
                dp_ref[rows, B_WIDTH + g * B_GDIM:B_WIDTH + (g + 1) * B_GDIM] = dvs[g]
                dlng_ref[:, lanes] += dlngs[g]
                dlnb_ref[:, lanes] += dlnbs[g]
                dw_ref[g] += dws[g]
                dbt_ref[:, g:g + 1] += dbs[g]

    vec = pl.BlockSpec((1, B_WIDTH), lambda i: (0, 0))
    wspec = pl.BlockSpec((B_GROUPS, B_CHUNK, B_CHUNK), lambda i: (0, 0, 0))
    bspec = pl.BlockSpec((B_CHUNK, B_GROUPS), lambda i: (0, 0))
    return pl.pallas_call(
        body, name=name, grid=(t // B_ROWS,),
        in_specs=[pl.BlockSpec((B_ROWS, B_WIDTH), lambda i: (i, 4)), pl.BlockSpec((B_ROWS, B_WIDTH), lambda i: (i, 5)),
                  pl.BlockSpec((B_ROWS, B_WIDTH), lambda i: (i, 1)), vec, vec, wspec, bspec,
                  pl.BlockSpec(memory_space=pl.ANY)],
        out_specs=[pl.BlockSpec((B_ROWS, 2 * B_WIDTH), lambda i: (i, 2)), vec, vec, wspec, bspec],
        out_shape=[jax.ShapeDtypeStruct(dproj.shape, F32), jax.ShapeDtypeStruct((1, B_WIDTH), F32),
                   jax.ShapeDtypeStruct((1, B_WIDTH), F32), jax.ShapeDtypeStruct((B_GROUPS, B_CHUNK, B_CHUNK), F32),
                   jax.ShapeDtypeStruct((B_CHUNK, B_GROUPS), F32)],
        input_output_aliases={7: 0},
        compiler_params=_params(("arbitrary",)),
    )(proj, proj, dmixin, ln_g, ln_b, w, bias_t, dproj)


C_PAIR = 2 * C_HEAD_DIM
C_PAIRS = C_HEADS // 2
C_SCALE = 1.0 / math.sqrt(C_HEAD_DIM)
C_ROT_DIM = 2 * C_ROT_HALF
ROPE_ROWS = 1024


def rope_tables(pos_col, name):
    t = pos_col.shape[0]

    def body(p_ref, c_ref, a_ref, b_ref):
        lane = jnp.bitwise_and(lax.broadcasted_iota(jnp.int32, (1, C_PAIR), 1), C_HEAD_DIM - 1)
        j = jnp.bitwise_and(lane, C_ROT_HALF - 1).astype(F32)
        inv = jnp.exp(j * (-math.log(ROPE_THETA) / C_ROT_HALF))
        ang = p_ref[...].astype(F32) * inv
        cos, sin = jnp.cos(ang), jnp.sin(ang)
        c_ref[...] = jnp.where(lane < C_ROT_DIM, cos, 1.0)
        a_ref[...] = jnp.where(lane < C_ROT_HALF, -sin, 0.0)
        b_ref[...] = jnp.where(jnp.logical_and(lane >= C_ROT_HALF, lane < C_ROT_DIM), sin, 0.0)

    tab = pl.BlockSpec((ROPE_ROWS, C_PAIR), lambda i: (i, 0))
    return pl.pallas_call(
        body, name=name, grid=(t // ROPE_ROWS,),
        in_specs=[pl.BlockSpec((ROPE_ROWS, 1), lambda i: (i, 0))],
        out_specs=[tab, tab, tab],
        out_shape=[jax.ShapeDtypeStruct((t, C_PAIR), F32)] * 3,
        compiler_params=_params(("arbitrary",)),
    )(pos_col)


def _rope(x, c, a, b):
    return x * c + pltpu.roll(x, C_PAIR - C_ROT_HALF, 1) * a + pltpu.roll(x, C_ROT_HALF, 1) * b


def _rope_t(d, c, a, b):
    return d * c + pltpu.roll(d * a, C_ROT_HALF, 1) + pltpu.roll(d * b, C_PAIR - C_ROT_HALF, 1)


def _attn_rows(idx, dil):
    nblk = SEQ // dil // C_BLOCK
    r, n = idx // nblk, idx % nblk
    start = r + dil * C_BLOCK * n
    prev = r + dil * C_BLOCK * jnp.maximum(n - 1, 0)
    if dil == 1:
        return pl.ds(pl.multiple_of(start, C_BLOCK), C_BLOCK), pl.ds(pl.multiple_of(prev, C_BLOCK), C_BLOCK), n > 0
    return pl.ds(start, C_BLOCK, stride=dil), pl.ds(prev, C_BLOCK, stride=dil), n > 0


def _head_masks():
    low = lax.broadcasted_iota(jnp.int32, (1, C_PAIR), 1) < C_HEAD_DIM
    return low, jnp.logical_not(low)


def _attn_mask(has_prev):
    i = jnp.bitwise_and(lax.broadcasted_iota(jnp.int32, (2 * C_BLOCK, 2 * C_BLOCK), 0), C_BLOCK - 1)
    j = lax.broadcasted_iota(jnp.int32, (2 * C_BLOCK, 2 * C_BLOCK), 1)
    return jnp.logical_or(j <= i, jnp.logical_and(j - C_BLOCK >= i, has_prev))


def _stack_heads(x):
    low, high = _head_masks()
    return jnp.concatenate([jnp.where(low, x, 0.0), jnp.where(high, x, 0.0)], axis=0)


def _unstack_heads(x):
    low, _ = _head_masks()
    return jnp.where(low, x[:C_BLOCK], x[C_BLOCK:])


def attn_fwd(qkv, cos_t, sin_a, sin_b, batch, name, exchange=None):
    t = qkv.shape[0]
    nbr = len(C_DILATIONS)

    def body(q_ref, k_ref, v_ref, c_ref, a_ref, b_ref, o_ref, l_ref, qs, ks, *stats):
        acc, mm, dd = stats[0:nbr], stats[nbr:2 * nbr], stats[2 * nbr:3 * nbr]
        c, a, b = c_ref[...], a_ref[...], b_ref[...]
        qs[...] = _rope(q_ref[...], c, a, b) * C_SCALE
        ks[...] = _rope(k_ref[...], c, a, b)
        def load(idx, dil):
            rows, prev, has_prev = _attn_rows(idx, dil)
            return rows, (has_prev, qs[rows, :], ks[rows, :], ks[prev, :], v_ref[rows, :], v_ref[prev, :])

        def scores(has_prev, q, k_own, k_prev, v_own, v_prev):
            k_cat = jnp.concatenate([k_own, k_prev], axis=0).astype(BF16)
            return jnp.where(_attn_mask(has_prev), _dot_nt(_stack_heads(q).astype(BF16), k_cat), NEG_BIG)

        def softmax(s):
            m = jnp.max(s, axis=-1, keepdims=True)
            p = jnp.exp(s - m)
            return p.astype(BF16), m, jnp.sum(p, axis=-1, keepdims=True)

        def values(pb, has_prev, q, k_own, k_prev, v_own, v_prev):
            low, high = _head_masks()
            v_cat = jnp.concatenate([v_own, v_prev], axis=0)
            p_wide = jnp.concatenate([pb[:C_BLOCK], pb[C_BLOCK:]], axis=1)
            v_tall = jnp.concatenate([jnp.where(low, v_cat, 0.0), jnp.where(high, v_cat, 0.0)], axis=0).astype(BF16)
            return _dot(p_wide, v_tall)

        for bi, dil in enumerate(C_DILATIONS):
            def pair(i, carry, bi=bi, dil=dil):
                low, _ = _head_masks()
                loaded = [load(2 * i + k, dil) for k in range(2)]
                ss = [scores(*ops) for _, ops in loaded]
                sm = [softmax(s) for s in ss]
                pvs = [values(pb, *ops) for (pb, _, _), (_, ops) in zip(sm, loaded)]
                for (rows, _), (_, m, den), pv in zip(loaded, sm, pvs):
                    acc[bi][rows, :] = pv
                    mm[bi][rows, :] = jnp.where(low, m[:C_BLOCK], m[C_BLOCK:])
                    dd[bi][rows, :] = jnp.where(low, den[:C_BLOCK], den[C_BLOCK:])
                return carry

            lax.fori_loop(0, SEQ // C_BLOCK // 2, pair, 0)
        step = 256
        for r0 in range(0, SEQ, step):
            rr = slice(r0, r0 + step)
            ms = [mm[g][rr, :] for g in range(nbr)]
            m_all = functools.reduce(jnp.maximum, ms)
            ws = [jnp.exp(m - m_all) for m in ms]
            num = sum(acc[g][rr, :] * ws[g] for g in range(nbr))
            den = sum(dd[g][rr, :] * ws[g] for g in range(nbr))
            o_ref[rr, :] = (num / den).astype(BF16)
            l_ref[rr, :] = m_all + jnp.log(den)

    def col(k):
        return pl.BlockSpec((SEQ, C_PAIR), lambda b, p: (b, k * C_PAIRS + p))

    tab = pl.BlockSpec((SEQ, C_PAIR), lambda b, p: (b, 0))
    return _call(
        body, name=name, grid=(batch, C_PAIRS),
        in_specs=[col(0), col(1), col(2), tab, tab, tab],
        out_specs=[col(0), col(0)],
        out_shape=[jax.ShapeDtypeStruct((t, D_MODEL), BF16), jax.ShapeDtypeStruct((t, D_MODEL), F32)],
        scratch_shapes=[pltpu.VMEM((SEQ, C_PAIR), F32)] * (2 + 3 * nbr),
        args=(qkv, qkv, qkv, cos_t, sin_a, sin_b), exchange=exchange)


def attn_bwd(qkv, cos_t, sin_a, sin_b, o, lse, do, batch, name, exchange=None):
    t = qkv.shape[0]

    def body(q_ref, k_ref, v_ref, c_ref, a_ref, b_ref, o_ref, l_ref, do_ref, dq_ref, dk_ref, dv_ref,
             qs, ks, dqs, dks, dvs, dlt):
        c, a, b = c_ref[...], a_ref[...], b_ref[...]
        qs[...] = _rope(q_ref[...], c, a, b) * C_SCALE
        ks[...] = _rope(k_ref[...], c, a, b)
        prod = do_ref[...] * o_ref[...].astype(F32)
        low = lax.broadcasted_iota(jnp.int32, (1, C_PAIR), 1) < C_HEAD_DIM
        s_low = jnp.sum(jnp.where(low, prod, 0.0), axis=-1, keepdims=True)
        s_all = jnp.sum(prod, axis=-1, keepdims=True)
        dlt[...] = jnp.where(low, s_low, s_all - s_low)
        dqs[...] = jnp.zeros_like(dqs)
        dks[...] = jnp.zeros_like(dks)
        dvs[...] = jnp.zeros_like(dvs)
        def load(idx, dil):
            rows, prev, has_prev = _attn_rows(idx, dil)
            return (rows, prev), (has_prev, qs[rows, :], do_ref[rows, :], ks[rows, :], ks[prev, :],
                                  v_ref[rows, :], v_ref[prev, :], l_ref[rows, :], dlt[rows, :])

        def operands(has_prev, q, do, k_own, k_prev, v_own, v_prev, l_full, d_full):
            lcol = jnp.concatenate([l_full[:, 0:1], l_full[:, C_HEAD_DIM:C_HEAD_DIM + 1]], axis=0)
            dcol = jnp.concatenate([d_full[:, 0:1], d_full[:, C_HEAD_DIM:C_HEAD_DIM + 1]], axis=0)
            return (_stack_heads(q).astype(BF16), _stack_heads(do).astype(BF16),
                    jnp.concatenate([k_own, k_prev], axis=0).astype(BF16),
                    jnp.concatenate([v_own, v_prev], axis=0).astype(BF16), lcol, dcol, _attn_mask(has_prev))

        for dil in C_DILATIONS:
            def pair(i, carry, dil=dil):
                loaded = [load(2 * i + k, dil) for k in range(2)]
                ops = [operands(*o) for _, o in loaded]
                ss = [_dot_nt(q_stack, k_cat) for q_stack, _, k_cat, _, _, _, _ in ops]
                dps = [_dot_nt(do_stack, v_cat) for _, do_stack, _, v_cat, _, _, _ in ops]
                ps = [jnp.exp(jnp.where(o[6], s, NEG_BIG) - o[4]) for s, o in zip(ss, ops)]
                dss = [(p * (dp - o[5])).astype(BF16) for p, dp, o in zip(ps, dps, ops)]
                dvs_ = [_dot_tn(p.astype(BF16), o[1]) for p, o in zip(ps, ops)]
                dks_ = [_dot_tn(ds, o[0]) for ds, o in zip(dss, ops)]
                dqs_ = [_unstack_heads(_dot(ds, o[2])) for ds, o in zip(dss, ops)]
                results = list(zip(dqs_, dks_, dvs_))
                for ((rows, prev), _), (dq, dk_cat, dv_cat) in zip(loaded, results):
                    dqs[rows, :] += dq
                    dks[rows, :] += dk_cat[:C_BLOCK]
                    dvs[rows, :] += dv_cat[:C_BLOCK]
                    dks[prev, :] += dk_cat[C_BLOCK:]
                    dvs[prev, :] += dv_cat[C_BLOCK:]
                return carry

            lax.fori_loop(0, SEQ // C_BLOCK // 2, pair, 0)
        dq_ref[...] = _rope_t(dqs[...] * C_SCALE, c, a, b)
        dk_ref[...] = _rope_t(dks[...], c, a, b)
        dv_ref[...] = dvs[...]

    def col(k):
        return pl.BlockSpec((SEQ, C_PAIR), lambda b, p: (b, k * C_PAIRS + p))

    tab = pl.BlockSpec((SEQ, C_PAIR), lambda b, p: (b, 0))
    out = jax.ShapeDtypeStruct((t, D_MODEL), F32)
    return _call(
        body, name=name, grid=(batch, C_PAIRS),
        in_specs=[col(0), col(1), col(2), tab, tab, tab, col(0), col(0), col(0)],
        out_specs=[col(0), col(0), col(0)],
        out_shape=[out, out, out],
        scratch_shapes=[pltpu.VMEM((SEQ, C_PAIR), F32)] * 6,
        args=(qkv, qkv, qkv, cos_t, sin_a, sin_b, o, lse, do), exchange=exchange)


def sibling_swap(arrays, name):
    n = len(arrays)

    def body(*refs):
        ins, outs = refs[:n], refs[n:2 * n]
        send_sems, recv_sems = refs[2 * n:]
        x, y, c, _ = _place()
        sends = []
        for a in range(n):
            cp = pltpu.make_async_remote_copy(
                src_ref=ins[a], dst_ref=outs[a], send_sem=send_sems.at[a], recv_sem=recv_sems.at[a],
                device_id=(x, y, 1 - c), device_id_type=MESH)
            cp.start()
            sends.append(cp)
        for cp in sends:
            cp.wait_recv()
        for cp in sends:
            cp.wait_send()

    return pl.pallas_call(
        body, name=name,
        in_specs=[ANY] * n, out_specs=[ANY] * n,
        out_shape=[jax.ShapeDtypeStruct(s.shape, s.dtype) for s in arrays],
        scratch_shapes=[pltpu.SemaphoreType.DMA((n,)), pltpu.SemaphoreType.DMA((n,))],
    )(*arrays)


def allreduce_small(slab, name):
    rows, lanes = slab.shape

    def body(x_ref, out_ref, gath, send_sems, recv_sems, local_sem):
        x, y, c, chips = _place()
        me, sibling = (x, y, c), (x, y, 1 - c)

        def slot(px, py, pc):
            return gath.at[4 * px + 2 * py + pc]

        def copy(k, block, to, src=None):
            return pltpu.make_async_remote_copy(
                src_ref=slot(*block) if src is None else src, dst_ref=slot(*block),
                send_sem=send_sems.at[k], recv_sem=recv_sems.at[k], device_id=to, device_id_type=MESH)

        mine = pltpu.make_async_copy(x_ref, slot(*me), local_sem)
        mine.start()
        first = [copy(0, me, sibling, src=x_ref)]
        first += [copy(1 + j, me, (*chip, c), src=x_ref) for j, chip in enumerate(chips)]
        for cp in first:
            cp.start()
        passed = [copy(4 + j, (*chip, c), sibling) for j, chip in enumerate(chips)]
        for j, chip in enumerate(chips):
            copy(1 + j, (*chip, c), me).wait_recv()
            passed[j].start()
        copy(0, sibling, me).wait_recv()
        for j, chip in enumerate(chips):
            copy(4 + j, (*chip, 1 - c), me).wait_recv()
        for cp in first + passed:
            cp.wait_send()
        mine.wait()
        total = gath[0]
        for d in range(1, N_DEV):
            total = total + gath[d]
        out_ref[...] = total

    return pl.pallas_call(
        body, name=name,
        in_specs=[pl.BlockSpec(memory_space=pltpu.VMEM)],
        out_specs=pl.BlockSpec(memory_space=pltpu.VMEM),
        out_shape=jax.ShapeDtypeStruct((rows, lanes), F32),
        scratch_shapes=[pltpu.VMEM((N_DEV, rows, lanes), F32),
                        pltpu.SemaphoreType.DMA((7,)), pltpu.SemaphoreType.DMA((7,)), pltpu.SemaphoreType.DMA],
    )(slab)


ELT_ROWS = 512


def reduce_slabs(r, name, part=0, parts=1, into=None):
    _, rows, cols = r.shape
    br = min(rows, ELT_ROWS)
    nblk = rows // br

    def body(r_ref, *rest):
        o_ref = rest[-1]
        o_ref[...] = ((r_ref[3].astype(F32) + r_ref[0].astype(F32)) + r_ref[1].astype(F32)) + r_ref[2].astype(F32)

    return pl.pallas_call(
        body, name=name, grid=(nblk,),
        in_specs=[pl.BlockSpec((N_CHIPS, br, cols), lambda i: (0, i, 0))] + ([] if into is None else [ANY]),
        out_specs=pl.BlockSpec((br, cols), lambda i: (part * nblk + i, 0)),
        out_shape=jax.ShapeDtypeStruct((parts * rows, cols), F32),
        input_output_aliases={} if into is None else {1: 0},
        compiler_params=_params(("arbitrary",)),
    )(*([r] if into is None else [r, into]))


def _adamw(w, g, m, v):
    m = ADAM_B1 * m + (1.0 - ADAM_B1) * g
    v = ADAM_B2 * v + (1.0 - ADAM_B2) * jnp.square(g)
    m_hat = m / (1.0 - ADAM_B1 ** ADAM_STEP)
    v_hat = v / (1.0 - ADAM_B2 ** ADAM_STEP)
    delta = -ADAM_LR * (m_hat / (jnp.sqrt(v_hat) + ADAM_EPS) + ADAM_WD * w)
    return delta, m, v


def adamw_big(w, s_mine, s_sibling, m, v, name):
    rows, cols = w.shape

    def body(w_ref, a_ref, b_ref, m_ref, v_ref, g_out, d_out, m_out, v_out):
        g = a_ref[...] + b_ref[...]
        g_out[...] = g
        d_out[...], m_out[...], v_out[...] = _adamw(w_ref[...], g, m_ref[...], v_ref[...])

    blk = pl.BlockSpec((min(rows, ELT_ROWS), cols), lambda i: (i, 0))
    out = jax.ShapeDtypeStruct((rows, cols), F32)
    return pl.pallas_call(
        body, name=name, grid=(rows // min(rows, ELT_ROWS),),
        in_specs=[blk] * 5, out_specs=[blk] * 4, out_shape=[out] * 4,
        compiler_params=_params(("arbitrary",)),
    )(w, s_mine, s_sibling, m, v)


def adamw_small(ws, gs, ms, vs, name):
    n = len(ws)

    def body(*refs):
        w_refs, g_refs, m_refs, v_refs = (refs[k * n:(k + 1) * n] for k in range(4))
        d_out, m_out, v_out = (refs[(4 + k) * n:(5 + k) * n] for k in range(3))
        for i in range(n):
            d_out[i][...], m_out[i][...], v_out[i][...] = _adamw(
                w_refs[i][...], g_refs[i][...], m_refs[i][...], v_refs[i][...])

    outs = [jax.ShapeDtypeStruct(w.shape, F32) for w in ws]
    res = pl.pallas_call(body, name=name, out_shape=outs * 3)(*ws, *gs, *ms, *vs)
    return res[:n], res[n:2 * n], res[2 * n:]


SLAB_LANES = 128
SLAB_ROW_ALIGN = 8


def _pack(parts):
    flat = jnp.concatenate([p.reshape(-1) for p in parts])
    rows = -(-flat.shape[0] // (SLAB_LANES * SLAB_ROW_ALIGN)) * SLAB_ROW_ALIGN
    flat = jnp.pad(flat, (0, rows * SLAB_LANES - flat.shape[0]))
    return flat.reshape(rows, SLAB_LANES)


def _unpack(slab, shapes):
    flat = slab.reshape(-1)
    out, pos = [], 0
    for s in shapes:
        size = math.prod(s)
        out.append(flat[pos:pos + size].reshape(s))
        pos += size
    return out


def kernel(x, positions, norm_mix_pre, norm_mix_post, norm_ffn_pre, norm_ffn_post, w_in_even, lb_table, a_norm, b_ln_g, b_ln_b, b_ws, b_bias, w_out_even, w_in_odd, w_out_odd, w_ff1, w_ff2, loss_target, m_norm_mix_pre, m_norm_mix_post, m_norm_ffn_pre, m_norm_ffn_post, m_w_in_even, m_lb_table, m_a_norm, m_b_ln_g, m_b_ln_b, m_b_ws, m_b_bias, m_w_out_even, m_w_in_odd, m_w_out_odd, m_w_ff1, m_w_ff2, v_norm_mix_pre, v_norm_mix_post, v_norm_ffn_pre, v_norm_ffn_post, v_w_in_even, v_lb_table, v_a_norm, v_b_ln_g, v_b_ln_b, v_b_ws, v_b_bias, v_w_out_even, v_w_in_odd, v_w_out_odd, v_w_ff1, v_w_ff2):
    batch = x.shape[0]
    t = batch * SEQ
    d = D_MODEL
    x0 = x.reshape(t, d)
    target = loss_target.reshape(t, d)

    def gain(p, layer):
        return p[layer:layer + 1]

    def gather(*shards):
        return _Exchange("gather", [w.astype(BF16) for w in shards])

    def scatter(*grads):
        return _Exchange("scatter", grads)

    (win_e,) = exchange_alone(gather(w_in_even[0]), "gather_in_even")
    bias_t = b_bias[0].T
    proj, h0, wout_e = norm_matmul(x0, gain(norm_mix_pre, 0), win_e, "in_proj_even", exchange=gather(w_out_even[0]))
    oa, states, w1_0, w2_0 = hgrn2_fwd(proj, lb_table, a_norm, batch, "hgrn2_fwd",
                                       exchange=gather(w_ff1[0], w_ff2[0]))
    mixin = gmlp_fwd(proj, oa, b_ln_g, b_ln_b, b_ws[0], bias_t, "gmlp_fwd")
    mix0, x1 = out_proj(mixin, wout_e, x0, gain(norm_mix_post, 0), "out_proj_even")
    x2, hf0, a0, y0, win_o, wout_o = ffn_fwd(x1, gain(norm_ffn_pre, 0), w1_0, w2_0, gain(norm_ffn_post, 0),
                                             "ffn_fwd_0", exchange=gather(w_in_odd[0], w_out_odd[0]))
    qkv, h1 = norm_matmul(x2, gain(norm_mix_pre, 1), win_o, "in_proj_odd")
    cos_t, sin_a, sin_b = rope_tables(positions.reshape(t, 1), "rope_tables")
    ao, lse, w1_1, w2_1 = attn_fwd(qkv, cos_t, sin_a, sin_b, batch, "attn_fwd", exchange=gather(w_ff1[1], w_ff2[1]))
    mix1, x3 = out_proj(ao, wout_o, x2, gain(norm_mix_post, 1), "out_proj_odd")
    x4, hf1, a1, y1 = ffn_fwd(x3, gain(norm_ffn_pre, 1), w1_1, w2_1, gain(norm_ffn_post, 1), "ffn_fwd_1")
    dx4, loss_part = loss_grad(x4, target, "loss_grad")
    loss = lax.psum(loss_part[0, 0], ("x", "y", "c"))

    hc = D_FF // N_CHIPS
    dx3, dy1, da1, dg_fpre1, dg_fpost1 = ffn_bwd(
        dx4, x3, y1, a1, gain(norm_ffn_pre, 1), gain(norm_ffn_post, 1), w1_1, w2_1, "ffn_bwd_1")
    g_w1_1 = weight_grad(hf1, da1, "b", d, hc, False, "wgrad_ff1_1")
    g_w2_1 = weight_grad(a1, dy1, "a", hc, d, True, "wgrad_ff2_1")
    dmix1, dao, dg_mpost1 = out_proj_bwd(dx3, mix1, gain(norm_mix_post, 1), wout_o, "out_proj_bwd_odd")
    g_wout_o = weight_grad(ao, dmix1, "a", d // N_CHIPS, d, False, "wgrad_out_odd")
    dq, dk, dv, r_w1_1, r_w2_1 = attn_bwd(qkv, cos_t, sin_a, sin_b, ao, lse, dao, batch, "attn_bwd",
                                          exchange=scatter(g_w1_1, g_w2_1))
    dqkv = jnp.concatenate([dq, dk, dv], axis=1)
    dx2, dg_mpre1 = norm_matmul_bwd(dqkv, win_o, x2, gain(norm_mix_pre, 1), dx3, "in_proj_bwd_odd")
    g_win_o = weight_grad(h1, dqkv, "b", d, 3 * d // N_CHIPS, False, "wgrad_in_odd")
    dx1, dy0, da0, dg_fpre0, dg_fpost0, r_wout_o, r_win_o = ffn_bwd(
        dx2, x1, y0, a0, gain(norm_ffn_pre, 0), gain(norm_ffn_post, 0), w1_0, w2_0, "ffn_bwd_0",
        exchange=scatter(g_wout_o, g_win_o))
    g_w1_0 = weight_grad(hf0, da0, "b", d, hc, False, "wgrad_ff1_0")
    g_w2_0 = weight_grad(a0, dy0, "a", hc, d, True, "wgrad_ff2_0")
    dmix0, dmixin, dg_mpost0 = out_proj_bwd(dx1, mix0, gain(norm_mix_post, 0), wout_e, "out_proj_bwd_even")
    g_wout_e = weight_grad(mixin, dmix0, "a", d // N_CHIPS, d, False, "wgrad_out_even")
    dproj, d_lb, d_anorm, r_w1_0, r_w2_0, r_wout_e = hgrn2_bwd(
        proj, states, lb_table, a_norm, dmixin, batch, "hgrn2_bwd", exchange=scatter(g_w1_0, g_w2_0, g_wout_e))
    dproj, d_lng, d_lnb, d_ws, d_bias_t = gmlp_bwd(proj, dmixin, b_ln_g, b_ln_b, b_ws[0], bias_t, dproj, "gmlp_bwd")
    dx0, dg_mpre0 = norm_matmul_bwd(dproj, win_e, x0, gain(norm_mix_pre, 0), dx1, "in_proj_bwd_even")
    g_win_e = weight_grad(h0, dproj, "b", d, 3 * d // N_CHIPS, False, "wgrad_in_even")
    (r_win_e,) = exchange_alone(scatter(g_win_e), "scatter_in_even")
    grad_x = dx0.reshape(x.shape)

    s_w1 = reduce_slabs(r_w1_1, "reduce_ff1_1", part=1, parts=2)
    s_w1 = reduce_slabs(r_w1_0, "reduce_ff1_0", part=0, parts=2, into=s_w1)
    s_w2 = reduce_slabs(r_w2_1, "reduce_ff2_1", part=1, parts=2)
    s_w2 = reduce_slabs(r_w2_0, "reduce_ff2_0", part=0, parts=2, into=s_w2)
    sums = [reduce_slabs(r_win_e, "reduce_in_even"), reduce_slabs(r_wout_e, "reduce_out_even"),
            reduce_slabs(r_win_o, "reduce_in_odd"), reduce_slabs(r_wout_o, "reduce_out_odd"), s_w1, s_w2]
    sibling = sibling_swap(sums, "sibling_swap")
    big_w = [w_in_even, w_out_even, w_in_odd, w_out_odd, w_ff1, w_ff2]
    big_m = [m_w_in_even, m_w_out_even, m_w_in_odd, m_w_out_odd, m_w_ff1, m_w_ff2]
    big_v = [v_w_in_even, v_w_out_even, v_w_in_odd, v_w_out_odd, v_w_ff1, v_w_ff2]
    big = []
    for i, (w, m, v) in enumerate(zip(big_w, big_m, big_v)):
        two_d = (-1, w.shape[-1])
        res = adamw_big(w.reshape(two_d), sums[i], sibling[i], m.reshape(two_d), v.reshape(two_d), "adamw_big_%d" % i)
        big.append([r.reshape(w.shape) for r in res])

    small_w = [norm_mix_pre, norm_mix_post, norm_ffn_pre, norm_ffn_post, lb_table, a_norm, b_ln_g, b_ln_b, b_ws, b_bias]
    small_m = [m_norm_mix_pre, m_norm_mix_post, m_norm_ffn_pre, m_norm_ffn_post, m_lb_table, m_a_norm, m_b_ln_g,
               m_b_ln_b, m_b_ws, m_b_bias]
    small_v = [v_norm_mix_pre, v_norm_mix_post, v_norm_ffn_pre, v_norm_ffn_post, v_lb_table, v_a_norm, v_b_ln_g,
               v_b_ln_b, v_b_ws, v_b_bias]
    partial = [jnp.concatenate([dg_mpre0, dg_mpre1]), jnp.concatenate([dg_mpost0, dg_mpost1]),
               jnp.concatenate([dg_fpre0, dg_fpre1]), jnp.concatenate([dg_fpost0, dg_fpost1]),
               d_lb, d_anorm, d_lng, d_lnb, d_ws[None], d_bias_t.T[None]]
    small_g = _unpack(allreduce_small(_pack(partial), "allreduce_small"), [w.shape for w in small_w])
    small_d, small_nm, small_nv = adamw_small(small_w, small_g, small_m, small_v, "adamw_small")

    order = ["norm_mix_pre", "norm_mix_post", "norm_ffn_pre", "norm_ffn_post", "w_in_even", "lb_table", "a_norm",
             "b_ln_g", "b_ln_b", "b_ws", "b_bias", "w_out_even", "w_in_odd", "w_out_odd", "w_ff1", "w_ff2"]
    small_names = ["norm_mix_pre", "norm_mix_post", "norm_ffn_pre", "norm_ffn_post", "lb_table", "a_norm",
                   "b_ln_g", "b_ln_b", "b_ws", "b_bias"]
    big_names = ["w_in_even", "w_out_even", "w_in_odd", "w_out_odd", "w_ff1", "w_ff2"]
    grads, deltas, new_m, new_v = {}, {}, {}, {}
    for i, nm in enumerate(small_names):
        grads[nm], deltas[nm], new_m[nm], new_v[nm] = small_g[i], small_d[i], small_nm[i], small_nv[i]
    for i, nm in enumerate(big_names):
        grads[nm], deltas[nm], new_m[nm], new_v[nm] = big[i]
    return (loss, grad_x, *[grads[n] for n in order], *[deltas[n] for n in order],
            *[new_m[n] for n in order], *[new_v[n] for n in order])
```

```python
import functools
import math

import jax
import jax.numpy as jnp
from jax import lax
from jax.experimental import pallas as pl
from jax.experimental.pallas import tpu as pltpu

F32 = jnp.float32
BF16 = jnp.bfloat16
MESH = pl.DeviceIdType.MESH

D_MODEL = 1024
SEQ = 2048
D_FF = 4096
N_CHIPS = 4
A_WIDTH = 512
A_HEADS = 4
A_DK = 128
A_CHUNK = 64
A_SUB = 16
B_WIDTH = 512
B_GROUPS = 4
B_CHUNK = 128
C_HEADS = 16
C_HEAD_DIM = 64
C_ROT_HALF = 8
C_BLOCK = 128
C_DILATIONS = (1, 4, 16)
ROPE_THETA = 500000.0
EPS = 1e-6
ADAM_LR = 0.001
ADAM_B1 = 0.9
ADAM_B2 = 0.999
ADAM_EPS = 1e-08
ADAM_WD = 0.01
ADAM_STEP = 10

ROW_TILE = 512
VMEM_LIMIT = 56 * 1024 * 1024
NEG_BIG = -1e30


def _params(sem=None):
    return pltpu.CompilerParams(dimension_semantics=sem, vmem_limit_bytes=VMEM_LIMIT)


def _dot(a, b):
    return jnp.dot(a, b, preferred_element_type=F32)


def _dot_nt(a, b):
    return lax.dot_general(a, b, (((1,), (1,)), ((), ())), preferred_element_type=F32)


def _dot_tn(a, b):
    return lax.dot_general(a, b, (((0,), (0,)), ((), ())), preferred_element_type=F32)


def _rms(x, g):
    r = lax.rsqrt(jnp.mean(x * x, axis=-1, keepdims=True) + EPS)
    return x * r * g


def _rms_bwd(x, g, dy):
    r = lax.rsqrt(jnp.mean(x * x, axis=-1, keepdims=True) + EPS)
    xh = x * r
    dg = jnp.sum(dy * xh, axis=0, keepdims=True)
    dxh = dy * g
    dx = r * (dxh - xh * jnp.mean(dxh * xh, axis=-1, keepdims=True))
    return dx, dg


def _accumulate(ref, val, first):
    @pl.when(first)
    def _():
        ref[...] = val

    @pl.when(jnp.logical_not(first))
    def _():
        ref[...] += val


N_DEV = 8
ANY = pl.BlockSpec(memory_space=pl.ANY)


def _place():
    x, y, c = lax.axis_index("x"), lax.axis_index("y"), lax.axis_index("c")
    return x, y, c, [(1 - x, y), (x, 1 - y), (1 - x, 1 - y)]


class _Exchange:
    def __init__(self, kind, arrays):
        self.kind, self.arrays, self.n = kind, list(arrays), len(arrays)
        if kind == "gather":
            self.out_shape = [jax.ShapeDtypeStruct((N_CHIPS,) + a.shape, a.dtype) for a in self.arrays]
        else:
            self.out_shape = [jax.ShapeDtypeStruct(a.shape, a.dtype) for a in self.arrays]
        self.scratch = [pltpu.SemaphoreType.DMA((3 * self.n,)), pltpu.SemaphoreType.DMA((3 * self.n,)),
                        pltpu.SemaphoreType.DMA((self.n,))]

    def _copies(self, ins, outs, sems):
        send_sems, recv_sems, local_sems = sems
        x, y, c, chips = _place()
        me = 2 * x + y
        local, remote = [], []
        for a in range(self.n):
            if self.kind == "gather":
                local.append(pltpu.make_async_copy(ins[a], outs[a].at[me], local_sems.at[a]))
            else:
                local.append(pltpu.make_async_copy(ins[a].at[me], outs[a].at[3], local_sems.at[a]))
            for j, (px, py) in enumerate(chips):
                if self.kind == "gather":
                    src, dst, landed = ins[a], outs[a].at[me], outs[a].at[2 * px + py]
                else:
                    src, dst, landed = ins[a].at[2 * px + py], outs[a].at[j], outs[a].at[j]
                send = pltpu.make_async_remote_copy(
                    src_ref=src, dst_ref=dst, send_sem=send_sems.at[3 * a + j], recv_sem=recv_sems.at[3 * a + j],
                    device_id=(px, py, c), device_id_type=MESH)
                recv = pltpu.make_async_remote_copy(
                    src_ref=src, dst_ref=landed, send_sem=send_sems.at[3 * a + j], recv_sem=recv_sems.at[3 * a + j],
                    device_id=(px, py, c), device_id_type=MESH)
                remote.append((send, recv))
        return local, remote

    def start(self, ins, outs, sems):
        local, remote = self._copies(ins, outs, sems)
        for cp in local:
            cp.start()
        for send, _ in remote:
            send.start()

    def finish(self, ins, outs, sems):
        local, remote = self._copies(ins, outs, sems)
        for _, recv in remote:
            recv.wait_recv()
        for send, _ in remote:
            send.wait_send()
        for cp in local:
            cp.wait()


def _call(body, *, name, grid, in_specs, out_specs, out_shape, args, scratch_shapes=(), aliases=None, exchange=None):
    if exchange is None:
        return pl.pallas_call(
            body, name=name, grid=grid, in_specs=in_specs, out_specs=out_specs, out_shape=out_shape,
            scratch_shapes=list(scratch_shapes), input_output_aliases=aliases or {},
            compiler_params=_params(("arbitrary",) * len(grid)))(*args)
    n_in, n_out, n_scr, n_ex = len(in_specs), len(out_specs), len(scratch_shapes), exchange.n
    steps = grid

    def wrapped(*refs):
        ins, refs = refs[:n_in], refs[n_in:]
        ex_in, refs = refs[:n_ex], refs[n_ex:]
        outs, refs = refs[:n_out], refs[n_out:]
        ex_out, refs = refs[:n_ex], refs[n_ex:]
        scr, sems = refs[:n_scr], refs[n_scr:]
        first = functools.reduce(jnp.logical_and, [pl.program_id(k) == 0 for k in range(len(steps))])
        last = functools.reduce(jnp.logical_and, [pl.program_id(k) == steps[k] - 1 for k in range(len(steps))])

        @pl.when(first)
        def _():
            exchange.start(ex_in, ex_out, sems)

        body(*ins, *outs, *scr)

        @pl.when(last)
        def _():
            exchange.finish(ex_in, ex_out, sems)

    return pl.pallas_call(
        wrapped, name=name, grid=grid,
        in_specs=list(in_specs) + [ANY] * n_ex, out_specs=list(out_specs) + [ANY] * n_ex,
        out_shape=list(out_shape) + exchange.out_shape,
        scratch_shapes=list(scratch_shapes) + exchange.scratch, input_output_aliases=aliases or {},
        compiler_params=_params(("arbitrary",) * len(grid)))(*args, *exchange.arrays)


def exchange_alone(exchange, name):
    def body(*refs):
        n = exchange.n
        exchange.start(refs[:n], refs[n:2 * n], refs[2 * n:])
        exchange.finish(refs[:n], refs[n:2 * n], refs[2 * n:])

    return pl.pallas_call(
        body, name=name, in_specs=[ANY] * exchange.n, out_specs=[ANY] * exchange.n,
        out_shape=exchange.out_shape, scratch_shapes=exchange.scratch)(*exchange.arrays)


def norm_matmul(x, g, wg, name, exchange=None):
    t, d = x.shape
    nl = wg.shape[2]

    def body(x_ref, g_ref, w_ref, o_ref, h_ref):
        @pl.when(pl.program_id(1) == 0)
        def _():
            h_ref[...] = _rms(x_ref[...], g_ref[...]).astype(BF16)

        o_ref[...] = _dot(h_ref[...], w_ref[...])

    return _call(
        body, name=name, grid=(t // ROW_TILE, N_CHIPS),
        in_specs=[pl.BlockSpec((ROW_TILE, d), lambda i, c: (i, 0)),
                  pl.BlockSpec((1, d), lambda i, c: (0, 0)),
                  pl.BlockSpec((None, d, nl), lambda i, c: (c, 0, 0))],
        out_specs=[pl.BlockSpec((ROW_TILE, nl), lambda i, c: (i, c)),
                   pl.BlockSpec((ROW_TILE, d), lambda i, c: (i, 0))],
        out_shape=[jax.ShapeDtypeStruct((t, N_CHIPS * nl), F32), jax.ShapeDtypeStruct((t, d), BF16)],
        args=(x, g, wg), exchange=exchange)


def norm_matmul_bwd(dproj, wg, x, g, dres, name, exchange=None):
    t, d = x.shape
    nl = wg.shape[2]

    def body(dp_ref, w_ref, x_ref, g_ref, dres_ref, dx_ref, dg_ref, acc):
        i, c = pl.program_id(0), pl.program_id(1)
        part = _dot_nt(dp_ref[...].astype(BF16), w_ref[...])
        _accumulate(acc, part, c == 0)

        @pl.when(c == N_CHIPS - 1)
        def _():
            dx, dg = _rms_bwd(x_ref[...], g_ref[...], acc[...])
            dx_ref[...] = dres_ref[...] + dx
            _accumulate(dg_ref, dg, i == 0)

    return _call(
        body, name=name, grid=(t // ROW_TILE, N_CHIPS),
        in_specs=[pl.BlockSpec((ROW_TILE, nl), lambda i, c: (i, c)),
                  pl.BlockSpec((None, d, nl), lambda i, c: (c, 0, 0)),
                  pl.BlockSpec((ROW_TILE, d), lambda i, c: (i, 0)),
                  pl.BlockSpec((1, d), lambda i, c: (0, 0)),
                  pl.BlockSpec((ROW_TILE, d), lambda i, c: (i, 0))],
        out_specs=[pl.BlockSpec((ROW_TILE, d), lambda i, c: (i, 0)),
                   pl.BlockSpec((1, d), lambda i, c: (0, 0))],
        out_shape=[jax.ShapeDtypeStruct((t, d), F32), jax.ShapeDtypeStruct((1, d), F32)],
        scratch_shapes=[pltpu.VMEM((ROW_TILE, d), F32)],
        args=(dproj, wg, x, g, dres), exchange=exchange)


def out_proj(a, wg, x, g, name):
    t, d = x.shape
    kl = wg.shape[1]

    def body(a_ref, w_ref, x_ref, g_ref, mix_ref, xo_ref):
        acc = _dot(a_ref[:, 0:kl], w_ref[0])
        for c in range(1, N_CHIPS):
            acc += _dot(a_ref[:, c * kl:(c + 1) * kl], w_ref[c])
        mix_ref[...] = acc
        xo_ref[...] = x_ref[...] + _rms(acc, g_ref[...])

    row = pl.BlockSpec((ROW_TILE, d), lambda i: (i, 0))
    return pl.pallas_call(
        body, name=name, grid=(t // ROW_TILE,),
        in_specs=[row, pl.BlockSpec((N_CHIPS, kl, d), lambda i: (0, 0, 0)), row,
                  pl.BlockSpec((1, d), lambda i: (0, 0))],
        out_specs=[row, row],
        out_shape=[jax.ShapeDtypeStruct((t, d), F32), jax.ShapeDtypeStruct((t, d), F32)],
        compiler_params=_params(("arbitrary",)),
    )(a, wg, x, g)


def out_proj_bwd(dxo, mix, g, wg, name):
    t, d = mix.shape
    kl = wg.shape[1]

    def body(dxo_ref, mix_ref, g_ref, w_ref, dmix_ref, da_ref, dg_ref):
        dmix, dg = _rms_bwd(mix_ref[...], g_ref[...], dxo_ref[...])
        dmb = dmix.astype(BF16)
        dmix_ref[...] = dmb
        for c in range(N_CHIPS):
            da_ref[:, c * kl:(c + 1) * kl] = _dot_nt(dmb, w_ref[c])
        _accumulate(dg_ref, dg, pl.program_id(0) == 0)

    row = pl.BlockSpec((ROW_TILE, d), lambda i: (i, 0))
    vec = pl.BlockSpec((1, d), lambda i: (0, 0))
    return pl.pallas_call(
        body, name=name, grid=(t // ROW_TILE,),
        in_specs=[row, row, vec, pl.BlockSpec((N_CHIPS, kl, d), lambda i: (0, 0, 0))],
        out_specs=[row, row, vec],
        out_shape=[jax.ShapeDtypeStruct((t, d), BF16), jax.ShapeDtypeStruct((t, d), F32),
                   jax.ShapeDtypeStruct((1, d), F32)],
        compiler_params=_params(("arbitrary",)),
    )(dxo, mix, g, wg)


def ffn_fwd(x, gpre, w1g, w2g, gpost, name, exchange=None):
    t, d = x.shape
    hc = w1g.shape[2]

    def body(x_ref, gpre_ref, w1_ref, w2_ref, gpost_ref, xo_ref, h_ref, a_ref, y_ref, acc):
        c = pl.program_id(1)

        @pl.when(c == 0)
        def _():
            h_ref[...] = _rms(x_ref[...], gpre_ref[...]).astype(BF16)

        a = _dot(h_ref[...], w1_ref[...])
        a_ref[...] = a.astype(BF16)
        r = jnp.square(jnp.maximum(a, 0.0)).astype(BF16)
        _accumulate(acc, _dot(r, w2_ref[...]), c == 0)

        @pl.when(c == N_CHIPS - 1)
        def _():
            y = acc[...]
            y_ref[...] = y
            xo_ref[...] = x_ref[...] + _rms(y, gpost_ref[...])

    row = pl.BlockSpec((ROW_TILE, d), lambda i, c: (i, 0))
    vec = pl.BlockSpec((1, d), lambda i, c: (0, 0))
    return _call(
        body, name=name, grid=(t // ROW_TILE, N_CHIPS),
        in_specs=[row, vec,
                  pl.BlockSpec((None, d, hc), lambda i, c: (c, 0, 0)),
                  pl.BlockSpec((None, hc, d), lambda i, c: (c, 0, 0)), vec],
        out_specs=[row, row, pl.BlockSpec((ROW_TILE, hc), lambda i, c: (i, c)), row],
        out_shape=[jax.ShapeDtypeStruct((t, d), F32), jax.ShapeDtypeStruct((t, d), BF16),
                   jax.ShapeDtypeStruct((t, N_CHIPS * hc), BF16), jax.ShapeDtypeStruct((t, d), F32)],
        scratch_shapes=[pltpu.VMEM((ROW_TILE, d), F32)],
        args=(x, gpre, w1g, w2g, gpost), exchange=exchange)


def ffn_bwd(dxo, x, y, a, gpre, gpost, w1g, w2g, name, exchange=None):
    t, d = x.shape
    hc = w1g.shape[2]

    def body(dxo_ref, x_ref, y_ref, a_ref, gpre_ref, gpost_ref, w1_ref, w2_ref,
             dxi_ref, dy_ref, da_ref, dgpre_ref, dgpost_ref, acc):
        i, c = pl.program_id(0), pl.program_id(1)

        @pl.when(c == 0)
        def _():
            dy, dg = _rms_bwd(y_ref[...], gpost_ref[...], dxo_ref[...])
            dy_ref[...] = dy.astype(BF16)
            _accumulate(dgpost_ref, dg, i == 0)

        dr = _dot_nt(dy_ref[...], w2_ref[...])
        da = (dr * (2.0 * jnp.maximum(a_ref[...].astype(F32), 0.0))).astype(BF16)
        da_ref[...] = da
        _accumulate(acc, _dot_nt(da, w1_ref[...]), c == 0)

        @pl.when(c == N_CHIPS - 1)
        def _():
            dx, dg = _rms_bwd(x_ref[...], gpre_ref[...], acc[...])
            dxi_ref[...] = dxo_ref[...] + dx
            _accumulate(dgpre_ref, dg, i == 0)

    row = pl.BlockSpec((ROW_TILE, d), lambda i, c: (i, 0))
    vec = pl.BlockSpec((1, d), lambda i, c: (0, 0))
    hid = pl.BlockSpec((ROW_TILE, hc), lambda i, c: (i, c))
    return _call(
        body, name=name, grid=(t // ROW_TILE, N_CHIPS),
        in_specs=[row, row, row, hid, vec, vec,
                  pl.BlockSpec((None, d, hc), lambda i, c: (c, 0, 0)),
                  pl.BlockSpec((None, hc, d), lambda i, c: (c, 0, 0))],
        out_specs=[row, row, hid, vec, vec],
        out_shape=[jax.ShapeDtypeStruct((t, d), F32), jax.ShapeDtypeStruct((t, d), BF16),
                   jax.ShapeDtypeStruct((t, N_CHIPS * hc), BF16),
                   jax.ShapeDtypeStruct((1, d), F32), jax.ShapeDtypeStruct((1, d), F32)],
        scratch_shapes=[pltpu.VMEM((ROW_TILE, d), F32)],
        args=(dxo, x, y, a, gpre, gpost, w1g, w2g), exchange=exchange)


def weight_grad(a, b, chunked, bk, bn, relu2, name, exchange=None):
    t = a.shape[0]
    a_on = chunked == "a"
    n_steps = t // ROW_TILE

    def body(a_ref, b_ref, o_ref, acc):
        s = pl.program_id(1)
        av = a_ref[...]
        if relu2:
            av = jnp.square(jnp.maximum(av.astype(F32), 0.0))
        _accumulate(acc, _dot_tn(av.astype(BF16), b_ref[...].astype(BF16)), s == 0)

        @pl.when(s == n_steps - 1)
        def _():
            o_ref[...] = acc[...].astype(BF16)

    res = _call(
        body, name=name, grid=(N_CHIPS, n_steps),
        in_specs=[pl.BlockSpec((ROW_TILE, bk), (lambda c, s: (s, c)) if a_on else (lambda c, s: (s, 0))),
                  pl.BlockSpec((ROW_TILE, bn), (lambda c, s: (s, 0)) if a_on else (lambda c, s: (s, c)))],
        out_specs=[pl.BlockSpec((None, bk, bn), lambda c, s: (c, 0, 0))],
        out_shape=[jax.ShapeDtypeStruct((N_CHIPS, bk, bn), BF16)],
        scratch_shapes=[pltpu.VMEM((bk, bn), F32)],
        args=(a, b), exchange=exchange)
    return res[0] if exchange is None else res


def loss_grad(xf, target, name):
    t, d = xf.shape

    def body(x_ref, t_ref, dy_ref, l_ref):
        e = x_ref[...] - t_ref[...]
        dy_ref[...] = e * (1.0 / d)
        part = jnp.sum(jnp.sum(e * e, axis=-1, keepdims=True), axis=0, keepdims=True) * (0.5 / d)
        _accumulate(l_ref, part, pl.program_id(0) == 0)

    row = pl.BlockSpec((ROW_TILE, d), lambda i: (i, 0))
    return pl.pallas_call(
        body, name=name, grid=(t // ROW_TILE,),
        in_specs=[row, row],
        out_specs=[row, pl.BlockSpec((1, 1), lambda i: (0, 0))],
        out_shape=[jax.ShapeDtypeStruct((t, d), F32), jax.ShapeDtypeStruct((1, 1), F32)],
        compiler_params=_params(("arbitrary",)),
    )(xf, target)


def _hgrn2_chunk(st, qs, fls, ivs, gls, l0, l1, l2, ng):
    nsub = len(qs)
    mx = jnp.maximum(jnp.maximum(l0, l1), l2)
    e0, e1, e2 = jnp.exp(l0 - mx), jnp.exp(l1 - mx), jnp.exp(l2 - mx)
    lb = e0 / (e0 + e1 + e2)
    rows = lax.broadcasted_iota(jnp.int32, (A_SUB, A_SUB), 0)
    cols = lax.broadcasted_iota(jnp.int32, (A_SUB, A_SUB), 1)
    tri = (rows >= cols).astype(F32)
    keep = (lax.broadcasted_iota(jnp.int32, (A_SUB, A_SUB, A_DK), 0)
            >= lax.broadcasted_iota(jnp.int32, (A_SUB, A_SUB, A_DK), 1))
    base = jnp.zeros_like(l0)
    bases, gs, ks, qfs = [], [], [], []
    for i in range(nsub):
        f = lb + (1.0 - lb) * jax.nn.sigmoid(fls[i])
        logf = jnp.log(f)
        bases.append(base)
        gs.append(base + jnp.dot(tri, logf, precision=lax.Precision.HIGHEST, preferred_element_type=F32))
        base = base + jnp.sum(logf, axis=0, keepdims=True)
        ks.append(1.0 - f)
        qfs.append(jax.nn.silu(qs[i]))
    g_last = base
    stb = st.astype(BF16)
    outs = []
    for i in range(nsub):
        o = _dot_nt((qfs[i] * jnp.exp(gs[i])).astype(BF16), stb)
        if i > 0:
            qt = (qfs[i] * jnp.exp(gs[i] - bases[i])).astype(BF16)
            kk = jnp.concatenate([ks[j] * jnp.exp(bases[i] - gs[j]) for j in range(i)], axis=0).astype(BF16)
            vv = jnp.concatenate(ivs[:i], axis=0).astype(BF16)
            o = o + _dot(_dot_nt(qt, kk).astype(BF16), vv)
        dec = jnp.exp(jnp.where(keep, gs[i][:, None, :] - gs[i][None, :, :], NEG_BIG))
        s_diag = jnp.sum(qfs[i][:, None, :] * ks[i][None, :, :] * dec, axis=-1)
        o = o + _dot(s_diag.astype(BF16), ivs[i].astype(BF16))
        o = o * lax.rsqrt(jnp.mean(o * o, axis=-1, keepdims=True) + EPS) * ng
        outs.append(o * jax.nn.silu(gls[i]))
    kdec = jnp.concatenate([ks[j] * jnp.exp(g_last - gs[j]) for j in range(nsub)], axis=0).astype(BF16)
    vall = jnp.concatenate(ivs, axis=0).astype(BF16)
    new_st = st * jnp.exp(g_last) + _dot_tn(vall, kdec)
    return new_st, outs


A_MAX_LOG_DECAY = 80.0


def _split3(x):
    hi = x.astype(BF16)
    r1 = x - hi.astype(F32)
    mid = r1.astype(BF16)
    return hi, mid, (r1 - mid.astype(F32)).astype(BF16)


def _tri_matmul(x, transpose):
    n = x.shape[0]
    r = lax.broadcasted_iota(jnp.int32, (n, n), 0)
    c = lax.broadcasted_iota(jnp.int32, (n, n), 1)
    tri = ((r <= c) if transpose else (r >= c)).astype(BF16)
    hi, mid, lo = _split3(x)
    return (_dot(tri, lo) + _dot(tri, mid)) + _dot(tri, hi)


@jax.custom_vjp
def _cumsum_rows(x):
    return _tri_matmul(x, False)


def _cumsum_rows_fwd(x):
    return _tri_matmul(x, False), None


def _cumsum_rows_bwd(_, dy):
    return (_tri_matmul(dy, True),)


_cumsum_rows.defvjp(_cumsum_rows_fwd, _cumsum_rows_bwd)


def _lower_bound(l0, l1, l2):
    mx = jnp.maximum(jnp.maximum(l0, l1), l2)
    e0, e1, e2 = jnp.exp(l0 - mx), jnp.exp(l1 - mx), jnp.exp(l2 - mx)
    return e0 / (e0 + e1 + e2)


def _b(x):
    return x.astype(BF16)


@jax.custom_vjp
def _mm(a, b):
    return _dot(_b(a), _b(b))


_mm.defvjp(lambda a, b: (_mm(a, b), (a, b)),
           lambda res, d: (_dot_nt(_b(d), _b(res[1])), _dot_tn(_b(res[0]), _b(d))))


@jax.custom_vjp
def _mm_nt(a, b):
    return _dot_nt(_b(a), _b(b))


_mm_nt.defvjp(lambda a, b: (_mm_nt(a, b), (a, b)),
              lambda res, d: (_dot(_b(d), _b(res[1])), _dot_tn(_b(d), _b(res[0]))))


def _dot_split(dot, a, b):
    ah, bh = _b(a), _b(b)
    al, bl = _b(a - ah.astype(F32)), _b(b - bh.astype(F32))
    return (dot(ah, bl) + dot(al, bh)) + dot(ah, bh)


@jax.custom_vjp
def _mm_scores(a, b):
    return _dot_nt(_b(a), _b(b))


_mm_scores.defvjp(lambda a, b: (_mm_scores(a, b), (a, b)),
                  lambda res, d: (_dot_split(_dot, d, res[1]), _dot_split(_dot_tn, d, res[0])))


@jax.custom_vjp
def _mm_tn(a, b):
    return _dot_tn(_b(a), _b(b))


_mm_tn.defvjp(lambda a, b: (_mm_tn(a, b), (a, b)),
              lambda res, d: (_dot_nt(_b(res[1]), _b(d)), _dot(_b(res[0]), _b(d))))


@jax.custom_vjp
def _split_heads(x):
    return tuple(x[:, h * A_DK:(h + 1) * A_DK] for h in range(A_HEADS))


def _split_heads_fwd(x):
    return _split_heads(x), None


def _split_heads_bwd(_, parts):
    return (jnp.concatenate(parts, axis=1),)


_split_heads.defvjp(_split_heads_fwd, _split_heads_bwd)


def _hgrn2_chunk_fast(sts, q, fl, iv, gl, l0, l1, l2, ng):
    lb = _lower_bound(l0, l1, l2)
    f = lb + (1.0 - lb) * jax.nn.sigmoid(fl)
    logf = jnp.log(f)
    g = _cumsum_rows(logf)
    g_last = jnp.sum(logf, axis=0, keepdims=True)
    k = 1.0 - f
    qgs = _split_heads(jax.nn.silu(q) * jnp.exp(g))
    kgs = _split_heads(k * jnp.exp(-g))
    kds = _split_heads(k * jnp.exp(g_last - g))
    ivs = _split_heads(iv)
    decays = _split_heads(jnp.exp(g_last))
    n = q.shape[0]
    causal = lax.broadcasted_iota(jnp.int32, (n, n), 0) >= lax.broadcasted_iota(jnp.int32, (n, n), 1)
    raw = [_mm_scores(qg, kg) for qg, kg in zip(qgs, kgs)]
    inter = [_mm_nt(qg, st) for qg, st in zip(qgs, sts)]
    scores = [jnp.where(causal, s, 0.0) for s in raw]
    os = [a + _mm(s, v) for a, s, v in zip(inter, scores, ivs)]
    new_sts = [st * d + _mm_tn(v, kd) for st, d, v, kd in zip(sts, decays, ivs, kds)]
    os = [o * lax.rsqrt(jnp.mean(o * o, axis=-1, keepdims=True) + EPS) for o in os]
    return new_sts, jnp.concatenate(os, axis=1) * ng * jax.nn.silu(gl)


def _chunk_decays_mildly(f_ref, lb_ref):
    lb = _lower_bound(lb_ref[0:1, :], lb_ref[1:2, :], lb_ref[2:3, :])
    logf = jnp.log(lb + (1.0 - lb) * jax.nn.sigmoid(f_ref[...]))
    return jnp.min(jnp.sum(logf, axis=0, keepdims=True)) >= -A_MAX_LOG_DECAY


def _sub_blocks(ref, head):
    lanes = slice(head * A_DK, (head + 1) * A_DK)
    return [ref[i * A_SUB:(i + 1) * A_SUB, lanes] for i in range(A_CHUNK // A_SUB)]


def hgrn2_fwd(proj, lb_table, a_norm, batch, name, exchange=None):
    t = proj.shape[0]
    n_chunks = t // batch // A_CHUNK
    nblk = A_WIDTH // A_DK

    def body(q_ref, f_ref, i_ref, g_ref, lb_ref, ng_ref, o_ref, st_ref, st):
        @pl.when(pl.program_id(1) == 0)
        def _():
            st[...] = jnp.zeros_like(st)

        st_ref[...] = st[...]
        mild = _chunk_decays_mildly(f_ref, lb_ref)

        @pl.when(mild)
        def _():
            new_sts, o = _hgrn2_chunk_fast(
                [st[h] for h in range(A_HEADS)], q_ref[...], f_ref[...], i_ref[...], g_ref[...],
                lb_ref[0:1, :], lb_ref[1:2, :], lb_ref[2:3, :], ng_ref[...])
            for h in range(A_HEADS):
                st[h] = new_sts[h]
            o_ref[...] = o.astype(BF16)

        @pl.when(jnp.logical_not(mild))
        def _():
            for h in range(A_HEADS):
                lanes = slice(h * A_DK, (h + 1) * A_DK)
                new_st, outs = _hgrn2_chunk(
                    st[h], _sub_blocks(q_ref, h), _sub_blocks(f_ref, h), _sub_blocks(i_ref, h),
                    _sub_blocks(g_ref, h), lb_ref[0:1, lanes], lb_ref[1:2, lanes], lb_ref[2:3, lanes],
                    ng_ref[:, lanes])
                st[h] = new_st
                for i, o in enumerate(outs):
                    o_ref[i * A_SUB:(i + 1) * A_SUB, lanes] = o.astype(BF16)

    def part(k):
        return pl.BlockSpec((A_CHUNK, A_WIDTH), lambda b, n: (b * n_chunks + n, k))

    return _call(
        body, name=name, grid=(batch, n_chunks),
        in_specs=[part(0), part(1), part(2), part(3),
                  pl.BlockSpec((3, A_WIDTH), lambda b, n: (0, 0)), pl.BlockSpec((1, A_WIDTH), lambda b, n: (0, 0))],
        out_specs=[pl.BlockSpec((A_CHUNK, A_WIDTH), lambda b, n: (b * n_chunks + n, 0)),
                   pl.BlockSpec((None, A_HEADS, A_DK, A_DK), lambda b, n: (b * n_chunks + n, 0, 0, 0))],
        out_shape=[jax.ShapeDtypeStruct((t, A_WIDTH), BF16),
                   jax.ShapeDtypeStruct((t // A_CHUNK, A_HEADS, A_DK, A_DK), F32)],
        scratch_shapes=[pltpu.VMEM((A_HEADS, A_DK, A_DK), F32)],
        args=(proj, proj, proj, proj, lb_table, a_norm), exchange=exchange)


def hgrn2_bwd(proj, states, lb_table, a_norm, do, batch, name, exchange=None):
    t = proj.shape[0]
    n_chunks = t // batch // A_CHUNK

    def body(q_ref, f_ref, i_ref, g_ref, st_ref, lb_ref, ng_ref, do_ref, dp_ref, dlb_ref, dng_ref, dst):
        @pl.when(jnp.logical_and(pl.program_id(0) == 0, pl.program_id(1) == 0))
        def _():
            dlb_ref[...] = jnp.zeros_like(dlb_ref)
            dng_ref[...] = jnp.zeros_like(dng_ref)

        @pl.when(pl.program_id(1) == 0)
        def _():
            dst[...] = jnp.zeros_like(dst)

        mild = _chunk_decays_mildly(f_ref, lb_ref)

        @pl.when(mild)
        def _():
            _, vjp = jax.vjp(
                _hgrn2_chunk_fast, [st_ref[h] for h in range(A_HEADS)], q_ref[...], f_ref[...], i_ref[...],
                g_ref[...], lb_ref[0:1, :], lb_ref[1:2, :], lb_ref[2:3, :], ng_ref[...])
            d_sts, dq, df, di, dg, dl0, dl1, dl2, dng = vjp(([dst[h] for h in range(A_HEADS)], do_ref[...].astype(F32)))
            for h in range(A_HEADS):
                dst[h] = d_sts[h]
            for k, part in enumerate((dq, df, di, dg)):
                dp_ref[:, k * A_WIDTH:(k + 1) * A_WIDTH] = part
            for row, val in enumerate((dl0, dl1, dl2)):
                dlb_ref[row:row + 1, :] += val
            dng_ref[...] += dng

        @pl.when(jnp.logical_not(mild))
        def _():
            for h in range(A_HEADS):
                lanes = slice(h * A_DK, (h + 1) * A_DK)
                _, vjp = jax.vjp(
                    _hgrn2_chunk, st_ref[h], _sub_blocks(q_ref, h), _sub_blocks(f_ref, h), _sub_blocks(i_ref, h),
                    _sub_blocks(g_ref, h), lb_ref[0:1, lanes], lb_ref[1:2, lanes], lb_ref[2:3, lanes],
                    ng_ref[:, lanes])
                douts = [x.astype(F32) for x in _sub_blocks(do_ref, h)]
                d_st, dqs, dfs, dis, dgs, dl0, dl1, dl2, dng = vjp((dst[h], douts))
                dst[h] = d_st
                for k, parts in enumerate((dqs, dfs, dis, dgs)):
                    for i in range(A_CHUNK // A_SUB):
                        dp_ref[i * A_SUB:(i + 1) * A_SUB,
                               k * A_WIDTH + h * A_DK:k * A_WIDTH + (h + 1) * A_DK] = parts[i]
                for row, val in enumerate((dl0, dl1, dl2)):
                    dlb_ref[row:row + 1, lanes] += val
                dng_ref[:, lanes] += dng

    def rev(b, n):
        return b * n_chunks + (n_chunks - 1 - n)

    def part(k):
        return pl.BlockSpec((A_CHUNK, A_WIDTH), lambda b, n: (rev(b, n), k))

    const3 = pl.BlockSpec((3, A_WIDTH), lambda b, n: (0, 0))
    const1 = pl.BlockSpec((1, A_WIDTH), lambda b, n: (0, 0))
    return _call(
        body, name=name, grid=(batch, n_chunks),
        in_specs=[part(0), part(1), part(2), part(3),
                  pl.BlockSpec((None, A_HEADS, A_DK, A_DK), lambda b, n: (rev(b, n), 0, 0, 0)),
                  const3, const1, part(0)],
        out_specs=[pl.BlockSpec((A_CHUNK, 4 * A_WIDTH), lambda b, n: (rev(b, n), 0)), const3, const1],
        out_shape=[jax.ShapeDtypeStruct((t, 4 * A_WIDTH + 2 * B_WIDTH), F32),
                   jax.ShapeDtypeStruct((3, A_WIDTH), F32), jax.ShapeDtypeStruct((1, A_WIDTH), F32)],
        scratch_shapes=[pltpu.VMEM((A_HEADS, A_DK, A_DK), F32)],
        args=(proj, proj, proj, proj, states, lb_table, a_norm, do), exchange=exchange)


B_GDIM = B_WIDTH // B_GROUPS
B_ROWS = 512


def _gmlp_chunk(ubs, vbs, lngs, lnbs, ws, bcols):
    vs = [jax.nn.gelu(v) for v in vbs]
    mu = sum(jnp.sum(v, axis=-1, keepdims=True) for v in vs) * (1.0 / B_WIDTH)
    var = sum(jnp.sum(jnp.square(v - mu), axis=-1, keepdims=True) for v in vs) * (1.0 / B_WIDTH)
    rstd = lax.rsqrt(var + EPS)
    tril = (lax.broadcasted_iota(jnp.int32, (B_CHUNK, B_CHUNK), 0)
            >= lax.broadcasted_iota(jnp.int32, (B_CHUNK, B_CHUNK), 1))
    outs = []
    for g in range(B_GROUPS):
        vn = (vs[g] - mu) * rstd * lngs[g] + lnbs[g]
        w = jnp.where(tril, ws[g], 0.0).astype(BF16)
        outs.append(jax.nn.gelu(ubs[g]) * (_dot(w, vn.astype(BF16)) + bcols[g]))
    return outs


def _gmlp_args(u_ref, v_ref, lng_ref, lnb_ref, w_ref, bt_ref, rows):
    def groups(ref):
        return [ref[rows, g * B_GDIM:(g + 1) * B_GDIM] for g in range(B_GROUPS)]

    def vec(ref):
        return [ref[:, g * B_GDIM:(g + 1) * B_GDIM] for g in range(B_GROUPS)]

    return (groups(u_ref), groups(v_ref), vec(lng_ref), vec(lnb_ref),
            [w_ref[g] for g in range(B_GROUPS)], [bt_ref[:, g:g + 1] for g in range(B_GROUPS)])


def gmlp_fwd(proj, oa, ln_g, ln_b, w, bias_t, name, exchange=None):
    t = proj.shape[0]

    def body(u_ref, v_ref, oa_ref, lng_ref, lnb_ref, w_ref, bt_ref, o_ref):
        o_ref[:, 0:A_WIDTH] = oa_ref[...]
        for n in range(B_ROWS // B_CHUNK):
            rows = slice(n * B_CHUNK, (n + 1) * B_CHUNK)
            outs = _gmlp_chunk(*_gmlp_args(u_ref, v_ref, lng_ref, lnb_ref, w_ref, bt_ref, rows))
            for g, o in enumerate(outs):
                o_ref[rows, A_WIDTH + g * B_GDIM:A_WIDTH + (g + 1) * B_GDIM] = o.astype(BF16)

    vec = pl.BlockSpec((1, B_WIDTH), lambda i: (0, 0))
    return _call(
        body, name=name, grid=(t // B_ROWS,),
        in_specs=[pl.BlockSpec((B_ROWS, B_WIDTH), lambda i: (i, 4)), pl.BlockSpec((B_ROWS, B_WIDTH), lambda i: (i, 5)),
                  pl.BlockSpec((B_ROWS, A_WIDTH), lambda i: (i, 0)), vec, vec,
                  pl.BlockSpec((B_GROUPS, B_CHUNK, B_CHUNK), lambda i: (0, 0, 0)),
                  pl.BlockSpec((B_CHUNK, B_GROUPS), lambda i: (0, 0))],
        out_specs=[pl.BlockSpec((B_ROWS, A_WIDTH + B_WIDTH), lambda i: (i, 0))],
        out_shape=[jax.ShapeDtypeStruct((t, A_WIDTH + B_WIDTH), BF16)],
        args=(proj, proj, oa, ln_g, ln_b, w, bias_t), exchange=exchange)


def gmlp_bwd(proj, dmixin, ln_g, ln_b, w, bias_t, dproj, name, exchange=None):
    t = proj.shape[0]

    def body(u_ref, v_ref, do_ref, lng_ref, lnb_ref, w_ref, bt_ref, dp_in_ref,
             dp_ref, dlng_ref, dlnb_ref, dw_ref, dbt_ref):
        del dp_in_ref

        @pl.when(pl.program_id(0) == 0)
        def _():
            for ref in (dlng_ref, dlnb_ref, dw_ref, dbt_ref):
                ref[...] = jnp.zeros_like(ref)

        for n in range(B_ROWS // B_CHUNK):
            rows = slice(n * B_CHUNK, (n + 1) * B_CHUNK)
            _, vjp = jax.vjp(_gmlp_chunk, *_gmlp_args(u_ref, v_ref, lng_ref, lnb_ref, w_ref, bt_ref, rows))
            douts = [do_ref[rows, g * B_GDIM:(g + 1) * B_GDIM] for g in range(B_GROUPS)]
            dus, dvs, dlngs, dlnbs, dws, dbs = vjp(douts)
            for g in range(B_GROUPS):
                lanes = slice(g * B_GDIM, (g + 1) * B_GDIM)
                dp_ref[rows, lanes] = dus[g]
                dp_ref[rows, B_WIDTH + g * B_GDIM:B_WIDTH + (g + 1) * B_GDIM] = dvs[g]
                dlng_ref[:, lanes] += dlngs[g]
                dlnb_ref[:, lanes] += dlnbs[g]
                dw_ref[g] += dws[g]
                dbt_ref[:, g:g + 1] += dbs[g]

    vec = pl.BlockSpec((1, B_WIDTH), lambda i: (0, 0))
    wspec = pl.BlockSpec((B_GROUPS, B_CHUNK, B_CHUNK), lambda i: (0, 0, 0))
    bspec = pl.BlockSpec((B_CHUNK, B_GROUPS), lambda i: (0, 0))
    return _call(
        body, name=name, grid=(t // B_ROWS,),
        in_specs=[pl.BlockSpec((B_ROWS, B_WIDTH), lambda i: (i, 4)), pl.BlockSpec((B_ROWS, B_WIDTH), lambda i: (i, 5)),
                  pl.BlockSpec((B_ROWS, B_WIDTH), lambda i: (i, 1)), vec, vec, wspec, bspec,
                  pl.BlockSpec(memory_space=pl.ANY)],
        out_specs=[pl.BlockSpec((B_ROWS, 2 * B_WIDTH), lambda i: (i, 2)), vec, vec, wspec, bspec],
        out_shape=[jax.ShapeDtypeStruct(dproj.shape, F32), jax.ShapeDtypeStruct((1, B_WIDTH), F32),
                   jax.ShapeDtypeStruct((1, B_WIDTH), F32), jax.ShapeDtypeStruct((B_GROUPS, B_CHUNK, B_CHUNK), F32),
                   jax.ShapeDtypeStruct((B_CHUNK, B_GROUPS), F32)],
        aliases={7: 0}, args=(proj, proj, dmixin, ln_g, ln_b, w, bias_t, dproj), exchange=exchange)


C_PAIR = 2 * C_HEAD_DIM
C_PAIRS = C_HEADS // 2
C_SCALE = 1.0 / math.sqrt(C_HEAD_DIM)
C_ROT_DIM = 2 * C_ROT_HALF
ROPE_ROWS = 1024


def rope_tables(pos_col, name):
    t = pos_col.shape[0]

    def body(p_ref, c_ref, a_ref, b_ref):
        lane = jnp.bitwise_and(lax.broadcasted_iota(jnp.int32, (1, C_PAIR), 1), C_HEAD_DIM - 1)
        j = jnp.bitwise_and(lane, C_ROT_HALF - 1).astype(F32)
        inv = jnp.exp(j * (-math.log(ROPE_THETA) / C_ROT_HALF))
        ang = p_ref[...].astype(F32) * inv
        cos, sin = jnp.cos(ang), jnp.sin(ang)
        c_ref[...] = jnp.where(lane < C_ROT_DIM, cos, 1.0)
        a_ref[...] = jnp.where(lane < C_ROT_HALF, -sin, 0.0)
        b_ref[...] = jnp.where(jnp.logical_and(lane >= C_ROT_HALF, lane < C_ROT_DIM), sin, 0.0)

    tab = pl.BlockSpec((ROPE_ROWS, C_PAIR), lambda i: (i, 0))
    return pl.pallas_call(
        body, name=name, grid=(t // ROPE_ROWS,),
        in_specs=[pl.BlockSpec((ROPE_ROWS, 1), lambda i: (i, 0))],
        out_specs=[tab, tab, tab],
        out_shape=[jax.ShapeDtypeStruct((t, C_PAIR), F32)] * 3,
        compiler_params=_params(("arbitrary",)),
    )(pos_col)


def _rope(x, c, a, b):
    return x * c + pltpu.roll(x, C_PAIR - C_ROT_HALF, 1) * a + pltpu.roll(x, C_ROT_HALF, 1) * b


def _rope_t(d, c, a, b):
    return d * c + pltpu.roll(d * a, C_ROT_HALF, 1) + pltpu.roll(d * b, C_PAIR - C_ROT_HALF, 1)


def _attn_rows(idx, dil):
    nblk = SEQ // dil // C_BLOCK
    r, n = idx // nblk, idx % nblk
    start = r + dil * C_BLOCK * n
    prev = r + dil * C_BLOCK * jnp.maximum(n - 1, 0)
    if dil == 1:
        return pl.ds(pl.multiple_of(start, C_BLOCK), C_BLOCK), pl.ds(pl.multiple_of(prev, C_BLOCK), C_BLOCK), n > 0
    return pl.ds(start, C_BLOCK, stride=dil), pl.ds(prev, C_BLOCK, stride=dil), n > 0


def _head_masks():
    low = lax.broadcasted_iota(jnp.int32, (1, C_PAIR), 1) < C_HEAD_DIM
    return low, jnp.logical_not(low)


def _attn_mask(has_prev):
    i = jnp.bitwise_and(lax.broadcasted_iota(jnp.int32, (2 * C_BLOCK, 2 * C_BLOCK), 0), C_BLOCK - 1)
    j = lax.broadcasted_iota(jnp.int32, (2 * C_BLOCK, 2 * C_BLOCK), 1)
    return jnp.logical_or(j <= i, jnp.logical_and(j - C_BLOCK >= i, has_prev))


def _stack_heads(x):
    low, high = _head_masks()
    return jnp.concatenate([jnp.where(low, x, 0.0), jnp.where(high, x, 0.0)], axis=0)


def _unstack_heads(x):
    low, _ = _head_masks()
    return jnp.where(low, x[:C_BLOCK], x[C_BLOCK:])


def attn_fwd(qkv, cos_t, sin_a, sin_b, batch, name, exchange=None):
    t = qkv.shape[0]
    nbr = len(C_DILATIONS)

    def body(q_ref, k_ref, v_ref, c_ref, a_ref, b_ref, o_ref, l_ref, qs, ks, *stats):
        acc, mm, dd = stats[0:nbr], stats[nbr:2 * nbr], stats[2 * nbr:3 * nbr]
        c, a, b = c_ref[...], a_ref[...], b_ref[...]
        qs[...] = _rope(q_ref[...], c, a, b) * C_SCALE
        ks[...] = _rope(k_ref[...], c, a, b)
        def load(idx, dil):
            rows, prev, has_prev = _attn_rows(idx, dil)
            return rows, (has_prev, qs[rows, :], ks[rows, :], ks[prev, :], v_ref[rows, :], v_ref[prev, :])

        def scores(has_prev, q, k_own, k_prev, v_own, v_prev):
            k_cat = jnp.concatenate([k_own, k_prev], axis=0).astype(BF16)
            return jnp.where(_attn_mask(has_prev), _dot_nt(_stack_heads(q).astype(BF16), k_cat), NEG_BIG)

        def softmax(s):
            m = jnp.max(s, axis=-1, keepdims=True)
            p = jnp.exp(s - m)
            return p.astype(BF16), m, jnp.sum(p, axis=-1, keepdims=True)

        def values(pb, has_prev, q, k_own, k_prev, v_own, v_prev):
            low, high = _head_masks()
            v_cat = jnp.concatenate([v_own, v_prev], axis=0)
            p_wide = jnp.concatenate([pb[:C_BLOCK], pb[C_BLOCK:]], axis=1)
            v_tall = jnp.concatenate([jnp.where(low, v_cat, 0.0), jnp.where(high, v_cat, 0.0)], axis=0).astype(BF16)
            return _dot(p_wide, v_tall)

        for bi, dil in enumerate(C_DILATIONS):
            def pair(i, carry, bi=bi, dil=dil):
                low, _ = _head_masks()
                loaded = [load(2 * i + k, dil) for k in range(2)]
                ss = [scores(*ops) for _, ops in loaded]
                sm = [softmax(s) for s in ss]
                pvs = [values(pb, *ops) for (pb, _, _), (_, ops) in zip(sm, loaded)]
                for (rows, _), (_, m, den), pv in zip(loaded, sm, pvs):
                    acc[bi][rows, :] = pv
                    mm[bi][rows, :] = jnp.where(low, m[:C_BLOCK], m[C_BLOCK:])
                    dd[bi][rows, :] = jnp.where(low, den[:C_BLOCK], den[C_BLOCK:])
                return carry

            lax.fori_loop(0, SEQ // C_BLOCK // 2, pair, 0)
        step = 256
        for r0 in range(0, SEQ, step):
            rr = slice(r0, r0 + step)
            ms = [mm[g][rr, :] for g in range(nbr)]
            m_all = functools.reduce(jnp.maximum, ms)
            ws = [jnp.exp(m - m_all) for m in ms]
            num = sum(acc[g][rr, :] * ws[g] for g in range(nbr))
            den = sum(dd[g][rr, :] * ws[g] for g in range(nbr))
            o_ref[rr, :] = (num / den).astype(BF16)
            l_ref[rr, :] = m_all + jnp.log(den)

    def col(k):
        return pl.BlockSpec((SEQ, C_PAIR), lambda b, p: (b, k * C_PAIRS + p))

    tab = pl.BlockSpec((SEQ, C_PAIR), lambda b, p: (b, 0))
    return _call(
        body, name=name, grid=(batch, C_PAIRS),
        in_specs=[col(0), col(1), col(2), tab, tab, tab],
        out_specs=[col(0), col(0)],
        out_shape=[jax.ShapeDtypeStruct((t, D_MODEL), BF16), jax.ShapeDtypeStruct((t, D_MODEL), F32)],
        scratch_shapes=[pltpu.VMEM((SEQ, C_PAIR), F32)] * (2 + 3 * nbr),
        args=(qkv, qkv, qkv, cos_t, sin_a, sin_b), exchange=exchange)


def attn_bwd(qkv, cos_t, sin_a, sin_b, o, lse, do, batch, name, exchange=None):
    t = qkv.shape[0]

    def body(q_ref, k_ref, v_ref, c_ref, a_ref, b_ref, o_ref, l_ref, do_ref, dq_ref, dk_ref, dv_ref,
             qs, ks, dqs, dks, dvs, dlt):
        c, a, b = c_ref[...], a_ref[...], b_ref[...]
        qs[...] = _rope(q_ref[...], c, a, b) * C_SCALE
        ks[...] = _rope(k_ref[...], c, a, b)
        prod = do_ref[...] * o_ref[...].astype(F32)
        low = lax.broadcasted_iota(jnp.int32, (1, C_PAIR), 1) < C_HEAD_DIM
        s_low = jnp.sum(jnp.where(low, prod, 0.0), axis=-1, keepdims=True)
        s_all = jnp.sum(prod, axis=-1, keepdims=True)
        dlt[...] = jnp.where(low, s_low, s_all - s_low)
        dqs[...] = jnp.zeros_like(dqs)
        dks[...] = jnp.zeros_like(dks)
        dvs[...] = jnp.zeros_like(dvs)
        def load(idx, dil):
            rows, prev, has_prev = _attn_rows(idx, dil)
            return (rows, prev), (has_prev, qs[rows, :], do_ref[rows, :], ks[rows, :], ks[prev, :],
                                  v_ref[rows, :], v_ref[prev, :], l_ref[rows, :], dlt[rows, :])

        def operands(has_prev, q, do, k_own, k_prev, v_own, v_prev, l_full, d_full):
            lcol = jnp.concatenate([l_full[:, 0:1], l_full[:, C_HEAD_DIM:C_HEAD_DIM + 1]], axis=0)
            dcol = jnp.concatenate([d_full[:, 0:1], d_full[:, C_HEAD_DIM:C_HEAD_DIM + 1]], axis=0)
            return (_stack_heads(q).astype(BF16), _stack_heads(do).astype(BF16),
                    jnp.concatenate([k_own, k_prev], axis=0).astype(BF16),
                    jnp.concatenate([v_own, v_prev], axis=0).astype(BF16), lcol, dcol, _attn_mask(has_prev))

        for dil in C_DILATIONS:
            def pair(i, carry, dil=dil):
                loaded = [load(2 * i + k, dil) for k in range(2)]
                ops = [operands(*o) for _, o in loaded]
                ss = [_dot_nt(q_stack, k_cat) for q_stack, _, k_cat, _, _, _, _ in ops]
                dps = [_dot_nt(do_stack, v_cat) for _, do_stack, _, v_cat, _, _, _ in ops]
                ps = [jnp.exp(jnp.where(o[6], s, NEG_BIG) - o[4]) for s, o in zip(ss, ops)]
                dss = [(p * (dp - o[5])).astype(BF16) for p, dp, o in zip(ps, dps, ops)]
                dvs_ = [_dot_tn(p.astype(BF16), o[1]) for p, o in zip(ps, ops)]
                dks_ = [_dot_tn(ds, o[0]) for ds, o in zip(dss, ops)]
                dqs_ = [_unstack_heads(_dot(ds, o[2])) for ds, o in zip(dss, ops)]
                results = list(zip(dqs_, dks_, dvs_))
                for ((rows, prev), _), (dq, dk_cat, dv_cat) in zip(loaded, results):
                    dqs[rows, :] += dq
                    dks[rows, :] += dk_cat[:C_BLOCK]
                    dvs[rows, :] += dv_cat[:C_BLOCK]
                    dks[prev, :] += dk_cat[C_BLOCK:]
                    dvs[prev, :] += dv_cat[C_BLOCK:]
                return carry

            lax.fori_loop(0, SEQ // C_BLOCK // 2, pair, 0)
        dq_ref[...] = _rope_t(dqs[...] * C_SCALE, c, a, b)
        dk_ref[...] = _rope_t(dks[...], c, a, b)
        dv_ref[...] = dvs[...]

    def col(k):
        return pl.BlockSpec((SEQ, C_PAIR), lambda b, p: (b, k * C_PAIRS + p))

    tab = pl.BlockSpec((SEQ, C_PAIR), lambda b, p: (b, 0))
    out = jax.ShapeDtypeStruct((t, D_MODEL), F32)
    return _call(
        body, name=name, grid=(batch, C_PAIRS),
        in_specs=[col(0), col(1), col(2), tab, tab, tab, col(0), col(0), col(0)],
        out_specs=[col(0), col(0), col(0)],
        out_shape=[out, out, out],
        scratch_shapes=[pltpu.VMEM((SEQ, C_PAIR), F32)] * 6,
        args=(qkv, qkv, qkv, cos_t, sin_a, sin_b, o, lse, do), exchange=exchange)


def sibling_swap(arrays, name):
    n = len(arrays)

    def body(*refs):
        ins, outs = refs[:n], refs[n:2 * n]
        send_sems, recv_sems = refs[2 * n:]
        x, y, c, _ = _place()
        sends = []
        for a in range(n):
            cp = pltpu.make_async_remote_copy(
                src_ref=ins[a], dst_ref=outs[a], send_sem=send_sems.at[a], recv_sem=recv_sems.at[a],
                device_id=(x, y, 1 - c), device_id_type=MESH)
            cp.start()
            sends.append(cp)
        for cp in sends:
            cp.wait_recv()
        for cp in sends:
            cp.wait_send()

    return pl.pallas_call(
        body, name=name,
        in_specs=[ANY] * n, out_specs=[ANY] * n,
        out_shape=[jax.ShapeDtypeStruct(s.shape, s.dtype) for s in arrays],
        scratch_shapes=[pltpu.SemaphoreType.DMA((n,)), pltpu.SemaphoreType.DMA((n,))],
    )(*arrays)


def allreduce_small(slab, name):
    rows, lanes = slab.shape

    def body(x_ref, out_ref, gath, send_sems, recv_sems, local_sem):
        x, y, c, chips = _place()
        me, sibling = (x, y, c), (x, y, 1 - c)

        def slot(px, py, pc):
            return gath.at[4 * px + 2 * py + pc]

        def copy(k, block, to, src=None):
            return pltpu.make_async_remote_copy(
                src_ref=slot(*block) if src is None else src, dst_ref=slot(*block),
                send_sem=send_sems.at[k], recv_sem=recv_sems.at[k], device_id=to, device_id_type=MESH)

        mine = pltpu.make_async_copy(x_ref, slot(*me), local_sem)
        mine.start()
        first = [copy(0, me, sibling, src=x_ref)]
        first += [copy(1 + j, me, (*chip, c), src=x_ref) for j, chip in enumerate(chips)]
        for cp in first:
            cp.start()
        passed = [copy(4 + j, (*chip, c), sibling) for j, chip in enumerate(chips)]
        for j, chip in enumerate(chips):
            copy(1 + j, (*chip, c), me).wait_recv()
            passed[j].start()
        copy(0, sibling, me).wait_recv()
        for j, chip in enumerate(chips):
            copy(4 + j, (*chip, 1 - c), me).wait_recv()
        for cp in first + passed:
            cp.wait_send()
        mine.wait()
        total = gath[0]
        for d in range(1, N_DEV):
            total = total + gath[d]
        out_ref[...] = total

    return pl.pallas_call(
        body, name=name,
        in_specs=[pl.BlockSpec(memory_space=pltpu.VMEM)],
        out_specs=pl.BlockSpec(memory_space=pltpu.VMEM),
        out_shape=jax.ShapeDtypeStruct((rows, lanes), F32),
        scratch_shapes=[pltpu.VMEM((N_DEV, rows, lanes), F32),
                        pltpu.SemaphoreType.DMA((7,)), pltpu.SemaphoreType.DMA((7,)), pltpu.SemaphoreType.DMA],
    )(slab)


ELT_ROWS = 512


def reduce_slabs(r, name, part=0, parts=1, into=None):
    _, rows, cols = r.shape
    br = min(rows, ELT_ROWS)
    nblk = rows // br

    def body(r_ref, *rest):
        o_ref = rest[-1]
        o_ref[...] = ((r_ref[3].astype(F32) + r_ref[0].astype(F32)) + r_ref[1].astype(F32)) + r_ref[2].astype(F32)

    return pl.pallas_call(
        body, name=name, grid=(nblk,),
        in_specs=[pl.BlockSpec((N_CHIPS, br, cols), lambda i: (0, i, 0))] + ([] if into is None else [ANY]),
        out_specs=pl.BlockSpec((br, cols), lambda i: (part * nblk + i, 0)),
        out_shape=jax.ShapeDtypeStruct((parts * rows, cols), F32),
        input_output_aliases={} if into is None else {1: 0},
        compiler_params=_params(("arbitrary",)),
    )(*([r] if into is None else [r, into]))


def _adamw(w, g, m, v):
    m = ADAM_B1 * m + (1.0 - ADAM_B1) * g
    v = ADAM_B2 * v + (1.0 - ADAM_B2) * jnp.square(g)
    m_hat = m / (1.0 - ADAM_B1 ** ADAM_STEP)
    v_hat = v / (1.0 - ADAM_B2 ** ADAM_STEP)
    delta = -ADAM_LR * (m_hat / (jnp.sqrt(v_hat) + ADAM_EPS) + ADAM_WD * w)
    return delta, m, v


def adamw_big(w, s_mine, s_sibling, m, v, name):
    rows, cols = w.shape

    def body(w_ref, a_ref, b_ref, m_ref, v_ref, g_out, d_out, m_out, v_out):
        g = a_ref[...] + b_ref[...]
        g_out[...] = g
        d_out[...], m_out[...], v_out[...] = _adamw(w_ref[...], g, m_ref[...], v_ref[...])

    blk = pl.BlockSpec((min(rows, ELT_ROWS), cols), lambda i: (i, 0))
    out = jax.ShapeDtypeStruct((rows, cols), F32)
    return pl.pallas_call(
        body, name=name, grid=(rows // min(rows, ELT_ROWS),),
        in_specs=[blk] * 5, out_specs=[blk] * 4, out_shape=[out] * 4,
        compiler_params=_params(("arbitrary",)),
    )(w, s_mine, s_sibling, m, v)


def adamw_small(ws, gs, ms, vs, name):
    n = len(ws)

    def body(*refs):
        w_refs, g_refs, m_refs, v_refs = (refs[k * n:(k + 1) * n] for k in range(4))
        d_out, m_out, v_out = (refs[(4 + k) * n:(5 + k) * n] for k in range(3))
        for i in range(n):
            d_out[i][...], m_out[i][...], v_out[i][...] = _adamw(
                w_refs[i][...], g_refs[i][...], m_refs[i][...], v_refs[i][...])

    outs = [jax.ShapeDtypeStruct(w.shape, F32) for w in ws]
    res = pl.pallas_call(body, name=name, out_shape=outs * 3)(*ws, *gs, *ms, *vs)
    return res[:n], res[n:2 * n], res[2 * n:]


SLAB_LANES = 128
SLAB_ROW_ALIGN = 8


def _pack(parts):
    flat = jnp.concatenate([p.reshape(-1) for p in parts])
    rows = -(-flat.shape[0] // (SLAB_LANES * SLAB_ROW_ALIGN)) * SLAB_ROW_ALIGN
    flat = jnp.pad(flat, (0, rows * SLAB_LANES - flat.shape[0]))
    return flat.reshape(rows, SLAB_LANES)


def _unpack(slab, shapes):
    flat = slab.reshape(-1)
    out, pos = [], 0
    for s in shapes:
        size = math.prod(s)
        out.append(flat[pos:pos + size].reshape(s))
        pos += size
    return out


def kernel(x, positions, norm_mix_pre, norm_mix_post, norm_ffn_pre, norm_ffn_post, w_in_even, lb_table, a_norm, b_ln_g, b_ln_b, b_ws, b_bias, w_out_even, w_in_odd, w_out_odd, w_ff1, w_ff2, loss_target, m_norm_mix_pre, m_norm_mix_post, m_norm_ffn_pre, m_norm_ffn_post, m_w_in_even, m_lb_table, m_a_norm, m_b_ln_g, m_b_ln_b, m_b_ws, m_b_bias, m_w_out_even, m_w_in_odd, m_w_out_odd, m_w_ff1, m_w_ff2, v_norm_mix_pre, v_norm_mix_post, v_norm_ffn_pre, v_norm_ffn_post, v_w_in_even, v_lb_table, v_a_norm, v_b_ln_g, v_b_ln_b, v_b_ws, v_b_bias, v_w_out_even, v_w_in_odd, v_w_out_odd, v_w_ff1, v_w_ff2):
    batch = x.shape[0]
    t = batch * SEQ
    d = D_MODEL
    x0 = x.reshape(t, d)
    target = loss_target.reshape(t, d)

    def gain(p, layer):
        return p[layer:layer + 1]

    def gather(*shards):
        return _Exchange("gather", [w.astype(BF16) for w in shards])

    def scatter(*grads):
        return _Exchange("scatter", grads)

    (win_e,) = exchange_alone(gather(w_in_even[0]), "gather_in_even")
    bias_t = b_bias[0].T
    proj, h0, w1_0 = norm_matmul(x0, gain(norm_mix_pre, 0), win_e, "in_proj_even", exchange=gather(w_ff1[0]))
    oa, states, w2_0 = hgrn2_fwd(proj, lb_table, a_norm, batch, "hgrn2_fwd", exchange=gather(w_ff2[0]))
    mixin, wout_e = gmlp_fwd(proj, oa, b_ln_g, b_ln_b, b_ws[0], bias_t, "gmlp_fwd", exchange=gather(w_out_even[0]))
    mix0, x1 = out_proj(mixin, wout_e, x0, gain(norm_mix_post, 0), "out_proj_even")
    x2, hf0, a0, y0, win_o, wout_o = ffn_fwd(x1, gain(norm_ffn_pre, 0), w1_0, w2_0, gain(norm_ffn_post, 0),
                                             "ffn_fwd_0", exchange=gather(w_in_odd[0], w_out_odd[0]))
    qkv, h1 = norm_matmul(x2, gain(norm_mix_pre, 1), win_o, "in_proj_odd")
    cos_t, sin_a, sin_b = rope_tables(positions.reshape(t, 1), "rope_tables")
    ao, lse, w1_1, w2_1 = attn_fwd(qkv, cos_t, sin_a, sin_b, batch, "attn_fwd", exchange=gather(w_ff1[1], w_ff2[1]))
    mix1, x3 = out_proj(ao, wout_o, x2, gain(norm_mix_post, 1), "out_proj_odd")
    x4, hf1, a1, y1 = ffn_fwd(x3, gain(norm_ffn_pre, 1), w1_1, w2_1, gain(norm_ffn_post, 1), "ffn_fwd_1")
    dx4, loss_part = loss_grad(x4, target, "loss_grad")

    hc = D_FF // N_CHIPS
    dx3, dy1, da1, dg_fpre1, dg_fpost1 = ffn_bwd(
        dx4, x3, y1, a1, gain(norm_ffn_pre, 1), gain(norm_ffn_post, 1), w1_1, w2_1, "ffn_bwd_1")
    g_w1_1 = weight_grad(hf1, da1, "b", d, hc, False, "wgrad_ff1_1")
    g_w2_1 = weight_grad(a1, dy1, "a", hc, d, True, "wgrad_ff2_1")
    dmix1, dao, dg_mpost1 = out_proj_bwd(dx3, mix1, gain(norm_mix_post, 1), wout_o, "out_proj_bwd_odd")
    g_wout_o = weight_grad(ao, dmix1, "a", d // N_CHIPS, d, False, "wgrad_out_odd")
    dq, dk, dv, r_w1_1, r_w2_1, r_wout_o = attn_bwd(qkv, cos_t, sin_a, sin_b, ao, lse, dao, batch, "attn_bwd",
                                                    exchange=scatter(g_w1_1, g_w2_1, g_wout_o))
    dqkv = jnp.concatenate([dq, dk, dv], axis=1)
    dx2, dg_mpre1 = norm_matmul_bwd(dqkv, win_o, x2, gain(norm_mix_pre, 1), dx3, "in_proj_bwd_odd")
    g_win_o = weight_grad(h1, dqkv, "b", d, 3 * d // N_CHIPS, False, "wgrad_in_odd")
    dx1, dy0, da0, dg_fpre0, dg_fpost0, r_win_o = ffn_bwd(
        dx2, x1, y0, a0, gain(norm_ffn_pre, 0), gain(norm_ffn_post, 0), w1_0, w2_0, "ffn_bwd_0",
        exchange=scatter(g_win_o))
    g_w1_0 = weight_grad(hf0, da0, "b", d, hc, False, "wgrad_ff1_0")
    g_w2_0 = weight_grad(a0, dy0, "a", hc, d, True, "wgrad_ff2_0")
    dmix0, dmixin, dg_mpost0 = out_proj_bwd(dx1, mix0, gain(norm_mix_post, 0), wout_e, "out_proj_bwd_even")
    g_wout_e = weight_grad(mixin, dmix0, "a", d // N_CHIPS, d, False, "wgrad_out_even")
    dproj, d_lb, d_anorm, r_w1_0 = hgrn2_bwd(
        proj, states, lb_table, a_norm, dmixin, batch, "hgrn2_bwd", exchange=scatter(g_w1_0))
    dproj, d_lng, d_lnb, d_ws, d_bias_t, r_wout_e = gmlp_bwd(
        proj, dmixin, b_ln_g, b_ln_b, b_ws[0], bias_t, dproj, "gmlp_bwd", exchange=scatter(g_wout_e))
    g_win_e, r_w2_0 = weight_grad(h0, dproj, "b", d, 3 * d // N_CHIPS, False, "wgrad_in_even",
                                  exchange=scatter(g_w2_0))
    dx0, dg_mpre0, r_win_e = norm_matmul_bwd(dproj, win_e, x0, gain(norm_mix_pre, 0), dx1, "in_proj_bwd_even",
                                             exchange=scatter(g_win_e))
    grad_x = dx0.reshape(x.shape)

    s_w1 = reduce_slabs(r_w1_1, "reduce_ff1_1", part=1, parts=2)
    s_w1 = reduce_slabs(r_w1_0, "reduce_ff1_0", part=0, parts=2, into=s_w1)
    s_w2 = reduce_slabs(r_w2_1, "reduce_ff2_1", part=1, parts=2)
    s_w2 = reduce_slabs(r_w2_0, "reduce_ff2_0", part=0, parts=2, into=s_w2)
    sums = [reduce_slabs(r_win_e, "reduce_in_even"), reduce_slabs(r_wout_e, "reduce_out_even"),
            reduce_slabs(r_win_o, "reduce_in_odd"), reduce_slabs(r_wout_o, "reduce_out_odd"), s_w1, s_w2]
    sibling = sibling_swap(sums, "sibling_swap")
    big_w = [w_in_even, w_out_even, w_in_odd, w_out_odd, w_ff1, w_ff2]
    big_m = [m_w_in_even, m_w_out_even, m_w_in_odd, m_w_out_odd, m_w_ff1, m_w_ff2]
    big_v = [v_w_in_even, v_w_out_even, v_w_in_odd, v_w_out_odd, v_w_ff1, v_w_ff2]
    big = []
    for i, (w, m, v) in enumerate(zip(big_w, big_m, big_v)):
        two_d = (-1, w.shape[-1])
        res = adamw_big(w.reshape(two_d), sums[i], sibling[i], m.reshape(two_d), v.reshape(two_d), "adamw_big_%d" % i)
        big.append([r.reshape(w.shape) for r in res])

    small_w = [norm_mix_pre, norm_mix_post, norm_ffn_pre, norm_ffn_post, lb_table, a_norm, b_ln_g, b_ln_b, b_ws, b_bias]
    small_m = [m_norm_mix_pre, m_norm_mix_post, m_norm_ffn_pre, m_norm_ffn_post, m_lb_table, m_a_norm, m_b_ln_g,
               m_b_ln_b, m_b_ws, m_b_bias]
    small_v = [v_norm_mix_pre, v_norm_mix_post, v_norm_ffn_pre, v_norm_ffn_post, v_lb_table, v_a_norm, v_b_ln_g,
               v_b_ln_b, v_b_ws, v_b_bias]
    partial = [jnp.concatenate([dg_mpre0, dg_mpre1]), jnp.concatenate([dg_mpost0, dg_mpost1]),
               jnp.concatenate([dg_fpre0, dg_fpre1]), jnp.concatenate([dg_fpost0, dg_fpost1]),
               d_lb, d_anorm, d_lng, d_lnb, d_ws[None], d_bias_t.T[None]]
    *small_g, loss = _unpack(allreduce_small(_pack(partial + [loss_part]), "allreduce_small"),
                             [w.shape for w in small_w] + [()])
    small_d, small_nm, small_nv = adamw_small(small_w, small_g, small_m, small_v, "adamw_small")

    order = ["norm_mix_pre", "norm_mix_post", "norm_ffn_pre", "norm_ffn_post", "w_in_even", "lb_table", "a_norm",
             "b_ln_g", "b_ln_b", "b_ws", "b_bias", "w_out_even", "w_in_odd", "w_out_odd", "w_ff1", "w_ff2"]
    small_names = ["norm_mix_pre", "norm_mix_post", "norm_ffn_pre", "norm_ffn_post", "lb_table", "a_norm",
                   "b_ln_g", "b_ln_b", "b_ws", "b_bias"]
    big_names = ["w_in_even", "w_out_even", "w_in_odd", "w_out_odd", "w_ff1", "w_ff2"]
    grads, deltas, new_m, new_v = {}, {}, {}, {}
    for i, nm in enumerate(small_names):
        grads[nm], deltas[nm], new_m[nm], new_v[nm] = small_g[i], small_d[i], small_nm[i], small_nv[i]
    for i, nm in enumerate(big_names):
        grads[nm], deltas[nm], new_m[nm], new_v[nm] = big[i]
    return (loss, grad_x, *[grads[n] for n in order], *[deltas[n] for n in order],
            *[new_m[n] for n in order], *[new_v[n] for n in order])
```

```python
import functools
import math

import jax
import jax.numpy as jnp
from jax import lax
from jax.experimental import pallas as pl
from jax.experimental.pallas import tpu as pltpu

F32 = jnp.float32
BF16 = jnp.bfloat16
MESH = pl.DeviceIdType.MESH

D_MODEL = 1024
SEQ = 2048
D_FF = 4096
N_CHIPS = 4
A_WIDTH = 512
A_HEADS = 4
A_DK = 128
A_CHUNK = 64
A_SUB = 16
B_WIDTH = 512
B_GROUPS = 4
B_CHUNK = 128
C_HEADS = 16
C_HEAD_DIM = 64
C_ROT_HALF = 8
C_BLOCK = 128
C_DILATIONS = (1, 4, 16)
ROPE_THETA = 500000.0
EPS = 1e-6
ADAM_LR = 0.001
ADAM_B1 = 0.9
ADAM_B2 = 0.999
ADAM_EPS = 1e-08
ADAM_WD = 0.01
ADAM_STEP = 10

ROW_TILE = 512
FFN_ROWS = 1024
WGRAD_ROWS = 2048
VMEM_LIMIT = 56 * 1024 * 1024
NEG_BIG = -1e30


def _params(sem=None):
    return pltpu.CompilerParams(dimension_semantics=sem, vmem_limit_bytes=VMEM_LIMIT)


def _dot(a, b):
    return jnp.dot(a, b, preferred_element_type=F32)


def _dot_nt(a, b):
    return lax.dot_general(a, b, (((1,), (1,)), ((), ())), preferred_element_type=F32)


def _dot_tn(a, b):
    return lax.dot_general(a, b, (((0,), (0,)), ((), ())), preferred_element_type=F32)


def _rms(x, g):
    r = lax.rsqrt(jnp.mean(x * x, axis=-1, keepdims=True) + EPS)
    return x * r * g


def _rms_bwd(x, g, dy):
    r = lax.rsqrt(jnp.mean(x * x, axis=-1, keepdims=True) + EPS)
    xh = x * r
    dg = jnp.sum(dy * xh, axis=0, keepdims=True)
    dxh = dy * g
    dx = r * (dxh - xh * jnp.mean(dxh * xh, axis=-1, keepdims=True))
    return dx, dg


def _accumulate(ref, val, first):
    @pl.when(first)
    def _():
        ref[...] = val

    @pl.when(jnp.logical_not(first))
    def _():
        ref[...] += val


N_DEV = 8
ANY = pl.BlockSpec(memory_space=pl.ANY)


def _place():
    x, y, c = lax.axis_index("x"), lax.axis_index("y"), lax.axis_index("c")
    return x, y, c, [(1 - x, y), (x, 1 - y), (1 - x, 1 - y)]


class _Exchange:
    def __init__(self, kind, arrays):
        self.kind, self.arrays, self.n = kind, list(arrays), len(arrays)
        if kind == "gather":
            self.out_shape = [jax.ShapeDtypeStruct((N_CHIPS,) + a.shape, a.dtype) for a in self.arrays]
        else:
            self.out_shape = [jax.ShapeDtypeStruct(a.shape, a.dtype) for a in self.arrays]
        self.scratch = [pltpu.SemaphoreType.DMA((3 * self.n,)), pltpu.SemaphoreType.DMA((3 * self.n,)),
                        pltpu.SemaphoreType.DMA((self.n,))]

    def _copies(self, ins, outs, sems):
        send_sems, recv_sems, local_sems = sems
        x, y, c, chips = _place()
        me = 2 * x + y
        local, remote = [], []
        for a in range(self.n):
            if self.kind == "gather":
                local.append(pltpu.make_async_copy(ins[a], outs[a].at[me], local_sems.at[a]))
            else:
                local.append(pltpu.make_async_copy(ins[a].at[me], outs[a].at[3], local_sems.at[a]))
            for j, (px, py) in enumerate(chips):
                if self.kind == "gather":
                    src, dst, landed = ins[a], outs[a].at[me], outs[a].at[2 * px + py]
                else:
                    src, dst, landed = ins[a].at[2 * px + py], outs[a].at[j], outs[a].at[j]
                send = pltpu.make_async_remote_copy(
                    src_ref=src, dst_ref=dst, send_sem=send_sems.at[3 * a + j], recv_sem=recv_sems.at[3 * a + j],
                    device_id=(px, py, c), device_id_type=MESH)
                recv = pltpu.make_async_remote_copy(
                    src_ref=src, dst_ref=landed, send_sem=send_sems.at[3 * a + j], recv_sem=recv_sems.at[3 * a + j],
                    device_id=(px, py, c), device_id_type=MESH)
                remote.append((send, recv))
        return local, remote

    def start(self, ins, outs, sems):
        local, remote = self._copies(ins, outs, sems)
        for cp in local:
            cp.start()
        for send, _ in remote:
            send.start()

    def finish(self, ins, outs, sems):
        local, remote = self._copies(ins, outs, sems)
        for _, recv in remote:
            recv.wait_recv()
        for send, _ in remote:
            send.wait_send()
        for cp in local:
            cp.wait()


def _call(body, *, name, grid, in_specs, out_specs, out_shape, args, scratch_shapes=(), aliases=None, exchange=None):
    if exchange is None:
        return pl.pallas_call(
            body, name=name, grid=grid, in_specs=in_specs, out_specs=out_specs, out_shape=out_shape,
            scratch_shapes=list(scratch_shapes), input_output_aliases=aliases or {},
            compiler_params=_params(("arbitrary",) * len(grid)))(*args)
    n_in, n_out, n_scr, n_ex = len(in_specs), len(out_specs), len(scratch_shapes), exchange.n
    steps = grid

    def wrapped(*refs):
        ins, refs = refs[:n_in], refs[n_in:]
        ex_in, refs = refs[:n_ex], refs[n_ex:]
        outs, refs = refs[:n_out], refs[n_out:]
        ex_out, refs = refs[:n_ex], refs[n_ex:]
        scr, sems = refs[:n_scr], refs[n_scr:]
        first = functools.reduce(jnp.logical_and, [pl.program_id(k) == 0 for k in range(len(steps))])
        last = functools.reduce(jnp.logical_and, [pl.program_id(k) == steps[k] - 1 for k in range(len(steps))])

        @pl.when(first)
        def _():
            exchange.start(ex_in, ex_out, sems)

        body(*ins, *outs, *scr)

        @pl.when(last)
        def _():
            exchange.finish(ex_in, ex_out, sems)

    return pl.pallas_call(
        wrapped, name=name, grid=grid,
        in_specs=list(in_specs) + [ANY] * n_ex, out_specs=list(out_specs) + [ANY] * n_ex,
        out_shape=list(out_shape) + exchange.out_shape,
        scratch_shapes=list(scratch_shapes) + exchange.scratch, input_output_aliases=aliases or {},
        compiler_params=_params(("arbitrary",) * len(grid)))(*args, *exchange.arrays)


def exchange_alone(exchange, name):
    def body(*refs):
        n = exchange.n
        exchange.start(refs[:n], refs[n:2 * n], refs[2 * n:])
        exchange.finish(refs[:n], refs[n:2 * n], refs[2 * n:])

    return pl.pallas_call(
        body, name=name, in_specs=[ANY] * exchange.n, out_specs=[ANY] * exchange.n,
        out_shape=exchange.out_shape, scratch_shapes=exchange.scratch)(*exchange.arrays)


def norm_matmul(x, g, wg, name, exchange=None):
    t, d = x.shape
    nl = wg.shape[2]

    def body(x_ref, g_ref, w_ref, o_ref, h_ref):
        h = _rms(x_ref[...], g_ref[...]).astype(BF16)
        h_ref[...] = h
        for c in range(N_CHIPS):
            o_ref[:, c * nl:(c + 1) * nl] = _dot(h, w_ref[c])

    return _call(
        body, name=name, grid=(t // ROW_TILE,),
        in_specs=[pl.BlockSpec((ROW_TILE, d), lambda i: (i, 0)),
                  pl.BlockSpec((1, d), lambda i: (0, 0)),
                  pl.BlockSpec((N_CHIPS, d, nl), lambda i: (0, 0, 0))],
        out_specs=[pl.BlockSpec((ROW_TILE, N_CHIPS * nl), lambda i: (i, 0)),
                   pl.BlockSpec((ROW_TILE, d), lambda i: (i, 0))],
        out_shape=[jax.ShapeDtypeStruct((t, N_CHIPS * nl), F32), jax.ShapeDtypeStruct((t, d), BF16)],
        args=(x, g, wg), exchange=exchange)


def norm_matmul_bwd(dproj, wg, x, g, dres, name, exchange=None):
    t, d = x.shape
    nl = wg.shape[2]

    def body(dp_ref, w_ref, x_ref, g_ref, dres_ref, dx_ref, dg_ref):
        dh = _dot_nt(dp_ref[:, 0:nl].astype(BF16), w_ref[0])
        for c in range(1, N_CHIPS):
            dh += _dot_nt(dp_ref[:, c * nl:(c + 1) * nl].astype(BF16), w_ref[c])
        dx, dg = _rms_bwd(x_ref[...], g_ref[...], dh)
        dx_ref[...] = dres_ref[...] + dx
        _accumulate(dg_ref, dg, pl.program_id(0) == 0)

    row = pl.BlockSpec((ROW_TILE, d), lambda i: (i, 0))
    vec = pl.BlockSpec((1, d), lambda i: (0, 0))
    return _call(
        body, name=name, grid=(t // ROW_TILE,),
        in_specs=[pl.BlockSpec((ROW_TILE, N_CHIPS * nl), lambda i: (i, 0)),
                  pl.BlockSpec((N_CHIPS, d, nl), lambda i: (0, 0, 0)), row, vec, row],
        out_specs=[row, vec],
        out_shape=[jax.ShapeDtypeStruct((t, d), F32), jax.ShapeDtypeStruct((1, d), F32)],
        args=(dproj, wg, x, g, dres), exchange=exchange)


def out_proj(a, wg, x, g, name):
    t, d = x.shape
    kl = wg.shape[1]

    def body(a_ref, w_ref, x_ref, g_ref, mix_ref, xo_ref):
        acc = _dot(a_ref[:, 0:kl], w_ref[0])
        for c in range(1, N_CHIPS):
            acc += _dot(a_ref[:, c * kl:(c + 1) * kl], w_ref[c])
        mix_ref[...] = acc
        xo_ref[...] = x_ref[...] + _rms(acc, g_ref[...])

    row = pl.BlockSpec((ROW_TILE, d), lambda i: (i, 0))
    return pl.pallas_call(
        body, name=name, grid=(t // ROW_TILE,),
        in_specs=[row, pl.BlockSpec((N_CHIPS, kl, d), lambda i: (0, 0, 0)), row,
                  pl.BlockSpec((1, d), lambda i: (0, 0))],
        out_specs=[row, row],
        out_shape=[jax.ShapeDtypeStruct((t, d), F32), jax.ShapeDtypeStruct((t, d), F32)],
        compiler_params=_params(("arbitrary",)),
    )(a, wg, x, g)


def out_proj_bwd(dxo, mix, g, wg, name):
    t, d = mix.shape
    kl = wg.shape[1]

    def body(dxo_ref, mix_ref, g_ref, w_ref, dmix_ref, da_ref, dg_ref):
        dmix, dg = _rms_bwd(mix_ref[...], g_ref[...], dxo_ref[...])
        dmb = dmix.astype(BF16)
        dmix_ref[...] = dmb
        for c in range(N_CHIPS):
            da_ref[:, c * kl:(c + 1) * kl] = _dot_nt(dmb, w_ref[c])
        _accumulate(dg_ref, dg, pl.program_id(0) == 0)

    row = pl.BlockSpec((ROW_TILE, d), lambda i: (i, 0))
    vec = pl.BlockSpec((1, d), lambda i: (0, 0))
    return pl.pallas_call(
        body, name=name, grid=(t // ROW_TILE,),
        in_specs=[row, row, vec, pl.BlockSpec((N_CHIPS, kl, d), lambda i: (0, 0, 0))],
        out_specs=[row, row, vec],
        out_shape=[jax.ShapeDtypeStruct((t, d), BF16), jax.ShapeDtypeStruct((t, d), F32),
                   jax.ShapeDtypeStruct((1, d), F32)],
        compiler_params=_params(("arbitrary",)),
    )(dxo, mix, g, wg)


def ffn_fwd(x, gpre, w1g, w2g, gpost, name, exchange=None):
    t, d = x.shape
    hc = w1g.shape[2]

    def body(x_ref, gpre_ref, w1_ref, w2_ref, gpost_ref, xo_ref, h_ref, a_ref, y_ref, acc):
        c = pl.program_id(1)

        @pl.when(c == 0)
        def _():
            h_ref[...] = _rms(x_ref[...], gpre_ref[...]).astype(BF16)

        a = _dot(h_ref[...], w1_ref[...])
        a_ref[...] = a.astype(BF16)
        r = jnp.square(jnp.maximum(a, 0.0)).astype(BF16)
        _accumulate(acc, _dot(r, w2_ref[...]), c == 0)

        @pl.when(c == N_CHIPS - 1)
        def _():
            y = acc[...]
            y_ref[...] = y
            xo_ref[...] = x_ref[...] + _rms(y, gpost_ref[...])

    row = pl.BlockSpec((FFN_ROWS, d), lambda i, c: (i, 0))
    vec = pl.BlockSpec((1, d), lambda i, c: (0, 0))
    return _call(
        body, name=name, grid=(t // FFN_ROWS, N_CHIPS),
        in_specs=[row, vec,
                  pl.BlockSpec((None, d, hc), lambda i, c: (c, 0, 0)),
                  pl.BlockSpec((None, hc, d), lambda i, c: (c, 0, 0)), vec],
        out_specs=[row, row, pl.BlockSpec((FFN_ROWS, hc), lambda i, c: (i, c)), row],
        out_shape=[jax.ShapeDtypeStruct((t, d), F32), jax.ShapeDtypeStruct((t, d), BF16),
                   jax.ShapeDtypeStruct((t, N_CHIPS * hc), BF16), jax.ShapeDtypeStruct((t, d), F32)],
        scratch_shapes=[pltpu.VMEM((FFN_ROWS, d), F32)],
        args=(x, gpre, w1g, w2g, gpost), exchange=exchange)


def ffn_bwd(dxo, x, y, a, gpre, gpost, w1g, w2g, name, exchange=None):
    t, d = x.shape
    hc = w1g.shape[2]

    def body(dxo_ref, x_ref, y_ref, a_ref, gpre_ref, gpost_ref, w1_ref, w2_ref,
             dxi_ref, dy_ref, da_ref, dgpre_ref, dgpost_ref, acc):
        i, c = pl.program_id(0), pl.program_id(1)

        @pl.when(c == 0)
        def _():
            dy, dg = _rms_bwd(y_ref[...], gpost_ref[...], dxo_ref[...])
            dy_ref[...] = dy.astype(BF16)
            _accumulate(dgpost_ref, dg, i == 0)

        dr = _dot_nt(dy_ref[...], w2_ref[...])
        da = (dr * (2.0 * jnp.maximum(a_ref[...].astype(F32), 0.0))).astype(BF16)
        da_ref[...] = da
        _accumulate(acc, _dot_nt(da, w1_ref[...]), c == 0)

        @pl.when(c == N_CHIPS - 1)
        def _():
            dx, dg = _rms_bwd(x_ref[...], gpre_ref[...], acc[...])
            dxi_ref[...] = dxo_ref[...] + dx
            _accumulate(dgpre_ref, dg, i == 0)

    row = pl.BlockSpec((ROW_TILE, d), lambda i, c: (i, 0))
    vec = pl.BlockSpec((1, d), lambda i, c: (0, 0))
    hid = pl.BlockSpec((ROW_TILE, hc), lambda i, c: (i, c))
    return _call(
        body, name=name, grid=(t // ROW_TILE, N_CHIPS),
        in_specs=[row, row, row, hid, vec, vec,
                  pl.BlockSpec((None, d, hc), lambda i, c: (c, 0, 0)),
                  pl.BlockSpec((None, hc, d), lambda i, c: (c, 0, 0))],
        out_specs=[row, row, hid, vec, vec],
        out_shape=[jax.ShapeDtypeStruct((t, d), F32), jax.ShapeDtypeStruct((t, d), BF16),
                   jax.ShapeDtypeStruct((t, N_CHIPS * hc), BF16),
                   jax.ShapeDtypeStruct((1, d), F32), jax.ShapeDtypeStruct((1, d), F32)],
        scratch_shapes=[pltpu.VMEM((ROW_TILE, d), F32)],
        args=(dxo, x, y, a, gpre, gpost, w1g, w2g), exchange=exchange)


def weight_grad(a, b, chunked, bk, bn, relu2, name, exchange=None):
    t = a.shape[0]
    a_on = chunked == "a"
    rows = min(t, WGRAD_ROWS)
    n_steps = t // rows

    def body(a_ref, b_ref, o_ref, acc):
        s = pl.program_id(1)
        av = a_ref[...]
        if relu2:
            av = jnp.square(jnp.maximum(av.astype(F32), 0.0))
        _accumulate(acc, _dot_tn(av.astype(BF16), b_ref[...].astype(BF16)), s == 0)

        @pl.when(s == n_steps - 1)
        def _():
            o_ref[...] = acc[...].astype(BF16)

    res = _call(
        body, name=name, grid=(N_CHIPS, n_steps),
        in_specs=[pl.BlockSpec((rows, bk), (lambda c, s: (s, c)) if a_on else (lambda c, s: (s, 0))),
                  pl.BlockSpec((rows, bn), (lambda c, s: (s, 0)) if a_on else (lambda c, s: (s, c)))],
        out_specs=[pl.BlockSpec((None, bk, bn), lambda c, s: (c, 0, 0))],
        out_shape=[jax.ShapeDtypeStruct((N_CHIPS, bk, bn), BF16)],
        scratch_shapes=[pltpu.VMEM((bk, bn), F32)],
        args=(a, b), exchange=exchange)
    return res[0] if exchange is None else res


def loss_grad(xf, target, name):
    t, d = xf.shape

    def body(x_ref, t_ref, dy_ref, l_ref):
        e = x_ref[...] - t_ref[...]
        dy_ref[...] = e * (1.0 / d)
        part = jnp.sum(jnp.sum(e * e, axis=-1, keepdims=True), axis=0, keepdims=True) * (0.5 / d)
        _accumulate(l_ref, part, pl.program_id(0) == 0)

    row = pl.BlockSpec((ROW_TILE, d), lambda i: (i, 0))
    return pl.pallas_call(
        body, name=name, grid=(t // ROW_TILE,),
        in_specs=[row, row],
        out_specs=[row, pl.BlockSpec((1, 1), lambda i: (0, 0))],
        out_shape=[jax.ShapeDtypeStruct((t, d), F32), jax.ShapeDtypeStruct((1, 1), F32)],
        compiler_params=_params(("arbitrary",)),
    )(xf, target)


def _hgrn2_chunk(st, qs, fls, ivs, gls, l0, l1, l2, ng):
    nsub = len(qs)
    mx = jnp.maximum(jnp.maximum(l0, l1), l2)
    e0, e1, e2 = jnp.exp(l0 - mx), jnp.exp(l1 - mx), jnp.exp(l2 - mx)
    lb = e0 / (e0 + e1 + e2)
    rows = lax.broadcasted_iota(jnp.int32, (A_SUB, A_SUB), 0)
    cols = lax.broadcasted_iota(jnp.int32, (A_SUB, A_SUB), 1)
    tri = (rows >= cols).astype(F32)
    keep = (lax.broadcasted_iota(jnp.int32, (A_SUB, A_SUB, A_DK), 0)
            >= lax.broadcasted_iota(jnp.int32, (A_SUB, A_SUB, A_DK), 1))
    base = jnp.zeros_like(l0)
    bases, gs, ks, qfs = [], [], [], []
    for i in range(nsub):
        f = lb + (1.0 - lb) * jax.nn.sigmoid(fls[i])
        logf = jnp.log(f)
        bases.append(base)
        gs.append(base + jnp.dot(tri, logf, precision=lax.Precision.HIGHEST, preferred_element_type=F32))
        base = base + jnp.sum(logf, axis=0, keepdims=True)
        ks.append(1.0 - f)
        qfs.append(jax.nn.silu(qs[i]))
    g_last = base
    stb = st.astype(BF16)
    outs = []
    for i in range(nsub):
        o = _dot_nt((qfs[i] * jnp.exp(gs[i])).astype(BF16), stb)
        if i > 0:
            qt = (qfs[i] * jnp.exp(gs[i] - bases[i])).astype(BF16)
            kk = jnp.concatenate([ks[j] * jnp.exp(bases[i] - gs[j]) for j in range(i)], axis=0).astype(BF16)
            vv = jnp.concatenate(ivs[:i], axis=0).astype(BF16)
            o = o + _dot(_dot_nt(qt, kk).astype(BF16), vv)
        dec = jnp.exp(jnp.where(keep, gs[i][:, None, :] - gs[i][None, :, :], NEG_BIG))
        s_diag = jnp.sum(qfs[i][:, None, :] * ks[i][None, :, :] * dec, axis=-1)
        o = o + _dot(s_diag.astype(BF16), ivs[i].astype(BF16))
        o = o * lax.rsqrt(jnp.mean(o * o, axis=-1, keepdims=True) + EPS) * ng
        outs.append(o * jax.nn.silu(gls[i]))
    kdec = jnp.concatenate([ks[j] * jnp.exp(g_last - gs[j]) for j in range(nsub)], axis=0).astype(BF16)
    vall = jnp.concatenate(ivs, axis=0).astype(BF16)
    new_st = st * jnp.exp(g_last) + _dot_tn(vall, kdec)
    return new_st, outs


A_MAX_LOG_DECAY = 80.0


def _split3(x):
    hi = x.astype(BF16)
    r1 = x - hi.astype(F32)
    mid = r1.astype(BF16)
    return hi, mid, (r1 - mid.astype(F32)).astype(BF16)


def _tri_matmul(x, transpose):
    n = x.shape[0]
    r = lax.broadcasted_iota(jnp.int32, (n, n), 0)
    c = lax.broadcasted_iota(jnp.int32, (n, n), 1)
    tri = ((r <= c) if transpose else (r >= c)).astype(BF16)
    hi, mid, lo = _split3(x)
    return (_dot(tri, lo) + _dot(tri, mid)) + _dot(tri, hi)


@jax.custom_vjp
def _cumsum_rows(x):
    return _tri_matmul(x, False)


def _cumsum_rows_fwd(x):
    return _tri_matmul(x, False), None


def _cumsum_rows_bwd(_, dy):
    return (_tri_matmul(dy, True),)


_cumsum_rows.defvjp(_cumsum_rows_fwd, _cumsum_rows_bwd)


def _lower_bound(l0, l1, l2):
    mx = jnp.maximum(jnp.maximum(l0, l1), l2)
    e0, e1, e2 = jnp.exp(l0 - mx), jnp.exp(l1 - mx), jnp.exp(l2 - mx)
    return e0 / (e0 + e1 + e2)


def _b(x):
    return x.astype(BF16)


@jax.custom_vjp
def _mm(a, b):
    return _dot(_b(a), _b(b))


_mm.defvjp(lambda a, b: (_mm(a, b), (a, b)),
           lambda res, d: (_dot_nt(_b(d), _b(res[1])), _dot_tn(_b(res[0]), _b(d))))


@jax.custom_vjp
def _mm_nt(a, b):
    return _dot_nt(_b(a), _b(b))


_mm_nt.defvjp(lambda a, b: (_mm_nt(a, b), (a, b)),
              lambda res, d: (_dot(_b(d), _b(res[1])), _dot_tn(_b(d), _b(res[0]))))


def _dot_split(dot, a, b):
    ah, bh = _b(a), _b(b)
    al, bl = _b(a - ah.astype(F32)), _b(b - bh.astype(F32))
    return (dot(ah, bl) + dot(al, bh)) + dot(ah, bh)


@jax.custom_vjp
def _mm_scores(a, b):
    return _dot_nt(_b(a), _b(b))


_mm_scores.defvjp(lambda a, b: (_mm_scores(a, b), (a, b)),
                  lambda res, d: (_dot_split(_dot, d, res[1]), _dot_split(_dot_tn, d, res[0])))


@jax.custom_vjp
def _mm_tn(a, b):
    return _dot_tn(_b(a), _b(b))


_mm_tn.defvjp(lambda a, b: (_mm_tn(a, b), (a, b)),
              lambda res, d: (_dot_nt(_b(res[1]), _b(d)), _dot(_b(res[0]), _b(d))))


@jax.custom_vjp
def _split_heads(x):
    return tuple(x[:, h * A_DK:(h + 1) * A_DK] for h in range(A_HEADS))


def _split_heads_fwd(x):
    return _split_heads(x), None


def _split_heads_bwd(_, parts):
    return (jnp.concatenate(parts, axis=1),)


_split_heads.defvjp(_split_heads_fwd, _split_heads_bwd)


def _hgrn2_chunk_fast(sts, q, fl, iv, gl, l0, l1, l2, ng):
    lb = _lower_bound(l0, l1, l2)
    f = lb + (1.0 - lb) * jax.nn.sigmoid(fl)
    logf = jnp.log(f)
    g = _cumsum_rows(logf)
    g_last = jnp.sum(logf, axis=0, keepdims=True)
    k = 1.0 - f
    qgs = _split_heads(jax.nn.silu(q) * jnp.exp(g))
    kgs = _split_heads(k * jnp.exp(-g))
    kds = _split_heads(k * jnp.exp(g_last - g))
    ivs = _split_heads(iv)
    decays = _split_heads(jnp.exp(g_last))
    n = q.shape[0]
    causal = lax.broadcasted_iota(jnp.int32, (n, n), 0) >= lax.broadcasted_iota(jnp.int32, (n, n), 1)
    raw = [_mm_scores(qg, kg) for qg, kg in zip(qgs, kgs)]
    inter = [_mm_nt(qg, st) for qg, st in zip(qgs, sts)]
    scores = [jnp.where(causal, s, 0.0) for s in raw]
    os = [a + _mm(s, v) for a, s, v in zip(inter, scores, ivs)]
    new_sts = [st * d + _mm_tn(v, kd) for st, d, v, kd in zip(sts, decays, ivs, kds)]
    os = [o * lax.rsqrt(jnp.mean(o * o, axis=-1, keepdims=True) + EPS) for o in os]
    return new_sts, jnp.concatenate(os, axis=1) * ng * jax.nn.silu(gl)


def _chunk_decays_mildly(f_ref, lb_ref):
    lb = _lower_bound(lb_ref[0:1, :], lb_ref[1:2, :], lb_ref[2:3, :])
    logf = jnp.log(lb + (1.0 - lb) * jax.nn.sigmoid(f_ref[...]))
    return jnp.min(jnp.sum(logf, axis=0, keepdims=True)) >= -A_MAX_LOG_DECAY


def _sub_blocks(ref, head):
    lanes = slice(head * A_DK, (head + 1) * A_DK)
    return [ref[i * A_SUB:(i + 1) * A_SUB, lanes] for i in range(A_CHUNK // A_SUB)]


def hgrn2_fwd(proj, lb_table, a_norm, batch, name, exchange=None):
    t = proj.shape[0]
    n_chunks = t // batch // A_CHUNK
    nblk = A_WIDTH // A_DK

    def body(q_ref, f_ref, i_ref, g_ref, lb_ref, ng_ref, o_ref, st_ref, st):
        @pl.when(pl.program_id(1) == 0)
        def _():
            st[...] = jnp.zeros_like(st)

        st_ref[...] = st[...]
        mild = _chunk_decays_mildly(f_ref, lb_ref)

        @pl.when(mild)
        def _():
            new_sts, o = _hgrn2_chunk_fast(
                [st[h] for h in range(A_HEADS)], q_ref[...], f_ref[...], i_ref[...], g_ref[...],
                lb_ref[0:1, :], lb_ref[1:2, :], lb_ref[2:3, :], ng_ref[...])
            for h in range(A_HEADS):
                st[h] = new_sts[h]
            o_ref[...] = o.astype(BF16)

        @pl.when(jnp.logical_not(mild))
        def _():
            for h in range(A_HEADS):
                lanes = slice(h * A_DK, (h + 1) * A_DK)
                new_st, outs = _hgrn2_chunk(
                    st[h], _sub_blocks(q_ref, h), _sub_blocks(f_ref, h), _sub_blocks(i_ref, h),
                    _sub_blocks(g_ref, h), lb_ref[0:1, lanes], lb_ref[1:2, lanes], lb_ref[2:3, lanes],
                    ng_ref[:, lanes])
                st[h] = new_st
                for i, o in enumerate(outs):
                    o_ref[i * A_SUB:(i + 1) * A_SUB, lanes] = o.astype(BF16)

    def part(k):
        return pl.BlockSpec((A_CHUNK, A_WIDTH), lambda b, n: (b * n_chunks + n, k))

    return _call(
        body, name=name, grid=(batch, n_chunks),
        in_specs=[part(0), part(1), part(2), part(3),
                  pl.BlockSpec((3, A_WIDTH), lambda b, n: (0, 0)), pl.BlockSpec((1, A_WIDTH), lambda b, n: (0, 0))],
        out_specs=[pl.BlockSpec((A_CHUNK, A_WIDTH), lambda b, n: (b * n_chunks + n, 0)),
                   pl.BlockSpec((None, A_HEADS, A_DK, A_DK), lambda b, n: (b * n_chunks + n, 0, 0, 0))],
        out_shape=[jax.ShapeDtypeStruct((t, A_WIDTH), BF16),
                   jax.ShapeDtypeStruct((t // A_CHUNK, A_HEADS, A_DK, A_DK), F32)],
        scratch_shapes=[pltpu.VMEM((A_HEADS, A_DK, A_DK), F32)],
        args=(proj, proj, proj, proj, lb_table, a_norm), exchange=exchange)


def hgrn2_bwd(proj, states, lb_table, a_norm, do, batch, name, exchange=None):
    t = proj.shape[0]
    n_chunks = t // batch // A_CHUNK

    def body(q_ref, f_ref, i_ref, g_ref, st_ref, lb_ref, ng_ref, do_ref, dp_ref, dlb_ref, dng_ref, dst):
        @pl.when(jnp.logical_and(pl.program_id(0) == 0, pl.program_id(1) == 0))
        def _():
            dlb_ref[...] = jnp.zeros_like(dlb_ref)
            dng_ref[...] = jnp.zeros_like(dng_ref)

        @pl.when(pl.program_id(1) == 0)
        def _():
            dst[...] = jnp.zeros_like(dst)

        mild = _chunk_decays_mildly(f_ref, lb_ref)

        @pl.when(mild)
        def _():
            _, vjp = jax.vjp(
                _hgrn2_chunk_fast, [st_ref[h] for h in range(A_HEADS)], q_ref[...], f_ref[...], i_ref[...],
                g_ref[...], lb_ref[0:1, :], lb_ref[1:2, :], lb_ref[2:3, :], ng_ref[...])
            d_sts, dq, df, di, dg, dl0, dl1, dl2, dng = vjp(([dst[h] for h in range(A_HEADS)], do_ref[...].astype(F32)))
            for h in range(A_HEADS):
                dst[h] = d_sts[h]
            for k, part in enumerate((dq, df, di, dg)):
                dp_ref[:, k * A_WIDTH:(k + 1) * A_WIDTH] = part
            for row, val in enumerate((dl0, dl1, dl2)):
                dlb_ref[row:row + 1, :] += val
            dng_ref[...] += dng

        @pl.when(jnp.logical_not(mild))
        def _():
            for h in range(A_HEADS):
                lanes = slice(h * A_DK, (h + 1) * A_DK)
                _, vjp = jax.vjp(
                    _hgrn2_chunk, st_ref[h], _sub_blocks(q_ref, h), _sub_blocks(f_ref, h), _sub_blocks(i_ref, h),
                    _sub_blocks(g_ref, h), lb_ref[0:1, lanes], lb_ref[1:2, lanes], lb_ref[2:3, lanes],
                    ng_ref[:, lanes])
                douts = [x.astype(F32) for x in _sub_blocks(do_ref, h)]
                d_st, dqs, dfs, dis, dgs, dl0, dl1, dl2, dng = vjp((dst[h], douts))
                dst[h] = d_st
                for k, parts in enumerate((dqs, dfs, dis, dgs)):
                    for i in range(A_CHUNK // A_SUB):
                        dp_ref[i * A_SUB:(i + 1) * A_SUB,
                               k * A_WIDTH + h * A_DK:k * A_WIDTH + (h + 1) * A_DK] = parts[i]
                for row, val in enumerate((dl0, dl1, dl2)):
                    dlb_ref[row:row + 1, lanes] += val
                dng_ref[:, lanes] += dng

    def rev(b, n):
        return b * n_chunks + (n_chunks - 1 - n)

    def part(k):
        return pl.BlockSpec((A_CHUNK, A_WIDTH), lambda b, n: (rev(b, n), k))

    const3 = pl.BlockSpec((3, A_WIDTH), lambda b, n: (0, 0))
    const1 = pl.BlockSpec((1, A_WIDTH), lambda b, n: (0, 0))
    return _call(
        body, name=name, grid=(batch, n_chunks),
        in_specs=[part(0), part(1), part(2), part(3),
                  pl.BlockSpec((None, A_HEADS, A_DK, A_DK), lambda b, n: (rev(b, n), 0, 0, 0)),
                  const3, const1, part(0)],
        out_specs=[pl.BlockSpec((A_CHUNK, 4 * A_WIDTH), lambda b, n: (rev(b, n), 0)), const3, const1],
        out_shape=[jax.ShapeDtypeStruct((t, 4 * A_WIDTH + 2 * B_WIDTH), F32),
                   jax.ShapeDtypeStruct((3, A_WIDTH), F32), jax.ShapeDtypeStruct((1, A_WIDTH), F32)],
        scratch_shapes=[pltpu.VMEM((A_HEADS, A_DK, A_DK), F32)],
        args=(proj, proj, proj, proj, states, lb_table, a_norm, do), exchange=exchange)


B_GDIM = B_WIDTH // B_GROUPS
B_ROWS = 512


def _gmlp_chunk(ubs, vbs, lngs, lnbs, ws, bcols):
    vs = [jax.nn.gelu(v) for v in vbs]
    mu = sum(jnp.sum(v, axis=-1, keepdims=True) for v in vs) * (1.0 / B_WIDTH)
    var = sum(jnp.sum(jnp.square(v - mu), axis=-1, keepdims=True) for v in vs) * (1.0 / B_WIDTH)
    rstd = lax.rsqrt(var + EPS)
    tril = (lax.broadcasted_iota(jnp.int32, (B_CHUNK, B_CHUNK), 0)
            >= lax.broadcasted_iota(jnp.int32, (B_CHUNK, B_CHUNK), 1))
    outs = []
    for g in range(B_GROUPS):
        vn = (vs[g] - mu) * rstd * lngs[g] + lnbs[g]
        w = jnp.where(tril, ws[g], 0.0).astype(BF16)
        outs.append(jax.nn.gelu(ubs[g]) * (_dot(w, vn.astype(BF16)) + bcols[g]))
    return outs


def _gmlp_args(u_ref, v_ref, lng_ref, lnb_ref, w_ref, bt_ref, rows):
    def groups(ref):
        return [ref[rows, g * B_GDIM:(g + 1) * B_GDIM] for g in range(B_GROUPS)]

    def vec(ref):
        return [ref[:, g * B_GDIM:(g + 1) * B_GDIM] for g in range(B_GROUPS)]

    return (groups(u_ref), groups(v_ref), vec(lng_ref), vec(lnb_ref),
            [w_ref[g] for g in range(B_GROUPS)], [bt_ref[:, g:g + 1] for g in range(B_GROUPS)])


def gmlp_fwd(proj, oa, ln_g, ln_b, w, bias_t, name, exchange=None):
    t = proj.shape[0]

    def body(u_ref, v_ref, oa_ref, lng_ref, lnb_ref, w_ref, bt_ref, o_ref):
        o_ref[:, 0:A_WIDTH] = oa_ref[...]
        for n in range(B_ROWS // B_CHUNK):
            rows = slice(n * B_CHUNK, (n + 1) * B_CHUNK)
            outs = _gmlp_chunk(*_gmlp_args(u_ref, v_ref, lng_ref, lnb_ref, w_ref, bt_ref, rows))
            for g, o in enumerate(outs):
                o_ref[rows, A_WIDTH + g * B_GDIM:A_WIDTH + (g + 1) * B_GDIM] = o.astype(BF16)

    vec = pl.BlockSpec((1, B_WIDTH), lambda i: (0, 0))
    return _call(
        body, name=name, grid=(t // B_ROWS,),
        in_specs=[pl.BlockSpec((B_ROWS, B_WIDTH), lambda i: (i, 4)), pl.BlockSpec((B_ROWS, B_WIDTH), lambda i: (i, 5)),
                  pl.BlockSpec((B_ROWS, A_WIDTH), lambda i: (i, 0)), vec, vec,
                  pl.BlockSpec((B_GROUPS, B_CHUNK, B_CHUNK), lambda i: (0, 0, 0)),
                  pl.BlockSpec((B_CHUNK, B_GROUPS), lambda i: (0, 0))],
        out_specs=[pl.BlockSpec((B_ROWS, A_WIDTH + B_WIDTH), lambda i: (i, 0))],
        out_shape=[jax.ShapeDtypeStruct((t, A_WIDTH + B_WIDTH), BF16)],
        args=(proj, proj, oa, ln_g, ln_b, w, bias_t), exchange=exchange)


def gmlp_bwd(proj, dmixin, ln_g, ln_b, w, bias_t, dproj, name, exchange=None):
    t = proj.shape[0]

    def body(u_ref, v_ref, do_ref, lng_ref, lnb_ref, w_ref, bt_ref, dp_in_ref,
             dp_ref, dlng_ref, dlnb_ref, dw_ref, dbt_ref):
        del dp_in_ref

        @pl.when(pl.program_id(0) == 0)
        def _():
            for ref in (dlng_ref, dlnb_ref, dw_ref, dbt_ref):
                ref[...] = jnp.zeros_like(ref)

        for n in range(B_ROWS // B_CHUNK):
            rows = slice(n * B_CHUNK, (n + 1) * B_CHUNK)
            _, vjp = jax.vjp(_gmlp_chunk, *_gmlp_args(u_ref, v_ref, lng_ref, lnb_ref, w_ref, bt_ref, rows))
            douts = [do_ref[rows, g * B_GDIM:(g + 1) * B_GDIM] for g in range(B_GROUPS)]
            dus, dvs, dlngs, dlnbs, dws, dbs = vjp(douts)
            for g in range(B_GROUPS):
                lanes = slice(g * B_GDIM, (g + 1) * B_GDIM)
                dp_ref[rows, lanes] = dus[g]
                dp_ref[rows, B_WIDTH + g * B_GDIM:B_WIDTH + (g + 1) * B_GDIM] = dvs[g]
                dlng_ref[:, lanes] += dlngs[g]
                dlnb_ref[:, lanes] += dlnbs[g]
                dw_ref[g] += dws[g]
                dbt_ref[:, g:g + 1] += dbs[g]

    vec = pl.BlockSpec((1, B_WIDTH), lambda i: (0, 0))
    wspec = pl.BlockSpec((B_GROUPS, B_CHUNK, B_CHUNK), lambda i: (0, 0, 0))
    bspec = pl.BlockSpec((B_CHUNK, B_GROUPS), lambda i: (0, 0))
    return _call(
        body, name=name, grid=(t // B_ROWS,),
        in_specs=[pl.BlockSpec((B_ROWS, B_WIDTH), lambda i: (i, 4)), pl.BlockSpec((B_ROWS, B_WIDTH), lambda i: (i, 5)),
                  pl.BlockSpec((B_ROWS, B_WIDTH), lambda i: (i, 1)), vec, vec, wspec, bspec,
                  pl.BlockSpec(memory_space=pl.ANY)],
        out_specs=[pl.BlockSpec((B_ROWS, 2 * B_WIDTH), lambda i: (i, 2)), vec, vec, wspec, bspec],
        out_shape=[jax.ShapeDtypeStruct(dproj.shape, F32), jax.ShapeDtypeStruct((1, B_WIDTH), F32),
                   jax.ShapeDtypeStruct((1, B_WIDTH), F32), jax.ShapeDtypeStruct((B_GROUPS, B_CHUNK, B_CHUNK), F32),
                   jax.ShapeDtypeStruct((B_CHUNK, B_GROUPS), F32)],
        aliases={7: 0}, args=(proj, proj, dmixin, ln_g, ln_b, w, bias_t, dproj), exchange=exchange)


C_PAIR = 2 * C_HEAD_DIM
C_PAIRS = C_HEADS // 2
C_SCALE = 1.0 / math.sqrt(C_HEAD_DIM)
C_ROT_DIM = 2 * C_ROT_HALF
ROPE_ROWS = 1024


def rope_tables(pos_col, name):
    t = pos_col.shape[0]

    def body(p_ref, c_ref, a_ref, b_ref):
        lane = jnp.bitwise_and(lax.broadcasted_iota(jnp.int32, (1, C_PAIR), 1), C_HEAD_DIM - 1)
        j = jnp.bitwise_and(lane, C_ROT_HALF - 1).astype(F32)
        inv = jnp.exp(j * (-math.log(ROPE_THETA) / C_ROT_HALF))
        ang = p_ref[...].astype(F32) * inv
        cos, sin = jnp.cos(ang), jnp.sin(ang)
        c_ref[...] = jnp.where(lane < C_ROT_DIM, cos, 1.0)
        a_ref[...] = jnp.where(lane < C_ROT_HALF, -sin, 0.0)
        b_ref[...] = jnp.where(jnp.logical_and(lane >= C_ROT_HALF, lane < C_ROT_DIM), sin, 0.0)

    tab = pl.BlockSpec((ROPE_ROWS, C_PAIR), lambda i: (i, 0))
    return pl.pallas_call(
        body, name=name, grid=(t // ROPE_ROWS,),
        in_specs=[pl.BlockSpec((ROPE_ROWS, 1), lambda i: (i, 0))],
        out_specs=[tab, tab, tab],
        out_shape=[jax.ShapeDtypeStruct((t, C_PAIR), F32)] * 3,
        compiler_params=_params(("arbitrary",)),
    )(pos_col)


def _rope(x, c, a, b):
    return x * c + pltpu.roll(x, C_PAIR - C_ROT_HALF, 1) * a + pltpu.roll(x, C_ROT_HALF, 1) * b


def _rope_t(d, c, a, b):
    return d * c + pltpu.roll(d * a, C_ROT_HALF, 1) + pltpu.roll(d * b, C_PAIR - C_ROT_HALF, 1)


def _attn_rows(idx, dil):
    nblk = SEQ // dil // C_BLOCK
    r, n = idx // nblk, idx % nblk
    start = r + dil * C_BLOCK * n
    prev = r + dil * C_BLOCK * jnp.maximum(n - 1, 0)
    if dil == 1:
        return pl.ds(pl.multiple_of(start, C_BLOCK), C_BLOCK), pl.ds(pl.multiple_of(prev, C_BLOCK), C_BLOCK), n > 0
    return pl.ds(start, C_BLOCK, stride=dil), pl.ds(prev, C_BLOCK, stride=dil), n > 0


def _head_masks():
    low = lax.broadcasted_iota(jnp.int32, (1, C_PAIR), 1) < C_HEAD_DIM
    return low, jnp.logical_not(low)


def _attn_mask(has_prev):
    i = jnp.bitwise_and(lax.broadcasted_iota(jnp.int32, (2 * C_BLOCK, 2 * C_BLOCK), 0), C_BLOCK - 1)
    j = lax.broadcasted_iota(jnp.int32, (2 * C_BLOCK, 2 * C_BLOCK), 1)
    return jnp.logical_or(j <= i, jnp.logical_and(j - C_BLOCK >= i, has_prev))


def _stack_heads(x):
    low, high = _head_masks()
    return jnp.concatenate([jnp.where(low, x, 0.0), jnp.where(high, x, 0.0)], axis=0)


def _unstack_heads(x):
    low, _ = _head_masks()
    return jnp.where(low, x[:C_BLOCK], x[C_BLOCK:])


def attn_fwd(qkv, cos_t, sin_a, sin_b, batch, name, exchange=None):
    t = qkv.shape[0]
    nbr = len(C_DILATIONS)

    def body(q_ref, k_ref, v_ref, c_ref, a_ref, b_ref, o_ref, l_ref, qs, ks, *stats):
        acc, mm, dd = stats[0:nbr], stats[nbr:2 * nbr], stats[2 * nbr:3 * nbr]
        c, a, b = c_ref[...], a_ref[...], b_ref[...]
        qs[...] = _rope(q_ref[...], c, a, b) * C_SCALE
        ks[...] = _rope(k_ref[...], c, a, b)
        def load(idx, dil):
            rows, prev, has_prev = _attn_rows(idx, dil)
            return rows, (has_prev, qs[rows, :], ks[rows, :], ks[prev, :], v_ref[rows, :], v_ref[prev, :])

        def scores(has_prev, q, k_own, k_prev, v_own, v_prev):
            k_cat = jnp.concatenate([k_own, k_prev], axis=0).astype(BF16)
            return jnp.where(_attn_mask(has_prev), _dot_nt(_stack_heads(q).astype(BF16), k_cat), NEG_BIG)

        def softmax(s):
            m = jnp.max(s, axis=-1, keepdims=True)
            p = jnp.exp(s - m)
            return p.astype(BF16), m, jnp.sum(p, axis=-1, keepdims=True)

        def values(pb, has_prev, q, k_own, k_prev, v_own, v_prev):
            low, high = _head_masks()
            v_cat = jnp.concatenate([v_own, v_prev], axis=0)
            p_wide = jnp.concatenate([pb[:C_BLOCK], pb[C_BLOCK:]], axis=1)
            v_tall = jnp.concatenate([jnp.where(low, v_cat, 0.0), jnp.where(high, v_cat, 0.0)], axis=0).astype(BF16)
            return _dot(p_wide, v_tall)

        for bi, dil in enumerate(C_DILATIONS):
            def pair(i, carry, bi=bi, dil=dil):
                low, _ = _head_masks()
                loaded = [load(2 * i + k, dil) for k in range(2)]
                ss = [scores(*ops) for _, ops in loaded]
                sm = [softmax(s) for s in ss]
                pvs = [values(pb, *ops) for (pb, _, _), (_, ops) in zip(sm, loaded)]
                for (rows, _), (_, m, den), pv in zip(loaded, sm, pvs):
                    acc[bi][rows, :] = pv
                    mm[bi][rows, :] = jnp.where(low, m[:C_BLOCK], m[C_BLOCK:])
                    dd[bi][rows, :] = jnp.where(low, den[:C_BLOCK], den[C_BLOCK:])
                return carry

            lax.fori_loop(0, SEQ // C_BLOCK // 2, pair, 0)
        step = 256
        for r0 in range(0, SEQ, step):
            rr = slice(r0, r0 + step)
            ms = [mm[g][rr, :] for g in range(nbr)]
            m_all = functools.reduce(jnp.maximum, ms)
            ws = [jnp.exp(m - m_all) for m in ms]
            num = sum(acc[g][rr, :] * ws[g] for g in range(nbr))
            den = sum(dd[g][rr, :] * ws[g] for g in range(nbr))
            o_ref[rr, :] = (num / den).astype(BF16)
            l_ref[rr, :] = m_all + jnp.log(den)

    def col(k):
        return pl.BlockSpec((SEQ, C_PAIR), lambda b, p: (b, k * C_PAIRS + p))

    tab = pl.BlockSpec((SEQ, C_PAIR), lambda b, p: (b, 0))
    return _call(
        body, name=name, grid=(batch, C_PAIRS),
        in_specs=[col(0), col(1), col(2), tab, tab, tab],
        out_specs=[col(0), col(0)],
        out_shape=[jax.ShapeDtypeStruct((t, D_MODEL), BF16), jax.ShapeDtypeStruct((t, D_MODEL), F32)],
        scratch_shapes=[pltpu.VMEM((SEQ, C_PAIR), F32)] * (2 + 3 * nbr),
        args=(qkv, qkv, qkv, cos_t, sin_a, sin_b), exchange=exchange)


def attn_bwd(qkv, cos_t, sin_a, sin_b, o, lse, do, batch, name, exchange=None):
    t = qkv.shape[0]

    def body(q_ref, k_ref, v_ref, c_ref, a_ref, b_ref, o_ref, l_ref, do_ref, dq_ref, dk_ref, dv_ref,
             qs, ks, dqs, dks, dvs, dlt):
        c, a, b = c_ref[...], a_ref[...], b_ref[...]
        qs[...] = _rope(q_ref[...], c, a, b) * C_SCALE
        ks[...] = _rope(k_ref[...], c, a, b)
        prod = do_ref[...] * o_ref[...].astype(F32)
        low = lax.broadcasted_iota(jnp.int32, (1, C_PAIR), 1) < C_HEAD_DIM
        s_low = jnp.sum(jnp.where(low, prod, 0.0), axis=-1, keepdims=True)
        s_all = jnp.sum(prod, axis=-1, keepdims=True)
        dlt[...] = jnp.where(low, s_low, s_all - s_low)
        dqs[...] = jnp.zeros_like(dqs)
        dks[...] = jnp.zeros_like(dks)
        dvs[...] = jnp.zeros_like(dvs)
        def load(idx, dil):
            rows, prev, has_prev = _attn_rows(idx, dil)
            return (rows, prev), (has_prev, qs[rows, :], do_ref[rows, :], ks[rows, :], ks[prev, :],
                                  v_ref[rows, :], v_ref[prev, :], l_ref[rows, :], dlt[rows, :])

        def operands(has_prev, q, do, k_own, k_prev, v_own, v_prev, l_full, d_full):
            lcol = jnp.concatenate([l_full[:, 0:1], l_full[:, C_HEAD_DIM:C_HEAD_DIM + 1]], axis=0)
            dcol = jnp.concatenate([d_full[:, 0:1], d_full[:, C_HEAD_DIM:C_HEAD_DIM + 1]], axis=0)
            return (_stack_heads(q).astype(BF16), _stack_heads(do).astype(BF16),
                    jnp.concatenate([k_own, k_prev], axis=0).astype(BF16),
                    jnp.concatenate([v_own, v_prev], axis=0).astype(BF16), lcol, dcol, _attn_mask(has_prev))

        for dil in C_DILATIONS:
            def pair(i, carry, dil=dil):
                loaded = [load(2 * i + k, dil) for k in range(2)]
                ops = [operands(*o) for _, o in loaded]
                ss = [_dot_nt(q_stack, k_cat) for q_stack, _, k_cat, _, _, _, _ in ops]
                dps = [_dot_nt(do_stack, v_cat) for _, do_stack, _, v_cat, _, _, _ in ops]
                ps = [jnp.exp(jnp.where(o[6], s, NEG_BIG) - o[4]) for s, o in zip(ss, ops)]
                dss = [(p * (dp - o[5])).astype(BF16) for p, dp, o in zip(ps, dps, ops)]
                dvs_ = [_dot_tn(p.astype(BF16), o[1]) for p, o in zip(ps, ops)]
                dks_ = [_dot_tn(ds, o[0]) for ds, o in zip(dss, ops)]
                dqs_ = [_unstack_heads(_dot(ds, o[2])) for ds, o in zip(dss, ops)]
                results = list(zip(dqs_, dks_, dvs_))
                for ((rows, prev), _), (dq, dk_cat, dv_cat) in zip(loaded, results):
                    dqs[rows, :] += dq
                    dks[rows, :] += dk_cat[:C_BLOCK]
                    dvs[rows, :] += dv_cat[:C_BLOCK]
                    dks[prev, :] += dk_cat[C_BLOCK:]
                    dvs[prev, :] += dv_cat[C_BLOCK:]
                return carry

            lax.fori_loop(0, SEQ // C_BLOCK // 2, pair, 0)
        dq_ref[...] = _rope_t(dqs[...] * C_SCALE, c, a, b)
        dk_ref[...] = _rope_t(dks[...], c, a, b)
        dv_ref[...] = dvs[...]

    def col(k):
        return pl.BlockSpec((SEQ, C_PAIR), lambda b, p: (b, k * C_PAIRS + p))

    tab = pl.BlockSpec((SEQ, C_PAIR), lambda b, p: (b, 0))
    out = jax.ShapeDtypeStruct((t, D_MODEL), F32)
    return _call(
        body, name=name, grid=(batch, C_PAIRS),
        in_specs=[col(0), col(1), col(2), tab, tab, tab, col(0), col(0), col(0)],
        out_specs=[col(0), col(0), col(0)],
        out_shape=[out, out, out],
        scratch_shapes=[pltpu.VMEM((SEQ, C_PAIR), F32)] * 6,
        args=(qkv, qkv, qkv, cos_t, sin_a, sin_b, o, lse, do), exchange=exchange)


def sibling_swap(arrays, name):
    n = len(arrays)

    def body(*refs):
        ins, outs = refs[:n], refs[n:2 * n]
        send_sems, recv_sems = refs[2 * n:]
        x, y, c, _ = _place()
        sends = []
        for a in range(n):
            cp = pltpu.make_async_remote_copy(
                src_ref=ins[a], dst_ref=outs[a], send_sem=send_sems.at[a], recv_sem=recv_sems.at[a],
                device_id=(x, y, 1 - c), device_id_type=MESH)
            cp.start()
            sends.append(cp)
        for cp in sends:
            cp.wait_recv()
        for cp in sends:
            cp.wait_send()

    return pl.pallas_call(
        body, name=name,
        in_specs=[ANY] * n, out_specs=[ANY] * n,
        out_shape=[jax.ShapeDtypeStruct(s.shape, s.dtype) for s in arrays],
        scratch_shapes=[pltpu.SemaphoreType.DMA((n,)), pltpu.SemaphoreType.DMA((n,))],
    )(*arrays)


def allreduce_small(slab, name):
    rows, lanes = slab.shape

    def body(x_ref, out_ref, gath, send_sems, recv_sems, local_sem):
        x, y, c, chips = _place()
        me, sibling = (x, y, c), (x, y, 1 - c)

        def slot(px, py, pc):
            return gath.at[4 * px + 2 * py + pc]

        def copy(k, block, to, src=None):
            return pltpu.make_async_remote_copy(
                src_ref=slot(*block) if src is None else src, dst_ref=slot(*block),
                send_sem=send_sems.at[k], recv_sem=recv_sems.at[k], device_id=to, device_id_type=MESH)

        mine = pltpu.make_async_copy(x_ref, slot(*me), local_sem)
        mine.start()
        first = [copy(0, me, sibling, src=x_ref)]
        first += [copy(1 + j, me, (*chip, c), src=x_ref) for j, chip in enumerate(chips)]
        for cp in first:
            cp.start()
        passed = [copy(4 + j, (*chip, c), sibling) for j, chip in enumerate(chips)]
        for j, chip in enumerate(chips):
            copy(1 + j, (*chip, c), me).wait_recv()
            passed[j].start()
        copy(0, sibling, me).wait_recv()
        for j, chip in enumerate(chips):
            copy(4 + j, (*chip, 1 - c), me).wait_recv()
        for cp in first + passed:
            cp.wait_send()
        mine.wait()
        total = gath[0]
        for d in range(1, N_DEV):
            total = total + gath[d]
        out_ref[...] = total

    return pl.pallas_call(
        body, name=name,
        in_specs=[pl.BlockSpec(memory_space=pltpu.VMEM)],
        out_specs=pl.BlockSpec(memory_space=pltpu.VMEM),
        out_shape=jax.ShapeDtypeStruct((rows, lanes), F32),
        scratch_shapes=[pltpu.VMEM((N_DEV, rows, lanes), F32),
                        pltpu.SemaphoreType.DMA((7,)), pltpu.SemaphoreType.DMA((7,)), pltpu.SemaphoreType.DMA],
    )(slab)


ELT_ROWS = 512


def reduce_slabs(r, name, part=0, parts=1, into=None):
    _, rows, cols = r.shape
    br = min(rows, ELT_ROWS)
    nblk = rows // br

    def body(r_ref, *rest):
        o_ref = rest[-1]
        o_ref[...] = ((r_ref[3].astype(F32) + r_ref[0].astype(F32)) + r_ref[1].astype(F32)) + r_ref[2].astype(F32)

    return pl.pallas_call(
        body, name=name, grid=(nblk,),
        in_specs=[pl.BlockSpec((N_CHIPS, br, cols), lambda i: (0, i, 0))] + ([] if into is None else [ANY]),
        out_specs=pl.BlockSpec((br, cols), lambda i: (part * nblk + i, 0)),
        out_shape=jax.ShapeDtypeStruct((parts * rows, cols), F32),
        input_output_aliases={} if into is None else {1: 0},
        compiler_params=_params(("arbitrary",)),
    )(*([r] if into is None else [r, into]))


def _adamw(w, g, m, v):
    m = ADAM_B1 * m + (1.0 - ADAM_B1) * g
    v = ADAM_B2 * v + (1.0 - ADAM_B2) * jnp.square(g)
    m_hat = m / (1.0 - ADAM_B1 ** ADAM_STEP)
    v_hat = v / (1.0 - ADAM_B2 ** ADAM_STEP)
    delta = -ADAM_LR * (m_hat / (jnp.sqrt(v_hat) + ADAM_EPS) + ADAM_WD * w)
    return delta, m, v


def adamw_big(w, s_mine, s_sibling, m, v, name):
    rows, cols = w.shape

    def body(w_ref, a_ref, b_ref, m_ref, v_ref, g_out, d_out, m_out, v_out):
        g = a_ref[...] + b_ref[...]
        g_out[...] = g
        d_out[...], m_out[...], v_out[...] = _adamw(w_ref[...], g, m_ref[...], v_ref[...])

    blk = pl.BlockSpec((min(rows, ELT_ROWS), cols), lambda i: (i, 0))
    out = jax.ShapeDtypeStruct((rows, cols), F32)
    return pl.pallas_call(
        body, name=name, grid=(rows // min(rows, ELT_ROWS),),
        in_specs=[blk] * 5, out_specs=[blk] * 4, out_shape=[out] * 4,
        compiler_params=_params(("arbitrary",)),
    )(w, s_mine, s_sibling, m, v)


def adamw_small(ws, gs, ms, vs, name):
    n = len(ws)

    def body(*refs):
        w_refs, g_refs, m_refs, v_refs = (refs[k * n:(k + 1) * n] for k in range(4))
        d_out, m_out, v_out = (refs[(4 + k) * n:(5 + k) * n] for k in range(3))
        for i in range(n):
            d_out[i][...], m_out[i][...], v_out[i][...] = _adamw(
                w_refs[i][...], g_refs[i][...], m_refs[i][...], v_refs[i][...])

    outs = [jax.ShapeDtypeStruct(w.shape, F32) for w in ws]
    res = pl.pallas_call(body, name=name, out_shape=outs * 3)(*ws, *gs, *ms, *vs)
    return res[:n], res[n:2 * n], res[2 * n:]


SLAB_LANES = 128
SLAB_ROW_ALIGN = 8


def _pack(parts):
    flat = jnp.concatenate([p.reshape(-1) for p in parts])
    rows = -(-flat.shape[0] // (SLAB_LANES * SLAB_ROW_ALIGN)) * SLAB_ROW_ALIGN
    flat = jnp.pad(flat, (0, rows * SLAB_LANES - flat.shape[0]))
    return flat.reshape(rows, SLAB_LANES)


def _unpack(slab, shapes):
    flat = slab.reshape(-1)
    out, pos = [], 0
    for s in shapes:
        size = math.prod(s)
        out.append(flat[pos:pos + size].reshape(s))
        pos += size
    return out


def kernel(x, positions, norm_mix_pre, norm_mix_post, norm_ffn_pre, norm_ffn_post, w_in_even, lb_table, a_norm, b_ln_g, b_ln_b, b_ws, b_bias, w_out_even, w_in_odd, w_out_odd, w_ff1, w_ff2, loss_target, m_norm_mix_pre, m_norm_mix_post, m_norm_ffn_pre, m_norm_ffn_post, m_w_in_even, m_lb_table, m_a_norm, m_b_ln_g, m_b_ln_b, m_b_ws, m_b_bias, m_w_out_even, m_w_in_odd, m_w_out_odd, m_w_ff1, m_w_ff2, v_norm_mix_pre, v_norm_mix_post, v_norm_ffn_pre, v_norm_ffn_post, v_w_in_even, v_lb_table, v_a_norm, v_b_ln_g, v_b_ln_b, v_b_ws, v_b_bias, v_w_out_even, v_w_in_odd, v_w_out_odd, v_w_ff1, v_w_ff2):
    batch = x.shape[0]
    t = batch * SEQ
    d = D_MODEL
    x0 = x.reshape(t, d)
    target = loss_target.reshape(t, d)

    def gain(p, layer):
        return p[layer:layer + 1]

    def gather(*shards):
        return _Exchange("gather", [w.astype(BF16) for w in shards])

    def scatter(*grads):
        return _Exchange("scatter", grads)

    (win_e,) = exchange_alone(gather(w_in_even[0]), "gather_in_even")
    bias_t = b_bias[0].T
    proj, h0, w1_0 = norm_matmul(x0, gain(norm_mix_pre, 0), win_e, "in_proj_even", exchange=gather(w_ff1[0]))
    oa, states, w2_0 = hgrn2_fwd(proj, lb_table, a_norm, batch, "hgrn2_fwd", exchange=gather(w_ff2[0]))
    mixin, wout_e = gmlp_fwd(proj, oa, b_ln_g, b_ln_b, b_ws[0], bias_t, "gmlp_fwd", exchange=gather(w_out_even[0]))
    mix0, x1 = out_proj(mixin, wout_e, x0, gain(norm_mix_post, 0), "out_proj_even")
    x2, hf0, a0, y0, win_o, wout_o = ffn_fwd(x1, gain(norm_ffn_pre, 0), w1_0, w2_0, gain(norm_ffn_post, 0),
                                             "ffn_fwd_0", exchange=gather(w_in_odd[0], w_out_odd[0]))
    qkv, h1 = norm_matmul(x2, gain(norm_mix_pre, 1), win_o, "in_proj_odd")
    cos_t, sin_a, sin_b = rope_tables(positions.reshape(t, 1), "rope_tables")
    ao, lse, w1_1, w2_1 = attn_fwd(qkv, cos_t, sin_a, sin_b, batch, "attn_fwd", exchange=gather(w_ff1[1], w_ff2[1]))
    mix1, x3 = out_proj(ao, wout_o, x2, gain(norm_mix_post, 1), "out_proj_odd")
    x4, hf1, a1, y1 = ffn_fwd(x3, gain(norm_ffn_pre, 1), w1_1, w2_1, gain(norm_ffn_post, 1), "ffn_fwd_1")
    dx4, loss_part = loss_grad(x4, target, "loss_grad")

    hc = D_FF // N_CHIPS
    dx3, dy1, da1, dg_fpre1, dg_fpost1 = ffn_bwd(
        dx4, x3, y1, a1, gain(norm_ffn_pre, 1), gain(norm_ffn_post, 1), w1_1, w2_1, "ffn_bwd_1")
    g_w1_1 = weight_grad(hf1, da1, "b", d, hc, False, "wgrad_ff1_1")
    g_w2_1 = weight_grad(a1, dy1, "a", hc, d, True, "wgrad_ff2_1")
    dmix1, dao, dg_mpost1 = out_proj_bwd(dx3, mix1, gain(norm_mix_post, 1), wout_o, "out_proj_bwd_odd")
    g_wout_o = weight_grad(ao, dmix1, "a", d // N_CHIPS, d, False, "wgrad_out_odd")
    dq, dk, dv, r_w1_1, r_w2_1, r_wout_o = attn_bwd(qkv, cos_t, sin_a, sin_b, ao, lse, dao, batch, "attn_bwd",
                                                    exchange=scatter(g_w1_1, g_w2_1, g_wout_o))
    dqkv = jnp.concatenate([dq, dk, dv], axis=1)
    dx2, dg_mpre1 = norm_matmul_bwd(dqkv, win_o, x2, gain(norm_mix_pre, 1), dx3, "in_proj_bwd_odd")
    g_win_o = weight_grad(h1, dqkv, "b", d, 3 * d // N_CHIPS, False, "wgrad_in_odd")
    dx1, dy0, da0, dg_fpre0, dg_fpost0, r_win_o = ffn_bwd(
        dx2, x1, y0, a0, gain(norm_ffn_pre, 0), gain(norm_ffn_post, 0), w1_0, w2_0, "ffn_bwd_0",
        exchange=scatter(g_win_o))
    g_w1_0 = weight_grad(hf0, da0, "b", d, hc, False, "wgrad_ff1_0")
    g_w2_0 = weight_grad(a0, dy0, "a", hc, d, True, "wgrad_ff2_0")
    dmix0, dmixin, dg_mpost0 = out_proj_bwd(dx1, mix0, gain(norm_mix_post, 0), wout_e, "out_proj_bwd_even")
    g_wout_e = weight_grad(mixin, dmix0, "a", d // N_CHIPS, d, False, "wgrad_out_even")
    dproj, d_lb, d_anorm, r_w1_0 = hgrn2_bwd(
        proj, states, lb_table, a_norm, dmixin, batch, "hgrn2_bwd", exchange=scatter(g_w1_0))
    dproj, d_lng, d_lnb, d_ws, d_bias_t, r_wout_e = gmlp_bwd(
        proj, dmixin, b_ln_g, b_ln_b, b_ws[0], bias_t, dproj, "gmlp_bwd", exchange=scatter(g_wout_e))
    g_win_e, r_w2_0 = weight_grad(h0, dproj, "b", d, 3 * d // N_CHIPS, False, "wgrad_in_even",
                                  exchange=scatter(g_w2_0))
    dx0, dg_mpre0, r_win_e = norm_matmul_bwd(dproj, win_e, x0, gain(norm_mix_pre, 0), dx1, "in_proj_bwd_even",
                                             exchange=scatter(g_win_e))
    grad_x = dx0.reshape(x.shape)

    s_w1 = reduce_slabs(r_w1_1, "reduce_ff1_1", part=1, parts=2)
    s_w1 = reduce_slabs(r_w1_0, "reduce_ff1_0", part=0, parts=2, into=s_w1)
    s_w2 = reduce_slabs(r_w2_1, "reduce_ff2_1", part=1, parts=2)
    s_w2 = reduce_slabs(r_w2_0, "reduce_ff2_0", part=0, parts=2, into=s_w2)
    sums = [reduce_slabs(r_win_e, "reduce_in_even"), reduce_slabs(r_wout_e, "reduce_out_even"),
            reduce_slabs(r_win_o, "reduce_in_odd"), reduce_slabs(r_wout_o, "reduce_out_odd"), s_w1, s_w2]
    sibling = sibling_swap(sums, "sibling_swap")
    big_w = [w_in_even, w_out_even, w_in_odd, w_out_odd, w_ff1, w_ff2]
    big_m = [m_w_in_even, m_w_out_even, m_w_in_odd, m_w_out_odd, m_w_ff1, m_w_ff2]
    big_v = [v_w_in_even, v_w_out_even, v_w_in_odd, v_w_out_odd, v_w_ff1, v_w_ff2]
    big = []
    for i, (w, m, v) in enumerate(zip(big_w, big_m, big_v)):
        two_d = (-1, w.shape[-1])
        res = adamw_big(w.reshape(two_d), sums[i], sibling[i], m.reshape(two_d), v.reshape(two_d), "adamw_big_%d" % i)
        big.append([r.reshape(w.shape) for r in res])

    small_w = [norm_mix_pre, norm_mix_post, norm_ffn_pre, norm_ffn_post, lb_table, a_norm, b_ln_g, b_ln_b, b_ws, b_bias]
    small_m = [m_norm_mix_pre, m_norm_mix_post, m_norm_ffn_pre, m_norm_ffn_post, m_lb_table, m_a_norm, m_b_ln_g,
               m_b_ln_b, m_b_ws, m_b_bias]
    small_v = [v_norm_mix_pre, v_norm_mix_post, v_norm_ffn_pre, v_norm_ffn_post, v_lb_table, v_a_norm, v_b_ln_g,
               v_b_ln_b, v_b_ws, v_b_bias]
    partial = [jnp.concatenate([dg_mpre0, dg_mpre1]), jnp.concatenate([dg_mpost0, dg_mpost1]),
               jnp.concatenate([dg_fpre0, dg_fpre1]), jnp.concatenate([dg_fpost0, dg_fpost1]),
               d_lb, d_anorm, d_lng, d_lnb, d_ws[None], d_bias_t.T[None]]
    *small_g, loss = _unpack(allreduce_small(_pack(partial + [loss_part]), "allreduce_small"),
                             [w.shape for w in small_w] + [()])
    small_d, small_nm, small_nv = adamw_small(small_w, small_g, small_m, small_v, "adamw_small")

    order = ["norm_mix_pre", "norm_mix_post", "norm_ffn_pre", "norm_ffn_post", "w_in_even", "lb_table", "a_norm",
             "b_ln_g", "b_ln_b", "b_ws", "b_bias", "w_out_even", "w_in_odd", "w_out_odd", "w_ff1", "w_ff2"]
    small_names = ["norm_mix_pre", "norm_mix_post", "norm_ffn_pre", "norm_ffn_post", "lb_table", "a_norm",
                   "b_ln_g", "b_ln_b", "b_ws", "b_bias"]
    big_names = ["w_in_even", "w_out_even", "w_in_odd", "w_out_odd", "w_ff1", "w_ff2"]
    grads, deltas, new_m, new_v = {}, {}, {}, {}
    for i, nm in enumerate(small_names):
        grads[nm], deltas[nm], new_m[nm], new_v[nm] = small_g[i], small_d[i], small_nm[i], small_nv[i]
    for i, nm in enumerate(big_names):
        grads[nm], deltas[nm], new_m[nm], new_v[nm] = big[i]
    return (loss, grad_x, *[grads[n] for n in order], *[deltas[n] for n in order],
            *[new_m[n] for n in order], *[new_v[n] for n in order])
```

```python
import functools
import math

import jax
import jax.numpy as jnp
from jax import lax
from jax.experimental import pallas as pl
from jax.experimental.pallas import tpu as pltpu

F32 = jnp.float32
BF16 = jnp.bfloat16
MESH = pl.DeviceIdType.MESH

D_MODEL = 1024
SEQ = 2048
D_FF = 4096
N_CHIPS = 4
A_WIDTH = 512
A_HEADS = 4
A_DK = 128
A_CHUNK = 64
A_SUB = 16
B_WIDTH = 512
B_GROUPS = 4
B_CHUNK = 128
C_HEADS = 16
C_HEAD_DIM = 64
C_ROT_HALF = 8
C_BLOCK = 128
C_DILATIONS = (1, 4, 16)
ROPE_THETA = 500000.0
EPS = 1e-6
ADAM_LR = 0.001
ADAM_B1 = 0.9
ADAM_B2 = 0.999
ADAM_EPS = 1e-08
ADAM_WD = 0.01
ADAM_STEP = 10

ROW_TILE = 512
FFN_ROWS = 1024
WGRAD_ROWS = 2048
VMEM_LIMIT = 56 * 1024 * 1024
NEG_BIG = -1e30


def _params(sem=None):
    return pltpu.CompilerParams(dimension_semantics=sem, vmem_limit_bytes=VMEM_LIMIT)


def _dot(a, b):
    return jnp.dot(a, b, preferred_element_type=F32)


def _dot_nt(a, b):
    return lax.dot_general(a, b, (((1,), (1,)), ((), ())), preferred_element_type=F32)


def _dot_tn(a, b):
    return lax.dot_general(a, b, (((0,), (0,)), ((), ())), preferred_element_type=F32)


def _rms(x, g):
    r = lax.rsqrt(jnp.mean(x * x, axis=-1, keepdims=True) + EPS)
    return x * r * g


def _rms_bwd(x, g, dy):
    r = lax.rsqrt(jnp.mean(x * x, axis=-1, keepdims=True) + EPS)
    xh = x * r
    dg = jnp.sum(dy * xh, axis=0, keepdims=True)
    dxh = dy * g
    dx = r * (dxh - xh * jnp.mean(dxh * xh, axis=-1, keepdims=True))
    return dx, dg


def _accumulate(ref, val, first):
    @pl.when(first)
    def _():
        ref[...] = val

    @pl.when(jnp.logical_not(first))
    def _():
        ref[...] += val


N_DEV = 8
ANY = pl.BlockSpec(memory_space=pl.ANY)


def _place():
    x, y, c = lax.axis_index("x"), lax.axis_index("y"), lax.axis_index("c")
    return x, y, c, [(1 - x, y), (x, 1 - y), (1 - x, 1 - y)]


class _Exchange:
    def __init__(self, kind, arrays):
        self.kind, self.arrays, self.n = kind, list(arrays), len(arrays)
        per_peer = pltpu.SemaphoreType.DMA((3 * self.n,))
        if kind == "gather":
            self.out_shape = [jax.ShapeDtypeStruct((N_CHIPS,) + a.shape, a.dtype) for a in self.arrays]
            self.scratch = [per_peer, per_peer, pltpu.SemaphoreType.DMA((self.n,)), per_peer, per_peer]
        else:
            self.out_shape = [jax.ShapeDtypeStruct(a.shape, a.dtype) for a in self.arrays]
            self.scratch = [per_peer, per_peer, pltpu.SemaphoreType.DMA((self.n,))]

    def _copies(self, ins, outs, sems):
        send_sems, recv_sems, local_sems = sems[:3]
        x, y, c, chips = _place()
        me = 2 * x + y
        local, remote = [], []
        for a in range(self.n):
            if self.kind == "gather":
                local.append(pltpu.make_async_copy(ins[a], outs[a].at[me], local_sems.at[a]))
                half = self.arrays[a].shape[0] // 2

                def rows(ref, core, half=half):
                    return ref.at[pl.ds(core * half, half)]
            else:
                local.append(pltpu.make_async_copy(ins[a].at[me], outs[a].at[3], local_sems.at[a]))
            for j, (px, py) in enumerate(chips):
                k = 3 * a + j
                peer = 2 * px + py

                def copy(src, dst, to, send_sem=send_sems.at[k], recv_sem=recv_sems.at[k]):
                    return pltpu.make_async_remote_copy(src_ref=src, dst_ref=dst, send_sem=send_sem, recv_sem=recv_sem,
                                                        device_id=to, device_id_type=MESH)

                if self.kind == "gather":
                    sent = copy(rows(ins[a], c), rows(outs[a].at[me], c), (px, py, c))
                    landed = copy(rows(ins[a], c), rows(outs[a].at[peer], c), (px, py, c))
                    on = dict(send_sem=sems[3].at[k], recv_sem=sems[4].at[k])
                    passed = copy(rows(outs[a].at[peer], c), rows(outs[a].at[peer], c), (x, y, 1 - c), **on)
                    handed = copy(rows(outs[a].at[peer], c), rows(outs[a].at[peer], 1 - c), (x, y, 1 - c), **on)
                    remote.append((sent, landed, passed, handed))
                else:
                    sent = copy(ins[a].at[peer], outs[a].at[j], (px, py, c))
                    remote.append((sent, sent, None, None))
        return local, remote

    def start(self, ins, outs, sems):
        local, remote = self._copies(ins, outs, sems)
        for cp in local:
            cp.start()
        for sent, _, _, _ in remote:
            sent.start()

    def finish(self, ins, outs, sems):
        local, remote = self._copies(ins, outs, sems)
        for _, landed, passed, _ in remote:
            landed.wait_recv()
            if passed is not None:
                passed.start()
        for sent, _, passed, handed in remote:
            if passed is not None:
                handed.wait_recv()
                passed.wait_send()
            sent.wait_send()
        for cp in local:
            cp.wait()


def _call(body, *, name, grid, in_specs, out_specs, out_shape, args, scratch_shapes=(), aliases=None, exchange=None):
    if exchange is None:
        return pl.pallas_call(
            body, name=name, grid=grid, in_specs=in_specs, out_specs=out_specs, out_shape=out_shape,
            scratch_shapes=list(scratch_shapes), input_output_aliases=aliases or {},
            compiler_params=_params(("arbitrary",) * len(grid)))(*args)
    n_in, n_out, n_scr, n_ex = len(in_specs), len(out_specs), len(scratch_shapes), exchange.n
    steps = grid

    def wrapped(*refs):
        ins, refs = refs[:n_in], refs[n_in:]
        ex_in, refs = refs[:n_ex], refs[n_ex:]
        outs, refs = refs[:n_out], refs[n_out:]
        ex_out, refs = refs[:n_ex], refs[n_ex:]
        scr, sems = refs[:n_scr], refs[n_scr:]
        first = functools.reduce(jnp.logical_and, [pl.program_id(k) == 0 for k in range(len(steps))])
        last = functools.reduce(jnp.logical_and, [pl.program_id(k) == steps[k] - 1 for k in range(len(steps))])

        @pl.when(first)
        def _():
            exchange.start(ex_in, ex_out, sems)

        body(*ins, *outs, *scr)

        @pl.when(last)
        def _():
            exchange.finish(ex_in, ex_out, sems)

    return pl.pallas_call(
        wrapped, name=name, grid=grid,
        in_specs=list(in_specs) + [ANY] * n_ex, out_specs=list(out_specs) + [ANY] * n_ex,
        out_shape=list(out_shape) + exchange.out_shape,
        scratch_shapes=list(scratch_shapes) + exchange.scratch, input_output_aliases=aliases or {},
        compiler_params=_params(("arbitrary",) * len(grid)))(*args, *exchange.arrays)


def exchange_alone(exchange, name):
    def body(*refs):
        n = exchange.n
        exchange.start(refs[:n], refs[n:2 * n], refs[2 * n:])
        exchange.finish(refs[:n], refs[n:2 * n], refs[2 * n:])

    return pl.pallas_call(
        body, name=name, in_specs=[ANY] * exchange.n, out_specs=[ANY] * exchange.n,
        out_shape=exchange.out_shape, scratch_shapes=exchange.scratch)(*exchange.arrays)


def norm_matmul(x, g, wg, name, exchange=None):
    t, d = x.shape
    nl = wg.shape[2]

    def body(x_ref, g_ref, w_ref, o_ref, h_ref):
        h = _rms(x_ref[...], g_ref[...]).astype(BF16)
        h_ref[...] = h
        for c in range(N_CHIPS):
            o_ref[:, c * nl:(c + 1) * nl] = _dot(h, w_ref[c])

    return _call(
        body, name=name, grid=(t // ROW_TILE,),
        in_specs=[pl.BlockSpec((ROW_TILE, d), lambda i: (i, 0)),
                  pl.BlockSpec((1, d), lambda i: (0, 0)),
                  pl.BlockSpec((N_CHIPS, d, nl), lambda i: (0, 0, 0))],
        out_specs=[pl.BlockSpec((ROW_TILE, N_CHIPS * nl), lambda i: (i, 0)),
                   pl.BlockSpec((ROW_TILE, d), lambda i: (i, 0))],
        out_shape=[jax.ShapeDtypeStruct((t, N_CHIPS * nl), F32), jax.ShapeDtypeStruct((t, d), BF16)],
        args=(x, g, wg), exchange=exchange)


def norm_matmul_bwd(dproj, wg, x, g, dres, name, exchange=None):
    t, d = x.shape
    nl = wg.shape[2]

    def body(dp_ref, w_ref, x_ref, g_ref, dres_ref, dx_ref, dg_ref):
        dh = _dot_nt(dp_ref[:, 0:nl].astype(BF16), w_ref[0])
        for c in range(1, N_CHIPS):
            dh += _dot_nt(dp_ref[:, c * nl:(c + 1) * nl].astype(BF16), w_ref[c])
        dx, dg = _rms_bwd(x_ref[...], g_ref[...], dh)
        dx_ref[...] = dres_ref[...] + dx
        _accumulate(dg_ref, dg, pl.program_id(0) == 0)

    row = pl.BlockSpec((ROW_TILE, d), lambda i: (i, 0))
    vec = pl.BlockSpec((1, d), lambda i: (0, 0))
    return _call(
        body, name=name, grid=(t // ROW_TILE,),
        in_specs=[pl.BlockSpec((ROW_TILE, N_CHIPS * nl), lambda i: (i, 0)),
                  pl.BlockSpec((N_CHIPS, d, nl), lambda i: (0, 0, 0)), row, vec, row],
        out_specs=[row, vec],
        out_shape=[jax.ShapeDtypeStruct((t, d), F32), jax.ShapeDtypeStruct((1, d), F32)],
        args=(dproj, wg, x, g, dres), exchange=exchange)


def out_proj(a, wg, x, g, name):
    t, d = x.shape
    kl = wg.shape[1]

    def body(a_ref, w_ref, x_ref, g_ref, mix_ref, xo_ref):
        acc = _dot(a_ref[:, 0:kl], w_ref[0])
        for c in range(1, N_CHIPS):
            acc += _dot(a_ref[:, c * kl:(c + 1) * kl], w_ref[c])
        mix_ref[...] = acc
        xo_ref[...] = x_ref[...] + _rms(acc, g_ref[...])

    row = pl.BlockSpec((ROW_TILE, d), lambda i: (i, 0))
    return pl.pallas_call(
        body, name=name, grid=(t // ROW_TILE,),
        in_specs=[row, pl.BlockSpec((N_CHIPS, kl, d), lambda i: (0, 0, 0)), row,
                  pl.BlockSpec((1, d), lambda i: (0, 0))],
        out_specs=[row, row],
        out_shape=[jax.ShapeDtypeStruct((t, d), F32), jax.ShapeDtypeStruct((t, d), F32)],
        compiler_params=_params(("arbitrary",)),
    )(a, wg, x, g)


def out_proj_bwd(dxo, mix, g, wg, name):
    t, d = mix.shape
    kl = wg.shape[1]

    def body(dxo_ref, mix_ref, g_ref, w_ref, dmix_ref, da_ref, dg_ref):
        dmix, dg = _rms_bwd(mix_ref[...], g_ref[...], dxo_ref[...])
        dmb = dmix.astype(BF16)
        dmix_ref[...] = dmb
        for c in range(N_CHIPS):
            da_ref[:, c * kl:(c + 1) * kl] = _dot_nt(dmb, w_ref[c])
        _accumulate(dg_ref, dg, pl.program_id(0) == 0)

    row = pl.BlockSpec((ROW_TILE, d), lambda i: (i, 0))
    vec = pl.BlockSpec((1, d), lambda i: (0, 0))
    return pl.pallas_call(
        body, name=name, grid=(t // ROW_TILE,),
        in_specs=[row, row, vec, pl.BlockSpec((N_CHIPS, kl, d), lambda i: (0, 0, 0))],
        out_specs=[row, row, vec],
        out_shape=[jax.ShapeDtypeStruct((t, d), BF16), jax.ShapeDtypeStruct((t, d), F32),
                   jax.ShapeDtypeStruct((1, d), F32)],
        compiler_params=_params(("arbitrary",)),
    )(dxo, mix, g, wg)


def ffn_fwd(x, gpre, w1g, w2g, gpost, name, exchange=None):
    t, d = x.shape
    hc = w1g.shape[2]

    def body(x_ref, gpre_ref, w1_ref, w2_ref, gpost_ref, xo_ref, h_ref, a_ref, y_ref, acc):
        c = pl.program_id(1)

        @pl.when(c == 0)
        def _():
            h_ref[...] = _rms(x_ref[...], gpre_ref[...]).astype(BF16)

        a = _dot(h_ref[...], w1_ref[...])
        a_ref[...] = a.astype(BF16)
        r = jnp.square(jnp.maximum(a, 0.0)).astype(BF16)
        _accumulate(acc, _dot(r, w2_ref[...]), c == 0)

        @pl.when(c == N_CHIPS - 1)
        def _():
            y = acc[...]
            y_ref[...] = y
            xo_ref[...] = x_ref[...] + _rms(y, gpost_ref[...])

    row = pl.BlockSpec((FFN_ROWS, d), lambda i, c: (i, 0))
    vec = pl.BlockSpec((1, d), lambda i, c: (0, 0))
    return _call(
        body, name=name, grid=(t // FFN_ROWS, N_CHIPS),
        in_specs=[row, vec,
                  pl.BlockSpec((None, d, hc), lambda i, c: (c, 0, 0)),
                  pl.BlockSpec((None, hc, d), lambda i, c: (c, 0, 0)), vec],
        out_specs=[row, row, pl.BlockSpec((FFN_ROWS, hc), lambda i, c: (i, c)), row],
        out_shape=[jax.ShapeDtypeStruct((t, d), F32), jax.ShapeDtypeStruct((t, d), BF16),
                   jax.ShapeDtypeStruct((t, N_CHIPS * hc), BF16), jax.ShapeDtypeStruct((t, d), F32)],
        scratch_shapes=[pltpu.VMEM((FFN_ROWS, d), F32)],
        args=(x, gpre, w1g, w2g, gpost), exchange=exchange)


def ffn_bwd(dxo, x, y, a, gpre, gpost, w1g, w2g, name, exchange=None):
    t, d = x.shape
    hc = w1g.shape[2]

    def body(dxo_ref, x_ref, y_ref, a_ref, gpre_ref, gpost_ref, w1_ref, w2_ref,
             dxi_ref, dy_ref, da_ref, dgpre_ref, dgpost_ref, acc):
        i, c = pl.program_id(0), pl.program_id(1)

        @pl.when(c == 0)
        def _():
            dy, dg = _rms_bwd(y_ref[...], gpost_ref[...], dxo_ref[...])
            dy_ref[...] = dy.astype(BF16)
            _accumulate(dgpost_ref, dg, i == 0)

        dr = _dot_nt(dy_ref[...], w2_ref[...])
        da = (dr * (2.0 * jnp.maximum(a_ref[...].astype(F32), 0.0))).astype(BF16)
        da_ref[...] = da
        _accumulate(acc, _dot_nt(da, w1_ref[...]), c == 0)

        @pl.when(c == N_CHIPS - 1)
        def _():
            dx, dg = _rms_bwd(x_ref[...], gpre_ref[...], acc[...])
            dxi_ref[...] = dxo_ref[...] + dx
            _accumulate(dgpre_ref, dg, i == 0)

    row = pl.BlockSpec((ROW_TILE, d), lambda i, c: (i, 0))
    vec = pl.BlockSpec((1, d), lambda i, c: (0, 0))
    hid = pl.BlockSpec((ROW_TILE, hc), lambda i, c: (i, c))
    return _call(
        body, name=name, grid=(t // ROW_TILE, N_CHIPS),
        in_specs=[row, row, row, hid, vec, vec,
                  pl.BlockSpec((None, d, hc), lambda i, c: (c, 0, 0)),
                  pl.BlockSpec((None, hc, d), lambda i, c: (c, 0, 0))],
        out_specs=[row, row, hid, vec, vec],
        out_shape=[jax.ShapeDtypeStruct((t, d), F32), jax.ShapeDtypeStruct((t, d), BF16),
                   jax.ShapeDtypeStruct((t, N_CHIPS * hc), BF16),
                   jax.ShapeDtypeStruct((1, d), F32), jax.ShapeDtypeStruct((1, d), F32)],
        scratch_shapes=[pltpu.VMEM((ROW_TILE, d), F32)],
        args=(dxo, x, y, a, gpre, gpost, w1g, w2g), exchange=exchange)


def weight_grad(a, b, chunked, bk, bn, relu2, name, exchange=None):
    t = a.shape[0]
    a_on = chunked == "a"
    rows = min(t, WGRAD_ROWS)
    n_steps = t // rows

    def body(a_ref, b_ref, o_ref, acc):
        s = pl.program_id(1)
        av = a_ref[...]
        if relu2:
            av = jnp.square(jnp.maximum(av.astype(F32), 0.0))
        _accumulate(acc, _dot_tn(av.astype(BF16), b_ref[...].astype(BF16)), s == 0)

        @pl.when(s == n_steps - 1)
        def _():
            o_ref[...] = acc[...].astype(BF16)

    res = _call(
        body, name=name, grid=(N_CHIPS, n_steps),
        in_specs=[pl.BlockSpec((rows, bk), (lambda c, s: (s, c)) if a_on else (lambda c, s: (s, 0))),
                  pl.BlockSpec((rows, bn), (lambda c, s: (s, 0)) if a_on else (lambda c, s: (s, c)))],
        out_specs=[pl.BlockSpec((None, bk, bn), lambda c, s: (c, 0, 0))],
        out_shape=[jax.ShapeDtypeStruct((N_CHIPS, bk, bn), BF16)],
        scratch_shapes=[pltpu.VMEM((bk, bn), F32)],
        args=(a, b), exchange=exchange)
    return res[0] if exchange is None else res


def loss_grad(xf, target, name):
    t, d = xf.shape

    def body(x_ref, t_ref, dy_ref, l_ref):
        e = x_ref[...] - t_ref[...]
        dy_ref[...] = e * (1.0 / d)
        part = jnp.sum(jnp.sum(e * e, axis=-1, keepdims=True), axis=0, keepdims=True) * (0.5 / d)
        _accumulate(l_ref, part, pl.program_id(0) == 0)

    row = pl.BlockSpec((ROW_TILE, d), lambda i: (i, 0))
    return pl.pallas_call(
        body, name=name, grid=(t // ROW_TILE,),
        in_specs=[row, row],
        out_specs=[row, pl.BlockSpec((1, 1), lambda i: (0, 0))],
        out_shape=[jax.ShapeDtypeStruct((t, d), F32), jax.ShapeDtypeStruct((1, 1), F32)],
        compiler_params=_params(("arbitrary",)),
    )(xf, target)


def _hgrn2_chunk(st, qs, fls, ivs, gls, l0, l1, l2, ng):
    nsub = len(qs)
    mx = jnp.maximum(jnp.maximum(l0, l1), l2)
    e0, e1, e2 = jnp.exp(l0 - mx), jnp.exp(l1 - mx), jnp.exp(l2 - mx)
    lb = e0 / (e0 + e1 + e2)
    rows = lax.broadcasted_iota(jnp.int32, (A_SUB, A_SUB), 0)
    cols = lax.broadcasted_iota(jnp.int32, (A_SUB, A_SUB), 1)
    tri = (rows >= cols).astype(F32)
    keep = (lax.broadcasted_iota(jnp.int32, (A_SUB, A_SUB, A_DK), 0)
            >= lax.broadcasted_iota(jnp.int32, (A_SUB, A_SUB, A_DK), 1))
    base = jnp.zeros_like(l0)
    bases, gs, ks, qfs = [], [], [], []
    for i in range(nsub):
        f = lb + (1.0 - lb) * jax.nn.sigmoid(fls[i])
        logf = jnp.log(f)
        bases.append(base)
        gs.append(base + jnp.dot(tri, logf, precision=lax.Precision.HIGHEST, preferred_element_type=F32))
        base = base + jnp.sum(logf, axis=0, keepdims=True)
        ks.append(1.0 - f)
        qfs.append(jax.nn.silu(qs[i]))
    g_last = base
    stb = st.astype(BF16)
    outs = []
    for i in range(nsub):
        o = _dot_nt((qfs[i] * jnp.exp(gs[i])).astype(BF16), stb)
        if i > 0:
            qt = (qfs[i] * jnp.exp(gs[i] - bases[i])).astype(BF16)
            kk = jnp.concatenate([ks[j] * jnp.exp(bases[i] - gs[j]) for j in range(i)], axis=0).astype(BF16)
            vv = jnp.concatenate(ivs[:i], axis=0).astype(BF16)
            o = o + _dot(_dot_nt(qt, kk).astype(BF16), vv)
        dec = jnp.exp(jnp.where(keep, gs[i][:, None, :] - gs[i][None, :, :], NEG_BIG))
        s_diag = jnp.sum(qfs[i][:, None, :] * ks[i][None, :, :] * dec, axis=-1)
        o = o + _dot(s_diag.astype(BF16), ivs[i].astype(BF16))
        o = o * lax.rsqrt(jnp.mean(o * o, axis=-1, keepdims=True) + EPS) * ng
        outs.append(o * jax.nn.silu(gls[i]))
    kdec = jnp.concatenate([ks[j] * jnp.exp(g_last - gs[j]) for j in range(nsub)], axis=0).astype(BF16)
    vall = jnp.concatenate(ivs, axis=0).astype(BF16)
    new_st = st * jnp.exp(g_last) + _dot_tn(vall, kdec)
    return new_st, outs


A_MAX_LOG_DECAY = 80.0


def _split3(x):
    hi = x.astype(BF16)
    r1 = x - hi.astype(F32)
    mid = r1.astype(BF16)
    return hi, mid, (r1 - mid.astype(F32)).astype(BF16)


def _tri_matmul(x, transpose):
    n = x.shape[0]
    r = lax.broadcasted_iota(jnp.int32, (n, n), 0)
    c = lax.broadcasted_iota(jnp.int32, (n, n), 1)
    tri = ((r <= c) if transpose else (r >= c)).astype(BF16)
    hi, mid, lo = _split3(x)
    return (_dot(tri, lo) + _dot(tri, mid)) + _dot(tri, hi)


@jax.custom_vjp
def _cumsum_rows(x):
    return _tri_matmul(x, False)


def _cumsum_rows_fwd(x):
    return _tri_matmul(x, False), None


def _cumsum_rows_bwd(_, dy):
    return (_tri_matmul(dy, True),)


_cumsum_rows.defvjp(_cumsum_rows_fwd, _cumsum_rows_bwd)


def _lower_bound(l0, l1, l2):
    mx = jnp.maximum(jnp.maximum(l0, l1), l2)
    e0, e1, e2 = jnp.exp(l0 - mx), jnp.exp(l1 - mx), jnp.exp(l2 - mx)
    return e0 / (e0 + e1 + e2)


def _b(x):
    return x.astype(BF16)


@jax.custom_vjp
def _mm(a, b):
    return _dot(_b(a), _b(b))


_mm.defvjp(lambda a, b: (_mm(a, b), (a, b)),
           lambda res, d: (_dot_nt(_b(d), _b(res[1])), _dot_tn(_b(res[0]), _b(d))))


@jax.custom_vjp
def _mm_nt(a, b):
    return _dot_nt(_b(a), _b(b))


_mm_nt.defvjp(lambda a, b: (_mm_nt(a, b), (a, b)),
              lambda res, d: (_dot(_b(d), _b(res[1])), _dot_tn(_b(d), _b(res[0]))))


def _dot_split(dot, a, b):
    ah, bh = _b(a), _b(b)
    al, bl = _b(a - ah.astype(F32)), _b(b - bh.astype(F32))
    return (dot(ah, bl) + dot(al, bh)) + dot(ah, bh)


@jax.custom_vjp
def _mm_scores(a, b):
    return _dot_nt(_b(a), _b(b))


_mm_scores.defvjp(lambda a, b: (_mm_scores(a, b), (a, b)),
                  lambda res, d: (_dot_split(_dot, d, res[1]), _dot_split(_dot_tn, d, res[0])))


@jax.custom_vjp
def _mm_tn(a, b):
    return _dot_tn(_b(a), _b(b))


_mm_tn.defvjp(lambda a, b: (_mm_tn(a, b), (a, b)),
              lambda res, d: (_dot_nt(_b(res[1]), _b(d)), _dot(_b(res[0]), _b(d))))


@jax.custom_vjp
def _split_heads(x):
    return tuple(x[:, h * A_DK:(h + 1) * A_DK] for h in range(A_HEADS))


def _split_heads_fwd(x):
    return _split_heads(x), None


def _split_heads_bwd(_, parts):
    return (jnp.concatenate(parts, axis=1),)


_split_heads.defvjp(_split_heads_fwd, _split_heads_bwd)


def _hgrn2_chunk_fast(sts, q, fl, iv, gl, l0, l1, l2, ng):
    lb = _lower_bound(l0, l1, l2)
    f = lb + (1.0 - lb) * jax.nn.sigmoid(fl)
    logf = jnp.log(f)
    g = _cumsum_rows(logf)
    g_last = jnp.sum(logf, axis=0, keepdims=True)
    k = 1.0 - f
    qgs = _split_heads(jax.nn.silu(q) * jnp.exp(g))
    kgs = _split_heads(k * jnp.exp(-g))
    kds = _split_heads(k * jnp.exp(g_last - g))
    ivs = _split_heads(iv)
    decays = _split_heads(jnp.exp(g_last))
    n = q.shape[0]
    causal = lax.broadcasted_iota(jnp.int32, (n, n), 0) >= lax.broadcasted_iota(jnp.int32, (n, n), 1)
    raw = [_mm_scores(qg, kg) for qg, kg in zip(qgs, kgs)]
    inter = [_mm_nt(qg, st) for qg, st in zip(qgs, sts)]
    scores = [jnp.where(causal, s, 0.0) for s in raw]
    os = [a + _mm(s, v) for a, s, v in zip(inter, scores, ivs)]
    new_sts = [st * d + _mm_tn(v, kd) for st, d, v, kd in zip(sts, decays, ivs, kds)]
    os = [o * lax.rsqrt(jnp.mean(o * o, axis=-1, keepdims=True) + EPS) for o in os]
    return new_sts, jnp.concatenate(os, axis=1) * ng * jax.nn.silu(gl)


def _chunk_decays_mildly(f_ref, lb_ref):
    lb = _lower_bound(lb_ref[0:1, :], lb_ref[1:2, :], lb_ref[2:3, :])
    logf = jnp.log(lb + (1.0 - lb) * jax.nn.sigmoid(f_ref[...]))
    return jnp.min(jnp.sum(logf, axis=0, keepdims=True)) >= -A_MAX_LOG_DECAY


def _sub_blocks(ref, head):
    lanes = slice(head * A_DK, (head + 1) * A_DK)
    return [ref[i * A_SUB:(i + 1) * A_SUB, lanes] for i in range(A_CHUNK // A_SUB)]


def hgrn2_fwd(proj, lb_table, a_norm, batch, name, exchange=None):
    t = proj.shape[0]
    n_chunks = t // batch // A_CHUNK
    nblk = A_WIDTH // A_DK

    def body(q_ref, f_ref, i_ref, g_ref, lb_ref, ng_ref, o_ref, st_ref, st):
        @pl.when(pl.program_id(1) == 0)
        def _():
            st[...] = jnp.zeros_like(st)

        st_ref[...] = st[...]
        mild = _chunk_decays_mildly(f_ref, lb_ref)

        @pl.when(mild)
        def _():
            new_sts, o = _hgrn2_chunk_fast(
                [st[h] for h in range(A_HEADS)], q_ref[...], f_ref[...], i_ref[...], g_ref[...],
                lb_ref[0:1, :], lb_ref[1:2, :], lb_ref[2:3, :], ng_ref[...])
            for h in range(A_HEADS):
                st[h] = new_sts[h]
            o_ref[...] = o.astype(BF16)

        @pl.when(jnp.logical_not(mild))
        def _():
            for h in range(A_HEADS):
                lanes = slice(h * A_DK, (h + 1) * A_DK)
                new_st, outs = _hgrn2_chunk(
                    st[h], _sub_blocks(q_ref, h), _sub_blocks(f_ref, h), _sub_blocks(i_ref, h),
                    _sub_blocks(g_ref, h), lb_ref[0:1, lanes], lb_ref[1:2, lanes], lb_ref[2:3, lanes],
                    ng_ref[:, lanes])
                st[h] = new_st
                for i, o in enumerate(outs):
                    o_ref[i * A_SUB:(i + 1) * A_SUB, lanes] = o.astype(BF16)

    def part(k):
        return pl.BlockSpec((A_CHUNK, A_WIDTH), lambda b, n: (b * n_chunks + n, k))

    return _call(
        body, name=name, grid=(batch, n_chunks),
        in_specs=[part(0), part(1), part(2), part(3),
                  pl.BlockSpec((3, A_WIDTH), lambda b, n: (0, 0)), pl.BlockSpec((1, A_WIDTH), lambda b, n: (0, 0))],
        out_specs=[pl.BlockSpec((A_CHUNK, A_WIDTH), lambda b, n: (b * n_chunks + n, 0)),
                   pl.BlockSpec((None, A_HEADS, A_DK, A_DK), lambda b, n: (b * n_chunks + n, 0, 0, 0))],
        out_shape=[jax.ShapeDtypeStruct((t, A_WIDTH), BF16),
                   jax.ShapeDtypeStruct((t // A_CHUNK, A_HEADS, A_DK, A_DK), F32)],
        scratch_shapes=[pltpu.VMEM((A_HEADS, A_DK, A_DK), F32)],
        args=(proj, proj, proj, proj, lb_table, a_norm), exchange=exchange)


def hgrn2_bwd(proj, states, lb_table, a_norm, do, batch, name, exchange=None):
    t = proj.shape[0]
    n_chunks = t // batch // A_CHUNK

    def body(q_ref, f_ref, i_ref, g_ref, st_ref, lb_ref, ng_ref, do_ref, dp_ref, dlb_ref, dng_ref, dst):
        @pl.when(jnp.logical_and(pl.program_id(0) == 0, pl.program_id(1) == 0))
        def _():
            dlb_ref[...] = jnp.zeros_like(dlb_ref)
            dng_ref[...] = jnp.zeros_like(dng_ref)

        @pl.when(pl.program_id(1) == 0)
        def _():
            dst[...] = jnp.zeros_like(dst)

        mild = _chunk_decays_mildly(f_ref, lb_ref)

        @pl.when(mild)
        def _():
            _, vjp = jax.vjp(
                _hgrn2_chunk_fast, [st_ref[h] for h in range(A_HEADS)], q_ref[...], f_ref[...], i_ref[...],
                g_ref[...], lb_ref[0:1, :], lb_ref[1:2, :], lb_ref[2:3, :], ng_ref[...])
            d_sts, dq, df, di, dg, dl0, dl1, dl2, dng = vjp(([dst[h] for h in range(A_HEADS)], do_ref[...].astype(F32)))
            for h in range(A_HEADS):
                dst[h] = d_sts[h]
            for k, part in enumerate((dq, df, di, dg)):
                dp_ref[:, k * A_WIDTH:(k + 1) * A_WIDTH] = part
            for row, val in enumerate((dl0, dl1, dl2)):
                dlb_ref[row:row + 1, :] += val
            dng_ref[...] += dng

        @pl.when(jnp.logical_not(mild))
        def _():
            for h in range(A_HEADS):
                lanes = slice(h * A_DK, (h + 1) * A_DK)
                _, vjp = jax.vjp(
                    _hgrn2_chunk, st_ref[h], _sub_blocks(q_ref, h), _sub_blocks(f_ref, h), _sub_blocks(i_ref, h),
                    _sub_blocks(g_ref, h), lb_ref[0:1, lanes], lb_ref[1:2, lanes], lb_ref[2:3, lanes],
                    ng_ref[:, lanes])
                douts = [x.astype(F32) for x in _sub_blocks(do_ref, h)]
                d_st, dqs, dfs, dis, dgs, dl0, dl1, dl2, dng = vjp((dst[h], douts))
                dst[h] = d_st
                for k, parts in enumerate((dqs, dfs, dis, dgs)):
                    for i in range(A_CHUNK // A_SUB):
                        dp_ref[i * A_SUB:(i + 1) * A_SUB,
                               k * A_WIDTH + h * A_DK:k * A_WIDTH + (h + 1) * A_DK] = parts[i]
                for row, val in enumerate((dl0, dl1, dl2)):
                    dlb_ref[row:row + 1, lanes] += val
                dng_ref[:, lanes] += dng

    def rev(b, n):
        return b * n_chunks + (n_chunks - 1 - n)

    def part(k):
        return pl.BlockSpec((A_CHUNK, A_WIDTH), lambda b, n: (rev(b, n), k))

    const3 = pl.BlockSpec((3, A_WIDTH), lambda b, n: (0, 0))
    const1 = pl.BlockSpec((1, A_WIDTH), lambda b, n: (0, 0))
    return _call(
        body, name=name, grid=(batch, n_chunks),
        in_specs=[part(0), part(1), part(2), part(3),
                  pl.BlockSpec((None, A_HEADS, A_DK, A_DK), lambda b, n: (rev(b, n), 0, 0, 0)),
                  const3, const1, part(0)],
        out_specs=[pl.BlockSpec((A_CHUNK, 4 * A_WIDTH), lambda b, n: (rev(b, n), 0)), const3, const1],
        out_shape=[jax.ShapeDtypeStruct((t, 4 * A_WIDTH + 2 * B_WIDTH), F32),
                   jax.ShapeDtypeStruct((3, A_WIDTH), F32), jax.ShapeDtypeStruct((1, A_WIDTH), F32)],
        scratch_shapes=[pltpu.VMEM((A_HEADS, A_DK, A_DK), F32)],
        args=(proj, proj, proj, proj, states, lb_table, a_norm, do), exchange=exchange)


B_GDIM = B_WIDTH // B_GROUPS
B_ROWS = 512


def _gmlp_chunk(ubs, vbs, lngs, lnbs, ws, bcols):
    vs = [jax.nn.gelu(v) for v in vbs]
    mu = sum(jnp.sum(v, axis=-1, keepdims=True) for v in vs) * (1.0 / B_WIDTH)
    var = sum(jnp.sum(jnp.square(v - mu), axis=-1, keepdims=True) for v in vs) * (1.0 / B_WIDTH)
    rstd = lax.rsqrt(var + EPS)
    tril = (lax.broadcasted_iota(jnp.int32, (B_CHUNK, B_CHUNK), 0)
            >= lax.broadcasted_iota(jnp.int32, (B_CHUNK, B_CHUNK), 1))
    outs = []
    for g in range(B_GROUPS):
        vn = (vs[g] - mu) * rstd * lngs[g] + lnbs[g]
        w = jnp.where(tril, ws[g], 0.0).astype(BF16)
        outs.append(jax.nn.gelu(ubs[g]) * (_dot(w, vn.astype(BF16)) + bcols[g]))
    return outs


def _gmlp_args(u_ref, v_ref, lng_ref, lnb_ref, w_ref, bt_ref, rows):
    def groups(ref):
        return [ref[rows, g * B_GDIM:(g + 1) * B_GDIM] for g in range(B_GROUPS)]

    def vec(ref):
        return [ref[:, g * B_GDIM:(g + 1) * B_GDIM] for g in range(B_GROUPS)]

    return (groups(u_ref), groups(v_ref), vec(lng_ref), vec(lnb_ref),
            [w_ref[g] for g in range(B_GROUPS)], [bt_ref[:, g:g + 1] for g in range(B_GROUPS)])


def gmlp_fwd(proj, oa, ln_g, ln_b, w, bias_t, name, exchange=None):
    t = proj.shape[0]

    def body(u_ref, v_ref, oa_ref, lng_ref, lnb_ref, w_ref, bt_ref, o_ref):
        o_ref[:, 0:A_WIDTH] = oa_ref[...]
        for n in range(B_ROWS // B_CHUNK):
            rows = slice(n * B_CHUNK, (n + 1) * B_CHUNK)
            outs = _gmlp_chunk(*_gmlp_args(u_ref, v_ref, lng_ref, lnb_ref, w_ref, bt_ref, rows))
            for g, o in enumerate(outs):
                o_ref[rows, A_WIDTH + g * B_GDIM:A_WIDTH + (g + 1) * B_GDIM] = o.astype(BF16)

    vec = pl.BlockSpec((1, B_WIDTH), lambda i: (0, 0))
    return _call(
        body, name=name, grid=(t // B_ROWS,),
        in_specs=[pl.BlockSpec((B_ROWS, B_WIDTH), lambda i: (i, 4)), pl.BlockSpec((B_ROWS, B_WIDTH), lambda i: (i, 5)),
                  pl.BlockSpec((B_ROWS, A_WIDTH), lambda i: (i, 0)), vec, vec,
                  pl.BlockSpec((B_GROUPS, B_CHUNK, B_CHUNK), lambda i: (0, 0, 0)),
                  pl.BlockSpec((B_CHUNK, B_GROUPS), lambda i: (0, 0))],
        out_specs=[pl.BlockSpec((B_ROWS, A_WIDTH + B_WIDTH), lambda i: (i, 0))],
        out_shape=[jax.ShapeDtypeStruct((t, A_WIDTH + B_WIDTH), BF16)],
        args=(proj, proj, oa, ln_g, ln_b, w, bias_t), exchange=exchange)


def gmlp_bwd(proj, dmixin, ln_g, ln_b, w, bias_t, dproj, name, exchange=None):
    t = proj.shape[0]

    def body(u_ref, v_ref, do_ref, lng_ref, lnb_ref, w_ref, bt_ref, dp_in_ref,
             dp_ref, dlng_ref, dlnb_ref, dw_ref, dbt_ref):
        del dp_in_ref

        @pl.when(pl.program_id(0) == 0)
        def _():
            for ref in (dlng_ref, dlnb_ref, dw_ref, dbt_ref):
                ref[...] = jnp.zeros_like(ref)

        for n in range(B_ROWS // B_CHUNK):
            rows = slice(n * B_CHUNK, (n + 1) * B_CHUNK)
            _, vjp = jax.vjp(_gmlp_chunk, *_gmlp_args(u_ref, v_ref, lng_ref, lnb_ref, w_ref, bt_ref, rows))
            douts = [do_ref[rows, g * B_GDIM:(g + 1) * B_GDIM] for g in range(B_GROUPS)]
            dus, dvs, dlngs, dlnbs, dws, dbs = vjp(douts)
            for g in range(B_GROUPS):
                lanes = slice(g * B_GDIM, (g + 1) * B_GDIM)
                dp_ref[rows, lanes] = dus[g]
                dp_ref[rows, B_WIDTH + g * B_GDIM:B_WIDTH + (g + 1) * B_GDIM] = dvs[g]
                dlng_ref[:, lanes] += dlngs[g]
                dlnb_ref[:, lanes] += dlnbs[g]
                dw_ref[g] += dws[g]
                dbt_ref[:, g:g + 1] += dbs[g]

    vec = pl.BlockSpec((1, B_WIDTH), lambda i: (0, 0))
    wspec = pl.BlockSpec((B_GROUPS, B_CHUNK, B_CHUNK), lambda i: (0, 0, 0))
    bspec = pl.BlockSpec((B_CHUNK, B_GROUPS), lambda i: (0, 0))
    return _call(
        body, name=name, grid=(t // B_ROWS,),
        in_specs=[pl.BlockSpec((B_ROWS, B_WIDTH), lambda i: (i, 4)), pl.BlockSpec((B_ROWS, B_WIDTH), lambda i: (i, 5)),
                  pl.BlockSpec((B_ROWS, B_WIDTH), lambda i: (i, 1)), vec, vec, wspec, bspec,
                  pl.BlockSpec(memory_space=pl.ANY)],
        out_specs=[pl.BlockSpec((B_ROWS, 2 * B_WIDTH), lambda i: (i, 2)), vec, vec, wspec, bspec],
        out_shape=[jax.ShapeDtypeStruct(dproj.shape, F32), jax.ShapeDtypeStruct((1, B_WIDTH), F32),
                   jax.ShapeDtypeStruct((1, B_WIDTH), F32), jax.ShapeDtypeStruct((B_GROUPS, B_CHUNK, B_CHUNK), F32),
                   jax.ShapeDtypeStruct((B_CHUNK, B_GROUPS), F32)],
        aliases={7: 0}, args=(proj, proj, dmixin, ln_g, ln_b, w, bias_t, dproj), exchange=exchange)


C_PAIR = 2 * C_HEAD_DIM
C_PAIRS = C_HEADS // 2
C_SCALE = 1.0 / math.sqrt(C_HEAD_DIM)
C_ROT_DIM = 2 * C_ROT_HALF
ROPE_ROWS = 1024


def rope_tables(pos_col, name):
    t = pos_col.shape[0]

    def body(p_ref, c_ref, a_ref, b_ref):
        lane = jnp.bitwise_and(lax.broadcasted_iota(jnp.int32, (1, C_PAIR), 1), C_HEAD_DIM - 1)
        j = jnp.bitwise_and(lane, C_ROT_HALF - 1).astype(F32)
        inv = jnp.exp(j * (-math.log(ROPE_THETA) / C_ROT_HALF))
        ang = p_ref[...].astype(F32) * inv
        cos, sin = jnp.cos(ang), jnp.sin(ang)
        c_ref[...] = jnp.where(lane < C_ROT_DIM, cos, 1.0)
        a_ref[...] = jnp.where(lane < C_ROT_HALF, -sin, 0.0)
        b_ref[...] = jnp.where(jnp.logical_and(lane >= C_ROT_HALF, lane < C_ROT_DIM), sin, 0.0)

    tab = pl.BlockSpec((ROPE_ROWS, C_PAIR), lambda i: (i, 0))
    return pl.pallas_call(
        body, name=name, grid=(t // ROPE_ROWS,),
        in_specs=[pl.BlockSpec((ROPE_ROWS, 1), lambda i: (i, 0))],
        out_specs=[tab, tab, tab],
        out_shape=[jax.ShapeDtypeStruct((t, C_PAIR), F32)] * 3,
        compiler_params=_params(("arbitrary",)),
    )(pos_col)


def _rope(x, c, a, b):
    return x * c + pltpu.roll(x, C_PAIR - C_ROT_HALF, 1) * a + pltpu.roll(x, C_ROT_HALF, 1) * b


def _rope_t(d, c, a, b):
    return d * c + pltpu.roll(d * a, C_ROT_HALF, 1) + pltpu.roll(d * b, C_PAIR - C_ROT_HALF, 1)


def _attn_rows(idx, dil):
    nblk = SEQ // dil // C_BLOCK
    r, n = idx // nblk, idx % nblk
    start = r + dil * C_BLOCK * n
    prev = r + dil * C_BLOCK * jnp.maximum(n - 1, 0)
    if dil == 1:
        return pl.ds(pl.multiple_of(start, C_BLOCK), C_BLOCK), pl.ds(pl.multiple_of(prev, C_BLOCK), C_BLOCK), n > 0
    return pl.ds(start, C_BLOCK, stride=dil), pl.ds(prev, C_BLOCK, stride=dil), n > 0


def _head_masks():
    low = lax.broadcasted_iota(jnp.int32, (1, C_PAIR), 1) < C_HEAD_DIM
    return low, jnp.logical_not(low)


def _attn_mask(has_prev):
    i = jnp.bitwise_and(lax.broadcasted_iota(jnp.int32, (2 * C_BLOCK, 2 * C_BLOCK), 0), C_BLOCK - 1)
    j = lax.broadcasted_iota(jnp.int32, (2 * C_BLOCK, 2 * C_BLOCK), 1)
    return jnp.logical_or(j <= i, jnp.logical_and(j - C_BLOCK >= i, has_prev))


def _stack_heads(x):
    low, high = _head_masks()
    return jnp.concatenate([jnp.where(low, x, 0.0), jnp.where(high, x, 0.0)], axis=0)


def _unstack_heads(x):
    low, _ = _head_masks()
    return jnp.where(low, x[:C_BLOCK], x[C_BLOCK:])


def attn_fwd(qkv, cos_t, sin_a, sin_b, batch, name, exchange=None):
    t = qkv.shape[0]
    nbr = len(C_DILATIONS)

    def body(q_ref, k_ref, v_ref, c_ref, a_ref, b_ref, o_ref, l_ref, qs, ks, *stats):
        acc, mm, dd = stats[0:nbr], stats[nbr:2 * nbr], stats[2 * nbr:3 * nbr]
        c, a, b = c_ref[...], a_ref[...], b_ref[...]
        qs[...] = _rope(q_ref[...], c, a, b) * C_SCALE
        ks[...] = _rope(k_ref[...], c, a, b)
        def load(idx, dil):
            rows, prev, has_prev = _attn_rows(idx, dil)
            return rows, (has_prev, qs[rows, :], ks[rows, :], ks[prev, :], v_ref[rows, :], v_ref[prev, :])

        def scores(has_prev, q, k_own, k_prev, v_own, v_prev):
            k_cat = jnp.concatenate([k_own, k_prev], axis=0).astype(BF16)
            return jnp.where(_attn_mask(has_prev), _dot_nt(_stack_heads(q).astype(BF16), k_cat), NEG_BIG)

        def softmax(s):
            m = jnp.max(s, axis=-1, keepdims=True)
            p = jnp.exp(s - m)
            return p.astype(BF16), m, jnp.sum(p, axis=-1, keepdims=True)

        def values(pb, has_prev, q, k_own, k_prev, v_own, v_prev):
            low, high = _head_masks()
            v_cat = jnp.concatenate([v_own, v_prev], axis=0)
            p_wide = jnp.concatenate([pb[:C_BLOCK], pb[C_BLOCK:]], axis=1)
            v_tall = jnp.concatenate([jnp.where(low, v_cat, 0.0), jnp.where(high, v_cat, 0.0)], axis=0).astype(BF16)
            return _dot(p_wide, v_tall)

        for bi, dil in enumerate(C_DILATIONS):
            def pair(i, carry, bi=bi, dil=dil):
                low, _ = _head_masks()
                loaded = [load(2 * i + k, dil) for k in range(2)]
                ss = [scores(*ops) for _, ops in loaded]
                sm = [softmax(s) for s in ss]
                pvs = [values(pb, *ops) for (pb, _, _), (_, ops) in zip(sm, loaded)]
                for (rows, _), (_, m, den), pv in zip(loaded, sm, pvs):
                    acc[bi][rows, :] = pv
                    mm[bi][rows, :] = jnp.where(low, m[:C_BLOCK], m[C_BLOCK:])
                    dd[bi][rows, :] = jnp.where(low, den[:C_BLOCK], den[C_BLOCK:])
                return carry

            lax.fori_loop(0, SEQ // C_BLOCK // 2, pair, 0)
        step = 256
        for r0 in range(0, SEQ, step):
            rr = slice(r0, r0 + step)
            ms = [mm[g][rr, :] for g in range(nbr)]
            m_all = functools.reduce(jnp.maximum, ms)
            ws = [jnp.exp(m - m_all) for m in ms]
            num = sum(acc[g][rr, :] * ws[g] for g in range(nbr))
            den = sum(dd[g][rr, :] * ws[g] for g in range(nbr))
            o_ref[rr, :] = (num / den).astype(BF16)
            l_ref[rr, :] = m_all + jnp.log(den)

    def col(k):
        return pl.BlockSpec((SEQ, C_PAIR), lambda b, p: (b, k * C_PAIRS + p))

    tab = pl.BlockSpec((SEQ, C_PAIR), lambda b, p: (b, 0))
    return _call(
        body, name=name, grid=(batch, C_PAIRS),
        in_specs=[col(0), col(1), col(2), tab, tab, tab],
        out_specs=[col(0), col(0)],
        out_shape=[jax.ShapeDtypeStruct((t, D_MODEL), BF16), jax.ShapeDtypeStruct((t, D_MODEL), F32)],
        scratch_shapes=[pltpu.VMEM((SEQ, C_PAIR), F32)] * (2 + 3 * nbr),
        args=(qkv, qkv, qkv, cos_t, sin_a, sin_b), exchange=exchange)


def attn_bwd(qkv, cos_t, sin_a, sin_b, o, lse, do, batch, name, exchange=None):
    t = qkv.shape[0]

    def body(q_ref, k_ref, v_ref, c_ref, a_ref, b_ref, o_ref, l_ref, do_ref, dq_ref, dk_ref, dv_ref,
             qs, ks, dqs, dks, dvs, dlt):
        c, a, b = c_ref[...], a_ref[...], b_ref[...]
        qs[...] = _rope(q_ref[...], c, a, b) * C_SCALE
        ks[...] = _rope(k_ref[...], c, a, b)
        prod = do_ref[...] * o_ref[...].astype(F32)
        low = lax.broadcasted_iota(jnp.int32, (1, C_PAIR), 1) < C_HEAD_DIM
        s_low = jnp.sum(jnp.where(low, prod, 0.0), axis=-1, keepdims=True)
        s_all = jnp.sum(prod, axis=-1, keepdims=True)
        dlt[...] = jnp.where(low, s_low, s_all - s_low)
        dqs[...] = jnp.zeros_like(dqs)
        dks[...] = jnp.zeros_like(dks)
        dvs[...] = jnp.zeros_like(dvs)
        def load(idx, dil):
            rows, prev, has_prev = _attn_rows(idx, dil)
            return (rows, prev), (has_prev, qs[rows, :], do_ref[rows, :], ks[rows, :], ks[prev, :],
                                  v_ref[rows, :], v_ref[prev, :], l_ref[rows, :], dlt[rows, :])

        def operands(has_prev, q, do, k_own, k_prev, v_own, v_prev, l_full, d_full):
            lcol = jnp.concatenate([l_full[:, 0:1], l_full[:, C_HEAD_DIM:C_HEAD_DIM + 1]], axis=0)
            dcol = jnp.concatenate([d_full[:, 0:1], d_full[:, C_HEAD_DIM:C_HEAD_DIM + 1]], axis=0)
            return (_stack_heads(q).astype(BF16), _stack_heads(do).astype(BF16),
                    jnp.concatenate([k_own, k_prev], axis=0).astype(BF16),
                    jnp.concatenate([v_own, v_prev], axis=0).astype(BF16), lcol, dcol, _attn_mask(has_prev))

        for dil in C_DILATIONS:
            def pair(i, carry, dil=dil):
                loaded = [load(2 * i + k, dil) for k in range(2)]
                ops = [operands(*o) for _, o in loaded]
                ss = [_dot_nt(q_stack, k_cat) for q_stack, _, k_cat, _, _, _, _ in ops]
                dps = [_dot_nt(do_stack, v_cat) for _, do_stack, _, v_cat, _, _, _ in ops]
                ps = [jnp.exp(jnp.where(o[6], s, NEG_BIG) - o[4]) for s, o in zip(ss, ops)]
                dss = [(p * (dp - o[5])).astype(BF16) for p, dp, o in zip(ps, dps, ops)]
                dvs_ = [_dot_tn(p.astype(BF16), o[1]) for p, o in zip(ps, ops)]
                dks_ = [_dot_tn(ds, o[0]) for ds, o in zip(dss, ops)]
                dqs_ = [_unstack_heads(_dot(ds, o[2])) for ds, o in zip(dss, ops)]
                results = list(zip(dqs_, dks_, dvs_))
                for ((rows, prev), _), (dq, dk_cat, dv_cat) in zip(loaded, results):
                    dqs[rows, :] += dq
                    dks[rows, :] += dk_cat[:C_BLOCK]
                    dvs[rows, :] += dv_cat[:C_BLOCK]
                    dks[prev, :] += dk_cat[C_BLOCK:]
                    dvs[prev, :] += dv_cat[C_BLOCK:]
                return carry

            lax.fori_loop(0, SEQ // C_BLOCK // 2, pair, 0)
        dq_ref[...] = _rope_t(dqs[...] * C_SCALE, c, a, b)
        dk_ref[...] = _rope_t(dks[...], c, a, b)
        dv_ref[...] = dvs[...]

    def col(k):
        return pl.BlockSpec((SEQ, C_PAIR), lambda b, p: (b, k * C_PAIRS + p))

    tab = pl.BlockSpec((SEQ, C_PAIR), lambda b, p: (b, 0))
    out = jax.ShapeDtypeStruct((t, D_MODEL), F32)
    return _call(
        body, name=name, grid=(batch, C_PAIRS),
        in_specs=[col(0), col(1), col(2), tab, tab, tab, col(0), col(0), col(0)],
        out_specs=[col(0), col(0), col(0)],
        out_shape=[out, out, out],
        scratch_shapes=[pltpu.VMEM((SEQ, C_PAIR), F32)] * 6,
        args=(qkv, qkv, qkv, cos_t, sin_a, sin_b, o, lse, do), exchange=exchange)


def sibling_swap(arrays, name):
    n = len(arrays)

    def body(*refs):
        ins, outs = refs[:n], refs[n:2 * n]
        send_sems, recv_sems = refs[2 * n:]
        x, y, c, _ = _place()
        sends = []
        for a in range(n):
            cp = pltpu.make_async_remote_copy(
                src_ref=ins[a], dst_ref=outs[a], send_sem=send_sems.at[a], recv_sem=recv_sems.at[a],
                device_id=(x, y, 1 - c), device_id_type=MESH)
            cp.start()
            sends.append(cp)
        for cp in sends:
            cp.wait_recv()
        for cp in sends:
            cp.wait_send()

    return pl.pallas_call(
        body, name=name,
        in_specs=[ANY] * n, out_specs=[ANY] * n,
        out_shape=[jax.ShapeDtypeStruct(s.shape, s.dtype) for s in arrays],
        scratch_shapes=[pltpu.SemaphoreType.DMA((n,)), pltpu.SemaphoreType.DMA((n,))],
    )(*arrays)


def allreduce_small(slab, name):
    rows, lanes = slab.shape

    def body(x_ref, out_ref, gath, send_sems, recv_sems, local_sem):
        x, y, c, chips = _place()
        me, sibling = (x, y, c), (x, y, 1 - c)

        def slot(px, py, pc):
            return gath.at[4 * px + 2 * py + pc]

        def copy(k, block, to, src=None):
            return pltpu.make_async_remote_copy(
                src_ref=slot(*block) if src is None else src, dst_ref=slot(*block),
                send_sem=send_sems.at[k], recv_sem=recv_sems.at[k], device_id=to, device_id_type=MESH)

        mine = pltpu.make_async_copy(x_ref, slot(*me), local_sem)
        mine.start()
        first = [copy(0, me, sibling, src=x_ref)]
        first += [copy(1 + j, me, (*chip, c), src=x_ref) for j, chip in enumerate(chips)]
        for cp in first:
            cp.start()
        passed = [copy(4 + j, (*chip, c), sibling) for j, chip in enumerate(chips)]
        for j, chip in enumerate(chips):
            copy(1 + j, (*chip, c), me).wait_recv()
            passed[j].start()
        copy(0, sibling, me).wait_recv()
        for j, chip in enumerate(chips):
            copy(4 + j, (*chip, 1 - c), me).wait_recv()
        for cp in first + passed:
            cp.wait_send()
        mine.wait()
        total = gath[0]
        for d in range(1, N_DEV):
            total = total + gath[d]
        out_ref[...] = total

    return pl.pallas_call(
        body, name=name,
        in_specs=[pl.BlockSpec(memory_space=pltpu.VMEM)],
        out_specs=pl.BlockSpec(memory_space=pltpu.VMEM),
        out_shape=jax.ShapeDtypeStruct((rows, lanes), F32),
        scratch_shapes=[pltpu.VMEM((N_DEV, rows, lanes), F32),
                        pltpu.SemaphoreType.DMA((7,)), pltpu.SemaphoreType.DMA((7,)), pltpu.SemaphoreType.DMA],
    )(slab)


ELT_ROWS = 512


def reduce_slabs(r, name, part=0, parts=1, into=None):
    _, rows, cols = r.shape
    br = min(rows, ELT_ROWS)
    nblk = rows // br

    def body(r_ref, *rest):
        o_ref = rest[-1]
        o_ref[...] = ((r_ref[3].astype(F32) + r_ref[0].astype(F32)) + r_ref[1].astype(F32)) + r_ref[2].astype(F32)

    return pl.pallas_call(
        body, name=name, grid=(nblk,),
        in_specs=[pl.BlockSpec((N_CHIPS, br, cols), lambda i: (0, i, 0))] + ([] if into is None else [ANY]),
        out_specs=pl.BlockSpec((br, cols), lambda i: (part * nblk + i, 0)),
        out_shape=jax.ShapeDtypeStruct((parts * rows, cols), F32),
        input_output_aliases={} if into is None else {1: 0},
        compiler_params=_params(("arbitrary",)),
    )(*([r] if into is None else [r, into]))


def _adamw(w, g, m, v):
    m = ADAM_B1 * m + (1.0 - ADAM_B1) * g
    v = ADAM_B2 * v + (1.0 - ADAM_B2) * jnp.square(g)
    m_hat = m / (1.0 - ADAM_B1 ** ADAM_STEP)
    v_hat = v / (1.0 - ADAM_B2 ** ADAM_STEP)
    delta = -ADAM_LR * (m_hat / (jnp.sqrt(v_hat) + ADAM_EPS) + ADAM_WD * w)
    return delta, m, v


def adamw_big(w, s_mine, s_sibling, m, v, name):
    rows, cols = w.shape

    def body(w_ref, a_ref, b_ref, m_ref, v_ref, g_out, d_out, m_out, v_out):
        g = a_ref[...] + b_ref[...]
        g_out[...] = g
        d_out[...], m_out[...], v_out[...] = _adamw(w_ref[...], g, m_ref[...], v_ref[...])

    blk = pl.BlockSpec((min(rows, ELT_ROWS), cols), lambda i: (i, 0))
    out = jax.ShapeDtypeStruct((rows, cols), F32)
    return pl.pallas_call(
        body, name=name, grid=(rows // min(rows, ELT_ROWS),),
        in_specs=[blk] * 5, out_specs=[blk] * 4, out_shape=[out] * 4,
        compiler_params=_params(("arbitrary",)),
    )(w, s_mine, s_sibling, m, v)


def adamw_small(ws, gs, ms, vs, name):
    n = len(ws)

    def body(*refs):
        w_refs, g_refs, m_refs, v_refs = (refs[k * n:(k + 1) * n] for k in range(4))
        d_out, m_out, v_out = (refs[(4 + k) * n:(5 + k) * n] for k in range(3))
        for i in range(n):
            d_out[i][...], m_out[i][...], v_out[i][...] = _adamw(
                w_refs[i][...], g_refs[i][...], m_refs[i][...], v_refs[i][...])

    outs = [jax.ShapeDtypeStruct(w.shape, F32) for w in ws]
    res = pl.pallas_call(body, name=name, out_shape=outs * 3)(*ws, *gs, *ms, *vs)
    return res[:n], res[n:2 * n], res[2 * n:]


SLAB_LANES = 128
SLAB_ROW_ALIGN = 8


def _pack(parts):
    flat = jnp.concatenate([p.reshape(-1) for p in parts])
    rows = -(-flat.shape[0] // (SLAB_LANES * SLAB_ROW_ALIGN)) * SLAB_ROW_ALIGN
    flat = jnp.pad(flat, (0, rows * SLAB_LANES - flat.shape[0]))
    return flat.reshape(rows, SLAB_LANES)


def _unpack(slab, shapes):
    flat = slab.reshape(-1)
    out, pos = [], 0
    for s in shapes:
        size = math.prod(s)
        out.append(flat[pos:pos + size].reshape(s))
        pos += size
    return out


def kernel(x, positions, norm_mix_pre, norm_mix_post, norm_ffn_pre, norm_ffn_post, w_in_even, lb_table, a_norm, b_ln_g, b_ln_b, b_ws, b_bias, w_out_even, w_in_odd, w_out_odd, w_ff1, w_ff2, loss_target, m_norm_mix_pre, m_norm_mix_post, m_norm_ffn_pre, m_norm_ffn_post, m_w_in_even, m_lb_table, m_a_norm, m_b_ln_g, m_b_ln_b, m_b_ws, m_b_bias, m_w_out_even, m_w_in_odd, m_w_out_odd, m_w_ff1, m_w_ff2, v_norm_mix_pre, v_norm_mix_post, v_norm_ffn_pre, v_norm_ffn_post, v_w_in_even, v_lb_table, v_a_norm, v_b_ln_g, v_b_ln_b, v_b_ws, v_b_bias, v_w_out_even, v_w_in_odd, v_w_out_odd, v_w_ff1, v_w_ff2):
    batch = x.shape[0]
    t = batch * SEQ
    d = D_MODEL
    x0 = x.reshape(t, d)
    target = loss_target.reshape(t, d)

    def gain(p, layer):
        return p[layer:layer + 1]

    def gather(*shards):
        return _Exchange("gather", [w.astype(BF16) for w in shards])

    def scatter(*grads):
        return _Exchange("scatter", grads)

    (win_e,) = exchange_alone(gather(w_in_even[0]), "gather_in_even")
    bias_t = b_bias[0].T
    proj, h0, w1_0 = norm_matmul(x0, gain(norm_mix_pre, 0), win_e, "in_proj_even", exchange=gather(w_ff1[0]))
    oa, states, w2_0 = hgrn2_fwd(proj, lb_table, a_norm, batch, "hgrn2_fwd", exchange=gather(w_ff2[0]))
    mixin, wout_e = gmlp_fwd(proj, oa, b_ln_g, b_ln_b, b_ws[0], bias_t, "gmlp_fwd", exchange=gather(w_out_even[0]))
    mix0, x1 = out_proj(mixin, wout_e, x0, gain(norm_mix_post, 0), "out_proj_even")
    x2, hf0, a0, y0, win_o, wout_o = ffn_fwd(x1, gain(norm_ffn_pre, 0), w1_0, w2_0, gain(norm_ffn_post, 0),
                                             "ffn_fwd_0", exchange=gather(w_in_odd[0], w_out_odd[0]))
    qkv, h1 = norm_matmul(x2, gain(norm_mix_pre, 1), win_o, "in_proj_odd")
    cos_t, sin_a, sin_b = rope_tables(positions.reshape(t, 1), "rope_tables")
    ao, lse, w1_1, w2_1 = attn_fwd(qkv, cos_t, sin_a, sin_b, batch, "attn_fwd", exchange=gather(w_ff1[1], w_ff2[1]))
    mix1, x3 = out_proj(ao, wout_o, x2, gain(norm_mix_post, 1), "out_proj_odd")
    x4, hf1, a1, y1 = ffn_fwd(x3, gain(norm_ffn_pre, 1), w1_1, w2_1, gain(norm_ffn_post, 1), "ffn_fwd_1")
    dx4, loss_part = loss_grad(x4, target, "loss_grad")

    hc = D_FF // N_CHIPS
    dx3, dy1, da1, dg_fpre1, dg_fpost1 = ffn_bwd(
        dx4, x3, y1, a1, gain(norm_ffn_pre, 1), gain(norm_ffn_post, 1), w1_1, w2_1, "ffn_bwd_1")
    g_w1_1 = weight_grad(hf1, da1, "b", d, hc, False, "wgrad_ff1_1")
    g_w2_1 = weight_grad(a1, dy1, "a", hc, d, True, "wgrad_ff2_1")
    dmix1, dao, dg_mpost1 = out_proj_bwd(dx3, mix1, gain(norm_mix_post, 1), wout_o, "out_proj_bwd_odd")
    g_wout_o = weight_grad(ao, dmix1, "a", d // N_CHIPS, d, False, "wgrad_out_odd")
    dq, dk, dv, r_w1_1, r_w2_1, r_wout_o = attn_bwd(qkv, cos_t, sin_a, sin_b, ao, lse, dao, batch, "attn_bwd",
                                                    exchange=scatter(g_w1_1, g_w2_1, g_wout_o))
    dqkv = jnp.concatenate([dq, dk, dv], axis=1)
    dx2, dg_mpre1 = norm_matmul_bwd(dqkv, win_o, x2, gain(norm_mix_pre, 1), dx3, "in_proj_bwd_odd")
    g_win_o = weight_grad(h1, dqkv, "b", d, 3 * d // N_CHIPS, False, "wgrad_in_odd")
    dx1, dy0, da0, dg_fpre0, dg_fpost0, r_win_o = ffn_bwd(
        dx2, x1, y0, a0, gain(norm_ffn_pre, 0), gain(norm_ffn_post, 0), w1_0, w2_0, "ffn_bwd_0",
        exchange=scatter(g_win_o))
    g_w1_0 = weight_grad(hf0, da0, "b", d, hc, False, "wgrad_ff1_0")
    g_w2_0 = weight_grad(a0, dy0, "a", hc, d, True, "wgrad_ff2_0")
    dmix0, dmixin, dg_mpost0 = out_proj_bwd(dx1, mix0, gain(norm_mix_post, 0), wout_e, "out_proj_bwd_even")
    g_wout_e = weight_grad(mixin, dmix0, "a", d // N_CHIPS, d, False, "wgrad_out_even")
    dproj, d_lb, d_anorm, r_w1_0 = hgrn2_bwd(
        proj, states, lb_table, a_norm, dmixin, batch, "hgrn2_bwd", exchange=scatter(g_w1_0))
    dproj, d_lng, d_lnb, d_ws, d_bias_t, r_wout_e = gmlp_bwd(
        proj, dmixin, b_ln_g, b_ln_b, b_ws[0], bias_t, dproj, "gmlp_bwd", exchange=scatter(g_wout_e))
    g_win_e, r_w2_0 = weight_grad(h0, dproj, "b", d, 3 * d // N_CHIPS, False, "wgrad_in_even",
                                  exchange=scatter(g_w2_0))
    dx0, dg_mpre0, r_win_e = norm_matmul_bwd(dproj, win_e, x0, gain(norm_mix_pre, 0), dx1, "in_proj_bwd_even",
                                             exchange=scatter(g_win_e))
    grad_x = dx0.reshape(x.shape)

    s_w1 = reduce_slabs(r_w1_1, "reduce_ff1_1", part=1, parts=2)
    s_w1 = reduce_slabs(r_w1_0, "reduce_ff1_0", part=0, parts=2, into=s_w1)
    s_w2 = reduce_slabs(r_w2_1, "reduce_ff2_1", part=1, parts=2)
    s_w2 = reduce_slabs(r_w2_0, "reduce_ff2_0", part=0, parts=2, into=s_w2)
    sums = [reduce_slabs(r_win_e, "reduce_in_even"), reduce_slabs(r_wout_e, "reduce_out_even"),
            reduce_slabs(r_win_o, "reduce_in_odd"), reduce_slabs(r_wout_o, "reduce_out_odd"), s_w1, s_w2]
    sibling = sibling_swap(sums, "sibling_swap")
    big_w = [w_in_even, w_out_even, w_in_odd, w_out_odd, w_ff1, w_ff2]
    big_m = [m_w_in_even, m_w_out_even, m_w_in_odd, m_w_out_odd, m_w_ff1, m_w_ff2]
    big_v = [v_w_in_even, v_w_out_even, v_w_in_odd, v_w_out_odd, v_w_ff1, v_w_ff2]
    big = []
    for i, (w, m, v) in enumerate(zip(big_w, big_m, big_v)):
        two_d = (-1, w.shape[-1])
        res = adamw_big(w.reshape(two_d), sums[i], sibling[i], m.reshape(two_d), v.reshape(two_d), "adamw_big_%d" % i)
        big.append([r.reshape(w.shape) for r in res])

    small_w = [norm_mix_pre, norm_mix_post, norm_ffn_pre, norm_ffn_post, lb_table, a_norm, b_ln_g, b_ln_b, b_ws, b_bias]
    small_m = [m_norm_mix_pre, m_norm_mix_post, m_norm_ffn_pre, m_norm_ffn_post, m_lb_table, m_a_norm, m_b_ln_g,
               m_b_ln_b, m_b_ws, m_b_bias]
    small_v = [v_norm_mix_pre, v_norm_mix_post, v_norm_ffn_pre, v_norm_ffn_post, v_lb_table, v_a_norm, v_b_ln_g,
               v_b_ln_b, v_b_ws, v_b_bias]
    partial = [jnp.concatenate([dg_mpre0, dg_mpre1]), jnp.concatenate([dg_mpost0, dg_mpost1]),
               jnp.concatenate([dg_fpre0, dg_fpre1]), jnp.concatenate([dg_fpost0, dg_fpost1]),
               d_lb, d_anorm, d_lng, d_lnb, d_ws[None], d_bias_t.T[None]]
    *small_g, loss = _unpack(allreduce_small(_pack(partial + [loss_part]), "allreduce_small"),
                             [w.shape for w in small_w] + [()])
    small_d, small_nm, small_nv = adamw_small(small_w, small_g, small_m, small_v, "adamw_small")

    order = ["norm_mix_pre", "norm_mix_post", "norm_ffn_pre", "norm_ffn_post", "w_in_even", "lb_table", "a_norm",
             "b_ln_g", "b_ln_b", "b_ws", "b_bias", "w_out_even", "w_in_odd", "w_out_odd", "w_ff1", "w_ff2"]
    small_names = ["norm_mix_pre", "norm_mix_post", "norm_ffn_pre", "norm_ffn_post", "lb_table", "a_norm",
                   "b_ln_g", "b_ln_b", "b_ws", "b_bias"]
    big_names = ["w_in_even", "w_out_even", "w_in_odd", "w_out_odd", "w_ff1", "w_ff2"]
    grads, deltas, new_m, new_v = {}, {}, {}, {}
    for i, nm in enumerate(small_names):
        grads[nm], deltas[nm], new_m[nm], new_v[nm] = small_g[i], small_d[i], small_nm[i], small_nv[i]
    for i, nm in enumerate(big_names):
        grads[nm], deltas[nm], new_m[nm], new_v[nm] = big[i]
    return (loss, grad_x, *[grads[n] for n in order], *[deltas[n] for n in order],
            *[new_m[n] for n in order], *[new_v[n] for n in order])
```

```python
import functools
import math

import jax
import jax.numpy as jnp
from jax import lax
from jax.experimental import pallas as pl
from jax.experimental.pallas import tpu as pltpu

F32 = jnp.float32
BF16 = jnp.bfloat16
MESH = pl.DeviceIdType.MESH

D_MODEL = 1024
SEQ = 2048
D_FF = 4096
N_CHIPS = 4
A_WIDTH = 512
A_HEADS = 4
A_DK = 128
A_CHUNK = 64
A_SUB = 16
B_WIDTH = 512
B_GROUPS = 4
B_CHUNK = 128
C_HEADS = 16
C_HEAD_DIM = 64
C_ROT_HALF = 8
C_BLOCK = 128
C_DILATIONS = (1, 4, 16)
ROPE_THETA = 500000.0
EPS = 1e-6
ADAM_LR = 0.001
ADAM_B1 = 0.9
ADAM_B2 = 0.999
ADAM_EPS = 1e-08
ADAM_WD = 0.01
ADAM_STEP = 10

ROW_TILE = 512
FFN_ROWS = 1024
WGRAD_ROWS = 2048
VMEM_LIMIT = 56 * 1024 * 1024
NEG_BIG = -1e30


def _params(sem=None):
    return pltpu.CompilerParams(dimension_semantics=sem, vmem_limit_bytes=VMEM_LIMIT)


def _dot(a, b):
    return jnp.dot(a, b, preferred_element_type=F32)


def _dot_nt(a, b):
    return lax.dot_general(a, b, (((1,), (1,)), ((), ())), preferred_element_type=F32)


def _dot_tn(a, b):
    return lax.dot_general(a, b, (((0,), (0,)), ((), ())), preferred_element_type=F32)


def _rms(x, g):
    r = lax.rsqrt(jnp.mean(x * x, axis=-1, keepdims=True) + EPS)
    return x * r * g


def _rms_bwd(x, g, dy):
    r = lax.rsqrt(jnp.mean(x * x, axis=-1, keepdims=True) + EPS)
    xh = x * r
    dg = jnp.sum(dy * xh, axis=0, keepdims=True)
    dxh = dy * g
    dx = r * (dxh - xh * jnp.mean(dxh * xh, axis=-1, keepdims=True))
    return dx, dg


def _accumulate(ref, val, first):
    @pl.when(first)
    def _():
        ref[...] = val

    @pl.when(jnp.logical_not(first))
    def _():
        ref[...] += val


N_DEV = 8
ANY = pl.BlockSpec(memory_space=pl.ANY)


def _place():
    x, y, c = lax.axis_index("x"), lax.axis_index("y"), lax.axis_index("c")
    return x, y, c, [(1 - x, y), (x, 1 - y), (1 - x, 1 - y)]


class _Exchange:
    def __init__(self, kind, arrays):
        self.kind, self.arrays, self.n = kind, list(arrays), len(arrays)
        per_peer = pltpu.SemaphoreType.DMA((3 * self.n,))
        if kind == "gather":
            self.out_shape = [jax.ShapeDtypeStruct((N_CHIPS,) + a.shape, a.dtype) for a in self.arrays]
            self.scratch = [per_peer, per_peer, pltpu.SemaphoreType.DMA((self.n,)), per_peer, per_peer]
        else:
            self.out_shape = [jax.ShapeDtypeStruct(a.shape, a.dtype) for a in self.arrays]
            self.scratch = [per_peer, per_peer, pltpu.SemaphoreType.DMA((self.n,))]

    def _copies(self, ins, outs, sems):
        send_sems, recv_sems, local_sems = sems[:3]
        x, y, c, chips = _place()
        me = 2 * x + y
        local, remote = [], []
        for a in range(self.n):
            if self.kind == "gather":
                local.append(pltpu.make_async_copy(ins[a], outs[a].at[me], local_sems.at[a]))
                half = self.arrays[a].shape[0] // 2

                def rows(ref, core, half=half):
                    return ref.at[pl.ds(core * half, half)]
            else:
                local.append(pltpu.make_async_copy(ins[a].at[me], outs[a].at[3], local_sems.at[a]))
            for j, (px, py) in enumerate(chips):
                k = 3 * a + j
                peer = 2 * px + py

                def copy(src, dst, to, send_sem=send_sems.at[k], recv_sem=recv_sems.at[k]):
                    return pltpu.make_async_remote_copy(src_ref=src, dst_ref=dst, send_sem=send_sem, recv_sem=recv_sem,
                                                        device_id=to, device_id_type=MESH)

                if self.kind == "gather":
                    sent = copy(rows(ins[a], c), rows(outs[a].at[me], c), (px, py, c))
                    landed = copy(rows(ins[a], c), rows(outs[a].at[peer], c), (px, py, c))
                    on = dict(send_sem=sems[3].at[k], recv_sem=sems[4].at[k])
                    passed = copy(rows(outs[a].at[peer], c), rows(outs[a].at[peer], c), (x, y, 1 - c), **on)
                    handed = copy(rows(outs[a].at[peer], c), rows(outs[a].at[peer], 1 - c), (x, y, 1 - c), **on)
                    remote.append((sent, landed, passed, handed))
                else:
                    sent = copy(ins[a].at[peer], outs[a].at[j], (px, py, c))
                    remote.append((sent, sent, None, None))
        return local, remote

    def start(self, ins, outs, sems):
        local, remote = self._copies(ins, outs, sems)
        for cp in local:
            cp.start()
        for sent, _, _, _ in remote:
            sent.start()

    def finish(self, ins, outs, sems):
        local, remote = self._copies(ins, outs, sems)
        for _, landed, passed, _ in remote:
            landed.wait_recv()
            if passed is not None:
                passed.start()
        for sent, _, passed, handed in remote:
            if passed is not None:
                handed.wait_recv()
                passed.wait_send()
            sent.wait_send()
        for cp in local:
            cp.wait()


def _call(body, *, name, grid, in_specs, out_specs, out_shape, args, scratch_shapes=(), aliases=None, exchange=None):
    if exchange is None:
        return pl.pallas_call(
            body, name=name, grid=grid, in_specs=in_specs, out_specs=out_specs, out_shape=out_shape,
            scratch_shapes=list(scratch_shapes), input_output_aliases=aliases or {},
            compiler_params=_params(("arbitrary",) * len(grid)))(*args)
    n_in, n_out, n_scr, n_ex = len(in_specs), len(out_specs), len(scratch_shapes), exchange.n
    steps = grid

    def wrapped(*refs):
        ins, refs = refs[:n_in], refs[n_in:]
        ex_in, refs = refs[:n_ex], refs[n_ex:]
        outs, refs = refs[:n_out], refs[n_out:]
        ex_out, refs = refs[:n_ex], refs[n_ex:]
        scr, sems = refs[:n_scr], refs[n_scr:]
        first = functools.reduce(jnp.logical_and, [pl.program_id(k) == 0 for k in range(len(steps))])
        last = functools.reduce(jnp.logical_and, [pl.program_id(k) == steps[k] - 1 for k in range(len(steps))])

        @pl.when(first)
        def _():
            exchange.start(ex_in, ex_out, sems)

        body(*ins, *outs, *scr)

        @pl.when(last)
        def _():
            exchange.finish(ex_in, ex_out, sems)

    return pl.pallas_call(
        wrapped, name=name, grid=grid,
        in_specs=list(in_specs) + [ANY] * n_ex, out_specs=list(out_specs) + [ANY] * n_ex,
        out_shape=list(out_shape) + exchange.out_shape,
        scratch_shapes=list(scratch_shapes) + exchange.scratch, input_output_aliases=aliases or {},
        compiler_params=_params(("arbitrary",) * len(grid)))(*args, *exchange.arrays)


def exchange_alone(exchange, name):
    def body(*refs):
        n = exchange.n
        exchange.start(refs[:n], refs[n:2 * n], refs[2 * n:])
        exchange.finish(refs[:n], refs[n:2 * n], refs[2 * n:])

    return pl.pallas_call(
        body, name=name, in_specs=[ANY] * exchange.n, out_specs=[ANY] * exchange.n,
        out_shape=exchange.out_shape, scratch_shapes=exchange.scratch)(*exchange.arrays)


def norm_matmul(x, g, wg, name, exchange=None):
    t, d = x.shape
    nl = wg.shape[2]

    def body(x_ref, g_ref, w_ref, o_ref, h_ref):
        h = _rms(x_ref[...], g_ref[...]).astype(BF16)
        h_ref[...] = h
        for c in range(N_CHIPS):
            o_ref[:, c * nl:(c + 1) * nl] = _dot(h, w_ref[c])

    return _call(
        body, name=name, grid=(t // ROW_TILE,),
        in_specs=[pl.BlockSpec((ROW_TILE, d), lambda i: (i, 0)),
                  pl.BlockSpec((1, d), lambda i: (0, 0)),
                  pl.BlockSpec((N_CHIPS, d, nl), lambda i: (0, 0, 0))],
        out_specs=[pl.BlockSpec((ROW_TILE, N_CHIPS * nl), lambda i: (i, 0)),
                   pl.BlockSpec((ROW_TILE, d), lambda i: (i, 0))],
        out_shape=[jax.ShapeDtypeStruct((t, N_CHIPS * nl), F32), jax.ShapeDtypeStruct((t, d), BF16)],
        args=(x, g, wg), exchange=exchange)


def norm_matmul_bwd(dproj, wg, x, g, dres, name, exchange=None):
    t, d = x.shape
    nl = wg.shape[2]

    def body(dp_ref, w_ref, x_ref, g_ref, dres_ref, dx_ref, dg_ref):
        dh = _dot_nt(dp_ref[:, 0:nl].astype(BF16), w_ref[0])
        for c in range(1, N_CHIPS):
            dh += _dot_nt(dp_ref[:, c * nl:(c + 1) * nl].astype(BF16), w_ref[c])
        dx, dg = _rms_bwd(x_ref[...], g_ref[...], dh)
        dx_ref[...] = dres_ref[...] + dx
        _accumulate(dg_ref, dg, pl.program_id(0) == 0)

    row = pl.BlockSpec((ROW_TILE, d), lambda i: (i, 0))
    vec = pl.BlockSpec((1, d), lambda i: (0, 0))
    return _call(
        body, name=name, grid=(t // ROW_TILE,),
        in_specs=[pl.BlockSpec((ROW_TILE, N_CHIPS * nl), lambda i: (i, 0)),
                  pl.BlockSpec((N_CHIPS, d, nl), lambda i: (0, 0, 0)), row, vec, row],
        out_specs=[row, vec],
        out_shape=[jax.ShapeDtypeStruct((t, d), F32), jax.ShapeDtypeStruct((1, d), F32)],
        args=(dproj, wg, x, g, dres), exchange=exchange)


def out_proj(a, wg, x, g, name):
    t, d = x.shape
    kl = wg.shape[1]

    def body(a_ref, w_ref, x_ref, g_ref, mix_ref, xo_ref):
        acc = _dot(a_ref[:, 0:kl], w_ref[0])
        for c in range(1, N_CHIPS):
            acc += _dot(a_ref[:, c * kl:(c + 1) * kl], w_ref[c])
        mix_ref[...] = acc
        xo_ref[...] = x_ref[...] + _rms(acc, g_ref[...])

    row = pl.BlockSpec((ROW_TILE, d), lambda i: (i, 0))
    return pl.pallas_call(
        body, name=name, grid=(t // ROW_TILE,),
        in_specs=[row, pl.BlockSpec((N_CHIPS, kl, d), lambda i: (0, 0, 0)), row,
                  pl.BlockSpec((1, d), lambda i: (0, 0))],
        out_specs=[row, row],
        out_shape=[jax.ShapeDtypeStruct((t, d), F32), jax.ShapeDtypeStruct((t, d), F32)],
        compiler_params=_params(("arbitrary",)),
    )(a, wg, x, g)


def out_proj_bwd(dxo, mix, g, wg, name):
    t, d = mix.shape
    kl = wg.shape[1]

    def body(dxo_ref, mix_ref, g_ref, w_ref, dmix_ref, da_ref, dg_ref):
        dmix, dg = _rms_bwd(mix_ref[...], g_ref[...], dxo_ref[...])
        dmb = dmix.astype(BF16)
        dmix_ref[...] = dmb
        for c in range(N_CHIPS):
            da_ref[:, c * kl:(c + 1) * kl] = _dot_nt(dmb, w_ref[c])
        _accumulate(dg_ref, dg, pl.program_id(0) == 0)

    row = pl.BlockSpec((ROW_TILE, d), lambda i: (i, 0))
    vec = pl.BlockSpec((1, d), lambda i: (0, 0))
    return pl.pallas_call(
        body, name=name, grid=(t // ROW_TILE,),
        in_specs=[row, row, vec, pl.BlockSpec((N_CHIPS, kl, d), lambda i: (0, 0, 0))],
        out_specs=[row, row, vec],
        out_shape=[jax.ShapeDtypeStruct((t, d), BF16), jax.ShapeDtypeStruct((t, d), F32),
                   jax.ShapeDtypeStruct((1, d), F32)],
        compiler_params=_params(("arbitrary",)),
    )(dxo, mix, g, wg)


def ffn_fwd(x, gpre, w1g, w2g, gpost, name, exchange=None):
    t, d = x.shape
    hc = w1g.shape[2]

    def body(x_ref, gpre_ref, w1_ref, w2_ref, gpost_ref, xo_ref, h_ref, a_ref, y_ref, acc):
        c = pl.program_id(1)

        @pl.when(c == 0)
        def _():
            h_ref[...] = _rms(x_ref[...], gpre_ref[...]).astype(BF16)

        a = _dot(h_ref[...], w1_ref[...])
        a_ref[...] = a.astype(BF16)
        r = jnp.square(jnp.maximum(a, 0.0)).astype(BF16)
        _accumulate(acc, _dot(r, w2_ref[...]), c == 0)

        @pl.when(c == N_CHIPS - 1)
        def _():
            y = acc[...]
            y_ref[...] = y
            xo_ref[...] = x_ref[...] + _rms(y, gpost_ref[...])

    row = pl.BlockSpec((FFN_ROWS, d), lambda i, c: (i, 0))
    vec = pl.BlockSpec((1, d), lambda i, c: (0, 0))
    return _call(
        body, name=name, grid=(t // FFN_ROWS, N_CHIPS),
        in_specs=[row, vec,
                  pl.BlockSpec((None, d, hc), lambda i, c: (c, 0, 0)),
                  pl.BlockSpec((None, hc, d), lambda i, c: (c, 0, 0)), vec],
        out_specs=[row, row, pl.BlockSpec((FFN_ROWS, hc), lambda i, c: (i, c)), row],
        out_shape=[jax.ShapeDtypeStruct((t, d), F32), jax.ShapeDtypeStruct((t, d), BF16),
                   jax.ShapeDtypeStruct((t, N_CHIPS * hc), BF16), jax.ShapeDtypeStruct((t, d), F32)],
        scratch_shapes=[pltpu.VMEM((FFN_ROWS, d), F32)],
        args=(x, gpre, w1g, w2g, gpost), exchange=exchange)


def ffn_bwd(dxo, x, y, a, gpre, gpost, w1g, w2g, name, exchange=None):
    t, d = x.shape
    hc = w1g.shape[2]

    def body(dxo_ref, x_ref, y_ref, a_ref, gpre_ref, gpost_ref, w1_ref, w2_ref,
             dxi_ref, dy_ref, da_ref, dgpre_ref, dgpost_ref, acc):
        i, c = pl.program_id(0), pl.program_id(1)

        @pl.when(c == 0)
        def _():
            dy, dg = _rms_bwd(y_ref[...], gpost_ref[...], dxo_ref[...])
            dy_ref[...] = dy.astype(BF16)
            _accumulate(dgpost_ref, dg, i == 0)

        dr = _dot_nt(dy_ref[...], w2_ref[...])
        da = (dr * (2.0 * jnp.maximum(a_ref[...].astype(F32), 0.0))).astype(BF16)
        da_ref[...] = da
        _accumulate(acc, _dot_nt(da, w1_ref[...]), c == 0)

        @pl.when(c == N_CHIPS - 1)
        def _():
            dx, dg = _rms_bwd(x_ref[...], gpre_ref[...], acc[...])
            dxi_ref[...] = dxo_ref[...] + dx
            _accumulate(dgpre_ref, dg, i == 0)

    row = pl.BlockSpec((ROW_TILE, d), lambda i, c: (i, 0))
    vec = pl.BlockSpec((1, d), lambda i, c: (0, 0))
    hid = pl.BlockSpec((ROW_TILE, hc), lambda i, c: (i, c))
    return _call(
        body, name=name, grid=(t // ROW_TILE, N_CHIPS),
        in_specs=[row, row, row, hid, vec, vec,
                  pl.BlockSpec((None, d, hc), lambda i, c: (c, 0, 0)),
                  pl.BlockSpec((None, hc, d), lambda i, c: (c, 0, 0))],
        out_specs=[row, row, hid, vec, vec],
        out_shape=[jax.ShapeDtypeStruct((t, d), F32), jax.ShapeDtypeStruct((t, d), BF16),
                   jax.ShapeDtypeStruct((t, N_CHIPS * hc), BF16),
                   jax.ShapeDtypeStruct((1, d), F32), jax.ShapeDtypeStruct((1, d), F32)],
        scratch_shapes=[pltpu.VMEM((ROW_TILE, d), F32)],
        args=(dxo, x, y, a, gpre, gpost, w1g, w2g), exchange=exchange)


def weight_grad(a, b, chunked, bk, bn, relu2, name, exchange=None):
    t = a.shape[0]
    a_on = chunked == "a"
    rows = min(t, WGRAD_ROWS)
    n_steps = t // rows

    def body(a_ref, b_ref, o_ref, acc):
        s = pl.program_id(1)
        av = a_ref[...]
        if relu2:
            av = jnp.square(jnp.maximum(av.astype(F32), 0.0))
        _accumulate(acc, _dot_tn(av.astype(BF16), b_ref[...].astype(BF16)), s == 0)

        @pl.when(s == n_steps - 1)
        def _():
            o_ref[...] = acc[...].astype(BF16)

    res = _call(
        body, name=name, grid=(N_CHIPS, n_steps),
        in_specs=[pl.BlockSpec((rows, bk), (lambda c, s: (s, c)) if a_on else (lambda c, s: (s, 0))),
                  pl.BlockSpec((rows, bn), (lambda c, s: (s, 0)) if a_on else (lambda c, s: (s, c)))],
        out_specs=[pl.BlockSpec((None, bk, bn), lambda c, s: (c, 0, 0))],
        out_shape=[jax.ShapeDtypeStruct((N_CHIPS, bk, bn), BF16)],
        scratch_shapes=[pltpu.VMEM((bk, bn), F32)],
        args=(a, b), exchange=exchange)
    return res[0] if exchange is None else res


def loss_grad(xf, target, name):
    t, d = xf.shape

    def body(x_ref, t_ref, dy_ref, l_ref):
        e = x_ref[...] - t_ref[...]
        dy_ref[...] = e * (1.0 / d)
        part = jnp.sum(jnp.sum(e * e, axis=-1, keepdims=True), axis=0, keepdims=True) * (0.5 / d)
        _accumulate(l_ref, part, pl.program_id(0) == 0)

    row = pl.BlockSpec((ROW_TILE, d), lambda i: (i, 0))
    return pl.pallas_call(
        body, name=name, grid=(t // ROW_TILE,),
        in_specs=[row, row],
        out_specs=[row, pl.BlockSpec((1, 1), lambda i: (0, 0))],
        out_shape=[jax.ShapeDtypeStruct((t, d), F32), jax.ShapeDtypeStruct((1, 1), F32)],
        compiler_params=_params(("arbitrary",)),
    )(xf, target)


def _hgrn2_chunk(st, qs, fls, ivs, gls, l0, l1, l2, ng):
    nsub = len(qs)
    mx = jnp.maximum(jnp.maximum(l0, l1), l2)
    e0, e1, e2 = jnp.exp(l0 - mx), jnp.exp(l1 - mx), jnp.exp(l2 - mx)
    lb = e0 / (e0 + e1 + e2)
    rows = lax.broadcasted_iota(jnp.int32, (A_SUB, A_SUB), 0)
    cols = lax.broadcasted_iota(jnp.int32, (A_SUB, A_SUB), 1)
    tri = (rows >= cols).astype(F32)
    keep = (lax.broadcasted_iota(jnp.int32, (A_SUB, A_SUB, A_DK), 0)
            >= lax.broadcasted_iota(jnp.int32, (A_SUB, A_SUB, A_DK), 1))
    base = jnp.zeros_like(l0)
    bases, gs, ks, qfs = [], [], [], []
    for i in range(nsub):
        f = lb + (1.0 - lb) * jax.nn.sigmoid(fls[i])
        logf = jnp.log(f)
        bases.append(base)
        gs.append(base + jnp.dot(tri, logf, precision=lax.Precision.HIGHEST, preferred_element_type=F32))
        base = base + jnp.sum(logf, axis=0, keepdims=True)
        ks.append(1.0 - f)
        qfs.append(jax.nn.silu(qs[i]))
    g_last = base
    stb = st.astype(BF16)
    outs = []
    for i in range(nsub):
        o = _dot_nt((qfs[i] * jnp.exp(gs[i])).astype(BF16), stb)
        if i > 0:
            qt = (qfs[i] * jnp.exp(gs[i] - bases[i])).astype(BF16)
            kk = jnp.concatenate([ks[j] * jnp.exp(bases[i] - gs[j]) for j in range(i)], axis=0).astype(BF16)
            vv = jnp.concatenate(ivs[:i], axis=0).astype(BF16)
            o = o + _dot(_dot_nt(qt, kk).astype(BF16), vv)
        dec = jnp.exp(jnp.where(keep, gs[i][:, None, :] - gs[i][None, :, :], NEG_BIG))
        s_diag = jnp.sum(qfs[i][:, None, :] * ks[i][None, :, :] * dec, axis=-1)
        o = o + _dot(s_diag.astype(BF16), ivs[i].astype(BF16))
        o = o * lax.rsqrt(jnp.mean(o * o, axis=-1, keepdims=True) + EPS) * ng
        outs.append(o * jax.nn.silu(gls[i]))
    kdec = jnp.concatenate([ks[j] * jnp.exp(g_last - gs[j]) for j in range(nsub)], axis=0).astype(BF16)
    vall = jnp.concatenate(ivs, axis=0).astype(BF16)
    new_st = st * jnp.exp(g_last) + _dot_tn(vall, kdec)
    return new_st, outs


A_MAX_LOG_DECAY = 80.0


def _split3(x):
    hi = x.astype(BF16)
    r1 = x - hi.astype(F32)
    mid = r1.astype(BF16)
    return hi, mid, (r1 - mid.astype(F32)).astype(BF16)


def _tri_matmul(x, transpose):
    n = x.shape[0]
    r = lax.broadcasted_iota(jnp.int32, (n, n), 0)
    c = lax.broadcasted_iota(jnp.int32, (n, n), 1)
    tri = ((r <= c) if transpose else (r >= c)).astype(BF16)
    hi, mid, lo = _split3(x)
    return (_dot(tri, lo) + _dot(tri, mid)) + _dot(tri, hi)


@jax.custom_vjp
def _cumsum_rows(x):
    return _tri_matmul(x, False)


def _cumsum_rows_fwd(x):
    return _tri_matmul(x, False), None


def _cumsum_rows_bwd(_, dy):
    return (_tri_matmul(dy, True),)


_cumsum_rows.defvjp(_cumsum_rows_fwd, _cumsum_rows_bwd)


def _lower_bound(l0, l1, l2):
    mx = jnp.maximum(jnp.maximum(l0, l1), l2)
    e0, e1, e2 = jnp.exp(l0 - mx), jnp.exp(l1 - mx), jnp.exp(l2 - mx)
    return e0 / (e0 + e1 + e2)


def _b(x):
    return x.astype(BF16)


@jax.custom_vjp
def _mm(a, b):
    return _dot(_b(a), _b(b))


_mm.defvjp(lambda a, b: (_mm(a, b), (a, b)),
           lambda res, d: (_dot_nt(_b(d), _b(res[1])), _dot_tn(_b(res[0]), _b(d))))


@jax.custom_vjp
def _mm_nt(a, b):
    return _dot_nt(_b(a), _b(b))


_mm_nt.defvjp(lambda a, b: (_mm_nt(a, b), (a, b)),
              lambda res, d: (_dot(_b(d), _b(res[1])), _dot_tn(_b(d), _b(res[0]))))


def _dot_split(dot, a, b):
    ah, bh = _b(a), _b(b)
    al, bl = _b(a - ah.astype(F32)), _b(b - bh.astype(F32))
    return (dot(ah, bl) + dot(al, bh)) + dot(ah, bh)


@jax.custom_vjp
def _mm_scores(a, b):
    return _dot_nt(_b(a), _b(b))


_mm_scores.defvjp(lambda a, b: (_mm_scores(a, b), (a, b)),
                  lambda res, d: (_dot_split(_dot, d, res[1]), _dot_split(_dot_tn, d, res[0])))


@jax.custom_vjp
def _mm_tn(a, b):
    return _dot_tn(_b(a), _b(b))


_mm_tn.defvjp(lambda a, b: (_mm_tn(a, b), (a, b)),
              lambda res, d: (_dot_nt(_b(res[1]), _b(d)), _dot(_b(res[0]), _b(d))))


@jax.custom_vjp
def _split_heads(x):
    return tuple(x[:, h * A_DK:(h + 1) * A_DK] for h in range(A_HEADS))


def _split_heads_fwd(x):
    return _split_heads(x), None


def _split_heads_bwd(_, parts):
    return (jnp.concatenate(parts, axis=1),)


_split_heads.defvjp(_split_heads_fwd, _split_heads_bwd)


def _hgrn2_chunk_fast(sts, q, fl, iv, gl, l0, l1, l2, ng):
    lb = _lower_bound(l0, l1, l2)
    f = lb + (1.0 - lb) * jax.nn.sigmoid(fl)
    logf = jnp.log(f)
    g = _cumsum_rows(logf)
    g_last = jnp.sum(logf, axis=0, keepdims=True)
    k = 1.0 - f
    qgs = _split_heads(jax.nn.silu(q) * jnp.exp(g))
    kgs = _split_heads(k * jnp.exp(-g))
    kds = _split_heads(k * jnp.exp(g_last - g))
    ivs = _split_heads(iv)
    decays = _split_heads(jnp.exp(g_last))
    n = q.shape[0]
    causal = lax.broadcasted_iota(jnp.int32, (n, n), 0) >= lax.broadcasted_iota(jnp.int32, (n, n), 1)
    raw = [_mm_scores(qg, kg) for qg, kg in zip(qgs, kgs)]
    inter = [_mm_nt(qg, st) for qg, st in zip(qgs, sts)]
    scores = [jnp.where(causal, s, 0.0) for s in raw]
    os = [a + _mm(s, v) for a, s, v in zip(inter, scores, ivs)]
    new_sts = [st * d + _mm_tn(v, kd) for st, d, v, kd in zip(sts, decays, ivs, kds)]
    os = [o * lax.rsqrt(jnp.mean(o * o, axis=-1, keepdims=True) + EPS) for o in os]
    return new_sts, jnp.concatenate(os, axis=1) * ng * jax.nn.silu(gl)


def _chunk_decays_mildly(f_ref, lb_ref):
    lb = _lower_bound(lb_ref[0:1, :], lb_ref[1:2, :], lb_ref[2:3, :])
    logf = jnp.log(lb + (1.0 - lb) * jax.nn.sigmoid(f_ref[...]))
    return jnp.min(jnp.sum(logf, axis=0, keepdims=True)) >= -A_MAX_LOG_DECAY


def _sub_blocks(ref, head):
    lanes = slice(head * A_DK, (head + 1) * A_DK)
    return [ref[i * A_SUB:(i + 1) * A_SUB, lanes] for i in range(A_CHUNK // A_SUB)]


def hgrn2_fwd(proj, lb_table, a_norm, batch, name, exchange=None):
    t = proj.shape[0]
    n_chunks = t // batch // A_CHUNK
    nblk = A_WIDTH // A_DK

    def body(q_ref, f_ref, i_ref, g_ref, lb_ref, ng_ref, o_ref, st_ref, st):
        @pl.when(pl.program_id(1) == 0)
        def _():
            st[...] = jnp.zeros_like(st)

        st_ref[...] = st[...]
        mild = _chunk_decays_mildly(f_ref, lb_ref)

        @pl.when(mild)
        def _():
            new_sts, o = _hgrn2_chunk_fast(
                [st[h] for h in range(A_HEADS)], q_ref[...], f_ref[...], i_ref[...], g_ref[...],
                lb_ref[0:1, :], lb_ref[1:2, :], lb_ref[2:3, :], ng_ref[...])
            for h in range(A_HEADS):
                st[h] = new_sts[h]
            o_ref[...] = o.astype(BF16)

        @pl.when(jnp.logical_not(mild))
        def _():
            for h in range(A_HEADS):
                lanes = slice(h * A_DK, (h + 1) * A_DK)
                new_st, outs = _hgrn2_chunk(
                    st[h], _sub_blocks(q_ref, h), _sub_blocks(f_ref, h), _sub_blocks(i_ref, h),
                    _sub_blocks(g_ref, h), lb_ref[0:1, lanes], lb_ref[1:2, lanes], lb_ref[2:3, lanes],
                    ng_ref[:, lanes])
                st[h] = new_st
                for i, o in enumerate(outs):
                    o_ref[i * A_SUB:(i + 1) * A_SUB, lanes] = o.astype(BF16)

    def part(k):
        return pl.BlockSpec((A_CHUNK, A_WIDTH), lambda b, n: (b * n_chunks + n, k))

    return _call(
        body, name=name, grid=(batch, n_chunks),
        in_specs=[part(0), part(1), part(2), part(3),
                  pl.BlockSpec((3, A_WIDTH), lambda b, n: (0, 0)), pl.BlockSpec((1, A_WIDTH), lambda b, n: (0, 0))],
        out_specs=[pl.BlockSpec((A_CHUNK, A_WIDTH), lambda b, n: (b * n_chunks + n, 0)),
                   pl.BlockSpec((None, A_HEADS, A_DK, A_DK), lambda b, n: (b * n_chunks + n, 0, 0, 0))],
        out_shape=[jax.ShapeDtypeStruct((t, A_WIDTH), BF16),
                   jax.ShapeDtypeStruct((t // A_CHUNK, A_HEADS, A_DK, A_DK), F32)],
        scratch_shapes=[pltpu.VMEM((A_HEADS, A_DK, A_DK), F32)],
        args=(proj, proj, proj, proj, lb_table, a_norm), exchange=exchange)


def hgrn2_bwd(proj, states, lb_table, a_norm, do, batch, name, exchange=None):
    t = proj.shape[0]
    n_chunks = t // batch // A_CHUNK

    def body(q_ref, f_ref, i_ref, g_ref, st_ref, lb_ref, ng_ref, do_ref, dp_ref, dlb_ref, dng_ref, dst):
        @pl.when(jnp.logical_and(pl.program_id(0) == 0, pl.program_id(1) == 0))
        def _():
            dlb_ref[...] = jnp.zeros_like(dlb_ref)
            dng_ref[...] = jnp.zeros_like(dng_ref)

        @pl.when(pl.program_id(1) == 0)
        def _():
            dst[...] = jnp.zeros_like(dst)

        mild = _chunk_decays_mildly(f_ref, lb_ref)

        @pl.when(mild)
        def _():
            _, vjp = jax.vjp(
                _hgrn2_chunk_fast, [st_ref[h] for h in range(A_HEADS)], q_ref[...], f_ref[...], i_ref[...],
                g_ref[...], lb_ref[0:1, :], lb_ref[1:2, :], lb_ref[2:3, :], ng_ref[...])
            d_sts, dq, df, di, dg, dl0, dl1, dl2, dng = vjp(([dst[h] for h in range(A_HEADS)], do_ref[...].astype(F32)))
            for h in range(A_HEADS):
                dst[h] = d_sts[h]
            for k, part in enumerate((dq, df, di, dg)):
                dp_ref[:, k * A_WIDTH:(k + 1) * A_WIDTH] = part
            for row, val in enumerate((dl0, dl1, dl2)):
                dlb_ref[row:row + 1, :] += val
            dng_ref[...] += dng

        @pl.when(jnp.logical_not(mild))
        def _():
            for h in range(A_HEADS):
                lanes = slice(h * A_DK, (h + 1) * A_DK)
                _, vjp = jax.vjp(
                    _hgrn2_chunk, st_ref[h], _sub_blocks(q_ref, h), _sub_blocks(f_ref, h), _sub_blocks(i_ref, h),
                    _sub_blocks(g_ref, h), lb_ref[0:1, lanes], lb_ref[1:2, lanes], lb_ref[2:3, lanes],
                    ng_ref[:, lanes])
                douts = [x.astype(F32) for x in _sub_blocks(do_ref, h)]
                d_st, dqs, dfs, dis, dgs, dl0, dl1, dl2, dng = vjp((dst[h], douts))
                dst[h] = d_st
                for k, parts in enumerate((dqs, dfs, dis, dgs)):
                    for i in range(A_CHUNK // A_SUB):
                        dp_ref[i * A_SUB:(i + 1) * A_SUB,
                               k * A_WIDTH + h * A_DK:k * A_WIDTH + (h + 1) * A_DK] = parts[i]
                for row, val in enumerate((dl0, dl1, dl2)):
                    dlb_ref[row:row + 1, lanes] += val
                dng_ref[:, lanes] += dng

    def rev(b, n):
        return b * n_chunks + (n_chunks - 1 - n)

    def part(k):
        return pl.BlockSpec((A_CHUNK, A_WIDTH), lambda b, n: (rev(b, n), k))

    const3 = pl.BlockSpec((3, A_WIDTH), lambda b, n: (0, 0))
    const1 = pl.BlockSpec((1, A_WIDTH), lambda b, n: (0, 0))
    return _call(
        body, name=name, grid=(batch, n_chunks),
        in_specs=[part(0), part(1), part(2), part(3),
                  pl.BlockSpec((None, A_HEADS, A_DK, A_DK), lambda b, n: (rev(b, n), 0, 0, 0)),
                  const3, const1, part(0)],
        out_specs=[pl.BlockSpec((A_CHUNK, 4 * A_WIDTH), lambda b, n: (rev(b, n), 0)), const3, const1],
        out_shape=[jax.ShapeDtypeStruct((t, 4 * A_WIDTH + 2 * B_WIDTH), F32),
                   jax.ShapeDtypeStruct((3, A_WIDTH), F32), jax.ShapeDtypeStruct((1, A_WIDTH), F32)],
        scratch_shapes=[pltpu.VMEM((A_HEADS, A_DK, A_DK), F32)],
        args=(proj, proj, proj, proj, states, lb_table, a_norm, do), exchange=exchange)


B_GDIM = B_WIDTH // B_GROUPS
B_ROWS = 512


def _gmlp_chunk(ubs, vbs, lngs, lnbs, ws, bcols):
    vs = [jax.nn.gelu(v) for v in vbs]
    mu = sum(jnp.sum(v, axis=-1, keepdims=True) for v in vs) * (1.0 / B_WIDTH)
    var = sum(jnp.sum(jnp.square(v - mu), axis=-1, keepdims=True) for v in vs) * (1.0 / B_WIDTH)
    rstd = lax.rsqrt(var + EPS)
    tril = (lax.broadcasted_iota(jnp.int32, (B_CHUNK, B_CHUNK), 0)
            >= lax.broadcasted_iota(jnp.int32, (B_CHUNK, B_CHUNK), 1))
    outs = []
    for g in range(B_GROUPS):
        vn = (vs[g] - mu) * rstd * lngs[g] + lnbs[g]
        w = jnp.where(tril, ws[g], 0.0).astype(BF16)
        outs.append(jax.nn.gelu(ubs[g]) * (_dot(w, vn.astype(BF16)) + bcols[g]))
    return outs


def _gmlp_args(u_ref, v_ref, lng_ref, lnb_ref, w_ref, bt_ref, rows):
    def groups(ref):
        return [ref[rows, g * B_GDIM:(g + 1) * B_GDIM] for g in range(B_GROUPS)]

    def vec(ref):
        return [ref[:, g * B_GDIM:(g + 1) * B_GDIM] for g in range(B_GROUPS)]

    return (groups(u_ref), groups(v_ref), vec(lng_ref), vec(lnb_ref),
            [w_ref[g] for g in range(B_GROUPS)], [bt_ref[:, g:g + 1] for g in range(B_GROUPS)])


def gmlp_fwd(proj, oa, ln_g, ln_b, w, bias_t, name, exchange=None):
    t = proj.shape[0]

    def body(u_ref, v_ref, oa_ref, lng_ref, lnb_ref, w_ref, bt_ref, o_ref):
        o_ref[:, 0:A_WIDTH] = oa_ref[...]
        for n in range(B_ROWS // B_CHUNK):
            rows = slice(n * B_CHUNK, (n + 1) * B_CHUNK)
            outs = _gmlp_chunk(*_gmlp_args(u_ref, v_ref, lng_ref, lnb_ref, w_ref, bt_ref, rows))
            for g, o in enumerate(outs):
                o_ref[rows, A_WIDTH + g * B_GDIM:A_WIDTH + (g + 1) * B_GDIM] = o.astype(BF16)

    vec = pl.BlockSpec((1, B_WIDTH), lambda i: (0, 0))
    return _call(
        body, name=name, grid=(t // B_ROWS,),
        in_specs=[pl.BlockSpec((B_ROWS, B_WIDTH), lambda i: (i, 4)), pl.BlockSpec((B_ROWS, B_WIDTH), lambda i: (i, 5)),
                  pl.BlockSpec((B_ROWS, A_WIDTH), lambda i: (i, 0)), vec, vec,
                  pl.BlockSpec((B_GROUPS, B_CHUNK, B_CHUNK), lambda i: (0, 0, 0)),
                  pl.BlockSpec((B_CHUNK, B_GROUPS), lambda i: (0, 0))],
        out_specs=[pl.BlockSpec((B_ROWS, A_WIDTH + B_WIDTH), lambda i: (i, 0))],
        out_shape=[jax.ShapeDtypeStruct((t, A_WIDTH + B_WIDTH), BF16)],
        args=(proj, proj, oa, ln_g, ln_b, w, bias_t), exchange=exchange)


def gmlp_bwd(proj, dmixin, ln_g, ln_b, w, bias_t, dproj, name, exchange=None):
    t = proj.shape[0]

    def body(u_ref, v_ref, do_ref, lng_ref, lnb_ref, w_ref, bt_ref, dp_in_ref,
             dp_ref, dlng_ref, dlnb_ref, dw_ref, dbt_ref):
        del dp_in_ref

        @pl.when(pl.program_id(0) == 0)
        def _():
            for ref in (dlng_ref, dlnb_ref, dw_ref, dbt_ref):
                ref[...] = jnp.zeros_like(ref)

        for n in range(B_ROWS // B_CHUNK):
            rows = slice(n * B_CHUNK, (n + 1) * B_CHUNK)
            _, vjp = jax.vjp(_gmlp_chunk, *_gmlp_args(u_ref, v_ref, lng_ref, lnb_ref, w_ref, bt_ref, rows))
            douts = [do_ref[rows, g * B_GDIM:(g + 1) * B_GDIM] for g in range(B_GROUPS)]
            dus, dvs, dlngs, dlnbs, dws, dbs = vjp(douts)
            for g in range(B_GROUPS):
                lanes = slice(g * B_GDIM, (g + 1) * B_GDIM)
                dp_ref[rows, lanes] = dus[g]
                dp_ref[rows, B_WIDTH + g * B_GDIM:B_WIDTH + (g + 1) * B_GDIM] = dvs[g]
                dlng_ref[:, lanes] += dlngs[g]
                dlnb_ref[:, lanes] += dlnbs[g]
                dw_ref[g] += dws[g]
                dbt_ref[:, g:g + 1] += dbs[g]

    vec = pl.BlockSpec((1, B_WIDTH), lambda i: (0, 0))
    wspec = pl.BlockSpec((B_GROUPS, B_CHUNK, B_CHUNK), lambda i: (0, 0, 0))
    bspec = pl.BlockSpec((B_CHUNK, B_GROUPS), lambda i: (0, 0))
    return _call(
        body, name=name, grid=(t // B_ROWS,),
        in_specs=[pl.BlockSpec((B_ROWS, B_WIDTH), lambda i: (i, 4)), pl.BlockSpec((B_ROWS, B_WIDTH), lambda i: (i, 5)),
                  pl.BlockSpec((B_ROWS, B_WIDTH), lambda i: (i, 1)), vec, vec, wspec, bspec,
                  pl.BlockSpec(memory_space=pl.ANY)],
        out_specs=[pl.BlockSpec((B_ROWS, 2 * B_WIDTH), lambda i: (i, 2)), vec, vec, wspec, bspec],
        out_shape=[jax.ShapeDtypeStruct(dproj.shape, F32), jax.ShapeDtypeStruct((1, B_WIDTH), F32),
                   jax.ShapeDtypeStruct((1, B_WIDTH), F32), jax.ShapeDtypeStruct((B_GROUPS, B_CHUNK, B_CHUNK), F32),
                   jax.ShapeDtypeStruct((B_CHUNK, B_GROUPS), F32)],
        aliases={7: 0}, args=(proj, proj, dmixin, ln_g, ln_b, w, bias_t, dproj), exchange=exchange)


C_PAIR = 2 * C_HEAD_DIM
C_PAIRS = C_HEADS // 2
C_SCALE = 1.0 / math.sqrt(C_HEAD_DIM)
C_ROT_DIM = 2 * C_ROT_HALF
ROPE_ROWS = 1024


def rope_tables(pos_col, name):
    t = pos_col.shape[0]

    def body(p_ref, c_ref, a_ref, b_ref):
        lane = jnp.bitwise_and(lax.broadcasted_iota(jnp.int32, (1, C_PAIR), 1), C_HEAD_DIM - 1)
        j = jnp.bitwise_and(lane, C_ROT_HALF - 1).astype(F32)
        inv = jnp.exp(j * (-math.log(ROPE_THETA) / C_ROT_HALF))
        ang = p_ref[...].astype(F32) * inv
        cos, sin = jnp.cos(ang), jnp.sin(ang)
        c_ref[...] = jnp.where(lane < C_ROT_DIM, cos, 1.0)
        a_ref[...] = jnp.where(lane < C_ROT_HALF, -sin, 0.0)
        b_ref[...] = jnp.where(jnp.logical_and(lane >= C_ROT_HALF, lane < C_ROT_DIM), sin, 0.0)

    tab = pl.BlockSpec((ROPE_ROWS, C_PAIR), lambda i: (i, 0))
    return pl.pallas_call(
        body, name=name, grid=(t // ROPE_ROWS,),
        in_specs=[pl.BlockSpec((ROPE_ROWS, 1), lambda i: (i, 0))],
        out_specs=[tab, tab, tab],
        out_shape=[jax.ShapeDtypeStruct((t, C_PAIR), F32)] * 3,
        compiler_params=_params(("arbitrary",)),
    )(pos_col)


def _rope(x, c, a, b):
    return x * c + pltpu.roll(x, C_PAIR - C_ROT_HALF, 1) * a + pltpu.roll(x, C_ROT_HALF, 1) * b


def _rope_t(d, c, a, b):
    return d * c + pltpu.roll(d * a, C_ROT_HALF, 1) + pltpu.roll(d * b, C_PAIR - C_ROT_HALF, 1)


def _attn_rows(idx, dil):
    nblk = SEQ // dil // C_BLOCK
    r, n = idx // nblk, idx % nblk
    start = r + dil * C_BLOCK * n
    prev = r + dil * C_BLOCK * jnp.maximum(n - 1, 0)
    if dil == 1:
        return pl.ds(pl.multiple_of(start, C_BLOCK), C_BLOCK), pl.ds(pl.multiple_of(prev, C_BLOCK), C_BLOCK), n > 0
    return pl.ds(start, C_BLOCK, stride=dil), pl.ds(prev, C_BLOCK, stride=dil), n > 0


def _head_masks():
    low = lax.broadcasted_iota(jnp.int32, (1, C_PAIR), 1) < C_HEAD_DIM
    return low, jnp.logical_not(low)


def _attn_mask(has_prev):
    i = jnp.bitwise_and(lax.broadcasted_iota(jnp.int32, (2 * C_BLOCK, 2 * C_BLOCK), 0), C_BLOCK - 1)
    j = lax.broadcasted_iota(jnp.int32, (2 * C_BLOCK, 2 * C_BLOCK), 1)
    return jnp.logical_or(j <= i, jnp.logical_and(j - C_BLOCK >= i, has_prev))


def _stack_heads(x):
    low, high = _head_masks()
    return jnp.concatenate([jnp.where(low, x, 0.0), jnp.where(high, x, 0.0)], axis=0)


def _unstack_heads(x):
    low, _ = _head_masks()
    return jnp.where(low, x[:C_BLOCK], x[C_BLOCK:])


def attn_fwd(qkv, cos_t, sin_a, sin_b, batch, name, exchange=None):
    t = qkv.shape[0]
    nbr = len(C_DILATIONS)

    def body(q_ref, k_ref, v_ref, c_ref, a_ref, b_ref, o_ref, l_ref, qs, ks, *stats):
        acc, mm, dd = stats[0:nbr], stats[nbr:2 * nbr], stats[2 * nbr:3 * nbr]
        c, a, b = c_ref[...], a_ref[...], b_ref[...]
        qs[...] = _rope(q_ref[...], c, a, b) * C_SCALE
        ks[...] = _rope(k_ref[...], c, a, b)
        def load(idx, dil):
            rows, prev, has_prev = _attn_rows(idx, dil)
            return rows, (has_prev, qs[rows, :], ks[rows, :], ks[prev, :], v_ref[rows, :], v_ref[prev, :])

        def scores(has_prev, q, k_own, k_prev, v_own, v_prev):
            k_cat = jnp.concatenate([k_own, k_prev], axis=0).astype(BF16)
            return jnp.where(_attn_mask(has_prev), _dot_nt(_stack_heads(q).astype(BF16), k_cat), NEG_BIG)

        def softmax(s):
            m = jnp.max(s, axis=-1, keepdims=True)
            p = jnp.exp(s - m)
            return p.astype(BF16), m, jnp.sum(p, axis=-1, keepdims=True)

        def values(pb, has_prev, q, k_own, k_prev, v_own, v_prev):
            low, high = _head_masks()
            v_cat = jnp.concatenate([v_own, v_prev], axis=0)
            p_wide = jnp.concatenate([pb[:C_BLOCK], pb[C_BLOCK:]], axis=1)
            v_tall = jnp.concatenate([jnp.where(low, v_cat, 0.0), jnp.where(high, v_cat, 0.0)], axis=0).astype(BF16)
            return _dot(p_wide, v_tall)

        for bi, dil in enumerate(C_DILATIONS):
            def pair(i, carry, bi=bi, dil=dil):
                low, _ = _head_masks()
                loaded = [load(2 * i + k, dil) for k in range(2)]
                ss = [scores(*ops) for _, ops in loaded]
                sm = [softmax(s) for s in ss]
                pvs = [values(pb, *ops) for (pb, _, _), (_, ops) in zip(sm, loaded)]
                for (rows, _), (_, m, den), pv in zip(loaded, sm, pvs):
                    acc[bi][rows, :] = pv
                    mm[bi][rows, :] = jnp.where(low, m[:C_BLOCK], m[C_BLOCK:])
                    dd[bi][rows, :] = jnp.where(low, den[:C_BLOCK], den[C_BLOCK:])
                return carry

            lax.fori_loop(0, SEQ // C_BLOCK // 2, pair, 0)
        step = 256
        for r0 in range(0, SEQ, step):
            rr = slice(r0, r0 + step)
            ms = [mm[g][rr, :] for g in range(nbr)]
            m_all = functools.reduce(jnp.maximum, ms)
            ws = [jnp.exp(m - m_all) for m in ms]
            num = sum(acc[g][rr, :] * ws[g] for g in range(nbr))
            den = sum(dd[g][rr, :] * ws[g] for g in range(nbr))
            o_ref[rr, :] = (num / den).astype(BF16)
            l_ref[rr, :] = m_all + jnp.log(den)

    def col(k):
        return pl.BlockSpec((SEQ, C_PAIR), lambda b, p: (b, k * C_PAIRS + p))

    tab = pl.BlockSpec((SEQ, C_PAIR), lambda b, p: (b, 0))
    return _call(
        body, name=name, grid=(batch, C_PAIRS),
        in_specs=[col(0), col(1), col(2), tab, tab, tab],
        out_specs=[col(0), col(0)],
        out_shape=[jax.ShapeDtypeStruct((t, D_MODEL), BF16), jax.ShapeDtypeStruct((t, D_MODEL), F32)],
        scratch_shapes=[pltpu.VMEM((SEQ, C_PAIR), F32)] * (2 + 3 * nbr),
        args=(qkv, qkv, qkv, cos_t, sin_a, sin_b), exchange=exchange)


def attn_bwd(qkv, cos_t, sin_a, sin_b, o, lse, do, batch, name, exchange=None):
    t = qkv.shape[0]

    def body(q_ref, k_ref, v_ref, c_ref, a_ref, b_ref, o_ref, l_ref, do_ref, dq_ref, dk_ref, dv_ref,
             qs, ks, dqs, dks, dvs, dlt):
        c, a, b = c_ref[...], a_ref[...], b_ref[...]
        qs[...] = _rope(q_ref[...], c, a, b) * C_SCALE
        ks[...] = _rope(k_ref[...], c, a, b)
        prod = do_ref[...] * o_ref[...].astype(F32)
        low = lax.broadcasted_iota(jnp.int32, (1, C_PAIR), 1) < C_HEAD_DIM
        s_low = jnp.sum(jnp.where(low, prod, 0.0), axis=-1, keepdims=True)
        s_all = jnp.sum(prod, axis=-1, keepdims=True)
        dlt[...] = jnp.where(low, s_low, s_all - s_low)
        dqs[...] = jnp.zeros_like(dqs)
        dks[...] = jnp.zeros_like(dks)
        dvs[...] = jnp.zeros_like(dvs)
        def load(idx, dil):
            rows, prev, has_prev = _attn_rows(idx, dil)
            return (rows, prev), (has_prev, qs[rows, :], do_ref[rows, :], ks[rows, :], ks[prev, :],
                                  v_ref[rows, :], v_ref[prev, :], l_ref[rows, :], dlt[rows, :])

        def operands(has_prev, q, do, k_own, k_prev, v_own, v_prev, l_full, d_full):
            lcol = jnp.concatenate([l_full[:, 0:1], l_full[:, C_HEAD_DIM:C_HEAD_DIM + 1]], axis=0)
            dcol = jnp.concatenate([d_full[:, 0:1], d_full[:, C_HEAD_DIM:C_HEAD_DIM + 1]], axis=0)
            return (_stack_heads(q).astype(BF16), _stack_heads(do).astype(BF16),
                    jnp.concatenate([k_own, k_prev], axis=0).astype(BF16),
                    jnp.concatenate([v_own, v_prev], axis=0).astype(BF16), lcol, dcol, _attn_mask(has_prev))

        for dil in C_DILATIONS:
            def pair(i, carry, dil=dil):
                loaded = [load(2 * i + k, dil) for k in range(2)]
                ops = [operands(*o) for _, o in loaded]
                ss = [_dot_nt(q_stack, k_cat) for q_stack, _, k_cat, _, _, _, _ in ops]
                dps = [_dot_nt(do_stack, v_cat) for _, do_stack, _, v_cat, _, _, _ in ops]
                ps = [jnp.exp(jnp.where(o[6], s, NEG_BIG) - o[4]) for s, o in zip(ss, ops)]
                dss = [(p * (dp - o[5])).astype(BF16) for p, dp, o in zip(ps, dps, ops)]
                dvs_ = [_dot_tn(p.astype(BF16), o[1]) for p, o in zip(ps, ops)]
                dks_ = [_dot_tn(ds, o[0]) for ds, o in zip(dss, ops)]
                dqs_ = [_unstack_heads(_dot(ds, o[2])) for ds, o in zip(dss, ops)]
                results = list(zip(dqs_, dks_, dvs_))
                for ((rows, prev), _), (dq, dk_cat, dv_cat) in zip(loaded, results):
                    dqs[rows, :] += dq
                    dks[rows, :] += dk_cat[:C_BLOCK]
                    dvs[rows, :] += dv_cat[:C_BLOCK]
                    dks[prev, :] += dk_cat[C_BLOCK:]
                    dvs[prev, :] += dv_cat[C_BLOCK:]
                return carry

            lax.fori_loop(0, SEQ // C_BLOCK // 2, pair, 0)
        dq_ref[...] = _rope_t(dqs[...] * C_SCALE, c, a, b)
        dk_ref[...] = _rope_t(dks[...], c, a, b)
        dv_ref[...] = dvs[...]

    def col(k):
        return pl.BlockSpec((SEQ, C_PAIR), lambda b, p: (b, k * C_PAIRS + p))

    tab = pl.BlockSpec((SEQ, C_PAIR), lambda b, p: (b, 0))
    out = jax.ShapeDtypeStruct((t, D_MODEL), F32)
    return _call(
        body, name=name, grid=(batch, C_PAIRS),
        in_specs=[col(0), col(1), col(2), tab, tab, tab, col(0), col(0), col(0)],
        out_specs=[col(0), col(0), col(0)],
        out_shape=[out, out, out],
        scratch_shapes=[pltpu.VMEM((SEQ, C_PAIR), F32)] * 6,
        args=(qkv, qkv, qkv, cos_t, sin_a, sin_b, o, lse, do), exchange=exchange)


def sibling_swap(arrays, name):
    n = len(arrays)

    def body(*refs):
        ins, outs = refs[:n], refs[n:2 * n]
        send_sems, recv_sems = refs[2 * n:]
        x, y, c, _ = _place()
        sends = []
        for a in range(n):
            cp = pltpu.make_async_remote_copy(
                src_ref=ins[a], dst_ref=outs[a], send_sem=send_sems.at[a], recv_sem=recv_sems.at[a],
                device_id=(x, y, 1 - c), device_id_type=MESH)
            cp.start()
            sends.append(cp)
        for cp in sends:
            cp.wait_recv()
        for cp in sends:
            cp.wait_send()

    return pl.pallas_call(
        body, name=name,
        in_specs=[ANY] * n, out_specs=[ANY] * n,
        out_shape=[jax.ShapeDtypeStruct(s.shape, s.dtype) for s in arrays],
        scratch_shapes=[pltpu.SemaphoreType.DMA((n,)), pltpu.SemaphoreType.DMA((n,))],
    )(*arrays)


def allreduce_small(slab, name):
    rows, lanes = slab.shape

    def body(x_ref, out_ref, gath, send_sems, recv_sems, local_sem):
        x, y, c, chips = _place()
        me, sibling = (x, y, c), (x, y, 1 - c)

        def slot(px, py, pc):
            return gath.at[4 * px + 2 * py + pc]

        def copy(k, block, to, src=None):
            return pltpu.make_async_remote_copy(
                src_ref=slot(*block) if src is None else src, dst_ref=slot(*block),
                send_sem=send_sems.at[k], recv_sem=recv_sems.at[k], device_id=to, device_id_type=MESH)

        mine = pltpu.make_async_copy(x_ref, slot(*me), local_sem)
        mine.start()
        first = [copy(0, me, sibling, src=x_ref)]
        first += [copy(1 + j, me, (*chip, c), src=x_ref) for j, chip in enumerate(chips)]
        for cp in first:
            cp.start()
        passed = [copy(4 + j, (*chip, c), sibling) for j, chip in enumerate(chips)]
        for j, chip in enumerate(chips):
            copy(1 + j, (*chip, c), me).wait_recv()
            passed[j].start()
        copy(0, sibling, me).wait_recv()
        for j, chip in enumerate(chips):
            copy(4 + j, (*chip, 1 - c), me).wait_recv()
        for cp in first + passed:
            cp.wait_send()
        mine.wait()
        total = gath[0]
        for d in range(1, N_DEV):
            total = total + gath[d]
        out_ref[...] = total

    return pl.pallas_call(
        body, name=name,
        in_specs=[pl.BlockSpec(memory_space=pltpu.VMEM)],
        out_specs=pl.BlockSpec(memory_space=pltpu.VMEM),
        out_shape=jax.ShapeDtypeStruct((rows, lanes), F32),
        scratch_shapes=[pltpu.VMEM((N_DEV, rows, lanes), F32),
                        pltpu.SemaphoreType.DMA((7,)), pltpu.SemaphoreType.DMA((7,)), pltpu.SemaphoreType.DMA],
    )(slab)


ELT_ROWS = 512


def reduce_slabs(r, name, part=0, parts=1, into=None):
    _, rows, cols = r.shape
    br = min(rows, ELT_ROWS)
    nblk = rows // br

    def body(r_ref, *rest):
        o_ref = rest[-1]
        o_ref[...] = ((r_ref[3].astype(F32) + r_ref[0].astype(F32)) + r_ref[1].astype(F32)) + r_ref[2].astype(F32)

    return pl.pallas_call(
        body, name=name, grid=(nblk,),
        in_specs=[pl.BlockSpec((N_CHIPS, br, cols), lambda i: (0, i, 0))] + ([] if into is None else [ANY]),
        out_specs=pl.BlockSpec((br, cols), lambda i: (part * nblk + i, 0)),
        out_shape=jax.ShapeDtypeStruct((parts * rows, cols), F32),
        input_output_aliases={} if into is None else {1: 0},
        compiler_params=_params(("arbitrary",)),
    )(*([r] if into is None else [r, into]))


def _adamw(w, g, m, v):
    m = ADAM_B1 * m + (1.0 - ADAM_B1) * g
    v = ADAM_B2 * v + (1.0 - ADAM_B2) * jnp.square(g)
    m_hat = m / (1.0 - ADAM_B1 ** ADAM_STEP)
    v_hat = v / (1.0 - ADAM_B2 ** ADAM_STEP)
    delta = -ADAM_LR * (m_hat / (jnp.sqrt(v_hat) + ADAM_EPS) + ADAM_WD * w)
    return delta, m, v


def adamw_big(w, s_mine, s_sibling, m, v, name):
    rows, cols = w.shape

    def body(w_ref, a_ref, b_ref, m_ref, v_ref, g_out, d_out, m_out, v_out):
        g = a_ref[...] + b_ref[...]
        g_out[...] = g
        d_out[...], m_out[...], v_out[...] = _adamw(w_ref[...], g, m_ref[...], v_ref[...])

    blk = pl.BlockSpec((min(rows, ELT_ROWS), cols), lambda i: (i, 0))
    out = jax.ShapeDtypeStruct((rows, cols), F32)
    return pl.pallas_call(
        body, name=name, grid=(rows // min(rows, ELT_ROWS),),
        in_specs=[blk] * 5, out_specs=[blk] * 4, out_shape=[out] * 4,
        compiler_params=_params(("arbitrary",)),
    )(w, s_mine, s_sibling, m, v)


def adamw_small(ws, gs, ms, vs, name):
    n = len(ws)

    def body(*refs):
        w_refs, g_refs, m_refs, v_refs = (refs[k * n:(k + 1) * n] for k in range(4))
        d_out, m_out, v_out = (refs[(4 + k) * n:(5 + k) * n] for k in range(3))
        for i in range(n):
            d_out[i][...], m_out[i][...], v_out[i][...] = _adamw(
                w_refs[i][...], g_refs[i][...], m_refs[i][...], v_refs[i][...])

    outs = [jax.ShapeDtypeStruct(w.shape, F32) for w in ws]
    res = pl.pallas_call(body, name=name, out_shape=outs * 3)(*ws, *gs, *ms, *vs)
    return res[:n], res[n:2 * n], res[2 * n:]


SLAB_LANES = 128
SLAB_ROW_ALIGN = 8


def _pack(parts):
    flat = jnp.concatenate([p.reshape(-1) for p in parts])
    rows = -(-flat.shape[0] // (SLAB_LANES * SLAB_ROW_ALIGN)) * SLAB_ROW_ALIGN
    flat = jnp.pad(flat, (0, rows * SLAB_LANES - flat.shape[0]))
    return flat.reshape(rows, SLAB_LANES)


def _unpack(slab, shapes):
    flat = slab.reshape(-1)
    out, pos = [], 0
    for s in shapes:
        size = math.prod(s)
        out.append(flat[pos:pos + size].reshape(s))
        pos += size
    return out


def kernel(x, positions, norm_mix_pre, norm_mix_post, norm_ffn_pre, norm_ffn_post, w_in_even, lb_table, a_norm, b_ln_g, b_ln_b, b_ws, b_bias, w_out_even, w_in_odd, w_out_odd, w_ff1, w_ff2, loss_target, m_norm_mix_pre, m_norm_mix_post, m_norm_ffn_pre, m_norm_ffn_post, m_w_in_even, m_lb_table, m_a_norm, m_b_ln_g, m_b_ln_b, m_b_ws, m_b_bias, m_w_out_even, m_w_in_odd, m_w_out_odd, m_w_ff1, m_w_ff2, v_norm_mix_pre, v_norm_mix_post, v_norm_ffn_pre, v_norm_ffn_post, v_w_in_even, v_lb_table, v_a_norm, v_b_ln_g, v_b_ln_b, v_b_ws, v_b_bias, v_w_out_even, v_w_in_odd, v_w_out_odd, v_w_ff1, v_w_ff2):
    batch = x.shape[0]
    t = batch * SEQ
    d = D_MODEL
    x0 = x.reshape(t, d)
    target = loss_target.reshape(t, d)

    def gain(p, layer):
        return p[layer:layer + 1]

    def gather(*shards):
        return _Exchange("gather", [w.astype(BF16) for w in shards])

    def scatter(*grads):
        return _Exchange("scatter", grads)

    (win_e,) = exchange_alone(gather(w_in_even[0]), "gather_in_even")
    bias_t = b_bias[0].T
    proj, h0, w1_0 = norm_matmul(x0, gain(norm_mix_pre, 0), win_e, "in_proj_even", exchange=gather(w_ff1[0]))
    oa, states, w2_0, wout_e = hgrn2_fwd(proj, lb_table, a_norm, batch, "hgrn2_fwd",
                                         exchange=gather(w_ff2[0], w_out_even[0]))
    (mixin,) = gmlp_fwd(proj, oa, b_ln_g, b_ln_b, b_ws[0], bias_t, "gmlp_fwd")
    mix0, x1 = out_proj(mixin, wout_e, x0, gain(norm_mix_post, 0), "out_proj_even")
    x2, hf0, a0, y0, win_o, wout_o = ffn_fwd(x1, gain(norm_ffn_pre, 0), w1_0, w2_0, gain(norm_ffn_post, 0),
                                             "ffn_fwd_0", exchange=gather(w_in_odd[0], w_out_odd[0]))
    qkv, h1 = norm_matmul(x2, gain(norm_mix_pre, 1), win_o, "in_proj_odd")
    cos_t, sin_a, sin_b = rope_tables(positions.reshape(t, 1), "rope_tables")
    ao, lse, w1_1, w2_1 = attn_fwd(qkv, cos_t, sin_a, sin_b, batch, "attn_fwd", exchange=gather(w_ff1[1], w_ff2[1]))
    mix1, x3 = out_proj(ao, wout_o, x2, gain(norm_mix_post, 1), "out_proj_odd")
    x4, hf1, a1, y1 = ffn_fwd(x3, gain(norm_ffn_pre, 1), w1_1, w2_1, gain(norm_ffn_post, 1), "ffn_fwd_1")
    dx4, loss_part = loss_grad(x4, target, "loss_grad")

    hc = D_FF // N_CHIPS
    dx3, dy1, da1, dg_fpre1, dg_fpost1 = ffn_bwd(
        dx4, x3, y1, a1, gain(norm_ffn_pre, 1), gain(norm_ffn_post, 1), w1_1, w2_1, "ffn_bwd_1")
    g_w1_1 = weight_grad(hf1, da1, "b", d, hc, False, "wgrad_ff1_1")
    g_w2_1 = weight_grad(a1, dy1, "a", hc, d, True, "wgrad_ff2_1")
    dmix1, dao, dg_mpost1 = out_proj_bwd(dx3, mix1, gain(norm_mix_post, 1), wout_o, "out_proj_bwd_odd")
    g_wout_o = weight_grad(ao, dmix1, "a", d // N_CHIPS, d, False, "wgrad_out_odd")
    dq, dk, dv, r_w1_1, r_w2_1, r_wout_o = attn_bwd(qkv, cos_t, sin_a, sin_b, ao, lse, dao, batch, "attn_bwd",
                                                    exchange=scatter(g_w1_1, g_w2_1, g_wout_o))
    dqkv = jnp.concatenate([dq, dk, dv], axis=1)
    dx2, dg_mpre1 = norm_matmul_bwd(dqkv, win_o, x2, gain(norm_mix_pre, 1), dx3, "in_proj_bwd_odd")
    g_win_o = weight_grad(h1, dqkv, "b", d, 3 * d // N_CHIPS, False, "wgrad_in_odd")
    dx1, dy0, da0, dg_fpre0, dg_fpost0, r_win_o = ffn_bwd(
        dx2, x1, y0, a0, gain(norm_ffn_pre, 0), gain(norm_ffn_post, 0), w1_0, w2_0, "ffn_bwd_0",
        exchange=scatter(g_win_o))
    g_w1_0 = weight_grad(hf0, da0, "b", d, hc, False, "wgrad_ff1_0")
    g_w2_0 = weight_grad(a0, dy0, "a", hc, d, True, "wgrad_ff2_0")
    dmix0, dmixin, dg_mpost0 = out_proj_bwd(dx1, mix0, gain(norm_mix_post, 0), wout_e, "out_proj_bwd_even")
    g_wout_e = weight_grad(mixin, dmix0, "a", d // N_CHIPS, d, False, "wgrad_out_even")
    dproj, d_lb, d_anorm, r_w1_0 = hgrn2_bwd(
        proj, states, lb_table, a_norm, dmixin, batch, "hgrn2_bwd", exchange=scatter(g_w1_0))
    dproj, d_lng, d_lnb, d_ws, d_bias_t, r_wout_e = gmlp_bwd(
        proj, dmixin, b_ln_g, b_ln_b, b_ws[0], bias_t, dproj, "gmlp_bwd", exchange=scatter(g_wout_e))
    g_win_e, r_w2_0 = weight_grad(h0, dproj, "b", d, 3 * d // N_CHIPS, False, "wgrad_in_even",
                                  exchange=scatter(g_w2_0))
    dx0, dg_mpre0, r_win_e = norm_matmul_bwd(dproj, win_e, x0, gain(norm_mix_pre, 0), dx1, "in_proj_bwd_even",
                                             exchange=scatter(g_win_e))
    grad_x = dx0.reshape(x.shape)

    s_w1 = reduce_slabs(r_w1_1, "reduce_ff1_1", part=1, parts=2)
    s_w1 = reduce_slabs(r_w1_0, "reduce_ff1_0", part=0, parts=2, into=s_w1)
    s_w2 = reduce_slabs(r_w2_1, "reduce_ff2_1", part=1, parts=2)
    s_w2 = reduce_slabs(r_w2_0, "reduce_ff2_0", part=0, parts=2, into=s_w2)
    sums = [reduce_slabs(r_win_e, "reduce_in_even"), reduce_slabs(r_wout_e, "reduce_out_even"),
            reduce_slabs(r_win_o, "reduce_in_odd"), reduce_slabs(r_wout_o, "reduce_out_odd"), s_w1, s_w2]
    sibling = sibling_swap(sums, "sibling_swap")
    big_w = [w_in_even, w_out_even, w_in_odd, w_out_odd, w_ff1, w_ff2]
    big_m = [m_w_in_even, m_w_out_even, m_w_in_odd, m_w_out_odd, m_w_ff1, m_w_ff2]
    big_v = [v_w_in_even, v_w_out_even, v_w_in_odd, v_w_out_odd, v_w_ff1, v_w_ff2]
    big = []
    for i, (w, m, v) in enumerate(zip(big_w, big_m, big_v)):
        two_d = (-1, w.shape[-1])
        res = adamw_big(w.reshape(two_d), sums[i], sibling[i], m.reshape(two_d), v.reshape(two_d), "adamw_big_%d" % i)
        big.append([r.reshape(w.shape) for r in res])

    small_w = [norm_mix_pre, norm_mix_post, norm_ffn_pre, norm_ffn_post, lb_table, a_norm, b_ln_g, b_ln_b, b_ws, b_bias]
    small_m = [m_norm_mix_pre, m_norm_mix_post, m_norm_ffn_pre, m_norm_ffn_post, m_lb_table, m_a_norm, m_b_ln_g,
               m_b_ln_b, m_b_ws, m_b_bias]
    small_v = [v_norm_mix_pre, v_norm_mix_post, v_norm_ffn_pre, v_norm_ffn_post, v_lb_table, v_a_norm, v_b_ln_g,
               v_b_ln_b, v_b_ws, v_b_bias]
    partial = [jnp.concatenate([dg_mpre0, dg_mpre1]), jnp.concatenate([dg_mpost0, dg_mpost1]),
               jnp.concatenate([dg_fpre0, dg_fpre1]), jnp.concatenate([dg_fpost0, dg_fpost1]),
               d_lb, d_anorm, d_lng, d_lnb, d_ws[None], d_bias_t.T[None]]
    *small_g, loss = _unpack(allreduce_small(_pack(partial + [loss_part]), "allreduce_small"),
                             [w.shape for w in small_w] + [()])
    small_d, small_nm, small_nv = adamw_small(small_w, small_g, small_m, small_v, "adamw_small")

    order = ["norm_mix_pre", "norm_mix_post", "norm_ffn_pre", "norm_ffn_post", "w_in_even", "lb_table", "a_norm",
             "b_ln_g", "b_ln_b", "b_ws", "b_bias", "w_out_even", "w_in_odd", "w_out_odd", "w_ff1", "w_ff2"]
    small_names = ["norm_mix_pre", "norm_mix_post", "norm_ffn_pre", "norm_ffn_post", "lb_table", "a_norm",
                   "b_ln_g", "b_ln_b", "b_ws", "b_bias"]
    big_names = ["w_in_even", "w_out_even", "w_in_odd", "w_out_odd", "w_ff1", "w_ff2"]
    grads, deltas, new_m, new_v = {}, {}, {}, {}
    for i, nm in enumerate(small_names):
        grads[nm], deltas[nm], new_m[nm], new_v[nm] = small_g[i], small_d[i], small_nm[i], small_nv[i]
    for i, nm in enumerate(big_names):
        grads[nm], deltas[nm], new_m[nm], new_v[nm] = big[i]
    return (loss, grad_x, *[grads[n] for n in order], *[deltas[n] for n in order],
            *[new_m[n] for n in order], *[new_v[n] for n in order])
```

```python
import functools
import math

import jax
import jax.numpy as jnp
from jax import lax
from jax.experimental import pallas as pl
from jax.experimental.pallas import tpu as pltpu

F32 = jnp.float32
BF16 = jnp.bfloat16
MESH = pl.DeviceIdType.MESH

D_MODEL = 1024
SEQ = 2048
D_FF = 4096
N_CHIPS = 4
A_WIDTH = 512
A_HEADS = 4
A_DK = 128
A_CHUNK = 64
A_SUB = 16
B_WIDTH = 512
B_GROUPS = 4
B_CHUNK = 128
C_HEADS = 16
C_HEAD_DIM = 64
C_ROT_HALF = 8
C_BLOCK = 128
C_DILATIONS = (1, 4, 16)
ROPE_THETA = 500000.0
EPS = 1e-6
ADAM_LR = 0.001
ADAM_B1 = 0.9
ADAM_B2 = 0.999
ADAM_EPS = 1e-08
ADAM_WD = 0.01
ADAM_STEP = 10

ROW_TILE = 512
FFN_ROWS = 1024
WGRAD_ROWS = 2048
VMEM_LIMIT = 56 * 1024 * 1024
NEG_BIG = -1e30


def _params(sem=None):
    return pltpu.CompilerParams(dimension_semantics=sem, vmem_limit_bytes=VMEM_LIMIT)


def _dot(a, b):
    return jnp.dot(a, b, preferred_element_type=F32)


def _dot_nt(a, b):
    return lax.dot_general(a, b, (((1,), (1,)), ((), ())), preferred_element_type=F32)


def _dot_tn(a, b):
    return lax.dot_general(a, b, (((0,), (0,)), ((), ())), preferred_element_type=F32)


def _rms(x, g):
    r = lax.rsqrt(jnp.mean(x * x, axis=-1, keepdims=True) + EPS)
    return x * r * g


def _rms_bwd(x, g, dy):
    r = lax.rsqrt(jnp.mean(x * x, axis=-1, keepdims=True) + EPS)
    xh = x * r
    dg = jnp.sum(dy * xh, axis=0, keepdims=True)
    dxh = dy * g
    dx = r * (dxh - xh * jnp.mean(dxh * xh, axis=-1, keepdims=True))
    return dx, dg


def _accumulate(ref, val, first):
    @pl.when(first)
    def _():
        ref[...] = val

    @pl.when(jnp.logical_not(first))
    def _():
        ref[...] += val


N_DEV = 8
ANY = pl.BlockSpec(memory_space=pl.ANY)


def _place():
    x, y, c = lax.axis_index("x"), lax.axis_index("y"), lax.axis_index("c")
    return x, y, c, [(1 - x, y), (x, 1 - y), (1 - x, 1 - y)]


class _Exchange:
    def __init__(self, kind, arrays):
        self.kind, self.arrays, self.n = kind, list(arrays), len(arrays)
        per_peer = pltpu.SemaphoreType.DMA((3 * self.n,))
        if kind == "gather":
            self.out_shape = [jax.ShapeDtypeStruct((N_CHIPS,) + a.shape, a.dtype) for a in self.arrays]
            self.scratch = [per_peer, per_peer, pltpu.SemaphoreType.DMA((self.n,)), per_peer, per_peer]
        else:
            self.out_shape = [jax.ShapeDtypeStruct(a.shape, a.dtype) for a in self.arrays]
            self.scratch = [per_peer, per_peer, pltpu.SemaphoreType.DMA((self.n,))]

    def _copies(self, ins, outs, sems):
        send_sems, recv_sems, local_sems = sems[:3]
        x, y, c, chips = _place()
        me = 2 * x + y
        local, remote = [], []
        for a in range(self.n):
            if self.kind == "gather":
                local.append(pltpu.make_async_copy(ins[a], outs[a].at[me], local_sems.at[a]))
                half = self.arrays[a].shape[0] // 2

                def rows(ref, core, half=half):
                    return ref.at[pl.ds(core * half, half)]
            else:
                local.append(pltpu.make_async_copy(ins[a].at[me], outs[a].at[3], local_sems.at[a]))
            for j, (px, py) in enumerate(chips):
                k = 3 * a + j
                peer = 2 * px + py

                def copy(src, dst, to, send_sem=send_sems.at[k], recv_sem=recv_sems.at[k]):
                    return pltpu.make_async_remote_copy(src_ref=src, dst_ref=dst, send_sem=send_sem, recv_sem=recv_sem,
                                                        device_id=to, device_id_type=MESH)

                if self.kind == "gather":
                    sent = copy(rows(ins[a], c), rows(outs[a].at[me], c), (px, py, c))
                    landed = copy(rows(ins[a], c), rows(outs[a].at[peer], c), (px, py, c))
                    on = dict(send_sem=sems[3].at[k], recv_sem=sems[4].at[k])
                    passed = copy(rows(outs[a].at[peer], c), rows(outs[a].at[peer], c), (x, y, 1 - c), **on)
                    handed = copy(rows(outs[a].at[peer], c), rows(outs[a].at[peer], 1 - c), (x, y, 1 - c), **on)
                    remote.append((sent, landed, passed, handed))
                else:
                    sent = copy(ins[a].at[peer], outs[a].at[j], (px, py, c))
                    remote.append((sent, sent, None, None))
        return local, remote

    def start(self, ins, outs, sems):
        local, remote = self._copies(ins, outs, sems)
        for cp in local:
            cp.start()
        for sent, _, _, _ in remote:
            sent.start()

    def finish(self, ins, outs, sems):
        local, remote = self._copies(ins, outs, sems)
        for _, landed, passed, _ in remote:
            landed.wait_recv()
            if passed is not None:
                passed.start()
        for sent, _, passed, handed in remote:
            if passed is not None:
                handed.wait_recv()
                passed.wait_send()
            sent.wait_send()
        for cp in local:
            cp.wait()


def _call(body, *, name, grid, in_specs, out_specs, out_shape, args, scratch_shapes=(), aliases=None, exchange=None):
    if exchange is None:
        return pl.pallas_call(
            body, name=name, grid=grid, in_specs=in_specs, out_specs=out_specs, out_shape=out_shape,
            scratch_shapes=list(scratch_shapes), input_output_aliases=aliases or {},
            compiler_params=_params(("arbitrary",) * len(grid)))(*args)
    n_in, n_out, n_scr, n_ex = len(in_specs), len(out_specs), len(scratch_shapes), exchange.n
    steps = grid

    def wrapped(*refs):
        ins, refs = refs[:n_in], refs[n_in:]
        ex_in, refs = refs[:n_ex], refs[n_ex:]
        outs, refs = refs[:n_out], refs[n_out:]
        ex_out, refs = refs[:n_ex], refs[n_ex:]
        scr, sems = refs[:n_scr], refs[n_scr:]
        first = functools.reduce(jnp.logical_and, [pl.program_id(k) == 0 for k in range(len(steps))])
        last = functools.reduce(jnp.logical_and, [pl.program_id(k) == steps[k] - 1 for k in range(len(steps))])

        @pl.when(first)
        def _():
            exchange.start(ex_in, ex_out, sems)

        body(*ins, *outs, *scr)

        @pl.when(last)
        def _():
            exchange.finish(ex_in, ex_out, sems)

    return pl.pallas_call(
        wrapped, name=name, grid=grid,
        in_specs=list(in_specs) + [ANY] * n_ex, out_specs=list(out_specs) + [ANY] * n_ex,
        out_shape=list(out_shape) + exchange.out_shape,
        scratch_shapes=list(scratch_shapes) + exchange.scratch, input_output_aliases=aliases or {},
        compiler_params=_params(("arbitrary",) * len(grid)))(*args, *exchange.arrays)


def exchange_alone(exchange, name):
    def body(*refs):
        n = exchange.n
        exchange.start(refs[:n], refs[n:2 * n], refs[2 * n:])
        exchange.finish(refs[:n], refs[n:2 * n], refs[2 * n:])

    return pl.pallas_call(
        body, name=name, in_specs=[ANY] * exchange.n, out_specs=[ANY] * exchange.n,
        out_shape=exchange.out_shape, scratch_shapes=exchange.scratch)(*exchange.arrays)


def norm_matmul(x, g, wg, name, exchange=None):
    t, d = x.shape
    nl = wg.shape[2]

    def body(x_ref, g_ref, w_ref, o_ref, h_ref):
        h = _rms(x_ref[...], g_ref[...]).astype(BF16)
        h_ref[...] = h
        for c in range(N_CHIPS):
            o_ref[:, c * nl:(c + 1) * nl] = _dot(h, w_ref[c])

    return _call(
        body, name=name, grid=(t // ROW_TILE,),
        in_specs=[pl.BlockSpec((ROW_TILE, d), lambda i: (i, 0)),
                  pl.BlockSpec((1, d), lambda i: (0, 0)),
                  pl.BlockSpec((N_CHIPS, d, nl), lambda i: (0, 0, 0))],
        out_specs=[pl.BlockSpec((ROW_TILE, N_CHIPS * nl), lambda i: (i, 0)),
                   pl.BlockSpec((ROW_TILE, d), lambda i: (i, 0))],
        out_shape=[jax.ShapeDtypeStruct((t, N_CHIPS * nl), F32), jax.ShapeDtypeStruct((t, d), BF16)],
        args=(x, g, wg), exchange=exchange)


def norm_matmul_bwd(dproj, wg, x, g, dres, name, exchange=None):
    t, d = x.shape
    nl = wg.shape[2]

    def body(dp_ref, w_ref, x_ref, g_ref, dres_ref, dx_ref, dg_ref):
        dh = _dot_nt(dp_ref[:, 0:nl].astype(BF16), w_ref[0])
        for c in range(1, N_CHIPS):
            dh += _dot_nt(dp_ref[:, c * nl:(c + 1) * nl].astype(BF16), w_ref[c])
        dx, dg = _rms_bwd(x_ref[...], g_ref[...], dh)
        dx_ref[...] = dres_ref[...] + dx
        _accumulate(dg_ref, dg, pl.program_id(0) == 0)

    row = pl.BlockSpec((ROW_TILE, d), lambda i: (i, 0))
    vec = pl.BlockSpec((1, d), lambda i: (0, 0))
    return _call(
        body, name=name, grid=(t // ROW_TILE,),
        in_specs=[pl.BlockSpec((ROW_TILE, N_CHIPS * nl), lambda i: (i, 0)),
                  pl.BlockSpec((N_CHIPS, d, nl), lambda i: (0, 0, 0)), row, vec, row],
        out_specs=[row, vec],
        out_shape=[jax.ShapeDtypeStruct((t, d), F32), jax.ShapeDtypeStruct((1, d), F32)],
        args=(dproj, wg, x, g, dres), exchange=exchange)


def out_proj(a, wg, x, g, name):
    t, d = x.shape
    kl = wg.shape[1]

    def body(a_ref, w_ref, x_ref, g_ref, mix_ref, xo_ref):
        acc = _dot(a_ref[:, 0:kl], w_ref[0])
        for c in range(1, N_CHIPS):
            acc += _dot(a_ref[:, c * kl:(c + 1) * kl], w_ref[c])
        mix_ref[...] = acc
        xo_ref[...] = x_ref[...] + _rms(acc, g_ref[...])

    row = pl.BlockSpec((ROW_TILE, d), lambda i: (i, 0))
    return pl.pallas_call(
        body, name=name, grid=(t // ROW_TILE,),
        in_specs=[row, pl.BlockSpec((N_CHIPS, kl, d), lambda i: (0, 0, 0)), row,
                  pl.BlockSpec((1, d), lambda i: (0, 0))],
        out_specs=[row, row],
        out_shape=[jax.ShapeDtypeStruct((t, d), F32), jax.ShapeDtypeStruct((t, d), F32)],
        compiler_params=_params(("arbitrary",)),
    )(a, wg, x, g)


def out_proj_bwd(dxo, mix, g, wg, name):
    t, d = mix.shape
    kl = wg.shape[1]

    def body(dxo_ref, mix_ref, g_ref, w_ref, dmix_ref, da_ref, dg_ref):
        dmix, dg = _rms_bwd(mix_ref[...], g_ref[...], dxo_ref[...])
        dmb = dmix.astype(BF16)
        dmix_ref[...] = dmb
        for c in range(N_CHIPS):
            da_ref[:, c * kl:(c + 1) * kl] = _dot_nt(dmb, w_ref[c])
        _accumulate(dg_ref, dg, pl.program_id(0) == 0)

    row = pl.BlockSpec((ROW_TILE, d), lambda i: (i, 0))
    vec = pl.BlockSpec((1, d), lambda i: (0, 0))
    return pl.pallas_call(
        body, name=name, grid=(t // ROW_TILE,),
        in_specs=[row, row, vec, pl.BlockSpec((N_CHIPS, kl, d), lambda i: (0, 0, 0))],
        out_specs=[row, row, vec],
        out_shape=[jax.ShapeDtypeStruct((t, d), BF16), jax.ShapeDtypeStruct((t, d), F32),
                   jax.ShapeDtypeStruct((1, d), F32)],
        compiler_params=_params(("arbitrary",)),
    )(dxo, mix, g, wg)


def ffn_fwd(x, gpre, w1g, w2g, gpost, name, exchange=None):
    t, d = x.shape
    hc = w1g.shape[2]

    def body(x_ref, gpre_ref, w1_ref, w2_ref, gpost_ref, xo_ref, h_ref, a_ref, y_ref, acc):
        c = pl.program_id(1)

        @pl.when(c == 0)
        def _():
            h_ref[...] = _rms(x_ref[...], gpre_ref[...]).astype(BF16)

        a = _dot(h_ref[...], w1_ref[...])
        a_ref[...] = a.astype(BF16)
        r = jnp.square(jnp.maximum(a, 0.0)).astype(BF16)
        _accumulate(acc, _dot(r, w2_ref[...]), c == 0)

        @pl.when(c == N_CHIPS - 1)
        def _():
            y = acc[...]
            y_ref[...] = y
            xo_ref[...] = x_ref[...] + _rms(y, gpost_ref[...])

    row = pl.BlockSpec((FFN_ROWS, d), lambda i, c: (i, 0))
    vec = pl.BlockSpec((1, d), lambda i, c: (0, 0))
    return _call(
        body, name=name, grid=(t // FFN_ROWS, N_CHIPS),
        in_specs=[row, vec,
                  pl.BlockSpec((None, d, hc), lambda i, c: (c, 0, 0)),
                  pl.BlockSpec((None, hc, d), lambda i, c: (c, 0, 0)), vec],
        out_specs=[row, row, pl.BlockSpec((FFN_ROWS, hc), lambda i, c: (i, c)), row],
        out_shape=[jax.ShapeDtypeStruct((t, d), F32), jax.ShapeDtypeStruct((t, d), BF16),
                   jax.ShapeDtypeStruct((t, N_CHIPS * hc), BF16), jax.ShapeDtypeStruct((t, d), F32)],
        scratch_shapes=[pltpu.VMEM((FFN_ROWS, d), F32)],
        args=(x, gpre, w1g, w2g, gpost), exchange=exchange)


def ffn_bwd(dxo, x, y, a, gpre, gpost, w1g, w2g, name, exchange=None):
    t, d = x.shape
    hc = w1g.shape[2]

    def body(dxo_ref, x_ref, y_ref, a_ref, gpre_ref, gpost_ref, w1_ref, w2_ref,
             dxi_ref, dy_ref, da_ref, dgpre_ref, dgpost_ref, acc):
        i, c = pl.program_id(0), pl.program_id(1)

        @pl.when(c == 0)
        def _():
            dy, dg = _rms_bwd(y_ref[...], gpost_ref[...], dxo_ref[...])
            dy_ref[...] = dy.astype(BF16)
            _accumulate(dgpost_ref, dg, i == 0)

        dr = _dot_nt(dy_ref[...], w2_ref[...])
        da = (dr * (2.0 * jnp.maximum(a_ref[...].astype(F32), 0.0))).astype(BF16)
        da_ref[...] = da
        _accumulate(acc, _dot_nt(da, w1_ref[...]), c == 0)

        @pl.when(c == N_CHIPS - 1)
        def _():
            dx, dg = _rms_bwd(x_ref[...], gpre_ref[...], acc[...])
            dxi_ref[...] = dxo_ref[...] + dx
            _accumulate(dgpre_ref, dg, i == 0)

    row = pl.BlockSpec((ROW_TILE, d), lambda i, c: (i, 0))
    vec = pl.BlockSpec((1, d), lambda i, c: (0, 0))
    hid = pl.BlockSpec((ROW_TILE, hc), lambda i, c: (i, c))
    return _call(
        body, name=name, grid=(t // ROW_TILE, N_CHIPS),
        in_specs=[row, row, row, hid, vec, vec,
                  pl.BlockSpec((None, d, hc), lambda i, c: (c, 0, 0)),
                  pl.BlockSpec((None, hc, d), lambda i, c: (c, 0, 0))],
        out_specs=[row, row, hid, vec, vec],
        out_shape=[jax.ShapeDtypeStruct((t, d), F32), jax.ShapeDtypeStruct((t, d), BF16),
                   jax.ShapeDtypeStruct((t, N_CHIPS * hc), BF16),
                   jax.ShapeDtypeStruct((1, d), F32), jax.ShapeDtypeStruct((1, d), F32)],
        scratch_shapes=[pltpu.VMEM((ROW_TILE, d), F32)],
        args=(dxo, x, y, a, gpre, gpost, w1g, w2g), exchange=exchange)


def weight_grad(a, b, chunked, bk, bn, relu2, name, exchange=None):
    t = a.shape[0]
    a_on = chunked == "a"
    rows = min(t, WGRAD_ROWS)
    n_steps = t // rows

    def body(a_ref, b_ref, o_ref, acc):
        s = pl.program_id(1)
        av = a_ref[...]
        if relu2:
            av = jnp.square(jnp.maximum(av.astype(F32), 0.0))
        _accumulate(acc, _dot_tn(av.astype(BF16), b_ref[...].astype(BF16)), s == 0)

        @pl.when(s == n_steps - 1)
        def _():
            o_ref[...] = acc[...].astype(BF16)

    res = _call(
        body, name=name, grid=(N_CHIPS, n_steps),
        in_specs=[pl.BlockSpec((rows, bk), (lambda c, s: (s, c)) if a_on else (lambda c, s: (s, 0))),
                  pl.BlockSpec((rows, bn), (lambda c, s: (s, 0)) if a_on else (lambda c, s: (s, c)))],
        out_specs=[pl.BlockSpec((None, bk, bn), lambda c, s: (c, 0, 0))],
        out_shape=[jax.ShapeDtypeStruct((N_CHIPS, bk, bn), BF16)],
        scratch_shapes=[pltpu.VMEM((bk, bn), F32)],
        args=(a, b), exchange=exchange)
    return res[0] if exchange is None else res


def loss_grad(xf, target, name):
    t, d = xf.shape

    def body(x_ref, t_ref, dy_ref, l_ref):
        e = x_ref[...] - t_ref[...]
        dy_ref[...] = e * (1.0 / d)
        part = jnp.sum(jnp.sum(e * e, axis=-1, keepdims=True), axis=0, keepdims=True) * (0.5 / d)
        _accumulate(l_ref, part, pl.program_id(0) == 0)

    row = pl.BlockSpec((ROW_TILE, d), lambda i: (i, 0))
    return pl.pallas_call(
        body, name=name, grid=(t // ROW_TILE,),
        in_specs=[row, row],
        out_specs=[row, pl.BlockSpec((1, 1), lambda i: (0, 0))],
        out_shape=[jax.ShapeDtypeStruct((t, d), F32), jax.ShapeDtypeStruct((1, 1), F32)],
        compiler_params=_params(("arbitrary",)),
    )(xf, target)


def _hgrn2_chunk(st, qs, fls, ivs, gls, l0, l1, l2, ng):
    nsub = len(qs)
    mx = jnp.maximum(jnp.maximum(l0, l1), l2)
    e0, e1, e2 = jnp.exp(l0 - mx), jnp.exp(l1 - mx), jnp.exp(l2 - mx)
    lb = e0 / (e0 + e1 + e2)
    rows = lax.broadcasted_iota(jnp.int32, (A_SUB, A_SUB), 0)
    cols = lax.broadcasted_iota(jnp.int32, (A_SUB, A_SUB), 1)
    tri = (rows >= cols).astype(F32)
    keep = (lax.broadcasted_iota(jnp.int32, (A_SUB, A_SUB, A_DK), 0)
            >= lax.broadcasted_iota(jnp.int32, (A_SUB, A_SUB, A_DK), 1))
    base = jnp.zeros_like(l0)
    bases, gs, ks, qfs = [], [], [], []
    for i in range(nsub):
        f = lb + (1.0 - lb) * jax.nn.sigmoid(fls[i])
        logf = jnp.log(f)
        bases.append(base)
        gs.append(base + jnp.dot(tri, logf, precision=lax.Precision.HIGHEST, preferred_element_type=F32))
        base = base + jnp.sum(logf, axis=0, keepdims=True)
        ks.append(1.0 - f)
        qfs.append(jax.nn.silu(qs[i]))
    g_last = base
    stb = st.astype(BF16)
    outs = []
    for i in range(nsub):
        o = _dot_nt((qfs[i] * jnp.exp(gs[i])).astype(BF16), stb)
        if i > 0:
            qt = (qfs[i] * jnp.exp(gs[i] - bases[i])).astype(BF16)
            kk = jnp.concatenate([ks[j] * jnp.exp(bases[i] - gs[j]) for j in range(i)], axis=0).astype(BF16)
            vv = jnp.concatenate(ivs[:i], axis=0).astype(BF16)
            o = o + _dot(_dot_nt(qt, kk).astype(BF16), vv)
        dec = jnp.exp(jnp.where(keep, gs[i][:, None, :] - gs[i][None, :, :], NEG_BIG))
        s_diag = jnp.sum(qfs[i][:, None, :] * ks[i][None, :, :] * dec, axis=-1)
        o = o + _dot(s_diag.astype(BF16), ivs[i].astype(BF16))
        o = o * lax.rsqrt(jnp.mean(o * o, axis=-1, keepdims=True) + EPS) * ng
        outs.append(o * jax.nn.silu(gls[i]))
    kdec = jnp.concatenate([ks[j] * jnp.exp(g_last - gs[j]) for j in range(nsub)], axis=0).astype(BF16)
    vall = jnp.concatenate(ivs, axis=0).astype(BF16)
    new_st = st * jnp.exp(g_last) + _dot_tn(vall, kdec)
    return new_st, outs


A_MAX_LOG_DECAY = 80.0


def _split3(x):
    hi = x.astype(BF16)
    r1 = x - hi.astype(F32)
    mid = r1.astype(BF16)
    return hi, mid, (r1 - mid.astype(F32)).astype(BF16)


def _tri_matmul(x, transpose):
    n = x.shape[0]
    r = lax.broadcasted_iota(jnp.int32, (n, n), 0)
    c = lax.broadcasted_iota(jnp.int32, (n, n), 1)
    tri = ((r <= c) if transpose else (r >= c)).astype(BF16)
    hi, mid, lo = _split3(x)
    return (_dot(tri, lo) + _dot(tri, mid)) + _dot(tri, hi)


@jax.custom_vjp
def _cumsum_rows(x):
    return _tri_matmul(x, False)


def _cumsum_rows_fwd(x):
    return _tri_matmul(x, False), None


def _cumsum_rows_bwd(_, dy):
    return (_tri_matmul(dy, True),)


_cumsum_rows.defvjp(_cumsum_rows_fwd, _cumsum_rows_bwd)


def _lower_bound(l0, l1, l2):
    mx = jnp.maximum(jnp.maximum(l0, l1), l2)
    e0, e1, e2 = jnp.exp(l0 - mx), jnp.exp(l1 - mx), jnp.exp(l2 - mx)
    return e0 / (e0 + e1 + e2)


def _b(x):
    return x.astype(BF16)


@jax.custom_vjp
def _mm(a, b):
    return _dot(_b(a), _b(b))


_mm.defvjp(lambda a, b: (_mm(a, b), (a, b)),
           lambda res, d: (_dot_nt(_b(d), _b(res[1])), _dot_tn(_b(res[0]), _b(d))))


@jax.custom_vjp
def _mm_nt(a, b):
    return _dot_nt(_b(a), _b(b))


_mm_nt.defvjp(lambda a, b: (_mm_nt(a, b), (a, b)),
              lambda res, d: (_dot(_b(d), _b(res[1])), _dot_tn(_b(d), _b(res[0]))))


def _dot_split(dot, a, b):
    ah, bh = _b(a), _b(b)
    al, bl = _b(a - ah.astype(F32)), _b(b - bh.astype(F32))
    return (dot(ah, bl) + dot(al, bh)) + dot(ah, bh)


@jax.custom_vjp
def _mm_scores(a, b):
    return _dot_nt(_b(a), _b(b))


_mm_scores.defvjp(lambda a, b: (_mm_scores(a, b), (a, b)),
                  lambda res, d: (_dot_split(_dot, d, res[1]), _dot_split(_dot_tn, d, res[0])))


@jax.custom_vjp
def _mm_tn(a, b):
    return _dot_tn(_b(a), _b(b))


_mm_tn.defvjp(lambda a, b: (_mm_tn(a, b), (a, b)),
              lambda res, d: (_dot_nt(_b(res[1]), _b(d)), _dot(_b(res[0]), _b(d))))


@jax.custom_vjp
def _split_heads(x):
    return tuple(x[:, h * A_DK:(h + 1) * A_DK] for h in range(A_HEADS))


def _split_heads_fwd(x):
    return _split_heads(x), None


def _split_heads_bwd(_, parts):
    return (jnp.concatenate(parts, axis=1),)


_split_heads.defvjp(_split_heads_fwd, _split_heads_bwd)


def _hgrn2_chunk_fast(sts, q, fl, iv, gl, l0, l1, l2, ng):
    lb = _lower_bound(l0, l1, l2)
    f = lb + (1.0 - lb) * jax.nn.sigmoid(fl)
    logf = jnp.log(f)
    g = _cumsum_rows(logf)
    g_last = jnp.sum(logf, axis=0, keepdims=True)
    k = 1.0 - f
    qgs = _split_heads(jax.nn.silu(q) * jnp.exp(g))
    kgs = _split_heads(k * jnp.exp(-g))
    kds = _split_heads(k * jnp.exp(g_last - g))
    ivs = _split_heads(iv)
    decays = _split_heads(jnp.exp(g_last))
    n = q.shape[0]
    causal = lax.broadcasted_iota(jnp.int32, (n, n), 0) >= lax.broadcasted_iota(jnp.int32, (n, n), 1)
    raw = [_mm_scores(qg, kg) for qg, kg in zip(qgs, kgs)]
    inter = [_mm_nt(qg, st) for qg, st in zip(qgs, sts)]
    scores = [jnp.where(causal, s, 0.0) for s in raw]
    os = [a + _mm(s, v) for a, s, v in zip(inter, scores, ivs)]
    new_sts = [st * d + _mm_tn(v, kd) for st, d, v, kd in zip(sts, decays, ivs, kds)]
    os = [o * lax.rsqrt(jnp.mean(o * o, axis=-1, keepdims=True) + EPS) for o in os]
    return new_sts, jnp.concatenate(os, axis=1) * ng * jax.nn.silu(gl)


def _chunk_decays_mildly(f_ref, lb_ref):
    lb = _lower_bound(lb_ref[0:1, :], lb_ref[1:2, :], lb_ref[2:3, :])
    logf = jnp.log(lb + (1.0 - lb) * jax.nn.sigmoid(f_ref[...]))
    return jnp.min(jnp.sum(logf, axis=0, keepdims=True)) >= -A_MAX_LOG_DECAY


def _sub_blocks(ref, head):
    lanes = slice(head * A_DK, (head + 1) * A_DK)
    return [ref[i * A_SUB:(i + 1) * A_SUB, lanes] for i in range(A_CHUNK // A_SUB)]


def hgrn2_fwd(proj, lb_table, a_norm, batch, name, exchange=None):
    t = proj.shape[0]
    n_chunks = t // batch // A_CHUNK
    nblk = A_WIDTH // A_DK

    def body(q_ref, f_ref, i_ref, g_ref, lb_ref, ng_ref, o_ref, st_ref, st):
        @pl.when(pl.program_id(1) == 0)
        def _():
            st[...] = jnp.zeros_like(st)

        st_ref[...] = st[...]
        mild = _chunk_decays_mildly(f_ref, lb_ref)

        @pl.when(mild)
        def _():
            new_sts, o = _hgrn2_chunk_fast(
                [st[h] for h in range(A_HEADS)], q_ref[...], f_ref[...], i_ref[...], g_ref[...],
                lb_ref[0:1, :], lb_ref[1:2, :], lb_ref[2:3, :], ng_ref[...])
            for h in range(A_HEADS):
                st[h] = new_sts[h]
            o_ref[...] = o.astype(BF16)

        @pl.when(jnp.logical_not(mild))
        def _():
            for h in range(A_HEADS):
                lanes = slice(h * A_DK, (h + 1) * A_DK)
                new_st, outs = _hgrn2_chunk(
                    st[h], _sub_blocks(q_ref, h), _sub_blocks(f_ref, h), _sub_blocks(i_ref, h),
                    _sub_blocks(g_ref, h), lb_ref[0:1, lanes], lb_ref[1:2, lanes], lb_ref[2:3, lanes],
                    ng_ref[:, lanes])
                st[h] = new_st
                for i, o in enumerate(outs):
                    o_ref[i * A_SUB:(i + 1) * A_SUB, lanes] = o.astype(BF16)

    def part(k):
        return pl.BlockSpec((A_CHUNK, A_WIDTH), lambda b, n: (b * n_chunks + n, k))

    return _call(
        body, name=name, grid=(batch, n_chunks),
        in_specs=[part(0), part(1), part(2), part(3),
                  pl.BlockSpec((3, A_WIDTH), lambda b, n: (0, 0)), pl.BlockSpec((1, A_WIDTH), lambda b, n: (0, 0))],
        out_specs=[pl.BlockSpec((A_CHUNK, A_WIDTH), lambda b, n: (b * n_chunks + n, 0)),
                   pl.BlockSpec((None, A_HEADS, A_DK, A_DK), lambda b, n: (b * n_chunks + n, 0, 0, 0))],
        out_shape=[jax.ShapeDtypeStruct((t, A_WIDTH), BF16),
                   jax.ShapeDtypeStruct((t // A_CHUNK, A_HEADS, A_DK, A_DK), F32)],
        scratch_shapes=[pltpu.VMEM((A_HEADS, A_DK, A_DK), F32)],
        args=(proj, proj, proj, proj, lb_table, a_norm), exchange=exchange)


def hgrn2_bwd(proj, states, lb_table, a_norm, do, batch, name, exchange=None):
    t = proj.shape[0]
    n_chunks = t // batch // A_CHUNK

    def body(q_ref, f_ref, i_ref, g_ref, st_ref, lb_ref, ng_ref, do_ref, dp_ref, dlb_ref, dng_ref, dst):
        @pl.when(jnp.logical_and(pl.program_id(0) == 0, pl.program_id(1) == 0))
        def _():
            dlb_ref[...] = jnp.zeros_like(dlb_ref)
            dng_ref[...] = jnp.zeros_like(dng_ref)

        @pl.when(pl.program_id(1) == 0)
        def _():
            dst[...] = jnp.zeros_like(dst)

        mild = _chunk_decays_mildly(f_ref, lb_ref)

        @pl.when(mild)
        def _():
            _, vjp = jax.vjp(
                _hgrn2_chunk_fast, [st_ref[h] for h in range(A_HEADS)], q_ref[...], f_ref[...], i_ref[...],
                g_ref[...], lb_ref[0:1, :], lb_ref[1:2, :], lb_ref[2:3, :], ng_ref[...])
            d_sts, dq, df, di, dg, dl0, dl1, dl2, dng = vjp(([dst[h] for h in range(A_HEADS)], do_ref[...].astype(F32)))
            for h in range(A_HEADS):
                dst[h] = d_sts[h]
            for k, part in enumerate((dq, df, di, dg)):
                dp_ref[:, k * A_WIDTH:(k + 1) * A_WIDTH] = part
            for row, val in enumerate((dl0, dl1, dl2)):
                dlb_ref[row:row + 1, :] += val
            dng_ref[...] += dng

        @pl.when(jnp.logical_not(mild))
        def _():
            for h in range(A_HEADS):
                lanes = slice(h * A_DK, (h + 1) * A_DK)
                _, vjp = jax.vjp(
                    _hgrn2_chunk, st_ref[h], _sub_blocks(q_ref, h), _sub_blocks(f_ref, h), _sub_blocks(i_ref, h),
                    _sub_blocks(g_ref, h), lb_ref[0:1, lanes], lb_ref[1:2, lanes], lb_ref[2:3, lanes],
                    ng_ref[:, lanes])
                douts = [x.astype(F32) for x in _sub_blocks(do_ref, h)]
                d_st, dqs, dfs, dis, dgs, dl0, dl1, dl2, dng = vjp((dst[h], douts))
                dst[h] = d_st
                for k, parts in enumerate((dqs, dfs, dis, dgs)):
                    for i in range(A_CHUNK // A_SUB):
                        dp_ref[i * A_SUB:(i + 1) * A_SUB,
                               k * A_WIDTH + h * A_DK:k * A_WIDTH + (h + 1) * A_DK] = parts[i]
                for row, val in enumerate((dl0, dl1, dl2)):
                    dlb_ref[row:row + 1, lanes] += val
                dng_ref[:, lanes] += dng

    def rev(b, n):
        return b * n_chunks + (n_chunks - 1 - n)

    def part(k):
        return pl.BlockSpec((A_CHUNK, A_WIDTH), lambda b, n: (rev(b, n), k))

    const3 = pl.BlockSpec((3, A_WIDTH), lambda b, n: (0, 0))
    const1 = pl.BlockSpec((1, A_WIDTH), lambda b, n: (0, 0))
    return _call(
        body, name=name, grid=(batch, n_chunks),
        in_specs=[part(0), part(1), part(2), part(3),
                  pl.BlockSpec((None, A_HEADS, A_DK, A_DK), lambda b, n: (rev(b, n), 0, 0, 0)),
                  const3, const1, part(0)],
        out_specs=[pl.BlockSpec((A_CHUNK, 4 * A_WIDTH), lambda b, n: (rev(b, n), 0)), const3, const1],
        out_shape=[jax.ShapeDtypeStruct((t, 4 * A_WIDTH + 2 * B_WIDTH), F32),
                   jax.ShapeDtypeStruct((3, A_WIDTH), F32), jax.ShapeDtypeStruct((1, A_WIDTH), F32)],
        scratch_shapes=[pltpu.VMEM((A_HEADS, A_DK, A_DK), F32)],
        args=(proj, proj, proj, proj, states, lb_table, a_norm, do), exchange=exchange)


B_GDIM = B_WIDTH // B_GROUPS
B_ROWS = 512


def _gmlp_chunk(ubs, vbs, lngs, lnbs, ws, bcols):
    vs = [jax.nn.gelu(v) for v in vbs]
    mu = sum(jnp.sum(v, axis=-1, keepdims=True) for v in vs) * (1.0 / B_WIDTH)
    var = sum(jnp.sum(jnp.square(v - mu), axis=-1, keepdims=True) for v in vs) * (1.0 / B_WIDTH)
    rstd = lax.rsqrt(var + EPS)
    tril = (lax.broadcasted_iota(jnp.int32, (B_CHUNK, B_CHUNK), 0)
            >= lax.broadcasted_iota(jnp.int32, (B_CHUNK, B_CHUNK), 1))
    outs = []
    for g in range(B_GROUPS):
        vn = (vs[g] - mu) * rstd * lngs[g] + lnbs[g]
        w = jnp.where(tril, ws[g], 0.0).astype(BF16)
        outs.append(jax.nn.gelu(ubs[g]) * (_dot(w, vn.astype(BF16)) + bcols[g]))
    return outs


def _gmlp_args(u_ref, v_ref, lng_ref, lnb_ref, w_ref, bt_ref, rows):
    def groups(ref):
        return [ref[rows, g * B_GDIM:(g + 1) * B_GDIM] for g in range(B_GROUPS)]

    def vec(ref):
        return [ref[:, g * B_GDIM:(g + 1) * B_GDIM] for g in range(B_GROUPS)]

    return (groups(u_ref), groups(v_ref), vec(lng_ref), vec(lnb_ref),
            [w_ref[g] for g in range(B_GROUPS)], [bt_ref[:, g:g + 1] for g in range(B_GROUPS)])


def gmlp_fwd(proj, oa, ln_g, ln_b, w, bias_t, name, exchange=None):
    t = proj.shape[0]

    def body(u_ref, v_ref, oa_ref, lng_ref, lnb_ref, w_ref, bt_ref, o_ref):
        o_ref[:, 0:A_WIDTH] = oa_ref[...]
        for n in range(B_ROWS // B_CHUNK):
            rows = slice(n * B_CHUNK, (n + 1) * B_CHUNK)
            outs = _gmlp_chunk(*_gmlp_args(u_ref, v_ref, lng_ref, lnb_ref, w_ref, bt_ref, rows))
            for g, o in enumerate(outs):
                o_ref[rows, A_WIDTH + g * B_GDIM:A_WIDTH + (g + 1) * B_GDIM] = o.astype(BF16)

    vec = pl.BlockSpec((1, B_WIDTH), lambda i: (0, 0))
    return _call(
        body, name=name, grid=(t // B_ROWS,),
        in_specs=[pl.BlockSpec((B_ROWS, B_WIDTH), lambda i: (i, 4)), pl.BlockSpec((B_ROWS, B_WIDTH), lambda i: (i, 5)),
                  pl.BlockSpec((B_ROWS, A_WIDTH), lambda i: (i, 0)), vec, vec,
                  pl.BlockSpec((B_GROUPS, B_CHUNK, B_CHUNK), lambda i: (0, 0, 0)),
                  pl.BlockSpec((B_CHUNK, B_GROUPS), lambda i: (0, 0))],
        out_specs=[pl.BlockSpec((B_ROWS, A_WIDTH + B_WIDTH), lambda i: (i, 0))],
        out_shape=[jax.ShapeDtypeStruct((t, A_WIDTH + B_WIDTH), BF16)],
        args=(proj, proj, oa, ln_g, ln_b, w, bias_t), exchange=exchange)


def gmlp_bwd(proj, dmixin, ln_g, ln_b, w, bias_t, dproj, name, exchange=None):
    t = proj.shape[0]

    def body(u_ref, v_ref, do_ref, lng_ref, lnb_ref, w_ref, bt_ref, dp_in_ref,
             dp_ref, dlng_ref, dlnb_ref, dw_ref, dbt_ref):
        del dp_in_ref

        @pl.when(pl.program_id(0) == 0)
        def _():
            for ref in (dlng_ref, dlnb_ref, dw_ref, dbt_ref):
                ref[...] = jnp.zeros_like(ref)

        for n in range(B_ROWS // B_CHUNK):
            rows = slice(n * B_CHUNK, (n + 1) * B_CHUNK)
            _, vjp = jax.vjp(_gmlp_chunk, *_gmlp_args(u_ref, v_ref, lng_ref, lnb_ref, w_ref, bt_ref, rows))
            douts = [do_ref[rows, g * B_GDIM:(g + 1) * B_GDIM] for g in range(B_GROUPS)]
            dus, dvs, dlngs, dlnbs, dws, dbs = vjp(douts)
            for g in range(B_GROUPS):
                lanes = slice(g * B_GDIM, (g + 1) * B_GDIM)
                dp_ref[rows, lanes] = dus[g]
                dp_ref[rows, B_WIDTH + g * B_GDIM:B_WIDTH + (g + 1) * B_GDIM] = dvs[g]
                dlng_ref[:, lanes] += dlngs[g]
                dlnb_ref[:, lanes] += dlnbs[g]
                dw_ref[g] += dws[g]
                dbt_ref[:, g:g + 1] += dbs[g]

    vec = pl.BlockSpec((1, B_WIDTH), lambda i: (0, 0))
    wspec = pl.BlockSpec((B_GROUPS, B_CHUNK, B_CHUNK), lambda i: (0, 0, 0))
    bspec = pl.BlockSpec((B_CHUNK, B_GROUPS), lambda i: (0, 0))
    return _call(
        body, name=name, grid=(t // B_ROWS,),
        in_specs=[pl.BlockSpec((B_ROWS, B_WIDTH), lambda i: (i, 4)), pl.BlockSpec((B_ROWS, B_WIDTH), lambda i: (i, 5)),
                  pl.BlockSpec((B_ROWS, B_WIDTH), lambda i: (i, 1)), vec, vec, wspec, bspec,
                  pl.BlockSpec(memory_space=pl.ANY)],
        out_specs=[pl.BlockSpec((B_ROWS, 2 * B_WIDTH), lambda i: (i, 2)), vec, vec, wspec, bspec],
        out_shape=[jax.ShapeDtypeStruct(dproj.shape, F32), jax.ShapeDtypeStruct((1, B_WIDTH), F32),
                   jax.ShapeDtypeStruct((1, B_WIDTH), F32), jax.ShapeDtypeStruct((B_GROUPS, B_CHUNK, B_CHUNK), F32),
                   jax.ShapeDtypeStruct((B_CHUNK, B_GROUPS), F32)],
        aliases={7: 0}, args=(proj, proj, dmixin, ln_g, ln_b, w, bias_t, dproj), exchange=exchange)


C_FWD_BLOCKS = 4
C_BWD_BLOCKS = 4
C_PAIR = 2 * C_HEAD_DIM
C_PAIRS = C_HEADS // 2
C_SCALE = 1.0 / math.sqrt(C_HEAD_DIM)
C_ROT_DIM = 2 * C_ROT_HALF
ROPE_ROWS = 1024


def rope_tables(pos_col, name):
    t = pos_col.shape[0]

    def body(p_ref, c_ref, a_ref, b_ref):
        lane = jnp.bitwise_and(lax.broadcasted_iota(jnp.int32, (1, C_PAIR), 1), C_HEAD_DIM - 1)
        j = jnp.bitwise_and(lane, C_ROT_HALF - 1).astype(F32)
        inv = jnp.exp(j * (-math.log(ROPE_THETA) / C_ROT_HALF))
        ang = p_ref[...].astype(F32) * inv
        cos, sin = jnp.cos(ang), jnp.sin(ang)
        c_ref[...] = jnp.where(lane < C_ROT_DIM, cos, 1.0)
        a_ref[...] = jnp.where(lane < C_ROT_HALF, -sin, 0.0)
        b_ref[...] = jnp.where(jnp.logical_and(lane >= C_ROT_HALF, lane < C_ROT_DIM), sin, 0.0)

    tab = pl.BlockSpec((ROPE_ROWS, C_PAIR), lambda i: (i, 0))
    return pl.pallas_call(
        body, name=name, grid=(t // ROPE_ROWS,),
        in_specs=[pl.BlockSpec((ROPE_ROWS, 1), lambda i: (i, 0))],
        out_specs=[tab, tab, tab],
        out_shape=[jax.ShapeDtypeStruct((t, C_PAIR), F32)] * 3,
        compiler_params=_params(("arbitrary",)),
    )(pos_col)


def _rope(x, c, a, b):
    return x * c + pltpu.roll(x, C_PAIR - C_ROT_HALF, 1) * a + pltpu.roll(x, C_ROT_HALF, 1) * b


def _rope_t(d, c, a, b):
    return d * c + pltpu.roll(d * a, C_ROT_HALF, 1) + pltpu.roll(d * b, C_PAIR - C_ROT_HALF, 1)


def _attn_rows(idx, dil):
    nblk = SEQ // dil // C_BLOCK
    r, n = idx // nblk, idx % nblk
    start = r + dil * C_BLOCK * n
    prev = r + dil * C_BLOCK * jnp.maximum(n - 1, 0)
    if dil == 1:
        return pl.ds(pl.multiple_of(start, C_BLOCK), C_BLOCK), pl.ds(pl.multiple_of(prev, C_BLOCK), C_BLOCK), n > 0
    return pl.ds(start, C_BLOCK, stride=dil), pl.ds(prev, C_BLOCK, stride=dil), n > 0


def _head_masks():
    low = lax.broadcasted_iota(jnp.int32, (1, C_PAIR), 1) < C_HEAD_DIM
    return low, jnp.logical_not(low)


def _attn_mask(has_prev):
    i = jnp.bitwise_and(lax.broadcasted_iota(jnp.int32, (2 * C_BLOCK, 2 * C_BLOCK), 0), C_BLOCK - 1)
    j = lax.broadcasted_iota(jnp.int32, (2 * C_BLOCK, 2 * C_BLOCK), 1)
    return jnp.logical_or(j <= i, jnp.logical_and(j - C_BLOCK >= i, has_prev))


def _stack_heads(x):
    low, high = _head_masks()
    return jnp.concatenate([jnp.where(low, x, 0.0), jnp.where(high, x, 0.0)], axis=0)


def _unstack_heads(x):
    low, _ = _head_masks()
    return jnp.where(low, x[:C_BLOCK], x[C_BLOCK:])


def attn_fwd(qkv, cos_t, sin_a, sin_b, batch, name, exchange=None):
    t = qkv.shape[0]
    nbr = len(C_DILATIONS)

    def body(q_ref, k_ref, v_ref, c_ref, a_ref, b_ref, o_ref, l_ref, qs, ks, *stats):
        acc, mm, dd = stats[0:nbr], stats[nbr:2 * nbr], stats[2 * nbr:3 * nbr]
        c, a, b = c_ref[...], a_ref[...], b_ref[...]
        qs[...] = _rope(q_ref[...], c, a, b) * C_SCALE
        ks[...] = _rope(k_ref[...], c, a, b)
        def load(idx, dil):
            rows, prev, has_prev = _attn_rows(idx, dil)
            return rows, (has_prev, qs[rows, :], ks[rows, :], ks[prev, :], v_ref[rows, :], v_ref[prev, :])

        def scores(has_prev, q, k_own, k_prev, v_own, v_prev):
            k_cat = jnp.concatenate([k_own, k_prev], axis=0).astype(BF16)
            return jnp.where(_attn_mask(has_prev), _dot_nt(_stack_heads(q).astype(BF16), k_cat), NEG_BIG)

        def softmax(s):
            m = jnp.max(s, axis=-1, keepdims=True)
            p = jnp.exp(s - m)
            return p.astype(BF16), m, jnp.sum(p, axis=-1, keepdims=True)

        def values(pb, has_prev, q, k_own, k_prev, v_own, v_prev):
            low, high = _head_masks()
            v_cat = jnp.concatenate([v_own, v_prev], axis=0)
            p_wide = jnp.concatenate([pb[:C_BLOCK], pb[C_BLOCK:]], axis=1)
            v_tall = jnp.concatenate([jnp.where(low, v_cat, 0.0), jnp.where(high, v_cat, 0.0)], axis=0).astype(BF16)
            return _dot(p_wide, v_tall)

        for bi, dil in enumerate(C_DILATIONS):
            def pair(i, carry, bi=bi, dil=dil):
                low, _ = _head_masks()
                loaded = [load(C_FWD_BLOCKS * i + k, dil) for k in range(C_FWD_BLOCKS)]
                ss = [scores(*ops) for _, ops in loaded]
                sm = [softmax(s) for s in ss]
                pvs = [values(pb, *ops) for (pb, _, _), (_, ops) in zip(sm, loaded)]
                for (rows, _), (_, m, den), pv in zip(loaded, sm, pvs):
                    acc[bi][rows, :] = pv
                    mm[bi][rows, :] = jnp.where(low, m[:C_BLOCK], m[C_BLOCK:])
                    dd[bi][rows, :] = jnp.where(low, den[:C_BLOCK], den[C_BLOCK:])
                return carry

            lax.fori_loop(0, SEQ // C_BLOCK // C_FWD_BLOCKS, pair, 0)
        step = 256
        for r0 in range(0, SEQ, step):
            rr = slice(r0, r0 + step)
            ms = [mm[g][rr, :] for g in range(nbr)]
            m_all = functools.reduce(jnp.maximum, ms)
            ws = [jnp.exp(m - m_all) for m in ms]
            num = sum(acc[g][rr, :] * ws[g] for g in range(nbr))
            den = sum(dd[g][rr, :] * ws[g] for g in range(nbr))
            o_ref[rr, :] = (num / den).astype(BF16)
            l_ref[rr, :] = m_all + jnp.log(den)

    def col(k):
        return pl.BlockSpec((SEQ, C_PAIR), lambda b, p: (b, k * C_PAIRS + p))

    tab = pl.BlockSpec((SEQ, C_PAIR), lambda b, p: (b, 0))
    return _call(
        body, name=name, grid=(batch, C_PAIRS),
        in_specs=[col(0), col(1), col(2), tab, tab, tab],
        out_specs=[col(0), col(0)],
        out_shape=[jax.ShapeDtypeStruct((t, D_MODEL), BF16), jax.ShapeDtypeStruct((t, D_MODEL), F32)],
        scratch_shapes=[pltpu.VMEM((SEQ, C_PAIR), F32)] * (2 + 3 * nbr),
        args=(qkv, qkv, qkv, cos_t, sin_a, sin_b), exchange=exchange)


def attn_bwd(qkv, cos_t, sin_a, sin_b, o, lse, do, batch, name, exchange=None):
    t = qkv.shape[0]

    def body(q_ref, k_ref, v_ref, c_ref, a_ref, b_ref, o_ref, l_ref, do_ref, dq_ref, dk_ref, dv_ref,
             qs, ks, dqs, dks, dvs, dlt):
        c, a, b = c_ref[...], a_ref[...], b_ref[...]
        qs[...] = _rope(q_ref[...], c, a, b) * C_SCALE
        ks[...] = _rope(k_ref[...], c, a, b)
        prod = do_ref[...] * o_ref[...].astype(F32)
        low = lax.broadcasted_iota(jnp.int32, (1, C_PAIR), 1) < C_HEAD_DIM
        s_low = jnp.sum(jnp.where(low, prod, 0.0), axis=-1, keepdims=True)
        s_all = jnp.sum(prod, axis=-1, keepdims=True)
        dlt[...] = jnp.where(low, s_low, s_all - s_low)
        dqs[...] = jnp.zeros_like(dqs)
        dks[...] = jnp.zeros_like(dks)
        dvs[...] = jnp.zeros_like(dvs)
        def load(idx, dil):
            rows, prev, has_prev = _attn_rows(idx, dil)
            return (rows, prev), (has_prev, qs[rows, :], do_ref[rows, :], ks[rows, :], ks[prev, :],
                                  v_ref[rows, :], v_ref[prev, :], l_ref[rows, :], dlt[rows, :])

        def operands(has_prev, q, do, k_own, k_prev, v_own, v_prev, l_full, d_full):
            lcol = jnp.concatenate([l_full[:, 0:1], l_full[:, C_HEAD_DIM:C_HEAD_DIM + 1]], axis=0)
            dcol = jnp.concatenate([d_full[:, 0:1], d_full[:, C_HEAD_DIM:C_HEAD_DIM + 1]], axis=0)
            return (_stack_heads(q).astype(BF16), _stack_heads(do).astype(BF16),
                    jnp.concatenate([k_own, k_prev], axis=0).astype(BF16),
                    jnp.concatenate([v_own, v_prev], axis=0).astype(BF16), lcol, dcol, _attn_mask(has_prev))

        for dil in C_DILATIONS:
            def pair(i, carry, dil=dil):
                loaded = [load(C_BWD_BLOCKS * i + k, dil) for k in range(C_BWD_BLOCKS)]
                ops = [operands(*o) for _, o in loaded]
                ss = [_dot_nt(q_stack, k_cat) for q_stack, _, k_cat, _, _, _, _ in ops]
                dps = [_dot_nt(do_stack, v_cat) for _, do_stack, _, v_cat, _, _, _ in ops]
                ps = [jnp.exp(jnp.where(o[6], s, NEG_BIG) - o[4]) for s, o in zip(ss, ops)]
                dss = [(p * (dp - o[5])).astype(BF16) for p, dp, o in zip(ps, dps, ops)]
                dvs_ = [_dot_tn(p.astype(BF16), o[1]) for p, o in zip(ps, ops)]
                dks_ = [_dot_tn(ds, o[0]) for ds, o in zip(dss, ops)]
                dqs_ = [_unstack_heads(_dot(ds, o[2])) for ds, o in zip(dss, ops)]
                results = list(zip(dqs_, dks_, dvs_))
                for ((rows, prev), _), (dq, dk_cat, dv_cat) in zip(loaded, results):
                    dqs[rows, :] += dq
                    dks[rows, :] += dk_cat[:C_BLOCK]
                    dvs[rows, :] += dv_cat[:C_BLOCK]
                    dks[prev, :] += dk_cat[C_BLOCK:]
                    dvs[prev, :] += dv_cat[C_BLOCK:]
                return carry

            lax.fori_loop(0, SEQ // C_BLOCK // C_BWD_BLOCKS, pair, 0)
        dq_ref[...] = _rope_t(dqs[...] * C_SCALE, c, a, b)
        dk_ref[...] = _rope_t(dks[...], c, a, b)
        dv_ref[...] = dvs[...]

    def col(k):
        return pl.BlockSpec((SEQ, C_PAIR), lambda b, p: (b, k * C_PAIRS + p))

    tab = pl.BlockSpec((SEQ, C_PAIR), lambda b, p: (b, 0))
    out = jax.ShapeDtypeStruct((t, D_MODEL), F32)
    return _call(
        body, name=name, grid=(batch, C_PAIRS),
        in_specs=[col(0), col(1), col(2), tab, tab, tab, col(0), col(0), col(0)],
        out_specs=[col(0), col(0), col(0)],
        out_shape=[out, out, out],
        scratch_shapes=[pltpu.VMEM((SEQ, C_PAIR), F32)] * 6,
        args=(qkv, qkv, qkv, cos_t, sin_a, sin_b, o, lse, do), exchange=exchange)


def sibling_swap(arrays, name):
    n = len(arrays)

    def body(*refs):
        ins, outs = refs[:n], refs[n:2 * n]
        send_sems, recv_sems = refs[2 * n:]
        x, y, c, _ = _place()
        sends = []
        for a in range(n):
            cp = pltpu.make_async_remote_copy(
                src_ref=ins[a], dst_ref=outs[a], send_sem=send_sems.at[a], recv_sem=recv_sems.at[a],
                device_id=(x, y, 1 - c), device_id_type=MESH)
            cp.start()
            sends.append(cp)
        for cp in sends:
            cp.wait_recv()
        for cp in sends:
            cp.wait_send()

    return pl.pallas_call(
        body, name=name,
        in_specs=[ANY] * n, out_specs=[ANY] * n,
        out_shape=[jax.ShapeDtypeStruct(s.shape, s.dtype) for s in arrays],
        scratch_shapes=[pltpu.SemaphoreType.DMA((n,)), pltpu.SemaphoreType.DMA((n,))],
    )(*arrays)


def allreduce_small(slab, name):
    rows, lanes = slab.shape

    def body(x_ref, out_ref, gath, send_sems, recv_sems, local_sem):
        x, y, c, chips = _place()
        me, sibling = (x, y, c), (x, y, 1 - c)

        def slot(px, py, pc):
            return gath.at[4 * px + 2 * py + pc]

        def copy(k, block, to, src=None):
            return pltpu.make_async_remote_copy(
                src_ref=slot(*block) if src is None else src, dst_ref=slot(*block),
                send_sem=send_sems.at[k], recv_sem=recv_sems.at[k], device_id=to, device_id_type=MESH)

        mine = pltpu.make_async_copy(x_ref, slot(*me), local_sem)
        mine.start()
        first = [copy(0, me, sibling, src=x_ref)]
        first += [copy(1 + j, me, (*chip, c), src=x_ref) for j, chip in enumerate(chips)]
        for cp in first:
            cp.start()
        passed = [copy(4 + j, (*chip, c), sibling) for j, chip in enumerate(chips)]
        for j, chip in enumerate(chips):
            copy(1 + j, (*chip, c), me).wait_recv()
            passed[j].start()
        copy(0, sibling, me).wait_recv()
        for j, chip in enumerate(chips):
            copy(4 + j, (*chip, 1 - c), me).wait_recv()
        for cp in first + passed:
            cp.wait_send()
        mine.wait()
        total = gath[0]
        for d in range(1, N_DEV):
            total = total + gath[d]
        out_ref[...] = total

    return pl.pallas_call(
        body, name=name,
        in_specs=[pl.BlockSpec(memory_space=pltpu.VMEM)],
        out_specs=pl.BlockSpec(memory_space=pltpu.VMEM),
        out_shape=jax.ShapeDtypeStruct((rows, lanes), F32),
        scratch_shapes=[pltpu.VMEM((N_DEV, rows, lanes), F32),
                        pltpu.SemaphoreType.DMA((7,)), pltpu.SemaphoreType.DMA((7,)), pltpu.SemaphoreType.DMA],
    )(slab)


ELT_ROWS = 512


def reduce_slabs(r, name, part=0, parts=1, into=None):
    _, rows, cols = r.shape
    br = min(rows, ELT_ROWS)
    nblk = rows // br

    def body(r_ref, *rest):
        o_ref = rest[-1]
        o_ref[...] = ((r_ref[3].astype(F32) + r_ref[0].astype(F32)) + r_ref[1].astype(F32)) + r_ref[2].astype(F32)

    return pl.pallas_call(
        body, name=name, grid=(nblk,),
        in_specs=[pl.BlockSpec((N_CHIPS, br, cols), lambda i: (0, i, 0))] + ([] if into is None else [ANY]),
        out_specs=pl.BlockSpec((br, cols), lambda i: (part * nblk + i, 0)),
        out_shape=jax.ShapeDtypeStruct((parts * rows, cols), F32),
        input_output_aliases={} if into is None else {1: 0},
        compiler_params=_params(("arbitrary",)),
    )(*([r] if into is None else [r, into]))


def _adamw(w, g, m, v):
    m = ADAM_B1 * m + (1.0 - ADAM_B1) * g
    v = ADAM_B2 * v + (1.0 - ADAM_B2) * jnp.square(g)
    m_hat = m / (1.0 - ADAM_B1 ** ADAM_STEP)
    v_hat = v / (1.0 - ADAM_B2 ** ADAM_STEP)
    delta = -ADAM_LR * (m_hat / (jnp.sqrt(v_hat) + ADAM_EPS) + ADAM_WD * w)
    return delta, m, v


def adamw_big(w, s_mine, s_sibling, m, v, name):
    rows, cols = w.shape

    def body(w_ref, a_ref, b_ref, m_ref, v_ref, g_out, d_out, m_out, v_out):
        g = a_ref[...] + b_ref[...]
        g_out[...] = g
        d_out[...], m_out[...], v_out[...] = _adamw(w_ref[...], g, m_ref[...], v_ref[...])

    blk = pl.BlockSpec((min(rows, ELT_ROWS), cols), lambda i: (i, 0))
    out = jax.ShapeDtypeStruct((rows, cols), F32)
    return pl.pallas_call(
        body, name=name, grid=(rows // min(rows, ELT_ROWS),),
        in_specs=[blk] * 5, out_specs=[blk] * 4, out_shape=[out] * 4,
        compiler_params=_params(("arbitrary",)),
    )(w, s_mine, s_sibling, m, v)


def adamw_small(ws, gs, ms, vs, name):
    n = len(ws)

    def body(*refs):
        w_refs, g_refs, m_refs, v_refs = (refs[k * n:(k + 1) * n] for k in range(4))
        d_out, m_out, v_out = (refs[(4 + k) * n:(5 + k) * n] for k in range(3))
        for i in range(n):
            d_out[i][...], m_out[i][...], v_out[i][...] = _adamw(
                w_refs[i][...], g_refs[i][...], m_refs[i][...], v_refs[i][...])

    outs = [jax.ShapeDtypeStruct(w.shape, F32) for w in ws]
    res = pl.pallas_call(body, name=name, out_shape=outs * 3)(*ws, *gs, *ms, *vs)
    return res[:n], res[n:2 * n], res[2 * n:]


SLAB_LANES = 128
SLAB_ROW_ALIGN = 8


def _pack(parts):
    flat = jnp.concatenate([p.reshape(-1) for p in parts])
    rows = -(-flat.shape[0] // (SLAB_LANES * SLAB_ROW_ALIGN)) * SLAB_ROW_ALIGN
    flat = jnp.pad(flat, (0, rows * SLAB_LANES - flat.shape[0]))
    return flat.reshape(rows, SLAB_LANES)


def _unpack(slab, shapes):
    flat = slab.reshape(-1)
    out, pos = [], 0
    for s in shapes:
        size = math.prod(s)
        out.append(flat[pos:pos + size].reshape(s))
        pos += size
    return out


def kernel(x, positions, norm_mix_pre, norm_mix_post, norm_ffn_pre, norm_ffn_post, w_in_even, lb_table, a_norm, b_ln_g, b_ln_b, b_ws, b_bias, w_out_even, w_in_odd, w_out_odd, w_ff1, w_ff2, loss_target, m_norm_mix_pre, m_norm_mix_post, m_norm_ffn_pre, m_norm_ffn_post, m_w_in_even, m_lb_table, m_a_norm, m_b_ln_g, m_b_ln_b, m_b_ws, m_b_bias, m_w_out_even, m_w_in_odd, m_w_out_odd, m_w_ff1, m_w_ff2, v_norm_mix_pre, v_norm_mix_post, v_norm_ffn_pre, v_norm_ffn_post, v_w_in_even, v_lb_table, v_a_norm, v_b_ln_g, v_b_ln_b, v_b_ws, v_b_bias, v_w_out_even, v_w_in_odd, v_w_out_odd, v_w_ff1, v_w_ff2):
    batch = x.shape[0]
    t = batch * SEQ
    d = D_MODEL
    x0 = x.reshape(t, d)
    target = loss_target.reshape(t, d)

    def gain(p, layer):
        return p[layer:layer + 1]

    def gather(*shards):
        return _Exchange("gather", [w.astype(BF16) for w in shards])

    def scatter(*grads):
        return _Exchange("scatter", grads)

    (win_e,) = exchange_alone(gather(w_in_even[0]), "gather_in_even")
    bias_t = b_bias[0].T
    proj, h0, w1_0 = norm_matmul(x0, gain(norm_mix_pre, 0), win_e, "in_proj_even", exchange=gather(w_ff1[0]))
    oa, states, w2_0, wout_e = hgrn2_fwd(proj, lb_table, a_norm, batch, "hgrn2_fwd",
                                         exchange=gather(w_ff2[0], w_out_even[0]))
    (mixin,) = gmlp_fwd(proj, oa, b_ln_g, b_ln_b, b_ws[0], bias_t, "gmlp_fwd")
    mix0, x1 = out_proj(mixin, wout_e, x0, gain(norm_mix_post, 0), "out_proj_even")
    x2, hf0, a0, y0, win_o, wout_o = ffn_fwd(x1, gain(norm_ffn_pre, 0), w1_0, w2_0, gain(norm_ffn_post, 0),
                                             "ffn_fwd_0", exchange=gather(w_in_odd[0], w_out_odd[0]))
    qkv, h1 = norm_matmul(x2, gain(norm_mix_pre, 1), win_o, "in_proj_odd")
    cos_t, sin_a, sin_b = rope_tables(positions.reshape(t, 1), "rope_tables")
    ao, lse, w1_1, w2_1 = attn_fwd(qkv, cos_t, sin_a, sin_b, batch, "attn_fwd", exchange=gather(w_ff1[1], w_ff2[1]))
    mix1, x3 = out_proj(ao, wout_o, x2, gain(norm_mix_post, 1), "out_proj_odd")
    x4, hf1, a1, y1 = ffn_fwd(x3, gain(norm_ffn_pre, 1), w1_1, w2_1, gain(norm_ffn_post, 1), "ffn_fwd_1")
    dx4, loss_part = loss_grad(x4, target, "loss_grad")

    hc = D_FF // N_CHIPS
    dx3, dy1, da1, dg_fpre1, dg_fpost1 = ffn_bwd(
        dx4, x3, y1, a1, gain(norm_ffn_pre, 1), gain(norm_ffn_post, 1), w1_1, w2_1, "ffn_bwd_1")
    g_w1_1 = weight_grad(hf1, da1, "b", d, hc, False, "wgrad_ff1_1")
    g_w2_1 = weight_grad(a1, dy1, "a", hc, d, True, "wgrad_ff2_1")
    dmix1, dao, dg_mpost1 = out_proj_bwd(dx3, mix1, gain(norm_mix_post, 1), wout_o, "out_proj_bwd_odd")
    g_wout_o = weight_grad(ao, dmix1, "a", d // N_CHIPS, d, False, "wgrad_out_odd")
    dq, dk, dv, r_w1_1, r_w2_1, r_wout_o = attn_bwd(qkv, cos_t, sin_a, sin_b, ao, lse, dao, batch, "attn_bwd",
                                                    exchange=scatter(g_w1_1, g_w2_1, g_wout_o))
    dqkv = jnp.concatenate([dq, dk, dv], axis=1)
    dx2, dg_mpre1 = norm_matmul_bwd(dqkv, win_o, x2, gain(norm_mix_pre, 1), dx3, "in_proj_bwd_odd")
    g_win_o = weight_grad(h1, dqkv, "b", d, 3 * d // N_CHIPS, False, "wgrad_in_odd")
    dx1, dy0, da0, dg_fpre0, dg_fpost0, r_win_o = ffn_bwd(
        dx2, x1, y0, a0, gain(norm_ffn_pre, 0), gain(norm_ffn_post, 0), w1_0, w2_0, "ffn_bwd_0",
        exchange=scatter(g_win_o))
    g_w1_0 = weight_grad(hf0, da0, "b", d, hc, False, "wgrad_ff1_0")
    g_w2_0 = weight_grad(a0, dy0, "a", hc, d, True, "wgrad_ff2_0")
    dmix0, dmixin, dg_mpost0 = out_proj_bwd(dx1, mix0, gain(norm_mix_post, 0), wout_e, "out_proj_bwd_even")
    g_wout_e = weight_grad(mixin, dmix0, "a", d // N_CHIPS, d, False, "wgrad_out_even")
    dproj, d_lb, d_anorm, r_w1_0 = hgrn2_bwd(
        proj, states, lb_table, a_norm, dmixin, batch, "hgrn2_bwd", exchange=scatter(g_w1_0))
    dproj, d_lng, d_lnb, d_ws, d_bias_t, r_wout_e = gmlp_bwd(
        proj, dmixin, b_ln_g, b_ln_b, b_ws[0], bias_t, dproj, "gmlp_bwd", exchange=scatter(g_wout_e))
    g_win_e, r_w2_0 = weight_grad(h0, dproj, "b", d, 3 * d // N_CHIPS, False, "wgrad_in_even",
                                  exchange=scatter(g_w2_0))
    dx0, dg_mpre0, r_win_e = norm_matmul_bwd(dproj, win_e, x0, gain(norm_mix_pre, 0), dx1, "in_proj_bwd_even",
                                             exchange=scatter(g_win_e))
    grad_x = dx0.reshape(x.shape)

    s_w1 = reduce_slabs(r_w1_1, "reduce_ff1_1", part=1, parts=2)
    s_w1 = reduce_slabs(r_w1_0, "reduce_ff1_0", part=0, parts=2, into=s_w1)
    s_w2 = reduce_slabs(r_w2_1, "reduce_ff2_1", part=1, parts=2)
    s_w2 = reduce_slabs(r_w2_0, "reduce_ff2_0", part=0, parts=2, into=s_w2)
    sums = [reduce_slabs(r_win_e, "reduce_in_even"), reduce_slabs(r_wout_e, "reduce_out_even"),
            reduce_slabs(r_win_o, "reduce_in_odd"), reduce_slabs(r_wout_o, "reduce_out_odd"), s_w1, s_w2]
    sibling = sibling_swap(sums, "sibling_swap")
    big_w = [w_in_even, w_out_even, w_in_odd, w_out_odd, w_ff1, w_ff2]
    big_m = [m_w_in_even, m_w_out_even, m_w_in_odd, m_w_out_odd, m_w_ff1, m_w_ff2]
    big_v = [v_w_in_even, v_w_out_even, v_w_in_odd, v_w_out_odd, v_w_ff1, v_w_ff2]
    big = []
    for i, (w, m, v) in enumerate(zip(big_w, big_m, big_v)):
        two_d = (-1, w.shape[-1])
        res = adamw_big(w.reshape(two_d), sums[i], sibling[i], m.reshape(two_d), v.reshape(two_d), "adamw_big_%d" % i)
        big.append([r.reshape(w.shape) for r in res])

    small_w = [norm_mix_pre, norm_mix_post, norm_ffn_pre, norm_ffn_post, lb_table, a_norm, b_ln_g, b_ln_b, b_ws, b_bias]
    small_m = [m_norm_mix_pre, m_norm_mix_post, m_norm_ffn_pre, m_norm_ffn_post, m_lb_table, m_a_norm, m_b_ln_g,
               m_b_ln_b, m_b_ws, m_b_bias]
    small_v = [v_norm_mix_pre, v_norm_mix_post, v_norm_ffn_pre, v_norm_ffn_post, v_lb_table, v_a_norm, v_b_ln_g,
               v_b_ln_b, v_b_ws, v_b_bias]
    partial = [jnp.concatenate([dg_mpre0, dg_mpre1]), jnp.concatenate([dg_mpost0, dg_mpost1]),
               jnp.concatenate([dg_fpre0, dg_fpre1]), jnp.concatenate([dg_fpost0, dg_fpost1]),
               d_lb, d_anorm, d_lng, d_lnb, d_ws[None], d_bias_t.T[None]]
    *small_g, loss = _unpack(allreduce_small(_pack(partial + [loss_part]), "allreduce_small"),
                             [w.shape for w in small_w] + [()])
    small_d, small_nm, small_nv = adamw_small(small_w, small_g, small_m, small_v, "adamw_small")

    order = ["norm_mix_pre", "norm_mix_post", "norm_ffn_pre", "norm_ffn_post", "w_in_even", "lb_table", "a_norm",
             "b_ln_g", "b_ln_b", "b_ws", "b_bias", "w_out_even", "w_in_odd", "w_out_odd", "w_ff1", "w_ff2"]
    small_names = ["norm_mix_pre", "norm_mix_post", "norm_ffn_pre", "norm_ffn_post", "lb_table", "a_norm",
                   "b_ln_g", "b_ln_b", "b_ws", "b_bias"]
    big_names = ["w_in_even", "w_out_even", "w_in_odd", "w_out_odd", "w_ff1", "w_ff2"]
    grads, deltas, new_m, new_v = {}, {}, {}, {}
    for i, nm in enumerate(small_names):
        grads[nm], deltas[nm], new_m[nm], new_v[nm] = small_g[i], small_d[i], small_nm[i], small_nv[i]
    for i, nm in enumerate(big_names):
        grads[nm], deltas[nm], new_m[nm], new_v[nm] = big[i]
    return (loss, grad_x, *[grads[n] for n in order], *[deltas[n] for n in order],
            *[new_m[n] for n in order], *[new_v[n] for n in order])
```

```python
import functools
import math

import jax
import jax.numpy as jnp
from jax import lax
from jax.experimental import pallas as pl
from jax.experimental.pallas import tpu as pltpu

F32 = jnp.float32
BF16 = jnp.bfloat16
MESH = pl.DeviceIdType.MESH

D_MODEL = 1024
SEQ = 2048
D_FF = 4096
N_CHIPS = 4
A_WIDTH = 512
A_HEADS = 4
A_DK = 128
A_CHUNK = 64
A_SUB = 16
B_WIDTH = 512
B_GROUPS = 4
B_CHUNK = 128
C_HEADS = 16
C_HEAD_DIM = 64
C_ROT_HALF = 8
C_BLOCK = 128
C_DILATIONS = (1, 4, 16)
ROPE_THETA = 500000.0
EPS = 1e-6
ADAM_LR = 0.001
ADAM_B1 = 0.9
ADAM_B2 = 0.999
ADAM_EPS = 1e-08
ADAM_WD = 0.01
ADAM_STEP = 10

ROW_TILE = 512
FFN_ROWS = 1024
WGRAD_ROWS = 2048
VMEM_LIMIT = 56 * 1024 * 1024
NEG_BIG = -1e30


def _params(sem=None):
    return pltpu.CompilerParams(dimension_semantics=sem, vmem_limit_bytes=VMEM_LIMIT)


def _dot(a, b):
    return jnp.dot(a, b, preferred_element_type=F32)


def _dot_nt(a, b):
    return lax.dot_general(a, b, (((1,), (1,)), ((), ())), preferred_element_type=F32)


def _dot_tn(a, b):
    return lax.dot_general(a, b, (((0,), (0,)), ((), ())), preferred_element_type=F32)


def _rms(x, g):
    r = lax.rsqrt(jnp.mean(x * x, axis=-1, keepdims=True) + EPS)
    return x * r * g


def _rms_bwd(x, g, dy):
    r = lax.rsqrt(jnp.mean(x * x, axis=-1, keepdims=True) + EPS)
    xh = x * r
    dg = jnp.sum(dy * xh, axis=0, keepdims=True)
    dxh = dy * g
    dx = r * (dxh - xh * jnp.mean(dxh * xh, axis=-1, keepdims=True))
    return dx, dg


def _accumulate(ref, val, first):
    @pl.when(first)
    def _():
        ref[...] = val

    @pl.when(jnp.logical_not(first))
    def _():
        ref[...] += val


N_DEV = 8
ANY = pl.BlockSpec(memory_space=pl.ANY)


def _place():
    x, y, c = lax.axis_index("x"), lax.axis_index("y"), lax.axis_index("c")
    return x, y, c, [(1 - x, y), (x, 1 - y), (1 - x, 1 - y)]


class _Exchange:
    def __init__(self, kind, arrays):
        self.kind, self.arrays, self.n = kind, list(arrays), len(arrays)
        per_peer = pltpu.SemaphoreType.DMA((3 * self.n,))
        if kind == "gather":
            self.out_shape = [jax.ShapeDtypeStruct((N_CHIPS,) + a.shape, a.dtype) for a in self.arrays]
            self.scratch = [per_peer, per_peer, pltpu.SemaphoreType.DMA((self.n,)), per_peer, per_peer]
        else:
            self.out_shape = [jax.ShapeDtypeStruct(a.shape, a.dtype) for a in self.arrays]
            self.scratch = [per_peer, per_peer, pltpu.SemaphoreType.DMA((self.n,))]

    def _copies(self, ins, outs, sems):
        send_sems, recv_sems, local_sems = sems[:3]
        x, y, c, chips = _place()
        me = 2 * x + y
        local, remote = [], []
        for a in range(self.n):
            if self.kind == "gather":
                local.append(pltpu.make_async_copy(ins[a], outs[a].at[me], local_sems.at[a]))
                half = self.arrays[a].shape[0] // 2

                def rows(ref, core, half=half):
                    return ref.at[pl.ds(core * half, half)]
            else:
                local.append(pltpu.make_async_copy(ins[a].at[me], outs[a].at[3], local_sems.at[a]))
            for j, (px, py) in enumerate(chips):
                k = 3 * a + j
                peer = 2 * px + py

                def copy(src, dst, to, send_sem=send_sems.at[k], recv_sem=recv_sems.at[k]):
                    return pltpu.make_async_remote_copy(src_ref=src, dst_ref=dst, send_sem=send_sem, recv_sem=recv_sem,
                                                        device_id=to, device_id_type=MESH)

                if self.kind == "gather":
                    sent = copy(rows(ins[a], c), rows(outs[a].at[me], c), (px, py, c))
                    landed = copy(rows(ins[a], c), rows(outs[a].at[peer], c), (px, py, c))
                    on = dict(send_sem=sems[3].at[k], recv_sem=sems[4].at[k])
                    passed = copy(rows(outs[a].at[peer], c), rows(outs[a].at[peer], c), (x, y, 1 - c), **on)
                    handed = copy(rows(outs[a].at[peer], c), rows(outs[a].at[peer], 1 - c), (x, y, 1 - c), **on)
                    remote.append((sent, landed, passed, handed))
                else:
                    sent = copy(ins[a].at[peer], outs[a].at[j], (px, py, c))
                    remote.append((sent, sent, None, None))
        return local, remote

    def start(self, ins, outs, sems):
        local, remote = self._copies(ins, outs, sems)
        for cp in local:
            cp.start()
        for sent, _, _, _ in remote:
            sent.start()

    def finish(self, ins, outs, sems):
        local, remote = self._copies(ins, outs, sems)
        for _, landed, passed, _ in remote:
            landed.wait_recv()
            if passed is not None:
                passed.start()
        for sent, _, passed, handed in remote:
            if passed is not None:
                handed.wait_recv()
                passed.wait_send()
            sent.wait_send()
        for cp in local:
            cp.wait()


def _call(body, *, name, grid, in_specs, out_specs, out_shape, args, scratch_shapes=(), aliases=None, exchange=None):
    if exchange is None:
        return pl.pallas_call(
            body, name=name, grid=grid, in_specs=in_specs, out_specs=out_specs, out_shape=out_shape,
            scratch_shapes=list(scratch_shapes), input_output_aliases=aliases or {},
            compiler_params=_params(("arbitrary",) * len(grid)))(*args)
    n_in, n_out, n_scr, n_ex = len(in_specs), len(out_specs), len(scratch_shapes), exchange.n
    steps = grid

    def wrapped(*refs):
        ins, refs = refs[:n_in], refs[n_in:]
        ex_in, refs = refs[:n_ex], refs[n_ex:]
        outs, refs = refs[:n_out], refs[n_out:]
        ex_out, refs = refs[:n_ex], refs[n_ex:]
        scr, sems = refs[:n_scr], refs[n_scr:]
        first = functools.reduce(jnp.logical_and, [pl.program_id(k) == 0 for k in range(len(steps))])
        last = functools.reduce(jnp.logical_and, [pl.program_id(k) == steps[k] - 1 for k in range(len(steps))])

        @pl.when(first)
        def _():
            exchange.start(ex_in, ex_out, sems)

        body(*ins, *outs, *scr)

        @pl.when(last)
        def _():
            exchange.finish(ex_in, ex_out, sems)

    return pl.pallas_call(
        wrapped, name=name, grid=grid,
        in_specs=list(in_specs) + [ANY] * n_ex, out_specs=list(out_specs) + [ANY] * n_ex,
        out_shape=list(out_shape) + exchange.out_shape,
        scratch_shapes=list(scratch_shapes) + exchange.scratch, input_output_aliases=aliases or {},
        compiler_params=_params(("arbitrary",) * len(grid)))(*args, *exchange.arrays)


def exchange_alone(exchange, name):
    def body(*refs):
        n = exchange.n
        exchange.start(refs[:n], refs[n:2 * n], refs[2 * n:])
        exchange.finish(refs[:n], refs[n:2 * n], refs[2 * n:])

    return pl.pallas_call(
        body, name=name, in_specs=[ANY] * exchange.n, out_specs=[ANY] * exchange.n,
        out_shape=exchange.out_shape, scratch_shapes=exchange.scratch)(*exchange.arrays)


def norm_matmul(x, g, wg, name, exchange=None):
    t, d = x.shape
    nl = wg.shape[2]

    def body(x_ref, g_ref, w_ref, o_ref, h_ref):
        h = _rms(x_ref[...], g_ref[...]).astype(BF16)
        h_ref[...] = h
        for c in range(N_CHIPS):
            o_ref[:, c * nl:(c + 1) * nl] = _dot(h, w_ref[c])

    return _call(
        body, name=name, grid=(t // ROW_TILE,),
        in_specs=[pl.BlockSpec((ROW_TILE, d), lambda i: (i, 0)),
                  pl.BlockSpec((1, d), lambda i: (0, 0)),
                  pl.BlockSpec((N_CHIPS, d, nl), lambda i: (0, 0, 0))],
        out_specs=[pl.BlockSpec((ROW_TILE, N_CHIPS * nl), lambda i: (i, 0)),
                   pl.BlockSpec((ROW_TILE, d), lambda i: (i, 0))],
        out_shape=[jax.ShapeDtypeStruct((t, N_CHIPS * nl), F32), jax.ShapeDtypeStruct((t, d), BF16)],
        args=(x, g, wg), exchange=exchange)


def norm_matmul_bwd(dproj, wg, x, g, dres, name, exchange=None):
    t, d = x.shape
    nl = wg.shape[2]

    def body(dp_ref, w_ref, x_ref, g_ref, dres_ref, dx_ref, dg_ref):
        dh = _dot_nt(dp_ref[:, 0:nl].astype(BF16), w_ref[0])
        for c in range(1, N_CHIPS):
            dh += _dot_nt(dp_ref[:, c * nl:(c + 1) * nl].astype(BF16), w_ref[c])
        dx, dg = _rms_bwd(x_ref[...], g_ref[...], dh)
        dx_ref[...] = dres_ref[...] + dx
        _accumulate(dg_ref, dg, pl.program_id(0) == 0)

    row = pl.BlockSpec((ROW_TILE, d), lambda i: (i, 0))
    vec = pl.BlockSpec((1, d), lambda i: (0, 0))
    return _call(
        body, name=name, grid=(t // ROW_TILE,),
        in_specs=[pl.BlockSpec((ROW_TILE, N_CHIPS * nl), lambda i: (i, 0)),
                  pl.BlockSpec((N_CHIPS, d, nl), lambda i: (0, 0, 0)), row, vec, row],
        out_specs=[row, vec],
        out_shape=[jax.ShapeDtypeStruct((t, d), F32), jax.ShapeDtypeStruct((1, d), F32)],
        args=(dproj, wg, x, g, dres), exchange=exchange)


def out_proj(a, wg, x, g, name):
    t, d = x.shape
    kl = wg.shape[1]

    def body(a_ref, w_ref, x_ref, g_ref, mix_ref, xo_ref):
        acc = _dot(a_ref[:, 0:kl], w_ref[0])
        for c in range(1, N_CHIPS):
            acc += _dot(a_ref[:, c * kl:(c + 1) * kl], w_ref[c])
        mix_ref[...] = acc
        xo_ref[...] = x_ref[...] + _rms(acc, g_ref[...])

    row = pl.BlockSpec((ROW_TILE, d), lambda i: (i, 0))
    return pl.pallas_call(
        body, name=name, grid=(t // ROW_TILE,),
        in_specs=[row, pl.BlockSpec((N_CHIPS, kl, d), lambda i: (0, 0, 0)), row,
                  pl.BlockSpec((1, d), lambda i: (0, 0))],
        out_specs=[row, row],
        out_shape=[jax.ShapeDtypeStruct((t, d), F32), jax.ShapeDtypeStruct((t, d), F32)],
        compiler_params=_params(("arbitrary",)),
    )(a, wg, x, g)


def out_proj_bwd(dxo, mix, g, wg, name):
    t, d = mix.shape
    kl = wg.shape[1]

    def body(dxo_ref, mix_ref, g_ref, w_ref, dmix_ref, da_ref, dg_ref):
        dmix, dg = _rms_bwd(mix_ref[...], g_ref[...], dxo_ref[...])
        dmb = dmix.astype(BF16)
        dmix_ref[...] = dmb
        for c in range(N_CHIPS):
            da_ref[:, c * kl:(c + 1) * kl] = _dot_nt(dmb, w_ref[c])
        _accumulate(dg_ref, dg, pl.program_id(0) == 0)

    row = pl.BlockSpec((ROW_TILE, d), lambda i: (i, 0))
    vec = pl.BlockSpec((1, d), lambda i: (0, 0))
    return pl.pallas_call(
        body, name=name, grid=(t // ROW_TILE,),
        in_specs=[row, row, vec, pl.BlockSpec((N_CHIPS, kl, d), lambda i: (0, 0, 0))],
        out_specs=[row, row, vec],
        out_shape=[jax.ShapeDtypeStruct((t, d), BF16), jax.ShapeDtypeStruct((t, d), F32),
                   jax.ShapeDtypeStruct((1, d), F32)],
        compiler_params=_params(("arbitrary",)),
    )(dxo, mix, g, wg)


def ffn_fwd(x, gpre, w1g, w2g, gpost, name, exchange=None):
    t, d = x.shape
    hc = w1g.shape[2]

    def body(x_ref, gpre_ref, w1_ref, w2_ref, gpost_ref, xo_ref, h_ref, a_ref, y_ref, acc):
        c = pl.program_id(1)

        @pl.when(c == 0)
        def _():
            h_ref[...] = _rms(x_ref[...], gpre_ref[...]).astype(BF16)

        a = _dot(h_ref[...], w1_ref[...])
        a_ref[...] = a.astype(BF16)
        r = jnp.square(jnp.maximum(a, 0.0)).astype(BF16)
        _accumulate(acc, _dot(r, w2_ref[...]), c == 0)

        @pl.when(c == N_CHIPS - 1)
        def _():
            y = acc[...]
            y_ref[...] = y
            xo_ref[...] = x_ref[...] + _rms(y, gpost_ref[...])

    row = pl.BlockSpec((FFN_ROWS, d), lambda i, c: (i, 0))
    vec = pl.BlockSpec((1, d), lambda i, c: (0, 0))
    return _call(
        body, name=name, grid=(t // FFN_ROWS, N_CHIPS),
        in_specs=[row, vec,
                  pl.BlockSpec((None, d, hc), lambda i, c: (c, 0, 0)),
                  pl.BlockSpec((None, hc, d), lambda i, c: (c, 0, 0)), vec],
        out_specs=[row, row, pl.BlockSpec((FFN_ROWS, hc), lambda i, c: (i, c)), row],
        out_shape=[jax.ShapeDtypeStruct((t, d), F32), jax.ShapeDtypeStruct((t, d), BF16),
                   jax.ShapeDtypeStruct((t, N_CHIPS * hc), BF16), jax.ShapeDtypeStruct((t, d), F32)],
        scratch_shapes=[pltpu.VMEM((FFN_ROWS, d), F32)],
        args=(x, gpre, w1g, w2g, gpost), exchange=exchange)


def ffn_bwd(dxo, x, y, a, gpre, gpost, w1g, w2g, name, exchange=None):
    t, d = x.shape
    hc = w1g.shape[2]

    def body(dxo_ref, x_ref, y_ref, a_ref, gpre_ref, gpost_ref, w1_ref, w2_ref,
             dxi_ref, dy_ref, da_ref, dgpre_ref, dgpost_ref, acc):
        i, c = pl.program_id(0), pl.program_id(1)

        @pl.when(c == 0)
        def _():
            dy, dg = _rms_bwd(y_ref[...], gpost_ref[...], dxo_ref[...])
            dy_ref[...] = dy.astype(BF16)
            _accumulate(dgpost_ref, dg, i == 0)

        dr = _dot_nt(dy_ref[...], w2_ref[...])
        da = (dr * (2.0 * jnp.maximum(a_ref[...].astype(F32), 0.0))).astype(BF16)
        da_ref[...] = da
        _accumulate(acc, _dot_nt(da, w1_ref[...]), c == 0)

        @pl.when(c == N_CHIPS - 1)
        def _():
            dx, dg = _rms_bwd(x_ref[...], gpre_ref[...], acc[...])
            dxi_ref[...] = dxo_ref[...] + dx
            _accumulate(dgpre_ref, dg, i == 0)

    row = pl.BlockSpec((ROW_TILE, d), lambda i, c: (i, 0))
    vec = pl.BlockSpec((1, d), lambda i, c: (0, 0))
    hid = pl.BlockSpec((ROW_TILE, hc), lambda i, c: (i, c))
    return _call(
        body, name=name, grid=(t // ROW_TILE, N_CHIPS),
        in_specs=[row, row, row, hid, vec, vec,
                  pl.BlockSpec((None, d, hc), lambda i, c: (c, 0, 0)),
                  pl.BlockSpec((None, hc, d), lambda i, c: (c, 0, 0))],
        out_specs=[row, row, hid, vec, vec],
        out_shape=[jax.ShapeDtypeStruct((t, d), F32), jax.ShapeDtypeStruct((t, d), BF16),
                   jax.ShapeDtypeStruct((t, N_CHIPS * hc), BF16),
                   jax.ShapeDtypeStruct((1, d), F32), jax.ShapeDtypeStruct((1, d), F32)],
        scratch_shapes=[pltpu.VMEM((ROW_TILE, d), F32)],
        args=(dxo, x, y, a, gpre, gpost, w1g, w2g), exchange=exchange)


def weight_grad(a, b, chunked, bk, bn, relu2, name, exchange=None):
    t = a.shape[0]
    a_on = chunked == "a"
    rows = min(t, WGRAD_ROWS)
    n_steps = t // rows

    def body(a_ref, b_ref, o_ref, acc):
        s = pl.program_id(1)
        av = a_ref[...]
        if relu2:
            av = jnp.square(jnp.maximum(av.astype(F32), 0.0))
        _accumulate(acc, _dot_tn(av.astype(BF16), b_ref[...].astype(BF16)), s == 0)

        @pl.when(s == n_steps - 1)
        def _():
            o_ref[...] = acc[...].astype(BF16)

    res = _call(
        body, name=name, grid=(N_CHIPS, n_steps),
        in_specs=[pl.BlockSpec((rows, bk), (lambda c, s: (s, c)) if a_on else (lambda c, s: (s, 0))),
                  pl.BlockSpec((rows, bn), (lambda c, s: (s, 0)) if a_on else (lambda c, s: (s, c)))],
        out_specs=[pl.BlockSpec((None, bk, bn), lambda c, s: (c, 0, 0))],
        out_shape=[jax.ShapeDtypeStruct((N_CHIPS, bk, bn), BF16)],
        scratch_shapes=[pltpu.VMEM((bk, bn), F32)],
        args=(a, b), exchange=exchange)
    return res[0] if exchange is None else res


def loss_grad(xf, target, name):
    t, d = xf.shape

    def body(x_ref, t_ref, dy_ref, l_ref):
        e = x_ref[...] - t_ref[...]
        dy_ref[...] = e * (1.0 / d)
        part = jnp.sum(jnp.sum(e * e, axis=-1, keepdims=True), axis=0, keepdims=True) * (0.5 / d)
        _accumulate(l_ref, part, pl.program_id(0) == 0)

    row = pl.BlockSpec((ROW_TILE, d), lambda i: (i, 0))
    return pl.pallas_call(
        body, name=name, grid=(t // ROW_TILE,),
        in_specs=[row, row],
        out_specs=[row, pl.BlockSpec((1, 1), lambda i: (0, 0))],
        out_shape=[jax.ShapeDtypeStruct((t, d), F32), jax.ShapeDtypeStruct((1, 1), F32)],
        compiler_params=_params(("arbitrary",)),
    )(xf, target)


def _hgrn2_chunk(st, qs, fls, ivs, gls, l0, l1, l2, ng):
    nsub = len(qs)
    mx = jnp.maximum(jnp.maximum(l0, l1), l2)
    e0, e1, e2 = jnp.exp(l0 - mx), jnp.exp(l1 - mx), jnp.exp(l2 - mx)
    lb = e0 / (e0 + e1 + e2)
    rows = lax.broadcasted_iota(jnp.int32, (A_SUB, A_SUB), 0)
    cols = lax.broadcasted_iota(jnp.int32, (A_SUB, A_SUB), 1)
    tri = (rows >= cols).astype(F32)
    keep = (lax.broadcasted_iota(jnp.int32, (A_SUB, A_SUB, A_DK), 0)
            >= lax.broadcasted_iota(jnp.int32, (A_SUB, A_SUB, A_DK), 1))
    base = jnp.zeros_like(l0)
    bases, gs, ks, qfs = [], [], [], []
    for i in range(nsub):
        f = lb + (1.0 - lb) * jax.nn.sigmoid(fls[i])
        logf = jnp.log(f)
        bases.append(base)
        gs.append(base + jnp.dot(tri, logf, precision=lax.Precision.HIGHEST, preferred_element_type=F32))
        base = base + jnp.sum(logf, axis=0, keepdims=True)
        ks.append(1.0 - f)
        qfs.append(jax.nn.silu(qs[i]))
    g_last = base
    stb = st.astype(BF16)
    outs = []
    for i in range(nsub):
        o = _dot_nt((qfs[i] * jnp.exp(gs[i])).astype(BF16), stb)
        if i > 0:
            qt = (qfs[i] * jnp.exp(gs[i] - bases[i])).astype(BF16)
            kk = jnp.concatenate([ks[j] * jnp.exp(bases[i] - gs[j]) for j in range(i)], axis=0).astype(BF16)
            vv = jnp.concatenate(ivs[:i], axis=0).astype(BF16)
            o = o + _dot(_dot_nt(qt, kk).astype(BF16), vv)
        dec = jnp.exp(jnp.where(keep, gs[i][:, None, :] - gs[i][None, :, :], NEG_BIG))
        s_diag = jnp.sum(qfs[i][:, None, :] * ks[i][None, :, :] * dec, axis=-1)
        o = o + _dot(s_diag.astype(BF16), ivs[i].astype(BF16))
        o = o * lax.rsqrt(jnp.mean(o * o, axis=-1, keepdims=True) + EPS) * ng
        outs.append(o * jax.nn.silu(gls[i]))
    kdec = jnp.concatenate([ks[j] * jnp.exp(g_last - gs[j]) for j in range(nsub)], axis=0).astype(BF16)
    vall = jnp.concatenate(ivs, axis=0).astype(BF16)
    new_st = st * jnp.exp(g_last) + _dot_tn(vall, kdec)
    return new_st, outs


A_MAX_LOG_DECAY = 80.0


def _split3(x):
    hi = x.astype(BF16)
    r1 = x - hi.astype(F32)
    mid = r1.astype(BF16)
    return hi, mid, (r1 - mid.astype(F32)).astype(BF16)


def _tri_matmul(x, transpose):
    n = x.shape[0]
    r = lax.broadcasted_iota(jnp.int32, (n, n), 0)
    c = lax.broadcasted_iota(jnp.int32, (n, n), 1)
    tri = ((r <= c) if transpose else (r >= c)).astype(BF16)
    hi, mid, lo = _split3(x)
    return (_dot(tri, lo) + _dot(tri, mid)) + _dot(tri, hi)


@jax.custom_vjp
def _cumsum_rows(x):
    return _tri_matmul(x, False)


def _cumsum_rows_fwd(x):
    return _tri_matmul(x, False), None


def _cumsum_rows_bwd(_, dy):
    return (_tri_matmul(dy, True),)


_cumsum_rows.defvjp(_cumsum_rows_fwd, _cumsum_rows_bwd)


def _lower_bound(l0, l1, l2):
    mx = jnp.maximum(jnp.maximum(l0, l1), l2)
    e0, e1, e2 = jnp.exp(l0 - mx), jnp.exp(l1 - mx), jnp.exp(l2 - mx)
    return e0 / (e0 + e1 + e2)


def _b(x):
    return x.astype(BF16)


@jax.custom_vjp
def _mm(a, b):
    return _dot(_b(a), _b(b))


_mm.defvjp(lambda a, b: (_mm(a, b), (a, b)),
           lambda res, d: (_dot_nt(_b(d), _b(res[1])), _dot_tn(_b(res[0]), _b(d))))


@jax.custom_vjp
def _mm_nt(a, b):
    return _dot_nt(_b(a), _b(b))


_mm_nt.defvjp(lambda a, b: (_mm_nt(a, b), (a, b)),
              lambda res, d: (_dot(_b(d), _b(res[1])), _dot_tn(_b(d), _b(res[0]))))


def _dot_split(dot, a, b):
    ah, bh = _b(a), _b(b)
    al, bl = _b(a - ah.astype(F32)), _b(b - bh.astype(F32))
    return (dot(ah, bl) + dot(al, bh)) + dot(ah, bh)


@jax.custom_vjp
def _mm_scores(a, b):
    return _dot_nt(_b(a), _b(b))


_mm_scores.defvjp(lambda a, b: (_mm_scores(a, b), (a, b)),
                  lambda res, d: (_dot_split(_dot, d, res[1]), _dot_split(_dot_tn, d, res[0])))


@jax.custom_vjp
def _mm_tn(a, b):
    return _dot_tn(_b(a), _b(b))


_mm_tn.defvjp(lambda a, b: (_mm_tn(a, b), (a, b)),
              lambda res, d: (_dot_nt(_b(res[1]), _b(d)), _dot(_b(res[0]), _b(d))))


@jax.custom_vjp
def _split_heads(x):
    return tuple(x[:, h * A_DK:(h + 1) * A_DK] for h in range(A_HEADS))


def _split_heads_fwd(x):
    return _split_heads(x), None


def _split_heads_bwd(_, parts):
    return (jnp.concatenate(parts, axis=1),)


_split_heads.defvjp(_split_heads_fwd, _split_heads_bwd)


def _hgrn2_chunk_fast(sts, q, fl, iv, gl, l0, l1, l2, ng):
    lb = _lower_bound(l0, l1, l2)
    f = lb + (1.0 - lb) * jax.nn.sigmoid(fl)
    return _hgrn2_fast_core(sts, q, f, jnp.log(f), iv, gl, ng)


def _hgrn2_fast_core(sts, q, f, logf, iv, gl, ng):
    g = _cumsum_rows(logf)
    g_last = jnp.sum(logf, axis=0, keepdims=True)
    k = 1.0 - f
    qgs = _split_heads(jax.nn.silu(q) * jnp.exp(g))
    kgs = _split_heads(k * jnp.exp(-g))
    kds = _split_heads(k * jnp.exp(g_last - g))
    ivs = _split_heads(iv)
    decays = _split_heads(jnp.exp(g_last))
    n = q.shape[0]
    causal = lax.broadcasted_iota(jnp.int32, (n, n), 0) >= lax.broadcasted_iota(jnp.int32, (n, n), 1)
    raw = [_mm_scores(qg, kg) for qg, kg in zip(qgs, kgs)]
    inter = [_mm_nt(qg, st) for qg, st in zip(qgs, sts)]
    scores = [jnp.where(causal, s, 0.0) for s in raw]
    os = [a + _mm(s, v) for a, s, v in zip(inter, scores, ivs)]
    new_sts = [st * d + _mm_tn(v, kd) for st, d, v, kd in zip(sts, decays, ivs, kds)]
    os = [o * lax.rsqrt(jnp.mean(o * o, axis=-1, keepdims=True) + EPS) for o in os]
    return new_sts, jnp.concatenate(os, axis=1) * ng * jax.nn.silu(gl)


A_STEP_CHUNKS = 4


def _chunk_rows(j):
    return pl.ds(pl.multiple_of(j * A_CHUNK, A_CHUNK), A_CHUNK)


def _sub_rows(j, i):
    return pl.ds(pl.multiple_of(j * A_CHUNK + i * A_SUB, A_SUB), A_SUB)


def _sub_blocks(ref, head, j):
    lanes = slice(head * A_DK, (head + 1) * A_DK)
    return [ref[_sub_rows(j, i), lanes] for i in range(A_CHUNK // A_SUB)]


def hgrn2_fwd(proj, lb_table, a_norm, batch, name, exchange=None):
    t = proj.shape[0]
    n_steps = t // batch // (A_CHUNK * A_STEP_CHUNKS)
    rows = A_CHUNK * A_STEP_CHUNKS

    def body(q_ref, f_ref, i_ref, g_ref, lb_ref, ng_ref, o_ref, st_ref, dec_ref, st):
        @pl.when(pl.program_id(1) == 0)
        def _():
            st[...] = jnp.zeros_like(st)

        def chunk(j, carry):
            r = _chunk_rows(j)
            st_ref[j] = st[...]
            lb = _lower_bound(lb_ref[0:1, :], lb_ref[1:2, :], lb_ref[2:3, :])
            f = lb + (1.0 - lb) * jax.nn.sigmoid(f_ref[r, :])
            logf = jnp.log(f)
            decay = jnp.sum(logf, axis=0, keepdims=True)
            dec_ref[j] = decay
            mild = jnp.min(decay) >= -A_MAX_LOG_DECAY

            @pl.when(mild)
            def _():
                new_sts, o = _hgrn2_fast_core([st[h] for h in range(A_HEADS)], q_ref[r, :], f, logf,
                                              i_ref[r, :], g_ref[r, :], ng_ref[...])
                for h in range(A_HEADS):
                    st[h] = new_sts[h]
                o_ref[r, :] = o.astype(BF16)

            @pl.when(jnp.logical_not(mild))
            def _():
                for h in range(A_HEADS):
                    lanes = slice(h * A_DK, (h + 1) * A_DK)
                    new_st, outs = _hgrn2_chunk(
                        st[h], _sub_blocks(q_ref, h, j), _sub_blocks(f_ref, h, j), _sub_blocks(i_ref, h, j),
                        _sub_blocks(g_ref, h, j), lb_ref[0:1, lanes], lb_ref[1:2, lanes], lb_ref[2:3, lanes],
                        ng_ref[:, lanes])
                    st[h] = new_st
                    for i, o in enumerate(outs):
                        o_ref[_sub_rows(j, i), lanes] = o.astype(BF16)

            return carry

        lax.fori_loop(0, A_STEP_CHUNKS, chunk, 0)

    def part(k):
        return pl.BlockSpec((rows, A_WIDTH), lambda b, n: (b * n_steps + n, k))

    return _call(
        body, name=name, grid=(batch, n_steps),
        in_specs=[part(0), part(1), part(2), part(3),
                  pl.BlockSpec((3, A_WIDTH), lambda b, n: (0, 0)), pl.BlockSpec((1, A_WIDTH), lambda b, n: (0, 0))],
        out_specs=[part(0),
                   pl.BlockSpec((A_STEP_CHUNKS, A_HEADS, A_DK, A_DK), lambda b, n: (b * n_steps + n, 0, 0, 0)),
                   pl.BlockSpec((A_STEP_CHUNKS, 1, A_WIDTH), lambda b, n: (b * n_steps + n, 0, 0))],
        out_shape=[jax.ShapeDtypeStruct((t, A_WIDTH), BF16),
                   jax.ShapeDtypeStruct((t // A_CHUNK, A_HEADS, A_DK, A_DK), F32),
                   jax.ShapeDtypeStruct((t // A_CHUNK, 1, A_WIDTH), F32)],
        scratch_shapes=[pltpu.VMEM((A_HEADS, A_DK, A_DK), F32)],
        args=(proj, proj, proj, proj, lb_table, a_norm), exchange=exchange)


def hgrn2_bwd(proj, states, decays, lb_table, a_norm, do, batch, name, exchange=None):
    t = proj.shape[0]
    n_steps = t // batch // (A_CHUNK * A_STEP_CHUNKS)
    rows = A_CHUNK * A_STEP_CHUNKS

    def body(q_ref, f_ref, i_ref, g_ref, st_ref, dec_ref, lb_ref, ng_ref, do_ref, dp_ref, dlb_ref, dng_ref, dst):
        @pl.when(jnp.logical_and(pl.program_id(0) == 0, pl.program_id(1) == 0))
        def _():
            dlb_ref[...] = jnp.zeros_like(dlb_ref)
            dng_ref[...] = jnp.zeros_like(dng_ref)

        @pl.when(pl.program_id(1) == 0)
        def _():
            dst[...] = jnp.zeros_like(dst)

        def chunk(jj, carry):
            j = A_STEP_CHUNKS - 1 - jj
            r = _chunk_rows(j)
            mild = jnp.min(dec_ref[j]) >= -A_MAX_LOG_DECAY

            @pl.when(mild)
            def _():
                _, vjp = jax.vjp(
                    _hgrn2_chunk_fast, [st_ref[j, h] for h in range(A_HEADS)], q_ref[r, :], f_ref[r, :],
                    i_ref[r, :], g_ref[r, :], lb_ref[0:1, :], lb_ref[1:2, :], lb_ref[2:3, :], ng_ref[...])
                d_sts, dq, df, di, dg, dl0, dl1, dl2, dng = vjp(
                    ([dst[h] for h in range(A_HEADS)], do_ref[r, :].astype(F32)))
                for h in range(A_HEADS):
                    dst[h] = d_sts[h]
                for k, part in enumerate((dq, df, di, dg)):
                    dp_ref[r, k * A_WIDTH:(k + 1) * A_WIDTH] = part
                for row, val in enumerate((dl0, dl1, dl2)):
                    dlb_ref[row:row + 1, :] += val
                dng_ref[...] += dng

            @pl.when(jnp.logical_not(mild))
            def _():
                for h in range(A_HEADS):
                    lanes = slice(h * A_DK, (h + 1) * A_DK)
                    _, vjp = jax.vjp(
                        _hgrn2_chunk, st_ref[j, h], _sub_blocks(q_ref, h, j), _sub_blocks(f_ref, h, j),
                        _sub_blocks(i_ref, h, j), _sub_blocks(g_ref, h, j), lb_ref[0:1, lanes], lb_ref[1:2, lanes],
                        lb_ref[2:3, lanes], ng_ref[:, lanes])
                    douts = [x.astype(F32) for x in _sub_blocks(do_ref, h, j)]
                    d_st, dqs, dfs, dis, dgs, dl0, dl1, dl2, dng = vjp((dst[h], douts))
                    dst[h] = d_st
                    for k, parts in enumerate((dqs, dfs, dis, dgs)):
                        for i in range(A_CHUNK // A_SUB):
                            dp_ref[_sub_rows(j, i), k * A_WIDTH + h * A_DK:k * A_WIDTH + (h + 1) * A_DK] = parts[i]
                    for row, val in enumerate((dl0, dl1, dl2)):
                        dlb_ref[row:row + 1, lanes] += val
                    dng_ref[:, lanes] += dng

            return carry

        lax.fori_loop(0, A_STEP_CHUNKS, chunk, 0)

    def rev(b, n):
        return b * n_steps + (n_steps - 1 - n)

    def part(k):
        return pl.BlockSpec((rows, A_WIDTH), lambda b, n: (rev(b, n), k))

    const3 = pl.BlockSpec((3, A_WIDTH), lambda b, n: (0, 0))
    const1 = pl.BlockSpec((1, A_WIDTH), lambda b, n: (0, 0))
    return _call(
        body, name=name, grid=(batch, n_steps),
        in_specs=[part(0), part(1), part(2), part(3),
                  pl.BlockSpec((A_STEP_CHUNKS, A_HEADS, A_DK, A_DK), lambda b, n: (rev(b, n), 0, 0, 0)),
                  pl.BlockSpec((A_STEP_CHUNKS, 1, A_WIDTH), lambda b, n: (rev(b, n), 0, 0)),
                  const3, const1, part(0)],
        out_specs=[pl.BlockSpec((rows, 4 * A_WIDTH), lambda b, n: (rev(b, n), 0)), const3, const1],
        out_shape=[jax.ShapeDtypeStruct((t, 4 * A_WIDTH + 2 * B_WIDTH), F32),
                   jax.ShapeDtypeStruct((3, A_WIDTH), F32), jax.ShapeDtypeStruct((1, A_WIDTH), F32)],
        scratch_shapes=[pltpu.VMEM((A_HEADS, A_DK, A_DK), F32)],
        args=(proj, proj, proj, proj, states, decays, lb_table, a_norm, do), exchange=exchange)


B_GDIM = B_WIDTH // B_GROUPS
B_ROWS = 512


def _gmlp_chunk(ubs, vbs, lngs, lnbs, ws, bcols):
    vs = [jax.nn.gelu(v) for v in vbs]
    mu = sum(jnp.sum(v, axis=-1, keepdims=True) for v in vs) * (1.0 / B_WIDTH)
    var = sum(jnp.sum(jnp.square(v - mu), axis=-1, keepdims=True) for v in vs) * (1.0 / B_WIDTH)
    rstd = lax.rsqrt(var + EPS)
    tril = (lax.broadcasted_iota(jnp.int32, (B_CHUNK, B_CHUNK), 0)
            >= lax.broadcasted_iota(jnp.int32, (B_CHUNK, B_CHUNK), 1))
    outs = []
    for g in range(B_GROUPS):
        vn = (vs[g] - mu) * rstd * lngs[g] + lnbs[g]
        w = jnp.where(tril, ws[g], 0.0).astype(BF16)
        outs.append(jax.nn.gelu(ubs[g]) * (_dot(w, vn.astype(BF16)) + bcols[g]))
    return outs


def _gmlp_args(u_ref, v_ref, lng_ref, lnb_ref, w_ref, bt_ref, rows):
    def groups(ref):
        return [ref[rows, g * B_GDIM:(g + 1) * B_GDIM] for g in range(B_GROUPS)]

    def vec(ref):
        return [ref[:, g * B_GDIM:(g + 1) * B_GDIM] for g in range(B_GROUPS)]

    return (groups(u_ref), groups(v_ref), vec(lng_ref), vec(lnb_ref),
            [w_ref[g] for g in range(B_GROUPS)], [bt_ref[:, g:g + 1] for g in range(B_GROUPS)])


def gmlp_fwd(proj, oa, ln_g, ln_b, w, bias_t, name, exchange=None):
    t = proj.shape[0]

    def body(u_ref, v_ref, oa_ref, lng_ref, lnb_ref, w_ref, bt_ref, o_ref):
        o_ref[:, 0:A_WIDTH] = oa_ref[...]
        for n in range(B_ROWS // B_CHUNK):
            rows = slice(n * B_CHUNK, (n + 1) * B_CHUNK)
            outs = _gmlp_chunk(*_gmlp_args(u_ref, v_ref, lng_ref, lnb_ref, w_ref, bt_ref, rows))
            for g, o in enumerate(outs):
                o_ref[rows, A_WIDTH + g * B_GDIM:A_WIDTH + (g + 1) * B_GDIM] = o.astype(BF16)

    vec = pl.BlockSpec((1, B_WIDTH), lambda i: (0, 0))
    return _call(
        body, name=name, grid=(t // B_ROWS,),
        in_specs=[pl.BlockSpec((B_ROWS, B_WIDTH), lambda i: (i, 4)), pl.BlockSpec((B_ROWS, B_WIDTH), lambda i: (i, 5)),
                  pl.BlockSpec((B_ROWS, A_WIDTH), lambda i: (i, 0)), vec, vec,
                  pl.BlockSpec((B_GROUPS, B_CHUNK, B_CHUNK), lambda i: (0, 0, 0)),
                  pl.BlockSpec((B_CHUNK, B_GROUPS), lambda i: (0, 0))],
        out_specs=[pl.BlockSpec((B_ROWS, A_WIDTH + B_WIDTH), lambda i: (i, 0))],
        out_shape=[jax.ShapeDtypeStruct((t, A_WIDTH + B_WIDTH), BF16)],
        args=(proj, proj, oa, ln_g, ln_b, w, bias_t), exchange=exchange)


def gmlp_bwd(proj, dmixin, ln_g, ln_b, w, bias_t, dproj, name, exchange=None):
    t = proj.shape[0]

    def body(u_ref, v_ref, do_ref, lng_ref, lnb_ref, w_ref, bt_ref, dp_in_ref,
             dp_ref, dlng_ref, dlnb_ref, dw_ref, dbt_ref):
        del dp_in_ref

        @pl.when(pl.program_id(0) == 0)
        def _():
            for ref in (dlng_ref, dlnb_ref, dw_ref, dbt_ref):
                ref[...] = jnp.zeros_like(ref)

        for n in range(B_ROWS // B_CHUNK):
            rows = slice(n * B_CHUNK, (n + 1) * B_CHUNK)
            _, vjp = jax.vjp(_gmlp_chunk, *_gmlp_args(u_ref, v_ref, lng_ref, lnb_ref, w_ref, bt_ref, rows))
            douts = [do_ref[rows, g * B_GDIM:(g + 1) * B_GDIM] for g in range(B_GROUPS)]
            dus, dvs, dlngs, dlnbs, dws, dbs = vjp(douts)
            for g in range(B_GROUPS):
                lanes = slice(g * B_GDIM, (g + 1) * B_GDIM)
                dp_ref[rows, lanes] = dus[g]
                dp_ref[rows, B_WIDTH + g * B_GDIM:B_WIDTH + (g + 1) * B_GDIM] = dvs[g]
                dlng_ref[:, lanes] += dlngs[g]
                dlnb_ref[:, lanes] += dlnbs[g]
                dw_ref[g] += dws[g]
                dbt_ref[:, g:g + 1] += dbs[g]

    vec = pl.BlockSpec((1, B_WIDTH), lambda i: (0, 0))
    wspec = pl.BlockSpec((B_GROUPS, B_CHUNK, B_CHUNK), lambda i: (0, 0, 0))
    bspec = pl.BlockSpec((B_CHUNK, B_GROUPS), lambda i: (0, 0))
    return _call(
        body, name=name, grid=(t // B_ROWS,),
        in_specs=[pl.BlockSpec((B_ROWS, B_WIDTH), lambda i: (i, 4)), pl.BlockSpec((B_ROWS, B_WIDTH), lambda i: (i, 5)),
                  pl.BlockSpec((B_ROWS, B_WIDTH), lambda i: (i, 1)), vec, vec, wspec, bspec,
                  pl.BlockSpec(memory_space=pl.ANY)],
        out_specs=[pl.BlockSpec((B_ROWS, 2 * B_WIDTH), lambda i: (i, 2)), vec, vec, wspec, bspec],
        out_shape=[jax.ShapeDtypeStruct(dproj.shape, F32), jax.ShapeDtypeStruct((1, B_WIDTH), F32),
                   jax.ShapeDtypeStruct((1, B_WIDTH), F32), jax.ShapeDtypeStruct((B_GROUPS, B_CHUNK, B_CHUNK), F32),
                   jax.ShapeDtypeStruct((B_CHUNK, B_GROUPS), F32)],
        aliases={7: 0}, args=(proj, proj, dmixin, ln_g, ln_b, w, bias_t, dproj), exchange=exchange)


C_FWD_BLOCKS = 4
C_BWD_BLOCKS = 4
C_PAIR = 2 * C_HEAD_DIM
C_PAIRS = C_HEADS // 2
C_SCALE = 1.0 / math.sqrt(C_HEAD_DIM)
C_ROT_DIM = 2 * C_ROT_HALF
ROPE_ROWS = 1024


def rope_tables(pos_col, name):
    t = pos_col.shape[0]

    def body(p_ref, c_ref, a_ref, b_ref):
        lane = jnp.bitwise_and(lax.broadcasted_iota(jnp.int32, (1, C_PAIR), 1), C_HEAD_DIM - 1)
        j = jnp.bitwise_and(lane, C_ROT_HALF - 1).astype(F32)
        inv = jnp.exp(j * (-math.log(ROPE_THETA) / C_ROT_HALF))
        ang = p_ref[...].astype(F32) * inv
        cos, sin = jnp.cos(ang), jnp.sin(ang)
        c_ref[...] = jnp.where(lane < C_ROT_DIM, cos, 1.0)
        a_ref[...] = jnp.where(lane < C_ROT_HALF, -sin, 0.0)
        b_ref[...] = jnp.where(jnp.logical_and(lane >= C_ROT_HALF, lane < C_ROT_DIM), sin, 0.0)

    tab = pl.BlockSpec((ROPE_ROWS, C_PAIR), lambda i: (i, 0))
    return pl.pallas_call(
        body, name=name, grid=(t // ROPE_ROWS,),
        in_specs=[pl.BlockSpec((ROPE_ROWS, 1), lambda i: (i, 0))],
        out_specs=[tab, tab, tab],
        out_shape=[jax.ShapeDtypeStruct((t, C_PAIR), F32)] * 3,
        compiler_params=_params(("arbitrary",)),
    )(pos_col)


def _rope(x, c, a, b):
    return x * c + pltpu.roll(x, C_PAIR - C_ROT_HALF, 1) * a + pltpu.roll(x, C_ROT_HALF, 1) * b


def _rope_t(d, c, a, b):
    return d * c + pltpu.roll(d * a, C_ROT_HALF, 1) + pltpu.roll(d * b, C_PAIR - C_ROT_HALF, 1)


def _attn_rows(idx, dil):
    nblk = SEQ // dil // C_BLOCK
    r, n = idx // nblk, idx % nblk
    start = r + dil * C_BLOCK * n
    prev = r + dil * C_BLOCK * jnp.maximum(n - 1, 0)
    if dil == 1:
        return pl.ds(pl.multiple_of(start, C_BLOCK), C_BLOCK), pl.ds(pl.multiple_of(prev, C_BLOCK), C_BLOCK), n > 0
    return pl.ds(start, C_BLOCK, stride=dil), pl.ds(prev, C_BLOCK, stride=dil), n > 0


def _head_masks():
    low = lax.broadcasted_iota(jnp.int32, (1, C_PAIR), 1) < C_HEAD_DIM
    return low, jnp.logical_not(low)


def _attn_mask(has_prev):
    i = jnp.bitwise_and(lax.broadcasted_iota(jnp.int32, (2 * C_BLOCK, 2 * C_BLOCK), 0), C_BLOCK - 1)
    j = lax.broadcasted_iota(jnp.int32, (2 * C_BLOCK, 2 * C_BLOCK), 1)
    return jnp.logical_or(j <= i, jnp.logical_and(j - C_BLOCK >= i, has_prev))


def _stack_heads(x):
    low, high = _head_masks()
    return jnp.concatenate([jnp.where(low, x, 0.0), jnp.where(high, x, 0.0)], axis=0)


def _unstack_heads(x):
    low, _ = _head_masks()
    return jnp.where(low, x[:C_BLOCK], x[C_BLOCK:])


def attn_fwd(qkv, cos_t, sin_a, sin_b, batch, name, exchange=None):
    t = qkv.shape[0]
    nbr = len(C_DILATIONS)

    def body(q_ref, k_ref, v_ref, c_ref, a_ref, b_ref, o_ref, l_ref, qs, ks, *stats):
        acc, mm, dd = stats[0:nbr], stats[nbr:2 * nbr], stats[2 * nbr:3 * nbr]
        c, a, b = c_ref[...], a_ref[...], b_ref[...]
        qs[...] = _rope(q_ref[...], c, a, b) * C_SCALE
        ks[...] = _rope(k_ref[...], c, a, b)
        def load(idx, dil):
            rows, prev, has_prev = _attn_rows(idx, dil)
            return rows, (has_prev, qs[rows, :], ks[rows, :], ks[prev, :], v_ref[rows, :], v_ref[prev, :])

        def scores(has_prev, q, k_own, k_prev, v_own, v_prev):
            k_cat = jnp.concatenate([k_own, k_prev], axis=0).astype(BF16)
            return jnp.where(_attn_mask(has_prev), _dot_nt(_stack_heads(q).astype(BF16), k_cat), NEG_BIG)

        def softmax(s):
            m = jnp.max(s, axis=-1, keepdims=True)
            p = jnp.exp(s - m)
            return p.astype(BF16), m, jnp.sum(p, axis=-1, keepdims=True)

        def values(pb, has_prev, q, k_own, k_prev, v_own, v_prev):
            low, high = _head_masks()
            v_cat = jnp.concatenate([v_own, v_prev], axis=0)
            p_wide = jnp.concatenate([pb[:C_BLOCK], pb[C_BLOCK:]], axis=1)
            v_tall = jnp.concatenate([jnp.where(low, v_cat, 0.0), jnp.where(high, v_cat, 0.0)], axis=0).astype(BF16)
            return _dot(p_wide, v_tall)

        for bi, dil in enumerate(C_DILATIONS):
            def pair(i, carry, bi=bi, dil=dil):
                low, _ = _head_masks()
                loaded = [load(C_FWD_BLOCKS * i + k, dil) for k in range(C_FWD_BLOCKS)]
                ss = [scores(*ops) for _, ops in loaded]
                sm = [softmax(s) for s in ss]
                pvs = [values(pb, *ops) for (pb, _, _), (_, ops) in zip(sm, loaded)]
                for (rows, _), (_, m, den), pv in zip(loaded, sm, pvs):
                    acc[bi][rows, :] = pv
                    mm[bi][rows, :] = jnp.where(low, m[:C_BLOCK], m[C_BLOCK:])
                    dd[bi][rows, :] = jnp.where(low, den[:C_BLOCK], den[C_BLOCK:])
                return carry

            lax.fori_loop(0, SEQ // C_BLOCK // C_FWD_BLOCKS, pair, 0)
        step = 256
        for r0 in range(0, SEQ, step):
            rr = slice(r0, r0 + step)
            ms = [mm[g][rr, :] for g in range(nbr)]
            m_all = functools.reduce(jnp.maximum, ms)
            ws = [jnp.exp(m - m_all) for m in ms]
            num = sum(acc[g][rr, :] * ws[g] for g in range(nbr))
            den = sum(dd[g][rr, :] * ws[g] for g in range(nbr))
            o_ref[rr, :] = (num / den).astype(BF16)
            l_ref[rr, :] = m_all + jnp.log(den)

    def col(k):
        return pl.BlockSpec((SEQ, C_PAIR), lambda b, p: (b, k * C_PAIRS + p))

    tab = pl.BlockSpec((SEQ, C_PAIR), lambda b, p: (b, 0))
    return _call(
        body, name=name, grid=(batch, C_PAIRS),
        in_specs=[col(0), col(1), col(2), tab, tab, tab],
        out_specs=[col(0), col(0)],
        out_shape=[jax.ShapeDtypeStruct((t, D_MODEL), BF16), jax.ShapeDtypeStruct((t, D_MODEL), F32)],
        scratch_shapes=[pltpu.VMEM((SEQ, C_PAIR), F32)] * (2 + 3 * nbr),
        args=(qkv, qkv, qkv, cos_t, sin_a, sin_b), exchange=exchange)


def attn_bwd(qkv, cos_t, sin_a, sin_b, o, lse, do, batch, name, exchange=None):
    t = qkv.shape[0]

    def body(q_ref, k_ref, v_ref, c_ref, a_ref, b_ref, o_ref, l_ref, do_ref, dq_ref, dk_ref, dv_ref,
             qs, ks, dqs, dks, dvs, dlt):
        c, a, b = c_ref[...], a_ref[...], b_ref[...]
        qs[...] = _rope(q_ref[...], c, a, b) * C_SCALE
        ks[...] = _rope(k_ref[...], c, a, b)
        prod = do_ref[...] * o_ref[...].astype(F32)
        low = lax.broadcasted_iota(jnp.int32, (1, C_PAIR), 1) < C_HEAD_DIM
        s_low = jnp.sum(jnp.where(low, prod, 0.0), axis=-1, keepdims=True)
        s_all = jnp.sum(prod, axis=-1, keepdims=True)
        dlt[...] = jnp.where(low, s_low, s_all - s_low)
        dqs[...] = jnp.zeros_like(dqs)
        dks[...] = jnp.zeros_like(dks)
        dvs[...] = jnp.zeros_like(dvs)
        def load(idx, dil):
            rows, prev, has_prev = _attn_rows(idx, dil)
            return (rows, prev), (has_prev, qs[rows, :], do_ref[rows, :], ks[rows, :], ks[prev, :],
                                  v_ref[rows, :], v_ref[prev, :], l_ref[rows, :], dlt[rows, :])

        def operands(has_prev, q, do, k_own, k_prev, v_own, v_prev, l_full, d_full):
            lcol = jnp.concatenate([l_full[:, 0:1], l_full[:, C_HEAD_DIM:C_HEAD_DIM + 1]], axis=0)
            dcol = jnp.concatenate([d_full[:, 0:1], d_full[:, C_HEAD_DIM:C_HEAD_DIM + 1]], axis=0)
            return (_stack_heads(q).astype(BF16), _stack_heads(do).astype(BF16),
                    jnp.concatenate([k_own, k_prev], axis=0).astype(BF16),
                    jnp.concatenate([v_own, v_prev], axis=0).astype(BF16), lcol, dcol, _attn_mask(has_prev))

        for dil in C_DILATIONS:
            def pair(i, carry, dil=dil):
                loaded = [load(C_BWD_BLOCKS * i + k, dil) for k in range(C_BWD_BLOCKS)]
                ops = [operands(*o) for _, o in loaded]
                ss = [_dot_nt(q_stack, k_cat) for q_stack, _, k_cat, _, _, _, _ in ops]
                dps = [_dot_nt(do_stack, v_cat) for _, do_stack, _, v_cat, _, _, _ in ops]
                ps = [jnp.exp(jnp.where(o[6], s, NEG_BIG) - o[4]) for s, o in zip(ss, ops)]
                dss = [(p * (dp - o[5])).astype(BF16) for p, dp, o in zip(ps, dps, ops)]
                dvs_ = [_dot_tn(p.astype(BF16), o[1]) for p, o in zip(ps, ops)]
                dks_ = [_dot_tn(ds, o[0]) for ds, o in zip(dss, ops)]
                dqs_ = [_unstack_heads(_dot(ds, o[2])) for ds, o in zip(dss, ops)]
                results = list(zip(dqs_, dks_, dvs_))
                for ((rows, prev), _), (dq, dk_cat, dv_cat) in zip(loaded, results):
                    dqs[rows, :] += dq
                    dks[rows, :] += dk_cat[:C_BLOCK]
                    dvs[rows, :] += dv_cat[:C_BLOCK]
                    dks[prev, :] += dk_cat[C_BLOCK:]
                    dvs[prev, :] += dv_cat[C_BLOCK:]
                return carry

            lax.fori_loop(0, SEQ // C_BLOCK // C_BWD_BLOCKS, pair, 0)
        dq_ref[...] = _rope_t(dqs[...] * C_SCALE, c, a, b)
        dk_ref[...] = _rope_t(dks[...], c, a, b)
        dv_ref[...] = dvs[...]

    def col(k):
        return pl.BlockSpec((SEQ, C_PAIR), lambda b, p: (b, k * C_PAIRS + p))

    tab = pl.BlockSpec((SEQ, C_PAIR), lambda b, p: (b, 0))
    out = jax.ShapeDtypeStruct((t, D_MODEL), F32)
    return _call(
        body, name=name, grid=(batch, C_PAIRS),
        in_specs=[col(0), col(1), col(2), tab, tab, tab, col(0), col(0), col(0)],
        out_specs=[col(0), col(0), col(0)],
        out_shape=[out, out, out],
        scratch_shapes=[pltpu.VMEM((SEQ, C_PAIR), F32)] * 6,
        args=(qkv, qkv, qkv, cos_t, sin_a, sin_b, o, lse, do), exchange=exchange)


def sibling_swap(arrays, name):
    n = len(arrays)

    def body(*refs):
        ins, outs = refs[:n], refs[n:2 * n]
        send_sems, recv_sems = refs[2 * n:]
        x, y, c, _ = _place()
        sends = []
        for a in range(n):
            cp = pltpu.make_async_remote_copy(
                src_ref=ins[a], dst_ref=outs[a], send_sem=send_sems.at[a], recv_sem=recv_sems.at[a],
                device_id=(x, y, 1 - c), device_id_type=MESH)
            cp.start()
            sends.append(cp)
        for cp in sends:
            cp.wait_recv()
        for cp in sends:
            cp.wait_send()

    return pl.pallas_call(
        body, name=name,
        in_specs=[ANY] * n, out_specs=[ANY] * n,
        out_shape=[jax.ShapeDtypeStruct(s.shape, s.dtype) for s in arrays],
        scratch_shapes=[pltpu.SemaphoreType.DMA((n,)), pltpu.SemaphoreType.DMA((n,))],
    )(*arrays)


def allreduce_small(slab, name):
    rows, lanes = slab.shape

    def body(x_ref, out_ref, gath, send_sems, recv_sems, local_sem):
        x, y, c, chips = _place()
        me, sibling = (x, y, c), (x, y, 1 - c)

        def slot(px, py, pc):
            return gath.at[4 * px + 2 * py + pc]

        def copy(k, block, to, src=None):
            return pltpu.make_async_remote_copy(
                src_ref=slot(*block) if src is None else src, dst_ref=slot(*block),
                send_sem=send_sems.at[k], recv_sem=recv_sems.at[k], device_id=to, device_id_type=MESH)

        mine = pltpu.make_async_copy(x_ref, slot(*me), local_sem)
        mine.start()
        first = [copy(0, me, sibling, src=x_ref)]
        first += [copy(1 + j, me, (*chip, c), src=x_ref) for j, chip in enumerate(chips)]
        for cp in first:
            cp.start()
        passed = [copy(4 + j, (*chip, c), sibling) for j, chip in enumerate(chips)]
        for j, chip in enumerate(chips):
            copy(1 + j, (*chip, c), me).wait_recv()
            passed[j].start()
        copy(0, sibling, me).wait_recv()
        for j, chip in enumerate(chips):
            copy(4 + j, (*chip, 1 - c), me).wait_recv()
        for cp in first + passed:
            cp.wait_send()
        mine.wait()
        total = gath[0]
        for d in range(1, N_DEV):
            total = total + gath[d]
        out_ref[...] = total

    return pl.pallas_call(
        body, name=name,
        in_specs=[pl.BlockSpec(memory_space=pltpu.VMEM)],
        out_specs=pl.BlockSpec(memory_space=pltpu.VMEM),
        out_shape=jax.ShapeDtypeStruct((rows, lanes), F32),
        scratch_shapes=[pltpu.VMEM((N_DEV, rows, lanes), F32),
                        pltpu.SemaphoreType.DMA((7,)), pltpu.SemaphoreType.DMA((7,)), pltpu.SemaphoreType.DMA],
    )(slab)


ELT_ROWS = 512


def reduce_slabs(r, name, part=0, parts=1, into=None):
    _, rows, cols = r.shape
    br = min(rows, ELT_ROWS)
    nblk = rows // br

    def body(r_ref, *rest):
        o_ref = rest[-1]
        o_ref[...] = ((r_ref[3].astype(F32) + r_ref[0].astype(F32)) + r_ref[1].astype(F32)) + r_ref[2].astype(F32)

    return pl.pallas_call(
        body, name=name, grid=(nblk,),
        in_specs=[pl.BlockSpec((N_CHIPS, br, cols), lambda i: (0, i, 0))] + ([] if into is None else [ANY]),
        out_specs=pl.BlockSpec((br, cols), lambda i: (part * nblk + i, 0)),
        out_shape=jax.ShapeDtypeStruct((parts * rows, cols), F32),
        input_output_aliases={} if into is None else {1: 0},
        compiler_params=_params(("arbitrary",)),
    )(*([r] if into is None else [r, into]))


def _adamw(w, g, m, v):
    m = ADAM_B1 * m + (1.0 - ADAM_B1) * g
    v = ADAM_B2 * v + (1.0 - ADAM_B2) * jnp.square(g)
    m_hat = m / (1.0 - ADAM_B1 ** ADAM_STEP)
    v_hat = v / (1.0 - ADAM_B2 ** ADAM_STEP)
    delta = -ADAM_LR * (m_hat / (jnp.sqrt(v_hat) + ADAM_EPS) + ADAM_WD * w)
    return delta, m, v


def adamw_big(w, s_mine, s_sibling, m, v, name):
    rows, cols = w.shape

    def body(w_ref, a_ref, b_ref, m_ref, v_ref, g_out, d_out, m_out, v_out):
        g = a_ref[...] + b_ref[...]
        g_out[...] = g
        d_out[...], m_out[...], v_out[...] = _adamw(w_ref[...], g, m_ref[...], v_ref[...])

    blk = pl.BlockSpec((min(rows, ELT_ROWS), cols), lambda i: (i, 0))
    out = jax.ShapeDtypeStruct((rows, cols), F32)
    return pl.pallas_call(
        body, name=name, grid=(rows // min(rows, ELT_ROWS),),
        in_specs=[blk] * 5, out_specs=[blk] * 4, out_shape=[out] * 4,
        compiler_params=_params(("arbitrary",)),
    )(w, s_mine, s_sibling, m, v)


def adamw_small(ws, gs, ms, vs, name):
    n = len(ws)

    def body(*refs):
        w_refs, g_refs, m_refs, v_refs = (refs[k * n:(k + 1) * n] for k in range(4))
        d_out, m_out, v_out = (refs[(4 + k) * n:(5 + k) * n] for k in range(3))
        for i in range(n):
            d_out[i][...], m_out[i][...], v_out[i][...] = _adamw(
                w_refs[i][...], g_refs[i][...], m_refs[i][...], v_refs[i][...])

    outs = [jax.ShapeDtypeStruct(w.shape, F32) for w in ws]
    res = pl.pallas_call(body, name=name, out_shape=outs * 3)(*ws, *gs, *ms, *vs)
    return res[:n], res[n:2 * n], res[2 * n:]


SLAB_LANES = 128
SLAB_ROW_ALIGN = 8


def _pack(parts):
    flat = jnp.concatenate([p.reshape(-1) for p in parts])
    rows = -(-flat.shape[0] // (SLAB_LANES * SLAB_ROW_ALIGN)) * SLAB_ROW_ALIGN
    flat = jnp.pad(flat, (0, rows * SLAB_LANES - flat.shape[0]))
    return flat.reshape(rows, SLAB_LANES)


def _unpack(slab, shapes):
    flat = slab.reshape(-1)
    out, pos = [], 0
    for s in shapes:
        size = math.prod(s)
        out.append(flat[pos:pos + size].reshape(s))
        pos += size
    return out


def kernel(x, positions, norm_mix_pre, norm_mix_post, norm_ffn_pre, norm_ffn_post, w_in_even, lb_table, a_norm, b_ln_g, b_ln_b, b_ws, b_bias, w_out_even, w_in_odd, w_out_odd, w_ff1, w_ff2, loss_target, m_norm_mix_pre, m_norm_mix_post, m_norm_ffn_pre, m_norm_ffn_post, m_w_in_even, m_lb_table, m_a_norm, m_b_ln_g, m_b_ln_b, m_b_ws, m_b_bias, m_w_out_even, m_w_in_odd, m_w_out_odd, m_w_ff1, m_w_ff2, v_norm_mix_pre, v_norm_mix_post, v_norm_ffn_pre, v_norm_ffn_post, v_w_in_even, v_lb_table, v_a_norm, v_b_ln_g, v_b_ln_b, v_b_ws, v_b_bias, v_w_out_even, v_w_in_odd, v_w_out_odd, v_w_ff1, v_w_ff2):
    batch = x.shape[0]
    t = batch * SEQ
    d = D_MODEL
    x0 = x.reshape(t, d)
    target = loss_target.reshape(t, d)

    def gain(p, layer):
        return p[layer:layer + 1]

    def gather(*shards):
        return _Exchange("gather", [w.astype(BF16) for w in shards])

    def scatter(*grads):
        return _Exchange("scatter", grads)

    (win_e,) = exchange_alone(gather(w_in_even[0]), "gather_in_even")
    bias_t = b_bias[0].T
    proj, h0, w1_0 = norm_matmul(x0, gain(norm_mix_pre, 0), win_e, "in_proj_even", exchange=gather(w_ff1[0]))
    oa, states, decays, w2_0, wout_e = hgrn2_fwd(proj, lb_table, a_norm, batch, "hgrn2_fwd",
                                                 exchange=gather(w_ff2[0], w_out_even[0]))
    (mixin,) = gmlp_fwd(proj, oa, b_ln_g, b_ln_b, b_ws[0], bias_t, "gmlp_fwd")
    mix0, x1 = out_proj(mixin, wout_e, x0, gain(norm_mix_post, 0), "out_proj_even")
    x2, hf0, a0, y0, win_o, wout_o = ffn_fwd(x1, gain(norm_ffn_pre, 0), w1_0, w2_0, gain(norm_ffn_post, 0),
                                             "ffn_fwd_0", exchange=gather(w_in_odd[0], w_out_odd[0]))
    qkv, h1 = norm_matmul(x2, gain(norm_mix_pre, 1), win_o, "in_proj_odd")
    cos_t, sin_a, sin_b = rope_tables(positions.reshape(t, 1), "rope_tables")
    ao, lse, w1_1, w2_1 = attn_fwd(qkv, cos_t, sin_a, sin_b, batch, "attn_fwd", exchange=gather(w_ff1[1], w_ff2[1]))
    mix1, x3 = out_proj(ao, wout_o, x2, gain(norm_mix_post, 1), "out_proj_odd")
    x4, hf1, a1, y1 = ffn_fwd(x3, gain(norm_ffn_pre, 1), w1_1, w2_1, gain(norm_ffn_post, 1), "ffn_fwd_1")
    dx4, loss_part = loss_grad(x4, target, "loss_grad")

    hc = D_FF // N_CHIPS
    dx3, dy1, da1, dg_fpre1, dg_fpost1 = ffn_bwd(
        dx4, x3, y1, a1, gain(norm_ffn_pre, 1), gain(norm_ffn_post, 1), w1_1, w2_1, "ffn_bwd_1")
    g_w1_1 = weight_grad(hf1, da1, "b", d, hc, False, "wgrad_ff1_1")
    g_w2_1 = weight_grad(a1, dy1, "a", hc, d, True, "wgrad_ff2_1")
    dmix1, dao, dg_mpost1 = out_proj_bwd(dx3, mix1, gain(norm_mix_post, 1), wout_o, "out_proj_bwd_odd")
    g_wout_o = weight_grad(ao, dmix1, "a", d // N_CHIPS, d, False, "wgrad_out_odd")
    dq, dk, dv, r_w1_1, r_w2_1, r_wout_o = attn_bwd(qkv, cos_t, sin_a, sin_b, ao, lse, dao, batch, "attn_bwd",
                                                    exchange=scatter(g_w1_1, g_w2_1, g_wout_o))
    dqkv = jnp.concatenate([dq, dk, dv], axis=1)
    dx2, dg_mpre1 = norm_matmul_bwd(dqkv, win_o, x2, gain(norm_mix_pre, 1), dx3, "in_proj_bwd_odd")
    g_win_o = weight_grad(h1, dqkv, "b", d, 3 * d // N_CHIPS, False, "wgrad_in_odd")
    dx1, dy0, da0, dg_fpre0, dg_fpost0, r_win_o = ffn_bwd(
        dx2, x1, y0, a0, gain(norm_ffn_pre, 0), gain(norm_ffn_post, 0), w1_0, w2_0, "ffn_bwd_0",
        exchange=scatter(g_win_o))
    g_w1_0 = weight_grad(hf0, da0, "b", d, hc, False, "wgrad_ff1_0")
    g_w2_0 = weight_grad(a0, dy0, "a", hc, d, True, "wgrad_ff2_0")
    dmix0, dmixin, dg_mpost0 = out_proj_bwd(dx1, mix0, gain(norm_mix_post, 0), wout_e, "out_proj_bwd_even")
    g_wout_e = weight_grad(mixin, dmix0, "a", d // N_CHIPS, d, False, "wgrad_out_even")
    dproj, d_lb, d_anorm, r_w1_0 = hgrn2_bwd(
        proj, states, decays, lb_table, a_norm, dmixin, batch, "hgrn2_bwd", exchange=scatter(g_w1_0))
    dproj, d_lng, d_lnb, d_ws, d_bias_t, r_wout_e = gmlp_bwd(
        proj, dmixin, b_ln_g, b_ln_b, b_ws[0], bias_t, dproj, "gmlp_bwd", exchange=scatter(g_wout_e))
    g_win_e, r_w2_0 = weight_grad(h0, dproj, "b", d, 3 * d // N_CHIPS, False, "wgrad_in_even",
                                  exchange=scatter(g_w2_0))
    dx0, dg_mpre0, r_win_e = norm_matmul_bwd(dproj, win_e, x0, gain(norm_mix_pre, 0), dx1, "in_proj_bwd_even",
                                             exchange=scatter(g_win_e))
    grad_x = dx0.reshape(x.shape)

    s_w1 = reduce_slabs(r_w1_1, "reduce_ff1_1", part=1, parts=2)
    s_w1 = reduce_slabs(r_w1_0, "reduce_ff1_0", part=0, parts=2, into=s_w1)
    s_w2 = reduce_slabs(r_w2_1, "reduce_ff2_1", part=1, parts=2)
    s_w2 = reduce_slabs(r_w2_0, "reduce_ff2_0", part=0, parts=2, into=s_w2)
    sums = [reduce_slabs(r_win_e, "reduce_in_even"), reduce_slabs(r_wout_e, "reduce_out_even"),
            reduce_slabs(r_win_o, "reduce_in_odd"), reduce_slabs(r_wout_o, "reduce_out_odd"), s_w1, s_w2]
    sibling = sibling_swap(sums, "sibling_swap")
    big_w = [w_in_even, w_out_even, w_in_odd, w_out_odd, w_ff1, w_ff2]
    big_m = [m_w_in_even, m_w_out_even, m_w_in_odd, m_w_out_odd, m_w_ff1, m_w_ff2]
    big_v = [v_w_in_even, v_w_out_even, v_w_in_odd, v_w_out_odd, v_w_ff1, v_w_ff2]
    big = []
    for i, (w, m, v) in enumerate(zip(big_w, big_m, big_v)):
        two_d = (-1, w.shape[-1])
        res = adamw_big(w.reshape(two_d), sums[i], sibling[i], m.reshape(two_d), v.reshape(two_d), "adamw_big_%d" % i)
        big.append([r.reshape(w.shape) for r in res])

    small_w = [norm_mix_pre, norm_mix_post, norm_ffn_pre, norm_ffn_post, lb_table, a_norm, b_ln_g, b_ln_b, b_ws, b_bias]
    small_m = [m_norm_mix_pre, m_norm_mix_post, m_norm_ffn_pre, m_norm_ffn_post, m_lb_table, m_a_norm, m_b_ln_g,
               m_b_ln_b, m_b_ws, m_b_bias]
    small_v = [v_norm_mix_pre, v_norm_mix_post, v_norm_ffn_pre, v_norm_ffn_post, v_lb_table, v_a_norm, v_b_ln_g,
               v_b_ln_b, v_b_ws, v_b_bias]
    partial = [jnp.concatenate([dg_mpre0, dg_mpre1]), jnp.concatenate([dg_mpost0, dg_mpost1]),
               jnp.concatenate([dg_fpre0, dg_fpre1]), jnp.concatenate([dg_fpost0, dg_fpost1]),
               d_lb, d_anorm, d_lng, d_lnb, d_ws[None], d_bias_t.T[None]]
    *small_g, loss = _unpack(allreduce_small(_pack(partial + [loss_part]), "allreduce_small"),
                             [w.shape for w in small_w] + [()])
    small_d, small_nm, small_nv = adamw_small(small_w, small_g, small_m, small_v, "adamw_small")

    order = ["norm_mix_pre", "norm_mix_post", "norm_ffn_pre", "norm_ffn_post", "w_in_even", "lb_table", "a_norm",
             "b_ln_g", "b_ln_b", "b_ws", "b_bias", "w_out_even", "w_in_odd", "w_out_odd", "w_ff1", "w_ff2"]
    small_names = ["norm_mix_pre", "norm_mix_post", "norm_ffn_pre", "norm_ffn_post", "lb_table", "a_norm",
                   "b_ln_g", "b_ln_b", "b_ws", "b_bias"]
    big_names = ["w_in_even", "w_out_even", "w_in_odd", "w_out_odd", "w_ff1", "w_ff2"]
    grads, deltas, new_m, new_v = {}, {}, {}, {}
    for i, nm in enumerate(small_names):
        grads[nm], deltas[nm], new_m[nm], new_v[nm] = small_g[i], small_d[i], small_nm[i], small_nv[i]
    for i, nm in enumerate(big_names):
        grads[nm], deltas[nm], new_m[nm], new_v[nm] = big[i]
    return (loss, grad_x, *[grads[n] for n in order], *[deltas[n] for n in order],
            *[new_m[n] for n in order], *[new_v[n] for n in order])
```

```python
import functools
import math

import jax
import jax.numpy as jnp
from jax import lax
from jax.experimental import pallas as pl
from jax.experimental.pallas import tpu as pltpu

F32 = jnp.float32
BF16 = jnp.bfloat16
MESH = pl.DeviceIdType.MESH

D_MODEL = 1024
SEQ = 2048
D_FF = 4096
N_CHIPS = 4
A_WIDTH = 512
A_HEADS = 4
A_DK = 128
A_CHUNK = 64
A_SUB = 16
B_WIDTH = 512
B_GROUPS = 4
B_CHUNK = 128
C_HEADS = 16
C_HEAD_DIM = 64
C_ROT_HALF = 8
C_BLOCK = 128
C_DILATIONS = (1, 4, 16)
ROPE_THETA = 500000.0
EPS = 1e-6
ADAM_LR = 0.001
ADAM_B1 = 0.9
ADAM_B2 = 0.999
ADAM_EPS = 1e-08
ADAM_WD = 0.01
ADAM_STEP = 10

ROW_TILE = 512
FFN_ROWS = 1024
WGRAD_ROWS = 2048
VMEM_LIMIT = 56 * 1024 * 1024
NEG_BIG = -1e30


def _params(sem=None):
    return pltpu.CompilerParams(dimension_semantics=sem, vmem_limit_bytes=VMEM_LIMIT)


def _dot(a, b):
    return jnp.dot(a, b, preferred_element_type=F32)


def _dot_nt(a, b):
    return lax.dot_general(a, b, (((1,), (1,)), ((), ())), preferred_element_type=F32)


def _dot_tn(a, b):
    return lax.dot_general(a, b, (((0,), (0,)), ((), ())), preferred_element_type=F32)


def _rms(x, g):
    r = lax.rsqrt(jnp.mean(x * x, axis=-1, keepdims=True) + EPS)
    return x * r * g


def _rms_bwd(x, g, dy):
    r = lax.rsqrt(jnp.mean(x * x, axis=-1, keepdims=True) + EPS)
    xh = x * r
    dg = jnp.sum(dy * xh, axis=0, keepdims=True)
    dxh = dy * g
    dx = r * (dxh - xh * jnp.mean(dxh * xh, axis=-1, keepdims=True))
    return dx, dg


def _accumulate(ref, val, first):
    @pl.when(first)
    def _():
        ref[...] = val

    @pl.when(jnp.logical_not(first))
    def _():
        ref[...] += val


N_DEV = 8
ANY = pl.BlockSpec(memory_space=pl.ANY)


def _place():
    x, y, c = lax.axis_index("x"), lax.axis_index("y"), lax.axis_index("c")
    return x, y, c, [(1 - x, y), (x, 1 - y), (1 - x, 1 - y)]


class _Exchange:
    def __init__(self, kind, arrays):
        self.kind, self.arrays, self.n = kind, list(arrays), len(arrays)
        per_peer = pltpu.SemaphoreType.DMA((3 * self.n,))
        if kind == "gather":
            self.out_shape = [jax.ShapeDtypeStruct((N_CHIPS,) + a.shape, a.dtype) for a in self.arrays]
            self.scratch = [per_peer, per_peer, pltpu.SemaphoreType.DMA((self.n,)), per_peer, per_peer]
        else:
            self.out_shape = [jax.ShapeDtypeStruct(a.shape, a.dtype) for a in self.arrays]
            self.scratch = [per_peer, per_peer, pltpu.SemaphoreType.DMA((self.n,))]

    def _copies(self, ins, outs, sems):
        send_sems, recv_sems, local_sems = sems[:3]
        x, y, c, chips = _place()
        me = 2 * x + y
        local, remote = [], []
        for a in range(self.n):
            if self.kind == "gather":
                local.append(pltpu.make_async_copy(ins[a], outs[a].at[me], local_sems.at[a]))
                half = self.arrays[a].shape[0] // 2

                def rows(ref, core, half=half):
                    return ref.at[pl.ds(core * half, half)]
            else:
                local.append(pltpu.make_async_copy(ins[a].at[me], outs[a].at[3], local_sems.at[a]))
            for j, (px, py) in enumerate(chips):
                k = 3 * a + j
                peer = 2 * px + py

                def copy(src, dst, to, send_sem=send_sems.at[k], recv_sem=recv_sems.at[k]):
                    return pltpu.make_async_remote_copy(src_ref=src, dst_ref=dst, send_sem=send_sem, recv_sem=recv_sem,
                                                        device_id=to, device_id_type=MESH)

                if self.kind == "gather":
                    sent = copy(rows(ins[a], c), rows(outs[a].at[me], c), (px, py, c))
                    landed = copy(rows(ins[a], c), rows(outs[a].at[peer], c), (px, py, c))
                    on = dict(send_sem=sems[3].at[k], recv_sem=sems[4].at[k])
                    passed = copy(rows(outs[a].at[peer], c), rows(outs[a].at[peer], c), (x, y, 1 - c), **on)
                    handed = copy(rows(outs[a].at[peer], c), rows(outs[a].at[peer], 1 - c), (x, y, 1 - c), **on)
                    remote.append((sent, landed, passed, handed))
                else:
                    sent = copy(ins[a].at[peer], outs[a].at[j], (px, py, c))
                    remote.append((sent, sent, None, None))
        return local, remote

    def start(self, ins, outs, sems):
        local, remote = self._copies(ins, outs, sems)
        for cp in local:
            cp.start()
        for sent, _, _, _ in remote:
            sent.start()

    def finish(self, ins, outs, sems):
        local, remote = self._copies(ins, outs, sems)
        for _, landed, passed, _ in remote:
            landed.wait_recv()
            if passed is not None:
                passed.start()
        for sent, _, passed, handed in remote:
            if passed is not None:
                handed.wait_recv()
                passed.wait_send()
            sent.wait_send()
        for cp in local:
            cp.wait()


def _call(body, *, name, grid, in_specs, out_specs, out_shape, args, scratch_shapes=(), aliases=None, exchange=None):
    if exchange is None:
        return pl.pallas_call(
            body, name=name, grid=grid, in_specs=in_specs, out_specs=out_specs, out_shape=out_shape,
            scratch_shapes=list(scratch_shapes), input_output_aliases=aliases or {},
            compiler_params=_params(("arbitrary",) * len(grid)))(*args)
    n_in, n_out, n_scr, n_ex = len(in_specs), len(out_specs), len(scratch_shapes), exchange.n
    steps = grid

    def wrapped(*refs):
        ins, refs = refs[:n_in], refs[n_in:]
        ex_in, refs = refs[:n_ex], refs[n_ex:]
        outs, refs = refs[:n_out], refs[n_out:]
        ex_out, refs = refs[:n_ex], refs[n_ex:]
        scr, sems = refs[:n_scr], refs[n_scr:]
        first = functools.reduce(jnp.logical_and, [pl.program_id(k) == 0 for k in range(len(steps))])
        last = functools.reduce(jnp.logical_and, [pl.program_id(k) == steps[k] - 1 for k in range(len(steps))])

        @pl.when(first)
        def _():
            exchange.start(ex_in, ex_out, sems)

        body(*ins, *outs, *scr)

        @pl.when(last)
        def _():
            exchange.finish(ex_in, ex_out, sems)

    return pl.pallas_call(
        wrapped, name=name, grid=grid,
        in_specs=list(in_specs) + [ANY] * n_ex, out_specs=list(out_specs) + [ANY] * n_ex,
        out_shape=list(out_shape) + exchange.out_shape,
        scratch_shapes=list(scratch_shapes) + exchange.scratch, input_output_aliases=aliases or {},
        compiler_params=_params(("arbitrary",) * len(grid)))(*args, *exchange.arrays)


def exchange_alone(exchange, name):
    def body(*refs):
        n = exchange.n
        exchange.start(refs[:n], refs[n:2 * n], refs[2 * n:])
        exchange.finish(refs[:n], refs[n:2 * n], refs[2 * n:])

    return pl.pallas_call(
        body, name=name, in_specs=[ANY] * exchange.n, out_specs=[ANY] * exchange.n,
        out_shape=exchange.out_shape, scratch_shapes=exchange.scratch)(*exchange.arrays)


def norm_matmul(x, g, wg, name, exchange=None):
    t, d = x.shape
    nl = wg.shape[2]

    def body(x_ref, g_ref, w_ref, o_ref, h_ref):
        h = _rms(x_ref[...], g_ref[...]).astype(BF16)
        h_ref[...] = h
        for c in range(N_CHIPS):
            o_ref[:, c * nl:(c + 1) * nl] = _dot(h, w_ref[c])

    return _call(
        body, name=name, grid=(t // ROW_TILE,),
        in_specs=[pl.BlockSpec((ROW_TILE, d), lambda i: (i, 0)),
                  pl.BlockSpec((1, d), lambda i: (0, 0)),
                  pl.BlockSpec((N_CHIPS, d, nl), lambda i: (0, 0, 0))],
        out_specs=[pl.BlockSpec((ROW_TILE, N_CHIPS * nl), lambda i: (i, 0)),
                   pl.BlockSpec((ROW_TILE, d), lambda i: (i, 0))],
        out_shape=[jax.ShapeDtypeStruct((t, N_CHIPS * nl), F32), jax.ShapeDtypeStruct((t, d), BF16)],
        args=(x, g, wg), exchange=exchange)


def norm_matmul_bwd(dproj, wg, x, g, dres, name, exchange=None):
    t, d = x.shape
    nl = wg.shape[2]

    def body(dp_ref, w_ref, x_ref, g_ref, dres_ref, dx_ref, dg_ref):
        dh = _dot_nt(dp_ref[:, 0:nl].astype(BF16), w_ref[0])
        for c in range(1, N_CHIPS):
            dh += _dot_nt(dp_ref[:, c * nl:(c + 1) * nl].astype(BF16), w_ref[c])
        dx, dg = _rms_bwd(x_ref[...], g_ref[...], dh)
        dx_ref[...] = dres_ref[...] + dx
        _accumulate(dg_ref, dg, pl.program_id(0) == 0)

    row = pl.BlockSpec((ROW_TILE, d), lambda i: (i, 0))
    vec = pl.BlockSpec((1, d), lambda i: (0, 0))
    return _call(
        body, name=name, grid=(t // ROW_TILE,),
        in_specs=[pl.BlockSpec((ROW_TILE, N_CHIPS * nl), lambda i: (i, 0)),
                  pl.BlockSpec((N_CHIPS, d, nl), lambda i: (0, 0, 0)), row, vec, row],
        out_specs=[row, vec],
        out_shape=[jax.ShapeDtypeStruct((t, d), F32), jax.ShapeDtypeStruct((1, d), F32)],
        args=(dproj, wg, x, g, dres), exchange=exchange)


def out_proj(a, wg, x, g, name):
    t, d = x.shape
    kl = wg.shape[1]

    def body(a_ref, w_ref, x_ref, g_ref, mix_ref, xo_ref):
        acc = _dot(a_ref[:, 0:kl], w_ref[0])
        for c in range(1, N_CHIPS):
            acc += _dot(a_ref[:, c * kl:(c + 1) * kl], w_ref[c])
        mix_ref[...] = acc
        xo_ref[...] = x_ref[...] + _rms(acc, g_ref[...])

    row = pl.BlockSpec((ROW_TILE, d), lambda i: (i, 0))
    return pl.pallas_call(
        body, name=name, grid=(t // ROW_TILE,),
        in_specs=[row, pl.BlockSpec((N_CHIPS, kl, d), lambda i: (0, 0, 0)), row,
                  pl.BlockSpec((1, d), lambda i: (0, 0))],
        out_specs=[row, row],
        out_shape=[jax.ShapeDtypeStruct((t, d), F32), jax.ShapeDtypeStruct((t, d), F32)],
        compiler_params=_params(("arbitrary",)),
    )(a, wg, x, g)


def out_proj_bwd(dxo, mix, g, wg, name):
    t, d = mix.shape
    kl = wg.shape[1]

    def body(dxo_ref, mix_ref, g_ref, w_ref, dmix_ref, da_ref, dg_ref):
        dmix, dg = _rms_bwd(mix_ref[...], g_ref[...], dxo_ref[...])
        dmb = dmix.astype(BF16)
        dmix_ref[...] = dmb
        for c in range(N_CHIPS):
            da_ref[:, c * kl:(c + 1) * kl] = _dot_nt(dmb, w_ref[c])
        _accumulate(dg_ref, dg, pl.program_id(0) == 0)

    row = pl.BlockSpec((ROW_TILE, d), lambda i: (i, 0))
    vec = pl.BlockSpec((1, d), lambda i: (0, 0))
    return pl.pallas_call(
        body, name=name, grid=(t // ROW_TILE,),
        in_specs=[row, row, vec, pl.BlockSpec((N_CHIPS, kl, d), lambda i: (0, 0, 0))],
        out_specs=[row, row, vec],
        out_shape=[jax.ShapeDtypeStruct((t, d), BF16), jax.ShapeDtypeStruct((t, d), F32),
                   jax.ShapeDtypeStruct((1, d), F32)],
        compiler_params=_params(("arbitrary",)),
    )(dxo, mix, g, wg)


def ffn_fwd(x, gpre, w1g, w2g, gpost, name, exchange=None, target=None):
    t, d = x.shape
    hc = w1g.shape[2]
    with_loss = target is not None

    def body(x_ref, gpre_ref, w1_ref, w2_ref, gpost_ref, *rest):
        if with_loss:
            t_ref, xo_ref, h_ref, a_ref, y_ref, l_ref, acc = rest
        else:
            xo_ref, h_ref, a_ref, y_ref, acc = rest
        i, c = pl.program_id(0), pl.program_id(1)

        @pl.when(c == 0)
        def _():
            h_ref[...] = _rms(x_ref[...], gpre_ref[...]).astype(BF16)

        a = _dot(h_ref[...], w1_ref[...])
        a_ref[...] = a.astype(BF16)
        r = jnp.square(jnp.maximum(a, 0.0)).astype(BF16)
        _accumulate(acc, _dot(r, w2_ref[...]), c == 0)

        @pl.when(c == N_CHIPS - 1)
        def _():
            y = acc[...]
            y_ref[...] = y
            xo = x_ref[...] + _rms(y, gpost_ref[...])
            if with_loss:
                e = xo - t_ref[...]
                xo_ref[...] = e * (1.0 / d)
                part = jnp.sum(jnp.sum(e * e, axis=-1, keepdims=True), axis=0, keepdims=True) * (0.5 / d)
                _accumulate(l_ref, part, i == 0)
            else:
                xo_ref[...] = xo

    row = pl.BlockSpec((FFN_ROWS, d), lambda i, c: (i, 0))
    vec = pl.BlockSpec((1, d), lambda i, c: (0, 0))
    one = pl.BlockSpec((1, 1), lambda i, c: (0, 0))
    return _call(
        body, name=name, grid=(t // FFN_ROWS, N_CHIPS),
        in_specs=[row, vec,
                  pl.BlockSpec((None, d, hc), lambda i, c: (c, 0, 0)),
                  pl.BlockSpec((None, hc, d), lambda i, c: (c, 0, 0)), vec] + ([row] if with_loss else []),
        out_specs=[row, row, pl.BlockSpec((FFN_ROWS, hc), lambda i, c: (i, c)), row] + ([one] if with_loss else []),
        out_shape=[jax.ShapeDtypeStruct((t, d), F32), jax.ShapeDtypeStruct((t, d), BF16),
                   jax.ShapeDtypeStruct((t, N_CHIPS * hc), BF16), jax.ShapeDtypeStruct((t, d), F32)]
        + ([jax.ShapeDtypeStruct((1, 1), F32)] if with_loss else []),
        scratch_shapes=[pltpu.VMEM((FFN_ROWS, d), F32)],
        args=(x, gpre, w1g, w2g, gpost) + ((target,) if with_loss else ()), exchange=exchange)


def ffn_bwd(dxo, x, y, a, gpre, gpost, w1g, w2g, name, exchange=None):
    t, d = x.shape
    hc = w1g.shape[2]

    def body(dxo_ref, x_ref, y_ref, a_ref, gpre_ref, gpost_ref, w1_ref, w2_ref,
             dxi_ref, dy_ref, da_ref, dgpre_ref, dgpost_ref, acc):
        i, c = pl.program_id(0), pl.program_id(1)

        @pl.when(c == 0)
        def _():
            dy, dg = _rms_bwd(y_ref[...], gpost_ref[...], dxo_ref[...])
            dy_ref[...] = dy.astype(BF16)
            _accumulate(dgpost_ref, dg, i == 0)

        dr = _dot_nt(dy_ref[...], w2_ref[...])
        da = (dr * (2.0 * jnp.maximum(a_ref[...].astype(F32), 0.0))).astype(BF16)
        da_ref[...] = da
        _accumulate(acc, _dot_nt(da, w1_ref[...]), c == 0)

        @pl.when(c == N_CHIPS - 1)
        def _():
            dx, dg = _rms_bwd(x_ref[...], gpre_ref[...], acc[...])
            dxi_ref[...] = dxo_ref[...] + dx
            _accumulate(dgpre_ref, dg, i == 0)

    row = pl.BlockSpec((ROW_TILE, d), lambda i, c: (i, 0))
    vec = pl.BlockSpec((1, d), lambda i, c: (0, 0))
    hid = pl.BlockSpec((ROW_TILE, hc), lambda i, c: (i, c))
    return _call(
        body, name=name, grid=(t // ROW_TILE, N_CHIPS),
        in_specs=[row, row, row, hid, vec, vec,
                  pl.BlockSpec((None, d, hc), lambda i, c: (c, 0, 0)),
                  pl.BlockSpec((None, hc, d), lambda i, c: (c, 0, 0))],
        out_specs=[row, row, hid, vec, vec],
        out_shape=[jax.ShapeDtypeStruct((t, d), F32), jax.ShapeDtypeStruct((t, d), BF16),
                   jax.ShapeDtypeStruct((t, N_CHIPS * hc), BF16),
                   jax.ShapeDtypeStruct((1, d), F32), jax.ShapeDtypeStruct((1, d), F32)],
        scratch_shapes=[pltpu.VMEM((ROW_TILE, d), F32)],
        args=(dxo, x, y, a, gpre, gpost, w1g, w2g), exchange=exchange)


def weight_grad(a, b, chunked, bk, bn, relu2, name, exchange=None):
    t = a.shape[0]
    a_on = chunked == "a"
    rows = min(t, WGRAD_ROWS)
    n_steps = t // rows

    def body(a_ref, b_ref, o_ref, acc):
        s = pl.program_id(1)
        av = a_ref[...]
        if relu2:
            av = jnp.square(jnp.maximum(av.astype(F32), 0.0))
        _accumulate(acc, _dot_tn(av.astype(BF16), b_ref[...].astype(BF16)), s == 0)

        @pl.when(s == n_steps - 1)
        def _():
            o_ref[...] = acc[...].astype(BF16)

    res = _call(
        body, name=name, grid=(N_CHIPS, n_steps),
        in_specs=[pl.BlockSpec((rows, bk), (lambda c, s: (s, c)) if a_on else (lambda c, s: (s, 0))),
                  pl.BlockSpec((rows, bn), (lambda c, s: (s, 0)) if a_on else (lambda c, s: (s, c)))],
        out_specs=[pl.BlockSpec((None, bk, bn), lambda c, s: (c, 0, 0))],
        out_shape=[jax.ShapeDtypeStruct((N_CHIPS, bk, bn), BF16)],
        scratch_shapes=[pltpu.VMEM((bk, bn), F32)],
        args=(a, b), exchange=exchange)
    return res[0] if exchange is None else res


def _hgrn2_chunk(st, qs, fls, ivs, gls, l0, l1, l2, ng):
    nsub = len(qs)
    mx = jnp.maximum(jnp.maximum(l0, l1), l2)
    e0, e1, e2 = jnp.exp(l0 - mx), jnp.exp(l1 - mx), jnp.exp(l2 - mx)
    lb = e0 / (e0 + e1 + e2)
    rows = lax.broadcasted_iota(jnp.int32, (A_SUB, A_SUB), 0)
    cols = lax.broadcasted_iota(jnp.int32, (A_SUB, A_SUB), 1)
    tri = (rows >= cols).astype(F32)
    keep = (lax.broadcasted_iota(jnp.int32, (A_SUB, A_SUB, A_DK), 0)
            >= lax.broadcasted_iota(jnp.int32, (A_SUB, A_SUB, A_DK), 1))
    base = jnp.zeros_like(l0)
    bases, gs, ks, qfs = [], [], [], []
    for i in range(nsub):
        f = lb + (1.0 - lb) * jax.nn.sigmoid(fls[i])
        logf = jnp.log(f)
        bases.append(base)
        gs.append(base + jnp.dot(tri, logf, precision=lax.Precision.HIGHEST, preferred_element_type=F32))
        base = base + jnp.sum(logf, axis=0, keepdims=True)
        ks.append(1.0 - f)
        qfs.append(jax.nn.silu(qs[i]))
    g_last = base
    stb = st.astype(BF16)
    outs = []
    for i in range(nsub):
        o = _dot_nt((qfs[i] * jnp.exp(gs[i])).astype(BF16), stb)
        if i > 0:
            qt = (qfs[i] * jnp.exp(gs[i] - bases[i])).astype(BF16)
            kk = jnp.concatenate([ks[j] * jnp.exp(bases[i] - gs[j]) for j in range(i)], axis=0).astype(BF16)
            vv = jnp.concatenate(ivs[:i], axis=0).astype(BF16)
            o = o + _dot(_dot_nt(qt, kk).astype(BF16), vv)
        dec = jnp.exp(jnp.where(keep, gs[i][:, None, :] - gs[i][None, :, :], NEG_BIG))
        s_diag = jnp.sum(qfs[i][:, None, :] * ks[i][None, :, :] * dec, axis=-1)
        o = o + _dot(s_diag.astype(BF16), ivs[i].astype(BF16))
        o = o * lax.rsqrt(jnp.mean(o * o, axis=-1, keepdims=True) + EPS) * ng
        outs.append(o * jax.nn.silu(gls[i]))
    kdec = jnp.concatenate([ks[j] * jnp.exp(g_last - gs[j]) for j in range(nsub)], axis=0).astype(BF16)
    vall = jnp.concatenate(ivs, axis=0).astype(BF16)
    new_st = st * jnp.exp(g_last) + _dot_tn(vall, kdec)
    return new_st, outs


A_MAX_LOG_DECAY = 80.0


def _split3(x):
    hi = x.astype(BF16)
    r1 = x - hi.astype(F32)
    mid = r1.astype(BF16)
    return hi, mid, (r1 - mid.astype(F32)).astype(BF16)


def _tri_matmul(x, transpose):
    n = x.shape[0]
    r = lax.broadcasted_iota(jnp.int32, (n, n), 0)
    c = lax.broadcasted_iota(jnp.int32, (n, n), 1)
    tri = ((r <= c) if transpose else (r >= c)).astype(BF16)
    hi, mid, lo = _split3(x)
    return (_dot(tri, lo) + _dot(tri, mid)) + _dot(tri, hi)


@jax.custom_vjp
def _cumsum_rows(x):
    return _tri_matmul(x, False)


def _cumsum_rows_fwd(x):
    return _tri_matmul(x, False), None


def _cumsum_rows_bwd(_, dy):
    return (_tri_matmul(dy, True),)


_cumsum_rows.defvjp(_cumsum_rows_fwd, _cumsum_rows_bwd)


def _lower_bound(l0, l1, l2):
    mx = jnp.maximum(jnp.maximum(l0, l1), l2)
    e0, e1, e2 = jnp.exp(l0 - mx), jnp.exp(l1 - mx), jnp.exp(l2 - mx)
    return e0 / (e0 + e1 + e2)


def _b(x):
    return x.astype(BF16)


@jax.custom_vjp
def _mm(a, b):
    return _dot(_b(a), _b(b))


_mm.defvjp(lambda a, b: (_mm(a, b), (a, b)),
           lambda res, d: (_dot_nt(_b(d), _b(res[1])), _dot_tn(_b(res[0]), _b(d))))


@jax.custom_vjp
def _mm_nt(a, b):
    return _dot_nt(_b(a), _b(b))


_mm_nt.defvjp(lambda a, b: (_mm_nt(a, b), (a, b)),
              lambda res, d: (_dot(_b(d), _b(res[1])), _dot_tn(_b(d), _b(res[0]))))


def _dot_split(dot, a, b):
    ah, bh = _b(a), _b(b)
    al, bl = _b(a - ah.astype(F32)), _b(b - bh.astype(F32))
    return (dot(ah, bl) + dot(al, bh)) + dot(ah, bh)


@jax.custom_vjp
def _mm_scores(a, b):
    return _dot_nt(_b(a), _b(b))


_mm_scores.defvjp(lambda a, b: (_mm_scores(a, b), (a, b)),
                  lambda res, d: (_dot_split(_dot, d, res[1]), _dot_split(_dot_tn, d, res[0])))


@jax.custom_vjp
def _mm_tn(a, b):
    return _dot_tn(_b(a), _b(b))


_mm_tn.defvjp(lambda a, b: (_mm_tn(a, b), (a, b)),
              lambda res, d: (_dot_nt(_b(res[1]), _b(d)), _dot(_b(res[0]), _b(d))))


@jax.custom_vjp
def _split_heads(x):
    return tuple(x[:, h * A_DK:(h + 1) * A_DK] for h in range(A_HEADS))


def _split_heads_fwd(x):
    return _split_heads(x), None


def _split_heads_bwd(_, parts):
    return (jnp.concatenate(parts, axis=1),)


_split_heads.defvjp(_split_heads_fwd, _split_heads_bwd)


def _hgrn2_chunk_fast(sts, q, fl, iv, gl, l0, l1, l2, ng):
    lb = _lower_bound(l0, l1, l2)
    f = lb + (1.0 - lb) * jax.nn.sigmoid(fl)
    return _hgrn2_fast_core(sts, q, f, jnp.log(f), iv, gl, ng)


def _hgrn2_fast_core(sts, q, f, logf, iv, gl, ng):
    g = _cumsum_rows(logf)
    g_last = jnp.sum(logf, axis=0, keepdims=True)
    k = 1.0 - f
    qgs = _split_heads(jax.nn.silu(q) * jnp.exp(g))
    kgs = _split_heads(k * jnp.exp(-g))
    kds = _split_heads(k * jnp.exp(g_last - g))
    ivs = _split_heads(iv)
    decays = _split_heads(jnp.exp(g_last))
    n = q.shape[0]
    causal = lax.broadcasted_iota(jnp.int32, (n, n), 0) >= lax.broadcasted_iota(jnp.int32, (n, n), 1)
    raw = [_mm_scores(qg, kg) for qg, kg in zip(qgs, kgs)]
    inter = [_mm_nt(qg, st) for qg, st in zip(qgs, sts)]
    scores = [jnp.where(causal, s, 0.0) for s in raw]
    os = [a + _mm(s, v) for a, s, v in zip(inter, scores, ivs)]
    new_sts = [st * d + _mm_tn(v, kd) for st, d, v, kd in zip(sts, decays, ivs, kds)]
    os = [o * lax.rsqrt(jnp.mean(o * o, axis=-1, keepdims=True) + EPS) for o in os]
    return new_sts, jnp.concatenate(os, axis=1) * ng * jax.nn.silu(gl)


A_STEP_CHUNKS = 4


def _chunk_rows(j):
    return pl.ds(pl.multiple_of(j * A_CHUNK, A_CHUNK), A_CHUNK)


def _sub_rows(j, i):
    return pl.ds(pl.multiple_of(j * A_CHUNK + i * A_SUB, A_SUB), A_SUB)


def _sub_blocks(ref, head, j):
    lanes = slice(head * A_DK, (head + 1) * A_DK)
    return [ref[_sub_rows(j, i), lanes] for i in range(A_CHUNK // A_SUB)]


def hgrn2_fwd(proj, lb_table, a_norm, batch, name, exchange=None):
    t = proj.shape[0]
    n_steps = t // batch // (A_CHUNK * A_STEP_CHUNKS)
    rows = A_CHUNK * A_STEP_CHUNKS

    def body(q_ref, f_ref, i_ref, g_ref, lb_ref, ng_ref, o_ref, st_ref, dec_ref, st):
        @pl.when(pl.program_id(1) == 0)
        def _():
            st[...] = jnp.zeros_like(st)

        def chunk(j, carry):
            r = _chunk_rows(j)
            st_ref[j] = st[...]
            lb = _lower_bound(lb_ref[0:1, :], lb_ref[1:2, :], lb_ref[2:3, :])
            f = lb + (1.0 - lb) * jax.nn.sigmoid(f_ref[r, :])
            logf = jnp.log(f)
            decay = jnp.sum(logf, axis=0, keepdims=True)
            dec_ref[j] = decay
            mild = jnp.min(decay) >= -A_MAX_LOG_DECAY

            @pl.when(mild)
            def _():
                new_sts, o = _hgrn2_fast_core([st[h] for h in range(A_HEADS)], q_ref[r, :], f, logf,
                                              i_ref[r, :], g_ref[r, :], ng_ref[...])
                for h in range(A_HEADS):
                    st[h] = new_sts[h]
                o_ref[r, :] = o.astype(BF16)

            @pl.when(jnp.logical_not(mild))
            def _():
                for h in range(A_HEADS):
                    lanes = slice(h * A_DK, (h + 1) * A_DK)
                    new_st, outs = _hgrn2_chunk(
                        st[h], _sub_blocks(q_ref, h, j), _sub_blocks(f_ref, h, j), _sub_blocks(i_ref, h, j),
                        _sub_blocks(g_ref, h, j), lb_ref[0:1, lanes], lb_ref[1:2, lanes], lb_ref[2:3, lanes],
                        ng_ref[:, lanes])
                    st[h] = new_st
                    for i, o in enumerate(outs):
                        o_ref[_sub_rows(j, i), lanes] = o.astype(BF16)

            return carry

        lax.fori_loop(0, A_STEP_CHUNKS, chunk, 0)

    def part(k):
        return pl.BlockSpec((rows, A_WIDTH), lambda b, n: (b * n_steps + n, k))

    return _call(
        body, name=name, grid=(batch, n_steps),
        in_specs=[part(0), part(1), part(2), part(3),
                  pl.BlockSpec((3, A_WIDTH), lambda b, n: (0, 0)), pl.BlockSpec((1, A_WIDTH), lambda b, n: (0, 0))],
        out_specs=[part(0),
                   pl.BlockSpec((A_STEP_CHUNKS, A_HEADS, A_DK, A_DK), lambda b, n: (b * n_steps + n, 0, 0, 0)),
                   pl.BlockSpec((A_STEP_CHUNKS, 1, A_WIDTH), lambda b, n: (b * n_steps + n, 0, 0))],
        out_shape=[jax.ShapeDtypeStruct((t, A_WIDTH), BF16),
                   jax.ShapeDtypeStruct((t // A_CHUNK, A_HEADS, A_DK, A_DK), F32),
                   jax.ShapeDtypeStruct((t // A_CHUNK, 1, A_WIDTH), F32)],
        scratch_shapes=[pltpu.VMEM((A_HEADS, A_DK, A_DK), F32)],
        args=(proj, proj, proj, proj, lb_table, a_norm), exchange=exchange)


def hgrn2_bwd(proj, states, decays, lb_table, a_norm, do, batch, name, exchange=None):
    t = proj.shape[0]
    n_steps = t // batch // (A_CHUNK * A_STEP_CHUNKS)
    rows = A_CHUNK * A_STEP_CHUNKS

    def body(q_ref, f_ref, i_ref, g_ref, st_ref, dec_ref, lb_ref, ng_ref, do_ref, dp_ref, dlb_ref, dng_ref, dst):
        @pl.when(jnp.logical_and(pl.program_id(0) == 0, pl.program_id(1) == 0))
        def _():
            dlb_ref[...] = jnp.zeros_like(dlb_ref)
            dng_ref[...] = jnp.zeros_like(dng_ref)

        @pl.when(pl.program_id(1) == 0)
        def _():
            dst[...] = jnp.zeros_like(dst)

        def chunk(jj, carry):
            j = A_STEP_CHUNKS - 1 - jj
            r = _chunk_rows(j)
            mild = jnp.min(dec_ref[j]) >= -A_MAX_LOG_DECAY

            @pl.when(mild)
            def _():
                _, vjp = jax.vjp(
                    _hgrn2_chunk_fast, [st_ref[j, h] for h in range(A_HEADS)], q_ref[r, :], f_ref[r, :],
                    i_ref[r, :], g_ref[r, :], lb_ref[0:1, :], lb_ref[1:2, :], lb_ref[2:3, :], ng_ref[...])
                d_sts, dq, df, di, dg, dl0, dl1, dl2, dng = vjp(
                    ([dst[h] for h in range(A_HEADS)], do_ref[r, :].astype(F32)))
                for h in range(A_HEADS):
                    dst[h] = d_sts[h]
                for k, part in enumerate((dq, df, di, dg)):
                    dp_ref[r, k * A_WIDTH:(k + 1) * A_WIDTH] = part
                for row, val in enumerate((dl0, dl1, dl2)):
                    dlb_ref[row:row + 1, :] += val
                dng_ref[...] += dng

            @pl.when(jnp.logical_not(mild))
            def _():
                for h in range(A_HEADS):
                    lanes = slice(h * A_DK, (h + 1) * A_DK)
                    _, vjp = jax.vjp(
                        _hgrn2_chunk, st_ref[j, h], _sub_blocks(q_ref, h, j), _sub_blocks(f_ref, h, j),
                        _sub_blocks(i_ref, h, j), _sub_blocks(g_ref, h, j), lb_ref[0:1, lanes], lb_ref[1:2, lanes],
                        lb_ref[2:3, lanes], ng_ref[:, lanes])
                    douts = [x.astype(F32) for x in _sub_blocks(do_ref, h, j)]
                    d_st, dqs, dfs, dis, dgs, dl0, dl1, dl2, dng = vjp((dst[h], douts))
                    dst[h] = d_st
                    for k, parts in enumerate((dqs, dfs, dis, dgs)):
                        for i in range(A_CHUNK // A_SUB):
                            dp_ref[_sub_rows(j, i), k * A_WIDTH + h * A_DK:k * A_WIDTH + (h + 1) * A_DK] = parts[i]
                    for row, val in enumerate((dl0, dl1, dl2)):
                        dlb_ref[row:row + 1, lanes] += val
                    dng_ref[:, lanes] += dng

            return carry

        lax.fori_loop(0, A_STEP_CHUNKS, chunk, 0)

    def rev(b, n):
        return b * n_steps + (n_steps - 1 - n)

    def part(k):
        return pl.BlockSpec((rows, A_WIDTH), lambda b, n: (rev(b, n), k))

    const3 = pl.BlockSpec((3, A_WIDTH), lambda b, n: (0, 0))
    const1 = pl.BlockSpec((1, A_WIDTH), lambda b, n: (0, 0))
    return _call(
        body, name=name, grid=(batch, n_steps),
        in_specs=[part(0), part(1), part(2), part(3),
                  pl.BlockSpec((A_STEP_CHUNKS, A_HEADS, A_DK, A_DK), lambda b, n: (rev(b, n), 0, 0, 0)),
                  pl.BlockSpec((A_STEP_CHUNKS, 1, A_WIDTH), lambda b, n: (rev(b, n), 0, 0)),
                  const3, const1, part(0)],
        out_specs=[pl.BlockSpec((rows, 4 * A_WIDTH), lambda b, n: (rev(b, n), 0)), const3, const1],
        out_shape=[jax.ShapeDtypeStruct((t, 4 * A_WIDTH + 2 * B_WIDTH), F32),
                   jax.ShapeDtypeStruct((3, A_WIDTH), F32), jax.ShapeDtypeStruct((1, A_WIDTH), F32)],
        scratch_shapes=[pltpu.VMEM((A_HEADS, A_DK, A_DK), F32)],
        args=(proj, proj, proj, proj, states, decays, lb_table, a_norm, do), exchange=exchange)


B_GDIM = B_WIDTH // B_GROUPS
B_ROWS = 512


def _gmlp_chunk(ubs, vbs, lngs, lnbs, ws, bcols):
    vs = [jax.nn.gelu(v) for v in vbs]
    mu = sum(jnp.sum(v, axis=-1, keepdims=True) for v in vs) * (1.0 / B_WIDTH)
    var = sum(jnp.sum(jnp.square(v - mu), axis=-1, keepdims=True) for v in vs) * (1.0 / B_WIDTH)
    rstd = lax.rsqrt(var + EPS)
    tril = (lax.broadcasted_iota(jnp.int32, (B_CHUNK, B_CHUNK), 0)
            >= lax.broadcasted_iota(jnp.int32, (B_CHUNK, B_CHUNK), 1))
    outs = []
    for g in range(B_GROUPS):
        vn = (vs[g] - mu) * rstd * lngs[g] + lnbs[g]
        w = jnp.where(tril, ws[g], 0.0).astype(BF16)
        outs.append(jax.nn.gelu(ubs[g]) * (_dot(w, vn.astype(BF16)) + bcols[g]))
    return outs


def _gmlp_args(u_ref, v_ref, lng_ref, lnb_ref, w_ref, bt_ref, rows):
    def groups(ref):
        return [ref[rows, g * B_GDIM:(g + 1) * B_GDIM] for g in range(B_GROUPS)]

    def vec(ref):
        return [ref[:, g * B_GDIM:(g + 1) * B_GDIM] for g in range(B_GROUPS)]

    return (groups(u_ref), groups(v_ref), vec(lng_ref), vec(lnb_ref),
            [w_ref[g] for g in range(B_GROUPS)], [bt_ref[:, g:g + 1] for g in range(B_GROUPS)])


def gmlp_fwd(proj, oa, ln_g, ln_b, w, bias_t, name, exchange=None):
    t = proj.shape[0]

    def body(u_ref, v_ref, oa_ref, lng_ref, lnb_ref, w_ref, bt_ref, o_ref):
        o_ref[:, 0:A_WIDTH] = oa_ref[...]
        for n in range(B_ROWS // B_CHUNK):
            rows = slice(n * B_CHUNK, (n + 1) * B_CHUNK)
            outs = _gmlp_chunk(*_gmlp_args(u_ref, v_ref, lng_ref, lnb_ref, w_ref, bt_ref, rows))
            for g, o in enumerate(outs):
                o_ref[rows, A_WIDTH + g * B_GDIM:A_WIDTH + (g + 1) * B_GDIM] = o.astype(BF16)

    vec = pl.BlockSpec((1, B_WIDTH), lambda i: (0, 0))
    return _call(
        body, name=name, grid=(t // B_ROWS,),
        in_specs=[pl.BlockSpec((B_ROWS, B_WIDTH), lambda i: (i, 4)), pl.BlockSpec((B_ROWS, B_WIDTH), lambda i: (i, 5)),
                  pl.BlockSpec((B_ROWS, A_WIDTH), lambda i: (i, 0)), vec, vec,
                  pl.BlockSpec((B_GROUPS, B_CHUNK, B_CHUNK), lambda i: (0, 0, 0)),
                  pl.BlockSpec((B_CHUNK, B_GROUPS), lambda i: (0, 0))],
        out_specs=[pl.BlockSpec((B_ROWS, A_WIDTH + B_WIDTH), lambda i: (i, 0))],
        out_shape=[jax.ShapeDtypeStruct((t, A_WIDTH + B_WIDTH), BF16)],
        args=(proj, proj, oa, ln_g, ln_b, w, bias_t), exchange=exchange)


def gmlp_bwd(proj, dmixin, ln_g, ln_b, w, bias_t, dproj, name, exchange=None):
    t = proj.shape[0]

    def body(u_ref, v_ref, do_ref, lng_ref, lnb_ref, w_ref, bt_ref, dp_in_ref,
             dp_ref, dlng_ref, dlnb_ref, dw_ref, dbt_ref):
        del dp_in_ref

        @pl.when(pl.program_id(0) == 0)
        def _():
            for ref in (dlng_ref, dlnb_ref, dw_ref, dbt_ref):
                ref[...] = jnp.zeros_like(ref)

        for n in range(B_ROWS // B_CHUNK):
            rows = slice(n * B_CHUNK, (n + 1) * B_CHUNK)
            _, vjp = jax.vjp(_gmlp_chunk, *_gmlp_args(u_ref, v_ref, lng_ref, lnb_ref, w_ref, bt_ref, rows))
            douts = [do_ref[rows, g * B_GDIM:(g + 1) * B_GDIM] for g in range(B_GROUPS)]
            dus, dvs, dlngs, dlnbs, dws, dbs = vjp(douts)
            for g in range(B_GROUPS):
                lanes = slice(g * B_GDIM, (g + 1) * B_GDIM)
                dp_ref[rows, lanes] = dus[g]
                dp_ref[rows, B_WIDTH + g * B_GDIM:B_WIDTH + (g + 1) * B_GDIM] = dvs[g]
                dlng_ref[:, lanes] += dlngs[g]
                dlnb_ref[:, lanes] += dlnbs[g]
                dw_ref[g] += dws[g]
                dbt_ref[:, g:g + 1] += dbs[g]

    vec = pl.BlockSpec((1, B_WIDTH), lambda i: (0, 0))
    wspec = pl.BlockSpec((B_GROUPS, B_CHUNK, B_CHUNK), lambda i: (0, 0, 0))
    bspec = pl.BlockSpec((B_CHUNK, B_GROUPS), lambda i: (0, 0))
    return _call(
        body, name=name, grid=(t // B_ROWS,),
        in_specs=[pl.BlockSpec((B_ROWS, B_WIDTH), lambda i: (i, 4)), pl.BlockSpec((B_ROWS, B_WIDTH), lambda i: (i, 5)),
                  pl.BlockSpec((B_ROWS, B_WIDTH), lambda i: (i, 1)), vec, vec, wspec, bspec,
                  pl.BlockSpec(memory_space=pl.ANY)],
        out_specs=[pl.BlockSpec((B_ROWS, 2 * B_WIDTH), lambda i: (i, 2)), vec, vec, wspec, bspec],
        out_shape=[jax.ShapeDtypeStruct(dproj.shape, F32), jax.ShapeDtypeStruct((1, B_WIDTH), F32),
                   jax.ShapeDtypeStruct((1, B_WIDTH), F32), jax.ShapeDtypeStruct((B_GROUPS, B_CHUNK, B_CHUNK), F32),
                   jax.ShapeDtypeStruct((B_CHUNK, B_GROUPS), F32)],
        aliases={7: 0}, args=(proj, proj, dmixin, ln_g, ln_b, w, bias_t, dproj), exchange=exchange)


C_FWD_BLOCKS = 8
C_BWD_BLOCKS = 4
C_PAIR = 2 * C_HEAD_DIM
C_PAIRS = C_HEADS // 2
C_SCALE = 1.0 / math.sqrt(C_HEAD_DIM)
C_ROT_DIM = 2 * C_ROT_HALF
ROPE_ROWS = 1024


def rope_tables(pos_col, name):
    t = pos_col.shape[0]

    def body(p_ref, c_ref, a_ref, b_ref):
        lane = jnp.bitwise_and(lax.broadcasted_iota(jnp.int32, (1, C_PAIR), 1), C_HEAD_DIM - 1)
        j = jnp.bitwise_and(lane, C_ROT_HALF - 1).astype(F32)
        inv = jnp.exp(j * (-math.log(ROPE_THETA) / C_ROT_HALF))
        ang = p_ref[...].astype(F32) * inv
        cos, sin = jnp.cos(ang), jnp.sin(ang)
        c_ref[...] = jnp.where(lane < C_ROT_DIM, cos, 1.0)
        a_ref[...] = jnp.where(lane < C_ROT_HALF, -sin, 0.0)
        b_ref[...] = jnp.where(jnp.logical_and(lane >= C_ROT_HALF, lane < C_ROT_DIM), sin, 0.0)

    tab = pl.BlockSpec((ROPE_ROWS, C_PAIR), lambda i: (i, 0))
    return pl.pallas_call(
        body, name=name, grid=(t // ROPE_ROWS,),
        in_specs=[pl.BlockSpec((ROPE_ROWS, 1), lambda i: (i, 0))],
        out_specs=[tab, tab, tab],
        out_shape=[jax.ShapeDtypeStruct((t, C_PAIR), F32)] * 3,
        compiler_params=_params(("arbitrary",)),
    )(pos_col)


def _rope(x, c, a, b):
    return x * c + pltpu.roll(x, C_PAIR - C_ROT_HALF, 1) * a + pltpu.roll(x, C_ROT_HALF, 1) * b


def _rope_t(d, c, a, b):
    return d * c + pltpu.roll(d * a, C_ROT_HALF, 1) + pltpu.roll(d * b, C_PAIR - C_ROT_HALF, 1)


def _attn_rows(idx, dil):
    nblk = SEQ // dil // C_BLOCK
    r, n = idx // nblk, idx % nblk
    start = r + dil * C_BLOCK * n
    prev = r + dil * C_BLOCK * jnp.maximum(n - 1, 0)
    if dil == 1:
        return pl.ds(pl.multiple_of(start, C_BLOCK), C_BLOCK), pl.ds(pl.multiple_of(prev, C_BLOCK), C_BLOCK), n > 0
    return pl.ds(start, C_BLOCK, stride=dil), pl.ds(prev, C_BLOCK, stride=dil), n > 0


def _head_masks():
    low = lax.broadcasted_iota(jnp.int32, (1, C_PAIR), 1) < C_HEAD_DIM
    return low, jnp.logical_not(low)


def _attn_mask(has_prev):
    i = jnp.bitwise_and(lax.broadcasted_iota(jnp.int32, (2 * C_BLOCK, 2 * C_BLOCK), 0), C_BLOCK - 1)
    j = lax.broadcasted_iota(jnp.int32, (2 * C_BLOCK, 2 * C_BLOCK), 1)
    return jnp.logical_or(j <= i, jnp.logical_and(j - C_BLOCK >= i, has_prev))


def _stack_heads(x):
    low, high = _head_masks()
    return jnp.concatenate([jnp.where(low, x, 0.0), jnp.where(high, x, 0.0)], axis=0)


def _unstack_heads(x):
    low, _ = _head_masks()
    return jnp.where(low, x[:C_BLOCK], x[C_BLOCK:])


def attn_fwd(qkv, cos_t, sin_a, sin_b, batch, name, exchange=None):
    t = qkv.shape[0]
    nbr = len(C_DILATIONS)

    def body(q_ref, k_ref, v_ref, c_ref, a_ref, b_ref, o_ref, l_ref, qs, ks, *stats):
        acc, mm, dd = stats[0:nbr], stats[nbr:2 * nbr], stats[2 * nbr:3 * nbr]
        c, a, b = c_ref[...], a_ref[...], b_ref[...]
        qs[...] = _rope(q_ref[...], c, a, b) * C_SCALE
        ks[...] = _rope(k_ref[...], c, a, b)
        def load(idx, dil):
            rows, prev, has_prev = _attn_rows(idx, dil)
            return rows, (has_prev, qs[rows, :], ks[rows, :], ks[prev, :], v_ref[rows, :], v_ref[prev, :])

        def scores(has_prev, q, k_own, k_prev, v_own, v_prev):
            k_cat = jnp.concatenate([k_own, k_prev], axis=0).astype(BF16)
            return jnp.where(_attn_mask(has_prev), _dot_nt(_stack_heads(q).astype(BF16), k_cat), NEG_BIG)

        def softmax(s):
            m = jnp.max(s, axis=-1, keepdims=True)
            p = jnp.exp(s - m)
            return p.astype(BF16), m, jnp.sum(p, axis=-1, keepdims=True)

        def values(pb, has_prev, q, k_own, k_prev, v_own, v_prev):
            low, high = _head_masks()
            v_cat = jnp.concatenate([v_own, v_prev], axis=0)
            p_wide = jnp.concatenate([pb[:C_BLOCK], pb[C_BLOCK:]], axis=1)
            v_tall = jnp.concatenate([jnp.where(low, v_cat, 0.0), jnp.where(high, v_cat, 0.0)], axis=0).astype(BF16)
            return _dot(p_wide, v_tall)

        for bi, dil in enumerate(C_DILATIONS):
            def pair(i, carry, bi=bi, dil=dil):
                low, _ = _head_masks()
                loaded = [load(C_FWD_BLOCKS * i + k, dil) for k in range(C_FWD_BLOCKS)]
                ss = [scores(*ops) for _, ops in loaded]
                sm = [softmax(s) for s in ss]
                pvs = [values(pb, *ops) for (pb, _, _), (_, ops) in zip(sm, loaded)]
                for (rows, _), (_, m, den), pv in zip(loaded, sm, pvs):
                    acc[bi][rows, :] = pv
                    mm[bi][rows, :] = jnp.where(low, m[:C_BLOCK], m[C_BLOCK:])
                    dd[bi][rows, :] = jnp.where(low, den[:C_BLOCK], den[C_BLOCK:])
                return carry

            lax.fori_loop(0, SEQ // C_BLOCK // C_FWD_BLOCKS, pair, 0)
        step = 256
        for r0 in range(0, SEQ, step):
            rr = slice(r0, r0 + step)
            ms = [mm[g][rr, :] for g in range(nbr)]
            m_all = functools.reduce(jnp.maximum, ms)
            ws = [jnp.exp(m - m_all) for m in ms]
            num = sum(acc[g][rr, :] * ws[g] for g in range(nbr))
            den = sum(dd[g][rr, :] * ws[g] for g in range(nbr))
            o_ref[rr, :] = (num / den).astype(BF16)
            l_ref[rr, :] = m_all + jnp.log(den)

    def col(k):
        return pl.BlockSpec((SEQ, C_PAIR), lambda b, p: (b, k * C_PAIRS + p))

    tab = pl.BlockSpec((SEQ, C_PAIR), lambda b, p: (b, 0))
    return _call(
        body, name=name, grid=(batch, C_PAIRS),
        in_specs=[col(0), col(1), col(2), tab, tab, tab],
        out_specs=[col(0), col(0)],
        out_shape=[jax.ShapeDtypeStruct((t, D_MODEL), BF16), jax.ShapeDtypeStruct((t, D_MODEL), F32)],
        scratch_shapes=[pltpu.VMEM((SEQ, C_PAIR), F32)] * (2 + 3 * nbr),
        args=(qkv, qkv, qkv, cos_t, sin_a, sin_b), exchange=exchange)


def attn_bwd(qkv, cos_t, sin_a, sin_b, o, lse, do, batch, name, exchange=None):
    t = qkv.shape[0]

    def body(q_ref, k_ref, v_ref, c_ref, a_ref, b_ref, o_ref, l_ref, do_ref, dq_ref, dk_ref, dv_ref,
             qs, ks, dqs, dks, dvs, dlt):
        c, a, b = c_ref[...], a_ref[...], b_ref[...]
        qs[...] = _rope(q_ref[...], c, a, b) * C_SCALE
        ks[...] = _rope(k_ref[...], c, a, b)
        prod = do_ref[...] * o_ref[...].astype(F32)
        low = lax.broadcasted_iota(jnp.int32, (1, C_PAIR), 1) < C_HEAD_DIM
        s_low = jnp.sum(jnp.where(low, prod, 0.0), axis=-1, keepdims=True)
        s_all = jnp.sum(prod, axis=-1, keepdims=True)
        dlt[...] = jnp.where(low, s_low, s_all - s_low)
        dqs[...] = jnp.zeros_like(dqs)
        dks[...] = jnp.zeros_like(dks)
        dvs[...] = jnp.zeros_like(dvs)
        def load(idx, dil):
            rows, prev, has_prev = _attn_rows(idx, dil)
            return (rows, prev), (has_prev, qs[rows, :], do_ref[rows, :], ks[rows, :], ks[prev, :],
                                  v_ref[rows, :], v_ref[prev, :], l_ref[rows, :], dlt[rows, :])

        def operands(has_prev, q, do, k_own, k_prev, v_own, v_prev, l_full, d_full):
            lcol = jnp.concatenate([l_full[:, 0:1], l_full[:, C_HEAD_DIM:C_HEAD_DIM + 1]], axis=0)
            dcol = jnp.concatenate([d_full[:, 0:1], d_full[:, C_HEAD_DIM:C_HEAD_DIM + 1]], axis=0)
            return (_stack_heads(q).astype(BF16), _stack_heads(do).astype(BF16),
                    jnp.concatenate([k_own, k_prev], axis=0).astype(BF16),
                    jnp.concatenate([v_own, v_prev], axis=0).astype(BF16), lcol, dcol, _attn_mask(has_prev))

        for dil in C_DILATIONS:
            def pair(i, carry, dil=dil):
                loaded = [load(C_BWD_BLOCKS * i + k, dil) for k in range(C_BWD_BLOCKS)]
                ops = [operands(*o) for _, o in loaded]
                ss = [_dot_nt(q_stack, k_cat) for q_stack, _, k_cat, _, _, _, _ in ops]
                dps = [_dot_nt(do_stack, v_cat) for _, do_stack, _, v_cat, _, _, _ in ops]
                ps = [jnp.exp(jnp.where(o[6], s, NEG_BIG) - o[4]) for s, o in zip(ss, ops)]
                dss = [(p * (dp - o[5])).astype(BF16) for p, dp, o in zip(ps, dps, ops)]
                dvs_ = [_dot_tn(p.astype(BF16), o[1]) for p, o in zip(ps, ops)]
                dks_ = [_dot_tn(ds, o[0]) for ds, o in zip(dss, ops)]
                dqs_ = [_unstack_heads(_dot(ds, o[2])) for ds, o in zip(dss, ops)]
                results = list(zip(dqs_, dks_, dvs_))
                for ((rows, prev), _), (dq, dk_cat, dv_cat) in zip(loaded, results):
                    dqs[rows, :] += dq
                    dks[rows, :] += dk_cat[:C_BLOCK]
                    dvs[rows, :] += dv_cat[:C_BLOCK]
                    dks[prev, :] += dk_cat[C_BLOCK:]
                    dvs[prev, :] += dv_cat[C_BLOCK:]
                return carry

            lax.fori_loop(0, SEQ // C_BLOCK // C_BWD_BLOCKS, pair, 0)
        dq_ref[...] = _rope_t(dqs[...] * C_SCALE, c, a, b).astype(BF16)
        dk_ref[...] = _rope_t(dks[...], c, a, b).astype(BF16)
        dv_ref[...] = dvs[...].astype(BF16)

    def col(k):
        return pl.BlockSpec((SEQ, C_PAIR), lambda b, p: (b, k * C_PAIRS + p))

    tab = pl.BlockSpec((SEQ, C_PAIR), lambda b, p: (b, 0))
    out = jax.ShapeDtypeStruct((t, D_MODEL), BF16)
    return _call(
        body, name=name, grid=(batch, C_PAIRS),
        in_specs=[col(0), col(1), col(2), tab, tab, tab, col(0), col(0), col(0)],
        out_specs=[col(0), col(0), col(0)],
        out_shape=[out, out, out],
        scratch_shapes=[pltpu.VMEM((SEQ, C_PAIR), F32)] * 6,
        args=(qkv, qkv, qkv, cos_t, sin_a, sin_b, o, lse, do), exchange=exchange)


def sibling_swap(arrays, name):
    n = len(arrays)

    def body(*refs):
        ins, outs = refs[:n], refs[n:2 * n]
        send_sems, recv_sems = refs[2 * n:]
        x, y, c, _ = _place()
        sends = []
        for a in range(n):
            cp = pltpu.make_async_remote_copy(
                src_ref=ins[a], dst_ref=outs[a], send_sem=send_sems.at[a], recv_sem=recv_sems.at[a],
                device_id=(x, y, 1 - c), device_id_type=MESH)
            cp.start()
            sends.append(cp)
        for cp in sends:
            cp.wait_recv()
        for cp in sends:
            cp.wait_send()

    return pl.pallas_call(
        body, name=name,
        in_specs=[ANY] * n, out_specs=[ANY] * n,
        out_shape=[jax.ShapeDtypeStruct(s.shape, s.dtype) for s in arrays],
        scratch_shapes=[pltpu.SemaphoreType.DMA((n,)), pltpu.SemaphoreType.DMA((n,))],
    )(*arrays)


def allreduce_small(slab, name):
    rows, lanes = slab.shape

    def body(x_ref, out_ref, gath, send_sems, recv_sems, local_sem):
        x, y, c, chips = _place()
        me, sibling = (x, y, c), (x, y, 1 - c)

        def slot(px, py, pc):
            return gath.at[4 * px + 2 * py + pc]

        def copy(k, block, to, src=None):
            return pltpu.make_async_remote_copy(
                src_ref=slot(*block) if src is None else src, dst_ref=slot(*block),
                send_sem=send_sems.at[k], recv_sem=recv_sems.at[k], device_id=to, device_id_type=MESH)

        mine = pltpu.make_async_copy(x_ref, slot(*me), local_sem)
        mine.start()
        first = [copy(0, me, sibling, src=x_ref)]
        first += [copy(1 + j, me, (*chip, c), src=x_ref) for j, chip in enumerate(chips)]
        for cp in first:
            cp.start()
        passed = [copy(4 + j, (*chip, c), sibling) for j, chip in enumerate(chips)]
        for j, chip in enumerate(chips):
            copy(1 + j, (*chip, c), me).wait_recv()
            passed[j].start()
        copy(0, sibling, me).wait_recv()
        for j, chip in enumerate(chips):
            copy(4 + j, (*chip, 1 - c), me).wait_recv()
        for cp in first + passed:
            cp.wait_send()
        mine.wait()
        total = gath[0]
        for d in range(1, N_DEV):
            total = total + gath[d]
        out_ref[...] = total

    return pl.pallas_call(
        body, name=name,
        in_specs=[pl.BlockSpec(memory_space=pltpu.VMEM)],
        out_specs=pl.BlockSpec(memory_space=pltpu.VMEM),
        out_shape=jax.ShapeDtypeStruct((rows, lanes), F32),
        scratch_shapes=[pltpu.VMEM((N_DEV, rows, lanes), F32),
                        pltpu.SemaphoreType.DMA((7,)), pltpu.SemaphoreType.DMA((7,)), pltpu.SemaphoreType.DMA],
    )(slab)


ELT_ROWS = 512


def reduce_slabs(r, name, part=0, parts=1, into=None):
    _, rows, cols = r.shape
    br = min(rows, ELT_ROWS)
    nblk = rows // br

    def body(r_ref, *rest):
        o_ref = rest[-1]
        o_ref[...] = ((r_ref[3].astype(F32) + r_ref[0].astype(F32)) + r_ref[1].astype(F32)) + r_ref[2].astype(F32)

    return pl.pallas_call(
        body, name=name, grid=(nblk,),
        in_specs=[pl.BlockSpec((N_CHIPS, br, cols), lambda i: (0, i, 0))] + ([] if into is None else [ANY]),
        out_specs=pl.BlockSpec((br, cols), lambda i: (part * nblk + i, 0)),
        out_shape=jax.ShapeDtypeStruct((parts * rows, cols), F32),
        input_output_aliases={} if into is None else {1: 0},
        compiler_params=_params(("arbitrary",)),
    )(*([r] if into is None else [r, into]))


def _adamw(w, g, m, v):
    m = ADAM_B1 * m + (1.0 - ADAM_B1) * g
    v = ADAM_B2 * v + (1.0 - ADAM_B2) * jnp.square(g)
    m_hat = m / (1.0 - ADAM_B1 ** ADAM_STEP)
    v_hat = v / (1.0 - ADAM_B2 ** ADAM_STEP)
    delta = -ADAM_LR * (m_hat / (jnp.sqrt(v_hat) + ADAM_EPS) + ADAM_WD * w)
    return delta, m, v


def adamw_big(w, s_mine, s_sibling, m, v, name):
    rows, cols = w.shape

    def body(w_ref, a_ref, b_ref, m_ref, v_ref, g_out, d_out, m_out, v_out):
        g = a_ref[...] + b_ref[...]
        g_out[...] = g
        d_out[...], m_out[...], v_out[...] = _adamw(w_ref[...], g, m_ref[...], v_ref[...])

    blk = pl.BlockSpec((min(rows, ELT_ROWS), cols), lambda i: (i, 0))
    out = jax.ShapeDtypeStruct((rows, cols), F32)
    return pl.pallas_call(
        body, name=name, grid=(rows // min(rows, ELT_ROWS),),
        in_specs=[blk] * 5, out_specs=[blk] * 4, out_shape=[out] * 4,
        compiler_params=_params(("arbitrary",)),
    )(w, s_mine, s_sibling, m, v)


def adamw_small(ws, gs, ms, vs, name):
    n = len(ws)

    def body(*refs):
        w_refs, g_refs, m_refs, v_refs = (refs[k * n:(k + 1) * n] for k in range(4))
        d_out, m_out, v_out = (refs[(4 + k) * n:(5 + k) * n] for k in range(3))
        for i in range(n):
            d_out[i][...], m_out[i][...], v_out[i][...] = _adamw(
                w_refs[i][...], g_refs[i][...], m_refs[i][...], v_refs[i][...])

    outs = [jax.ShapeDtypeStruct(w.shape, F32) for w in ws]
    res = pl.pallas_call(body, name=name, out_shape=outs * 3)(*ws, *gs, *ms, *vs)
    return res[:n], res[n:2 * n], res[2 * n:]


SLAB_LANES = 128
SLAB_ROW_ALIGN = 8


def _pack(parts):
    flat = jnp.concatenate([p.reshape(-1) for p in parts])
    rows = -(-flat.shape[0] // (SLAB_LANES * SLAB_ROW_ALIGN)) * SLAB_ROW_ALIGN
    flat = jnp.pad(flat, (0, rows * SLAB_LANES - flat.shape[0]))
    return flat.reshape(rows, SLAB_LANES)


def _unpack(slab, shapes):
    flat = slab.reshape(-1)
    out, pos = [], 0
    for s in shapes:
        size = math.prod(s)
        out.append(flat[pos:pos + size].reshape(s))
        pos += size
    return out


def kernel(x, positions, norm_mix_pre, norm_mix_post, norm_ffn_pre, norm_ffn_post, w_in_even, lb_table, a_norm, b_ln_g, b_ln_b, b_ws, b_bias, w_out_even, w_in_odd, w_out_odd, w_ff1, w_ff2, loss_target, m_norm_mix_pre, m_norm_mix_post, m_norm_ffn_pre, m_norm_ffn_post, m_w_in_even, m_lb_table, m_a_norm, m_b_ln_g, m_b_ln_b, m_b_ws, m_b_bias, m_w_out_even, m_w_in_odd, m_w_out_odd, m_w_ff1, m_w_ff2, v_norm_mix_pre, v_norm_mix_post, v_norm_ffn_pre, v_norm_ffn_post, v_w_in_even, v_lb_table, v_a_norm, v_b_ln_g, v_b_ln_b, v_b_ws, v_b_bias, v_w_out_even, v_w_in_odd, v_w_out_odd, v_w_ff1, v_w_ff2):
    batch = x.shape[0]
    t = batch * SEQ
    d = D_MODEL
    x0 = x.reshape(t, d)
    target = loss_target.reshape(t, d)

    def gain(p, layer):
        return p[layer:layer + 1]

    def gather(*shards):
        return _Exchange("gather", [w.astype(BF16) for w in shards])

    def scatter(*grads):
        return _Exchange("scatter", grads)

    (win_e,) = exchange_alone(gather(w_in_even[0]), "gather_in_even")
    bias_t = b_bias[0].T
    proj, h0, w1_0 = norm_matmul(x0, gain(norm_mix_pre, 0), win_e, "in_proj_even", exchange=gather(w_ff1[0]))
    oa, states, decays, w2_0, wout_e = hgrn2_fwd(proj, lb_table, a_norm, batch, "hgrn2_fwd",
                                                 exchange=gather(w_ff2[0], w_out_even[0]))
    (mixin,) = gmlp_fwd(proj, oa, b_ln_g, b_ln_b, b_ws[0], bias_t, "gmlp_fwd")
    mix0, x1 = out_proj(mixin, wout_e, x0, gain(norm_mix_post, 0), "out_proj_even")
    x2, hf0, a0, y0, win_o, wout_o = ffn_fwd(x1, gain(norm_ffn_pre, 0), w1_0, w2_0, gain(norm_ffn_post, 0),
                                             "ffn_fwd_0", exchange=gather(w_in_odd[0], w_out_odd[0]))
    qkv, h1 = norm_matmul(x2, gain(norm_mix_pre, 1), win_o, "in_proj_odd")
    cos_t, sin_a, sin_b = rope_tables(positions.reshape(t, 1), "rope_tables")
    ao, lse, w1_1, w2_1 = attn_fwd(qkv, cos_t, sin_a, sin_b, batch, "attn_fwd", exchange=gather(w_ff1[1], w_ff2[1]))
    mix1, x3 = out_proj(ao, wout_o, x2, gain(norm_mix_post, 1), "out_proj_odd")
    dx4, hf1, a1, y1, loss_part = ffn_fwd(x3, gain(norm_ffn_pre, 1), w1_1, w2_1, gain(norm_ffn_post, 1),
                                          "ffn_fwd_1", target=target)

    hc = D_FF // N_CHIPS
    dx3, dy1, da1, dg_fpre1, dg_fpost1 = ffn_bwd(
        dx4, x3, y1, a1, gain(norm_ffn_pre, 1), gain(norm_ffn_post, 1), w1_1, w2_1, "ffn_bwd_1")
    g_w1_1 = weight_grad(hf1, da1, "b", d, hc, False, "wgrad_ff1_1")
    g_w2_1 = weight_grad(a1, dy1, "a", hc, d, True, "wgrad_ff2_1")
    dmix1, dao, dg_mpost1 = out_proj_bwd(dx3, mix1, gain(norm_mix_post, 1), wout_o, "out_proj_bwd_odd")
    g_wout_o = weight_grad(ao, dmix1, "a", d // N_CHIPS, d, False, "wgrad_out_odd")
    dq, dk, dv, r_w1_1, r_w2_1, r_wout_o = attn_bwd(qkv, cos_t, sin_a, sin_b, ao, lse, dao, batch, "attn_bwd",
                                                    exchange=scatter(g_w1_1, g_w2_1, g_wout_o))
    dqkv = jnp.concatenate([dq, dk, dv], axis=1)
    dx2, dg_mpre1 = norm_matmul_bwd(dqkv, win_o, x2, gain(norm_mix_pre, 1), dx3, "in_proj_bwd_odd")
    g_win_o = weight_grad(h1, dqkv, "b", d, 3 * d // N_CHIPS, False, "wgrad_in_odd")
    dx1, dy0, da0, dg_fpre0, dg_fpost0, r_win_o = ffn_bwd(
        dx2, x1, y0, a0, gain(norm_ffn_pre, 0), gain(norm_ffn_post, 0), w1_0, w2_0, "ffn_bwd_0",
        exchange=scatter(g_win_o))
    g_w1_0 = weight_grad(hf0, da0, "b", d, hc, False, "wgrad_ff1_0")
    g_w2_0 = weight_grad(a0, dy0, "a", hc, d, True, "wgrad_ff2_0")
    dmix0, dmixin, dg_mpost0 = out_proj_bwd(dx1, mix0, gain(norm_mix_post, 0), wout_e, "out_proj_bwd_even")
    g_wout_e = weight_grad(mixin, dmix0, "a", d // N_CHIPS, d, False, "wgrad_out_even")
    dproj, d_lb, d_anorm, r_w1_0 = hgrn2_bwd(
        proj, states, decays, lb_table, a_norm, dmixin, batch, "hgrn2_bwd", exchange=scatter(g_w1_0))
    dproj, d_lng, d_lnb, d_ws, d_bias_t, r_wout_e = gmlp_bwd(
        proj, dmixin, b_ln_g, b_ln_b, b_ws[0], bias_t, dproj, "gmlp_bwd", exchange=scatter(g_wout_e))
    g_win_e, r_w2_0 = weight_grad(h0, dproj, "b", d, 3 * d // N_CHIPS, False, "wgrad_in_even",
                                  exchange=scatter(g_w2_0))
    dx0, dg_mpre0, r_win_e = norm_matmul_bwd(dproj, win_e, x0, gain(norm_mix_pre, 0), dx1, "in_proj_bwd_even",
                                             exchange=scatter(g_win_e))
    grad_x = dx0.reshape(x.shape)

    s_w1 = reduce_slabs(r_w1_1, "reduce_ff1_1", part=1, parts=2)
    s_w1 = reduce_slabs(r_w1_0, "reduce_ff1_0", part=0, parts=2, into=s_w1)
    s_w2 = reduce_slabs(r_w2_1, "reduce_ff2_1", part=1, parts=2)
    s_w2 = reduce_slabs(r_w2_0, "reduce_ff2_0", part=0, parts=2, into=s_w2)
    sums = [reduce_slabs(r_win_e, "reduce_in_even"), reduce_slabs(r_wout_e, "reduce_out_even"),
            reduce_slabs(r_win_o, "reduce_in_odd"), reduce_slabs(r_wout_o, "reduce_out_odd"), s_w1, s_w2]
    sibling = sibling_swap(sums, "sibling_swap")
    big_w = [w_in_even, w_out_even, w_in_odd, w_out_odd, w_ff1, w_ff2]
    big_m = [m_w_in_even, m_w_out_even, m_w_in_odd, m_w_out_odd, m_w_ff1, m_w_ff2]
    big_v = [v_w_in_even, v_w_out_even, v_w_in_odd, v_w_out_odd, v_w_ff1, v_w_ff2]
    big = []
    for i, (w, m, v) in enumerate(zip(big_w, big_m, big_v)):
        two_d = (-1, w.shape[-1])
        res = adamw_big(w.reshape(two_d), sums[i], sibling[i], m.reshape(two_d), v.reshape(two_d), "adamw_big_%d" % i)
        big.append([r.reshape(w.shape) for r in res])

    small_w = [norm_mix_pre, norm_mix_post, norm_ffn_pre, norm_ffn_post, lb_table, a_norm, b_ln_g, b_ln_b, b_ws, b_bias]
    small_m = [m_norm_mix_pre, m_norm_mix_post, m_norm_ffn_pre, m_norm_ffn_post, m_lb_table, m_a_norm, m_b_ln_g,
               m_b_ln_b, m_b_ws, m_b_bias]
    small_v = [v_norm_mix_pre, v_norm_mix_post, v_norm_ffn_pre, v_norm_ffn_post, v_lb_table, v_a_norm, v_b_ln_g,
               v_b_ln_b, v_b_ws, v_b_bias]
    partial = [jnp.concatenate([dg_mpre0, dg_mpre1]), jnp.concatenate([dg_mpost0, dg_mpost1]),
               jnp.concatenate([dg_fpre0, dg_fpre1]), jnp.concatenate([dg_fpost0, dg_fpost1]),
               d_lb, d_anorm, d_lng, d_lnb, d_ws[None], d_bias_t.T[None]]
    *small_g, loss = _unpack(allreduce_small(_pack(partial + [loss_part]), "allreduce_small"),
                             [w.shape for w in small_w] + [()])
    small_d, small_nm, small_nv = adamw_small(small_w, small_g, small_m, small_v, "adamw_small")

    order = ["norm_mix_pre", "norm_mix_post", "norm_ffn_pre", "norm_ffn_post", "w_in_even", "lb_table", "a_norm",
             "b_ln_g", "b_ln_b", "b_ws", "b_bias", "w_out_even", "w_in_odd", "w_out_odd", "w_ff1", "w_ff2"]
    small_names = ["norm_mix_pre", "norm_mix_post", "norm_ffn_pre", "norm_ffn_post", "lb_table", "a_norm",
                   "b_ln_g", "b_ln_b", "b_ws", "b_bias"]
    big_names = ["w_in_even", "w_out_even", "w_in_odd", "w_out_odd", "w_ff1", "w_ff2"]
    grads, deltas, new_m, new_v = {}, {}, {}, {}
    for i, nm in enumerate(small_names):
        grads[nm], deltas[nm], new_m[nm], new_v[nm] = small_g[i], small_d[i], small_nm[i], small_nv[i]
    for i, nm in enumerate(big_names):
        grads[nm], deltas[nm], new_m[nm], new_v[nm] = big[i]
    return (loss, grad_x, *[grads[n] for n in order], *[deltas[n] for n in order],
            *[new_m[n] for n in order], *[new_v[n] for n in order])
```

```python
import functools
import math

import jax
import jax.numpy as jnp
from jax import lax
from jax.experimental import pallas as pl
from jax.experimental.pallas import tpu as pltpu

F32 = jnp.float32
BF16 = jnp.bfloat16
MESH = pl.DeviceIdType.MESH

D_MODEL = 1024
SEQ = 2048
D_FF = 4096
N_CHIPS = 4
A_WIDTH = 512
A_HEADS = 4
A_DK = 128
A_CHUNK = 64
A_SUB = 16
B_WIDTH = 512
B_GROUPS = 4
B_CHUNK = 128
C_HEADS = 16
C_HEAD_DIM = 64
C_ROT_HALF = 8
C_BLOCK = 128
C_DILATIONS = (1, 4, 16)
ROPE_THETA = 500000.0
EPS = 1e-6
ADAM_LR = 0.001
ADAM_B1 = 0.9
ADAM_B2 = 0.999
ADAM_EPS = 1e-08
ADAM_WD = 0.01
ADAM_STEP = 10

ROW_TILE = 512
FFN_ROWS = 1024
WGRAD_ROWS = 2048
VMEM_LIMIT = 56 * 1024 * 1024
NEG_BIG = -1e30


def _params(sem=None):
    return pltpu.CompilerParams(dimension_semantics=sem, vmem_limit_bytes=VMEM_LIMIT)


def _dot(a, b):
    return jnp.dot(a, b, preferred_element_type=F32)


def _dot_nt(a, b):
    return lax.dot_general(a, b, (((1,), (1,)), ((), ())), preferred_element_type=F32)


def _dot_tn(a, b):
    return lax.dot_general(a, b, (((0,), (0,)), ((), ())), preferred_element_type=F32)


def _rms(x, g):
    r = lax.rsqrt(jnp.mean(x * x, axis=-1, keepdims=True) + EPS)
    return x * r * g


def _rms_bwd(x, g, dy):
    r = lax.rsqrt(jnp.mean(x * x, axis=-1, keepdims=True) + EPS)
    xh = x * r
    dg = jnp.sum(dy * xh, axis=0, keepdims=True)
    dxh = dy * g
    dx = r * (dxh - xh * jnp.mean(dxh * xh, axis=-1, keepdims=True))
    return dx, dg


def _accumulate(ref, val, first):
    @pl.when(first)
    def _():
        ref[...] = val

    @pl.when(jnp.logical_not(first))
    def _():
        ref[...] += val


N_DEV = 8
ANY = pl.BlockSpec(memory_space=pl.ANY)


def _place():
    x, y, c = lax.axis_index("x"), lax.axis_index("y"), lax.axis_index("c")
    return x, y, c, [(1 - x, y), (x, 1 - y), (1 - x, 1 - y)]


class _Exchange:
    def __init__(self, kind, arrays):
        self.kind, self.arrays, self.n = kind, list(arrays), len(arrays)
        per_peer = pltpu.SemaphoreType.DMA((3 * self.n,))
        if kind == "gather":
            self.out_shape = [jax.ShapeDtypeStruct((N_CHIPS,) + a.shape, a.dtype) for a in self.arrays]
            self.scratch = [per_peer, per_peer, pltpu.SemaphoreType.DMA((self.n,)), per_peer, per_peer]
        else:
            self.out_shape = [jax.ShapeDtypeStruct(a.shape, a.dtype) for a in self.arrays]
            self.scratch = [per_peer, per_peer, pltpu.SemaphoreType.DMA((self.n,))]

    def _copies(self, ins, outs, sems):
        send_sems, recv_sems, local_sems = sems[:3]
        x, y, c, chips = _place()
        me = 2 * x + y
        local, remote = [], []
        for a in range(self.n):
            if self.kind == "gather":
                local.append(pltpu.make_async_copy(ins[a], outs[a].at[me], local_sems.at[a]))
                half = self.arrays[a].shape[0] // 2

                def rows(ref, core, half=half):
                    return ref.at[pl.ds(core * half, half)]
            else:
                local.append(pltpu.make_async_copy(ins[a].at[me], outs[a].at[3], local_sems.at[a]))
            for j, (px, py) in enumerate(chips):
                k = 3 * a + j
                peer = 2 * px + py

                def copy(src, dst, to, send_sem=send_sems.at[k], recv_sem=recv_sems.at[k]):
                    return pltpu.make_async_remote_copy(src_ref=src, dst_ref=dst, send_sem=send_sem, recv_sem=recv_sem,
                                                        device_id=to, device_id_type=MESH)

                if self.kind == "gather":
                    sent = copy(rows(ins[a], c), rows(outs[a].at[me], c), (px, py, c))
                    landed = copy(rows(ins[a], c), rows(outs[a].at[peer], c), (px, py, c))
                    on = dict(send_sem=sems[3].at[k], recv_sem=sems[4].at[k])
                    passed = copy(rows(outs[a].at[peer], c), rows(outs[a].at[peer], c), (x, y, 1 - c), **on)
                    handed = copy(rows(outs[a].at[peer], c), rows(outs[a].at[peer], 1 - c), (x, y, 1 - c), **on)
                    remote.append((sent, landed, passed, handed))
                else:
                    sent = copy(ins[a].at[peer], outs[a].at[j], (px, py, c))
                    remote.append((sent, sent, None, None))
        return local, remote

    def start(self, ins, outs, sems):
        local, remote = self._copies(ins, outs, sems)
        for cp in local:
            cp.start()
        for sent, _, _, _ in remote:
            sent.start()

    def finish(self, ins, outs, sems):
        local, remote = self._copies(ins, outs, sems)
        for _, landed, passed, _ in remote:
            landed.wait_recv()
            if passed is not None:
                passed.start()
        for sent, _, passed, handed in remote:
            if passed is not None:
                handed.wait_recv()
                passed.wait_send()
            sent.wait_send()
        for cp in local:
            cp.wait()


def _call(body, *, name, grid, in_specs, out_specs, out_shape, args, scratch_shapes=(), aliases=None, exchange=None):
    if exchange is None:
        return pl.pallas_call(
            body, name=name, grid=grid, in_specs=in_specs, out_specs=out_specs, out_shape=out_shape,
            scratch_shapes=list(scratch_shapes), input_output_aliases=aliases or {},
            compiler_params=_params(("arbitrary",) * len(grid)))(*args)
    n_in, n_out, n_scr, n_ex = len(in_specs), len(out_specs), len(scratch_shapes), exchange.n
    steps = grid

    def wrapped(*refs):
        ins, refs = refs[:n_in], refs[n_in:]
        ex_in, refs = refs[:n_ex], refs[n_ex:]
        outs, refs = refs[:n_out], refs[n_out:]
        ex_out, refs = refs[:n_ex], refs[n_ex:]
        scr, sems = refs[:n_scr], refs[n_scr:]
        first = functools.reduce(jnp.logical_and, [pl.program_id(k) == 0 for k in range(len(steps))])
        last = functools.reduce(jnp.logical_and, [pl.program_id(k) == steps[k] - 1 for k in range(len(steps))])

        @pl.when(first)
        def _():
            exchange.start(ex_in, ex_out, sems)

        body(*ins, *outs, *scr)

        @pl.when(last)
        def _():
            exchange.finish(ex_in, ex_out, sems)

    return pl.pallas_call(
        wrapped, name=name, grid=grid,
        in_specs=list(in_specs) + [ANY] * n_ex, out_specs=list(out_specs) + [ANY] * n_ex,
        out_shape=list(out_shape) + exchange.out_shape,
        scratch_shapes=list(scratch_shapes) + exchange.scratch, input_output_aliases=aliases or {},
        compiler_params=_params(("arbitrary",) * len(grid)))(*args, *exchange.arrays)


def exchange_alone(exchange, name):
    def body(*refs):
        n = exchange.n
        exchange.start(refs[:n], refs[n:2 * n], refs[2 * n:])
        exchange.finish(refs[:n], refs[n:2 * n], refs[2 * n:])

    return pl.pallas_call(
        body, name=name, in_specs=[ANY] * exchange.n, out_specs=[ANY] * exchange.n,
        out_shape=exchange.out_shape, scratch_shapes=exchange.scratch)(*exchange.arrays)


def norm_matmul(x, g, wg, name, exchange=None):
    t, d = x.shape
    nl = wg.shape[2]

    def body(x_ref, g_ref, w_ref, o_ref, h_ref):
        h = _rms(x_ref[...], g_ref[...]).astype(BF16)
        h_ref[...] = h
        for c in range(N_CHIPS):
            o_ref[:, c * nl:(c + 1) * nl] = _dot(h, w_ref[c])

    return _call(
        body, name=name, grid=(t // ROW_TILE,),
        in_specs=[pl.BlockSpec((ROW_TILE, d), lambda i: (i, 0)),
                  pl.BlockSpec((1, d), lambda i: (0, 0)),
                  pl.BlockSpec((N_CHIPS, d, nl), lambda i: (0, 0, 0))],
        out_specs=[pl.BlockSpec((ROW_TILE, N_CHIPS * nl), lambda i: (i, 0)),
                   pl.BlockSpec((ROW_TILE, d), lambda i: (i, 0))],
        out_shape=[jax.ShapeDtypeStruct((t, N_CHIPS * nl), F32), jax.ShapeDtypeStruct((t, d), BF16)],
        args=(x, g, wg), exchange=exchange)


def norm_matmul_bwd(dproj, wg, x, g, dres, name, exchange=None):
    t, d = x.shape
    nl = wg.shape[2]

    def body(dp_ref, w_ref, x_ref, g_ref, dres_ref, dx_ref, dg_ref):
        dh = _dot_nt(dp_ref[:, 0:nl].astype(BF16), w_ref[0])
        for c in range(1, N_CHIPS):
            dh += _dot_nt(dp_ref[:, c * nl:(c + 1) * nl].astype(BF16), w_ref[c])
        dx, dg = _rms_bwd(x_ref[...], g_ref[...], dh)
        dx_ref[...] = dres_ref[...] + dx
        _accumulate(dg_ref, dg, pl.program_id(0) == 0)

    row = pl.BlockSpec((ROW_TILE, d), lambda i: (i, 0))
    vec = pl.BlockSpec((1, d), lambda i: (0, 0))
    return _call(
        body, name=name, grid=(t // ROW_TILE,),
        in_specs=[pl.BlockSpec((ROW_TILE, N_CHIPS * nl), lambda i: (i, 0)),
                  pl.BlockSpec((N_CHIPS, d, nl), lambda i: (0, 0, 0)), row, vec, row],
        out_specs=[row, vec],
        out_shape=[jax.ShapeDtypeStruct((t, d), F32), jax.ShapeDtypeStruct((1, d), F32)],
        args=(dproj, wg, x, g, dres), exchange=exchange)


def out_proj(a, wg, x, g, name):
    t, d = x.shape
    kl = wg.shape[1]

    def body(a_ref, w_ref, x_ref, g_ref, mix_ref, xo_ref):
        acc = _dot(a_ref[:, 0:kl], w_ref[0])
        for c in range(1, N_CHIPS):
            acc += _dot(a_ref[:, c * kl:(c + 1) * kl], w_ref[c])
        mix_ref[...] = acc
        xo_ref[...] = x_ref[...] + _rms(acc, g_ref[...])

    row = pl.BlockSpec((ROW_TILE, d), lambda i: (i, 0))
    return pl.pallas_call(
        body, name=name, grid=(t // ROW_TILE,),
        in_specs=[row, pl.BlockSpec((N_CHIPS, kl, d), lambda i: (0, 0, 0)), row,
                  pl.BlockSpec((1, d), lambda i: (0, 0))],
        out_specs=[row, row],
        out_shape=[jax.ShapeDtypeStruct((t, d), F32), jax.ShapeDtypeStruct((t, d), F32)],
        compiler_params=_params(("arbitrary",)),
    )(a, wg, x, g)


def out_proj_bwd(dxo, mix, g, wg, name):
    t, d = mix.shape
    kl = wg.shape[1]

    def body(dxo_ref, mix_ref, g_ref, w_ref, dmix_ref, da_ref, dg_ref):
        dmix, dg = _rms_bwd(mix_ref[...], g_ref[...], dxo_ref[...])
        dmb = dmix.astype(BF16)
        dmix_ref[...] = dmb
        for c in range(N_CHIPS):
            da_ref[:, c * kl:(c + 1) * kl] = _dot_nt(dmb, w_ref[c])
        _accumulate(dg_ref, dg, pl.program_id(0) == 0)

    row = pl.BlockSpec((ROW_TILE, d), lambda i: (i, 0))
    vec = pl.BlockSpec((1, d), lambda i: (0, 0))
    return pl.pallas_call(
        body, name=name, grid=(t // ROW_TILE,),
        in_specs=[row, row, vec, pl.BlockSpec((N_CHIPS, kl, d), lambda i: (0, 0, 0))],
        out_specs=[row, row, vec],
        out_shape=[jax.ShapeDtypeStruct((t, d), BF16), jax.ShapeDtypeStruct((t, d), F32),
                   jax.ShapeDtypeStruct((1, d), F32)],
        compiler_params=_params(("arbitrary",)),
    )(dxo, mix, g, wg)


def ffn_fwd(x, gpre, w1g, w2g, gpost, name, exchange=None, target=None):
    t, d = x.shape
    hc = w1g.shape[2]
    with_loss = target is not None

    def body(x_ref, gpre_ref, w1_ref, w2_ref, gpost_ref, *rest):
        if with_loss:
            t_ref, xo_ref, h_ref, a_ref, y_ref, l_ref, acc = rest
        else:
            xo_ref, h_ref, a_ref, y_ref, acc = rest
        i, c = pl.program_id(0), pl.program_id(1)

        @pl.when(c == 0)
        def _():
            h_ref[...] = _rms(x_ref[...], gpre_ref[...]).astype(BF16)

        a = _dot(h_ref[...], w1_ref[...])
        a_ref[...] = a.astype(BF16)
        r = jnp.square(jnp.maximum(a, 0.0)).astype(BF16)
        _accumulate(acc, _dot(r, w2_ref[...]), c == 0)

        @pl.when(c == N_CHIPS - 1)
        def _():
            y = acc[...]
            y_ref[...] = y
            xo = x_ref[...] + _rms(y, gpost_ref[...])
            if with_loss:
                e = xo - t_ref[...]
                xo_ref[...] = e * (1.0 / d)
                part = jnp.sum(jnp.sum(e * e, axis=-1, keepdims=True), axis=0, keepdims=True) * (0.5 / d)
                _accumulate(l_ref, part, i == 0)
            else:
                xo_ref[...] = xo

    row = pl.BlockSpec((FFN_ROWS, d), lambda i, c: (i, 0))
    vec = pl.BlockSpec((1, d), lambda i, c: (0, 0))
    one = pl.BlockSpec((1, 1), lambda i, c: (0, 0))
    return _call(
        body, name=name, grid=(t // FFN_ROWS, N_CHIPS),
        in_specs=[row, vec,
                  pl.BlockSpec((None, d, hc), lambda i, c: (c, 0, 0)),
                  pl.BlockSpec((None, hc, d), lambda i, c: (c, 0, 0)), vec] + ([row] if with_loss else []),
        out_specs=[row, row, pl.BlockSpec((FFN_ROWS, hc), lambda i, c: (i, c)), row] + ([one] if with_loss else []),
        out_shape=[jax.ShapeDtypeStruct((t, d), F32), jax.ShapeDtypeStruct((t, d), BF16),
                   jax.ShapeDtypeStruct((t, N_CHIPS * hc), BF16), jax.ShapeDtypeStruct((t, d), F32)]
        + ([jax.ShapeDtypeStruct((1, 1), F32)] if with_loss else []),
        scratch_shapes=[pltpu.VMEM((FFN_ROWS, d), F32)],
        args=(x, gpre, w1g, w2g, gpost) + ((target,) if with_loss else ()), exchange=exchange)


def ffn_bwd(dxo, x, y, a, gpre, gpost, w1g, w2g, name, exchange=None):
    t, d = x.shape
    hc = w1g.shape[2]

    def body(dxo_ref, x_ref, y_ref, a_ref, gpre_ref, gpost_ref, w1_ref, w2_ref,
             dxi_ref, dy_ref, da_ref, dgpre_ref, dgpost_ref, acc):
        i, c = pl.program_id(0), pl.program_id(1)

        @pl.when(c == 0)
        def _():
            dy, dg = _rms_bwd(y_ref[...], gpost_ref[...], dxo_ref[...])
            dy_ref[...] = dy.astype(BF16)
            _accumulate(dgpost_ref, dg, i == 0)

        dr = _dot_nt(dy_ref[...], w2_ref[...])
        da = (dr * (2.0 * jnp.maximum(a_ref[...].astype(F32), 0.0))).astype(BF16)
        da_ref[...] = da
        _accumulate(acc, _dot_nt(da, w1_ref[...]), c == 0)

        @pl.when(c == N_CHIPS - 1)
        def _():
            dx, dg = _rms_bwd(x_ref[...], gpre_ref[...], acc[...])
            dxi_ref[...] = dxo_ref[...] + dx
            _accumulate(dgpre_ref, dg, i == 0)

    row = pl.BlockSpec((ROW_TILE, d), lambda i, c: (i, 0))
    vec = pl.BlockSpec((1, d), lambda i, c: (0, 0))
    hid = pl.BlockSpec((ROW_TILE, hc), lambda i, c: (i, c))
    return _call(
        body, name=name, grid=(t // ROW_TILE, N_CHIPS),
        in_specs=[row, row, row, hid, vec, vec,
                  pl.BlockSpec((None, d, hc), lambda i, c: (c, 0, 0)),
                  pl.BlockSpec((None, hc, d), lambda i, c: (c, 0, 0))],
        out_specs=[row, row, hid, vec, vec],
        out_shape=[jax.ShapeDtypeStruct((t, d), F32), jax.ShapeDtypeStruct((t, d), BF16),
                   jax.ShapeDtypeStruct((t, N_CHIPS * hc), BF16),
                   jax.ShapeDtypeStruct((1, d), F32), jax.ShapeDtypeStruct((1, d), F32)],
        scratch_shapes=[pltpu.VMEM((ROW_TILE, d), F32)],
        args=(dxo, x, y, a, gpre, gpost, w1g, w2g), exchange=exchange)


def weight_grad(a, b, chunked, bk, bn, relu2, name, exchange=None):
    t = a.shape[0]
    a_on = chunked == "a"
    rows = min(t, WGRAD_ROWS)
    n_steps = t // rows

    def body(a_ref, b_ref, o_ref, acc):
        s = pl.program_id(1)
        av = a_ref[...]
        if relu2:
            av = jnp.square(jnp.maximum(av.astype(F32), 0.0))
        _accumulate(acc, _dot_tn(av.astype(BF16), b_ref[...].astype(BF16)), s == 0)

        @pl.when(s == n_steps - 1)
        def _():
            o_ref[...] = acc[...].astype(BF16)

    res = _call(
        body, name=name, grid=(N_CHIPS, n_steps),
        in_specs=[pl.BlockSpec((rows, bk), (lambda c, s: (s, c)) if a_on else (lambda c, s: (s, 0))),
                  pl.BlockSpec((rows, bn), (lambda c, s: (s, 0)) if a_on else (lambda c, s: (s, c)))],
        out_specs=[pl.BlockSpec((None, bk, bn), lambda c, s: (c, 0, 0))],
        out_shape=[jax.ShapeDtypeStruct((N_CHIPS, bk, bn), BF16)],
        scratch_shapes=[pltpu.VMEM((bk, bn), F32)],
        args=(a, b), exchange=exchange)
    return res[0] if exchange is None else res


def _hgrn2_chunk(st, qs, fls, ivs, gls, l0, l1, l2, ng):
    nsub = len(qs)
    mx = jnp.maximum(jnp.maximum(l0, l1), l2)
    e0, e1, e2 = jnp.exp(l0 - mx), jnp.exp(l1 - mx), jnp.exp(l2 - mx)
    lb = e0 / (e0 + e1 + e2)
    rows = lax.broadcasted_iota(jnp.int32, (A_SUB, A_SUB), 0)
    cols = lax.broadcasted_iota(jnp.int32, (A_SUB, A_SUB), 1)
    tri = (rows >= cols).astype(F32)
    keep = (lax.broadcasted_iota(jnp.int32, (A_SUB, A_SUB, A_DK), 0)
            >= lax.broadcasted_iota(jnp.int32, (A_SUB, A_SUB, A_DK), 1))
    base = jnp.zeros_like(l0)
    bases, gs, ks, qfs = [], [], [], []
    for i in range(nsub):
        f = lb + (1.0 - lb) * jax.nn.sigmoid(fls[i])
        logf = jnp.log(f)
        bases.append(base)
        gs.append(base + jnp.dot(tri, logf, precision=lax.Precision.HIGHEST, preferred_element_type=F32))
        base = base + jnp.sum(logf, axis=0, keepdims=True)
        ks.append(1.0 - f)
        qfs.append(jax.nn.silu(qs[i]))
    g_last = base
    stb = st.astype(BF16)
    outs = []
    for i in range(nsub):
        o = _dot_nt((qfs[i] * jnp.exp(gs[i])).astype(BF16), stb)
        if i > 0:
            qt = (qfs[i] * jnp.exp(gs[i] - bases[i])).astype(BF16)
            kk = jnp.concatenate([ks[j] * jnp.exp(bases[i] - gs[j]) for j in range(i)], axis=0).astype(BF16)
            vv = jnp.concatenate(ivs[:i], axis=0).astype(BF16)
            o = o + _dot(_dot_nt(qt, kk).astype(BF16), vv)
        dec = jnp.exp(jnp.where(keep, gs[i][:, None, :] - gs[i][None, :, :], NEG_BIG))
        s_diag = jnp.sum(qfs[i][:, None, :] * ks[i][None, :, :] * dec, axis=-1)
        o = o + _dot(s_diag.astype(BF16), ivs[i].astype(BF16))
        o = o * lax.rsqrt(jnp.mean(o * o, axis=-1, keepdims=True) + EPS) * ng
        outs.append(o * jax.nn.silu(gls[i]))
    kdec = jnp.concatenate([ks[j] * jnp.exp(g_last - gs[j]) for j in range(nsub)], axis=0).astype(BF16)
    vall = jnp.concatenate(ivs, axis=0).astype(BF16)
    new_st = st * jnp.exp(g_last) + _dot_tn(vall, kdec)
    return new_st, outs


A_MAX_LOG_DECAY = 80.0


def _split3(x):
    hi = x.astype(BF16)
    r1 = x - hi.astype(F32)
    mid = r1.astype(BF16)
    return hi, mid, (r1 - mid.astype(F32)).astype(BF16)


def _tri_matmul(x, transpose):
    n = x.shape[0]
    r = lax.broadcasted_iota(jnp.int32, (n, n), 0)
    c = lax.broadcasted_iota(jnp.int32, (n, n), 1)
    tri = ((r <= c) if transpose else (r >= c)).astype(BF16)
    hi, mid, lo = _split3(x)
    return (_dot(tri, lo) + _dot(tri, mid)) + _dot(tri, hi)


@jax.custom_vjp
def _cumsum_rows(x):
    return _tri_matmul(x, False)


def _cumsum_rows_fwd(x):
    return _tri_matmul(x, False), None


def _cumsum_rows_bwd(_, dy):
    return (_tri_matmul(dy, True),)


_cumsum_rows.defvjp(_cumsum_rows_fwd, _cumsum_rows_bwd)


def _lower_bound(l0, l1, l2):
    mx = jnp.maximum(jnp.maximum(l0, l1), l2)
    e0, e1, e2 = jnp.exp(l0 - mx), jnp.exp(l1 - mx), jnp.exp(l2 - mx)
    return e0 / (e0 + e1 + e2)


def _b(x):
    return x.astype(BF16)


@jax.custom_vjp
def _mm(a, b):
    return _dot(_b(a), _b(b))


_mm.defvjp(lambda a, b: (_mm(a, b), (a, b)),
           lambda res, d: (_dot_nt(_b(d), _b(res[1])), _dot_tn(_b(res[0]), _b(d))))


@jax.custom_vjp
def _mm_nt(a, b):
    return _dot_nt(_b(a), _b(b))


_mm_nt.defvjp(lambda a, b: (_mm_nt(a, b), (a, b)),
              lambda res, d: (_dot(_b(d), _b(res[1])), _dot_tn(_b(d), _b(res[0]))))


def _dot_split(dot, a, b):
    ah, bh = _b(a), _b(b)
    al, bl = _b(a - ah.astype(F32)), _b(b - bh.astype(F32))
    return (dot(ah, bl) + dot(al, bh)) + dot(ah, bh)


@jax.custom_vjp
def _mm_scores(a, b):
    return _dot_nt(_b(a), _b(b))


_mm_scores.defvjp(lambda a, b: (_mm_scores(a, b), (a, b)),
                  lambda res, d: (_dot_split(_dot, d, res[1]), _dot_split(_dot_tn, d, res[0])))


@jax.custom_vjp
def _mm_tn(a, b):
    return _dot_tn(_b(a), _b(b))


_mm_tn.defvjp(lambda a, b: (_mm_tn(a, b), (a, b)),
              lambda res, d: (_dot_nt(_b(res[1]), _b(d)), _dot(_b(res[0]), _b(d))))


@jax.custom_vjp
def _split_heads(x):
    return tuple(x[:, h * A_DK:(h + 1) * A_DK] for h in range(A_HEADS))


def _split_heads_fwd(x):
    return _split_heads(x), None


def _split_heads_bwd(_, parts):
    return (jnp.concatenate(parts, axis=1),)


_split_heads.defvjp(_split_heads_fwd, _split_heads_bwd)


def _hgrn2_chunk_fast(sts, q, fl, iv, gl, l0, l1, l2, ng):
    lb = _lower_bound(l0, l1, l2)
    f = lb + (1.0 - lb) * jax.nn.sigmoid(fl)
    return _hgrn2_fast_core(sts, q, f, jnp.log(f), iv, gl, ng)


def _hgrn2_fast_core(sts, q, f, logf, iv, gl, ng):
    g = _cumsum_rows(logf)
    g_last = jnp.sum(logf, axis=0, keepdims=True)
    k = 1.0 - f
    qgs = _split_heads(jax.nn.silu(q) * jnp.exp(g))
    kgs = _split_heads(k * jnp.exp(-g))
    kds = _split_heads(k * jnp.exp(g_last - g))
    ivs = _split_heads(iv)
    decays = _split_heads(jnp.exp(g_last))
    n = q.shape[0]
    causal = lax.broadcasted_iota(jnp.int32, (n, n), 0) >= lax.broadcasted_iota(jnp.int32, (n, n), 1)
    raw = [_mm_scores(qg, kg) for qg, kg in zip(qgs, kgs)]
    inter = [_mm_nt(qg, st) for qg, st in zip(qgs, sts)]
    scores = [jnp.where(causal, s, 0.0) for s in raw]
    os = [a + _mm(s, v) for a, s, v in zip(inter, scores, ivs)]
    new_sts = [st * d + _mm_tn(v, kd) for st, d, v, kd in zip(sts, decays, ivs, kds)]
    os = [o * lax.rsqrt(jnp.mean(o * o, axis=-1, keepdims=True) + EPS) for o in os]
    return new_sts, jnp.concatenate(os, axis=1) * ng * jax.nn.silu(gl)


A_STEP_CHUNKS = 4


def _chunk_rows(j):
    return pl.ds(pl.multiple_of(j * A_CHUNK, A_CHUNK), A_CHUNK)


def _sub_rows(j, i):
    return pl.ds(pl.multiple_of(j * A_CHUNK + i * A_SUB, A_SUB), A_SUB)


def _sub_blocks(ref, head, j):
    lanes = slice(head * A_DK, (head + 1) * A_DK)
    return [ref[_sub_rows(j, i), lanes] for i in range(A_CHUNK // A_SUB)]


def hgrn2_fwd(proj, lb_table, a_norm, batch, name, exchange=None):
    t = proj.shape[0]
    n_steps = t // batch // (A_CHUNK * A_STEP_CHUNKS)
    rows = A_CHUNK * A_STEP_CHUNKS

    def body(q_ref, f_ref, i_ref, g_ref, lb_ref, ng_ref, o_ref, st_ref, dec_ref, st):
        @pl.when(pl.program_id(1) == 0)
        def _():
            st[...] = jnp.zeros_like(st)

        def chunk(j, carry):
            r = _chunk_rows(j)
            st_ref[j] = st[...]
            lb = _lower_bound(lb_ref[0:1, :], lb_ref[1:2, :], lb_ref[2:3, :])
            f = lb + (1.0 - lb) * jax.nn.sigmoid(f_ref[r, :])
            logf = jnp.log(f)
            decay = jnp.sum(logf, axis=0, keepdims=True)
            dec_ref[j] = decay
            mild = False

            @pl.when(mild)
            def _():
                new_sts, o = _hgrn2_fast_core([st[h] for h in range(A_HEADS)], q_ref[r, :], f, logf,
                                              i_ref[r, :], g_ref[r, :], ng_ref[...])
                for h in range(A_HEADS):
                    st[h] = new_sts[h]
                o_ref[r, :] = o.astype(BF16)

            @pl.when(jnp.logical_not(mild))
            def _():
                for h in range(A_HEADS):
                    lanes = slice(h * A_DK, (h + 1) * A_DK)
                    new_st, outs = _hgrn2_chunk(
                        st[h], _sub_blocks(q_ref, h, j), _sub_blocks(f_ref, h, j), _sub_blocks(i_ref, h, j),
                        _sub_blocks(g_ref, h, j), lb_ref[0:1, lanes], lb_ref[1:2, lanes], lb_ref[2:3, lanes],
                        ng_ref[:, lanes])
                    st[h] = new_st
                    for i, o in enumerate(outs):
                        o_ref[_sub_rows(j, i), lanes] = o.astype(BF16)

            return carry

        lax.fori_loop(0, A_STEP_CHUNKS, chunk, 0)

    def part(k):
        return pl.BlockSpec((rows, A_WIDTH), lambda b, n: (b * n_steps + n, k))

    return _call(
        body, name=name, grid=(batch, n_steps),
        in_specs=[part(0), part(1), part(2), part(3),
                  pl.BlockSpec((3, A_WIDTH), lambda b, n: (0, 0)), pl.BlockSpec((1, A_WIDTH), lambda b, n: (0, 0))],
        out_specs=[part(0),
                   pl.BlockSpec((A_STEP_CHUNKS, A_HEADS, A_DK, A_DK), lambda b, n: (b * n_steps + n, 0, 0, 0)),
                   pl.BlockSpec((A_STEP_CHUNKS, 1, A_WIDTH), lambda b, n: (b * n_steps + n, 0, 0))],
        out_shape=[jax.ShapeDtypeStruct((t, A_WIDTH), BF16),
                   jax.ShapeDtypeStruct((t // A_CHUNK, A_HEADS, A_DK, A_DK), F32),
                   jax.ShapeDtypeStruct((t // A_CHUNK, 1, A_WIDTH), F32)],
        scratch_shapes=[pltpu.VMEM((A_HEADS, A_DK, A_DK), F32)],
        args=(proj, proj, proj, proj, lb_table, a_norm), exchange=exchange)


def hgrn2_bwd(proj, states, decays, lb_table, a_norm, do, batch, name, exchange=None):
    t = proj.shape[0]
    n_steps = t // batch // (A_CHUNK * A_STEP_CHUNKS)
    rows = A_CHUNK * A_STEP_CHUNKS

    def body(q_ref, f_ref, i_ref, g_ref, st_ref, dec_ref, lb_ref, ng_ref, do_ref, dp_ref, dlb_ref, dng_ref, dst):
        @pl.when(jnp.logical_and(pl.program_id(0) == 0, pl.program_id(1) == 0))
        def _():
            dlb_ref[...] = jnp.zeros_like(dlb_ref)
            dng_ref[...] = jnp.zeros_like(dng_ref)

        @pl.when(pl.program_id(1) == 0)
        def _():
            dst[...] = jnp.zeros_like(dst)

        def chunk(jj, carry):
            j = A_STEP_CHUNKS - 1 - jj
            r = _chunk_rows(j)
            mild = False

            @pl.when(mild)
            def _():
                _, vjp = jax.vjp(
                    _hgrn2_chunk_fast, [st_ref[j, h] for h in range(A_HEADS)], q_ref[r, :], f_ref[r, :],
                    i_ref[r, :], g_ref[r, :], lb_ref[0:1, :], lb_ref[1:2, :], lb_ref[2:3, :], ng_ref[...])
                d_sts, dq, df, di, dg, dl0, dl1, dl2, dng = vjp(
                    ([dst[h] for h in range(A_HEADS)], do_ref[r, :].astype(F32)))
                for h in range(A_HEADS):
                    dst[h] = d_sts[h]
                for k, part in enumerate((dq, df, di, dg)):
                    dp_ref[r, k * A_WIDTH:(k + 1) * A_WIDTH] = part
                for row, val in enumerate((dl0, dl1, dl2)):
                    dlb_ref[row:row + 1, :] += val
                dng_ref[...] += dng

            @pl.when(jnp.logical_not(mild))
            def _():
                for h in range(A_HEADS):
                    lanes = slice(h * A_DK, (h + 1) * A_DK)
                    _, vjp = jax.vjp(
                        _hgrn2_chunk, st_ref[j, h], _sub_blocks(q_ref, h, j), _sub_blocks(f_ref, h, j),
                        _sub_blocks(i_ref, h, j), _sub_blocks(g_ref, h, j), lb_ref[0:1, lanes], lb_ref[1:2, lanes],
                        lb_ref[2:3, lanes], ng_ref[:, lanes])
                    douts = [x.astype(F32) for x in _sub_blocks(do_ref, h, j)]
                    d_st, dqs, dfs, dis, dgs, dl0, dl1, dl2, dng = vjp((dst[h], douts))
                    dst[h] = d_st
                    for k, parts in enumerate((dqs, dfs, dis, dgs)):
                        for i in range(A_CHUNK // A_SUB):
                            dp_ref[_sub_rows(j, i), k * A_WIDTH + h * A_DK:k * A_WIDTH + (h + 1) * A_DK] = parts[i]
                    for row, val in enumerate((dl0, dl1, dl2)):
                        dlb_ref[row:row + 1, lanes] += val
                    dng_ref[:, lanes] += dng

            return carry

        lax.fori_loop(0, A_STEP_CHUNKS, chunk, 0)

    def rev(b, n):
        return b * n_steps + (n_steps - 1 - n)

    def part(k):
        return pl.BlockSpec((rows, A_WIDTH), lambda b, n: (rev(b, n), k))

    const3 = pl.BlockSpec((3, A_WIDTH), lambda b, n: (0, 0))
    const1 = pl.BlockSpec((1, A_WIDTH), lambda b, n: (0, 0))
    return _call(
        body, name=name, grid=(batch, n_steps),
        in_specs=[part(0), part(1), part(2), part(3),
                  pl.BlockSpec((A_STEP_CHUNKS, A_HEADS, A_DK, A_DK), lambda b, n: (rev(b, n), 0, 0, 0)),
                  pl.BlockSpec((A_STEP_CHUNKS, 1, A_WIDTH), lambda b, n: (rev(b, n), 0, 0)),
                  const3, const1, part(0)],
        out_specs=[pl.BlockSpec((rows, 4 * A_WIDTH), lambda b, n: (rev(b, n), 0)), const3, const1],
        out_shape=[jax.ShapeDtypeStruct((t, 4 * A_WIDTH + 2 * B_WIDTH), F32),
                   jax.ShapeDtypeStruct((3, A_WIDTH), F32), jax.ShapeDtypeStruct((1, A_WIDTH), F32)],
        scratch_shapes=[pltpu.VMEM((A_HEADS, A_DK, A_DK), F32)],
        args=(proj, proj, proj, proj, states, decays, lb_table, a_norm, do), exchange=exchange)


B_GDIM = B_WIDTH // B_GROUPS
B_ROWS = 512


def _gmlp_chunk(ubs, vbs, lngs, lnbs, ws, bcols):
    vs = [jax.nn.gelu(v) for v in vbs]
    mu = sum(jnp.sum(v, axis=-1, keepdims=True) for v in vs) * (1.0 / B_WIDTH)
    var = sum(jnp.sum(jnp.square(v - mu), axis=-1, keepdims=True) for v in vs) * (1.0 / B_WIDTH)
    rstd = lax.rsqrt(var + EPS)
    tril = (lax.broadcasted_iota(jnp.int32, (B_CHUNK, B_CHUNK), 0)
            >= lax.broadcasted_iota(jnp.int32, (B_CHUNK, B_CHUNK), 1))
    outs = []
    for g in range(B_GROUPS):
        vn = (vs[g] - mu) * rstd * lngs[g] + lnbs[g]
        w = jnp.where(tril, ws[g], 0.0).astype(BF16)
        outs.append(jax.nn.gelu(ubs[g]) * (_dot(w, vn.astype(BF16)) + bcols[g]))
    return outs


def _gmlp_args(u_ref, v_ref, lng_ref, lnb_ref, w_ref, bt_ref, rows):
    def groups(ref):
        return [ref[rows, g * B_GDIM:(g + 1) * B_GDIM] for g in range(B_GROUPS)]

    def vec(ref):
        return [ref[:, g * B_GDIM:(g + 1) * B_GDIM] for g in range(B_GROUPS)]

    return (groups(u_ref), groups(v_ref), vec(lng_ref), vec(lnb_ref),
            [w_ref[g] for g in range(B_GROUPS)], [bt_ref[:, g:g + 1] for g in range(B_GROUPS)])


def gmlp_fwd(proj, oa, ln_g, ln_b, w, bias_t, name, exchange=None):
    t = proj.shape[0]

    def body(u_ref, v_ref, oa_ref, lng_ref, lnb_ref, w_ref, bt_ref, o_ref):
        o_ref[:, 0:A_WIDTH] = oa_ref[...]
        for n in range(B_ROWS // B_CHUNK):
            rows = slice(n * B_CHUNK, (n + 1) * B_CHUNK)
            outs = _gmlp_chunk(*_gmlp_args(u_ref, v_ref, lng_ref, lnb_ref, w_ref, bt_ref, rows))
            for g, o in enumerate(outs):
                o_ref[rows, A_WIDTH + g * B_GDIM:A_WIDTH + (g + 1) * B_GDIM] = o.astype(BF16)

    vec = pl.BlockSpec((1, B_WIDTH), lambda i: (0, 0))
    return _call(
        body, name=name, grid=(t // B_ROWS,),
        in_specs=[pl.BlockSpec((B_ROWS, B_WIDTH), lambda i: (i, 4)), pl.BlockSpec((B_ROWS, B_WIDTH), lambda i: (i, 5)),
                  pl.BlockSpec((B_ROWS, A_WIDTH), lambda i: (i, 0)), vec, vec,
                  pl.BlockSpec((B_GROUPS, B_CHUNK, B_CHUNK), lambda i: (0, 0, 0)),
                  pl.BlockSpec((B_CHUNK, B_GROUPS), lambda i: (0, 0))],
        out_specs=[pl.BlockSpec((B_ROWS, A_WIDTH + B_WIDTH), lambda i: (i, 0))],
        out_shape=[jax.ShapeDtypeStruct((t, A_WIDTH + B_WIDTH), BF16)],
        args=(proj, proj, oa, ln_g, ln_b, w, bias_t), exchange=exchange)


def gmlp_bwd(proj, dmixin, ln_g, ln_b, w, bias_t, dproj, name, exchange=None):
    t = proj.shape[0]

    def body(u_ref, v_ref, do_ref, lng_ref, lnb_ref, w_ref, bt_ref, dp_in_ref,
             dp_ref, dlng_ref, dlnb_ref, dw_ref, dbt_ref):
        del dp_in_ref

        @pl.when(pl.program_id(0) == 0)
        def _():
            for ref in (dlng_ref, dlnb_ref, dw_ref, dbt_ref):
                ref[...] = jnp.zeros_like(ref)

        for n in range(B_ROWS // B_CHUNK):
            rows = slice(n * B_CHUNK, (n + 1) * B_CHUNK)
            _, vjp = jax.vjp(_gmlp_chunk, *_gmlp_args(u_ref, v_ref, lng_ref, lnb_ref, w_ref, bt_ref, rows))
            douts = [do_ref[rows, g * B_GDIM:(g + 1) * B_GDIM] for g in range(B_GROUPS)]
            dus, dvs, dlngs, dlnbs, dws, dbs = vjp(douts)
            for g in range(B_GROUPS):
                lanes = slice(g * B_GDIM, (g + 1) * B_GDIM)
                dp_ref[rows, lanes] = dus[g]
                dp_ref[rows, B_WIDTH + g * B_GDIM:B_WIDTH + (g + 1) * B_GDIM] = dvs[g]
                dlng_ref[:, lanes] += dlngs[g]
                dlnb_ref[:, lanes] += dlnbs[g]
                dw_ref[g] += dws[g]
                dbt_ref[:, g:g + 1] += dbs[g]

    vec = pl.BlockSpec((1, B_WIDTH), lambda i: (0, 0))
    wspec = pl.BlockSpec((B_GROUPS, B_CHUNK, B_CHUNK), lambda i: (0, 0, 0))
    bspec = pl.BlockSpec((B_CHUNK, B_GROUPS), lambda i: (0, 0))
    return _call(
        body, name=name, grid=(t // B_ROWS,),
        in_specs=[pl.BlockSpec((B_ROWS, B_WIDTH), lambda i: (i, 4)), pl.BlockSpec((B_ROWS, B_WIDTH), lambda i: (i, 5)),
                  pl.BlockSpec((B_ROWS, B_WIDTH), lambda i: (i, 1)), vec, vec, wspec, bspec,
                  pl.BlockSpec(memory_space=pl.ANY)],
        out_specs=[pl.BlockSpec((B_ROWS, 2 * B_WIDTH), lambda i: (i, 2)), vec, vec, wspec, bspec],
        out_shape=[jax.ShapeDtypeStruct(dproj.shape, F32), jax.ShapeDtypeStruct((1, B_WIDTH), F32),
                   jax.ShapeDtypeStruct((1, B_WIDTH), F32), jax.ShapeDtypeStruct((B_GROUPS, B_CHUNK, B_CHUNK), F32),
                   jax.ShapeDtypeStruct((B_CHUNK, B_GROUPS), F32)],
        aliases={7: 0}, args=(proj, proj, dmixin, ln_g, ln_b, w, bias_t, dproj), exchange=exchange)


C_FWD_BLOCKS = 8
C_BWD_BLOCKS = 4
C_PAIR = 2 * C_HEAD_DIM
C_PAIRS = C_HEADS // 2
C_SCALE = 1.0 / math.sqrt(C_HEAD_DIM)
C_ROT_DIM = 2 * C_ROT_HALF
ROPE_ROWS = 1024


def rope_tables(pos_col, name):
    t = pos_col.shape[0]

    def body(p_ref, c_ref, a_ref, b_ref):
        lane = jnp.bitwise_and(lax.broadcasted_iota(jnp.int32, (1, C_PAIR), 1), C_HEAD_DIM - 1)
        j = jnp.bitwise_and(lane, C_ROT_HALF - 1).astype(F32)
        inv = jnp.exp(j * (-math.log(ROPE_THETA) / C_ROT_HALF))
        ang = p_ref[...].astype(F32) * inv
        cos, sin = jnp.cos(ang), jnp.sin(ang)
        c_ref[...] = jnp.where(lane < C_ROT_DIM, cos, 1.0)
        a_ref[...] = jnp.where(lane < C_ROT_HALF, -sin, 0.0)
        b_ref[...] = jnp.where(jnp.logical_and(lane >= C_ROT_HALF, lane < C_ROT_DIM), sin, 0.0)

    tab = pl.BlockSpec((ROPE_ROWS, C_PAIR), lambda i: (i, 0))
    return pl.pallas_call(
        body, name=name, grid=(t // ROPE_ROWS,),
        in_specs=[pl.BlockSpec((ROPE_ROWS, 1), lambda i: (i, 0))],
        out_specs=[tab, tab, tab],
        out_shape=[jax.ShapeDtypeStruct((t, C_PAIR), F32)] * 3,
        compiler_params=_params(("arbitrary",)),
    )(pos_col)


def _rope(x, c, a, b):
    return x * c + pltpu.roll(x, C_PAIR - C_ROT_HALF, 1) * a + pltpu.roll(x, C_ROT_HALF, 1) * b


def _rope_t(d, c, a, b):
    return d * c + pltpu.roll(d * a, C_ROT_HALF, 1) + pltpu.roll(d * b, C_PAIR - C_ROT_HALF, 1)


def _attn_rows(idx, dil):
    nblk = SEQ // dil // C_BLOCK
    r, n = idx // nblk, idx % nblk
    start = r + dil * C_BLOCK * n
    prev = r + dil * C_BLOCK * jnp.maximum(n - 1, 0)
    if dil == 1:
        return pl.ds(pl.multiple_of(start, C_BLOCK), C_BLOCK), pl.ds(pl.multiple_of(prev, C_BLOCK), C_BLOCK), n > 0
    return pl.ds(start, C_BLOCK, stride=dil), pl.ds(prev, C_BLOCK, stride=dil), n > 0


def _head_masks():
    low = lax.broadcasted_iota(jnp.int32, (1, C_PAIR), 1) < C_HEAD_DIM
    return low, jnp.logical_not(low)


def _attn_mask(has_prev):
    i = jnp.bitwise_and(lax.broadcasted_iota(jnp.int32, (2 * C_BLOCK, 2 * C_BLOCK), 0), C_BLOCK - 1)
    j = lax.broadcasted_iota(jnp.int32, (2 * C_BLOCK, 2 * C_BLOCK), 1)
    return jnp.logical_or(j <= i, jnp.logical_and(j - C_BLOCK >= i, has_prev))


def _stack_heads(x):
    low, high = _head_masks()
    return jnp.concatenate([jnp.where(low, x, 0.0), jnp.where(high, x, 0.0)], axis=0)


def _unstack_heads(x):
    low, _ = _head_masks()
    return jnp.where(low, x[:C_BLOCK], x[C_BLOCK:])


def attn_fwd(qkv, cos_t, sin_a, sin_b, batch, name, exchange=None):
    t = qkv.shape[0]
    nbr = len(C_DILATIONS)

    def body(q_ref, k_ref, v_ref, c_ref, a_ref, b_ref, o_ref, l_ref, qs, ks, *stats):
        acc, mm, dd = stats[0:nbr], stats[nbr:2 * nbr], stats[2 * nbr:3 * nbr]
        c, a, b = c_ref[...], a_ref[...], b_ref[...]
        qs[...] = _rope(q_ref[...], c, a, b) * C_SCALE
        ks[...] = _rope(k_ref[...], c, a, b)
        def load(idx, dil):
            rows, prev, has_prev = _attn_rows(idx, dil)
            return rows, (has_prev, qs[rows, :], ks[rows, :], ks[prev, :], v_ref[rows, :], v_ref[prev, :])

        def scores(has_prev, q, k_own, k_prev, v_own, v_prev):
            k_cat = jnp.concatenate([k_own, k_prev], axis=0).astype(BF16)
            return jnp.where(_attn_mask(has_prev), _dot_nt(_stack_heads(q).astype(BF16), k_cat), NEG_BIG)

        def softmax(s):
            m = jnp.max(s, axis=-1, keepdims=True)
            p = jnp.exp(s - m)
            return p.astype(BF16), m, jnp.sum(p, axis=-1, keepdims=True)

        def values(pb, has_prev, q, k_own, k_prev, v_own, v_prev):
            low, high = _head_masks()
            v_cat = jnp.concatenate([v_own, v_prev], axis=0)
            p_wide = jnp.concatenate([pb[:C_BLOCK], pb[C_BLOCK:]], axis=1)
            v_tall = jnp.concatenate([jnp.where(low, v_cat, 0.0), jnp.where(high, v_cat, 0.0)], axis=0).astype(BF16)
            return _dot(p_wide, v_tall)

        for bi, dil in enumerate(C_DILATIONS):
            def pair(i, carry, bi=bi, dil=dil):
                low, _ = _head_masks()
                loaded = [load(C_FWD_BLOCKS * i + k, dil) for k in range(C_FWD_BLOCKS)]
                ss = [scores(*ops) for _, ops in loaded]
                sm = [softmax(s) for s in ss]
                pvs = [values(pb, *ops) for (pb, _, _), (_, ops) in zip(sm, loaded)]
                for (rows, _), (_, m, den), pv in zip(loaded, sm, pvs):
                    acc[bi][rows, :] = pv
                    mm[bi][rows, :] = jnp.where(low, m[:C_BLOCK], m[C_BLOCK:])
                    dd[bi][rows, :] = jnp.where(low, den[:C_BLOCK], den[C_BLOCK:])
                return carry

            lax.fori_loop(0, SEQ // C_BLOCK // C_FWD_BLOCKS, pair, 0)
        step = 256
        for r0 in range(0, SEQ, step):
            rr = slice(r0, r0 + step)
            ms = [mm[g][rr, :] for g in range(nbr)]
            m_all = functools.reduce(jnp.maximum, ms)
            ws = [jnp.exp(m - m_all) for m in ms]
            num = sum(acc[g][rr, :] * ws[g] for g in range(nbr))
            den = sum(dd[g][rr, :] * ws[g] for g in range(nbr))
            o_ref[rr, :] = (num / den).astype(BF16)
            l_ref[rr, :] = m_all + jnp.log(den)

    def col(k):
        return pl.BlockSpec((SEQ, C_PAIR), lambda b, p: (b, k * C_PAIRS + p))

    tab = pl.BlockSpec((SEQ, C_PAIR), lambda b, p: (b, 0))
    return _call(
        body, name=name, grid=(batch, C_PAIRS),
        in_specs=[col(0), col(1), col(2), tab, tab, tab],
        out_specs=[col(0), col(0)],
        out_shape=[jax.ShapeDtypeStruct((t, D_MODEL), BF16), jax.ShapeDtypeStruct((t, D_MODEL), F32)],
        scratch_shapes=[pltpu.VMEM((SEQ, C_PAIR), F32)] * (2 + 3 * nbr),
        args=(qkv, qkv, qkv, cos_t, sin_a, sin_b), exchange=exchange)


def attn_bwd(qkv, cos_t, sin_a, sin_b, o, lse, do, batch, name, exchange=None):
    t = qkv.shape[0]

    def body(q_ref, k_ref, v_ref, c_ref, a_ref, b_ref, o_ref, l_ref, do_ref, dq_ref, dk_ref, dv_ref,
             qs, ks, dqs, dks, dvs, dlt):
        c, a, b = c_ref[...], a_ref[...], b_ref[...]
        qs[...] = _rope(q_ref[...], c, a, b) * C_SCALE
        ks[...] = _rope(k_ref[...], c, a, b)
        prod = do_ref[...] * o_ref[...].astype(F32)
        low = lax.broadcasted_iota(jnp.int32, (1, C_PAIR), 1) < C_HEAD_DIM
        s_low = jnp.sum(jnp.where(low, prod, 0.0), axis=-1, keepdims=True)
        s_all = jnp.sum(prod, axis=-1, keepdims=True)
        dlt[...] = jnp.where(low, s_low, s_all - s_low)
        dqs[...] = jnp.zeros_like(dqs)
        dks[...] = jnp.zeros_like(dks)
        dvs[...] = jnp.zeros_like(dvs)
        def load(idx, dil):
            rows, prev, has_prev = _attn_rows(idx, dil)
            return (rows, prev), (has_prev, qs[rows, :], do_ref[rows, :], ks[rows, :], ks[prev, :],
                                  v_ref[rows, :], v_ref[prev, :], l_ref[rows, :], dlt[rows, :])

        def operands(has_prev, q, do, k_own, k_prev, v_own, v_prev, l_full, d_full):
            lcol = jnp.concatenate([l_full[:, 0:1], l_full[:, C_HEAD_DIM:C_HEAD_DIM + 1]], axis=0)
            dcol = jnp.concatenate([d_full[:, 0:1], d_full[:, C_HEAD_DIM:C_HEAD_DIM + 1]], axis=0)
            return (_stack_heads(q).astype(BF16), _stack_heads(do).astype(BF16),
                    jnp.concatenate([k_own, k_prev], axis=0).astype(BF16),
                    jnp.concatenate([v_own, v_prev], axis=0).astype(BF16), lcol, dcol, _attn_mask(has_prev))

        for dil in C_DILATIONS:
            def pair(i, carry, dil=dil):
                loaded = [load(C_BWD_BLOCKS * i + k, dil) for k in range(C_BWD_BLOCKS)]
                ops = [operands(*o) for _, o in loaded]
                ss = [_dot_nt(q_stack, k_cat) for q_stack, _, k_cat, _, _, _, _ in ops]
                dps = [_dot_nt(do_stack, v_cat) for _, do_stack, _, v_cat, _, _, _ in ops]
                ps = [jnp.exp(jnp.where(o[6], s, NEG_BIG) - o[4]) for s, o in zip(ss, ops)]
                dss = [(p * (dp - o[5])).astype(BF16) for p, dp, o in zip(ps, dps, ops)]
                dvs_ = [_dot_tn(p.astype(BF16), o[1]) for p, o in zip(ps, ops)]
                dks_ = [_dot_tn(ds, o[0]) for ds, o in zip(dss, ops)]
                dqs_ = [_unstack_heads(_dot(ds, o[2])) for ds, o in zip(dss, ops)]
                results = list(zip(dqs_, dks_, dvs_))
                for ((rows, prev), _), (dq, dk_cat, dv_cat) in zip(loaded, results):
                    dqs[rows, :] += dq
                    dks[rows, :] += dk_cat[:C_BLOCK]
                    dvs[rows, :] += dv_cat[:C_BLOCK]
                    dks[prev, :] += dk_cat[C_BLOCK:]
                    dvs[prev, :] += dv_cat[C_BLOCK:]
                return carry

            lax.fori_loop(0, SEQ // C_BLOCK // C_BWD_BLOCKS, pair, 0)
        dq_ref[...] = _rope_t(dqs[...] * C_SCALE, c, a, b).astype(BF16)
        dk_ref[...] = _rope_t(dks[...], c, a, b).astype(BF16)
        dv_ref[...] = dvs[...].astype(BF16)

    def col(k):
        return pl.BlockSpec((SEQ, C_PAIR), lambda b, p: (b, k * C_PAIRS + p))

    tab = pl.BlockSpec((SEQ, C_PAIR), lambda b, p: (b, 0))
    out = jax.ShapeDtypeStruct((t, D_MODEL), BF16)
    return _call(
        body, name=name, grid=(batch, C_PAIRS),
        in_specs=[col(0), col(1), col(2), tab, tab, tab, col(0), col(0), col(0)],
        out_specs=[col(0), col(0), col(0)],
        out_shape=[out, out, out],
        scratch_shapes=[pltpu.VMEM((SEQ, C_PAIR), F32)] * 6,
        args=(qkv, qkv, qkv, cos_t, sin_a, sin_b, o, lse, do), exchange=exchange)


def sibling_swap(arrays, name):
    n = len(arrays)

    def body(*refs):
        ins, outs = refs[:n], refs[n:2 * n]
        send_sems, recv_sems = refs[2 * n:]
        x, y, c, _ = _place()
        sends = []
        for a in range(n):
            cp = pltpu.make_async_remote_copy(
                src_ref=ins[a], dst_ref=outs[a], send_sem=send_sems.at[a], recv_sem=recv_sems.at[a],
                device_id=(x, y, 1 - c), device_id_type=MESH)
            cp.start()
            sends.append(cp)
        for cp in sends:
            cp.wait_recv()
        for cp in sends:
            cp.wait_send()

    return pl.pallas_call(
        body, name=name,
        in_specs=[ANY] * n, out_specs=[ANY] * n,
        out_shape=[jax.ShapeDtypeStruct(s.shape, s.dtype) for s in arrays],
        scratch_shapes=[pltpu.SemaphoreType.DMA((n,)), pltpu.SemaphoreType.DMA((n,))],
    )(*arrays)


def allreduce_small(slab, name):
    rows, lanes = slab.shape

    def body(x_ref, out_ref, gath, send_sems, recv_sems, local_sem):
        x, y, c, chips = _place()
        me, sibling = (x, y, c), (x, y, 1 - c)

        def slot(px, py, pc):
            return gath.at[4 * px + 2 * py + pc]

        def copy(k, block, to, src=None):
            return pltpu.make_async_remote_copy(
                src_ref=slot(*block) if src is None else src, dst_ref=slot(*block),
                send_sem=send_sems.at[k], recv_sem=recv_sems.at[k], device_id=to, device_id_type=MESH)

        mine = pltpu.make_async_copy(x_ref, slot(*me), local_sem)
        mine.start()
        first = [copy(0, me, sibling, src=x_ref)]
        first += [copy(1 + j, me, (*chip, c), src=x_ref) for j, chip in enumerate(chips)]
        for cp in first:
            cp.start()
        passed = [copy(4 + j, (*chip, c), sibling) for j, chip in enumerate(chips)]
        for j, chip in enumerate(chips):
            copy(1 + j, (*chip, c), me).wait_recv()
            passed[j].start()
        copy(0, sibling, me).wait_recv()
        for j, chip in enumerate(chips):
            copy(4 + j, (*chip, 1 - c), me).wait_recv()
        for cp in first + passed:
            cp.wait_send()
        mine.wait()
        total = gath[0]
        for d in range(1, N_DEV):
            total = total + gath[d]
        out_ref[...] = total

    return pl.pallas_call(
        body, name=name,
        in_specs=[pl.BlockSpec(memory_space=pltpu.VMEM)],
        out_specs=pl.BlockSpec(memory_space=pltpu.VMEM),
        out_shape=jax.ShapeDtypeStruct((rows, lanes), F32),
        scratch_shapes=[pltpu.VMEM((N_DEV, rows, lanes), F32),
                        pltpu.SemaphoreType.DMA((7,)), pltpu.SemaphoreType.DMA((7,)), pltpu.SemaphoreType.DMA],
    )(slab)


ELT_ROWS = 512


def reduce_slabs(r, name, part=0, parts=1, into=None):
    _, rows, cols = r.shape
    br = min(rows, ELT_ROWS)
    nblk = rows // br

    def body(r_ref, *rest):
        o_ref = rest[-1]
        o_ref[...] = ((r_ref[3].astype(F32) + r_ref[0].astype(F32)) + r_ref[1].astype(F32)) + r_ref[2].astype(F32)

    return pl.pallas_call(
        body, name=name, grid=(nblk,),
        in_specs=[pl.BlockSpec((N_CHIPS, br, cols), lambda i: (0, i, 0))] + ([] if into is None else [ANY]),
        out_specs=pl.BlockSpec((br, cols), lambda i: (part * nblk + i, 0)),
        out_shape=jax.ShapeDtypeStruct((parts * rows, cols), F32),
        input_output_aliases={} if into is None else {1: 0},
        compiler_params=_params(("arbitrary",)),
    )(*([r] if into is None else [r, into]))


def _adamw(w, g, m, v):
    m = ADAM_B1 * m + (1.0 - ADAM_B1) * g
    v = ADAM_B2 * v + (1.0 - ADAM_B2) * jnp.square(g)
    m_hat = m / (1.0 - ADAM_B1 ** ADAM_STEP)
    v_hat = v / (1.0 - ADAM_B2 ** ADAM_STEP)
    delta = -ADAM_LR * (m_hat / (jnp.sqrt(v_hat) + ADAM_EPS) + ADAM_WD * w)
    return delta, m, v


def adamw_big(w, s_mine, s_sibling, m, v, name):
    rows, cols = w.shape

    def body(w_ref, a_ref, b_ref, m_ref, v_ref, g_out, d_out, m_out, v_out):
        g = a_ref[...] + b_ref[...]
        g_out[...] = g
        d_out[...], m_out[...], v_out[...] = _adamw(w_ref[...], g, m_ref[...], v_ref[...])

    blk = pl.BlockSpec((min(rows, ELT_ROWS), cols), lambda i: (i, 0))
    out = jax.ShapeDtypeStruct((rows, cols), F32)
    return pl.pallas_call(
        body, name=name, grid=(rows // min(rows, ELT_ROWS),),
        in_specs=[blk] * 5, out_specs=[blk] * 4, out_shape=[out] * 4,
        compiler_params=_params(("arbitrary",)),
    )(w, s_mine, s_sibling, m, v)


def adamw_small(ws, gs, ms, vs, name):
    n = len(ws)

    def body(*refs):
        w_refs, g_refs, m_refs, v_refs = (refs[k * n:(k + 1) * n] for k in range(4))
        d_out, m_out, v_out = (refs[(4 + k) * n:(5 + k) * n] for k in range(3))
        for i in range(n):
            d_out[i][...], m_out[i][...], v_out[i][...] = _adamw(
                w_refs[i][...], g_refs[i][...], m_refs[i][...], v_refs[i][...])

    outs = [jax.ShapeDtypeStruct(w.shape, F32) for w in ws]
    res = pl.pallas_call(body, name=name, out_shape=outs * 3)(*ws, *gs, *ms, *vs)
    return res[:n], res[n:2 * n], res[2 * n:]


SLAB_LANES = 128
SLAB_ROW_ALIGN = 8


def _pack(parts):
    flat = jnp.concatenate([p.reshape(-1) for p in parts])
    rows = -(-flat.shape[0] // (SLAB_LANES * SLAB_ROW_ALIGN)) * SLAB_ROW_ALIGN
    flat = jnp.pad(flat, (0, rows * SLAB_LANES - flat.shape[0]))
    return flat.reshape(rows, SLAB_LANES)


def _unpack(slab, shapes):
    flat = slab.reshape(-1)
    out, pos = [], 0
    for s in shapes:
        size = math.prod(s)
        out.append(flat[pos:pos + size].reshape(s))
        pos += size
    return out


def kernel(x, positions, norm_mix_pre, norm_mix_post, norm_ffn_pre, norm_ffn_post, w_in_even, lb_table, a_norm, b_ln_g, b_ln_b, b_ws, b_bias, w_out_even, w_in_odd, w_out_odd, w_ff1, w_ff2, loss_target, m_norm_mix_pre, m_norm_mix_post, m_norm_ffn_pre, m_norm_ffn_post, m_w_in_even, m_lb_table, m_a_norm, m_b_ln_g, m_b_ln_b, m_b_ws, m_b_bias, m_w_out_even, m_w_in_odd, m_w_out_odd, m_w_ff1, m_w_ff2, v_norm_mix_pre, v_norm_mix_post, v_norm_ffn_pre, v_norm_ffn_post, v_w_in_even, v_lb_table, v_a_norm, v_b_ln_g, v_b_ln_b, v_b_ws, v_b_bias, v_w_out_even, v_w_in_odd, v_w_out_odd, v_w_ff1, v_w_ff2):
    batch = x.shape[0]
    t = batch * SEQ
    d = D_MODEL
    x0 = x.reshape(t, d)
    target = loss_target.reshape(t, d)

    def gain(p, layer):
        return p[layer:layer + 1]

    def gather(*shards):
        return _Exchange("gather", [w.astype(BF16) for w in shards])

    def scatter(*grads):
        return _Exchange("scatter", grads)

    (win_e,) = exchange_alone(gather(w_in_even[0]), "gather_in_even")
    bias_t = b_bias[0].T
    proj, h0, w1_0 = norm_matmul(x0, gain(norm_mix_pre, 0), win_e, "in_proj_even", exchange=gather(w_ff1[0]))
    oa, states, decays, w2_0, wout_e = hgrn2_fwd(proj, lb_table, a_norm, batch, "hgrn2_fwd",
                                                 exchange=gather(w_ff2[0], w_out_even[0]))
    (mixin,) = gmlp_fwd(proj, oa, b_ln_g, b_ln_b, b_ws[0], bias_t, "gmlp_fwd")
    mix0, x1 = out_proj(mixin, wout_e, x0, gain(norm_mix_post, 0), "out_proj_even")
    x2, hf0, a0, y0, win_o, wout_o = ffn_fwd(x1, gain(norm_ffn_pre, 0), w1_0, w2_0, gain(norm_ffn_post, 0),
                                             "ffn_fwd_0", exchange=gather(w_in_odd[0], w_out_odd[0]))
    qkv, h1 = norm_matmul(x2, gain(norm_mix_pre, 1), win_o, "in_proj_odd")
    cos_t, sin_a, sin_b = rope_tables(positions.reshape(t, 1), "rope_tables")
    ao, lse, w1_1, w2_1 = attn_fwd(qkv, cos_t, sin_a, sin_b, batch, "attn_fwd", exchange=gather(w_ff1[1], w_ff2[1]))
    mix1, x3 = out_proj(ao, wout_o, x2, gain(norm_mix_post, 1), "out_proj_odd")
    dx4, hf1, a1, y1, loss_part = ffn_fwd(x3, gain(norm_ffn_pre, 1), w1_1, w2_1, gain(norm_ffn_post, 1),
                                          "ffn_fwd_1", target=target)

    hc = D_FF // N_CHIPS
    dx3, dy1, da1, dg_fpre1, dg_fpost1 = ffn_bwd(
        dx4, x3, y1, a1, gain(norm_ffn_pre, 1), gain(norm_ffn_post, 1), w1_1, w2_1, "ffn_bwd_1")
    g_w1_1 = weight_grad(hf1, da1, "b", d, hc, False, "wgrad_ff1_1")
    g_w2_1 = weight_grad(a1, dy1, "a", hc, d, True, "wgrad_ff2_1")
    dmix1, dao, dg_mpost1 = out_proj_bwd(dx3, mix1, gain(norm_mix_post, 1), wout_o, "out_proj_bwd_odd")
    g_wout_o = weight_grad(ao, dmix1, "a", d // N_CHIPS, d, False, "wgrad_out_odd")
    dq, dk, dv, r_w1_1, r_w2_1, r_wout_o = attn_bwd(qkv, cos_t, sin_a, sin_b, ao, lse, dao, batch, "attn_bwd",
                                                    exchange=scatter(g_w1_1, g_w2_1, g_wout_o))
    dqkv = jnp.concatenate([dq, dk, dv], axis=1)
    dx2, dg_mpre1 = norm_matmul_bwd(dqkv, win_o, x2, gain(norm_mix_pre, 1), dx3, "in_proj_bwd_odd")
    g_win_o = weight_grad(h1, dqkv, "b", d, 3 * d // N_CHIPS, False, "wgrad_in_odd")
    dx1, dy0, da0, dg_fpre0, dg_fpost0, r_win_o = ffn_bwd(
        dx2, x1, y0, a0, gain(norm_ffn_pre, 0), gain(norm_ffn_post, 0), w1_0, w2_0, "ffn_bwd_0",
        exchange=scatter(g_win_o))
    g_w1_0 = weight_grad(hf0, da0, "b", d, hc, False, "wgrad_ff1_0")
    g_w2_0 = weight_grad(a0, dy0, "a", hc, d, True, "wgrad_ff2_0")
    dmix0, dmixin, dg_mpost0 = out_proj_bwd(dx1, mix0, gain(norm_mix_post, 0), wout_e, "out_proj_bwd_even")
    g_wout_e = weight_grad(mixin, dmix0, "a", d // N_CHIPS, d, False, "wgrad_out_even")
    dproj, d_lb, d_anorm, r_w1_0 = hgrn2_bwd(
        proj, states, decays, lb_table, a_norm, dmixin, batch, "hgrn2_bwd", exchange=scatter(g_w1_0))
    dproj, d_lng, d_lnb, d_ws, d_bias_t, r_wout_e = gmlp_bwd(
        proj, dmixin, b_ln_g, b_ln_b, b_ws[0], bias_t, dproj, "gmlp_bwd", exchange=scatter(g_wout_e))
    g_win_e, r_w2_0 = weight_grad(h0, dproj, "b", d, 3 * d // N_CHIPS, False, "wgrad_in_even",
                                  exchange=scatter(g_w2_0))
    dx0, dg_mpre0, r_win_e = norm_matmul_bwd(dproj, win_e, x0, gain(norm_mix_pre, 0), dx1, "in_proj_bwd_even",
                                             exchange=scatter(g_win_e))
    grad_x = dx0.reshape(x.shape)

    s_w1 = reduce_slabs(r_w1_1, "reduce_ff1_1", part=1, parts=2)
    s_w1 = reduce_slabs(r_w1_0, "reduce_ff1_0", part=0, parts=2, into=s_w1)
    s_w2 = reduce_slabs(r_w2_1, "reduce_ff2_1", part=1, parts=2)
    s_w2 = reduce_slabs(r_w2_0, "reduce_ff2_0", part=0, parts=2, into=s_w2)
    sums = [reduce_slabs(r_win_e, "reduce_in_even"), reduce_slabs(r_wout_e, "reduce_out_even"),
            reduce_slabs(r_win_o, "reduce_in_odd"), reduce_slabs(r_wout_o, "reduce_out_odd"), s_w1, s_w2]
    sibling = sibling_swap(sums, "sibling_swap")
    big_w = [w_in_even, w_out_even, w_in_odd, w_out_odd, w_ff1, w_ff2]
    big_m = [m_w_in_even, m_w_out_even, m_w_in_odd, m_w_out_odd, m_w_ff1, m_w_ff2]
    big_v = [v_w_in_even, v_w_out_even, v_w_in_odd, v_w_out_odd, v_w_ff1, v_w_ff2]
    big = []
    for i, (w, m, v) in enumerate(zip(big_w, big_m, big_v)):
        two_d = (-1, w.shape[-1])
        res = adamw_big(w.reshape(two_d), sums[i], sibling[i], m.reshape(two_d), v.reshape(two_d), "adamw_big_%d" % i)
        big.append([r.reshape(w.shape) for r in res])

    small_w = [norm_mix_pre, norm_mix_post, norm_ffn_pre, norm_ffn_post, lb_table, a_norm, b_ln_g, b_ln_b, b_ws, b_bias]
    small_m = [m_norm_mix_pre, m_norm_mix_post, m_norm_ffn_pre, m_norm_ffn_post, m_lb_table, m_a_norm, m_b_ln_g,
               m_b_ln_b, m_b_ws, m_b_bias]
    small_v = [v_norm_mix_pre, v_norm_mix_post, v_norm_ffn_pre, v_norm_ffn_post, v_lb_table, v_a_norm, v_b_ln_g,
               v_b_ln_b, v_b_ws, v_b_bias]
    partial = [jnp.concatenate([dg_mpre0, dg_mpre1]), jnp.concatenate([dg_mpost0, dg_mpost1]),
               jnp.concatenate([dg_fpre0, dg_fpre1]), jnp.concatenate([dg_fpost0, dg_fpost1]),
               d_lb, d_anorm, d_lng, d_lnb, d_ws[None], d_bias_t.T[None]]
    *small_g, loss = _unpack(allreduce_small(_pack(partial + [loss_part]), "allreduce_small"),
                             [w.shape for w in small_w] + [()])
    small_d, small_nm, small_nv = adamw_small(small_w, small_g, small_m, small_v, "adamw_small")

    order = ["norm_mix_pre", "norm_mix_post", "norm_ffn_pre", "norm_ffn_post", "w_in_even", "lb_table", "a_norm",
             "b_ln_g", "b_ln_b", "b_ws", "b_bias", "w_out_even", "w_in_odd", "w_out_odd", "w_ff1", "w_ff2"]
    small_names = ["norm_mix_pre", "norm_mix_post", "norm_ffn_pre", "norm_ffn_post", "lb_table", "a_norm",
                   "b_ln_g", "b_ln_b", "b_ws", "b_bias"]
    big_names = ["w_in_even", "w_out_even", "w_in_odd", "w_out_odd", "w_ff1", "w_ff2"]
    grads, deltas, new_m, new_v = {}, {}, {}, {}
    for i, nm in enumerate(small_names):
        grads[nm], deltas[nm], new_m[nm], new_v[nm] = small_g[i], small_d[i], small_nm[i], small_nv[i]
    for i, nm in enumerate(big_names):
        grads[nm], deltas[nm], new_m[nm], new_v[nm] = big[i]
    return (loss, grad_x, *[grads[n] for n in order], *[deltas[n] for n in order],
            *[new_m[n] for n in order], *[new_v[n] for n in order])
```

```python
import functools
import math

import jax
import jax.numpy as jnp
from jax import lax
from jax.experimental import pallas as pl
from jax.experimental.pallas import tpu as pltpu

F32 = jnp.float32
BF16 = jnp.bfloat16
MESH = pl.DeviceIdType.MESH

D_MODEL = 1024
SEQ = 2048
D_FF = 4096
N_CHIPS = 4
A_WIDTH = 512
A_HEADS = 4
A_DK = 128
A_CHUNK = 64
A_SUB = 16
B_WIDTH = 512
B_GROUPS = 4
B_CHUNK = 128
C_HEADS = 16
C_HEAD_DIM = 64
C_ROT_HALF = 8
C_BLOCK = 128
C_DILATIONS = (1, 4, 16)
ROPE_THETA = 500000.0
EPS = 1e-6
ADAM_LR = 0.001
ADAM_B1 = 0.9
ADAM_B2 = 0.999
ADAM_EPS = 1e-08
ADAM_WD = 0.01
ADAM_STEP = 10

ROW_TILE = 512
FFN_ROWS = 1024
WGRAD_ROWS = 2048
VMEM_LIMIT = 56 * 1024 * 1024
NEG_BIG = -1e30


def _params(sem=None):
    return pltpu.CompilerParams(dimension_semantics=sem, vmem_limit_bytes=VMEM_LIMIT)


def _dot(a, b):
    return jnp.dot(a, b, preferred_element_type=F32)


def _dot_nt(a, b):
    return lax.dot_general(a, b, (((1,), (1,)), ((), ())), preferred_element_type=F32)


def _dot_tn(a, b):
    return lax.dot_general(a, b, (((0,), (0,)), ((), ())), preferred_element_type=F32)


def _rms(x, g):
    r = lax.rsqrt(jnp.mean(x * x, axis=-1, keepdims=True) + EPS)
    return x * r * g


def _rms_bwd(x, g, dy):
    r = lax.rsqrt(jnp.mean(x * x, axis=-1, keepdims=True) + EPS)
    xh = x * r
    dg = jnp.sum(dy * xh, axis=0, keepdims=True)
    dxh = dy * g
    dx = r * (dxh - xh * jnp.mean(dxh * xh, axis=-1, keepdims=True))
    return dx, dg


def _accumulate(ref, val, first):
    @pl.when(first)
    def _():
        ref[...] = val

    @pl.when(jnp.logical_not(first))
    def _():
        ref[...] += val


N_DEV = 8
ANY = pl.BlockSpec(memory_space=pl.ANY)


def _place():
    x, y, c = lax.axis_index("x"), lax.axis_index("y"), lax.axis_index("c")
    return x, y, c, [(1 - x, y), (x, 1 - y), (1 - x, 1 - y)]


class _Exchange:
    def __init__(self, kind, arrays):
        self.kind, self.arrays, self.n = kind, list(arrays), len(arrays)
        per_peer = pltpu.SemaphoreType.DMA((3 * self.n,))
        if kind == "gather":
            self.out_shape = [jax.ShapeDtypeStruct((N_CHIPS,) + a.shape, a.dtype) for a in self.arrays]
            self.scratch = [per_peer, per_peer, pltpu.SemaphoreType.DMA((self.n,)), per_peer, per_peer]
        else:
            self.out_shape = [jax.ShapeDtypeStruct(a.shape, a.dtype) for a in self.arrays]
            self.scratch = [per_peer, per_peer, pltpu.SemaphoreType.DMA((self.n,))]

    def _copies(self, ins, outs, sems):
        send_sems, recv_sems, local_sems = sems[:3]
        x, y, c, chips = _place()
        me = 2 * x + y
        local, remote = [], []
        for a in range(self.n):
            if self.kind == "gather":
                local.append(pltpu.make_async_copy(ins[a], outs[a].at[me], local_sems.at[a]))
                half = self.arrays[a].shape[0] // 2

                def rows(ref, core, half=half):
                    return ref.at[pl.ds(core * half, half)]
            else:
                local.append(pltpu.make_async_copy(ins[a].at[me], outs[a].at[3], local_sems.at[a]))
            for j, (px, py) in enumerate(chips):
                k = 3 * a + j
                peer = 2 * px + py

                def copy(src, dst, to, send_sem=send_sems.at[k], recv_sem=recv_sems.at[k]):
                    return pltpu.make_async_remote_copy(src_ref=src, dst_ref=dst, send_sem=send_sem, recv_sem=recv_sem,
                                                        device_id=to, device_id_type=MESH)

                if self.kind == "gather":
                    sent = copy(rows(ins[a], c), rows(outs[a].at[me], c), (px, py, c))
                    landed = copy(rows(ins[a], c), rows(outs[a].at[peer], c), (px, py, c))
                    on = dict(send_sem=sems[3].at[k], recv_sem=sems[4].at[k])
                    passed = copy(rows(outs[a].at[peer], c), rows(outs[a].at[peer], c), (x, y, 1 - c), **on)
                    handed = copy(rows(outs[a].at[peer], c), rows(outs[a].at[peer], 1 - c), (x, y, 1 - c), **on)
                    remote.append((sent, landed, passed, handed))
                else:
                    sent = copy(ins[a].at[peer], outs[a].at[j], (px, py, c))
                    remote.append((sent, sent, None, None))
        return local, remote

    def start(self, ins, outs, sems):
        local, remote = self._copies(ins, outs, sems)
        for cp in local:
            cp.start()
        for sent, _, _, _ in remote:
            sent.start()

    def finish(self, ins, outs, sems):
        local, remote = self._copies(ins, outs, sems)
        for _, landed, passed, _ in remote:
            landed.wait_recv()
            if passed is not None:
                passed.start()
        for sent, _, passed, handed in remote:
            if passed is not None:
                handed.wait_recv()
                passed.wait_send()
            sent.wait_send()
        for cp in local:
            cp.wait()


def _call(body, *, name, grid, in_specs, out_specs, out_shape, args, scratch_shapes=(), aliases=None, exchange=None):
    if exchange is None:
        return pl.pallas_call(
            body, name=name, grid=grid, in_specs=in_specs, out_specs=out_specs, out_shape=out_shape,
            scratch_shapes=list(scratch_shapes), input_output_aliases=aliases or {},
            compiler_params=_params(("arbitrary",) * len(grid)))(*args)
    n_in, n_out, n_scr, n_ex = len(in_specs), len(out_specs), len(scratch_shapes), exchange.n
    steps = grid

    def wrapped(*refs):
        ins, refs = refs[:n_in], refs[n_in:]
        ex_in, refs = refs[:n_ex], refs[n_ex:]
        outs, refs = refs[:n_out], refs[n_out:]
        ex_out, refs = refs[:n_ex], refs[n_ex:]
        scr, sems = refs[:n_scr], refs[n_scr:]
        first = functools.reduce(jnp.logical_and, [pl.program_id(k) == 0 for k in range(len(steps))])
        last = functools.reduce(jnp.logical_and, [pl.program_id(k) == steps[k] - 1 for k in range(len(steps))])

        @pl.when(first)
        def _():
            exchange.start(ex_in, ex_out, sems)

        body(*ins, *outs, *scr)

        @pl.when(last)
        def _():
            exchange.finish(ex_in, ex_out, sems)

    return pl.pallas_call(
        wrapped, name=name, grid=grid,
        in_specs=list(in_specs) + [ANY] * n_ex, out_specs=list(out_specs) + [ANY] * n_ex,
        out_shape=list(out_shape) + exchange.out_shape,
        scratch_shapes=list(scratch_shapes) + exchange.scratch, input_output_aliases=aliases or {},
        compiler_params=_params(("arbitrary",) * len(grid)))(*args, *exchange.arrays)


def exchange_alone(exchange, name):
    def body(*refs):
        n = exchange.n
        exchange.start(refs[:n], refs[n:2 * n], refs[2 * n:])
        exchange.finish(refs[:n], refs[n:2 * n], refs[2 * n:])

    return pl.pallas_call(
        body, name=name, in_specs=[ANY] * exchange.n, out_specs=[ANY] * exchange.n,
        out_shape=exchange.out_shape, scratch_shapes=exchange.scratch)(*exchange.arrays)


def norm_matmul(x, g, wg, name, exchange=None):
    t, d = x.shape
    nl = wg.shape[2]

    def body(x_ref, g_ref, w_ref, o_ref, h_ref):
        h = _rms(x_ref[...], g_ref[...]).astype(BF16)
        h_ref[...] = h
        for c in range(N_CHIPS):
            o_ref[:, c * nl:(c + 1) * nl] = _dot(h, w_ref[c])

    return _call(
        body, name=name, grid=(t // ROW_TILE,),
        in_specs=[pl.BlockSpec((ROW_TILE, d), lambda i: (i, 0)),
                  pl.BlockSpec((1, d), lambda i: (0, 0)),
                  pl.BlockSpec((N_CHIPS, d, nl), lambda i: (0, 0, 0))],
        out_specs=[pl.BlockSpec((ROW_TILE, N_CHIPS * nl), lambda i: (i, 0)),
                   pl.BlockSpec((ROW_TILE, d), lambda i: (i, 0))],
        out_shape=[jax.ShapeDtypeStruct((t, N_CHIPS * nl), F32), jax.ShapeDtypeStruct((t, d), BF16)],
        args=(x, g, wg), exchange=exchange)


def norm_matmul_bwd(dproj, wg, x, g, dres, name, exchange=None):
    t, d = x.shape
    nl = wg.shape[2]

    def body(dp_ref, w_ref, x_ref, g_ref, dres_ref, dx_ref, dg_ref):
        dh = _dot_nt(dp_ref[:, 0:nl].astype(BF16), w_ref[0])
        for c in range(1, N_CHIPS):
            dh += _dot_nt(dp_ref[:, c * nl:(c + 1) * nl].astype(BF16), w_ref[c])
        dx, dg = _rms_bwd(x_ref[...], g_ref[...], dh)
        dx_ref[...] = dres_ref[...] + dx
        _accumulate(dg_ref, dg, pl.program_id(0) == 0)

    row = pl.BlockSpec((ROW_TILE, d), lambda i: (i, 0))
    vec = pl.BlockSpec((1, d), lambda i: (0, 0))
    return _call(
        body, name=name, grid=(t // ROW_TILE,),
        in_specs=[pl.BlockSpec((ROW_TILE, N_CHIPS * nl), lambda i: (i, 0)),
                  pl.BlockSpec((N_CHIPS, d, nl), lambda i: (0, 0, 0)), row, vec, row],
        out_specs=[row, vec],
        out_shape=[jax.ShapeDtypeStruct((t, d), F32), jax.ShapeDtypeStruct((1, d), F32)],
        args=(dproj, wg, x, g, dres), exchange=exchange)


def out_proj(a, wg, x, g, name):
    t, d = x.shape
    kl = wg.shape[1]

    def body(a_ref, w_ref, x_ref, g_ref, mix_ref, xo_ref):
        acc = _dot(a_ref[:, 0:kl], w_ref[0])
        for c in range(1, N_CHIPS):
            acc += _dot(a_ref[:, c * kl:(c + 1) * kl], w_ref[c])
        mix_ref[...] = acc
        xo_ref[...] = x_ref[...] + _rms(acc, g_ref[...])

    row = pl.BlockSpec((ROW_TILE, d), lambda i: (i, 0))
    return pl.pallas_call(
        body, name=name, grid=(t // ROW_TILE,),
        in_specs=[row, pl.BlockSpec((N_CHIPS, kl, d), lambda i: (0, 0, 0)), row,
                  pl.BlockSpec((1, d), lambda i: (0, 0))],
        out_specs=[row, row],
        out_shape=[jax.ShapeDtypeStruct((t, d), F32), jax.ShapeDtypeStruct((t, d), F32)],
        compiler_params=_params(("arbitrary",)),
    )(a, wg, x, g)


def out_proj_bwd(dxo, mix, g, wg, name):
    t, d = mix.shape
    kl = wg.shape[1]

    def body(dxo_ref, mix_ref, g_ref, w_ref, dmix_ref, da_ref, dg_ref):
        dmix, dg = _rms_bwd(mix_ref[...], g_ref[...], dxo_ref[...])
        dmb = dmix.astype(BF16)
        dmix_ref[...] = dmb
        for c in range(N_CHIPS):
            da_ref[:, c * kl:(c + 1) * kl] = _dot_nt(dmb, w_ref[c])
        _accumulate(dg_ref, dg, pl.program_id(0) == 0)

    row = pl.BlockSpec((ROW_TILE, d), lambda i: (i, 0))
    vec = pl.BlockSpec((1, d), lambda i: (0, 0))
    return pl.pallas_call(
        body, name=name, grid=(t // ROW_TILE,),
        in_specs=[row, row, vec, pl.BlockSpec((N_CHIPS, kl, d), lambda i: (0, 0, 0))],
        out_specs=[row, row, vec],
        out_shape=[jax.ShapeDtypeStruct((t, d), BF16), jax.ShapeDtypeStruct((t, d), F32),
                   jax.ShapeDtypeStruct((1, d), F32)],
        compiler_params=_params(("arbitrary",)),
    )(dxo, mix, g, wg)


def ffn_fwd(x, gpre, w1g, w2g, gpost, name, exchange=None, target=None):
    t, d = x.shape
    hc = w1g.shape[2]
    with_loss = target is not None

    def body(x_ref, gpre_ref, w1_ref, w2_ref, gpost_ref, *rest):
        if with_loss:
            t_ref, xo_ref, h_ref, a_ref, y_ref, l_ref, acc = rest
        else:
            xo_ref, h_ref, a_ref, y_ref, acc = rest
        i, c = pl.program_id(0), pl.program_id(1)

        @pl.when(c == 0)
        def _():
            h_ref[...] = _rms(x_ref[...], gpre_ref[...]).astype(BF16)

        a = _dot(h_ref[...], w1_ref[...])
        a_ref[...] = a.astype(BF16)
        r = jnp.square(jnp.maximum(a, 0.0)).astype(BF16)
        _accumulate(acc, _dot(r, w2_ref[...]), c == 0)

        @pl.when(c == N_CHIPS - 1)
        def _():
            y = acc[...]
            y_ref[...] = y
            xo = x_ref[...] + _rms(y, gpost_ref[...])
            if with_loss:
                e = xo - t_ref[...]
                xo_ref[...] = e * (1.0 / d)
                part = jnp.sum(jnp.sum(e * e, axis=-1, keepdims=True), axis=0, keepdims=True) * (0.5 / d)
                _accumulate(l_ref, part, i == 0)
            else:
                xo_ref[...] = xo

    row = pl.BlockSpec((FFN_ROWS, d), lambda i, c: (i, 0))
    vec = pl.BlockSpec((1, d), lambda i, c: (0, 0))
    one = pl.BlockSpec((1, 1), lambda i, c: (0, 0))
    return _call(
        body, name=name, grid=(t // FFN_ROWS, N_CHIPS),
        in_specs=[row, vec,
                  pl.BlockSpec((None, d, hc), lambda i, c: (c, 0, 0)),
                  pl.BlockSpec((None, hc, d), lambda i, c: (c, 0, 0)), vec] + ([row] if with_loss else []),
        out_specs=[row, row, pl.BlockSpec((FFN_ROWS, hc), lambda i, c: (i, c)), row] + ([one] if with_loss else []),
        out_shape=[jax.ShapeDtypeStruct((t, d), F32), jax.ShapeDtypeStruct((t, d), BF16),
                   jax.ShapeDtypeStruct((t, N_CHIPS * hc), BF16), jax.ShapeDtypeStruct((t, d), F32)]
        + ([jax.ShapeDtypeStruct((1, 1), F32)] if with_loss else []),
        scratch_shapes=[pltpu.VMEM((FFN_ROWS, d), F32)],
        args=(x, gpre, w1g, w2g, gpost) + ((target,) if with_loss else ()), exchange=exchange)


def ffn_bwd(dxo, x, y, a, gpre, gpost, w1g, w2g, name, exchange=None):
    t, d = x.shape
    hc = w1g.shape[2]

    def body(dxo_ref, x_ref, y_ref, a_ref, gpre_ref, gpost_ref, w1_ref, w2_ref,
             dxi_ref, dy_ref, da_ref, dgpre_ref, dgpost_ref, acc):
        i, c = pl.program_id(0), pl.program_id(1)

        @pl.when(c == 0)
        def _():
            dy, dg = _rms_bwd(y_ref[...], gpost_ref[...], dxo_ref[...])
            dy_ref[...] = dy.astype(BF16)
            _accumulate(dgpost_ref, dg, i == 0)

        dr = _dot_nt(dy_ref[...], w2_ref[...])
        da = (dr * (2.0 * jnp.maximum(a_ref[...].astype(F32), 0.0))).astype(BF16)
        da_ref[...] = da
        _accumulate(acc, _dot_nt(da, w1_ref[...]), c == 0)

        @pl.when(c == N_CHIPS - 1)
        def _():
            dx, dg = _rms_bwd(x_ref[...], gpre_ref[...], acc[...])
            dxi_ref[...] = dxo_ref[...] + dx
            _accumulate(dgpre_ref, dg, i == 0)

    row = pl.BlockSpec((ROW_TILE, d), lambda i, c: (i, 0))
    vec = pl.BlockSpec((1, d), lambda i, c: (0, 0))
    hid = pl.BlockSpec((ROW_TILE, hc), lambda i, c: (i, c))
    return _call(
        body, name=name, grid=(t // ROW_TILE, N_CHIPS),
        in_specs=[row, row, row, hid, vec, vec,
                  pl.BlockSpec((None, d, hc), lambda i, c: (c, 0, 0)),
                  pl.BlockSpec((None, hc, d), lambda i, c: (c, 0, 0))],
        out_specs=[row, row, hid, vec, vec],
        out_shape=[jax.ShapeDtypeStruct((t, d), F32), jax.ShapeDtypeStruct((t, d), BF16),
                   jax.ShapeDtypeStruct((t, N_CHIPS * hc), BF16),
                   jax.ShapeDtypeStruct((1, d), F32), jax.ShapeDtypeStruct((1, d), F32)],
        scratch_shapes=[pltpu.VMEM((ROW_TILE, d), F32)],
        args=(dxo, x, y, a, gpre, gpost, w1g, w2g), exchange=exchange)


def weight_grad(a, b, chunked, bk, bn, relu2, name, exchange=None):
    t = a.shape[0]
    a_on = chunked == "a"
    rows = min(t, WGRAD_ROWS)
    n_steps = t // rows

    def body(a_ref, b_ref, o_ref, acc):
        s = pl.program_id(1)
        av = a_ref[...]
        if relu2:
            av = jnp.square(jnp.maximum(av.astype(F32), 0.0))
        _accumulate(acc, _dot_tn(av.astype(BF16), b_ref[...].astype(BF16)), s == 0)

        @pl.when(s == n_steps - 1)
        def _():
            o_ref[...] = acc[...].astype(BF16)

    res = _call(
        body, name=name, grid=(N_CHIPS, n_steps),
        in_specs=[pl.BlockSpec((rows, bk), (lambda c, s: (s, c)) if a_on else (lambda c, s: (s, 0))),
                  pl.BlockSpec((rows, bn), (lambda c, s: (s, 0)) if a_on else (lambda c, s: (s, c)))],
        out_specs=[pl.BlockSpec((None, bk, bn), lambda c, s: (c, 0, 0))],
        out_shape=[jax.ShapeDtypeStruct((N_CHIPS, bk, bn), BF16)],
        scratch_shapes=[pltpu.VMEM((bk, bn), F32)],
        args=(a, b), exchange=exchange)
    return res[0] if exchange is None else res


def _hgrn2_chunk(st, qs, fls, ivs, gls, l0, l1, l2, ng):
    nsub = len(qs)
    mx = jnp.maximum(jnp.maximum(l0, l1), l2)
    e0, e1, e2 = jnp.exp(l0 - mx), jnp.exp(l1 - mx), jnp.exp(l2 - mx)
    lb = e0 / (e0 + e1 + e2)
    rows = lax.broadcasted_iota(jnp.int32, (A_SUB, A_SUB), 0)
    cols = lax.broadcasted_iota(jnp.int32, (A_SUB, A_SUB), 1)
    tri = (rows >= cols).astype(F32)
    keep = (lax.broadcasted_iota(jnp.int32, (A_SUB, A_SUB, A_DK), 0)
            >= lax.broadcasted_iota(jnp.int32, (A_SUB, A_SUB, A_DK), 1))
    base = jnp.zeros_like(l0)
    bases, gs, ks, qfs = [], [], [], []
    for i in range(nsub):
        f = lb + (1.0 - lb) * jax.nn.sigmoid(fls[i])
        logf = jnp.log(f)
        bases.append(base)
        gs.append(base + jnp.dot(tri, logf, precision=lax.Precision.HIGHEST, preferred_element_type=F32))
        base = base + jnp.sum(logf, axis=0, keepdims=True)
        ks.append(1.0 - f)
        qfs.append(jax.nn.silu(qs[i]))
    g_last = base
    stb = st.astype(BF16)
    outs = []
    for i in range(nsub):
        o = _dot_nt((qfs[i] * jnp.exp(gs[i])).astype(BF16), stb)
        if i > 0:
            qt = (qfs[i] * jnp.exp(gs[i] - bases[i])).astype(BF16)
            kk = jnp.concatenate([ks[j] * jnp.exp(bases[i] - gs[j]) for j in range(i)], axis=0).astype(BF16)
            vv = jnp.concatenate(ivs[:i], axis=0).astype(BF16)
            o = o + _dot(_dot_nt(qt, kk).astype(BF16), vv)
        dec = jnp.exp(jnp.where(keep, gs[i][:, None, :] - gs[i][None, :, :], NEG_BIG))
        s_diag = jnp.sum(qfs[i][:, None, :] * ks[i][None, :, :] * dec, axis=-1)
        o = o + _dot(s_diag.astype(BF16), ivs[i].astype(BF16))
        o = o * lax.rsqrt(jnp.mean(o * o, axis=-1, keepdims=True) + EPS) * ng
        outs.append(o * jax.nn.silu(gls[i]))
    kdec = jnp.concatenate([ks[j] * jnp.exp(g_last - gs[j]) for j in range(nsub)], axis=0).astype(BF16)
    vall = jnp.concatenate(ivs, axis=0).astype(BF16)
    new_st = st * jnp.exp(g_last) + _dot_tn(vall, kdec)
    return new_st, outs


A_MAX_LOG_DECAY = 80.0


def _split3(x):
    hi = x.astype(BF16)
    r1 = x - hi.astype(F32)
    mid = r1.astype(BF16)
    return hi, mid, (r1 - mid.astype(F32)).astype(BF16)


def _tri_matmul(x, transpose):
    n = x.shape[0]
    r = lax.broadcasted_iota(jnp.int32, (n, n), 0)
    c = lax.broadcasted_iota(jnp.int32, (n, n), 1)
    tri = ((r <= c) if transpose else (r >= c)).astype(BF16)
    hi, mid, lo = _split3(x)
    return (_dot(tri, lo) + _dot(tri, mid)) + _dot(tri, hi)


@jax.custom_vjp
def _cumsum_rows(x):
    return _tri_matmul(x, False)


def _cumsum_rows_fwd(x):
    return _tri_matmul(x, False), None


def _cumsum_rows_bwd(_, dy):
    return (_tri_matmul(dy, True),)


_cumsum_rows.defvjp(_cumsum_rows_fwd, _cumsum_rows_bwd)


def _lower_bound(l0, l1, l2):
    mx = jnp.maximum(jnp.maximum(l0, l1), l2)
    e0, e1, e2 = jnp.exp(l0 - mx), jnp.exp(l1 - mx), jnp.exp(l2 - mx)
    return e0 / (e0 + e1 + e2)


def _b(x):
    return x.astype(BF16)


@jax.custom_vjp
def _mm(a, b):
    return _dot(_b(a), _b(b))


_mm.defvjp(lambda a, b: (_mm(a, b), (a, b)),
           lambda res, d: (_dot_nt(_b(d), _b(res[1])), _dot_tn(_b(res[0]), _b(d))))


@jax.custom_vjp
def _mm_nt(a, b):
    return _dot_nt(_b(a), _b(b))


_mm_nt.defvjp(lambda a, b: (_mm_nt(a, b), (a, b)),
              lambda res, d: (_dot(_b(d), _b(res[1])), _dot_tn(_b(d), _b(res[0]))))


def _dot_split(dot, a, b):
    ah, bh = _b(a), _b(b)
    al, bl = _b(a - ah.astype(F32)), _b(b - bh.astype(F32))
    return (dot(ah, bl) + dot(al, bh)) + dot(ah, bh)


@jax.custom_vjp
def _mm_scores(a, b):
    return _dot_nt(_b(a), _b(b))


_mm_scores.defvjp(lambda a, b: (_mm_scores(a, b), (a, b)),
                  lambda res, d: (_dot_split(_dot, d, res[1]), _dot_split(_dot_tn, d, res[0])))


@jax.custom_vjp
def _mm_tn(a, b):
    return _dot_tn(_b(a), _b(b))


_mm_tn.defvjp(lambda a, b: (_mm_tn(a, b), (a, b)),
              lambda res, d: (_dot_nt(_b(res[1]), _b(d)), _dot(_b(res[0]), _b(d))))


@jax.custom_vjp
def _split_heads(x):
    return tuple(x[:, h * A_DK:(h + 1) * A_DK] for h in range(A_HEADS))


def _split_heads_fwd(x):
    return _split_heads(x), None


def _split_heads_bwd(_, parts):
    return (jnp.concatenate(parts, axis=1),)


_split_heads.defvjp(_split_heads_fwd, _split_heads_bwd)


def _hgrn2_chunk_fast(sts, q, fl, iv, gl, l0, l1, l2, ng):
    lb = _lower_bound(l0, l1, l2)
    f = lb + (1.0 - lb) * jax.nn.sigmoid(fl)
    return _hgrn2_fast_core(sts, q, f, jnp.log(f), iv, gl, ng)


def _hgrn2_fast_core(sts, q, f, logf, iv, gl, ng):
    g = _cumsum_rows(logf)
    g_last = jnp.sum(logf, axis=0, keepdims=True)
    k = 1.0 - f
    qgs = _split_heads(jax.nn.silu(q) * jnp.exp(g))
    kgs = _split_heads(k * jnp.exp(-g))
    kds = _split_heads(k * jnp.exp(g_last - g))
    ivs = _split_heads(iv)
    decays = _split_heads(jnp.exp(g_last))
    n = q.shape[0]
    causal = lax.broadcasted_iota(jnp.int32, (n, n), 0) >= lax.broadcasted_iota(jnp.int32, (n, n), 1)
    raw = [_mm_scores(qg, kg) for qg, kg in zip(qgs, kgs)]
    inter = [_mm_nt(qg, st) for qg, st in zip(qgs, sts)]
    scores = [jnp.where(causal, s, 0.0) for s in raw]
    os = [a + _mm(s, v) for a, s, v in zip(inter, scores, ivs)]
    new_sts = [st * d + _mm_tn(v, kd) for st, d, v, kd in zip(sts, decays, ivs, kds)]
    os = [o * lax.rsqrt(jnp.mean(o * o, axis=-1, keepdims=True) + EPS) for o in os]
    return new_sts, jnp.concatenate(os, axis=1) * ng * jax.nn.silu(gl)


A_STEP_CHUNKS = 4


def _chunk_rows(j):
    return pl.ds(pl.multiple_of(j * A_CHUNK, A_CHUNK), A_CHUNK)


def _sub_rows(j, i):
    return pl.ds(pl.multiple_of(j * A_CHUNK + i * A_SUB, A_SUB), A_SUB)


def _sub_blocks(ref, head, j):
    lanes = slice(head * A_DK, (head + 1) * A_DK)
    return [ref[_sub_rows(j, i), lanes] for i in range(A_CHUNK // A_SUB)]


def hgrn2_fwd(proj, lb_table, a_norm, batch, name, exchange=None):
    t = proj.shape[0]
    n_steps = t // batch // (A_CHUNK * A_STEP_CHUNKS)
    rows = A_CHUNK * A_STEP_CHUNKS

    def body(q_ref, f_ref, i_ref, g_ref, lb_ref, ng_ref, o_ref, st_ref, dec_ref, st):
        @pl.when(pl.program_id(1) == 0)
        def _():
            st[...] = jnp.zeros_like(st)

        def chunk(j, carry):
            r = _chunk_rows(j)
            st_ref[j] = st[...]
            lb = _lower_bound(lb_ref[0:1, :], lb_ref[1:2, :], lb_ref[2:3, :])
            f = lb + (1.0 - lb) * jax.nn.sigmoid(f_ref[r, :])
            logf = jnp.log(f)
            decay = jnp.sum(logf, axis=0, keepdims=True)
            dec_ref[j] = decay
            mild = jnp.min(decay) >= -A_MAX_LOG_DECAY

            @pl.when(mild)
            def _():
                new_sts, o = _hgrn2_fast_core([st[h] for h in range(A_HEADS)], q_ref[r, :], f, logf,
                                              i_ref[r, :], g_ref[r, :], ng_ref[...])
                for h in range(A_HEADS):
                    st[h] = new_sts[h]
                o_ref[r, :] = o.astype(BF16)

            @pl.when(jnp.logical_not(mild))
            def _():
                for h in range(A_HEADS):
                    lanes = slice(h * A_DK, (h + 1) * A_DK)
                    new_st, outs = _hgrn2_chunk(
                        st[h], _sub_blocks(q_ref, h, j), _sub_blocks(f_ref, h, j), _sub_blocks(i_ref, h, j),
                        _sub_blocks(g_ref, h, j), lb_ref[0:1, lanes], lb_ref[1:2, lanes], lb_ref[2:3, lanes],
                        ng_ref[:, lanes])
                    st[h] = new_st
                    for i, o in enumerate(outs):
                        o_ref[_sub_rows(j, i), lanes] = o.astype(BF16)

            return carry

        lax.fori_loop(0, A_STEP_CHUNKS, chunk, 0)

    def part(k):
        return pl.BlockSpec((rows, A_WIDTH), lambda b, n: (b * n_steps + n, k))

    return _call(
        body, name=name, grid=(batch, n_steps),
        in_specs=[part(0), part(1), part(2), part(3),
                  pl.BlockSpec((3, A_WIDTH), lambda b, n: (0, 0)), pl.BlockSpec((1, A_WIDTH), lambda b, n: (0, 0))],
        out_specs=[part(0),
                   pl.BlockSpec((A_STEP_CHUNKS, A_HEADS, A_DK, A_DK), lambda b, n: (b * n_steps + n, 0, 0, 0)),
                   pl.BlockSpec((A_STEP_CHUNKS, 1, A_WIDTH), lambda b, n: (b * n_steps + n, 0, 0))],
        out_shape=[jax.ShapeDtypeStruct((t, A_WIDTH), BF16),
                   jax.ShapeDtypeStruct((t // A_CHUNK, A_HEADS, A_DK, A_DK), F32),
                   jax.ShapeDtypeStruct((t // A_CHUNK, 1, A_WIDTH), F32)],
        scratch_shapes=[pltpu.VMEM((A_HEADS, A_DK, A_DK), F32)],
        args=(proj, proj, proj, proj, lb_table, a_norm), exchange=exchange)


def hgrn2_bwd(proj, states, decays, lb_table, a_norm, do, batch, name, exchange=None):
    t = proj.shape[0]
    n_steps = t // batch // (A_CHUNK * A_STEP_CHUNKS)
    rows = A_CHUNK * A_STEP_CHUNKS

    def body(q_ref, f_ref, i_ref, g_ref, st_ref, dec_ref, lb_ref, ng_ref, do_ref, dp_ref, dlb_ref, dng_ref, dst):
        @pl.when(jnp.logical_and(pl.program_id(0) == 0, pl.program_id(1) == 0))
        def _():
            dlb_ref[...] = jnp.zeros_like(dlb_ref)
            dng_ref[...] = jnp.zeros_like(dng_ref)

        @pl.when(pl.program_id(1) == 0)
        def _():
            dst[...] = jnp.zeros_like(dst)

        def chunk(jj, carry):
            j = A_STEP_CHUNKS - 1 - jj
            r = _chunk_rows(j)
            mild = jnp.min(dec_ref[j]) >= -A_MAX_LOG_DECAY

            @pl.when(mild)
            def _():
                _, vjp = jax.vjp(
                    _hgrn2_chunk_fast, [st_ref[j, h] for h in range(A_HEADS)], q_ref[r, :], f_ref[r, :],
                    i_ref[r, :], g_ref[r, :], lb_ref[0:1, :], lb_ref[1:2, :], lb_ref[2:3, :], ng_ref[...])
                d_sts, dq, df, di, dg, dl0, dl1, dl2, dng = vjp(
                    ([dst[h] for h in range(A_HEADS)], do_ref[r, :].astype(F32)))
                for h in range(A_HEADS):
                    dst[h] = d_sts[h]
                for k, part in enumerate((dq, df, di, dg)):
                    dp_ref[r, k * A_WIDTH:(k + 1) * A_WIDTH] = part
                for row, val in enumerate((dl0, dl1, dl2)):
                    dlb_ref[row:row + 1, :] += val
                dng_ref[...] += dng

            @pl.when(jnp.logical_not(mild))
            def _():
                for h in range(A_HEADS):
                    lanes = slice(h * A_DK, (h + 1) * A_DK)
                    _, vjp = jax.vjp(
                        _hgrn2_chunk, st_ref[j, h], _sub_blocks(q_ref, h, j), _sub_blocks(f_ref, h, j),
                        _sub_blocks(i_ref, h, j), _sub_blocks(g_ref, h, j), lb_ref[0:1, lanes], lb_ref[1:2, lanes],
                        lb_ref[2:3, lanes], ng_ref[:, lanes])
                    douts = [x.astype(F32) for x in _sub_blocks(do_ref, h, j)]
                    d_st, dqs, dfs, dis, dgs, dl0, dl1, dl2, dng = vjp((dst[h], douts))
                    dst[h] = d_st
                    for k, parts in enumerate((dqs, dfs, dis, dgs)):
                        for i in range(A_CHUNK // A_SUB):
                            dp_ref[_sub_rows(j, i), k * A_WIDTH + h * A_DK:k * A_WIDTH + (h + 1) * A_DK] = parts[i]
                    for row, val in enumerate((dl0, dl1, dl2)):
                        dlb_ref[row:row + 1, lanes] += val
                    dng_ref[:, lanes] += dng

            return carry

        lax.fori_loop(0, A_STEP_CHUNKS, chunk, 0)

    def rev(b, n):
        return b * n_steps + (n_steps - 1 - n)

    def part(k):
        return pl.BlockSpec((rows, A_WIDTH), lambda b, n: (rev(b, n), k))

    const3 = pl.BlockSpec((3, A_WIDTH), lambda b, n: (0, 0))
    const1 = pl.BlockSpec((1, A_WIDTH), lambda b, n: (0, 0))
    return _call(
        body, name=name, grid=(batch, n_steps),
        in_specs=[part(0), part(1), part(2), part(3),
                  pl.BlockSpec((A_STEP_CHUNKS, A_HEADS, A_DK, A_DK), lambda b, n: (rev(b, n), 0, 0, 0)),
                  pl.BlockSpec((A_STEP_CHUNKS, 1, A_WIDTH), lambda b, n: (rev(b, n), 0, 0)),
                  const3, const1, part(0)],
        out_specs=[pl.BlockSpec((rows, 4 * A_WIDTH), lambda b, n: (rev(b, n), 0)), const3, const1],
        out_shape=[jax.ShapeDtypeStruct((t, 4 * A_WIDTH + 2 * B_WIDTH), F32),
                   jax.ShapeDtypeStruct((3, A_WIDTH), F32), jax.ShapeDtypeStruct((1, A_WIDTH), F32)],
        scratch_shapes=[pltpu.VMEM((A_HEADS, A_DK, A_DK), F32)],
        args=(proj, proj, proj, proj, states, decays, lb_table, a_norm, do), exchange=exchange)


B_GDIM = B_WIDTH // B_GROUPS
B_ROWS = 512


def _gmlp_chunk(ubs, vbs, lngs, lnbs, ws, bcols):
    vs = [jax.nn.gelu(v) for v in vbs]
    mu = sum(jnp.sum(v, axis=-1, keepdims=True) for v in vs) * (1.0 / B_WIDTH)
    var = sum(jnp.sum(jnp.square(v - mu), axis=-1, keepdims=True) for v in vs) * (1.0 / B_WIDTH)
    rstd = lax.rsqrt(var + EPS)
    tril = (lax.broadcasted_iota(jnp.int32, (B_CHUNK, B_CHUNK), 0)
            >= lax.broadcasted_iota(jnp.int32, (B_CHUNK, B_CHUNK), 1))
    outs = []
    for g in range(B_GROUPS):
        vn = (vs[g] - mu) * rstd * lngs[g] + lnbs[g]
        w = jnp.where(tril, ws[g], 0.0).astype(BF16)
        outs.append(jax.nn.gelu(ubs[g]) * (_dot(w, vn.astype(BF16)) + bcols[g]))
    return outs


def _gmlp_args(u_ref, v_ref, lng_ref, lnb_ref, w_ref, bt_ref, rows):
    def groups(ref):
        return [ref[rows, g * B_GDIM:(g + 1) * B_GDIM] for g in range(B_GROUPS)]

    def vec(ref):
        return [ref[:, g * B_GDIM:(g + 1) * B_GDIM] for g in range(B_GROUPS)]

    return (groups(u_ref), groups(v_ref), vec(lng_ref), vec(lnb_ref),
            [w_ref[g] for g in range(B_GROUPS)], [bt_ref[:, g:g + 1] for g in range(B_GROUPS)])


def gmlp_fwd(proj, oa, ln_g, ln_b, w, bias_t, name, exchange=None):
    t = proj.shape[0]

    def body(u_ref, v_ref, oa_ref, lng_ref, lnb_ref, w_ref, bt_ref, o_ref):
        o_ref[:, 0:A_WIDTH] = oa_ref[...]
        for n in range(B_ROWS // B_CHUNK):
            rows = slice(n * B_CHUNK, (n + 1) * B_CHUNK)
            outs = _gmlp_chunk(*_gmlp_args(u_ref, v_ref, lng_ref, lnb_ref, w_ref, bt_ref, rows))
            for g, o in enumerate(outs):
                o_ref[rows, A_WIDTH + g * B_GDIM:A_WIDTH + (g + 1) * B_GDIM] = o.astype(BF16)

    vec = pl.BlockSpec((1, B_WIDTH), lambda i: (0, 0))
    return _call(
        body, name=name, grid=(t // B_ROWS,),
        in_specs=[pl.BlockSpec((B_ROWS, B_WIDTH), lambda i: (i, 4)), pl.BlockSpec((B_ROWS, B_WIDTH), lambda i: (i, 5)),
                  pl.BlockSpec((B_ROWS, A_WIDTH), lambda i: (i, 0)), vec, vec,
                  pl.BlockSpec((B_GROUPS, B_CHUNK, B_CHUNK), lambda i: (0, 0, 0)),
                  pl.BlockSpec((B_CHUNK, B_GROUPS), lambda i: (0, 0))],
        out_specs=[pl.BlockSpec((B_ROWS, A_WIDTH + B_WIDTH), lambda i: (i, 0))],
        out_shape=[jax.ShapeDtypeStruct((t, A_WIDTH + B_WIDTH), BF16)],
        args=(proj, proj, oa, ln_g, ln_b, w, bias_t), exchange=exchange)


def gmlp_bwd(proj, dmixin, ln_g, ln_b, w, bias_t, dproj, name, exchange=None):
    t = proj.shape[0]

    def body(u_ref, v_ref, do_ref, lng_ref, lnb_ref, w_ref, bt_ref, dp_in_ref,
             dp_ref, dlng_ref, dlnb_ref, dw_ref, dbt_ref):
        del dp_in_ref

        @pl.when(pl.program_id(0) == 0)
        def _():
            for ref in (dlng_ref, dlnb_ref, dw_ref, dbt_ref):
                ref[...] = jnp.zeros_like(ref)

        for n in range(B_ROWS // B_CHUNK):
            rows = slice(n * B_CHUNK, (n + 1) * B_CHUNK)
            _, vjp = jax.vjp(_gmlp_chunk, *_gmlp_args(u_ref, v_ref, lng_ref, lnb_ref, w_ref, bt_ref, rows))
            douts = [do_ref[rows, g * B_GDIM:(g + 1) * B_GDIM] for g in range(B_GROUPS)]
            dus, dvs, dlngs, dlnbs, dws, dbs = vjp(douts)
            for g in range(B_GROUPS):
                lanes = slice(g * B_GDIM, (g + 1) * B_GDIM)
                dp_ref[rows, lanes] = dus[g]
                dp_ref[rows, B_WIDTH + g * B_GDIM:B_WIDTH + (g + 1) * B_GDIM] = dvs[g]
                dlng_ref[:, lanes] += dlngs[g]
                dlnb_ref[:, lanes] += dlnbs[g]
                dw_ref[g] += dws[g]
                dbt_ref[:, g:g + 1] += dbs[g]

    vec = pl.BlockSpec((1, B_WIDTH), lambda i: (0, 0))
    wspec = pl.BlockSpec((B_GROUPS, B_CHUNK, B_CHUNK), lambda i: (0, 0, 0))
    bspec = pl.BlockSpec((B_CHUNK, B_GROUPS), lambda i: (0, 0))
    return _call(
        body, name=name, grid=(t // B_ROWS,),
        in_specs=[pl.BlockSpec((B_ROWS, B_WIDTH), lambda i: (i, 4)), pl.BlockSpec((B_ROWS, B_WIDTH), lambda i: (i, 5)),
                  pl.BlockSpec((B_ROWS, B_WIDTH), lambda i: (i, 1)), vec, vec, wspec, bspec,
                  pl.BlockSpec(memory_space=pl.ANY)],
        out_specs=[pl.BlockSpec((B_ROWS, 2 * B_WIDTH), lambda i: (i, 2)), vec, vec, wspec, bspec],
        out_shape=[jax.ShapeDtypeStruct(dproj.shape, F32), jax.ShapeDtypeStruct((1, B_WIDTH), F32),
                   jax.ShapeDtypeStruct((1, B_WIDTH), F32), jax.ShapeDtypeStruct((B_GROUPS, B_CHUNK, B_CHUNK), F32),
                   jax.ShapeDtypeStruct((B_CHUNK, B_GROUPS), F32)],
        aliases={7: 0}, args=(proj, proj, dmixin, ln_g, ln_b, w, bias_t, dproj), exchange=exchange)


C_FWD_BLOCKS = 8
C_BWD_BLOCKS = 4
C_PAIR = 2 * C_HEAD_DIM
C_PAIRS = C_HEADS // 2
C_SCALE = 1.0 / math.sqrt(C_HEAD_DIM)
C_ROT_DIM = 2 * C_ROT_HALF
ROPE_ROWS = 1024


def rope_tables(pos_col, name):
    t = pos_col.shape[0]

    def body(p_ref, c_ref, a_ref, b_ref):
        lane = jnp.bitwise_and(lax.broadcasted_iota(jnp.int32, (1, C_PAIR), 1), C_HEAD_DIM - 1)
        j = jnp.bitwise_and(lane, C_ROT_HALF - 1).astype(F32)
        inv = jnp.exp(j * (-math.log(ROPE_THETA) / C_ROT_HALF))
        ang = p_ref[...].astype(F32) * inv
        cos, sin = jnp.cos(ang), jnp.sin(ang)
        c_ref[...] = jnp.where(lane < C_ROT_DIM, cos, 1.0)
        a_ref[...] = jnp.where(lane < C_ROT_HALF, -sin, 0.0)
        b_ref[...] = jnp.where(jnp.logical_and(lane >= C_ROT_HALF, lane < C_ROT_DIM), sin, 0.0)

    tab = pl.BlockSpec((ROPE_ROWS, C_PAIR), lambda i: (i, 0))
    return pl.pallas_call(
        body, name=name, grid=(t // ROPE_ROWS,),
        in_specs=[pl.BlockSpec((ROPE_ROWS, 1), lambda i: (i, 0))],
        out_specs=[tab, tab, tab],
        out_shape=[jax.ShapeDtypeStruct((t, C_PAIR), F32)] * 3,
        compiler_params=_params(("arbitrary",)),
    )(pos_col)


def _rope(x, c, a, b):
    return x * c + pltpu.roll(x, C_PAIR - C_ROT_HALF, 1) * a + pltpu.roll(x, C_ROT_HALF, 1) * b


def _rope_t(d, c, a, b):
    return d * c + pltpu.roll(d * a, C_ROT_HALF, 1) + pltpu.roll(d * b, C_PAIR - C_ROT_HALF, 1)


def _attn_rows(idx, dil):
    nblk = SEQ // dil // C_BLOCK
    r, n = idx // nblk, idx % nblk
    start = r + dil * C_BLOCK * n
    prev = r + dil * C_BLOCK * jnp.maximum(n - 1, 0)
    if dil == 1:
        return pl.ds(pl.multiple_of(start, C_BLOCK), C_BLOCK), pl.ds(pl.multiple_of(prev, C_BLOCK), C_BLOCK), n > 0
    return pl.ds(start, C_BLOCK, stride=dil), pl.ds(prev, C_BLOCK, stride=dil), n > 0


def _head_masks():
    low = lax.broadcasted_iota(jnp.int32, (1, C_PAIR), 1) < C_HEAD_DIM
    return low, jnp.logical_not(low)


def _attn_mask(has_prev):
    i = jnp.bitwise_and(lax.broadcasted_iota(jnp.int32, (2 * C_BLOCK, 2 * C_BLOCK), 0), C_BLOCK - 1)
    j = lax.broadcasted_iota(jnp.int32, (2 * C_BLOCK, 2 * C_BLOCK), 1)
    return jnp.logical_or(j <= i, jnp.logical_and(j - C_BLOCK >= i, has_prev))


def _stack_heads(x):
    low, high = _head_masks()
    return jnp.concatenate([jnp.where(low, x, 0.0), jnp.where(high, x, 0.0)], axis=0)


def _unstack_heads(x):
    low, _ = _head_masks()
    return jnp.where(low, x[:C_BLOCK], x[C_BLOCK:])


def attn_fwd(qkv, cos_t, sin_a, sin_b, batch, name, exchange=None):
    t = qkv.shape[0]
    nbr = len(C_DILATIONS)

    def body(q_ref, k_ref, v_ref, c_ref, a_ref, b_ref, o_ref, l_ref, qs, ks, *stats):
        acc, mm, dd = stats[0:nbr], stats[nbr:2 * nbr], stats[2 * nbr:3 * nbr]
        c, a, b = c_ref[...], a_ref[...], b_ref[...]
        qs[...] = _rope(q_ref[...], c, a, b) * C_SCALE
        ks[...] = _rope(k_ref[...], c, a, b)
        def load(idx, dil):
            rows, prev, has_prev = _attn_rows(idx, dil)
            return rows, (has_prev, qs[rows, :], ks[rows, :], ks[prev, :], v_ref[rows, :], v_ref[prev, :])

        def scores(has_prev, q, k_own, k_prev, v_own, v_prev):
            k_cat = jnp.concatenate([k_own, k_prev], axis=0).astype(BF16)
            return jnp.where(_attn_mask(has_prev), _dot_nt(_stack_heads(q).astype(BF16), k_cat), NEG_BIG)

        def softmax(s):
            m = jnp.max(s, axis=-1, keepdims=True)
            p = jnp.exp(s - m)
            return p.astype(BF16), m, jnp.sum(p, axis=-1, keepdims=True)

        def values(pb, has_prev, q, k_own, k_prev, v_own, v_prev):
            low, high = _head_masks()
            v_cat = jnp.concatenate([v_own, v_prev], axis=0)
            p_wide = jnp.concatenate([pb[:C_BLOCK], pb[C_BLOCK:]], axis=1)
            v_tall = jnp.concatenate([jnp.where(low, v_cat, 0.0), jnp.where(high, v_cat, 0.0)], axis=0).astype(BF16)
            return _dot(p_wide, v_tall)

        for bi, dil in enumerate(C_DILATIONS):
            def pair(i, carry, bi=bi, dil=dil):
                low, _ = _head_masks()
                loaded = [load(C_FWD_BLOCKS * i + k, dil) for k in range(C_FWD_BLOCKS)]
                ss = [scores(*ops) for _, ops in loaded]
                sm = [softmax(s) for s in ss]
                pvs = [values(pb, *ops) for (pb, _, _), (_, ops) in zip(sm, loaded)]
                for (rows, _), (_, m, den), pv in zip(loaded, sm, pvs):
                    acc[bi][rows, :] = pv
                    mm[bi][rows, :] = jnp.where(low, m[:C_BLOCK], m[C_BLOCK:])
                    dd[bi][rows, :] = jnp.where(low, den[:C_BLOCK], den[C_BLOCK:])
                return carry

            lax.fori_loop(0, SEQ // C_BLOCK // C_FWD_BLOCKS, pair, 0)
        step = 256
        for r0 in range(0, SEQ, step):
            rr = slice(r0, r0 + step)
            ms = [mm[g][rr, :] for g in range(nbr)]
            m_all = functools.reduce(jnp.maximum, ms)
            ws = [jnp.exp(m - m_all) for m in ms]
            num = sum(acc[g][rr, :] * ws[g] for g in range(nbr))
            den = sum(dd[g][rr, :] * ws[g] for g in range(nbr))
            o_ref[rr, :] = (num / den).astype(BF16)
            l_ref[rr, :] = m_all + jnp.log(den)

    def col(k):
        return pl.BlockSpec((SEQ, C_PAIR), lambda b, p: (b, k * C_PAIRS + p))

    tab = pl.BlockSpec((SEQ, C_PAIR), lambda b, p: (b, 0))
    return _call(
        body, name=name, grid=(batch, C_PAIRS),
        in_specs=[col(0), col(1), col(2), tab, tab, tab],
        out_specs=[col(0), col(0)],
        out_shape=[jax.ShapeDtypeStruct((t, D_MODEL), BF16), jax.ShapeDtypeStruct((t, D_MODEL), F32)],
        scratch_shapes=[pltpu.VMEM((SEQ, C_PAIR), F32)] * (2 + 3 * nbr),
        args=(qkv, qkv, qkv, cos_t, sin_a, sin_b), exchange=exchange)


def attn_bwd(qkv, cos_t, sin_a, sin_b, o, lse, do, batch, name, exchange=None):
    t = qkv.shape[0]

    def body(q_ref, k_ref, v_ref, c_ref, a_ref, b_ref, o_ref, l_ref, do_ref, dq_ref, dk_ref, dv_ref,
             qs, ks, dqs, dks, dvs, dlt):
        c, a, b = c_ref[...], a_ref[...], b_ref[...]
        qs[...] = _rope(q_ref[...], c, a, b) * C_SCALE
        ks[...] = _rope(k_ref[...], c, a, b)
        prod = do_ref[...] * o_ref[...].astype(F32)
        low = lax.broadcasted_iota(jnp.int32, (1, C_PAIR), 1) < C_HEAD_DIM
        s_low = jnp.sum(jnp.where(low, prod, 0.0), axis=-1, keepdims=True)
        s_all = jnp.sum(prod, axis=-1, keepdims=True)
        dlt[...] = jnp.where(low, s_low, s_all - s_low)
        dqs[...] = jnp.zeros_like(dqs)
        dks[...] = jnp.zeros_like(dks)
        dvs[...] = jnp.zeros_like(dvs)
        def load(idx, dil):
            rows, prev, has_prev = _attn_rows(idx, dil)
            return (rows, prev), (has_prev, qs[rows, :], do_ref[rows, :], ks[rows, :], ks[prev, :],
                                  v_ref[rows, :], v_ref[prev, :], l_ref[rows, :], dlt[rows, :])

        def operands(has_prev, q, do, k_own, k_prev, v_own, v_prev, l_full, d_full):
            lcol = jnp.concatenate([l_full[:, 0:1], l_full[:, C_HEAD_DIM:C_HEAD_DIM + 1]], axis=0)
            dcol = jnp.concatenate([d_full[:, 0:1], d_full[:, C_HEAD_DIM:C_HEAD_DIM + 1]], axis=0)
            return (_stack_heads(q).astype(BF16), _stack_heads(do).astype(BF16),
                    jnp.concatenate([k_own, k_prev], axis=0).astype(BF16),
                    jnp.concatenate([v_own, v_prev], axis=0).astype(BF16), lcol, dcol, _attn_mask(has_prev))

        for dil in C_DILATIONS:
            def pair(i, carry, dil=dil):
                loaded = [load(C_BWD_BLOCKS * i + k, dil) for k in range(C_BWD_BLOCKS)]
                ops = [operands(*o) for _, o in loaded]
                ss = [_dot_nt(q_stack, k_cat) for q_stack, _, k_cat, _, _, _, _ in ops]
                dps = [_dot_nt(do_stack, v_cat) for _, do_stack, _, v_cat, _, _, _ in ops]
                ps = [jnp.exp(jnp.where(o[6], s, NEG_BIG) - o[4]) for s, o in zip(ss, ops)]
                dss = [(p * (dp - o[5])).astype(BF16) for p, dp, o in zip(ps, dps, ops)]
                dvs_ = [_dot_tn(p.astype(BF16), o[1]) for p, o in zip(ps, ops)]
                dks_ = [_dot_tn(ds, o[0]) for ds, o in zip(dss, ops)]
                dqs_ = [_unstack_heads(_dot(ds, o[2])) for ds, o in zip(dss, ops)]
                results = list(zip(dqs_, dks_, dvs_))
                for ((rows, prev), _), (dq, dk_cat, dv_cat) in zip(loaded, results):
                    dqs[rows, :] += dq
                    dks[rows, :] += dk_cat[:C_BLOCK]
                    dvs[rows, :] += dv_cat[:C_BLOCK]
                    dks[prev, :] += dk_cat[C_BLOCK:]
                    dvs[prev, :] += dv_cat[C_BLOCK:]
                return carry

            lax.fori_loop(0, SEQ // C_BLOCK // C_BWD_BLOCKS, pair, 0)
        dq_ref[...] = _rope_t(dqs[...] * C_SCALE, c, a, b).astype(BF16)
        dk_ref[...] = _rope_t(dks[...], c, a, b).astype(BF16)
        dv_ref[...] = dvs[...].astype(BF16)

    def col(k):
        return pl.BlockSpec((SEQ, C_PAIR), lambda b, p: (b, k * C_PAIRS + p))

    tab = pl.BlockSpec((SEQ, C_PAIR), lambda b, p: (b, 0))
    out = jax.ShapeDtypeStruct((t, D_MODEL), BF16)
    return _call(
        body, name=name, grid=(batch, C_PAIRS),
        in_specs=[col(0), col(1), col(2), tab, tab, tab, col(0), col(0), col(0)],
        out_specs=[col(0), col(0), col(0)],
        out_shape=[out, out, out],
        scratch_shapes=[pltpu.VMEM((SEQ, C_PAIR), F32)] * 6,
        args=(qkv, qkv, qkv, cos_t, sin_a, sin_b, o, lse, do), exchange=exchange)


def sibling_swap(arrays, name):
    n = len(arrays)

    def body(*refs):
        ins, outs = refs[:n], refs[n:2 * n]
        send_sems, recv_sems = refs[2 * n:]
        x, y, c, _ = _place()
        sends = []
        for a in range(n):
            cp = pltpu.make_async_remote_copy(
                src_ref=ins[a], dst_ref=outs[a], send_sem=send_sems.at[a], recv_sem=recv_sems.at[a],
                device_id=(x, y, 1 - c), device_id_type=MESH)
            cp.start()
            sends.append(cp)
        for cp in sends:
            cp.wait_recv()
        for cp in sends:
            cp.wait_send()

    return pl.pallas_call(
        body, name=name,
        in_specs=[ANY] * n, out_specs=[ANY] * n,
        out_shape=[jax.ShapeDtypeStruct(s.shape, s.dtype) for s in arrays],
        scratch_shapes=[pltpu.SemaphoreType.DMA((n,)), pltpu.SemaphoreType.DMA((n,))],
    )(*arrays)


def allreduce_small(slab, name):
    rows, lanes = slab.shape

    def body(x_ref, out_ref, gath, send_sems, recv_sems, local_sem):
        x, y, c, chips = _place()
        me, sibling = (x, y, c), (x, y, 1 - c)

        def slot(px, py, pc):
            return gath.at[4 * px + 2 * py + pc]

        def copy(k, block, to, src=None):
            return pltpu.make_async_remote_copy(
                src_ref=slot(*block) if src is None else src, dst_ref=slot(*block),
                send_sem=send_sems.at[k], recv_sem=recv_sems.at[k], device_id=to, device_id_type=MESH)

        mine = pltpu.make_async_copy(x_ref, slot(*me), local_sem)
        mine.start()
        first = [copy(0, me, sibling, src=x_ref)]
        first += [copy(1 + j, me, (*chip, c), src=x_ref) for j, chip in enumerate(chips)]
        for cp in first:
            cp.start()
        passed = [copy(4 + j, (*chip, c), sibling) for j, chip in enumerate(chips)]
        for j, chip in enumerate(chips):
            copy(1 + j, (*chip, c), me).wait_recv()
            passed[j].start()
        copy(0, sibling, me).wait_recv()
        for j, chip in enumerate(chips):
            copy(4 + j, (*chip, 1 - c), me).wait_recv()
        for cp in first + passed:
            cp.wait_send()
        mine.wait()
        total = gath[0]
        for d in range(1, N_DEV):
            total = total + gath[d]
        out_ref[...] = total

    return pl.pallas_call(
        body, name=name,
        in_specs=[pl.BlockSpec(memory_space=pltpu.VMEM)],
        out_specs=pl.BlockSpec(memory_space=pltpu.VMEM),
        out_shape=jax.ShapeDtypeStruct((rows, lanes), F32),
        scratch_shapes=[pltpu.VMEM((N_DEV, rows, lanes), F32),
                        pltpu.SemaphoreType.DMA((7,)), pltpu.SemaphoreType.DMA((7,)), pltpu.SemaphoreType.DMA],
    )(slab)


ELT_ROWS = 512


def reduce_slabs(r, name, part=0, parts=1, into=None):
    _, rows, cols = r.shape
    br = min(rows, ELT_ROWS)
    nblk = rows // br

    def body(r_ref, *rest):
        o_ref = rest[-1]
        o_ref[...] = ((r_ref[3].astype(F32) + r_ref[0].astype(F32)) + r_ref[1].astype(F32)) + r_ref[2].astype(F32)

    return pl.pallas_call(
        body, name=name, grid=(nblk,),
        in_specs=[pl.BlockSpec((N_CHIPS, br, cols), lambda i: (0, i, 0))] + ([] if into is None else [ANY]),
        out_specs=pl.BlockSpec((br, cols), lambda i: (part * nblk + i, 0)),
        out_shape=jax.ShapeDtypeStruct((parts * rows, cols), F32),
        input_output_aliases={} if into is None else {1: 0},
        compiler_params=_params(("arbitrary",)),
    )(*([r] if into is None else [r, into]))


def _adamw(w, g, m, v):
    m = ADAM_B1 * m + (1.0 - ADAM_B1) * g
    v = ADAM_B2 * v + (1.0 - ADAM_B2) * jnp.square(g)
    m_hat = m / (1.0 - ADAM_B1 ** ADAM_STEP)
    v_hat = v / (1.0 - ADAM_B2 ** ADAM_STEP)
    delta = -ADAM_LR * (m_hat / (jnp.sqrt(v_hat) + ADAM_EPS) + ADAM_WD * w)
    return delta, m, v


def adamw_big(w, s_mine, s_sibling, m, v, name):
    rows, cols = w.shape

    def body(w_ref, a_ref, b_ref, m_ref, v_ref, g_out, d_out, m_out, v_out):
        g = a_ref[...] + b_ref[...]
        g_out[...] = g
        d_out[...], m_out[...], v_out[...] = _adamw(w_ref[...], g, m_ref[...], v_ref[...])

    blk = pl.BlockSpec((min(rows, ELT_ROWS), cols), lambda i: (i, 0))
    out = jax.ShapeDtypeStruct((rows, cols), F32)
    return pl.pallas_call(
        body, name=name, grid=(rows // min(rows, ELT_ROWS),),
        in_specs=[blk] * 5, out_specs=[blk] * 4, out_shape=[out] * 4,
        compiler_params=_params(("arbitrary",)),
    )(w, s_mine, s_sibling, m, v)


def adamw_small(ws, gs, ms, vs, name):
    n = len(ws)

    def body(*refs):
        w_refs, g_refs, m_refs, v_refs = (refs[k * n:(k + 1) * n] for k in range(4))
        d_out, m_out, v_out = (refs[(4 + k) * n:(5 + k) * n] for k in range(3))
        for i in range(n):
            d_out[i][...], m_out[i][...], v_out[i][...] = _adamw(
                w_refs[i][...], g_refs[i][...], m_refs[i][...], v_refs[i][...])

    outs = [jax.ShapeDtypeStruct(w.shape, F32) for w in ws]
    res = pl.pallas_call(body, name=name, out_shape=outs * 3)(*ws, *gs, *ms, *vs)
    return res[:n], res[n:2 * n], res[2 * n:]


SLAB_LANES = 128
SLAB_ROW_ALIGN = 8


def _pack(parts):
    flat = jnp.concatenate([p.reshape(-1) for p in parts])
    rows = -(-flat.shape[0] // (SLAB_LANES * SLAB_ROW_ALIGN)) * SLAB_ROW_ALIGN
    flat = jnp.pad(flat, (0, rows * SLAB_LANES - flat.shape[0]))
    return flat.reshape(rows, SLAB_LANES)


def _unpack(slab, shapes):
    flat = slab.reshape(-1)
    out, pos = [], 0
    for s in shapes:
        size = math.prod(s)
        out.append(flat[pos:pos + size].reshape(s))
        pos += size
    return out


def kernel(x, positions, norm_mix_pre, norm_mix_post, norm_ffn_pre, norm_ffn_post, w_in_even, lb_table, a_norm, b_ln_g, b_ln_b, b_ws, b_bias, w_out_even, w_in_odd, w_out_odd, w_ff1, w_ff2, loss_target, m_norm_mix_pre, m_norm_mix_post, m_norm_ffn_pre, m_norm_ffn_post, m_w_in_even, m_lb_table, m_a_norm, m_b_ln_g, m_b_ln_b, m_b_ws, m_b_bias, m_w_out_even, m_w_in_odd, m_w_out_odd, m_w_ff1, m_w_ff2, v_norm_mix_pre, v_norm_mix_post, v_norm_ffn_pre, v_norm_ffn_post, v_w_in_even, v_lb_table, v_a_norm, v_b_ln_g, v_b_ln_b, v_b_ws, v_b_bias, v_w_out_even, v_w_in_odd, v_w_out_odd, v_w_ff1, v_w_ff2):
    batch = x.shape[0]
    t = batch * SEQ
    d = D_MODEL
    x0 = x.reshape(t, d)
    target = loss_target.reshape(t, d)

    def gain(p, layer):
        return p[layer:layer + 1]

    def gather(*shards):
        return _Exchange("gather", [w.astype(BF16) for w in shards])

    def scatter(*grads):
        return _Exchange("scatter", grads)

    (win_e,) = exchange_alone(gather(w_in_even[0]), "gather_in_even")
    bias_t = b_bias[0].T
    proj, h0, w1_0 = norm_matmul(x0, gain(norm_mix_pre, 0), win_e, "in_proj_even", exchange=gather(w_ff1[0]))
    oa, states, decays, w2_0 = hgrn2_fwd(proj, lb_table, a_norm, batch, "hgrn2_fwd", exchange=gather(w_ff2[0]))
    mixin, wout_e = gmlp_fwd(proj, oa, b_ln_g, b_ln_b, b_ws[0], bias_t, "gmlp_fwd", exchange=gather(w_out_even[0]))
    mix0, x1 = out_proj(mixin, wout_e, x0, gain(norm_mix_post, 0), "out_proj_even")
    x2, hf0, a0, y0, win_o, wout_o = ffn_fwd(x1, gain(norm_ffn_pre, 0), w1_0, w2_0, gain(norm_ffn_post, 0),
                                             "ffn_fwd_0", exchange=gather(w_in_odd[0], w_out_odd[0]))
    qkv, h1 = norm_matmul(x2, gain(norm_mix_pre, 1), win_o, "in_proj_odd")
    cos_t, sin_a, sin_b = rope_tables(positions.reshape(t, 1), "rope_tables")
    ao, lse, w1_1, w2_1 = attn_fwd(qkv, cos_t, sin_a, sin_b, batch, "attn_fwd", exchange=gather(w_ff1[1], w_ff2[1]))
    mix1, x3 = out_proj(ao, wout_o, x2, gain(norm_mix_post, 1), "out_proj_odd")
    dx4, hf1, a1, y1, loss_part = ffn_fwd(x3, gain(norm_ffn_pre, 1), w1_1, w2_1, gain(norm_ffn_post, 1),
                                          "ffn_fwd_1", target=target)

    hc = D_FF // N_CHIPS
    dx3, dy1, da1, dg_fpre1, dg_fpost1 = ffn_bwd(
        dx4, x3, y1, a1, gain(norm_ffn_pre, 1), gain(norm_ffn_post, 1), w1_1, w2_1, "ffn_bwd_1")
    g_w1_1 = weight_grad(hf1, da1, "b", d, hc, False, "wgrad_ff1_1")
    g_w2_1 = weight_grad(a1, dy1, "a", hc, d, True, "wgrad_ff2_1")
    dmix1, dao, dg_mpost1 = out_proj_bwd(dx3, mix1, gain(norm_mix_post, 1), wout_o, "out_proj_bwd_odd")
    g_wout_o = weight_grad(ao, dmix1, "a", d // N_CHIPS, d, False, "wgrad_out_odd")
    dq, dk, dv, r_w1_1, r_w2_1, r_wout_o = attn_bwd(qkv, cos_t, sin_a, sin_b, ao, lse, dao, batch, "attn_bwd",
                                                    exchange=scatter(g_w1_1, g_w2_1, g_wout_o))
    dqkv = jnp.concatenate([dq, dk, dv], axis=1)
    dx2, dg_mpre1 = norm_matmul_bwd(dqkv, win_o, x2, gain(norm_mix_pre, 1), dx3, "in_proj_bwd_odd")
    g_win_o = weight_grad(h1, dqkv, "b", d, 3 * d // N_CHIPS, False, "wgrad_in_odd")
    dx1, dy0, da0, dg_fpre0, dg_fpost0, r_win_o = ffn_bwd(
        dx2, x1, y0, a0, gain(norm_ffn_pre, 0), gain(norm_ffn_post, 0), w1_0, w2_0, "ffn_bwd_0",
        exchange=scatter(g_win_o))
    g_w1_0 = weight_grad(hf0, da0, "b", d, hc, False, "wgrad_ff1_0")
    g_w2_0 = weight_grad(a0, dy0, "a", hc, d, True, "wgrad_ff2_0")
    dmix0, dmixin, dg_mpost0 = out_proj_bwd(dx1, mix0, gain(norm_mix_post, 0), wout_e, "out_proj_bwd_even")
    g_wout_e = weight_grad(mixin, dmix0, "a", d // N_CHIPS, d, False, "wgrad_out_even")
    dproj, d_lb, d_anorm, r_w1_0 = hgrn2_bwd(
        proj, states, decays, lb_table, a_norm, dmixin, batch, "hgrn2_bwd", exchange=scatter(g_w1_0))
    dproj, d_lng, d_lnb, d_ws, d_bias_t, r_w2_0 = gmlp_bwd(
        proj, dmixin, b_ln_g, b_ln_b, b_ws[0], bias_t, dproj, "gmlp_bwd", exchange=scatter(g_w2_0))
    g_win_e, r_wout_e = weight_grad(h0, dproj, "b", d, 3 * d // N_CHIPS, False, "wgrad_in_even",
                                    exchange=scatter(g_wout_e))
    dx0, dg_mpre0, r_win_e = norm_matmul_bwd(dproj, win_e, x0, gain(norm_mix_pre, 0), dx1, "in_proj_bwd_even",
                                             exchange=scatter(g_win_e))
    grad_x = dx0.reshape(x.shape)

    s_w1 = reduce_slabs(r_w1_1, "reduce_ff1_1", part=1, parts=2)
    s_w1 = reduce_slabs(r_w1_0, "reduce_ff1_0", part=0, parts=2, into=s_w1)
    s_w2 = reduce_slabs(r_w2_1, "reduce_ff2_1", part=1, parts=2)
    s_w2 = reduce_slabs(r_w2_0, "reduce_ff2_0", part=0, parts=2, into=s_w2)
    sums = [reduce_slabs(r_win_e, "reduce_in_even"), reduce_slabs(r_wout_e, "reduce_out_even"),
            reduce_slabs(r_win_o, "reduce_in_odd"), reduce_slabs(r_wout_o, "reduce_out_odd"), s_w1, s_w2]
    sibling = sibling_swap(sums, "sibling_swap")
    big_w = [w_in_even, w_out_even, w_in_odd, w_out_odd, w_ff1, w_ff2]
    big_m = [m_w_in_even, m_w_out_even, m_w_in_odd, m_w_out_odd, m_w_ff1, m_w_ff2]
    big_v = [v_w_in_even, v_w_out_even, v_w_in_odd, v_w_out_odd, v_w_ff1, v_w_ff2]
    big = []
    for i, (w, m, v) in enumerate(zip(big_w, big_m, big_v)):
        two_d = (-1, w.shape[-1])
        res = adamw_big(w.reshape(two_d), sums[i], sibling[i], m.reshape(two_d), v.reshape(two_d), "adamw_big_%d" % i)
        big.append([r.reshape(w.shape) for r in res])

    small_w = [norm_mix_pre, norm_mix_post, norm_ffn_pre, norm_ffn_post, lb_table, a_norm, b_ln_g, b_ln_b, b_ws, b_bias]
    small_m = [m_norm_mix_pre, m_norm_mix_post, m_norm_ffn_pre, m_norm_ffn_post, m_lb_table, m_a_norm, m_b_ln_g,
               m_b_ln_b, m_b_ws, m_b_bias]
    small_v = [v_norm_mix_pre, v_norm_mix_post, v_norm_ffn_pre, v_norm_ffn_post, v_lb_table, v_a_norm, v_b_ln_g,
               v_b_ln_b, v_b_ws, v_b_bias]
    partial = [jnp.concatenate([dg_mpre0, dg_mpre1]), jnp.concatenate([dg_mpost0, dg_mpost1]),
               jnp.concatenate([dg_fpre0, dg_fpre1]), jnp.concatenate([dg_fpost0, dg_fpost1]),
               d_lb, d_anorm, d_lng, d_lnb, d_ws[None], d_bias_t.T[None]]
    *small_g, loss = _unpack(allreduce_small(_pack(partial + [loss_part]), "allreduce_small"),
                             [w.shape for w in small_w] + [()])
    small_d, small_nm, small_nv = adamw_small(small_w, small_g, small_m, small_v, "adamw_small")

    order = ["norm_mix_pre", "norm_mix_post", "norm_ffn_pre", "norm_ffn_post", "w_in_even", "lb_table", "a_norm",
             "b_ln_g", "b_ln_b", "b_ws", "b_bias", "w_out_even", "w_in_odd", "w_out_odd", "w_ff1", "w_ff2"]
    small_names = ["norm_mix_pre", "norm_mix_post", "norm_ffn_pre", "norm_ffn_post", "lb_table", "a_norm",
                   "b_ln_g", "b_ln_b", "b_ws", "b_bias"]
    big_names = ["w_in_even", "w_out_even", "w_in_odd", "w_out_odd", "w_ff1", "w_ff2"]
    grads, deltas, new_m, new_v = {}, {}, {}, {}
    for i, nm in enumerate(small_names):
        grads[nm], deltas[nm], new_m[nm], new_v[nm] = small_g[i], small_d[i], small_nm[i], small_nv[i]
    for i, nm in enumerate(big_names):
        grads[nm], deltas[nm], new_m[nm], new_v[nm] = big[i]
    return (loss, grad_x, *[grads[n] for n in order], *[deltas[n] for n in order],
            *[new_m[n] for n in order], *[new_v[n] for n in order])
```

```python
import functools
import math

import jax
import jax.numpy as jnp
from jax import lax
from jax.experimental import pallas as pl
from jax.experimental.pallas import tpu as pltpu

F32 = jnp.float32
BF16 = jnp.bfloat16
MESH = pl.DeviceIdType.MESH

D_MODEL = 1024
SEQ = 2048
D_FF = 4096
N_CHIPS = 4
A_WIDTH = 512
A_HEADS = 4
A_DK = 128
A_CHUNK = 64
A_SUB = 16
B_WIDTH = 512
B_GROUPS = 4
B_CHUNK = 128
C_HEADS = 16
C_HEAD_DIM = 64
C_ROT_HALF = 8
C_BLOCK = 128
C_DILATIONS = (1, 4, 16)
ROPE_THETA = 500000.0
EPS = 1e-6
ADAM_LR = 0.001
ADAM_B1 = 0.9
ADAM_B2 = 0.999
ADAM_EPS = 1e-08
ADAM_WD = 0.01
ADAM_STEP = 10

ROW_TILE = 512
FFN_ROWS = 1024
WGRAD_ROWS = 2048
VMEM_LIMIT = 56 * 1024 * 1024
NEG_BIG = -1e30


def _params(sem=None):
    return pltpu.CompilerParams(dimension_semantics=sem, vmem_limit_bytes=VMEM_LIMIT)


def _dot(a, b):
    return jnp.dot(a, b, preferred_element_type=F32)


def _dot_nt(a, b):
    return lax.dot_general(a, b, (((1,), (1,)), ((), ())), preferred_element_type=F32)


def _dot_tn(a, b):
    return lax.dot_general(a, b, (((0,), (0,)), ((), ())), preferred_element_type=F32)


def _rms(x, g):
    r = lax.rsqrt(jnp.mean(x * x, axis=-1, keepdims=True) + EPS)
    return x * r * g


def _rms_bwd(x, g, dy):
    r = lax.rsqrt(jnp.mean(x * x, axis=-1, keepdims=True) + EPS)
    xh = x * r
    dg = jnp.sum(dy * xh, axis=0, keepdims=True)
    dxh = dy * g
    dx = r * (dxh - xh * jnp.mean(dxh * xh, axis=-1, keepdims=True))
    return dx, dg


def _accumulate(ref, val, first):
    @pl.when(first)
    def _():
        ref[...] = val

    @pl.when(jnp.logical_not(first))
    def _():
        ref[...] += val


N_DEV = 8
ANY = pl.BlockSpec(memory_space=pl.ANY)


def _place():
    x, y, c = lax.axis_index("x"), lax.axis_index("y"), lax.axis_index("c")
    return x, y, c, [(1 - x, y), (x, 1 - y), (1 - x, 1 - y)]


class _Exchange:
    def __init__(self, kind, arrays):
        self.kind, self.arrays, self.n = kind, list(arrays), len(arrays)
        per_peer = pltpu.SemaphoreType.DMA((3 * self.n,))
        if kind == "gather":
            self.out_shape = [jax.ShapeDtypeStruct((N_CHIPS,) + a.shape, a.dtype) for a in self.arrays]
            self.scratch = [per_peer, per_peer, pltpu.SemaphoreType.DMA((self.n,)), per_peer, per_peer]
        else:
            self.out_shape = [jax.ShapeDtypeStruct(a.shape, a.dtype) for a in self.arrays]
            self.scratch = [per_peer, per_peer, pltpu.SemaphoreType.DMA((self.n,))]

    def _copies(self, ins, outs, sems):
        send_sems, recv_sems, local_sems = sems[:3]
        x, y, c, chips = _place()
        me = 2 * x + y
        local, remote = [], []
        for a in range(self.n):
            if self.kind == "gather":
                local.append(pltpu.make_async_copy(ins[a], outs[a].at[me], local_sems.at[a]))
                half = self.arrays[a].shape[0] // 2

                def rows(ref, core, half=half):
                    return ref.at[pl.ds(core * half, half)]
            else:
                local.append(pltpu.make_async_copy(ins[a].at[me], outs[a].at[3], local_sems.at[a]))
            for j, (px, py) in enumerate(chips):
                k = 3 * a + j
                peer = 2 * px + py

                def copy(src, dst, to, send_sem=send_sems.at[k], recv_sem=recv_sems.at[k]):
                    return pltpu.make_async_remote_copy(src_ref=src, dst_ref=dst, send_sem=send_sem, recv_sem=recv_sem,
                                                        device_id=to, device_id_type=MESH)

                if self.kind == "gather":
                    sent = copy(rows(ins[a], c), rows(outs[a].at[me], c), (px, py, c))
                    landed = copy(rows(ins[a], c), rows(outs[a].at[peer], c), (px, py, c))
                    on = dict(send_sem=sems[3].at[k], recv_sem=sems[4].at[k])
                    passed = copy(rows(outs[a].at[peer], c), rows(outs[a].at[peer], c), (x, y, 1 - c), **on)
                    handed = copy(rows(outs[a].at[peer], c), rows(outs[a].at[peer], 1 - c), (x, y, 1 - c), **on)
                    remote.append((sent, landed, passed, handed))
                else:
                    sent = copy(ins[a].at[peer], outs[a].at[j], (px, py, c))
                    remote.append((sent, sent, None, None))
        return local, remote

    def start(self, ins, outs, sems):
        local, remote = self._copies(ins, outs, sems)
        for cp in local:
            cp.start()
        for sent, _, _, _ in remote:
            sent.start()

    def finish(self, ins, outs, sems):
        local, remote = self._copies(ins, outs, sems)
        for _, landed, passed, _ in remote:
            landed.wait_recv()
            if passed is not None:
                passed.start()
        for sent, _, passed, handed in remote:
            if passed is not None:
                handed.wait_recv()
                passed.wait_send()
            sent.wait_send()
        for cp in local:
            cp.wait()


def _call(body, *, name, grid, in_specs, out_specs, out_shape, args, scratch_shapes=(), aliases=None, exchange=None):
    if exchange is None:
        return pl.pallas_call(
            body, name=name, grid=grid, in_specs=in_specs, out_specs=out_specs, out_shape=out_shape,
            scratch_shapes=list(scratch_shapes), input_output_aliases=aliases or {},
            compiler_params=_params(("arbitrary",) * len(grid)))(*args)
    n_in, n_out, n_scr, n_ex = len(in_specs), len(out_specs), len(scratch_shapes), exchange.n
    steps = grid

    def wrapped(*refs):
        ins, refs = refs[:n_in], refs[n_in:]
        ex_in, refs = refs[:n_ex], refs[n_ex:]
        outs, refs = refs[:n_out], refs[n_out:]
        ex_out, refs = refs[:n_ex], refs[n_ex:]
        scr, sems = refs[:n_scr], refs[n_scr:]
        first = functools.reduce(jnp.logical_and, [pl.program_id(k) == 0 for k in range(len(steps))])
        last = functools.reduce(jnp.logical_and, [pl.program_id(k) == steps[k] - 1 for k in range(len(steps))])

        @pl.when(first)
        def _():
            exchange.start(ex_in, ex_out, sems)

        body(*ins, *outs, *scr)

        @pl.when(last)
        def _():
            exchange.finish(ex_in, ex_out, sems)

    return pl.pallas_call(
        wrapped, name=name, grid=grid,
        in_specs=list(in_specs) + [ANY] * n_ex, out_specs=list(out_specs) + [ANY] * n_ex,
        out_shape=list(out_shape) + exchange.out_shape,
        scratch_shapes=list(scratch_shapes) + exchange.scratch, input_output_aliases=aliases or {},
        compiler_params=_params(("arbitrary",) * len(grid)))(*args, *exchange.arrays)


def exchange_alone(exchange, name):
    def body(*refs):
        n = exchange.n
        exchange.start(refs[:n], refs[n:2 * n], refs[2 * n:])
        exchange.finish(refs[:n], refs[n:2 * n], refs[2 * n:])

    return pl.pallas_call(
        body, name=name, in_specs=[ANY] * exchange.n, out_specs=[ANY] * exchange.n,
        out_shape=exchange.out_shape, scratch_shapes=exchange.scratch)(*exchange.arrays)


def norm_matmul(x, g, wg, name, exchange=None):
    t, d = x.shape
    nl = wg.shape[2]

    def body(x_ref, g_ref, w_ref, o_ref, h_ref):
        h = _rms(x_ref[...], g_ref[...]).astype(BF16)
        h_ref[...] = h
        for c in range(N_CHIPS):
            o_ref[:, c * nl:(c + 1) * nl] = _dot(h, w_ref[c])

    return _call(
        body, name=name, grid=(t // ROW_TILE,),
        in_specs=[pl.BlockSpec((ROW_TILE, d), lambda i: (i, 0)),
                  pl.BlockSpec((1, d), lambda i: (0, 0)),
                  pl.BlockSpec((N_CHIPS, d, nl), lambda i: (0, 0, 0))],
        out_specs=[pl.BlockSpec((ROW_TILE, N_CHIPS * nl), lambda i: (i, 0)),
                   pl.BlockSpec((ROW_TILE, d), lambda i: (i, 0))],
        out_shape=[jax.ShapeDtypeStruct((t, N_CHIPS * nl), F32), jax.ShapeDtypeStruct((t, d), BF16)],
        args=(x, g, wg), exchange=exchange)


def norm_matmul_bwd(dproj, wg, x, g, dres, name, exchange=None):
    t, d = x.shape
    nl = wg.shape[2]

    def body(dp_ref, w_ref, x_ref, g_ref, dres_ref, dx_ref, dg_ref):
        dh = _dot_nt(dp_ref[:, 0:nl].astype(BF16), w_ref[0])
        for c in range(1, N_CHIPS):
            dh += _dot_nt(dp_ref[:, c * nl:(c + 1) * nl].astype(BF16), w_ref[c])
        dx, dg = _rms_bwd(x_ref[...], g_ref[...], dh)
        dx_ref[...] = dres_ref[...] + dx
        _accumulate(dg_ref, dg, pl.program_id(0) == 0)

    row = pl.BlockSpec((ROW_TILE, d), lambda i: (i, 0))
    vec = pl.BlockSpec((1, d), lambda i: (0, 0))
    return _call(
        body, name=name, grid=(t // ROW_TILE,),
        in_specs=[pl.BlockSpec((ROW_TILE, N_CHIPS * nl), lambda i: (i, 0)),
                  pl.BlockSpec((N_CHIPS, d, nl), lambda i: (0, 0, 0)), row, vec, row],
        out_specs=[row, vec],
        out_shape=[jax.ShapeDtypeStruct((t, d), F32), jax.ShapeDtypeStruct((1, d), F32)],
        args=(dproj, wg, x, g, dres), exchange=exchange)


def out_proj(a, wg, x, g, name):
    t, d = x.shape
    kl = wg.shape[1]

    def body(a_ref, w_ref, x_ref, g_ref, mix_ref, xo_ref):
        acc = _dot(a_ref[:, 0:kl], w_ref[0])
        for c in range(1, N_CHIPS):
            acc += _dot(a_ref[:, c * kl:(c + 1) * kl], w_ref[c])
        mix_ref[...] = acc
        xo_ref[...] = x_ref[...] + _rms(acc, g_ref[...])

    row = pl.BlockSpec((ROW_TILE, d), lambda i: (i, 0))
    return pl.pallas_call(
        body, name=name, grid=(t // ROW_TILE,),
        in_specs=[row, pl.BlockSpec((N_CHIPS, kl, d), lambda i: (0, 0, 0)), row,
                  pl.BlockSpec((1, d), lambda i: (0, 0))],
        out_specs=[row, row],
        out_shape=[jax.ShapeDtypeStruct((t, d), F32), jax.ShapeDtypeStruct((t, d), F32)],
        compiler_params=_params(("arbitrary",)),
    )(a, wg, x, g)


def out_proj_bwd(dxo, mix, g, wg, name):
    t, d = mix.shape
    kl = wg.shape[1]

    def body(dxo_ref, mix_ref, g_ref, w_ref, dmix_ref, da_ref, dg_ref):
        dmix, dg = _rms_bwd(mix_ref[...], g_ref[...], dxo_ref[...])
        dmb = dmix.astype(BF16)
        dmix_ref[...] = dmb
        for c in range(N_CHIPS):
            da_ref[:, c * kl:(c + 1) * kl] = _dot_nt(dmb, w_ref[c])
        _accumulate(dg_ref, dg, pl.program_id(0) == 0)

    row = pl.BlockSpec((ROW_TILE, d), lambda i: (i, 0))
    vec = pl.BlockSpec((1, d), lambda i: (0, 0))
    return pl.pallas_call(
        body, name=name, grid=(t // ROW_TILE,),
        in_specs=[row, row, vec, pl.BlockSpec((N_CHIPS, kl, d), lambda i: (0, 0, 0))],
        out_specs=[row, row, vec],
        out_shape=[jax.ShapeDtypeStruct((t, d), BF16), jax.ShapeDtypeStruct((t, d), F32),
                   jax.ShapeDtypeStruct((1, d), F32)],
        compiler_params=_params(("arbitrary",)),
    )(dxo, mix, g, wg)


def ffn_fwd(x, gpre, w1g, w2g, gpost, name, exchange=None, target=None):
    t, d = x.shape
    hc = w1g.shape[2]
    with_loss = target is not None

    def body(x_ref, gpre_ref, w1_ref, w2_ref, gpost_ref, *rest):
        if with_loss:
            t_ref, xo_ref, h_ref, a_ref, y_ref, l_ref, acc = rest
        else:
            xo_ref, h_ref, a_ref, y_ref, acc = rest
        i, c = pl.program_id(0), pl.program_id(1)

        @pl.when(c == 0)
        def _():
            h_ref[...] = _rms(x_ref[...], gpre_ref[...]).astype(BF16)

        a = _dot(h_ref[...], w1_ref[...])
        a_ref[...] = a.astype(BF16)
        r = jnp.square(jnp.maximum(a, 0.0)).astype(BF16)
        _accumulate(acc, _dot(r, w2_ref[...]), c == 0)

        @pl.when(c == N_CHIPS - 1)
        def _():
            y = acc[...]
            y_ref[...] = y
            xo = x_ref[...] + _rms(y, gpost_ref[...])
            if with_loss:
                e = xo - t_ref[...]
                xo_ref[...] = e * (1.0 / d)
                part = jnp.sum(jnp.sum(e * e, axis=-1, keepdims=True), axis=0, keepdims=True) * (0.5 / d)
                _accumulate(l_ref, part, i == 0)
            else:
                xo_ref[...] = xo

    row = pl.BlockSpec((FFN_ROWS, d), lambda i, c: (i, 0))
    vec = pl.BlockSpec((1, d), lambda i, c: (0, 0))
    one = pl.BlockSpec((1, 1), lambda i, c: (0, 0))
    return _call(
        body, name=name, grid=(t // FFN_ROWS, N_CHIPS),
        in_specs=[row, vec,
                  pl.BlockSpec((None, d, hc), lambda i, c: (c, 0, 0)),
                  pl.BlockSpec((None, hc, d), lambda i, c: (c, 0, 0)), vec] + ([row] if with_loss else []),
        out_specs=[row, row, pl.BlockSpec((FFN_ROWS, hc), lambda i, c: (i, c)), row] + ([one] if with_loss else []),
        out_shape=[jax.ShapeDtypeStruct((t, d), F32), jax.ShapeDtypeStruct((t, d), BF16),
                   jax.ShapeDtypeStruct((t, N_CHIPS * hc), BF16), jax.ShapeDtypeStruct((t, d), F32)]
        + ([jax.ShapeDtypeStruct((1, 1), F32)] if with_loss else []),
        scratch_shapes=[pltpu.VMEM((FFN_ROWS, d), F32)],
        args=(x, gpre, w1g, w2g, gpost) + ((target,) if with_loss else ()), exchange=exchange)


def ffn_bwd(dxo, x, y, a, gpre, gpost, w1g, w2g, name, exchange=None):
    t, d = x.shape
    hc = w1g.shape[2]

    def body(dxo_ref, x_ref, y_ref, a_ref, gpre_ref, gpost_ref, w1_ref, w2_ref,
             dxi_ref, dy_ref, da_ref, dgpre_ref, dgpost_ref, acc):
        i, c = pl.program_id(0), pl.program_id(1)

        @pl.when(c == 0)
        def _():
            dy, dg = _rms_bwd(y_ref[...], gpost_ref[...], dxo_ref[...])
            dy_ref[...] = dy.astype(BF16)
            _accumulate(dgpost_ref, dg, i == 0)

        dr = _dot_nt(dy_ref[...], w2_ref[...])
        da = (dr * (2.0 * jnp.maximum(a_ref[...].astype(F32), 0.0))).astype(BF16)
        da_ref[...] = da
        _accumulate(acc, _dot_nt(da, w1_ref[...]), c == 0)

        @pl.when(c == N_CHIPS - 1)
        def _():
            dx, dg = _rms_bwd(x_ref[...], gpre_ref[...], acc[...])
            dxi_ref[...] = dxo_ref[...] + dx
            _accumulate(dgpre_ref, dg, i == 0)

    row = pl.BlockSpec((ROW_TILE, d), lambda i, c: (i, 0))
    vec = pl.BlockSpec((1, d), lambda i, c: (0, 0))
    hid = pl.BlockSpec((ROW_TILE, hc), lambda i, c: (i, c))
    return _call(
        body, name=name, grid=(t // ROW_TILE, N_CHIPS),
        in_specs=[row, row, row, hid, vec, vec,
                  pl.BlockSpec((None, d, hc), lambda i, c: (c, 0, 0)),
                  pl.BlockSpec((None, hc, d), lambda i, c: (c, 0, 0))],
        out_specs=[row, row, hid, vec, vec],
        out_shape=[jax.ShapeDtypeStruct((t, d), F32), jax.ShapeDtypeStruct((t, d), BF16),
                   jax.ShapeDtypeStruct((t, N_CHIPS * hc), BF16),
                   jax.ShapeDtypeStruct((1, d), F32), jax.ShapeDtypeStruct((1, d), F32)],
        scratch_shapes=[pltpu.VMEM((ROW_TILE, d), F32)],
        args=(dxo, x, y, a, gpre, gpost, w1g, w2g), exchange=exchange)


def weight_grad(a, b, chunked, bk, bn, relu2, name, exchange=None):
    t = a.shape[0]
    a_on = chunked == "a"
    rows = min(t, WGRAD_ROWS)
    n_steps = t // rows

    def body(a_ref, b_ref, o_ref, acc):
        s = pl.program_id(1)
        av = a_ref[...]
        if relu2:
            av = jnp.square(jnp.maximum(av.astype(F32), 0.0))
        _accumulate(acc, _dot_tn(av.astype(BF16), b_ref[...].astype(BF16)), s == 0)

        @pl.when(s == n_steps - 1)
        def _():
            o_ref[...] = acc[...].astype(BF16)

    res = _call(
        body, name=name, grid=(N_CHIPS, n_steps),
        in_specs=[pl.BlockSpec((rows, bk), (lambda c, s: (s, c)) if a_on else (lambda c, s: (s, 0))),
                  pl.BlockSpec((rows, bn), (lambda c, s: (s, 0)) if a_on else (lambda c, s: (s, c)))],
        out_specs=[pl.BlockSpec((None, bk, bn), lambda c, s: (c, 0, 0))],
        out_shape=[jax.ShapeDtypeStruct((N_CHIPS, bk, bn), BF16)],
        scratch_shapes=[pltpu.VMEM((bk, bn), F32)],
        args=(a, b), exchange=exchange)
    return res[0] if exchange is None else res


def _hgrn2_chunk(st, qs, fls, ivs, gls, l0, l1, l2, ng):
    nsub = len(qs)
    mx = jnp.maximum(jnp.maximum(l0, l1), l2)
    e0, e1, e2 = jnp.exp(l0 - mx), jnp.exp(l1 - mx), jnp.exp(l2 - mx)
    lb = e0 / (e0 + e1 + e2)
    rows = lax.broadcasted_iota(jnp.int32, (A_SUB, A_SUB), 0)
    cols = lax.broadcasted_iota(jnp.int32, (A_SUB, A_SUB), 1)
    tri = (rows >= cols).astype(F32)
    keep = (lax.broadcasted_iota(jnp.int32, (A_SUB, A_SUB, A_DK), 0)
            >= lax.broadcasted_iota(jnp.int32, (A_SUB, A_SUB, A_DK), 1))
    base = jnp.zeros_like(l0)
    bases, gs, ks, qfs = [], [], [], []
    for i in range(nsub):
        f = lb + (1.0 - lb) * jax.nn.sigmoid(fls[i])
        logf = jnp.log(f)
        bases.append(base)
        gs.append(base + jnp.dot(tri, logf, precision=lax.Precision.HIGHEST, preferred_element_type=F32))
        base = base + jnp.sum(logf, axis=0, keepdims=True)
        ks.append(1.0 - f)
        qfs.append(jax.nn.silu(qs[i]))
    g_last = base
    stb = st.astype(BF16)
    outs = []
    for i in range(nsub):
        o = _dot_nt((qfs[i] * jnp.exp(gs[i])).astype(BF16), stb)
        if i > 0:
            qt = (qfs[i] * jnp.exp(gs[i] - bases[i])).astype(BF16)
            kk = jnp.concatenate([ks[j] * jnp.exp(bases[i] - gs[j]) for j in range(i)], axis=0).astype(BF16)
            vv = jnp.concatenate(ivs[:i], axis=0).astype(BF16)
            o = o + _dot(_dot_nt(qt, kk).astype(BF16), vv)
        dec = jnp.exp(jnp.where(keep, gs[i][:, None, :] - gs[i][None, :, :], NEG_BIG))
        s_diag = jnp.sum(qfs[i][:, None, :] * ks[i][None, :, :] * dec, axis=-1)
        o = o + _dot(s_diag.astype(BF16), ivs[i].astype(BF16))
        o = o * lax.rsqrt(jnp.mean(o * o, axis=-1, keepdims=True) + EPS) * ng
        outs.append(o * jax.nn.silu(gls[i]))
    kdec = jnp.concatenate([ks[j] * jnp.exp(g_last - gs[j]) for j in range(nsub)], axis=0).astype(BF16)
    vall = jnp.concatenate(ivs, axis=0).astype(BF16)
    new_st = st * jnp.exp(g_last) + _dot_tn(vall, kdec)
    return new_st, outs


A_MAX_LOG_DECAY = 60.0


def _half_sums(logf):
    n = logf.shape[0]
    first = lax.broadcasted_iota(jnp.int32, logf.shape, 0) < n // 2
    return (jnp.sum(jnp.where(first, logf, 0.0), axis=0, keepdims=True),
            jnp.sum(jnp.where(first, 0.0, logf), axis=0, keepdims=True))


def _split3(x):
    hi = x.astype(BF16)
    r1 = x - hi.astype(F32)
    mid = r1.astype(BF16)
    return hi, mid, (r1 - mid.astype(F32)).astype(BF16)


def _tri_matmul(x, transpose):
    n = x.shape[0]
    r = lax.broadcasted_iota(jnp.int32, (n, n), 0)
    c = lax.broadcasted_iota(jnp.int32, (n, n), 1)
    tri = ((r <= c) if transpose else (r >= c)).astype(BF16)
    hi, mid, lo = _split3(x)
    return (_dot(tri, lo) + _dot(tri, mid)) + _dot(tri, hi)


@jax.custom_vjp
def _cumsum_rows(x):
    return _tri_matmul(x, False)


def _cumsum_rows_fwd(x):
    return _tri_matmul(x, False), None


def _cumsum_rows_bwd(_, dy):
    return (_tri_matmul(dy, True),)


_cumsum_rows.defvjp(_cumsum_rows_fwd, _cumsum_rows_bwd)


def _lower_bound(l0, l1, l2):
    mx = jnp.maximum(jnp.maximum(l0, l1), l2)
    e0, e1, e2 = jnp.exp(l0 - mx), jnp.exp(l1 - mx), jnp.exp(l2 - mx)
    return e0 / (e0 + e1 + e2)


def _b(x):
    return x.astype(BF16)


@jax.custom_vjp
def _mm(a, b):
    return _dot(_b(a), _b(b))


_mm.defvjp(lambda a, b: (_mm(a, b), (a, b)),
           lambda res, d: (_dot_nt(_b(d), _b(res[1])), _dot_tn(_b(res[0]), _b(d))))


@jax.custom_vjp
def _mm_nt(a, b):
    return _dot_nt(_b(a), _b(b))


_mm_nt.defvjp(lambda a, b: (_mm_nt(a, b), (a, b)),
              lambda res, d: (_dot(_b(d), _b(res[1])), _dot_tn(_b(d), _b(res[0]))))


def _dot_split(dot, a, b):
    ah, bh = _b(a), _b(b)
    al, bl = _b(a - ah.astype(F32)), _b(b - bh.astype(F32))
    return (dot(ah, bl) + dot(al, bh)) + dot(ah, bh)


@jax.custom_vjp
def _mm_scores(a, b):
    return _dot_nt(_b(a), _b(b))


_mm_scores.defvjp(lambda a, b: (_mm_scores(a, b), (a, b)),
                  lambda res, d: (_dot_split(_dot, d, res[1]), _dot_split(_dot_tn, d, res[0])))


@jax.custom_vjp
def _mm_tn(a, b):
    return _dot_tn(_b(a), _b(b))


_mm_tn.defvjp(lambda a, b: (_mm_tn(a, b), (a, b)),
              lambda res, d: (_dot_nt(_b(res[1]), _b(d)), _dot(_b(res[0]), _b(d))))


@jax.custom_vjp
def _split_heads(x):
    return tuple(x[:, h * A_DK:(h + 1) * A_DK] for h in range(A_HEADS))


def _split_heads_fwd(x):
    return _split_heads(x), None


def _split_heads_bwd(_, parts):
    return (jnp.concatenate(parts, axis=1),)


_split_heads.defvjp(_split_heads_fwd, _split_heads_bwd)


def _hgrn2_chunk_fast(sts, q, fl, iv, gl, l0, l1, l2, ng):
    lb = _lower_bound(l0, l1, l2)
    f = lb + (1.0 - lb) * jax.nn.sigmoid(fl)
    return _hgrn2_fast_core(sts, q, f, jnp.log(f), iv, gl, ng)


def _hgrn2_fast_core(sts, q, f, logf, iv, gl, ng):
    g = _cumsum_rows(logf)
    g_mid, g_last = _half_sums(logf)
    g_last = g_mid + g_last
    k = 1.0 - f
    qf = jax.nn.silu(q)
    qms = _split_heads(qf * jnp.exp(g - g_mid))
    kms = _split_heads(k * jnp.exp(g_mid - g))
    qgs = _split_heads(qf * jnp.exp(g))
    kds = _split_heads(k * jnp.exp(g_last - g))
    ivs = _split_heads(iv)
    decays = _split_heads(jnp.exp(g_last))
    n = q.shape[0]
    causal = lax.broadcasted_iota(jnp.int32, (n, n), 0) >= lax.broadcasted_iota(jnp.int32, (n, n), 1)
    raw = [_mm_scores(qm, km) for qm, km in zip(qms, kms)]
    inter = [_mm_nt(qg, st) for qg, st in zip(qgs, sts)]
    scores = [jnp.where(causal, s, 0.0) for s in raw]
    os = [a + _mm(s, v) for a, s, v in zip(inter, scores, ivs)]
    new_sts = [st * d + _mm_tn(v, kd) for st, d, v, kd in zip(sts, decays, ivs, kds)]
    os = [o * lax.rsqrt(jnp.mean(o * o, axis=-1, keepdims=True) + EPS) for o in os]
    return new_sts, jnp.concatenate(os, axis=1) * ng * jax.nn.silu(gl)


A_STEP_CHUNKS = 4


def _chunk_rows(j):
    return pl.ds(pl.multiple_of(j * A_CHUNK, A_CHUNK), A_CHUNK)


def _sub_rows(j, i):
    return pl.ds(pl.multiple_of(j * A_CHUNK + i * A_SUB, A_SUB), A_SUB)


def _sub_blocks(ref, head, j):
    lanes = slice(head * A_DK, (head + 1) * A_DK)
    return [ref[_sub_rows(j, i), lanes] for i in range(A_CHUNK // A_SUB)]


def hgrn2_fwd(proj, lb_table, a_norm, batch, name, exchange=None):
    t = proj.shape[0]
    n_steps = t // batch // (A_CHUNK * A_STEP_CHUNKS)
    rows = A_CHUNK * A_STEP_CHUNKS

    def body(q_ref, f_ref, i_ref, g_ref, lb_ref, ng_ref, o_ref, st_ref, dec_ref, st):
        @pl.when(pl.program_id(1) == 0)
        def _():
            st[...] = jnp.zeros_like(st)

        def chunk(j, carry):
            r = _chunk_rows(j)
            st_ref[j] = st[...]
            lb = _lower_bound(lb_ref[0:1, :], lb_ref[1:2, :], lb_ref[2:3, :])
            f = lb + (1.0 - lb) * jax.nn.sigmoid(f_ref[r, :])
            logf = jnp.log(f)
            decay = jnp.minimum(*_half_sums(logf))
            dec_ref[j] = decay
            mild = jnp.min(decay) >= -A_MAX_LOG_DECAY

            @pl.when(mild)
            def _():
                new_sts, o = _hgrn2_fast_core([st[h] for h in range(A_HEADS)], q_ref[r, :], f, logf,
                                              i_ref[r, :], g_ref[r, :], ng_ref[...])
                for h in range(A_HEADS):
                    st[h] = new_sts[h]
                o_ref[r, :] = o.astype(BF16)

            @pl.when(jnp.logical_not(mild))
            def _():
                for h in range(A_HEADS):
                    lanes = slice(h * A_DK, (h + 1) * A_DK)
                    new_st, outs = _hgrn2_chunk(
                        st[h], _sub_blocks(q_ref, h, j), _sub_blocks(f_ref, h, j), _sub_blocks(i_ref, h, j),
                        _sub_blocks(g_ref, h, j), lb_ref[0:1, lanes], lb_ref[1:2, lanes], lb_ref[2:3, lanes],
                        ng_ref[:, lanes])
                    st[h] = new_st
                    for i, o in enumerate(outs):
                        o_ref[_sub_rows(j, i), lanes] = o.astype(BF16)

            return carry

        lax.fori_loop(0, A_STEP_CHUNKS, chunk, 0)

    def part(k):
        return pl.BlockSpec((rows, A_WIDTH), lambda b, n: (b * n_steps + n, k))

    return _call(
        body, name=name, grid=(batch, n_steps),
        in_specs=[part(0), part(1), part(2), part(3),
                  pl.BlockSpec((3, A_WIDTH), lambda b, n: (0, 0)), pl.BlockSpec((1, A_WIDTH), lambda b, n: (0, 0))],
        out_specs=[part(0),
                   pl.BlockSpec((A_STEP_CHUNKS, A_HEADS, A_DK, A_DK), lambda b, n: (b * n_steps + n, 0, 0, 0)),
                   pl.BlockSpec((A_STEP_CHUNKS, 1, A_WIDTH), lambda b, n: (b * n_steps + n, 0, 0))],
        out_shape=[jax.ShapeDtypeStruct((t, A_WIDTH), BF16),
                   jax.ShapeDtypeStruct((t // A_CHUNK, A_HEADS, A_DK, A_DK), F32),
                   jax.ShapeDtypeStruct((t // A_CHUNK, 1, A_WIDTH), F32)],
        scratch_shapes=[pltpu.VMEM((A_HEADS, A_DK, A_DK), F32)],
        args=(proj, proj, proj, proj, lb_table, a_norm), exchange=exchange)


def hgrn2_bwd(proj, states, decays, lb_table, a_norm, do, batch, name, exchange=None):
    t = proj.shape[0]
    n_steps = t // batch // (A_CHUNK * A_STEP_CHUNKS)
    rows = A_CHUNK * A_STEP_CHUNKS

    def body(q_ref, f_ref, i_ref, g_ref, st_ref, dec_ref, lb_ref, ng_ref, do_ref, dp_ref, dlb_ref, dng_ref, dst):
        @pl.when(jnp.logical_and(pl.program_id(0) == 0, pl.program_id(1) == 0))
        def _():
            dlb_ref[...] = jnp.zeros_like(dlb_ref)
            dng_ref[...] = jnp.zeros_like(dng_ref)

        @pl.when(pl.program_id(1) == 0)
        def _():
            dst[...] = jnp.zeros_like(dst)

        def chunk(jj, carry):
            j = A_STEP_CHUNKS - 1 - jj
            r = _chunk_rows(j)
            mild = jnp.min(dec_ref[j]) >= -A_MAX_LOG_DECAY

            @pl.when(mild)
            def _():
                _, vjp = jax.vjp(
                    _hgrn2_chunk_fast, [st_ref[j, h] for h in range(A_HEADS)], q_ref[r, :], f_ref[r, :],
                    i_ref[r, :], g_ref[r, :], lb_ref[0:1, :], lb_ref[1:2, :], lb_ref[2:3, :], ng_ref[...])
                d_sts, dq, df, di, dg, dl0, dl1, dl2, dng = vjp(
                    ([dst[h] for h in range(A_HEADS)], do_ref[r, :].astype(F32)))
                for h in range(A_HEADS):
                    dst[h] = d_sts[h]
                for k, part in enumerate((dq, df, di, dg)):
                    dp_ref[r, k * A_WIDTH:(k + 1) * A_WIDTH] = part
                for row, val in enumerate((dl0, dl1, dl2)):
                    dlb_ref[row:row + 1, :] += val
                dng_ref[...] += dng

            @pl.when(jnp.logical_not(mild))
            def _():
                for h in range(A_HEADS):
                    lanes = slice(h * A_DK, (h + 1) * A_DK)
                    _, vjp = jax.vjp(
                        _hgrn2_chunk, st_ref[j, h], _sub_blocks(q_ref, h, j), _sub_blocks(f_ref, h, j),
                        _sub_blocks(i_ref, h, j), _sub_blocks(g_ref, h, j), lb_ref[0:1, lanes], lb_ref[1:2, lanes],
                        lb_ref[2:3, lanes], ng_ref[:, lanes])
                    douts = [x.astype(F32) for x in _sub_blocks(do_ref, h, j)]
                    d_st, dqs, dfs, dis, dgs, dl0, dl1, dl2, dng = vjp((dst[h], douts))
                    dst[h] = d_st
                    for k, parts in enumerate((dqs, dfs, dis, dgs)):
                        for i in range(A_CHUNK // A_SUB):
                            dp_ref[_sub_rows(j, i), k * A_WIDTH + h * A_DK:k * A_WIDTH + (h + 1) * A_DK] = parts[i]
                    for row, val in enumerate((dl0, dl1, dl2)):
                        dlb_ref[row:row + 1, lanes] += val
                    dng_ref[:, lanes] += dng

            return carry

        lax.fori_loop(0, A_STEP_CHUNKS, chunk, 0)

    def rev(b, n):
        return b * n_steps + (n_steps - 1 - n)

    def part(k):
        return pl.BlockSpec((rows, A_WIDTH), lambda b, n: (rev(b, n), k))

    const3 = pl.BlockSpec((3, A_WIDTH), lambda b, n: (0, 0))
    const1 = pl.BlockSpec((1, A_WIDTH), lambda b, n: (0, 0))
    return _call(
        body, name=name, grid=(batch, n_steps),
        in_specs=[part(0), part(1), part(2), part(3),
                  pl.BlockSpec((A_STEP_CHUNKS, A_HEADS, A_DK, A_DK), lambda b, n: (rev(b, n), 0, 0, 0)),
                  pl.BlockSpec((A_STEP_CHUNKS, 1, A_WIDTH), lambda b, n: (rev(b, n), 0, 0)),
                  const3, const1, part(0)],
        out_specs=[pl.BlockSpec((rows, 4 * A_WIDTH), lambda b, n: (rev(b, n), 0)), const3, const1],
        out_shape=[jax.ShapeDtypeStruct((t, 4 * A_WIDTH + 2 * B_WIDTH), F32),
                   jax.ShapeDtypeStruct((3, A_WIDTH), F32), jax.ShapeDtypeStruct((1, A_WIDTH), F32)],
        scratch_shapes=[pltpu.VMEM((A_HEADS, A_DK, A_DK), F32)],
        args=(proj, proj, proj, proj, states, decays, lb_table, a_norm, do), exchange=exchange)


B_GDIM = B_WIDTH // B_GROUPS
B_ROWS = 512


def _gmlp_chunk(ubs, vbs, lngs, lnbs, ws, bcols):
    vs = [jax.nn.gelu(v) for v in vbs]
    mu = sum(jnp.sum(v, axis=-1, keepdims=True) for v in vs) * (1.0 / B_WIDTH)
    var = sum(jnp.sum(jnp.square(v - mu), axis=-1, keepdims=True) for v in vs) * (1.0 / B_WIDTH)
    rstd = lax.rsqrt(var + EPS)
    tril = (lax.broadcasted_iota(jnp.int32, (B_CHUNK, B_CHUNK), 0)
            >= lax.broadcasted_iota(jnp.int32, (B_CHUNK, B_CHUNK), 1))
    outs = []
    for g in range(B_GROUPS):
        vn = (vs[g] - mu) * rstd * lngs[g] + lnbs[g]
        w = jnp.where(tril, ws[g], 0.0).astype(BF16)
        outs.append(jax.nn.gelu(ubs[g]) * (_dot(w, vn.astype(BF16)) + bcols[g]))
    return outs


def _gmlp_args(u_ref, v_ref, lng_ref, lnb_ref, w_ref, bt_ref, rows):
    def groups(ref):
        return [ref[rows, g * B_GDIM:(g + 1) * B_GDIM] for g in range(B_GROUPS)]

    def vec(ref):
        return [ref[:, g * B_GDIM:(g + 1) * B_GDIM] for g in range(B_GROUPS)]

    return (groups(u_ref), groups(v_ref), vec(lng_ref), vec(lnb_ref),
            [w_ref[g] for g in range(B_GROUPS)], [bt_ref[:, g:g + 1] for g in range(B_GROUPS)])


def gmlp_fwd(proj, oa, ln_g, ln_b, w, bias_t, name, exchange=None):
    t = proj.shape[0]

    def body(u_ref, v_ref, oa_ref, lng_ref, lnb_ref, w_ref, bt_ref, o_ref):
        o_ref[:, 0:A_WIDTH] = oa_ref[...]
        for n in range(B_ROWS // B_CHUNK):
            rows = slice(n * B_CHUNK, (n + 1) * B_CHUNK)
            outs = _gmlp_chunk(*_gmlp_args(u_ref, v_ref, lng_ref, lnb_ref, w_ref, bt_ref, rows))
            for g, o in enumerate(outs):
                o_ref[rows, A_WIDTH + g * B_GDIM:A_WIDTH + (g + 1) * B_GDIM] = o.astype(BF16)

    vec = pl.BlockSpec((1, B_WIDTH), lambda i: (0, 0))
    return _call(
        body, name=name, grid=(t // B_ROWS,),
        in_specs=[pl.BlockSpec((B_ROWS, B_WIDTH), lambda i: (i, 4)), pl.BlockSpec((B_ROWS, B_WIDTH), lambda i: (i, 5)),
                  pl.BlockSpec((B_ROWS, A_WIDTH), lambda i: (i, 0)), vec, vec,
                  pl.BlockSpec((B_GROUPS, B_CHUNK, B_CHUNK), lambda i: (0, 0, 0)),
                  pl.BlockSpec((B_CHUNK, B_GROUPS), lambda i: (0, 0))],
        out_specs=[pl.BlockSpec((B_ROWS, A_WIDTH + B_WIDTH), lambda i: (i, 0))],
        out_shape=[jax.ShapeDtypeStruct((t, A_WIDTH + B_WIDTH), BF16)],
        args=(proj, proj, oa, ln_g, ln_b, w, bias_t), exchange=exchange)


def gmlp_bwd(proj, dmixin, ln_g, ln_b, w, bias_t, dproj, name, exchange=None):
    t = proj.shape[0]

    def body(u_ref, v_ref, do_ref, lng_ref, lnb_ref, w_ref, bt_ref, dp_in_ref,
             dp_ref, dlng_ref, dlnb_ref, dw_ref, dbt_ref):
        del dp_in_ref

        @pl.when(pl.program_id(0) == 0)
        def _():
            for ref in (dlng_ref, dlnb_ref, dw_ref, dbt_ref):
                ref[...] = jnp.zeros_like(ref)

        for n in range(B_ROWS // B_CHUNK):
            rows = slice(n * B_CHUNK, (n + 1) * B_CHUNK)
            _, vjp = jax.vjp(_gmlp_chunk, *_gmlp_args(u_ref, v_ref, lng_ref, lnb_ref, w_ref, bt_ref, rows))
            douts = [do_ref[rows, g * B_GDIM:(g + 1) * B_GDIM] for g in range(B_GROUPS)]
            dus, dvs, dlngs, dlnbs, dws, dbs = vjp(douts)
            for g in range(B_GROUPS):
                lanes = slice(g * B_GDIM, (g + 1) * B_GDIM)
                dp_ref[rows, lanes] = dus[g]
                dp_ref[rows, B_WIDTH + g * B_GDIM:B_WIDTH + (g + 1) * B_GDIM] = dvs[g]
                dlng_ref[:, lanes] += dlngs[g]
                dlnb_ref[:, lanes] += dlnbs[g]
                dw_ref[g] += dws[g]
                dbt_ref[:, g:g + 1] += dbs[g]

    vec = pl.BlockSpec((1, B_WIDTH), lambda i: (0, 0))
    wspec = pl.BlockSpec((B_GROUPS, B_CHUNK, B_CHUNK), lambda i: (0, 0, 0))
    bspec = pl.BlockSpec((B_CHUNK, B_GROUPS), lambda i: (0, 0))
    return _call(
        body, name=name, grid=(t // B_ROWS,),
        in_specs=[pl.BlockSpec((B_ROWS, B_WIDTH), lambda i: (i, 4)), pl.BlockSpec((B_ROWS, B_WIDTH), lambda i: (i, 5)),
                  pl.BlockSpec((B_ROWS, B_WIDTH), lambda i: (i, 1)), vec, vec, wspec, bspec,
                  pl.BlockSpec(memory_space=pl.ANY)],
        out_specs=[pl.BlockSpec((B_ROWS, 2 * B_WIDTH), lambda i: (i, 2)), vec, vec, wspec, bspec],
        out_shape=[jax.ShapeDtypeStruct(dproj.shape, F32), jax.ShapeDtypeStruct((1, B_WIDTH), F32),
                   jax.ShapeDtypeStruct((1, B_WIDTH), F32), jax.ShapeDtypeStruct((B_GROUPS, B_CHUNK, B_CHUNK), F32),
                   jax.ShapeDtypeStruct((B_CHUNK, B_GROUPS), F32)],
        aliases={7: 0}, args=(proj, proj, dmixin, ln_g, ln_b, w, bias_t, dproj), exchange=exchange)


C_FWD_BLOCKS = 8
C_BWD_BLOCKS = 4
C_PAIR = 2 * C_HEAD_DIM
C_PAIRS = C_HEADS // 2
C_SCALE = 1.0 / math.sqrt(C_HEAD_DIM)
C_ROT_DIM = 2 * C_ROT_HALF
ROPE_ROWS = 1024


def rope_tables(pos_col, name):
    t = pos_col.shape[0]

    def body(p_ref, c_ref, a_ref, b_ref):
        lane = jnp.bitwise_and(lax.broadcasted_iota(jnp.int32, (1, C_PAIR), 1), C_HEAD_DIM - 1)
        j = jnp.bitwise_and(lane, C_ROT_HALF - 1).astype(F32)
        inv = jnp.exp(j * (-math.log(ROPE_THETA) / C_ROT_HALF))
        ang = p_ref[...].astype(F32) * inv
        cos, sin = jnp.cos(ang), jnp.sin(ang)
        c_ref[...] = jnp.where(lane < C_ROT_DIM, cos, 1.0)
        a_ref[...] = jnp.where(lane < C_ROT_HALF, -sin, 0.0)
        b_ref[...] = jnp.where(jnp.logical_and(lane >= C_ROT_HALF, lane < C_ROT_DIM), sin, 0.0)

    tab = pl.BlockSpec((ROPE_ROWS, C_PAIR), lambda i: (i, 0))
    return pl.pallas_call(
        body, name=name, grid=(t // ROPE_ROWS,),
        in_specs=[pl.BlockSpec((ROPE_ROWS, 1), lambda i: (i, 0))],
        out_specs=[tab, tab, tab],
        out_shape=[jax.ShapeDtypeStruct((t, C_PAIR), F32)] * 3,
        compiler_params=_params(("arbitrary",)),
    )(pos_col)


def _rope(x, c, a, b):
    return x * c + pltpu.roll(x, C_PAIR - C_ROT_HALF, 1) * a + pltpu.roll(x, C_ROT_HALF, 1) * b


def _rope_t(d, c, a, b):
    return d * c + pltpu.roll(d * a, C_ROT_HALF, 1) + pltpu.roll(d * b, C_PAIR - C_ROT_HALF, 1)


def _attn_rows(idx, dil):
    nblk = SEQ // dil // C_BLOCK
    r, n = idx // nblk, idx % nblk
    start = r + dil * C_BLOCK * n
    prev = r + dil * C_BLOCK * jnp.maximum(n - 1, 0)
    if dil == 1:
        return pl.ds(pl.multiple_of(start, C_BLOCK), C_BLOCK), pl.ds(pl.multiple_of(prev, C_BLOCK), C_BLOCK), n > 0
    return pl.ds(start, C_BLOCK, stride=dil), pl.ds(prev, C_BLOCK, stride=dil), n > 0


def _head_masks():
    low = lax.broadcasted_iota(jnp.int32, (1, C_PAIR), 1) < C_HEAD_DIM
    return low, jnp.logical_not(low)


def _attn_mask(has_prev):
    i = jnp.bitwise_and(lax.broadcasted_iota(jnp.int32, (2 * C_BLOCK, 2 * C_BLOCK), 0), C_BLOCK - 1)
    j = lax.broadcasted_iota(jnp.int32, (2 * C_BLOCK, 2 * C_BLOCK), 1)
    return jnp.logical_or(j <= i, jnp.logical_and(j - C_BLOCK >= i, has_prev))


def _stack_heads(x):
    low, high = _head_masks()
    return jnp.concatenate([jnp.where(low, x, 0.0), jnp.where(high, x, 0.0)], axis=0)


def _unstack_heads(x):
    low, _ = _head_masks()
    return jnp.where(low, x[:C_BLOCK], x[C_BLOCK:])


def attn_fwd(qkv, cos_t, sin_a, sin_b, batch, name, exchange=None):
    t = qkv.shape[0]
    nbr = len(C_DILATIONS)

    def body(q_ref, k_ref, v_ref, c_ref, a_ref, b_ref, o_ref, l_ref, qs, ks, *stats):
        acc, mm, dd = stats[0:nbr], stats[nbr:2 * nbr], stats[2 * nbr:3 * nbr]
        c, a, b = c_ref[...], a_ref[...], b_ref[...]
        qs[...] = _rope(q_ref[...], c, a, b) * C_SCALE
        ks[...] = _rope(k_ref[...], c, a, b)
        def load(idx, dil):
            rows, prev, has_prev = _attn_rows(idx, dil)
            return rows, (has_prev, qs[rows, :], ks[rows, :], ks[prev, :], v_ref[rows, :], v_ref[prev, :])

        def scores(has_prev, q, k_own, k_prev, v_own, v_prev):
            k_cat = jnp.concatenate([k_own, k_prev], axis=0).astype(BF16)
            return jnp.where(_attn_mask(has_prev), _dot_nt(_stack_heads(q).astype(BF16), k_cat), NEG_BIG)

        def softmax(s):
            m = jnp.max(s, axis=-1, keepdims=True)
            p = jnp.exp(s - m)
            return p.astype(BF16), m, jnp.sum(p, axis=-1, keepdims=True)

        def values(pb, has_prev, q, k_own, k_prev, v_own, v_prev):
            low, high = _head_masks()
            v_cat = jnp.concatenate([v_own, v_prev], axis=0)
            p_wide = jnp.concatenate([pb[:C_BLOCK], pb[C_BLOCK:]], axis=1)
            v_tall = jnp.concatenate([jnp.where(low, v_cat, 0.0), jnp.where(high, v_cat, 0.0)], axis=0).astype(BF16)
            return _dot(p_wide, v_tall)

        for bi, dil in enumerate(C_DILATIONS):
            def pair(i, carry, bi=bi, dil=dil):
                low, _ = _head_masks()
                loaded = [load(C_FWD_BLOCKS * i + k, dil) for k in range(C_FWD_BLOCKS)]
                ss = [scores(*ops) for _, ops in loaded]
                sm = [softmax(s) for s in ss]
                pvs = [values(pb, *ops) for (pb, _, _), (_, ops) in zip(sm, loaded)]
                for (rows, _), (_, m, den), pv in zip(loaded, sm, pvs):
                    acc[bi][rows, :] = pv
                    mm[bi][rows, :] = jnp.where(low, m[:C_BLOCK], m[C_BLOCK:])
                    dd[bi][rows, :] = jnp.where(low, den[:C_BLOCK], den[C_BLOCK:])
                return carry

            lax.fori_loop(0, SEQ // C_BLOCK // C_FWD_BLOCKS, pair, 0)
        step = 256
        for r0 in range(0, SEQ, step):
            rr = slice(r0, r0 + step)
            ms = [mm[g][rr, :] for g in range(nbr)]
            m_all = functools.reduce(jnp.maximum, ms)
            ws = [jnp.exp(m - m_all) for m in ms]
            num = sum(acc[g][rr, :] * ws[g] for g in range(nbr))
            den = sum(dd[g][rr, :] * ws[g] for g in range(nbr))
            o_ref[rr, :] = (num / den).astype(BF16)
            l_ref[rr, :] = m_all + jnp.log(den)

    def col(k):
        return pl.BlockSpec((SEQ, C_PAIR), lambda b, p: (b, k * C_PAIRS + p))

    tab = pl.BlockSpec((SEQ, C_PAIR), lambda b, p: (b, 0))
    return _call(
        body, name=name, grid=(batch, C_PAIRS),
        in_specs=[col(0), col(1), col(2), tab, tab, tab],
        out_specs=[col(0), col(0)],
        out_shape=[jax.ShapeDtypeStruct((t, D_MODEL), BF16), jax.ShapeDtypeStruct((t, D_MODEL), F32)],
        scratch_shapes=[pltpu.VMEM((SEQ, C_PAIR), F32)] * (2 + 3 * nbr),
        args=(qkv, qkv, qkv, cos_t, sin_a, sin_b), exchange=exchange)


def attn_bwd(qkv, cos_t, sin_a, sin_b, o, lse, do, batch, name, exchange=None):
    t = qkv.shape[0]

    def body(q_ref, k_ref, v_ref, c_ref, a_ref, b_ref, o_ref, l_ref, do_ref, dq_ref, dk_ref, dv_ref,
             qs, ks, dqs, dks, dvs, dlt):
        c, a, b = c_ref[...], a_ref[...], b_ref[...]
        qs[...] = _rope(q_ref[...], c, a, b) * C_SCALE
        ks[...] = _rope(k_ref[...], c, a, b)
        prod = do_ref[...] * o_ref[...].astype(F32)
        low = lax.broadcasted_iota(jnp.int32, (1, C_PAIR), 1) < C_HEAD_DIM
        s_low = jnp.sum(jnp.where(low, prod, 0.0), axis=-1, keepdims=True)
        s_all = jnp.sum(prod, axis=-1, keepdims=True)
        dlt[...] = jnp.where(low, s_low, s_all - s_low)
        dqs[...] = jnp.zeros_like(dqs)
        dks[...] = jnp.zeros_like(dks)
        dvs[...] = jnp.zeros_like(dvs)
        def load(idx, dil):
            rows, prev, has_prev = _attn_rows(idx, dil)
            return (rows, prev), (has_prev, qs[rows, :], do_ref[rows, :], ks[rows, :], ks[prev, :],
                                  v_ref[rows, :], v_ref[prev, :], l_ref[rows, :], dlt[rows, :])

        def operands(has_prev, q, do, k_own, k_prev, v_own, v_prev, l_full, d_full):
            lcol = jnp.concatenate([l_full[:, 0:1], l_full[:, C_HEAD_DIM:C_HEAD_DIM + 1]], axis=0)
            dcol = jnp.concatenate([d_full[:, 0:1], d_full[:, C_HEAD_DIM:C_HEAD_DIM + 1]], axis=0)
            return (_stack_heads(q).astype(BF16), _stack_heads(do).astype(BF16),
                    jnp.concatenate([k_own, k_prev], axis=0).astype(BF16),
                    jnp.concatenate([v_own, v_prev], axis=0).astype(BF16), lcol, dcol, _attn_mask(has_prev))

        for dil in C_DILATIONS:
            def pair(i, carry, dil=dil):
                loaded = [load(C_BWD_BLOCKS * i + k, dil) for k in range(C_BWD_BLOCKS)]
                ops = [operands(*o) for _, o in loaded]
                ss = [_dot_nt(q_stack, k_cat) for q_stack, _, k_cat, _, _, _, _ in ops]
                dps = [_dot_nt(do_stack, v_cat) for _, do_stack, _, v_cat, _, _, _ in ops]
                ps = [jnp.exp(jnp.where(o[6], s, NEG_BIG) - o[4]) for s, o in zip(ss, ops)]
                dss = [(p * (dp - o[5])).astype(BF16) for p, dp, o in zip(ps, dps, ops)]
                dvs_ = [_dot_tn(p.astype(BF16), o[1]) for p, o in zip(ps, ops)]
                dks_ = [_dot_tn(ds, o[0]) for ds, o in zip(dss, ops)]
                dqs_ = [_unstack_heads(_dot(ds, o[2])) for ds, o in zip(dss, ops)]
                results = list(zip(dqs_, dks_, dvs_))
                for ((rows, prev), _), (dq, dk_cat, dv_cat) in zip(loaded, results):
                    dqs[rows, :] += dq
                    dks[rows, :] += dk_cat[:C_BLOCK]
                    dvs[rows, :] += dv_cat[:C_BLOCK]
                    dks[prev, :] += dk_cat[C_BLOCK:]
                    dvs[prev, :] += dv_cat[C_BLOCK:]
                return carry

            lax.fori_loop(0, SEQ // C_BLOCK // C_BWD_BLOCKS, pair, 0)
        dq_ref[...] = _rope_t(dqs[...] * C_SCALE, c, a, b).astype(BF16)
        dk_ref[...] = _rope_t(dks[...], c, a, b).astype(BF16)
        dv_ref[...] = dvs[...].astype(BF16)

    def col(k):
        return pl.BlockSpec((SEQ, C_PAIR), lambda b, p: (b, k * C_PAIRS + p))

    tab = pl.BlockSpec((SEQ, C_PAIR), lambda b, p: (b, 0))
    out = jax.ShapeDtypeStruct((t, D_MODEL), BF16)
    return _call(
        body, name=name, grid=(batch, C_PAIRS),
        in_specs=[col(0), col(1), col(2), tab, tab, tab, col(0), col(0), col(0)],
        out_specs=[col(0), col(0), col(0)],
        out_shape=[out, out, out],
        scratch_shapes=[pltpu.VMEM((SEQ, C_PAIR), F32)] * 6,
        args=(qkv, qkv, qkv, cos_t, sin_a, sin_b, o, lse, do), exchange=exchange)


def sibling_swap(arrays, name):
    n = len(arrays)

    def body(*refs):
        ins, outs = refs[:n], refs[n:2 * n]
        send_sems, recv_sems = refs[2 * n:]
        x, y, c, _ = _place()
        sends = []
        for a in range(n):
            cp = pltpu.make_async_remote_copy(
                src_ref=ins[a], dst_ref=outs[a], send_sem=send_sems.at[a], recv_sem=recv_sems.at[a],
                device_id=(x, y, 1 - c), device_id_type=MESH)
            cp.start()
            sends.append(cp)
        for cp in sends:
            cp.wait_recv()
        for cp in sends:
            cp.wait_send()

    return pl.pallas_call(
        body, name=name,
        in_specs=[ANY] * n, out_specs=[ANY] * n,
        out_shape=[jax.ShapeDtypeStruct(s.shape, s.dtype) for s in arrays],
        scratch_shapes=[pltpu.SemaphoreType.DMA((n,)), pltpu.SemaphoreType.DMA((n,))],
    )(*arrays)


def allreduce_small(slab, name):
    rows, lanes = slab.shape

    def body(x_ref, out_ref, gath, send_sems, recv_sems, local_sem):
        x, y, c, chips = _place()
        me, sibling = (x, y, c), (x, y, 1 - c)

        def slot(px, py, pc):
            return gath.at[4 * px + 2 * py + pc]

        def copy(k, block, to, src=None):
            return pltpu.make_async_remote_copy(
                src_ref=slot(*block) if src is None else src, dst_ref=slot(*block),
                send_sem=send_sems.at[k], recv_sem=recv_sems.at[k], device_id=to, device_id_type=MESH)

        mine = pltpu.make_async_copy(x_ref, slot(*me), local_sem)
        mine.start()
        first = [copy(0, me, sibling, src=x_ref)]
        first += [copy(1 + j, me, (*chip, c), src=x_ref) for j, chip in enumerate(chips)]
        for cp in first:
            cp.start()
        passed = [copy(4 + j, (*chip, c), sibling) for j, chip in enumerate(chips)]
        for j, chip in enumerate(chips):
            copy(1 + j, (*chip, c), me).wait_recv()
            passed[j].start()
        copy(0, sibling, me).wait_recv()
        for j, chip in enumerate(chips):
            copy(4 + j, (*chip, 1 - c), me).wait_recv()
        for cp in first + passed:
            cp.wait_send()
        mine.wait()
        total = gath[0]
        for d in range(1, N_DEV):
            total = total + gath[d]
        out_ref[...] = total

    return pl.pallas_call(
        body, name=name,
        in_specs=[pl.BlockSpec(memory_space=pltpu.VMEM)],
        out_specs=pl.BlockSpec(memory_space=pltpu.VMEM),
        out_shape=jax.ShapeDtypeStruct((rows, lanes), F32),
        scratch_shapes=[pltpu.VMEM((N_DEV, rows, lanes), F32),
                        pltpu.SemaphoreType.DMA((7,)), pltpu.SemaphoreType.DMA((7,)), pltpu.SemaphoreType.DMA],
    )(slab)


ELT_ROWS = 512


def reduce_slabs(r, name, part=0, parts=1, into=None):
    _, rows, cols = r.shape
    br = min(rows, ELT_ROWS)
    nblk = rows // br

    def body(r_ref, *rest):
        o_ref = rest[-1]
        o_ref[...] = ((r_ref[3].astype(F32) + r_ref[0].astype(F32)) + r_ref[1].astype(F32)) + r_ref[2].astype(F32)

    return pl.pallas_call(
        body, name=name, grid=(nblk,),
        in_specs=[pl.BlockSpec((N_CHIPS, br, cols), lambda i: (0, i, 0))] + ([] if into is None else [ANY]),
        out_specs=pl.BlockSpec((br, cols), lambda i: (part * nblk + i, 0)),
        out_shape=jax.ShapeDtypeStruct((parts * rows, cols), F32),
        input_output_aliases={} if into is None else {1: 0},
        compiler_params=_params(("arbitrary",)),
    )(*([r] if into is None else [r, into]))


def _adamw(w, g, m, v):
    m = ADAM_B1 * m + (1.0 - ADAM_B1) * g
    v = ADAM_B2 * v + (1.0 - ADAM_B2) * jnp.square(g)
    m_hat = m / (1.0 - ADAM_B1 ** ADAM_STEP)
    v_hat = v / (1.0 - ADAM_B2 ** ADAM_STEP)
    delta = -ADAM_LR * (m_hat / (jnp.sqrt(v_hat) + ADAM_EPS) + ADAM_WD * w)
    return delta, m, v


def adamw_big(w, s_mine, s_sibling, m, v, name):
    rows, cols = w.shape

    def body(w_ref, a_ref, b_ref, m_ref, v_ref, g_out, d_out, m_out, v_out):
        g = a_ref[...] + b_ref[...]
        g_out[...] = g
        d_out[...], m_out[...], v_out[...] = _adamw(w_ref[...], g, m_ref[...], v_ref[...])

    blk = pl.BlockSpec((min(rows, ELT_ROWS), cols), lambda i: (i, 0))
    out = jax.ShapeDtypeStruct((rows, cols), F32)
    return pl.pallas_call(
        body, name=name, grid=(rows // min(rows, ELT_ROWS),),
        in_specs=[blk] * 5, out_specs=[blk] * 4, out_shape=[out] * 4,
        compiler_params=_params(("arbitrary",)),
    )(w, s_mine, s_sibling, m, v)


def adamw_small(ws, gs, ms, vs, name):
    n = len(ws)

    def body(*refs):
        w_refs, g_refs, m_refs, v_refs = (refs[k * n:(k + 1) * n] for k in range(4))
        d_out, m_out, v_out = (refs[(4 + k) * n:(5 + k) * n] for k in range(3))
        for i in range(n):
            d_out[i][...], m_out[i][...], v_out[i][...] = _adamw(
                w_refs[i][...], g_refs[i][...], m_refs[i][...], v_refs[i][...])

    outs = [jax.ShapeDtypeStruct(w.shape, F32) for w in ws]
    res = pl.pallas_call(body, name=name, out_shape=outs * 3)(*ws, *gs, *ms, *vs)
    return res[:n], res[n:2 * n], res[2 * n:]


SLAB_LANES = 128
SLAB_ROW_ALIGN = 8


def _pack(parts):
    flat = jnp.concatenate([p.reshape(-1) for p in parts])
    rows = -(-flat.shape[0] // (SLAB_LANES * SLAB_ROW_ALIGN)) * SLAB_ROW_ALIGN
    flat = jnp.pad(flat, (0, rows * SLAB_LANES - flat.shape[0]))
    return flat.reshape(rows, SLAB_LANES)


def _unpack(slab, shapes):
    flat = slab.reshape(-1)
    out, pos = [], 0
    for s in shapes:
        size = math.prod(s)
        out.append(flat[pos:pos + size].reshape(s))
        pos += size
    return out


def kernel(x, positions, norm_mix_pre, norm_mix_post, norm_ffn_pre, norm_ffn_post, w_in_even, lb_table, a_norm, b_ln_g, b_ln_b, b_ws, b_bias, w_out_even, w_in_odd, w_out_odd, w_ff1, w_ff2, loss_target, m_norm_mix_pre, m_norm_mix_post, m_norm_ffn_pre, m_norm_ffn_post, m_w_in_even, m_lb_table, m_a_norm, m_b_ln_g, m_b_ln_b, m_b_ws, m_b_bias, m_w_out_even, m_w_in_odd, m_w_out_odd, m_w_ff1, m_w_ff2, v_norm_mix_pre, v_norm_mix_post, v_norm_ffn_pre, v_norm_ffn_post, v_w_in_even, v_lb_table, v_a_norm, v_b_ln_g, v_b_ln_b, v_b_ws, v_b_bias, v_w_out_even, v_w_in_odd, v_w_out_odd, v_w_ff1, v_w_ff2):
    batch = x.shape[0]
    t = batch * SEQ
    d = D_MODEL
    x0 = x.reshape(t, d)
    target = loss_target.reshape(t, d)

    def gain(p, layer):
        return p[layer:layer + 1]

    def gather(*shards):
        return _Exchange("gather", [w.astype(BF16) for w in shards])

    def scatter(*grads):
        return _Exchange("scatter", grads)

    (win_e,) = exchange_alone(gather(w_in_even[0]), "gather_in_even")
    bias_t = b_bias[0].T
    proj, h0, w1_0 = norm_matmul(x0, gain(norm_mix_pre, 0), win_e, "in_proj_even", exchange=gather(w_ff1[0]))
    oa, states, decays, w2_0 = hgrn2_fwd(proj, lb_table, a_norm, batch, "hgrn2_fwd", exchange=gather(w_ff2[0]))
    mixin, wout_e = gmlp_fwd(proj, oa, b_ln_g, b_ln_b, b_ws[0], bias_t, "gmlp_fwd", exchange=gather(w_out_even[0]))
    mix0, x1 = out_proj(mixin, wout_e, x0, gain(norm_mix_post, 0), "out_proj_even")
    x2, hf0, a0, y0, win_o, wout_o = ffn_fwd(x1, gain(norm_ffn_pre, 0), w1_0, w2_0, gain(norm_ffn_post, 0),
                                             "ffn_fwd_0", exchange=gather(w_in_odd[0], w_out_odd[0]))
    qkv, h1 = norm_matmul(x2, gain(norm_mix_pre, 1), win_o, "in_proj_odd")
    cos_t, sin_a, sin_b = rope_tables(positions.reshape(t, 1), "rope_tables")
    ao, lse, w1_1, w2_1 = attn_fwd(qkv, cos_t, sin_a, sin_b, batch, "attn_fwd", exchange=gather(w_ff1[1], w_ff2[1]))
    mix1, x3 = out_proj(ao, wout_o, x2, gain(norm_mix_post, 1), "out_proj_odd")
    dx4, hf1, a1, y1, loss_part = ffn_fwd(x3, gain(norm_ffn_pre, 1), w1_1, w2_1, gain(norm_ffn_post, 1),
                                          "ffn_fwd_1", target=target)

    hc = D_FF // N_CHIPS
    dx3, dy1, da1, dg_fpre1, dg_fpost1 = ffn_bwd(
        dx4, x3, y1, a1, gain(norm_ffn_pre, 1), gain(norm_ffn_post, 1), w1_1, w2_1, "ffn_bwd_1")
    g_w1_1 = weight_grad(hf1, da1, "b", d, hc, False, "wgrad_ff1_1")
    g_w2_1 = weight_grad(a1, dy1, "a", hc, d, True, "wgrad_ff2_1")
    dmix1, dao, dg_mpost1 = out_proj_bwd(dx3, mix1, gain(norm_mix_post, 1), wout_o, "out_proj_bwd_odd")
    g_wout_o = weight_grad(ao, dmix1, "a", d // N_CHIPS, d, False, "wgrad_out_odd")
    dq, dk, dv, r_w1_1, r_w2_1, r_wout_o = attn_bwd(qkv, cos_t, sin_a, sin_b, ao, lse, dao, batch, "attn_bwd",
                                                    exchange=scatter(g_w1_1, g_w2_1, g_wout_o))
    dqkv = jnp.concatenate([dq, dk, dv], axis=1)
    dx2, dg_mpre1 = norm_matmul_bwd(dqkv, win_o, x2, gain(norm_mix_pre, 1), dx3, "in_proj_bwd_odd")
    g_win_o = weight_grad(h1, dqkv, "b", d, 3 * d // N_CHIPS, False, "wgrad_in_odd")
    dx1, dy0, da0, dg_fpre0, dg_fpost0, r_win_o = ffn_bwd(
        dx2, x1, y0, a0, gain(norm_ffn_pre, 0), gain(norm_ffn_post, 0), w1_0, w2_0, "ffn_bwd_0",
        exchange=scatter(g_win_o))
    g_w1_0 = weight_grad(hf0, da0, "b", d, hc, False, "wgrad_ff1_0")
    g_w2_0 = weight_grad(a0, dy0, "a", hc, d, True, "wgrad_ff2_0")
    dmix0, dmixin, dg_mpost0 = out_proj_bwd(dx1, mix0, gain(norm_mix_post, 0), wout_e, "out_proj_bwd_even")
    g_wout_e = weight_grad(mixin, dmix0, "a", d // N_CHIPS, d, False, "wgrad_out_even")
    dproj, d_lb, d_anorm, r_w1_0 = hgrn2_bwd(
        proj, states, decays, lb_table, a_norm, dmixin, batch, "hgrn2_bwd", exchange=scatter(g_w1_0))
    dproj, d_lng, d_lnb, d_ws, d_bias_t, r_w2_0 = gmlp_bwd(
        proj, dmixin, b_ln_g, b_ln_b, b_ws[0], bias_t, dproj, "gmlp_bwd", exchange=scatter(g_w2_0))
    g_win_e, r_wout_e = weight_grad(h0, dproj, "b", d, 3 * d // N_CHIPS, False, "wgrad_in_even",
                                    exchange=scatter(g_wout_e))
    dx0, dg_mpre0, r_win_e = norm_matmul_bwd(dproj, win_e, x0, gain(norm_mix_pre, 0), dx1, "in_proj_bwd_even",
                                             exchange=scatter(g_win_e))
    grad_x = dx0.reshape(x.shape)

    s_w1 = reduce_slabs(r_w1_1, "reduce_ff1_1", part=1, parts=2)
    s_w1 = reduce_slabs(r_w1_0, "reduce_ff1_0", part=0, parts=2, into=s_w1)
    s_w2 = reduce_slabs(r_w2_1, "reduce_ff2_1", part=1, parts=2)
    s_w2 = reduce_slabs(r_w2_0, "reduce_ff2_0", part=0, parts=2, into=s_w2)
    sums = [reduce_slabs(r_win_e, "reduce_in_even"), reduce_slabs(r_wout_e, "reduce_out_even"),
            reduce_slabs(r_win_o, "reduce_in_odd"), reduce_slabs(r_wout_o, "reduce_out_odd"), s_w1, s_w2]
    sibling = sibling_swap(sums, "sibling_swap")
    big_w = [w_in_even, w_out_even, w_in_odd, w_out_odd, w_ff1, w_ff2]
    big_m = [m_w_in_even, m_w_out_even, m_w_in_odd, m_w_out_odd, m_w_ff1, m_w_ff2]
    big_v = [v_w_in_even, v_w_out_even, v_w_in_odd, v_w_out_odd, v_w_ff1, v_w_ff2]
    big = []
    for i, (w, m, v) in enumerate(zip(big_w, big_m, big_v)):
        two_d = (-1, w.shape[-1])
        res = adamw_big(w.reshape(two_d), sums[i], sibling[i], m.reshape(two_d), v.reshape(two_d), "adamw_big_%d" % i)
        big.append([r.reshape(w.shape) for r in res])

    small_w = [norm_mix_pre, norm_mix_post, norm_ffn_pre, norm_ffn_post, lb_table, a_norm, b_ln_g, b_ln_b, b_ws, b_bias]
    small_m = [m_norm_mix_pre, m_norm_mix_post, m_norm_ffn_pre, m_norm_ffn_post, m_lb_table, m_a_norm, m_b_ln_g,
               m_b_ln_b, m_b_ws, m_b_bias]
    small_v = [v_norm_mix_pre, v_norm_mix_post, v_norm_ffn_pre, v_norm_ffn_post, v_lb_table, v_a_norm, v_b_ln_g,
               v_b_ln_b, v_b_ws, v_b_bias]
    partial = [jnp.concatenate([dg_mpre0, dg_mpre1]), jnp.concatenate([dg_mpost0, dg_mpost1]),
               jnp.concatenate([dg_fpre0, dg_fpre1]), jnp.concatenate([dg_fpost0, dg_fpost1]),
               d_lb, d_anorm, d_lng, d_lnb, d_ws[None], d_bias_t.T[None]]
    *small_g, loss = _unpack(allreduce_small(_pack(partial + [loss_part]), "allreduce_small"),
                             [w.shape for w in small_w] + [()])
    small_d, small_nm, small_nv = adamw_small(small_w, small_g, small_m, small_v, "adamw_small")

    order = ["norm_mix_pre", "norm_mix_post", "norm_ffn_pre", "norm_ffn_post", "w_in_even", "lb_table", "a_norm",
             "b_ln_g", "b_ln_b", "b_ws", "b_bias", "w_out_even", "w_in_odd", "w_out_odd", "w_ff1", "w_ff2"]
    small_names = ["norm_mix_pre", "norm_mix_post", "norm_ffn_pre", "norm_ffn_post", "lb_table", "a_norm",
                   "b_ln_g", "b_ln_b", "b_ws", "b_bias"]
    big_names = ["w_in_even", "w_out_even", "w_in_odd", "w_out_odd", "w_ff1", "w_ff2"]
    grads, deltas, new_m, new_v = {}, {}, {}, {}
    for i, nm in enumerate(small_names):
        grads[nm], deltas[nm], new_m[nm], new_v[nm] = small_g[i], small_d[i], small_nm[i], small_nv[i]
    for i, nm in enumerate(big_names):
        grads[nm], deltas[nm], new_m[nm], new_v[nm] = big[i]
    return (loss, grad_x, *[grads[n] for n in order], *[deltas[n] for n in order],
            *[new_m[n] for n in order], *[new_v[n] for n in order])
```

```python
import functools
import math

import jax
import jax.numpy as jnp
from jax import lax
from jax.experimental import pallas as pl
from jax.experimental.pallas import tpu as pltpu

F32 = jnp.float32
BF16 = jnp.bfloat16
MESH = pl.DeviceIdType.MESH

D_MODEL = 1024
SEQ = 2048
D_FF = 4096
N_CHIPS = 4
A_WIDTH = 512
A_HEADS = 4
A_DK = 128
A_CHUNK = 64
A_SUB = 16
B_WIDTH = 512
B_GROUPS = 4
B_CHUNK = 128
C_HEADS = 16
C_HEAD_DIM = 64
C_ROT_HALF = 8
C_BLOCK = 128
C_DILATIONS = (1, 4, 16)
ROPE_THETA = 500000.0
EPS = 1e-6
ADAM_LR = 0.001
ADAM_B1 = 0.9
ADAM_B2 = 0.999
ADAM_EPS = 1e-08
ADAM_WD = 0.01
ADAM_STEP = 10

ROW_TILE = 512
FFN_ROWS = 1024
WGRAD_ROWS = 2048
VMEM_LIMIT = 56 * 1024 * 1024
NEG_BIG = -1e30


def _params(sem=None):
    return pltpu.CompilerParams(dimension_semantics=sem, vmem_limit_bytes=VMEM_LIMIT)


def _dot(a, b):
    return jnp.dot(a, b, preferred_element_type=F32)


def _dot_nt(a, b):
    return lax.dot_general(a, b, (((1,), (1,)), ((), ())), preferred_element_type=F32)


def _dot_tn(a, b):
    return lax.dot_general(a, b, (((0,), (0,)), ((), ())), preferred_element_type=F32)


def _rms(x, g):
    r = lax.rsqrt(jnp.mean(x * x, axis=-1, keepdims=True) + EPS)
    return x * r * g


def _rms_bwd(x, g, dy):
    r = lax.rsqrt(jnp.mean(x * x, axis=-1, keepdims=True) + EPS)
    xh = x * r
    dg = jnp.sum(dy * xh, axis=0, keepdims=True)
    dxh = dy * g
    dx = r * (dxh - xh * jnp.mean(dxh * xh, axis=-1, keepdims=True))
    return dx, dg


def _accumulate(ref, val, first):
    @pl.when(first)
    def _():
        ref[...] = val

    @pl.when(jnp.logical_not(first))
    def _():
        ref[...] += val


N_DEV = 8
ANY = pl.BlockSpec(memory_space=pl.ANY)


def _place():
    x, y, c = lax.axis_index("x"), lax.axis_index("y"), lax.axis_index("c")
    return x, y, c, [(1 - x, y), (x, 1 - y), (1 - x, 1 - y)]


class _Exchange:
    def __init__(self, kind, arrays):
        self.kind, self.arrays, self.n = kind, list(arrays), len(arrays)
        per_peer = pltpu.SemaphoreType.DMA((3 * self.n,))
        if kind == "gather":
            self.out_shape = [jax.ShapeDtypeStruct((N_CHIPS,) + a.shape, a.dtype) for a in self.arrays]
            self.scratch = [per_peer, per_peer, pltpu.SemaphoreType.DMA((self.n,)), per_peer, per_peer]
        else:
            self.out_shape = [jax.ShapeDtypeStruct(a.shape, a.dtype) for a in self.arrays]
            self.scratch = [per_peer, per_peer, pltpu.SemaphoreType.DMA((self.n,))]

    def _copies(self, ins, outs, sems):
        send_sems, recv_sems, local_sems = sems[:3]
        x, y, c, chips = _place()
        me = 2 * x + y
        local, remote = [], []
        for a in range(self.n):
            if self.kind == "gather":
                local.append(pltpu.make_async_copy(ins[a], outs[a].at[me], local_sems.at[a]))
                half = self.arrays[a].shape[0] // 2

                def rows(ref, core, half=half):
                    return ref.at[pl.ds(core * half, half)]
            else:
                local.append(pltpu.make_async_copy(ins[a].at[me], outs[a].at[3], local_sems.at[a]))
            for j, (px, py) in enumerate(chips):
                k = 3 * a + j
                peer = 2 * px + py

                def copy(src, dst, to, send_sem=send_sems.at[k], recv_sem=recv_sems.at[k]):
                    return pltpu.make_async_remote_copy(src_ref=src, dst_ref=dst, send_sem=send_sem, recv_sem=recv_sem,
                                                        device_id=to, device_id_type=MESH)

                if self.kind == "gather":
                    sent = copy(rows(ins[a], c), rows(outs[a].at[me], c), (px, py, c))
                    landed = copy(rows(ins[a], c), rows(outs[a].at[peer], c), (px, py, c))
                    on = dict(send_sem=sems[3].at[k], recv_sem=sems[4].at[k])
                    passed = copy(rows(outs[a].at[peer], c), rows(outs[a].at[peer], c), (x, y, 1 - c), **on)
                    handed = copy(rows(outs[a].at[peer], c), rows(outs[a].at[peer], 1 - c), (x, y, 1 - c), **on)
                    remote.append((sent, landed, passed, handed))
                else:
                    sent = copy(ins[a].at[peer], outs[a].at[j], (px, py, c))
                    remote.append((sent, sent, None, None))
        return local, remote

    def start(self, ins, outs, sems):
        local, remote = self._copies(ins, outs, sems)
        for cp in local:
            cp.start()
        for sent, _, _, _ in remote:
            sent.start()

    def finish(self, ins, outs, sems):
        local, remote = self._copies(ins, outs, sems)
        for _, landed, passed, _ in remote:
            landed.wait_recv()
            if passed is not None:
                passed.start()
        for sent, _, passed, handed in remote:
            if passed is not None:
                handed.wait_recv()
                passed.wait_send()
            sent.wait_send()
        for cp in local:
            cp.wait()


def _call(body, *, name, grid, in_specs, out_specs, out_shape, args, scratch_shapes=(), aliases=None, exchange=None):
    if exchange is None:
        return pl.pallas_call(
            body, name=name, grid=grid, in_specs=in_specs, out_specs=out_specs, out_shape=out_shape,
            scratch_shapes=list(scratch_shapes), input_output_aliases=aliases or {},
            compiler_params=_params(("arbitrary",) * len(grid)))(*args)
    n_in, n_out, n_scr, n_ex = len(in_specs), len(out_specs), len(scratch_shapes), exchange.n
    steps = grid

    def wrapped(*refs):
        ins, refs = refs[:n_in], refs[n_in:]
        ex_in, refs = refs[:n_ex], refs[n_ex:]
        outs, refs = refs[:n_out], refs[n_out:]
        ex_out, refs = refs[:n_ex], refs[n_ex:]
        scr, sems = refs[:n_scr], refs[n_scr:]
        first = functools.reduce(jnp.logical_and, [pl.program_id(k) == 0 for k in range(len(steps))])
        last = functools.reduce(jnp.logical_and, [pl.program_id(k) == steps[k] - 1 for k in range(len(steps))])

        @pl.when(first)
        def _():
            exchange.start(ex_in, ex_out, sems)

        body(*ins, *outs, *scr)

        @pl.when(last)
        def _():
            exchange.finish(ex_in, ex_out, sems)

    return pl.pallas_call(
        wrapped, name=name, grid=grid,
        in_specs=list(in_specs) + [ANY] * n_ex, out_specs=list(out_specs) + [ANY] * n_ex,
        out_shape=list(out_shape) + exchange.out_shape,
        scratch_shapes=list(scratch_shapes) + exchange.scratch, input_output_aliases=aliases or {},
        compiler_params=_params(("arbitrary",) * len(grid)))(*args, *exchange.arrays)


def exchange_alone(exchange, name):
    def body(*refs):
        n = exchange.n
        exchange.start(refs[:n], refs[n:2 * n], refs[2 * n:])
        exchange.finish(refs[:n], refs[n:2 * n], refs[2 * n:])

    return pl.pallas_call(
        body, name=name, in_specs=[ANY] * exchange.n, out_specs=[ANY] * exchange.n,
        out_shape=exchange.out_shape, scratch_shapes=exchange.scratch)(*exchange.arrays)


def norm_matmul(x, g, wg, name, exchange=None):
    t, d = x.shape
    nl = wg.shape[2]

    def body(x_ref, g_ref, w_ref, o_ref, h_ref):
        h = _rms(x_ref[...], g_ref[...]).astype(BF16)
        h_ref[...] = h
        for c in range(N_CHIPS):
            o_ref[:, c * nl:(c + 1) * nl] = _dot(h, w_ref[c])

    return _call(
        body, name=name, grid=(t // ROW_TILE,),
        in_specs=[pl.BlockSpec((ROW_TILE, d), lambda i: (i, 0)),
                  pl.BlockSpec((1, d), lambda i: (0, 0)),
                  pl.BlockSpec((N_CHIPS, d, nl), lambda i: (0, 0, 0))],
        out_specs=[pl.BlockSpec((ROW_TILE, N_CHIPS * nl), lambda i: (i, 0)),
                   pl.BlockSpec((ROW_TILE, d), lambda i: (i, 0))],
        out_shape=[jax.ShapeDtypeStruct((t, N_CHIPS * nl), F32), jax.ShapeDtypeStruct((t, d), BF16)],
        args=(x, g, wg), exchange=exchange)


def norm_matmul_bwd(dproj, wg, x, g, dres, name, exchange=None):
    t, d = x.shape
    nl = wg.shape[2]

    def body(dp_ref, w_ref, x_ref, g_ref, dres_ref, dx_ref, dg_ref):
        dh = _dot_nt(dp_ref[:, 0:nl].astype(BF16), w_ref[0])
        for c in range(1, N_CHIPS):
            dh += _dot_nt(dp_ref[:, c * nl:(c + 1) * nl].astype(BF16), w_ref[c])
        dx, dg = _rms_bwd(x_ref[...], g_ref[...], dh)
        dx_ref[...] = dres_ref[...] + dx
        _accumulate(dg_ref, dg, pl.program_id(0) == 0)

    row = pl.BlockSpec((ROW_TILE, d), lambda i: (i, 0))
    vec = pl.BlockSpec((1, d), lambda i: (0, 0))
    return _call(
        body, name=name, grid=(t // ROW_TILE,),
        in_specs=[pl.BlockSpec((ROW_TILE, N_CHIPS * nl), lambda i: (i, 0)),
                  pl.BlockSpec((N_CHIPS, d, nl), lambda i: (0, 0, 0)), row, vec, row],
        out_specs=[row, vec],
        out_shape=[jax.ShapeDtypeStruct((t, d), F32), jax.ShapeDtypeStruct((1, d), F32)],
        args=(dproj, wg, x, g, dres), exchange=exchange)


def out_proj(a, wg, x, g, name):
    t, d = x.shape
    kl = wg.shape[1]

    def body(a_ref, w_ref, x_ref, g_ref, mix_ref, xo_ref):
        acc = _dot(a_ref[:, 0:kl], w_ref[0])
        for c in range(1, N_CHIPS):
            acc += _dot(a_ref[:, c * kl:(c + 1) * kl], w_ref[c])
        mix_ref[...] = acc
        xo_ref[...] = x_ref[...] + _rms(acc, g_ref[...])

    row = pl.BlockSpec((ROW_TILE, d), lambda i: (i, 0))
    return pl.pallas_call(
        body, name=name, grid=(t // ROW_TILE,),
        in_specs=[row, pl.BlockSpec((N_CHIPS, kl, d), lambda i: (0, 0, 0)), row,
                  pl.BlockSpec((1, d), lambda i: (0, 0))],
        out_specs=[row, row],
        out_shape=[jax.ShapeDtypeStruct((t, d), F32), jax.ShapeDtypeStruct((t, d), F32)],
        compiler_params=_params(("arbitrary",)),
    )(a, wg, x, g)


def out_proj_bwd(dxo, mix, g, wg, name):
    t, d = mix.shape
    kl = wg.shape[1]

    def body(dxo_ref, mix_ref, g_ref, w_ref, dmix_ref, da_ref, dg_ref):
        dmix, dg = _rms_bwd(mix_ref[...], g_ref[...], dxo_ref[...])
        dmb = dmix.astype(BF16)
        dmix_ref[...] = dmb
        for c in range(N_CHIPS):
            da_ref[:, c * kl:(c + 1) * kl] = _dot_nt(dmb, w_ref[c])
        _accumulate(dg_ref, dg, pl.program_id(0) == 0)

    row = pl.BlockSpec((ROW_TILE, d), lambda i: (i, 0))
    vec = pl.BlockSpec((1, d), lambda i: (0, 0))
    return pl.pallas_call(
        body, name=name, grid=(t // ROW_TILE,),
        in_specs=[row, row, vec, pl.BlockSpec((N_CHIPS, kl, d), lambda i: (0, 0, 0))],
        out_specs=[row, row, vec],
        out_shape=[jax.ShapeDtypeStruct((t, d), BF16), jax.ShapeDtypeStruct((t, d), F32),
                   jax.ShapeDtypeStruct((1, d), F32)],
        compiler_params=_params(("arbitrary",)),
    )(dxo, mix, g, wg)


def ffn_fwd(x, gpre, w1g, w2g, gpost, name, exchange=None, target=None):
    t, d = x.shape
    hc = w1g.shape[2]
    with_loss = target is not None

    def body(x_ref, gpre_ref, w1_ref, w2_ref, gpost_ref, *rest):
        if with_loss:
            t_ref, xo_ref, h_ref, a_ref, y_ref, l_ref, acc = rest
        else:
            xo_ref, h_ref, a_ref, y_ref, acc = rest
        i, c = pl.program_id(0), pl.program_id(1)

        @pl.when(c == 0)
        def _():
            h_ref[...] = _rms(x_ref[...], gpre_ref[...]).astype(BF16)

        a = _dot(h_ref[...], w1_ref[...])
        a_ref[...] = a.astype(BF16)
        r = jnp.square(jnp.maximum(a, 0.0)).astype(BF16)
        _accumulate(acc, _dot(r, w2_ref[...]), c == 0)

        @pl.when(c == N_CHIPS - 1)
        def _():
            y = acc[...]
            y_ref[...] = y
            xo = x_ref[...] + _rms(y, gpost_ref[...])
            if with_loss:
                e = xo - t_ref[...]
                xo_ref[...] = e * (1.0 / d)
                part = jnp.sum(jnp.sum(e * e, axis=-1, keepdims=True), axis=0, keepdims=True) * (0.5 / d)
                _accumulate(l_ref, part, i == 0)
            else:
                xo_ref[...] = xo

    row = pl.BlockSpec((FFN_ROWS, d), lambda i, c: (i, 0))
    vec = pl.BlockSpec((1, d), lambda i, c: (0, 0))
    one = pl.BlockSpec((1, 1), lambda i, c: (0, 0))
    return _call(
        body, name=name, grid=(t // FFN_ROWS, N_CHIPS),
        in_specs=[row, vec,
                  pl.BlockSpec((None, d, hc), lambda i, c: (c, 0, 0)),
                  pl.BlockSpec((None, hc, d), lambda i, c: (c, 0, 0)), vec] + ([row] if with_loss else []),
        out_specs=[row, row, pl.BlockSpec((FFN_ROWS, hc), lambda i, c: (i, c)), row] + ([one] if with_loss else []),
        out_shape=[jax.ShapeDtypeStruct((t, d), F32), jax.ShapeDtypeStruct((t, d), BF16),
                   jax.ShapeDtypeStruct((t, N_CHIPS * hc), BF16), jax.ShapeDtypeStruct((t, d), F32)]
        + ([jax.ShapeDtypeStruct((1, 1), F32)] if with_loss else []),
        scratch_shapes=[pltpu.VMEM((FFN_ROWS, d), F32)],
        args=(x, gpre, w1g, w2g, gpost) + ((target,) if with_loss else ()), exchange=exchange)


def ffn_bwd(dxo, x, y, a, gpre, gpost, w1g, w2g, name, exchange=None):
    t, d = x.shape
    hc = w1g.shape[2]

    def body(dxo_ref, x_ref, y_ref, a_ref, gpre_ref, gpost_ref, w1_ref, w2_ref,
             dxi_ref, dy_ref, da_ref, dgpre_ref, dgpost_ref, acc):
        i, c = pl.program_id(0), pl.program_id(1)

        @pl.when(c == 0)
        def _():
            dy, dg = _rms_bwd(y_ref[...], gpost_ref[...], dxo_ref[...])
            dy_ref[...] = dy.astype(BF16)
            _accumulate(dgpost_ref, dg, i == 0)

        dr = _dot_nt(dy_ref[...], w2_ref[...])
        da = (dr * (2.0 * jnp.maximum(a_ref[...].astype(F32), 0.0))).astype(BF16)
        da_ref[...] = da
        _accumulate(acc, _dot_nt(da, w1_ref[...]), c == 0)

        @pl.when(c == N_CHIPS - 1)
        def _():
            dx, dg = _rms_bwd(x_ref[...], gpre_ref[...], acc[...])
            dxi_ref[...] = dxo_ref[...] + dx
            _accumulate(dgpre_ref, dg, i == 0)

    row = pl.BlockSpec((ROW_TILE, d), lambda i, c: (i, 0))
    vec = pl.BlockSpec((1, d), lambda i, c: (0, 0))
    hid = pl.BlockSpec((ROW_TILE, hc), lambda i, c: (i, c))
    return _call(
        body, name=name, grid=(t // ROW_TILE, N_CHIPS),
        in_specs=[row, row, row, hid, vec, vec,
                  pl.BlockSpec((None, d, hc), lambda i, c: (c, 0, 0)),
                  pl.BlockSpec((None, hc, d), lambda i, c: (c, 0, 0))],
        out_specs=[row, row, hid, vec, vec],
        out_shape=[jax.ShapeDtypeStruct((t, d), F32), jax.ShapeDtypeStruct((t, d), BF16),
                   jax.ShapeDtypeStruct((t, N_CHIPS * hc), BF16),
                   jax.ShapeDtypeStruct((1, d), F32), jax.ShapeDtypeStruct((1, d), F32)],
        scratch_shapes=[pltpu.VMEM((ROW_TILE, d), F32)],
        args=(dxo, x, y, a, gpre, gpost, w1g, w2g), exchange=exchange)


def weight_grad(a, b, chunked, bk, bn, relu2, name, exchange=None):
    t = a.shape[0]
    a_on = chunked == "a"
    rows = min(t, WGRAD_ROWS)
    n_steps = t // rows

    def body(a_ref, b_ref, o_ref, acc):
        s = pl.program_id(1)
        av = a_ref[...]
        if relu2:
            av = jnp.square(jnp.maximum(av.astype(F32), 0.0))
        _accumulate(acc, _dot_tn(av.astype(BF16), b_ref[...].astype(BF16)), s == 0)

        @pl.when(s == n_steps - 1)
        def _():
            o_ref[...] = acc[...].astype(BF16)

    res = _call(
        body, name=name, grid=(N_CHIPS, n_steps),
        in_specs=[pl.BlockSpec((rows, bk), (lambda c, s: (s, c)) if a_on else (lambda c, s: (s, 0))),
                  pl.BlockSpec((rows, bn), (lambda c, s: (s, 0)) if a_on else (lambda c, s: (s, c)))],
        out_specs=[pl.BlockSpec((None, bk, bn), lambda c, s: (c, 0, 0))],
        out_shape=[jax.ShapeDtypeStruct((N_CHIPS, bk, bn), BF16)],
        scratch_shapes=[pltpu.VMEM((bk, bn), F32)],
        args=(a, b), exchange=exchange)
    return res[0] if exchange is None else res


def _hgrn2_chunk(st, qs, fls, ivs, gls, l0, l1, l2, ng):
    nsub = len(qs)
    mx = jnp.maximum(jnp.maximum(l0, l1), l2)
    e0, e1, e2 = jnp.exp(l0 - mx), jnp.exp(l1 - mx), jnp.exp(l2 - mx)
    lb = e0 / (e0 + e1 + e2)
    rows = lax.broadcasted_iota(jnp.int32, (A_SUB, A_SUB), 0)
    cols = lax.broadcasted_iota(jnp.int32, (A_SUB, A_SUB), 1)
    tri = (rows >= cols).astype(F32)
    keep = (lax.broadcasted_iota(jnp.int32, (A_SUB, A_SUB, A_DK), 0)
            >= lax.broadcasted_iota(jnp.int32, (A_SUB, A_SUB, A_DK), 1))
    base = jnp.zeros_like(l0)
    bases, gs, ks, qfs = [], [], [], []
    for i in range(nsub):
        f = lb + (1.0 - lb) * jax.nn.sigmoid(fls[i])
        logf = jnp.log(f)
        bases.append(base)
        gs.append(base + jnp.dot(tri, logf, precision=lax.Precision.HIGHEST, preferred_element_type=F32))
        base = base + jnp.sum(logf, axis=0, keepdims=True)
        ks.append(1.0 - f)
        qfs.append(jax.nn.silu(qs[i]))
    g_last = base
    stb = st.astype(BF16)
    outs = []
    for i in range(nsub):
        o = _dot_nt((qfs[i] * jnp.exp(gs[i])).astype(BF16), stb)
        if i > 0:
            qt = (qfs[i] * jnp.exp(gs[i] - bases[i])).astype(BF16)
            kk = jnp.concatenate([ks[j] * jnp.exp(bases[i] - gs[j]) for j in range(i)], axis=0).astype(BF16)
            vv = jnp.concatenate(ivs[:i], axis=0).astype(BF16)
            o = o + _dot(_dot_nt(qt, kk).astype(BF16), vv)
        dec = jnp.exp(jnp.where(keep, gs[i][:, None, :] - gs[i][None, :, :], NEG_BIG))
        s_diag = jnp.sum(qfs[i][:, None, :] * ks[i][None, :, :] * dec, axis=-1)
        o = o + _dot(s_diag.astype(BF16), ivs[i].astype(BF16))
        o = o * lax.rsqrt(jnp.mean(o * o, axis=-1, keepdims=True) + EPS) * ng
        outs.append(o * jax.nn.silu(gls[i]))
    kdec = jnp.concatenate([ks[j] * jnp.exp(g_last - gs[j]) for j in range(nsub)], axis=0).astype(BF16)
    vall = jnp.concatenate(ivs, axis=0).astype(BF16)
    new_st = st * jnp.exp(g_last) + _dot_tn(vall, kdec)
    return new_st, outs


A_MAX_LOG_DECAY = 60.0


def _half_sums(logf):
    n = logf.shape[0]
    first = lax.broadcasted_iota(jnp.int32, logf.shape, 0) < n // 2
    return (jnp.sum(jnp.where(first, logf, 0.0), axis=0, keepdims=True),
            jnp.sum(jnp.where(first, 0.0, logf), axis=0, keepdims=True))


def _split3(x):
    hi = x.astype(BF16)
    r1 = x - hi.astype(F32)
    mid = r1.astype(BF16)
    return hi, mid, (r1 - mid.astype(F32)).astype(BF16)


def _tri_matmul(x, transpose):
    n = x.shape[0]
    r = lax.broadcasted_iota(jnp.int32, (n, n), 0)
    c = lax.broadcasted_iota(jnp.int32, (n, n), 1)
    tri = ((r <= c) if transpose else (r >= c)).astype(BF16)
    hi, mid, lo = _split3(x)
    return (_dot(tri, lo) + _dot(tri, mid)) + _dot(tri, hi)


@jax.custom_vjp
def _cumsum_rows(x):
    return _tri_matmul(x, False)


def _cumsum_rows_fwd(x):
    return _tri_matmul(x, False), None


def _cumsum_rows_bwd(_, dy):
    return (_tri_matmul(dy, True),)


_cumsum_rows.defvjp(_cumsum_rows_fwd, _cumsum_rows_bwd)


def _lower_bound(l0, l1, l2):
    mx = jnp.maximum(jnp.maximum(l0, l1), l2)
    e0, e1, e2 = jnp.exp(l0 - mx), jnp.exp(l1 - mx), jnp.exp(l2 - mx)
    return e0 / (e0 + e1 + e2)


def _b(x):
    return x.astype(BF16)


@jax.custom_vjp
def _mm(a, b):
    return _dot(_b(a), _b(b))


_mm.defvjp(lambda a, b: (_mm(a, b), (a, b)),
           lambda res, d: (_dot_nt(_b(d), _b(res[1])), _dot_tn(_b(res[0]), _b(d))))


@jax.custom_vjp
def _mm_nt(a, b):
    return _dot_nt(_b(a), _b(b))


_mm_nt.defvjp(lambda a, b: (_mm_nt(a, b), (a, b)),
              lambda res, d: (_dot(_b(d), _b(res[1])), _dot_tn(_b(d), _b(res[0]))))


def _dot_split(dot, a, b):
    ah, bh = _b(a), _b(b)
    al, bl = _b(a - ah.astype(F32)), _b(b - bh.astype(F32))
    return (dot(ah, bl) + dot(al, bh)) + dot(ah, bh)


@jax.custom_vjp
def _mm_scores(a, b):
    return _dot_nt(_b(a), _b(b))


_mm_scores.defvjp(lambda a, b: (_mm_scores(a, b), (a, b)),
                  lambda res, d: (_dot_split(_dot, d, res[1]), _dot_split(_dot_tn, d, res[0])))


@jax.custom_vjp
def _mm_tn(a, b):
    return _dot_tn(_b(a), _b(b))


_mm_tn.defvjp(lambda a, b: (_mm_tn(a, b), (a, b)),
              lambda res, d: (_dot_nt(_b(res[1]), _b(d)), _dot(_b(res[0]), _b(d))))


@jax.custom_vjp
def _split_heads(x):
    return tuple(x[:, h * A_DK:(h + 1) * A_DK] for h in range(A_HEADS))


def _split_heads_fwd(x):
    return _split_heads(x), None


def _split_heads_bwd(_, parts):
    return (jnp.concatenate(parts, axis=1),)


_split_heads.defvjp(_split_heads_fwd, _split_heads_bwd)


def _hgrn2_chunk_fast(sts, q, fl, iv, gl, l0, l1, l2, ng):
    lb = _lower_bound(l0, l1, l2)
    f = lb + (1.0 - lb) * jax.nn.sigmoid(fl)
    return _hgrn2_fast_core(sts, q, f, jnp.log(f), iv, gl, ng)


def _hgrn2_fast_core(sts, q, f, logf, iv, gl, ng):
    g = _cumsum_rows(logf)
    g_mid, g_last = _half_sums(logf)
    g_last = g_mid + g_last
    k = 1.0 - f
    qf = jax.nn.silu(q)
    qms = _split_heads(qf * jnp.exp(g - g_mid))
    kms = _split_heads(k * jnp.exp(g_mid - g))
    qgs = _split_heads(qf * jnp.exp(g))
    kds = _split_heads(k * jnp.exp(g_last - g))
    ivs = _split_heads(iv)
    decays = _split_heads(jnp.exp(g_last))
    n = q.shape[0]
    causal = lax.broadcasted_iota(jnp.int32, (n, n), 0) >= lax.broadcasted_iota(jnp.int32, (n, n), 1)
    raw = [_mm_scores(qm, km) for qm, km in zip(qms, kms)]
    inter = [_mm_nt(qg, st) for qg, st in zip(qgs, sts)]
    scores = [jnp.where(causal, s, 0.0) for s in raw]
    os = [a + _mm(s, v) for a, s, v in zip(inter, scores, ivs)]
    new_sts = [st * d + _mm_tn(v, kd) for st, d, v, kd in zip(sts, decays, ivs, kds)]
    os = [o * lax.rsqrt(jnp.mean(o * o, axis=-1, keepdims=True) + EPS) for o in os]
    return new_sts, jnp.concatenate(os, axis=1) * ng * jax.nn.silu(gl)


A_STEP_CHUNKS = 4


def _chunk_rows(j):
    return pl.ds(pl.multiple_of(j * A_CHUNK, A_CHUNK), A_CHUNK)


def _sub_rows(j, i):
    return pl.ds(pl.multiple_of(j * A_CHUNK + i * A_SUB, A_SUB), A_SUB)


def _sub_blocks(ref, head, j):
    lanes = slice(head * A_DK, (head + 1) * A_DK)
    return [ref[_sub_rows(j, i), lanes] for i in range(A_CHUNK // A_SUB)]


def hgrn2_fwd(proj, lb_table, a_norm, batch, name, exchange=None):
    t = proj.shape[0]
    n_steps = t // batch // (A_CHUNK * A_STEP_CHUNKS)
    rows = A_CHUNK * A_STEP_CHUNKS

    def body(q_ref, f_ref, i_ref, g_ref, lb_ref, ng_ref, o_ref, st_ref, dec_ref, st):
        @pl.when(pl.program_id(1) == 0)
        def _():
            st[...] = jnp.zeros_like(st)

        def chunk(j, carry):
            r = _chunk_rows(j)
            st_ref[j] = st[...]
            lb = _lower_bound(lb_ref[0:1, :], lb_ref[1:2, :], lb_ref[2:3, :])
            f = lb + (1.0 - lb) * jax.nn.sigmoid(f_ref[r, :])
            logf = jnp.log(f)
            decay = jnp.minimum(*_half_sums(logf))
            dec_ref[j] = decay
            mild = jnp.min(decay) >= -A_MAX_LOG_DECAY

            @pl.when(mild)
            def _():
                new_sts, o = _hgrn2_fast_core([st[h] for h in range(A_HEADS)], q_ref[r, :], f, logf,
                                              i_ref[r, :], g_ref[r, :], ng_ref[...])
                for h in range(A_HEADS):
                    st[h] = new_sts[h]
                o_ref[r, :] = o.astype(BF16)

            @pl.when(jnp.logical_not(mild))
            def _():
                for h in range(A_HEADS):
                    lanes = slice(h * A_DK, (h + 1) * A_DK)
                    new_st, outs = _hgrn2_chunk(
                        st[h], _sub_blocks(q_ref, h, j), _sub_blocks(f_ref, h, j), _sub_blocks(i_ref, h, j),
                        _sub_blocks(g_ref, h, j), lb_ref[0:1, lanes], lb_ref[1:2, lanes], lb_ref[2:3, lanes],
                        ng_ref[:, lanes])
                    st[h] = new_st
                    for i, o in enumerate(outs):
                        o_ref[_sub_rows(j, i), lanes] = o.astype(BF16)

            return carry

        lax.fori_loop(0, A_STEP_CHUNKS, chunk, 0)

    def part(k):
        return pl.BlockSpec((rows, A_WIDTH), lambda b, n: (b * n_steps + n, k))

    return _call(
        body, name=name, grid=(batch, n_steps),
        in_specs=[part(0), part(1), part(2), part(3),
                  pl.BlockSpec((3, A_WIDTH), lambda b, n: (0, 0)), pl.BlockSpec((1, A_WIDTH), lambda b, n: (0, 0))],
        out_specs=[part(0),
                   pl.BlockSpec((A_STEP_CHUNKS, A_HEADS, A_DK, A_DK), lambda b, n: (b * n_steps + n, 0, 0, 0)),
                   pl.BlockSpec((A_STEP_CHUNKS, 1, A_WIDTH), lambda b, n: (b * n_steps + n, 0, 0))],
        out_shape=[jax.ShapeDtypeStruct((t, A_WIDTH), BF16),
                   jax.ShapeDtypeStruct((t // A_CHUNK, A_HEADS, A_DK, A_DK), F32),
                   jax.ShapeDtypeStruct((t // A_CHUNK, 1, A_WIDTH), F32)],
        scratch_shapes=[pltpu.VMEM((A_HEADS, A_DK, A_DK), F32)],
        args=(proj, proj, proj, proj, lb_table, a_norm), exchange=exchange)


def hgrn2_bwd(proj, states, decays, lb_table, a_norm, do, batch, name, exchange=None):
    t = proj.shape[0]
    n_steps = t // batch // (A_CHUNK * A_STEP_CHUNKS)
    rows = A_CHUNK * A_STEP_CHUNKS

    def body(q_ref, f_ref, i_ref, g_ref, st_ref, dec_ref, lb_ref, ng_ref, do_ref, dp_ref, dlb_ref, dng_ref, dst):
        @pl.when(jnp.logical_and(pl.program_id(0) == 0, pl.program_id(1) == 0))
        def _():
            dlb_ref[...] = jnp.zeros_like(dlb_ref)
            dng_ref[...] = jnp.zeros_like(dng_ref)

        @pl.when(pl.program_id(1) == 0)
        def _():
            dst[...] = jnp.zeros_like(dst)

        def chunk(jj, carry):
            j = A_STEP_CHUNKS - 1 - jj
            r = _chunk_rows(j)
            mild = jnp.min(dec_ref[j]) >= -A_MAX_LOG_DECAY

            @pl.when(mild)
            def _():
                _, vjp = jax.vjp(
                    _hgrn2_chunk_fast, [st_ref[j, h] for h in range(A_HEADS)], q_ref[r, :], f_ref[r, :],
                    i_ref[r, :], g_ref[r, :], lb_ref[0:1, :], lb_ref[1:2, :], lb_ref[2:3, :], ng_ref[...])
                d_sts, dq, df, di, dg, dl0, dl1, dl2, dng = vjp(
                    ([dst[h] for h in range(A_HEADS)], do_ref[r, :].astype(F32)))
                for h in range(A_HEADS):
                    dst[h] = d_sts[h]
                for k, part in enumerate((dq, df, di, dg)):
                    dp_ref[r, k * A_WIDTH:(k + 1) * A_WIDTH] = part
                for row, val in enumerate((dl0, dl1, dl2)):
                    dlb_ref[row:row + 1, :] += val
                dng_ref[...] += dng

            @pl.when(jnp.logical_not(mild))
            def _():
                for h in range(A_HEADS):
                    lanes = slice(h * A_DK, (h + 1) * A_DK)
                    _, vjp = jax.vjp(
                        _hgrn2_chunk, st_ref[j, h], _sub_blocks(q_ref, h, j), _sub_blocks(f_ref, h, j),
                        _sub_blocks(i_ref, h, j), _sub_blocks(g_ref, h, j), lb_ref[0:1, lanes], lb_ref[1:2, lanes],
                        lb_ref[2:3, lanes], ng_ref[:, lanes])
                    douts = [x.astype(F32) for x in _sub_blocks(do_ref, h, j)]
                    d_st, dqs, dfs, dis, dgs, dl0, dl1, dl2, dng = vjp((dst[h], douts))
                    dst[h] = d_st
                    for k, parts in enumerate((dqs, dfs, dis, dgs)):
                        for i in range(A_CHUNK // A_SUB):
                            dp_ref[_sub_rows(j, i), k * A_WIDTH + h * A_DK:k * A_WIDTH + (h + 1) * A_DK] = parts[i]
                    for row, val in enumerate((dl0, dl1, dl2)):
                        dlb_ref[row:row + 1, lanes] += val
                    dng_ref[:, lanes] += dng

            return carry

        lax.fori_loop(0, A_STEP_CHUNKS, chunk, 0)

    def rev(b, n):
        return b * n_steps + (n_steps - 1 - n)

    def part(k):
        return pl.BlockSpec((rows, A_WIDTH), lambda b, n: (rev(b, n), k))

    const3 = pl.BlockSpec((3, A_WIDTH), lambda b, n: (0, 0))
    const1 = pl.BlockSpec((1, A_WIDTH), lambda b, n: (0, 0))
    return _call(
        body, name=name, grid=(batch, n_steps),
        in_specs=[part(0), part(1), part(2), part(3),
                  pl.BlockSpec((A_STEP_CHUNKS, A_HEADS, A_DK, A_DK), lambda b, n: (rev(b, n), 0, 0, 0)),
                  pl.BlockSpec((A_STEP_CHUNKS, 1, A_WIDTH), lambda b, n: (rev(b, n), 0, 0)),
                  const3, const1, part(0)],
        out_specs=[pl.BlockSpec((rows, 4 * A_WIDTH), lambda b, n: (rev(b, n), 0)), const3, const1],
        out_shape=[jax.ShapeDtypeStruct((t, 4 * A_WIDTH + 2 * B_WIDTH), F32),
                   jax.ShapeDtypeStruct((3, A_WIDTH), F32), jax.ShapeDtypeStruct((1, A_WIDTH), F32)],
        scratch_shapes=[pltpu.VMEM((A_HEADS, A_DK, A_DK), F32)],
        args=(proj, proj, proj, proj, states, decays, lb_table, a_norm, do), exchange=exchange)


B_GDIM = B_WIDTH // B_GROUPS
B_ROWS = 512


def _gmlp_chunk(ubs, vbs, lngs, lnbs, ws, bcols):
    vs = [jax.nn.gelu(v) for v in vbs]
    mu = sum(jnp.sum(v, axis=-1, keepdims=True) for v in vs) * (1.0 / B_WIDTH)
    var = sum(jnp.sum(jnp.square(v - mu), axis=-1, keepdims=True) for v in vs) * (1.0 / B_WIDTH)
    rstd = lax.rsqrt(var + EPS)
    tril = (lax.broadcasted_iota(jnp.int32, (B_CHUNK, B_CHUNK), 0)
            >= lax.broadcasted_iota(jnp.int32, (B_CHUNK, B_CHUNK), 1))
    outs = []
    for g in range(B_GROUPS):
        vn = (vs[g] - mu) * rstd * lngs[g] + lnbs[g]
        w = jnp.where(tril, ws[g], 0.0).astype(BF16)
        outs.append(jax.nn.gelu(ubs[g]) * (_dot(w, vn.astype(BF16)) + bcols[g]))
    return outs


def _gmlp_args(u_ref, v_ref, lng_ref, lnb_ref, w_ref, bt_ref, rows):
    def groups(ref):
        return [ref[rows, g * B_GDIM:(g + 1) * B_GDIM] for g in range(B_GROUPS)]

    def vec(ref):
        return [ref[:, g * B_GDIM:(g + 1) * B_GDIM] for g in range(B_GROUPS)]

    return (groups(u_ref), groups(v_ref), vec(lng_ref), vec(lnb_ref),
            [w_ref[g] for g in range(B_GROUPS)], [bt_ref[:, g:g + 1] for g in range(B_GROUPS)])


def gmlp_fwd(proj, oa, ln_g, ln_b, w, bias_t, name, exchange=None):
    t = proj.shape[0]

    def body(u_ref, v_ref, oa_ref, lng_ref, lnb_ref, w_ref, bt_ref, o_ref):
        o_ref[:, 0:A_WIDTH] = oa_ref[...]
        for n in range(B_ROWS // B_CHUNK):
            rows = slice(n * B_CHUNK, (n + 1) * B_CHUNK)
            outs = _gmlp_chunk(*_gmlp_args(u_ref, v_ref, lng_ref, lnb_ref, w_ref, bt_ref, rows))
            for g, o in enumerate(outs):
                o_ref[rows, A_WIDTH + g * B_GDIM:A_WIDTH + (g + 1) * B_GDIM] = o.astype(BF16)

    vec = pl.BlockSpec((1, B_WIDTH), lambda i: (0, 0))
    return _call(
        body, name=name, grid=(t // B_ROWS,),
        in_specs=[pl.BlockSpec((B_ROWS, B_WIDTH), lambda i: (i, 4)), pl.BlockSpec((B_ROWS, B_WIDTH), lambda i: (i, 5)),
                  pl.BlockSpec((B_ROWS, A_WIDTH), lambda i: (i, 0)), vec, vec,
                  pl.BlockSpec((B_GROUPS, B_CHUNK, B_CHUNK), lambda i: (0, 0, 0)),
                  pl.BlockSpec((B_CHUNK, B_GROUPS), lambda i: (0, 0))],
        out_specs=[pl.BlockSpec((B_ROWS, A_WIDTH + B_WIDTH), lambda i: (i, 0))],
        out_shape=[jax.ShapeDtypeStruct((t, A_WIDTH + B_WIDTH), BF16)],
        args=(proj, proj, oa, ln_g, ln_b, w, bias_t), exchange=exchange)


def gmlp_bwd(proj, dmixin, ln_g, ln_b, w, bias_t, dproj, name, exchange=None):
    t = proj.shape[0]

    def body(u_ref, v_ref, do_ref, lng_ref, lnb_ref, w_ref, bt_ref, dp_in_ref,
             dp_ref, dlng_ref, dlnb_ref, dw_ref, dbt_ref):
        del dp_in_ref

        @pl.when(pl.program_id(0) == 0)
        def _():
            for ref in (dlng_ref, dlnb_ref, dw_ref, dbt_ref):
                ref[...] = jnp.zeros_like(ref)

        for n in range(B_ROWS // B_CHUNK):
            rows = slice(n * B_CHUNK, (n + 1) * B_CHUNK)
            _, vjp = jax.vjp(_gmlp_chunk, *_gmlp_args(u_ref, v_ref, lng_ref, lnb_ref, w_ref, bt_ref, rows))
            douts = [do_ref[rows, g * B_GDIM:(g + 1) * B_GDIM] for g in range(B_GROUPS)]
            dus, dvs, dlngs, dlnbs, dws, dbs = vjp(douts)
            for g in range(B_GROUPS):
                lanes = slice(g * B_GDIM, (g + 1) * B_GDIM)
                dp_ref[rows, lanes] = dus[g]
                dp_ref[rows, B_WIDTH + g * B_GDIM:B_WIDTH + (g + 1) * B_GDIM] = dvs[g]
                dlng_ref[:, lanes] += dlngs[g]
                dlnb_ref[:, lanes] += dlnbs[g]
                dw_ref[g] += dws[g]
                dbt_ref[:, g:g + 1] += dbs[g]

    vec = pl.BlockSpec((1, B_WIDTH), lambda i: (0, 0))
    wspec = pl.BlockSpec((B_GROUPS, B_CHUNK, B_CHUNK), lambda i: (0, 0, 0))
    bspec = pl.BlockSpec((B_CHUNK, B_GROUPS), lambda i: (0, 0))
    return _call(
        body, name=name, grid=(t // B_ROWS,),
        in_specs=[pl.BlockSpec((B_ROWS, B_WIDTH), lambda i: (i, 4)), pl.BlockSpec((B_ROWS, B_WIDTH), lambda i: (i, 5)),
                  pl.BlockSpec((B_ROWS, B_WIDTH), lambda i: (i, 1)), vec, vec, wspec, bspec,
                  pl.BlockSpec(memory_space=pl.ANY)],
        out_specs=[pl.BlockSpec((B_ROWS, 2 * B_WIDTH), lambda i: (i, 2)), vec, vec, wspec, bspec],
        out_shape=[jax.ShapeDtypeStruct(dproj.shape, F32), jax.ShapeDtypeStruct((1, B_WIDTH), F32),
                   jax.ShapeDtypeStruct((1, B_WIDTH), F32), jax.ShapeDtypeStruct((B_GROUPS, B_CHUNK, B_CHUNK), F32),
                   jax.ShapeDtypeStruct((B_CHUNK, B_GROUPS), F32)],
        aliases={7: 0}, args=(proj, proj, dmixin, ln_g, ln_b, w, bias_t, dproj), exchange=exchange)


C_FWD_BLOCKS = 8
C_BWD_BLOCKS = 4
C_PAIR = 2 * C_HEAD_DIM
C_PAIRS = C_HEADS // 2
C_SCALE = 1.0 / math.sqrt(C_HEAD_DIM)
C_ROT_DIM = 2 * C_ROT_HALF
ROPE_ROWS = 1024


def rope_tables(pos_col, name):
    t = pos_col.shape[0]

    def body(p_ref, c_ref, a_ref, b_ref):
        lane = jnp.bitwise_and(lax.broadcasted_iota(jnp.int32, (1, C_PAIR), 1), C_HEAD_DIM - 1)
        j = jnp.bitwise_and(lane, C_ROT_HALF - 1).astype(F32)
        inv = jnp.exp(j * (-math.log(ROPE_THETA) / C_ROT_HALF))
        ang = p_ref[...].astype(F32) * inv
        cos, sin = jnp.cos(ang), jnp.sin(ang)
        c_ref[...] = jnp.where(lane < C_ROT_DIM, cos, 1.0)
        a_ref[...] = jnp.where(lane < C_ROT_HALF, -sin, 0.0)
        b_ref[...] = jnp.where(jnp.logical_and(lane >= C_ROT_HALF, lane < C_ROT_DIM), sin, 0.0)

    tab = pl.BlockSpec((ROPE_ROWS, C_PAIR), lambda i: (i, 0))
    return pl.pallas_call(
        body, name=name, grid=(t // ROPE_ROWS,),
        in_specs=[pl.BlockSpec((ROPE_ROWS, 1), lambda i: (i, 0))],
        out_specs=[tab, tab, tab],
        out_shape=[jax.ShapeDtypeStruct((t, C_PAIR), F32)] * 3,
        compiler_params=_params(("arbitrary",)),
    )(pos_col)


def _rope(x, c, a, b):
    return x * c + pltpu.roll(x, C_PAIR - C_ROT_HALF, 1) * a + pltpu.roll(x, C_ROT_HALF, 1) * b


def _rope_t(d, c, a, b):
    return d * c + pltpu.roll(d * a, C_ROT_HALF, 1) + pltpu.roll(d * b, C_PAIR - C_ROT_HALF, 1)


C_RES = 16


def _residue_major(a, batch):
    return a.reshape(batch, SEQ // C_RES, C_RES, -1).transpose(0, 2, 1, 3).reshape(a.shape)


def _sequence_order(a, batch):
    return a.reshape(batch, C_RES, SEQ // C_RES, -1).transpose(0, 2, 1, 3).reshape(a.shape)


def _block_pieces(idx, dil):
    nblk = SEQ // dil // C_BLOCK
    r, n = idx // nblk, idx % nblk
    per = C_RES // dil
    size = C_BLOCK // per

    def pieces(blk):
        return [((dil * a + r) * (SEQ // C_RES) + size * blk, size) for a in range(per)]

    return pieces(n), pieces(jnp.maximum(n - 1, 0)), n > 0


def _get_rows(ref, pieces):
    return jnp.concatenate([ref[pl.ds(pl.multiple_of(start, 8), size), :] for start, size in pieces], axis=0)


def _set_rows(ref, pieces, val, add=False):
    for k, (start, size) in enumerate(pieces):
        rows = pl.ds(pl.multiple_of(start, 8), size)
        part = val[k * size:(k + 1) * size]
        ref[rows, :] = ref[rows, :] + part if add else part


def _head_masks():
    low = lax.broadcasted_iota(jnp.int32, (1, C_PAIR), 1) < C_HEAD_DIM
    return low, jnp.logical_not(low)


def _attn_mask(has_prev, dil):
    per = C_RES // dil
    size = C_BLOCK // per

    def position(x):
        x = jnp.bitwise_and(x, C_BLOCK - 1)
        return per * jnp.bitwise_and(x, size - 1) + x // size

    j = lax.broadcasted_iota(jnp.int32, (2 * C_BLOCK, 2 * C_BLOCK), 1)
    pi = position(lax.broadcasted_iota(jnp.int32, (2 * C_BLOCK, 2 * C_BLOCK), 0))
    pj = position(j)
    own = j < C_BLOCK
    return jnp.logical_or(jnp.logical_and(own, pj <= pi),
                          jnp.logical_and(jnp.logical_and(jnp.logical_not(own), pj >= pi), has_prev))


def _stack_heads(x):
    low, high = _head_masks()
    return jnp.concatenate([jnp.where(low, x, 0.0), jnp.where(high, x, 0.0)], axis=0)


def _unstack_heads(x):
    low, _ = _head_masks()
    return jnp.where(low, x[:C_BLOCK], x[C_BLOCK:])


def attn_fwd(qkv, cos_t, sin_a, sin_b, batch, name, exchange=None):
    t = qkv.shape[0]
    nbr = len(C_DILATIONS)

    def body(q_ref, k_ref, v_ref, c_ref, a_ref, b_ref, o_ref, l_ref, qs, ks, *stats):
        acc, mm, dd = stats[0:nbr], stats[nbr:2 * nbr], stats[2 * nbr:3 * nbr]
        c, a, b = c_ref[...], a_ref[...], b_ref[...]
        qs[...] = _rope(q_ref[...], c, a, b) * C_SCALE
        ks[...] = _rope(k_ref[...], c, a, b)

        def load(idx, dil):
            own, prev, has_prev = _block_pieces(idx, dil)
            return own, (has_prev, _get_rows(qs, own), _get_rows(ks, own), _get_rows(ks, prev),
                         _get_rows(v_ref, own), _get_rows(v_ref, prev))

        def scores(dil, has_prev, q, k_own, k_prev, v_own, v_prev):
            k_cat = jnp.concatenate([k_own, k_prev], axis=0).astype(BF16)
            return jnp.where(_attn_mask(has_prev, dil), _dot_nt(_stack_heads(q).astype(BF16), k_cat), NEG_BIG)

        def softmax(s):
            m = jnp.max(s, axis=-1, keepdims=True)
            p = jnp.exp(s - m)
            return p.astype(BF16), m, jnp.sum(p, axis=-1, keepdims=True)

        def values(pb, has_prev, q, k_own, k_prev, v_own, v_prev):
            low, high = _head_masks()
            v_cat = jnp.concatenate([v_own, v_prev], axis=0)
            p_wide = jnp.concatenate([pb[:C_BLOCK], pb[C_BLOCK:]], axis=1)
            v_tall = jnp.concatenate([jnp.where(low, v_cat, 0.0), jnp.where(high, v_cat, 0.0)], axis=0).astype(BF16)
            return _dot(p_wide, v_tall)

        for bi, dil in enumerate(C_DILATIONS):
            def pair(i, carry, bi=bi, dil=dil):
                low, _ = _head_masks()
                loaded = [load(C_FWD_BLOCKS * i + k, dil) for k in range(C_FWD_BLOCKS)]
                ss = [scores(dil, *ops) for _, ops in loaded]
                sm = [softmax(s) for s in ss]
                pvs = [values(pb, *ops) for (pb, _, _), (_, ops) in zip(sm, loaded)]
                for (own, _), (_, m, den), pv in zip(loaded, sm, pvs):
                    _set_rows(acc[bi], own, pv)
                    _set_rows(mm[bi], own, jnp.where(low, m[:C_BLOCK], m[C_BLOCK:]))
                    _set_rows(dd[bi], own, jnp.where(low, den[:C_BLOCK], den[C_BLOCK:]))
                return carry

            lax.fori_loop(0, SEQ // C_BLOCK // C_FWD_BLOCKS, pair, 0)
        step = 2 * C_BLOCK
        for r0 in range(0, SEQ, step):
            rr = slice(r0, r0 + step)
            ms = [mm[g][rr, :] for g in range(nbr)]
            m_all = functools.reduce(jnp.maximum, ms)
            ws = [jnp.exp(m - m_all) for m in ms]
            num = sum(acc[g][rr, :] * ws[g] for g in range(nbr))
            den = sum(dd[g][rr, :] * ws[g] for g in range(nbr))
            o_ref[rr, :] = (num / den).astype(BF16)
            l_ref[rr, :] = m_all + jnp.log(den)

    def col(k):
        return pl.BlockSpec((SEQ, C_PAIR), lambda b, p: (b, k * C_PAIRS + p))

    tab = pl.BlockSpec((SEQ, C_PAIR), lambda b, p: (b, 0))
    return _call(
        body, name=name, grid=(batch, C_PAIRS),
        in_specs=[col(0), col(1), col(2), tab, tab, tab],
        out_specs=[col(0), col(0)],
        out_shape=[jax.ShapeDtypeStruct((t, D_MODEL), BF16), jax.ShapeDtypeStruct((t, D_MODEL), F32)],
        scratch_shapes=[pltpu.VMEM((SEQ, C_PAIR), F32)] * (2 + 3 * nbr),
        args=(qkv, qkv, qkv, cos_t, sin_a, sin_b), exchange=exchange)


def attn_bwd(qkv, cos_t, sin_a, sin_b, o, lse, do, batch, name, exchange=None):
    t = qkv.shape[0]

    def body(q_ref, k_ref, v_ref, c_ref, a_ref, b_ref, o_ref, l_ref, do_ref, dq_ref, dk_ref, dv_ref,
             qs, ks, dqs, dks, dvs, dlt):
        low, _ = _head_masks()
        c, a, b = c_ref[...], a_ref[...], b_ref[...]
        qs[...] = _rope(q_ref[...], c, a, b) * C_SCALE
        ks[...] = _rope(k_ref[...], c, a, b)
        prod = do_ref[...] * o_ref[...].astype(F32)
        s_low = jnp.sum(jnp.where(low, prod, 0.0), axis=-1, keepdims=True)
        s_all = jnp.sum(prod, axis=-1, keepdims=True)
        dlt[...] = jnp.where(low, s_low, s_all - s_low)
        dqs[...] = jnp.zeros_like(dqs)
        dks[...] = jnp.zeros_like(dks)
        dvs[...] = jnp.zeros_like(dvs)

        def load(idx, dil):
            own, prev, has_prev = _block_pieces(idx, dil)
            return (own, prev), (has_prev, _get_rows(qs, own), _get_rows(do_ref, own), _get_rows(ks, own),
                                 _get_rows(ks, prev), _get_rows(v_ref, own), _get_rows(v_ref, prev),
                                 _get_rows(l_ref, own), _get_rows(dlt, own))

        def operands(dil, has_prev, q, do, k_own, k_prev, v_own, v_prev, l_full, d_full):
            lcol = jnp.concatenate([l_full[:, 0:1], l_full[:, C_HEAD_DIM:C_HEAD_DIM + 1]], axis=0)
            dcol = jnp.concatenate([d_full[:, 0:1], d_full[:, C_HEAD_DIM:C_HEAD_DIM + 1]], axis=0)
            return (_stack_heads(q).astype(BF16), _stack_heads(do).astype(BF16),
                    jnp.concatenate([k_own, k_prev], axis=0).astype(BF16),
                    jnp.concatenate([v_own, v_prev], axis=0).astype(BF16), lcol, dcol, _attn_mask(has_prev, dil))

        for dil in C_DILATIONS:
            def pair(i, carry, dil=dil):
                loaded = [load(C_BWD_BLOCKS * i + k, dil) for k in range(C_BWD_BLOCKS)]
                ops = [operands(dil, *o) for _, o in loaded]
                ss = [_dot_nt(q_stack, k_cat) for q_stack, _, k_cat, _, _, _, _ in ops]
                dps = [_dot_nt(do_stack, v_cat) for _, do_stack, _, v_cat, _, _, _ in ops]
                ps = [jnp.exp(jnp.where(o[6], s, NEG_BIG) - o[4]) for s, o in zip(ss, ops)]
                dss = [(p * (dp - o[5])).astype(BF16) for p, dp, o in zip(ps, dps, ops)]
                dvs_ = [_dot_tn(p.astype(BF16), o[1]) for p, o in zip(ps, ops)]
                dks_ = [_dot_tn(ds, o[0]) for ds, o in zip(dss, ops)]
                dqs_ = [_unstack_heads(_dot(ds, o[2])) for ds, o in zip(dss, ops)]
                for ((own, prev), _), dq, dk_cat, dv_cat in zip(loaded, dqs_, dks_, dvs_):
                    _set_rows(dqs, own, dq, add=True)
                    _set_rows(dks, own, dk_cat[:C_BLOCK], add=True)
                    _set_rows(dvs, own, dv_cat[:C_BLOCK], add=True)
                    _set_rows(dks, prev, dk_cat[C_BLOCK:], add=True)
                    _set_rows(dvs, prev, dv_cat[C_BLOCK:], add=True)
                return carry

            lax.fori_loop(0, SEQ // C_BLOCK // C_BWD_BLOCKS, pair, 0)
        dq_ref[...] = _rope_t(dqs[...] * C_SCALE, c, a, b).astype(BF16)
        dk_ref[...] = _rope_t(dks[...], c, a, b).astype(BF16)
        dv_ref[...] = dvs[...].astype(BF16)

    def col(k):
        return pl.BlockSpec((SEQ, C_PAIR), lambda b, p: (b, k * C_PAIRS + p))

    tab = pl.BlockSpec((SEQ, C_PAIR), lambda b, p: (b, 0))
    out = jax.ShapeDtypeStruct((t, D_MODEL), BF16)
    return _call(
        body, name=name, grid=(batch, C_PAIRS),
        in_specs=[col(0), col(1), col(2), tab, tab, tab, col(0), col(0), col(0)],
        out_specs=[col(0), col(0), col(0)],
        out_shape=[out, out, out],
        scratch_shapes=[pltpu.VMEM((SEQ, C_PAIR), F32)] * 6,
        args=(qkv, qkv, qkv, cos_t, sin_a, sin_b, o, lse, do), exchange=exchange)


def sibling_swap(arrays, name):
    n = len(arrays)

    def body(*refs):
        ins, outs = refs[:n], refs[n:2 * n]
        send_sems, recv_sems = refs[2 * n:]
        x, y, c, _ = _place()
        sends = []
        for a in range(n):
            cp = pltpu.make_async_remote_copy(
                src_ref=ins[a], dst_ref=outs[a], send_sem=send_sems.at[a], recv_sem=recv_sems.at[a],
                device_id=(x, y, 1 - c), device_id_type=MESH)
            cp.start()
            sends.append(cp)
        for cp in sends:
            cp.wait_recv()
        for cp in sends:
            cp.wait_send()

    return pl.pallas_call(
        body, name=name,
        in_specs=[ANY] * n, out_specs=[ANY] * n,
        out_shape=[jax.ShapeDtypeStruct(s.shape, s.dtype) for s in arrays],
        scratch_shapes=[pltpu.SemaphoreType.DMA((n,)), pltpu.SemaphoreType.DMA((n,))],
    )(*arrays)


def allreduce_small(slab, name):
    rows, lanes = slab.shape

    def body(x_ref, out_ref, gath, send_sems, recv_sems, local_sem):
        x, y, c, chips = _place()
        me, sibling = (x, y, c), (x, y, 1 - c)

        def slot(px, py, pc):
            return gath.at[4 * px + 2 * py + pc]

        def copy(k, block, to, src=None):
            return pltpu.make_async_remote_copy(
                src_ref=slot(*block) if src is None else src, dst_ref=slot(*block),
                send_sem=send_sems.at[k], recv_sem=recv_sems.at[k], device_id=to, device_id_type=MESH)

        mine = pltpu.make_async_copy(x_ref, slot(*me), local_sem)
        mine.start()
        first = [copy(0, me, sibling, src=x_ref)]
        first += [copy(1 + j, me, (*chip, c), src=x_ref) for j, chip in enumerate(chips)]
        for cp in first:
            cp.start()
        passed = [copy(4 + j, (*chip, c), sibling) for j, chip in enumerate(chips)]
        for j, chip in enumerate(chips):
            copy(1 + j, (*chip, c), me).wait_recv()
            passed[j].start()
        copy(0, sibling, me).wait_recv()
        for j, chip in enumerate(chips):
            copy(4 + j, (*chip, 1 - c), me).wait_recv()
        for cp in first + passed:
            cp.wait_send()
        mine.wait()
        total = gath[0]
        for d in range(1, N_DEV):
            total = total + gath[d]
        out_ref[...] = total

    return pl.pallas_call(
        body, name=name,
        in_specs=[pl.BlockSpec(memory_space=pltpu.VMEM)],
        out_specs=pl.BlockSpec(memory_space=pltpu.VMEM),
        out_shape=jax.ShapeDtypeStruct((rows, lanes), F32),
        scratch_shapes=[pltpu.VMEM((N_DEV, rows, lanes), F32),
                        pltpu.SemaphoreType.DMA((7,)), pltpu.SemaphoreType.DMA((7,)), pltpu.SemaphoreType.DMA],
    )(slab)


ELT_ROWS = 512


def reduce_slabs(r, name, part=0, parts=1, into=None):
    _, rows, cols = r.shape
    br = min(rows, ELT_ROWS)
    nblk = rows // br

    def body(r_ref, *rest):
        o_ref = rest[-1]
        o_ref[...] = ((r_ref[3].astype(F32) + r_ref[0].astype(F32)) + r_ref[1].astype(F32)) + r_ref[2].astype(F32)

    return pl.pallas_call(
        body, name=name, grid=(nblk,),
        in_specs=[pl.BlockSpec((N_CHIPS, br, cols), lambda i: (0, i, 0))] + ([] if into is None else [ANY]),
        out_specs=pl.BlockSpec((br, cols), lambda i: (part * nblk + i, 0)),
        out_shape=jax.ShapeDtypeStruct((parts * rows, cols), F32),
        input_output_aliases={} if into is None else {1: 0},
        compiler_params=_params(("arbitrary",)),
    )(*([r] if into is None else [r, into]))


def _adamw(w, g, m, v):
    m = ADAM_B1 * m + (1.0 - ADAM_B1) * g
    v = ADAM_B2 * v + (1.0 - ADAM_B2) * jnp.square(g)
    m_hat = m / (1.0 - ADAM_B1 ** ADAM_STEP)
    v_hat = v / (1.0 - ADAM_B2 ** ADAM_STEP)
    delta = -ADAM_LR * (m_hat / (jnp.sqrt(v_hat) + ADAM_EPS) + ADAM_WD * w)
    return delta, m, v


def adamw_big(w, s_mine, s_sibling, m, v, name):
    rows, cols = w.shape

    def body(w_ref, a_ref, b_ref, m_ref, v_ref, g_out, d_out, m_out, v_out):
        g = a_ref[...] + b_ref[...]
        g_out[...] = g
        d_out[...], m_out[...], v_out[...] = _adamw(w_ref[...], g, m_ref[...], v_ref[...])

    blk = pl.BlockSpec((min(rows, ELT_ROWS), cols), lambda i: (i, 0))
    out = jax.ShapeDtypeStruct((rows, cols), F32)
    return pl.pallas_call(
        body, name=name, grid=(rows // min(rows, ELT_ROWS),),
        in_specs=[blk] * 5, out_specs=[blk] * 4, out_shape=[out] * 4,
        compiler_params=_params(("arbitrary",)),
    )(w, s_mine, s_sibling, m, v)


def adamw_small(ws, gs, ms, vs, name):
    n = len(ws)

    def body(*refs):
        w_refs, g_refs, m_refs, v_refs = (refs[k * n:(k + 1) * n] for k in range(4))
        d_out, m_out, v_out = (refs[(4 + k) * n:(5 + k) * n] for k in range(3))
        for i in range(n):
            d_out[i][...], m_out[i][...], v_out[i][...] = _adamw(
                w_refs[i][...], g_refs[i][...], m_refs[i][...], v_refs[i][...])

    outs = [jax.ShapeDtypeStruct(w.shape, F32) for w in ws]
    res = pl.pallas_call(body, name=name, out_shape=outs * 3)(*ws, *gs, *ms, *vs)
    return res[:n], res[n:2 * n], res[2 * n:]


SLAB_LANES = 128
SLAB_ROW_ALIGN = 8


def _pack(parts):
    flat = jnp.concatenate([p.reshape(-1) for p in parts])
    rows = -(-flat.shape[0] // (SLAB_LANES * SLAB_ROW_ALIGN)) * SLAB_ROW_ALIGN
    flat = jnp.pad(flat, (0, rows * SLAB_LANES - flat.shape[0]))
    return flat.reshape(rows, SLAB_LANES)


def _unpack(slab, shapes):
    flat = slab.reshape(-1)
    out, pos = [], 0
    for s in shapes:
        size = math.prod(s)
        out.append(flat[pos:pos + size].reshape(s))
        pos += size
    return out


def kernel(x, positions, norm_mix_pre, norm_mix_post, norm_ffn_pre, norm_ffn_post, w_in_even, lb_table, a_norm, b_ln_g, b_ln_b, b_ws, b_bias, w_out_even, w_in_odd, w_out_odd, w_ff1, w_ff2, loss_target, m_norm_mix_pre, m_norm_mix_post, m_norm_ffn_pre, m_norm_ffn_post, m_w_in_even, m_lb_table, m_a_norm, m_b_ln_g, m_b_ln_b, m_b_ws, m_b_bias, m_w_out_even, m_w_in_odd, m_w_out_odd, m_w_ff1, m_w_ff2, v_norm_mix_pre, v_norm_mix_post, v_norm_ffn_pre, v_norm_ffn_post, v_w_in_even, v_lb_table, v_a_norm, v_b_ln_g, v_b_ln_b, v_b_ws, v_b_bias, v_w_out_even, v_w_in_odd, v_w_out_odd, v_w_ff1, v_w_ff2):
    batch = x.shape[0]
    t = batch * SEQ
    d = D_MODEL
    x0 = x.reshape(t, d)
    target = loss_target.reshape(t, d)

    def gain(p, layer):
        return p[layer:layer + 1]

    def gather(*shards):
        return _Exchange("gather", [w.astype(BF16) for w in shards])

    def scatter(*grads):
        return _Exchange("scatter", grads)

    (win_e,) = exchange_alone(gather(w_in_even[0]), "gather_in_even")
    bias_t = b_bias[0].T
    proj, h0, w1_0 = norm_matmul(x0, gain(norm_mix_pre, 0), win_e, "in_proj_even", exchange=gather(w_ff1[0]))
    oa, states, decays, w2_0 = hgrn2_fwd(proj, lb_table, a_norm, batch, "hgrn2_fwd", exchange=gather(w_ff2[0]))
    mixin, wout_e = gmlp_fwd(proj, oa, b_ln_g, b_ln_b, b_ws[0], bias_t, "gmlp_fwd", exchange=gather(w_out_even[0]))
    mix0, x1 = out_proj(mixin, wout_e, x0, gain(norm_mix_post, 0), "out_proj_even")
    x2, hf0, a0, y0, win_o, wout_o = ffn_fwd(x1, gain(norm_ffn_pre, 0), w1_0, w2_0, gain(norm_ffn_post, 0),
                                             "ffn_fwd_0", exchange=gather(w_in_odd[0], w_out_odd[0]))
    x2p = _residue_major(x2, batch)
    qkv, h1 = norm_matmul(x2p, gain(norm_mix_pre, 1), win_o, "in_proj_odd")
    cos_t, sin_a, sin_b = rope_tables(_residue_major(positions.reshape(t, 1), batch), "rope_tables")
    ao, lse, w1_1, w2_1 = attn_fwd(qkv, cos_t, sin_a, sin_b, batch, "attn_fwd", exchange=gather(w_ff1[1], w_ff2[1]))
    mix1, x3 = out_proj(ao, wout_o, x2p, gain(norm_mix_post, 1), "out_proj_odd")
    dx4, hf1, a1, y1, loss_part = ffn_fwd(x3, gain(norm_ffn_pre, 1), w1_1, w2_1, gain(norm_ffn_post, 1),
                                          "ffn_fwd_1", target=_residue_major(target, batch))

    hc = D_FF // N_CHIPS
    dx3, dy1, da1, dg_fpre1, dg_fpost1 = ffn_bwd(
        dx4, x3, y1, a1, gain(norm_ffn_pre, 1), gain(norm_ffn_post, 1), w1_1, w2_1, "ffn_bwd_1")
    g_w1_1 = weight_grad(hf1, da1, "b", d, hc, False, "wgrad_ff1_1")
    g_w2_1 = weight_grad(a1, dy1, "a", hc, d, True, "wgrad_ff2_1")
    dmix1, dao, dg_mpost1 = out_proj_bwd(dx3, mix1, gain(norm_mix_post, 1), wout_o, "out_proj_bwd_odd")
    g_wout_o = weight_grad(ao, dmix1, "a", d // N_CHIPS, d, False, "wgrad_out_odd")
    dq, dk, dv, r_w1_1, r_w2_1, r_wout_o = attn_bwd(qkv, cos_t, sin_a, sin_b, ao, lse, dao, batch, "attn_bwd",
                                                    exchange=scatter(g_w1_1, g_w2_1, g_wout_o))
    dqkv = jnp.concatenate([dq, dk, dv], axis=1)
    dx2p, dg_mpre1 = norm_matmul_bwd(dqkv, win_o, x2p, gain(norm_mix_pre, 1), dx3, "in_proj_bwd_odd")
    dx2 = _sequence_order(dx2p, batch)
    g_win_o = weight_grad(h1, dqkv, "b", d, 3 * d // N_CHIPS, False, "wgrad_in_odd")
    dx1, dy0, da0, dg_fpre0, dg_fpost0, r_win_o = ffn_bwd(
        dx2, x1, y0, a0, gain(norm_ffn_pre, 0), gain(norm_ffn_post, 0), w1_0, w2_0, "ffn_bwd_0",
        exchange=scatter(g_win_o))
    g_w1_0 = weight_grad(hf0, da0, "b", d, hc, False, "wgrad_ff1_0")
    g_w2_0 = weight_grad(a0, dy0, "a", hc, d, True, "wgrad_ff2_0")
    dmix0, dmixin, dg_mpost0 = out_proj_bwd(dx1, mix0, gain(norm_mix_post, 0), wout_e, "out_proj_bwd_even")
    g_wout_e = weight_grad(mixin, dmix0, "a", d // N_CHIPS, d, False, "wgrad_out_even")
    dproj, d_lb, d_anorm, r_w1_0 = hgrn2_bwd(
        proj, states, decays, lb_table, a_norm, dmixin, batch, "hgrn2_bwd", exchange=scatter(g_w1_0))
    dproj, d_lng, d_lnb, d_ws, d_bias_t, r_w2_0 = gmlp_bwd(
        proj, dmixin, b_ln_g, b_ln_b, b_ws[0], bias_t, dproj, "gmlp_bwd", exchange=scatter(g_w2_0))
    g_win_e, r_wout_e = weight_grad(h0, dproj, "b", d, 3 * d // N_CHIPS, False, "wgrad_in_even",
                                    exchange=scatter(g_wout_e))
    dx0, dg_mpre0, r_win_e = norm_matmul_bwd(dproj, win_e, x0, gain(norm_mix_pre, 0), dx1, "in_proj_bwd_even",
                                             exchange=scatter(g_win_e))
    grad_x = dx0.reshape(x.shape)

    s_w1 = reduce_slabs(r_w1_1, "reduce_ff1_1", part=1, parts=2)
    s_w1 = reduce_slabs(r_w1_0, "reduce_ff1_0", part=0, parts=2, into=s_w1)
    s_w2 = reduce_slabs(r_w2_1, "reduce_ff2_1", part=1, parts=2)
    s_w2 = reduce_slabs(r_w2_0, "reduce_ff2_0", part=0, parts=2, into=s_w2)
    sums = [reduce_slabs(r_win_e, "reduce_in_even"), reduce_slabs(r_wout_e, "reduce_out_even"),
            reduce_slabs(r_win_o, "reduce_in_odd"), reduce_slabs(r_wout_o, "reduce_out_odd"), s_w1, s_w2]
    sibling = sibling_swap(sums, "sibling_swap")
    big_w = [w_in_even, w_out_even, w_in_odd, w_out_odd, w_ff1, w_ff2]
    big_m = [m_w_in_even, m_w_out_even, m_w_in_odd, m_w_out_odd, m_w_ff1, m_w_ff2]
    big_v = [v_w_in_even, v_w_out_even, v_w_in_odd, v_w_out_odd, v_w_ff1, v_w_ff2]
    big = []
    for i, (w, m, v) in enumerate(zip(big_w, big_m, big_v)):
        two_d = (-1, w.shape[-1])
        res = adamw_big(w.reshape(two_d), sums[i], sibling[i], m.reshape(two_d), v.reshape(two_d), "adamw_big_%d" % i)
        big.append([r.reshape(w.shape) for r in res])

    small_w = [norm_mix_pre, norm_mix_post, norm_ffn_pre, norm_ffn_post, lb_table, a_norm, b_ln_g, b_ln_b, b_ws, b_bias]
    small_m = [m_norm_mix_pre, m_norm_mix_post, m_norm_ffn_pre, m_norm_ffn_post, m_lb_table, m_a_norm, m_b_ln_g,
               m_b_ln_b, m_b_ws, m_b_bias]
    small_v = [v_norm_mix_pre, v_norm_mix_post, v_norm_ffn_pre, v_norm_ffn_post, v_lb_table, v_a_norm, v_b_ln_g,
               v_b_ln_b, v_b_ws, v_b_bias]
    partial = [jnp.concatenate([dg_mpre0, dg_mpre1]), jnp.concatenate([dg_mpost0, dg_mpost1]),
               jnp.concatenate([dg_fpre0, dg_fpre1]), jnp.concatenate([dg_fpost0, dg_fpost1]),
               d_lb, d_anorm, d_lng, d_lnb, d_ws[None], d_bias_t.T[None]]
    *small_g, loss = _unpack(allreduce_small(_pack(partial + [loss_part]), "allreduce_small"),
                             [w.shape for w in small_w] + [()])
    small_d, small_nm, small_nv = adamw_small(small_w, small_g, small_m, small_v, "adamw_small")

    order = ["norm_mix_pre", "norm_mix_post", "norm_ffn_pre", "norm_ffn_post", "w_in_even", "lb_table", "a_norm",
             "b_ln_g", "b_ln_b", "b_ws", "b_bias", "w_out_even", "w_in_odd", "w_out_odd", "w_ff1", "w_ff2"]
    small_names = ["norm_mix_pre", "norm_mix_post", "norm_ffn_pre", "norm_ffn_post", "lb_table", "a_norm",
                   "b_ln_g", "b_ln_b", "b_ws", "b_bias"]
    big_names = ["w_in_even", "w_out_even", "w_in_odd", "w_out_odd", "w_ff1", "w_ff2"]
    grads, deltas, new_m, new_v = {}, {}, {}, {}
    for i, nm in enumerate(small_names):
        grads[nm], deltas[nm], new_m[nm], new_v[nm] = small_g[i], small_d[i], small_nm[i], small_nv[i]
    for i, nm in enumerate(big_names):
        grads[nm], deltas[nm], new_m[nm], new_v[nm] = big[i]
    return (loss, grad_x, *[grads[n] for n in order], *[deltas[n] for n in order],
            *[new_m[n] for n in order], *[new_v[n] for n in order])
```

```python
import functools
import math

import jax
import jax.numpy as jnp
from jax import lax
from jax.experimental import pallas as pl
from jax.experimental.pallas import tpu as pltpu

F32 = jnp.float32
BF16 = jnp.bfloat16
MESH = pl.DeviceIdType.MESH

D_MODEL = 1024
SEQ = 2048
D_FF = 4096
N_CHIPS = 4
A_WIDTH = 512
A_HEADS = 4
A_DK = 128
A_CHUNK = 64
A_SUB = 16
B_WIDTH = 512
B_GROUPS = 4
B_CHUNK = 128
C_HEADS = 16
C_HEAD_DIM = 64
C_ROT_HALF = 8
C_BLOCK = 128
C_DILATIONS = (1, 4, 16)
ROPE_THETA = 500000.0
EPS = 1e-6
ADAM_LR = 0.001
ADAM_B1 = 0.9
ADAM_B2 = 0.999
ADAM_EPS = 1e-08
ADAM_WD = 0.01
ADAM_STEP = 10

ROW_TILE = 512
FFN_ROWS = 1024
WGRAD_ROWS = 2048
VMEM_LIMIT = 56 * 1024 * 1024
NEG_BIG = -1e30


def _params(sem=None):
    return pltpu.CompilerParams(dimension_semantics=sem, vmem_limit_bytes=VMEM_LIMIT)


def _dot(a, b):
    return jnp.dot(a, b, preferred_element_type=F32)


def _dot_nt(a, b):
    return lax.dot_general(a, b, (((1,), (1,)), ((), ())), preferred_element_type=F32)


def _dot_tn(a, b):
    return lax.dot_general(a, b, (((0,), (0,)), ((), ())), preferred_element_type=F32)


def _rms(x, g):
    r = lax.rsqrt(jnp.mean(x * x, axis=-1, keepdims=True) + EPS)
    return x * r * g


def _rms_bwd(x, g, dy):
    r = lax.rsqrt(jnp.mean(x * x, axis=-1, keepdims=True) + EPS)
    xh = x * r
    dg = jnp.sum(dy * xh, axis=0, keepdims=True)
    dxh = dy * g
    dx = r * (dxh - xh * jnp.mean(dxh * xh, axis=-1, keepdims=True))
    return dx, dg


def _accumulate(ref, val, first):
    @pl.when(first)
    def _():
        ref[...] = val

    @pl.when(jnp.logical_not(first))
    def _():
        ref[...] += val


N_DEV = 8
ANY = pl.BlockSpec(memory_space=pl.ANY)


def _place():
    x, y, c = lax.axis_index("x"), lax.axis_index("y"), lax.axis_index("c")
    return x, y, c, [(1 - x, y), (x, 1 - y), (1 - x, 1 - y)]


class _Exchange:
    def __init__(self, kind, arrays):
        self.kind, self.arrays, self.n = kind, list(arrays), len(arrays)
        per_peer = pltpu.SemaphoreType.DMA((3 * self.n,))
        if kind == "gather":
            self.out_shape = [jax.ShapeDtypeStruct((N_CHIPS,) + a.shape, a.dtype) for a in self.arrays]
            self.scratch = [per_peer, per_peer, pltpu.SemaphoreType.DMA((self.n,)), per_peer, per_peer]
        else:
            self.out_shape = [jax.ShapeDtypeStruct(a.shape, a.dtype) for a in self.arrays]
            self.scratch = [per_peer, per_peer, pltpu.SemaphoreType.DMA((self.n,))]

    def _copies(self, ins, outs, sems):
        send_sems, recv_sems, local_sems = sems[:3]
        x, y, c, chips = _place()
        me = 2 * x + y
        local, remote = [], []
        for a in range(self.n):
            if self.kind == "gather":
                local.append(pltpu.make_async_copy(ins[a], outs[a].at[me], local_sems.at[a]))
                half = self.arrays[a].shape[0] // 2

                def rows(ref, core, half=half):
                    return ref.at[pl.ds(core * half, half)]
            else:
                local.append(pltpu.make_async_copy(ins[a].at[me], outs[a].at[3], local_sems.at[a]))
            for j, (px, py) in enumerate(chips):
                k = 3 * a + j
                peer = 2 * px + py

                def copy(src, dst, to, send_sem=send_sems.at[k], recv_sem=recv_sems.at[k]):
                    return pltpu.make_async_remote_copy(src_ref=src, dst_ref=dst, send_sem=send_sem, recv_sem=recv_sem,
                                                        device_id=to, device_id_type=MESH)

                if self.kind == "gather":
                    sent = copy(rows(ins[a], c), rows(outs[a].at[me], c), (px, py, c))
                    landed = copy(rows(ins[a], c), rows(outs[a].at[peer], c), (px, py, c))
                    on = dict(send_sem=sems[3].at[k], recv_sem=sems[4].at[k])
                    passed = copy(rows(outs[a].at[peer], c), rows(outs[a].at[peer], c), (x, y, 1 - c), **on)
                    handed = copy(rows(outs[a].at[peer], c), rows(outs[a].at[peer], 1 - c), (x, y, 1 - c), **on)
                    remote.append((sent, landed, passed, handed))
                else:
                    sent = copy(ins[a].at[peer], outs[a].at[j], (px, py, c))
                    remote.append((sent, sent, None, None))
        return local, remote

    def start(self, ins, outs, sems):
        local, remote = self._copies(ins, outs, sems)
        for cp in local:
            cp.start()
        for sent, _, _, _ in remote:
            sent.start()

    def finish(self, ins, outs, sems):
        local, remote = self._copies(ins, outs, sems)
        for _, landed, passed, _ in remote:
            landed.wait_recv()
            if passed is not None:
                passed.start()
        for sent, _, passed, handed in remote:
            if passed is not None:
                handed.wait_recv()
                passed.wait_send()
            sent.wait_send()
        for cp in local:
            cp.wait()


def _call(body, *, name, grid, in_specs, out_specs, out_shape, args, scratch_shapes=(), aliases=None, exchange=None):
    if exchange is None:
        return pl.pallas_call(
            body, name=name, grid=grid, in_specs=in_specs, out_specs=out_specs, out_shape=out_shape,
            scratch_shapes=list(scratch_shapes), input_output_aliases=aliases or {},
            compiler_params=_params(("arbitrary",) * len(grid)))(*args)
    n_in, n_out, n_scr, n_ex = len(in_specs), len(out_specs), len(scratch_shapes), exchange.n
    steps = grid

    def wrapped(*refs):
        ins, refs = refs[:n_in], refs[n_in:]
        ex_in, refs = refs[:n_ex], refs[n_ex:]
        outs, refs = refs[:n_out], refs[n_out:]
        ex_out, refs = refs[:n_ex], refs[n_ex:]
        scr, sems = refs[:n_scr], refs[n_scr:]
        first = functools.reduce(jnp.logical_and, [pl.program_id(k) == 0 for k in range(len(steps))])
        last = functools.reduce(jnp.logical_and, [pl.program_id(k) == steps[k] - 1 for k in range(len(steps))])

        @pl.when(first)
        def _():
            exchange.start(ex_in, ex_out, sems)

        body(*ins, *outs, *scr)

        @pl.when(last)
        def _():
            exchange.finish(ex_in, ex_out, sems)

    return pl.pallas_call(
        wrapped, name=name, grid=grid,
        in_specs=list(in_specs) + [ANY] * n_ex, out_specs=list(out_specs) + [ANY] * n_ex,
        out_shape=list(out_shape) + exchange.out_shape,
        scratch_shapes=list(scratch_shapes) + exchange.scratch, input_output_aliases=aliases or {},
        compiler_params=_params(("arbitrary",) * len(grid)))(*args, *exchange.arrays)


def exchange_alone(exchange, name):
    def body(*refs):
        n = exchange.n
        exchange.start(refs[:n], refs[n:2 * n], refs[2 * n:])
        exchange.finish(refs[:n], refs[n:2 * n], refs[2 * n:])

    return pl.pallas_call(
        body, name=name, in_specs=[ANY] * exchange.n, out_specs=[ANY] * exchange.n,
        out_shape=exchange.out_shape, scratch_shapes=exchange.scratch)(*exchange.arrays)


def norm_matmul(x, g, wg, name, exchange=None):
    t, d = x.shape
    nl = wg.shape[2]

    def body(x_ref, g_ref, w_ref, o_ref, h_ref):
        h = _rms(x_ref[...], g_ref[...]).astype(BF16)
        h_ref[...] = h
        for c in range(N_CHIPS):
            o_ref[:, c * nl:(c + 1) * nl] = _dot(h, w_ref[c])

    return _call(
        body, name=name, grid=(t // ROW_TILE,),
        in_specs=[pl.BlockSpec((ROW_TILE, d), lambda i: (i, 0)),
                  pl.BlockSpec((1, d), lambda i: (0, 0)),
                  pl.BlockSpec((N_CHIPS, d, nl), lambda i: (0, 0, 0))],
        out_specs=[pl.BlockSpec((ROW_TILE, N_CHIPS * nl), lambda i: (i, 0)),
                   pl.BlockSpec((ROW_TILE, d), lambda i: (i, 0))],
        out_shape=[jax.ShapeDtypeStruct((t, N_CHIPS * nl), F32), jax.ShapeDtypeStruct((t, d), BF16)],
        args=(x, g, wg), exchange=exchange)


def norm_matmul_bwd(dproj, wg, x, g, dres, name, exchange=None):
    t, d = x.shape
    nl = wg.shape[2]
    stacked = dproj.ndim == 3
    piece = math.gcd(nl, dproj.shape[-1])

    def body(dp_ref, w_ref, x_ref, g_ref, dres_ref, dx_ref, dg_ref):
        dh = None
        for j in range(N_CHIPS * nl // piece):
            c, off = divmod(j * piece, nl)
            if stacked:
                p, lo = divmod(j * piece, dproj.shape[-1])
                lhs = dp_ref[p, :, lo:lo + piece]
            else:
                lhs = dp_ref[:, j * piece:(j + 1) * piece]
            part = _dot_nt(lhs.astype(BF16), w_ref[c, :, off:off + piece])
            dh = part if dh is None else dh + part
        dx, dg = _rms_bwd(x_ref[...], g_ref[...], dh)
        dx_ref[...] = dres_ref[...] + dx
        _accumulate(dg_ref, dg, pl.program_id(0) == 0)

    row = pl.BlockSpec((ROW_TILE, d), lambda i: (i, 0))
    vec = pl.BlockSpec((1, d), lambda i: (0, 0))
    if stacked:
        dp_spec = pl.BlockSpec((dproj.shape[0], ROW_TILE, dproj.shape[-1]), lambda i: (0, i, 0))
    else:
        dp_spec = pl.BlockSpec((ROW_TILE, N_CHIPS * nl), lambda i: (i, 0))
    return _call(
        body, name=name, grid=(t // ROW_TILE,),
        in_specs=[dp_spec, pl.BlockSpec((N_CHIPS, d, nl), lambda i: (0, 0, 0)), row, vec, row],
        out_specs=[row, vec],
        out_shape=[jax.ShapeDtypeStruct((t, d), F32), jax.ShapeDtypeStruct((1, d), F32)],
        args=(dproj, wg, x, g, dres), exchange=exchange)


def out_proj(a, wg, x, g, name):
    t, d = x.shape
    kl = wg.shape[1]

    def body(a_ref, w_ref, x_ref, g_ref, mix_ref, xo_ref):
        acc = _dot(a_ref[:, 0:kl], w_ref[0])
        for c in range(1, N_CHIPS):
            acc += _dot(a_ref[:, c * kl:(c + 1) * kl], w_ref[c])
        mix_ref[...] = acc
        xo_ref[...] = x_ref[...] + _rms(acc, g_ref[...])

    row = pl.BlockSpec((ROW_TILE, d), lambda i: (i, 0))
    return pl.pallas_call(
        body, name=name, grid=(t // ROW_TILE,),
        in_specs=[row, pl.BlockSpec((N_CHIPS, kl, d), lambda i: (0, 0, 0)), row,
                  pl.BlockSpec((1, d), lambda i: (0, 0))],
        out_specs=[row, row],
        out_shape=[jax.ShapeDtypeStruct((t, d), F32), jax.ShapeDtypeStruct((t, d), F32)],
        compiler_params=_params(("arbitrary",)),
    )(a, wg, x, g)


def out_proj_bwd(dxo, mix, g, wg, name):
    t, d = mix.shape
    kl = wg.shape[1]

    def body(dxo_ref, mix_ref, g_ref, w_ref, dmix_ref, da_ref, dg_ref):
        dmix, dg = _rms_bwd(mix_ref[...], g_ref[...], dxo_ref[...])
        dmb = dmix.astype(BF16)
        dmix_ref[...] = dmb
        for c in range(N_CHIPS):
            da_ref[:, c * kl:(c + 1) * kl] = _dot_nt(dmb, w_ref[c])
        _accumulate(dg_ref, dg, pl.program_id(0) == 0)

    row = pl.BlockSpec((ROW_TILE, d), lambda i: (i, 0))
    vec = pl.BlockSpec((1, d), lambda i: (0, 0))
    return pl.pallas_call(
        body, name=name, grid=(t // ROW_TILE,),
        in_specs=[row, row, vec, pl.BlockSpec((N_CHIPS, kl, d), lambda i: (0, 0, 0))],
        out_specs=[row, row, vec],
        out_shape=[jax.ShapeDtypeStruct((t, d), BF16), jax.ShapeDtypeStruct((t, d), F32),
                   jax.ShapeDtypeStruct((1, d), F32)],
        compiler_params=_params(("arbitrary",)),
    )(dxo, mix, g, wg)


def ffn_fwd(x, gpre, w1g, w2g, gpost, name, exchange=None, target=None):
    t, d = x.shape
    hc = w1g.shape[2]
    with_loss = target is not None

    def body(x_ref, gpre_ref, w1_ref, w2_ref, gpost_ref, *rest):
        if with_loss:
            t_ref, xo_ref, h_ref, a_ref, y_ref, l_ref, acc = rest
        else:
            xo_ref, h_ref, a_ref, y_ref, acc = rest
        i, c = pl.program_id(0), pl.program_id(1)

        @pl.when(c == 0)
        def _():
            h_ref[...] = _rms(x_ref[...], gpre_ref[...]).astype(BF16)

        a = _dot(h_ref[...], w1_ref[...])
        a_ref[...] = a.astype(BF16)
        r = jnp.square(jnp.maximum(a, 0.0)).astype(BF16)
        _accumulate(acc, _dot(r, w2_ref[...]), c == 0)

        @pl.when(c == N_CHIPS - 1)
        def _():
            y = acc[...]
            y_ref[...] = y
            xo = x_ref[...] + _rms(y, gpost_ref[...])
            if with_loss:
                e = xo - t_ref[...]
                xo_ref[...] = e * (1.0 / d)
                part = jnp.sum(jnp.sum(e * e, axis=-1, keepdims=True), axis=0, keepdims=True) * (0.5 / d)
                _accumulate(l_ref, part, i == 0)
            else:
                xo_ref[...] = xo

    row = pl.BlockSpec((FFN_ROWS, d), lambda i, c: (i, 0))
    vec = pl.BlockSpec((1, d), lambda i, c: (0, 0))
    one = pl.BlockSpec((1, 1), lambda i, c: (0, 0))
    return _call(
        body, name=name, grid=(t // FFN_ROWS, N_CHIPS),
        in_specs=[row, vec,
                  pl.BlockSpec((None, d, hc), lambda i, c: (c, 0, 0)),
                  pl.BlockSpec((None, hc, d), lambda i, c: (c, 0, 0)), vec] + ([row] if with_loss else []),
        out_specs=[row, row, pl.BlockSpec((FFN_ROWS, hc), lambda i, c: (i, c)), row] + ([one] if with_loss else []),
        out_shape=[jax.ShapeDtypeStruct((t, d), F32), jax.ShapeDtypeStruct((t, d), BF16),
                   jax.ShapeDtypeStruct((t, N_CHIPS * hc), BF16), jax.ShapeDtypeStruct((t, d), F32)]
        + ([jax.ShapeDtypeStruct((1, 1), F32)] if with_loss else []),
        scratch_shapes=[pltpu.VMEM((FFN_ROWS, d), F32)],
        args=(x, gpre, w1g, w2g, gpost) + ((target,) if with_loss else ()), exchange=exchange)


def ffn_bwd(dxo, x, y, a, gpre, gpost, w1g, w2g, name, exchange=None):
    t, d = x.shape
    hc = w1g.shape[2]

    def body(dxo_ref, x_ref, y_ref, a_ref, gpre_ref, gpost_ref, w1_ref, w2_ref,
             dxi_ref, dy_ref, da_ref, dgpre_ref, dgpost_ref, acc):
        i, c = pl.program_id(0), pl.program_id(1)

        @pl.when(c == 0)
        def _():
            dy, dg = _rms_bwd(y_ref[...], gpost_ref[...], dxo_ref[...])
            dy_ref[...] = dy.astype(BF16)
            _accumulate(dgpost_ref, dg, i == 0)

        dr = _dot_nt(dy_ref[...], w2_ref[...])
        da = (dr * (2.0 * jnp.maximum(a_ref[...].astype(F32), 0.0))).astype(BF16)
        da_ref[...] = da
        _accumulate(acc, _dot_nt(da, w1_ref[...]), c == 0)

        @pl.when(c == N_CHIPS - 1)
        def _():
            dx, dg = _rms_bwd(x_ref[...], gpre_ref[...], acc[...])
            dxi_ref[...] = dxo_ref[...] + dx
            _accumulate(dgpre_ref, dg, i == 0)

    row = pl.BlockSpec((ROW_TILE, d), lambda i, c: (i, 0))
    vec = pl.BlockSpec((1, d), lambda i, c: (0, 0))
    hid = pl.BlockSpec((ROW_TILE, hc), lambda i, c: (i, c))
    return _call(
        body, name=name, grid=(t // ROW_TILE, N_CHIPS),
        in_specs=[row, row, row, hid, vec, vec,
                  pl.BlockSpec((None, d, hc), lambda i, c: (c, 0, 0)),
                  pl.BlockSpec((None, hc, d), lambda i, c: (c, 0, 0))],
        out_specs=[row, row, hid, vec, vec],
        out_shape=[jax.ShapeDtypeStruct((t, d), F32), jax.ShapeDtypeStruct((t, d), BF16),
                   jax.ShapeDtypeStruct((t, N_CHIPS * hc), BF16),
                   jax.ShapeDtypeStruct((1, d), F32), jax.ShapeDtypeStruct((1, d), F32)],
        scratch_shapes=[pltpu.VMEM((ROW_TILE, d), F32)],
        args=(dxo, x, y, a, gpre, gpost, w1g, w2g), exchange=exchange)


def weight_grad(a, b, chunked, bk, bn, relu2, name, exchange=None):
    t = a.shape[0]
    a_on = chunked == "a"
    rows = min(t, WGRAD_ROWS)
    n_steps = t // rows

    def body(a_ref, b_ref, o_ref, acc):
        s = pl.program_id(1)
        av = a_ref[...]
        if relu2:
            av = jnp.square(jnp.maximum(av.astype(F32), 0.0))
        _accumulate(acc, _dot_tn(av.astype(BF16), b_ref[...].astype(BF16)), s == 0)

        @pl.when(s == n_steps - 1)
        def _():
            o_ref[...] = acc[...].astype(BF16)

    res = _call(
        body, name=name, grid=(N_CHIPS, n_steps),
        in_specs=[pl.BlockSpec((rows, bk), (lambda c, s: (s, c)) if a_on else (lambda c, s: (s, 0))),
                  pl.BlockSpec((rows, bn), (lambda c, s: (s, 0)) if a_on else (lambda c, s: (s, c)))],
        out_specs=[pl.BlockSpec((None, bk, bn), lambda c, s: (c, 0, 0))],
        out_shape=[jax.ShapeDtypeStruct((N_CHIPS, bk, bn), BF16)],
        scratch_shapes=[pltpu.VMEM((bk, bn), F32)],
        args=(a, b), exchange=exchange)
    return res[0] if exchange is None else res


def weight_grad_stacked(a, b3, bn, name):
    t, bk = a.shape
    width = b3.shape[-1]
    piece = math.gcd(bn, width)
    rows = min(t, WGRAD_ROWS)
    n_steps = t // rows

    def body(a_ref, b_ref, o_ref, acc):
        c, s = pl.program_id(0), pl.program_id(1)
        av = a_ref[...].astype(BF16)
        for chunk in range(N_CHIPS):
            @pl.when(c == chunk)
            def _(chunk=chunk):
                for k in range(bn // piece):
                    p, lo = divmod(chunk * bn + k * piece, width)
                    part = _dot_tn(av, b_ref[p, :, lo:lo + piece].astype(BF16))
                    _accumulate(acc.at[:, k * piece:(k + 1) * piece], part, s == 0)

        @pl.when(s == n_steps - 1)
        def _():
            o_ref[...] = acc[...].astype(BF16)

    return pl.pallas_call(
        body, name=name, grid=(N_CHIPS, n_steps),
        in_specs=[pl.BlockSpec((rows, bk), lambda c, s: (s, 0)),
                  pl.BlockSpec((b3.shape[0], rows, width), lambda c, s: (0, s, 0))],
        out_specs=pl.BlockSpec((None, bk, bn), lambda c, s: (c, 0, 0)),
        out_shape=jax.ShapeDtypeStruct((N_CHIPS, bk, bn), BF16),
        scratch_shapes=[pltpu.VMEM((bk, bn), F32)],
        compiler_params=_params(("arbitrary", "arbitrary")),
    )(a, b3)


def _hgrn2_chunk(st, qs, fls, ivs, gls, l0, l1, l2, ng):
    nsub = len(qs)
    mx = jnp.maximum(jnp.maximum(l0, l1), l2)
    e0, e1, e2 = jnp.exp(l0 - mx), jnp.exp(l1 - mx), jnp.exp(l2 - mx)
    lb = e0 / (e0 + e1 + e2)
    rows = lax.broadcasted_iota(jnp.int32, (A_SUB, A_SUB), 0)
    cols = lax.broadcasted_iota(jnp.int32, (A_SUB, A_SUB), 1)
    tri = (rows >= cols).astype(F32)
    keep = (lax.broadcasted_iota(jnp.int32, (A_SUB, A_SUB, A_DK), 0)
            >= lax.broadcasted_iota(jnp.int32, (A_SUB, A_SUB, A_DK), 1))
    base = jnp.zeros_like(l0)
    bases, gs, ks, qfs = [], [], [], []
    for i in range(nsub):
        f = lb + (1.0 - lb) * jax.nn.sigmoid(fls[i])
        logf = jnp.log(f)
        bases.append(base)
        gs.append(base + jnp.dot(tri, logf, precision=lax.Precision.HIGHEST, preferred_element_type=F32))
        base = base + jnp.sum(logf, axis=0, keepdims=True)
        ks.append(1.0 - f)
        qfs.append(jax.nn.silu(qs[i]))
    g_last = base
    stb = st.astype(BF16)
    outs = []
    for i in range(nsub):
        o = _dot_nt((qfs[i] * jnp.exp(gs[i])).astype(BF16), stb)
        if i > 0:
            qt = (qfs[i] * jnp.exp(gs[i] - bases[i])).astype(BF16)
            kk = jnp.concatenate([ks[j] * jnp.exp(bases[i] - gs[j]) for j in range(i)], axis=0).astype(BF16)
            vv = jnp.concatenate(ivs[:i], axis=0).astype(BF16)
            o = o + _dot(_dot_nt(qt, kk).astype(BF16), vv)
        dec = jnp.exp(jnp.where(keep, gs[i][:, None, :] - gs[i][None, :, :], NEG_BIG))
        s_diag = jnp.sum(qfs[i][:, None, :] * ks[i][None, :, :] * dec, axis=-1)
        o = o + _dot(s_diag.astype(BF16), ivs[i].astype(BF16))
        o = o * lax.rsqrt(jnp.mean(o * o, axis=-1, keepdims=True) + EPS) * ng
        outs.append(o * jax.nn.silu(gls[i]))
    kdec = jnp.concatenate([ks[j] * jnp.exp(g_last - gs[j]) for j in range(nsub)], axis=0).astype(BF16)
    vall = jnp.concatenate(ivs, axis=0).astype(BF16)
    new_st = st * jnp.exp(g_last) + _dot_tn(vall, kdec)
    return new_st, outs


A_MAX_LOG_DECAY = 60.0


def _half_sums(logf):
    n = logf.shape[0]
    first = lax.broadcasted_iota(jnp.int32, logf.shape, 0) < n // 2
    return (jnp.sum(jnp.where(first, logf, 0.0), axis=0, keepdims=True),
            jnp.sum(jnp.where(first, 0.0, logf), axis=0, keepdims=True))


def _split3(x):
    hi = x.astype(BF16)
    r1 = x - hi.astype(F32)
    mid = r1.astype(BF16)
    return hi, mid, (r1 - mid.astype(F32)).astype(BF16)


def _tri_matmul(x, transpose):
    n = x.shape[0]
    r = lax.broadcasted_iota(jnp.int32, (n, n), 0)
    c = lax.broadcasted_iota(jnp.int32, (n, n), 1)
    tri = ((r <= c) if transpose else (r >= c)).astype(BF16)
    hi, mid, lo = _split3(x)
    return (_dot(tri, lo) + _dot(tri, mid)) + _dot(tri, hi)


@jax.custom_vjp
def _cumsum_rows(x):
    return _tri_matmul(x, False)


def _cumsum_rows_fwd(x):
    return _tri_matmul(x, False), None


def _cumsum_rows_bwd(_, dy):
    return (_tri_matmul(dy, True),)


_cumsum_rows.defvjp(_cumsum_rows_fwd, _cumsum_rows_bwd)


def _lower_bound(l0, l1, l2):
    mx = jnp.maximum(jnp.maximum(l0, l1), l2)
    e0, e1, e2 = jnp.exp(l0 - mx), jnp.exp(l1 - mx), jnp.exp(l2 - mx)
    return e0 / (e0 + e1 + e2)


def _b(x):
    return x.astype(BF16)


@jax.custom_vjp
def _mm(a, b):
    return _dot(_b(a), _b(b))


_mm.defvjp(lambda a, b: (_mm(a, b), (a, b)),
           lambda res, d: (_dot_nt(_b(d), _b(res[1])), _dot_tn(_b(res[0]), _b(d))))


@jax.custom_vjp
def _mm_nt(a, b):
    return _dot_nt(_b(a), _b(b))


_mm_nt.defvjp(lambda a, b: (_mm_nt(a, b), (a, b)),
              lambda res, d: (_dot(_b(d), _b(res[1])), _dot_tn(_b(d), _b(res[0]))))


def _dot_split(dot, a, b):
    ah, bh = _b(a), _b(b)
    al, bl = _b(a - ah.astype(F32)), _b(b - bh.astype(F32))
    return (dot(ah, bl) + dot(al, bh)) + dot(ah, bh)


@jax.custom_vjp
def _mm_scores(a, b):
    return _dot_nt(_b(a), _b(b))


_mm_scores.defvjp(lambda a, b: (_mm_scores(a, b), (a, b)),
                  lambda res, d: (_dot_split(_dot, d, res[1]), _dot_split(_dot_tn, d, res[0])))


@jax.custom_vjp
def _mm_tn(a, b):
    return _dot_tn(_b(a), _b(b))


_mm_tn.defvjp(lambda a, b: (_mm_tn(a, b), (a, b)),
              lambda res, d: (_dot_nt(_b(res[1]), _b(d)), _dot(_b(res[0]), _b(d))))


@jax.custom_vjp
def _split_heads(x):
    return tuple(x[:, h * A_DK:(h + 1) * A_DK] for h in range(A_HEADS))


def _split_heads_fwd(x):
    return _split_heads(x), None


def _split_heads_bwd(_, parts):
    return (jnp.concatenate(parts, axis=1),)


_split_heads.defvjp(_split_heads_fwd, _split_heads_bwd)


def _hgrn2_chunk_fast(sts, q, fl, iv, gl, l0, l1, l2, ng):
    lb = _lower_bound(l0, l1, l2)
    f = lb + (1.0 - lb) * jax.nn.sigmoid(fl)
    return _hgrn2_fast_core(sts, q, f, jnp.log(f), iv, gl, ng)


def _hgrn2_fast_core(sts, q, f, logf, iv, gl, ng):
    g = _cumsum_rows(logf)
    g_mid, g_last = _half_sums(logf)
    g_last = g_mid + g_last
    k = 1.0 - f
    qf = jax.nn.silu(q)
    qms = _split_heads(qf * jnp.exp(g - g_mid))
    kms = _split_heads(k * jnp.exp(g_mid - g))
    qgs = _split_heads(qf * jnp.exp(g))
    kds = _split_heads(k * jnp.exp(g_last - g))
    ivs = _split_heads(iv)
    decays = _split_heads(jnp.exp(g_last))
    n = q.shape[0]
    causal = lax.broadcasted_iota(jnp.int32, (n, n), 0) >= lax.broadcasted_iota(jnp.int32, (n, n), 1)
    raw = [_mm_scores(qm, km) for qm, km in zip(qms, kms)]
    inter = [_mm_nt(qg, st) for qg, st in zip(qgs, sts)]
    scores = [jnp.where(causal, s, 0.0) for s in raw]
    os = [a + _mm(s, v) for a, s, v in zip(inter, scores, ivs)]
    new_sts = [st * d + _mm_tn(v, kd) for st, d, v, kd in zip(sts, decays, ivs, kds)]
    os = [o * lax.rsqrt(jnp.mean(o * o, axis=-1, keepdims=True) + EPS) for o in os]
    return new_sts, jnp.concatenate(os, axis=1) * ng * jax.nn.silu(gl)


A_STEP_CHUNKS = 4


def _chunk_rows(j):
    return pl.ds(pl.multiple_of(j * A_CHUNK, A_CHUNK), A_CHUNK)


def _sub_rows(j, i):
    return pl.ds(pl.multiple_of(j * A_CHUNK + i * A_SUB, A_SUB), A_SUB)


def _sub_blocks(ref, head, j):
    lanes = slice(head * A_DK, (head + 1) * A_DK)
    return [ref[_sub_rows(j, i), lanes] for i in range(A_CHUNK // A_SUB)]


def hgrn2_fwd(proj, lb_table, a_norm, batch, name, exchange=None):
    t = proj.shape[0]
    n_steps = t // batch // (A_CHUNK * A_STEP_CHUNKS)
    rows = A_CHUNK * A_STEP_CHUNKS

    def body(q_ref, f_ref, i_ref, g_ref, lb_ref, ng_ref, o_ref, st_ref, dec_ref, st):
        @pl.when(pl.program_id(1) == 0)
        def _():
            st[...] = jnp.zeros_like(st)

        def chunk(j, carry):
            r = _chunk_rows(j)
            st_ref[j] = st[...]
            lb = _lower_bound(lb_ref[0:1, :], lb_ref[1:2, :], lb_ref[2:3, :])
            f = lb + (1.0 - lb) * jax.nn.sigmoid(f_ref[r, :])
            logf = jnp.log(f)
            decay = jnp.minimum(*_half_sums(logf))
            dec_ref[j] = decay
            mild = jnp.min(decay) >= -A_MAX_LOG_DECAY

            @pl.when(mild)
            def _():
                new_sts, o = _hgrn2_fast_core([st[h] for h in range(A_HEADS)], q_ref[r, :], f, logf,
                                              i_ref[r, :], g_ref[r, :], ng_ref[...])
                for h in range(A_HEADS):
                    st[h] = new_sts[h]
                o_ref[r, :] = o.astype(BF16)

            @pl.when(jnp.logical_not(mild))
            def _():
                for h in range(A_HEADS):
                    lanes = slice(h * A_DK, (h + 1) * A_DK)
                    new_st, outs = _hgrn2_chunk(
                        st[h], _sub_blocks(q_ref, h, j), _sub_blocks(f_ref, h, j), _sub_blocks(i_ref, h, j),
                        _sub_blocks(g_ref, h, j), lb_ref[0:1, lanes], lb_ref[1:2, lanes], lb_ref[2:3, lanes],
                        ng_ref[:, lanes])
                    st[h] = new_st
                    for i, o in enumerate(outs):
                        o_ref[_sub_rows(j, i), lanes] = o.astype(BF16)

            return carry

        lax.fori_loop(0, A_STEP_CHUNKS, chunk, 0)

    def part(k):
        return pl.BlockSpec((rows, A_WIDTH), lambda b, n: (b * n_steps + n, k))

    return _call(
        body, name=name, grid=(batch, n_steps),
        in_specs=[part(0), part(1), part(2), part(3),
                  pl.BlockSpec((3, A_WIDTH), lambda b, n: (0, 0)), pl.BlockSpec((1, A_WIDTH), lambda b, n: (0, 0))],
        out_specs=[part(0),
                   pl.BlockSpec((A_STEP_CHUNKS, A_HEADS, A_DK, A_DK), lambda b, n: (b * n_steps + n, 0, 0, 0)),
                   pl.BlockSpec((A_STEP_CHUNKS, 1, A_WIDTH), lambda b, n: (b * n_steps + n, 0, 0))],
        out_shape=[jax.ShapeDtypeStruct((t, A_WIDTH), BF16),
                   jax.ShapeDtypeStruct((t // A_CHUNK, A_HEADS, A_DK, A_DK), F32),
                   jax.ShapeDtypeStruct((t // A_CHUNK, 1, A_WIDTH), F32)],
        scratch_shapes=[pltpu.VMEM((A_HEADS, A_DK, A_DK), F32)],
        args=(proj, proj, proj, proj, lb_table, a_norm), exchange=exchange)


def hgrn2_bwd(proj, states, decays, lb_table, a_norm, do, batch, name, exchange=None):
    t = proj.shape[0]
    n_steps = t // batch // (A_CHUNK * A_STEP_CHUNKS)
    rows = A_CHUNK * A_STEP_CHUNKS

    def body(q_ref, f_ref, i_ref, g_ref, st_ref, dec_ref, lb_ref, ng_ref, do_ref, dp_ref, dlb_ref, dng_ref, dst):
        @pl.when(jnp.logical_and(pl.program_id(0) == 0, pl.program_id(1) == 0))
        def _():
            dlb_ref[...] = jnp.zeros_like(dlb_ref)
            dng_ref[...] = jnp.zeros_like(dng_ref)

        @pl.when(pl.program_id(1) == 0)
        def _():
            dst[...] = jnp.zeros_like(dst)

        def chunk(jj, carry):
            j = A_STEP_CHUNKS - 1 - jj
            r = _chunk_rows(j)
            mild = jnp.min(dec_ref[j]) >= -A_MAX_LOG_DECAY

            @pl.when(mild)
            def _():
                _, vjp = jax.vjp(
                    _hgrn2_chunk_fast, [st_ref[j, h] for h in range(A_HEADS)], q_ref[r, :], f_ref[r, :],
                    i_ref[r, :], g_ref[r, :], lb_ref[0:1, :], lb_ref[1:2, :], lb_ref[2:3, :], ng_ref[...])
                d_sts, dq, df, di, dg, dl0, dl1, dl2, dng = vjp(
                    ([dst[h] for h in range(A_HEADS)], do_ref[r, :].astype(F32)))
                for h in range(A_HEADS):
                    dst[h] = d_sts[h]
                for k, part in enumerate((dq, df, di, dg)):
                    dp_ref[r, k * A_WIDTH:(k + 1) * A_WIDTH] = part
                for row, val in enumerate((dl0, dl1, dl2)):
                    dlb_ref[row:row + 1, :] += val
                dng_ref[...] += dng

            @pl.when(jnp.logical_not(mild))
            def _():
                for h in range(A_HEADS):
                    lanes = slice(h * A_DK, (h + 1) * A_DK)
                    _, vjp = jax.vjp(
                        _hgrn2_chunk, st_ref[j, h], _sub_blocks(q_ref, h, j), _sub_blocks(f_ref, h, j),
                        _sub_blocks(i_ref, h, j), _sub_blocks(g_ref, h, j), lb_ref[0:1, lanes], lb_ref[1:2, lanes],
                        lb_ref[2:3, lanes], ng_ref[:, lanes])
                    douts = [x.astype(F32) for x in _sub_blocks(do_ref, h, j)]
                    d_st, dqs, dfs, dis, dgs, dl0, dl1, dl2, dng = vjp((dst[h], douts))
                    dst[h] = d_st
                    for k, parts in enumerate((dqs, dfs, dis, dgs)):
                        for i in range(A_CHUNK // A_SUB):
                            dp_ref[_sub_rows(j, i), k * A_WIDTH + h * A_DK:k * A_WIDTH + (h + 1) * A_DK] = parts[i]
                    for row, val in enumerate((dl0, dl1, dl2)):
                        dlb_ref[row:row + 1, lanes] += val
                    dng_ref[:, lanes] += dng

            return carry

        lax.fori_loop(0, A_STEP_CHUNKS, chunk, 0)

    def rev(b, n):
        return b * n_steps + (n_steps - 1 - n)

    def part(k):
        return pl.BlockSpec((rows, A_WIDTH), lambda b, n: (rev(b, n), k))

    const3 = pl.BlockSpec((3, A_WIDTH), lambda b, n: (0, 0))
    const1 = pl.BlockSpec((1, A_WIDTH), lambda b, n: (0, 0))
    return _call(
        body, name=name, grid=(batch, n_steps),
        in_specs=[part(0), part(1), part(2), part(3),
                  pl.BlockSpec((A_STEP_CHUNKS, A_HEADS, A_DK, A_DK), lambda b, n: (rev(b, n), 0, 0, 0)),
                  pl.BlockSpec((A_STEP_CHUNKS, 1, A_WIDTH), lambda b, n: (rev(b, n), 0, 0)),
                  const3, const1, part(0)],
        out_specs=[pl.BlockSpec((rows, 4 * A_WIDTH), lambda b, n: (rev(b, n), 0)), const3, const1],
        out_shape=[jax.ShapeDtypeStruct((t, 4 * A_WIDTH + 2 * B_WIDTH), F32),
                   jax.ShapeDtypeStruct((3, A_WIDTH), F32), jax.ShapeDtypeStruct((1, A_WIDTH), F32)],
        scratch_shapes=[pltpu.VMEM((A_HEADS, A_DK, A_DK), F32)],
        args=(proj, proj, proj, proj, states, decays, lb_table, a_norm, do), exchange=exchange)


B_GDIM = B_WIDTH // B_GROUPS
B_ROWS = 512


def _gmlp_chunk(ubs, vbs, lngs, lnbs, ws, bcols):
    vs = [jax.nn.gelu(v) for v in vbs]
    mu = sum(jnp.sum(v, axis=-1, keepdims=True) for v in vs) * (1.0 / B_WIDTH)
    var = sum(jnp.sum(jnp.square(v - mu), axis=-1, keepdims=True) for v in vs) * (1.0 / B_WIDTH)
    rstd = lax.rsqrt(var + EPS)
    tril = (lax.broadcasted_iota(jnp.int32, (B_CHUNK, B_CHUNK), 0)
            >= lax.broadcasted_iota(jnp.int32, (B_CHUNK, B_CHUNK), 1))
    outs = []
    for g in range(B_GROUPS):
        vn = (vs[g] - mu) * rstd * lngs[g] + lnbs[g]
        w = jnp.where(tril, ws[g], 0.0).astype(BF16)
        outs.append(jax.nn.gelu(ubs[g]) * (_dot(w, vn.astype(BF16)) + bcols[g]))
    return outs


def _gmlp_args(u_ref, v_ref, lng_ref, lnb_ref, w_ref, bt_ref, rows):
    def groups(ref):
        return [ref[rows, g * B_GDIM:(g + 1) * B_GDIM] for g in range(B_GROUPS)]

    def vec(ref):
        return [ref[:, g * B_GDIM:(g + 1) * B_GDIM] for g in range(B_GROUPS)]

    return (groups(u_ref), groups(v_ref), vec(lng_ref), vec(lnb_ref),
            [w_ref[g] for g in range(B_GROUPS)], [bt_ref[:, g:g + 1] for g in range(B_GROUPS)])


def gmlp_fwd(proj, oa, ln_g, ln_b, w, bias_t, name, exchange=None):
    t = proj.shape[0]

    def body(u_ref, v_ref, oa_ref, lng_ref, lnb_ref, w_ref, bt_ref, o_ref):
        o_ref[:, 0:A_WIDTH] = oa_ref[...]
        for n in range(B_ROWS // B_CHUNK):
            rows = slice(n * B_CHUNK, (n + 1) * B_CHUNK)
            outs = _gmlp_chunk(*_gmlp_args(u_ref, v_ref, lng_ref, lnb_ref, w_ref, bt_ref, rows))
            for g, o in enumerate(outs):
                o_ref[rows, A_WIDTH + g * B_GDIM:A_WIDTH + (g + 1) * B_GDIM] = o.astype(BF16)

    vec = pl.BlockSpec((1, B_WIDTH), lambda i: (0, 0))
    return _call(
        body, name=name, grid=(t // B_ROWS,),
        in_specs=[pl.BlockSpec((B_ROWS, B_WIDTH), lambda i: (i, 4)), pl.BlockSpec((B_ROWS, B_WIDTH), lambda i: (i, 5)),
                  pl.BlockSpec((B_ROWS, A_WIDTH), lambda i: (i, 0)), vec, vec,
                  pl.BlockSpec((B_GROUPS, B_CHUNK, B_CHUNK), lambda i: (0, 0, 0)),
                  pl.BlockSpec((B_CHUNK, B_GROUPS), lambda i: (0, 0))],
        out_specs=[pl.BlockSpec((B_ROWS, A_WIDTH + B_WIDTH), lambda i: (i, 0))],
        out_shape=[jax.ShapeDtypeStruct((t, A_WIDTH + B_WIDTH), BF16)],
        args=(proj, proj, oa, ln_g, ln_b, w, bias_t), exchange=exchange)


def gmlp_bwd(proj, dmixin, ln_g, ln_b, w, bias_t, dproj, name, exchange=None):
    t = proj.shape[0]

    def body(u_ref, v_ref, do_ref, lng_ref, lnb_ref, w_ref, bt_ref, dp_in_ref,
             dp_ref, dlng_ref, dlnb_ref, dw_ref, dbt_ref):
        del dp_in_ref

        @pl.when(pl.program_id(0) == 0)
        def _():
            for ref in (dlng_ref, dlnb_ref, dw_ref, dbt_ref):
                ref[...] = jnp.zeros_like(ref)

        for n in range(B_ROWS // B_CHUNK):
            rows = slice(n * B_CHUNK, (n + 1) * B_CHUNK)
            _, vjp = jax.vjp(_gmlp_chunk, *_gmlp_args(u_ref, v_ref, lng_ref, lnb_ref, w_ref, bt_ref, rows))
            douts = [do_ref[rows, g * B_GDIM:(g + 1) * B_GDIM] for g in range(B_GROUPS)]
            dus, dvs, dlngs, dlnbs, dws, dbs = vjp(douts)
            for g in range(B_GROUPS):
                lanes = slice(g * B_GDIM, (g + 1) * B_GDIM)
                dp_ref[rows, lanes] = dus[g]
                dp_ref[rows, B_WIDTH + g * B_GDIM:B_WIDTH + (g + 1) * B_GDIM] = dvs[g]
                dlng_ref[:, lanes] += dlngs[g]
                dlnb_ref[:, lanes] += dlnbs[g]
                dw_ref[g] += dws[g]
                dbt_ref[:, g:g + 1] += dbs[g]

    vec = pl.BlockSpec((1, B_WIDTH), lambda i: (0, 0))
    wspec = pl.BlockSpec((B_GROUPS, B_CHUNK, B_CHUNK), lambda i: (0, 0, 0))
    bspec = pl.BlockSpec((B_CHUNK, B_GROUPS), lambda i: (0, 0))
    return _call(
        body, name=name, grid=(t // B_ROWS,),
        in_specs=[pl.BlockSpec((B_ROWS, B_WIDTH), lambda i: (i, 4)), pl.BlockSpec((B_ROWS, B_WIDTH), lambda i: (i, 5)),
                  pl.BlockSpec((B_ROWS, B_WIDTH), lambda i: (i, 1)), vec, vec, wspec, bspec,
                  pl.BlockSpec(memory_space=pl.ANY)],
        out_specs=[pl.BlockSpec((B_ROWS, 2 * B_WIDTH), lambda i: (i, 2)), vec, vec, wspec, bspec],
        out_shape=[jax.ShapeDtypeStruct(dproj.shape, F32), jax.ShapeDtypeStruct((1, B_WIDTH), F32),
                   jax.ShapeDtypeStruct((1, B_WIDTH), F32), jax.ShapeDtypeStruct((B_GROUPS, B_CHUNK, B_CHUNK), F32),
                   jax.ShapeDtypeStruct((B_CHUNK, B_GROUPS), F32)],
        aliases={7: 0}, args=(proj, proj, dmixin, ln_g, ln_b, w, bias_t, dproj), exchange=exchange)


C_FWD_BLOCKS = 8
C_BWD_BLOCKS = 4
C_PAIR = 2 * C_HEAD_DIM
C_PAIRS = C_HEADS // 2
C_SCALE = 1.0 / math.sqrt(C_HEAD_DIM)
C_ROT_DIM = 2 * C_ROT_HALF
ROPE_ROWS = 1024


def rope_tables(pos_col, name):
    t = pos_col.shape[0]

    def body(p_ref, c_ref, a_ref, b_ref):
        lane = jnp.bitwise_and(lax.broadcasted_iota(jnp.int32, (1, C_PAIR), 1), C_HEAD_DIM - 1)
        j = jnp.bitwise_and(lane, C_ROT_HALF - 1).astype(F32)
        inv = jnp.exp(j * (-math.log(ROPE_THETA) / C_ROT_HALF))
        ang = p_ref[...].astype(F32) * inv
        cos, sin = jnp.cos(ang), jnp.sin(ang)
        c_ref[...] = jnp.where(lane < C_ROT_DIM, cos, 1.0)
        a_ref[...] = jnp.where(lane < C_ROT_HALF, -sin, 0.0)
        b_ref[...] = jnp.where(jnp.logical_and(lane >= C_ROT_HALF, lane < C_ROT_DIM), sin, 0.0)

    tab = pl.BlockSpec((ROPE_ROWS, C_PAIR), lambda i: (i, 0))
    return pl.pallas_call(
        body, name=name, grid=(t // ROPE_ROWS,),
        in_specs=[pl.BlockSpec((ROPE_ROWS, 1), lambda i: (i, 0))],
        out_specs=[tab, tab, tab],
        out_shape=[jax.ShapeDtypeStruct((t, C_PAIR), F32)] * 3,
        compiler_params=_params(("arbitrary",)),
    )(pos_col)


def _rope(x, c, a, b):
    return x * c + pltpu.roll(x, C_PAIR - C_ROT_HALF, 1) * a + pltpu.roll(x, C_ROT_HALF, 1) * b


def _rope_t(d, c, a, b):
    return d * c + pltpu.roll(d * a, C_ROT_HALF, 1) + pltpu.roll(d * b, C_PAIR - C_ROT_HALF, 1)


C_RES = 16


def _residue_major(a, batch):
    return a.reshape(batch, SEQ // C_RES, C_RES, -1).transpose(0, 2, 1, 3).reshape(a.shape)


def _sequence_order(a, batch):
    return a.reshape(batch, C_RES, SEQ // C_RES, -1).transpose(0, 2, 1, 3).reshape(a.shape)


def _block_pieces(idx, dil):
    nblk = SEQ // dil // C_BLOCK
    r, n = idx // nblk, idx % nblk
    per = C_RES // dil
    size = C_BLOCK // per

    def pieces(blk):
        return [((dil * a + r) * (SEQ // C_RES) + size * blk, size) for a in range(per)]

    return pieces(n), pieces(jnp.maximum(n - 1, 0)), n > 0


def _get_rows(ref, pieces):
    return jnp.concatenate([ref[pl.ds(pl.multiple_of(start, 8), size), :] for start, size in pieces], axis=0)


def _set_rows(ref, pieces, val, add=False):
    for k, (start, size) in enumerate(pieces):
        rows = pl.ds(pl.multiple_of(start, 8), size)
        part = val[k * size:(k + 1) * size]
        ref[rows, :] = ref[rows, :] + part if add else part


def _head_masks():
    low = lax.broadcasted_iota(jnp.int32, (1, C_PAIR), 1) < C_HEAD_DIM
    return low, jnp.logical_not(low)


def _attn_mask(has_prev, dil):
    per = C_RES // dil
    size = C_BLOCK // per

    def position(x):
        x = jnp.bitwise_and(x, C_BLOCK - 1)
        return per * jnp.bitwise_and(x, size - 1) + x // size

    j = lax.broadcasted_iota(jnp.int32, (2 * C_BLOCK, 2 * C_BLOCK), 1)
    pi = position(lax.broadcasted_iota(jnp.int32, (2 * C_BLOCK, 2 * C_BLOCK), 0))
    pj = position(j)
    own = j < C_BLOCK
    return jnp.logical_or(jnp.logical_and(own, pj <= pi),
                          jnp.logical_and(jnp.logical_and(jnp.logical_not(own), pj >= pi), has_prev))


def _stack_heads(x):
    low, high = _head_masks()
    return jnp.concatenate([jnp.where(low, x, 0.0), jnp.where(high, x, 0.0)], axis=0)


def _unstack_heads(x):
    low, _ = _head_masks()
    return jnp.where(low, x[:C_BLOCK], x[C_BLOCK:])


def attn_fwd(qkv, cos_t, sin_a, sin_b, batch, name, exchange=None):
    t = qkv.shape[0]
    nbr = len(C_DILATIONS)

    def body(q_ref, k_ref, v_ref, c_ref, a_ref, b_ref, o_ref, l_ref, qs, ks, *stats):
        acc, mm, dd = stats[0:nbr], stats[nbr:2 * nbr], stats[2 * nbr:3 * nbr]
        c, a, b = c_ref[...], a_ref[...], b_ref[...]
        qs[...] = _rope(q_ref[...], c, a, b) * C_SCALE
        ks[...] = _rope(k_ref[...], c, a, b)

        def load(idx, dil):
            own, prev, has_prev = _block_pieces(idx, dil)
            return own, (has_prev, _get_rows(qs, own), _get_rows(ks, own), _get_rows(ks, prev),
                         _get_rows(v_ref, own), _get_rows(v_ref, prev))

        def scores(dil, has_prev, q, k_own, k_prev, v_own, v_prev):
            k_cat = jnp.concatenate([k_own, k_prev], axis=0).astype(BF16)
            return jnp.where(_attn_mask(has_prev, dil), _dot_nt(_stack_heads(q).astype(BF16), k_cat), NEG_BIG)

        def softmax(s):
            m = jnp.max(s, axis=-1, keepdims=True)
            p = jnp.exp(s - m)
            return p.astype(BF16), m, jnp.sum(p, axis=-1, keepdims=True)

        def values(pb, has_prev, q, k_own, k_prev, v_own, v_prev):
            low, high = _head_masks()
            v_cat = jnp.concatenate([v_own, v_prev], axis=0)
            p_wide = jnp.concatenate([pb[:C_BLOCK], pb[C_BLOCK:]], axis=1)
            v_tall = jnp.concatenate([jnp.where(low, v_cat, 0.0), jnp.where(high, v_cat, 0.0)], axis=0).astype(BF16)
            return _dot(p_wide, v_tall)

        for bi, dil in enumerate(C_DILATIONS):
            def pair(i, carry, bi=bi, dil=dil):
                low, _ = _head_masks()
                loaded = [load(C_FWD_BLOCKS * i + k, dil) for k in range(C_FWD_BLOCKS)]
                ss = [scores(dil, *ops) for _, ops in loaded]
                sm = [softmax(s) for s in ss]
                pvs = [values(pb, *ops) for (pb, _, _), (_, ops) in zip(sm, loaded)]
                for (own, _), (_, m, den), pv in zip(loaded, sm, pvs):
                    _set_rows(acc[bi], own, pv)
                    _set_rows(mm[bi], own, jnp.where(low, m[:C_BLOCK], m[C_BLOCK:]))
                    _set_rows(dd[bi], own, jnp.where(low, den[:C_BLOCK], den[C_BLOCK:]))
                return carry

            lax.fori_loop(0, SEQ // C_BLOCK // C_FWD_BLOCKS, pair, 0)
        step = 2 * C_BLOCK
        for r0 in range(0, SEQ, step):
            rr = slice(r0, r0 + step)
            ms = [mm[g][rr, :] for g in range(nbr)]
            m_all = functools.reduce(jnp.maximum, ms)
            ws = [jnp.exp(m - m_all) for m in ms]
            num = sum(acc[g][rr, :] * ws[g] for g in range(nbr))
            den = sum(dd[g][rr, :] * ws[g] for g in range(nbr))
            o_ref[rr, :] = (num / den).astype(BF16)
            l_ref[rr, :] = m_all + jnp.log(den)

    def col(k):
        return pl.BlockSpec((SEQ, C_PAIR), lambda b, p: (b, k * C_PAIRS + p))

    tab = pl.BlockSpec((SEQ, C_PAIR), lambda b, p: (b, 0))
    return _call(
        body, name=name, grid=(batch, C_PAIRS),
        in_specs=[col(0), col(1), col(2), tab, tab, tab],
        out_specs=[col(0), col(0)],
        out_shape=[jax.ShapeDtypeStruct((t, D_MODEL), BF16), jax.ShapeDtypeStruct((t, D_MODEL), F32)],
        scratch_shapes=[pltpu.VMEM((SEQ, C_PAIR), F32)] * (2 + 3 * nbr),
        args=(qkv, qkv, qkv, cos_t, sin_a, sin_b), exchange=exchange)


def attn_bwd(qkv, cos_t, sin_a, sin_b, o, lse, do, batch, name, exchange=None):
    t = qkv.shape[0]

    def body(q_ref, k_ref, v_ref, c_ref, a_ref, b_ref, o_ref, l_ref, do_ref, dqkv_ref, qs, ks, dqs, dks, dvs, dlt):
        low, _ = _head_masks()
        c, a, b = c_ref[...], a_ref[...], b_ref[...]
        qs[...] = _rope(q_ref[...], c, a, b) * C_SCALE
        ks[...] = _rope(k_ref[...], c, a, b)
        prod = do_ref[...] * o_ref[...].astype(F32)
        s_low = jnp.sum(jnp.where(low, prod, 0.0), axis=-1, keepdims=True)
        s_all = jnp.sum(prod, axis=-1, keepdims=True)
        dlt[...] = jnp.where(low, s_low, s_all - s_low)
        dqs[...] = jnp.zeros_like(dqs)
        dks[...] = jnp.zeros_like(dks)
        dvs[...] = jnp.zeros_like(dvs)

        def load(idx, dil):
            own, prev, has_prev = _block_pieces(idx, dil)
            return (own, prev), (has_prev, _get_rows(qs, own), _get_rows(do_ref, own), _get_rows(ks, own),
                                 _get_rows(ks, prev), _get_rows(v_ref, own), _get_rows(v_ref, prev),
                                 _get_rows(l_ref, own), _get_rows(dlt, own))

        def operands(dil, has_prev, q, do, k_own, k_prev, v_own, v_prev, l_full, d_full):
            lcol = jnp.concatenate([l_full[:, 0:1], l_full[:, C_HEAD_DIM:C_HEAD_DIM + 1]], axis=0)
            dcol = jnp.concatenate([d_full[:, 0:1], d_full[:, C_HEAD_DIM:C_HEAD_DIM + 1]], axis=0)
            return (_stack_heads(q).astype(BF16), _stack_heads(do).astype(BF16),
                    jnp.concatenate([k_own, k_prev], axis=0).astype(BF16),
                    jnp.concatenate([v_own, v_prev], axis=0).astype(BF16), lcol, dcol, _attn_mask(has_prev, dil))

        for dil in C_DILATIONS:
            def pair(i, carry, dil=dil):
                loaded = [load(C_BWD_BLOCKS * i + k, dil) for k in range(C_BWD_BLOCKS)]
                ops = [operands(dil, *o) for _, o in loaded]
                ss = [_dot_nt(q_stack, k_cat) for q_stack, _, k_cat, _, _, _, _ in ops]
                dps = [_dot_nt(do_stack, v_cat) for _, do_stack, _, v_cat, _, _, _ in ops]
                ps = [jnp.exp(jnp.where(o[6], s, NEG_BIG) - o[4]) for s, o in zip(ss, ops)]
                dss = [(p * (dp - o[5])).astype(BF16) for p, dp, o in zip(ps, dps, ops)]
                dvs_ = [_dot_tn(p.astype(BF16), o[1]) for p, o in zip(ps, ops)]
                dks_ = [_dot_tn(ds, o[0]) for ds, o in zip(dss, ops)]
                dqs_ = [_unstack_heads(_dot(ds, o[2])) for ds, o in zip(dss, ops)]
                for ((own, prev), _), dq, dk_cat, dv_cat in zip(loaded, dqs_, dks_, dvs_):
                    _set_rows(dqs, own, dq, add=True)
                    _set_rows(dks, own, dk_cat[:C_BLOCK], add=True)
                    _set_rows(dvs, own, dv_cat[:C_BLOCK], add=True)
                    _set_rows(dks, prev, dk_cat[C_BLOCK:], add=True)
                    _set_rows(dvs, prev, dv_cat[C_BLOCK:], add=True)
                return carry

            lax.fori_loop(0, SEQ // C_BLOCK // C_BWD_BLOCKS, pair, 0)
        dqkv_ref[0] = _rope_t(dqs[...] * C_SCALE, c, a, b).astype(BF16)
        dqkv_ref[1] = _rope_t(dks[...], c, a, b).astype(BF16)
        dqkv_ref[2] = dvs[...].astype(BF16)

    def col(k):
        return pl.BlockSpec((SEQ, C_PAIR), lambda b, p: (b, k * C_PAIRS + p))

    tab = pl.BlockSpec((SEQ, C_PAIR), lambda b, p: (b, 0))
    return _call(
        body, name=name, grid=(batch, C_PAIRS),
        in_specs=[col(0), col(1), col(2), tab, tab, tab, col(0), col(0), col(0)],
        out_specs=[pl.BlockSpec((3, SEQ, C_PAIR), lambda b, p: (0, b, p))],
        out_shape=[jax.ShapeDtypeStruct((3, t, D_MODEL), BF16)],
        scratch_shapes=[pltpu.VMEM((SEQ, C_PAIR), F32)] * 6,
        args=(qkv, qkv, qkv, cos_t, sin_a, sin_b, o, lse, do), exchange=exchange)


def sibling_swap(arrays, name):
    n = len(arrays)

    def body(*refs):
        ins, outs = refs[:n], refs[n:2 * n]
        send_sems, recv_sems = refs[2 * n:]
        x, y, c, _ = _place()
        sends = []
        for a in range(n):
            cp = pltpu.make_async_remote_copy(
                src_ref=ins[a], dst_ref=outs[a], send_sem=send_sems.at[a], recv_sem=recv_sems.at[a],
                device_id=(x, y, 1 - c), device_id_type=MESH)
            cp.start()
            sends.append(cp)
        for cp in sends:
            cp.wait_recv()
        for cp in sends:
            cp.wait_send()

    return pl.pallas_call(
        body, name=name,
        in_specs=[ANY] * n, out_specs=[ANY] * n,
        out_shape=[jax.ShapeDtypeStruct(s.shape, s.dtype) for s in arrays],
        scratch_shapes=[pltpu.SemaphoreType.DMA((n,)), pltpu.SemaphoreType.DMA((n,))],
    )(*arrays)


def allreduce_small(slab, name):
    rows, lanes = slab.shape

    def body(x_ref, out_ref, gath, send_sems, recv_sems, local_sem):
        x, y, c, chips = _place()
        me, sibling = (x, y, c), (x, y, 1 - c)

        def slot(px, py, pc):
            return gath.at[4 * px + 2 * py + pc]

        def copy(k, block, to, src=None):
            return pltpu.make_async_remote_copy(
                src_ref=slot(*block) if src is None else src, dst_ref=slot(*block),
                send_sem=send_sems.at[k], recv_sem=recv_sems.at[k], device_id=to, device_id_type=MESH)

        mine = pltpu.make_async_copy(x_ref, slot(*me), local_sem)
        mine.start()
        first = [copy(0, me, sibling, src=x_ref)]
        first += [copy(1 + j, me, (*chip, c), src=x_ref) for j, chip in enumerate(chips)]
        for cp in first:
            cp.start()
        passed = [copy(4 + j, (*chip, c), sibling) for j, chip in enumerate(chips)]
        for j, chip in enumerate(chips):
            copy(1 + j, (*chip, c), me).wait_recv()
            passed[j].start()
        copy(0, sibling, me).wait_recv()
        for j, chip in enumerate(chips):
            copy(4 + j, (*chip, 1 - c), me).wait_recv()
        for cp in first + passed:
            cp.wait_send()
        mine.wait()
        total = gath[0]
        for d in range(1, N_DEV):
            total = total + gath[d]
        out_ref[...] = total

    return pl.pallas_call(
        body, name=name,
        in_specs=[pl.BlockSpec(memory_space=pltpu.VMEM)],
        out_specs=pl.BlockSpec(memory_space=pltpu.VMEM),
        out_shape=jax.ShapeDtypeStruct((rows, lanes), F32),
        scratch_shapes=[pltpu.VMEM((N_DEV, rows, lanes), F32),
                        pltpu.SemaphoreType.DMA((7,)), pltpu.SemaphoreType.DMA((7,)), pltpu.SemaphoreType.DMA],
    )(slab)


ELT_ROWS = 512


def reduce_slabs(r, name, part=0, parts=1, into=None):
    _, rows, cols = r.shape
    br = min(rows, ELT_ROWS)
    nblk = rows // br

    def body(r_ref, *rest):
        o_ref = rest[-1]
        o_ref[...] = ((r_ref[3].astype(F32) + r_ref[0].astype(F32)) + r_ref[1].astype(F32)) + r_ref[2].astype(F32)

    return pl.pallas_call(
        body, name=name, grid=(nblk,),
        in_specs=[pl.BlockSpec((N_CHIPS, br, cols), lambda i: (0, i, 0))] + ([] if into is None else [ANY]),
        out_specs=pl.BlockSpec((br, cols), lambda i: (part * nblk + i, 0)),
        out_shape=jax.ShapeDtypeStruct((parts * rows, cols), F32),
        input_output_aliases={} if into is None else {1: 0},
        compiler_params=_params(("arbitrary",)),
    )(*([r] if into is None else [r, into]))


def _adamw(w, g, m, v):
    m = ADAM_B1 * m + (1.0 - ADAM_B1) * g
    v = ADAM_B2 * v + (1.0 - ADAM_B2) * jnp.square(g)
    m_hat = m / (1.0 - ADAM_B1 ** ADAM_STEP)
    v_hat = v / (1.0 - ADAM_B2 ** ADAM_STEP)
    delta = -ADAM_LR * (m_hat / (jnp.sqrt(v_hat) + ADAM_EPS) + ADAM_WD * w)
    return delta, m, v


def adamw_big(w, s_mine, s_sibling, m, v, name):
    rows, cols = w.shape

    def body(w_ref, a_ref, b_ref, m_ref, v_ref, g_out, d_out, m_out, v_out):
        g = a_ref[...] + b_ref[...]
        g_out[...] = g
        d_out[...], m_out[...], v_out[...] = _adamw(w_ref[...], g, m_ref[...], v_ref[...])

    blk = pl.BlockSpec((min(rows, ELT_ROWS), cols), lambda i: (i, 0))
    out = jax.ShapeDtypeStruct((rows, cols), F32)
    return pl.pallas_call(
        body, name=name, grid=(rows // min(rows, ELT_ROWS),),
        in_specs=[blk] * 5, out_specs=[blk] * 4, out_shape=[out] * 4,
        compiler_params=_params(("arbitrary",)),
    )(w, s_mine, s_sibling, m, v)


def adamw_small(ws, gs, ms, vs, name):
    n = len(ws)

    def body(*refs):
        w_refs, g_refs, m_refs, v_refs = (refs[k * n:(k + 1) * n] for k in range(4))
        d_out, m_out, v_out = (refs[(4 + k) * n:(5 + k) * n] for k in range(3))
        for i in range(n):
            d_out[i][...], m_out[i][...], v_out[i][...] = _adamw(
                w_refs[i][...], g_refs[i][...], m_refs[i][...], v_refs[i][...])

    outs = [jax.ShapeDtypeStruct(w.shape, F32) for w in ws]
    res = pl.pallas_call(body, name=name, out_shape=outs * 3)(*ws, *gs, *ms, *vs)
    return res[:n], res[n:2 * n], res[2 * n:]


SLAB_LANES = 128
SLAB_ROW_ALIGN = 8


def _pack(parts):
    flat = jnp.concatenate([p.reshape(-1) for p in parts])
    rows = -(-flat.shape[0] // (SLAB_LANES * SLAB_ROW_ALIGN)) * SLAB_ROW_ALIGN
    flat = jnp.pad(flat, (0, rows * SLAB_LANES - flat.shape[0]))
    return flat.reshape(rows, SLAB_LANES)


def _unpack(slab, shapes):
    flat = slab.reshape(-1)
    out, pos = [], 0
    for s in shapes:
        size = math.prod(s)
        out.append(flat[pos:pos + size].reshape(s))
        pos += size
    return out


def kernel(x, positions, norm_mix_pre, norm_mix_post, norm_ffn_pre, norm_ffn_post, w_in_even, lb_table, a_norm, b_ln_g, b_ln_b, b_ws, b_bias, w_out_even, w_in_odd, w_out_odd, w_ff1, w_ff2, loss_target, m_norm_mix_pre, m_norm_mix_post, m_norm_ffn_pre, m_norm_ffn_post, m_w_in_even, m_lb_table, m_a_norm, m_b_ln_g, m_b_ln_b, m_b_ws, m_b_bias, m_w_out_even, m_w_in_odd, m_w_out_odd, m_w_ff1, m_w_ff2, v_norm_mix_pre, v_norm_mix_post, v_norm_ffn_pre, v_norm_ffn_post, v_w_in_even, v_lb_table, v_a_norm, v_b_ln_g, v_b_ln_b, v_b_ws, v_b_bias, v_w_out_even, v_w_in_odd, v_w_out_odd, v_w_ff1, v_w_ff2):
    batch = x.shape[0]
    t = batch * SEQ
    d = D_MODEL
    x0 = x.reshape(t, d)
    target = loss_target.reshape(t, d)

    def gain(p, layer):
        return p[layer:layer + 1]

    def gather(*shards):
        return _Exchange("gather", [w.astype(BF16) for w in shards])

    def scatter(*grads):
        return _Exchange("scatter", grads)

    (win_e,) = exchange_alone(gather(w_in_even[0]), "gather_in_even")
    bias_t = b_bias[0].T
    proj, h0, w1_0 = norm_matmul(x0, gain(norm_mix_pre, 0), win_e, "in_proj_even", exchange=gather(w_ff1[0]))
    oa, states, decays, w2_0 = hgrn2_fwd(proj, lb_table, a_norm, batch, "hgrn2_fwd", exchange=gather(w_ff2[0]))
    mixin, wout_e = gmlp_fwd(proj, oa, b_ln_g, b_ln_b, b_ws[0], bias_t, "gmlp_fwd", exchange=gather(w_out_even[0]))
    mix0, x1 = out_proj(mixin, wout_e, x0, gain(norm_mix_post, 0), "out_proj_even")
    x2, hf0, a0, y0, win_o, wout_o = ffn_fwd(x1, gain(norm_ffn_pre, 0), w1_0, w2_0, gain(norm_ffn_post, 0),
                                             "ffn_fwd_0", exchange=gather(w_in_odd[0], w_out_odd[0]))
    x2p = _residue_major(x2, batch)
    qkv, h1 = norm_matmul(x2p, gain(norm_mix_pre, 1), win_o, "in_proj_odd")
    cos_t, sin_a, sin_b = rope_tables(_residue_major(positions.reshape(t, 1), batch), "rope_tables")
    ao, lse, w1_1, w2_1 = attn_fwd(qkv, cos_t, sin_a, sin_b, batch, "attn_fwd", exchange=gather(w_ff1[1], w_ff2[1]))
    mix1, x3 = out_proj(ao, wout_o, x2p, gain(norm_mix_post, 1), "out_proj_odd")
    dx4, hf1, a1, y1, loss_part = ffn_fwd(x3, gain(norm_ffn_pre, 1), w1_1, w2_1, gain(norm_ffn_post, 1),
                                          "ffn_fwd_1", target=_residue_major(target, batch))

    hc = D_FF // N_CHIPS
    dx3, dy1, da1, dg_fpre1, dg_fpost1 = ffn_bwd(
        dx4, x3, y1, a1, gain(norm_ffn_pre, 1), gain(norm_ffn_post, 1), w1_1, w2_1, "ffn_bwd_1")
    g_w1_1 = weight_grad(hf1, da1, "b", d, hc, False, "wgrad_ff1_1")
    g_w2_1 = weight_grad(a1, dy1, "a", hc, d, True, "wgrad_ff2_1")
    dmix1, dao, dg_mpost1 = out_proj_bwd(dx3, mix1, gain(norm_mix_post, 1), wout_o, "out_proj_bwd_odd")
    g_wout_o = weight_grad(ao, dmix1, "a", d // N_CHIPS, d, False, "wgrad_out_odd")
    dqkv, r_w1_1, r_w2_1, r_wout_o = attn_bwd(qkv, cos_t, sin_a, sin_b, ao, lse, dao, batch, "attn_bwd",
                                              exchange=scatter(g_w1_1, g_w2_1, g_wout_o))
    dx2p, dg_mpre1 = norm_matmul_bwd(dqkv, win_o, x2p, gain(norm_mix_pre, 1), dx3, "in_proj_bwd_odd")
    dx2 = _sequence_order(dx2p, batch)
    g_win_o = weight_grad_stacked(h1, dqkv, 3 * d // N_CHIPS, "wgrad_in_odd")
    dx1, dy0, da0, dg_fpre0, dg_fpost0, r_win_o = ffn_bwd(
        dx2, x1, y0, a0, gain(norm_ffn_pre, 0), gain(norm_ffn_post, 0), w1_0, w2_0, "ffn_bwd_0",
        exchange=scatter(g_win_o))
    g_w1_0 = weight_grad(hf0, da0, "b", d, hc, False, "wgrad_ff1_0")
    g_w2_0 = weight_grad(a0, dy0, "a", hc, d, True, "wgrad_ff2_0")
    dmix0, dmixin, dg_mpost0 = out_proj_bwd(dx1, mix0, gain(norm_mix_post, 0), wout_e, "out_proj_bwd_even")
    g_wout_e = weight_grad(mixin, dmix0, "a", d // N_CHIPS, d, False, "wgrad_out_even")
    dproj, d_lb, d_anorm, r_w1_0 = hgrn2_bwd(
        proj, states, decays, lb_table, a_norm, dmixin, batch, "hgrn2_bwd", exchange=scatter(g_w1_0))
    dproj, d_lng, d_lnb, d_ws, d_bias_t, r_w2_0 = gmlp_bwd(
        proj, dmixin, b_ln_g, b_ln_b, b_ws[0], bias_t, dproj, "gmlp_bwd", exchange=scatter(g_w2_0))
    g_win_e, r_wout_e = weight_grad(h0, dproj, "b", d, 3 * d // N_CHIPS, False, "wgrad_in_even",
                                    exchange=scatter(g_wout_e))
    dx0, dg_mpre0, r_win_e = norm_matmul_bwd(dproj, win_e, x0, gain(norm_mix_pre, 0), dx1, "in_proj_bwd_even",
                                             exchange=scatter(g_win_e))
    grad_x = dx0.reshape(x.shape)

    s_w1 = reduce_slabs(r_w1_1, "reduce_ff1_1", part=1, parts=2)
    s_w1 = reduce_slabs(r_w1_0, "reduce_ff1_0", part=0, parts=2, into=s_w1)
    s_w2 = reduce_slabs(r_w2_1, "reduce_ff2_1", part=1, parts=2)
    s_w2 = reduce_slabs(r_w2_0, "reduce_ff2_0", part=0, parts=2, into=s_w2)
    sums = [reduce_slabs(r_win_e, "reduce_in_even"), reduce_slabs(r_wout_e, "reduce_out_even"),
            reduce_slabs(r_win_o, "reduce_in_odd"), reduce_slabs(r_wout_o, "reduce_out_odd"), s_w1, s_w2]
    sibling = sibling_swap(sums, "sibling_swap")
    big_w = [w_in_even, w_out_even, w_in_odd, w_out_odd, w_ff1, w_ff2]
    big_m = [m_w_in_even, m_w_out_even, m_w_in_odd, m_w_out_odd, m_w_ff1, m_w_ff2]
    big_v = [v_w_in_even, v_w_out_even, v_w_in_odd, v_w_out_odd, v_w_ff1, v_w_ff2]
    big = []
    for i, (w, m, v) in enumerate(zip(big_w, big_m, big_v)):
        two_d = (-1, w.shape[-1])
        res = adamw_big(w.reshape(two_d), sums[i], sibling[i], m.reshape(two_d), v.reshape(two_d), "adamw_big_%d" % i)
        big.append([r.reshape(w.shape) for r in res])

    small_w = [norm_mix_pre, norm_mix_post, norm_ffn_pre, norm_ffn_post, lb_table, a_norm, b_ln_g, b_ln_b, b_ws, b_bias]
    small_m = [m_norm_mix_pre, m_norm_mix_post, m_norm_ffn_pre, m_norm_ffn_post, m_lb_table, m_a_norm, m_b_ln_g,
               m_b_ln_b, m_b_ws, m_b_bias]
    small_v = [v_norm_mix_pre, v_norm_mix_post, v_norm_ffn_pre, v_norm_ffn_post, v_lb_table, v_a_norm, v_b_ln_g,
               v_b_ln_b, v_b_ws, v_b_bias]
    partial = [jnp.concatenate([dg_mpre0, dg_mpre1]), jnp.concatenate([dg_mpost0, dg_mpost1]),
               jnp.concatenate([dg_fpre0, dg_fpre1]), jnp.concatenate([dg_fpost0, dg_fpost1]),
               d_lb, d_anorm, d_lng, d_lnb, d_ws[None], d_bias_t.T[None]]
    *small_g, loss = _unpack(allreduce_small(_pack(partial + [loss_part]), "allreduce_small"),
                             [w.shape for w in small_w] + [()])
    small_d, small_nm, small_nv = adamw_small(small_w, small_g, small_m, small_v, "adamw_small")

    order = ["norm_mix_pre", "norm_mix_post", "norm_ffn_pre", "norm_ffn_post", "w_in_even", "lb_table", "a_norm",
             "b_ln_g", "b_ln_b", "b_ws", "b_bias", "w_out_even", "w_in_odd", "w_out_odd", "w_ff1", "w_ff2"]
    small_names = ["norm_mix_pre", "norm_mix_post", "norm_ffn_pre", "norm_ffn_post", "lb_table", "a_norm",
                   "b_ln_g", "b_ln_b", "b_ws", "b_bias"]
    big_names = ["w_in_even", "w_out_even", "w_in_odd", "w_out_odd", "w_ff1", "w_ff2"]
    grads, deltas, new_m, new_v = {}, {}, {}, {}
    for i, nm in enumerate(small_names):
        grads[nm], deltas[nm], new_m[nm], new_v[nm] = small_g[i], small_d[i], small_nm[i], small_nv[i]
    for i, nm in enumerate(big_names):
        grads[nm], deltas[nm], new_m[nm], new_v[nm] = big[i]
    return (loss, grad_x, *[grads[n] for n in order], *[deltas[n] for n in order],
            *[new_m[n] for n in order], *[new_v[n] for n in order])
```

```python
import functools
import math

import jax
import jax.numpy as jnp
from jax import lax
from jax.experimental import pallas as pl
from jax.experimental.pallas import tpu as pltpu

F32 = jnp.float32
BF16 = jnp.bfloat16
MESH = pl.DeviceIdType.MESH

D_MODEL = 1024
SEQ = 2048
D_FF = 4096
N_CHIPS = 4
A_WIDTH = 512
A_HEADS = 4
A_DK = 128
A_CHUNK = 64
A_SUB = 16
B_WIDTH = 512
B_GROUPS = 4
B_CHUNK = 128
C_HEADS = 16
C_HEAD_DIM = 64
C_ROT_HALF = 8
C_BLOCK = 128
C_DILATIONS = (1, 4, 16)
ROPE_THETA = 500000.0
EPS = 1e-6
ADAM_LR = 0.001
ADAM_B1 = 0.9
ADAM_B2 = 0.999
ADAM_EPS = 1e-08
ADAM_WD = 0.01
ADAM_STEP = 10

ROW_TILE = 512
FFN_ROWS = 1024
WGRAD_ROWS = 2048
VMEM_LIMIT = 56 * 1024 * 1024
NEG_BIG = -1e30


def _params(sem=None):
    return pltpu.CompilerParams(dimension_semantics=sem, vmem_limit_bytes=VMEM_LIMIT)


def _dot(a, b):
    return jnp.dot(a, b, preferred_element_type=F32)


def _dot_nt(a, b):
    return lax.dot_general(a, b, (((1,), (1,)), ((), ())), preferred_element_type=F32)


def _dot_tn(a, b):
    return lax.dot_general(a, b, (((0,), (0,)), ((), ())), preferred_element_type=F32)


def _rms(x, g):
    r = lax.rsqrt(jnp.mean(x * x, axis=-1, keepdims=True) + EPS)
    return x * r * g


def _rms_bwd(x, g, dy):
    r = lax.rsqrt(jnp.mean(x * x, axis=-1, keepdims=True) + EPS)
    xh = x * r
    dg = jnp.sum(dy * xh, axis=0, keepdims=True)
    dxh = dy * g
    dx = r * (dxh - xh * jnp.mean(dxh * xh, axis=-1, keepdims=True))
    return dx, dg


def _accumulate(ref, val, first):
    @pl.when(first)
    def _():
        ref[...] = val

    @pl.when(jnp.logical_not(first))
    def _():
        ref[...] += val


N_DEV = 8
ANY = pl.BlockSpec(memory_space=pl.ANY)


def _place():
    x, y, c = lax.axis_index("x"), lax.axis_index("y"), lax.axis_index("c")
    return x, y, c, [(1 - x, y), (x, 1 - y), (1 - x, 1 - y)]


class _Exchange:
    def __init__(self, kind, arrays):
        self.kind, self.arrays, self.n = kind, list(arrays), len(arrays)
        per_peer = pltpu.SemaphoreType.DMA((3 * self.n,))
        if kind == "gather":
            self.out_shape = [jax.ShapeDtypeStruct((N_CHIPS,) + a.shape, a.dtype) for a in self.arrays]
            self.scratch = [per_peer, per_peer, pltpu.SemaphoreType.DMA((self.n,)), per_peer, per_peer]
        else:
            self.out_shape = [jax.ShapeDtypeStruct(a.shape, a.dtype) for a in self.arrays]
            self.scratch = [per_peer, per_peer, pltpu.SemaphoreType.DMA((self.n,))]

    def _copies(self, ins, outs, sems):
        send_sems, recv_sems, local_sems = sems[:3]
        x, y, c, chips = _place()
        me = 2 * x + y
        local, remote = [], []
        for a in range(self.n):
            if self.kind == "gather":
                local.append(pltpu.make_async_copy(ins[a], outs[a].at[me], local_sems.at[a]))
                half = self.arrays[a].shape[0] // 2

                def rows(ref, core, half=half):
                    return ref.at[pl.ds(core * half, half)]
            else:
                local.append(pltpu.make_async_copy(ins[a].at[me], outs[a].at[3], local_sems.at[a]))
            for j, (px, py) in enumerate(chips):
                k = 3 * a + j
                peer = 2 * px + py

                def copy(src, dst, to, send_sem=send_sems.at[k], recv_sem=recv_sems.at[k]):
                    return pltpu.make_async_remote_copy(src_ref=src, dst_ref=dst, send_sem=send_sem, recv_sem=recv_sem,
                                                        device_id=to, device_id_type=MESH)

                if self.kind == "gather":
                    sent = copy(rows(ins[a], c), rows(outs[a].at[me], c), (px, py, c))
                    landed = copy(rows(ins[a], c), rows(outs[a].at[peer], c), (px, py, c))
                    on = dict(send_sem=sems[3].at[k], recv_sem=sems[4].at[k])
                    passed = copy(rows(outs[a].at[peer], c), rows(outs[a].at[peer], c), (x, y, 1 - c), **on)
                    handed = copy(rows(outs[a].at[peer], c), rows(outs[a].at[peer], 1 - c), (x, y, 1 - c), **on)
                    remote.append((sent, landed, passed, handed))
                else:
                    sent = copy(ins[a].at[peer], outs[a].at[j], (px, py, c))
                    remote.append((sent, sent, None, None))
        return local, remote

    def start(self, ins, outs, sems):
        local, remote = self._copies(ins, outs, sems)
        for cp in local:
            cp.start()
        for sent, _, _, _ in remote:
            sent.start()

    def finish(self, ins, outs, sems):
        local, remote = self._copies(ins, outs, sems)
        for _, landed, passed, _ in remote:
            landed.wait_recv()
            if passed is not None:
                passed.start()
        for sent, _, passed, handed in remote:
            if passed is not None:
                handed.wait_recv()
                passed.wait_send()
            sent.wait_send()
        for cp in local:
            cp.wait()


def _call(body, *, name, grid, in_specs, out_specs, out_shape, args, scratch_shapes=(), aliases=None, exchange=None):
    if exchange is None:
        return pl.pallas_call(
            body, name=name, grid=grid, in_specs=in_specs, out_specs=out_specs, out_shape=out_shape,
            scratch_shapes=list(scratch_shapes), input_output_aliases=aliases or {},
            compiler_params=_params(("arbitrary",) * len(grid)))(*args)
    n_in, n_out, n_scr, n_ex = len(in_specs), len(out_specs), len(scratch_shapes), exchange.n
    steps = grid

    def wrapped(*refs):
        ins, refs = refs[:n_in], refs[n_in:]
        ex_in, refs = refs[:n_ex], refs[n_ex:]
        outs, refs = refs[:n_out], refs[n_out:]
        ex_out, refs = refs[:n_ex], refs[n_ex:]
        scr, sems = refs[:n_scr], refs[n_scr:]
        first = functools.reduce(jnp.logical_and, [pl.program_id(k) == 0 for k in range(len(steps))])
        last = functools.reduce(jnp.logical_and, [pl.program_id(k) == steps[k] - 1 for k in range(len(steps))])

        @pl.when(first)
        def _():
            exchange.start(ex_in, ex_out, sems)

        body(*ins, *outs, *scr)

        @pl.when(last)
        def _():
            exchange.finish(ex_in, ex_out, sems)

    return pl.pallas_call(
        wrapped, name=name, grid=grid,
        in_specs=list(in_specs) + [ANY] * n_ex, out_specs=list(out_specs) + [ANY] * n_ex,
        out_shape=list(out_shape) + exchange.out_shape,
        scratch_shapes=list(scratch_shapes) + exchange.scratch, input_output_aliases=aliases or {},
        compiler_params=_params(("arbitrary",) * len(grid)))(*args, *exchange.arrays)


def exchange_alone(exchange, name):
    def body(*refs):
        n = exchange.n
        exchange.start(refs[:n], refs[n:2 * n], refs[2 * n:])
        exchange.finish(refs[:n], refs[n:2 * n], refs[2 * n:])

    return pl.pallas_call(
        body, name=name, in_specs=[ANY] * exchange.n, out_specs=[ANY] * exchange.n,
        out_shape=exchange.out_shape, scratch_shapes=exchange.scratch)(*exchange.arrays)


def norm_matmul(x, g, wg, name, exchange=None):
    t, d = x.shape
    nl = wg.shape[2]

    def body(x_ref, g_ref, w_ref, o_ref, h_ref):
        h = _rms(x_ref[...], g_ref[...]).astype(BF16)
        h_ref[...] = h
        for c in range(N_CHIPS):
            o_ref[:, c * nl:(c + 1) * nl] = _dot(h, w_ref[c])

    return _call(
        body, name=name, grid=(t // ROW_TILE,),
        in_specs=[pl.BlockSpec((ROW_TILE, d), lambda i: (i, 0)),
                  pl.BlockSpec((1, d), lambda i: (0, 0)),
                  pl.BlockSpec((N_CHIPS, d, nl), lambda i: (0, 0, 0))],
        out_specs=[pl.BlockSpec((ROW_TILE, N_CHIPS * nl), lambda i: (i, 0)),
                   pl.BlockSpec((ROW_TILE, d), lambda i: (i, 0))],
        out_shape=[jax.ShapeDtypeStruct((t, N_CHIPS * nl), F32), jax.ShapeDtypeStruct((t, d), BF16)],
        args=(x, g, wg), exchange=exchange)


def norm_matmul_bwd(dproj, wg, x, g, dres, name, exchange=None):
    t, d = x.shape
    nl = wg.shape[2]
    stacked = dproj.ndim == 3
    piece = math.gcd(nl, dproj.shape[-1])

    def body(dp_ref, w_ref, x_ref, g_ref, dres_ref, dx_ref, dg_ref):
        dh = None
        for j in range(N_CHIPS * nl // piece):
            c, off = divmod(j * piece, nl)
            if stacked:
                p, lo = divmod(j * piece, dproj.shape[-1])
                lhs = dp_ref[p, :, lo:lo + piece]
            else:
                lhs = dp_ref[:, j * piece:(j + 1) * piece]
            part = _dot_nt(lhs.astype(BF16), w_ref[c, :, off:off + piece])
            dh = part if dh is None else dh + part
        dx, dg = _rms_bwd(x_ref[...], g_ref[...], dh)
        dx_ref[...] = dres_ref[...] + dx
        _accumulate(dg_ref, dg, pl.program_id(0) == 0)

    row = pl.BlockSpec((ROW_TILE, d), lambda i: (i, 0))
    vec = pl.BlockSpec((1, d), lambda i: (0, 0))
    if stacked:
        dp_spec = pl.BlockSpec((dproj.shape[0], ROW_TILE, dproj.shape[-1]), lambda i: (0, i, 0))
    else:
        dp_spec = pl.BlockSpec((ROW_TILE, N_CHIPS * nl), lambda i: (i, 0))
    return _call(
        body, name=name, grid=(t // ROW_TILE,),
        in_specs=[dp_spec, pl.BlockSpec((N_CHIPS, d, nl), lambda i: (0, 0, 0)), row, vec, row],
        out_specs=[row, vec],
        out_shape=[jax.ShapeDtypeStruct((t, d), F32), jax.ShapeDtypeStruct((1, d), F32)],
        args=(dproj, wg, x, g, dres), exchange=exchange)


def out_proj(a, wg, x, g, name):
    t, d = x.shape
    kl = wg.shape[1]

    def body(a_ref, w_ref, x_ref, g_ref, mix_ref, xo_ref):
        acc = _dot(a_ref[:, 0:kl], w_ref[0])
        for c in range(1, N_CHIPS):
            acc += _dot(a_ref[:, c * kl:(c + 1) * kl], w_ref[c])
        mix_ref[...] = acc
        xo_ref[...] = x_ref[...] + _rms(acc, g_ref[...])

    row = pl.BlockSpec((ROW_TILE, d), lambda i: (i, 0))
    return pl.pallas_call(
        body, name=name, grid=(t // ROW_TILE,),
        in_specs=[row, pl.BlockSpec((N_CHIPS, kl, d), lambda i: (0, 0, 0)), row,
                  pl.BlockSpec((1, d), lambda i: (0, 0))],
        out_specs=[row, row],
        out_shape=[jax.ShapeDtypeStruct((t, d), F32), jax.ShapeDtypeStruct((t, d), F32)],
        compiler_params=_params(("arbitrary",)),
    )(a, wg, x, g)


def out_proj_bwd(dxo, mix, g, wg, name):
    t, d = mix.shape
    kl = wg.shape[1]

    def body(dxo_ref, mix_ref, g_ref, w_ref, dmix_ref, da_ref, dg_ref):
        dmix, dg = _rms_bwd(mix_ref[...], g_ref[...], dxo_ref[...])
        dmb = dmix.astype(BF16)
        dmix_ref[...] = dmb
        for c in range(N_CHIPS):
            da_ref[:, c * kl:(c + 1) * kl] = _dot_nt(dmb, w_ref[c])
        _accumulate(dg_ref, dg, pl.program_id(0) == 0)

    row = pl.BlockSpec((ROW_TILE, d), lambda i: (i, 0))
    vec = pl.BlockSpec((1, d), lambda i: (0, 0))
    return pl.pallas_call(
        body, name=name, grid=(t // ROW_TILE,),
        in_specs=[row, row, vec, pl.BlockSpec((N_CHIPS, kl, d), lambda i: (0, 0, 0))],
        out_specs=[row, row, vec],
        out_shape=[jax.ShapeDtypeStruct((t, d), BF16), jax.ShapeDtypeStruct((t, d), F32),
                   jax.ShapeDtypeStruct((1, d), F32)],
        compiler_params=_params(("arbitrary",)),
    )(dxo, mix, g, wg)


def ffn_fwd(x, gpre, w1g, w2g, gpost, name, exchange=None, target=None):
    t, d = x.shape
    hc = w1g.shape[2]
    with_loss = target is not None

    def body(x_ref, gpre_ref, w1_ref, w2_ref, gpost_ref, *rest):
        if with_loss:
            t_ref, xo_ref, h_ref, a_ref, y_ref, l_ref, acc = rest
        else:
            xo_ref, h_ref, a_ref, y_ref, acc = rest
        i, c = pl.program_id(0), pl.program_id(1)

        @pl.when(c == 0)
        def _():
            h_ref[...] = _rms(x_ref[...], gpre_ref[...]).astype(BF16)

        a = _dot(h_ref[...], w1_ref[...])
        a_ref[...] = a.astype(BF16)
        r = jnp.square(jnp.maximum(a, 0.0)).astype(BF16)
        _accumulate(acc, _dot(r, w2_ref[...]), c == 0)

        @pl.when(c == N_CHIPS - 1)
        def _():
            y = acc[...]
            y_ref[...] = y
            xo = x_ref[...] + _rms(y, gpost_ref[...])
            if with_loss:
                e = xo - t_ref[...]
                xo_ref[...] = e * (1.0 / d)
                part = jnp.sum(jnp.sum(e * e, axis=-1, keepdims=True), axis=0, keepdims=True) * (0.5 / d)
                _accumulate(l_ref, part, i == 0)
            else:
                xo_ref[...] = xo

    row = pl.BlockSpec((FFN_ROWS, d), lambda i, c: (i, 0))
    vec = pl.BlockSpec((1, d), lambda i, c: (0, 0))
    one = pl.BlockSpec((1, 1), lambda i, c: (0, 0))
    return _call(
        body, name=name, grid=(t // FFN_ROWS, N_CHIPS),
        in_specs=[row, vec,
                  pl.BlockSpec((None, d, hc), lambda i, c: (c, 0, 0)),
                  pl.BlockSpec((None, hc, d), lambda i, c: (c, 0, 0)), vec] + ([row] if with_loss else []),
        out_specs=[row, row, pl.BlockSpec((FFN_ROWS, hc), lambda i, c: (i, c)), row] + ([one] if with_loss else []),
        out_shape=[jax.ShapeDtypeStruct((t, d), F32), jax.ShapeDtypeStruct((t, d), BF16),
                   jax.ShapeDtypeStruct((t, N_CHIPS * hc), BF16), jax.ShapeDtypeStruct((t, d), F32)]
        + ([jax.ShapeDtypeStruct((1, 1), F32)] if with_loss else []),
        scratch_shapes=[pltpu.VMEM((FFN_ROWS, d), F32)],
        args=(x, gpre, w1g, w2g, gpost) + ((target,) if with_loss else ()), exchange=exchange)


def ffn_bwd(dxo, x, y, a, gpre, gpost, w1g, w2g, name, exchange=None):
    t, d = x.shape
    hc = w1g.shape[2]

    def body(dxo_ref, x_ref, y_ref, a_ref, gpre_ref, gpost_ref, w1_ref, w2_ref,
             dxi_ref, dy_ref, da_ref, dgpre_ref, dgpost_ref, acc):
        i, c = pl.program_id(0), pl.program_id(1)

        @pl.when(c == 0)
        def _():
            dy, dg = _rms_bwd(y_ref[...], gpost_ref[...], dxo_ref[...])
            dy_ref[...] = dy.astype(BF16)
            _accumulate(dgpost_ref, dg, i == 0)

        dr = _dot_nt(dy_ref[...], w2_ref[...])
        da = (dr * (2.0 * jnp.maximum(a_ref[...].astype(F32), 0.0))).astype(BF16)
        da_ref[...] = da
        _accumulate(acc, _dot_nt(da, w1_ref[...]), c == 0)

        @pl.when(c == N_CHIPS - 1)
        def _():
            dx, dg = _rms_bwd(x_ref[...], gpre_ref[...], acc[...])
            dxi_ref[...] = dxo_ref[...] + dx
            _accumulate(dgpre_ref, dg, i == 0)

    row = pl.BlockSpec((ROW_TILE, d), lambda i, c: (i, 0))
    vec = pl.BlockSpec((1, d), lambda i, c: (0, 0))
    hid = pl.BlockSpec((ROW_TILE, hc), lambda i, c: (i, c))
    return _call(
        body, name=name, grid=(t // ROW_TILE, N_CHIPS),
        in_specs=[row, row, row, hid, vec, vec,
                  pl.BlockSpec((None, d, hc), lambda i, c: (c, 0, 0)),
                  pl.BlockSpec((None, hc, d), lambda i, c: (c, 0, 0))],
        out_specs=[row, row, hid, vec, vec],
        out_shape=[jax.ShapeDtypeStruct((t, d), F32), jax.ShapeDtypeStruct((t, d), BF16),
                   jax.ShapeDtypeStruct((t, N_CHIPS * hc), BF16),
                   jax.ShapeDtypeStruct((1, d), F32), jax.ShapeDtypeStruct((1, d), F32)],
        scratch_shapes=[pltpu.VMEM((ROW_TILE, d), F32)],
        args=(dxo, x, y, a, gpre, gpost, w1g, w2g), exchange=exchange)


def weight_grad(a, b, chunked, bk, bn, relu2, name, exchange=None):
    t = a.shape[0]
    a_on = chunked == "a"
    rows = min(t, WGRAD_ROWS)
    n_steps = t // rows

    def body(a_ref, b_ref, o_ref, acc):
        s = pl.program_id(1)
        av = a_ref[...]
        if relu2:
            av = jnp.square(jnp.maximum(av.astype(F32), 0.0))
        _accumulate(acc, _dot_tn(av.astype(BF16), b_ref[...].astype(BF16)), s == 0)

        @pl.when(s == n_steps - 1)
        def _():
            o_ref[...] = acc[...].astype(BF16)

    res = _call(
        body, name=name, grid=(N_CHIPS, n_steps),
        in_specs=[pl.BlockSpec((rows, bk), (lambda c, s: (s, c)) if a_on else (lambda c, s: (s, 0))),
                  pl.BlockSpec((rows, bn), (lambda c, s: (s, 0)) if a_on else (lambda c, s: (s, c)))],
        out_specs=[pl.BlockSpec((None, bk, bn), lambda c, s: (c, 0, 0))],
        out_shape=[jax.ShapeDtypeStruct((N_CHIPS, bk, bn), BF16)],
        scratch_shapes=[pltpu.VMEM((bk, bn), F32)],
        args=(a, b), exchange=exchange)
    return res[0] if exchange is None else res


def weight_grad_stacked(a, b3, bn, name):
    t, bk = a.shape
    width = b3.shape[-1]
    piece = math.gcd(bn, width)
    rows = min(t, WGRAD_ROWS)
    n_steps = t // rows

    def body(a_ref, b_ref, o_hbm, acc, staged, sem):
        s, c = pl.program_id(0), pl.program_id(1)
        av = a_ref[...].astype(BF16)
        for chunk in range(N_CHIPS):
            @pl.when(c == chunk)
            def _(chunk=chunk):
                for k in range(bn // piece):
                    p, lo = divmod(chunk * bn + k * piece, width)
                    part = _dot_tn(av, b_ref[p, :, lo:lo + piece].astype(BF16))
                    _accumulate(acc.at[chunk, :, k * piece:(k + 1) * piece], part, s == 0)

                @pl.when(s == n_steps - 1)
                def _():
                    staged[...] = acc[chunk].astype(BF16)
                    copy = pltpu.make_async_copy(staged, o_hbm.at[chunk], sem)
                    copy.start()
                    copy.wait()

    return pl.pallas_call(
        body, name=name, grid=(n_steps, N_CHIPS),
        in_specs=[pl.BlockSpec((rows, bk), lambda s, c: (s, 0)),
                  pl.BlockSpec((b3.shape[0], rows, width), lambda s, c: (0, s, 0))],
        out_specs=ANY,
        out_shape=jax.ShapeDtypeStruct((N_CHIPS, bk, bn), BF16),
        scratch_shapes=[pltpu.VMEM((N_CHIPS, bk, bn), F32), pltpu.VMEM((bk, bn), BF16), pltpu.SemaphoreType.DMA],
        compiler_params=_params(("arbitrary", "arbitrary")),
    )(a, b3)


def _hgrn2_chunk(st, qs, fls, ivs, gls, l0, l1, l2, ng):
    nsub = len(qs)
    mx = jnp.maximum(jnp.maximum(l0, l1), l2)
    e0, e1, e2 = jnp.exp(l0 - mx), jnp.exp(l1 - mx), jnp.exp(l2 - mx)
    lb = e0 / (e0 + e1 + e2)
    rows = lax.broadcasted_iota(jnp.int32, (A_SUB, A_SUB), 0)
    cols = lax.broadcasted_iota(jnp.int32, (A_SUB, A_SUB), 1)
    tri = (rows >= cols).astype(F32)
    keep = (lax.broadcasted_iota(jnp.int32, (A_SUB, A_SUB, A_DK), 0)
            >= lax.broadcasted_iota(jnp.int32, (A_SUB, A_SUB, A_DK), 1))
    base = jnp.zeros_like(l0)
    bases, gs, ks, qfs = [], [], [], []
    for i in range(nsub):
        f = lb + (1.0 - lb) * jax.nn.sigmoid(fls[i])
        logf = jnp.log(f)
        bases.append(base)
        gs.append(base + jnp.dot(tri, logf, precision=lax.Precision.HIGHEST, preferred_element_type=F32))
        base = base + jnp.sum(logf, axis=0, keepdims=True)
        ks.append(1.0 - f)
        qfs.append(jax.nn.silu(qs[i]))
    g_last = base
    stb = st.astype(BF16)
    outs = []
    for i in range(nsub):
        o = _dot_nt((qfs[i] * jnp.exp(gs[i])).astype(BF16), stb)
        if i > 0:
            qt = (qfs[i] * jnp.exp(gs[i] - bases[i])).astype(BF16)
            kk = jnp.concatenate([ks[j] * jnp.exp(bases[i] - gs[j]) for j in range(i)], axis=0).astype(BF16)
            vv = jnp.concatenate(ivs[:i], axis=0).astype(BF16)
            o = o + _dot(_dot_nt(qt, kk).astype(BF16), vv)
        dec = jnp.exp(jnp.where(keep, gs[i][:, None, :] - gs[i][None, :, :], NEG_BIG))
        s_diag = jnp.sum(qfs[i][:, None, :] * ks[i][None, :, :] * dec, axis=-1)
        o = o + _dot(s_diag.astype(BF16), ivs[i].astype(BF16))
        o = o * lax.rsqrt(jnp.mean(o * o, axis=-1, keepdims=True) + EPS) * ng
        outs.append(o * jax.nn.silu(gls[i]))
    kdec = jnp.concatenate([ks[j] * jnp.exp(g_last - gs[j]) for j in range(nsub)], axis=0).astype(BF16)
    vall = jnp.concatenate(ivs, axis=0).astype(BF16)
    new_st = st * jnp.exp(g_last) + _dot_tn(vall, kdec)
    return new_st, outs


A_MAX_LOG_DECAY = 60.0


def _half_sums(logf):
    n = logf.shape[0]
    first = lax.broadcasted_iota(jnp.int32, logf.shape, 0) < n // 2
    return (jnp.sum(jnp.where(first, logf, 0.0), axis=0, keepdims=True),
            jnp.sum(jnp.where(first, 0.0, logf), axis=0, keepdims=True))


def _split3(x):
    hi = x.astype(BF16)
    r1 = x - hi.astype(F32)
    mid = r1.astype(BF16)
    return hi, mid, (r1 - mid.astype(F32)).astype(BF16)


def _tri_matmul(x, transpose):
    n = x.shape[0]
    r = lax.broadcasted_iota(jnp.int32, (n, n), 0)
    c = lax.broadcasted_iota(jnp.int32, (n, n), 1)
    tri = ((r <= c) if transpose else (r >= c)).astype(BF16)
    hi, mid, lo = _split3(x)
    return (_dot(tri, lo) + _dot(tri, mid)) + _dot(tri, hi)


@jax.custom_vjp
def _cumsum_rows(x):
    return _tri_matmul(x, False)


def _cumsum_rows_fwd(x):
    return _tri_matmul(x, False), None


def _cumsum_rows_bwd(_, dy):
    return (_tri_matmul(dy, True),)


_cumsum_rows.defvjp(_cumsum_rows_fwd, _cumsum_rows_bwd)


def _lower_bound(l0, l1, l2):
    mx = jnp.maximum(jnp.maximum(l0, l1), l2)
    e0, e1, e2 = jnp.exp(l0 - mx), jnp.exp(l1 - mx), jnp.exp(l2 - mx)
    return e0 / (e0 + e1 + e2)


def _b(x):
    return x.astype(BF16)


@jax.custom_vjp
def _mm(a, b):
    return _dot(_b(a), _b(b))


_mm.defvjp(lambda a, b: (_mm(a, b), (a, b)),
           lambda res, d: (_dot_nt(_b(d), _b(res[1])), _dot_tn(_b(res[0]), _b(d))))


@jax.custom_vjp
def _mm_nt(a, b):
    return _dot_nt(_b(a), _b(b))


_mm_nt.defvjp(lambda a, b: (_mm_nt(a, b), (a, b)),
              lambda res, d: (_dot(_b(d), _b(res[1])), _dot_tn(_b(d), _b(res[0]))))


def _dot_split(dot, a, b):
    ah, bh = _b(a), _b(b)
    al, bl = _b(a - ah.astype(F32)), _b(b - bh.astype(F32))
    return (dot(ah, bl) + dot(al, bh)) + dot(ah, bh)


@jax.custom_vjp
def _mm_scores(a, b):
    return _dot_nt(_b(a), _b(b))


_mm_scores.defvjp(lambda a, b: (_mm_scores(a, b), (a, b)),
                  lambda res, d: (_dot_split(_dot, d, res[1]), _dot_split(_dot_tn, d, res[0])))


@jax.custom_vjp
def _mm_tn(a, b):
    return _dot_tn(_b(a), _b(b))


_mm_tn.defvjp(lambda a, b: (_mm_tn(a, b), (a, b)),
              lambda res, d: (_dot_nt(_b(res[1]), _b(d)), _dot(_b(res[0]), _b(d))))


@jax.custom_vjp
def _split_heads(x):
    return tuple(x[:, h * A_DK:(h + 1) * A_DK] for h in range(A_HEADS))


def _split_heads_fwd(x):
    return _split_heads(x), None


def _split_heads_bwd(_, parts):
    return (jnp.concatenate(parts, axis=1),)


_split_heads.defvjp(_split_heads_fwd, _split_heads_bwd)


def _hgrn2_chunk_fast(sts, q, fl, iv, gl, l0, l1, l2, ng):
    lb = _lower_bound(l0, l1, l2)
    f = lb + (1.0 - lb) * jax.nn.sigmoid(fl)
    return _hgrn2_fast_core(sts, q, f, jnp.log(f), iv, gl, ng)


def _hgrn2_fast_core(sts, q, f, logf, iv, gl, ng):
    g = _cumsum_rows(logf)
    g_mid, g_last = _half_sums(logf)
    g_last = g_mid + g_last
    k = 1.0 - f
    qf = jax.nn.silu(q)
    qms = _split_heads(qf * jnp.exp(g - g_mid))
    kms = _split_heads(k * jnp.exp(g_mid - g))
    qgs = _split_heads(qf * jnp.exp(g))
    kds = _split_heads(k * jnp.exp(g_last - g))
    ivs = _split_heads(iv)
    decays = _split_heads(jnp.exp(g_last))
    n = q.shape[0]
    causal = lax.broadcasted_iota(jnp.int32, (n, n), 0) >= lax.broadcasted_iota(jnp.int32, (n, n), 1)
    raw = [_mm_scores(qm, km) for qm, km in zip(qms, kms)]
    inter = [_mm_nt(qg, st) for qg, st in zip(qgs, sts)]
    scores = [jnp.where(causal, s, 0.0) for s in raw]
    os = [a + _mm(s, v) for a, s, v in zip(inter, scores, ivs)]
    new_sts = [st * d + _mm_tn(v, kd) for st, d, v, kd in zip(sts, decays, ivs, kds)]
    os = [o * lax.rsqrt(jnp.mean(o * o, axis=-1, keepdims=True) + EPS) for o in os]
    return new_sts, jnp.concatenate(os, axis=1) * ng * jax.nn.silu(gl)


A_STEP_CHUNKS = 4


def _chunk_rows(j):
    return pl.ds(pl.multiple_of(j * A_CHUNK, A_CHUNK), A_CHUNK)


def _sub_rows(j, i):
    return pl.ds(pl.multiple_of(j * A_CHUNK + i * A_SUB, A_SUB), A_SUB)


def _sub_blocks(ref, head, j):
    lanes = slice(head * A_DK, (head + 1) * A_DK)
    return [ref[_sub_rows(j, i), lanes] for i in range(A_CHUNK // A_SUB)]


def hgrn2_fwd(proj, lb_table, a_norm, batch, name, exchange=None):
    t = proj.shape[0]
    n_steps = t // batch // (A_CHUNK * A_STEP_CHUNKS)
    rows = A_CHUNK * A_STEP_CHUNKS

    def body(q_ref, f_ref, i_ref, g_ref, lb_ref, ng_ref, o_ref, st_ref, dec_ref, st):
        @pl.when(pl.program_id(1) == 0)
        def _():
            st[...] = jnp.zeros_like(st)

        def chunk(j, carry):
            r = _chunk_rows(j)
            st_ref[j] = st[...]
            lb = _lower_bound(lb_ref[0:1, :], lb_ref[1:2, :], lb_ref[2:3, :])
            f = lb + (1.0 - lb) * jax.nn.sigmoid(f_ref[r, :])
            logf = jnp.log(f)
            decay = jnp.minimum(*_half_sums(logf))
            dec_ref[j] = decay
            mild = jnp.min(decay) >= -A_MAX_LOG_DECAY

            @pl.when(mild)
            def _():
                new_sts, o = _hgrn2_fast_core([st[h] for h in range(A_HEADS)], q_ref[r, :], f, logf,
                                              i_ref[r, :], g_ref[r, :], ng_ref[...])
                for h in range(A_HEADS):
                    st[h] = new_sts[h]
                o_ref[r, :] = o.astype(BF16)

            @pl.when(jnp.logical_not(mild))
            def _():
                for h in range(A_HEADS):
                    lanes = slice(h * A_DK, (h + 1) * A_DK)
                    new_st, outs = _hgrn2_chunk(
                        st[h], _sub_blocks(q_ref, h, j), _sub_blocks(f_ref, h, j), _sub_blocks(i_ref, h, j),
                        _sub_blocks(g_ref, h, j), lb_ref[0:1, lanes], lb_ref[1:2, lanes], lb_ref[2:3, lanes],
                        ng_ref[:, lanes])
                    st[h] = new_st
                    for i, o in enumerate(outs):
                        o_ref[_sub_rows(j, i), lanes] = o.astype(BF16)

            return carry

        lax.fori_loop(0, A_STEP_CHUNKS, chunk, 0)

    def part(k):
        return pl.BlockSpec((rows, A_WIDTH), lambda b, n: (b * n_steps + n, k))

    return _call(
        body, name=name, grid=(batch, n_steps),
        in_specs=[part(0), part(1), part(2), part(3),
                  pl.BlockSpec((3, A_WIDTH), lambda b, n: (0, 0)), pl.BlockSpec((1, A_WIDTH), lambda b, n: (0, 0))],
        out_specs=[part(0),
                   pl.BlockSpec((A_STEP_CHUNKS, A_HEADS, A_DK, A_DK), lambda b, n: (b * n_steps + n, 0, 0, 0)),
                   pl.BlockSpec((A_STEP_CHUNKS, 1, A_WIDTH), lambda b, n: (b * n_steps + n, 0, 0))],
        out_shape=[jax.ShapeDtypeStruct((t, A_WIDTH), BF16),
                   jax.ShapeDtypeStruct((t // A_CHUNK, A_HEADS, A_DK, A_DK), F32),
                   jax.ShapeDtypeStruct((t // A_CHUNK, 1, A_WIDTH), F32)],
        scratch_shapes=[pltpu.VMEM((A_HEADS, A_DK, A_DK), F32)],
        args=(proj, proj, proj, proj, lb_table, a_norm), exchange=exchange)


def hgrn2_bwd(proj, states, decays, lb_table, a_norm, do, batch, name, exchange=None):
    t = proj.shape[0]
    n_steps = t // batch // (A_CHUNK * A_STEP_CHUNKS)
    rows = A_CHUNK * A_STEP_CHUNKS

    def body(q_ref, f_ref, i_ref, g_ref, st_ref, dec_ref, lb_ref, ng_ref, do_ref, dp_ref, dlb_ref, dng_ref, dst):
        @pl.when(jnp.logical_and(pl.program_id(0) == 0, pl.program_id(1) == 0))
        def _():
            dlb_ref[...] = jnp.zeros_like(dlb_ref)
            dng_ref[...] = jnp.zeros_like(dng_ref)

        @pl.when(pl.program_id(1) == 0)
        def _():
            dst[...] = jnp.zeros_like(dst)

        def chunk(jj, carry):
            j = A_STEP_CHUNKS - 1 - jj
            r = _chunk_rows(j)
            mild = jnp.min(dec_ref[j]) >= -A_MAX_LOG_DECAY

            @pl.when(mild)
            def _():
                _, vjp = jax.vjp(
                    _hgrn2_chunk_fast, [st_ref[j, h] for h in range(A_HEADS)], q_ref[r, :], f_ref[r, :],
                    i_ref[r, :], g_ref[r, :], lb_ref[0:1, :], lb_ref[1:2, :], lb_ref[2:3, :], ng_ref[...])
                d_sts, dq, df, di, dg, dl0, dl1, dl2, dng = vjp(
                    ([dst[h] for h in range(A_HEADS)], do_ref[r, :].astype(F32)))
                for h in range(A_HEADS):
                    dst[h] = d_sts[h]
                for k, part in enumerate((dq, df, di, dg)):
                    dp_ref[r, k * A_WIDTH:(k + 1) * A_WIDTH] = part
                for row, val in enumerate((dl0, dl1, dl2)):
                    dlb_ref[row:row + 1, :] += val
                dng_ref[...] += dng

            @pl.when(jnp.logical_not(mild))
            def _():
                for h in range(A_HEADS):
                    lanes = slice(h * A_DK, (h + 1) * A_DK)
                    _, vjp = jax.vjp(
                        _hgrn2_chunk, st_ref[j, h], _sub_blocks(q_ref, h, j), _sub_blocks(f_ref, h, j),
                        _sub_blocks(i_ref, h, j), _sub_blocks(g_ref, h, j), lb_ref[0:1, lanes], lb_ref[1:2, lanes],
                        lb_ref[2:3, lanes], ng_ref[:, lanes])
                    douts = [x.astype(F32) for x in _sub_blocks(do_ref, h, j)]
                    d_st, dqs, dfs, dis, dgs, dl0, dl1, dl2, dng = vjp((dst[h], douts))
                    dst[h] = d_st
                    for k, parts in enumerate((dqs, dfs, dis, dgs)):
                        for i in range(A_CHUNK // A_SUB):
                            dp_ref[_sub_rows(j, i), k * A_WIDTH + h * A_DK:k * A_WIDTH + (h + 1) * A_DK] = parts[i]
                    for row, val in enumerate((dl0, dl1, dl2)):
                        dlb_ref[row:row + 1, lanes] += val
                    dng_ref[:, lanes] += dng

            return carry

        lax.fori_loop(0, A_STEP_CHUNKS, chunk, 0)

    def rev(b, n):
        return b * n_steps + (n_steps - 1 - n)

    def part(k):
        return pl.BlockSpec((rows, A_WIDTH), lambda b, n: (rev(b, n), k))

    const3 = pl.BlockSpec((3, A_WIDTH), lambda b, n: (0, 0))
    const1 = pl.BlockSpec((1, A_WIDTH), lambda b, n: (0, 0))
    return _call(
        body, name=name, grid=(batch, n_steps),
        in_specs=[part(0), part(1), part(2), part(3),
                  pl.BlockSpec((A_STEP_CHUNKS, A_HEADS, A_DK, A_DK), lambda b, n: (rev(b, n), 0, 0, 0)),
                  pl.BlockSpec((A_STEP_CHUNKS, 1, A_WIDTH), lambda b, n: (rev(b, n), 0, 0)),
                  const3, const1, part(0)],
        out_specs=[pl.BlockSpec((rows, 4 * A_WIDTH), lambda b, n: (rev(b, n), 0)), const3, const1],
        out_shape=[jax.ShapeDtypeStruct((t, 4 * A_WIDTH + 2 * B_WIDTH), F32),
                   jax.ShapeDtypeStruct((3, A_WIDTH), F32), jax.ShapeDtypeStruct((1, A_WIDTH), F32)],
        scratch_shapes=[pltpu.VMEM((A_HEADS, A_DK, A_DK), F32)],
        args=(proj, proj, proj, proj, states, decays, lb_table, a_norm, do), exchange=exchange)


B_GDIM = B_WIDTH // B_GROUPS
B_ROWS = 512


def _gmlp_chunk(ubs, vbs, lngs, lnbs, ws, bcols):
    vs = [jax.nn.gelu(v) for v in vbs]
    mu = sum(jnp.sum(v, axis=-1, keepdims=True) for v in vs) * (1.0 / B_WIDTH)
    var = sum(jnp.sum(jnp.square(v - mu), axis=-1, keepdims=True) for v in vs) * (1.0 / B_WIDTH)
    rstd = lax.rsqrt(var + EPS)
    tril = (lax.broadcasted_iota(jnp.int32, (B_CHUNK, B_CHUNK), 0)
            >= lax.broadcasted_iota(jnp.int32, (B_CHUNK, B_CHUNK), 1))
    outs = []
    for g in range(B_GROUPS):
        vn = (vs[g] - mu) * rstd * lngs[g] + lnbs[g]
        w = jnp.where(tril, ws[g], 0.0).astype(BF16)
        outs.append(jax.nn.gelu(ubs[g]) * (_dot(w, vn.astype(BF16)) + bcols[g]))
    return outs


def _gmlp_args(u_ref, v_ref, lng_ref, lnb_ref, w_ref, bt_ref, rows):
    def groups(ref):
        return [ref[rows, g * B_GDIM:(g + 1) * B_GDIM] for g in range(B_GROUPS)]

    def vec(ref):
        return [ref[:, g * B_GDIM:(g + 1) * B_GDIM] for g in range(B_GROUPS)]

    return (groups(u_ref), groups(v_ref), vec(lng_ref), vec(lnb_ref),
            [w_ref[g] for g in range(B_GROUPS)], [bt_ref[:, g:g + 1] for g in range(B_GROUPS)])


def gmlp_fwd(proj, oa, ln_g, ln_b, w, bias_t, name, exchange=None):
    t = proj.shape[0]

    def body(u_ref, v_ref, oa_ref, lng_ref, lnb_ref, w_ref, bt_ref, o_ref):
        o_ref[:, 0:A_WIDTH] = oa_ref[...]
        for n in range(B_ROWS // B_CHUNK):
            rows = slice(n * B_CHUNK, (n + 1) * B_CHUNK)
            outs = _gmlp_chunk(*_gmlp_args(u_ref, v_ref, lng_ref, lnb_ref, w_ref, bt_ref, rows))
            for g, o in enumerate(outs):
                o_ref[rows, A_WIDTH + g * B_GDIM:A_WIDTH + (g + 1) * B_GDIM] = o.astype(BF16)

    vec = pl.BlockSpec((1, B_WIDTH), lambda i: (0, 0))
    return _call(
        body, name=name, grid=(t // B_ROWS,),
        in_specs=[pl.BlockSpec((B_ROWS, B_WIDTH), lambda i: (i, 4)), pl.BlockSpec((B_ROWS, B_WIDTH), lambda i: (i, 5)),
                  pl.BlockSpec((B_ROWS, A_WIDTH), lambda i: (i, 0)), vec, vec,
                  pl.BlockSpec((B_GROUPS, B_CHUNK, B_CHUNK), lambda i: (0, 0, 0)),
                  pl.BlockSpec((B_CHUNK, B_GROUPS), lambda i: (0, 0))],
        out_specs=[pl.BlockSpec((B_ROWS, A_WIDTH + B_WIDTH), lambda i: (i, 0))],
        out_shape=[jax.ShapeDtypeStruct((t, A_WIDTH + B_WIDTH), BF16)],
        args=(proj, proj, oa, ln_g, ln_b, w, bias_t), exchange=exchange)


def gmlp_bwd(proj, dmixin, ln_g, ln_b, w, bias_t, dproj, name, exchange=None):
    t = proj.shape[0]

    def body(u_ref, v_ref, do_ref, lng_ref, lnb_ref, w_ref, bt_ref, dp_in_ref,
             dp_ref, dlng_ref, dlnb_ref, dw_ref, dbt_ref):
        del dp_in_ref

        @pl.when(pl.program_id(0) == 0)
        def _():
            for ref in (dlng_ref, dlnb_ref, dw_ref, dbt_ref):
                ref[...] = jnp.zeros_like(ref)

        for n in range(B_ROWS // B_CHUNK):
            rows = slice(n * B_CHUNK, (n + 1) * B_CHUNK)
            _, vjp = jax.vjp(_gmlp_chunk, *_gmlp_args(u_ref, v_ref, lng_ref, lnb_ref, w_ref, bt_ref, rows))
            douts = [do_ref[rows, g * B_GDIM:(g + 1) * B_GDIM] for g in range(B_GROUPS)]
            dus, dvs, dlngs, dlnbs, dws, dbs = vjp(douts)
            for g in range(B_GROUPS):
                lanes = slice(g * B_GDIM, (g + 1) * B_GDIM)
                dp_ref[rows, lanes] = dus[g]
                dp_ref[rows, B_WIDTH + g * B_GDIM:B_WIDTH + (g + 1) * B_GDIM] = dvs[g]
                dlng_ref[:, lanes] += dlngs[g]
                dlnb_ref[:, lanes] += dlnbs[g]
                dw_ref[g] += dws[g]
                dbt_ref[:, g:g + 1] += dbs[g]

    vec = pl.BlockSpec((1, B_WIDTH), lambda i: (0, 0))
    wspec = pl.BlockSpec((B_GROUPS, B_CHUNK, B_CHUNK), lambda i: (0, 0, 0))
    bspec = pl.BlockSpec((B_CHUNK, B_GROUPS), lambda i: (0, 0))
    return _call(
        body, name=name, grid=(t // B_ROWS,),
        in_specs=[pl.BlockSpec((B_ROWS, B_WIDTH), lambda i: (i, 4)), pl.BlockSpec((B_ROWS, B_WIDTH), lambda i: (i, 5)),
                  pl.BlockSpec((B_ROWS, B_WIDTH), lambda i: (i, 1)), vec, vec, wspec, bspec,
                  pl.BlockSpec(memory_space=pl.ANY)],
        out_specs=[pl.BlockSpec((B_ROWS, 2 * B_WIDTH), lambda i: (i, 2)), vec, vec, wspec, bspec],
        out_shape=[jax.ShapeDtypeStruct(dproj.shape, F32), jax.ShapeDtypeStruct((1, B_WIDTH), F32),
                   jax.ShapeDtypeStruct((1, B_WIDTH), F32), jax.ShapeDtypeStruct((B_GROUPS, B_CHUNK, B_CHUNK), F32),
                   jax.ShapeDtypeStruct((B_CHUNK, B_GROUPS), F32)],
        aliases={7: 0}, args=(proj, proj, dmixin, ln_g, ln_b, w, bias_t, dproj), exchange=exchange)


C_FWD_BLOCKS = 8
C_BWD_BLOCKS = 4
C_PAIR = 2 * C_HEAD_DIM
C_PAIRS = C_HEADS // 2
C_SCALE = 1.0 / math.sqrt(C_HEAD_DIM)
C_ROT_DIM = 2 * C_ROT_HALF
ROPE_ROWS = 1024


def rope_tables(pos_col, name):
    t = pos_col.shape[0]

    def body(p_ref, c_ref, a_ref, b_ref):
        lane = jnp.bitwise_and(lax.broadcasted_iota(jnp.int32, (1, C_PAIR), 1), C_HEAD_DIM - 1)
        j = jnp.bitwise_and(lane, C_ROT_HALF - 1).astype(F32)
        inv = jnp.exp(j * (-math.log(ROPE_THETA) / C_ROT_HALF))
        ang = p_ref[...].astype(F32) * inv
        cos, sin = jnp.cos(ang), jnp.sin(ang)
        c_ref[...] = jnp.where(lane < C_ROT_DIM, cos, 1.0)
        a_ref[...] = jnp.where(lane < C_ROT_HALF, -sin, 0.0)
        b_ref[...] = jnp.where(jnp.logical_and(lane >= C_ROT_HALF, lane < C_ROT_DIM), sin, 0.0)

    tab = pl.BlockSpec((ROPE_ROWS, C_PAIR), lambda i: (i, 0))
    return pl.pallas_call(
        body, name=name, grid=(t // ROPE_ROWS,),
        in_specs=[pl.BlockSpec((ROPE_ROWS, 1), lambda i: (i, 0))],
        out_specs=[tab, tab, tab],
        out_shape=[jax.ShapeDtypeStruct((t, C_PAIR), F32)] * 3,
        compiler_params=_params(("arbitrary",)),
    )(pos_col)


def _rope(x, c, a, b):
    return x * c + pltpu.roll(x, C_PAIR - C_ROT_HALF, 1) * a + pltpu.roll(x, C_ROT_HALF, 1) * b


def _rope_t(d, c, a, b):
    return d * c + pltpu.roll(d * a, C_ROT_HALF, 1) + pltpu.roll(d * b, C_PAIR - C_ROT_HALF, 1)


C_RES = 16


def _residue_major(a, batch):
    return a.reshape(batch, SEQ // C_RES, C_RES, -1).transpose(0, 2, 1, 3).reshape(a.shape)


def _sequence_order(a, batch):
    return a.reshape(batch, C_RES, SEQ // C_RES, -1).transpose(0, 2, 1, 3).reshape(a.shape)


def _block_pieces(idx, dil):
    nblk = SEQ // dil // C_BLOCK
    r, n = idx // nblk, idx % nblk
    per = C_RES // dil
    size = C_BLOCK // per

    def pieces(blk):
        return [((dil * a + r) * (SEQ // C_RES) + size * blk, size) for a in range(per)]

    return pieces(n), pieces(jnp.maximum(n - 1, 0)), n > 0


def _get_rows(ref, pieces):
    return jnp.concatenate([ref[pl.ds(pl.multiple_of(start, 8), size), :] for start, size in pieces], axis=0)


def _set_rows(ref, pieces, val, add=False):
    for k, (start, size) in enumerate(pieces):
        rows = pl.ds(pl.multiple_of(start, 8), size)
        part = val[k * size:(k + 1) * size]
        ref[rows, :] = ref[rows, :] + part if add else part


def _head_masks():
    low = lax.broadcasted_iota(jnp.int32, (1, C_PAIR), 1) < C_HEAD_DIM
    return low, jnp.logical_not(low)


def _attn_mask(has_prev, dil):
    per = C_RES // dil
    size = C_BLOCK // per

    def position(x):
        x = jnp.bitwise_and(x, C_BLOCK - 1)
        return per * jnp.bitwise_and(x, size - 1) + x // size

    j = lax.broadcasted_iota(jnp.int32, (2 * C_BLOCK, 2 * C_BLOCK), 1)
    pi = position(lax.broadcasted_iota(jnp.int32, (2 * C_BLOCK, 2 * C_BLOCK), 0))
    pj = position(j)
    own = j < C_BLOCK
    return jnp.logical_or(jnp.logical_and(own, pj <= pi),
                          jnp.logical_and(jnp.logical_and(jnp.logical_not(own), pj >= pi), has_prev))


def _stack_heads(x):
    low, high = _head_masks()
    return jnp.concatenate([jnp.where(low, x, 0.0), jnp.where(high, x, 0.0)], axis=0)


def _unstack_heads(x):
    low, _ = _head_masks()
    return jnp.where(low, x[:C_BLOCK], x[C_BLOCK:])


def attn_fwd(qkv, cos_t, sin_a, sin_b, batch, name, exchange=None):
    t = qkv.shape[0]
    nbr = len(C_DILATIONS)

    def body(q_ref, k_ref, v_ref, c_ref, a_ref, b_ref, o_ref, l_ref, qs, ks, *stats):
        acc, mm, dd = stats[0:nbr], stats[nbr:2 * nbr], stats[2 * nbr:3 * nbr]
        c, a, b = c_ref[...], a_ref[...], b_ref[...]
        qs[...] = _rope(q_ref[...], c, a, b) * C_SCALE
        ks[...] = _rope(k_ref[...], c, a, b)

        def load(idx, dil):
            own, prev, has_prev = _block_pieces(idx, dil)
            return own, (has_prev, _get_rows(qs, own), _get_rows(ks, own), _get_rows(ks, prev),
                         _get_rows(v_ref, own), _get_rows(v_ref, prev))

        def scores(dil, has_prev, q, k_own, k_prev, v_own, v_prev):
            k_cat = jnp.concatenate([k_own, k_prev], axis=0).astype(BF16)
            return jnp.where(_attn_mask(has_prev, dil), _dot_nt(_stack_heads(q).astype(BF16), k_cat), NEG_BIG)

        def softmax(s):
            m = jnp.max(s, axis=-1, keepdims=True)
            p = jnp.exp(s - m)
            return p.astype(BF16), m, jnp.sum(p, axis=-1, keepdims=True)

        def values(pb, has_prev, q, k_own, k_prev, v_own, v_prev):
            low, high = _head_masks()
            v_cat = jnp.concatenate([v_own, v_prev], axis=0)
            p_wide = jnp.concatenate([pb[:C_BLOCK], pb[C_BLOCK:]], axis=1)
            v_tall = jnp.concatenate([jnp.where(low, v_cat, 0.0), jnp.where(high, v_cat, 0.0)], axis=0).astype(BF16)
            return _dot(p_wide, v_tall)

        for bi, dil in enumerate(C_DILATIONS):
            def pair(i, carry, bi=bi, dil=dil):
                low, _ = _head_masks()
                loaded = [load(C_FWD_BLOCKS * i + k, dil) for k in range(C_FWD_BLOCKS)]
                ss = [scores(dil, *ops) for _, ops in loaded]
                sm = [softmax(s) for s in ss]
                pvs = [values(pb, *ops) for (pb, _, _), (_, ops) in zip(sm, loaded)]
                for (own, _), (_, m, den), pv in zip(loaded, sm, pvs):
                    _set_rows(acc[bi], own, pv)
                    _set_rows(mm[bi], own, jnp.where(low, m[:C_BLOCK], m[C_BLOCK:]))
                    _set_rows(dd[bi], own, jnp.where(low, den[:C_BLOCK], den[C_BLOCK:]))
                return carry

            lax.fori_loop(0, SEQ // C_BLOCK // C_FWD_BLOCKS, pair, 0)
        step = 2 * C_BLOCK
        for r0 in range(0, SEQ, step):
            rr = slice(r0, r0 + step)
            ms = [mm[g][rr, :] for g in range(nbr)]
            m_all = functools.reduce(jnp.maximum, ms)
            ws = [jnp.exp(m - m_all) for m in ms]
            num = sum(acc[g][rr, :] * ws[g] for g in range(nbr))
            den = sum(dd[g][rr, :] * ws[g] for g in range(nbr))
            o_ref[rr, :] = (num / den).astype(BF16)
            l_ref[rr, :] = m_all + jnp.log(den)

    def col(k):
        return pl.BlockSpec((SEQ, C_PAIR), lambda b, p: (b, k * C_PAIRS + p))

    tab = pl.BlockSpec((SEQ, C_PAIR), lambda b, p: (b, 0))
    return _call(
        body, name=name, grid=(batch, C_PAIRS),
        in_specs=[col(0), col(1), col(2), tab, tab, tab],
        out_specs=[col(0), col(0)],
        out_shape=[jax.ShapeDtypeStruct((t, D_MODEL), BF16), jax.ShapeDtypeStruct((t, D_MODEL), F32)],
        scratch_shapes=[pltpu.VMEM((SEQ, C_PAIR), F32)] * (2 + 3 * nbr),
        args=(qkv, qkv, qkv, cos_t, sin_a, sin_b), exchange=exchange)


def attn_bwd(qkv, cos_t, sin_a, sin_b, o, lse, do, batch, name, exchange=None):
    t = qkv.shape[0]

    def body(q_ref, k_ref, v_ref, c_ref, a_ref, b_ref, o_ref, l_ref, do_ref, dqkv_ref, qs, ks, dqs, dks, dvs, dlt):
        low, _ = _head_masks()
        c, a, b = c_ref[...], a_ref[...], b_ref[...]
        qs[...] = _rope(q_ref[...], c, a, b) * C_SCALE
        ks[...] = _rope(k_ref[...], c, a, b)
        prod = do_ref[...] * o_ref[...].astype(F32)
        s_low = jnp.sum(jnp.where(low, prod, 0.0), axis=-1, keepdims=True)
        s_all = jnp.sum(prod, axis=-1, keepdims=True)
        dlt[...] = jnp.where(low, s_low, s_all - s_low)
        dqs[...] = jnp.zeros_like(dqs)
        dks[...] = jnp.zeros_like(dks)
        dvs[...] = jnp.zeros_like(dvs)

        def load(idx, dil):
            own, prev, has_prev = _block_pieces(idx, dil)
            return (own, prev), (has_prev, _get_rows(qs, own), _get_rows(do_ref, own), _get_rows(ks, own),
                                 _get_rows(ks, prev), _get_rows(v_ref, own), _get_rows(v_ref, prev),
                                 _get_rows(l_ref, own), _get_rows(dlt, own))

        def operands(dil, has_prev, q, do, k_own, k_prev, v_own, v_prev, l_full, d_full):
            lcol = jnp.concatenate([l_full[:, 0:1], l_full[:, C_HEAD_DIM:C_HEAD_DIM + 1]], axis=0)
            dcol = jnp.concatenate([d_full[:, 0:1], d_full[:, C_HEAD_DIM:C_HEAD_DIM + 1]], axis=0)
            return (_stack_heads(q).astype(BF16), _stack_heads(do).astype(BF16),
                    jnp.concatenate([k_own, k_prev], axis=0).astype(BF16),
                    jnp.concatenate([v_own, v_prev], axis=0).astype(BF16), lcol, dcol, _attn_mask(has_prev, dil))

        for dil in C_DILATIONS:
            def pair(i, carry, dil=dil):
                loaded = [load(C_BWD_BLOCKS * i + k, dil) for k in range(C_BWD_BLOCKS)]
                ops = [operands(dil, *o) for _, o in loaded]
                ss = [_dot_nt(q_stack, k_cat) for q_stack, _, k_cat, _, _, _, _ in ops]
                dps = [_dot_nt(do_stack, v_cat) for _, do_stack, _, v_cat, _, _, _ in ops]
                ps = [jnp.exp(jnp.where(o[6], s, NEG_BIG) - o[4]) for s, o in zip(ss, ops)]
                dss = [(p * (dp - o[5])).astype(BF16) for p, dp, o in zip(ps, dps, ops)]
                dvs_ = [_dot_tn(p.astype(BF16), o[1]) for p, o in zip(ps, ops)]
                dks_ = [_dot_tn(ds, o[0]) for ds, o in zip(dss, ops)]
                dqs_ = [_unstack_heads(_dot(ds, o[2])) for ds, o in zip(dss, ops)]
                for ((own, prev), _), dq, dk_cat, dv_cat in zip(loaded, dqs_, dks_, dvs_):
                    _set_rows(dqs, own, dq, add=True)
                    _set_rows(dks, own, dk_cat[:C_BLOCK], add=True)
                    _set_rows(dvs, own, dv_cat[:C_BLOCK], add=True)
                    _set_rows(dks, prev, dk_cat[C_BLOCK:], add=True)
                    _set_rows(dvs, prev, dv_cat[C_BLOCK:], add=True)
                return carry

            lax.fori_loop(0, SEQ // C_BLOCK // C_BWD_BLOCKS, pair, 0)
        dqkv_ref[0] = _rope_t(dqs[...] * C_SCALE, c, a, b).astype(BF16)
        dqkv_ref[1] = _rope_t(dks[...], c, a, b).astype(BF16)
        dqkv_ref[2] = dvs[...].astype(BF16)

    def col(k):
        return pl.BlockSpec((SEQ, C_PAIR), lambda b, p: (b, k * C_PAIRS + p))

    tab = pl.BlockSpec((SEQ, C_PAIR), lambda b, p: (b, 0))
    return _call(
        body, name=name, grid=(batch, C_PAIRS),
        in_specs=[col(0), col(1), col(2), tab, tab, tab, col(0), col(0), col(0)],
        out_specs=[pl.BlockSpec((3, SEQ, C_PAIR), lambda b, p: (0, b, p))],
        out_shape=[jax.ShapeDtypeStruct((3, t, D_MODEL), BF16)],
        scratch_shapes=[pltpu.VMEM((SEQ, C_PAIR), F32)] * 6,
        args=(qkv, qkv, qkv, cos_t, sin_a, sin_b, o, lse, do), exchange=exchange)


def sibling_swap(arrays, name):
    n = len(arrays)

    def body(*refs):
        ins, outs = refs[:n], refs[n:2 * n]
        send_sems, recv_sems = refs[2 * n:]
        x, y, c, _ = _place()
        sends = []
        for a in range(n):
            cp = pltpu.make_async_remote_copy(
                src_ref=ins[a], dst_ref=outs[a], send_sem=send_sems.at[a], recv_sem=recv_sems.at[a],
                device_id=(x, y, 1 - c), device_id_type=MESH)
            cp.start()
            sends.append(cp)
        for cp in sends:
            cp.wait_recv()
        for cp in sends:
            cp.wait_send()

    return pl.pallas_call(
        body, name=name,
        in_specs=[ANY] * n, out_specs=[ANY] * n,
        out_shape=[jax.ShapeDtypeStruct(s.shape, s.dtype) for s in arrays],
        scratch_shapes=[pltpu.SemaphoreType.DMA((n,)), pltpu.SemaphoreType.DMA((n,))],
    )(*arrays)


def allreduce_small(slab, name):
    rows, lanes = slab.shape

    def body(x_ref, out_ref, gath, send_sems, recv_sems, local_sem):
        x, y, c, chips = _place()
        me, sibling = (x, y, c), (x, y, 1 - c)

        def slot(px, py, pc):
            return gath.at[4 * px + 2 * py + pc]

        def copy(k, block, to, src=None):
            return pltpu.make_async_remote_copy(
                src_ref=slot(*block) if src is None else src, dst_ref=slot(*block),
                send_sem=send_sems.at[k], recv_sem=recv_sems.at[k], device_id=to, device_id_type=MESH)

        mine = pltpu.make_async_copy(x_ref, slot(*me), local_sem)
        mine.start()
        first = [copy(0, me, sibling, src=x_ref)]
        first += [copy(1 + j, me, (*chip, c), src=x_ref) for j, chip in enumerate(chips)]
        for cp in first:
            cp.start()
        passed = [copy(4 + j, (*chip, c), sibling) for j, chip in enumerate(chips)]
        for j, chip in enumerate(chips):
            copy(1 + j, (*chip, c), me).wait_recv()
            passed[j].start()
        copy(0, sibling, me).wait_recv()
        for j, chip in enumerate(chips):
            copy(4 + j, (*chip, 1 - c), me).wait_recv()
        for cp in first + passed:
            cp.wait_send()
        mine.wait()
        total = gath[0]
        for d in range(1, N_DEV):
            total = total + gath[d]
        out_ref[...] = total

    return pl.pallas_call(
        body, name=name,
        in_specs=[pl.BlockSpec(memory_space=pltpu.VMEM)],
        out_specs=pl.BlockSpec(memory_space=pltpu.VMEM),
        out_shape=jax.ShapeDtypeStruct((rows, lanes), F32),
        scratch_shapes=[pltpu.VMEM((N_DEV, rows, lanes), F32),
                        pltpu.SemaphoreType.DMA((7,)), pltpu.SemaphoreType.DMA((7,)), pltpu.SemaphoreType.DMA],
    )(slab)


ELT_ROWS = 512


def reduce_slabs(r, name, part=0, parts=1, into=None):
    _, rows, cols = r.shape
    br = min(rows, ELT_ROWS)
    nblk = rows // br

    def body(r_ref, *rest):
        o_ref = rest[-1]
        o_ref[...] = ((r_ref[3].astype(F32) + r_ref[0].astype(F32)) + r_ref[1].astype(F32)) + r_ref[2].astype(F32)

    return pl.pallas_call(
        body, name=name, grid=(nblk,),
        in_specs=[pl.BlockSpec((N_CHIPS, br, cols), lambda i: (0, i, 0))] + ([] if into is None else [ANY]),
        out_specs=pl.BlockSpec((br, cols), lambda i: (part * nblk + i, 0)),
        out_shape=jax.ShapeDtypeStruct((parts * rows, cols), F32),
        input_output_aliases={} if into is None else {1: 0},
        compiler_params=_params(("arbitrary",)),
    )(*([r] if into is None else [r, into]))


def _adamw(w, g, m, v):
    m = ADAM_B1 * m + (1.0 - ADAM_B1) * g
    v = ADAM_B2 * v + (1.0 - ADAM_B2) * jnp.square(g)
    m_hat = m / (1.0 - ADAM_B1 ** ADAM_STEP)
    v_hat = v / (1.0 - ADAM_B2 ** ADAM_STEP)
    delta = -ADAM_LR * (m_hat / (jnp.sqrt(v_hat) + ADAM_EPS) + ADAM_WD * w)
    return delta, m, v


def adamw_big(w, s_mine, s_sibling, m, v, name):
    rows, cols = w.shape

    def body(w_ref, a_ref, b_ref, m_ref, v_ref, g_out, d_out, m_out, v_out):
        g = a_ref[...] + b_ref[...]
        g_out[...] = g
        d_out[...], m_out[...], v_out[...] = _adamw(w_ref[...], g, m_ref[...], v_ref[...])

    blk = pl.BlockSpec((min(rows, ELT_ROWS), cols), lambda i: (i, 0))
    out = jax.ShapeDtypeStruct((rows, cols), F32)
    return pl.pallas_call(
        body, name=name, grid=(rows // min(rows, ELT_ROWS),),
        in_specs=[blk] * 5, out_specs=[blk] * 4, out_shape=[out] * 4,
        compiler_params=_params(("arbitrary",)),
    )(w, s_mine, s_sibling, m, v)


def adamw_small(ws, gs, ms, vs, name):
    n = len(ws)

    def body(*refs):
        w_refs, g_refs, m_refs, v_refs = (refs[k * n:(k + 1) * n] for k in range(4))
        d_out, m_out, v_out = (refs[(4 + k) * n:(5 + k) * n] for k in range(3))
        for i in range(n):
            d_out[i][...], m_out[i][...], v_out[i][...] = _adamw(
                w_refs[i][...], g_refs[i][...], m_refs[i][...], v_refs[i][...])

    outs = [jax.ShapeDtypeStruct(w.shape, F32) for w in ws]
    res = pl.pallas_call(body, name=name, out_shape=outs * 3)(*ws, *gs, *ms, *vs)
    return res[:n], res[n:2 * n], res[2 * n:]


SLAB_LANES = 128
SLAB_ROW_ALIGN = 8


def _pack(parts):
    flat = jnp.concatenate([p.reshape(-1) for p in parts])
    rows = -(-flat.shape[0] // (SLAB_LANES * SLAB_ROW_ALIGN)) * SLAB_ROW_ALIGN
    flat = jnp.pad(flat, (0, rows * SLAB_LANES - flat.shape[0]))
    return flat.reshape(rows, SLAB_LANES)


def _unpack(slab, shapes):
    flat = slab.reshape(-1)
    out, pos = [], 0
    for s in shapes:
        size = math.prod(s)
        out.append(flat[pos:pos + size].reshape(s))
        pos += size
    return out


def kernel(x, positions, norm_mix_pre, norm_mix_post, norm_ffn_pre, norm_ffn_post, w_in_even, lb_table, a_norm, b_ln_g, b_ln_b, b_ws, b_bias, w_out_even, w_in_odd, w_out_odd, w_ff1, w_ff2, loss_target, m_norm_mix_pre, m_norm_mix_post, m_norm_ffn_pre, m_norm_ffn_post, m_w_in_even, m_lb_table, m_a_norm, m_b_ln_g, m_b_ln_b, m_b_ws, m_b_bias, m_w_out_even, m_w_in_odd, m_w_out_odd, m_w_ff1, m_w_ff2, v_norm_mix_pre, v_norm_mix_post, v_norm_ffn_pre, v_norm_ffn_post, v_w_in_even, v_lb_table, v_a_norm, v_b_ln_g, v_b_ln_b, v_b_ws, v_b_bias, v_w_out_even, v_w_in_odd, v_w_out_odd, v_w_ff1, v_w_ff2):
    batch = x.shape[0]
    t = batch * SEQ
    d = D_MODEL
    x0 = x.reshape(t, d)
    target = loss_target.reshape(t, d)

    def gain(p, layer):
        return p[layer:layer + 1]

    def gather(*shards):
        return _Exchange("gather", [w.astype(BF16) for w in shards])

    def scatter(*grads):
        return _Exchange("scatter", grads)

    (win_e,) = exchange_alone(gather(w_in_even[0]), "gather_in_even")
    bias_t = b_bias[0].T
    proj, h0, w1_0 = norm_matmul(x0, gain(norm_mix_pre, 0), win_e, "in_proj_even", exchange=gather(w_ff1[0]))
    oa, states, decays, w2_0 = hgrn2_fwd(proj, lb_table, a_norm, batch, "hgrn2_fwd", exchange=gather(w_ff2[0]))
    mixin, wout_e = gmlp_fwd(proj, oa, b_ln_g, b_ln_b, b_ws[0], bias_t, "gmlp_fwd", exchange=gather(w_out_even[0]))
    mix0, x1 = out_proj(mixin, wout_e, x0, gain(norm_mix_post, 0), "out_proj_even")
    x2, hf0, a0, y0, win_o, wout_o = ffn_fwd(x1, gain(norm_ffn_pre, 0), w1_0, w2_0, gain(norm_ffn_post, 0),
                                             "ffn_fwd_0", exchange=gather(w_in_odd[0], w_out_odd[0]))
    x2p = _residue_major(x2, batch)
    qkv, h1 = norm_matmul(x2p, gain(norm_mix_pre, 1), win_o, "in_proj_odd")
    cos_t, sin_a, sin_b = rope_tables(_residue_major(positions.reshape(t, 1), batch), "rope_tables")
    ao, lse, w1_1, w2_1 = attn_fwd(qkv, cos_t, sin_a, sin_b, batch, "attn_fwd", exchange=gather(w_ff1[1], w_ff2[1]))
    mix1, x3 = out_proj(ao, wout_o, x2p, gain(norm_mix_post, 1), "out_proj_odd")
    dx4, hf1, a1, y1, loss_part = ffn_fwd(x3, gain(norm_ffn_pre, 1), w1_1, w2_1, gain(norm_ffn_post, 1),
                                          "ffn_fwd_1", target=_residue_major(target, batch))

    hc = D_FF // N_CHIPS
    dx3, dy1, da1, dg_fpre1, dg_fpost1 = ffn_bwd(
        dx4, x3, y1, a1, gain(norm_ffn_pre, 1), gain(norm_ffn_post, 1), w1_1, w2_1, "ffn_bwd_1")
    g_w1_1 = weight_grad(hf1, da1, "b", d, hc, False, "wgrad_ff1_1")
    g_w2_1 = weight_grad(a1, dy1, "a", hc, d, True, "wgrad_ff2_1")
    dmix1, dao, dg_mpost1 = out_proj_bwd(dx3, mix1, gain(norm_mix_post, 1), wout_o, "out_proj_bwd_odd")
    g_wout_o = weight_grad(ao, dmix1, "a", d // N_CHIPS, d, False, "wgrad_out_odd")
    dqkv, r_w1_1, r_w2_1, r_wout_o = attn_bwd(qkv, cos_t, sin_a, sin_b, ao, lse, dao, batch, "attn_bwd",
                                              exchange=scatter(g_w1_1, g_w2_1, g_wout_o))
    dx2p, dg_mpre1 = norm_matmul_bwd(dqkv, win_o, x2p, gain(norm_mix_pre, 1), dx3, "in_proj_bwd_odd")
    dx2 = _sequence_order(dx2p, batch)
    g_win_o = weight_grad_stacked(h1, dqkv, 3 * d // N_CHIPS, "wgrad_in_odd")
    dx1, dy0, da0, dg_fpre0, dg_fpost0, r_win_o = ffn_bwd(
        dx2, x1, y0, a0, gain(norm_ffn_pre, 0), gain(norm_ffn_post, 0), w1_0, w2_0, "ffn_bwd_0",
        exchange=scatter(g_win_o))
    g_w1_0 = weight_grad(hf0, da0, "b", d, hc, False, "wgrad_ff1_0")
    g_w2_0 = weight_grad(a0, dy0, "a", hc, d, True, "wgrad_ff2_0")
    dmix0, dmixin, dg_mpost0 = out_proj_bwd(dx1, mix0, gain(norm_mix_post, 0), wout_e, "out_proj_bwd_even")
    g_wout_e = weight_grad(mixin, dmix0, "a", d // N_CHIPS, d, False, "wgrad_out_even")
    dproj, d_lb, d_anorm, r_w1_0 = hgrn2_bwd(
        proj, states, decays, lb_table, a_norm, dmixin, batch, "hgrn2_bwd", exchange=scatter(g_w1_0))
    dproj, d_lng, d_lnb, d_ws, d_bias_t, r_w2_0 = gmlp_bwd(
        proj, dmixin, b_ln_g, b_ln_b, b_ws[0], bias_t, dproj, "gmlp_bwd", exchange=scatter(g_w2_0))
    g_win_e, r_wout_e = weight_grad(h0, dproj, "b", d, 3 * d // N_CHIPS, False, "wgrad_in_even",
                                    exchange=scatter(g_wout_e))
    dx0, dg_mpre0, r_win_e = norm_matmul_bwd(dproj, win_e, x0, gain(norm_mix_pre, 0), dx1, "in_proj_bwd_even",
                                             exchange=scatter(g_win_e))
    grad_x = dx0.reshape(x.shape)

    s_w1 = reduce_slabs(r_w1_1, "reduce_ff1_1", part=1, parts=2)
    s_w1 = reduce_slabs(r_w1_0, "reduce_ff1_0", part=0, parts=2, into=s_w1)
    s_w2 = reduce_slabs(r_w2_1, "reduce_ff2_1", part=1, parts=2)
    s_w2 = reduce_slabs(r_w2_0, "reduce_ff2_0", part=0, parts=2, into=s_w2)
    sums = [reduce_slabs(r_win_e, "reduce_in_even"), reduce_slabs(r_wout_e, "reduce_out_even"),
            reduce_slabs(r_win_o, "reduce_in_odd"), reduce_slabs(r_wout_o, "reduce_out_odd"), s_w1, s_w2]
    sibling = sibling_swap(sums, "sibling_swap")
    big_w = [w_in_even, w_out_even, w_in_odd, w_out_odd, w_ff1, w_ff2]
    big_m = [m_w_in_even, m_w_out_even, m_w_in_odd, m_w_out_odd, m_w_ff1, m_w_ff2]
    big_v = [v_w_in_even, v_w_out_even, v_w_in_odd, v_w_out_odd, v_w_ff1, v_w_ff2]
    big = []
    for i, (w, m, v) in enumerate(zip(big_w, big_m, big_v)):
        two_d = (-1, w.shape[-1])
        res = adamw_big(w.reshape(two_d), sums[i], sibling[i], m.reshape(two_d), v.reshape(two_d), "adamw_big_%d" % i)
        big.append([r.reshape(w.shape) for r in res])

    small_w = [norm_mix_pre, norm_mix_post, norm_ffn_pre, norm_ffn_post, lb_table, a_norm, b_ln_g, b_ln_b, b_ws, b_bias]
    small_m = [m_norm_mix_pre, m_norm_mix_post, m_norm_ffn_pre, m_norm_ffn_post, m_lb_table, m_a_norm, m_b_ln_g,
               m_b_ln_b, m_b_ws, m_b_bias]
    small_v = [v_norm_mix_pre, v_norm_mix_post, v_norm_ffn_pre, v_norm_ffn_post, v_lb_table, v_a_norm, v_b_ln_g,
               v_b_ln_b, v_b_ws, v_b_bias]
    partial = [jnp.concatenate([dg_mpre0, dg_mpre1]), jnp.concatenate([dg_mpost0, dg_mpost1]),
               jnp.concatenate([dg_fpre0, dg_fpre1]), jnp.concatenate([dg_fpost0, dg_fpost1]),
               d_lb, d_anorm, d_lng, d_lnb, d_ws[None], d_bias_t.T[None]]
    *small_g, loss = _unpack(allreduce_small(_pack(partial + [loss_part]), "allreduce_small"),
                             [w.shape for w in small_w] + [()])
    small_d, small_nm, small_nv = adamw_small(small_w, small_g, small_m, small_v, "adamw_small")

    order = ["norm_mix_pre", "norm_mix_post", "norm_ffn_pre", "norm_ffn_post", "w_in_even", "lb_table", "a_norm",
             "b_ln_g", "b_ln_b", "b_ws", "b_bias", "w_out_even", "w_in_odd", "w_out_odd", "w_ff1", "w_ff2"]
    small_names = ["norm_mix_pre", "norm_mix_post", "norm_ffn_pre", "norm_ffn_post", "lb_table", "a_norm",
                   "b_ln_g", "b_ln_b", "b_ws", "b_bias"]
    big_names = ["w_in_even", "w_out_even", "w_in_odd", "w_out_odd", "w_ff1", "w_ff2"]
    grads, deltas, new_m, new_v = {}, {}, {}, {}
    for i, nm in enumerate(small_names):
        grads[nm], deltas[nm], new_m[nm], new_v[nm] = small_g[i], small_d[i], small_nm[i], small_nv[i]
    for i, nm in enumerate(big_names):
        grads[nm], deltas[nm], new_m[nm], new_v[nm] = big[i]
    return (loss, grad_x, *[grads[n] for n in order], *[deltas[n] for n in order],
            *[new_m[n] for n in order], *[new_v[n] for n in order])
```

```python
import functools
import math

import jax
import jax.numpy as jnp
from jax import lax
from jax.experimental import pallas as pl
from jax.experimental.pallas import tpu as pltpu

F32 = jnp.float32
BF16 = jnp.bfloat16
MESH = pl.DeviceIdType.MESH

D_MODEL = 1024
SEQ = 2048
D_FF = 4096
N_CHIPS = 4
A_WIDTH = 512
A_HEADS = 4
A_DK = 128
A_CHUNK = 64
A_SUB = 16
B_WIDTH = 512
B_GROUPS = 4
B_CHUNK = 128
C_HEADS = 16
C_HEAD_DIM = 64
C_ROT_HALF = 8
C_BLOCK = 128
C_DILATIONS = (1, 4, 16)
ROPE_THETA = 500000.0
EPS = 1e-6
ADAM_LR = 0.001
ADAM_B1 = 0.9
ADAM_B2 = 0.999
ADAM_EPS = 1e-08
ADAM_WD = 0.01
ADAM_STEP = 10

ROW_TILE = 512
FFN_ROWS = 1024
WGRAD_ROWS = 2048
VMEM_LIMIT = 56 * 1024 * 1024
NEG_BIG = -1e30


def _params(sem=None):
    return pltpu.CompilerParams(dimension_semantics=sem, vmem_limit_bytes=VMEM_LIMIT)


def _dot(a, b):
    return jnp.dot(a, b, preferred_element_type=F32)


def _dot_nt(a, b):
    return lax.dot_general(a, b, (((1,), (1,)), ((), ())), preferred_element_type=F32)


def _dot_tn(a, b):
    return lax.dot_general(a, b, (((0,), (0,)), ((), ())), preferred_element_type=F32)


def _rms(x, g):
    r = lax.rsqrt(jnp.mean(x * x, axis=-1, keepdims=True) + EPS)
    return x * r * g


def _rms_bwd(x, g, dy):
    r = lax.rsqrt(jnp.mean(x * x, axis=-1, keepdims=True) + EPS)
    xh = x * r
    dg = jnp.sum(dy * xh, axis=0, keepdims=True)
    dxh = dy * g
    dx = r * (dxh - xh * jnp.mean(dxh * xh, axis=-1, keepdims=True))
    return dx, dg


def _accumulate(ref, val, first):
    @pl.when(first)
    def _():
        ref[...] = val

    @pl.when(jnp.logical_not(first))
    def _():
        ref[...] += val


N_DEV = 8
ANY = pl.BlockSpec(memory_space=pl.ANY)


def _place():
    x, y, c = lax.axis_index("x"), lax.axis_index("y"), lax.axis_index("c")
    return x, y, c, [(1 - x, y), (x, 1 - y), (1 - x, 1 - y)]


class _Exchange:
    def __init__(self, kind, arrays):
        self.kind, self.arrays, self.n = kind, list(arrays), len(arrays)
        per_peer = pltpu.SemaphoreType.DMA((3 * self.n,))
        if kind == "gather":
            self.out_shape = [jax.ShapeDtypeStruct((N_CHIPS,) + a.shape, a.dtype) for a in self.arrays]
            self.scratch = [per_peer, per_peer, pltpu.SemaphoreType.DMA((self.n,)), per_peer, per_peer]
        else:
            self.out_shape = [jax.ShapeDtypeStruct(a.shape, a.dtype) for a in self.arrays]
            self.scratch = [per_peer, per_peer, pltpu.SemaphoreType.DMA((self.n,))]

    def _copies(self, ins, outs, sems):
        send_sems, recv_sems, local_sems = sems[:3]
        x, y, c, chips = _place()
        me = 2 * x + y
        local, remote = [], []
        for a in range(self.n):
            if self.kind == "gather":
                local.append(pltpu.make_async_copy(ins[a], outs[a].at[me], local_sems.at[a]))
                half = self.arrays[a].shape[0] // 2

                def rows(ref, core, half=half):
                    return ref.at[pl.ds(core * half, half)]
            else:
                local.append(pltpu.make_async_copy(ins[a].at[me], outs[a].at[3], local_sems.at[a]))
            for j, (px, py) in enumerate(chips):
                k = 3 * a + j
                peer = 2 * px + py

                def copy(src, dst, to, send_sem=send_sems.at[k], recv_sem=recv_sems.at[k]):
                    return pltpu.make_async_remote_copy(src_ref=src, dst_ref=dst, send_sem=send_sem, recv_sem=recv_sem,
                                                        device_id=to, device_id_type=MESH)

                if self.kind == "gather":
                    sent = copy(rows(ins[a], c), rows(outs[a].at[me], c), (px, py, c))
                    landed = copy(rows(ins[a], c), rows(outs[a].at[peer], c), (px, py, c))
                    on = dict(send_sem=sems[3].at[k], recv_sem=sems[4].at[k])
                    passed = copy(rows(outs[a].at[peer], c), rows(outs[a].at[peer], c), (x, y, 1 - c), **on)
                    handed = copy(rows(outs[a].at[peer], c), rows(outs[a].at[peer], 1 - c), (x, y, 1 - c), **on)
                    remote.append((sent, landed, passed, handed))
                else:
                    sent = copy(ins[a].at[peer], outs[a].at[j], (px, py, c))
                    remote.append((sent, sent, None, None))
        return local, remote

    def start(self, ins, outs, sems):
        local, remote = self._copies(ins, outs, sems)
        for cp in local:
            cp.start()
        for sent, _, _, _ in remote:
            sent.start()

    def finish(self, ins, outs, sems):
        local, remote = self._copies(ins, outs, sems)
        for _, landed, passed, _ in remote:
            landed.wait_recv()
            if passed is not None:
                passed.start()
        for sent, _, passed, handed in remote:
            if passed is not None:
                handed.wait_recv()
                passed.wait_send()
            sent.wait_send()
        for cp in local:
            cp.wait()


def _call(body, *, name, grid, in_specs, out_specs, out_shape, args, scratch_shapes=(), aliases=None, exchange=None):
    if exchange is None:
        return pl.pallas_call(
            body, name=name, grid=grid, in_specs=in_specs, out_specs=out_specs, out_shape=out_shape,
            scratch_shapes=list(scratch_shapes), input_output_aliases=aliases or {},
            compiler_params=_params(("arbitrary",) * len(grid)))(*args)
    n_in, n_out, n_scr, n_ex = len(in_specs), len(out_specs), len(scratch_shapes), exchange.n
    steps = grid

    def wrapped(*refs):
        ins, refs = refs[:n_in], refs[n_in:]
        ex_in, refs = refs[:n_ex], refs[n_ex:]
        outs, refs = refs[:n_out], refs[n_out:]
        ex_out, refs = refs[:n_ex], refs[n_ex:]
        scr, sems = refs[:n_scr], refs[n_scr:]
        first = functools.reduce(jnp.logical_and, [pl.program_id(k) == 0 for k in range(len(steps))])
        last = functools.reduce(jnp.logical_and, [pl.program_id(k) == steps[k] - 1 for k in range(len(steps))])

        @pl.when(first)
        def _():
            exchange.start(ex_in, ex_out, sems)

        body(*ins, *outs, *scr)

        @pl.when(last)
        def _():
            exchange.finish(ex_in, ex_out, sems)

    return pl.pallas_call(
        wrapped, name=name, grid=grid,
        in_specs=list(in_specs) + [ANY] * n_ex, out_specs=list(out_specs) + [ANY] * n_ex,
        out_shape=list(out_shape) + exchange.out_shape,
        scratch_shapes=list(scratch_shapes) + exchange.scratch, input_output_aliases=aliases or {},
        compiler_params=_params(("arbitrary",) * len(grid)))(*args, *exchange.arrays)


def exchange_alone(exchange, name):
    def body(*refs):
        n = exchange.n
        exchange.start(refs[:n], refs[n:2 * n], refs[2 * n:])
        exchange.finish(refs[:n], refs[n:2 * n], refs[2 * n:])

    return pl.pallas_call(
        body, name=name, in_specs=[ANY] * exchange.n, out_specs=[ANY] * exchange.n,
        out_shape=exchange.out_shape, scratch_shapes=exchange.scratch)(*exchange.arrays)


def norm_matmul(x, g, wg, name, exchange=None):
    t, d = x.shape
    nl = wg.shape[2]

    def body(x_ref, g_ref, w_ref, o_ref, h_ref):
        h = _rms(x_ref[...], g_ref[...]).astype(BF16)
        h_ref[...] = h
        for c in range(N_CHIPS):
            o_ref[:, c * nl:(c + 1) * nl] = _dot(h, w_ref[c])

    return _call(
        body, name=name, grid=(t // ROW_TILE,),
        in_specs=[pl.BlockSpec((ROW_TILE, d), lambda i: (i, 0)),
                  pl.BlockSpec((1, d), lambda i: (0, 0)),
                  pl.BlockSpec((N_CHIPS, d, nl), lambda i: (0, 0, 0))],
        out_specs=[pl.BlockSpec((ROW_TILE, N_CHIPS * nl), lambda i: (i, 0)),
                   pl.BlockSpec((ROW_TILE, d), lambda i: (i, 0))],
        out_shape=[jax.ShapeDtypeStruct((t, N_CHIPS * nl), F32), jax.ShapeDtypeStruct((t, d), BF16)],
        args=(x, g, wg), exchange=exchange)


def norm_matmul_bwd(dproj, wg, x, g, dres, name, exchange=None):
    t, d = x.shape
    nl = wg.shape[2]
    stacked = dproj.ndim == 3
    piece = math.gcd(nl, dproj.shape[-1])

    def body(dp_ref, w_ref, x_ref, g_ref, dres_ref, dx_ref, dg_ref):
        dh = None
        for j in range(N_CHIPS * nl // piece):
            c, off = divmod(j * piece, nl)
            if stacked:
                p, lo = divmod(j * piece, dproj.shape[-1])
                lhs = dp_ref[p, :, lo:lo + piece]
            else:
                lhs = dp_ref[:, j * piece:(j + 1) * piece]
            part = _dot_nt(lhs.astype(BF16), w_ref[c, :, off:off + piece])
            dh = part if dh is None else dh + part
        dx, dg = _rms_bwd(x_ref[...], g_ref[...], dh)
        dx_ref[...] = dres_ref[...] + dx
        _accumulate(dg_ref, dg, pl.program_id(0) == 0)

    row = pl.BlockSpec((ROW_TILE, d), lambda i: (i, 0))
    vec = pl.BlockSpec((1, d), lambda i: (0, 0))
    if stacked:
        dp_spec = pl.BlockSpec((dproj.shape[0], ROW_TILE, dproj.shape[-1]), lambda i: (0, i, 0))
    else:
        dp_spec = pl.BlockSpec((ROW_TILE, N_CHIPS * nl), lambda i: (i, 0))
    return _call(
        body, name=name, grid=(t // ROW_TILE,),
        in_specs=[dp_spec, pl.BlockSpec((N_CHIPS, d, nl), lambda i: (0, 0, 0)), row, vec, row],
        out_specs=[row, vec],
        out_shape=[jax.ShapeDtypeStruct((t, d), F32), jax.ShapeDtypeStruct((1, d), F32)],
        args=(dproj, wg, x, g, dres), exchange=exchange)


def out_proj(a, wg, x, g, name):
    t, d = x.shape
    kl = wg.shape[1]

    def body(a_ref, w_ref, x_ref, g_ref, mix_ref, xo_ref):
        acc = _dot(a_ref[:, 0:kl], w_ref[0])
        for c in range(1, N_CHIPS):
            acc += _dot(a_ref[:, c * kl:(c + 1) * kl], w_ref[c])
        mix_ref[...] = acc
        xo_ref[...] = x_ref[...] + _rms(acc, g_ref[...])

    row = pl.BlockSpec((ROW_TILE, d), lambda i: (i, 0))
    return pl.pallas_call(
        body, name=name, grid=(t // ROW_TILE,),
        in_specs=[row, pl.BlockSpec((N_CHIPS, kl, d), lambda i: (0, 0, 0)), row,
                  pl.BlockSpec((1, d), lambda i: (0, 0))],
        out_specs=[row, row],
        out_shape=[jax.ShapeDtypeStruct((t, d), F32), jax.ShapeDtypeStruct((t, d), F32)],
        compiler_params=_params(("arbitrary",)),
    )(a, wg, x, g)


def out_proj_bwd(dxo, mix, g, wg, name):
    t, d = mix.shape
    kl = wg.shape[1]

    def body(dxo_ref, mix_ref, g_ref, w_ref, dmix_ref, da_ref, dg_ref):
        dmix, dg = _rms_bwd(mix_ref[...], g_ref[...], dxo_ref[...])
        dmb = dmix.astype(BF16)
        dmix_ref[...] = dmb
        for c in range(N_CHIPS):
            da_ref[:, c * kl:(c + 1) * kl] = _dot_nt(dmb, w_ref[c])
        _accumulate(dg_ref, dg, pl.program_id(0) == 0)

    row = pl.BlockSpec((ROW_TILE, d), lambda i: (i, 0))
    vec = pl.BlockSpec((1, d), lambda i: (0, 0))
    return pl.pallas_call(
        body, name=name, grid=(t // ROW_TILE,),
        in_specs=[row, row, vec, pl.BlockSpec((N_CHIPS, kl, d), lambda i: (0, 0, 0))],
        out_specs=[row, row, vec],
        out_shape=[jax.ShapeDtypeStruct((t, d), BF16), jax.ShapeDtypeStruct((t, d), F32),
                   jax.ShapeDtypeStruct((1, d), F32)],
        compiler_params=_params(("arbitrary",)),
    )(dxo, mix, g, wg)


def ffn_fwd(x, gpre, w1g, w2g, gpost, name, exchange=None, target=None):
    t, d = x.shape
    hc = w1g.shape[2]
    with_loss = target is not None

    def body(x_ref, gpre_ref, w1_ref, w2_ref, gpost_ref, *rest):
        if with_loss:
            t_ref, xo_ref, h_ref, a_ref, y_ref, l_ref, acc = rest
        else:
            xo_ref, h_ref, a_ref, y_ref, acc = rest
        i, c = pl.program_id(0), pl.program_id(1)

        @pl.when(c == 0)
        def _():
            h_ref[...] = _rms(x_ref[...], gpre_ref[...]).astype(BF16)

        a = _dot(h_ref[...], w1_ref[...])
        a_ref[...] = a.astype(BF16)
        r = jnp.square(jnp.maximum(a, 0.0)).astype(BF16)
        _accumulate(acc, _dot(r, w2_ref[...]), c == 0)

        @pl.when(c == N_CHIPS - 1)
        def _():
            y = acc[...]
            y_ref[...] = y
            xo = x_ref[...] + _rms(y, gpost_ref[...])
            if with_loss:
                e = xo - t_ref[...]
                xo_ref[...] = e * (1.0 / d)
                part = jnp.sum(jnp.sum(e * e, axis=-1, keepdims=True), axis=0, keepdims=True) * (0.5 / d)
                _accumulate(l_ref, part, i == 0)
            else:
                xo_ref[...] = xo

    row = pl.BlockSpec((FFN_ROWS, d), lambda i, c: (i, 0))
    vec = pl.BlockSpec((1, d), lambda i, c: (0, 0))
    one = pl.BlockSpec((1, 1), lambda i, c: (0, 0))
    return _call(
        body, name=name, grid=(t // FFN_ROWS, N_CHIPS),
        in_specs=[row, vec,
                  pl.BlockSpec((None, d, hc), lambda i, c: (c, 0, 0)),
                  pl.BlockSpec((None, hc, d), lambda i, c: (c, 0, 0)), vec] + ([row] if with_loss else []),
        out_specs=[row, row, pl.BlockSpec((FFN_ROWS, hc), lambda i, c: (i, c)), row] + ([one] if with_loss else []),
        out_shape=[jax.ShapeDtypeStruct((t, d), F32), jax.ShapeDtypeStruct((t, d), BF16),
                   jax.ShapeDtypeStruct((t, N_CHIPS * hc), BF16), jax.ShapeDtypeStruct((t, d), F32)]
        + ([jax.ShapeDtypeStruct((1, 1), F32)] if with_loss else []),
        scratch_shapes=[pltpu.VMEM((FFN_ROWS, d), F32)],
        args=(x, gpre, w1g, w2g, gpost) + ((target,) if with_loss else ()), exchange=exchange)


def ffn_bwd(dxo, x, y, a, gpre, gpost, w1g, w2g, name, exchange=None):
    t, d = x.shape
    hc = w1g.shape[2]

    def body(dxo_ref, x_ref, y_ref, a_ref, gpre_ref, gpost_ref, w1_ref, w2_ref,
             dxi_ref, dy_ref, da_ref, dgpre_ref, dgpost_ref, acc):
        i, c = pl.program_id(0), pl.program_id(1)

        @pl.when(c == 0)
        def _():
            dy, dg = _rms_bwd(y_ref[...], gpost_ref[...], dxo_ref[...])
            dy_ref[...] = dy.astype(BF16)
            _accumulate(dgpost_ref, dg, i == 0)

        dr = _dot_nt(dy_ref[...], w2_ref[...])
        da = (dr * (2.0 * jnp.maximum(a_ref[...].astype(F32), 0.0))).astype(BF16)
        da_ref[...] = da
        _accumulate(acc, _dot_nt(da, w1_ref[...]), c == 0)

        @pl.when(c == N_CHIPS - 1)
        def _():
            dx, dg = _rms_bwd(x_ref[...], gpre_ref[...], acc[...])
            dxi_ref[...] = dxo_ref[...] + dx
            _accumulate(dgpre_ref, dg, i == 0)

    row = pl.BlockSpec((ROW_TILE, d), lambda i, c: (i, 0))
    vec = pl.BlockSpec((1, d), lambda i, c: (0, 0))
    hid = pl.BlockSpec((ROW_TILE, hc), lambda i, c: (i, c))
    return _call(
        body, name=name, grid=(t // ROW_TILE, N_CHIPS),
        in_specs=[row, row, row, hid, vec, vec,
                  pl.BlockSpec((None, d, hc), lambda i, c: (c, 0, 0)),
                  pl.BlockSpec((None, hc, d), lambda i, c: (c, 0, 0))],
        out_specs=[row, row, hid, vec, vec],
        out_shape=[jax.ShapeDtypeStruct((t, d), F32), jax.ShapeDtypeStruct((t, d), BF16),
                   jax.ShapeDtypeStruct((t, N_CHIPS * hc), BF16),
                   jax.ShapeDtypeStruct((1, d), F32), jax.ShapeDtypeStruct((1, d), F32)],
        scratch_shapes=[pltpu.VMEM((ROW_TILE, d), F32)],
        args=(dxo, x, y, a, gpre, gpost, w1g, w2g), exchange=exchange)


def weight_grad(a, b, chunked, bk, bn, relu2, name, exchange=None):
    t = a.shape[0]
    a_on = chunked == "a"
    rows = min(t, WGRAD_ROWS)
    n_steps = t // rows

    def body(a_ref, b_ref, o_ref, acc):
        s = pl.program_id(1)
        av = a_ref[...]
        if relu2:
            av = jnp.square(jnp.maximum(av.astype(F32), 0.0))
        _accumulate(acc, _dot_tn(av.astype(BF16), b_ref[...].astype(BF16)), s == 0)

        @pl.when(s == n_steps - 1)
        def _():
            o_ref[...] = acc[...].astype(BF16)

    res = _call(
        body, name=name, grid=(N_CHIPS, n_steps),
        in_specs=[pl.BlockSpec((rows, bk), (lambda c, s: (s, c)) if a_on else (lambda c, s: (s, 0))),
                  pl.BlockSpec((rows, bn), (lambda c, s: (s, 0)) if a_on else (lambda c, s: (s, c)))],
        out_specs=[pl.BlockSpec((None, bk, bn), lambda c, s: (c, 0, 0))],
        out_shape=[jax.ShapeDtypeStruct((N_CHIPS, bk, bn), BF16)],
        scratch_shapes=[pltpu.VMEM((bk, bn), F32)],
        args=(a, b), exchange=exchange)
    return res[0] if exchange is None else res


def weight_grad_stacked(a, b3, bn, name):
    t, bk = a.shape
    width = b3.shape[-1]
    piece = math.gcd(bn, width)
    rows = min(t, WGRAD_ROWS)
    n_steps = t // rows

    def body(a_ref, b_ref, o_hbm, acc, staged, sem):
        s, c = pl.program_id(0), pl.program_id(1)
        av = a_ref[...].astype(BF16)
        for chunk in range(N_CHIPS):
            @pl.when(c == chunk)
            def _(chunk=chunk):
                cols = [divmod(chunk * bn + k * piece, width) for k in range(bn // piece)]
                b = jnp.concatenate([b_ref[p, :, lo:lo + piece] for p, lo in cols], axis=1).astype(BF16)
                _accumulate(acc.at[chunk], _dot_tn(av, b), s == 0)

                @pl.when(s == n_steps - 1)
                def _():
                    staged[...] = acc[chunk].astype(BF16)
                    copy = pltpu.make_async_copy(staged, o_hbm.at[chunk], sem)
                    copy.start()
                    copy.wait()

    return pl.pallas_call(
        body, name=name, grid=(n_steps, N_CHIPS),
        in_specs=[pl.BlockSpec((rows, bk), lambda s, c: (s, 0)),
                  pl.BlockSpec((b3.shape[0], rows, width), lambda s, c: (0, s, 0))],
        out_specs=ANY,
        out_shape=jax.ShapeDtypeStruct((N_CHIPS, bk, bn), BF16),
        scratch_shapes=[pltpu.VMEM((N_CHIPS, bk, bn), F32), pltpu.VMEM((bk, bn), BF16), pltpu.SemaphoreType.DMA],
        compiler_params=_params(("arbitrary", "arbitrary")),
    )(a, b3)


def _hgrn2_chunk(st, qs, fls, ivs, gls, l0, l1, l2, ng):
    nsub = len(qs)
    mx = jnp.maximum(jnp.maximum(l0, l1), l2)
    e0, e1, e2 = jnp.exp(l0 - mx), jnp.exp(l1 - mx), jnp.exp(l2 - mx)
    lb = e0 / (e0 + e1 + e2)
    rows = lax.broadcasted_iota(jnp.int32, (A_SUB, A_SUB), 0)
    cols = lax.broadcasted_iota(jnp.int32, (A_SUB, A_SUB), 1)
    tri = (rows >= cols).astype(F32)
    keep = (lax.broadcasted_iota(jnp.int32, (A_SUB, A_SUB, A_DK), 0)
            >= lax.broadcasted_iota(jnp.int32, (A_SUB, A_SUB, A_DK), 1))
    base = jnp.zeros_like(l0)
    bases, gs, ks, qfs = [], [], [], []
    for i in range(nsub):
        f = lb + (1.0 - lb) * jax.nn.sigmoid(fls[i])
        logf = jnp.log(f)
        bases.append(base)
        gs.append(base + jnp.dot(tri, logf, precision=lax.Precision.HIGHEST, preferred_element_type=F32))
        base = base + jnp.sum(logf, axis=0, keepdims=True)
        ks.append(1.0 - f)
        qfs.append(jax.nn.silu(qs[i]))
    g_last = base
    stb = st.astype(BF16)
    outs = []
    for i in range(nsub):
        o = _dot_nt((qfs[i] * jnp.exp(gs[i])).astype(BF16), stb)
        if i > 0:
            qt = (qfs[i] * jnp.exp(gs[i] - bases[i])).astype(BF16)
            kk = jnp.concatenate([ks[j] * jnp.exp(bases[i] - gs[j]) for j in range(i)], axis=0).astype(BF16)
            vv = jnp.concatenate(ivs[:i], axis=0).astype(BF16)
            o = o + _dot(_dot_nt(qt, kk).astype(BF16), vv)
        dec = jnp.exp(jnp.where(keep, gs[i][:, None, :] - gs[i][None, :, :], NEG_BIG))
        s_diag = jnp.sum(qfs[i][:, None, :] * ks[i][None, :, :] * dec, axis=-1)
        o = o + _dot(s_diag.astype(BF16), ivs[i].astype(BF16))
        o = o * lax.rsqrt(jnp.mean(o * o, axis=-1, keepdims=True) + EPS) * ng
        outs.append(o * jax.nn.silu(gls[i]))
    kdec = jnp.concatenate([ks[j] * jnp.exp(g_last - gs[j]) for j in range(nsub)], axis=0).astype(BF16)
    vall = jnp.concatenate(ivs, axis=0).astype(BF16)
    new_st = st * jnp.exp(g_last) + _dot_tn(vall, kdec)
    return new_st, outs


A_MAX_LOG_DECAY = 60.0


def _half_sums(logf):
    n = logf.shape[0]
    first = lax.broadcasted_iota(jnp.int32, logf.shape, 0) < n // 2
    return (jnp.sum(jnp.where(first, logf, 0.0), axis=0, keepdims=True),
            jnp.sum(jnp.where(first, 0.0, logf), axis=0, keepdims=True))


def _split3(x):
    hi = x.astype(BF16)
    r1 = x - hi.astype(F32)
    mid = r1.astype(BF16)
    return hi, mid, (r1 - mid.astype(F32)).astype(BF16)


def _tri_matmul(x, transpose):
    n = x.shape[0]
    r = lax.broadcasted_iota(jnp.int32, (n, n), 0)
    c = lax.broadcasted_iota(jnp.int32, (n, n), 1)
    tri = ((r <= c) if transpose else (r >= c)).astype(BF16)
    hi, mid, lo = _split3(x)
    return (_dot(tri, lo) + _dot(tri, mid)) + _dot(tri, hi)


@jax.custom_vjp
def _cumsum_rows(x):
    return _tri_matmul(x, False)


def _cumsum_rows_fwd(x):
    return _tri_matmul(x, False), None


def _cumsum_rows_bwd(_, dy):
    return (_tri_matmul(dy, True),)


_cumsum_rows.defvjp(_cumsum_rows_fwd, _cumsum_rows_bwd)


def _lower_bound(l0, l1, l2):
    mx = jnp.maximum(jnp.maximum(l0, l1), l2)
    e0, e1, e2 = jnp.exp(l0 - mx), jnp.exp(l1 - mx), jnp.exp(l2 - mx)
    return e0 / (e0 + e1 + e2)


def _b(x):
    return x.astype(BF16)


@jax.custom_vjp
def _mm(a, b):
    return _dot(_b(a), _b(b))


_mm.defvjp(lambda a, b: (_mm(a, b), (a, b)),
           lambda res, d: (_dot_nt(_b(d), _b(res[1])), _dot_tn(_b(res[0]), _b(d))))


@jax.custom_vjp
def _mm_nt(a, b):
    return _dot_nt(_b(a), _b(b))


_mm_nt.defvjp(lambda a, b: (_mm_nt(a, b), (a, b)),
              lambda res, d: (_dot(_b(d), _b(res[1])), _dot_tn(_b(d), _b(res[0]))))


def _dot_split(dot, a, b):
    ah, bh = _b(a), _b(b)
    al, bl = _b(a - ah.astype(F32)), _b(b - bh.astype(F32))
    return (dot(ah, bl) + dot(al, bh)) + dot(ah, bh)


@jax.custom_vjp
def _mm_scores(a, b):
    return _dot_nt(_b(a), _b(b))


_mm_scores.defvjp(lambda a, b: (_mm_scores(a, b), (a, b)),
                  lambda res, d: (_dot_split(_dot, d, res[1]), _dot_split(_dot_tn, d, res[0])))


@jax.custom_vjp
def _mm_tn(a, b):
    return _dot_tn(_b(a), _b(b))


_mm_tn.defvjp(lambda a, b: (_mm_tn(a, b), (a, b)),
              lambda res, d: (_dot_nt(_b(res[1]), _b(d)), _dot(_b(res[0]), _b(d))))


@jax.custom_vjp
def _split_heads(x):
    return tuple(x[:, h * A_DK:(h + 1) * A_DK] for h in range(A_HEADS))


def _split_heads_fwd(x):
    return _split_heads(x), None


def _split_heads_bwd(_, parts):
    return (jnp.concatenate(parts, axis=1),)


_split_heads.defvjp(_split_heads_fwd, _split_heads_bwd)


def _hgrn2_chunk_fast(sts, q, fl, iv, gl, l0, l1, l2, ng):
    lb = _lower_bound(l0, l1, l2)
    f = lb + (1.0 - lb) * jax.nn.sigmoid(fl)
    return _hgrn2_fast_core(sts, q, f, jnp.log(f), iv, gl, ng)


def _hgrn2_fast_core(sts, q, f, logf, iv, gl, ng):
    g = _cumsum_rows(logf)
    g_mid, g_last = _half_sums(logf)
    g_last = g_mid + g_last
    k = 1.0 - f
    qf = jax.nn.silu(q)
    qms = _split_heads(qf * jnp.exp(g - g_mid))
    kms = _split_heads(k * jnp.exp(g_mid - g))
    qgs = _split_heads(qf * jnp.exp(g))
    kds = _split_heads(k * jnp.exp(g_last - g))
    ivs = _split_heads(iv)
    decays = _split_heads(jnp.exp(g_last))
    n = q.shape[0]
    causal = lax.broadcasted_iota(jnp.int32, (n, n), 0) >= lax.broadcasted_iota(jnp.int32, (n, n), 1)
    raw = [_mm_scores(qm, km) for qm, km in zip(qms, kms)]
    inter = [_mm_nt(qg, st) for qg, st in zip(qgs, sts)]
    scores = [jnp.where(causal, s, 0.0) for s in raw]
    os = [a + _mm(s, v) for a, s, v in zip(inter, scores, ivs)]
    new_sts = [st * d + _mm_tn(v, kd) for st, d, v, kd in zip(sts, decays, ivs, kds)]
    os = [o * lax.rsqrt(jnp.mean(o * o, axis=-1, keepdims=True) + EPS) for o in os]
    return new_sts, jnp.concatenate(os, axis=1) * ng * jax.nn.silu(gl)


A_STEP_CHUNKS = 4


def _chunk_rows(j):
    return pl.ds(pl.multiple_of(j * A_CHUNK, A_CHUNK), A_CHUNK)


def _sub_rows(j, i):
    return pl.ds(pl.multiple_of(j * A_CHUNK + i * A_SUB, A_SUB), A_SUB)


def _sub_blocks(ref, head, j):
    lanes = slice(head * A_DK, (head + 1) * A_DK)
    return [ref[_sub_rows(j, i), lanes] for i in range(A_CHUNK // A_SUB)]


def hgrn2_fwd(proj, lb_table, a_norm, batch, name, exchange=None):
    t = proj.shape[0]
    n_steps = t // batch // (A_CHUNK * A_STEP_CHUNKS)
    rows = A_CHUNK * A_STEP_CHUNKS

    def body(q_ref, f_ref, i_ref, g_ref, lb_ref, ng_ref, o_ref, st_ref, dec_ref, st):
        @pl.when(pl.program_id(1) == 0)
        def _():
            st[...] = jnp.zeros_like(st)

        def chunk(j, carry):
            r = _chunk_rows(j)
            st_ref[j] = st[...]
            lb = _lower_bound(lb_ref[0:1, :], lb_ref[1:2, :], lb_ref[2:3, :])
            f = lb + (1.0 - lb) * jax.nn.sigmoid(f_ref[r, :])
            logf = jnp.log(f)
            decay = jnp.minimum(*_half_sums(logf))
            dec_ref[j] = decay
            mild = jnp.min(decay) >= -A_MAX_LOG_DECAY

            @pl.when(mild)
            def _():
                new_sts, o = _hgrn2_fast_core([st[h] for h in range(A_HEADS)], q_ref[r, :], f, logf,
                                              i_ref[r, :], g_ref[r, :], ng_ref[...])
                for h in range(A_HEADS):
                    st[h] = new_sts[h]
                o_ref[r, :] = o.astype(BF16)

            @pl.when(jnp.logical_not(mild))
            def _():
                for h in range(A_HEADS):
                    lanes = slice(h * A_DK, (h + 1) * A_DK)
                    new_st, outs = _hgrn2_chunk(
                        st[h], _sub_blocks(q_ref, h, j), _sub_blocks(f_ref, h, j), _sub_blocks(i_ref, h, j),
                        _sub_blocks(g_ref, h, j), lb_ref[0:1, lanes], lb_ref[1:2, lanes], lb_ref[2:3, lanes],
                        ng_ref[:, lanes])
                    st[h] = new_st
                    for i, o in enumerate(outs):
                        o_ref[_sub_rows(j, i), lanes] = o.astype(BF16)

            return carry

        lax.fori_loop(0, A_STEP_CHUNKS, chunk, 0)

    def part(k):
        return pl.BlockSpec((rows, A_WIDTH), lambda b, n: (b * n_steps + n, k))

    return _call(
        body, name=name, grid=(batch, n_steps),
        in_specs=[part(0), part(1), part(2), part(3),
                  pl.BlockSpec((3, A_WIDTH), lambda b, n: (0, 0)), pl.BlockSpec((1, A_WIDTH), lambda b, n: (0, 0))],
        out_specs=[part(0),
                   pl.BlockSpec((A_STEP_CHUNKS, A_HEADS, A_DK, A_DK), lambda b, n: (b * n_steps + n, 0, 0, 0)),
                   pl.BlockSpec((A_STEP_CHUNKS, 1, A_WIDTH), lambda b, n: (b * n_steps + n, 0, 0))],
        out_shape=[jax.ShapeDtypeStruct((t, A_WIDTH), BF16),
                   jax.ShapeDtypeStruct((t // A_CHUNK, A_HEADS, A_DK, A_DK), F32),
                   jax.ShapeDtypeStruct((t // A_CHUNK, 1, A_WIDTH), F32)],
        scratch_shapes=[pltpu.VMEM((A_HEADS, A_DK, A_DK), F32)],
        args=(proj, proj, proj, proj, lb_table, a_norm), exchange=exchange)


def hgrn2_bwd(proj, states, decays, lb_table, a_norm, do, batch, name, exchange=None):
    t = proj.shape[0]
    n_steps = t // batch // (A_CHUNK * A_STEP_CHUNKS)
    rows = A_CHUNK * A_STEP_CHUNKS

    def body(q_ref, f_ref, i_ref, g_ref, st_ref, dec_ref, lb_ref, ng_ref, do_ref, dp_ref, dlb_ref, dng_ref, dst):
        @pl.when(jnp.logical_and(pl.program_id(0) == 0, pl.program_id(1) == 0))
        def _():
            dlb_ref[...] = jnp.zeros_like(dlb_ref)
            dng_ref[...] = jnp.zeros_like(dng_ref)

        @pl.when(pl.program_id(1) == 0)
        def _():
            dst[...] = jnp.zeros_like(dst)

        def chunk(jj, carry):
            j = A_STEP_CHUNKS - 1 - jj
            r = _chunk_rows(j)
            mild = jnp.min(dec_ref[j]) >= -A_MAX_LOG_DECAY

            @pl.when(mild)
            def _():
                _, vjp = jax.vjp(
                    _hgrn2_chunk_fast, [st_ref[j, h] for h in range(A_HEADS)], q_ref[r, :], f_ref[r, :],
                    i_ref[r, :], g_ref[r, :], lb_ref[0:1, :], lb_ref[1:2, :], lb_ref[2:3, :], ng_ref[...])
                d_sts, dq, df, di, dg, dl0, dl1, dl2, dng = vjp(
                    ([dst[h] for h in range(A_HEADS)], do_ref[r, :].astype(F32)))
                for h in range(A_HEADS):
                    dst[h] = d_sts[h]
                for k, part in enumerate((dq, df, di, dg)):
                    dp_ref[r, k * A_WIDTH:(k + 1) * A_WIDTH] = part
                for row, val in enumerate((dl0, dl1, dl2)):
                    dlb_ref[row:row + 1, :] += val
                dng_ref[...] += dng

            @pl.when(jnp.logical_not(mild))
            def _():
                for h in range(A_HEADS):
                    lanes = slice(h * A_DK, (h + 1) * A_DK)
                    _, vjp = jax.vjp(
                        _hgrn2_chunk, st_ref[j, h], _sub_blocks(q_ref, h, j), _sub_blocks(f_ref, h, j),
                        _sub_blocks(i_ref, h, j), _sub_blocks(g_ref, h, j), lb_ref[0:1, lanes], lb_ref[1:2, lanes],
                        lb_ref[2:3, lanes], ng_ref[:, lanes])
                    douts = [x.astype(F32) for x in _sub_blocks(do_ref, h, j)]
                    d_st, dqs, dfs, dis, dgs, dl0, dl1, dl2, dng = vjp((dst[h], douts))
                    dst[h] = d_st
                    for k, parts in enumerate((dqs, dfs, dis, dgs)):
                        for i in range(A_CHUNK // A_SUB):
                            dp_ref[_sub_rows(j, i), k * A_WIDTH + h * A_DK:k * A_WIDTH + (h + 1) * A_DK] = parts[i]
                    for row, val in enumerate((dl0, dl1, dl2)):
                        dlb_ref[row:row + 1, lanes] += val
                    dng_ref[:, lanes] += dng

            return carry

        lax.fori_loop(0, A_STEP_CHUNKS, chunk, 0)

    def rev(b, n):
        return b * n_steps + (n_steps - 1 - n)

    def part(k):
        return pl.BlockSpec((rows, A_WIDTH), lambda b, n: (rev(b, n), k))

    const3 = pl.BlockSpec((3, A_WIDTH), lambda b, n: (0, 0))
    const1 = pl.BlockSpec((1, A_WIDTH), lambda b, n: (0, 0))
    return _call(
        body, name=name, grid=(batch, n_steps),
        in_specs=[part(0), part(1), part(2), part(3),
                  pl.BlockSpec((A_STEP_CHUNKS, A_HEADS, A_DK, A_DK), lambda b, n: (rev(b, n), 0, 0, 0)),
                  pl.BlockSpec((A_STEP_CHUNKS, 1, A_WIDTH), lambda b, n: (rev(b, n), 0, 0)),
                  const3, const1, part(0)],
        out_specs=[pl.BlockSpec((rows, 4 * A_WIDTH), lambda b, n: (rev(b, n), 0)), const3, const1],
        out_shape=[jax.ShapeDtypeStruct((t, 4 * A_WIDTH + 2 * B_WIDTH), F32),
                   jax.ShapeDtypeStruct((3, A_WIDTH), F32), jax.ShapeDtypeStruct((1, A_WIDTH), F32)],
        scratch_shapes=[pltpu.VMEM((A_HEADS, A_DK, A_DK), F32)],
        args=(proj, proj, proj, proj, states, decays, lb_table, a_norm, do), exchange=exchange)


B_GDIM = B_WIDTH // B_GROUPS
B_ROWS = 512


def _gmlp_chunk(ubs, vbs, lngs, lnbs, ws, bcols):
    vs = [jax.nn.gelu(v) for v in vbs]
    mu = sum(jnp.sum(v, axis=-1, keepdims=True) for v in vs) * (1.0 / B_WIDTH)
    var = sum(jnp.sum(jnp.square(v - mu), axis=-1, keepdims=True) for v in vs) * (1.0 / B_WIDTH)
    rstd = lax.rsqrt(var + EPS)
    tril = (lax.broadcasted_iota(jnp.int32, (B_CHUNK, B_CHUNK), 0)
            >= lax.broadcasted_iota(jnp.int32, (B_CHUNK, B_CHUNK), 1))
    outs = []
    for g in range(B_GROUPS):
        vn = (vs[g] - mu) * rstd * lngs[g] + lnbs[g]
        w = jnp.where(tril, ws[g], 0.0).astype(BF16)
        outs.append(jax.nn.gelu(ubs[g]) * (_dot(w, vn.astype(BF16)) + bcols[g]))
    return outs


def _gmlp_args(u_ref, v_ref, lng_ref, lnb_ref, w_ref, bt_ref, rows):
    def groups(ref):
        return [ref[rows, g * B_GDIM:(g + 1) * B_GDIM] for g in range(B_GROUPS)]

    def vec(ref):
        return [ref[:, g * B_GDIM:(g + 1) * B_GDIM] for g in range(B_GROUPS)]

    return (groups(u_ref), groups(v_ref), vec(lng_ref), vec(lnb_ref),
            [w_ref[g] for g in range(B_GROUPS)], [bt_ref[:, g:g + 1] for g in range(B_GROUPS)])


def gmlp_fwd(proj, oa, ln_g, ln_b, w, bias_t, name, exchange=None):
    t = proj.shape[0]

    def body(u_ref, v_ref, oa_ref, lng_ref, lnb_ref, w_ref, bt_ref, o_ref):
        o_ref[:, 0:A_WIDTH] = oa_ref[...]
        for n in range(B_ROWS // B_CHUNK):
            rows = slice(n * B_CHUNK, (n + 1) * B_CHUNK)
            outs = _gmlp_chunk(*_gmlp_args(u_ref, v_ref, lng_ref, lnb_ref, w_ref, bt_ref, rows))
            for g, o in enumerate(outs):
                o_ref[rows, A_WIDTH + g * B_GDIM:A_WIDTH + (g + 1) * B_GDIM] = o.astype(BF16)

    vec = pl.BlockSpec((1, B_WIDTH), lambda i: (0, 0))
    return _call(
        body, name=name, grid=(t // B_ROWS,),
        in_specs=[pl.BlockSpec((B_ROWS, B_WIDTH), lambda i: (i, 4)), pl.BlockSpec((B_ROWS, B_WIDTH), lambda i: (i, 5)),
                  pl.BlockSpec((B_ROWS, A_WIDTH), lambda i: (i, 0)), vec, vec,
                  pl.BlockSpec((B_GROUPS, B_CHUNK, B_CHUNK), lambda i: (0, 0, 0)),
                  pl.BlockSpec((B_CHUNK, B_GROUPS), lambda i: (0, 0))],
        out_specs=[pl.BlockSpec((B_ROWS, A_WIDTH + B_WIDTH), lambda i: (i, 0))],
        out_shape=[jax.ShapeDtypeStruct((t, A_WIDTH + B_WIDTH), BF16)],
        args=(proj, proj, oa, ln_g, ln_b, w, bias_t), exchange=exchange)


def gmlp_bwd(proj, dmixin, ln_g, ln_b, w, bias_t, dproj, name, exchange=None):
    t = proj.shape[0]

    def body(u_ref, v_ref, do_ref, lng_ref, lnb_ref, w_ref, bt_ref, dp_in_ref,
             dp_ref, dlng_ref, dlnb_ref, dw_ref, dbt_ref):
        del dp_in_ref

        @pl.when(pl.program_id(0) == 0)
        def _():
            for ref in (dlng_ref, dlnb_ref, dw_ref, dbt_ref):
                ref[...] = jnp.zeros_like(ref)

        for n in range(B_ROWS // B_CHUNK):
            rows = slice(n * B_CHUNK, (n + 1) * B_CHUNK)
            _, vjp = jax.vjp(_gmlp_chunk, *_gmlp_args(u_ref, v_ref, lng_ref, lnb_ref, w_ref, bt_ref, rows))
            douts = [do_ref[rows, g * B_GDIM:(g + 1) * B_GDIM] for g in range(B_GROUPS)]
            dus, dvs, dlngs, dlnbs, dws, dbs = vjp(douts)
            for g in range(B_GROUPS):
                lanes = slice(g * B_GDIM, (g + 1) * B_GDIM)
                dp_ref[rows, lanes] = dus[g]
                dp_ref[rows, B_WIDTH + g * B_GDIM:B_WIDTH + (g + 1) * B_GDIM] = dvs[g]
                dlng_ref[:, lanes] += dlngs[g]
                dlnb_ref[:, lanes] += dlnbs[g]
                dw_ref[g] += dws[g]
                dbt_ref[:, g:g + 1] += dbs[g]

    vec = pl.BlockSpec((1, B_WIDTH), lambda i: (0, 0))
    wspec = pl.BlockSpec((B_GROUPS, B_CHUNK, B_CHUNK), lambda i: (0, 0, 0))
    bspec = pl.BlockSpec((B_CHUNK, B_GROUPS), lambda i: (0, 0))
    return _call(
        body, name=name, grid=(t // B_ROWS,),
        in_specs=[pl.BlockSpec((B_ROWS, B_WIDTH), lambda i: (i, 4)), pl.BlockSpec((B_ROWS, B_WIDTH), lambda i: (i, 5)),
                  pl.BlockSpec((B_ROWS, B_WIDTH), lambda i: (i, 1)), vec, vec, wspec, bspec,
                  pl.BlockSpec(memory_space=pl.ANY)],
        out_specs=[pl.BlockSpec((B_ROWS, 2 * B_WIDTH), lambda i: (i, 2)), vec, vec, wspec, bspec],
        out_shape=[jax.ShapeDtypeStruct(dproj.shape, F32), jax.ShapeDtypeStruct((1, B_WIDTH), F32),
                   jax.ShapeDtypeStruct((1, B_WIDTH), F32), jax.ShapeDtypeStruct((B_GROUPS, B_CHUNK, B_CHUNK), F32),
                   jax.ShapeDtypeStruct((B_CHUNK, B_GROUPS), F32)],
        aliases={7: 0}, args=(proj, proj, dmixin, ln_g, ln_b, w, bias_t, dproj), exchange=exchange)


C_FWD_BLOCKS = 8
C_BWD_BLOCKS = 4
C_PAIR = 2 * C_HEAD_DIM
C_PAIRS = C_HEADS // 2
C_SCALE = 1.0 / math.sqrt(C_HEAD_DIM)
C_ROT_DIM = 2 * C_ROT_HALF
ROPE_ROWS = 1024


def rope_tables(pos_col, name):
    t = pos_col.shape[0]

    def body(p_ref, c_ref, a_ref, b_ref):
        lane = jnp.bitwise_and(lax.broadcasted_iota(jnp.int32, (1, C_PAIR), 1), C_HEAD_DIM - 1)
        j = jnp.bitwise_and(lane, C_ROT_HALF - 1).astype(F32)
        inv = jnp.exp(j * (-math.log(ROPE_THETA) / C_ROT_HALF))
        ang = p_ref[...].astype(F32) * inv
        cos, sin = jnp.cos(ang), jnp.sin(ang)
        c_ref[...] = jnp.where(lane < C_ROT_DIM, cos, 1.0)
        a_ref[...] = jnp.where(lane < C_ROT_HALF, -sin, 0.0)
        b_ref[...] = jnp.where(jnp.logical_and(lane >= C_ROT_HALF, lane < C_ROT_DIM), sin, 0.0)

    tab = pl.BlockSpec((ROPE_ROWS, C_PAIR), lambda i: (i, 0))
    return pl.pallas_call(
        body, name=name, grid=(t // ROPE_ROWS,),
        in_specs=[pl.BlockSpec((ROPE_ROWS, 1), lambda i: (i, 0))],
        out_specs=[tab, tab, tab],
        out_shape=[jax.ShapeDtypeStruct((t, C_PAIR), F32)] * 3,
        compiler_params=_params(("arbitrary",)),
    )(pos_col)


def _rope(x, c, a, b):
    return x * c + pltpu.roll(x, C_PAIR - C_ROT_HALF, 1) * a + pltpu.roll(x, C_ROT_HALF, 1) * b


def _rope_t(d, c, a, b):
    return d * c + pltpu.roll(d * a, C_ROT_HALF, 1) + pltpu.roll(d * b, C_PAIR - C_ROT_HALF, 1)


C_RES = 16


def _residue_major(a, batch):
    return a.reshape(batch, SEQ // C_RES, C_RES, -1).transpose(0, 2, 1, 3).reshape(a.shape)


def _sequence_order(a, batch):
    return a.reshape(batch, C_RES, SEQ // C_RES, -1).transpose(0, 2, 1, 3).reshape(a.shape)


def _block_pieces(idx, dil):
    nblk = SEQ // dil // C_BLOCK
    r, n = idx // nblk, idx % nblk
    per = C_RES // dil
    size = C_BLOCK // per

    def pieces(blk):
        return [((dil * a + r) * (SEQ // C_RES) + size * blk, size) for a in range(per)]

    return pieces(n), pieces(jnp.maximum(n - 1, 0)), n > 0


def _get_rows(ref, pieces):
    return jnp.concatenate([ref[pl.ds(pl.multiple_of(start, 8), size), :] for start, size in pieces], axis=0)


def _set_rows(ref, pieces, val, add=False):
    for k, (start, size) in enumerate(pieces):
        rows = pl.ds(pl.multiple_of(start, 8), size)
        part = val[k * size:(k + 1) * size]
        ref[rows, :] = ref[rows, :] + part if add else part


def _head_masks():
    low = lax.broadcasted_iota(jnp.int32, (1, C_PAIR), 1) < C_HEAD_DIM
    return low, jnp.logical_not(low)


def _attn_mask(has_prev, dil):
    per = C_RES // dil
    size = C_BLOCK // per

    def position(x):
        x = jnp.bitwise_and(x, C_BLOCK - 1)
        return per * jnp.bitwise_and(x, size - 1) + x // size

    j = lax.broadcasted_iota(jnp.int32, (2 * C_BLOCK, 2 * C_BLOCK), 1)
    pi = position(lax.broadcasted_iota(jnp.int32, (2 * C_BLOCK, 2 * C_BLOCK), 0))
    pj = position(j)
    own = j < C_BLOCK
    return jnp.logical_or(jnp.logical_and(own, pj <= pi),
                          jnp.logical_and(jnp.logical_and(jnp.logical_not(own), pj >= pi), has_prev))


def _stack_heads(x):
    low, high = _head_masks()
    return jnp.concatenate([jnp.where(low, x, 0.0), jnp.where(high, x, 0.0)], axis=0)


def _unstack_heads(x):
    low, _ = _head_masks()
    return jnp.where(low, x[:C_BLOCK], x[C_BLOCK:])


def attn_fwd(qkv, cos_t, sin_a, sin_b, batch, name, exchange=None):
    t = qkv.shape[0]
    nbr = len(C_DILATIONS)

    def body(q_ref, k_ref, v_ref, c_ref, a_ref, b_ref, o_ref, l_ref, qs, ks, *stats):
        acc, mm, dd = stats[0:nbr], stats[nbr:2 * nbr], stats[2 * nbr:3 * nbr]
        c, a, b = c_ref[...], a_ref[...], b_ref[...]
        qs[...] = _rope(q_ref[...], c, a, b) * C_SCALE
        ks[...] = _rope(k_ref[...], c, a, b)

        def load(idx, dil):
            own, prev, has_prev = _block_pieces(idx, dil)
            return own, (has_prev, _get_rows(qs, own), _get_rows(ks, own), _get_rows(ks, prev),
                         _get_rows(v_ref, own), _get_rows(v_ref, prev))

        def scores(dil, has_prev, q, k_own, k_prev, v_own, v_prev):
            k_cat = jnp.concatenate([k_own, k_prev], axis=0).astype(BF16)
            return jnp.where(_attn_mask(has_prev, dil), _dot_nt(_stack_heads(q).astype(BF16), k_cat), NEG_BIG)

        def softmax(s):
            m = jnp.max(s, axis=-1, keepdims=True)
            p = jnp.exp(s - m)
            return p.astype(BF16), m, jnp.sum(p, axis=-1, keepdims=True)

        def values(pb, has_prev, q, k_own, k_prev, v_own, v_prev):
            low, high = _head_masks()
            v_cat = jnp.concatenate([v_own, v_prev], axis=0)
            p_wide = jnp.concatenate([pb[:C_BLOCK], pb[C_BLOCK:]], axis=1)
            v_tall = jnp.concatenate([jnp.where(low, v_cat, 0.0), jnp.where(high, v_cat, 0.0)], axis=0).astype(BF16)
            return _dot(p_wide, v_tall)

        for bi, dil in enumerate(C_DILATIONS):
            def pair(i, carry, bi=bi, dil=dil):
                low, _ = _head_masks()
                loaded = [load(C_FWD_BLOCKS * i + k, dil) for k in range(C_FWD_BLOCKS)]
                ss = [scores(dil, *ops) for _, ops in loaded]
                sm = [softmax(s) for s in ss]
                pvs = [values(pb, *ops) for (pb, _, _), (_, ops) in zip(sm, loaded)]
                for (own, _), (_, m, den), pv in zip(loaded, sm, pvs):
                    _set_rows(acc[bi], own, pv)
                    _set_rows(mm[bi], own, jnp.where(low, m[:C_BLOCK], m[C_BLOCK:]))
                    _set_rows(dd[bi], own, jnp.where(low, den[:C_BLOCK], den[C_BLOCK:]))
                return carry

            lax.fori_loop(0, SEQ // C_BLOCK // C_FWD_BLOCKS, pair, 0)
        step = 2 * C_BLOCK
        for r0 in range(0, SEQ, step):
            rr = slice(r0, r0 + step)
            ms = [mm[g][rr, :] for g in range(nbr)]
            m_all = functools.reduce(jnp.maximum, ms)
            ws = [jnp.exp(m - m_all) for m in ms]
            num = sum(acc[g][rr, :] * ws[g] for g in range(nbr))
            den = sum(dd[g][rr, :] * ws[g] for g in range(nbr))
            o_ref[rr, :] = (num / den).astype(BF16)
            l_ref[rr, :] = m_all + jnp.log(den)

    def col(k):
        return pl.BlockSpec((SEQ, C_PAIR), lambda b, p: (b, k * C_PAIRS + p))

    tab = pl.BlockSpec((SEQ, C_PAIR), lambda b, p: (b, 0))
    return _call(
        body, name=name, grid=(batch, C_PAIRS),
        in_specs=[col(0), col(1), col(2), tab, tab, tab],
        out_specs=[col(0), col(0)],
        out_shape=[jax.ShapeDtypeStruct((t, D_MODEL), BF16), jax.ShapeDtypeStruct((t, D_MODEL), F32)],
        scratch_shapes=[pltpu.VMEM((SEQ, C_PAIR), F32)] * (2 + 3 * nbr),
        args=(qkv, qkv, qkv, cos_t, sin_a, sin_b), exchange=exchange)


def attn_bwd(qkv, cos_t, sin_a, sin_b, o, lse, do, batch, name, exchange=None):
    t = qkv.shape[0]

    def body(q_ref, k_ref, v_ref, c_ref, a_ref, b_ref, o_ref, l_ref, do_ref, dqkv_ref, qs, ks, dqs, dks, dvs, dlt):
        low, _ = _head_masks()
        c, a, b = c_ref[...], a_ref[...], b_ref[...]
        qs[...] = _rope(q_ref[...], c, a, b) * C_SCALE
        ks[...] = _rope(k_ref[...], c, a, b)
        prod = do_ref[...] * o_ref[...].astype(F32)
        s_low = jnp.sum(jnp.where(low, prod, 0.0), axis=-1, keepdims=True)
        s_all = jnp.sum(prod, axis=-1, keepdims=True)
        dlt[...] = jnp.where(low, s_low, s_all - s_low)
        dqs[...] = jnp.zeros_like(dqs)
        dks[...] = jnp.zeros_like(dks)
        dvs[...] = jnp.zeros_like(dvs)

        def load(idx, dil):
            own, prev, has_prev = _block_pieces(idx, dil)
            return (own, prev), (has_prev, _get_rows(qs, own), _get_rows(do_ref, own), _get_rows(ks, own),
                                 _get_rows(ks, prev), _get_rows(v_ref, own), _get_rows(v_ref, prev),
                                 _get_rows(l_ref, own), _get_rows(dlt, own))

        def operands(dil, has_prev, q, do, k_own, k_prev, v_own, v_prev, l_full, d_full):
            lcol = jnp.concatenate([l_full[:, 0:1], l_full[:, C_HEAD_DIM:C_HEAD_DIM + 1]], axis=0)
            dcol = jnp.concatenate([d_full[:, 0:1], d_full[:, C_HEAD_DIM:C_HEAD_DIM + 1]], axis=0)
            return (_stack_heads(q).astype(BF16), _stack_heads(do).astype(BF16),
                    jnp.concatenate([k_own, k_prev], axis=0).astype(BF16),
                    jnp.concatenate([v_own, v_prev], axis=0).astype(BF16), lcol, dcol, _attn_mask(has_prev, dil))

        for dil in C_DILATIONS:
            def pair(i, carry, dil=dil):
                loaded = [load(C_BWD_BLOCKS * i + k, dil) for k in range(C_BWD_BLOCKS)]
                ops = [operands(dil, *o) for _, o in loaded]
                ss = [_dot_nt(q_stack, k_cat) for q_stack, _, k_cat, _, _, _, _ in ops]
                dps = [_dot_nt(do_stack, v_cat) for _, do_stack, _, v_cat, _, _, _ in ops]
                ps = [jnp.exp(jnp.where(o[6], s, NEG_BIG) - o[4]) for s, o in zip(ss, ops)]
                dss = [(p * (dp - o[5])).astype(BF16) for p, dp, o in zip(ps, dps, ops)]
                dvs_ = [_dot_tn(p.astype(BF16), o[1]) for p, o in zip(ps, ops)]
                dks_ = [_dot_tn(ds, o[0]) for ds, o in zip(dss, ops)]
                dqs_ = [_unstack_heads(_dot(ds, o[2])) for ds, o in zip(dss, ops)]
                for ((own, prev), _), dq, dk_cat, dv_cat in zip(loaded, dqs_, dks_, dvs_):
                    _set_rows(dqs, own, dq, add=True)
                    _set_rows(dks, own, dk_cat[:C_BLOCK], add=True)
                    _set_rows(dvs, own, dv_cat[:C_BLOCK], add=True)
                    _set_rows(dks, prev, dk_cat[C_BLOCK:], add=True)
                    _set_rows(dvs, prev, dv_cat[C_BLOCK:], add=True)
                return carry

            lax.fori_loop(0, SEQ // C_BLOCK // C_BWD_BLOCKS, pair, 0)
        dqkv_ref[0] = _rope_t(dqs[...] * C_SCALE, c, a, b).astype(BF16)
        dqkv_ref[1] = _rope_t(dks[...], c, a, b).astype(BF16)
        dqkv_ref[2] = dvs[...].astype(BF16)

    def col(k):
        return pl.BlockSpec((SEQ, C_PAIR), lambda b, p: (b, k * C_PAIRS + p))

    tab = pl.BlockSpec((SEQ, C_PAIR), lambda b, p: (b, 0))
    return _call(
        body, name=name, grid=(batch, C_PAIRS),
        in_specs=[col(0), col(1), col(2), tab, tab, tab, col(0), col(0), col(0)],
        out_specs=[pl.BlockSpec((3, SEQ, C_PAIR), lambda b, p: (0, b, p))],
        out_shape=[jax.ShapeDtypeStruct((3, t, D_MODEL), BF16)],
        scratch_shapes=[pltpu.VMEM((SEQ, C_PAIR), F32)] * 6,
        args=(qkv, qkv, qkv, cos_t, sin_a, sin_b, o, lse, do), exchange=exchange)


def sibling_swap(arrays, name):
    n = len(arrays)

    def body(*refs):
        ins, outs = refs[:n], refs[n:2 * n]
        send_sems, recv_sems = refs[2 * n:]
        x, y, c, _ = _place()
        sends = []
        for a in range(n):
            cp = pltpu.make_async_remote_copy(
                src_ref=ins[a], dst_ref=outs[a], send_sem=send_sems.at[a], recv_sem=recv_sems.at[a],
                device_id=(x, y, 1 - c), device_id_type=MESH)
            cp.start()
            sends.append(cp)
        for cp in sends:
            cp.wait_recv()
        for cp in sends:
            cp.wait_send()

    return pl.pallas_call(
        body, name=name,
        in_specs=[ANY] * n, out_specs=[ANY] * n,
        out_shape=[jax.ShapeDtypeStruct(s.shape, s.dtype) for s in arrays],
        scratch_shapes=[pltpu.SemaphoreType.DMA((n,)), pltpu.SemaphoreType.DMA((n,))],
    )(*arrays)


def allreduce_small(slab, name):
    rows, lanes = slab.shape

    def body(x_ref, out_ref, gath, send_sems, recv_sems, local_sem):
        x, y, c, chips = _place()
        me, sibling = (x, y, c), (x, y, 1 - c)

        def slot(px, py, pc):
            return gath.at[4 * px + 2 * py + pc]

        def copy(k, block, to, src=None):
            return pltpu.make_async_remote_copy(
                src_ref=slot(*block) if src is None else src, dst_ref=slot(*block),
                send_sem=send_sems.at[k], recv_sem=recv_sems.at[k], device_id=to, device_id_type=MESH)

        mine = pltpu.make_async_copy(x_ref, slot(*me), local_sem)
        mine.start()
        first = [copy(0, me, sibling, src=x_ref)]
        first += [copy(1 + j, me, (*chip, c), src=x_ref) for j, chip in enumerate(chips)]
        for cp in first:
            cp.start()
        passed = [copy(4 + j, (*chip, c), sibling) for j, chip in enumerate(chips)]
        for j, chip in enumerate(chips):
            copy(1 + j, (*chip, c), me).wait_recv()
            passed[j].start()
        copy(0, sibling, me).wait_recv()
        for j, chip in enumerate(chips):
            copy(4 + j, (*chip, 1 - c), me).wait_recv()
        for cp in first + passed:
            cp.wait_send()
        mine.wait()
        total = gath[0]
        for d in range(1, N_DEV):
            total = total + gath[d]
        out_ref[...] = total

    return pl.pallas_call(
        body, name=name,
        in_specs=[pl.BlockSpec(memory_space=pltpu.VMEM)],
        out_specs=pl.BlockSpec(memory_space=pltpu.VMEM),
        out_shape=jax.ShapeDtypeStruct((rows, lanes), F32),
        scratch_shapes=[pltpu.VMEM((N_DEV, rows, lanes), F32),
                        pltpu.SemaphoreType.DMA((7,)), pltpu.SemaphoreType.DMA((7,)), pltpu.SemaphoreType.DMA],
    )(slab)


ELT_ROWS = 512


def reduce_slabs(r, name, part=0, parts=1, into=None):
    _, rows, cols = r.shape
    br = min(rows, ELT_ROWS)
    nblk = rows // br

    def body(r_ref, *rest):
        o_ref = rest[-1]
        o_ref[...] = ((r_ref[3].astype(F32) + r_ref[0].astype(F32)) + r_ref[1].astype(F32)) + r_ref[2].astype(F32)

    return pl.pallas_call(
        body, name=name, grid=(nblk,),
        in_specs=[pl.BlockSpec((N_CHIPS, br, cols), lambda i: (0, i, 0))] + ([] if into is None else [ANY]),
        out_specs=pl.BlockSpec((br, cols), lambda i: (part * nblk + i, 0)),
        out_shape=jax.ShapeDtypeStruct((parts * rows, cols), F32),
        input_output_aliases={} if into is None else {1: 0},
        compiler_params=_params(("arbitrary",)),
    )(*([r] if into is None else [r, into]))


def _adamw(w, g, m, v):
    m = ADAM_B1 * m + (1.0 - ADAM_B1) * g
    v = ADAM_B2 * v + (1.0 - ADAM_B2) * jnp.square(g)
    m_hat = m / (1.0 - ADAM_B1 ** ADAM_STEP)
    v_hat = v / (1.0 - ADAM_B2 ** ADAM_STEP)
    delta = -ADAM_LR * (m_hat / (jnp.sqrt(v_hat) + ADAM_EPS) + ADAM_WD * w)
    return delta, m, v


def adamw_big(w, s_mine, s_sibling, m, v, name):
    rows, cols = w.shape

    def body(w_ref, a_ref, b_ref, m_ref, v_ref, g_out, d_out, m_out, v_out):
        g = a_ref[...] + b_ref[...]
        g_out[...] = g
        d_out[...], m_out[...], v_out[...] = _adamw(w_ref[...], g, m_ref[...], v_ref[...])

    blk = pl.BlockSpec((min(rows, ELT_ROWS), cols), lambda i: (i, 0))
    out = jax.ShapeDtypeStruct((rows, cols), F32)
    return pl.pallas_call(
        body, name=name, grid=(rows // min(rows, ELT_ROWS),),
        in_specs=[blk] * 5, out_specs=[blk] * 4, out_shape=[out] * 4,
        compiler_params=_params(("arbitrary",)),
    )(w, s_mine, s_sibling, m, v)


def adamw_small(ws, gs, ms, vs, name):
    n = len(ws)

    def body(*refs):
        w_refs, g_refs, m_refs, v_refs = (refs[k * n:(k + 1) * n] for k in range(4))
        d_out, m_out, v_out = (refs[(4 + k) * n:(5 + k) * n] for k in range(3))
        for i in range(n):
            d_out[i][...], m_out[i][...], v_out[i][...] = _adamw(
                w_refs[i][...], g_refs[i][...], m_refs[i][...], v_refs[i][...])

    outs = [jax.ShapeDtypeStruct(w.shape, F32) for w in ws]
    res = pl.pallas_call(body, name=name, out_shape=outs * 3)(*ws, *gs, *ms, *vs)
    return res[:n], res[n:2 * n], res[2 * n:]


SLAB_LANES = 128
SLAB_ROW_ALIGN = 8


def _pack(parts):
    flat = jnp.concatenate([p.reshape(-1) for p in parts])
    rows = -(-flat.shape[0] // (SLAB_LANES * SLAB_ROW_ALIGN)) * SLAB_ROW_ALIGN
    flat = jnp.pad(flat, (0, rows * SLAB_LANES - flat.shape[0]))
    return flat.reshape(rows, SLAB_LANES)


def _unpack(slab, shapes):
    flat = slab.reshape(-1)
    out, pos = [], 0
    for s in shapes:
        size = math.prod(s)
        out.append(flat[pos:pos + size].reshape(s))
        pos += size
    return out


def kernel(x, positions, norm_mix_pre, norm_mix_post, norm_ffn_pre, norm_ffn_post, w_in_even, lb_table, a_norm, b_ln_g, b_ln_b, b_ws, b_bias, w_out_even, w_in_odd, w_out_odd, w_ff1, w_ff2, loss_target, m_norm_mix_pre, m_norm_mix_post, m_norm_ffn_pre, m_norm_ffn_post, m_w_in_even, m_lb_table, m_a_norm, m_b_ln_g, m_b_ln_b, m_b_ws, m_b_bias, m_w_out_even, m_w_in_odd, m_w_out_odd, m_w_ff1, m_w_ff2, v_norm_mix_pre, v_norm_mix_post, v_norm_ffn_pre, v_norm_ffn_post, v_w_in_even, v_lb_table, v_a_norm, v_b_ln_g, v_b_ln_b, v_b_ws, v_b_bias, v_w_out_even, v_w_in_odd, v_w_out_odd, v_w_ff1, v_w_ff2):
    batch = x.shape[0]
    t = batch * SEQ
    d = D_MODEL
    x0 = x.reshape(t, d)
    target = loss_target.reshape(t, d)

    def gain(p, layer):
        return p[layer:layer + 1]

    def gather(*shards):
        return _Exchange("gather", [w.astype(BF16) for w in shards])

    def scatter(*grads):
        return _Exchange("scatter", grads)

    (win_e,) = exchange_alone(gather(w_in_even[0]), "gather_in_even")
    bias_t = b_bias[0].T
    proj, h0, w1_0 = norm_matmul(x0, gain(norm_mix_pre, 0), win_e, "in_proj_even", exchange=gather(w_ff1[0]))
    oa, states, decays, w2_0 = hgrn2_fwd(proj, lb_table, a_norm, batch, "hgrn2_fwd", exchange=gather(w_ff2[0]))
    mixin, wout_e = gmlp_fwd(proj, oa, b_ln_g, b_ln_b, b_ws[0], bias_t, "gmlp_fwd", exchange=gather(w_out_even[0]))
    mix0, x1 = out_proj(mixin, wout_e, x0, gain(norm_mix_post, 0), "out_proj_even")
    x2, hf0, a0, y0, win_o, wout_o = ffn_fwd(x1, gain(norm_ffn_pre, 0), w1_0, w2_0, gain(norm_ffn_post, 0),
                                             "ffn_fwd_0", exchange=gather(w_in_odd[0], w_out_odd[0]))
    x2p = _residue_major(x2, batch)
    qkv, h1 = norm_matmul(x2p, gain(norm_mix_pre, 1), win_o, "in_proj_odd")
    cos_t, sin_a, sin_b = rope_tables(_residue_major(positions.reshape(t, 1), batch), "rope_tables")
    ao, lse, w1_1, w2_1 = attn_fwd(qkv, cos_t, sin_a, sin_b, batch, "attn_fwd", exchange=gather(w_ff1[1], w_ff2[1]))
    mix1, x3 = out_proj(ao, wout_o, x2p, gain(norm_mix_post, 1), "out_proj_odd")
    dx4, hf1, a1, y1, loss_part = ffn_fwd(x3, gain(norm_ffn_pre, 1), w1_1, w2_1, gain(norm_ffn_post, 1),
                                          "ffn_fwd_1", target=_residue_major(target, batch))

    hc = D_FF // N_CHIPS
    dx3, dy1, da1, dg_fpre1, dg_fpost1 = ffn_bwd(
        dx4, x3, y1, a1, gain(norm_ffn_pre, 1), gain(norm_ffn_post, 1), w1_1, w2_1, "ffn_bwd_1")
    g_w1_1 = weight_grad(hf1, da1, "b", d, hc, False, "wgrad_ff1_1")
    g_w2_1 = weight_grad(a1, dy1, "a", hc, d, True, "wgrad_ff2_1")
    dmix1, dao, dg_mpost1 = out_proj_bwd(dx3, mix1, gain(norm_mix_post, 1), wout_o, "out_proj_bwd_odd")
    g_wout_o = weight_grad(ao, dmix1, "a", d // N_CHIPS, d, False, "wgrad_out_odd")
    dqkv, r_w1_1, r_w2_1, r_wout_o = attn_bwd(qkv, cos_t, sin_a, sin_b, ao, lse, dao, batch, "attn_bwd",
                                              exchange=scatter(g_w1_1, g_w2_1, g_wout_o))
    dx2p, dg_mpre1 = norm_matmul_bwd(dqkv, win_o, x2p, gain(norm_mix_pre, 1), dx3, "in_proj_bwd_odd")
    dx2 = _sequence_order(dx2p, batch)
    g_win_o = weight_grad_stacked(h1, dqkv, 3 * d // N_CHIPS, "wgrad_in_odd")
    dx1, dy0, da0, dg_fpre0, dg_fpost0, r_win_o = ffn_bwd(
        dx2, x1, y0, a0, gain(norm_ffn_pre, 0), gain(norm_ffn_post, 0), w1_0, w2_0, "ffn_bwd_0",
        exchange=scatter(g_win_o))
    g_w1_0 = weight_grad(hf0, da0, "b", d, hc, False, "wgrad_ff1_0")
    g_w2_0 = weight_grad(a0, dy0, "a", hc, d, True, "wgrad_ff2_0")
    dmix0, dmixin, dg_mpost0 = out_proj_bwd(dx1, mix0, gain(norm_mix_post, 0), wout_e, "out_proj_bwd_even")
    g_wout_e = weight_grad(mixin, dmix0, "a", d // N_CHIPS, d, False, "wgrad_out_even")
    dproj, d_lb, d_anorm, r_w1_0 = hgrn2_bwd(
        proj, states, decays, lb_table, a_norm, dmixin, batch, "hgrn2_bwd", exchange=scatter(g_w1_0))
    dproj, d_lng, d_lnb, d_ws, d_bias_t, r_w2_0 = gmlp_bwd(
        proj, dmixin, b_ln_g, b_ln_b, b_ws[0], bias_t, dproj, "gmlp_bwd", exchange=scatter(g_w2_0))
    g_win_e, r_wout_e = weight_grad(h0, dproj, "b", d, 3 * d // N_CHIPS, False, "wgrad_in_even",
                                    exchange=scatter(g_wout_e))
    dx0, dg_mpre0, r_win_e = norm_matmul_bwd(dproj, win_e, x0, gain(norm_mix_pre, 0), dx1, "in_proj_bwd_even",
                                             exchange=scatter(g_win_e))
    grad_x = dx0.reshape(x.shape)

    s_w1 = reduce_slabs(r_w1_1, "reduce_ff1_1", part=1, parts=2)
    s_w1 = reduce_slabs(r_w1_0, "reduce_ff1_0", part=0, parts=2, into=s_w1)
    s_w2 = reduce_slabs(r_w2_1, "reduce_ff2_1", part=1, parts=2)
    s_w2 = reduce_slabs(r_w2_0, "reduce_ff2_0", part=0, parts=2, into=s_w2)
    sums = [reduce_slabs(r_win_e, "reduce_in_even"), reduce_slabs(r_wout_e, "reduce_out_even"),
            reduce_slabs(r_win_o, "reduce_in_odd"), reduce_slabs(r_wout_o, "reduce_out_odd"), s_w1, s_w2]
    sibling = sibling_swap(sums, "sibling_swap")
    big_w = [w_in_even, w_out_even, w_in_odd, w_out_odd, w_ff1, w_ff2]
    big_m = [m_w_in_even, m_w_out_even, m_w_in_odd, m_w_out_odd, m_w_ff1, m_w_ff2]
    big_v = [v_w_in_even, v_w_out_even, v_w_in_odd, v_w_out_odd, v_w_ff1, v_w_ff2]
    big = []
    for i, (w, m, v) in enumerate(zip(big_w, big_m, big_v)):
        two_d = (-1, w.shape[-1])
        res = adamw_big(w.reshape(two_d), sums[i], sibling[i], m.reshape(two_d), v.reshape(two_d), "adamw_big_%d" % i)
        big.append([r.reshape(w.shape) for r in res])

    small_w = [norm_mix_pre, norm_mix_post, norm_ffn_pre, norm_ffn_post, lb_table, a_norm, b_ln_g, b_ln_b, b_ws, b_bias]
    small_m = [m_norm_mix_pre, m_norm_mix_post, m_norm_ffn_pre, m_norm_ffn_post, m_lb_table, m_a_norm, m_b_ln_g,
               m_b_ln_b, m_b_ws, m_b_bias]
    small_v = [v_norm_mix_pre, v_norm_mix_post, v_norm_ffn_pre, v_norm_ffn_post, v_lb_table, v_a_norm, v_b_ln_g,
               v_b_ln_b, v_b_ws, v_b_bias]
    partial = [jnp.concatenate([dg_mpre0, dg_mpre1]), jnp.concatenate([dg_mpost0, dg_mpost1]),
               jnp.concatenate([dg_fpre0, dg_fpre1]), jnp.concatenate([dg_fpost0, dg_fpost1]),
               d_lb, d_anorm, d_lng, d_lnb, d_ws[None], d_bias_t.T[None]]
    *small_g, loss = _unpack(allreduce_small(_pack(partial + [loss_part]), "allreduce_small"),
                             [w.shape for w in small_w] + [()])
    small_d, small_nm, small_nv = adamw_small(small_w, small_g, small_m, small_v, "adamw_small")

    order = ["norm_mix_pre", "norm_mix_post", "norm_ffn_pre", "norm_ffn_post", "w_in_even", "lb_table", "a_norm",
             "b_ln_g", "b_ln_b", "b_ws", "b_bias", "w_out_even", "w_in_odd", "w_out_odd", "w_ff1", "w_ff2"]
    small_names = ["norm_mix_pre", "norm_mix_post", "norm_ffn_pre", "norm_ffn_post", "lb_table", "a_norm",
                   "b_ln_g", "b_ln_b", "b_ws", "b_bias"]
    big_names = ["w_in_even", "w_out_even", "w_in_odd", "w_out_odd", "w_ff1", "w_ff2"]
    grads, deltas, new_m, new_v = {}, {}, {}, {}
    for i, nm in enumerate(small_names):
        grads[nm], deltas[nm], new_m[nm], new_v[nm] = small_g[i], small_d[i], small_nm[i], small_nv[i]
    for i, nm in enumerate(big_names):
        grads[nm], deltas[nm], new_m[nm], new_v[nm] = big[i]
    return (loss, grad_x, *[grads[n] for n in order], *[deltas[n] for n in order],
            *[new_m[n] for n in order], *[new_v[n] for n in order])
```

```python
import functools
import math

import jax
import jax.numpy as jnp
from jax import lax
from jax.experimental import pallas as pl
from jax.experimental.pallas import tpu as pltpu

F32 = jnp.float32
BF16 = jnp.bfloat16
MESH = pl.DeviceIdType.MESH

D_MODEL = 1024
SEQ = 2048
D_FF = 4096
N_CHIPS = 4
A_WIDTH = 512
A_HEADS = 4
A_DK = 128
A_CHUNK = 64
A_SUB = 16
B_WIDTH = 512
B_GROUPS = 4
B_CHUNK = 128
C_HEADS = 16
C_HEAD_DIM = 64
C_ROT_HALF = 8
C_BLOCK = 128
C_DILATIONS = (1, 4, 16)
ROPE_THETA = 500000.0
EPS = 1e-6
ADAM_LR = 0.001
ADAM_B1 = 0.9
ADAM_B2 = 0.999
ADAM_EPS = 1e-08
ADAM_WD = 0.01
ADAM_STEP = 10

ROW_TILE = 512
FFN_ROWS = 1024
WGRAD_ROWS = 2048
VMEM_LIMIT = 56 * 1024 * 1024
NEG_BIG = -1e30


def _params(sem=None):
    return pltpu.CompilerParams(dimension_semantics=sem, vmem_limit_bytes=VMEM_LIMIT)


def _dot(a, b):
    return jnp.dot(a, b, preferred_element_type=F32)


def _dot_nt(a, b):
    return lax.dot_general(a, b, (((1,), (1,)), ((), ())), preferred_element_type=F32)


def _dot_tn(a, b):
    return lax.dot_general(a, b, (((0,), (0,)), ((), ())), preferred_element_type=F32)


def _rms(x, g):
    r = lax.rsqrt(jnp.mean(x * x, axis=-1, keepdims=True) + EPS)
    return x * r * g


def _rms_bwd(x, g, dy):
    r = lax.rsqrt(jnp.mean(x * x, axis=-1, keepdims=True) + EPS)
    xh = x * r
    dg = jnp.sum(dy * xh, axis=0, keepdims=True)
    dxh = dy * g
    dx = r * (dxh - xh * jnp.mean(dxh * xh, axis=-1, keepdims=True))
    return dx, dg


def _accumulate(ref, val, first):
    @pl.when(first)
    def _():
        ref[...] = val

    @pl.when(jnp.logical_not(first))
    def _():
        ref[...] += val


N_DEV = 8
ANY = pl.BlockSpec(memory_space=pl.ANY)


def _place():
    x, y, c = lax.axis_index("x"), lax.axis_index("y"), lax.axis_index("c")
    return x, y, c, [(1 - x, y), (x, 1 - y), (1 - x, 1 - y)]


class _Exchange:
    def __init__(self, kind, arrays):
        self.kind, self.arrays, self.n = kind, list(arrays), len(arrays)
        per_peer = pltpu.SemaphoreType.DMA((3 * self.n,))
        if kind == "gather":
            self.out_shape = [jax.ShapeDtypeStruct((N_CHIPS,) + a.shape, a.dtype) for a in self.arrays]
            self.scratch = [per_peer, per_peer, pltpu.SemaphoreType.DMA((self.n,)), per_peer, per_peer]
        else:
            self.out_shape = [jax.ShapeDtypeStruct(a.shape, a.dtype) for a in self.arrays]
            self.scratch = [per_peer, per_peer, pltpu.SemaphoreType.DMA((self.n,))]

    def _copies(self, ins, outs, sems):
        send_sems, recv_sems, local_sems = sems[:3]
        x, y, c, chips = _place()
        me = 2 * x + y
        local, remote = [], []
        for a in range(self.n):
            if self.kind == "gather":
                local.append(pltpu.make_async_copy(ins[a], outs[a].at[me], local_sems.at[a]))
                half = self.arrays[a].shape[0] // 2

                def rows(ref, core, half=half):
                    return ref.at[pl.ds(core * half, half)]
            else:
                local.append(pltpu.make_async_copy(ins[a].at[me], outs[a].at[3], local_sems.at[a]))
            for j, (px, py) in enumerate(chips):
                k = 3 * a + j
                peer = 2 * px + py

                def copy(src, dst, to, send_sem=send_sems.at[k], recv_sem=recv_sems.at[k]):
                    return pltpu.make_async_remote_copy(src_ref=src, dst_ref=dst, send_sem=send_sem, recv_sem=recv_sem,
                                                        device_id=to, device_id_type=MESH)

                if self.kind == "gather":
                    sent = copy(rows(ins[a], c), rows(outs[a].at[me], c), (px, py, c))
                    landed = copy(rows(ins[a], c), rows(outs[a].at[peer], c), (px, py, c))
                    on = dict(send_sem=sems[3].at[k], recv_sem=sems[4].at[k])
                    passed = copy(rows(outs[a].at[peer], c), rows(outs[a].at[peer], c), (x, y, 1 - c), **on)
                    handed = copy(rows(outs[a].at[peer], c), rows(outs[a].at[peer], 1 - c), (x, y, 1 - c), **on)
                    remote.append((sent, landed, passed, handed))
                else:
                    sent = copy(ins[a].at[peer], outs[a].at[j], (px, py, c))
                    remote.append((sent, sent, None, None))
        return local, remote

    def start(self, ins, outs, sems):
        local, remote = self._copies(ins, outs, sems)
        for cp in local:
            cp.start()
        for sent, _, _, _ in remote:
            sent.start()

    def finish(self, ins, outs, sems):
        local, remote = self._copies(ins, outs, sems)
        for _, landed, passed, _ in remote:
            landed.wait_recv()
            if passed is not None:
                passed.start()
        for sent, _, passed, handed in remote:
            if passed is not None:
                handed.wait_recv()
                passed.wait_send()
            sent.wait_send()
        for cp in local:
            cp.wait()


def _call(body, *, name, grid, in_specs, out_specs, out_shape, args, scratch_shapes=(), aliases=None, exchange=None):
    if exchange is None:
        return pl.pallas_call(
            body, name=name, grid=grid, in_specs=in_specs, out_specs=out_specs, out_shape=out_shape,
            scratch_shapes=list(scratch_shapes), input_output_aliases=aliases or {},
            compiler_params=_params(("arbitrary",) * len(grid)))(*args)
    n_in, n_out, n_scr, n_ex = len(in_specs), len(out_specs), len(scratch_shapes), exchange.n
    steps = grid

    def wrapped(*refs):
        ins, refs = refs[:n_in], refs[n_in:]
        ex_in, refs = refs[:n_ex], refs[n_ex:]
        outs, refs = refs[:n_out], refs[n_out:]
        ex_out, refs = refs[:n_ex], refs[n_ex:]
        scr, sems = refs[:n_scr], refs[n_scr:]
        first = functools.reduce(jnp.logical_and, [pl.program_id(k) == 0 for k in range(len(steps))])
        last = functools.reduce(jnp.logical_and, [pl.program_id(k) == steps[k] - 1 for k in range(len(steps))])

        @pl.when(first)
        def _():
            exchange.start(ex_in, ex_out, sems)

        body(*ins, *outs, *scr)

        @pl.when(last)
        def _():
            exchange.finish(ex_in, ex_out, sems)

    return pl.pallas_call(
        wrapped, name=name, grid=grid,
        in_specs=list(in_specs) + [ANY] * n_ex, out_specs=list(out_specs) + [ANY] * n_ex,
        out_shape=list(out_shape) + exchange.out_shape,
        scratch_shapes=list(scratch_shapes) + exchange.scratch, input_output_aliases=aliases or {},
        compiler_params=_params(("arbitrary",) * len(grid)))(*args, *exchange.arrays)


def exchange_alone(exchange, name):
    def body(*refs):
        n = exchange.n
        exchange.start(refs[:n], refs[n:2 * n], refs[2 * n:])
        exchange.finish(refs[:n], refs[n:2 * n], refs[2 * n:])

    return pl.pallas_call(
        body, name=name, in_specs=[ANY] * exchange.n, out_specs=[ANY] * exchange.n,
        out_shape=exchange.out_shape, scratch_shapes=exchange.scratch)(*exchange.arrays)


def norm_matmul(x, g, wg, name, exchange=None):
    t, d = x.shape
    nl = wg.shape[2]

    def body(x_ref, g_ref, w_ref, o_ref, h_ref):
        h = _rms(x_ref[...], g_ref[...]).astype(BF16)
        h_ref[...] = h
        for c in range(N_CHIPS):
            o_ref[:, c * nl:(c + 1) * nl] = _dot(h, w_ref[c])

    return _call(
        body, name=name, grid=(t // ROW_TILE,),
        in_specs=[pl.BlockSpec((ROW_TILE, d), lambda i: (i, 0)),
                  pl.BlockSpec((1, d), lambda i: (0, 0)),
                  pl.BlockSpec((N_CHIPS, d, nl), lambda i: (0, 0, 0))],
        out_specs=[pl.BlockSpec((ROW_TILE, N_CHIPS * nl), lambda i: (i, 0)),
                   pl.BlockSpec((ROW_TILE, d), lambda i: (i, 0))],
        out_shape=[jax.ShapeDtypeStruct((t, N_CHIPS * nl), F32), jax.ShapeDtypeStruct((t, d), BF16)],
        args=(x, g, wg), exchange=exchange)


def norm_matmul_bwd(dproj, wg, x, g, dres, name, exchange=None):
    t, d = x.shape
    nl = wg.shape[2]
    stacked = dproj.ndim == 3
    piece = math.gcd(nl, dproj.shape[-1])

    def body(dp_ref, w_ref, x_ref, g_ref, dres_ref, dx_ref, dg_ref):
        dh = None
        for j in range(N_CHIPS * nl // piece):
            c, off = divmod(j * piece, nl)
            if stacked:
                p, lo = divmod(j * piece, dproj.shape[-1])
                lhs = dp_ref[p, :, lo:lo + piece]
            else:
                lhs = dp_ref[:, j * piece:(j + 1) * piece]
            part = _dot_nt(lhs.astype(BF16), w_ref[c, :, off:off + piece])
            dh = part if dh is None else dh + part
        dx, dg = _rms_bwd(x_ref[...], g_ref[...], dh)
        dx_ref[...] = dres_ref[...] + dx
        _accumulate(dg_ref, dg, pl.program_id(0) == 0)

    row = pl.BlockSpec((ROW_TILE, d), lambda i: (i, 0))
    vec = pl.BlockSpec((1, d), lambda i: (0, 0))
    if stacked:
        dp_spec = pl.BlockSpec((dproj.shape[0], ROW_TILE, dproj.shape[-1]), lambda i: (0, i, 0))
    else:
        dp_spec = pl.BlockSpec((ROW_TILE, N_CHIPS * nl), lambda i: (i, 0))
    return _call(
        body, name=name, grid=(t // ROW_TILE,),
        in_specs=[dp_spec, pl.BlockSpec((N_CHIPS, d, nl), lambda i: (0, 0, 0)), row, vec, row],
        out_specs=[row, vec],
        out_shape=[jax.ShapeDtypeStruct((t, d), F32), jax.ShapeDtypeStruct((1, d), F32)],
        args=(dproj, wg, x, g, dres), exchange=exchange)


def out_proj(a, wg, x, g, name):
    t, d = x.shape
    kl = wg.shape[1]

    def body(a_ref, w_ref, x_ref, g_ref, mix_ref, xo_ref):
        acc = _dot(a_ref[:, 0:kl], w_ref[0])
        for c in range(1, N_CHIPS):
            acc += _dot(a_ref[:, c * kl:(c + 1) * kl], w_ref[c])
        mix_ref[...] = acc
        xo_ref[...] = x_ref[...] + _rms(acc, g_ref[...])

    row = pl.BlockSpec((ROW_TILE, d), lambda i: (i, 0))
    return pl.pallas_call(
        body, name=name, grid=(t // ROW_TILE,),
        in_specs=[row, pl.BlockSpec((N_CHIPS, kl, d), lambda i: (0, 0, 0)), row,
                  pl.BlockSpec((1, d), lambda i: (0, 0))],
        out_specs=[row, row],
        out_shape=[jax.ShapeDtypeStruct((t, d), F32), jax.ShapeDtypeStruct((t, d), F32)],
        compiler_params=_params(("arbitrary",)),
    )(a, wg, x, g)


def out_proj_bwd(dxo, mix, g, wg, name):
    t, d = mix.shape
    kl = wg.shape[1]

    def body(dxo_ref, mix_ref, g_ref, w_ref, dmix_ref, da_ref, dg_ref):
        dmix, dg = _rms_bwd(mix_ref[...], g_ref[...], dxo_ref[...])
        dmb = dmix.astype(BF16)
        dmix_ref[...] = dmb
        for c in range(N_CHIPS):
            da_ref[:, c * kl:(c + 1) * kl] = _dot_nt(dmb, w_ref[c])
        _accumulate(dg_ref, dg, pl.program_id(0) == 0)

    row = pl.BlockSpec((ROW_TILE, d), lambda i: (i, 0))
    vec = pl.BlockSpec((1, d), lambda i: (0, 0))
    return pl.pallas_call(
        body, name=name, grid=(t // ROW_TILE,),
        in_specs=[row, row, vec, pl.BlockSpec((N_CHIPS, kl, d), lambda i: (0, 0, 0))],
        out_specs=[row, row, vec],
        out_shape=[jax.ShapeDtypeStruct((t, d), BF16), jax.ShapeDtypeStruct((t, d), F32),
                   jax.ShapeDtypeStruct((1, d), F32)],
        compiler_params=_params(("arbitrary",)),
    )(dxo, mix, g, wg)


def ffn_fwd(x, gpre, w1g, w2g, gpost, name, exchange=None, target=None):
    t, d = x.shape
    hc = w1g.shape[2]
    with_loss = target is not None

    def body(x_ref, gpre_ref, w1_ref, w2_ref, gpost_ref, *rest):
        if with_loss:
            t_ref, xo_ref, h_ref, a_ref, y_ref, l_ref, acc = rest
        else:
            xo_ref, h_ref, a_ref, y_ref, acc = rest
        i, c = pl.program_id(0), pl.program_id(1)

        @pl.when(c == 0)
        def _():
            h_ref[...] = _rms(x_ref[...], gpre_ref[...]).astype(BF16)

        a = _dot(h_ref[...], w1_ref[...])
        a_ref[...] = a.astype(BF16)
        r = jnp.square(jnp.maximum(a, 0.0)).astype(BF16)
        _accumulate(acc, _dot(r, w2_ref[...]), c == 0)

        @pl.when(c == N_CHIPS - 1)
        def _():
            y = acc[...]
            y_ref[...] = y
            xo = x_ref[...] + _rms(y, gpost_ref[...])
            if with_loss:
                e = xo - t_ref[...]
                xo_ref[...] = e * (1.0 / d)
                part = jnp.sum(jnp.sum(e * e, axis=-1, keepdims=True), axis=0, keepdims=True) * (0.5 / d)
                _accumulate(l_ref, part, i == 0)
            else:
                xo_ref[...] = xo

    row = pl.BlockSpec((FFN_ROWS, d), lambda i, c: (i, 0))
    vec = pl.BlockSpec((1, d), lambda i, c: (0, 0))
    one = pl.BlockSpec((1, 1), lambda i, c: (0, 0))
    return _call(
        body, name=name, grid=(t // FFN_ROWS, N_CHIPS),
        in_specs=[row, vec,
                  pl.BlockSpec((None, d, hc), lambda i, c: (c, 0, 0)),
                  pl.BlockSpec((None, hc, d), lambda i, c: (c, 0, 0)), vec] + ([row] if with_loss else []),
        out_specs=[row, row, pl.BlockSpec((FFN_ROWS, hc), lambda i, c: (i, c)), row] + ([one] if with_loss else []),
        out_shape=[jax.ShapeDtypeStruct((t, d), F32), jax.ShapeDtypeStruct((t, d), BF16),
                   jax.ShapeDtypeStruct((t, N_CHIPS * hc), BF16), jax.ShapeDtypeStruct((t, d), F32)]
        + ([jax.ShapeDtypeStruct((1, 1), F32)] if with_loss else []),
        scratch_shapes=[pltpu.VMEM((FFN_ROWS, d), F32)],
        args=(x, gpre, w1g, w2g, gpost) + ((target,) if with_loss else ()), exchange=exchange)


def ffn_bwd(dxo, x, y, a, gpre, gpost, w1g, w2g, name, exchange=None):
    t, d = x.shape
    hc = w1g.shape[2]

    def body(dxo_ref, x_ref, y_ref, a_ref, gpre_ref, gpost_ref, w1_ref, w2_ref,
             dxi_ref, dy_ref, da_ref, dgpre_ref, dgpost_ref, acc):
        i, c = pl.program_id(0), pl.program_id(1)

        @pl.when(c == 0)
        def _():
            dy, dg = _rms_bwd(y_ref[...], gpost_ref[...], dxo_ref[...])
            dy_ref[...] = dy.astype(BF16)
            _accumulate(dgpost_ref, dg, i == 0)

        dr = _dot_nt(dy_ref[...], w2_ref[...])
        da = (dr * (2.0 * jnp.maximum(a_ref[...].astype(F32), 0.0))).astype(BF16)
        da_ref[...] = da
        _accumulate(acc, _dot_nt(da, w1_ref[...]), c == 0)

        @pl.when(c == N_CHIPS - 1)
        def _():
            dx, dg = _rms_bwd(x_ref[...], gpre_ref[...], acc[...])
            dxi_ref[...] = dxo_ref[...] + dx
            _accumulate(dgpre_ref, dg, i == 0)

    row = pl.BlockSpec((ROW_TILE, d), lambda i, c: (i, 0))
    vec = pl.BlockSpec((1, d), lambda i, c: (0, 0))
    hid = pl.BlockSpec((ROW_TILE, hc), lambda i, c: (i, c))
    return _call(
        body, name=name, grid=(t // ROW_TILE, N_CHIPS),
        in_specs=[row, row, row, hid, vec, vec,
                  pl.BlockSpec((None, d, hc), lambda i, c: (c, 0, 0)),
                  pl.BlockSpec((None, hc, d), lambda i, c: (c, 0, 0))],
        out_specs=[row, row, hid, vec, vec],
        out_shape=[jax.ShapeDtypeStruct((t, d), F32), jax.ShapeDtypeStruct((t, d), BF16),
                   jax.ShapeDtypeStruct((t, N_CHIPS * hc), BF16),
                   jax.ShapeDtypeStruct((1, d), F32), jax.ShapeDtypeStruct((1, d), F32)],
        scratch_shapes=[pltpu.VMEM((ROW_TILE, d), F32)],
        args=(dxo, x, y, a, gpre, gpost, w1g, w2g), exchange=exchange)


def weight_grad(a, b, chunked, bk, bn, relu2, name, exchange=None):
    t = a.shape[0]
    a_on = chunked == "a"
    rows = min(t, WGRAD_ROWS)
    n_steps = t // rows

    def body(a_ref, b_ref, o_ref, acc):
        s = pl.program_id(1)
        av = a_ref[...]
        if relu2:
            av = jnp.square(jnp.maximum(av.astype(F32), 0.0))
        _accumulate(acc, _dot_tn(av.astype(BF16), b_ref[...].astype(BF16)), s == 0)

        @pl.when(s == n_steps - 1)
        def _():
            o_ref[...] = acc[...].astype(BF16)

    res = _call(
        body, name=name, grid=(N_CHIPS, n_steps),
        in_specs=[pl.BlockSpec((rows, bk), (lambda c, s: (s, c)) if a_on else (lambda c, s: (s, 0))),
                  pl.BlockSpec((rows, bn), (lambda c, s: (s, 0)) if a_on else (lambda c, s: (s, c)))],
        out_specs=[pl.BlockSpec((None, bk, bn), lambda c, s: (c, 0, 0))],
        out_shape=[jax.ShapeDtypeStruct((N_CHIPS, bk, bn), BF16)],
        scratch_shapes=[pltpu.VMEM((bk, bn), F32)],
        args=(a, b), exchange=exchange)
    return res[0] if exchange is None else res


def weight_grad_stacked(a, b3, bn, name):
    t, bk = a.shape
    width = b3.shape[-1]
    piece = math.gcd(bn, width)
    rows = min(t, WGRAD_ROWS)
    n_steps = t // rows

    def body(a_ref, b_ref, o_hbm, acc, staged, sem):
        s, c = pl.program_id(0), pl.program_id(1)
        av = a_ref[...].astype(BF16)
        for chunk in range(N_CHIPS):
            @pl.when(c == chunk)
            def _(chunk=chunk):
                cols = [divmod(chunk * bn + k * piece, width) for k in range(bn // piece)]
                b = jnp.concatenate([b_ref[p, :, lo:lo + piece] for p, lo in cols], axis=1).astype(BF16)
                _accumulate(acc.at[chunk], _dot_tn(av, b), s == 0)

                @pl.when(s == n_steps - 1)
                def _():
                    staged[...] = acc[chunk].astype(BF16)
                    copy = pltpu.make_async_copy(staged, o_hbm.at[chunk], sem)
                    copy.start()
                    copy.wait()

    return pl.pallas_call(
        body, name=name, grid=(n_steps, N_CHIPS),
        in_specs=[pl.BlockSpec((rows, bk), lambda s, c: (s, 0)),
                  pl.BlockSpec((b3.shape[0], rows, width), lambda s, c: (0, s, 0))],
        out_specs=ANY,
        out_shape=jax.ShapeDtypeStruct((N_CHIPS, bk, bn), BF16),
        scratch_shapes=[pltpu.VMEM((N_CHIPS, bk, bn), F32), pltpu.VMEM((bk, bn), BF16), pltpu.SemaphoreType.DMA],
        compiler_params=_params(("arbitrary", "arbitrary")),
    )(a, b3)


def _hgrn2_chunk(st, qs, fls, ivs, gls, l0, l1, l2, ng):
    nsub = len(qs)
    mx = jnp.maximum(jnp.maximum(l0, l1), l2)
    e0, e1, e2 = jnp.exp(l0 - mx), jnp.exp(l1 - mx), jnp.exp(l2 - mx)
    lb = e0 / (e0 + e1 + e2)
    rows = lax.broadcasted_iota(jnp.int32, (A_SUB, A_SUB), 0)
    cols = lax.broadcasted_iota(jnp.int32, (A_SUB, A_SUB), 1)
    tri = (rows >= cols).astype(F32)
    keep = (lax.broadcasted_iota(jnp.int32, (A_SUB, A_SUB, A_DK), 0)
            >= lax.broadcasted_iota(jnp.int32, (A_SUB, A_SUB, A_DK), 1))
    base = jnp.zeros_like(l0)
    bases, gs, ks, qfs = [], [], [], []
    for i in range(nsub):
        f = lb + (1.0 - lb) * jax.nn.sigmoid(fls[i])
        logf = jnp.log(f)
        bases.append(base)
        gs.append(base + jnp.dot(tri, logf, precision=lax.Precision.HIGHEST, preferred_element_type=F32))
        base = base + jnp.sum(logf, axis=0, keepdims=True)
        ks.append(1.0 - f)
        qfs.append(jax.nn.silu(qs[i]))
    g_last = base
    stb = st.astype(BF16)
    outs = []
    for i in range(nsub):
        o = _dot_nt((qfs[i] * jnp.exp(gs[i])).astype(BF16), stb)
        if i > 0:
            qt = (qfs[i] * jnp.exp(gs[i] - bases[i])).astype(BF16)
            kk = jnp.concatenate([ks[j] * jnp.exp(bases[i] - gs[j]) for j in range(i)], axis=0).astype(BF16)
            vv = jnp.concatenate(ivs[:i], axis=0).astype(BF16)
            o = o + _dot(_dot_nt(qt, kk).astype(BF16), vv)
        dec = jnp.exp(jnp.where(keep, gs[i][:, None, :] - gs[i][None, :, :], NEG_BIG))
        s_diag = jnp.sum(qfs[i][:, None, :] * ks[i][None, :, :] * dec, axis=-1)
        o = o + _dot(s_diag.astype(BF16), ivs[i].astype(BF16))
        o = o * lax.rsqrt(jnp.mean(o * o, axis=-1, keepdims=True) + EPS) * ng
        outs.append(o * jax.nn.silu(gls[i]))
    kdec = jnp.concatenate([ks[j] * jnp.exp(g_last - gs[j]) for j in range(nsub)], axis=0).astype(BF16)
    vall = jnp.concatenate(ivs, axis=0).astype(BF16)
    new_st = st * jnp.exp(g_last) + _dot_tn(vall, kdec)
    return new_st, outs


A_MAX_LOG_DECAY = 60.0


def _half_sums(logf):
    n = logf.shape[0]
    first = lax.broadcasted_iota(jnp.int32, logf.shape, 0) < n // 2
    return (jnp.sum(jnp.where(first, logf, 0.0), axis=0, keepdims=True),
            jnp.sum(jnp.where(first, 0.0, logf), axis=0, keepdims=True))


def _split3(x):
    hi = x.astype(BF16)
    r1 = x - hi.astype(F32)
    mid = r1.astype(BF16)
    return hi, mid, (r1 - mid.astype(F32)).astype(BF16)


def _tri_matmul(x, transpose):
    n = x.shape[0]
    r = lax.broadcasted_iota(jnp.int32, (n, n), 0)
    c = lax.broadcasted_iota(jnp.int32, (n, n), 1)
    tri = ((r <= c) if transpose else (r >= c)).astype(BF16)
    hi, mid, lo = _split3(x)
    return (_dot(tri, lo) + _dot(tri, mid)) + _dot(tri, hi)


@jax.custom_vjp
def _cumsum_rows(x):
    return _tri_matmul(x, False)


def _cumsum_rows_fwd(x):
    return _tri_matmul(x, False), None


def _cumsum_rows_bwd(_, dy):
    return (_tri_matmul(dy, True),)


_cumsum_rows.defvjp(_cumsum_rows_fwd, _cumsum_rows_bwd)


def _lower_bound(l0, l1, l2):
    mx = jnp.maximum(jnp.maximum(l0, l1), l2)
    e0, e1, e2 = jnp.exp(l0 - mx), jnp.exp(l1 - mx), jnp.exp(l2 - mx)
    return e0 / (e0 + e1 + e2)


def _b(x):
    return x.astype(BF16)


@jax.custom_vjp
def _mm(a, b):
    return _dot(_b(a), _b(b))


_mm.defvjp(lambda a, b: (_mm(a, b), (a, b)),
           lambda res, d: (_dot_nt(_b(d), _b(res[1])), _dot_tn(_b(res[0]), _b(d))))


@jax.custom_vjp
def _mm_nt(a, b):
    return _dot_nt(_b(a), _b(b))


_mm_nt.defvjp(lambda a, b: (_mm_nt(a, b), (a, b)),
              lambda res, d: (_dot(_b(d), _b(res[1])), _dot_tn(_b(d), _b(res[0]))))


def _dot_split(dot, a, b):
    ah, bh = _b(a), _b(b)
    al, bl = _b(a - ah.astype(F32)), _b(b - bh.astype(F32))
    return (dot(ah, bl) + dot(al, bh)) + dot(ah, bh)


@jax.custom_vjp
def _mm_scores(a, b):
    return _dot_nt(_b(a), _b(b))


_mm_scores.defvjp(lambda a, b: (_mm_scores(a, b), (a, b)),
                  lambda res, d: (_dot_split(_dot, d, res[1]), _dot_split(_dot_tn, d, res[0])))


@jax.custom_vjp
def _mm_tn(a, b):
    return _dot_tn(_b(a), _b(b))


_mm_tn.defvjp(lambda a, b: (_mm_tn(a, b), (a, b)),
              lambda res, d: (_dot_nt(_b(res[1]), _b(d)), _dot(_b(res[0]), _b(d))))


@jax.custom_vjp
def _split_heads(x):
    return tuple(x[:, h * A_DK:(h + 1) * A_DK] for h in range(A_HEADS))


def _split_heads_fwd(x):
    return _split_heads(x), None


def _split_heads_bwd(_, parts):
    return (jnp.concatenate(parts, axis=1),)


_split_heads.defvjp(_split_heads_fwd, _split_heads_bwd)


def _hgrn2_chunk_fast(sts, q, fl, iv, gl, l0, l1, l2, ng):
    lb = _lower_bound(l0, l1, l2)
    f = lb + (1.0 - lb) * jax.nn.sigmoid(fl)
    return _hgrn2_fast_core(sts, q, f, jnp.log(f), iv, gl, ng)


def _hgrn2_fast_core(sts, q, f, logf, iv, gl, ng):
    g = _cumsum_rows(logf)
    g_mid, g_last = _half_sums(logf)
    g_last = g_mid + g_last
    k = 1.0 - f
    qf = jax.nn.silu(q)
    qms = _split_heads(qf * jnp.exp(g - g_mid))
    kms = _split_heads(k * jnp.exp(g_mid - g))
    qgs = _split_heads(qf * jnp.exp(g))
    kds = _split_heads(k * jnp.exp(g_last - g))
    ivs = _split_heads(iv)
    decays = _split_heads(jnp.exp(g_last))
    n = q.shape[0]
    causal = lax.broadcasted_iota(jnp.int32, (n, n), 0) >= lax.broadcasted_iota(jnp.int32, (n, n), 1)
    raw = [_mm_scores(qm, km) for qm, km in zip(qms, kms)]
    inter = [_mm_nt(qg, st) for qg, st in zip(qgs, sts)]
    scores = [jnp.where(causal, s, 0.0) for s in raw]
    os = [a + _mm(s, v) for a, s, v in zip(inter, scores, ivs)]
    new_sts = [st * d + _mm_tn(v, kd) for st, d, v, kd in zip(sts, decays, ivs, kds)]
    os = [o * lax.rsqrt(jnp.mean(o * o, axis=-1, keepdims=True) + EPS) for o in os]
    return new_sts, jnp.concatenate(os, axis=1) * ng * jax.nn.silu(gl)


A_STEP_CHUNKS = 4


def _chunk_rows(j):
    return pl.ds(pl.multiple_of(j * A_CHUNK, A_CHUNK), A_CHUNK)


def _sub_rows(j, i):
    return pl.ds(pl.multiple_of(j * A_CHUNK + i * A_SUB, A_SUB), A_SUB)


def _sub_blocks(ref, head, j):
    lanes = slice(head * A_DK, (head + 1) * A_DK)
    return [ref[_sub_rows(j, i), lanes] for i in range(A_CHUNK // A_SUB)]


def hgrn2_fwd(proj, lb_table, a_norm, batch, name, exchange=None):
    t = proj.shape[0]
    n_steps = t // batch // (A_CHUNK * A_STEP_CHUNKS)
    rows = A_CHUNK * A_STEP_CHUNKS

    def body(q_ref, f_ref, i_ref, g_ref, lb_ref, ng_ref, o_ref, st_ref, dec_ref, st):
        @pl.when(pl.program_id(1) == 0)
        def _():
            st[...] = jnp.zeros_like(st)

        def chunk(j, carry):
            r = _chunk_rows(j)
            st_ref[j] = st[...]
            lb = _lower_bound(lb_ref[0:1, :], lb_ref[1:2, :], lb_ref[2:3, :])
            f = lb + (1.0 - lb) * jax.nn.sigmoid(f_ref[r, :])
            logf = jnp.log(f)
            decay = jnp.minimum(*_half_sums(logf))
            dec_ref[j] = decay
            mild = jnp.min(decay) >= -A_MAX_LOG_DECAY

            @pl.when(mild)
            def _():
                new_sts, o = _hgrn2_fast_core([st[h] for h in range(A_HEADS)], q_ref[r, :], f, logf,
                                              i_ref[r, :], g_ref[r, :], ng_ref[...])
                for h in range(A_HEADS):
                    st[h] = new_sts[h]
                o_ref[r, :] = o.astype(BF16)

            @pl.when(jnp.logical_not(mild))
            def _():
                for h in range(A_HEADS):
                    lanes = slice(h * A_DK, (h + 1) * A_DK)
                    new_st, outs = _hgrn2_chunk(
                        st[h], _sub_blocks(q_ref, h, j), _sub_blocks(f_ref, h, j), _sub_blocks(i_ref, h, j),
                        _sub_blocks(g_ref, h, j), lb_ref[0:1, lanes], lb_ref[1:2, lanes], lb_ref[2:3, lanes],
                        ng_ref[:, lanes])
                    st[h] = new_st
                    for i, o in enumerate(outs):
                        o_ref[_sub_rows(j, i), lanes] = o.astype(BF16)

            return carry

        lax.fori_loop(0, A_STEP_CHUNKS, chunk, 0)

    def part(k):
        return pl.BlockSpec((rows, A_WIDTH), lambda b, n: (b * n_steps + n, k))

    return _call(
        body, name=name, grid=(batch, n_steps),
        in_specs=[part(0), part(1), part(2), part(3),
                  pl.BlockSpec((3, A_WIDTH), lambda b, n: (0, 0)), pl.BlockSpec((1, A_WIDTH), lambda b, n: (0, 0))],
        out_specs=[part(0),
                   pl.BlockSpec((A_STEP_CHUNKS, A_HEADS, A_DK, A_DK), lambda b, n: (b * n_steps + n, 0, 0, 0)),
                   pl.BlockSpec((A_STEP_CHUNKS, 1, A_WIDTH), lambda b, n: (b * n_steps + n, 0, 0))],
        out_shape=[jax.ShapeDtypeStruct((t, A_WIDTH), BF16),
                   jax.ShapeDtypeStruct((t // A_CHUNK, A_HEADS, A_DK, A_DK), F32),
                   jax.ShapeDtypeStruct((t // A_CHUNK, 1, A_WIDTH), F32)],
        scratch_shapes=[pltpu.VMEM((A_HEADS, A_DK, A_DK), F32)],
        args=(proj, proj, proj, proj, lb_table, a_norm), exchange=exchange)


def hgrn2_bwd(proj, states, decays, lb_table, a_norm, do, batch, name, exchange=None):
    t = proj.shape[0]
    n_steps = t // batch // (A_CHUNK * A_STEP_CHUNKS)
    rows = A_CHUNK * A_STEP_CHUNKS

    def body(q_ref, f_ref, i_ref, g_ref, st_ref, dec_ref, lb_ref, ng_ref, do_ref, dp_ref, dlb_ref, dng_ref, dst):
        @pl.when(jnp.logical_and(pl.program_id(0) == 0, pl.program_id(1) == 0))
        def _():
            dlb_ref[...] = jnp.zeros_like(dlb_ref)
            dng_ref[...] = jnp.zeros_like(dng_ref)

        @pl.when(pl.program_id(1) == 0)
        def _():
            dst[...] = jnp.zeros_like(dst)

        def chunk(jj, carry):
            j = A_STEP_CHUNKS - 1 - jj
            r = _chunk_rows(j)
            mild = jnp.min(dec_ref[j]) >= -A_MAX_LOG_DECAY

            @pl.when(mild)
            def _():
                _, vjp = jax.vjp(
                    _hgrn2_chunk_fast, [st_ref[j, h] for h in range(A_HEADS)], q_ref[r, :], f_ref[r, :],
                    i_ref[r, :], g_ref[r, :], lb_ref[0:1, :], lb_ref[1:2, :], lb_ref[2:3, :], ng_ref[...])
                d_sts, dq, df, di, dg, dl0, dl1, dl2, dng = vjp(
                    ([dst[h] for h in range(A_HEADS)], do_ref[r, :].astype(F32)))
                for h in range(A_HEADS):
                    dst[h] = d_sts[h]
                for k, part in enumerate((dq, df, di, dg)):
                    dp_ref[r, k * A_WIDTH:(k + 1) * A_WIDTH] = part
                for row, val in enumerate((dl0, dl1, dl2)):
                    dlb_ref[row:row + 1, :] += val
                dng_ref[...] += dng

            @pl.when(jnp.logical_not(mild))
            def _():
                for h in range(A_HEADS):
                    lanes = slice(h * A_DK, (h + 1) * A_DK)
                    _, vjp = jax.vjp(
                        _hgrn2_chunk, st_ref[j, h], _sub_blocks(q_ref, h, j), _sub_blocks(f_ref, h, j),
                        _sub_blocks(i_ref, h, j), _sub_blocks(g_ref, h, j), lb_ref[0:1, lanes], lb_ref[1:2, lanes],
                        lb_ref[2:3, lanes], ng_ref[:, lanes])
                    douts = [x.astype(F32) for x in _sub_blocks(do_ref, h, j)]
                    d_st, dqs, dfs, dis, dgs, dl0, dl1, dl2, dng = vjp((dst[h], douts))
                    dst[h] = d_st
                    for k, parts in enumerate((dqs, dfs, dis, dgs)):
                        for i in range(A_CHUNK // A_SUB):
                            dp_ref[_sub_rows(j, i), k * A_WIDTH + h * A_DK:k * A_WIDTH + (h + 1) * A_DK] = parts[i]
                    for row, val in enumerate((dl0, dl1, dl2)):
                        dlb_ref[row:row + 1, lanes] += val
                    dng_ref[:, lanes] += dng

            return carry

        lax.fori_loop(0, A_STEP_CHUNKS, chunk, 0)

    def rev(b, n):
        return b * n_steps + (n_steps - 1 - n)

    def part(k):
        return pl.BlockSpec((rows, A_WIDTH), lambda b, n: (rev(b, n), k))

    const3 = pl.BlockSpec((3, A_WIDTH), lambda b, n: (0, 0))
    const1 = pl.BlockSpec((1, A_WIDTH), lambda b, n: (0, 0))
    return _call(
        body, name=name, grid=(batch, n_steps),
        in_specs=[part(0), part(1), part(2), part(3),
                  pl.BlockSpec((A_STEP_CHUNKS, A_HEADS, A_DK, A_DK), lambda b, n: (rev(b, n), 0, 0, 0)),
                  pl.BlockSpec((A_STEP_CHUNKS, 1, A_WIDTH), lambda b, n: (rev(b, n), 0, 0)),
                  const3, const1, part(0)],
        out_specs=[pl.BlockSpec((rows, 4 * A_WIDTH), lambda b, n: (rev(b, n), 0)), const3, const1],
        out_shape=[jax.ShapeDtypeStruct((t, 4 * A_WIDTH + 2 * B_WIDTH), F32),
                   jax.ShapeDtypeStruct((3, A_WIDTH), F32), jax.ShapeDtypeStruct((1, A_WIDTH), F32)],
        scratch_shapes=[pltpu.VMEM((A_HEADS, A_DK, A_DK), F32)],
        args=(proj, proj, proj, proj, states, decays, lb_table, a_norm, do), exchange=exchange)


B_GDIM = B_WIDTH // B_GROUPS
B_ROWS = 512


def _gmlp_chunk(ubs, vbs, lngs, lnbs, ws, bcols):
    vs = [jax.nn.gelu(v) for v in vbs]
    mu = sum(jnp.sum(v, axis=-1, keepdims=True) for v in vs) * (1.0 / B_WIDTH)
    var = sum(jnp.sum(jnp.square(v - mu), axis=-1, keepdims=True) for v in vs) * (1.0 / B_WIDTH)
    rstd = lax.rsqrt(var + EPS)
    tril = (lax.broadcasted_iota(jnp.int32, (B_CHUNK, B_CHUNK), 0)
            >= lax.broadcasted_iota(jnp.int32, (B_CHUNK, B_CHUNK), 1))
    outs = []
    for g in range(B_GROUPS):
        vn = (vs[g] - mu) * rstd * lngs[g] + lnbs[g]
        w = jnp.where(tril, ws[g], 0.0).astype(BF16)
        outs.append(jax.nn.gelu(ubs[g]) * (_dot(w, vn.astype(BF16)) + bcols[g]))
    return outs


def _gmlp_args(u_ref, v_ref, lng_ref, lnb_ref, w_ref, bt_ref, rows):
    def groups(ref):
        return [ref[rows, g * B_GDIM:(g + 1) * B_GDIM] for g in range(B_GROUPS)]

    def vec(ref):
        return [ref[:, g * B_GDIM:(g + 1) * B_GDIM] for g in range(B_GROUPS)]

    return (groups(u_ref), groups(v_ref), vec(lng_ref), vec(lnb_ref),
            [w_ref[g] for g in range(B_GROUPS)], [bt_ref[:, g:g + 1] for g in range(B_GROUPS)])


def gmlp_fwd(proj, oa, ln_g, ln_b, w, bias_t, name, exchange=None):
    t = proj.shape[0]

    def body(u_ref, v_ref, oa_ref, lng_ref, lnb_ref, w_ref, bt_ref, o_ref):
        o_ref[:, 0:A_WIDTH] = oa_ref[...]
        for n in range(B_ROWS // B_CHUNK):
            rows = slice(n * B_CHUNK, (n + 1) * B_CHUNK)
            outs = _gmlp_chunk(*_gmlp_args(u_ref, v_ref, lng_ref, lnb_ref, w_ref, bt_ref, rows))
            for g, o in enumerate(outs):
                o_ref[rows, A_WIDTH + g * B_GDIM:A_WIDTH + (g + 1) * B_GDIM] = o.astype(BF16)

    vec = pl.BlockSpec((1, B_WIDTH), lambda i: (0, 0))
    return _call(
        body, name=name, grid=(t // B_ROWS,),
        in_specs=[pl.BlockSpec((B_ROWS, B_WIDTH), lambda i: (i, 4)), pl.BlockSpec((B_ROWS, B_WIDTH), lambda i: (i, 5)),
                  pl.BlockSpec((B_ROWS, A_WIDTH), lambda i: (i, 0)), vec, vec,
                  pl.BlockSpec((B_GROUPS, B_CHUNK, B_CHUNK), lambda i: (0, 0, 0)),
                  pl.BlockSpec((B_CHUNK, B_GROUPS), lambda i: (0, 0))],
        out_specs=[pl.BlockSpec((B_ROWS, A_WIDTH + B_WIDTH), lambda i: (i, 0))],
        out_shape=[jax.ShapeDtypeStruct((t, A_WIDTH + B_WIDTH), BF16)],
        args=(proj, proj, oa, ln_g, ln_b, w, bias_t), exchange=exchange)


def gmlp_bwd(proj, dmixin, ln_g, ln_b, w, bias_t, dproj, name, exchange=None):
    t = proj.shape[0]

    def body(u_ref, v_ref, do_ref, lng_ref, lnb_ref, w_ref, bt_ref, dp_in_ref,
             dp_ref, dlng_ref, dlnb_ref, dw_ref, dbt_ref):
        del dp_in_ref

        @pl.when(pl.program_id(0) == 0)
        def _():
            for ref in (dlng_ref, dlnb_ref, dw_ref, dbt_ref):
                ref[...] = jnp.zeros_like(ref)

        for n in range(B_ROWS // B_CHUNK):
            rows = slice(n * B_CHUNK, (n + 1) * B_CHUNK)
            _, vjp = jax.vjp(_gmlp_chunk, *_gmlp_args(u_ref, v_ref, lng_ref, lnb_ref, w_ref, bt_ref, rows))
            douts = [do_ref[rows, g * B_GDIM:(g + 1) * B_GDIM] for g in range(B_GROUPS)]
            dus, dvs, dlngs, dlnbs, dws, dbs = vjp(douts)
            for g in range(B_GROUPS):
                lanes = slice(g * B_GDIM, (g + 1) * B_GDIM)
                dp_ref[rows, lanes] = dus[g]
                dp_ref[rows, B_WIDTH + g * B_GDIM:B_WIDTH + (g + 1) * B_GDIM] = dvs[g]
                dlng_ref[:, lanes] += dlngs[g]
                dlnb_ref[:, lanes] += dlnbs[g]
                dw_ref[g] += dws[g]
                dbt_ref[:, g:g + 1] += dbs[g]

    vec = pl.BlockSpec((1, B_WIDTH), lambda i: (0, 0))
    wspec = pl.BlockSpec((B_GROUPS, B_CHUNK, B_CHUNK), lambda i: (0, 0, 0))
    bspec = pl.BlockSpec((B_CHUNK, B_GROUPS), lambda i: (0, 0))
    return _call(
        body, name=name, grid=(t // B_ROWS,),
        in_specs=[pl.BlockSpec((B_ROWS, B_WIDTH), lambda i: (i, 4)), pl.BlockSpec((B_ROWS, B_WIDTH), lambda i: (i, 5)),
                  pl.BlockSpec((B_ROWS, B_WIDTH), lambda i: (i, 1)), vec, vec, wspec, bspec,
                  pl.BlockSpec(memory_space=pl.ANY)],
        out_specs=[pl.BlockSpec((B_ROWS, 2 * B_WIDTH), lambda i: (i, 2)), vec, vec, wspec, bspec],
        out_shape=[jax.ShapeDtypeStruct(dproj.shape, F32), jax.ShapeDtypeStruct((1, B_WIDTH), F32),
                   jax.ShapeDtypeStruct((1, B_WIDTH), F32), jax.ShapeDtypeStruct((B_GROUPS, B_CHUNK, B_CHUNK), F32),
                   jax.ShapeDtypeStruct((B_CHUNK, B_GROUPS), F32)],
        aliases={7: 0}, args=(proj, proj, dmixin, ln_g, ln_b, w, bias_t, dproj), exchange=exchange)


C_FWD_BLOCKS = 8
C_BWD_BLOCKS = 4
C_PAIR = 2 * C_HEAD_DIM
C_PAIRS = C_HEADS // 2
C_SCALE = 1.0 / math.sqrt(C_HEAD_DIM)
C_ROT_DIM = 2 * C_ROT_HALF
ROPE_ROWS = 1024


def rope_tables(pos_col, name):
    t = pos_col.shape[0]

    def body(p_ref, c_ref, a_ref, b_ref):
        lane = jnp.bitwise_and(lax.broadcasted_iota(jnp.int32, (1, C_PAIR), 1), C_HEAD_DIM - 1)
        j = jnp.bitwise_and(lane, C_ROT_HALF - 1).astype(F32)
        inv = jnp.exp(j * (-math.log(ROPE_THETA) / C_ROT_HALF))
        ang = p_ref[...].astype(F32) * inv
        cos, sin = jnp.cos(ang), jnp.sin(ang)
        c_ref[...] = jnp.where(lane < C_ROT_DIM, cos, 1.0)
        a_ref[...] = jnp.where(lane < C_ROT_HALF, -sin, 0.0)
        b_ref[...] = jnp.where(jnp.logical_and(lane >= C_ROT_HALF, lane < C_ROT_DIM), sin, 0.0)

    tab = pl.BlockSpec((ROPE_ROWS, C_PAIR), lambda i: (i, 0))
    return pl.pallas_call(
        body, name=name, grid=(t // ROPE_ROWS,),
        in_specs=[pl.BlockSpec((ROPE_ROWS, 1), lambda i: (i, 0))],
        out_specs=[tab, tab, tab],
        out_shape=[jax.ShapeDtypeStruct((t, C_PAIR), F32)] * 3,
        compiler_params=_params(("arbitrary",)),
    )(pos_col)


def _rope(x, c, a, b):
    return x * c + pltpu.roll(x, C_PAIR - C_ROT_HALF, 1) * a + pltpu.roll(x, C_ROT_HALF, 1) * b


def _rope_t(d, c, a, b):
    return d * c + pltpu.roll(d * a, C_ROT_HALF, 1) + pltpu.roll(d * b, C_PAIR - C_ROT_HALF, 1)


C_RES = 16


def _residue_major(a, batch):
    return a.reshape(batch, SEQ // C_RES, C_RES, -1).transpose(0, 2, 1, 3).reshape(a.shape)


def _sequence_order(a, batch):
    return a.reshape(batch, C_RES, SEQ // C_RES, -1).transpose(0, 2, 1, 3).reshape(a.shape)


def _block_pieces(idx, dil):
    nblk = SEQ // dil // C_BLOCK
    r, n = idx // nblk, idx % nblk
    per = C_RES // dil
    size = C_BLOCK // per

    def pieces(blk):
        return [((dil * a + r) * (SEQ // C_RES) + size * blk, size) for a in range(per)]

    return pieces(n), pieces(jnp.maximum(n - 1, 0)), n > 0


def _get_rows(ref, pieces):
    return jnp.concatenate([ref[pl.ds(pl.multiple_of(start, 8), size), :] for start, size in pieces], axis=0)


def _set_rows(ref, pieces, val, add=False):
    for k, (start, size) in enumerate(pieces):
        rows = pl.ds(pl.multiple_of(start, 8), size)
        part = val[k * size:(k + 1) * size]
        ref[rows, :] = ref[rows, :] + part if add else part


def _head_masks():
    low = lax.broadcasted_iota(jnp.int32, (1, C_PAIR), 1) < C_HEAD_DIM
    return low, jnp.logical_not(low)


def _attn_mask(has_prev, dil):
    per = C_RES // dil
    size = C_BLOCK // per

    def position(x):
        x = jnp.bitwise_and(x, C_BLOCK - 1)
        return per * jnp.bitwise_and(x, size - 1) + x // size

    j = lax.broadcasted_iota(jnp.int32, (2 * C_BLOCK, 2 * C_BLOCK), 1)
    pi = position(lax.broadcasted_iota(jnp.int32, (2 * C_BLOCK, 2 * C_BLOCK), 0))
    pj = position(j)
    own = j < C_BLOCK
    return jnp.logical_or(jnp.logical_and(own, pj <= pi),
                          jnp.logical_and(jnp.logical_and(jnp.logical_not(own), pj >= pi), has_prev))


def _stack_heads(x):
    low, high = _head_masks()
    return jnp.concatenate([jnp.where(low, x, 0.0), jnp.where(high, x, 0.0)], axis=0)


def _unstack_heads(x):
    low, _ = _head_masks()
    return jnp.where(low, x[:C_BLOCK], x[C_BLOCK:])


def attn_fwd(qkv, cos_t, sin_a, sin_b, batch, name, exchange=None):
    t = qkv.shape[0]
    nbr = len(C_DILATIONS)

    def body(q_ref, k_ref, v_ref, c_ref, a_ref, b_ref, o_ref, l_ref, qs, ks, *stats):
        acc, mm, dd = stats[0:nbr], stats[nbr:2 * nbr], stats[2 * nbr:3 * nbr]
        c, a, b = c_ref[...], a_ref[...], b_ref[...]
        qs[...] = _rope(q_ref[...], c, a, b) * C_SCALE
        ks[...] = _rope(k_ref[...], c, a, b)

        def load(idx, dil):
            own, prev, has_prev = _block_pieces(idx, dil)
            return own, (has_prev, _get_rows(qs, own), _get_rows(ks, own), _get_rows(ks, prev),
                         _get_rows(v_ref, own), _get_rows(v_ref, prev))

        def scores(dil, has_prev, q, k_own, k_prev, v_own, v_prev):
            k_cat = jnp.concatenate([k_own, k_prev], axis=0).astype(BF16)
            return jnp.where(_attn_mask(has_prev, dil), _dot_nt(_stack_heads(q).astype(BF16), k_cat), NEG_BIG)

        def softmax(s):
            m = jnp.max(s, axis=-1, keepdims=True)
            p = jnp.exp(s - m)
            return p.astype(BF16), m, jnp.sum(p, axis=-1, keepdims=True)

        def values(pb, has_prev, q, k_own, k_prev, v_own, v_prev):
            low, high = _head_masks()
            v_cat = jnp.concatenate([v_own, v_prev], axis=0)
            p_wide = jnp.concatenate([pb[:C_BLOCK], pb[C_BLOCK:]], axis=1)
            v_tall = jnp.concatenate([jnp.where(low, v_cat, 0.0), jnp.where(high, v_cat, 0.0)], axis=0).astype(BF16)
            return _dot(p_wide, v_tall)

        for bi, dil in enumerate(C_DILATIONS):
            def one_pass(i, carry,bi=bi, dil=dil):
                low, _ = _head_masks()
                loaded = [load(C_FWD_BLOCKS * i + k, dil) for k in range(C_FWD_BLOCKS)]
                ss = [scores(dil, *ops) for _, ops in loaded]
                sm = [softmax(s) for s in ss]
                pvs = [values(pb, *ops) for (pb, _, _), (_, ops) in zip(sm, loaded)]
                for (own, _), (_, m, den), pv in zip(loaded, sm, pvs):
                    _set_rows(acc[bi], own, pv)
                    _set_rows(mm[bi], own, jnp.where(low, m[:C_BLOCK], m[C_BLOCK:]))
                    _set_rows(dd[bi], own, jnp.where(low, den[:C_BLOCK], den[C_BLOCK:]))
                return carry

            lax.fori_loop(0, SEQ // C_BLOCK // C_FWD_BLOCKS, one_pass, 0)
        step = 2 * C_BLOCK
        for r0 in range(0, SEQ, step):
            rr = slice(r0, r0 + step)
            ms = [mm[g][rr, :] for g in range(nbr)]
            m_all = functools.reduce(jnp.maximum, ms)
            ws = [jnp.exp(m - m_all) for m in ms]
            num = sum(acc[g][rr, :] * ws[g] for g in range(nbr))
            den = sum(dd[g][rr, :] * ws[g] for g in range(nbr))
            o_ref[rr, :] = (num / den).astype(BF16)
            l_ref[rr, :] = m_all + jnp.log(den)

    def col(k):
        return pl.BlockSpec((SEQ, C_PAIR), lambda b, p: (b, k * C_PAIRS + p))

    tab = pl.BlockSpec((SEQ, C_PAIR), lambda b, p: (b, 0))
    return _call(
        body, name=name, grid=(batch, C_PAIRS),
        in_specs=[col(0), col(1), col(2), tab, tab, tab],
        out_specs=[col(0), col(0)],
        out_shape=[jax.ShapeDtypeStruct((t, D_MODEL), BF16), jax.ShapeDtypeStruct((t, D_MODEL), F32)],
        scratch_shapes=[pltpu.VMEM((SEQ, C_PAIR), F32)] * (2 + 3 * nbr),
        args=(qkv, qkv, qkv, cos_t, sin_a, sin_b), exchange=exchange)


def attn_bwd(qkv, cos_t, sin_a, sin_b, o, lse, do, batch, name, exchange=None):
    t = qkv.shape[0]

    def body(q_ref, k_ref, v_ref, c_ref, a_ref, b_ref, o_ref, l_ref, do_ref, dqkv_ref, qs, ks, dqs, dks, dvs, dlt):
        low, _ = _head_masks()
        c, a, b = c_ref[...], a_ref[...], b_ref[...]
        qs[...] = _rope(q_ref[...], c, a, b) * C_SCALE
        ks[...] = _rope(k_ref[...], c, a, b)
        prod = do_ref[...] * o_ref[...].astype(F32)
        s_low = jnp.sum(jnp.where(low, prod, 0.0), axis=-1, keepdims=True)
        s_all = jnp.sum(prod, axis=-1, keepdims=True)
        dlt[...] = jnp.where(low, s_low, s_all - s_low)
        dqs[...] = jnp.zeros_like(dqs)
        dks[...] = jnp.zeros_like(dks)
        dvs[...] = jnp.zeros_like(dvs)

        def load(idx, dil):
            own, prev, has_prev = _block_pieces(idx, dil)
            return (own, prev), (has_prev, _get_rows(qs, own), _get_rows(do_ref, own), _get_rows(ks, own),
                                 _get_rows(ks, prev), _get_rows(v_ref, own), _get_rows(v_ref, prev),
                                 _get_rows(l_ref, own), _get_rows(dlt, own))

        def operands(dil, has_prev, q, do, k_own, k_prev, v_own, v_prev, l_full, d_full):
            lcol = jnp.concatenate([l_full[:, 0:1], l_full[:, C_HEAD_DIM:C_HEAD_DIM + 1]], axis=0)
            dcol = jnp.concatenate([d_full[:, 0:1], d_full[:, C_HEAD_DIM:C_HEAD_DIM + 1]], axis=0)
            return (_stack_heads(q).astype(BF16), _stack_heads(do).astype(BF16),
                    jnp.concatenate([k_own, k_prev], axis=0).astype(BF16),
                    jnp.concatenate([v_own, v_prev], axis=0).astype(BF16), lcol, dcol, _attn_mask(has_prev, dil))

        for dil in C_DILATIONS:
            def one_pass(i, carry,dil=dil):
                loaded = [load(C_BWD_BLOCKS * i + k, dil) for k in range(C_BWD_BLOCKS)]
                ops = [operands(dil, *o) for _, o in loaded]
                ss = [_dot_nt(q_stack, k_cat) for q_stack, _, k_cat, _, _, _, _ in ops]
                dps = [_dot_nt(do_stack, v_cat) for _, do_stack, _, v_cat, _, _, _ in ops]
                ps = [jnp.exp(jnp.where(o[6], s, NEG_BIG) - o[4]) for s, o in zip(ss, ops)]
                dss = [(p * (dp - o[5])).astype(BF16) for p, dp, o in zip(ps, dps, ops)]
                dvs_ = [_dot_tn(p.astype(BF16), o[1]) for p, o in zip(ps, ops)]
                dks_ = [_dot_tn(ds, o[0]) for ds, o in zip(dss, ops)]
                dqs_ = [_unstack_heads(_dot(ds, o[2])) for ds, o in zip(dss, ops)]
                for ((own, prev), _), dq, dk_cat, dv_cat in zip(loaded, dqs_, dks_, dvs_):
                    _set_rows(dqs, own, dq, add=True)
                    _set_rows(dks, own, dk_cat[:C_BLOCK], add=True)
                    _set_rows(dvs, own, dv_cat[:C_BLOCK], add=True)
                    _set_rows(dks, prev, dk_cat[C_BLOCK:], add=True)
                    _set_rows(dvs, prev, dv_cat[C_BLOCK:], add=True)
                return carry

            lax.fori_loop(0, SEQ // C_BLOCK // C_BWD_BLOCKS, one_pass, 0)
        dqkv_ref[0] = _rope_t(dqs[...] * C_SCALE, c, a, b).astype(BF16)
        dqkv_ref[1] = _rope_t(dks[...], c, a, b).astype(BF16)
        dqkv_ref[2] = dvs[...].astype(BF16)

    def col(k):
        return pl.BlockSpec((SEQ, C_PAIR), lambda b, p: (b, k * C_PAIRS + p))

    tab = pl.BlockSpec((SEQ, C_PAIR), lambda b, p: (b, 0))
    return _call(
        body, name=name, grid=(batch, C_PAIRS),
        in_specs=[col(0), col(1), col(2), tab, tab, tab, col(0), col(0), col(0)],
        out_specs=[pl.BlockSpec((3, SEQ, C_PAIR), lambda b, p: (0, b, p))],
        out_shape=[jax.ShapeDtypeStruct((3, t, D_MODEL), BF16)],
        scratch_shapes=[pltpu.VMEM((SEQ, C_PAIR), F32)] * 6,
        args=(qkv, qkv, qkv, cos_t, sin_a, sin_b, o, lse, do), exchange=exchange)


def sibling_swap(arrays, name):
    n = len(arrays)

    def body(*refs):
        ins, outs = refs[:n], refs[n:2 * n]
        send_sems, recv_sems = refs[2 * n:]
        x, y, c, _ = _place()
        sends = []
        for a in range(n):
            cp = pltpu.make_async_remote_copy(
                src_ref=ins[a], dst_ref=outs[a], send_sem=send_sems.at[a], recv_sem=recv_sems.at[a],
                device_id=(x, y, 1 - c), device_id_type=MESH)
            cp.start()
            sends.append(cp)
        for cp in sends:
            cp.wait_recv()
        for cp in sends:
            cp.wait_send()

    return pl.pallas_call(
        body, name=name,
        in_specs=[ANY] * n, out_specs=[ANY] * n,
        out_shape=[jax.ShapeDtypeStruct(s.shape, s.dtype) for s in arrays],
        scratch_shapes=[pltpu.SemaphoreType.DMA((n,)), pltpu.SemaphoreType.DMA((n,))],
    )(*arrays)


def allreduce_small(slab, name):
    rows, lanes = slab.shape

    def body(x_ref, out_ref, gath, send_sems, recv_sems, local_sem):
        x, y, c, chips = _place()
        me, sibling = (x, y, c), (x, y, 1 - c)

        def slot(px, py, pc):
            return gath.at[4 * px + 2 * py + pc]

        def copy(k, block, to, src=None):
            return pltpu.make_async_remote_copy(
                src_ref=slot(*block) if src is None else src, dst_ref=slot(*block),
                send_sem=send_sems.at[k], recv_sem=recv_sems.at[k], device_id=to, device_id_type=MESH)

        mine = pltpu.make_async_copy(x_ref, slot(*me), local_sem)
        mine.start()
        first = [copy(0, me, sibling, src=x_ref)]
        first += [copy(1 + j, me, (*chip, c), src=x_ref) for j, chip in enumerate(chips)]
        for cp in first:
            cp.start()
        passed = [copy(4 + j, (*chip, c), sibling) for j, chip in enumerate(chips)]
        for j, chip in enumerate(chips):
            copy(1 + j, (*chip, c), me).wait_recv()
            passed[j].start()
        copy(0, sibling, me).wait_recv()
        for j, chip in enumerate(chips):
            copy(4 + j, (*chip, 1 - c), me).wait_recv()
        for cp in first + passed:
            cp.wait_send()
        mine.wait()
        total = gath[0]
        for d in range(1, N_DEV):
            total = total + gath[d]
        out_ref[...] = total

    return pl.pallas_call(
        body, name=name,
        in_specs=[pl.BlockSpec(memory_space=pltpu.VMEM)],
        out_specs=pl.BlockSpec(memory_space=pltpu.VMEM),
        out_shape=jax.ShapeDtypeStruct((rows, lanes), F32),
        scratch_shapes=[pltpu.VMEM((N_DEV, rows, lanes), F32),
                        pltpu.SemaphoreType.DMA((7,)), pltpu.SemaphoreType.DMA((7,)), pltpu.SemaphoreType.DMA],
    )(slab)


ELT_ROWS = 512


def reduce_slabs(r, name, part=0, parts=1, into=None):
    _, rows, cols = r.shape
    br = min(rows, ELT_ROWS)
    nblk = rows // br

    def body(r_ref, *rest):
        o_ref = rest[-1]
        o_ref[...] = ((r_ref[3].astype(F32) + r_ref[0].astype(F32)) + r_ref[1].astype(F32)) + r_ref[2].astype(F32)

    return pl.pallas_call(
        body, name=name, grid=(nblk,),
        in_specs=[pl.BlockSpec((N_CHIPS, br, cols), lambda i: (0, i, 0))] + ([] if into is None else [ANY]),
        out_specs=pl.BlockSpec((br, cols), lambda i: (part * nblk + i, 0)),
        out_shape=jax.ShapeDtypeStruct((parts * rows, cols), F32),
        input_output_aliases={} if into is None else {1: 0},
        compiler_params=_params(("arbitrary",)),
    )(*([r] if into is None else [r, into]))


def _adamw(w, g, m, v):
    m = ADAM_B1 * m + (1.0 - ADAM_B1) * g
    v = ADAM_B2 * v + (1.0 - ADAM_B2) * jnp.square(g)
    m_hat = m / (1.0 - ADAM_B1 ** ADAM_STEP)
    v_hat = v / (1.0 - ADAM_B2 ** ADAM_STEP)
    delta = -ADAM_LR * (m_hat / (jnp.sqrt(v_hat) + ADAM_EPS) + ADAM_WD * w)
    return delta, m, v


def adamw_big(w, s_mine, s_sibling, m, v, name):
    rows, cols = w.shape

    def body(w_ref, a_ref, b_ref, m_ref, v_ref, g_out, d_out, m_out, v_out):
        g = a_ref[...] + b_ref[...]
        g_out[...] = g
        d_out[...], m_out[...], v_out[...] = _adamw(w_ref[...], g, m_ref[...], v_ref[...])

    blk = pl.BlockSpec((min(rows, ELT_ROWS), cols), lambda i: (i, 0))
    out = jax.ShapeDtypeStruct((rows, cols), F32)
    return pl.pallas_call(
        body, name=name, grid=(rows // min(rows, ELT_ROWS),),
        in_specs=[blk] * 5, out_specs=[blk] * 4, out_shape=[out] * 4,
        compiler_params=_params(("arbitrary",)),
    )(w, s_mine, s_sibling, m, v)


def adamw_small(ws, gs, ms, vs, name):
    n = len(ws)

    def body(*refs):
        w_refs, g_refs, m_refs, v_refs = (refs[k * n:(k + 1) * n] for k in range(4))
        d_out, m_out, v_out = (refs[(4 + k) * n:(5 + k) * n] for k in range(3))
        for i in range(n):
            d_out[i][...], m_out[i][...], v_out[i][...] = _adamw(
                w_refs[i][...], g_refs[i][...], m_refs[i][...], v_refs[i][...])

    outs = [jax.ShapeDtypeStruct(w.shape, F32) for w in ws]
    res = pl.pallas_call(body, name=name, out_shape=outs * 3)(*ws, *gs, *ms, *vs)
    return res[:n], res[n:2 * n], res[2 * n:]


SLAB_LANES = 128
SLAB_ROW_ALIGN = 8


def _pack(parts):
    flat = jnp.concatenate([p.reshape(-1) for p in parts])
    rows = -(-flat.shape[0] // (SLAB_LANES * SLAB_ROW_ALIGN)) * SLAB_ROW_ALIGN
    flat = jnp.pad(flat, (0, rows * SLAB_LANES - flat.shape[0]))
    return flat.reshape(rows, SLAB_LANES)


def _unpack(slab, shapes):
    flat = slab.reshape(-1)
    out, pos = [], 0
    for s in shapes:
        size = math.prod(s)
        out.append(flat[pos:pos + size].reshape(s))
        pos += size
    return out


def kernel(x, positions, norm_mix_pre, norm_mix_post, norm_ffn_pre, norm_ffn_post, w_in_even, lb_table, a_norm, b_ln_g, b_ln_b, b_ws, b_bias, w_out_even, w_in_odd, w_out_odd, w_ff1, w_ff2, loss_target, m_norm_mix_pre, m_norm_mix_post, m_norm_ffn_pre, m_norm_ffn_post, m_w_in_even, m_lb_table, m_a_norm, m_b_ln_g, m_b_ln_b, m_b_ws, m_b_bias, m_w_out_even, m_w_in_odd, m_w_out_odd, m_w_ff1, m_w_ff2, v_norm_mix_pre, v_norm_mix_post, v_norm_ffn_pre, v_norm_ffn_post, v_w_in_even, v_lb_table, v_a_norm, v_b_ln_g, v_b_ln_b, v_b_ws, v_b_bias, v_w_out_even, v_w_in_odd, v_w_out_odd, v_w_ff1, v_w_ff2):
    batch = x.shape[0]
    t = batch * SEQ
    d = D_MODEL
    x0 = x.reshape(t, d)
    target = loss_target.reshape(t, d)

    def gain(p, layer):
        return p[layer:layer + 1]

    def gather(*shards):
        return _Exchange("gather", [w.astype(BF16) for w in shards])

    def scatter(*grads):
        return _Exchange("scatter", grads)

    (win_e,) = exchange_alone(gather(w_in_even[0]), "gather_in_even")
    bias_t = b_bias[0].T
    proj, h0, w1_0 = norm_matmul(x0, gain(norm_mix_pre, 0), win_e, "in_proj_even", exchange=gather(w_ff1[0]))
    oa, states, decays, w2_0 = hgrn2_fwd(proj, lb_table, a_norm, batch, "hgrn2_fwd", exchange=gather(w_ff2[0]))
    mixin, wout_e = gmlp_fwd(proj, oa, b_ln_g, b_ln_b, b_ws[0], bias_t, "gmlp_fwd", exchange=gather(w_out_even[0]))
    mix0, x1 = out_proj(mixin, wout_e, x0, gain(norm_mix_post, 0), "out_proj_even")
    x2, hf0, a0, y0, win_o, wout_o = ffn_fwd(x1, gain(norm_ffn_pre, 0), w1_0, w2_0, gain(norm_ffn_post, 0),
                                             "ffn_fwd_0", exchange=gather(w_in_odd[0], w_out_odd[0]))
    x2p = _residue_major(x2, batch)
    qkv, h1 = norm_matmul(x2p, gain(norm_mix_pre, 1), win_o, "in_proj_odd")
    cos_t, sin_a, sin_b = rope_tables(_residue_major(positions.reshape(t, 1), batch), "rope_tables")
    ao, lse, w1_1, w2_1 = attn_fwd(qkv, cos_t, sin_a, sin_b, batch, "attn_fwd", exchange=gather(w_ff1[1], w_ff2[1]))
    mix1, x3 = out_proj(ao, wout_o, x2p, gain(norm_mix_post, 1), "out_proj_odd")
    dx4, hf1, a1, y1, loss_part = ffn_fwd(x3, gain(norm_ffn_pre, 1), w1_1, w2_1, gain(norm_ffn_post, 1),
                                          "ffn_fwd_1", target=_residue_major(target, batch))

    hc = D_FF // N_CHIPS
    dx3, dy1, da1, dg_fpre1, dg_fpost1 = ffn_bwd(
        dx4, x3, y1, a1, gain(norm_ffn_pre, 1), gain(norm_ffn_post, 1), w1_1, w2_1, "ffn_bwd_1")
    g_w1_1 = weight_grad(hf1, da1, "b", d, hc, False, "wgrad_ff1_1")
    g_w2_1 = weight_grad(a1, dy1, "a", hc, d, True, "wgrad_ff2_1")
    dmix1, dao, dg_mpost1 = out_proj_bwd(dx3, mix1, gain(norm_mix_post, 1), wout_o, "out_proj_bwd_odd")
    g_wout_o = weight_grad(ao, dmix1, "a", d // N_CHIPS, d, False, "wgrad_out_odd")
    dqkv, r_w1_1, r_w2_1, r_wout_o = attn_bwd(qkv, cos_t, sin_a, sin_b, ao, lse, dao, batch, "attn_bwd",
                                              exchange=scatter(g_w1_1, g_w2_1, g_wout_o))
    dx2p, dg_mpre1 = norm_matmul_bwd(dqkv, win_o, x2p, gain(norm_mix_pre, 1), dx3, "in_proj_bwd_odd")
    dx2 = _sequence_order(dx2p, batch)
    g_win_o = weight_grad_stacked(h1, dqkv, 3 * d // N_CHIPS, "wgrad_in_odd")
    dx1, dy0, da0, dg_fpre0, dg_fpost0, r_win_o = ffn_bwd(
        dx2, x1, y0, a0, gain(norm_ffn_pre, 0), gain(norm_ffn_post, 0), w1_0, w2_0, "ffn_bwd_0",
        exchange=scatter(g_win_o))
    g_w1_0 = weight_grad(hf0, da0, "b", d, hc, False, "wgrad_ff1_0")
    g_w2_0, r_w1_0 = weight_grad(a0, dy0, "a", hc, d, True, "wgrad_ff2_0", exchange=scatter(g_w1_0))
    dmix0, dmixin, dg_mpost0 = out_proj_bwd(dx1, mix0, gain(norm_mix_post, 0), wout_e, "out_proj_bwd_even")
    g_wout_e = weight_grad(mixin, dmix0, "a", d // N_CHIPS, d, False, "wgrad_out_even")
    dproj, d_lb, d_anorm, r_w2_0 = hgrn2_bwd(
        proj, states, decays, lb_table, a_norm, dmixin, batch, "hgrn2_bwd", exchange=scatter(g_w2_0))
    dproj, d_lng, d_lnb, d_ws, d_bias_t, r_wout_e = gmlp_bwd(
        proj, dmixin, b_ln_g, b_ln_b, b_ws[0], bias_t, dproj, "gmlp_bwd", exchange=scatter(g_wout_e))
    g_win_e = weight_grad(h0, dproj, "b", d, 3 * d // N_CHIPS, False, "wgrad_in_even")
    dx0, dg_mpre0, r_win_e = norm_matmul_bwd(dproj, win_e, x0, gain(norm_mix_pre, 0), dx1, "in_proj_bwd_even",
                                             exchange=scatter(g_win_e))
    grad_x = dx0.reshape(x.shape)

    s_w1 = reduce_slabs(r_w1_1, "reduce_ff1_1", part=1, parts=2)
    s_w1 = reduce_slabs(r_w1_0, "reduce_ff1_0", part=0, parts=2, into=s_w1)
    s_w2 = reduce_slabs(r_w2_1, "reduce_ff2_1", part=1, parts=2)
    s_w2 = reduce_slabs(r_w2_0, "reduce_ff2_0", part=0, parts=2, into=s_w2)
    sums = [reduce_slabs(r_win_e, "reduce_in_even"), reduce_slabs(r_wout_e, "reduce_out_even"),
            reduce_slabs(r_win_o, "reduce_in_odd"), reduce_slabs(r_wout_o, "reduce_out_odd"), s_w1, s_w2]
    sibling = sibling_swap(sums, "sibling_swap")
    big_w = [w_in_even, w_out_even, w_in_odd, w_out_odd, w_ff1, w_ff2]
    big_m = [m_w_in_even, m_w_out_even, m_w_in_odd, m_w_out_odd, m_w_ff1, m_w_ff2]
    big_v = [v_w_in_even, v_w_out_even, v_w_in_odd, v_w_out_odd, v_w_ff1, v_w_ff2]
    big = []
    for i, (w, m, v) in enumerate(zip(big_w, big_m, big_v)):
        two_d = (-1, w.shape[-1])
        res = adamw_big(w.reshape(two_d), sums[i], sibling[i], m.reshape(two_d), v.reshape(two_d), "adamw_big_%d" % i)
        big.append([r.reshape(w.shape) for r in res])

    small_w = [norm_mix_pre, norm_mix_post, norm_ffn_pre, norm_ffn_post, lb_table, a_norm, b_ln_g, b_ln_b, b_ws, b_bias]
    small_m = [m_norm_mix_pre, m_norm_mix_post, m_norm_ffn_pre, m_norm_ffn_post, m_lb_table, m_a_norm, m_b_ln_g,
               m_b_ln_b, m_b_ws, m_b_bias]
    small_v = [v_norm_mix_pre, v_norm_mix_post, v_norm_ffn_pre, v_norm_ffn_post, v_lb_table, v_a_norm, v_b_ln_g,
               v_b_ln_b, v_b_ws, v_b_bias]
    partial = [jnp.concatenate([dg_mpre0, dg_mpre1]), jnp.concatenate([dg_mpost0, dg_mpost1]),
               jnp.concatenate([dg_fpre0, dg_fpre1]), jnp.concatenate([dg_fpost0, dg_fpost1]),
               d_lb, d_anorm, d_lng, d_lnb, d_ws[None], d_bias_t.T[None]]
    *small_g, loss = _unpack(allreduce_small(_pack(partial + [loss_part]), "allreduce_small"),
                             [w.shape for w in small_w] + [()])
    small_d, small_nm, small_nv = adamw_small(small_w, small_g, small_m, small_v, "adamw_small")

    order = ["norm_mix_pre", "norm_mix_post", "norm_ffn_pre", "norm_ffn_post", "w_in_even", "lb_table", "a_norm",
             "b_ln_g", "b_ln_b", "b_ws", "b_bias", "w_out_even", "w_in_odd", "w_out_odd", "w_ff1", "w_ff2"]
    small_names = ["norm_mix_pre", "norm_mix_post", "norm_ffn_pre", "norm_ffn_post", "lb_table", "a_norm",
                   "b_ln_g", "b_ln_b", "b_ws", "b_bias"]
    big_names = ["w_in_even", "w_out_even", "w_in_odd", "w_out_odd", "w_ff1", "w_ff2"]
    grads, deltas, new_m, new_v = {}, {}, {}, {}
    for i, nm in enumerate(small_names):
        grads[nm], deltas[nm], new_m[nm], new_v[nm] = small_g[i], small_d[i], small_nm[i], small_nv[i]
    for i, nm in enumerate(big_names):
        grads[nm], deltas[nm], new_m[nm], new_v[nm] = big[i]
    return (loss, grad_x, *[grads[n] for n in order], *[deltas[n] for n in order],
            *[new_m[n] for n in order], *[new_v[n] for n in order])
```

```python
import functools
import math

import jax
import jax.numpy as jnp
from jax import lax
from jax.experimental import pallas as pl
from jax.experimental.pallas import tpu as pltpu

F32 = jnp.float32
BF16 = jnp.bfloat16
MESH = pl.DeviceIdType.MESH

D_MODEL = 1024
SEQ = 2048
D_FF = 4096
N_CHIPS = 4
A_WIDTH = 512
A_HEADS = 4
A_DK = 128
A_CHUNK = 64
A_SUB = 16
B_WIDTH = 512
B_GROUPS = 4
B_CHUNK = 128
C_HEADS = 16
C_HEAD_DIM = 64
C_ROT_HALF = 8
C_BLOCK = 128
C_DILATIONS = (1, 4, 16)
ROPE_THETA = 500000.0
EPS = 1e-6
ADAM_LR = 0.001
ADAM_B1 = 0.9
ADAM_B2 = 0.999
ADAM_EPS = 1e-08
ADAM_WD = 0.01
ADAM_STEP = 10

ROW_TILE = 512
FFN_ROWS = 1024
WGRAD_ROWS = 2048
VMEM_LIMIT = 56 * 1024 * 1024
NEG_BIG = -1e30


def _params(sem=None):
    return pltpu.CompilerParams(dimension_semantics=sem, vmem_limit_bytes=VMEM_LIMIT)


def _dot(a, b):
    return jnp.dot(a, b, preferred_element_type=F32)


def _dot_nt(a, b):
    return lax.dot_general(a, b, (((1,), (1,)), ((), ())), preferred_element_type=F32)


def _dot_tn(a, b):
    return lax.dot_general(a, b, (((0,), (0,)), ((), ())), preferred_element_type=F32)


def _rms(x, g):
    r = lax.rsqrt(jnp.mean(x * x, axis=-1, keepdims=True) + EPS)
    return x * r * g


def _rms_bwd(x, g, dy):
    r = lax.rsqrt(jnp.mean(x * x, axis=-1, keepdims=True) + EPS)
    xh = x * r
    dg = jnp.sum(dy * xh, axis=0, keepdims=True)
    dxh = dy * g
    dx = r * (dxh - xh * jnp.mean(dxh * xh, axis=-1, keepdims=True))
    return dx, dg


def _accumulate(ref, val, first):
    @pl.when(first)
    def _():
        ref[...] = val

    @pl.when(jnp.logical_not(first))
    def _():
        ref[...] += val


N_DEV = 8
ANY = pl.BlockSpec(memory_space=pl.ANY)


def _place():
    x, y, c = lax.axis_index("x"), lax.axis_index("y"), lax.axis_index("c")
    return x, y, c, [(1 - x, y), (x, 1 - y), (1 - x, 1 - y)]


class _Exchange:
    def __init__(self, kind, arrays):
        self.kind, self.arrays, self.n = kind, list(arrays), len(arrays)
        per_peer = pltpu.SemaphoreType.DMA((3 * self.n,))
        if kind == "gather":
            self.out_shape = [jax.ShapeDtypeStruct((N_CHIPS,) + a.shape, a.dtype) for a in self.arrays]
            self.scratch = [per_peer, per_peer, pltpu.SemaphoreType.DMA((self.n,)), per_peer, per_peer]
        else:
            self.out_shape = [jax.ShapeDtypeStruct(a.shape, a.dtype) for a in self.arrays]
            self.scratch = [per_peer, per_peer, pltpu.SemaphoreType.DMA((self.n,))]

    def _copies(self, ins, outs, sems):
        send_sems, recv_sems, local_sems = sems[:3]
        x, y, c, chips = _place()
        me = 2 * x + y
        local, remote = [], []
        for a in range(self.n):
            if self.kind == "gather":
                local.append(pltpu.make_async_copy(ins[a], outs[a].at[me], local_sems.at[a]))
                half = self.arrays[a].shape[0] // 2

                def rows(ref, core, half=half):
                    return ref.at[pl.ds(core * half, half)]
            else:
                local.append(pltpu.make_async_copy(ins[a].at[me], outs[a].at[3], local_sems.at[a]))
            for j, (px, py) in enumerate(chips):
                k = 3 * a + j
                peer = 2 * px + py

                def copy(src, dst, to, send_sem=send_sems.at[k], recv_sem=recv_sems.at[k]):
                    return pltpu.make_async_remote_copy(src_ref=src, dst_ref=dst, send_sem=send_sem, recv_sem=recv_sem,
                                                        device_id=to, device_id_type=MESH)

                if self.kind == "gather":
                    sent = copy(rows(ins[a], c), rows(outs[a].at[me], c), (px, py, c))
                    landed = copy(rows(ins[a], c), rows(outs[a].at[peer], c), (px, py, c))
                    on = dict(send_sem=sems[3].at[k], recv_sem=sems[4].at[k])
                    passed = copy(rows(outs[a].at[peer], c), rows(outs[a].at[peer], c), (x, y, 1 - c), **on)
                    handed = copy(rows(outs[a].at[peer], c), rows(outs[a].at[peer], 1 - c), (x, y, 1 - c), **on)
                    remote.append((sent, landed, passed, handed))
                else:
                    sent = copy(ins[a].at[peer], outs[a].at[j], (px, py, c))
                    remote.append((sent, sent, None, None))
        return local, remote

    def start(self, ins, outs, sems):
        local, remote = self._copies(ins, outs, sems)
        for cp in local:
            cp.start()
        for sent, _, _, _ in remote:
            sent.start()

    def finish(self, ins, outs, sems):
        local, remote = self._copies(ins, outs, sems)
        for _, landed, passed, _ in remote:
            landed.wait_recv()
            if passed is not None:
                passed.start()
        for sent, _, passed, handed in remote:
            if passed is not None:
                handed.wait_recv()
                passed.wait_send()
            sent.wait_send()
        for cp in local:
            cp.wait()


def _call(body, *, name, grid, in_specs, out_specs, out_shape, args, scratch_shapes=(), aliases=None, exchange=None):
    if exchange is None:
        return pl.pallas_call(
            body, name=name, grid=grid, in_specs=in_specs, out_specs=out_specs, out_shape=out_shape,
            scratch_shapes=list(scratch_shapes), input_output_aliases=aliases or {},
            compiler_params=_params(("arbitrary",) * len(grid)))(*args)
    n_in, n_out, n_scr, n_ex = len(in_specs), len(out_specs), len(scratch_shapes), exchange.n
    steps = grid

    def wrapped(*refs):
        ins, refs = refs[:n_in], refs[n_in:]
        ex_in, refs = refs[:n_ex], refs[n_ex:]
        outs, refs = refs[:n_out], refs[n_out:]
        ex_out, refs = refs[:n_ex], refs[n_ex:]
        scr, sems = refs[:n_scr], refs[n_scr:]
        first = functools.reduce(jnp.logical_and, [pl.program_id(k) == 0 for k in range(len(steps))])
        last = functools.reduce(jnp.logical_and, [pl.program_id(k) == steps[k] - 1 for k in range(len(steps))])

        @pl.when(first)
        def _():
            exchange.start(ex_in, ex_out, sems)

        body(*ins, *outs, *scr)

        @pl.when(last)
        def _():
            exchange.finish(ex_in, ex_out, sems)

    return pl.pallas_call(
        wrapped, name=name, grid=grid,
        in_specs=list(in_specs) + [ANY] * n_ex, out_specs=list(out_specs) + [ANY] * n_ex,
        out_shape=list(out_shape) + exchange.out_shape,
        scratch_shapes=list(scratch_shapes) + exchange.scratch, input_output_aliases=aliases or {},
        compiler_params=_params(("arbitrary",) * len(grid)))(*args, *exchange.arrays)


def exchange_alone(exchange, name):
    def body(*refs):
        n = exchange.n
        exchange.start(refs[:n], refs[n:2 * n], refs[2 * n:])
        exchange.finish(refs[:n], refs[n:2 * n], refs[2 * n:])

    return pl.pallas_call(
        body, name=name, in_specs=[ANY] * exchange.n, out_specs=[ANY] * exchange.n,
        out_shape=exchange.out_shape, scratch_shapes=exchange.scratch)(*exchange.arrays)


def norm_matmul(x, g, wg, name, exchange=None):
    t, d = x.shape
    nl = wg.shape[2]

    def body(x_ref, g_ref, w_ref, o_ref, h_ref):
        h = _rms(x_ref[...], g_ref[...]).astype(BF16)
        h_ref[...] = h
        for c in range(N_CHIPS):
            o_ref[:, c * nl:(c + 1) * nl] = _dot(h, w_ref[c])

    return _call(
        body, name=name, grid=(t // ROW_TILE,),
        in_specs=[pl.BlockSpec((ROW_TILE, d), lambda i: (i, 0)),
                  pl.BlockSpec((1, d), lambda i: (0, 0)),
                  pl.BlockSpec((N_CHIPS, d, nl), lambda i: (0, 0, 0))],
        out_specs=[pl.BlockSpec((ROW_TILE, N_CHIPS * nl), lambda i: (i, 0)),
                   pl.BlockSpec((ROW_TILE, d), lambda i: (i, 0))],
        out_shape=[jax.ShapeDtypeStruct((t, N_CHIPS * nl), F32), jax.ShapeDtypeStruct((t, d), BF16)],
        args=(x, g, wg), exchange=exchange)


def norm_matmul_bwd(dproj, wg, x, g, dres, name, exchange=None):
    t, d = x.shape
    nl = wg.shape[2]
    stacked = dproj.ndim == 3
    piece = math.gcd(nl, dproj.shape[-1])

    def body(dp_ref, w_ref, x_ref, g_ref, dres_ref, dx_ref, dg_ref):
        dh = None
        for j in range(N_CHIPS * nl // piece):
            c, off = divmod(j * piece, nl)
            if stacked:
                p, lo = divmod(j * piece, dproj.shape[-1])
                lhs = dp_ref[p, :, lo:lo + piece]
            else:
                lhs = dp_ref[:, j * piece:(j + 1) * piece]
            part = _dot_nt(lhs.astype(BF16), w_ref[c, :, off:off + piece])
            dh = part if dh is None else dh + part
        dx, dg = _rms_bwd(x_ref[...], g_ref[...], dh)
        dx_ref[...] = dres_ref[...] + dx
        _accumulate(dg_ref, dg, pl.program_id(0) == 0)

    row = pl.BlockSpec((ROW_TILE, d), lambda i: (i, 0))
    vec = pl.BlockSpec((1, d), lambda i: (0, 0))
    if stacked:
        dp_spec = pl.BlockSpec((dproj.shape[0], ROW_TILE, dproj.shape[-1]), lambda i: (0, i, 0))
    else:
        dp_spec = pl.BlockSpec((ROW_TILE, N_CHIPS * nl), lambda i: (i, 0))
    return _call(
        body, name=name, grid=(t // ROW_TILE,),
        in_specs=[dp_spec, pl.BlockSpec((N_CHIPS, d, nl), lambda i: (0, 0, 0)), row, vec, row],
        out_specs=[row, vec],
        out_shape=[jax.ShapeDtypeStruct((t, d), F32), jax.ShapeDtypeStruct((1, d), F32)],
        args=(dproj, wg, x, g, dres), exchange=exchange)


def out_proj(a, wg, x, g, name):
    t, d = x.shape
    kl = wg.shape[1]

    def body(a_ref, w_ref, x_ref, g_ref, mix_ref, xo_ref):
        acc = _dot(a_ref[:, 0:kl], w_ref[0])
        for c in range(1, N_CHIPS):
            acc += _dot(a_ref[:, c * kl:(c + 1) * kl], w_ref[c])
        mix_ref[...] = acc
        xo_ref[...] = x_ref[...] + _rms(acc, g_ref[...])

    row = pl.BlockSpec((ROW_TILE, d), lambda i: (i, 0))
    return pl.pallas_call(
        body, name=name, grid=(t // ROW_TILE,),
        in_specs=[row, pl.BlockSpec((N_CHIPS, kl, d), lambda i: (0, 0, 0)), row,
                  pl.BlockSpec((1, d), lambda i: (0, 0))],
        out_specs=[row, row],
        out_shape=[jax.ShapeDtypeStruct((t, d), F32), jax.ShapeDtypeStruct((t, d), F32)],
        compiler_params=_params(("arbitrary",)),
    )(a, wg, x, g)


def out_proj_bwd(dxo, mix, g, wg, name):
    t, d = mix.shape
    kl = wg.shape[1]

    def body(dxo_ref, mix_ref, g_ref, w_ref, dmix_ref, da_ref, dg_ref):
        dmix, dg = _rms_bwd(mix_ref[...], g_ref[...], dxo_ref[...])
        dmb = dmix.astype(BF16)
        dmix_ref[...] = dmb
        for c in range(N_CHIPS):
            da_ref[:, c * kl:(c + 1) * kl] = _dot_nt(dmb, w_ref[c])
        _accumulate(dg_ref, dg, pl.program_id(0) == 0)

    row = pl.BlockSpec((ROW_TILE, d), lambda i: (i, 0))
    vec = pl.BlockSpec((1, d), lambda i: (0, 0))
    return pl.pallas_call(
        body, name=name, grid=(t // ROW_TILE,),
        in_specs=[row, row, vec, pl.BlockSpec((N_CHIPS, kl, d), lambda i: (0, 0, 0))],
        out_specs=[row, row, vec],
        out_shape=[jax.ShapeDtypeStruct((t, d), BF16), jax.ShapeDtypeStruct((t, d), F32),
                   jax.ShapeDtypeStruct((1, d), F32)],
        compiler_params=_params(("arbitrary",)),
    )(dxo, mix, g, wg)


def ffn_fwd(x, gpre, w1g, w2g, gpost, name, exchange=None, target=None):
    t, d = x.shape
    hc = w1g.shape[2]
    with_loss = target is not None

    def body(x_ref, gpre_ref, w1_ref, w2_ref, gpost_ref, *rest):
        if with_loss:
            t_ref, xo_ref, h_ref, a_ref, y_ref, l_ref, acc = rest
        else:
            xo_ref, h_ref, a_ref, y_ref, acc = rest
        i, c = pl.program_id(0), pl.program_id(1)

        @pl.when(c == 0)
        def _():
            h_ref[...] = _rms(x_ref[...], gpre_ref[...]).astype(BF16)

        a = _dot(h_ref[...], w1_ref[...])
        a_ref[...] = a.astype(BF16)
        r = jnp.square(jnp.maximum(a, 0.0)).astype(BF16)
        _accumulate(acc, _dot(r, w2_ref[...]), c == 0)

        @pl.when(c == N_CHIPS - 1)
        def _():
            y = acc[...]
            y_ref[...] = y
            xo = x_ref[...] + _rms(y, gpost_ref[...])
            if with_loss:
                e = xo - t_ref[...]
                xo_ref[...] = e * (1.0 / d)
                part = jnp.sum(jnp.sum(e * e, axis=-1, keepdims=True), axis=0, keepdims=True) * (0.5 / d)
                _accumulate(l_ref, part, i == 0)
            else:
                xo_ref[...] = xo

    row = pl.BlockSpec((FFN_ROWS, d), lambda i, c: (i, 0))
    vec = pl.BlockSpec((1, d), lambda i, c: (0, 0))
    one = pl.BlockSpec((1, 1), lambda i, c: (0, 0))
    return _call(
        body, name=name, grid=(t // FFN_ROWS, N_CHIPS),
        in_specs=[row, vec,
                  pl.BlockSpec((None, d, hc), lambda i, c: (c, 0, 0)),
                  pl.BlockSpec((None, hc, d), lambda i, c: (c, 0, 0)), vec] + ([row] if with_loss else []),
        out_specs=[row, row, pl.BlockSpec((FFN_ROWS, hc), lambda i, c: (i, c)), row] + ([one] if with_loss else []),
        out_shape=[jax.ShapeDtypeStruct((t, d), F32), jax.ShapeDtypeStruct((t, d), BF16),
                   jax.ShapeDtypeStruct((t, N_CHIPS * hc), BF16), jax.ShapeDtypeStruct((t, d), F32)]
        + ([jax.ShapeDtypeStruct((1, 1), F32)] if with_loss else []),
        scratch_shapes=[pltpu.VMEM((FFN_ROWS, d), F32)],
        args=(x, gpre, w1g, w2g, gpost) + ((target,) if with_loss else ()), exchange=exchange)


def ffn_bwd(dxo, x, y, a, gpre, gpost, w1g, w2g, name, exchange=None):
    t, d = x.shape
    hc = w1g.shape[2]

    def body(dxo_ref, x_ref, y_ref, a_ref, gpre_ref, gpost_ref, w1_ref, w2_ref,
             dxi_ref, dy_ref, da_ref, dgpre_ref, dgpost_ref, acc):
        i, c = pl.program_id(0), pl.program_id(1)

        @pl.when(c == 0)
        def _():
            dy, dg = _rms_bwd(y_ref[...], gpost_ref[...], dxo_ref[...])
            dy_ref[...] = dy.astype(BF16)
            _accumulate(dgpost_ref, dg, i == 0)

        dr = _dot_nt(dy_ref[...], w2_ref[...])
        da = (dr * (2.0 * jnp.maximum(a_ref[...].astype(F32), 0.0))).astype(BF16)
        da_ref[...] = da
        _accumulate(acc, _dot_nt(da, w1_ref[...]), c == 0)

        @pl.when(c == N_CHIPS - 1)
        def _():
            dx, dg = _rms_bwd(x_ref[...], gpre_ref[...], acc[...])
            dxi_ref[...] = dxo_ref[...] + dx
            _accumulate(dgpre_ref, dg, i == 0)

    row = pl.BlockSpec((ROW_TILE, d), lambda i, c: (i, 0))
    vec = pl.BlockSpec((1, d), lambda i, c: (0, 0))
    hid = pl.BlockSpec((ROW_TILE, hc), lambda i, c: (i, c))
    return _call(
        body, name=name, grid=(t // ROW_TILE, N_CHIPS),
        in_specs=[row, row, row, hid, vec, vec,
                  pl.BlockSpec((None, d, hc), lambda i, c: (c, 0, 0)),
                  pl.BlockSpec((None, hc, d), lambda i, c: (c, 0, 0))],
        out_specs=[row, row, hid, vec, vec],
        out_shape=[jax.ShapeDtypeStruct((t, d), F32), jax.ShapeDtypeStruct((t, d), BF16),
                   jax.ShapeDtypeStruct((t, N_CHIPS * hc), BF16),
                   jax.ShapeDtypeStruct((1, d), F32), jax.ShapeDtypeStruct((1, d), F32)],
        scratch_shapes=[pltpu.VMEM((ROW_TILE, d), F32)],
        args=(dxo, x, y, a, gpre, gpost, w1g, w2g), exchange=exchange)


def weight_grad(a, b, chunked, bk, bn, relu2, name, exchange=None):
    t = a.shape[0]
    a_on = chunked == "a"
    rows = min(t, WGRAD_ROWS)
    n_steps = t // rows

    def body(a_ref, b_ref, o_ref, acc):
        s = pl.program_id(1)
        av = a_ref[...]
        if relu2:
            av = jnp.square(jnp.maximum(av.astype(F32), 0.0))
        _accumulate(acc, _dot_tn(av.astype(BF16), b_ref[...].astype(BF16)), s == 0)

        @pl.when(s == n_steps - 1)
        def _():
            o_ref[...] = acc[...].astype(BF16)

    res = _call(
        body, name=name, grid=(N_CHIPS, n_steps),
        in_specs=[pl.BlockSpec((rows, bk), (lambda c, s: (s, c)) if a_on else (lambda c, s: (s, 0))),
                  pl.BlockSpec((rows, bn), (lambda c, s: (s, 0)) if a_on else (lambda c, s: (s, c)))],
        out_specs=[pl.BlockSpec((None, bk, bn), lambda c, s: (c, 0, 0))],
        out_shape=[jax.ShapeDtypeStruct((N_CHIPS, bk, bn), BF16)],
        scratch_shapes=[pltpu.VMEM((bk, bn), F32)],
        args=(a, b), exchange=exchange)
    return res[0] if exchange is None else res


def weight_grad_stacked(a, b3, bn, name):
    t, bk = a.shape
    width = b3.shape[-1]
    piece = math.gcd(bn, width)
    rows = min(t, WGRAD_ROWS)
    n_steps = t // rows

    def body(a_ref, b_ref, o_hbm, acc, staged, sem):
        s, c = pl.program_id(0), pl.program_id(1)
        av = a_ref[...].astype(BF16)
        for chunk in range(N_CHIPS):
            @pl.when(c == chunk)
            def _(chunk=chunk):
                cols = [divmod(chunk * bn + k * piece, width) for k in range(bn // piece)]
                b = jnp.concatenate([b_ref[p, :, lo:lo + piece] for p, lo in cols], axis=1).astype(BF16)
                _accumulate(acc.at[chunk], _dot_tn(av, b), s == 0)

                @pl.when(s == n_steps - 1)
                def _():
                    staged[...] = acc[chunk].astype(BF16)
                    copy = pltpu.make_async_copy(staged, o_hbm.at[chunk], sem)
                    copy.start()
                    copy.wait()

    return pl.pallas_call(
        body, name=name, grid=(n_steps, N_CHIPS),
        in_specs=[pl.BlockSpec((rows, bk), lambda s, c: (s, 0)),
                  pl.BlockSpec((b3.shape[0], rows, width), lambda s, c: (0, s, 0))],
        out_specs=ANY,
        out_shape=jax.ShapeDtypeStruct((N_CHIPS, bk, bn), BF16),
        scratch_shapes=[pltpu.VMEM((N_CHIPS, bk, bn), F32), pltpu.VMEM((bk, bn), BF16), pltpu.SemaphoreType.DMA],
        compiler_params=_params(("arbitrary", "arbitrary")),
    )(a, b3)


def _hgrn2_chunk(st, qs, fls, ivs, gls, l0, l1, l2, ng):
    nsub = len(qs)
    mx = jnp.maximum(jnp.maximum(l0, l1), l2)
    e0, e1, e2 = jnp.exp(l0 - mx), jnp.exp(l1 - mx), jnp.exp(l2 - mx)
    lb = e0 / (e0 + e1 + e2)
    rows = lax.broadcasted_iota(jnp.int32, (A_SUB, A_SUB), 0)
    cols = lax.broadcasted_iota(jnp.int32, (A_SUB, A_SUB), 1)
    tri = (rows >= cols).astype(F32)
    keep = (lax.broadcasted_iota(jnp.int32, (A_SUB, A_SUB, A_DK), 0)
            >= lax.broadcasted_iota(jnp.int32, (A_SUB, A_SUB, A_DK), 1))
    base = jnp.zeros_like(l0)
    bases, gs, ks, qfs = [], [], [], []
    for i in range(nsub):
        f = lb + (1.0 - lb) * jax.nn.sigmoid(fls[i])
        logf = jnp.log(f)
        bases.append(base)
        gs.append(base + jnp.dot(tri, logf, precision=lax.Precision.HIGHEST, preferred_element_type=F32))
        base = base + jnp.sum(logf, axis=0, keepdims=True)
        ks.append(1.0 - f)
        qfs.append(jax.nn.silu(qs[i]))
    g_last = base
    stb = st.astype(BF16)
    outs = []
    for i in range(nsub):
        o = _dot_nt((qfs[i] * jnp.exp(gs[i])).astype(BF16), stb)
        if i > 0:
            qt = (qfs[i] * jnp.exp(gs[i] - bases[i])).astype(BF16)
            kk = jnp.concatenate([ks[j] * jnp.exp(bases[i] - gs[j]) for j in range(i)], axis=0).astype(BF16)
            vv = jnp.concatenate(ivs[:i], axis=0).astype(BF16)
            o = o + _dot(_dot_nt(qt, kk).astype(BF16), vv)
        dec = jnp.exp(jnp.where(keep, gs[i][:, None, :] - gs[i][None, :, :], NEG_BIG))
        s_diag = jnp.sum(qfs[i][:, None, :] * ks[i][None, :, :] * dec, axis=-1)
        o = o + _dot(s_diag.astype(BF16), ivs[i].astype(BF16))
        o = o * lax.rsqrt(jnp.mean(o * o, axis=-1, keepdims=True) + EPS) * ng
        outs.append(o * jax.nn.silu(gls[i]))
    kdec = jnp.concatenate([ks[j] * jnp.exp(g_last - gs[j]) for j in range(nsub)], axis=0).astype(BF16)
    vall = jnp.concatenate(ivs, axis=0).astype(BF16)
    new_st = st * jnp.exp(g_last) + _dot_tn(vall, kdec)
    return new_st, outs


A_MAX_LOG_DECAY = 60.0


def _half_sums(logf):
    n = logf.shape[0]
    first = lax.broadcasted_iota(jnp.int32, logf.shape, 0) < n // 2
    return (jnp.sum(jnp.where(first, logf, 0.0), axis=0, keepdims=True),
            jnp.sum(jnp.where(first, 0.0, logf), axis=0, keepdims=True))


def _split3(x):
    hi = x.astype(BF16)
    r1 = x - hi.astype(F32)
    mid = r1.astype(BF16)
    return hi, mid, (r1 - mid.astype(F32)).astype(BF16)


def _tri_matmul(x, transpose):
    n = x.shape[0]
    r = lax.broadcasted_iota(jnp.int32, (n, n), 0)
    c = lax.broadcasted_iota(jnp.int32, (n, n), 1)
    tri = ((r <= c) if transpose else (r >= c)).astype(BF16)
    hi, mid, lo = _split3(x)
    return (_dot(tri, lo) + _dot(tri, mid)) + _dot(tri, hi)


@jax.custom_vjp
def _cumsum_rows(x):
    return _tri_matmul(x, False)


def _cumsum_rows_fwd(x):
    return _tri_matmul(x, False), None


def _cumsum_rows_bwd(_, dy):
    return (_tri_matmul(dy, True),)


_cumsum_rows.defvjp(_cumsum_rows_fwd, _cumsum_rows_bwd)


def _lower_bound(l0, l1, l2):
    mx = jnp.maximum(jnp.maximum(l0, l1), l2)
    e0, e1, e2 = jnp.exp(l0 - mx), jnp.exp(l1 - mx), jnp.exp(l2 - mx)
    return e0 / (e0 + e1 + e2)


def _b(x):
    return x.astype(BF16)


@jax.custom_vjp
def _mm(a, b):
    return _dot(_b(a), _b(b))


_mm.defvjp(lambda a, b: (_mm(a, b), (a, b)),
           lambda res, d: (_dot_nt(_b(d), _b(res[1])), _dot_tn(_b(res[0]), _b(d))))


@jax.custom_vjp
def _mm_nt(a, b):
    return _dot_nt(_b(a), _b(b))


_mm_nt.defvjp(lambda a, b: (_mm_nt(a, b), (a, b)),
              lambda res, d: (_dot(_b(d), _b(res[1])), _dot_tn(_b(d), _b(res[0]))))


def _dot_split(dot, a, b):
    ah, bh = _b(a), _b(b)
    al, bl = _b(a - ah.astype(F32)), _b(b - bh.astype(F32))
    return (dot(ah, bl) + dot(al, bh)) + dot(ah, bh)


@jax.custom_vjp
def _mm_scores(a, b):
    return _dot_nt(_b(a), _b(b))


_mm_scores.defvjp(lambda a, b: (_mm_scores(a, b), (a, b)),
                  lambda res, d: (_dot_split(_dot, d, res[1]), _dot_split(_dot_tn, d, res[0])))


@jax.custom_vjp
def _mm_tn(a, b):
    return _dot_tn(_b(a), _b(b))


_mm_tn.defvjp(lambda a, b: (_mm_tn(a, b), (a, b)),
              lambda res, d: (_dot_nt(_b(res[1]), _b(d)), _dot(_b(res[0]), _b(d))))


@jax.custom_vjp
def _split_heads(x):
    return tuple(x[:, h * A_DK:(h + 1) * A_DK] for h in range(A_HEADS))


def _split_heads_fwd(x):
    return _split_heads(x), None


def _split_heads_bwd(_, parts):
    return (jnp.concatenate(parts, axis=1),)


_split_heads.defvjp(_split_heads_fwd, _split_heads_bwd)


def _hgrn2_chunk_fast(sts, q, fl, iv, gl, l0, l1, l2, ng):
    lb = _lower_bound(l0, l1, l2)
    f = lb + (1.0 - lb) * jax.nn.sigmoid(fl)
    return _hgrn2_fast_core(sts, q, f, jnp.log(f), iv, gl, ng)


def _hgrn2_fast_core(sts, q, f, logf, iv, gl, ng):
    g = _cumsum_rows(logf)
    g_mid, g_last = _half_sums(logf)
    g_last = g_mid + g_last
    k = 1.0 - f
    qf = jax.nn.silu(q)
    qms = _split_heads(qf * jnp.exp(g - g_mid))
    kms = _split_heads(k * jnp.exp(g_mid - g))
    qgs = _split_heads(qf * jnp.exp(g))
    kds = _split_heads(k * jnp.exp(g_last - g))
    ivs = _split_heads(iv)
    decays = _split_heads(jnp.exp(g_last))
    n = q.shape[0]
    causal = lax.broadcasted_iota(jnp.int32, (n, n), 0) >= lax.broadcasted_iota(jnp.int32, (n, n), 1)
    raw = [_mm_scores(qm, km) for qm, km in zip(qms, kms)]
    inter = [_mm_nt(qg, st) for qg, st in zip(qgs, sts)]
    scores = [jnp.where(causal, s, 0.0) for s in raw]
    os = [a + _mm(s, v) for a, s, v in zip(inter, scores, ivs)]
    new_sts = [st * d + _mm_tn(v, kd) for st, d, v, kd in zip(sts, decays, ivs, kds)]
    os = [o * lax.rsqrt(jnp.mean(o * o, axis=-1, keepdims=True) + EPS) for o in os]
    return new_sts, jnp.concatenate(os, axis=1) * ng * jax.nn.silu(gl)


A_STEP_CHUNKS = 4


def _chunk_rows(j):
    return pl.ds(pl.multiple_of(j * A_CHUNK, A_CHUNK), A_CHUNK)


def _sub_rows(j, i):
    return pl.ds(pl.multiple_of(j * A_CHUNK + i * A_SUB, A_SUB), A_SUB)


def _sub_blocks(ref, head, j):
    lanes = slice(head * A_DK, (head + 1) * A_DK)
    return [ref[_sub_rows(j, i), lanes] for i in range(A_CHUNK // A_SUB)]


def hgrn2_fwd(proj, lb_table, a_norm, batch, name, exchange=None):
    t = proj.shape[0]
    n_steps = t // batch // (A_CHUNK * A_STEP_CHUNKS)
    rows = A_CHUNK * A_STEP_CHUNKS

    def body(q_ref, f_ref, i_ref, g_ref, lb_ref, ng_ref, o_ref, st_ref, dec_ref, st):
        @pl.when(pl.program_id(1) == 0)
        def _():
            st[...] = jnp.zeros_like(st)

        def chunk(j, carry):
            r = _chunk_rows(j)
            st_ref[j] = st[...]
            lb = _lower_bound(lb_ref[0:1, :], lb_ref[1:2, :], lb_ref[2:3, :])
            f = lb + (1.0 - lb) * jax.nn.sigmoid(f_ref[r, :])
            logf = jnp.log(f)
            decay = jnp.minimum(*_half_sums(logf))
            dec_ref[j] = decay
            mild = jnp.min(decay) >= -A_MAX_LOG_DECAY

            @pl.when(mild)
            def _():
                new_sts, o = _hgrn2_fast_core([st[h] for h in range(A_HEADS)], q_ref[r, :], f, logf,
                                              i_ref[r, :], g_ref[r, :], ng_ref[...])
                for h in range(A_HEADS):
                    st[h] = new_sts[h]
                o_ref[r, :] = o.astype(BF16)

            @pl.when(jnp.logical_not(mild))
            def _():
                for h in range(A_HEADS):
                    lanes = slice(h * A_DK, (h + 1) * A_DK)
                    new_st, outs = _hgrn2_chunk(
                        st[h], _sub_blocks(q_ref, h, j), _sub_blocks(f_ref, h, j), _sub_blocks(i_ref, h, j),
                        _sub_blocks(g_ref, h, j), lb_ref[0:1, lanes], lb_ref[1:2, lanes], lb_ref[2:3, lanes],
                        ng_ref[:, lanes])
                    st[h] = new_st
                    for i, o in enumerate(outs):
                        o_ref[_sub_rows(j, i), lanes] = o.astype(BF16)

            return carry

        lax.fori_loop(0, A_STEP_CHUNKS, chunk, 0)

    def part(k):
        return pl.BlockSpec((rows, A_WIDTH), lambda b, n: (b * n_steps + n, k))

    return _call(
        body, name=name, grid=(batch, n_steps),
        in_specs=[part(0), part(1), part(2), part(3),
                  pl.BlockSpec((3, A_WIDTH), lambda b, n: (0, 0)), pl.BlockSpec((1, A_WIDTH), lambda b, n: (0, 0))],
        out_specs=[part(0),
                   pl.BlockSpec((A_STEP_CHUNKS, A_HEADS, A_DK, A_DK), lambda b, n: (b * n_steps + n, 0, 0, 0)),
                   pl.BlockSpec((A_STEP_CHUNKS, 1, A_WIDTH), lambda b, n: (b * n_steps + n, 0, 0))],
        out_shape=[jax.ShapeDtypeStruct((t, A_WIDTH), BF16),
                   jax.ShapeDtypeStruct((t // A_CHUNK, A_HEADS, A_DK, A_DK), F32),
                   jax.ShapeDtypeStruct((t // A_CHUNK, 1, A_WIDTH), F32)],
        scratch_shapes=[pltpu.VMEM((A_HEADS, A_DK, A_DK), F32)],
        args=(proj, proj, proj, proj, lb_table, a_norm), exchange=exchange)


def hgrn2_bwd(proj, states, decays, lb_table, a_norm, do, batch, name, exchange=None):
    t = proj.shape[0]
    n_steps = t // batch // (A_CHUNK * A_STEP_CHUNKS)
    rows = A_CHUNK * A_STEP_CHUNKS

    def body(q_ref, f_ref, i_ref, g_ref, st_ref, dec_ref, lb_ref, ng_ref, do_ref, dp_ref, dlb_ref, dng_ref, dst):
        @pl.when(jnp.logical_and(pl.program_id(0) == 0, pl.program_id(1) == 0))
        def _():
            dlb_ref[...] = jnp.zeros_like(dlb_ref)
            dng_ref[...] = jnp.zeros_like(dng_ref)

        @pl.when(pl.program_id(1) == 0)
        def _():
            dst[...] = jnp.zeros_like(dst)

        def chunk(jj, carry):
            j = A_STEP_CHUNKS - 1 - jj
            r = _chunk_rows(j)
            mild = jnp.min(dec_ref[j]) >= -A_MAX_LOG_DECAY

            @pl.when(mild)
            def _():
                _, vjp = jax.vjp(
                    _hgrn2_chunk_fast, [st_ref[j, h] for h in range(A_HEADS)], q_ref[r, :], f_ref[r, :],
                    i_ref[r, :], g_ref[r, :], lb_ref[0:1, :], lb_ref[1:2, :], lb_ref[2:3, :], ng_ref[...])
                d_sts, dq, df, di, dg, dl0, dl1, dl2, dng = vjp(
                    ([dst[h] for h in range(A_HEADS)], do_ref[r, :].astype(F32)))
                for h in range(A_HEADS):
                    dst[h] = d_sts[h]
                for k, part in enumerate((dq, df, di, dg)):
                    dp_ref[r, k * A_WIDTH:(k + 1) * A_WIDTH] = part
                for row, val in enumerate((dl0, dl1, dl2)):
                    dlb_ref[row:row + 1, :] += val
                dng_ref[...] += dng

            @pl.when(jnp.logical_not(mild))
            def _():
                for h in range(A_HEADS):
                    lanes = slice(h * A_DK, (h + 1) * A_DK)
                    _, vjp = jax.vjp(
                        _hgrn2_chunk, st_ref[j, h], _sub_blocks(q_ref, h, j), _sub_blocks(f_ref, h, j),
                        _sub_blocks(i_ref, h, j), _sub_blocks(g_ref, h, j), lb_ref[0:1, lanes], lb_ref[1:2, lanes],
                        lb_ref[2:3, lanes], ng_ref[:, lanes])
                    douts = [x.astype(F32) for x in _sub_blocks(do_ref, h, j)]
                    d_st, dqs, dfs, dis, dgs, dl0, dl1, dl2, dng = vjp((dst[h], douts))
                    dst[h] = d_st
                    for k, parts in enumerate((dqs, dfs, dis, dgs)):
                        for i in range(A_CHUNK // A_SUB):
                            dp_ref[_sub_rows(j, i), k * A_WIDTH + h * A_DK:k * A_WIDTH + (h + 1) * A_DK] = parts[i]
                    for row, val in enumerate((dl0, dl1, dl2)):
                        dlb_ref[row:row + 1, lanes] += val
                    dng_ref[:, lanes] += dng

            return carry

        lax.fori_loop(0, A_STEP_CHUNKS, chunk, 0)

    def rev(b, n):
        return b * n_steps + (n_steps - 1 - n)

    def part(k):
        return pl.BlockSpec((rows, A_WIDTH), lambda b, n: (rev(b, n), k))

    const3 = pl.BlockSpec((3, A_WIDTH), lambda b, n: (0, 0))
    const1 = pl.BlockSpec((1, A_WIDTH), lambda b, n: (0, 0))
    return _call(
        body, name=name, grid=(batch, n_steps),
        in_specs=[part(0), part(1), part(2), part(3),
                  pl.BlockSpec((A_STEP_CHUNKS, A_HEADS, A_DK, A_DK), lambda b, n: (rev(b, n), 0, 0, 0)),
                  pl.BlockSpec((A_STEP_CHUNKS, 1, A_WIDTH), lambda b, n: (rev(b, n), 0, 0)),
                  const3, const1, part(0)],
        out_specs=[pl.BlockSpec((rows, 4 * A_WIDTH), lambda b, n: (rev(b, n), 0)), const3, const1],
        out_shape=[jax.ShapeDtypeStruct((t, 4 * A_WIDTH + 2 * B_WIDTH), F32),
                   jax.ShapeDtypeStruct((3, A_WIDTH), F32), jax.ShapeDtypeStruct((1, A_WIDTH), F32)],
        scratch_shapes=[pltpu.VMEM((A_HEADS, A_DK, A_DK), F32)],
        args=(proj, proj, proj, proj, states, decays, lb_table, a_norm, do), exchange=exchange)


B_GDIM = B_WIDTH // B_GROUPS
B_ROWS = 512


def _gmlp_chunk(ubs, vbs, lngs, lnbs, ws, bcols):
    vs = [jax.nn.gelu(v) for v in vbs]
    mu = sum(jnp.sum(v, axis=-1, keepdims=True) for v in vs) * (1.0 / B_WIDTH)
    var = sum(jnp.sum(jnp.square(v - mu), axis=-1, keepdims=True) for v in vs) * (1.0 / B_WIDTH)
    rstd = lax.rsqrt(var + EPS)
    tril = (lax.broadcasted_iota(jnp.int32, (B_CHUNK, B_CHUNK), 0)
            >= lax.broadcasted_iota(jnp.int32, (B_CHUNK, B_CHUNK), 1))
    outs = []
    for g in range(B_GROUPS):
        vn = (vs[g] - mu) * rstd * lngs[g] + lnbs[g]
        w = jnp.where(tril, ws[g], 0.0).astype(BF16)
        outs.append(jax.nn.gelu(ubs[g]) * (_dot(w, vn.astype(BF16)) + bcols[g]))
    return outs


def _gmlp_args(u_ref, v_ref, lng_ref, lnb_ref, w_ref, bt_ref, rows):
    def groups(ref):
        return [ref[rows, g * B_GDIM:(g + 1) * B_GDIM] for g in range(B_GROUPS)]

    def vec(ref):
        return [ref[:, g * B_GDIM:(g + 1) * B_GDIM] for g in range(B_GROUPS)]

    return (groups(u_ref), groups(v_ref), vec(lng_ref), vec(lnb_ref),
            [w_ref[g] for g in range(B_GROUPS)], [bt_ref[:, g:g + 1] for g in range(B_GROUPS)])


def gmlp_fwd(proj, oa, ln_g, ln_b, w, bias_t, name, exchange=None):
    t = proj.shape[0]

    def body(u_ref, v_ref, oa_ref, lng_ref, lnb_ref, w_ref, bt_ref, o_ref):
        o_ref[:, 0:A_WIDTH] = oa_ref[...]
        for n in range(B_ROWS // B_CHUNK):
            rows = slice(n * B_CHUNK, (n + 1) * B_CHUNK)
            outs = _gmlp_chunk(*_gmlp_args(u_ref, v_ref, lng_ref, lnb_ref, w_ref, bt_ref, rows))
            for g, o in enumerate(outs):
                o_ref[rows, A_WIDTH + g * B_GDIM:A_WIDTH + (g + 1) * B_GDIM] = o.astype(BF16)

    vec = pl.BlockSpec((1, B_WIDTH), lambda i: (0, 0))
    return _call(
        body, name=name, grid=(t // B_ROWS,),
        in_specs=[pl.BlockSpec((B_ROWS, B_WIDTH), lambda i: (i, 4)), pl.BlockSpec((B_ROWS, B_WIDTH), lambda i: (i, 5)),
                  pl.BlockSpec((B_ROWS, A_WIDTH), lambda i: (i, 0)), vec, vec,
                  pl.BlockSpec((B_GROUPS, B_CHUNK, B_CHUNK), lambda i: (0, 0, 0)),
                  pl.BlockSpec((B_CHUNK, B_GROUPS), lambda i: (0, 0))],
        out_specs=[pl.BlockSpec((B_ROWS, A_WIDTH + B_WIDTH), lambda i: (i, 0))],
        out_shape=[jax.ShapeDtypeStruct((t, A_WIDTH + B_WIDTH), BF16)],
        args=(proj, proj, oa, ln_g, ln_b, w, bias_t), exchange=exchange)


def gmlp_bwd(proj, dmixin, ln_g, ln_b, w, bias_t, dproj, name, exchange=None):
    t = proj.shape[0]

    def body(u_ref, v_ref, do_ref, lng_ref, lnb_ref, w_ref, bt_ref, dp_in_ref,
             dp_ref, dlng_ref, dlnb_ref, dw_ref, dbt_ref):
        del dp_in_ref

        @pl.when(pl.program_id(0) == 0)
        def _():
            for ref in (dlng_ref, dlnb_ref, dw_ref, dbt_ref):
                ref[...] = jnp.zeros_like(ref)

        for n in range(B_ROWS // B_CHUNK):
            rows = slice(n * B_CHUNK, (n + 1) * B_CHUNK)
            _, vjp = jax.vjp(_gmlp_chunk, *_gmlp_args(u_ref, v_ref, lng_ref, lnb_ref, w_ref, bt_ref, rows))
            douts = [do_ref[rows, g * B_GDIM:(g + 1) * B_GDIM] for g in range(B_GROUPS)]
            dus, dvs, dlngs, dlnbs, dws, dbs = vjp(douts)
            for g in range(B_GROUPS):
                lanes = slice(g * B_GDIM, (g + 1) * B_GDIM)
                dp_ref[rows, lanes] = dus[g]
                dp_ref[rows, B_WIDTH + g * B_GDIM:B_WIDTH + (g + 1) * B_GDIM] = dvs[g]
                dlng_ref[:, lanes] += dlngs[g]
                dlnb_ref[:, lanes] += dlnbs[g]
                dw_ref[g] += dws[g]
                dbt_ref[:, g:g + 1] += dbs[g]

    vec = pl.BlockSpec((1, B_WIDTH), lambda i: (0, 0))
    wspec = pl.BlockSpec((B_GROUPS, B_CHUNK, B_CHUNK), lambda i: (0, 0, 0))
    bspec = pl.BlockSpec((B_CHUNK, B_GROUPS), lambda i: (0, 0))
    return _call(
        body, name=name, grid=(t // B_ROWS,),
        in_specs=[pl.BlockSpec((B_ROWS, B_WIDTH), lambda i: (i, 4)), pl.BlockSpec((B_ROWS, B_WIDTH), lambda i: (i, 5)),
                  pl.BlockSpec((B_ROWS, B_WIDTH), lambda i: (i, 1)), vec, vec, wspec, bspec,
                  pl.BlockSpec(memory_space=pl.ANY)],
        out_specs=[pl.BlockSpec((B_ROWS, 2 * B_WIDTH), lambda i: (i, 2)), vec, vec, wspec, bspec],
        out_shape=[jax.ShapeDtypeStruct(dproj.shape, F32), jax.ShapeDtypeStruct((1, B_WIDTH), F32),
                   jax.ShapeDtypeStruct((1, B_WIDTH), F32), jax.ShapeDtypeStruct((B_GROUPS, B_CHUNK, B_CHUNK), F32),
                   jax.ShapeDtypeStruct((B_CHUNK, B_GROUPS), F32)],
        aliases={7: 0}, args=(proj, proj, dmixin, ln_g, ln_b, w, bias_t, dproj), exchange=exchange)


C_FWD_BLOCKS = 8
C_BWD_BLOCKS = 8
C_PAIR = 2 * C_HEAD_DIM
C_PAIRS = C_HEADS // 2
C_SCALE = 1.0 / math.sqrt(C_HEAD_DIM)
C_ROT_DIM = 2 * C_ROT_HALF
ROPE_ROWS = 1024


def rope_tables(pos_col, name):
    t = pos_col.shape[0]

    def body(p_ref, c_ref, a_ref, b_ref):
        lane = jnp.bitwise_and(lax.broadcasted_iota(jnp.int32, (1, C_PAIR), 1), C_HEAD_DIM - 1)
        j = jnp.bitwise_and(lane, C_ROT_HALF - 1).astype(F32)
        inv = jnp.exp(j * (-math.log(ROPE_THETA) / C_ROT_HALF))
        ang = p_ref[...].astype(F32) * inv
        cos, sin = jnp.cos(ang), jnp.sin(ang)
        c_ref[...] = jnp.where(lane < C_ROT_DIM, cos, 1.0)
        a_ref[...] = jnp.where(lane < C_ROT_HALF, -sin, 0.0)
        b_ref[...] = jnp.where(jnp.logical_and(lane >= C_ROT_HALF, lane < C_ROT_DIM), sin, 0.0)

    tab = pl.BlockSpec((ROPE_ROWS, C_PAIR), lambda i: (i, 0))
    return pl.pallas_call(
        body, name=name, grid=(t // ROPE_ROWS,),
        in_specs=[pl.BlockSpec((ROPE_ROWS, 1), lambda i: (i, 0))],
        out_specs=[tab, tab, tab],
        out_shape=[jax.ShapeDtypeStruct((t, C_PAIR), F32)] * 3,
        compiler_params=_params(("arbitrary",)),
    )(pos_col)


def _rope(x, c, a, b):
    return x * c + pltpu.roll(x, C_PAIR - C_ROT_HALF, 1) * a + pltpu.roll(x, C_ROT_HALF, 1) * b


def _rope_t(d, c, a, b):
    return d * c + pltpu.roll(d * a, C_ROT_HALF, 1) + pltpu.roll(d * b, C_PAIR - C_ROT_HALF, 1)


C_RES = 16


def _residue_major(a, batch):
    return a.reshape(batch, SEQ // C_RES, C_RES, -1).transpose(0, 2, 1, 3).reshape(a.shape)


def _sequence_order(a, batch):
    return a.reshape(batch, C_RES, SEQ // C_RES, -1).transpose(0, 2, 1, 3).reshape(a.shape)


def _block_pieces(idx, dil):
    nblk = SEQ // dil // C_BLOCK
    r, n = idx // nblk, idx % nblk
    per = C_RES // dil
    size = C_BLOCK // per

    def pieces(blk):
        return [((dil * a + r) * (SEQ // C_RES) + size * blk, size) for a in range(per)]

    return pieces(n), pieces(jnp.maximum(n - 1, 0)), n > 0


def _get_rows(ref, pieces):
    return jnp.concatenate([ref[pl.ds(pl.multiple_of(start, 8), size), :] for start, size in pieces], axis=0)


def _set_rows(ref, pieces, val, add=False):
    for k, (start, size) in enumerate(pieces):
        rows = pl.ds(pl.multiple_of(start, 8), size)
        part = val[k * size:(k + 1) * size]
        ref[rows, :] = ref[rows, :] + part if add else part


def _head_masks():
    low = lax.broadcasted_iota(jnp.int32, (1, C_PAIR), 1) < C_HEAD_DIM
    return low, jnp.logical_not(low)


def _attn_mask(has_prev, dil):
    per = C_RES // dil
    size = C_BLOCK // per

    def position(x):
        x = jnp.bitwise_and(x, C_BLOCK - 1)
        return per * jnp.bitwise_and(x, size - 1) + x // size

    j = lax.broadcasted_iota(jnp.int32, (2 * C_BLOCK, 2 * C_BLOCK), 1)
    pi = position(lax.broadcasted_iota(jnp.int32, (2 * C_BLOCK, 2 * C_BLOCK), 0))
    pj = position(j)
    own = j < C_BLOCK
    return jnp.logical_or(jnp.logical_and(own, pj <= pi),
                          jnp.logical_and(jnp.logical_and(jnp.logical_not(own), pj >= pi), has_prev))


def _stack_heads(x):
    low, high = _head_masks()
    return jnp.concatenate([jnp.where(low, x, 0.0), jnp.where(high, x, 0.0)], axis=0)


def _unstack_heads(x):
    low, _ = _head_masks()
    return jnp.where(low, x[:C_BLOCK], x[C_BLOCK:])


def attn_fwd(qkv, cos_t, sin_a, sin_b, batch, name, exchange=None):
    t = qkv.shape[0]
    nbr = len(C_DILATIONS)

    def body(q_ref, k_ref, v_ref, c_ref, a_ref, b_ref, o_ref, l_ref, qs, ks, *stats):
        acc, mm, dd = stats[0:nbr], stats[nbr:2 * nbr], stats[2 * nbr:3 * nbr]
        c, a, b = c_ref[...], a_ref[...], b_ref[...]
        qs[...] = _rope(q_ref[...], c, a, b) * C_SCALE
        ks[...] = _rope(k_ref[...], c, a, b)

        def load(idx, dil):
            own, prev, has_prev = _block_pieces(idx, dil)
            return own, (has_prev, _get_rows(qs, own), _get_rows(ks, own), _get_rows(ks, prev),
                         _get_rows(v_ref, own), _get_rows(v_ref, prev))

        def scores(dil, has_prev, q, k_own, k_prev, v_own, v_prev):
            k_cat = jnp.concatenate([k_own, k_prev], axis=0).astype(BF16)
            return jnp.where(_attn_mask(has_prev, dil), _dot_nt(_stack_heads(q).astype(BF16), k_cat), NEG_BIG)

        def softmax(s):
            m = jnp.max(s, axis=-1, keepdims=True)
            p = jnp.exp(s - m)
            return p.astype(BF16), m, jnp.sum(p, axis=-1, keepdims=True)

        def values(pb, has_prev, q, k_own, k_prev, v_own, v_prev):
            low, high = _head_masks()
            v_cat = jnp.concatenate([v_own, v_prev], axis=0)
            p_wide = jnp.concatenate([pb[:C_BLOCK], pb[C_BLOCK:]], axis=1)
            v_tall = jnp.concatenate([jnp.where(low, v_cat, 0.0), jnp.where(high, v_cat, 0.0)], axis=0).astype(BF16)
            return _dot(p_wide, v_tall)

        for bi, dil in enumerate(C_DILATIONS):
            def pair(i, carry, bi=bi, dil=dil):
                low, _ = _head_masks()
                loaded = [load(C_FWD_BLOCKS * i + k, dil) for k in range(C_FWD_BLOCKS)]
                ss = [scores(dil, *ops) for _, ops in loaded]
                sm = [softmax(s) for s in ss]
                pvs = [values(pb, *ops) for (pb, _, _), (_, ops) in zip(sm, loaded)]
                for (own, _), (_, m, den), pv in zip(loaded, sm, pvs):
                    _set_rows(acc[bi], own, pv)
                    _set_rows(mm[bi], own, jnp.where(low, m[:C_BLOCK], m[C_BLOCK:]))
                    _set_rows(dd[bi], own, jnp.where(low, den[:C_BLOCK], den[C_BLOCK:]))
                return carry

            lax.fori_loop(0, SEQ // C_BLOCK // C_FWD_BLOCKS, pair, 0)
        step = 2 * C_BLOCK
        for r0 in range(0, SEQ, step):
            rr = slice(r0, r0 + step)
            ms = [mm[g][rr, :] for g in range(nbr)]
            m_all = functools.reduce(jnp.maximum, ms)
            ws = [jnp.exp(m - m_all) for m in ms]
            num = sum(acc[g][rr, :] * ws[g] for g in range(nbr))
            den = sum(dd[g][rr, :] * ws[g] for g in range(nbr))
            o_ref[rr, :] = (num / den).astype(BF16)
            l_ref[rr, :] = m_all + jnp.log(den)

    def col(k):
        return pl.BlockSpec((SEQ, C_PAIR), lambda b, p: (b, k * C_PAIRS + p))

    tab = pl.BlockSpec((SEQ, C_PAIR), lambda b, p: (b, 0))
    return _call(
        body, name=name, grid=(batch, C_PAIRS),
        in_specs=[col(0), col(1), col(2), tab, tab, tab],
        out_specs=[col(0), col(0)],
        out_shape=[jax.ShapeDtypeStruct((t, D_MODEL), BF16), jax.ShapeDtypeStruct((t, D_MODEL), F32)],
        scratch_shapes=[pltpu.VMEM((SEQ, C_PAIR), F32)] * (2 + 3 * nbr),
        args=(qkv, qkv, qkv, cos_t, sin_a, sin_b), exchange=exchange)


def attn_bwd(qkv, cos_t, sin_a, sin_b, o, lse, do, batch, name, exchange=None):
    t = qkv.shape[0]

    def body(q_ref, k_ref, v_ref, c_ref, a_ref, b_ref, o_ref, l_ref, do_ref, dqkv_ref, qs, ks, dqs, dks, dvs, dlt):
        low, _ = _head_masks()
        c, a, b = c_ref[...], a_ref[...], b_ref[...]
        qs[...] = _rope(q_ref[...], c, a, b) * C_SCALE
        ks[...] = _rope(k_ref[...], c, a, b)
        prod = do_ref[...] * o_ref[...].astype(F32)
        s_low = jnp.sum(jnp.where(low, prod, 0.0), axis=-1, keepdims=True)
        s_all = jnp.sum(prod, axis=-1, keepdims=True)
        dlt[...] = jnp.where(low, s_low, s_all - s_low)
        dqs[...] = jnp.zeros_like(dqs)
        dks[...] = jnp.zeros_like(dks)
        dvs[...] = jnp.zeros_like(dvs)

        def load(idx, dil):
            own, prev, has_prev = _block_pieces(idx, dil)
            return (own, prev), (has_prev, _get_rows(qs, own), _get_rows(do_ref, own), _get_rows(ks, own),
                                 _get_rows(ks, prev), _get_rows(v_ref, own), _get_rows(v_ref, prev),
                                 _get_rows(l_ref, own), _get_rows(dlt, own))

        def operands(dil, has_prev, q, do, k_own, k_prev, v_own, v_prev, l_full, d_full):
            lcol = jnp.concatenate([l_full[:, 0:1], l_full[:, C_HEAD_DIM:C_HEAD_DIM + 1]], axis=0)
            dcol = jnp.concatenate([d_full[:, 0:1], d_full[:, C_HEAD_DIM:C_HEAD_DIM + 1]], axis=0)
            return (_stack_heads(q).astype(BF16), _stack_heads(do).astype(BF16),
                    jnp.concatenate([k_own, k_prev], axis=0).astype(BF16),
                    jnp.concatenate([v_own, v_prev], axis=0).astype(BF16), lcol, dcol, _attn_mask(has_prev, dil))

        for dil in C_DILATIONS:
            def pair(i, carry, dil=dil):
                loaded = [load(C_BWD_BLOCKS * i + k, dil) for k in range(C_BWD_BLOCKS)]
                ops = [operands(dil, *o) for _, o in loaded]
                ss = [_dot_nt(q_stack, k_cat) for q_stack, _, k_cat, _, _, _, _ in ops]
                dps = [_dot_nt(do_stack, v_cat) for _, do_stack, _, v_cat, _, _, _ in ops]
                ps = [jnp.exp(jnp.where(o[6], s, NEG_BIG) - o[4]) for s, o in zip(ss, ops)]
                dss = [(p * (dp - o[5])).astype(BF16) for p, dp, o in zip(ps, dps, ops)]
                dvs_ = [_dot_tn(p.astype(BF16), o[1]) for p, o in zip(ps, ops)]
                dks_ = [_dot_tn(ds, o[0]) for ds, o in zip(dss, ops)]
                dqs_ = [_unstack_heads(_dot(ds, o[2])) for ds, o in zip(dss, ops)]
                for ((own, prev), _), dq, dk_cat, dv_cat in zip(loaded, dqs_, dks_, dvs_):
                    _set_rows(dqs, own, dq, add=True)
                    _set_rows(dks, own, dk_cat[:C_BLOCK], add=True)
                    _set_rows(dvs, own, dv_cat[:C_BLOCK], add=True)
                    _set_rows(dks, prev, dk_cat[C_BLOCK:], add=True)
                    _set_rows(dvs, prev, dv_cat[C_BLOCK:], add=True)
                return carry

            lax.fori_loop(0, SEQ // C_BLOCK // C_BWD_BLOCKS, pair, 0)
        dqkv_ref[0] = _rope_t(dqs[...] * C_SCALE, c, a, b).astype(BF16)
        dqkv_ref[1] = _rope_t(dks[...], c, a, b).astype(BF16)
        dqkv_ref[2] = dvs[...].astype(BF16)

    def col(k):
        return pl.BlockSpec((SEQ, C_PAIR), lambda b, p: (b, k * C_PAIRS + p))

    tab = pl.BlockSpec((SEQ, C_PAIR), lambda b, p: (b, 0))
    return _call(
        body, name=name, grid=(batch, C_PAIRS),
        in_specs=[col(0), col(1), col(2), tab, tab, tab, col(0), col(0), col(0)],
        out_specs=[pl.BlockSpec((3, SEQ, C_PAIR), lambda b, p: (0, b, p))],
        out_shape=[jax.ShapeDtypeStruct((3, t, D_MODEL), BF16)],
        scratch_shapes=[pltpu.VMEM((SEQ, C_PAIR), F32)] * 6,
        args=(qkv, qkv, qkv, cos_t, sin_a, sin_b, o, lse, do), exchange=exchange)


def sibling_swap(arrays, name):
    n = len(arrays)

    def body(*refs):
        ins, outs = refs[:n], refs[n:2 * n]
        send_sems, recv_sems = refs[2 * n:]
        x, y, c, _ = _place()
        sends = []
        for a in range(n):
            cp = pltpu.make_async_remote_copy(
                src_ref=ins[a], dst_ref=outs[a], send_sem=send_sems.at[a], recv_sem=recv_sems.at[a],
                device_id=(x, y, 1 - c), device_id_type=MESH)
            cp.start()
            sends.append(cp)
        for cp in sends:
            cp.wait_recv()
        for cp in sends:
            cp.wait_send()

    return pl.pallas_call(
        body, name=name,
        in_specs=[ANY] * n, out_specs=[ANY] * n,
        out_shape=[jax.ShapeDtypeStruct(s.shape, s.dtype) for s in arrays],
        scratch_shapes=[pltpu.SemaphoreType.DMA((n,)), pltpu.SemaphoreType.DMA((n,))],
    )(*arrays)


def allreduce_small(slab, name):
    rows, lanes = slab.shape

    def body(x_ref, out_ref, gath, send_sems, recv_sems, local_sem):
        x, y, c, chips = _place()
        me, sibling = (x, y, c), (x, y, 1 - c)

        def slot(px, py, pc):
            return gath.at[4 * px + 2 * py + pc]

        def copy(k, block, to, src=None):
            return pltpu.make_async_remote_copy(
                src_ref=slot(*block) if src is None else src, dst_ref=slot(*block),
                send_sem=send_sems.at[k], recv_sem=recv_sems.at[k], device_id=to, device_id_type=MESH)

        mine = pltpu.make_async_copy(x_ref, slot(*me), local_sem)
        mine.start()
        first = [copy(0, me, sibling, src=x_ref)]
        first += [copy(1 + j, me, (*chip, c), src=x_ref) for j, chip in enumerate(chips)]
        for cp in first:
            cp.start()
        passed = [copy(4 + j, (*chip, c), sibling) for j, chip in enumerate(chips)]
        for j, chip in enumerate(chips):
            copy(1 + j, (*chip, c), me).wait_recv()
            passed[j].start()
        copy(0, sibling, me).wait_recv()
        for j, chip in enumerate(chips):
            copy(4 + j, (*chip, 1 - c), me).wait_recv()
        for cp in first + passed:
            cp.wait_send()
        mine.wait()
        total = gath[0]
        for d in range(1, N_DEV):
            total = total + gath[d]
        out_ref[...] = total

    return pl.pallas_call(
        body, name=name,
        in_specs=[pl.BlockSpec(memory_space=pltpu.VMEM)],
        out_specs=pl.BlockSpec(memory_space=pltpu.VMEM),
        out_shape=jax.ShapeDtypeStruct((rows, lanes), F32),
        scratch_shapes=[pltpu.VMEM((N_DEV, rows, lanes), F32),
                        pltpu.SemaphoreType.DMA((7,)), pltpu.SemaphoreType.DMA((7,)), pltpu.SemaphoreType.DMA],
    )(slab)


ELT_ROWS = 512


def reduce_slabs(r, name, part=0, parts=1, into=None):
    _, rows, cols = r.shape
    br = min(rows, ELT_ROWS)
    nblk = rows // br

    def body(r_ref, *rest):
        o_ref = rest[-1]
        o_ref[...] = ((r_ref[3].astype(F32) + r_ref[0].astype(F32)) + r_ref[1].astype(F32)) + r_ref[2].astype(F32)

    return pl.pallas_call(
        body, name=name, grid=(nblk,),
        in_specs=[pl.BlockSpec((N_CHIPS, br, cols), lambda i: (0, i, 0))] + ([] if into is None else [ANY]),
        out_specs=pl.BlockSpec((br, cols), lambda i: (part * nblk + i, 0)),
        out_shape=jax.ShapeDtypeStruct((parts * rows, cols), F32),
        input_output_aliases={} if into is None else {1: 0},
        compiler_params=_params(("arbitrary",)),
    )(*([r] if into is None else [r, into]))


def _adamw(w, g, m, v):
    m = ADAM_B1 * m + (1.0 - ADAM_B1) * g
    v = ADAM_B2 * v + (1.0 - ADAM_B2) * jnp.square(g)
    m_hat = m / (1.0 - ADAM_B1 ** ADAM_STEP)
    v_hat = v / (1.0 - ADAM_B2 ** ADAM_STEP)
    delta = -ADAM_LR * (m_hat / (jnp.sqrt(v_hat) + ADAM_EPS) + ADAM_WD * w)
    return delta, m, v


def adamw_big(w, s_mine, s_sibling, m, v, name):
    rows, cols = w.shape

    def body(w_ref, a_ref, b_ref, m_ref, v_ref, g_out, d_out, m_out, v_out):
        g = a_ref[...] + b_ref[...]
        g_out[...] = g
        d_out[...], m_out[...], v_out[...] = _adamw(w_ref[...], g, m_ref[...], v_ref[...])

    blk = pl.BlockSpec((min(rows, ELT_ROWS), cols), lambda i: (i, 0))
    out = jax.ShapeDtypeStruct((rows, cols), F32)
    return pl.pallas_call(
        body, name=name, grid=(rows // min(rows, ELT_ROWS),),
        in_specs=[blk] * 5, out_specs=[blk] * 4, out_shape=[out] * 4,
        compiler_params=_params(("arbitrary",)),
    )(w, s_mine, s_sibling, m, v)


def adamw_small(ws, gs, ms, vs, name):
    n = len(ws)

    def body(*refs):
        w_refs, g_refs, m_refs, v_refs = (refs[k * n:(k + 1) * n] for k in range(4))
        d_out, m_out, v_out = (refs[(4 + k) * n:(5 + k) * n] for k in range(3))
        for i in range(n):
            d_out[i][...], m_out[i][...], v_out[i][...] = _adamw(
                w_refs[i][...], g_refs[i][...], m_refs[i][...], v_refs[i][...])

    outs = [jax.ShapeDtypeStruct(w.shape, F32) for w in ws]
    res = pl.pallas_call(body, name=name, out_shape=outs * 3)(*ws, *gs, *ms, *vs)
    return res[:n], res[n:2 * n], res[2 * n:]


SLAB_LANES = 128
SLAB_ROW_ALIGN = 8


def _pack(parts):
    flat = jnp.concatenate([p.reshape(-1) for p in parts])
    rows = -(-flat.shape[0] // (SLAB_LANES * SLAB_ROW_ALIGN)) * SLAB_ROW_ALIGN
    flat = jnp.pad(flat, (0, rows * SLAB_LANES - flat.shape[0]))
    return flat.reshape(rows, SLAB_LANES)


def _unpack(slab, shapes):
    flat = slab.reshape(-1)
    out, pos = [], 0
    for s in shapes:
        size = math.prod(s)
        out.append(flat[pos:pos + size].reshape(s))
        pos += size
    return out


def kernel(x, positions, norm_mix_pre, norm_mix_post, norm_ffn_pre, norm_ffn_post, w_in_even, lb_table, a_norm, b_ln_g, b_ln_b, b_ws, b_bias, w_out_even, w_in_odd, w_out_odd, w_ff1, w_ff2, loss_target, m_norm_mix_pre, m_norm_mix_post, m_norm_ffn_pre, m_norm_ffn_post, m_w_in_even, m_lb_table, m_a_norm, m_b_ln_g, m_b_ln_b, m_b_ws, m_b_bias, m_w_out_even, m_w_in_odd, m_w_out_odd, m_w_ff1, m_w_ff2, v_norm_mix_pre, v_norm_mix_post, v_norm_ffn_pre, v_norm_ffn_post, v_w_in_even, v_lb_table, v_a_norm, v_b_ln_g, v_b_ln_b, v_b_ws, v_b_bias, v_w_out_even, v_w_in_odd, v_w_out_odd, v_w_ff1, v_w_ff2):
    batch = x.shape[0]
    t = batch * SEQ
    d = D_MODEL
    x0 = x.reshape(t, d)
    target = loss_target.reshape(t, d)

    def gain(p, layer):
        return p[layer:layer + 1]

    def gather(*shards):
        return _Exchange("gather", [w.astype(BF16) for w in shards])

    def scatter(*grads):
        return _Exchange("scatter", grads)

    (win_e,) = exchange_alone(gather(w_in_even[0]), "gather_in_even")
    bias_t = b_bias[0].T
    proj, h0, w1_0 = norm_matmul(x0, gain(norm_mix_pre, 0), win_e, "in_proj_even", exchange=gather(w_ff1[0]))
    oa, states, decays, w2_0 = hgrn2_fwd(proj, lb_table, a_norm, batch, "hgrn2_fwd", exchange=gather(w_ff2[0]))
    mixin, wout_e = gmlp_fwd(proj, oa, b_ln_g, b_ln_b, b_ws[0], bias_t, "gmlp_fwd", exchange=gather(w_out_even[0]))
    mix0, x1 = out_proj(mixin, wout_e, x0, gain(norm_mix_post, 0), "out_proj_even")
    x2, hf0, a0, y0, win_o, wout_o = ffn_fwd(x1, gain(norm_ffn_pre, 0), w1_0, w2_0, gain(norm_ffn_post, 0),
                                             "ffn_fwd_0", exchange=gather(w_in_odd[0], w_out_odd[0]))
    x2p = _residue_major(x2, batch)
    qkv, h1 = norm_matmul(x2p, gain(norm_mix_pre, 1), win_o, "in_proj_odd")
    cos_t, sin_a, sin_b = rope_tables(_residue_major(positions.reshape(t, 1), batch), "rope_tables")
    ao, lse, w1_1, w2_1 = attn_fwd(qkv, cos_t, sin_a, sin_b, batch, "attn_fwd", exchange=gather(w_ff1[1], w_ff2[1]))
    mix1, x3 = out_proj(ao, wout_o, x2p, gain(norm_mix_post, 1), "out_proj_odd")
    dx4, hf1, a1, y1, loss_part = ffn_fwd(x3, gain(norm_ffn_pre, 1), w1_1, w2_1, gain(norm_ffn_post, 1),
                                          "ffn_fwd_1", target=_residue_major(target, batch))

    hc = D_FF // N_CHIPS
    dx3, dy1, da1, dg_fpre1, dg_fpost1 = ffn_bwd(
        dx4, x3, y1, a1, gain(norm_ffn_pre, 1), gain(norm_ffn_post, 1), w1_1, w2_1, "ffn_bwd_1")
    g_w1_1 = weight_grad(hf1, da1, "b", d, hc, False, "wgrad_ff1_1")
    g_w2_1 = weight_grad(a1, dy1, "a", hc, d, True, "wgrad_ff2_1")
    dmix1, dao, dg_mpost1 = out_proj_bwd(dx3, mix1, gain(norm_mix_post, 1), wout_o, "out_proj_bwd_odd")
    g_wout_o = weight_grad(ao, dmix1, "a", d // N_CHIPS, d, False, "wgrad_out_odd")
    dqkv, r_w1_1, r_w2_1, r_wout_o = attn_bwd(qkv, cos_t, sin_a, sin_b, ao, lse, dao, batch, "attn_bwd",
                                              exchange=scatter(g_w1_1, g_w2_1, g_wout_o))
    dx2p, dg_mpre1 = norm_matmul_bwd(dqkv, win_o, x2p, gain(norm_mix_pre, 1), dx3, "in_proj_bwd_odd")
    dx2 = _sequence_order(dx2p, batch)
    g_win_o = weight_grad_stacked(h1, dqkv, 3 * d // N_CHIPS, "wgrad_in_odd")
    dx1, dy0, da0, dg_fpre0, dg_fpost0, r_win_o = ffn_bwd(
        dx2, x1, y0, a0, gain(norm_ffn_pre, 0), gain(norm_ffn_post, 0), w1_0, w2_0, "ffn_bwd_0",
        exchange=scatter(g_win_o))
    g_w1_0 = weight_grad(hf0, da0, "b", d, hc, False, "wgrad_ff1_0")
    g_w2_0 = weight_grad(a0, dy0, "a", hc, d, True, "wgrad_ff2_0")
    dmix0, dmixin, dg_mpost0 = out_proj_bwd(dx1, mix0, gain(norm_mix_post, 0), wout_e, "out_proj_bwd_even")
    g_wout_e = weight_grad(mixin, dmix0, "a", d // N_CHIPS, d, False, "wgrad_out_even")
    dproj, d_lb, d_anorm, r_w1_0 = hgrn2_bwd(
        proj, states, decays, lb_table, a_norm, dmixin, batch, "hgrn2_bwd", exchange=scatter(g_w1_0))
    dproj, d_lng, d_lnb, d_ws, d_bias_t, r_w2_0 = gmlp_bwd(
        proj, dmixin, b_ln_g, b_ln_b, b_ws[0], bias_t, dproj, "gmlp_bwd", exchange=scatter(g_w2_0))
    g_win_e, r_wout_e = weight_grad(h0, dproj, "b", d, 3 * d // N_CHIPS, False, "wgrad_in_even",
                                    exchange=scatter(g_wout_e))
    dx0, dg_mpre0, r_win_e = norm_matmul_bwd(dproj, win_e, x0, gain(norm_mix_pre, 0), dx1, "in_proj_bwd_even",
                                             exchange=scatter(g_win_e))
    grad_x = dx0.reshape(x.shape)

    s_w1 = reduce_slabs(r_w1_1, "reduce_ff1_1", part=1, parts=2)
    s_w1 = reduce_slabs(r_w1_0, "reduce_ff1_0", part=0, parts=2, into=s_w1)
    s_w2 = reduce_slabs(r_w2_1, "reduce_ff2_1", part=1, parts=2)
    s_w2 = reduce_slabs(r_w2_0, "reduce_ff2_0", part=0, parts=2, into=s_w2)
    sums = [reduce_slabs(r_win_e, "reduce_in_even"), reduce_slabs(r_wout_e, "reduce_out_even"),
            reduce_slabs(r_win_o, "reduce_in_odd"), reduce_slabs(r_wout_o, "reduce_out_odd"), s_w1, s_w2]
    sibling = sibling_swap(sums, "sibling_swap")
    big_w = [w_in_even, w_out_even, w_in_odd, w_out_odd, w_ff1, w_ff2]
    big_m = [m_w_in_even, m_w_out_even, m_w_in_odd, m_w_out_odd, m_w_ff1, m_w_ff2]
    big_v = [v_w_in_even, v_w_out_even, v_w_in_odd, v_w_out_odd, v_w_ff1, v_w_ff2]
    big = []
    for i, (w, m, v) in enumerate(zip(big_w, big_m, big_v)):
        two_d = (-1, w.shape[-1])
        res = adamw_big(w.reshape(two_d), sums[i], sibling[i], m.reshape(two_d), v.reshape(two_d), "adamw_big_%d" % i)
        big.append([r.reshape(w.shape) for r in res])

    small_w = [norm_mix_pre, norm_mix_post, norm_ffn_pre, norm_ffn_post, lb_table, a_norm, b_ln_g, b_ln_b, b_ws, b_bias]
    small_m = [m_norm_mix_pre, m_norm_mix_post, m_norm_ffn_pre, m_norm_ffn_post, m_lb_table, m_a_norm, m_b_ln_g,
               m_b_ln_b, m_b_ws, m_b_bias]
    small_v = [v_norm_mix_pre, v_norm_mix_post, v_norm_ffn_pre, v_norm_ffn_post, v_lb_table, v_a_norm, v_b_ln_g,
               v_b_ln_b, v_b_ws, v_b_bias]
    partial = [jnp.concatenate([dg_mpre0, dg_mpre1]), jnp.concatenate([dg_mpost0, dg_mpost1]),
               jnp.concatenate([dg_fpre0, dg_fpre1]), jnp.concatenate([dg_fpost0, dg_fpost1]),
               d_lb, d_anorm, d_lng, d_lnb, d_ws[None], d_bias_t.T[None]]
    *small_g, loss = _unpack(allreduce_small(_pack(partial + [loss_part]), "allreduce_small"),
                             [w.shape for w in small_w] + [()])
    small_d, small_nm, small_nv = adamw_small(small_w, small_g, small_m, small_v, "adamw_small")

    order = ["norm_mix_pre", "norm_mix_post", "norm_ffn_pre", "norm_ffn_post", "w_in_even", "lb_table", "a_norm",
             "b_ln_g", "b_ln_b", "b_ws", "b_bias", "w_out_even", "w_in_odd", "w_out_odd", "w_ff1", "w_ff2"]
    small_names = ["norm_mix_pre", "norm_mix_post", "norm_ffn_pre", "norm_ffn_post", "lb_table", "a_norm",
                   "b_ln_g", "b_ln_b", "b_ws", "b_bias"]
    big_names = ["w_in_even", "w_out_even", "w_in_odd", "w_out_odd", "w_ff1", "w_ff2"]
    grads, deltas, new_m, new_v = {}, {}, {}, {}
    for i, nm in enumerate(small_names):
        grads[nm], deltas[nm], new_m[nm], new_v[nm] = small_g[i], small_d[i], small_nm[i], small_nv[i]
    for i, nm in enumerate(big_names):
        grads[nm], deltas[nm], new_m[nm], new_v[nm] = big[i]
    return (loss, grad_x, *[grads[n] for n in order], *[deltas[n] for n in order],
            *[new_m[n] for n in order], *[new_v[n] for n in order])
```

```python
import functools
import math

import jax
import jax.numpy as jnp
from jax import lax
from jax.experimental import pallas as pl
from jax.experimental.pallas import tpu as pltpu

F32 = jnp.float32
BF16 = jnp.bfloat16
MESH = pl.DeviceIdType.MESH

D_MODEL = 1024
SEQ = 2048
D_FF = 4096
N_CHIPS = 4
A_WIDTH = 512
A_HEADS = 4
A_DK = 128
A_CHUNK = 64
A_SUB = 16
B_WIDTH = 512
B_GROUPS = 4
B_CHUNK = 128
C_HEADS = 16
C_HEAD_DIM = 64
C_ROT_HALF = 8
C_BLOCK = 128
C_DILATIONS = (1, 4, 16)
ROPE_THETA = 500000.0
EPS = 1e-6
ADAM_LR = 0.001
ADAM_B1 = 0.9
ADAM_B2 = 0.999
ADAM_EPS = 1e-08
ADAM_WD = 0.01
ADAM_STEP = 10

ROW_TILE = 512
FFN_ROWS = 1024
WGRAD_ROWS = 2048
VMEM_LIMIT = 56 * 1024 * 1024
NEG_BIG = -1e30


def _params(sem=None):
    return pltpu.CompilerParams(dimension_semantics=sem, vmem_limit_bytes=VMEM_LIMIT)


def _dot(a, b):
    return jnp.dot(a, b, preferred_element_type=F32)


def _dot_nt(a, b):
    return lax.dot_general(a, b, (((1,), (1,)), ((), ())), preferred_element_type=F32)


def _dot_tn(a, b):
    return lax.dot_general(a, b, (((0,), (0,)), ((), ())), preferred_element_type=F32)


def _rms(x, g):
    r = lax.rsqrt(jnp.mean(x * x, axis=-1, keepdims=True) + EPS)
    return x * r * g


def _rms_bwd(x, g, dy):
    r = lax.rsqrt(jnp.mean(x * x, axis=-1, keepdims=True) + EPS)
    xh = x * r
    dg = jnp.sum(dy * xh, axis=0, keepdims=True)
    dxh = dy * g
    dx = r * (dxh - xh * jnp.mean(dxh * xh, axis=-1, keepdims=True))
    return dx, dg


def _accumulate(ref, val, first):
    @pl.when(first)
    def _():
        ref[...] = val

    @pl.when(jnp.logical_not(first))
    def _():
        ref[...] += val


N_DEV = 8
ANY = pl.BlockSpec(memory_space=pl.ANY)


def _place():
    x, y, c = lax.axis_index("x"), lax.axis_index("y"), lax.axis_index("c")
    return x, y, c, [(1 - x, y), (x, 1 - y), (1 - x, 1 - y)]


class _Exchange:
    def __init__(self, kind, arrays):
        self.kind, self.arrays, self.n = kind, list(arrays), len(arrays)
        per_peer = pltpu.SemaphoreType.DMA((3 * self.n,))
        if kind == "gather":
            self.out_shape = [jax.ShapeDtypeStruct((N_CHIPS,) + a.shape, a.dtype) for a in self.arrays]
            self.scratch = [per_peer, per_peer, pltpu.SemaphoreType.DMA((self.n,)), per_peer, per_peer]
        else:
            self.out_shape = [jax.ShapeDtypeStruct(a.shape, a.dtype) for a in self.arrays]
            self.scratch = [per_peer, per_peer, pltpu.SemaphoreType.DMA((self.n,))]

    def _copies(self, ins, outs, sems):
        send_sems, recv_sems, local_sems = sems[:3]
        x, y, c, chips = _place()
        me = 2 * x + y
        local, remote = [], []
        for a in range(self.n):
            if self.kind == "gather":
                local.append(pltpu.make_async_copy(ins[a], outs[a].at[me], local_sems.at[a]))
                half = self.arrays[a].shape[0] // 2

                def rows(ref, core, half=half):
                    return ref.at[pl.ds(core * half, half)]
            else:
                local.append(pltpu.make_async_copy(ins[a].at[me], outs[a].at[3], local_sems.at[a]))
            for j, (px, py) in enumerate(chips):
                k = 3 * a + j
                peer = 2 * px + py

                def copy(src, dst, to, send_sem=send_sems.at[k], recv_sem=recv_sems.at[k]):
                    return pltpu.make_async_remote_copy(src_ref=src, dst_ref=dst, send_sem=send_sem, recv_sem=recv_sem,
                                                        device_id=to, device_id_type=MESH)

                if self.kind == "gather":
                    sent = copy(rows(ins[a], c), rows(outs[a].at[me], c), (px, py, c))
                    landed = copy(rows(ins[a], c), rows(outs[a].at[peer], c), (px, py, c))
                    on = dict(send_sem=sems[3].at[k], recv_sem=sems[4].at[k])
                    passed = copy(rows(outs[a].at[peer], c), rows(outs[a].at[peer], c), (x, y, 1 - c), **on)
                    handed = copy(rows(outs[a].at[peer], c), rows(outs[a].at[peer], 1 - c), (x, y, 1 - c), **on)
                    remote.append((sent, landed, passed, handed))
                else:
                    sent = copy(ins[a].at[peer], outs[a].at[j], (px, py, c))
                    remote.append((sent, sent, None, None))
        return local, remote

    def start(self, ins, outs, sems):
        local, remote = self._copies(ins, outs, sems)
        for cp in local:
            cp.start()
        for sent, _, _, _ in remote:
            sent.start()

    def finish(self, ins, outs, sems):
        local, remote = self._copies(ins, outs, sems)
        for _, landed, passed, _ in remote:
            landed.wait_recv()
            if passed is not None:
                passed.start()
        for sent, _, passed, handed in remote:
            if passed is not None:
                handed.wait_recv()
                passed.wait_send()
            sent.wait_send()
        for cp in local:
            cp.wait()


def _call(body, *, name, grid, in_specs, out_specs, out_shape, args, scratch_shapes=(), aliases=None, exchange=None):
    if exchange is None:
        return pl.pallas_call(
            body, name=name, grid=grid, in_specs=in_specs, out_specs=out_specs, out_shape=out_shape,
            scratch_shapes=list(scratch_shapes), input_output_aliases=aliases or {},
            compiler_params=_params(("arbitrary",) * len(grid)))(*args)
    n_in, n_out, n_scr, n_ex = len(in_specs), len(out_specs), len(scratch_shapes), exchange.n
    steps = grid

    def wrapped(*refs):
        ins, refs = refs[:n_in], refs[n_in:]
        ex_in, refs = refs[:n_ex], refs[n_ex:]
        outs, refs = refs[:n_out], refs[n_out:]
        ex_out, refs = refs[:n_ex], refs[n_ex:]
        scr, sems = refs[:n_scr], refs[n_scr:]
        first = functools.reduce(jnp.logical_and, [pl.program_id(k) == 0 for k in range(len(steps))])
        last = functools.reduce(jnp.logical_and, [pl.program_id(k) == steps[k] - 1 for k in range(len(steps))])

        @pl.when(first)
        def _():
            exchange.start(ex_in, ex_out, sems)

        body(*ins, *outs, *scr)

        @pl.when(last)
        def _():
            exchange.finish(ex_in, ex_out, sems)

    return pl.pallas_call(
        wrapped, name=name, grid=grid,
        in_specs=list(in_specs) + [ANY] * n_ex, out_specs=list(out_specs) + [ANY] * n_ex,
        out_shape=list(out_shape) + exchange.out_shape,
        scratch_shapes=list(scratch_shapes) + exchange.scratch, input_output_aliases=aliases or {},
        compiler_params=_params(("arbitrary",) * len(grid)))(*args, *exchange.arrays)


def exchange_alone(exchange, name):
    def body(*refs):
        n = exchange.n
        exchange.start(refs[:n], refs[n:2 * n], refs[2 * n:])
        exchange.finish(refs[:n], refs[n:2 * n], refs[2 * n:])

    return pl.pallas_call(
        body, name=name, in_specs=[ANY] * exchange.n, out_specs=[ANY] * exchange.n,
        out_shape=exchange.out_shape, scratch_shapes=exchange.scratch)(*exchange.arrays)


def norm_matmul(x, g, wg, name, exchange=None):
    t, d = x.shape
    nl = wg.shape[2]

    def body(x_ref, g_ref, w_ref, o_ref, h_ref):
        h = _rms(x_ref[...], g_ref[...]).astype(BF16)
        h_ref[...] = h
        for c in range(N_CHIPS):
            o_ref[:, c * nl:(c + 1) * nl] = _dot(h, w_ref[c])

    return _call(
        body, name=name, grid=(t // ROW_TILE,),
        in_specs=[pl.BlockSpec((ROW_TILE, d), lambda i: (i, 0)),
                  pl.BlockSpec((1, d), lambda i: (0, 0)),
                  pl.BlockSpec((N_CHIPS, d, nl), lambda i: (0, 0, 0))],
        out_specs=[pl.BlockSpec((ROW_TILE, N_CHIPS * nl), lambda i: (i, 0)),
                   pl.BlockSpec((ROW_TILE, d), lambda i: (i, 0))],
        out_shape=[jax.ShapeDtypeStruct((t, N_CHIPS * nl), F32), jax.ShapeDtypeStruct((t, d), BF16)],
        args=(x, g, wg), exchange=exchange)


def norm_matmul_bwd(dproj, wg, x, g, dres, name, exchange=None):
    t, d = x.shape
    nl = wg.shape[2]
    stacked = dproj.ndim == 3
    piece = math.gcd(nl, dproj.shape[-1])

    def body(dp_ref, w_ref, x_ref, g_ref, dres_ref, dx_ref, dg_ref):
        dh = None
        for j in range(N_CHIPS * nl // piece):
            c, off = divmod(j * piece, nl)
            if stacked:
                p, lo = divmod(j * piece, dproj.shape[-1])
                lhs = dp_ref[p, :, lo:lo + piece]
            else:
                lhs = dp_ref[:, j * piece:(j + 1) * piece]
            part = _dot_nt(lhs.astype(BF16), w_ref[c, :, off:off + piece])
            dh = part if dh is None else dh + part
        dx, dg = _rms_bwd(x_ref[...], g_ref[...], dh)
        dx_ref[...] = dres_ref[...] + dx
        _accumulate(dg_ref, dg, pl.program_id(0) == 0)

    row = pl.BlockSpec((ROW_TILE, d), lambda i: (i, 0))
    vec = pl.BlockSpec((1, d), lambda i: (0, 0))
    if stacked:
        dp_spec = pl.BlockSpec((dproj.shape[0], ROW_TILE, dproj.shape[-1]), lambda i: (0, i, 0))
    else:
        dp_spec = pl.BlockSpec((ROW_TILE, N_CHIPS * nl), lambda i: (i, 0))
    return _call(
        body, name=name, grid=(t // ROW_TILE,),
        in_specs=[dp_spec, pl.BlockSpec((N_CHIPS, d, nl), lambda i: (0, 0, 0)), row, vec, row],
        out_specs=[row, vec],
        out_shape=[jax.ShapeDtypeStruct((t, d), F32), jax.ShapeDtypeStruct((1, d), F32)],
        args=(dproj, wg, x, g, dres), exchange=exchange)


def out_proj(a, wg, x, g, name):
    t, d = x.shape
    kl = wg.shape[1]

    def body(a_ref, w_ref, x_ref, g_ref, mix_ref, xo_ref):
        acc = _dot(a_ref[:, 0:kl], w_ref[0])
        for c in range(1, N_CHIPS):
            acc += _dot(a_ref[:, c * kl:(c + 1) * kl], w_ref[c])
        mix_ref[...] = acc
        xo_ref[...] = x_ref[...] + _rms(acc, g_ref[...])

    row = pl.BlockSpec((ROW_TILE, d), lambda i: (i, 0))
    return pl.pallas_call(
        body, name=name, grid=(t // ROW_TILE,),
        in_specs=[row, pl.BlockSpec((N_CHIPS, kl, d), lambda i: (0, 0, 0)), row,
                  pl.BlockSpec((1, d), lambda i: (0, 0))],
        out_specs=[row, row],
        out_shape=[jax.ShapeDtypeStruct((t, d), F32), jax.ShapeDtypeStruct((t, d), F32)],
        compiler_params=_params(("arbitrary",)),
    )(a, wg, x, g)


def out_proj_bwd(dxo, mix, g, wg, name):
    t, d = mix.shape
    kl = wg.shape[1]

    def body(dxo_ref, mix_ref, g_ref, w_ref, dmix_ref, da_ref, dg_ref):
        dmix, dg = _rms_bwd(mix_ref[...], g_ref[...], dxo_ref[...])
        dmb = dmix.astype(BF16)
        dmix_ref[...] = dmb
        for c in range(N_CHIPS):
            da_ref[:, c * kl:(c + 1) * kl] = _dot_nt(dmb, w_ref[c])
        _accumulate(dg_ref, dg, pl.program_id(0) == 0)

    row = pl.BlockSpec((ROW_TILE, d), lambda i: (i, 0))
    vec = pl.BlockSpec((1, d), lambda i: (0, 0))
    return pl.pallas_call(
        body, name=name, grid=(t // ROW_TILE,),
        in_specs=[row, row, vec, pl.BlockSpec((N_CHIPS, kl, d), lambda i: (0, 0, 0))],
        out_specs=[row, row, vec],
        out_shape=[jax.ShapeDtypeStruct((t, d), BF16), jax.ShapeDtypeStruct((t, d), F32),
                   jax.ShapeDtypeStruct((1, d), F32)],
        compiler_params=_params(("arbitrary",)),
    )(dxo, mix, g, wg)


def ffn_fwd(x, gpre, w1g, w2g, gpost, name, exchange=None, target=None):
    t, d = x.shape
    hc = w1g.shape[2]
    with_loss = target is not None

    def body(x_ref, gpre_ref, w1_ref, w2_ref, gpost_ref, *rest):
        if with_loss:
            t_ref, xo_ref, h_ref, a_ref, y_ref, l_ref, acc = rest
        else:
            xo_ref, h_ref, a_ref, y_ref, acc = rest
        i, c = pl.program_id(0), pl.program_id(1)

        @pl.when(c == 0)
        def _():
            h_ref[...] = _rms(x_ref[...], gpre_ref[...]).astype(BF16)

        a = _dot(h_ref[...], w1_ref[...])
        a_ref[...] = a.astype(BF16)
        r = jnp.square(jnp.maximum(a, 0.0)).astype(BF16)
        _accumulate(acc, _dot(r, w2_ref[...]), c == 0)

        @pl.when(c == N_CHIPS - 1)
        def _():
            y = acc[...]
            y_ref[...] = y
            xo = x_ref[...] + _rms(y, gpost_ref[...])
            if with_loss:
                e = xo - t_ref[...]
                xo_ref[...] = e * (1.0 / d)
                part = jnp.sum(jnp.sum(e * e, axis=-1, keepdims=True), axis=0, keepdims=True) * (0.5 / d)
                _accumulate(l_ref, part, i == 0)
            else:
                xo_ref[...] = xo

    row = pl.BlockSpec((FFN_ROWS, d), lambda i, c: (i, 0))
    vec = pl.BlockSpec((1, d), lambda i, c: (0, 0))
    one = pl.BlockSpec((1, 1), lambda i, c: (0, 0))
    return _call(
        body, name=name, grid=(t // FFN_ROWS, N_CHIPS),
        in_specs=[row, vec,
                  pl.BlockSpec((None, d, hc), lambda i, c: (c, 0, 0)),
                  pl.BlockSpec((None, hc, d), lambda i, c: (c, 0, 0)), vec] + ([row] if with_loss else []),
        out_specs=[row, row, pl.BlockSpec((FFN_ROWS, hc), lambda i, c: (i, c)), row] + ([one] if with_loss else []),
        out_shape=[jax.ShapeDtypeStruct((t, d), F32), jax.ShapeDtypeStruct((t, d), BF16),
                   jax.ShapeDtypeStruct((t, N_CHIPS * hc), BF16), jax.ShapeDtypeStruct((t, d), F32)]
        + ([jax.ShapeDtypeStruct((1, 1), F32)] if with_loss else []),
        scratch_shapes=[pltpu.VMEM((FFN_ROWS, d), F32)],
        args=(x, gpre, w1g, w2g, gpost) + ((target,) if with_loss else ()), exchange=exchange)


def ffn_bwd(dxo, x, y, a, gpre, gpost, w1g, w2g, name, exchange=None):
    t, d = x.shape
    hc = w1g.shape[2]

    def body(dxo_ref, x_ref, y_ref, a_ref, gpre_ref, gpost_ref, w1_ref, w2_ref,
             dxi_ref, dy_ref, da_ref, dgpre_ref, dgpost_ref, acc):
        i, c = pl.program_id(0), pl.program_id(1)

        @pl.when(c == 0)
        def _():
            dy, dg = _rms_bwd(y_ref[...], gpost_ref[...], dxo_ref[...])
            dy_ref[...] = dy.astype(BF16)
            _accumulate(dgpost_ref, dg, i == 0)

        dr = _dot_nt(dy_ref[...], w2_ref[...])
        da = (dr * (2.0 * jnp.maximum(a_ref[...].astype(F32), 0.0))).astype(BF16)
        da_ref[...] = da
        _accumulate(acc, _dot_nt(da, w1_ref[...]), c == 0)

        @pl.when(c == N_CHIPS - 1)
        def _():
            dx, dg = _rms_bwd(x_ref[...], gpre_ref[...], acc[...])
            dxi_ref[...] = dxo_ref[...] + dx
            _accumulate(dgpre_ref, dg, i == 0)

    row = pl.BlockSpec((ROW_TILE, d), lambda i, c: (i, 0))
    vec = pl.BlockSpec((1, d), lambda i, c: (0, 0))
    hid = pl.BlockSpec((ROW_TILE, hc), lambda i, c: (i, c))
    return _call(
        body, name=name, grid=(t // ROW_TILE, N_CHIPS),
        in_specs=[row, row, row, hid, vec, vec,
                  pl.BlockSpec((None, d, hc), lambda i, c: (c, 0, 0)),
                  pl.BlockSpec((None, hc, d), lambda i, c: (c, 0, 0))],
        out_specs=[row, row, hid, vec, vec],
        out_shape=[jax.ShapeDtypeStruct((t, d), F32), jax.ShapeDtypeStruct((t, d), BF16),
                   jax.ShapeDtypeStruct((t, N_CHIPS * hc), BF16),
                   jax.ShapeDtypeStruct((1, d), F32), jax.ShapeDtypeStruct((1, d), F32)],
        scratch_shapes=[pltpu.VMEM((ROW_TILE, d), F32)],
        args=(dxo, x, y, a, gpre, gpost, w1g, w2g), exchange=exchange)


def weight_grad(a, b, chunked, bk, bn, relu2, name, exchange=None):
    t = a.shape[0]
    a_on = chunked == "a"
    rows = min(t, WGRAD_ROWS)
    n_steps = t // rows

    def body(a_ref, b_ref, o_ref, acc):
        s = pl.program_id(1)
        av = a_ref[...]
        if relu2:
            av = jnp.square(jnp.maximum(av.astype(F32), 0.0))
        _accumulate(acc, _dot_tn(av.astype(BF16), b_ref[...].astype(BF16)), s == 0)

        @pl.when(s == n_steps - 1)
        def _():
            o_ref[...] = acc[...].astype(BF16)

    res = _call(
        body, name=name, grid=(N_CHIPS, n_steps),
        in_specs=[pl.BlockSpec((rows, bk), (lambda c, s: (s, c)) if a_on else (lambda c, s: (s, 0))),
                  pl.BlockSpec((rows, bn), (lambda c, s: (s, 0)) if a_on else (lambda c, s: (s, c)))],
        out_specs=[pl.BlockSpec((None, bk, bn), lambda c, s: (c, 0, 0))],
        out_shape=[jax.ShapeDtypeStruct((N_CHIPS, bk, bn), BF16)],
        scratch_shapes=[pltpu.VMEM((bk, bn), F32)],
        args=(a, b), exchange=exchange)
    return res[0] if exchange is None else res


def weight_grad_stacked(a, b3, bn, name):
    t, bk = a.shape
    width = b3.shape[-1]
    piece = math.gcd(bn, width)
    rows = min(t, WGRAD_ROWS)
    n_steps = t // rows

    def body(a_ref, b_ref, o_hbm, acc, staged, sem):
        s, c = pl.program_id(0), pl.program_id(1)
        av = a_ref[...].astype(BF16)
        for chunk in range(N_CHIPS):
            @pl.when(c == chunk)
            def _(chunk=chunk):
                cols = [divmod(chunk * bn + k * piece, width) for k in range(bn // piece)]
                b = jnp.concatenate([b_ref[p, :, lo:lo + piece] for p, lo in cols], axis=1).astype(BF16)
                _accumulate(acc.at[chunk], _dot_tn(av, b), s == 0)

                @pl.when(s == n_steps - 1)
                def _():
                    staged[...] = acc[chunk].astype(BF16)
                    copy = pltpu.make_async_copy(staged, o_hbm.at[chunk], sem)
                    copy.start()
                    copy.wait()

    return pl.pallas_call(
        body, name=name, grid=(n_steps, N_CHIPS),
        in_specs=[pl.BlockSpec((rows, bk), lambda s, c: (s, 0)),
                  pl.BlockSpec((b3.shape[0], rows, width), lambda s, c: (0, s, 0))],
        out_specs=ANY,
        out_shape=jax.ShapeDtypeStruct((N_CHIPS, bk, bn), BF16),
        scratch_shapes=[pltpu.VMEM((N_CHIPS, bk, bn), F32), pltpu.VMEM((bk, bn), BF16), pltpu.SemaphoreType.DMA],
        compiler_params=_params(("arbitrary", "arbitrary")),
    )(a, b3)


def _hgrn2_chunk(st, qs, fls, ivs, gls, l0, l1, l2, ng):
    nsub = len(qs)
    mx = jnp.maximum(jnp.maximum(l0, l1), l2)
    e0, e1, e2 = jnp.exp(l0 - mx), jnp.exp(l1 - mx), jnp.exp(l2 - mx)
    lb = e0 / (e0 + e1 + e2)
    rows = lax.broadcasted_iota(jnp.int32, (A_SUB, A_SUB), 0)
    cols = lax.broadcasted_iota(jnp.int32, (A_SUB, A_SUB), 1)
    tri = (rows >= cols).astype(F32)
    keep = (lax.broadcasted_iota(jnp.int32, (A_SUB, A_SUB, A_DK), 0)
            >= lax.broadcasted_iota(jnp.int32, (A_SUB, A_SUB, A_DK), 1))
    base = jnp.zeros_like(l0)
    bases, gs, ks, qfs = [], [], [], []
    for i in range(nsub):
        f = lb + (1.0 - lb) * jax.nn.sigmoid(fls[i])
        logf = jnp.log(f)
        bases.append(base)
        gs.append(base + jnp.dot(tri, logf, precision=lax.Precision.HIGHEST, preferred_element_type=F32))
        base = base + jnp.sum(logf, axis=0, keepdims=True)
        ks.append(1.0 - f)
        qfs.append(jax.nn.silu(qs[i]))
    g_last = base
    stb = st.astype(BF16)
    outs = []
    for i in range(nsub):
        o = _dot_nt((qfs[i] * jnp.exp(gs[i])).astype(BF16), stb)
        if i > 0:
            qt = (qfs[i] * jnp.exp(gs[i] - bases[i])).astype(BF16)
            kk = jnp.concatenate([ks[j] * jnp.exp(bases[i] - gs[j]) for j in range(i)], axis=0).astype(BF16)
            vv = jnp.concatenate(ivs[:i], axis=0).astype(BF16)
            o = o + _dot(_dot_nt(qt, kk).astype(BF16), vv)
        dec = jnp.exp(jnp.where(keep, gs[i][:, None, :] - gs[i][None, :, :], NEG_BIG))
        s_diag = jnp.sum(qfs[i][:, None, :] * ks[i][None, :, :] * dec, axis=-1)
        o = o + _dot(s_diag.astype(BF16), ivs[i].astype(BF16))
        o = o * lax.rsqrt(jnp.mean(o * o, axis=-1, keepdims=True) + EPS) * ng
        outs.append(o * jax.nn.silu(gls[i]))
    kdec = jnp.concatenate([ks[j] * jnp.exp(g_last - gs[j]) for j in range(nsub)], axis=0).astype(BF16)
    vall = jnp.concatenate(ivs, axis=0).astype(BF16)
    new_st = st * jnp.exp(g_last) + _dot_tn(vall, kdec)
    return new_st, outs


A_MAX_LOG_DECAY = 60.0


def _half_sums(logf):
    n = logf.shape[0]
    first = lax.broadcasted_iota(jnp.int32, logf.shape, 0) < n // 2
    return (jnp.sum(jnp.where(first, logf, 0.0), axis=0, keepdims=True),
            jnp.sum(jnp.where(first, 0.0, logf), axis=0, keepdims=True))


def _split3(x):
    hi = x.astype(BF16)
    r1 = x - hi.astype(F32)
    mid = r1.astype(BF16)
    return hi, mid, (r1 - mid.astype(F32)).astype(BF16)


def _tri_matmul(x, transpose):
    n = x.shape[0]
    r = lax.broadcasted_iota(jnp.int32, (n, n), 0)
    c = lax.broadcasted_iota(jnp.int32, (n, n), 1)
    tri = ((r <= c) if transpose else (r >= c)).astype(BF16)
    hi, mid, lo = _split3(x)
    return (_dot(tri, lo) + _dot(tri, mid)) + _dot(tri, hi)


@jax.custom_vjp
def _cumsum_rows(x):
    return _tri_matmul(x, False)


def _cumsum_rows_fwd(x):
    return _tri_matmul(x, False), None


def _cumsum_rows_bwd(_, dy):
    return (_tri_matmul(dy, True),)


_cumsum_rows.defvjp(_cumsum_rows_fwd, _cumsum_rows_bwd)


def _lower_bound(l0, l1, l2):
    mx = jnp.maximum(jnp.maximum(l0, l1), l2)
    e0, e1, e2 = jnp.exp(l0 - mx), jnp.exp(l1 - mx), jnp.exp(l2 - mx)
    return e0 / (e0 + e1 + e2)


def _b(x):
    return x.astype(BF16)


@jax.custom_vjp
def _mm(a, b):
    return _dot(_b(a), _b(b))


_mm.defvjp(lambda a, b: (_mm(a, b), (a, b)),
           lambda res, d: (_dot_nt(_b(d), _b(res[1])), _dot_tn(_b(res[0]), _b(d))))


@jax.custom_vjp
def _mm_nt(a, b):
    return _dot_nt(_b(a), _b(b))


_mm_nt.defvjp(lambda a, b: (_mm_nt(a, b), (a, b)),
              lambda res, d: (_dot(_b(d), _b(res[1])), _dot_tn(_b(d), _b(res[0]))))


def _dot_split(dot, a, b):
    ah, bh = _b(a), _b(b)
    al, bl = _b(a - ah.astype(F32)), _b(b - bh.astype(F32))
    return (dot(ah, bl) + dot(al, bh)) + dot(ah, bh)


@jax.custom_vjp
def _mm_scores(a, b):
    return _dot_nt(_b(a), _b(b))


_mm_scores.defvjp(lambda a, b: (_mm_scores(a, b), (a, b)),
                  lambda res, d: (_dot_split(_dot, d, res[1]), _dot_split(_dot_tn, d, res[0])))


@jax.custom_vjp
def _mm_tn(a, b):
    return _dot_tn(_b(a), _b(b))


_mm_tn.defvjp(lambda a, b: (_mm_tn(a, b), (a, b)),
              lambda res, d: (_dot_nt(_b(res[1]), _b(d)), _dot(_b(res[0]), _b(d))))


@jax.custom_vjp
def _split_heads(x):
    return tuple(x[:, h * A_DK:(h + 1) * A_DK] for h in range(A_HEADS))


def _split_heads_fwd(x):
    return _split_heads(x), None


def _split_heads_bwd(_, parts):
    return (jnp.concatenate(parts, axis=1),)


_split_heads.defvjp(_split_heads_fwd, _split_heads_bwd)


def _hgrn2_chunk_fast(sts, q, fl, iv, gl, l0, l1, l2, ng):
    lb = _lower_bound(l0, l1, l2)
    f = lb + (1.0 - lb) * jax.nn.sigmoid(fl)
    return _hgrn2_fast_core(sts, q, f, jnp.log(f), iv, gl, ng)


def _hgrn2_fast_core(sts, q, f, logf, iv, gl, ng):
    g = _cumsum_rows(logf)
    g_mid, g_last = _half_sums(logf)
    g_last = g_mid + g_last
    k = 1.0 - f
    qf = jax.nn.silu(q)
    qms = _split_heads(qf * jnp.exp(g - g_mid))
    kms = _split_heads(k * jnp.exp(g_mid - g))
    qgs = _split_heads(qf * jnp.exp(g))
    kds = _split_heads(k * jnp.exp(g_last - g))
    ivs = _split_heads(iv)
    decays = _split_heads(jnp.exp(g_last))
    n = q.shape[0]
    causal = lax.broadcasted_iota(jnp.int32, (n, n), 0) >= lax.broadcasted_iota(jnp.int32, (n, n), 1)
    raw = [_mm_scores(qm, km) for qm, km in zip(qms, kms)]
    inter = [_mm_nt(qg, st) for qg, st in zip(qgs, sts)]
    scores = [jnp.where(causal, s, 0.0) for s in raw]
    os = [a + _mm(s, v) for a, s, v in zip(inter, scores, ivs)]
    new_sts = [st * d + _mm_tn(v, kd) for st, d, v, kd in zip(sts, decays, ivs, kds)]
    os = [o * lax.rsqrt(jnp.mean(o * o, axis=-1, keepdims=True) + EPS) for o in os]
    return new_sts, jnp.concatenate(os, axis=1) * ng * jax.nn.silu(gl)


A_STEP_CHUNKS = 4


def _chunk_rows(j):
    return pl.ds(pl.multiple_of(j * A_CHUNK, A_CHUNK), A_CHUNK)


def _sub_rows(j, i):
    return pl.ds(pl.multiple_of(j * A_CHUNK + i * A_SUB, A_SUB), A_SUB)


def _sub_blocks(ref, head, j):
    lanes = slice(head * A_DK, (head + 1) * A_DK)
    return [ref[_sub_rows(j, i), lanes] for i in range(A_CHUNK // A_SUB)]


def hgrn2_fwd(proj, lb_table, a_norm, batch, name, exchange=None):
    t = proj.shape[0]
    n_steps = t // batch // (A_CHUNK * A_STEP_CHUNKS)
    rows = A_CHUNK * A_STEP_CHUNKS

    def body(q_ref, f_ref, i_ref, g_ref, lb_ref, ng_ref, o_ref, st_ref, dec_ref, st):
        @pl.when(pl.program_id(1) == 0)
        def _():
            st[...] = jnp.zeros_like(st)

        def chunk(j, carry):
            r = _chunk_rows(j)
            st_ref[j] = st[...]
            lb = _lower_bound(lb_ref[0:1, :], lb_ref[1:2, :], lb_ref[2:3, :])
            f = lb + (1.0 - lb) * jax.nn.sigmoid(f_ref[r, :])
            logf = jnp.log(f)
            decay = jnp.minimum(*_half_sums(logf))
            dec_ref[j] = decay
            mild = jnp.min(decay) >= -A_MAX_LOG_DECAY

            @pl.when(mild)
            def _():
                new_sts, o = _hgrn2_fast_core([st[h] for h in range(A_HEADS)], q_ref[r, :], f, logf,
                                              i_ref[r, :], g_ref[r, :], ng_ref[...])
                for h in range(A_HEADS):
                    st[h] = new_sts[h]
                o_ref[r, :] = o.astype(BF16)

            @pl.when(jnp.logical_not(mild))
            def _():
                for h in range(A_HEADS):
                    lanes = slice(h * A_DK, (h + 1) * A_DK)
                    new_st, outs = _hgrn2_chunk(
                        st[h], _sub_blocks(q_ref, h, j), _sub_blocks(f_ref, h, j), _sub_blocks(i_ref, h, j),
                        _sub_blocks(g_ref, h, j), lb_ref[0:1, lanes], lb_ref[1:2, lanes], lb_ref[2:3, lanes],
                        ng_ref[:, lanes])
                    st[h] = new_st
                    for i, o in enumerate(outs):
                        o_ref[_sub_rows(j, i), lanes] = o.astype(BF16)

            return carry

        lax.fori_loop(0, A_STEP_CHUNKS, chunk, 0)

    def part(k):
        return pl.BlockSpec((rows, A_WIDTH), lambda b, n: (b * n_steps + n, k))

    return _call(
        body, name=name, grid=(batch, n_steps),
        in_specs=[part(0), part(1), part(2), part(3),
                  pl.BlockSpec((3, A_WIDTH), lambda b, n: (0, 0)), pl.BlockSpec((1, A_WIDTH), lambda b, n: (0, 0))],
        out_specs=[part(0),
                   pl.BlockSpec((A_STEP_CHUNKS, A_HEADS, A_DK, A_DK), lambda b, n: (b * n_steps + n, 0, 0, 0)),
                   pl.BlockSpec((A_STEP_CHUNKS, 1, A_WIDTH), lambda b, n: (b * n_steps + n, 0, 0))],
        out_shape=[jax.ShapeDtypeStruct((t, A_WIDTH), BF16),
                   jax.ShapeDtypeStruct((t // A_CHUNK, A_HEADS, A_DK, A_DK), F32),
                   jax.ShapeDtypeStruct((t // A_CHUNK, 1, A_WIDTH), F32)],
        scratch_shapes=[pltpu.VMEM((A_HEADS, A_DK, A_DK), F32)],
        args=(proj, proj, proj, proj, lb_table, a_norm), exchange=exchange)


def hgrn2_bwd(proj, states, decays, lb_table, a_norm, do, batch, name, exchange=None):
    t = proj.shape[0]
    n_steps = t // batch // (A_CHUNK * A_STEP_CHUNKS)
    rows = A_CHUNK * A_STEP_CHUNKS

    def body(q_ref, f_ref, i_ref, g_ref, st_ref, dec_ref, lb_ref, ng_ref, do_ref, dp_ref, dlb_ref, dng_ref, dst):
        @pl.when(jnp.logical_and(pl.program_id(0) == 0, pl.program_id(1) == 0))
        def _():
            dlb_ref[...] = jnp.zeros_like(dlb_ref)
            dng_ref[...] = jnp.zeros_like(dng_ref)

        @pl.when(pl.program_id(1) == 0)
        def _():
            dst[...] = jnp.zeros_like(dst)

        def chunk(jj, carry):
            j = A_STEP_CHUNKS - 1 - jj
            r = _chunk_rows(j)
            mild = jnp.min(dec_ref[j]) >= -A_MAX_LOG_DECAY

            @pl.when(mild)
            def _():
                _, vjp = jax.vjp(
                    _hgrn2_chunk_fast, [st_ref[j, h] for h in range(A_HEADS)], q_ref[r, :], f_ref[r, :],
                    i_ref[r, :], g_ref[r, :], lb_ref[0:1, :], lb_ref[1:2, :], lb_ref[2:3, :], ng_ref[...])
                d_sts, dq, df, di, dg, dl0, dl1, dl2, dng = vjp(
                    ([dst[h] for h in range(A_HEADS)], do_ref[r, :].astype(F32)))
                for h in range(A_HEADS):
                    dst[h] = d_sts[h]
                for k, part in enumerate((dq, df, di, dg)):
                    dp_ref[r, k * A_WIDTH:(k + 1) * A_WIDTH] = part
                for row, val in enumerate((dl0, dl1, dl2)):
                    dlb_ref[row:row + 1, :] += val
                dng_ref[...] += dng

            @pl.when(jnp.logical_not(mild))
            def _():
                for h in range(A_HEADS):
                    lanes = slice(h * A_DK, (h + 1) * A_DK)
                    _, vjp = jax.vjp(
                        _hgrn2_chunk, st_ref[j, h], _sub_blocks(q_ref, h, j), _sub_blocks(f_ref, h, j),
                        _sub_blocks(i_ref, h, j), _sub_blocks(g_ref, h, j), lb_ref[0:1, lanes], lb_ref[1:2, lanes],
                        lb_ref[2:3, lanes], ng_ref[:, lanes])
                    douts = [x.astype(F32) for x in _sub_blocks(do_ref, h, j)]
                    d_st, dqs, dfs, dis, dgs, dl0, dl1, dl2, dng = vjp((dst[h], douts))
                    dst[h] = d_st
                    for k, parts in enumerate((dqs, dfs, dis, dgs)):
                        for i in range(A_CHUNK // A_SUB):
                            dp_ref[_sub_rows(j, i), k * A_WIDTH + h * A_DK:k * A_WIDTH + (h + 1) * A_DK] = parts[i]
                    for row, val in enumerate((dl0, dl1, dl2)):
                        dlb_ref[row:row + 1, lanes] += val
                    dng_ref[:, lanes] += dng

            return carry

        lax.fori_loop(0, A_STEP_CHUNKS, chunk, 0)

    def rev(b, n):
        return b * n_steps + (n_steps - 1 - n)

    def part(k):
        return pl.BlockSpec((rows, A_WIDTH), lambda b, n: (rev(b, n), k))

    const3 = pl.BlockSpec((3, A_WIDTH), lambda b, n: (0, 0))
    const1 = pl.BlockSpec((1, A_WIDTH), lambda b, n: (0, 0))
    return _call(
        body, name=name, grid=(batch, n_steps),
        in_specs=[part(0), part(1), part(2), part(3),
                  pl.BlockSpec((A_STEP_CHUNKS, A_HEADS, A_DK, A_DK), lambda b, n: (rev(b, n), 0, 0, 0)),
                  pl.BlockSpec((A_STEP_CHUNKS, 1, A_WIDTH), lambda b, n: (rev(b, n), 0, 0)),
                  const3, const1, part(0)],
        out_specs=[pl.BlockSpec((rows, 4 * A_WIDTH), lambda b, n: (rev(b, n), 0)), const3, const1],
        out_shape=[jax.ShapeDtypeStruct((t, 4 * A_WIDTH + 2 * B_WIDTH), F32),
                   jax.ShapeDtypeStruct((3, A_WIDTH), F32), jax.ShapeDtypeStruct((1, A_WIDTH), F32)],
        scratch_shapes=[pltpu.VMEM((A_HEADS, A_DK, A_DK), F32)],
        args=(proj, proj, proj, proj, states, decays, lb_table, a_norm, do), exchange=exchange)


B_GDIM = B_WIDTH // B_GROUPS
B_ROWS = 512


def _gmlp_chunk(ubs, vbs, lngs, lnbs, ws, bcols):
    vs = [jax.nn.gelu(v) for v in vbs]
    mu = sum(jnp.sum(v, axis=-1, keepdims=True) for v in vs) * (1.0 / B_WIDTH)
    var = sum(jnp.sum(jnp.square(v - mu), axis=-1, keepdims=True) for v in vs) * (1.0 / B_WIDTH)
    rstd = lax.rsqrt(var + EPS)
    tril = (lax.broadcasted_iota(jnp.int32, (B_CHUNK, B_CHUNK), 0)
            >= lax.broadcasted_iota(jnp.int32, (B_CHUNK, B_CHUNK), 1))
    outs = []
    for g in range(B_GROUPS):
        vn = (vs[g] - mu) * rstd * lngs[g] + lnbs[g]
        w = jnp.where(tril, ws[g], 0.0).astype(BF16)
        outs.append(jax.nn.gelu(ubs[g]) * (_dot(w, vn.astype(BF16)) + bcols[g]))
    return outs


def _gmlp_args(u_ref, v_ref, lng_ref, lnb_ref, w_ref, bt_ref, rows):
    def groups(ref):
        return [ref[rows, g * B_GDIM:(g + 1) * B_GDIM] for g in range(B_GROUPS)]

    def vec(ref):
        return [ref[:, g * B_GDIM:(g + 1) * B_GDIM] for g in range(B_GROUPS)]

    return (groups(u_ref), groups(v_ref), vec(lng_ref), vec(lnb_ref),
            [w_ref[g] for g in range(B_GROUPS)], [bt_ref[:, g:g + 1] for g in range(B_GROUPS)])


def gmlp_fwd(proj, oa, ln_g, ln_b, w, bias_t, name, exchange=None):
    t = proj.shape[0]

    def body(u_ref, v_ref, oa_ref, lng_ref, lnb_ref, w_ref, bt_ref, o_ref):
        o_ref[:, 0:A_WIDTH] = oa_ref[...]
        for n in range(B_ROWS // B_CHUNK):
            rows = slice(n * B_CHUNK, (n + 1) * B_CHUNK)
            outs = _gmlp_chunk(*_gmlp_args(u_ref, v_ref, lng_ref, lnb_ref, w_ref, bt_ref, rows))
            for g, o in enumerate(outs):
                o_ref[rows, A_WIDTH + g * B_GDIM:A_WIDTH + (g + 1) * B_GDIM] = o.astype(BF16)

    vec = pl.BlockSpec((1, B_WIDTH), lambda i: (0, 0))
    return _call(
        body, name=name, grid=(t // B_ROWS,),
        in_specs=[pl.BlockSpec((B_ROWS, B_WIDTH), lambda i: (i, 4)), pl.BlockSpec((B_ROWS, B_WIDTH), lambda i: (i, 5)),
                  pl.BlockSpec((B_ROWS, A_WIDTH), lambda i: (i, 0)), vec, vec,
                  pl.BlockSpec((B_GROUPS, B_CHUNK, B_CHUNK), lambda i: (0, 0, 0)),
                  pl.BlockSpec((B_CHUNK, B_GROUPS), lambda i: (0, 0))],
        out_specs=[pl.BlockSpec((B_ROWS, A_WIDTH + B_WIDTH), lambda i: (i, 0))],
        out_shape=[jax.ShapeDtypeStruct((t, A_WIDTH + B_WIDTH), BF16)],
        args=(proj, proj, oa, ln_g, ln_b, w, bias_t), exchange=exchange)


def gmlp_bwd(proj, dmixin, ln_g, ln_b, w, bias_t, dproj, name, exchange=None):
    t = proj.shape[0]

    def body(u_ref, v_ref, do_ref, lng_ref, lnb_ref, w_ref, bt_ref, dp_in_ref,
             dp_ref, dlng_ref, dlnb_ref, dw_ref, dbt_ref):
        del dp_in_ref

        @pl.when(pl.program_id(0) == 0)
        def _():
            for ref in (dlng_ref, dlnb_ref, dw_ref, dbt_ref):
                ref[...] = jnp.zeros_like(ref)

        for n in range(B_ROWS // B_CHUNK):
            rows = slice(n * B_CHUNK, (n + 1) * B_CHUNK)
            _, vjp = jax.vjp(_gmlp_chunk, *_gmlp_args(u_ref, v_ref, lng_ref, lnb_ref, w_ref, bt_ref, rows))
            douts = [do_ref[rows, g * B_GDIM:(g + 1) * B_GDIM] for g in range(B_GROUPS)]
            dus, dvs, dlngs, dlnbs, dws, dbs = vjp(douts)
            for g in range(B_GROUPS):
                lanes = slice(g * B_GDIM, (g + 1) * B_GDIM)
                dp_ref[rows, lanes] = dus[g]
                dp_ref[rows, B_WIDTH + g * B_GDIM:B_WIDTH + (g + 1) * B_GDIM] = dvs[g]
                dlng_ref[:, lanes] += dlngs[g]
                dlnb_ref[:, lanes] += dlnbs[g]
                dw_ref[g] += dws[g]
                dbt_ref[:, g:g + 1] += dbs[g]

    vec = pl.BlockSpec((1, B_WIDTH), lambda i: (0, 0))
    wspec = pl.BlockSpec((B_GROUPS, B_CHUNK, B_CHUNK), lambda i: (0, 0, 0))
    bspec = pl.BlockSpec((B_CHUNK, B_GROUPS), lambda i: (0, 0))
    return _call(
        body, name=name, grid=(t // B_ROWS,),
        in_specs=[pl.BlockSpec((B_ROWS, B_WIDTH), lambda i: (i, 4)), pl.BlockSpec((B_ROWS, B_WIDTH), lambda i: (i, 5)),
                  pl.BlockSpec((B_ROWS, B_WIDTH), lambda i: (i, 1)), vec, vec, wspec, bspec,
                  pl.BlockSpec(memory_space=pl.ANY)],
        out_specs=[pl.BlockSpec((B_ROWS, 2 * B_WIDTH), lambda i: (i, 2)), vec, vec, wspec, bspec],
        out_shape=[jax.ShapeDtypeStruct(dproj.shape, F32), jax.ShapeDtypeStruct((1, B_WIDTH), F32),
                   jax.ShapeDtypeStruct((1, B_WIDTH), F32), jax.ShapeDtypeStruct((B_GROUPS, B_CHUNK, B_CHUNK), F32),
                   jax.ShapeDtypeStruct((B_CHUNK, B_GROUPS), F32)],
        aliases={7: 0}, args=(proj, proj, dmixin, ln_g, ln_b, w, bias_t, dproj), exchange=exchange)


C_FWD_BLOCKS = 8
C_BWD_BLOCKS = 8
C_PAIR = 2 * C_HEAD_DIM
C_PAIRS = C_HEADS // 2
C_SCALE = 1.0 / math.sqrt(C_HEAD_DIM)
C_ROT_DIM = 2 * C_ROT_HALF
ROPE_ROWS = 1024


def rope_tables(pos_col, name):
    t = pos_col.shape[0]

    def body(p_ref, c_ref, a_ref, b_ref):
        lane = jnp.bitwise_and(lax.broadcasted_iota(jnp.int32, (1, C_PAIR), 1), C_HEAD_DIM - 1)
        j = jnp.bitwise_and(lane, C_ROT_HALF - 1).astype(F32)
        inv = jnp.exp(j * (-math.log(ROPE_THETA) / C_ROT_HALF))
        ang = p_ref[...].astype(F32) * inv
        cos, sin = jnp.cos(ang), jnp.sin(ang)
        c_ref[...] = jnp.where(lane < C_ROT_DIM, cos, 1.0)
        a_ref[...] = jnp.where(lane < C_ROT_HALF, -sin, 0.0)
        b_ref[...] = jnp.where(jnp.logical_and(lane >= C_ROT_HALF, lane < C_ROT_DIM), sin, 0.0)

    tab = pl.BlockSpec((ROPE_ROWS, C_PAIR), lambda i: (i, 0))
    return pl.pallas_call(
        body, name=name, grid=(t // ROPE_ROWS,),
        in_specs=[pl.BlockSpec((ROPE_ROWS, 1), lambda i: (i, 0))],
        out_specs=[tab, tab, tab],
        out_shape=[jax.ShapeDtypeStruct((t, C_PAIR), F32)] * 3,
        compiler_params=_params(("arbitrary",)),
    )(pos_col)


def _rope(x, c, a, b):
    return x * c + pltpu.roll(x, C_PAIR - C_ROT_HALF, 1) * a + pltpu.roll(x, C_ROT_HALF, 1) * b


def _rope_t(d, c, a, b):
    return d * c + pltpu.roll(d * a, C_ROT_HALF, 1) + pltpu.roll(d * b, C_PAIR - C_ROT_HALF, 1)


C_RES = 16


def _residue_major(a, batch):
    return a.reshape(batch, SEQ // C_RES, C_RES, -1).transpose(0, 2, 1, 3).reshape(a.shape)


def _sequence_order(a, batch):
    return a.reshape(batch, C_RES, SEQ // C_RES, -1).transpose(0, 2, 1, 3).reshape(a.shape)


def _block_pieces(idx, dil):
    nblk = SEQ // dil // C_BLOCK
    r, n = idx // nblk, idx % nblk
    per = C_RES // dil
    size = C_BLOCK // per

    def pieces(blk):
        return [((dil * a + r) * (SEQ // C_RES) + size * blk, size) for a in range(per)]

    return pieces(n), pieces(jnp.maximum(n - 1, 0)), n > 0


def _get_rows(ref, pieces):
    return jnp.concatenate([ref[pl.ds(pl.multiple_of(start, 8), size), :] for start, size in pieces], axis=0)


def _set_rows(ref, pieces, val, add=False):
    for k, (start, size) in enumerate(pieces):
        rows = pl.ds(pl.multiple_of(start, 8), size)
        part = val[k * size:(k + 1) * size]
        ref[rows, :] = ref[rows, :] + part if add else part


def _head_masks():
    low = lax.broadcasted_iota(jnp.int32, (1, C_PAIR), 1) < C_HEAD_DIM
    return low, jnp.logical_not(low)


def _attn_mask(has_prev, dil):
    per = C_RES // dil
    size = C_BLOCK // per

    def position(x):
        x = jnp.bitwise_and(x, C_BLOCK - 1)
        return per * jnp.bitwise_and(x, size - 1) + x // size

    j = lax.broadcasted_iota(jnp.int32, (2 * C_BLOCK, 2 * C_BLOCK), 1)
    pi = position(lax.broadcasted_iota(jnp.int32, (2 * C_BLOCK, 2 * C_BLOCK), 0))
    pj = position(j)
    own = j < C_BLOCK
    return jnp.logical_or(jnp.logical_and(own, pj <= pi),
                          jnp.logical_and(jnp.logical_and(jnp.logical_not(own), pj >= pi), has_prev))


def _stack_heads(x):
    low, high = _head_masks()
    return jnp.concatenate([jnp.where(low, x, 0.0), jnp.where(high, x, 0.0)], axis=0)


def _unstack_heads(x):
    low, _ = _head_masks()
    return jnp.where(low, x[:C_BLOCK], x[C_BLOCK:])


def attn_fwd(qkv, cos_t, sin_a, sin_b, batch, name, exchange=None):
    t = qkv.shape[0]
    nbr = len(C_DILATIONS)

    def body(q_ref, k_ref, v_ref, c_ref, a_ref, b_ref, o_ref, l_ref, qr_ref, kr_ref, qs, ks, *stats):
        acc, mm, dd = stats[0:nbr], stats[nbr:2 * nbr], stats[2 * nbr:3 * nbr]
        c, a, b = c_ref[...], a_ref[...], b_ref[...]
        qs[...] = _rope(q_ref[...], c, a, b) * C_SCALE
        ks[...] = _rope(k_ref[...], c, a, b)
        qr_ref[...] = qs[...].astype(BF16)
        kr_ref[...] = ks[...].astype(BF16)

        def load(idx, dil):
            own, prev, has_prev = _block_pieces(idx, dil)
            return own, (has_prev, _get_rows(qs, own), _get_rows(ks, own), _get_rows(ks, prev),
                         _get_rows(v_ref, own), _get_rows(v_ref, prev))

        def scores(dil, has_prev, q, k_own, k_prev, v_own, v_prev):
            k_cat = jnp.concatenate([k_own, k_prev], axis=0).astype(BF16)
            return jnp.where(_attn_mask(has_prev, dil), _dot_nt(_stack_heads(q).astype(BF16), k_cat), NEG_BIG)

        def softmax(s):
            m = jnp.max(s, axis=-1, keepdims=True)
            p = jnp.exp(s - m)
            return p.astype(BF16), m, jnp.sum(p, axis=-1, keepdims=True)

        def values(pb, has_prev, q, k_own, k_prev, v_own, v_prev):
            low, high = _head_masks()
            v_cat = jnp.concatenate([v_own, v_prev], axis=0)
            p_wide = jnp.concatenate([pb[:C_BLOCK], pb[C_BLOCK:]], axis=1)
            v_tall = jnp.concatenate([jnp.where(low, v_cat, 0.0), jnp.where(high, v_cat, 0.0)], axis=0).astype(BF16)
            return _dot(p_wide, v_tall)

        for bi, dil in enumerate(C_DILATIONS):
            def pair(i, carry, bi=bi, dil=dil):
                low, _ = _head_masks()
                loaded = [load(C_FWD_BLOCKS * i + k, dil) for k in range(C_FWD_BLOCKS)]
                ss = [scores(dil, *ops) for _, ops in loaded]
                sm = [softmax(s) for s in ss]
                pvs = [values(pb, *ops) for (pb, _, _), (_, ops) in zip(sm, loaded)]
                for (own, _), (_, m, den), pv in zip(loaded, sm, pvs):
                    _set_rows(acc[bi], own, pv)
                    _set_rows(mm[bi], own, jnp.where(low, m[:C_BLOCK], m[C_BLOCK:]))
                    _set_rows(dd[bi], own, jnp.where(low, den[:C_BLOCK], den[C_BLOCK:]))
                return carry

            lax.fori_loop(0, SEQ // C_BLOCK // C_FWD_BLOCKS, pair, 0)
        step = 2 * C_BLOCK
        for r0 in range(0, SEQ, step):
            rr = slice(r0, r0 + step)
            ms = [mm[g][rr, :] for g in range(nbr)]
            m_all = functools.reduce(jnp.maximum, ms)
            ws = [jnp.exp(m - m_all) for m in ms]
            num = sum(acc[g][rr, :] * ws[g] for g in range(nbr))
            den = sum(dd[g][rr, :] * ws[g] for g in range(nbr))
            o_ref[rr, :] = (num / den).astype(BF16)
            l_ref[rr, :] = m_all + jnp.log(den)

    def col(k):
        return pl.BlockSpec((SEQ, C_PAIR), lambda b, p: (b, k * C_PAIRS + p))

    tab = pl.BlockSpec((SEQ, C_PAIR), lambda b, p: (b, 0))
    return _call(
        body, name=name, grid=(batch, C_PAIRS),
        in_specs=[col(0), col(1), col(2), tab, tab, tab],
        out_specs=[col(0), col(0), col(0), col(0)],
        out_shape=[jax.ShapeDtypeStruct((t, D_MODEL), BF16), jax.ShapeDtypeStruct((t, D_MODEL), F32),
                   jax.ShapeDtypeStruct((t, D_MODEL), BF16), jax.ShapeDtypeStruct((t, D_MODEL), BF16)],
        scratch_shapes=[pltpu.VMEM((SEQ, C_PAIR), F32)] * (2 + 3 * nbr),
        args=(qkv, qkv, qkv, cos_t, sin_a, sin_b), exchange=exchange)


def attn_bwd(qr, kr, qkv, cos_t, sin_a, sin_b, o, lse, do, batch, name, exchange=None):
    t = qkv.shape[0]

    def body(q_ref, k_ref, v_ref, c_ref, a_ref, b_ref, o_ref, l_ref, do_ref, dqkv_ref, qs, ks, dqs, dks, dvs, dlt):
        low, _ = _head_masks()
        c, a, b = c_ref[...], a_ref[...], b_ref[...]
        qs[...] = q_ref[...].astype(F32)
        ks[...] = k_ref[...].astype(F32)
        prod = do_ref[...] * o_ref[...].astype(F32)
        s_low = jnp.sum(jnp.where(low, prod, 0.0), axis=-1, keepdims=True)
        s_all = jnp.sum(prod, axis=-1, keepdims=True)
        dlt[...] = jnp.where(low, s_low, s_all - s_low)
        dqs[...] = jnp.zeros_like(dqs)
        dks[...] = jnp.zeros_like(dks)
        dvs[...] = jnp.zeros_like(dvs)

        def load(idx, dil):
            own, prev, has_prev = _block_pieces(idx, dil)
            return (own, prev), (has_prev, _get_rows(qs, own), _get_rows(do_ref, own), _get_rows(ks, own),
                                 _get_rows(ks, prev), _get_rows(v_ref, own), _get_rows(v_ref, prev),
                                 _get_rows(l_ref, own), _get_rows(dlt, own))

        def operands(dil, has_prev, q, do, k_own, k_prev, v_own, v_prev, l_full, d_full):
            lcol = jnp.concatenate([l_full[:, 0:1], l_full[:, C_HEAD_DIM:C_HEAD_DIM + 1]], axis=0)
            dcol = jnp.concatenate([d_full[:, 0:1], d_full[:, C_HEAD_DIM:C_HEAD_DIM + 1]], axis=0)
            return (_stack_heads(q).astype(BF16), _stack_heads(do).astype(BF16),
                    jnp.concatenate([k_own, k_prev], axis=0).astype(BF16),
                    jnp.concatenate([v_own, v_prev], axis=0).astype(BF16), lcol, dcol, _attn_mask(has_prev, dil))

        for dil in C_DILATIONS:
            def pair(i, carry, dil=dil):
                loaded = [load(C_BWD_BLOCKS * i + k, dil) for k in range(C_BWD_BLOCKS)]
                ops = [operands(dil, *o) for _, o in loaded]
                ss = [_dot_nt(q_stack, k_cat) for q_stack, _, k_cat, _, _, _, _ in ops]
                dps = [_dot_nt(do_stack, v_cat) for _, do_stack, _, v_cat, _, _, _ in ops]
                ps = [jnp.exp(jnp.where(o[6], s, NEG_BIG) - o[4]) for s, o in zip(ss, ops)]
                dss = [(p * (dp - o[5])).astype(BF16) for p, dp, o in zip(ps, dps, ops)]
                dvs_ = [_dot_tn(p.astype(BF16), o[1]) for p, o in zip(ps, ops)]
                dks_ = [_dot_tn(ds, o[0]) for ds, o in zip(dss, ops)]
                dqs_ = [_unstack_heads(_dot(ds, o[2])) for ds, o in zip(dss, ops)]
                for ((own, prev), _), dq, dk_cat, dv_cat in zip(loaded, dqs_, dks_, dvs_):
                    _set_rows(dqs, own, dq, add=True)
                    _set_rows(dks, own, dk_cat[:C_BLOCK], add=True)
                    _set_rows(dvs, own, dv_cat[:C_BLOCK], add=True)
                    _set_rows(dks, prev, dk_cat[C_BLOCK:], add=True)
                    _set_rows(dvs, prev, dv_cat[C_BLOCK:], add=True)
                return carry

            lax.fori_loop(0, SEQ // C_BLOCK // C_BWD_BLOCKS, pair, 0)
        dqkv_ref[0] = _rope_t(dqs[...] * C_SCALE, c, a, b).astype(BF16)
        dqkv_ref[1] = _rope_t(dks[...], c, a, b).astype(BF16)
        dqkv_ref[2] = dvs[...].astype(BF16)

    def col(k):
        return pl.BlockSpec((SEQ, C_PAIR), lambda b, p: (b, k * C_PAIRS + p))

    tab = pl.BlockSpec((SEQ, C_PAIR), lambda b, p: (b, 0))
    return _call(
        body, name=name, grid=(batch, C_PAIRS),
        in_specs=[col(0), col(0), col(2), tab, tab, tab, col(0), col(0), col(0)],
        out_specs=[pl.BlockSpec((3, SEQ, C_PAIR), lambda b, p: (0, b, p))],
        out_shape=[jax.ShapeDtypeStruct((3, t, D_MODEL), BF16)],
        scratch_shapes=[pltpu.VMEM((SEQ, C_PAIR), F32)] * 6,
        args=(qr, kr, qkv, cos_t, sin_a, sin_b, o, lse, do), exchange=exchange)


def sibling_swap(arrays, name):
    n = len(arrays)

    def body(*refs):
        ins, outs = refs[:n], refs[n:2 * n]
        send_sems, recv_sems = refs[2 * n:]
        x, y, c, _ = _place()
        sends = []
        for a in range(n):
            cp = pltpu.make_async_remote_copy(
                src_ref=ins[a], dst_ref=outs[a], send_sem=send_sems.at[a], recv_sem=recv_sems.at[a],
                device_id=(x, y, 1 - c), device_id_type=MESH)
            cp.start()
            sends.append(cp)
        for cp in sends:
            cp.wait_recv()
        for cp in sends:
            cp.wait_send()

    return pl.pallas_call(
        body, name=name,
        in_specs=[ANY] * n, out_specs=[ANY] * n,
        out_shape=[jax.ShapeDtypeStruct(s.shape, s.dtype) for s in arrays],
        scratch_shapes=[pltpu.SemaphoreType.DMA((n,)), pltpu.SemaphoreType.DMA((n,))],
    )(*arrays)


def allreduce_small(slab, name):
    rows, lanes = slab.shape

    def body(x_ref, out_ref, gath, send_sems, recv_sems, local_sem):
        x, y, c, chips = _place()
        me, sibling = (x, y, c), (x, y, 1 - c)

        def slot(px, py, pc):
            return gath.at[4 * px + 2 * py + pc]

        def copy(k, block, to, src=None):
            return pltpu.make_async_remote_copy(
                src_ref=slot(*block) if src is None else src, dst_ref=slot(*block),
                send_sem=send_sems.at[k], recv_sem=recv_sems.at[k], device_id=to, device_id_type=MESH)

        mine = pltpu.make_async_copy(x_ref, slot(*me), local_sem)
        mine.start()
        first = [copy(0, me, sibling, src=x_ref)]
        first += [copy(1 + j, me, (*chip, c), src=x_ref) for j, chip in enumerate(chips)]
        for cp in first:
            cp.start()
        passed = [copy(4 + j, (*chip, c), sibling) for j, chip in enumerate(chips)]
        for j, chip in enumerate(chips):
            copy(1 + j, (*chip, c), me).wait_recv()
            passed[j].start()
        copy(0, sibling, me).wait_recv()
        for j, chip in enumerate(chips):
            copy(4 + j, (*chip, 1 - c), me).wait_recv()
        for cp in first + passed:
            cp.wait_send()
        mine.wait()
        total = gath[0]
        for d in range(1, N_DEV):
            total = total + gath[d]
        out_ref[...] = total

    return pl.pallas_call(
        body, name=name,
        in_specs=[pl.BlockSpec(memory_space=pltpu.VMEM)],
        out_specs=pl.BlockSpec(memory_space=pltpu.VMEM),
        out_shape=jax.ShapeDtypeStruct((rows, lanes), F32),
        scratch_shapes=[pltpu.VMEM((N_DEV, rows, lanes), F32),
                        pltpu.SemaphoreType.DMA((7,)), pltpu.SemaphoreType.DMA((7,)), pltpu.SemaphoreType.DMA],
    )(slab)


ELT_ROWS = 512


def reduce_slabs(r, name, part=0, parts=1, into=None):
    _, rows, cols = r.shape
    br = min(rows, ELT_ROWS)
    nblk = rows // br

    def body(r_ref, *rest):
        o_ref = rest[-1]
        o_ref[...] = ((r_ref[3].astype(F32) + r_ref[0].astype(F32)) + r_ref[1].astype(F32)) + r_ref[2].astype(F32)

    return pl.pallas_call(
        body, name=name, grid=(nblk,),
        in_specs=[pl.BlockSpec((N_CHIPS, br, cols), lambda i: (0, i, 0))] + ([] if into is None else [ANY]),
        out_specs=pl.BlockSpec((br, cols), lambda i: (part * nblk + i, 0)),
        out_shape=jax.ShapeDtypeStruct((parts * rows, cols), F32),
        input_output_aliases={} if into is None else {1: 0},
        compiler_params=_params(("arbitrary",)),
    )(*([r] if into is None else [r, into]))


def _adamw(w, g, m, v):
    m = ADAM_B1 * m + (1.0 - ADAM_B1) * g
    v = ADAM_B2 * v + (1.0 - ADAM_B2) * jnp.square(g)
    m_hat = m / (1.0 - ADAM_B1 ** ADAM_STEP)
    v_hat = v / (1.0 - ADAM_B2 ** ADAM_STEP)
    delta = -ADAM_LR * (m_hat / (jnp.sqrt(v_hat) + ADAM_EPS) + ADAM_WD * w)
    return delta, m, v


def adamw_big(w, s_mine, s_sibling, m, v, name):
    rows, cols = w.shape

    def body(w_ref, a_ref, b_ref, m_ref, v_ref, g_out, d_out, m_out, v_out):
        g = a_ref[...] + b_ref[...]
        g_out[...] = g
        d_out[...], m_out[...], v_out[...] = _adamw(w_ref[...], g, m_ref[...], v_ref[...])

    blk = pl.BlockSpec((min(rows, ELT_ROWS), cols), lambda i: (i, 0))
    out = jax.ShapeDtypeStruct((rows, cols), F32)
    return pl.pallas_call(
        body, name=name, grid=(rows // min(rows, ELT_ROWS),),
        in_specs=[blk] * 5, out_specs=[blk] * 4, out_shape=[out] * 4,
        compiler_params=_params(("arbitrary",)),
    )(w, s_mine, s_sibling, m, v)


def adamw_small(ws, gs, ms, vs, name):
    n = len(ws)

    def body(*refs):
        w_refs, g_refs, m_refs, v_refs = (refs[k * n:(k + 1) * n] for k in range(4))
        d_out, m_out, v_out = (refs[(4 + k) * n:(5 + k) * n] for k in range(3))
        for i in range(n):
            d_out[i][...], m_out[i][...], v_out[i][...] = _adamw(
                w_refs[i][...], g_refs[i][...], m_refs[i][...], v_refs[i][...])

    outs = [jax.ShapeDtypeStruct(w.shape, F32) for w in ws]
    res = pl.pallas_call(body, name=name, out_shape=outs * 3)(*ws, *gs, *ms, *vs)
    return res[:n], res[n:2 * n], res[2 * n:]


SLAB_LANES = 128
SLAB_ROW_ALIGN = 8


def _pack(parts):
    flat = jnp.concatenate([p.reshape(-1) for p in parts])
    rows = -(-flat.shape[0] // (SLAB_LANES * SLAB_ROW_ALIGN)) * SLAB_ROW_ALIGN
    flat = jnp.pad(flat, (0, rows * SLAB_LANES - flat.shape[0]))
    return flat.reshape(rows, SLAB_LANES)


def _unpack(slab, shapes):
    flat = slab.reshape(-1)
    out, pos = [], 0
    for s in shapes:
        size = math.prod(s)
        out.append(flat[pos:pos + size].reshape(s))
        pos += size
    return out


def kernel(x, positions, norm_mix_pre, norm_mix_post, norm_ffn_pre, norm_ffn_post, w_in_even, lb_table, a_norm, b_ln_g, b_ln_b, b_ws, b_bias, w_out_even, w_in_odd, w_out_odd, w_ff1, w_ff2, loss_target, m_norm_mix_pre, m_norm_mix_post, m_norm_ffn_pre, m_norm_ffn_post, m_w_in_even, m_lb_table, m_a_norm, m_b_ln_g, m_b_ln_b, m_b_ws, m_b_bias, m_w_out_even, m_w_in_odd, m_w_out_odd, m_w_ff1, m_w_ff2, v_norm_mix_pre, v_norm_mix_post, v_norm_ffn_pre, v_norm_ffn_post, v_w_in_even, v_lb_table, v_a_norm, v_b_ln_g, v_b_ln_b, v_b_ws, v_b_bias, v_w_out_even, v_w_in_odd, v_w_out_odd, v_w_ff1, v_w_ff2):
    batch = x.shape[0]
    t = batch * SEQ
    d = D_MODEL
    x0 = x.reshape(t, d)
    target = loss_target.reshape(t, d)

    def gain(p, layer):
        return p[layer:layer + 1]

    def gather(*shards):
        return _Exchange("gather", [w.astype(BF16) for w in shards])

    def scatter(*grads):
        return _Exchange("scatter", grads)

    (win_e,) = exchange_alone(gather(w_in_even[0]), "gather_in_even")
    bias_t = b_bias[0].T
    proj, h0, w1_0 = norm_matmul(x0, gain(norm_mix_pre, 0), win_e, "in_proj_even", exchange=gather(w_ff1[0]))
    oa, states, decays, w2_0 = hgrn2_fwd(proj, lb_table, a_norm, batch, "hgrn2_fwd", exchange=gather(w_ff2[0]))
    mixin, wout_e = gmlp_fwd(proj, oa, b_ln_g, b_ln_b, b_ws[0], bias_t, "gmlp_fwd", exchange=gather(w_out_even[0]))
    mix0, x1 = out_proj(mixin, wout_e, x0, gain(norm_mix_post, 0), "out_proj_even")
    x2, hf0, a0, y0, win_o, wout_o = ffn_fwd(x1, gain(norm_ffn_pre, 0), w1_0, w2_0, gain(norm_ffn_post, 0),
                                             "ffn_fwd_0", exchange=gather(w_in_odd[0], w_out_odd[0]))
    x2p = _residue_major(x2, batch)
    qkv, h1 = norm_matmul(x2p, gain(norm_mix_pre, 1), win_o, "in_proj_odd")
    cos_t, sin_a, sin_b = rope_tables(_residue_major(positions.reshape(t, 1), batch), "rope_tables")
    ao, lse, q_rot, k_rot, w1_1, w2_1 = attn_fwd(qkv, cos_t, sin_a, sin_b, batch, "attn_fwd",
                                                 exchange=gather(w_ff1[1], w_ff2[1]))
    mix1, x3 = out_proj(ao, wout_o, x2p, gain(norm_mix_post, 1), "out_proj_odd")
    dx4, hf1, a1, y1, loss_part = ffn_fwd(x3, gain(norm_ffn_pre, 1), w1_1, w2_1, gain(norm_ffn_post, 1),
                                          "ffn_fwd_1", target=_residue_major(target, batch))

    hc = D_FF // N_CHIPS
    dx3, dy1, da1, dg_fpre1, dg_fpost1 = ffn_bwd(
        dx4, x3, y1, a1, gain(norm_ffn_pre, 1), gain(norm_ffn_post, 1), w1_1, w2_1, "ffn_bwd_1")
    g_w1_1 = weight_grad(hf1, da1, "b", d, hc, False, "wgrad_ff1_1")
    g_w2_1 = weight_grad(a1, dy1, "a", hc, d, True, "wgrad_ff2_1")
    dmix1, dao, dg_mpost1 = out_proj_bwd(dx3, mix1, gain(norm_mix_post, 1), wout_o, "out_proj_bwd_odd")
    g_wout_o = weight_grad(ao, dmix1, "a", d // N_CHIPS, d, False, "wgrad_out_odd")
    dqkv, r_w1_1, r_w2_1, r_wout_o = attn_bwd(q_rot, k_rot, qkv, cos_t, sin_a, sin_b, ao, lse, dao, batch, "attn_bwd",
                                              exchange=scatter(g_w1_1, g_w2_1, g_wout_o))
    dx2p, dg_mpre1 = norm_matmul_bwd(dqkv, win_o, x2p, gain(norm_mix_pre, 1), dx3, "in_proj_bwd_odd")
    dx2 = _sequence_order(dx2p, batch)
    g_win_o = weight_grad_stacked(h1, dqkv, 3 * d // N_CHIPS, "wgrad_in_odd")
    dx1, dy0, da0, dg_fpre0, dg_fpost0, r_win_o = ffn_bwd(
        dx2, x1, y0, a0, gain(norm_ffn_pre, 0), gain(norm_ffn_post, 0), w1_0, w2_0, "ffn_bwd_0",
        exchange=scatter(g_win_o))
    g_w1_0 = weight_grad(hf0, da0, "b", d, hc, False, "wgrad_ff1_0")
    g_w2_0 = weight_grad(a0, dy0, "a", hc, d, True, "wgrad_ff2_0")
    dmix0, dmixin, dg_mpost0 = out_proj_bwd(dx1, mix0, gain(norm_mix_post, 0), wout_e, "out_proj_bwd_even")
    g_wout_e = weight_grad(mixin, dmix0, "a", d // N_CHIPS, d, False, "wgrad_out_even")
    dproj, d_lb, d_anorm, r_w1_0 = hgrn2_bwd(
        proj, states, decays, lb_table, a_norm, dmixin, batch, "hgrn2_bwd", exchange=scatter(g_w1_0))
    dproj, d_lng, d_lnb, d_ws, d_bias_t, r_w2_0 = gmlp_bwd(
        proj, dmixin, b_ln_g, b_ln_b, b_ws[0], bias_t, dproj, "gmlp_bwd", exchange=scatter(g_w2_0))
    g_win_e, r_wout_e = weight_grad(h0, dproj, "b", d, 3 * d // N_CHIPS, False, "wgrad_in_even",
                                    exchange=scatter(g_wout_e))
    dx0, dg_mpre0, r_win_e = norm_matmul_bwd(dproj, win_e, x0, gain(norm_mix_pre, 0), dx1, "in_proj_bwd_even",
                                             exchange=scatter(g_win_e))
    grad_x = dx0.reshape(x.shape)

    s_w1 = reduce_slabs(r_w1_1, "reduce_ff1_1", part=1, parts=2)
    s_w1 = reduce_slabs(r_w1_0, "reduce_ff1_0", part=0, parts=2, into=s_w1)
    s_w2 = reduce_slabs(r_w2_1, "reduce_ff2_1", part=1, parts=2)
    s_w2 = reduce_slabs(r_w2_0, "reduce_ff2_0", part=0, parts=2, into=s_w2)
    sums = [reduce_slabs(r_win_e, "reduce_in_even"), reduce_slabs(r_wout_e, "reduce_out_even"),
            reduce_slabs(r_win_o, "reduce_in_odd"), reduce_slabs(r_wout_o, "reduce_out_odd"), s_w1, s_w2]
    sibling = sibling_swap(sums, "sibling_swap")
    big_w = [w_in_even, w_out_even, w_in_odd, w_out_odd, w_ff1, w_ff2]
    big_m = [m_w_in_even, m_w_out_even, m_w_in_odd, m_w_out_odd, m_w_ff1, m_w_ff2]
    big_v = [v_w_in_even, v_w_out_even, v_w_in_odd, v_w_out_odd, v_w_ff1, v_w_ff2]
    big = []
    for i, (w, m, v) in enumerate(zip(big_w, big_m, big_v)):
        two_d = (-1, w.shape[-1])
        res = adamw_big(w.reshape(two_d), sums[i], sibling[i], m.reshape(two_d), v.reshape(two_d), "adamw_big_%d" % i)
        big.append([r.reshape(w.shape) for r in res])

    small_w = [norm_mix_pre, norm_mix_post, norm_ffn_pre, norm_ffn_post, lb_table, a_norm, b_ln_g, b_ln_b, b_ws, b_bias]
    small_m = [m_norm_mix_pre, m_norm_mix_post, m_norm_ffn_pre, m_norm_ffn_post, m_lb_table, m_a_norm, m_b_ln_g,
               m_b_ln_b, m_b_ws, m_b_bias]
    small_v = [v_norm_mix_pre, v_norm_mix_post, v_norm_ffn_pre, v_norm_ffn_post, v_lb_table, v_a_norm, v_b_ln_g,
               v_b_ln_b, v_b_ws, v_b_bias]
    partial = [jnp.concatenate([dg_mpre0, dg_mpre1]), jnp.concatenate([dg_mpost0, dg_mpost1]),
               jnp.concatenate([dg_fpre0, dg_fpre1]), jnp.concatenate([dg_fpost0, dg_fpost1]),
               d_lb, d_anorm, d_lng, d_lnb, d_ws[None], d_bias_t.T[None]]
    *small_g, loss = _unpack(allreduce_small(_pack(partial + [loss_part]), "allreduce_small"),
                             [w.shape for w in small_w] + [()])
    small_d, small_nm, small_nv = adamw_small(small_w, small_g, small_m, small_v, "adamw_small")

    order = ["norm_mix_pre", "norm_mix_post", "norm_ffn_pre", "norm_ffn_post", "w_in_even", "lb_table", "a_norm",
             "b_ln_g", "b_ln_b", "b_ws", "b_bias", "w_out_even", "w_in_odd", "w_out_odd", "w_ff1", "w_ff2"]
    small_names = ["norm_mix_pre", "norm_mix_post", "norm_ffn_pre", "norm_ffn_post", "lb_table", "a_norm",
                   "b_ln_g", "b_ln_b", "b_ws", "b_bias"]
    big_names = ["w_in_even", "w_out_even", "w_in_odd", "w_out_odd", "w_ff1", "w_ff2"]
    grads, deltas, new_m, new_v = {}, {}, {}, {}
    for i, nm in enumerate(small_names):
        grads[nm], deltas[nm], new_m[nm], new_v[nm] = small_g[i], small_d[i], small_nm[i], small_nv[i]
    for i, nm in enumerate(big_names):
        grads[nm], deltas[nm], new_m[nm], new_v[nm] = big[i]
    return (loss, grad_x, *[grads[n] for n in order], *[deltas[n] for n in order],
            *[new_m[n] for n in order], *[new_v[n] for n in order])
```

```python
import functools
import math

import jax
import jax.numpy as jnp
from jax import lax
from jax.experimental import pallas as pl
from jax.experimental.pallas import tpu as pltpu

F32 = jnp.float32
BF16 = jnp.bfloat16
MESH = pl.DeviceIdType.MESH

D_MODEL = 1024
SEQ = 2048
D_FF = 4096
N_CHIPS = 4
A_WIDTH = 512
A_HEADS = 4
A_DK = 128
A_CHUNK = 64
A_SUB = 16
B_WIDTH = 512
B_GROUPS = 4
B_CHUNK = 128
C_HEADS = 16
C_HEAD_DIM = 64
C_ROT_HALF = 8
C_BLOCK = 128
C_DILATIONS = (1, 4, 16)
ROPE_THETA = 500000.0
EPS = 1e-6
ADAM_LR = 0.001
ADAM_B1 = 0.9
ADAM_B2 = 0.999
ADAM_EPS = 1e-08
ADAM_WD = 0.01
ADAM_STEP = 10

ROW_TILE = 512
FFN_ROWS = 1024
PROJ_ROWS = 1024
WGRAD_ROWS = 2048
VMEM_LIMIT = 56 * 1024 * 1024
NEG_BIG = -1e30


def _params(sem=None):
    return pltpu.CompilerParams(dimension_semantics=sem, vmem_limit_bytes=VMEM_LIMIT)


def _dot(a, b):
    return jnp.dot(a, b, preferred_element_type=F32)


def _dot_nt(a, b):
    return lax.dot_general(a, b, (((1,), (1,)), ((), ())), preferred_element_type=F32)


def _dot_tn(a, b):
    return lax.dot_general(a, b, (((0,), (0,)), ((), ())), preferred_element_type=F32)


def _rms(x, g):
    r = lax.rsqrt(jnp.mean(x * x, axis=-1, keepdims=True) + EPS)
    return x * r * g


def _rms_bwd(x, g, dy):
    r = lax.rsqrt(jnp.mean(x * x, axis=-1, keepdims=True) + EPS)
    xh = x * r
    dg = jnp.sum(dy * xh, axis=0, keepdims=True)
    dxh = dy * g
    dx = r * (dxh - xh * jnp.mean(dxh * xh, axis=-1, keepdims=True))
    return dx, dg


def _accumulate(ref, val, first):
    @pl.when(first)
    def _():
        ref[...] = val

    @pl.when(jnp.logical_not(first))
    def _():
        ref[...] += val


N_DEV = 8
ANY = pl.BlockSpec(memory_space=pl.ANY)


def _place():
    x, y, c = lax.axis_index("x"), lax.axis_index("y"), lax.axis_index("c")
    return x, y, c, [(1 - x, y), (x, 1 - y), (1 - x, 1 - y)]


class _Exchange:
    def __init__(self, kind, arrays):
        self.kind, self.arrays, self.n = kind, list(arrays), len(arrays)
        per_peer = pltpu.SemaphoreType.DMA((3 * self.n,))
        if kind == "gather":
            self.out_shape = [jax.ShapeDtypeStruct((N_CHIPS,) + a.shape, a.dtype) for a in self.arrays]
            self.scratch = [per_peer, per_peer, pltpu.SemaphoreType.DMA((self.n,)), per_peer, per_peer]
        else:
            self.out_shape = [jax.ShapeDtypeStruct(a.shape, a.dtype) for a in self.arrays]
            self.scratch = [per_peer, per_peer, pltpu.SemaphoreType.DMA((self.n,))]

    def _copies(self, ins, outs, sems):
        send_sems, recv_sems, local_sems = sems[:3]
        x, y, c, chips = _place()
        me = 2 * x + y
        local, remote = [], []
        for a in range(self.n):
            if self.kind == "gather":
                local.append(pltpu.make_async_copy(ins[a], outs[a].at[me], local_sems.at[a]))
                half = self.arrays[a].shape[0] // 2

                def rows(ref, core, half=half):
                    return ref.at[pl.ds(core * half, half)]
            else:
                local.append(pltpu.make_async_copy(ins[a].at[me], outs[a].at[3], local_sems.at[a]))
            for j, (px, py) in enumerate(chips):
                k = 3 * a + j
                peer = 2 * px + py

                def copy(src, dst, to, send_sem=send_sems.at[k], recv_sem=recv_sems.at[k]):
                    return pltpu.make_async_remote_copy(src_ref=src, dst_ref=dst, send_sem=send_sem, recv_sem=recv_sem,
                                                        device_id=to, device_id_type=MESH)

                if self.kind == "gather":
                    sent = copy(rows(ins[a], c), rows(outs[a].at[me], c), (px, py, c))
                    landed = copy(rows(ins[a], c), rows(outs[a].at[peer], c), (px, py, c))
                    on = dict(send_sem=sems[3].at[k], recv_sem=sems[4].at[k])
                    passed = copy(rows(outs[a].at[peer], c), rows(outs[a].at[peer], c), (x, y, 1 - c), **on)
                    handed = copy(rows(outs[a].at[peer], c), rows(outs[a].at[peer], 1 - c), (x, y, 1 - c), **on)
                    remote.append((sent, landed, passed, handed))
                else:
                    sent = copy(ins[a].at[peer], outs[a].at[j], (px, py, c))
                    remote.append((sent, sent, None, None))
        return local, remote

    def start(self, ins, outs, sems):
        local, remote = self._copies(ins, outs, sems)
        for cp in local:
            cp.start()
        for sent, _, _, _ in remote:
            sent.start()

    def finish(self, ins, outs, sems):
        local, remote = self._copies(ins, outs, sems)
        for _, landed, passed, _ in remote:
            landed.wait_recv()
            if passed is not None:
                passed.start()
        for sent, _, passed, handed in remote:
            if passed is not None:
                handed.wait_recv()
                passed.wait_send()
            sent.wait_send()
        for cp in local:
            cp.wait()


def _call(body, *, name, grid, in_specs, out_specs, out_shape, args, scratch_shapes=(), aliases=None, exchange=None):
    if exchange is None:
        return pl.pallas_call(
            body, name=name, grid=grid, in_specs=in_specs, out_specs=out_specs, out_shape=out_shape,
            scratch_shapes=list(scratch_shapes), input_output_aliases=aliases or {},
            compiler_params=_params(("arbitrary",) * len(grid)))(*args)
    n_in, n_out, n_scr, n_ex = len(in_specs), len(out_specs), len(scratch_shapes), exchange.n
    steps = grid

    def wrapped(*refs):
        ins, refs = refs[:n_in], refs[n_in:]
        ex_in, refs = refs[:n_ex], refs[n_ex:]
        outs, refs = refs[:n_out], refs[n_out:]
        ex_out, refs = refs[:n_ex], refs[n_ex:]
        scr, sems = refs[:n_scr], refs[n_scr:]
        first = functools.reduce(jnp.logical_and, [pl.program_id(k) == 0 for k in range(len(steps))])
        last = functools.reduce(jnp.logical_and, [pl.program_id(k) == steps[k] - 1 for k in range(len(steps))])

        @pl.when(first)
        def _():
            exchange.start(ex_in, ex_out, sems)

        body(*ins, *outs, *scr)

        @pl.when(last)
        def _():
            exchange.finish(ex_in, ex_out, sems)

    return pl.pallas_call(
        wrapped, name=name, grid=grid,
        in_specs=list(in_specs) + [ANY] * n_ex, out_specs=list(out_specs) + [ANY] * n_ex,
        out_shape=list(out_shape) + exchange.out_shape,
        scratch_shapes=list(scratch_shapes) + exchange.scratch, input_output_aliases=aliases or {},
        compiler_params=_params(("arbitrary",) * len(grid)))(*args, *exchange.arrays)


def exchange_alone(exchange, name):
    def body(*refs):
        n = exchange.n
        exchange.start(refs[:n], refs[n:2 * n], refs[2 * n:])
        exchange.finish(refs[:n], refs[n:2 * n], refs[2 * n:])

    return pl.pallas_call(
        body, name=name, in_specs=[ANY] * exchange.n, out_specs=[ANY] * exchange.n,
        out_shape=exchange.out_shape, scratch_shapes=exchange.scratch)(*exchange.arrays)


def norm_matmul(x, g, wg, name, exchange=None):
    t, d = x.shape
    nl = wg.shape[2]

    def body(x_ref, g_ref, w_ref, o_ref, h_ref):
        h = _rms(x_ref[...], g_ref[...]).astype(BF16)
        h_ref[...] = h
        for c in range(N_CHIPS):
            o_ref[:, c * nl:(c + 1) * nl] = _dot(h, w_ref[c])

    return _call(
        body, name=name, grid=(t // PROJ_ROWS,),
        in_specs=[pl.BlockSpec((PROJ_ROWS, d), lambda i: (i, 0)),
                  pl.BlockSpec((1, d), lambda i: (0, 0)),
                  pl.BlockSpec((N_CHIPS, d, nl), lambda i: (0, 0, 0))],
        out_specs=[pl.BlockSpec((PROJ_ROWS, N_CHIPS * nl), lambda i: (i, 0)),
                   pl.BlockSpec((PROJ_ROWS, d), lambda i: (i, 0))],
        out_shape=[jax.ShapeDtypeStruct((t, N_CHIPS * nl), F32), jax.ShapeDtypeStruct((t, d), BF16)],
        args=(x, g, wg), exchange=exchange)


def norm_matmul_bwd(dproj, wg, x, g, dres, name, exchange=None):
    t, d = x.shape
    nl = wg.shape[2]
    stacked = dproj.ndim == 3
    piece = math.gcd(nl, dproj.shape[-1])

    def body(dp_ref, w_ref, x_ref, g_ref, dres_ref, dx_ref, dg_ref):
        dh = None
        for j in range(N_CHIPS * nl // piece):
            c, off = divmod(j * piece, nl)
            if stacked:
                p, lo = divmod(j * piece, dproj.shape[-1])
                lhs = dp_ref[p, :, lo:lo + piece]
            else:
                lhs = dp_ref[:, j * piece:(j + 1) * piece]
            part = _dot_nt(lhs.astype(BF16), w_ref[c, :, off:off + piece])
            dh = part if dh is None else dh + part
        dx, dg = _rms_bwd(x_ref[...], g_ref[...], dh)
        dx_ref[...] = dres_ref[...] + dx
        _accumulate(dg_ref, dg, pl.program_id(0) == 0)

    row = pl.BlockSpec((ROW_TILE, d), lambda i: (i, 0))
    vec = pl.BlockSpec((1, d), lambda i: (0, 0))
    if stacked:
        dp_spec = pl.BlockSpec((dproj.shape[0], ROW_TILE, dproj.shape[-1]), lambda i: (0, i, 0))
    else:
        dp_spec = pl.BlockSpec((ROW_TILE, N_CHIPS * nl), lambda i: (i, 0))
    return _call(
        body, name=name, grid=(t // ROW_TILE,),
        in_specs=[dp_spec, pl.BlockSpec((N_CHIPS, d, nl), lambda i: (0, 0, 0)), row, vec, row],
        out_specs=[row, vec],
        out_shape=[jax.ShapeDtypeStruct((t, d), F32), jax.ShapeDtypeStruct((1, d), F32)],
        args=(dproj, wg, x, g, dres), exchange=exchange)


def out_proj(a, wg, x, g, name):
    t, d = x.shape
    kl = wg.shape[1]

    def body(a_ref, w_ref, x_ref, g_ref, mix_ref, xo_ref):
        acc = _dot(a_ref[:, 0:kl], w_ref[0])
        for c in range(1, N_CHIPS):
            acc += _dot(a_ref[:, c * kl:(c + 1) * kl], w_ref[c])
        mix_ref[...] = acc
        xo_ref[...] = x_ref[...] + _rms(acc, g_ref[...])

    row = pl.BlockSpec((ROW_TILE, d), lambda i: (i, 0))
    return pl.pallas_call(
        body, name=name, grid=(t // ROW_TILE,),
        in_specs=[row, pl.BlockSpec((N_CHIPS, kl, d), lambda i: (0, 0, 0)), row,
                  pl.BlockSpec((1, d), lambda i: (0, 0))],
        out_specs=[row, row],
        out_shape=[jax.ShapeDtypeStruct((t, d), F32), jax.ShapeDtypeStruct((t, d), F32)],
        compiler_params=_params(("arbitrary",)),
    )(a, wg, x, g)


def out_proj_bwd(dxo, mix, g, wg, name):
    t, d = mix.shape
    kl = wg.shape[1]

    def body(dxo_ref, mix_ref, g_ref, w_ref, dmix_ref, da_ref, dg_ref):
        dmix, dg = _rms_bwd(mix_ref[...], g_ref[...], dxo_ref[...])
        dmb = dmix.astype(BF16)
        dmix_ref[...] = dmb
        for c in range(N_CHIPS):
            da_ref[:, c * kl:(c + 1) * kl] = _dot_nt(dmb, w_ref[c])
        _accumulate(dg_ref, dg, pl.program_id(0) == 0)

    row = pl.BlockSpec((ROW_TILE, d), lambda i: (i, 0))
    vec = pl.BlockSpec((1, d), lambda i: (0, 0))
    return pl.pallas_call(
        body, name=name, grid=(t // ROW_TILE,),
        in_specs=[row, row, vec, pl.BlockSpec((N_CHIPS, kl, d), lambda i: (0, 0, 0))],
        out_specs=[row, row, vec],
        out_shape=[jax.ShapeDtypeStruct((t, d), BF16), jax.ShapeDtypeStruct((t, d), F32),
                   jax.ShapeDtypeStruct((1, d), F32)],
        compiler_params=_params(("arbitrary",)),
    )(dxo, mix, g, wg)


def ffn_fwd(x, gpre, w1g, w2g, gpost, name, exchange=None, target=None):
    t, d = x.shape
    hc = w1g.shape[2]
    with_loss = target is not None

    def body(x_ref, gpre_ref, w1_ref, w2_ref, gpost_ref, *rest):
        if with_loss:
            t_ref, xo_ref, h_ref, a_ref, y_ref, l_ref, acc = rest
        else:
            xo_ref, h_ref, a_ref, y_ref, acc = rest
        i, c = pl.program_id(0), pl.program_id(1)

        @pl.when(c == 0)
        def _():
            h_ref[...] = _rms(x_ref[...], gpre_ref[...]).astype(BF16)

        a = _dot(h_ref[...], w1_ref[...])
        a_ref[...] = a.astype(BF16)
        r = jnp.square(jnp.maximum(a, 0.0)).astype(BF16)
        _accumulate(acc, _dot(r, w2_ref[...]), c == 0)

        @pl.when(c == N_CHIPS - 1)
        def _():
            y = acc[...]
            y_ref[...] = y
            xo = x_ref[...] + _rms(y, gpost_ref[...])
            if with_loss:
                e = xo - t_ref[...]
                xo_ref[...] = e * (1.0 / d)
                part = jnp.sum(jnp.sum(e * e, axis=-1, keepdims=True), axis=0, keepdims=True) * (0.5 / d)
                _accumulate(l_ref, part, i == 0)
            else:
                xo_ref[...] = xo

    row = pl.BlockSpec((FFN_ROWS, d), lambda i, c: (i, 0))
    vec = pl.BlockSpec((1, d), lambda i, c: (0, 0))
    one = pl.BlockSpec((1, 1), lambda i, c: (0, 0))
    return _call(
        body, name=name, grid=(t // FFN_ROWS, N_CHIPS),
        in_specs=[row, vec,
                  pl.BlockSpec((None, d, hc), lambda i, c: (c, 0, 0)),
                  pl.BlockSpec((None, hc, d), lambda i, c: (c, 0, 0)), vec] + ([row] if with_loss else []),
        out_specs=[row, row, pl.BlockSpec((FFN_ROWS, hc), lambda i, c: (i, c)), row] + ([one] if with_loss else []),
        out_shape=[jax.ShapeDtypeStruct((t, d), F32), jax.ShapeDtypeStruct((t, d), BF16),
                   jax.ShapeDtypeStruct((t, N_CHIPS * hc), BF16), jax.ShapeDtypeStruct((t, d), F32)]
        + ([jax.ShapeDtypeStruct((1, 1), F32)] if with_loss else []),
        scratch_shapes=[pltpu.VMEM((FFN_ROWS, d), F32)],
        args=(x, gpre, w1g, w2g, gpost) + ((target,) if with_loss else ()), exchange=exchange)


def ffn_bwd(dxo, x, y, a, gpre, gpost, w1g, w2g, name, exchange=None):
    t, d = x.shape
    hc = w1g.shape[2]

    def body(dxo_ref, x_ref, y_ref, a_ref, gpre_ref, gpost_ref, w1_ref, w2_ref,
             dxi_ref, dy_ref, da_ref, dgpre_ref, dgpost_ref, acc):
        i, c = pl.program_id(0), pl.program_id(1)

        @pl.when(c == 0)
        def _():
            dy, dg = _rms_bwd(y_ref[...], gpost_ref[...], dxo_ref[...])
            dy_ref[...] = dy.astype(BF16)
            _accumulate(dgpost_ref, dg, i == 0)

        dr = _dot_nt(dy_ref[...], w2_ref[...])
        da = (dr * (2.0 * jnp.maximum(a_ref[...].astype(F32), 0.0))).astype(BF16)
        da_ref[...] = da
        _accumulate(acc, _dot_nt(da, w1_ref[...]), c == 0)

        @pl.when(c == N_CHIPS - 1)
        def _():
            dx, dg = _rms_bwd(x_ref[...], gpre_ref[...], acc[...])
            dxi_ref[...] = dxo_ref[...] + dx
            _accumulate(dgpre_ref, dg, i == 0)

    row = pl.BlockSpec((ROW_TILE, d), lambda i, c: (i, 0))
    vec = pl.BlockSpec((1, d), lambda i, c: (0, 0))
    hid = pl.BlockSpec((ROW_TILE, hc), lambda i, c: (i, c))
    return _call(
        body, name=name, grid=(t // ROW_TILE, N_CHIPS),
        in_specs=[row, row, row, hid, vec, vec,
                  pl.BlockSpec((None, d, hc), lambda i, c: (c, 0, 0)),
                  pl.BlockSpec((None, hc, d), lambda i, c: (c, 0, 0))],
        out_specs=[row, row, hid, vec, vec],
        out_shape=[jax.ShapeDtypeStruct((t, d), F32), jax.ShapeDtypeStruct((t, d), BF16),
                   jax.ShapeDtypeStruct((t, N_CHIPS * hc), BF16),
                   jax.ShapeDtypeStruct((1, d), F32), jax.ShapeDtypeStruct((1, d), F32)],
        scratch_shapes=[pltpu.VMEM((ROW_TILE, d), F32)],
        args=(dxo, x, y, a, gpre, gpost, w1g, w2g), exchange=exchange)


def weight_grad(a, b, chunked, bk, bn, relu2, name, exchange=None):
    t = a.shape[0]
    a_on = chunked == "a"
    rows = min(t, WGRAD_ROWS)
    n_steps = t // rows

    def body(a_ref, b_ref, o_ref, acc):
        s = pl.program_id(1)
        av = a_ref[...]
        if relu2:
            av = jnp.square(jnp.maximum(av.astype(F32), 0.0))
        _accumulate(acc, _dot_tn(av.astype(BF16), b_ref[...].astype(BF16)), s == 0)

        @pl.when(s == n_steps - 1)
        def _():
            o_ref[...] = acc[...].astype(BF16)

    res = _call(
        body, name=name, grid=(N_CHIPS, n_steps),
        in_specs=[pl.BlockSpec((rows, bk), (lambda c, s: (s, c)) if a_on else (lambda c, s: (s, 0))),
                  pl.BlockSpec((rows, bn), (lambda c, s: (s, 0)) if a_on else (lambda c, s: (s, c)))],
        out_specs=[pl.BlockSpec((None, bk, bn), lambda c, s: (c, 0, 0))],
        out_shape=[jax.ShapeDtypeStruct((N_CHIPS, bk, bn), BF16)],
        scratch_shapes=[pltpu.VMEM((bk, bn), F32)],
        args=(a, b), exchange=exchange)
    return res[0] if exchange is None else res


def weight_grad_stacked(a, b3, bn, name):
    t, bk = a.shape
    width = b3.shape[-1]
    piece = math.gcd(bn, width)
    rows = min(t, WGRAD_ROWS)
    n_steps = t // rows

    def body(a_ref, b_ref, o_hbm, acc, staged, sem):
        s, c = pl.program_id(0), pl.program_id(1)
        av = a_ref[...].astype(BF16)
        for chunk in range(N_CHIPS):
            @pl.when(c == chunk)
            def _(chunk=chunk):
                cols = [divmod(chunk * bn + k * piece, width) for k in range(bn // piece)]
                b = jnp.concatenate([b_ref[p, :, lo:lo + piece] for p, lo in cols], axis=1).astype(BF16)
                _accumulate(acc.at[chunk], _dot_tn(av, b), s == 0)

                @pl.when(s == n_steps - 1)
                def _():
                    staged[...] = acc[chunk].astype(BF16)
                    copy = pltpu.make_async_copy(staged, o_hbm.at[chunk], sem)
                    copy.start()
                    copy.wait()

    return pl.pallas_call(
        body, name=name, grid=(n_steps, N_CHIPS),
        in_specs=[pl.BlockSpec((rows, bk), lambda s, c: (s, 0)),
                  pl.BlockSpec((b3.shape[0], rows, width), lambda s, c: (0, s, 0))],
        out_specs=ANY,
        out_shape=jax.ShapeDtypeStruct((N_CHIPS, bk, bn), BF16),
        scratch_shapes=[pltpu.VMEM((N_CHIPS, bk, bn), F32), pltpu.VMEM((bk, bn), BF16), pltpu.SemaphoreType.DMA],
        compiler_params=_params(("arbitrary", "arbitrary")),
    )(a, b3)


def _hgrn2_chunk(st, qs, fls, ivs, gls, l0, l1, l2, ng):
    nsub = len(qs)
    mx = jnp.maximum(jnp.maximum(l0, l1), l2)
    e0, e1, e2 = jnp.exp(l0 - mx), jnp.exp(l1 - mx), jnp.exp(l2 - mx)
    lb = e0 / (e0 + e1 + e2)
    rows = lax.broadcasted_iota(jnp.int32, (A_SUB, A_SUB), 0)
    cols = lax.broadcasted_iota(jnp.int32, (A_SUB, A_SUB), 1)
    tri = (rows >= cols).astype(F32)
    keep = (lax.broadcasted_iota(jnp.int32, (A_SUB, A_SUB, A_DK), 0)
            >= lax.broadcasted_iota(jnp.int32, (A_SUB, A_SUB, A_DK), 1))
    base = jnp.zeros_like(l0)
    bases, gs, ks, qfs = [], [], [], []
    for i in range(nsub):
        f = lb + (1.0 - lb) * jax.nn.sigmoid(fls[i])
        logf = jnp.log(f)
        bases.append(base)
        gs.append(base + jnp.dot(tri, logf, precision=lax.Precision.HIGHEST, preferred_element_type=F32))
        base = base + jnp.sum(logf, axis=0, keepdims=True)
        ks.append(1.0 - f)
        qfs.append(jax.nn.silu(qs[i]))
    g_last = base
    stb = st.astype(BF16)
    outs = []
    for i in range(nsub):
        o = _dot_nt((qfs[i] * jnp.exp(gs[i])).astype(BF16), stb)
        if i > 0:
            qt = (qfs[i] * jnp.exp(gs[i] - bases[i])).astype(BF16)
            kk = jnp.concatenate([ks[j] * jnp.exp(bases[i] - gs[j]) for j in range(i)], axis=0).astype(BF16)
            vv = jnp.concatenate(ivs[:i], axis=0).astype(BF16)
            o = o + _dot(_dot_nt(qt, kk).astype(BF16), vv)
        dec = jnp.exp(jnp.where(keep, gs[i][:, None, :] - gs[i][None, :, :], NEG_BIG))
        s_diag = jnp.sum(qfs[i][:, None, :] * ks[i][None, :, :] * dec, axis=-1)
        o = o + _dot(s_diag.astype(BF16), ivs[i].astype(BF16))
        o = o * lax.rsqrt(jnp.mean(o * o, axis=-1, keepdims=True) + EPS) * ng
        outs.append(o * jax.nn.silu(gls[i]))
    kdec = jnp.concatenate([ks[j] * jnp.exp(g_last - gs[j]) for j in range(nsub)], axis=0).astype(BF16)
    vall = jnp.concatenate(ivs, axis=0).astype(BF16)
    new_st = st * jnp.exp(g_last) + _dot_tn(vall, kdec)
    return new_st, outs


A_MAX_LOG_DECAY = 60.0


def _half_sums(logf):
    n = logf.shape[0]
    first = lax.broadcasted_iota(jnp.int32, logf.shape, 0) < n // 2
    return (jnp.sum(jnp.where(first, logf, 0.0), axis=0, keepdims=True),
            jnp.sum(jnp.where(first, 0.0, logf), axis=0, keepdims=True))


def _split3(x):
    hi = x.astype(BF16)
    r1 = x - hi.astype(F32)
    mid = r1.astype(BF16)
    return hi, mid, (r1 - mid.astype(F32)).astype(BF16)


def _tri_matmul(x, transpose):
    n = x.shape[0]
    r = lax.broadcasted_iota(jnp.int32, (n, n), 0)
    c = lax.broadcasted_iota(jnp.int32, (n, n), 1)
    tri = ((r <= c) if transpose else (r >= c)).astype(BF16)
    hi, mid, lo = _split3(x)
    return (_dot(tri, lo) + _dot(tri, mid)) + _dot(tri, hi)


@jax.custom_vjp
def _cumsum_rows(x):
    return _tri_matmul(x, False)


def _cumsum_rows_fwd(x):
    return _tri_matmul(x, False), None


def _cumsum_rows_bwd(_, dy):
    return (_tri_matmul(dy, True),)


_cumsum_rows.defvjp(_cumsum_rows_fwd, _cumsum_rows_bwd)


def _lower_bound(l0, l1, l2):
    mx = jnp.maximum(jnp.maximum(l0, l1), l2)
    e0, e1, e2 = jnp.exp(l0 - mx), jnp.exp(l1 - mx), jnp.exp(l2 - mx)
    return e0 / (e0 + e1 + e2)


def _b(x):
    return x.astype(BF16)


@jax.custom_vjp
def _mm(a, b):
    return _dot(_b(a), _b(b))


_mm.defvjp(lambda a, b: (_mm(a, b), (a, b)),
           lambda res, d: (_dot_nt(_b(d), _b(res[1])), _dot_tn(_b(res[0]), _b(d))))


@jax.custom_vjp
def _mm_nt(a, b):
    return _dot_nt(_b(a), _b(b))


_mm_nt.defvjp(lambda a, b: (_mm_nt(a, b), (a, b)),
              lambda res, d: (_dot(_b(d), _b(res[1])), _dot_tn(_b(d), _b(res[0]))))


def _dot_split(dot, a, b):
    ah, bh = _b(a), _b(b)
    al, bl = _b(a - ah.astype(F32)), _b(b - bh.astype(F32))
    return (dot(ah, bl) + dot(al, bh)) + dot(ah, bh)


@jax.custom_vjp
def _mm_scores(a, b):
    return _dot_nt(_b(a), _b(b))


_mm_scores.defvjp(lambda a, b: (_mm_scores(a, b), (a, b)),
                  lambda res, d: (_dot_split(_dot, d, res[1]), _dot_split(_dot_tn, d, res[0])))


@jax.custom_vjp
def _mm_tn(a, b):
    return _dot_tn(_b(a), _b(b))


_mm_tn.defvjp(lambda a, b: (_mm_tn(a, b), (a, b)),
              lambda res, d: (_dot_nt(_b(res[1]), _b(d)), _dot(_b(res[0]), _b(d))))


@jax.custom_vjp
def _split_heads(x):
    return tuple(x[:, h * A_DK:(h + 1) * A_DK] for h in range(A_HEADS))


def _split_heads_fwd(x):
    return _split_heads(x), None


def _split_heads_bwd(_, parts):
    return (jnp.concatenate(parts, axis=1),)


_split_heads.defvjp(_split_heads_fwd, _split_heads_bwd)


def _hgrn2_chunk_fast(sts, q, fl, iv, gl, l0, l1, l2, ng):
    lb = _lower_bound(l0, l1, l2)
    f = lb + (1.0 - lb) * jax.nn.sigmoid(fl)
    return _hgrn2_fast_core(sts, q, f, jnp.log(f), iv, gl, ng)


def _hgrn2_fast_core(sts, q, f, logf, iv, gl, ng):
    g = _cumsum_rows(logf)
    g_mid, g_last = _half_sums(logf)
    g_last = g_mid + g_last
    k = 1.0 - f
    qf = jax.nn.silu(q)
    qms = _split_heads(qf * jnp.exp(g - g_mid))
    kms = _split_heads(k * jnp.exp(g_mid - g))
    qgs = _split_heads(qf * jnp.exp(g))
    kds = _split_heads(k * jnp.exp(g_last - g))
    ivs = _split_heads(iv)
    decays = _split_heads(jnp.exp(g_last))
    n = q.shape[0]
    causal = lax.broadcasted_iota(jnp.int32, (n, n), 0) >= lax.broadcasted_iota(jnp.int32, (n, n), 1)
    raw = [_mm_scores(qm, km) for qm, km in zip(qms, kms)]
    inter = [_mm_nt(qg, st) for qg, st in zip(qgs, sts)]
    scores = [jnp.where(causal, s, 0.0) for s in raw]
    os = [a + _mm(s, v) for a, s, v in zip(inter, scores, ivs)]
    new_sts = [st * d + _mm_tn(v, kd) for st, d, v, kd in zip(sts, decays, ivs, kds)]
    os = [o * lax.rsqrt(jnp.mean(o * o, axis=-1, keepdims=True) + EPS) for o in os]
    return new_sts, jnp.concatenate(os, axis=1) * ng * jax.nn.silu(gl)


A_STEP_CHUNKS = 4


def _chunk_rows(j):
    return pl.ds(pl.multiple_of(j * A_CHUNK, A_CHUNK), A_CHUNK)


def _sub_rows(j, i):
    return pl.ds(pl.multiple_of(j * A_CHUNK + i * A_SUB, A_SUB), A_SUB)


def _sub_blocks(ref, head, j):
    lanes = slice(head * A_DK, (head + 1) * A_DK)
    return [ref[_sub_rows(j, i), lanes] for i in range(A_CHUNK // A_SUB)]


def hgrn2_fwd(proj, lb_table, a_norm, batch, name, exchange=None):
    t = proj.shape[0]
    n_steps = t // batch // (A_CHUNK * A_STEP_CHUNKS)
    rows = A_CHUNK * A_STEP_CHUNKS

    def body(q_ref, f_ref, i_ref, g_ref, lb_ref, ng_ref, o_ref, st_ref, dec_ref, st):
        @pl.when(pl.program_id(1) == 0)
        def _():
            st[...] = jnp.zeros_like(st)

        def chunk(j, carry):
            r = _chunk_rows(j)
            st_ref[j] = st[...]
            lb = _lower_bound(lb_ref[0:1, :], lb_ref[1:2, :], lb_ref[2:3, :])
            f = lb + (1.0 - lb) * jax.nn.sigmoid(f_ref[r, :])
            logf = jnp.log(f)
            decay = jnp.minimum(*_half_sums(logf))
            dec_ref[j] = decay
            mild = jnp.min(decay) >= -A_MAX_LOG_DECAY

            @pl.when(mild)
            def _():
                new_sts, o = _hgrn2_fast_core([st[h] for h in range(A_HEADS)], q_ref[r, :], f, logf,
                                              i_ref[r, :], g_ref[r, :], ng_ref[...])
                for h in range(A_HEADS):
                    st[h] = new_sts[h]
                o_ref[r, :] = o.astype(BF16)

            @pl.when(jnp.logical_not(mild))
            def _():
                for h in range(A_HEADS):
                    lanes = slice(h * A_DK, (h + 1) * A_DK)
                    new_st, outs = _hgrn2_chunk(
                        st[h], _sub_blocks(q_ref, h, j), _sub_blocks(f_ref, h, j), _sub_blocks(i_ref, h, j),
                        _sub_blocks(g_ref, h, j), lb_ref[0:1, lanes], lb_ref[1:2, lanes], lb_ref[2:3, lanes],
                        ng_ref[:, lanes])
                    st[h] = new_st
                    for i, o in enumerate(outs):
                        o_ref[_sub_rows(j, i), lanes] = o.astype(BF16)

            return carry

        lax.fori_loop(0, A_STEP_CHUNKS, chunk, 0)

    def part(k):
        return pl.BlockSpec((rows, A_WIDTH), lambda b, n: (b * n_steps + n, k))

    return _call(
        body, name=name, grid=(batch, n_steps),
        in_specs=[part(0), part(1), part(2), part(3),
                  pl.BlockSpec((3, A_WIDTH), lambda b, n: (0, 0)), pl.BlockSpec((1, A_WIDTH), lambda b, n: (0, 0))],
        out_specs=[part(0),
                   pl.BlockSpec((A_STEP_CHUNKS, A_HEADS, A_DK, A_DK), lambda b, n: (b * n_steps + n, 0, 0, 0)),
                   pl.BlockSpec((A_STEP_CHUNKS, 1, A_WIDTH), lambda b, n: (b * n_steps + n, 0, 0))],
        out_shape=[jax.ShapeDtypeStruct((t, A_WIDTH), BF16),
                   jax.ShapeDtypeStruct((t // A_CHUNK, A_HEADS, A_DK, A_DK), F32),
                   jax.ShapeDtypeStruct((t // A_CHUNK, 1, A_WIDTH), F32)],
        scratch_shapes=[pltpu.VMEM((A_HEADS, A_DK, A_DK), F32)],
        args=(proj, proj, proj, proj, lb_table, a_norm), exchange=exchange)


def hgrn2_bwd(proj, states, decays, lb_table, a_norm, do, batch, name, exchange=None):
    t = proj.shape[0]
    n_steps = t // batch // (A_CHUNK * A_STEP_CHUNKS)
    rows = A_CHUNK * A_STEP_CHUNKS

    def body(q_ref, f_ref, i_ref, g_ref, st_ref, dec_ref, lb_ref, ng_ref, do_ref, dp_ref, dlb_ref, dng_ref, dst):
        @pl.when(jnp.logical_and(pl.program_id(0) == 0, pl.program_id(1) == 0))
        def _():
            dlb_ref[...] = jnp.zeros_like(dlb_ref)
            dng_ref[...] = jnp.zeros_like(dng_ref)

        @pl.when(pl.program_id(1) == 0)
        def _():
            dst[...] = jnp.zeros_like(dst)

        def chunk(jj, carry):
            j = A_STEP_CHUNKS - 1 - jj
            r = _chunk_rows(j)
            mild = jnp.min(dec_ref[j]) >= -A_MAX_LOG_DECAY

            @pl.when(mild)
            def _():
                _, vjp = jax.vjp(
                    _hgrn2_chunk_fast, [st_ref[j, h] for h in range(A_HEADS)], q_ref[r, :], f_ref[r, :],
                    i_ref[r, :], g_ref[r, :], lb_ref[0:1, :], lb_ref[1:2, :], lb_ref[2:3, :], ng_ref[...])
                d_sts, dq, df, di, dg, dl0, dl1, dl2, dng = vjp(
                    ([dst[h] for h in range(A_HEADS)], do_ref[r, :].astype(F32)))
                for h in range(A_HEADS):
                    dst[h] = d_sts[h]
                for k, part in enumerate((dq, df, di, dg)):
                    dp_ref[r, k * A_WIDTH:(k + 1) * A_WIDTH] = part
                for row, val in enumerate((dl0, dl1, dl2)):
                    dlb_ref[row:row + 1, :] += val
                dng_ref[...] += dng

            @pl.when(jnp.logical_not(mild))
            def _():
                for h in range(A_HEADS):
                    lanes = slice(h * A_DK, (h + 1) * A_DK)
                    _, vjp = jax.vjp(
                        _hgrn2_chunk, st_ref[j, h], _sub_blocks(q_ref, h, j), _sub_blocks(f_ref, h, j),
                        _sub_blocks(i_ref, h, j), _sub_blocks(g_ref, h, j), lb_ref[0:1, lanes], lb_ref[1:2, lanes],
                        lb_ref[2:3, lanes], ng_ref[:, lanes])
                    douts = [x.astype(F32) for x in _sub_blocks(do_ref, h, j)]
                    d_st, dqs, dfs, dis, dgs, dl0, dl1, dl2, dng = vjp((dst[h], douts))
                    dst[h] = d_st
                    for k, parts in enumerate((dqs, dfs, dis, dgs)):
                        for i in range(A_CHUNK // A_SUB):
                            dp_ref[_sub_rows(j, i), k * A_WIDTH + h * A_DK:k * A_WIDTH + (h + 1) * A_DK] = parts[i]
                    for row, val in enumerate((dl0, dl1, dl2)):
                        dlb_ref[row:row + 1, lanes] += val
                    dng_ref[:, lanes] += dng

            return carry

        lax.fori_loop(0, A_STEP_CHUNKS, chunk, 0)

    def rev(b, n):
        return b * n_steps + (n_steps - 1 - n)

    def part(k):
        return pl.BlockSpec((rows, A_WIDTH), lambda b, n: (rev(b, n), k))

    const3 = pl.BlockSpec((3, A_WIDTH), lambda b, n: (0, 0))
    const1 = pl.BlockSpec((1, A_WIDTH), lambda b, n: (0, 0))
    return _call(
        body, name=name, grid=(batch, n_steps),
        in_specs=[part(0), part(1), part(2), part(3),
                  pl.BlockSpec((A_STEP_CHUNKS, A_HEADS, A_DK, A_DK), lambda b, n: (rev(b, n), 0, 0, 0)),
                  pl.BlockSpec((A_STEP_CHUNKS, 1, A_WIDTH), lambda b, n: (rev(b, n), 0, 0)),
                  const3, const1, part(0)],
        out_specs=[pl.BlockSpec((rows, 4 * A_WIDTH), lambda b, n: (rev(b, n), 0)), const3, const1],
        out_shape=[jax.ShapeDtypeStruct((t, 4 * A_WIDTH + 2 * B_WIDTH), F32),
                   jax.ShapeDtypeStruct((3, A_WIDTH), F32), jax.ShapeDtypeStruct((1, A_WIDTH), F32)],
        scratch_shapes=[pltpu.VMEM((A_HEADS, A_DK, A_DK), F32)],
        args=(proj, proj, proj, proj, states, decays, lb_table, a_norm, do), exchange=exchange)


B_GDIM = B_WIDTH // B_GROUPS
B_ROWS = 512


def _gmlp_chunk(ubs, vbs, lngs, lnbs, ws, bcols):
    vs = [jax.nn.gelu(v) for v in vbs]
    mu = sum(jnp.sum(v, axis=-1, keepdims=True) for v in vs) * (1.0 / B_WIDTH)
    var = sum(jnp.sum(jnp.square(v - mu), axis=-1, keepdims=True) for v in vs) * (1.0 / B_WIDTH)
    rstd = lax.rsqrt(var + EPS)
    tril = (lax.broadcasted_iota(jnp.int32, (B_CHUNK, B_CHUNK), 0)
            >= lax.broadcasted_iota(jnp.int32, (B_CHUNK, B_CHUNK), 1))
    outs = []
    for g in range(B_GROUPS):
        vn = (vs[g] - mu) * rstd * lngs[g] + lnbs[g]
        w = jnp.where(tril, ws[g], 0.0).astype(BF16)
        outs.append(jax.nn.gelu(ubs[g]) * (_dot(w, vn.astype(BF16)) + bcols[g]))
    return outs


def _gmlp_args(u_ref, v_ref, lng_ref, lnb_ref, w_ref, bt_ref, rows):
    def groups(ref):
        return [ref[rows, g * B_GDIM:(g + 1) * B_GDIM] for g in range(B_GROUPS)]

    def vec(ref):
        return [ref[:, g * B_GDIM:(g + 1) * B_GDIM] for g in range(B_GROUPS)]

    return (groups(u_ref), groups(v_ref), vec(lng_ref), vec(lnb_ref),
            [w_ref[g] for g in range(B_GROUPS)], [bt_ref[:, g:g + 1] for g in range(B_GROUPS)])


def gmlp_fwd(proj, oa, ln_g, ln_b, w, bias_t, name, exchange=None):
    t = proj.shape[0]

    def body(u_ref, v_ref, oa_ref, lng_ref, lnb_ref, w_ref, bt_ref, o_ref):
        o_ref[:, 0:A_WIDTH] = oa_ref[...]
        for n in range(B_ROWS // B_CHUNK):
            rows = slice(n * B_CHUNK, (n + 1) * B_CHUNK)
            outs = _gmlp_chunk(*_gmlp_args(u_ref, v_ref, lng_ref, lnb_ref, w_ref, bt_ref, rows))
            for g, o in enumerate(outs):
                o_ref[rows, A_WIDTH + g * B_GDIM:A_WIDTH + (g + 1) * B_GDIM] = o.astype(BF16)

    vec = pl.BlockSpec((1, B_WIDTH), lambda i: (0, 0))
    return _call(
        body, name=name, grid=(t // B_ROWS,),
        in_specs=[pl.BlockSpec((B_ROWS, B_WIDTH), lambda i: (i, 4)), pl.BlockSpec((B_ROWS, B_WIDTH), lambda i: (i, 5)),
                  pl.BlockSpec((B_ROWS, A_WIDTH), lambda i: (i, 0)), vec, vec,
                  pl.BlockSpec((B_GROUPS, B_CHUNK, B_CHUNK), lambda i: (0, 0, 0)),
                  pl.BlockSpec((B_CHUNK, B_GROUPS), lambda i: (0, 0))],
        out_specs=[pl.BlockSpec((B_ROWS, A_WIDTH + B_WIDTH), lambda i: (i, 0))],
        out_shape=[jax.ShapeDtypeStruct((t, A_WIDTH + B_WIDTH), BF16)],
        args=(proj, proj, oa, ln_g, ln_b, w, bias_t), exchange=exchange)


def gmlp_bwd(proj, dmixin, ln_g, ln_b, w, bias_t, dproj, name, exchange=None):
    t = proj.shape[0]

    def body(u_ref, v_ref, do_ref, lng_ref, lnb_ref, w_ref, bt_ref, dp_in_ref,
             dp_ref, dlng_ref, dlnb_ref, dw_ref, dbt_ref):
        del dp_in_ref

        @pl.when(pl.program_id(0) == 0)
        def _():
            for ref in (dlng_ref, dlnb_ref, dw_ref, dbt_ref):
                ref[...] = jnp.zeros_like(ref)

        for n in range(B_ROWS // B_CHUNK):
            rows = slice(n * B_CHUNK, (n + 1) * B_CHUNK)
            _, vjp = jax.vjp(_gmlp_chunk, *_gmlp_args(u_ref, v_ref, lng_ref, lnb_ref, w_ref, bt_ref, rows))
            douts = [do_ref[rows, g * B_GDIM:(g + 1) * B_GDIM] for g in range(B_GROUPS)]
            dus, dvs, dlngs, dlnbs, dws, dbs = vjp(douts)
            for g in range(B_GROUPS):
                lanes = slice(g * B_GDIM, (g + 1) * B_GDIM)
                dp_ref[rows, lanes] = dus[g]
                dp_ref[rows, B_WIDTH + g * B_GDIM:B_WIDTH + (g + 1) * B_GDIM] = dvs[g]
                dlng_ref[:, lanes] += dlngs[g]
                dlnb_ref[:, lanes] += dlnbs[g]
                dw_ref[g] += dws[g]
                dbt_ref[:, g:g + 1] += dbs[g]

    vec = pl.BlockSpec((1, B_WIDTH), lambda i: (0, 0))
    wspec = pl.BlockSpec((B_GROUPS, B_CHUNK, B_CHUNK), lambda i: (0, 0, 0))
    bspec = pl.BlockSpec((B_CHUNK, B_GROUPS), lambda i: (0, 0))
    return _call(
        body, name=name, grid=(t // B_ROWS,),
        in_specs=[pl.BlockSpec((B_ROWS, B_WIDTH), lambda i: (i, 4)), pl.BlockSpec((B_ROWS, B_WIDTH), lambda i: (i, 5)),
                  pl.BlockSpec((B_ROWS, B_WIDTH), lambda i: (i, 1)), vec, vec, wspec, bspec,
                  pl.BlockSpec(memory_space=pl.ANY)],
        out_specs=[pl.BlockSpec((B_ROWS, 2 * B_WIDTH), lambda i: (i, 2)), vec, vec, wspec, bspec],
        out_shape=[jax.ShapeDtypeStruct(dproj.shape, F32), jax.ShapeDtypeStruct((1, B_WIDTH), F32),
                   jax.ShapeDtypeStruct((1, B_WIDTH), F32), jax.ShapeDtypeStruct((B_GROUPS, B_CHUNK, B_CHUNK), F32),
                   jax.ShapeDtypeStruct((B_CHUNK, B_GROUPS), F32)],
        aliases={7: 0}, args=(proj, proj, dmixin, ln_g, ln_b, w, bias_t, dproj), exchange=exchange)


C_FWD_BLOCKS = 8
C_BWD_BLOCKS = 8
C_PAIR = 2 * C_HEAD_DIM
C_PAIRS = C_HEADS // 2
C_SCALE = 1.0 / math.sqrt(C_HEAD_DIM)
C_ROT_DIM = 2 * C_ROT_HALF
ROPE_ROWS = 1024


def rope_tables(pos_col, name):
    t = pos_col.shape[0]

    def body(p_ref, c_ref, a_ref, b_ref):
        lane = jnp.bitwise_and(lax.broadcasted_iota(jnp.int32, (1, C_PAIR), 1), C_HEAD_DIM - 1)
        j = jnp.bitwise_and(lane, C_ROT_HALF - 1).astype(F32)
        inv = jnp.exp(j * (-math.log(ROPE_THETA) / C_ROT_HALF))
        ang = p_ref[...].astype(F32) * inv
        cos, sin = jnp.cos(ang), jnp.sin(ang)
        c_ref[...] = jnp.where(lane < C_ROT_DIM, cos, 1.0)
        a_ref[...] = jnp.where(lane < C_ROT_HALF, -sin, 0.0)
        b_ref[...] = jnp.where(jnp.logical_and(lane >= C_ROT_HALF, lane < C_ROT_DIM), sin, 0.0)

    tab = pl.BlockSpec((ROPE_ROWS, C_PAIR), lambda i: (i, 0))
    return pl.pallas_call(
        body, name=name, grid=(t // ROPE_ROWS,),
        in_specs=[pl.BlockSpec((ROPE_ROWS, 1), lambda i: (i, 0))],
        out_specs=[tab, tab, tab],
        out_shape=[jax.ShapeDtypeStruct((t, C_PAIR), F32)] * 3,
        compiler_params=_params(("arbitrary",)),
    )(pos_col)


def _rope(x, c, a, b):
    return x * c + pltpu.roll(x, C_PAIR - C_ROT_HALF, 1) * a + pltpu.roll(x, C_ROT_HALF, 1) * b


def _rope_t(d, c, a, b):
    return d * c + pltpu.roll(d * a, C_ROT_HALF, 1) + pltpu.roll(d * b, C_PAIR - C_ROT_HALF, 1)


C_RES = 16


def _residue_major(a, batch):
    return a.reshape(batch, SEQ // C_RES, C_RES, -1).transpose(0, 2, 1, 3).reshape(a.shape)


def _sequence_order(a, batch):
    return a.reshape(batch, C_RES, SEQ // C_RES, -1).transpose(0, 2, 1, 3).reshape(a.shape)


def _block_pieces(idx, dil):
    nblk = SEQ // dil // C_BLOCK
    r, n = idx // nblk, idx % nblk
    per = C_RES // dil
    size = C_BLOCK // per

    def pieces(blk):
        return [((dil * a + r) * (SEQ // C_RES) + size * blk, size) for a in range(per)]

    return pieces(n), pieces(jnp.maximum(n - 1, 0)), n > 0


def _get_rows(ref, pieces):
    return jnp.concatenate([ref[pl.ds(pl.multiple_of(start, 8), size), :] for start, size in pieces], axis=0)


def _set_rows(ref, pieces, val, add=False):
    for k, (start, size) in enumerate(pieces):
        rows = pl.ds(pl.multiple_of(start, 8), size)
        part = val[k * size:(k + 1) * size]
        ref[rows, :] = ref[rows, :] + part if add else part


def _head_masks():
    low = lax.broadcasted_iota(jnp.int32, (1, C_PAIR), 1) < C_HEAD_DIM
    return low, jnp.logical_not(low)


def _attn_mask(has_prev, dil):
    per = C_RES // dil
    size = C_BLOCK // per

    def position(x):
        x = jnp.bitwise_and(x, C_BLOCK - 1)
        return per * jnp.bitwise_and(x, size - 1) + x // size

    j = lax.broadcasted_iota(jnp.int32, (2 * C_BLOCK, 2 * C_BLOCK), 1)
    pi = position(lax.broadcasted_iota(jnp.int32, (2 * C_BLOCK, 2 * C_BLOCK), 0))
    pj = position(j)
    own = j < C_BLOCK
    return jnp.logical_or(jnp.logical_and(own, pj <= pi),
                          jnp.logical_and(jnp.logical_and(jnp.logical_not(own), pj >= pi), has_prev))


def _stack_heads(x):
    low, high = _head_masks()
    return jnp.concatenate([jnp.where(low, x, 0.0), jnp.where(high, x, 0.0)], axis=0)


def _unstack_heads(x):
    low, _ = _head_masks()
    return jnp.where(low, x[:C_BLOCK], x[C_BLOCK:])


def attn_fwd(qkv, cos_t, sin_a, sin_b, batch, name, exchange=None):
    t = qkv.shape[0]
    nbr = len(C_DILATIONS)

    def body(q_ref, k_ref, v_ref, c_ref, a_ref, b_ref, o_ref, l_ref, qr_ref, kr_ref, qs, ks, *stats):
        acc, mm, dd = stats[0:nbr], stats[nbr:2 * nbr], stats[2 * nbr:3 * nbr]
        c, a, b = c_ref[...], a_ref[...], b_ref[...]
        qs[...] = _rope(q_ref[...], c, a, b) * C_SCALE
        ks[...] = _rope(k_ref[...], c, a, b)
        qr_ref[...] = qs[...].astype(BF16)
        kr_ref[...] = ks[...].astype(BF16)

        def load(idx, dil):
            own, prev, has_prev = _block_pieces(idx, dil)
            return own, (has_prev, _get_rows(qs, own), _get_rows(ks, own), _get_rows(ks, prev),
                         _get_rows(v_ref, own), _get_rows(v_ref, prev))

        def scores(dil, has_prev, q, k_own, k_prev, v_own, v_prev):
            k_cat = jnp.concatenate([k_own, k_prev], axis=0).astype(BF16)
            return jnp.where(_attn_mask(has_prev, dil), _dot_nt(_stack_heads(q).astype(BF16), k_cat), NEG_BIG)

        def softmax(s):
            m = jnp.max(s, axis=-1, keepdims=True)
            p = jnp.exp(s - m)
            return p.astype(BF16), m, jnp.sum(p, axis=-1, keepdims=True)

        def values(pb, has_prev, q, k_own, k_prev, v_own, v_prev):
            low, high = _head_masks()
            v_cat = jnp.concatenate([v_own, v_prev], axis=0)
            p_wide = jnp.concatenate([pb[:C_BLOCK], pb[C_BLOCK:]], axis=1)
            v_tall = jnp.concatenate([jnp.where(low, v_cat, 0.0), jnp.where(high, v_cat, 0.0)], axis=0).astype(BF16)
            return _dot(p_wide, v_tall)

        for bi, dil in enumerate(C_DILATIONS):
            def pair(i, carry, bi=bi, dil=dil):
                low, _ = _head_masks()
                loaded = [load(C_FWD_BLOCKS * i + k, dil) for k in range(C_FWD_BLOCKS)]
                ss = [scores(dil, *ops) for _, ops in loaded]
                sm = [softmax(s) for s in ss]
                pvs = [values(pb, *ops) for (pb, _, _), (_, ops) in zip(sm, loaded)]
                for (own, _), (_, m, den), pv in zip(loaded, sm, pvs):
                    _set_rows(acc[bi], own, pv)
                    _set_rows(mm[bi], own, jnp.where(low, m[:C_BLOCK], m[C_BLOCK:]))
                    _set_rows(dd[bi], own, jnp.where(low, den[:C_BLOCK], den[C_BLOCK:]))
                return carry

            lax.fori_loop(0, SEQ // C_BLOCK // C_FWD_BLOCKS, pair, 0)
        step = 2 * C_BLOCK
        for r0 in range(0, SEQ, step):
            rr = slice(r0, r0 + step)
            ms = [mm[g][rr, :] for g in range(nbr)]
            m_all = functools.reduce(jnp.maximum, ms)
            ws = [jnp.exp(m - m_all) for m in ms]
            num = sum(acc[g][rr, :] * ws[g] for g in range(nbr))
            den = sum(dd[g][rr, :] * ws[g] for g in range(nbr))
            o_ref[rr, :] = (num / den).astype(BF16)
            l_ref[rr, :] = m_all + jnp.log(den)

    def col(k):
        return pl.BlockSpec((SEQ, C_PAIR), lambda b, p: (b, k * C_PAIRS + p))

    tab = pl.BlockSpec((SEQ, C_PAIR), lambda b, p: (b, 0))
    return _call(
        body, name=name, grid=(batch, C_PAIRS),
        in_specs=[col(0), col(1), col(2), tab, tab, tab],
        out_specs=[col(0), col(0), col(0), col(0)],
        out_shape=[jax.ShapeDtypeStruct((t, D_MODEL), BF16), jax.ShapeDtypeStruct((t, D_MODEL), F32),
                   jax.ShapeDtypeStruct((t, D_MODEL), BF16), jax.ShapeDtypeStruct((t, D_MODEL), BF16)],
        scratch_shapes=[pltpu.VMEM((SEQ, C_PAIR), F32)] * (2 + 3 * nbr),
        args=(qkv, qkv, qkv, cos_t, sin_a, sin_b), exchange=exchange)


def attn_bwd(qr, kr, qkv, cos_t, sin_a, sin_b, o, lse, do, batch, name, exchange=None):
    t = qkv.shape[0]

    def body(q_ref, k_ref, v_ref, c_ref, a_ref, b_ref, o_ref, l_ref, do_ref, dqkv_ref, qs, ks, dqs, dks, dvs, dlt):
        low, _ = _head_masks()
        c, a, b = c_ref[...], a_ref[...], b_ref[...]
        qs[...] = q_ref[...].astype(F32)
        ks[...] = k_ref[...].astype(F32)
        prod = do_ref[...] * o_ref[...].astype(F32)
        s_low = jnp.sum(jnp.where(low, prod, 0.0), axis=-1, keepdims=True)
        s_all = jnp.sum(prod, axis=-1, keepdims=True)
        dlt[...] = jnp.where(low, s_low, s_all - s_low)
        dqs[...] = jnp.zeros_like(dqs)
        dks[...] = jnp.zeros_like(dks)
        dvs[...] = jnp.zeros_like(dvs)

        def load(idx, dil):
            own, prev, has_prev = _block_pieces(idx, dil)
            return (own, prev), (has_prev, _get_rows(qs, own), _get_rows(do_ref, own), _get_rows(ks, own),
                                 _get_rows(ks, prev), _get_rows(v_ref, own), _get_rows(v_ref, prev),
                                 _get_rows(l_ref, own), _get_rows(dlt, own))

        def operands(dil, has_prev, q, do, k_own, k_prev, v_own, v_prev, l_full, d_full):
            lcol = jnp.concatenate([l_full[:, 0:1], l_full[:, C_HEAD_DIM:C_HEAD_DIM + 1]], axis=0)
            dcol = jnp.concatenate([d_full[:, 0:1], d_full[:, C_HEAD_DIM:C_HEAD_DIM + 1]], axis=0)
            return (_stack_heads(q).astype(BF16), _stack_heads(do).astype(BF16),
                    jnp.concatenate([k_own, k_prev], axis=0).astype(BF16),
                    jnp.concatenate([v_own, v_prev], axis=0).astype(BF16), lcol, dcol, _attn_mask(has_prev, dil))

        for dil in C_DILATIONS:
            def pair(i, carry, dil=dil):
                loaded = [load(C_BWD_BLOCKS * i + k, dil) for k in range(C_BWD_BLOCKS)]
                ops = [operands(dil, *o) for _, o in loaded]
                ss = [_dot_nt(q_stack, k_cat) for q_stack, _, k_cat, _, _, _, _ in ops]
                dps = [_dot_nt(do_stack, v_cat) for _, do_stack, _, v_cat, _, _, _ in ops]
                ps = [jnp.exp(jnp.where(o[6], s, NEG_BIG) - o[4]) for s, o in zip(ss, ops)]
                dss = [(p * (dp - o[5])).astype(BF16) for p, dp, o in zip(ps, dps, ops)]
                dvs_ = [_dot_tn(p.astype(BF16), o[1]) for p, o in zip(ps, ops)]
                dks_ = [_dot_tn(ds, o[0]) for ds, o in zip(dss, ops)]
                dqs_ = [_unstack_heads(_dot(ds, o[2])) for ds, o in zip(dss, ops)]
                for ((own, prev), _), dq, dk_cat, dv_cat in zip(loaded, dqs_, dks_, dvs_):
                    _set_rows(dqs, own, dq, add=True)
                    _set_rows(dks, own, dk_cat[:C_BLOCK], add=True)
                    _set_rows(dvs, own, dv_cat[:C_BLOCK], add=True)
                    _set_rows(dks, prev, dk_cat[C_BLOCK:], add=True)
                    _set_rows(dvs, prev, dv_cat[C_BLOCK:], add=True)
                return carry

            lax.fori_loop(0, SEQ // C_BLOCK // C_BWD_BLOCKS, pair, 0)
        dqkv_ref[0] = _rope_t(dqs[...] * C_SCALE, c, a, b).astype(BF16)
        dqkv_ref[1] = _rope_t(dks[...], c, a, b).astype(BF16)
        dqkv_ref[2] = dvs[...].astype(BF16)

    def col(k):
        return pl.BlockSpec((SEQ, C_PAIR), lambda b, p: (b, k * C_PAIRS + p))

    tab = pl.BlockSpec((SEQ, C_PAIR), lambda b, p: (b, 0))
    return _call(
        body, name=name, grid=(batch, C_PAIRS),
        in_specs=[col(0), col(0), col(2), tab, tab, tab, col(0), col(0), col(0)],
        out_specs=[pl.BlockSpec((3, SEQ, C_PAIR), lambda b, p: (0, b, p))],
        out_shape=[jax.ShapeDtypeStruct((3, t, D_MODEL), BF16)],
        scratch_shapes=[pltpu.VMEM((SEQ, C_PAIR), F32)] * 6,
        args=(qr, kr, qkv, cos_t, sin_a, sin_b, o, lse, do), exchange=exchange)


def sibling_swap(arrays, name):
    n = len(arrays)

    def body(*refs):
        ins, outs = refs[:n], refs[n:2 * n]
        send_sems, recv_sems = refs[2 * n:]
        x, y, c, _ = _place()
        sends = []
        for a in range(n):
            cp = pltpu.make_async_remote_copy(
                src_ref=ins[a], dst_ref=outs[a], send_sem=send_sems.at[a], recv_sem=recv_sems.at[a],
                device_id=(x, y, 1 - c), device_id_type=MESH)
            cp.start()
            sends.append(cp)
        for cp in sends:
            cp.wait_recv()
        for cp in sends:
            cp.wait_send()

    return pl.pallas_call(
        body, name=name,
        in_specs=[ANY] * n, out_specs=[ANY] * n,
        out_shape=[jax.ShapeDtypeStruct(s.shape, s.dtype) for s in arrays],
        scratch_shapes=[pltpu.SemaphoreType.DMA((n,)), pltpu.SemaphoreType.DMA((n,))],
    )(*arrays)


def allreduce_small(slab, name):
    rows, lanes = slab.shape

    def body(x_ref, out_ref, gath, send_sems, recv_sems, local_sem):
        x, y, c, chips = _place()
        me, sibling = (x, y, c), (x, y, 1 - c)

        def slot(px, py, pc):
            return gath.at[4 * px + 2 * py + pc]

        def copy(k, block, to, src=None):
            return pltpu.make_async_remote_copy(
                src_ref=slot(*block) if src is None else src, dst_ref=slot(*block),
                send_sem=send_sems.at[k], recv_sem=recv_sems.at[k], device_id=to, device_id_type=MESH)

        mine = pltpu.make_async_copy(x_ref, slot(*me), local_sem)
        mine.start()
        first = [copy(0, me, sibling, src=x_ref)]
        first += [copy(1 + j, me, (*chip, c), src=x_ref) for j, chip in enumerate(chips)]
        for cp in first:
            cp.start()
        passed = [copy(4 + j, (*chip, c), sibling) for j, chip in enumerate(chips)]
        for j, chip in enumerate(chips):
            copy(1 + j, (*chip, c), me).wait_recv()
            passed[j].start()
        copy(0, sibling, me).wait_recv()
        for j, chip in enumerate(chips):
            copy(4 + j, (*chip, 1 - c), me).wait_recv()
        for cp in first + passed:
            cp.wait_send()
        mine.wait()
        total = gath[0]
        for d in range(1, N_DEV):
            total = total + gath[d]
        out_ref[...] = total

    return pl.pallas_call(
        body, name=name,
        in_specs=[pl.BlockSpec(memory_space=pltpu.VMEM)],
        out_specs=pl.BlockSpec(memory_space=pltpu.VMEM),
        out_shape=jax.ShapeDtypeStruct((rows, lanes), F32),
        scratch_shapes=[pltpu.VMEM((N_DEV, rows, lanes), F32),
                        pltpu.SemaphoreType.DMA((7,)), pltpu.SemaphoreType.DMA((7,)), pltpu.SemaphoreType.DMA],
    )(slab)


ELT_ROWS = 512


def reduce_slabs(r, name, part=0, parts=1, into=None):
    _, rows, cols = r.shape
    br = min(rows, ELT_ROWS)
    nblk = rows // br

    def body(r_ref, *rest):
        o_ref = rest[-1]
        o_ref[...] = ((r_ref[3].astype(F32) + r_ref[0].astype(F32)) + r_ref[1].astype(F32)) + r_ref[2].astype(F32)

    return pl.pallas_call(
        body, name=name, grid=(nblk,),
        in_specs=[pl.BlockSpec((N_CHIPS, br, cols), lambda i: (0, i, 0))] + ([] if into is None else [ANY]),
        out_specs=pl.BlockSpec((br, cols), lambda i: (part * nblk + i, 0)),
        out_shape=jax.ShapeDtypeStruct((parts * rows, cols), F32),
        input_output_aliases={} if into is None else {1: 0},
        compiler_params=_params(("arbitrary",)),
    )(*([r] if into is None else [r, into]))


def _adamw(w, g, m, v):
    m = ADAM_B1 * m + (1.0 - ADAM_B1) * g
    v = ADAM_B2 * v + (1.0 - ADAM_B2) * jnp.square(g)
    m_hat = m / (1.0 - ADAM_B1 ** ADAM_STEP)
    v_hat = v / (1.0 - ADAM_B2 ** ADAM_STEP)
    delta = -ADAM_LR * (m_hat / (jnp.sqrt(v_hat) + ADAM_EPS) + ADAM_WD * w)
    return delta, m, v


def adamw_big(w, s_mine, s_sibling, m, v, name):
    rows, cols = w.shape

    def body(w_ref, a_ref, b_ref, m_ref, v_ref, g_out, d_out, m_out, v_out):
        g = a_ref[...] + b_ref[...]
        g_out[...] = g
        d_out[...], m_out[...], v_out[...] = _adamw(w_ref[...], g, m_ref[...], v_ref[...])

    blk = pl.BlockSpec((min(rows, ELT_ROWS), cols), lambda i: (i, 0))
    out = jax.ShapeDtypeStruct((rows, cols), F32)
    return pl.pallas_call(
        body, name=name, grid=(rows // min(rows, ELT_ROWS),),
        in_specs=[blk] * 5, out_specs=[blk] * 4, out_shape=[out] * 4,
        compiler_params=_params(("arbitrary",)),
    )(w, s_mine, s_sibling, m, v)


def adamw_small(ws, gs, ms, vs, name):
    n = len(ws)

    def body(*refs):
        w_refs, g_refs, m_refs, v_refs = (refs[k * n:(k + 1) * n] for k in range(4))
        d_out, m_out, v_out = (refs[(4 + k) * n:(5 + k) * n] for k in range(3))
        for i in range(n):
            d_out[i][...], m_out[i][...], v_out[i][...] = _adamw(
                w_refs[i][...], g_refs[i][...], m_refs[i][...], v_refs[i][...])

    outs = [jax.ShapeDtypeStruct(w.shape, F32) for w in ws]
    res = pl.pallas_call(body, name=name, out_shape=outs * 3)(*ws, *gs, *ms, *vs)
    return res[:n], res[n:2 * n], res[2 * n:]


SLAB_LANES = 128
SLAB_ROW_ALIGN = 8


def _pack(parts):
    flat = jnp.concatenate([p.reshape(-1) for p in parts])
    rows = -(-flat.shape[0] // (SLAB_LANES * SLAB_ROW_ALIGN)) * SLAB_ROW_ALIGN
    flat = jnp.pad(flat, (0, rows * SLAB_LANES - flat.shape[0]))
    return flat.reshape(rows, SLAB_LANES)


def _unpack(slab, shapes):
    flat = slab.reshape(-1)
    out, pos = [], 0
    for s in shapes:
        size = math.prod(s)
        out.append(flat[pos:pos + size].reshape(s))
        pos += size
    return out


def kernel(x, positions, norm_mix_pre, norm_mix_post, norm_ffn_pre, norm_ffn_post, w_in_even, lb_table, a_norm, b_ln_g, b_ln_b, b_ws, b_bias, w_out_even, w_in_odd, w_out_odd, w_ff1, w_ff2, loss_target, m_norm_mix_pre, m_norm_mix_post, m_norm_ffn_pre, m_norm_ffn_post, m_w_in_even, m_lb_table, m_a_norm, m_b_ln_g, m_b_ln_b, m_b_ws, m_b_bias, m_w_out_even, m_w_in_odd, m_w_out_odd, m_w_ff1, m_w_ff2, v_norm_mix_pre, v_norm_mix_post, v_norm_ffn_pre, v_norm_ffn_post, v_w_in_even, v_lb_table, v_a_norm, v_b_ln_g, v_b_ln_b, v_b_ws, v_b_bias, v_w_out_even, v_w_in_odd, v_w_out_odd, v_w_ff1, v_w_ff2):
    batch = x.shape[0]
    t = batch * SEQ
    d = D_MODEL
    x0 = x.reshape(t, d)
    target = loss_target.reshape(t, d)

    def gain(p, layer):
        return p[layer:layer + 1]

    def gather(*shards):
        return _Exchange("gather", [w.astype(BF16) for w in shards])

    def scatter(*grads):
        return _Exchange("scatter", grads)

    (win_e,) = exchange_alone(gather(w_in_even[0]), "gather_in_even")
    bias_t = b_bias[0].T
    proj, h0, w1_0 = norm_matmul(x0, gain(norm_mix_pre, 0), win_e, "in_proj_even", exchange=gather(w_ff1[0]))
    oa, states, decays, w2_0 = hgrn2_fwd(proj, lb_table, a_norm, batch, "hgrn2_fwd", exchange=gather(w_ff2[0]))
    mixin, wout_e = gmlp_fwd(proj, oa, b_ln_g, b_ln_b, b_ws[0], bias_t, "gmlp_fwd", exchange=gather(w_out_even[0]))
    mix0, x1 = out_proj(mixin, wout_e, x0, gain(norm_mix_post, 0), "out_proj_even")
    x2, hf0, a0, y0, win_o, wout_o = ffn_fwd(x1, gain(norm_ffn_pre, 0), w1_0, w2_0, gain(norm_ffn_post, 0),
                                             "ffn_fwd_0", exchange=gather(w_in_odd[0], w_out_odd[0]))
    x2p = _residue_major(x2, batch)
    qkv, h1 = norm_matmul(x2p, gain(norm_mix_pre, 1), win_o, "in_proj_odd")
    cos_t, sin_a, sin_b = rope_tables(_residue_major(positions.reshape(t, 1), batch), "rope_tables")
    ao, lse, q_rot, k_rot, w1_1, w2_1 = attn_fwd(qkv, cos_t, sin_a, sin_b, batch, "attn_fwd",
                                                 exchange=gather(w_ff1[1], w_ff2[1]))
    mix1, x3 = out_proj(ao, wout_o, x2p, gain(norm_mix_post, 1), "out_proj_odd")
    dx4, hf1, a1, y1, loss_part = ffn_fwd(x3, gain(norm_ffn_pre, 1), w1_1, w2_1, gain(norm_ffn_post, 1),
                                          "ffn_fwd_1", target=_residue_major(target, batch))

    hc = D_FF // N_CHIPS
    dx3, dy1, da1, dg_fpre1, dg_fpost1 = ffn_bwd(
        dx4, x3, y1, a1, gain(norm_ffn_pre, 1), gain(norm_ffn_post, 1), w1_1, w2_1, "ffn_bwd_1")
    g_w1_1 = weight_grad(hf1, da1, "b", d, hc, False, "wgrad_ff1_1")
    g_w2_1 = weight_grad(a1, dy1, "a", hc, d, True, "wgrad_ff2_1")
    dmix1, dao, dg_mpost1 = out_proj_bwd(dx3, mix1, gain(norm_mix_post, 1), wout_o, "out_proj_bwd_odd")
    g_wout_o = weight_grad(ao, dmix1, "a", d // N_CHIPS, d, False, "wgrad_out_odd")
    dqkv, r_w1_1, r_w2_1, r_wout_o = attn_bwd(q_rot, k_rot, qkv, cos_t, sin_a, sin_b, ao, lse, dao, batch, "attn_bwd",
                                              exchange=scatter(g_w1_1, g_w2_1, g_wout_o))
    dx2p, dg_mpre1 = norm_matmul_bwd(dqkv, win_o, x2p, gain(norm_mix_pre, 1), dx3, "in_proj_bwd_odd")
    dx2 = _sequence_order(dx2p, batch)
    g_win_o = weight_grad_stacked(h1, dqkv, 3 * d // N_CHIPS, "wgrad_in_odd")
    dx1, dy0, da0, dg_fpre0, dg_fpost0, r_win_o = ffn_bwd(
        dx2, x1, y0, a0, gain(norm_ffn_pre, 0), gain(norm_ffn_post, 0), w1_0, w2_0, "ffn_bwd_0",
        exchange=scatter(g_win_o))
    g_w1_0 = weight_grad(hf0, da0, "b", d, hc, False, "wgrad_ff1_0")
    g_w2_0 = weight_grad(a0, dy0, "a", hc, d, True, "wgrad_ff2_0")
    dmix0, dmixin, dg_mpost0 = out_proj_bwd(dx1, mix0, gain(norm_mix_post, 0), wout_e, "out_proj_bwd_even")
    g_wout_e = weight_grad(mixin, dmix0, "a", d // N_CHIPS, d, False, "wgrad_out_even")
    dproj, d_lb, d_anorm, r_w1_0 = hgrn2_bwd(
        proj, states, decays, lb_table, a_norm, dmixin, batch, "hgrn2_bwd", exchange=scatter(g_w1_0))
    dproj, d_lng, d_lnb, d_ws, d_bias_t, r_w2_0 = gmlp_bwd(
        proj, dmixin, b_ln_g, b_ln_b, b_ws[0], bias_t, dproj, "gmlp_bwd", exchange=scatter(g_w2_0))
    g_win_e, r_wout_e = weight_grad(h0, dproj, "b", d, 3 * d // N_CHIPS, False, "wgrad_in_even",
                                    exchange=scatter(g_wout_e))
    dx0, dg_mpre0, r_win_e = norm_matmul_bwd(dproj, win_e, x0, gain(norm_mix_pre, 0), dx1, "in_proj_bwd_even",
                                             exchange=scatter(g_win_e))
    grad_x = dx0.reshape(x.shape)

    s_w1 = reduce_slabs(r_w1_1, "reduce_ff1_1", part=1, parts=2)
    s_w1 = reduce_slabs(r_w1_0, "reduce_ff1_0", part=0, parts=2, into=s_w1)
    s_w2 = reduce_slabs(r_w2_1, "reduce_ff2_1", part=1, parts=2)
    s_w2 = reduce_slabs(r_w2_0, "reduce_ff2_0", part=0, parts=2, into=s_w2)
    sums = [reduce_slabs(r_win_e, "reduce_in_even"), reduce_slabs(r_wout_e, "reduce_out_even"),
            reduce_slabs(r_win_o, "reduce_in_odd"), reduce_slabs(r_wout_o, "reduce_out_odd"), s_w1, s_w2]
    sibling = sibling_swap(sums, "sibling_swap")
    big_w = [w_in_even, w_out_even, w_in_odd, w_out_odd, w_ff1, w_ff2]
    big_m = [m_w_in_even, m_w_out_even, m_w_in_odd, m_w_out_odd, m_w_ff1, m_w_ff2]
    big_v = [v_w_in_even, v_w_out_even, v_w_in_odd, v_w_out_odd, v_w_ff1, v_w_ff2]
    big = []
    for i, (w, m, v) in enumerate(zip(big_w, big_m, big_v)):
        two_d = (-1, w.shape[-1])
        res = adamw_big(w.reshape(two_d), sums[i], sibling[i], m.reshape(two_d), v.reshape(two_d), "adamw_big_%d" % i)
        big.append([r.reshape(w.shape) for r in res])

    small_w = [norm_mix_pre, norm_mix_post, norm_ffn_pre, norm_ffn_post, lb_table, a_norm, b_ln_g, b_ln_b, b_ws, b_bias]
    small_m = [m_norm_mix_pre, m_norm_mix_post, m_norm_ffn_pre, m_norm_ffn_post, m_lb_table, m_a_norm, m_b_ln_g,
               m_b_ln_b, m_b_ws, m_b_bias]
    small_v = [v_norm_mix_pre, v_norm_mix_post, v_norm_ffn_pre, v_norm_ffn_post, v_lb_table, v_a_norm, v_b_ln_g,
               v_b_ln_b, v_b_ws, v_b_bias]
    partial = [jnp.concatenate([dg_mpre0, dg_mpre1]), jnp.concatenate([dg_mpost0, dg_mpost1]),
               jnp.concatenate([dg_fpre0, dg_fpre1]), jnp.concatenate([dg_fpost0, dg_fpost1]),
               d_lb, d_anorm, d_lng, d_lnb, d_ws[None], d_bias_t.T[None]]
    *small_g, loss = _unpack(allreduce_small(_pack(partial + [loss_part]), "allreduce_small"),
                             [w.shape for w in small_w] + [()])
    small_d, small_nm, small_nv = adamw_small(small_w, small_g, small_m, small_v, "adamw_small")

    order = ["norm_mix_pre", "norm_mix_post", "norm_ffn_pre", "norm_ffn_post", "w_in_even", "lb_table", "a_norm",
             "b_ln_g", "b_ln_b", "b_ws", "b_bias", "w_out_even", "w_in_odd", "w_out_odd", "w_ff1", "w_ff2"]
    small_names = ["norm_mix_pre", "norm_mix_post", "norm_ffn_pre", "norm_ffn_post", "lb_table", "a_norm",
                   "b_ln_g", "b_ln_b", "b_ws", "b_bias"]
    big_names = ["w_in_even", "w_out_even", "w_in_odd", "w_out_odd", "w_ff1", "w_ff2"]
    grads, deltas, new_m, new_v = {}, {}, {}, {}
    for i, nm in enumerate(small_names):
        grads[nm], deltas[nm], new_m[nm], new_v[nm] = small_g[i], small_d[i], small_nm[i], small_nv[i]
    for i, nm in enumerate(big_names):
        grads[nm], deltas[nm], new_m[nm], new_v[nm] = big[i]
    return (loss, grad_x, *[grads[n] for n in order], *[deltas[n] for n in order],
            *[new_m[n] for n in order], *[new_v[n] for n in order])
```

```python
import functools
import math

import jax
import jax.numpy as jnp
from jax import lax
from jax.experimental import pallas as pl
from jax.experimental.pallas import tpu as pltpu

F32 = jnp.float32
BF16 = jnp.bfloat16
MESH = pl.DeviceIdType.MESH

D_MODEL = 1024
SEQ = 2048
D_FF = 4096
N_CHIPS = 4
A_WIDTH = 512
A_HEADS = 4
A_DK = 128
A_CHUNK = 64
A_SUB = 16
B_WIDTH = 512
B_GROUPS = 4
B_CHUNK = 128
C_HEADS = 16
C_HEAD_DIM = 64
C_ROT_HALF = 8
C_BLOCK = 128
C_DILATIONS = (1, 4, 16)
ROPE_THETA = 500000.0
EPS = 1e-6
ADAM_LR = 0.001
ADAM_B1 = 0.9
ADAM_B2 = 0.999
ADAM_EPS = 1e-08
ADAM_WD = 0.01
ADAM_STEP = 10

ROW_TILE = 512
FFN_ROWS = 1024
WGRAD_ROWS = 2048
VMEM_LIMIT = 56 * 1024 * 1024
NEG_BIG = -1e30


def _params(sem=None):
    return pltpu.CompilerParams(dimension_semantics=sem, vmem_limit_bytes=VMEM_LIMIT)


def _dot(a, b):
    return jnp.dot(a, b, preferred_element_type=F32)


def _dot_nt(a, b):
    return lax.dot_general(a, b, (((1,), (1,)), ((), ())), preferred_element_type=F32)


def _dot_tn(a, b):
    return lax.dot_general(a, b, (((0,), (0,)), ((), ())), preferred_element_type=F32)


def _rms(x, g):
    r = lax.rsqrt(jnp.mean(x * x, axis=-1, keepdims=True) + EPS)
    return x * r * g


def _rms_bwd(x, g, dy):
    r = lax.rsqrt(jnp.mean(x * x, axis=-1, keepdims=True) + EPS)
    xh = x * r
    dg = jnp.sum(dy * xh, axis=0, keepdims=True)
    dxh = dy * g
    dx = r * (dxh - xh * jnp.mean(dxh * xh, axis=-1, keepdims=True))
    return dx, dg


def _accumulate(ref, val, first):
    @pl.when(first)
    def _():
        ref[...] = val

    @pl.when(jnp.logical_not(first))
    def _():
        ref[...] += val


N_DEV = 8
ANY = pl.BlockSpec(memory_space=pl.ANY)


def _place():
    x, y, c = lax.axis_index("x"), lax.axis_index("y"), lax.axis_index("c")
    return x, y, c, [(1 - x, y), (x, 1 - y), (1 - x, 1 - y)]


class _Exchange:
    def __init__(self, kind, arrays):
        self.kind, self.arrays, self.n = kind, list(arrays), len(arrays)
        per_peer = pltpu.SemaphoreType.DMA((3 * self.n,))
        if kind == "gather":
            self.out_shape = [jax.ShapeDtypeStruct((N_CHIPS,) + a.shape, a.dtype) for a in self.arrays]
            self.scratch = [per_peer, per_peer, pltpu.SemaphoreType.DMA((self.n,)), per_peer, per_peer]
        else:
            self.out_shape = [jax.ShapeDtypeStruct(a.shape, a.dtype) for a in self.arrays]
            self.scratch = [per_peer, per_peer, pltpu.SemaphoreType.DMA((self.n,))]

    def _copies(self, ins, outs, sems):
        send_sems, recv_sems, local_sems = sems[:3]
        x, y, c, chips = _place()
        me = 2 * x + y
        local, remote = [], []
        for a in range(self.n):
            if self.kind == "gather":
                local.append(pltpu.make_async_copy(ins[a], outs[a].at[me], local_sems.at[a]))
                half = self.arrays[a].shape[0] // 2

                def rows(ref, core, half=half):
                    return ref.at[pl.ds(core * half, half)]
            else:
                local.append(pltpu.make_async_copy(ins[a].at[me], outs[a].at[3], local_sems.at[a]))
            for j, (px, py) in enumerate(chips):
                k = 3 * a + j
                peer = 2 * px + py

                def copy(src, dst, to, send_sem=send_sems.at[k], recv_sem=recv_sems.at[k]):
                    return pltpu.make_async_remote_copy(src_ref=src, dst_ref=dst, send_sem=send_sem, recv_sem=recv_sem,
                                                        device_id=to, device_id_type=MESH)

                if self.kind == "gather":
                    sent = copy(rows(ins[a], c), rows(outs[a].at[me], c), (px, py, c))
                    landed = copy(rows(ins[a], c), rows(outs[a].at[peer], c), (px, py, c))
                    on = dict(send_sem=sems[3].at[k], recv_sem=sems[4].at[k])
                    passed = copy(rows(outs[a].at[peer], c), rows(outs[a].at[peer], c), (x, y, 1 - c), **on)
                    handed = copy(rows(outs[a].at[peer], c), rows(outs[a].at[peer], 1 - c), (x, y, 1 - c), **on)
                    remote.append((sent, landed, passed, handed))
                else:
                    sent = copy(ins[a].at[peer], outs[a].at[j], (px, py, c))
                    remote.append((sent, sent, None, None))
        return local, remote

    def start(self, ins, outs, sems):
        local, remote = self._copies(ins, outs, sems)
        for cp in local:
            cp.start()
        for sent, _, _, _ in remote:
            sent.start()

    def finish(self, ins, outs, sems):
        local, remote = self._copies(ins, outs, sems)
        for _, landed, passed, _ in remote:
            landed.wait_recv()
            if passed is not None:
                passed.start()
        for sent, _, passed, handed in remote:
            if passed is not None:
                handed.wait_recv()
                passed.wait_send()
            sent.wait_send()
        for cp in local:
            cp.wait()


class _Swap:
    def __init__(self, arrays):
        self.arrays, self.n = list(arrays), len(arrays)
        self.out_shape = [jax.ShapeDtypeStruct(a.shape, a.dtype) for a in self.arrays]
        self.scratch = [pltpu.SemaphoreType.DMA((self.n,)), pltpu.SemaphoreType.DMA((self.n,))]

    def _copies(self, ins, outs, sems):
        x, y, c, _ = _place()
        return [pltpu.make_async_remote_copy(src_ref=ins[a], dst_ref=outs[a], send_sem=sems[0].at[a],
                                             recv_sem=sems[1].at[a], device_id=(x, y, 1 - c), device_id_type=MESH)
                for a in range(self.n)]

    def start(self, ins, outs, sems):
        for cp in self._copies(ins, outs, sems):
            cp.start()

    def finish(self, ins, outs, sems):
        for cp in self._copies(ins, outs, sems):
            cp.wait_recv()
            cp.wait_send()


class _Both:
    def __init__(self, first, second):
        self.parts = (first, second)
        self.arrays, self.n = first.arrays + second.arrays, first.n + second.n
        self.out_shape = first.out_shape + second.out_shape
        self.scratch = first.scratch + second.scratch

    def _split(self, ins, outs, sems):
        a, b = self.parts
        return ((a, ins[:a.n], outs[:a.n], sems[:len(a.scratch)]),
                (b, ins[a.n:], outs[a.n:], sems[len(a.scratch):]))

    def start(self, ins, outs, sems):
        for ex, i, o, s in self._split(ins, outs, sems):
            ex.start(i, o, s)

    def finish(self, ins, outs, sems):
        for ex, i, o, s in self._split(ins, outs, sems):
            ex.finish(i, o, s)


def _call(body, *, name, grid, in_specs, out_specs, out_shape, args, scratch_shapes=(), aliases=None, exchange=None):
    if exchange is None:
        return pl.pallas_call(
            body, name=name, grid=grid, in_specs=in_specs, out_specs=out_specs, out_shape=out_shape,
            scratch_shapes=list(scratch_shapes), input_output_aliases=aliases or {},
            compiler_params=_params(("arbitrary",) * len(grid)))(*args)
    n_in, n_out, n_scr, n_ex = len(in_specs), len(out_specs), len(scratch_shapes), exchange.n
    steps = grid

    def wrapped(*refs):
        ins, refs = refs[:n_in], refs[n_in:]
        ex_in, refs = refs[:n_ex], refs[n_ex:]
        outs, refs = refs[:n_out], refs[n_out:]
        ex_out, refs = refs[:n_ex], refs[n_ex:]
        scr, sems = refs[:n_scr], refs[n_scr:]
        first = functools.reduce(jnp.logical_and, [pl.program_id(k) == 0 for k in range(len(steps))])
        last = functools.reduce(jnp.logical_and, [pl.program_id(k) == steps[k] - 1 for k in range(len(steps))])

        @pl.when(first)
        def _():
            exchange.start(ex_in, ex_out, sems)

        body(*ins, *outs, *scr)

        @pl.when(last)
        def _():
            exchange.finish(ex_in, ex_out, sems)

    return pl.pallas_call(
        wrapped, name=name, grid=grid,
        in_specs=list(in_specs) + [ANY] * n_ex, out_specs=list(out_specs) + [ANY] * n_ex,
        out_shape=list(out_shape) + exchange.out_shape,
        scratch_shapes=list(scratch_shapes) + exchange.scratch, input_output_aliases=aliases or {},
        compiler_params=_params(("arbitrary",) * len(grid)))(*args, *exchange.arrays)


def exchange_alone(exchange, name):
    def body(*refs):
        n = exchange.n
        exchange.start(refs[:n], refs[n:2 * n], refs[2 * n:])
        exchange.finish(refs[:n], refs[n:2 * n], refs[2 * n:])

    return pl.pallas_call(
        body, name=name, in_specs=[ANY] * exchange.n, out_specs=[ANY] * exchange.n,
        out_shape=exchange.out_shape, scratch_shapes=exchange.scratch)(*exchange.arrays)


def norm_matmul(x, g, wg, name, exchange=None):
    t, d = x.shape
    nl = wg.shape[2]

    def body(x_ref, g_ref, w_ref, o_ref, h_ref):
        h = _rms(x_ref[...], g_ref[...]).astype(BF16)
        h_ref[...] = h
        for c in range(N_CHIPS):
            o_ref[:, c * nl:(c + 1) * nl] = _dot(h, w_ref[c])

    return _call(
        body, name=name, grid=(t // ROW_TILE,),
        in_specs=[pl.BlockSpec((ROW_TILE, d), lambda i: (i, 0)),
                  pl.BlockSpec((1, d), lambda i: (0, 0)),
                  pl.BlockSpec((N_CHIPS, d, nl), lambda i: (0, 0, 0))],
        out_specs=[pl.BlockSpec((ROW_TILE, N_CHIPS * nl), lambda i: (i, 0)),
                   pl.BlockSpec((ROW_TILE, d), lambda i: (i, 0))],
        out_shape=[jax.ShapeDtypeStruct((t, N_CHIPS * nl), F32), jax.ShapeDtypeStruct((t, d), BF16)],
        args=(x, g, wg), exchange=exchange)


def norm_matmul_bwd(dproj, wg, x, g, dres, name, exchange=None):
    t, d = x.shape
    nl = wg.shape[2]
    stacked = dproj.ndim == 3
    piece = math.gcd(nl, dproj.shape[-1])

    def body(dp_ref, w_ref, x_ref, g_ref, dres_ref, dx_ref, dg_ref):
        dh = None
        for j in range(N_CHIPS * nl // piece):
            c, off = divmod(j * piece, nl)
            if stacked:
                p, lo = divmod(j * piece, dproj.shape[-1])
                lhs = dp_ref[p, :, lo:lo + piece]
            else:
                lhs = dp_ref[:, j * piece:(j + 1) * piece]
            part = _dot_nt(lhs.astype(BF16), w_ref[c, :, off:off + piece])
            dh = part if dh is None else dh + part
        dx, dg = _rms_bwd(x_ref[...], g_ref[...], dh)
        dx_ref[...] = dres_ref[...] + dx
        _accumulate(dg_ref, dg, pl.program_id(0) == 0)

    row = pl.BlockSpec((ROW_TILE, d), lambda i: (i, 0))
    vec = pl.BlockSpec((1, d), lambda i: (0, 0))
    if stacked:
        dp_spec = pl.BlockSpec((dproj.shape[0], ROW_TILE, dproj.shape[-1]), lambda i: (0, i, 0))
    else:
        dp_spec = pl.BlockSpec((ROW_TILE, N_CHIPS * nl), lambda i: (i, 0))
    return _call(
        body, name=name, grid=(t // ROW_TILE,),
        in_specs=[dp_spec, pl.BlockSpec((N_CHIPS, d, nl), lambda i: (0, 0, 0)), row, vec, row],
        out_specs=[row, vec],
        out_shape=[jax.ShapeDtypeStruct((t, d), F32), jax.ShapeDtypeStruct((1, d), F32)],
        args=(dproj, wg, x, g, dres), exchange=exchange)


def out_proj(a, wg, x, g, name):
    t, d = x.shape
    kl = wg.shape[1]

    def body(a_ref, w_ref, x_ref, g_ref, mix_ref, xo_ref):
        acc = _dot(a_ref[:, 0:kl], w_ref[0])
        for c in range(1, N_CHIPS):
            acc += _dot(a_ref[:, c * kl:(c + 1) * kl], w_ref[c])
        mix_ref[...] = acc
        xo_ref[...] = x_ref[...] + _rms(acc, g_ref[...])

    row = pl.BlockSpec((ROW_TILE, d), lambda i: (i, 0))
    return pl.pallas_call(
        body, name=name, grid=(t // ROW_TILE,),
        in_specs=[row, pl.BlockSpec((N_CHIPS, kl, d), lambda i: (0, 0, 0)), row,
                  pl.BlockSpec((1, d), lambda i: (0, 0))],
        out_specs=[row, row],
        out_shape=[jax.ShapeDtypeStruct((t, d), F32), jax.ShapeDtypeStruct((t, d), F32)],
        compiler_params=_params(("arbitrary",)),
    )(a, wg, x, g)


def out_proj_bwd(dxo, mix, g, wg, name):
    t, d = mix.shape
    kl = wg.shape[1]

    def body(dxo_ref, mix_ref, g_ref, w_ref, dmix_ref, da_ref, dg_ref):
        dmix, dg = _rms_bwd(mix_ref[...], g_ref[...], dxo_ref[...])
        dmb = dmix.astype(BF16)
        dmix_ref[...] = dmb
        for c in range(N_CHIPS):
            da_ref[:, c * kl:(c + 1) * kl] = _dot_nt(dmb, w_ref[c])
        _accumulate(dg_ref, dg, pl.program_id(0) == 0)

    row = pl.BlockSpec((ROW_TILE, d), lambda i: (i, 0))
    vec = pl.BlockSpec((1, d), lambda i: (0, 0))
    return pl.pallas_call(
        body, name=name, grid=(t // ROW_TILE,),
        in_specs=[row, row, vec, pl.BlockSpec((N_CHIPS, kl, d), lambda i: (0, 0, 0))],
        out_specs=[row, row, vec],
        out_shape=[jax.ShapeDtypeStruct((t, d), BF16), jax.ShapeDtypeStruct((t, d), F32),
                   jax.ShapeDtypeStruct((1, d), F32)],
        compiler_params=_params(("arbitrary",)),
    )(dxo, mix, g, wg)


def ffn_fwd(x, gpre, w1g, w2g, gpost, name, exchange=None, target=None):
    t, d = x.shape
    hc = w1g.shape[2]
    with_loss = target is not None

    def body(x_ref, gpre_ref, w1_ref, w2_ref, gpost_ref, *rest):
        if with_loss:
            t_ref, xo_ref, h_ref, a_ref, y_ref, l_ref, acc = rest
        else:
            xo_ref, h_ref, a_ref, y_ref, acc = rest
        i, c = pl.program_id(0), pl.program_id(1)

        @pl.when(c == 0)
        def _():
            h_ref[...] = _rms(x_ref[...], gpre_ref[...]).astype(BF16)

        a = _dot(h_ref[...], w1_ref[...])
        a_ref[...] = a.astype(BF16)
        r = jnp.square(jnp.maximum(a, 0.0)).astype(BF16)
        _accumulate(acc, _dot(r, w2_ref[...]), c == 0)

        @pl.when(c == N_CHIPS - 1)
        def _():
            y = acc[...]
            y_ref[...] = y
            xo = x_ref[...] + _rms(y, gpost_ref[...])
            if with_loss:
                e = xo - t_ref[...]
                xo_ref[...] = e * (1.0 / d)
                part = jnp.sum(jnp.sum(e * e, axis=-1, keepdims=True), axis=0, keepdims=True) * (0.5 / d)
                _accumulate(l_ref, part, i == 0)
            else:
                xo_ref[...] = xo

    row = pl.BlockSpec((FFN_ROWS, d), lambda i, c: (i, 0))
    vec = pl.BlockSpec((1, d), lambda i, c: (0, 0))
    one = pl.BlockSpec((1, 1), lambda i, c: (0, 0))
    return _call(
        body, name=name, grid=(t // FFN_ROWS, N_CHIPS),
        in_specs=[row, vec,
                  pl.BlockSpec((None, d, hc), lambda i, c: (c, 0, 0)),
                  pl.BlockSpec((None, hc, d), lambda i, c: (c, 0, 0)), vec] + ([row] if with_loss else []),
        out_specs=[row, row, pl.BlockSpec((FFN_ROWS, hc), lambda i, c: (i, c)), row] + ([one] if with_loss else []),
        out_shape=[jax.ShapeDtypeStruct((t, d), F32), jax.ShapeDtypeStruct((t, d), BF16),
                   jax.ShapeDtypeStruct((t, N_CHIPS * hc), BF16), jax.ShapeDtypeStruct((t, d), F32)]
        + ([jax.ShapeDtypeStruct((1, 1), F32)] if with_loss else []),
        scratch_shapes=[pltpu.VMEM((FFN_ROWS, d), F32)],
        args=(x, gpre, w1g, w2g, gpost) + ((target,) if with_loss else ()), exchange=exchange)


def ffn_bwd(dxo, x, y, a, gpre, gpost, w1g, w2g, name, exchange=None):
    t, d = x.shape
    hc = w1g.shape[2]

    def body(dxo_ref, x_ref, y_ref, a_ref, gpre_ref, gpost_ref, w1_ref, w2_ref,
             dxi_ref, dy_ref, da_ref, dgpre_ref, dgpost_ref, acc):
        i, c = pl.program_id(0), pl.program_id(1)

        @pl.when(c == 0)
        def _():
            dy, dg = _rms_bwd(y_ref[...], gpost_ref[...], dxo_ref[...])
            dy_ref[...] = dy.astype(BF16)
            _accumulate(dgpost_ref, dg, i == 0)

        dr = _dot_nt(dy_ref[...], w2_ref[...])
        da = (dr * (2.0 * jnp.maximum(a_ref[...].astype(F32), 0.0))).astype(BF16)
        da_ref[...] = da
        _accumulate(acc, _dot_nt(da, w1_ref[...]), c == 0)

        @pl.when(c == N_CHIPS - 1)
        def _():
            dx, dg = _rms_bwd(x_ref[...], gpre_ref[...], acc[...])
            dxi_ref[...] = dxo_ref[...] + dx
            _accumulate(dgpre_ref, dg, i == 0)

    row = pl.BlockSpec((ROW_TILE, d), lambda i, c: (i, 0))
    vec = pl.BlockSpec((1, d), lambda i, c: (0, 0))
    hid = pl.BlockSpec((ROW_TILE, hc), lambda i, c: (i, c))
    return _call(
        body, name=name, grid=(t // ROW_TILE, N_CHIPS),
        in_specs=[row, row, row, hid, vec, vec,
                  pl.BlockSpec((None, d, hc), lambda i, c: (c, 0, 0)),
                  pl.BlockSpec((None, hc, d), lambda i, c: (c, 0, 0))],
        out_specs=[row, row, hid, vec, vec],
        out_shape=[jax.ShapeDtypeStruct((t, d), F32), jax.ShapeDtypeStruct((t, d), BF16),
                   jax.ShapeDtypeStruct((t, N_CHIPS * hc), BF16),
                   jax.ShapeDtypeStruct((1, d), F32), jax.ShapeDtypeStruct((1, d), F32)],
        scratch_shapes=[pltpu.VMEM((ROW_TILE, d), F32)],
        args=(dxo, x, y, a, gpre, gpost, w1g, w2g), exchange=exchange)


def weight_grad(a, b, chunked, bk, bn, relu2, name, exchange=None):
    t = a.shape[0]
    a_on = chunked == "a"
    rows = min(t, WGRAD_ROWS)
    n_steps = t // rows

    def body(a_ref, b_ref, o_ref, acc):
        s = pl.program_id(1)
        av = a_ref[...]
        if relu2:
            av = jnp.square(jnp.maximum(av.astype(F32), 0.0))
        _accumulate(acc, _dot_tn(av.astype(BF16), b_ref[...].astype(BF16)), s == 0)

        @pl.when(s == n_steps - 1)
        def _():
            o_ref[...] = acc[...].astype(BF16)

    res = _call(
        body, name=name, grid=(N_CHIPS, n_steps),
        in_specs=[pl.BlockSpec((rows, bk), (lambda c, s: (s, c)) if a_on else (lambda c, s: (s, 0))),
                  pl.BlockSpec((rows, bn), (lambda c, s: (s, 0)) if a_on else (lambda c, s: (s, c)))],
        out_specs=[pl.BlockSpec((None, bk, bn), lambda c, s: (c, 0, 0))],
        out_shape=[jax.ShapeDtypeStruct((N_CHIPS, bk, bn), BF16)],
        scratch_shapes=[pltpu.VMEM((bk, bn), F32)],
        args=(a, b), exchange=exchange)
    return res[0] if exchange is None else res


def weight_grad_stacked(a, b3, bn, name):
    t, bk = a.shape
    width = b3.shape[-1]
    piece = math.gcd(bn, width)
    rows = min(t, WGRAD_ROWS)
    n_steps = t // rows

    def body(a_ref, b_ref, o_hbm, acc, staged, sem):
        s, c = pl.program_id(0), pl.program_id(1)
        av = a_ref[...].astype(BF16)
        for chunk in range(N_CHIPS):
            @pl.when(c == chunk)
            def _(chunk=chunk):
                cols = [divmod(chunk * bn + k * piece, width) for k in range(bn // piece)]
                b = jnp.concatenate([b_ref[p, :, lo:lo + piece] for p, lo in cols], axis=1).astype(BF16)
                _accumulate(acc.at[chunk], _dot_tn(av, b), s == 0)

                @pl.when(s == n_steps - 1)
                def _():
                    staged[...] = acc[chunk].astype(BF16)
                    copy = pltpu.make_async_copy(staged, o_hbm.at[chunk], sem)
                    copy.start()
                    copy.wait()

    return pl.pallas_call(
        body, name=name, grid=(n_steps, N_CHIPS),
        in_specs=[pl.BlockSpec((rows, bk), lambda s, c: (s, 0)),
                  pl.BlockSpec((b3.shape[0], rows, width), lambda s, c: (0, s, 0))],
        out_specs=ANY,
        out_shape=jax.ShapeDtypeStruct((N_CHIPS, bk, bn), BF16),
        scratch_shapes=[pltpu.VMEM((N_CHIPS, bk, bn), F32), pltpu.VMEM((bk, bn), BF16), pltpu.SemaphoreType.DMA],
        compiler_params=_params(("arbitrary", "arbitrary")),
    )(a, b3)


def _hgrn2_chunk(st, qs, fls, ivs, gls, l0, l1, l2, ng):
    nsub = len(qs)
    mx = jnp.maximum(jnp.maximum(l0, l1), l2)
    e0, e1, e2 = jnp.exp(l0 - mx), jnp.exp(l1 - mx), jnp.exp(l2 - mx)
    lb = e0 / (e0 + e1 + e2)
    rows = lax.broadcasted_iota(jnp.int32, (A_SUB, A_SUB), 0)
    cols = lax.broadcasted_iota(jnp.int32, (A_SUB, A_SUB), 1)
    tri = (rows >= cols).astype(F32)
    keep = (lax.broadcasted_iota(jnp.int32, (A_SUB, A_SUB, A_DK), 0)
            >= lax.broadcasted_iota(jnp.int32, (A_SUB, A_SUB, A_DK), 1))
    base = jnp.zeros_like(l0)
    bases, gs, ks, qfs = [], [], [], []
    for i in range(nsub):
        f = lb + (1.0 - lb) * jax.nn.sigmoid(fls[i])
        logf = jnp.log(f)
        bases.append(base)
        gs.append(base + jnp.dot(tri, logf, precision=lax.Precision.HIGHEST, preferred_element_type=F32))
        base = base + jnp.sum(logf, axis=0, keepdims=True)
        ks.append(1.0 - f)
        qfs.append(jax.nn.silu(qs[i]))
    g_last = base
    stb = st.astype(BF16)
    outs = []
    for i in range(nsub):
        o = _dot_nt((qfs[i] * jnp.exp(gs[i])).astype(BF16), stb)
        if i > 0:
            qt = (qfs[i] * jnp.exp(gs[i] - bases[i])).astype(BF16)
            kk = jnp.concatenate([ks[j] * jnp.exp(bases[i] - gs[j]) for j in range(i)], axis=0).astype(BF16)
            vv = jnp.concatenate(ivs[:i], axis=0).astype(BF16)
            o = o + _dot(_dot_nt(qt, kk).astype(BF16), vv)
        dec = jnp.exp(jnp.where(keep, gs[i][:, None, :] - gs[i][None, :, :], NEG_BIG))
        s_diag = jnp.sum(qfs[i][:, None, :] * ks[i][None, :, :] * dec, axis=-1)
        o = o + _dot(s_diag.astype(BF16), ivs[i].astype(BF16))
        o = o * lax.rsqrt(jnp.mean(o * o, axis=-1, keepdims=True) + EPS) * ng
        outs.append(o * jax.nn.silu(gls[i]))
    kdec = jnp.concatenate([ks[j] * jnp.exp(g_last - gs[j]) for j in range(nsub)], axis=0).astype(BF16)
    vall = jnp.concatenate(ivs, axis=0).astype(BF16)
    new_st = st * jnp.exp(g_last) + _dot_tn(vall, kdec)
    return new_st, outs


A_MAX_LOG_DECAY = 60.0


def _half_sums(logf):
    n = logf.shape[0]
    first = lax.broadcasted_iota(jnp.int32, logf.shape, 0) < n // 2
    return (jnp.sum(jnp.where(first, logf, 0.0), axis=0, keepdims=True),
            jnp.sum(jnp.where(first, 0.0, logf), axis=0, keepdims=True))


def _split3(x):
    hi = x.astype(BF16)
    r1 = x - hi.astype(F32)
    mid = r1.astype(BF16)
    return hi, mid, (r1 - mid.astype(F32)).astype(BF16)


def _tri_matmul(x, transpose):
    n = x.shape[0]
    r = lax.broadcasted_iota(jnp.int32, (n, n), 0)
    c = lax.broadcasted_iota(jnp.int32, (n, n), 1)
    tri = ((r <= c) if transpose else (r >= c)).astype(BF16)
    hi, mid, lo = _split3(x)
    return (_dot(tri, lo) + _dot(tri, mid)) + _dot(tri, hi)


@jax.custom_vjp
def _cumsum_rows(x):
    return _tri_matmul(x, False)


def _cumsum_rows_fwd(x):
    return _tri_matmul(x, False), None


def _cumsum_rows_bwd(_, dy):
    return (_tri_matmul(dy, True),)


_cumsum_rows.defvjp(_cumsum_rows_fwd, _cumsum_rows_bwd)


def _lower_bound(l0, l1, l2):
    mx = jnp.maximum(jnp.maximum(l0, l1), l2)
    e0, e1, e2 = jnp.exp(l0 - mx), jnp.exp(l1 - mx), jnp.exp(l2 - mx)
    return e0 / (e0 + e1 + e2)


def _b(x):
    return x.astype(BF16)


@jax.custom_vjp
def _mm(a, b):
    return _dot(_b(a), _b(b))


_mm.defvjp(lambda a, b: (_mm(a, b), (a, b)),
           lambda res, d: (_dot_nt(_b(d), _b(res[1])), _dot_tn(_b(res[0]), _b(d))))


@jax.custom_vjp
def _mm_nt(a, b):
    return _dot_nt(_b(a), _b(b))


_mm_nt.defvjp(lambda a, b: (_mm_nt(a, b), (a, b)),
              lambda res, d: (_dot(_b(d), _b(res[1])), _dot_tn(_b(d), _b(res[0]))))


def _dot_split(dot, a, b):
    ah, bh = _b(a), _b(b)
    al, bl = _b(a - ah.astype(F32)), _b(b - bh.astype(F32))
    return (dot(ah, bl) + dot(al, bh)) + dot(ah, bh)


@jax.custom_vjp
def _mm_scores(a, b):
    return _dot_nt(_b(a), _b(b))


_mm_scores.defvjp(lambda a, b: (_mm_scores(a, b), (a, b)),
                  lambda res, d: (_dot_split(_dot, d, res[1]), _dot_split(_dot_tn, d, res[0])))


@jax.custom_vjp
def _mm_tn(a, b):
    return _dot_tn(_b(a), _b(b))


_mm_tn.defvjp(lambda a, b: (_mm_tn(a, b), (a, b)),
              lambda res, d: (_dot_nt(_b(res[1]), _b(d)), _dot(_b(res[0]), _b(d))))


@jax.custom_vjp
def _split_heads(x):
    return tuple(x[:, h * A_DK:(h + 1) * A_DK] for h in range(A_HEADS))


def _split_heads_fwd(x):
    return _split_heads(x), None


def _split_heads_bwd(_, parts):
    return (jnp.concatenate(parts, axis=1),)


_split_heads.defvjp(_split_heads_fwd, _split_heads_bwd)


def _hgrn2_chunk_fast(sts, q, fl, iv, gl, l0, l1, l2, ng):
    lb = _lower_bound(l0, l1, l2)
    f = lb + (1.0 - lb) * jax.nn.sigmoid(fl)
    return _hgrn2_fast_core(sts, q, f, jnp.log(f), iv, gl, ng)


def _hgrn2_fast_core(sts, q, f, logf, iv, gl, ng):
    g = _cumsum_rows(logf)
    g_mid, g_last = _half_sums(logf)
    g_last = g_mid + g_last
    k = 1.0 - f
    qf = jax.nn.silu(q)
    qms = _split_heads(qf * jnp.exp(g - g_mid))
    kms = _split_heads(k * jnp.exp(g_mid - g))
    qgs = _split_heads(qf * jnp.exp(g))
    kds = _split_heads(k * jnp.exp(g_last - g))
    ivs = _split_heads(iv)
    decays = _split_heads(jnp.exp(g_last))
    n = q.shape[0]
    causal = lax.broadcasted_iota(jnp.int32, (n, n), 0) >= lax.broadcasted_iota(jnp.int32, (n, n), 1)
    raw = [_mm_scores(qm, km) for qm, km in zip(qms, kms)]
    inter = [_mm_nt(qg, st) for qg, st in zip(qgs, sts)]
    scores = [jnp.where(causal, s, 0.0) for s in raw]
    os = [a + _mm(s, v) for a, s, v in zip(inter, scores, ivs)]
    new_sts = [st * d + _mm_tn(v, kd) for st, d, v, kd in zip(sts, decays, ivs, kds)]
    os = [o * lax.rsqrt(jnp.mean(o * o, axis=-1, keepdims=True) + EPS) for o in os]
    return new_sts, jnp.concatenate(os, axis=1) * ng * jax.nn.silu(gl)


A_STEP_CHUNKS = 4


def _chunk_rows(j):
    return pl.ds(pl.multiple_of(j * A_CHUNK, A_CHUNK), A_CHUNK)


def _sub_rows(j, i):
    return pl.ds(pl.multiple_of(j * A_CHUNK + i * A_SUB, A_SUB), A_SUB)


def _sub_blocks(ref, head, j):
    lanes = slice(head * A_DK, (head + 1) * A_DK)
    return [ref[_sub_rows(j, i), lanes] for i in range(A_CHUNK // A_SUB)]


def hgrn2_fwd(proj, lb_table, a_norm, batch, name, exchange=None):
    t = proj.shape[0]
    n_steps = t // batch // (A_CHUNK * A_STEP_CHUNKS)
    rows = A_CHUNK * A_STEP_CHUNKS

    def body(q_ref, f_ref, i_ref, g_ref, lb_ref, ng_ref, o_ref, st_ref, dec_ref, st):
        @pl.when(pl.program_id(1) == 0)
        def _():
            st[...] = jnp.zeros_like(st)

        def chunk(j, carry):
            r = _chunk_rows(j)
            st_ref[j] = st[...]
            lb = _lower_bound(lb_ref[0:1, :], lb_ref[1:2, :], lb_ref[2:3, :])
            f = lb + (1.0 - lb) * jax.nn.sigmoid(f_ref[r, :])
            logf = jnp.log(f)
            decay = jnp.minimum(*_half_sums(logf))
            dec_ref[j] = decay
            mild = jnp.min(decay) >= -A_MAX_LOG_DECAY

            @pl.when(mild)
            def _():
                new_sts, o = _hgrn2_fast_core([st[h] for h in range(A_HEADS)], q_ref[r, :], f, logf,
                                              i_ref[r, :], g_ref[r, :], ng_ref[...])
                for h in range(A_HEADS):
                    st[h] = new_sts[h]
                o_ref[r, :] = o.astype(BF16)

            @pl.when(jnp.logical_not(mild))
            def _():
                for h in range(A_HEADS):
                    lanes = slice(h * A_DK, (h + 1) * A_DK)
                    new_st, outs = _hgrn2_chunk(
                        st[h], _sub_blocks(q_ref, h, j), _sub_blocks(f_ref, h, j), _sub_blocks(i_ref, h, j),
                        _sub_blocks(g_ref, h, j), lb_ref[0:1, lanes], lb_ref[1:2, lanes], lb_ref[2:3, lanes],
                        ng_ref[:, lanes])
                    st[h] = new_st
                    for i, o in enumerate(outs):
                        o_ref[_sub_rows(j, i), lanes] = o.astype(BF16)

            return carry

        lax.fori_loop(0, A_STEP_CHUNKS, chunk, 0)

    def part(k):
        return pl.BlockSpec((rows, A_WIDTH), lambda b, n: (b * n_steps + n, k))

    return _call(
        body, name=name, grid=(batch, n_steps),
        in_specs=[part(0), part(1), part(2), part(3),
                  pl.BlockSpec((3, A_WIDTH), lambda b, n: (0, 0)), pl.BlockSpec((1, A_WIDTH), lambda b, n: (0, 0))],
        out_specs=[part(0),
                   pl.BlockSpec((A_STEP_CHUNKS, A_HEADS, A_DK, A_DK), lambda b, n: (b * n_steps + n, 0, 0, 0)),
                   pl.BlockSpec((A_STEP_CHUNKS, 1, A_WIDTH), lambda b, n: (b * n_steps + n, 0, 0))],
        out_shape=[jax.ShapeDtypeStruct((t, A_WIDTH), BF16),
                   jax.ShapeDtypeStruct((t // A_CHUNK, A_HEADS, A_DK, A_DK), F32),
                   jax.ShapeDtypeStruct((t // A_CHUNK, 1, A_WIDTH), F32)],
        scratch_shapes=[pltpu.VMEM((A_HEADS, A_DK, A_DK), F32)],
        args=(proj, proj, proj, proj, lb_table, a_norm), exchange=exchange)


def hgrn2_bwd(proj, states, decays, lb_table, a_norm, do, batch, name, exchange=None):
    t = proj.shape[0]
    n_steps = t // batch // (A_CHUNK * A_STEP_CHUNKS)
    rows = A_CHUNK * A_STEP_CHUNKS

    def body(q_ref, f_ref, i_ref, g_ref, st_ref, dec_ref, lb_ref, ng_ref, do_ref, dp_ref, dlb_ref, dng_ref, dst):
        @pl.when(jnp.logical_and(pl.program_id(0) == 0, pl.program_id(1) == 0))
        def _():
            dlb_ref[...] = jnp.zeros_like(dlb_ref)
            dng_ref[...] = jnp.zeros_like(dng_ref)

        @pl.when(pl.program_id(1) == 0)
        def _():
            dst[...] = jnp.zeros_like(dst)

        def chunk(jj, carry):
            j = A_STEP_CHUNKS - 1 - jj
            r = _chunk_rows(j)
            mild = jnp.min(dec_ref[j]) >= -A_MAX_LOG_DECAY

            @pl.when(mild)
            def _():
                _, vjp = jax.vjp(
                    _hgrn2_chunk_fast, [st_ref[j, h] for h in range(A_HEADS)], q_ref[r, :], f_ref[r, :],
                    i_ref[r, :], g_ref[r, :], lb_ref[0:1, :], lb_ref[1:2, :], lb_ref[2:3, :], ng_ref[...])
                d_sts, dq, df, di, dg, dl0, dl1, dl2, dng = vjp(
                    ([dst[h] for h in range(A_HEADS)], do_ref[r, :].astype(F32)))
                for h in range(A_HEADS):
                    dst[h] = d_sts[h]
                for k, part in enumerate((dq, df, di, dg)):
                    dp_ref[r, k * A_WIDTH:(k + 1) * A_WIDTH] = part
                for row, val in enumerate((dl0, dl1, dl2)):
                    dlb_ref[row:row + 1, :] += val
                dng_ref[...] += dng

            @pl.when(jnp.logical_not(mild))
            def _():
                for h in range(A_HEADS):
                    lanes = slice(h * A_DK, (h + 1) * A_DK)
                    _, vjp = jax.vjp(
                        _hgrn2_chunk, st_ref[j, h], _sub_blocks(q_ref, h, j), _sub_blocks(f_ref, h, j),
                        _sub_blocks(i_ref, h, j), _sub_blocks(g_ref, h, j), lb_ref[0:1, lanes], lb_ref[1:2, lanes],
                        lb_ref[2:3, lanes], ng_ref[:, lanes])
                    douts = [x.astype(F32) for x in _sub_blocks(do_ref, h, j)]
                    d_st, dqs, dfs, dis, dgs, dl0, dl1, dl2, dng = vjp((dst[h], douts))
                    dst[h] = d_st
                    for k, parts in enumerate((dqs, dfs, dis, dgs)):
                        for i in range(A_CHUNK // A_SUB):
                            dp_ref[_sub_rows(j, i), k * A_WIDTH + h * A_DK:k * A_WIDTH + (h + 1) * A_DK] = parts[i]
                    for row, val in enumerate((dl0, dl1, dl2)):
                        dlb_ref[row:row + 1, lanes] += val
                    dng_ref[:, lanes] += dng

            return carry

        lax.fori_loop(0, A_STEP_CHUNKS, chunk, 0)

    def rev(b, n):
        return b * n_steps + (n_steps - 1 - n)

    def part(k):
        return pl.BlockSpec((rows, A_WIDTH), lambda b, n: (rev(b, n), k))

    const3 = pl.BlockSpec((3, A_WIDTH), lambda b, n: (0, 0))
    const1 = pl.BlockSpec((1, A_WIDTH), lambda b, n: (0, 0))
    return _call(
        body, name=name, grid=(batch, n_steps),
        in_specs=[part(0), part(1), part(2), part(3),
                  pl.BlockSpec((A_STEP_CHUNKS, A_HEADS, A_DK, A_DK), lambda b, n: (rev(b, n), 0, 0, 0)),
                  pl.BlockSpec((A_STEP_CHUNKS, 1, A_WIDTH), lambda b, n: (rev(b, n), 0, 0)),
                  const3, const1, part(0)],
        out_specs=[pl.BlockSpec((rows, 4 * A_WIDTH), lambda b, n: (rev(b, n), 0)), const3, const1],
        out_shape=[jax.ShapeDtypeStruct((t, 4 * A_WIDTH + 2 * B_WIDTH), F32),
                   jax.ShapeDtypeStruct((3, A_WIDTH), F32), jax.ShapeDtypeStruct((1, A_WIDTH), F32)],
        scratch_shapes=[pltpu.VMEM((A_HEADS, A_DK, A_DK), F32)],
        args=(proj, proj, proj, proj, states, decays, lb_table, a_norm, do), exchange=exchange)


B_GDIM = B_WIDTH // B_GROUPS
B_ROWS = 512


def _gmlp_chunk(ubs, vbs, lngs, lnbs, ws, bcols):
    vs = [jax.nn.gelu(v) for v in vbs]
    mu = sum(jnp.sum(v, axis=-1, keepdims=True) for v in vs) * (1.0 / B_WIDTH)
    var = sum(jnp.sum(jnp.square(v - mu), axis=-1, keepdims=True) for v in vs) * (1.0 / B_WIDTH)
    rstd = lax.rsqrt(var + EPS)
    tril = (lax.broadcasted_iota(jnp.int32, (B_CHUNK, B_CHUNK), 0)
            >= lax.broadcasted_iota(jnp.int32, (B_CHUNK, B_CHUNK), 1))
    outs = []
    for g in range(B_GROUPS):
        vn = (vs[g] - mu) * rstd * lngs[g] + lnbs[g]
        w = jnp.where(tril, ws[g], 0.0).astype(BF16)
        outs.append(jax.nn.gelu(ubs[g]) * (_dot(w, vn.astype(BF16)) + bcols[g]))
    return outs


def _gmlp_args(u_ref, v_ref, lng_ref, lnb_ref, w_ref, bt_ref, rows):
    def groups(ref):
        return [ref[rows, g * B_GDIM:(g + 1) * B_GDIM] for g in range(B_GROUPS)]

    def vec(ref):
        return [ref[:, g * B_GDIM:(g + 1) * B_GDIM] for g in range(B_GROUPS)]

    return (groups(u_ref), groups(v_ref), vec(lng_ref), vec(lnb_ref),
            [w_ref[g] for g in range(B_GROUPS)], [bt_ref[:, g:g + 1] for g in range(B_GROUPS)])


def gmlp_fwd(proj, oa, ln_g, ln_b, w, bias_t, name, exchange=None):
    t = proj.shape[0]

    def body(u_ref, v_ref, oa_ref, lng_ref, lnb_ref, w_ref, bt_ref, o_ref):
        o_ref[:, 0:A_WIDTH] = oa_ref[...]
        for n in range(B_ROWS // B_CHUNK):
            rows = slice(n * B_CHUNK, (n + 1) * B_CHUNK)
            outs = _gmlp_chunk(*_gmlp_args(u_ref, v_ref, lng_ref, lnb_ref, w_ref, bt_ref, rows))
            for g, o in enumerate(outs):
                o_ref[rows, A_WIDTH + g * B_GDIM:A_WIDTH + (g + 1) * B_GDIM] = o.astype(BF16)

    vec = pl.BlockSpec((1, B_WIDTH), lambda i: (0, 0))
    return _call(
        body, name=name, grid=(t // B_ROWS,),
        in_specs=[pl.BlockSpec((B_ROWS, B_WIDTH), lambda i: (i, 4)), pl.BlockSpec((B_ROWS, B_WIDTH), lambda i: (i, 5)),
                  pl.BlockSpec((B_ROWS, A_WIDTH), lambda i: (i, 0)), vec, vec,
                  pl.BlockSpec((B_GROUPS, B_CHUNK, B_CHUNK), lambda i: (0, 0, 0)),
                  pl.BlockSpec((B_CHUNK, B_GROUPS), lambda i: (0, 0))],
        out_specs=[pl.BlockSpec((B_ROWS, A_WIDTH + B_WIDTH), lambda i: (i, 0))],
        out_shape=[jax.ShapeDtypeStruct((t, A_WIDTH + B_WIDTH), BF16)],
        args=(proj, proj, oa, ln_g, ln_b, w, bias_t), exchange=exchange)


def gmlp_bwd(proj, dmixin, ln_g, ln_b, w, bias_t, dproj, name, exchange=None):
    t = proj.shape[0]

    def body(u_ref, v_ref, do_ref, lng_ref, lnb_ref, w_ref, bt_ref, dp_in_ref,
             dp_ref, dlng_ref, dlnb_ref, dw_ref, dbt_ref):
        del dp_in_ref

        @pl.when(pl.program_id(0) == 0)
        def _():
            for ref in (dlng_ref, dlnb_ref, dw_ref, dbt_ref):
                ref[...] = jnp.zeros_like(ref)

        for n in range(B_ROWS // B_CHUNK):
            rows = slice(n * B_CHUNK, (n + 1) * B_CHUNK)
            _, vjp = jax.vjp(_gmlp_chunk, *_gmlp_args(u_ref, v_ref, lng_ref, lnb_ref, w_ref, bt_ref, rows))
            douts = [do_ref[rows, g * B_GDIM:(g + 1) * B_GDIM] for g in range(B_GROUPS)]
            dus, dvs, dlngs, dlnbs, dws, dbs = vjp(douts)
            for g in range(B_GROUPS):
                lanes = slice(g * B_GDIM, (g + 1) * B_GDIM)
                dp_ref[rows, lanes] = dus[g]
                dp_ref[rows, B_WIDTH + g * B_GDIM:B_WIDTH + (g + 1) * B_GDIM] = dvs[g]
                dlng_ref[:, lanes] += dlngs[g]
                dlnb_ref[:, lanes] += dlnbs[g]
                dw_ref[g] += dws[g]
                dbt_ref[:, g:g + 1] += dbs[g]

    vec = pl.BlockSpec((1, B_WIDTH), lambda i: (0, 0))
    wspec = pl.BlockSpec((B_GROUPS, B_CHUNK, B_CHUNK), lambda i: (0, 0, 0))
    bspec = pl.BlockSpec((B_CHUNK, B_GROUPS), lambda i: (0, 0))
    return _call(
        body, name=name, grid=(t // B_ROWS,),
        in_specs=[pl.BlockSpec((B_ROWS, B_WIDTH), lambda i: (i, 4)), pl.BlockSpec((B_ROWS, B_WIDTH), lambda i: (i, 5)),
                  pl.BlockSpec((B_ROWS, B_WIDTH), lambda i: (i, 1)), vec, vec, wspec, bspec,
                  pl.BlockSpec(memory_space=pl.ANY)],
        out_specs=[pl.BlockSpec((B_ROWS, 2 * B_WIDTH), lambda i: (i, 2)), vec, vec, wspec, bspec],
        out_shape=[jax.ShapeDtypeStruct(dproj.shape, F32), jax.ShapeDtypeStruct((1, B_WIDTH), F32),
                   jax.ShapeDtypeStruct((1, B_WIDTH), F32), jax.ShapeDtypeStruct((B_GROUPS, B_CHUNK, B_CHUNK), F32),
                   jax.ShapeDtypeStruct((B_CHUNK, B_GROUPS), F32)],
        aliases={7: 0}, args=(proj, proj, dmixin, ln_g, ln_b, w, bias_t, dproj), exchange=exchange)


C_FWD_BLOCKS = 8
C_BWD_BLOCKS = 8
C_PAIR = 2 * C_HEAD_DIM
C_PAIRS = C_HEADS // 2
C_SCALE = 1.0 / math.sqrt(C_HEAD_DIM)
C_ROT_DIM = 2 * C_ROT_HALF
ROPE_ROWS = 1024


def rope_tables(pos_col, name):
    t = pos_col.shape[0]

    def body(p_ref, c_ref, a_ref, b_ref):
        lane = jnp.bitwise_and(lax.broadcasted_iota(jnp.int32, (1, C_PAIR), 1), C_HEAD_DIM - 1)
        j = jnp.bitwise_and(lane, C_ROT_HALF - 1).astype(F32)
        inv = jnp.exp(j * (-math.log(ROPE_THETA) / C_ROT_HALF))
        ang = p_ref[...].astype(F32) * inv
        cos, sin = jnp.cos(ang), jnp.sin(ang)
        c_ref[...] = jnp.where(lane < C_ROT_DIM, cos, 1.0)
        a_ref[...] = jnp.where(lane < C_ROT_HALF, -sin, 0.0)
        b_ref[...] = jnp.where(jnp.logical_and(lane >= C_ROT_HALF, lane < C_ROT_DIM), sin, 0.0)

    tab = pl.BlockSpec((ROPE_ROWS, C_PAIR), lambda i: (i, 0))
    return pl.pallas_call(
        body, name=name, grid=(t // ROPE_ROWS,),
        in_specs=[pl.BlockSpec((ROPE_ROWS, 1), lambda i: (i, 0))],
        out_specs=[tab, tab, tab],
        out_shape=[jax.ShapeDtypeStruct((t, C_PAIR), F32)] * 3,
        compiler_params=_params(("arbitrary",)),
    )(pos_col)


def _rope(x, c, a, b):
    return x * c + pltpu.roll(x, C_PAIR - C_ROT_HALF, 1) * a + pltpu.roll(x, C_ROT_HALF, 1) * b


def _rope_t(d, c, a, b):
    return d * c + pltpu.roll(d * a, C_ROT_HALF, 1) + pltpu.roll(d * b, C_PAIR - C_ROT_HALF, 1)


C_RES = 16


def _residue_major(a, batch):
    return a.reshape(batch, SEQ // C_RES, C_RES, -1).transpose(0, 2, 1, 3).reshape(a.shape)


def _sequence_order(a, batch):
    return a.reshape(batch, C_RES, SEQ // C_RES, -1).transpose(0, 2, 1, 3).reshape(a.shape)


def _block_pieces(idx, dil):
    nblk = SEQ // dil // C_BLOCK
    r, n = idx // nblk, idx % nblk
    per = C_RES // dil
    size = C_BLOCK // per

    def pieces(blk):
        return [((dil * a + r) * (SEQ // C_RES) + size * blk, size) for a in range(per)]

    return pieces(n), pieces(jnp.maximum(n - 1, 0)), n > 0


def _get_rows(ref, pieces):
    return jnp.concatenate([ref[pl.ds(pl.multiple_of(start, 8), size), :] for start, size in pieces], axis=0)


def _set_rows(ref, pieces, val, add=False):
    for k, (start, size) in enumerate(pieces):
        rows = pl.ds(pl.multiple_of(start, 8), size)
        part = val[k * size:(k + 1) * size]
        ref[rows, :] = ref[rows, :] + part if add else part


def _head_masks():
    low = lax.broadcasted_iota(jnp.int32, (1, C_PAIR), 1) < C_HEAD_DIM
    return low, jnp.logical_not(low)


def _attn_mask(has_prev, dil):
    per = C_RES // dil
    size = C_BLOCK // per

    def position(x):
        x = jnp.bitwise_and(x, C_BLOCK - 1)
        return per * jnp.bitwise_and(x, size - 1) + x // size

    j = lax.broadcasted_iota(jnp.int32, (2 * C_BLOCK, 2 * C_BLOCK), 1)
    pi = position(lax.broadcasted_iota(jnp.int32, (2 * C_BLOCK, 2 * C_BLOCK), 0))
    pj = position(j)
    own = j < C_BLOCK
    return jnp.logical_or(jnp.logical_and(own, pj <= pi),
                          jnp.logical_and(jnp.logical_and(jnp.logical_not(own), pj >= pi), has_prev))


def _stack_heads(x):
    low, high = _head_masks()
    return jnp.concatenate([jnp.where(low, x, 0.0), jnp.where(high, x, 0.0)], axis=0)


def _unstack_heads(x):
    low, _ = _head_masks()
    return jnp.where(low, x[:C_BLOCK], x[C_BLOCK:])


def attn_fwd(qkv, cos_t, sin_a, sin_b, batch, name, exchange=None):
    t = qkv.shape[0]
    nbr = len(C_DILATIONS)

    def body(q_ref, k_ref, v_ref, c_ref, a_ref, b_ref, o_ref, l_ref, qr_ref, kr_ref, qs, ks, *stats):
        acc, mm, dd = stats[0:nbr], stats[nbr:2 * nbr], stats[2 * nbr:3 * nbr]
        c, a, b = c_ref[...], a_ref[...], b_ref[...]
        qs[...] = _rope(q_ref[...], c, a, b) * C_SCALE
        ks[...] = _rope(k_ref[...], c, a, b)
        qr_ref[...] = qs[...].astype(BF16)
        kr_ref[...] = ks[...].astype(BF16)

        def load(idx, dil):
            own, prev, has_prev = _block_pieces(idx, dil)
            return own, (has_prev, _get_rows(qs, own), _get_rows(ks, own), _get_rows(ks, prev),
                         _get_rows(v_ref, own), _get_rows(v_ref, prev))

        def scores(dil, has_prev, q, k_own, k_prev, v_own, v_prev):
            k_cat = jnp.concatenate([k_own, k_prev], axis=0).astype(BF16)
            return jnp.where(_attn_mask(has_prev, dil), _dot_nt(_stack_heads(q).astype(BF16), k_cat), NEG_BIG)

        def softmax(s):
            m = jnp.max(s, axis=-1, keepdims=True)
            p = jnp.exp(s - m)
            return p.astype(BF16), m, jnp.sum(p, axis=-1, keepdims=True)

        def values(pb, has_prev, q, k_own, k_prev, v_own, v_prev):
            low, high = _head_masks()
            v_cat = jnp.concatenate([v_own, v_prev], axis=0)
            p_wide = jnp.concatenate([pb[:C_BLOCK], pb[C_BLOCK:]], axis=1)
            v_tall = jnp.concatenate([jnp.where(low, v_cat, 0.0), jnp.where(high, v_cat, 0.0)], axis=0).astype(BF16)
            return _dot(p_wide, v_tall)

        for bi, dil in enumerate(C_DILATIONS):
            def pair(i, carry, bi=bi, dil=dil):
                low, _ = _head_masks()
                loaded = [load(C_FWD_BLOCKS * i + k, dil) for k in range(C_FWD_BLOCKS)]
                ss = [scores(dil, *ops) for _, ops in loaded]
                sm = [softmax(s) for s in ss]
                pvs = [values(pb, *ops) for (pb, _, _), (_, ops) in zip(sm, loaded)]
                for (own, _), (_, m, den), pv in zip(loaded, sm, pvs):
                    _set_rows(acc[bi], own, pv)
                    _set_rows(mm[bi], own, jnp.where(low, m[:C_BLOCK], m[C_BLOCK:]))
                    _set_rows(dd[bi], own, jnp.where(low, den[:C_BLOCK], den[C_BLOCK:]))
                return carry

            lax.fori_loop(0, SEQ // C_BLOCK // C_FWD_BLOCKS, pair, 0)
        step = 2 * C_BLOCK
        for r0 in range(0, SEQ, step):
            rr = slice(r0, r0 + step)
            ms = [mm[g][rr, :] for g in range(nbr)]
            m_all = functools.reduce(jnp.maximum, ms)
            ws = [jnp.exp(m - m_all) for m in ms]
            num = sum(acc[g][rr, :] * ws[g] for g in range(nbr))
            den = sum(dd[g][rr, :] * ws[g] for g in range(nbr))
            o_ref[rr, :] = (num / den).astype(BF16)
            l_ref[rr, :] = m_all + jnp.log(den)

    def col(k):
        return pl.BlockSpec((SEQ, C_PAIR), lambda b, p: (b, k * C_PAIRS + p))

    tab = pl.BlockSpec((SEQ, C_PAIR), lambda b, p: (b, 0))
    return _call(
        body, name=name, grid=(batch, C_PAIRS),
        in_specs=[col(0), col(1), col(2), tab, tab, tab],
        out_specs=[col(0), col(0), col(0), col(0)],
        out_shape=[jax.ShapeDtypeStruct((t, D_MODEL), BF16), jax.ShapeDtypeStruct((t, D_MODEL), F32),
                   jax.ShapeDtypeStruct((t, D_MODEL), BF16), jax.ShapeDtypeStruct((t, D_MODEL), BF16)],
        scratch_shapes=[pltpu.VMEM((SEQ, C_PAIR), F32)] * (2 + 3 * nbr),
        args=(qkv, qkv, qkv, cos_t, sin_a, sin_b), exchange=exchange)


def attn_bwd(qr, kr, qkv, cos_t, sin_a, sin_b, o, lse, do, batch, name, exchange=None):
    t = qkv.shape[0]

    def body(q_ref, k_ref, v_ref, c_ref, a_ref, b_ref, o_ref, l_ref, do_ref, dqkv_ref, qs, ks, dqs, dks, dvs, dlt):
        low, _ = _head_masks()
        c, a, b = c_ref[...], a_ref[...], b_ref[...]
        qs[...] = q_ref[...].astype(F32)
        ks[...] = k_ref[...].astype(F32)
        prod = do_ref[...] * o_ref[...].astype(F32)
        s_low = jnp.sum(jnp.where(low, prod, 0.0), axis=-1, keepdims=True)
        s_all = jnp.sum(prod, axis=-1, keepdims=True)
        dlt[...] = jnp.where(low, s_low, s_all - s_low)
        dqs[...] = jnp.zeros_like(dqs)
        dks[...] = jnp.zeros_like(dks)
        dvs[...] = jnp.zeros_like(dvs)

        def load(idx, dil):
            own, prev, has_prev = _block_pieces(idx, dil)
            return (own, prev), (has_prev, _get_rows(qs, own), _get_rows(do_ref, own), _get_rows(ks, own),
                                 _get_rows(ks, prev), _get_rows(v_ref, own), _get_rows(v_ref, prev),
                                 _get_rows(l_ref, own), _get_rows(dlt, own))

        def operands(dil, has_prev, q, do, k_own, k_prev, v_own, v_prev, l_full, d_full):
            lcol = jnp.concatenate([l_full[:, 0:1], l_full[:, C_HEAD_DIM:C_HEAD_DIM + 1]], axis=0)
            dcol = jnp.concatenate([d_full[:, 0:1], d_full[:, C_HEAD_DIM:C_HEAD_DIM + 1]], axis=0)
            return (_stack_heads(q).astype(BF16), _stack_heads(do).astype(BF16),
                    jnp.concatenate([k_own, k_prev], axis=0).astype(BF16),
                    jnp.concatenate([v_own, v_prev], axis=0).astype(BF16), lcol, dcol, _attn_mask(has_prev, dil))

        for dil in C_DILATIONS:
            def pair(i, carry, dil=dil):
                loaded = [load(C_BWD_BLOCKS * i + k, dil) for k in range(C_BWD_BLOCKS)]
                ops = [operands(dil, *o) for _, o in loaded]
                ss = [_dot_nt(q_stack, k_cat) for q_stack, _, k_cat, _, _, _, _ in ops]
                dps = [_dot_nt(do_stack, v_cat) for _, do_stack, _, v_cat, _, _, _ in ops]
                ps = [jnp.exp(jnp.where(o[6], s, NEG_BIG) - o[4]) for s, o in zip(ss, ops)]
                dss = [(p * (dp - o[5])).astype(BF16) for p, dp, o in zip(ps, dps, ops)]
                dvs_ = [_dot_tn(p.astype(BF16), o[1]) for p, o in zip(ps, ops)]
                dks_ = [_dot_tn(ds, o[0]) for ds, o in zip(dss, ops)]
                dqs_ = [_unstack_heads(_dot(ds, o[2])) for ds, o in zip(dss, ops)]
                for ((own, prev), _), dq, dk_cat, dv_cat in zip(loaded, dqs_, dks_, dvs_):
                    _set_rows(dqs, own, dq, add=True)
                    _set_rows(dks, own, dk_cat[:C_BLOCK], add=True)
                    _set_rows(dvs, own, dv_cat[:C_BLOCK], add=True)
                    _set_rows(dks, prev, dk_cat[C_BLOCK:], add=True)
                    _set_rows(dvs, prev, dv_cat[C_BLOCK:], add=True)
                return carry

            lax.fori_loop(0, SEQ // C_BLOCK // C_BWD_BLOCKS, pair, 0)
        dqkv_ref[0] = _rope_t(dqs[...] * C_SCALE, c, a, b).astype(BF16)
        dqkv_ref[1] = _rope_t(dks[...], c, a, b).astype(BF16)
        dqkv_ref[2] = dvs[...].astype(BF16)

    def col(k):
        return pl.BlockSpec((SEQ, C_PAIR), lambda b, p: (b, k * C_PAIRS + p))

    tab = pl.BlockSpec((SEQ, C_PAIR), lambda b, p: (b, 0))
    return _call(
        body, name=name, grid=(batch, C_PAIRS),
        in_specs=[col(0), col(0), col(2), tab, tab, tab, col(0), col(0), col(0)],
        out_specs=[pl.BlockSpec((3, SEQ, C_PAIR), lambda b, p: (0, b, p))],
        out_shape=[jax.ShapeDtypeStruct((3, t, D_MODEL), BF16)],
        scratch_shapes=[pltpu.VMEM((SEQ, C_PAIR), F32)] * 6,
        args=(qr, kr, qkv, cos_t, sin_a, sin_b, o, lse, do), exchange=exchange)


def allreduce_small(slab, name):
    rows, lanes = slab.shape

    def body(x_ref, out_ref, gath, send_sems, recv_sems, local_sem):
        x, y, c, chips = _place()
        me, sibling = (x, y, c), (x, y, 1 - c)

        def slot(px, py, pc):
            return gath.at[4 * px + 2 * py + pc]

        def copy(k, block, to, src=None):
            return pltpu.make_async_remote_copy(
                src_ref=slot(*block) if src is None else src, dst_ref=slot(*block),
                send_sem=send_sems.at[k], recv_sem=recv_sems.at[k], device_id=to, device_id_type=MESH)

        mine = pltpu.make_async_copy(x_ref, slot(*me), local_sem)
        mine.start()
        first = [copy(0, me, sibling, src=x_ref)]
        first += [copy(1 + j, me, (*chip, c), src=x_ref) for j, chip in enumerate(chips)]
        for cp in first:
            cp.start()
        passed = [copy(4 + j, (*chip, c), sibling) for j, chip in enumerate(chips)]
        for j, chip in enumerate(chips):
            copy(1 + j, (*chip, c), me).wait_recv()
            passed[j].start()
        copy(0, sibling, me).wait_recv()
        for j, chip in enumerate(chips):
            copy(4 + j, (*chip, 1 - c), me).wait_recv()
        for cp in first + passed:
            cp.wait_send()
        mine.wait()
        total = gath[0]
        for d in range(1, N_DEV):
            total = total + gath[d]
        out_ref[...] = total

    return pl.pallas_call(
        body, name=name,
        in_specs=[pl.BlockSpec(memory_space=pltpu.VMEM)],
        out_specs=pl.BlockSpec(memory_space=pltpu.VMEM),
        out_shape=jax.ShapeDtypeStruct((rows, lanes), F32),
        scratch_shapes=[pltpu.VMEM((N_DEV, rows, lanes), F32),
                        pltpu.SemaphoreType.DMA((7,)), pltpu.SemaphoreType.DMA((7,)), pltpu.SemaphoreType.DMA],
    )(slab)


ELT_ROWS = 512


def reduce_slabs(r, name):
    r = r.reshape(N_CHIPS, -1, r.shape[-1])
    _, rows, cols = r.shape
    br = min(rows, ELT_ROWS)

    def body(r_ref, o_ref):
        o_ref[...] = ((r_ref[3].astype(F32) + r_ref[0].astype(F32)) + r_ref[1].astype(F32)) + r_ref[2].astype(F32)

    return pl.pallas_call(
        body, name=name, grid=(rows // br,),
        in_specs=[pl.BlockSpec((N_CHIPS, br, cols), lambda i: (0, i, 0))],
        out_specs=pl.BlockSpec((br, cols), lambda i: (i, 0)),
        out_shape=jax.ShapeDtypeStruct((rows, cols), F32),
        compiler_params=_params(("arbitrary",)),
    )(r)


def _adamw(w, g, m, v):
    m = ADAM_B1 * m + (1.0 - ADAM_B1) * g
    v = ADAM_B2 * v + (1.0 - ADAM_B2) * jnp.square(g)
    m_hat = m / (1.0 - ADAM_B1 ** ADAM_STEP)
    v_hat = v / (1.0 - ADAM_B2 ** ADAM_STEP)
    delta = -ADAM_LR * (m_hat / (jnp.sqrt(v_hat) + ADAM_EPS) + ADAM_WD * w)
    return delta, m, v


def adamw_big(w, s_mine, s_sibling, m, v, name):
    rows, cols = w.shape
    parts = len(s_mine)
    br = min(rows // parts, ELT_ROWS)
    nb = rows // parts // br

    def body(w_ref, m_ref, v_ref, *rest):
        sums, (g_out, d_out, m_out, v_out) = rest[:2 * parts], rest[2 * parts:]
        p = pl.program_id(0)
        g = sums[0][...] + sums[parts][...]
        for k in range(1, parts):
            g = jnp.where(p == k, sums[k][...] + sums[parts + k][...], g)
        g_out[...] = g
        d_out[...], m_out[...], v_out[...] = _adamw(w_ref[...], g, m_ref[...], v_ref[...])

    def part_spec(k):
        return pl.BlockSpec((br, cols), lambda p, i: (jnp.where(p == k, i, jnp.where(p < k, 0, nb - 1)), 0))

    blk = pl.BlockSpec((br, cols), lambda p, i: (p * nb + i, 0))
    out = jax.ShapeDtypeStruct((rows, cols), F32)
    return pl.pallas_call(
        body, name=name, grid=(parts, nb),
        in_specs=[blk] * 3 + [part_spec(k) for k in range(parts)] * 2, out_specs=[blk] * 4, out_shape=[out] * 4,
        compiler_params=_params(("arbitrary", "arbitrary")),
    )(w, m, v, *s_mine, *s_sibling)


def adamw_small(ws, gs, ms, vs, name):
    n = len(ws)

    def body(*refs):
        w_refs, g_refs, m_refs, v_refs = (refs[k * n:(k + 1) * n] for k in range(4))
        d_out, m_out, v_out = (refs[(4 + k) * n:(5 + k) * n] for k in range(3))
        for i in range(n):
            d_out[i][...], m_out[i][...], v_out[i][...] = _adamw(
                w_refs[i][...], g_refs[i][...], m_refs[i][...], v_refs[i][...])

    outs = [jax.ShapeDtypeStruct(w.shape, F32) for w in ws]
    res = pl.pallas_call(body, name=name, out_shape=outs * 3)(*ws, *gs, *ms, *vs)
    return res[:n], res[n:2 * n], res[2 * n:]


SLAB_LANES = 128
SLAB_ROW_ALIGN = 8


def _pack(parts):
    flat = jnp.concatenate([p.reshape(-1) for p in parts])
    rows = -(-flat.shape[0] // (SLAB_LANES * SLAB_ROW_ALIGN)) * SLAB_ROW_ALIGN
    flat = jnp.pad(flat, (0, rows * SLAB_LANES - flat.shape[0]))
    return flat.reshape(rows, SLAB_LANES)


def _unpack(slab, shapes):
    flat = slab.reshape(-1)
    out, pos = [], 0
    for s in shapes:
        size = math.prod(s)
        out.append(flat[pos:pos + size].reshape(s))
        pos += size
    return out


def kernel(x, positions, norm_mix_pre, norm_mix_post, norm_ffn_pre, norm_ffn_post, w_in_even, lb_table, a_norm, b_ln_g, b_ln_b, b_ws, b_bias, w_out_even, w_in_odd, w_out_odd, w_ff1, w_ff2, loss_target, m_norm_mix_pre, m_norm_mix_post, m_norm_ffn_pre, m_norm_ffn_post, m_w_in_even, m_lb_table, m_a_norm, m_b_ln_g, m_b_ln_b, m_b_ws, m_b_bias, m_w_out_even, m_w_in_odd, m_w_out_odd, m_w_ff1, m_w_ff2, v_norm_mix_pre, v_norm_mix_post, v_norm_ffn_pre, v_norm_ffn_post, v_w_in_even, v_lb_table, v_a_norm, v_b_ln_g, v_b_ln_b, v_b_ws, v_b_bias, v_w_out_even, v_w_in_odd, v_w_out_odd, v_w_ff1, v_w_ff2):
    batch = x.shape[0]
    t = batch * SEQ
    d = D_MODEL
    x0 = x.reshape(t, d)
    target = loss_target.reshape(t, d)

    def gain(p, layer):
        return p[layer:layer + 1]

    def gather(*shards):
        return _Exchange("gather", [w.astype(BF16) for w in shards])

    def scatter(*grads):
        return _Exchange("scatter", grads)

    (win_e,) = exchange_alone(gather(w_in_even[0]), "gather_in_even")
    bias_t = b_bias[0].T
    proj, h0, w1_0 = norm_matmul(x0, gain(norm_mix_pre, 0), win_e, "in_proj_even", exchange=gather(w_ff1[0]))
    oa, states, decays, w2_0 = hgrn2_fwd(proj, lb_table, a_norm, batch, "hgrn2_fwd", exchange=gather(w_ff2[0]))
    mixin, wout_e = gmlp_fwd(proj, oa, b_ln_g, b_ln_b, b_ws[0], bias_t, "gmlp_fwd", exchange=gather(w_out_even[0]))
    mix0, x1 = out_proj(mixin, wout_e, x0, gain(norm_mix_post, 0), "out_proj_even")
    x2, hf0, a0, y0, win_o, wout_o = ffn_fwd(x1, gain(norm_ffn_pre, 0), w1_0, w2_0, gain(norm_ffn_post, 0),
                                             "ffn_fwd_0", exchange=gather(w_in_odd[0], w_out_odd[0]))
    x2p = _residue_major(x2, batch)
    qkv, h1 = norm_matmul(x2p, gain(norm_mix_pre, 1), win_o, "in_proj_odd")
    cos_t, sin_a, sin_b = rope_tables(_residue_major(positions.reshape(t, 1), batch), "rope_tables")
    ao, lse, q_rot, k_rot, w1_1, w2_1 = attn_fwd(qkv, cos_t, sin_a, sin_b, batch, "attn_fwd",
                                                 exchange=gather(w_ff1[1], w_ff2[1]))
    mix1, x3 = out_proj(ao, wout_o, x2p, gain(norm_mix_post, 1), "out_proj_odd")
    dx4, hf1, a1, y1, loss_part = ffn_fwd(x3, gain(norm_ffn_pre, 1), w1_1, w2_1, gain(norm_ffn_post, 1),
                                          "ffn_fwd_1", target=_residue_major(target, batch))

    hc = D_FF // N_CHIPS
    dx3, dy1, da1, dg_fpre1, dg_fpost1 = ffn_bwd(
        dx4, x3, y1, a1, gain(norm_ffn_pre, 1), gain(norm_ffn_post, 1), w1_1, w2_1, "ffn_bwd_1")
    g_w1_1 = weight_grad(hf1, da1, "b", d, hc, False, "wgrad_ff1_1")
    g_w2_1 = weight_grad(a1, dy1, "a", hc, d, True, "wgrad_ff2_1")
    dmix1, dao, dg_mpost1 = out_proj_bwd(dx3, mix1, gain(norm_mix_post, 1), wout_o, "out_proj_bwd_odd")
    g_wout_o = weight_grad(ao, dmix1, "a", d // N_CHIPS, d, False, "wgrad_out_odd")
    dqkv, r_w1_1, r_w2_1, r_wout_o = attn_bwd(q_rot, k_rot, qkv, cos_t, sin_a, sin_b, ao, lse, dao, batch, "attn_bwd",
                                              exchange=scatter(g_w1_1, g_w2_1, g_wout_o))
    dx2p, dg_mpre1 = norm_matmul_bwd(dqkv, win_o, x2p, gain(norm_mix_pre, 1), dx3, "in_proj_bwd_odd")
    dx2 = _sequence_order(dx2p, batch)
    g_win_o = weight_grad_stacked(h1, dqkv, 3 * d // N_CHIPS, "wgrad_in_odd")
    s_w1_1, s_w2_1, s_wout_o = (reduce_slabs(r, n) for r, n in (
        (r_w1_1, "reduce_ff1_1"), (r_w2_1, "reduce_ff2_1"), (r_wout_o, "reduce_out_odd")))
    dx1, dy0, da0, dg_fpre0, dg_fpost0, r_win_o, t_w1_1, t_w2_1, t_wout_o = ffn_bwd(
        dx2, x1, y0, a0, gain(norm_ffn_pre, 0), gain(norm_ffn_post, 0), w1_0, w2_0, "ffn_bwd_0",
        exchange=_Both(scatter(g_win_o), _Swap([s_w1_1, s_w2_1, s_wout_o])))
    g_w1_0 = weight_grad(hf0, da0, "b", d, hc, False, "wgrad_ff1_0")
    g_w2_0 = weight_grad(a0, dy0, "a", hc, d, True, "wgrad_ff2_0")
    dmix0, dmixin, dg_mpost0 = out_proj_bwd(dx1, mix0, gain(norm_mix_post, 0), wout_e, "out_proj_bwd_even")
    g_wout_e = weight_grad(mixin, dmix0, "a", d // N_CHIPS, d, False, "wgrad_out_even")
    s_win_o = reduce_slabs(r_win_o, "reduce_in_odd")
    dproj, d_lb, d_anorm, r_w1_0, t_win_o = hgrn2_bwd(
        proj, states, decays, lb_table, a_norm, dmixin, batch, "hgrn2_bwd",
        exchange=_Both(scatter(g_w1_0), _Swap([s_win_o])))
    s_w1_0 = reduce_slabs(r_w1_0, "reduce_ff1_0")
    dproj, d_lng, d_lnb, d_ws, d_bias_t, r_w2_0, t_w1_0 = gmlp_bwd(
        proj, dmixin, b_ln_g, b_ln_b, b_ws[0], bias_t, dproj, "gmlp_bwd",
        exchange=_Both(scatter(g_w2_0), _Swap([s_w1_0])))
    s_w2_0 = reduce_slabs(r_w2_0, "reduce_ff2_0")
    g_win_e, r_wout_e, t_w2_0 = weight_grad(h0, dproj, "b", d, 3 * d // N_CHIPS, False, "wgrad_in_even",
                                            exchange=_Both(scatter(g_wout_e), _Swap([s_w2_0])))
    s_wout_e = reduce_slabs(r_wout_e, "reduce_out_even")
    dx0, dg_mpre0, r_win_e, t_wout_e = norm_matmul_bwd(
        dproj, win_e, x0, gain(norm_mix_pre, 0), dx1, "in_proj_bwd_even",
        exchange=_Both(scatter(g_win_e), _Swap([s_wout_e])))
    grad_x = dx0.reshape(x.shape)
    s_win_e = reduce_slabs(r_win_e, "reduce_in_even")
    (t_win_e,) = exchange_alone(_Swap([s_win_e]), "sibling_swap")

    big_w = [w_in_even, w_out_even, w_in_odd, w_out_odd, w_ff1, w_ff2]
    big_m = [m_w_in_even, m_w_out_even, m_w_in_odd, m_w_out_odd, m_w_ff1, m_w_ff2]
    big_v = [v_w_in_even, v_w_out_even, v_w_in_odd, v_w_out_odd, v_w_ff1, v_w_ff2]
    mine = [[s_win_e], [s_wout_e], [s_win_o], [s_wout_o], [s_w1_0, s_w1_1], [s_w2_0, s_w2_1]]
    theirs = [[t_win_e], [t_wout_e], [t_win_o], [t_wout_o], [t_w1_0, t_w1_1], [t_w2_0, t_w2_1]]
    big = []
    for i, (w, m, v) in enumerate(zip(big_w, big_m, big_v)):
        two_d = (-1, w.shape[-1])
        res = adamw_big(w.reshape(two_d), mine[i], theirs[i], m.reshape(two_d), v.reshape(two_d), "adamw_big_%d" % i)
        big.append([r.reshape(w.shape) for r in res])

    small_w = [norm_mix_pre, norm_mix_post, norm_ffn_pre, norm_ffn_post, lb_table, a_norm, b_ln_g, b_ln_b, b_ws, b_bias]
    small_m = [m_norm_mix_pre, m_norm_mix_post, m_norm_ffn_pre, m_norm_ffn_post, m_lb_table, m_a_norm, m_b_ln_g,
               m_b_ln_b, m_b_ws, m_b_bias]
    small_v = [v_norm_mix_pre, v_norm_mix_post, v_norm_ffn_pre, v_norm_ffn_post, v_lb_table, v_a_norm, v_b_ln_g,
               v_b_ln_b, v_b_ws, v_b_bias]
    partial = [jnp.concatenate([dg_mpre0, dg_mpre1]), jnp.concatenate([dg_mpost0, dg_mpost1]),
               jnp.concatenate([dg_fpre0, dg_fpre1]), jnp.concatenate([dg_fpost0, dg_fpost1]),
               d_lb, d_anorm, d_lng, d_lnb, d_ws[None], d_bias_t.T[None]]
    *small_g, loss = _unpack(allreduce_small(_pack(partial + [loss_part]), "allreduce_small"),
                             [w.shape for w in small_w] + [()])
    small_d, small_nm, small_nv = adamw_small(small_w, small_g, small_m, small_v, "adamw_small")

    order = ["norm_mix_pre", "norm_mix_post", "norm_ffn_pre", "norm_ffn_post", "w_in_even", "lb_table", "a_norm",
             "b_ln_g", "b_ln_b", "b_ws", "b_bias", "w_out_even", "w_in_odd", "w_out_odd", "w_ff1", "w_ff2"]
    small_names = ["norm_mix_pre", "norm_mix_post", "norm_ffn_pre", "norm_ffn_post", "lb_table", "a_norm",
                   "b_ln_g", "b_ln_b", "b_ws", "b_bias"]
    big_names = ["w_in_even", "w_out_even", "w_in_odd", "w_out_odd", "w_ff1", "w_ff2"]
    grads, deltas, new_m, new_v = {}, {}, {}, {}
    for i, nm in enumerate(small_names):
        grads[nm], deltas[nm], new_m[nm], new_v[nm] = small_g[i], small_d[i], small_nm[i], small_nv[i]
    for i, nm in enumerate(big_names):
        grads[nm], deltas[nm], new_m[nm], new_v[nm] = big[i]
    return (loss, grad_x, *[grads[n] for n in order], *[deltas[n] for n in order],
            *[new_m[n] for n in order], *[new_v[n] for n in order])
```

```python
import functools
import math

import jax
import jax.numpy as jnp
from jax import lax
from jax.experimental import pallas as pl
from jax.experimental.pallas import tpu as pltpu

F32 = jnp.float32
BF16 = jnp.bfloat16
MESH = pl.DeviceIdType.MESH

D_MODEL = 1024
SEQ = 2048
D_FF = 4096
N_CHIPS = 4
A_WIDTH = 512
A_HEADS = 4
A_DK = 128
A_CHUNK = 64
A_SUB = 16
B_WIDTH = 512
B_GROUPS = 4
B_CHUNK = 128
C_HEADS = 16
C_HEAD_DIM = 64
C_ROT_HALF = 8
C_BLOCK = 128
C_DILATIONS = (1, 4, 16)
ROPE_THETA = 500000.0
EPS = 1e-6
ADAM_LR = 0.001
ADAM_B1 = 0.9
ADAM_B2 = 0.999
ADAM_EPS = 1e-08
ADAM_WD = 0.01
ADAM_STEP = 10

ROW_TILE = 512
FFN_ROWS = 1024
WGRAD_ROWS = 2048
VMEM_LIMIT = 56 * 1024 * 1024
NEG_BIG = -1e30


def _params(sem=None):
    return pltpu.CompilerParams(dimension_semantics=sem, vmem_limit_bytes=VMEM_LIMIT)


def _dot(a, b):
    return jnp.dot(a, b, preferred_element_type=F32)


def _dot_nt(a, b):
    return lax.dot_general(a, b, (((1,), (1,)), ((), ())), preferred_element_type=F32)


def _dot_tn(a, b):
    return lax.dot_general(a, b, (((0,), (0,)), ((), ())), preferred_element_type=F32)


def _rms(x, g):
    r = lax.rsqrt(jnp.mean(x * x, axis=-1, keepdims=True) + EPS)
    return x * r * g


def _rms_bwd(x, g, dy):
    r = lax.rsqrt(jnp.mean(x * x, axis=-1, keepdims=True) + EPS)
    xh = x * r
    dg = jnp.sum(dy * xh, axis=0, keepdims=True)
    dxh = dy * g
    dx = r * (dxh - xh * jnp.mean(dxh * xh, axis=-1, keepdims=True))
    return dx, dg


def _accumulate(ref, val, first):
    @pl.when(first)
    def _():
        ref[...] = val

    @pl.when(jnp.logical_not(first))
    def _():
        ref[...] += val


N_DEV = 8
ANY = pl.BlockSpec(memory_space=pl.ANY)


def _place():
    x, y, c = lax.axis_index("x"), lax.axis_index("y"), lax.axis_index("c")
    return x, y, c, [(1 - x, y), (x, 1 - y), (1 - x, 1 - y)]


class _Exchange:
    def __init__(self, kind, arrays):
        self.kind, self.arrays, self.n = kind, list(arrays), len(arrays)
        per_peer = pltpu.SemaphoreType.DMA((3 * self.n,))
        if kind == "gather":
            self.out_shape = [jax.ShapeDtypeStruct((N_CHIPS,) + a.shape, a.dtype) for a in self.arrays]
            self.scratch = [per_peer, per_peer, pltpu.SemaphoreType.DMA((self.n,)), per_peer, per_peer]
        else:
            self.out_shape = [jax.ShapeDtypeStruct(a.shape, a.dtype) for a in self.arrays]
            self.scratch = [per_peer, per_peer, pltpu.SemaphoreType.DMA((self.n,))]

    def _copies(self, ins, outs, sems):
        send_sems, recv_sems, local_sems = sems[:3]
        x, y, c, chips = _place()
        me = 2 * x + y
        local, remote = [], []
        for a in range(self.n):
            if self.kind == "gather":
                local.append(pltpu.make_async_copy(ins[a], outs[a].at[me], local_sems.at[a]))
                half = self.arrays[a].shape[0] // 2

                def rows(ref, core, half=half):
                    return ref.at[pl.ds(core * half, half)]
            else:
                local.append(pltpu.make_async_copy(ins[a].at[me], outs[a].at[3], local_sems.at[a]))
            for j, (px, py) in enumerate(chips):
                k = 3 * a + j
                peer = 2 * px + py

                def copy(src, dst, to, send_sem=send_sems.at[k], recv_sem=recv_sems.at[k]):
                    return pltpu.make_async_remote_copy(src_ref=src, dst_ref=dst, send_sem=send_sem, recv_sem=recv_sem,
                                                        device_id=to, device_id_type=MESH)

                if self.kind == "gather":
                    sent = copy(rows(ins[a], c), rows(outs[a].at[me], c), (px, py, c))
                    landed = copy(rows(ins[a], c), rows(outs[a].at[peer], c), (px, py, c))
                    on = dict(send_sem=sems[3].at[k], recv_sem=sems[4].at[k])
                    passed = copy(rows(outs[a].at[peer], c), rows(outs[a].at[peer], c), (x, y, 1 - c), **on)
                    handed = copy(rows(outs[a].at[peer], c), rows(outs[a].at[peer], 1 - c), (x, y, 1 - c), **on)
                    remote.append((sent, landed, passed, handed))
                else:
                    sent = copy(ins[a].at[peer], outs[a].at[j], (px, py, c))
                    remote.append((sent, sent, None, None))
        return local, remote

    def start(self, ins, outs, sems):
        local, remote = self._copies(ins, outs, sems)
        for cp in local:
            cp.start()
        for sent, _, _, _ in remote:
            sent.start()

    def finish(self, ins, outs, sems):
        local, remote = self._copies(ins, outs, sems)
        for _, landed, passed, _ in remote:
            landed.wait_recv()
            if passed is not None:
                passed.start()
        for sent, _, passed, handed in remote:
            if passed is not None:
                handed.wait_recv()
                passed.wait_send()
            sent.wait_send()
        for cp in local:
            cp.wait()


class _Swap:
    def __init__(self, arrays):
        self.arrays, self.n = list(arrays), len(arrays)
        self.out_shape = [jax.ShapeDtypeStruct(a.shape, a.dtype) for a in self.arrays]
        self.scratch = [pltpu.SemaphoreType.DMA((self.n,)), pltpu.SemaphoreType.DMA((self.n,))]

    def _copies(self, ins, outs, sems):
        x, y, c, _ = _place()
        return [pltpu.make_async_remote_copy(src_ref=ins[a], dst_ref=outs[a], send_sem=sems[0].at[a],
                                             recv_sem=sems[1].at[a], device_id=(x, y, 1 - c), device_id_type=MESH)
                for a in range(self.n)]

    def start(self, ins, outs, sems):
        for cp in self._copies(ins, outs, sems):
            cp.start()

    def finish(self, ins, outs, sems):
        for cp in self._copies(ins, outs, sems):
            cp.wait_recv()
            cp.wait_send()


class _Both:
    def __init__(self, first, second):
        self.parts = (first, second)
        self.arrays, self.n = first.arrays + second.arrays, first.n + second.n
        self.out_shape = first.out_shape + second.out_shape
        self.scratch = first.scratch + second.scratch

    def _split(self, ins, outs, sems):
        a, b = self.parts
        return ((a, ins[:a.n], outs[:a.n], sems[:len(a.scratch)]),
                (b, ins[a.n:], outs[a.n:], sems[len(a.scratch):]))

    def start(self, ins, outs, sems):
        for ex, i, o, s in self._split(ins, outs, sems):
            ex.start(i, o, s)

    def finish(self, ins, outs, sems):
        for ex, i, o, s in self._split(ins, outs, sems):
            ex.finish(i, o, s)


def _call(body, *, name, grid, in_specs, out_specs, out_shape, args, scratch_shapes=(), aliases=None, exchange=None):
    if exchange is None:
        return pl.pallas_call(
            body, name=name, grid=grid, in_specs=in_specs, out_specs=out_specs, out_shape=out_shape,
            scratch_shapes=list(scratch_shapes), input_output_aliases=aliases or {},
            compiler_params=_params(("arbitrary",) * len(grid)))(*args)
    n_in, n_out, n_scr, n_ex = len(in_specs), len(out_specs), len(scratch_shapes), exchange.n
    steps = grid

    def wrapped(*refs):
        ins, refs = refs[:n_in], refs[n_in:]
        ex_in, refs = refs[:n_ex], refs[n_ex:]
        outs, refs = refs[:n_out], refs[n_out:]
        ex_out, refs = refs[:n_ex], refs[n_ex:]
        scr, sems = refs[:n_scr], refs[n_scr:]
        first = functools.reduce(jnp.logical_and, [pl.program_id(k) == 0 for k in range(len(steps))])
        last = functools.reduce(jnp.logical_and, [pl.program_id(k) == steps[k] - 1 for k in range(len(steps))])

        @pl.when(first)
        def _():
            exchange.start(ex_in, ex_out, sems)

        body(*ins, *outs, *scr)

        @pl.when(last)
        def _():
            exchange.finish(ex_in, ex_out, sems)

    return pl.pallas_call(
        wrapped, name=name, grid=grid,
        in_specs=list(in_specs) + [ANY] * n_ex, out_specs=list(out_specs) + [ANY] * n_ex,
        out_shape=list(out_shape) + exchange.out_shape,
        scratch_shapes=list(scratch_shapes) + exchange.scratch, input_output_aliases=aliases or {},
        compiler_params=_params(("arbitrary",) * len(grid)))(*args, *exchange.arrays)


def exchange_alone(exchange, name):
    def body(*refs):
        n = exchange.n
        exchange.start(refs[:n], refs[n:2 * n], refs[2 * n:])
        exchange.finish(refs[:n], refs[n:2 * n], refs[2 * n:])

    return pl.pallas_call(
        body, name=name, in_specs=[ANY] * exchange.n, out_specs=[ANY] * exchange.n,
        out_shape=exchange.out_shape, scratch_shapes=exchange.scratch)(*exchange.arrays)


def norm_matmul(x, g, wg, name, exchange=None):
    t, d = x.shape
    nl = wg.shape[2]

    def body(x_ref, g_ref, w_ref, o_ref, h_ref):
        h = _rms(x_ref[...], g_ref[...]).astype(BF16)
        h_ref[...] = h
        for c in range(N_CHIPS):
            o_ref[:, c * nl:(c + 1) * nl] = _dot(h, w_ref[c])

    return _call(
        body, name=name, grid=(t // ROW_TILE,),
        in_specs=[pl.BlockSpec((ROW_TILE, d), lambda i: (i, 0)),
                  pl.BlockSpec((1, d), lambda i: (0, 0)),
                  pl.BlockSpec((N_CHIPS, d, nl), lambda i: (0, 0, 0))],
        out_specs=[pl.BlockSpec((ROW_TILE, N_CHIPS * nl), lambda i: (i, 0)),
                   pl.BlockSpec((ROW_TILE, d), lambda i: (i, 0))],
        out_shape=[jax.ShapeDtypeStruct((t, N_CHIPS * nl), F32), jax.ShapeDtypeStruct((t, d), BF16)],
        args=(x, g, wg), exchange=exchange)


def norm_matmul_bwd(dproj, wg, x, g, dres, name, exchange=None):
    t, d = x.shape
    nl = wg.shape[2]
    stacked = dproj.ndim == 3
    piece = math.gcd(nl, dproj.shape[-1])

    def body(dp_ref, w_ref, x_ref, g_ref, dres_ref, dx_ref, dg_ref):
        dh = None
        for j in range(N_CHIPS * nl // piece):
            c, off = divmod(j * piece, nl)
            if stacked:
                p, lo = divmod(j * piece, dproj.shape[-1])
                lhs = dp_ref[p, :, lo:lo + piece]
            else:
                lhs = dp_ref[:, j * piece:(j + 1) * piece]
            part = _dot_nt(lhs.astype(BF16), w_ref[c, :, off:off + piece])
            dh = part if dh is None else dh + part
        dx, dg = _rms_bwd(x_ref[...], g_ref[...], dh)
        dx_ref[...] = dres_ref[...] + dx
        _accumulate(dg_ref, dg, pl.program_id(0) == 0)

    row = pl.BlockSpec((ROW_TILE, d), lambda i: (i, 0))
    vec = pl.BlockSpec((1, d), lambda i: (0, 0))
    if stacked:
        dp_spec = pl.BlockSpec((dproj.shape[0], ROW_TILE, dproj.shape[-1]), lambda i: (0, i, 0))
    else:
        dp_spec = pl.BlockSpec((ROW_TILE, N_CHIPS * nl), lambda i: (i, 0))
    return _call(
        body, name=name, grid=(t // ROW_TILE,),
        in_specs=[dp_spec, pl.BlockSpec((N_CHIPS, d, nl), lambda i: (0, 0, 0)), row, vec, row],
        out_specs=[row, vec],
        out_shape=[jax.ShapeDtypeStruct((t, d), F32), jax.ShapeDtypeStruct((1, d), F32)],
        args=(dproj, wg, x, g, dres), exchange=exchange)


def out_proj(a, wg, x, g, name):
    t, d = x.shape
    kl = wg.shape[1]

    def body(a_ref, w_ref, x_ref, g_ref, mix_ref, xo_ref):
        acc = _dot(a_ref[:, 0:kl], w_ref[0])
        for c in range(1, N_CHIPS):
            acc += _dot(a_ref[:, c * kl:(c + 1) * kl], w_ref[c])
        mix_ref[...] = acc
        xo_ref[...] = x_ref[...] + _rms(acc, g_ref[...])

    row = pl.BlockSpec((ROW_TILE, d), lambda i: (i, 0))
    return pl.pallas_call(
        body, name=name, grid=(t // ROW_TILE,),
        in_specs=[row, pl.BlockSpec((N_CHIPS, kl, d), lambda i: (0, 0, 0)), row,
                  pl.BlockSpec((1, d), lambda i: (0, 0))],
        out_specs=[row, row],
        out_shape=[jax.ShapeDtypeStruct((t, d), F32), jax.ShapeDtypeStruct((t, d), F32)],
        compiler_params=_params(("arbitrary",)),
    )(a, wg, x, g)


def out_proj_bwd(dxo, mix, g, wg, name):
    t, d = mix.shape
    kl = wg.shape[1]

    def body(dxo_ref, mix_ref, g_ref, w_ref, dmix_ref, da_ref, dg_ref):
        dmix, dg = _rms_bwd(mix_ref[...], g_ref[...], dxo_ref[...])
        dmb = dmix.astype(BF16)
        dmix_ref[...] = dmb
        for c in range(N_CHIPS):
            da_ref[:, c * kl:(c + 1) * kl] = _dot_nt(dmb, w_ref[c])
        _accumulate(dg_ref, dg, pl.program_id(0) == 0)

    row = pl.BlockSpec((ROW_TILE, d), lambda i: (i, 0))
    vec = pl.BlockSpec((1, d), lambda i: (0, 0))
    return pl.pallas_call(
        body, name=name, grid=(t // ROW_TILE,),
        in_specs=[row, row, vec, pl.BlockSpec((N_CHIPS, kl, d), lambda i: (0, 0, 0))],
        out_specs=[row, row, vec],
        out_shape=[jax.ShapeDtypeStruct((t, d), BF16), jax.ShapeDtypeStruct((t, d), F32),
                   jax.ShapeDtypeStruct((1, d), F32)],
        compiler_params=_params(("arbitrary",)),
    )(dxo, mix, g, wg)


def ffn_fwd(x, gpre, w1g, w2g, gpost, name, exchange=None, target=None):
    t, d = x.shape
    hc = w1g.shape[2]
    with_loss = target is not None

    def body(x_ref, gpre_ref, w1_ref, w2_ref, gpost_ref, *rest):
        if with_loss:
            t_ref, xo_ref, h_ref, a_ref, y_ref, l_ref, acc = rest
        else:
            xo_ref, h_ref, a_ref, y_ref, acc = rest
        i, c = pl.program_id(0), pl.program_id(1)

        @pl.when(c == 0)
        def _():
            h_ref[...] = _rms(x_ref[...], gpre_ref[...]).astype(BF16)

        a = _dot(h_ref[...], w1_ref[...])
        a_ref[...] = a.astype(BF16)
        r = jnp.square(jnp.maximum(a, 0.0)).astype(BF16)
        _accumulate(acc, _dot(r, w2_ref[...]), c == 0)

        @pl.when(c == N_CHIPS - 1)
        def _():
            y = acc[...]
            y_ref[...] = y
            xo = x_ref[...] + _rms(y, gpost_ref[...])
            if with_loss:
                e = xo - t_ref[...]
                xo_ref[...] = e * (1.0 / d)
                part = jnp.sum(jnp.sum(e * e, axis=-1, keepdims=True), axis=0, keepdims=True) * (0.5 / d)
                _accumulate(l_ref, part, i == 0)
            else:
                xo_ref[...] = xo

    row = pl.BlockSpec((FFN_ROWS, d), lambda i, c: (i, 0))
    vec = pl.BlockSpec((1, d), lambda i, c: (0, 0))
    one = pl.BlockSpec((1, 1), lambda i, c: (0, 0))
    return _call(
        body, name=name, grid=(t // FFN_ROWS, N_CHIPS),
        in_specs=[row, vec,
                  pl.BlockSpec((None, d, hc), lambda i, c: (c, 0, 0)),
                  pl.BlockSpec((None, hc, d), lambda i, c: (c, 0, 0)), vec] + ([row] if with_loss else []),
        out_specs=[row, row, pl.BlockSpec((FFN_ROWS, hc), lambda i, c: (i, c)), row] + ([one] if with_loss else []),
        out_shape=[jax.ShapeDtypeStruct((t, d), F32), jax.ShapeDtypeStruct((t, d), BF16),
                   jax.ShapeDtypeStruct((t, N_CHIPS * hc), BF16), jax.ShapeDtypeStruct((t, d), F32)]
        + ([jax.ShapeDtypeStruct((1, 1), F32)] if with_loss else []),
        scratch_shapes=[pltpu.VMEM((FFN_ROWS, d), F32)],
        args=(x, gpre, w1g, w2g, gpost) + ((target,) if with_loss else ()), exchange=exchange)


def ffn_bwd(dxo, x, y, a, gpre, gpost, w1g, w2g, name, exchange=None):
    t, d = x.shape
    hc = w1g.shape[2]

    def body(dxo_ref, x_ref, y_ref, a_ref, gpre_ref, gpost_ref, w1_ref, w2_ref,
             dxi_ref, dy_ref, da_ref, dgpre_ref, dgpost_ref, acc):
        i, c = pl.program_id(0), pl.program_id(1)

        @pl.when(c == 0)
        def _():
            dy, dg = _rms_bwd(y_ref[...], gpost_ref[...], dxo_ref[...])
            dy_ref[...] = dy.astype(BF16)
            _accumulate(dgpost_ref, dg, i == 0)

        dr = _dot_nt(dy_ref[...], w2_ref[...])
        da = (dr * (2.0 * jnp.maximum(a_ref[...].astype(F32), 0.0))).astype(BF16)
        da_ref[...] = da
        _accumulate(acc, _dot_nt(da, w1_ref[...]), c == 0)

        @pl.when(c == N_CHIPS - 1)
        def _():
            dx, dg = _rms_bwd(x_ref[...], gpre_ref[...], acc[...])
            dxi_ref[...] = dxo_ref[...] + dx
            _accumulate(dgpre_ref, dg, i == 0)

    row = pl.BlockSpec((ROW_TILE, d), lambda i, c: (i, 0))
    vec = pl.BlockSpec((1, d), lambda i, c: (0, 0))
    hid = pl.BlockSpec((ROW_TILE, hc), lambda i, c: (i, c))
    return _call(
        body, name=name, grid=(t // ROW_TILE, N_CHIPS),
        in_specs=[row, row, row, hid, vec, vec,
                  pl.BlockSpec((None, d, hc), lambda i, c: (c, 0, 0)),
                  pl.BlockSpec((None, hc, d), lambda i, c: (c, 0, 0))],
        out_specs=[row, row, hid, vec, vec],
        out_shape=[jax.ShapeDtypeStruct((t, d), F32), jax.ShapeDtypeStruct((t, d), BF16),
                   jax.ShapeDtypeStruct((t, N_CHIPS * hc), BF16),
                   jax.ShapeDtypeStruct((1, d), F32), jax.ShapeDtypeStruct((1, d), F32)],
        scratch_shapes=[pltpu.VMEM((ROW_TILE, d), F32)],
        args=(dxo, x, y, a, gpre, gpost, w1g, w2g), exchange=exchange)


def weight_grad(a, b, chunked, bk, bn, relu2, name, exchange=None):
    t = a.shape[0]
    a_on = chunked == "a"
    rows = min(t, WGRAD_ROWS)
    n_steps = t // rows

    def body(a_ref, b_ref, o_ref, acc):
        s = pl.program_id(1)
        av = a_ref[...]
        if relu2:
            av = jnp.square(jnp.maximum(av.astype(F32), 0.0))
        _accumulate(acc, _dot_tn(av.astype(BF16), b_ref[...].astype(BF16)), s == 0)

        @pl.when(s == n_steps - 1)
        def _():
            o_ref[...] = acc[...].astype(BF16)

    res = _call(
        body, name=name, grid=(N_CHIPS, n_steps),
        in_specs=[pl.BlockSpec((rows, bk), (lambda c, s: (s, c)) if a_on else (lambda c, s: (s, 0))),
                  pl.BlockSpec((rows, bn), (lambda c, s: (s, 0)) if a_on else (lambda c, s: (s, c)))],
        out_specs=[pl.BlockSpec((None, bk, bn), lambda c, s: (c, 0, 0))],
        out_shape=[jax.ShapeDtypeStruct((N_CHIPS, bk, bn), BF16)],
        scratch_shapes=[pltpu.VMEM((bk, bn), F32)],
        args=(a, b), exchange=exchange)
    return res[0] if exchange is None else res


def weight_grad_stacked(a, b3, bn, name):
    t, bk = a.shape
    width = b3.shape[-1]
    piece = math.gcd(bn, width)
    rows = min(t, WGRAD_ROWS)
    n_steps = t // rows

    def body(a_ref, b_ref, o_hbm, acc, staged, sem):
        s, c = pl.program_id(0), pl.program_id(1)
        av = a_ref[...].astype(BF16)
        for chunk in range(N_CHIPS):
            @pl.when(c == chunk)
            def _(chunk=chunk):
                cols = [divmod(chunk * bn + k * piece, width) for k in range(bn // piece)]
                b = jnp.concatenate([b_ref[p, :, lo:lo + piece] for p, lo in cols], axis=1).astype(BF16)
                _accumulate(acc.at[chunk], _dot_tn(av, b), s == 0)

                @pl.when(s == n_steps - 1)
                def _():
                    staged[...] = acc[chunk].astype(BF16)
                    copy = pltpu.make_async_copy(staged, o_hbm.at[chunk], sem)
                    copy.start()
                    copy.wait()

    return pl.pallas_call(
        body, name=name, grid=(n_steps, N_CHIPS),
        in_specs=[pl.BlockSpec((rows, bk), lambda s, c: (s, 0)),
                  pl.BlockSpec((b3.shape[0], rows, width), lambda s, c: (0, s, 0))],
        out_specs=ANY,
        out_shape=jax.ShapeDtypeStruct((N_CHIPS, bk, bn), BF16),
        scratch_shapes=[pltpu.VMEM((N_CHIPS, bk, bn), F32), pltpu.VMEM((bk, bn), BF16), pltpu.SemaphoreType.DMA],
        compiler_params=_params(("arbitrary", "arbitrary")),
    )(a, b3)


def _hgrn2_chunk(st, qs, fls, ivs, gls, l0, l1, l2, ng):
    nsub = len(qs)
    mx = jnp.maximum(jnp.maximum(l0, l1), l2)
    e0, e1, e2 = jnp.exp(l0 - mx), jnp.exp(l1 - mx), jnp.exp(l2 - mx)
    lb = e0 / (e0 + e1 + e2)
    rows = lax.broadcasted_iota(jnp.int32, (A_SUB, A_SUB), 0)
    cols = lax.broadcasted_iota(jnp.int32, (A_SUB, A_SUB), 1)
    tri = (rows >= cols).astype(F32)
    keep = (lax.broadcasted_iota(jnp.int32, (A_SUB, A_SUB, A_DK), 0)
            >= lax.broadcasted_iota(jnp.int32, (A_SUB, A_SUB, A_DK), 1))
    base = jnp.zeros_like(l0)
    bases, gs, ks, qfs = [], [], [], []
    for i in range(nsub):
        f = lb + (1.0 - lb) * jax.nn.sigmoid(fls[i])
        logf = jnp.log(f)
        bases.append(base)
        gs.append(base + jnp.dot(tri, logf, precision=lax.Precision.HIGHEST, preferred_element_type=F32))
        base = base + jnp.sum(logf, axis=0, keepdims=True)
        ks.append(1.0 - f)
        qfs.append(jax.nn.silu(qs[i]))
    g_last = base
    stb = st.astype(BF16)
    outs = []
    for i in range(nsub):
        o = _dot_nt((qfs[i] * jnp.exp(gs[i])).astype(BF16), stb)
        if i > 0:
            qt = (qfs[i] * jnp.exp(gs[i] - bases[i])).astype(BF16)
            kk = jnp.concatenate([ks[j] * jnp.exp(bases[i] - gs[j]) for j in range(i)], axis=0).astype(BF16)
            vv = jnp.concatenate(ivs[:i], axis=0).astype(BF16)
            o = o + _dot(_dot_nt(qt, kk).astype(BF16), vv)
        dec = jnp.exp(jnp.where(keep, gs[i][:, None, :] - gs[i][None, :, :], NEG_BIG))
        s_diag = jnp.sum(qfs[i][:, None, :] * ks[i][None, :, :] * dec, axis=-1)
        o = o + _dot(s_diag.astype(BF16), ivs[i].astype(BF16))
        o = o * lax.rsqrt(jnp.mean(o * o, axis=-1, keepdims=True) + EPS) * ng
        outs.append(o * jax.nn.silu(gls[i]))
    kdec = jnp.concatenate([ks[j] * jnp.exp(g_last - gs[j]) for j in range(nsub)], axis=0).astype(BF16)
    vall = jnp.concatenate(ivs, axis=0).astype(BF16)
    new_st = st * jnp.exp(g_last) + _dot_tn(vall, kdec)
    return new_st, outs


A_MAX_LOG_DECAY = 60.0


def _half_sums(logf):
    n = logf.shape[0]
    first = lax.broadcasted_iota(jnp.int32, logf.shape, 0) < n // 2
    return (jnp.sum(jnp.where(first, logf, 0.0), axis=0, keepdims=True),
            jnp.sum(jnp.where(first, 0.0, logf), axis=0, keepdims=True))


def _split3(x):
    hi = x.astype(BF16)
    r1 = x - hi.astype(F32)
    mid = r1.astype(BF16)
    return hi, mid, (r1 - mid.astype(F32)).astype(BF16)


def _tri_matmul(x, transpose):
    n = x.shape[0]
    r = lax.broadcasted_iota(jnp.int32, (n, n), 0)
    c = lax.broadcasted_iota(jnp.int32, (n, n), 1)
    tri = ((r <= c) if transpose else (r >= c)).astype(BF16)
    hi, mid, lo = _split3(x)
    return (_dot(tri, lo) + _dot(tri, mid)) + _dot(tri, hi)


@jax.custom_vjp
def _cumsum_rows(x):
    return _tri_matmul(x, False)


def _cumsum_rows_fwd(x):
    return _tri_matmul(x, False), None


def _cumsum_rows_bwd(_, dy):
    return (_tri_matmul(dy, True),)


_cumsum_rows.defvjp(_cumsum_rows_fwd, _cumsum_rows_bwd)


def _lower_bound(l0, l1, l2):
    mx = jnp.maximum(jnp.maximum(l0, l1), l2)
    e0, e1, e2 = jnp.exp(l0 - mx), jnp.exp(l1 - mx), jnp.exp(l2 - mx)
    return e0 / (e0 + e1 + e2)


def _b(x):
    return x.astype(BF16)


@jax.custom_vjp
def _mm(a, b):
    return _dot(_b(a), _b(b))


_mm.defvjp(lambda a, b: (_mm(a, b), (a, b)),
           lambda res, d: (_dot_nt(_b(d), _b(res[1])), _dot_tn(_b(res[0]), _b(d))))


@jax.custom_vjp
def _mm_nt(a, b):
    return _dot_nt(_b(a), _b(b))


_mm_nt.defvjp(lambda a, b: (_mm_nt(a, b), (a, b)),
              lambda res, d: (_dot(_b(d), _b(res[1])), _dot_tn(_b(d), _b(res[0]))))


def _dot_split(dot, a, b):
    ah, bh = _b(a), _b(b)
    al, bl = _b(a - ah.astype(F32)), _b(b - bh.astype(F32))
    return (dot(ah, bl) + dot(al, bh)) + dot(ah, bh)


@jax.custom_vjp
def _mm_scores(a, b):
    return _dot_nt(_b(a), _b(b))


_mm_scores.defvjp(lambda a, b: (_mm_scores(a, b), (a, b)),
                  lambda res, d: (_dot_split(_dot, d, res[1]), _dot_split(_dot_tn, d, res[0])))


@jax.custom_vjp
def _mm_tn(a, b):
    return _dot_tn(_b(a), _b(b))


_mm_tn.defvjp(lambda a, b: (_mm_tn(a, b), (a, b)),
              lambda res, d: (_dot_nt(_b(res[1]), _b(d)), _dot(_b(res[0]), _b(d))))


@jax.custom_vjp
def _split_heads(x):
    return tuple(x[:, h * A_DK:(h + 1) * A_DK] for h in range(A_HEADS))


def _split_heads_fwd(x):
    return _split_heads(x), None


def _split_heads_bwd(_, parts):
    return (jnp.concatenate(parts, axis=1),)


_split_heads.defvjp(_split_heads_fwd, _split_heads_bwd)


def _hgrn2_chunk_fast(sts, q, fl, iv, gl, l0, l1, l2, ng):
    lb = _lower_bound(l0, l1, l2)
    f = lb + (1.0 - lb) * jax.nn.sigmoid(fl)
    return _hgrn2_fast_core(sts, q, f, jnp.log(f), iv, gl, ng)


def _hgrn2_fast_core(sts, q, f, logf, iv, gl, ng):
    g = _cumsum_rows(logf)
    g_mid, g_last = _half_sums(logf)
    g_last = g_mid + g_last
    k = 1.0 - f
    qf = jax.nn.silu(q)
    qms = _split_heads(qf * jnp.exp(g - g_mid))
    kms = _split_heads(k * jnp.exp(g_mid - g))
    qgs = _split_heads(qf * jnp.exp(g))
    kds = _split_heads(k * jnp.exp(g_last - g))
    ivs = _split_heads(iv)
    decays = _split_heads(jnp.exp(g_last))
    n = q.shape[0]
    causal = lax.broadcasted_iota(jnp.int32, (n, n), 0) >= lax.broadcasted_iota(jnp.int32, (n, n), 1)
    raw = [_mm_scores(qm, km) for qm, km in zip(qms, kms)]
    inter = [_mm_nt(qg, st) for qg, st in zip(qgs, sts)]
    scores = [jnp.where(causal, s, 0.0) for s in raw]
    os = [a + _mm(s, v) for a, s, v in zip(inter, scores, ivs)]
    new_sts = [st * d + _mm_tn(v, kd) for st, d, v, kd in zip(sts, decays, ivs, kds)]
    os = [o * lax.rsqrt(jnp.mean(o * o, axis=-1, keepdims=True) + EPS) for o in os]
    return new_sts, jnp.concatenate(os, axis=1) * ng * jax.nn.silu(gl)


A_STEP_CHUNKS = 4


def _chunk_rows(j):
    return pl.ds(pl.multiple_of(j * A_CHUNK, A_CHUNK), A_CHUNK)


def _sub_rows(j, i):
    return pl.ds(pl.multiple_of(j * A_CHUNK + i * A_SUB, A_SUB), A_SUB)


def _sub_blocks(ref, head, j):
    lanes = slice(head * A_DK, (head + 1) * A_DK)
    return [ref[_sub_rows(j, i), lanes] for i in range(A_CHUNK // A_SUB)]


def hgrn2_fwd(proj, lb_table, a_norm, batch, name, exchange=None):
    t = proj.shape[0]
    n_steps = t // batch // (A_CHUNK * A_STEP_CHUNKS)
    rows = A_CHUNK * A_STEP_CHUNKS

    def body(q_ref, f_ref, i_ref, g_ref, lb_ref, ng_ref, o_ref, st_ref, dec_ref, st):
        @pl.when(pl.program_id(1) == 0)
        def _():
            st[...] = jnp.zeros_like(st)

        def chunk(j, carry):
            r = _chunk_rows(j)
            st_ref[j] = st[...]
            lb = _lower_bound(lb_ref[0:1, :], lb_ref[1:2, :], lb_ref[2:3, :])
            f = lb + (1.0 - lb) * jax.nn.sigmoid(f_ref[r, :])
            logf = jnp.log(f)
            decay = jnp.minimum(*_half_sums(logf))
            dec_ref[j] = decay
            mild = jnp.min(decay) >= -A_MAX_LOG_DECAY

            @pl.when(mild)
            def _():
                new_sts, o = _hgrn2_fast_core([st[h] for h in range(A_HEADS)], q_ref[r, :], f, logf,
                                              i_ref[r, :], g_ref[r, :], ng_ref[...])
                for h in range(A_HEADS):
                    st[h] = new_sts[h]
                o_ref[r, :] = o.astype(BF16)

            @pl.when(jnp.logical_not(mild))
            def _():
                for h in range(A_HEADS):
                    lanes = slice(h * A_DK, (h + 1) * A_DK)
                    new_st, outs = _hgrn2_chunk(
                        st[h], _sub_blocks(q_ref, h, j), _sub_blocks(f_ref, h, j), _sub_blocks(i_ref, h, j),
                        _sub_blocks(g_ref, h, j), lb_ref[0:1, lanes], lb_ref[1:2, lanes], lb_ref[2:3, lanes],
                        ng_ref[:, lanes])
                    st[h] = new_st
                    for i, o in enumerate(outs):
                        o_ref[_sub_rows(j, i), lanes] = o.astype(BF16)

            return carry

        lax.fori_loop(0, A_STEP_CHUNKS, chunk, 0)

    def part(k):
        return pl.BlockSpec((rows, A_WIDTH), lambda b, n: (b * n_steps + n, k))

    return _call(
        body, name=name, grid=(batch, n_steps),
        in_specs=[part(0), part(1), part(2), part(3),
                  pl.BlockSpec((3, A_WIDTH), lambda b, n: (0, 0)), pl.BlockSpec((1, A_WIDTH), lambda b, n: (0, 0))],
        out_specs=[part(0),
                   pl.BlockSpec((A_STEP_CHUNKS, A_HEADS, A_DK, A_DK), lambda b, n: (b * n_steps + n, 0, 0, 0)),
                   pl.BlockSpec((A_STEP_CHUNKS, 1, A_WIDTH), lambda b, n: (b * n_steps + n, 0, 0))],
        out_shape=[jax.ShapeDtypeStruct((t, A_WIDTH), BF16),
                   jax.ShapeDtypeStruct((t // A_CHUNK, A_HEADS, A_DK, A_DK), F32),
                   jax.ShapeDtypeStruct((t // A_CHUNK, 1, A_WIDTH), F32)],
        scratch_shapes=[pltpu.VMEM((A_HEADS, A_DK, A_DK), F32)],
        args=(proj, proj, proj, proj, lb_table, a_norm), exchange=exchange)


def hgrn2_bwd(proj, states, decays, lb_table, a_norm, do, batch, name, exchange=None):
    t = proj.shape[0]
    n_steps = t // batch // (A_CHUNK * A_STEP_CHUNKS)
    rows = A_CHUNK * A_STEP_CHUNKS

    def body(q_ref, f_ref, i_ref, g_ref, st_ref, dec_ref, lb_ref, ng_ref, do_ref, dp_ref, dlb_ref, dng_ref, dst):
        @pl.when(jnp.logical_and(pl.program_id(0) == 0, pl.program_id(1) == 0))
        def _():
            dlb_ref[...] = jnp.zeros_like(dlb_ref)
            dng_ref[...] = jnp.zeros_like(dng_ref)

        @pl.when(pl.program_id(1) == 0)
        def _():
            dst[...] = jnp.zeros_like(dst)

        def chunk(jj, carry):
            j = A_STEP_CHUNKS - 1 - jj
            r = _chunk_rows(j)
            mild = jnp.min(dec_ref[j]) >= -A_MAX_LOG_DECAY

            @pl.when(mild)
            def _():
                _, vjp = jax.vjp(
                    _hgrn2_chunk_fast, [st_ref[j, h] for h in range(A_HEADS)], q_ref[r, :], f_ref[r, :],
                    i_ref[r, :], g_ref[r, :], lb_ref[0:1, :], lb_ref[1:2, :], lb_ref[2:3, :], ng_ref[...])
                d_sts, dq, df, di, dg, dl0, dl1, dl2, dng = vjp(
                    ([dst[h] for h in range(A_HEADS)], do_ref[r, :].astype(F32)))
                for h in range(A_HEADS):
                    dst[h] = d_sts[h]
                for k, part in enumerate((dq, df, di, dg)):
                    dp_ref[r, k * A_WIDTH:(k + 1) * A_WIDTH] = part
                for row, val in enumerate((dl0, dl1, dl2)):
                    dlb_ref[row:row + 1, :] += val
                dng_ref[...] += dng

            @pl.when(jnp.logical_not(mild))
            def _():
                for h in range(A_HEADS):
                    lanes = slice(h * A_DK, (h + 1) * A_DK)
                    _, vjp = jax.vjp(
                        _hgrn2_chunk, st_ref[j, h], _sub_blocks(q_ref, h, j), _sub_blocks(f_ref, h, j),
                        _sub_blocks(i_ref, h, j), _sub_blocks(g_ref, h, j), lb_ref[0:1, lanes], lb_ref[1:2, lanes],
                        lb_ref[2:3, lanes], ng_ref[:, lanes])
                    douts = [x.astype(F32) for x in _sub_blocks(do_ref, h, j)]
                    d_st, dqs, dfs, dis, dgs, dl0, dl1, dl2, dng = vjp((dst[h], douts))
                    dst[h] = d_st
                    for k, parts in enumerate((dqs, dfs, dis, dgs)):
                        for i in range(A_CHUNK // A_SUB):
                            dp_ref[_sub_rows(j, i), k * A_WIDTH + h * A_DK:k * A_WIDTH + (h + 1) * A_DK] = parts[i]
                    for row, val in enumerate((dl0, dl1, dl2)):
                        dlb_ref[row:row + 1, lanes] += val
                    dng_ref[:, lanes] += dng

            return carry

        lax.fori_loop(0, A_STEP_CHUNKS, chunk, 0)

    def rev(b, n):
        return b * n_steps + (n_steps - 1 - n)

    def part(k):
        return pl.BlockSpec((rows, A_WIDTH), lambda b, n: (rev(b, n), k))

    const3 = pl.BlockSpec((3, A_WIDTH), lambda b, n: (0, 0))
    const1 = pl.BlockSpec((1, A_WIDTH), lambda b, n: (0, 0))
    return _call(
        body, name=name, grid=(batch, n_steps),
        in_specs=[part(0), part(1), part(2), part(3),
                  pl.BlockSpec((A_STEP_CHUNKS, A_HEADS, A_DK, A_DK), lambda b, n: (rev(b, n), 0, 0, 0)),
                  pl.BlockSpec((A_STEP_CHUNKS, 1, A_WIDTH), lambda b, n: (rev(b, n), 0, 0)),
                  const3, const1, part(0)],
        out_specs=[pl.BlockSpec((rows, 4 * A_WIDTH), lambda b, n: (rev(b, n), 0)), const3, const1],
        out_shape=[jax.ShapeDtypeStruct((t, 4 * A_WIDTH + 2 * B_WIDTH), F32),
                   jax.ShapeDtypeStruct((3, A_WIDTH), F32), jax.ShapeDtypeStruct((1, A_WIDTH), F32)],
        scratch_shapes=[pltpu.VMEM((A_HEADS, A_DK, A_DK), F32)],
        args=(proj, proj, proj, proj, states, decays, lb_table, a_norm, do), exchange=exchange)


B_GDIM = B_WIDTH // B_GROUPS
B_ROWS = 512


def _gmlp_chunk(ubs, vbs, lngs, lnbs, ws, bcols):
    vs = [jax.nn.gelu(v) for v in vbs]
    mu = sum(jnp.sum(v, axis=-1, keepdims=True) for v in vs) * (1.0 / B_WIDTH)
    var = sum(jnp.sum(jnp.square(v - mu), axis=-1, keepdims=True) for v in vs) * (1.0 / B_WIDTH)
    rstd = lax.rsqrt(var + EPS)
    tril = (lax.broadcasted_iota(jnp.int32, (B_CHUNK, B_CHUNK), 0)
            >= lax.broadcasted_iota(jnp.int32, (B_CHUNK, B_CHUNK), 1))
    outs = []
    for g in range(B_GROUPS):
        vn = (vs[g] - mu) * rstd * lngs[g] + lnbs[g]
        w = jnp.where(tril, ws[g], 0.0).astype(BF16)
        outs.append(jax.nn.gelu(ubs[g]) * (_dot(w, vn.astype(BF16)) + bcols[g]))
    return outs


def _gmlp_args(u_ref, v_ref, lng_ref, lnb_ref, w_ref, bt_ref, rows):
    def groups(ref):
        return [ref[rows, g * B_GDIM:(g + 1) * B_GDIM] for g in range(B_GROUPS)]

    def vec(ref):
        return [ref[:, g * B_GDIM:(g + 1) * B_GDIM] for g in range(B_GROUPS)]

    return (groups(u_ref), groups(v_ref), vec(lng_ref), vec(lnb_ref),
            [w_ref[g] for g in range(B_GROUPS)], [bt_ref[:, g:g + 1] for g in range(B_GROUPS)])


def gmlp_fwd(proj, oa, ln_g, ln_b, w, bias_t, name, exchange=None):
    t = proj.shape[0]

    def body(u_ref, v_ref, oa_ref, lng_ref, lnb_ref, w_ref, bt_ref, o_ref):
        o_ref[:, 0:A_WIDTH] = oa_ref[...]
        for n in range(B_ROWS // B_CHUNK):
            rows = slice(n * B_CHUNK, (n + 1) * B_CHUNK)
            outs = _gmlp_chunk(*_gmlp_args(u_ref, v_ref, lng_ref, lnb_ref, w_ref, bt_ref, rows))
            for g, o in enumerate(outs):
                o_ref[rows, A_WIDTH + g * B_GDIM:A_WIDTH + (g + 1) * B_GDIM] = o.astype(BF16)

    vec = pl.BlockSpec((1, B_WIDTH), lambda i: (0, 0))
    return _call(
        body, name=name, grid=(t // B_ROWS,),
        in_specs=[pl.BlockSpec((B_ROWS, B_WIDTH), lambda i: (i, 4)), pl.BlockSpec((B_ROWS, B_WIDTH), lambda i: (i, 5)),
                  pl.BlockSpec((B_ROWS, A_WIDTH), lambda i: (i, 0)), vec, vec,
                  pl.BlockSpec((B_GROUPS, B_CHUNK, B_CHUNK), lambda i: (0, 0, 0)),
                  pl.BlockSpec((B_CHUNK, B_GROUPS), lambda i: (0, 0))],
        out_specs=[pl.BlockSpec((B_ROWS, A_WIDTH + B_WIDTH), lambda i: (i, 0))],
        out_shape=[jax.ShapeDtypeStruct((t, A_WIDTH + B_WIDTH), BF16)],
        args=(proj, proj, oa, ln_g, ln_b, w, bias_t), exchange=exchange)


def gmlp_bwd(proj, dmixin, ln_g, ln_b, w, bias_t, dproj, name, exchange=None):
    t = proj.shape[0]

    def body(u_ref, v_ref, do_ref, lng_ref, lnb_ref, w_ref, bt_ref, dp_in_ref,
             dp_ref, dlng_ref, dlnb_ref, dw_ref, dbt_ref):
        del dp_in_ref

        @pl.when(pl.program_id(0) == 0)
        def _():
            for ref in (dlng_ref, dlnb_ref, dw_ref, dbt_ref):
                ref[...] = jnp.zeros_like(ref)

        for n in range(B_ROWS // B_CHUNK):
            rows = slice(n * B_CHUNK, (n + 1) * B_CHUNK)
            _, vjp = jax.vjp(_gmlp_chunk, *_gmlp_args(u_ref, v_ref, lng_ref, lnb_ref, w_ref, bt_ref, rows))
            douts = [do_ref[rows, g * B_GDIM:(g + 1) * B_GDIM] for g in range(B_GROUPS)]
            dus, dvs, dlngs, dlnbs, dws, dbs = vjp(douts)
            for g in range(B_GROUPS):
                lanes = slice(g * B_GDIM, (g + 1) * B_GDIM)
                dp_ref[rows, lanes] = dus[g]
                dp_ref[rows, B_WIDTH + g * B_GDIM:B_WIDTH + (g + 1) * B_GDIM] = dvs[g]
                dlng_ref[:, lanes] += dlngs[g]
                dlnb_ref[:, lanes] += dlnbs[g]
                dw_ref[g] += dws[g]
                dbt_ref[:, g:g + 1] += dbs[g]

    vec = pl.BlockSpec((1, B_WIDTH), lambda i: (0, 0))
    wspec = pl.BlockSpec((B_GROUPS, B_CHUNK, B_CHUNK), lambda i: (0, 0, 0))
    bspec = pl.BlockSpec((B_CHUNK, B_GROUPS), lambda i: (0, 0))
    return _call(
        body, name=name, grid=(t // B_ROWS,),
        in_specs=[pl.BlockSpec((B_ROWS, B_WIDTH), lambda i: (i, 4)), pl.BlockSpec((B_ROWS, B_WIDTH), lambda i: (i, 5)),
                  pl.BlockSpec((B_ROWS, B_WIDTH), lambda i: (i, 1)), vec, vec, wspec, bspec,
                  pl.BlockSpec(memory_space=pl.ANY)],
        out_specs=[pl.BlockSpec((B_ROWS, 2 * B_WIDTH), lambda i: (i, 2)), vec, vec, wspec, bspec],
        out_shape=[jax.ShapeDtypeStruct(dproj.shape, F32), jax.ShapeDtypeStruct((1, B_WIDTH), F32),
                   jax.ShapeDtypeStruct((1, B_WIDTH), F32), jax.ShapeDtypeStruct((B_GROUPS, B_CHUNK, B_CHUNK), F32),
                   jax.ShapeDtypeStruct((B_CHUNK, B_GROUPS), F32)],
        aliases={7: 0}, args=(proj, proj, dmixin, ln_g, ln_b, w, bias_t, dproj), exchange=exchange)


C_FWD_BLOCKS = 8
C_BWD_BLOCKS = 8
C_PAIR = 2 * C_HEAD_DIM
C_PAIRS = C_HEADS // 2
C_SCALE = 1.0 / math.sqrt(C_HEAD_DIM)
C_ROT_DIM = 2 * C_ROT_HALF
ROPE_ROWS = 1024


def rope_tables(pos_col, name, exchange=None):
    t = pos_col.shape[0]

    def body(p_ref, c_ref, a_ref, b_ref):
        lane = jnp.bitwise_and(lax.broadcasted_iota(jnp.int32, (1, C_PAIR), 1), C_HEAD_DIM - 1)
        j = jnp.bitwise_and(lane, C_ROT_HALF - 1).astype(F32)
        inv = jnp.exp(j * (-math.log(ROPE_THETA) / C_ROT_HALF))
        ang = p_ref[...].astype(F32) * inv
        cos, sin = jnp.cos(ang), jnp.sin(ang)
        c_ref[...] = jnp.where(lane < C_ROT_DIM, cos, 1.0)
        a_ref[...] = jnp.where(lane < C_ROT_HALF, -sin, 0.0)
        b_ref[...] = jnp.where(jnp.logical_and(lane >= C_ROT_HALF, lane < C_ROT_DIM), sin, 0.0)

    tab = pl.BlockSpec((ROPE_ROWS, C_PAIR), lambda i: (i, 0))
    return _call(
        body, name=name, grid=(t // ROPE_ROWS,),
        in_specs=[pl.BlockSpec((ROPE_ROWS, 1), lambda i: (i, 0))],
        out_specs=[tab, tab, tab],
        out_shape=[jax.ShapeDtypeStruct((t, C_PAIR), F32)] * 3,
        args=(pos_col,), exchange=exchange)


def _rope(x, c, a, b):
    return x * c + pltpu.roll(x, C_PAIR - C_ROT_HALF, 1) * a + pltpu.roll(x, C_ROT_HALF, 1) * b


def _rope_t(d, c, a, b):
    return d * c + pltpu.roll(d * a, C_ROT_HALF, 1) + pltpu.roll(d * b, C_PAIR - C_ROT_HALF, 1)


C_RES = 16


def _residue_major(a, batch):
    return a.reshape(batch, SEQ // C_RES, C_RES, -1).transpose(0, 2, 1, 3).reshape(a.shape)


def _sequence_order(a, batch):
    return a.reshape(batch, C_RES, SEQ // C_RES, -1).transpose(0, 2, 1, 3).reshape(a.shape)


def _block_pieces(idx, dil):
    nblk = SEQ // dil // C_BLOCK
    r, n = idx // nblk, idx % nblk
    per = C_RES // dil
    size = C_BLOCK // per

    def pieces(blk):
        return [((dil * a + r) * (SEQ // C_RES) + size * blk, size) for a in range(per)]

    return pieces(n), pieces(jnp.maximum(n - 1, 0)), n > 0


def _get_rows(ref, pieces):
    return jnp.concatenate([ref[pl.ds(pl.multiple_of(start, 8), size), :] for start, size in pieces], axis=0)


def _set_rows(ref, pieces, val, add=False):
    for k, (start, size) in enumerate(pieces):
        rows = pl.ds(pl.multiple_of(start, 8), size)
        part = val[k * size:(k + 1) * size]
        ref[rows, :] = ref[rows, :] + part if add else part


def _head_masks():
    low = lax.broadcasted_iota(jnp.int32, (1, C_PAIR), 1) < C_HEAD_DIM
    return low, jnp.logical_not(low)


def _attn_mask(has_prev, dil):
    per = C_RES // dil
    size = C_BLOCK // per

    def position(x):
        x = jnp.bitwise_and(x, C_BLOCK - 1)
        return per * jnp.bitwise_and(x, size - 1) + x // size

    j = lax.broadcasted_iota(jnp.int32, (2 * C_BLOCK, 2 * C_BLOCK), 1)
    pi = position(lax.broadcasted_iota(jnp.int32, (2 * C_BLOCK, 2 * C_BLOCK), 0))
    pj = position(j)
    own = j < C_BLOCK
    return jnp.logical_or(jnp.logical_and(own, pj <= pi),
                          jnp.logical_and(jnp.logical_and(jnp.logical_not(own), pj >= pi), has_prev))


def _stack_heads(x):
    low, high = _head_masks()
    return jnp.concatenate([jnp.where(low, x, 0.0), jnp.where(high, x, 0.0)], axis=0)


def _unstack_heads(x):
    low, _ = _head_masks()
    return jnp.where(low, x[:C_BLOCK], x[C_BLOCK:])


def attn_fwd(qkv, cos_t, sin_a, sin_b, batch, name, exchange=None):
    t = qkv.shape[0]
    nbr = len(C_DILATIONS)

    def body(q_ref, k_ref, v_ref, c_ref, a_ref, b_ref, o_ref, l_ref, qr_ref, kr_ref, qs, ks, *stats):
        acc, mm, dd = stats[0:nbr], stats[nbr:2 * nbr], stats[2 * nbr:3 * nbr]
        c, a, b = c_ref[...], a_ref[...], b_ref[...]
        qs[...] = _rope(q_ref[...], c, a, b) * C_SCALE
        ks[...] = _rope(k_ref[...], c, a, b)
        qr_ref[...] = qs[...].astype(BF16)
        kr_ref[...] = ks[...].astype(BF16)

        def load(idx, dil):
            own, prev, has_prev = _block_pieces(idx, dil)
            return own, (has_prev, _get_rows(qs, own), _get_rows(ks, own), _get_rows(ks, prev),
                         _get_rows(v_ref, own), _get_rows(v_ref, prev))

        def scores(dil, has_prev, q, k_own, k_prev, v_own, v_prev):
            k_cat = jnp.concatenate([k_own, k_prev], axis=0).astype(BF16)
            return jnp.where(_attn_mask(has_prev, dil), _dot_nt(_stack_heads(q).astype(BF16), k_cat), NEG_BIG)

        def softmax(s):
            m = jnp.max(s, axis=-1, keepdims=True)
            p = jnp.exp(s - m)
            return p.astype(BF16), m, jnp.sum(p, axis=-1, keepdims=True)

        def values(pb, has_prev, q, k_own, k_prev, v_own, v_prev):
            low, high = _head_masks()
            v_cat = jnp.concatenate([v_own, v_prev], axis=0)
            p_wide = jnp.concatenate([pb[:C_BLOCK], pb[C_BLOCK:]], axis=1)
            v_tall = jnp.concatenate([jnp.where(low, v_cat, 0.0), jnp.where(high, v_cat, 0.0)], axis=0).astype(BF16)
            return _dot(p_wide, v_tall)

        for bi, dil in enumerate(C_DILATIONS):
            def pair(i, carry, bi=bi, dil=dil):
                low, _ = _head_masks()
                loaded = [load(C_FWD_BLOCKS * i + k, dil) for k in range(C_FWD_BLOCKS)]
                ss = [scores(dil, *ops) for _, ops in loaded]
                sm = [softmax(s) for s in ss]
                pvs = [values(pb, *ops) for (pb, _, _), (_, ops) in zip(sm, loaded)]
                for (own, _), (_, m, den), pv in zip(loaded, sm, pvs):
                    _set_rows(acc[bi], own, pv)
                    _set_rows(mm[bi], own, jnp.where(low, m[:C_BLOCK], m[C_BLOCK:]))
                    _set_rows(dd[bi], own, jnp.where(low, den[:C_BLOCK], den[C_BLOCK:]))
                return carry

            lax.fori_loop(0, SEQ // C_BLOCK // C_FWD_BLOCKS, pair, 0)
        step = 2 * C_BLOCK
        for r0 in range(0, SEQ, step):
            rr = slice(r0, r0 + step)
            ms = [mm[g][rr, :] for g in range(nbr)]
            m_all = functools.reduce(jnp.maximum, ms)
            ws = [jnp.exp(m - m_all) for m in ms]
            num = sum(acc[g][rr, :] * ws[g] for g in range(nbr))
            den = sum(dd[g][rr, :] * ws[g] for g in range(nbr))
            o_ref[rr, :] = (num / den).astype(BF16)
            l_ref[rr, :] = m_all + jnp.log(den)

    def col(k):
        return pl.BlockSpec((SEQ, C_PAIR), lambda b, p: (b, k * C_PAIRS + p))

    tab = pl.BlockSpec((SEQ, C_PAIR), lambda b, p: (b, 0))
    return _call(
        body, name=name, grid=(batch, C_PAIRS),
        in_specs=[col(0), col(1), col(2), tab, tab, tab],
        out_specs=[col(0), col(0), col(0), col(0)],
        out_shape=[jax.ShapeDtypeStruct((t, D_MODEL), BF16), jax.ShapeDtypeStruct((t, D_MODEL), F32),
                   jax.ShapeDtypeStruct((t, D_MODEL), BF16), jax.ShapeDtypeStruct((t, D_MODEL), BF16)],
        scratch_shapes=[pltpu.VMEM((SEQ, C_PAIR), F32)] * (2 + 3 * nbr),
        args=(qkv, qkv, qkv, cos_t, sin_a, sin_b), exchange=exchange)


def attn_bwd(qr, kr, qkv, cos_t, sin_a, sin_b, o, lse, do, batch, name, exchange=None):
    t = qkv.shape[0]

    def body(q_ref, k_ref, v_ref, c_ref, a_ref, b_ref, o_ref, l_ref, do_ref, dqkv_ref, qs, ks, dqs, dks, dvs, dlt):
        low, _ = _head_masks()
        c, a, b = c_ref[...], a_ref[...], b_ref[...]
        qs[...] = q_ref[...].astype(F32)
        ks[...] = k_ref[...].astype(F32)
        prod = do_ref[...] * o_ref[...].astype(F32)
        s_low = jnp.sum(jnp.where(low, prod, 0.0), axis=-1, keepdims=True)
        s_all = jnp.sum(prod, axis=-1, keepdims=True)
        dlt[...] = jnp.where(low, s_low, s_all - s_low)
        dqs[...] = jnp.zeros_like(dqs)
        dks[...] = jnp.zeros_like(dks)
        dvs[...] = jnp.zeros_like(dvs)

        def load(idx, dil):
            own, prev, has_prev = _block_pieces(idx, dil)
            return (own, prev), (has_prev, _get_rows(qs, own), _get_rows(do_ref, own), _get_rows(ks, own),
                                 _get_rows(ks, prev), _get_rows(v_ref, own), _get_rows(v_ref, prev),
                                 _get_rows(l_ref, own), _get_rows(dlt, own))

        def operands(dil, has_prev, q, do, k_own, k_prev, v_own, v_prev, l_full, d_full):
            lcol = jnp.concatenate([l_full[:, 0:1], l_full[:, C_HEAD_DIM:C_HEAD_DIM + 1]], axis=0)
            dcol = jnp.concatenate([d_full[:, 0:1], d_full[:, C_HEAD_DIM:C_HEAD_DIM + 1]], axis=0)
            return (_stack_heads(q).astype(BF16), _stack_heads(do).astype(BF16),
                    jnp.concatenate([k_own, k_prev], axis=0).astype(BF16),
                    jnp.concatenate([v_own, v_prev], axis=0).astype(BF16), lcol, dcol, _attn_mask(has_prev, dil))

        for dil in C_DILATIONS:
            def pair(i, carry, dil=dil):
                loaded = [load(C_BWD_BLOCKS * i + k, dil) for k in range(C_BWD_BLOCKS)]
                ops = [operands(dil, *o) for _, o in loaded]
                ss = [_dot_nt(q_stack, k_cat) for q_stack, _, k_cat, _, _, _, _ in ops]
                dps = [_dot_nt(do_stack, v_cat) for _, do_stack, _, v_cat, _, _, _ in ops]
                ps = [jnp.exp(jnp.where(o[6], s, NEG_BIG) - o[4]) for s, o in zip(ss, ops)]
                dss = [(p * (dp - o[5])).astype(BF16) for p, dp, o in zip(ps, dps, ops)]
                dvs_ = [_dot_tn(p.astype(BF16), o[1]) for p, o in zip(ps, ops)]
                dks_ = [_dot_tn(ds, o[0]) for ds, o in zip(dss, ops)]
                dqs_ = [_unstack_heads(_dot(ds, o[2])) for ds, o in zip(dss, ops)]
                for ((own, prev), _), dq, dk_cat, dv_cat in zip(loaded, dqs_, dks_, dvs_):
                    _set_rows(dqs, own, dq, add=True)
                    _set_rows(dks, own, dk_cat[:C_BLOCK], add=True)
                    _set_rows(dvs, own, dv_cat[:C_BLOCK], add=True)
                    _set_rows(dks, prev, dk_cat[C_BLOCK:], add=True)
                    _set_rows(dvs, prev, dv_cat[C_BLOCK:], add=True)
                return carry

            lax.fori_loop(0, SEQ // C_BLOCK // C_BWD_BLOCKS, pair, 0)
        dqkv_ref[0] = _rope_t(dqs[...] * C_SCALE, c, a, b).astype(BF16)
        dqkv_ref[1] = _rope_t(dks[...], c, a, b).astype(BF16)
        dqkv_ref[2] = dvs[...].astype(BF16)

    def col(k):
        return pl.BlockSpec((SEQ, C_PAIR), lambda b, p: (b, k * C_PAIRS + p))

    tab = pl.BlockSpec((SEQ, C_PAIR), lambda b, p: (b, 0))
    return _call(
        body, name=name, grid=(batch, C_PAIRS),
        in_specs=[col(0), col(0), col(2), tab, tab, tab, col(0), col(0), col(0)],
        out_specs=[pl.BlockSpec((3, SEQ, C_PAIR), lambda b, p: (0, b, p))],
        out_shape=[jax.ShapeDtypeStruct((3, t, D_MODEL), BF16)],
        scratch_shapes=[pltpu.VMEM((SEQ, C_PAIR), F32)] * 6,
        args=(qr, kr, qkv, cos_t, sin_a, sin_b, o, lse, do), exchange=exchange)


def allreduce_small(slab, name):
    rows, lanes = slab.shape

    def body(x_ref, out_ref, gath, send_sems, recv_sems, local_sem):
        x, y, c, chips = _place()
        me, sibling = (x, y, c), (x, y, 1 - c)

        def slot(px, py, pc):
            return gath.at[4 * px + 2 * py + pc]

        def copy(k, block, to, src=None):
            return pltpu.make_async_remote_copy(
                src_ref=slot(*block) if src is None else src, dst_ref=slot(*block),
                send_sem=send_sems.at[k], recv_sem=recv_sems.at[k], device_id=to, device_id_type=MESH)

        mine = pltpu.make_async_copy(x_ref, slot(*me), local_sem)
        mine.start()
        first = [copy(0, me, sibling, src=x_ref)]
        first += [copy(1 + j, me, (*chip, c), src=x_ref) for j, chip in enumerate(chips)]
        for cp in first:
            cp.start()
        passed = [copy(4 + j, (*chip, c), sibling) for j, chip in enumerate(chips)]
        for j, chip in enumerate(chips):
            copy(1 + j, (*chip, c), me).wait_recv()
            passed[j].start()
        copy(0, sibling, me).wait_recv()
        for j, chip in enumerate(chips):
            copy(4 + j, (*chip, 1 - c), me).wait_recv()
        for cp in first + passed:
            cp.wait_send()
        mine.wait()
        total = gath[0]
        for d in range(1, N_DEV):
            total = total + gath[d]
        out_ref[...] = total

    return pl.pallas_call(
        body, name=name,
        in_specs=[pl.BlockSpec(memory_space=pltpu.VMEM)],
        out_specs=pl.BlockSpec(memory_space=pltpu.VMEM),
        out_shape=jax.ShapeDtypeStruct((rows, lanes), F32),
        scratch_shapes=[pltpu.VMEM((N_DEV, rows, lanes), F32),
                        pltpu.SemaphoreType.DMA((7,)), pltpu.SemaphoreType.DMA((7,)), pltpu.SemaphoreType.DMA],
    )(slab)


ELT_ROWS = 512


def reduce_slabs(r, name):
    r = r.reshape(N_CHIPS, -1, r.shape[-1])
    _, rows, cols = r.shape
    br = min(rows, ELT_ROWS)

    def body(r_ref, o_ref):
        o_ref[...] = ((r_ref[3].astype(F32) + r_ref[0].astype(F32)) + r_ref[1].astype(F32)) + r_ref[2].astype(F32)

    return pl.pallas_call(
        body, name=name, grid=(rows // br,),
        in_specs=[pl.BlockSpec((N_CHIPS, br, cols), lambda i: (0, i, 0))],
        out_specs=pl.BlockSpec((br, cols), lambda i: (i, 0)),
        out_shape=jax.ShapeDtypeStruct((rows, cols), F32),
        compiler_params=_params(("arbitrary",)),
    )(r)


def _adamw(w, g, m, v):
    m = ADAM_B1 * m + (1.0 - ADAM_B1) * g
    v = ADAM_B2 * v + (1.0 - ADAM_B2) * jnp.square(g)
    m_hat = m / (1.0 - ADAM_B1 ** ADAM_STEP)
    v_hat = v / (1.0 - ADAM_B2 ** ADAM_STEP)
    delta = -ADAM_LR * (m_hat / (jnp.sqrt(v_hat) + ADAM_EPS) + ADAM_WD * w)
    return delta, m, v


def adamw_big(w, s_mine, s_sibling, m, v, name):
    rows, cols = w.shape
    parts = len(s_mine)
    br = min(rows // parts, ELT_ROWS)
    nb = rows // parts // br

    def body(w_ref, m_ref, v_ref, *rest):
        sums, (g_out, d_out, m_out, v_out) = rest[:2 * parts], rest[2 * parts:]
        p = pl.program_id(0)
        g = sums[0][...] + sums[parts][...]
        for k in range(1, parts):
            g = jnp.where(p == k, sums[k][...] + sums[parts + k][...], g)
        g_out[...] = g
        d_out[...], m_out[...], v_out[...] = _adamw(w_ref[...], g, m_ref[...], v_ref[...])

    def part_spec(k):
        return pl.BlockSpec((br, cols), lambda p, i: (jnp.where(p == k, i, jnp.where(p < k, 0, nb - 1)), 0))

    blk = pl.BlockSpec((br, cols), lambda p, i: (p * nb + i, 0))
    out = jax.ShapeDtypeStruct((rows, cols), F32)
    return pl.pallas_call(
        body, name=name, grid=(parts, nb),
        in_specs=[blk] * 3 + [part_spec(k) for k in range(parts)] * 2, out_specs=[blk] * 4, out_shape=[out] * 4,
        compiler_params=_params(("arbitrary", "arbitrary")),
    )(w, m, v, *s_mine, *s_sibling)


def adamw_small(ws, gs, ms, vs, name):
    n = len(ws)

    def body(*refs):
        w_refs, g_refs, m_refs, v_refs = (refs[k * n:(k + 1) * n] for k in range(4))
        d_out, m_out, v_out = (refs[(4 + k) * n:(5 + k) * n] for k in range(3))
        for i in range(n):
            d_out[i][...], m_out[i][...], v_out[i][...] = _adamw(
                w_refs[i][...], g_refs[i][...], m_refs[i][...], v_refs[i][...])

    outs = [jax.ShapeDtypeStruct(w.shape, F32) for w in ws]
    res = pl.pallas_call(body, name=name, out_shape=outs * 3)(*ws, *gs, *ms, *vs)
    return res[:n], res[n:2 * n], res[2 * n:]


SLAB_LANES = 128
SLAB_ROW_ALIGN = 8


def _pack(parts):
    flat = jnp.concatenate([p.reshape(-1) for p in parts])
    rows = -(-flat.shape[0] // (SLAB_LANES * SLAB_ROW_ALIGN)) * SLAB_ROW_ALIGN
    flat = jnp.pad(flat, (0, rows * SLAB_LANES - flat.shape[0]))
    return flat.reshape(rows, SLAB_LANES)


def _unpack(slab, shapes):
    flat = slab.reshape(-1)
    out, pos = [], 0
    for s in shapes:
        size = math.prod(s)
        out.append(flat[pos:pos + size].reshape(s))
        pos += size
    return out


def kernel(x, positions, norm_mix_pre, norm_mix_post, norm_ffn_pre, norm_ffn_post, w_in_even, lb_table, a_norm, b_ln_g, b_ln_b, b_ws, b_bias, w_out_even, w_in_odd, w_out_odd, w_ff1, w_ff2, loss_target, m_norm_mix_pre, m_norm_mix_post, m_norm_ffn_pre, m_norm_ffn_post, m_w_in_even, m_lb_table, m_a_norm, m_b_ln_g, m_b_ln_b, m_b_ws, m_b_bias, m_w_out_even, m_w_in_odd, m_w_out_odd, m_w_ff1, m_w_ff2, v_norm_mix_pre, v_norm_mix_post, v_norm_ffn_pre, v_norm_ffn_post, v_w_in_even, v_lb_table, v_a_norm, v_b_ln_g, v_b_ln_b, v_b_ws, v_b_bias, v_w_out_even, v_w_in_odd, v_w_out_odd, v_w_ff1, v_w_ff2):
    batch = x.shape[0]
    t = batch * SEQ
    d = D_MODEL
    x0 = x.reshape(t, d)
    target = loss_target.reshape(t, d)

    def gain(p, layer):
        return p[layer:layer + 1]

    def gather(*shards):
        return _Exchange("gather", [w.astype(BF16) for w in shards])

    def scatter(*grads):
        return _Exchange("scatter", grads)

    cos_t, sin_a, sin_b, win_e = rope_tables(_residue_major(positions.reshape(t, 1), batch), "rope_tables",
                                             exchange=gather(w_in_even[0]))
    bias_t = b_bias[0].T
    proj, h0, w1_0 = norm_matmul(x0, gain(norm_mix_pre, 0), win_e, "in_proj_even", exchange=gather(w_ff1[0]))
    oa, states, decays, w2_0 = hgrn2_fwd(proj, lb_table, a_norm, batch, "hgrn2_fwd", exchange=gather(w_ff2[0]))
    mixin, wout_e = gmlp_fwd(proj, oa, b_ln_g, b_ln_b, b_ws[0], bias_t, "gmlp_fwd", exchange=gather(w_out_even[0]))
    mix0, x1 = out_proj(mixin, wout_e, x0, gain(norm_mix_post, 0), "out_proj_even")
    x2, hf0, a0, y0, win_o, wout_o = ffn_fwd(x1, gain(norm_ffn_pre, 0), w1_0, w2_0, gain(norm_ffn_post, 0),
                                             "ffn_fwd_0", exchange=gather(w_in_odd[0], w_out_odd[0]))
    x2p = _residue_major(x2, batch)
    qkv, h1 = norm_matmul(x2p, gain(norm_mix_pre, 1), win_o, "in_proj_odd")
    ao, lse, q_rot, k_rot, w1_1, w2_1 = attn_fwd(qkv, cos_t, sin_a, sin_b, batch, "attn_fwd",
                                                 exchange=gather(w_ff1[1], w_ff2[1]))
    mix1, x3 = out_proj(ao, wout_o, x2p, gain(norm_mix_post, 1), "out_proj_odd")
    dx4, hf1, a1, y1, loss_part = ffn_fwd(x3, gain(norm_ffn_pre, 1), w1_1, w2_1, gain(norm_ffn_post, 1),
                                          "ffn_fwd_1", target=_residue_major(target, batch))

    hc = D_FF // N_CHIPS
    dx3, dy1, da1, dg_fpre1, dg_fpost1 = ffn_bwd(
        dx4, x3, y1, a1, gain(norm_ffn_pre, 1), gain(norm_ffn_post, 1), w1_1, w2_1, "ffn_bwd_1")
    g_w1_1 = weight_grad(hf1, da1, "b", d, hc, False, "wgrad_ff1_1")
    g_w2_1 = weight_grad(a1, dy1, "a", hc, d, True, "wgrad_ff2_1")
    dmix1, dao, dg_mpost1 = out_proj_bwd(dx3, mix1, gain(norm_mix_post, 1), wout_o, "out_proj_bwd_odd")
    g_wout_o = weight_grad(ao, dmix1, "a", d // N_CHIPS, d, False, "wgrad_out_odd")
    dqkv, r_w1_1, r_w2_1, r_wout_o = attn_bwd(q_rot, k_rot, qkv, cos_t, sin_a, sin_b, ao, lse, dao, batch, "attn_bwd",
                                              exchange=scatter(g_w1_1, g_w2_1, g_wout_o))
    dx2p, dg_mpre1 = norm_matmul_bwd(dqkv, win_o, x2p, gain(norm_mix_pre, 1), dx3, "in_proj_bwd_odd")
    dx2 = _sequence_order(dx2p, batch)
    g_win_o = weight_grad_stacked(h1, dqkv, 3 * d // N_CHIPS, "wgrad_in_odd")
    s_w1_1, s_w2_1, s_wout_o = (reduce_slabs(r, n) for r, n in (
        (r_w1_1, "reduce_ff1_1"), (r_w2_1, "reduce_ff2_1"), (r_wout_o, "reduce_out_odd")))
    dx1, dy0, da0, dg_fpre0, dg_fpost0, r_win_o, t_w1_1, t_w2_1, t_wout_o = ffn_bwd(
        dx2, x1, y0, a0, gain(norm_ffn_pre, 0), gain(norm_ffn_post, 0), w1_0, w2_0, "ffn_bwd_0",
        exchange=_Both(scatter(g_win_o), _Swap([s_w1_1, s_w2_1, s_wout_o])))
    g_w1_0 = weight_grad(hf0, da0, "b", d, hc, False, "wgrad_ff1_0")
    g_w2_0 = weight_grad(a0, dy0, "a", hc, d, True, "wgrad_ff2_0")
    dmix0, dmixin, dg_mpost0 = out_proj_bwd(dx1, mix0, gain(norm_mix_post, 0), wout_e, "out_proj_bwd_even")
    g_wout_e = weight_grad(mixin, dmix0, "a", d // N_CHIPS, d, False, "wgrad_out_even")
    s_win_o = reduce_slabs(r_win_o, "reduce_in_odd")
    dproj, d_lb, d_anorm, r_w1_0, t_win_o = hgrn2_bwd(
        proj, states, decays, lb_table, a_norm, dmixin, batch, "hgrn2_bwd",
        exchange=_Both(scatter(g_w1_0), _Swap([s_win_o])))
    s_w1_0 = reduce_slabs(r_w1_0, "reduce_ff1_0")
    dproj, d_lng, d_lnb, d_ws, d_bias_t, r_w2_0, t_w1_0 = gmlp_bwd(
        proj, dmixin, b_ln_g, b_ln_b, b_ws[0], bias_t, dproj, "gmlp_bwd",
        exchange=_Both(scatter(g_w2_0), _Swap([s_w1_0])))
    s_w2_0 = reduce_slabs(r_w2_0, "reduce_ff2_0")
    g_win_e, r_wout_e, t_w2_0 = weight_grad(h0, dproj, "b", d, 3 * d // N_CHIPS, False, "wgrad_in_even",
                                            exchange=_Both(scatter(g_wout_e), _Swap([s_w2_0])))
    s_wout_e = reduce_slabs(r_wout_e, "reduce_out_even")
    dx0, dg_mpre0, r_win_e, t_wout_e = norm_matmul_bwd(
        dproj, win_e, x0, gain(norm_mix_pre, 0), dx1, "in_proj_bwd_even",
        exchange=_Both(scatter(g_win_e), _Swap([s_wout_e])))
    grad_x = dx0.reshape(x.shape)
    s_win_e = reduce_slabs(r_win_e, "reduce_in_even")
    (t_win_e,) = exchange_alone(_Swap([s_win_e]), "sibling_swap")

    big_w = [w_in_even, w_out_even, w_in_odd, w_out_odd, w_ff1, w_ff2]
    big_m = [m_w_in_even, m_w_out_even, m_w_in_odd, m_w_out_odd, m_w_ff1, m_w_ff2]
    big_v = [v_w_in_even, v_w_out_even, v_w_in_odd, v_w_out_odd, v_w_ff1, v_w_ff2]
    mine = [[s_win_e], [s_wout_e], [s_win_o], [s_wout_o], [s_w1_0, s_w1_1], [s_w2_0, s_w2_1]]
    theirs = [[t_win_e], [t_wout_e], [t_win_o], [t_wout_o], [t_w1_0, t_w1_1], [t_w2_0, t_w2_1]]
    big = []
    for i, (w, m, v) in enumerate(zip(big_w, big_m, big_v)):
        two_d = (-1, w.shape[-1])
        res = adamw_big(w.reshape(two_d), mine[i], theirs[i], m.reshape(two_d), v.reshape(two_d), "adamw_big_%d" % i)
        big.append([r.reshape(w.shape) for r in res])

    small_w = [norm_mix_pre, norm_mix_post, norm_ffn_pre, norm_ffn_post, lb_table, a_norm, b_ln_g, b_ln_b, b_ws, b_bias]
    small_m = [m_norm_mix_pre, m_norm_mix_post, m_norm_ffn_pre, m_norm_ffn_post, m_lb_table, m_a_norm, m_b_ln_g,
               m_b_ln_b, m_b_ws, m_b_bias]
    small_v = [v_norm_mix_pre, v_norm_mix_post, v_norm_ffn_pre, v_norm_ffn_post, v_lb_table, v_a_norm, v_b_ln_g,
               v_b_ln_b, v_b_ws, v_b_bias]
    partial = [jnp.concatenate([dg_mpre0, dg_mpre1]), jnp.concatenate([dg_mpost0, dg_mpost1]),
               jnp.concatenate([dg_fpre0, dg_fpre1]), jnp.concatenate([dg_fpost0, dg_fpost1]),
               d_lb, d_anorm, d_lng, d_lnb, d_ws[None], d_bias_t.T[None]]
    *small_g, loss = _unpack(allreduce_small(_pack(partial + [loss_part]), "allreduce_small"),
                             [w.shape for w in small_w] + [()])
    small_d, small_nm, small_nv = adamw_small(small_w, small_g, small_m, small_v, "adamw_small")

    order = ["norm_mix_pre", "norm_mix_post", "norm_ffn_pre", "norm_ffn_post", "w_in_even", "lb_table", "a_norm",
             "b_ln_g", "b_ln_b", "b_ws", "b_bias", "w_out_even", "w_in_odd", "w_out_odd", "w_ff1", "w_ff2"]
    small_names = ["norm_mix_pre", "norm_mix_post", "norm_ffn_pre", "norm_ffn_post", "lb_table", "a_norm",
                   "b_ln_g", "b_ln_b", "b_ws", "b_bias"]
    big_names = ["w_in_even", "w_out_even", "w_in_odd", "w_out_odd", "w_ff1", "w_ff2"]
    grads, deltas, new_m, new_v = {}, {}, {}, {}
    for i, nm in enumerate(small_names):
        grads[nm], deltas[nm], new_m[nm], new_v[nm] = small_g[i], small_d[i], small_nm[i], small_nv[i]
    for i, nm in enumerate(big_names):
        grads[nm], deltas[nm], new_m[nm], new_v[nm] = big[i]
    return (loss, grad_x, *[grads[n] for n in order], *[deltas[n] for n in order],
            *[new_m[n] for n in order], *[new_v[n] for n in order])
```

```python
import functools
import math

import jax
import jax.numpy as jnp
from jax import lax
from jax.experimental import pallas as pl
from jax.experimental.pallas import tpu as pltpu

F32 = jnp.float32
BF16 = jnp.bfloat16
MESH = pl.DeviceIdType.MESH

D_MODEL = 1024
SEQ = 2048
D_FF = 4096
N_CHIPS = 4
A_WIDTH = 512
A_HEADS = 4
A_DK = 128
A_CHUNK = 64
A_SUB = 16
B_WIDTH = 512
B_GROUPS = 4
B_CHUNK = 128
C_HEADS = 16
C_HEAD_DIM = 64
C_ROT_HALF = 8
C_BLOCK = 128
C_DILATIONS = (1, 4, 16)
ROPE_THETA = 500000.0
EPS = 1e-6
ADAM_LR = 0.001
ADAM_B1 = 0.9
ADAM_B2 = 0.999
ADAM_EPS = 1e-08
ADAM_WD = 0.01
ADAM_STEP = 10

ROW_TILE = 512
FFN_ROWS = 1024
WGRAD_ROWS = 2048
VMEM_LIMIT = 56 * 1024 * 1024
NEG_BIG = -1e30


def _params(sem=None):
    return pltpu.CompilerParams(dimension_semantics=sem, vmem_limit_bytes=VMEM_LIMIT)


def _dot(a, b):
    return jnp.dot(a, b, preferred_element_type=F32)


def _dot_nt(a, b):
    return lax.dot_general(a, b, (((1,), (1,)), ((), ())), preferred_element_type=F32)


def _dot_tn(a, b):
    return lax.dot_general(a, b, (((0,), (0,)), ((), ())), preferred_element_type=F32)


def _rms(x, g):
    r = lax.rsqrt(jnp.mean(x * x, axis=-1, keepdims=True) + EPS)
    return x * r * g


def _rms_bwd(x, g, dy):
    r = lax.rsqrt(jnp.mean(x * x, axis=-1, keepdims=True) + EPS)
    xh = x * r
    dg = jnp.sum(dy * xh, axis=0, keepdims=True)
    dxh = dy * g
    dx = r * (dxh - xh * jnp.mean(dxh * xh, axis=-1, keepdims=True))
    return dx, dg


def _accumulate(ref, val, first):
    @pl.when(first)
    def _():
        ref[...] = val

    @pl.when(jnp.logical_not(first))
    def _():
        ref[...] += val


N_DEV = 8
ANY = pl.BlockSpec(memory_space=pl.ANY)


def _place():
    x, y, c = lax.axis_index("x"), lax.axis_index("y"), lax.axis_index("c")
    return x, y, c, [(1 - x, y), (x, 1 - y), (1 - x, 1 - y)]


class _Exchange:
    def __init__(self, kind, arrays):
        self.kind, self.arrays, self.n = kind, list(arrays), len(arrays)
        per_peer = pltpu.SemaphoreType.DMA((3 * self.n,))
        if kind == "gather":
            self.out_shape = [jax.ShapeDtypeStruct((N_CHIPS,) + a.shape, a.dtype) for a in self.arrays]
            self.scratch = [per_peer, per_peer, pltpu.SemaphoreType.DMA((self.n,)), per_peer, per_peer]
        else:
            self.out_shape = [jax.ShapeDtypeStruct(a.shape, a.dtype) for a in self.arrays]
            self.scratch = [per_peer, per_peer, pltpu.SemaphoreType.DMA((self.n,))]

    def _copies(self, ins, outs, sems):
        send_sems, recv_sems, local_sems = sems[:3]
        x, y, c, chips = _place()
        me = 2 * x + y
        local, remote = [], []
        for a in range(self.n):
            if self.kind == "gather":
                local.append(pltpu.make_async_copy(ins[a], outs[a].at[me], local_sems.at[a]))
                half = self.arrays[a].shape[0] // 2

                def rows(ref, core, half=half):
                    return ref.at[pl.ds(core * half, half)]
            else:
                local.append(pltpu.make_async_copy(ins[a].at[me], outs[a].at[3], local_sems.at[a]))
            for j, (px, py) in enumerate(chips):
                k = 3 * a + j
                peer = 2 * px + py

                def copy(src, dst, to, send_sem=send_sems.at[k], recv_sem=recv_sems.at[k]):
                    return pltpu.make_async_remote_copy(src_ref=src, dst_ref=dst, send_sem=send_sem, recv_sem=recv_sem,
                                                        device_id=to, device_id_type=MESH)

                if self.kind == "gather":
                    sent = copy(rows(ins[a], c), rows(outs[a].at[me], c), (px, py, c))
                    landed = copy(rows(ins[a], c), rows(outs[a].at[peer], c), (px, py, c))
                    on = dict(send_sem=sems[3].at[k], recv_sem=sems[4].at[k])
                    passed = copy(rows(outs[a].at[peer], c), rows(outs[a].at[peer], c), (x, y, 1 - c), **on)
                    handed = copy(rows(outs[a].at[peer], c), rows(outs[a].at[peer], 1 - c), (x, y, 1 - c), **on)
                    remote.append((sent, landed, passed, handed))
                else:
                    sent = copy(ins[a].at[peer], outs[a].at[j], (px, py, c))
                    remote.append((sent, sent, None, None))
        return local, remote

    def start(self, ins, outs, sems):
        local, remote = self._copies(ins, outs, sems)
        for cp in local:
            cp.start()
        for sent, _, _, _ in remote:
            sent.start()

    def finish(self, ins, outs, sems):
        local, remote = self._copies(ins, outs, sems)
        for _, landed, passed, _ in remote:
            landed.wait_recv()
            if passed is not None:
                passed.start()
        for sent, _, passed, handed in remote:
            if passed is not None:
                handed.wait_recv()
                passed.wait_send()
            sent.wait_send()
        for cp in local:
            cp.wait()


class _Swap:
    def __init__(self, arrays):
        self.arrays, self.n = list(arrays), len(arrays)
        self.out_shape = [jax.ShapeDtypeStruct(a.shape, a.dtype) for a in self.arrays]
        self.scratch = [pltpu.SemaphoreType.DMA((self.n,)), pltpu.SemaphoreType.DMA((self.n,))]

    def _copies(self, ins, outs, sems):
        x, y, c, _ = _place()
        return [pltpu.make_async_remote_copy(src_ref=ins[a], dst_ref=outs[a], send_sem=sems[0].at[a],
                                             recv_sem=sems[1].at[a], device_id=(x, y, 1 - c), device_id_type=MESH)
                for a in range(self.n)]

    def start(self, ins, outs, sems):
        for cp in self._copies(ins, outs, sems):
            cp.start()

    def finish(self, ins, outs, sems):
        for cp in self._copies(ins, outs, sems):
            cp.wait_recv()
            cp.wait_send()


class _Both:
    def __init__(self, first, second):
        self.parts = (first, second)
        self.arrays, self.n = first.arrays + second.arrays, first.n + second.n
        self.out_shape = first.out_shape + second.out_shape
        self.scratch = first.scratch + second.scratch

    def _split(self, ins, outs, sems):
        a, b = self.parts
        return ((a, ins[:a.n], outs[:a.n], sems[:len(a.scratch)]),
                (b, ins[a.n:], outs[a.n:], sems[len(a.scratch):]))

    def start(self, ins, outs, sems):
        for ex, i, o, s in self._split(ins, outs, sems):
            ex.start(i, o, s)

    def finish(self, ins, outs, sems):
        for ex, i, o, s in self._split(ins, outs, sems):
            ex.finish(i, o, s)


def _call(body, *, name, grid, in_specs, out_specs, out_shape, args, scratch_shapes=(), aliases=None, exchange=None):
    if exchange is None:
        return pl.pallas_call(
            body, name=name, grid=grid, in_specs=in_specs, out_specs=out_specs, out_shape=out_shape,
            scratch_shapes=list(scratch_shapes), input_output_aliases=aliases or {},
            compiler_params=_params(("arbitrary",) * len(grid)))(*args)
    n_in, n_out, n_scr, n_ex = len(in_specs), len(out_specs), len(scratch_shapes), exchange.n
    steps = grid

    def wrapped(*refs):
        ins, refs = refs[:n_in], refs[n_in:]
        ex_in, refs = refs[:n_ex], refs[n_ex:]
        outs, refs = refs[:n_out], refs[n_out:]
        ex_out, refs = refs[:n_ex], refs[n_ex:]
        scr, sems = refs[:n_scr], refs[n_scr:]
        first = functools.reduce(jnp.logical_and, [pl.program_id(k) == 0 for k in range(len(steps))])
        last = functools.reduce(jnp.logical_and, [pl.program_id(k) == steps[k] - 1 for k in range(len(steps))])

        @pl.when(first)
        def _():
            exchange.start(ex_in, ex_out, sems)

        body(*ins, *outs, *scr)

        @pl.when(last)
        def _():
            exchange.finish(ex_in, ex_out, sems)

    return pl.pallas_call(
        wrapped, name=name, grid=grid,
        in_specs=list(in_specs) + [ANY] * n_ex, out_specs=list(out_specs) + [ANY] * n_ex,
        out_shape=list(out_shape) + exchange.out_shape,
        scratch_shapes=list(scratch_shapes) + exchange.scratch, input_output_aliases=aliases or {},
        compiler_params=_params(("arbitrary",) * len(grid)))(*args, *exchange.arrays)


def exchange_alone(exchange, name):
    def body(*refs):
        n = exchange.n
        exchange.start(refs[:n], refs[n:2 * n], refs[2 * n:])
        exchange.finish(refs[:n], refs[n:2 * n], refs[2 * n:])

    return pl.pallas_call(
        body, name=name, in_specs=[ANY] * exchange.n, out_specs=[ANY] * exchange.n,
        out_shape=exchange.out_shape, scratch_shapes=exchange.scratch)(*exchange.arrays)


def norm_matmul(x, g, wg, name, exchange=None):
    t, d = x.shape
    nl = wg.shape[2]

    def body(x_ref, g_ref, w_ref, o_ref, h_ref):
        h = _rms(x_ref[...], g_ref[...]).astype(BF16)
        h_ref[...] = h
        for c in range(N_CHIPS):
            o_ref[:, c * nl:(c + 1) * nl] = _dot(h, w_ref[c])

    return _call(
        body, name=name, grid=(t // ROW_TILE,),
        in_specs=[pl.BlockSpec((ROW_TILE, d), lambda i: (i, 0)),
                  pl.BlockSpec((1, d), lambda i: (0, 0)),
                  pl.BlockSpec((N_CHIPS, d, nl), lambda i: (0, 0, 0))],
        out_specs=[pl.BlockSpec((ROW_TILE, N_CHIPS * nl), lambda i: (i, 0)),
                   pl.BlockSpec((ROW_TILE, d), lambda i: (i, 0))],
        out_shape=[jax.ShapeDtypeStruct((t, N_CHIPS * nl), F32), jax.ShapeDtypeStruct((t, d), BF16)],
        args=(x, g, wg), exchange=exchange)


def norm_matmul_bwd(dproj, wg, x, g, dres, name, exchange=None):
    t, d = x.shape
    nl = wg.shape[2]
    stacked = dproj.ndim == 3
    piece = math.gcd(nl, dproj.shape[-1])

    def body(dp_ref, w_ref, x_ref, g_ref, dres_ref, dx_ref, dg_ref):
        dh = None
        for j in range(N_CHIPS * nl // piece):
            c, off = divmod(j * piece, nl)
            if stacked:
                p, lo = divmod(j * piece, dproj.shape[-1])
                lhs = dp_ref[p, :, lo:lo + piece]
            else:
                lhs = dp_ref[:, j * piece:(j + 1) * piece]
            part = _dot_nt(lhs.astype(BF16), w_ref[c, :, off:off + piece])
            dh = part if dh is None else dh + part
        dx, dg = _rms_bwd(x_ref[...], g_ref[...], dh)
        dx_ref[...] = dres_ref[...] + dx
        _accumulate(dg_ref, dg, pl.program_id(0) == 0)

    row = pl.BlockSpec((ROW_TILE, d), lambda i: (i, 0))
    vec = pl.BlockSpec((1, d), lambda i: (0, 0))
    if stacked:
        dp_spec = pl.BlockSpec((dproj.shape[0], ROW_TILE, dproj.shape[-1]), lambda i: (0, i, 0))
    else:
        dp_spec = pl.BlockSpec((ROW_TILE, N_CHIPS * nl), lambda i: (i, 0))
    return _call(
        body, name=name, grid=(t // ROW_TILE,),
        in_specs=[dp_spec, pl.BlockSpec((N_CHIPS, d, nl), lambda i: (0, 0, 0)), row, vec, row],
        out_specs=[row, vec],
        out_shape=[jax.ShapeDtypeStruct((t, d), F32), jax.ShapeDtypeStruct((1, d), F32)],
        args=(dproj, wg, x, g, dres), exchange=exchange)


def out_proj(a, wg, x, g, name):
    t, d = x.shape
    kl = wg.shape[1]

    def body(a_ref, w_ref, x_ref, g_ref, mix_ref, xo_ref):
        acc = _dot(a_ref[:, 0:kl], w_ref[0])
        for c in range(1, N_CHIPS):
            acc += _dot(a_ref[:, c * kl:(c + 1) * kl], w_ref[c])
        mix_ref[...] = acc
        xo_ref[...] = x_ref[...] + _rms(acc, g_ref[...])

    row = pl.BlockSpec((ROW_TILE, d), lambda i: (i, 0))
    return pl.pallas_call(
        body, name=name, grid=(t // ROW_TILE,),
        in_specs=[row, pl.BlockSpec((N_CHIPS, kl, d), lambda i: (0, 0, 0)), row,
                  pl.BlockSpec((1, d), lambda i: (0, 0))],
        out_specs=[row, row],
        out_shape=[jax.ShapeDtypeStruct((t, d), F32), jax.ShapeDtypeStruct((t, d), F32)],
        compiler_params=_params(("arbitrary",)),
    )(a, wg, x, g)


def out_proj_bwd(dxo, mix, g, wg, name):
    t, d = mix.shape
    kl = wg.shape[1]

    def body(dxo_ref, mix_ref, g_ref, w_ref, dmix_ref, da_ref, dg_ref):
        dmix, dg = _rms_bwd(mix_ref[...], g_ref[...], dxo_ref[...])
        dmb = dmix.astype(BF16)
        dmix_ref[...] = dmb
        for c in range(N_CHIPS):
            da_ref[:, c * kl:(c + 1) * kl] = _dot_nt(dmb, w_ref[c])
        _accumulate(dg_ref, dg, pl.program_id(0) == 0)

    row = pl.BlockSpec((ROW_TILE, d), lambda i: (i, 0))
    vec = pl.BlockSpec((1, d), lambda i: (0, 0))
    return pl.pallas_call(
        body, name=name, grid=(t // ROW_TILE,),
        in_specs=[row, row, vec, pl.BlockSpec((N_CHIPS, kl, d), lambda i: (0, 0, 0))],
        out_specs=[row, row, vec],
        out_shape=[jax.ShapeDtypeStruct((t, d), BF16), jax.ShapeDtypeStruct((t, d), F32),
                   jax.ShapeDtypeStruct((1, d), F32)],
        compiler_params=_params(("arbitrary",)),
    )(dxo, mix, g, wg)


def ffn_fwd(x, gpre, w1g, w2g, gpost, name, exchange=None, target=None):
    t, d = x.shape
    hc = w1g.shape[2]
    with_loss = target is not None

    def body(x_ref, gpre_ref, w1_ref, w2_ref, gpost_ref, *rest):
        if with_loss:
            t_ref, xo_ref, h_ref, a_ref, y_ref, l_ref, acc = rest
        else:
            xo_ref, h_ref, a_ref, y_ref, acc = rest
        i, c = pl.program_id(0), pl.program_id(1)

        @pl.when(c == 0)
        def _():
            h_ref[...] = _rms(x_ref[...], gpre_ref[...]).astype(BF16)

        a = _dot(h_ref[...], w1_ref[...])
        a_ref[...] = a.astype(BF16)
        r = jnp.square(jnp.maximum(a, 0.0)).astype(BF16)
        _accumulate(acc, _dot(r, w2_ref[...]), c == 0)

        @pl.when(c == N_CHIPS - 1)
        def _():
            y = acc[...]
            y_ref[...] = y
            xo = x_ref[...] + _rms(y, gpost_ref[...])
            if with_loss:
                e = xo - t_ref[...]
                xo_ref[...] = e * (1.0 / d)
                part = jnp.sum(jnp.sum(e * e, axis=-1, keepdims=True), axis=0, keepdims=True) * (0.5 / d)
                _accumulate(l_ref, part, i == 0)
            else:
                xo_ref[...] = xo

    row = pl.BlockSpec((FFN_ROWS, d), lambda i, c: (i, 0))
    vec = pl.BlockSpec((1, d), lambda i, c: (0, 0))
    one = pl.BlockSpec((1, 1), lambda i, c: (0, 0))
    return _call(
        body, name=name, grid=(t // FFN_ROWS, N_CHIPS),
        in_specs=[row, vec,
                  pl.BlockSpec((None, d, hc), lambda i, c: (c, 0, 0)),
                  pl.BlockSpec((None, hc, d), lambda i, c: (c, 0, 0)), vec] + ([row] if with_loss else []),
        out_specs=[row, row, pl.BlockSpec((FFN_ROWS, hc), lambda i, c: (i, c)), row] + ([one] if with_loss else []),
        out_shape=[jax.ShapeDtypeStruct((t, d), F32), jax.ShapeDtypeStruct((t, d), BF16),
                   jax.ShapeDtypeStruct((t, N_CHIPS * hc), BF16), jax.ShapeDtypeStruct((t, d), F32)]
        + ([jax.ShapeDtypeStruct((1, 1), F32)] if with_loss else []),
        scratch_shapes=[pltpu.VMEM((FFN_ROWS, d), F32)],
        args=(x, gpre, w1g, w2g, gpost) + ((target,) if with_loss else ()), exchange=exchange)


def ffn_bwd(dxo, x, y, a, gpre, gpost, w1g, w2g, name, exchange=None):
    t, d = x.shape
    hc = w1g.shape[2]

    def body(dxo_ref, x_ref, y_ref, a_ref, gpre_ref, gpost_ref, w1_ref, w2_ref,
             dxi_ref, dy_ref, da_ref, dgpre_ref, dgpost_ref, acc):
        i, c = pl.program_id(0), pl.program_id(1)

        @pl.when(c == 0)
        def _():
            dy, dg = _rms_bwd(y_ref[...], gpost_ref[...], dxo_ref[...])
            dy_ref[...] = dy.astype(BF16)
            _accumulate(dgpost_ref, dg, i == 0)

        dr = _dot_nt(dy_ref[...], w2_ref[...])
        da = (dr * (2.0 * jnp.maximum(a_ref[...].astype(F32), 0.0))).astype(BF16)
        da_ref[...] = da
        _accumulate(acc, _dot_nt(da, w1_ref[...]), c == 0)

        @pl.when(c == N_CHIPS - 1)
        def _():
            dx, dg = _rms_bwd(x_ref[...], gpre_ref[...], acc[...])
            dxi_ref[...] = dxo_ref[...] + dx
            _accumulate(dgpre_ref, dg, i == 0)

    row = pl.BlockSpec((ROW_TILE, d), lambda i, c: (i, 0))
    vec = pl.BlockSpec((1, d), lambda i, c: (0, 0))
    hid = pl.BlockSpec((ROW_TILE, hc), lambda i, c: (i, c))
    return _call(
        body, name=name, grid=(t // ROW_TILE, N_CHIPS),
        in_specs=[row, row, row, hid, vec, vec,
                  pl.BlockSpec((None, d, hc), lambda i, c: (c, 0, 0)),
                  pl.BlockSpec((None, hc, d), lambda i, c: (c, 0, 0))],
        out_specs=[row, row, hid, vec, vec],
        out_shape=[jax.ShapeDtypeStruct((t, d), F32), jax.ShapeDtypeStruct((t, d), BF16),
                   jax.ShapeDtypeStruct((t, N_CHIPS * hc), BF16),
                   jax.ShapeDtypeStruct((1, d), F32), jax.ShapeDtypeStruct((1, d), F32)],
        scratch_shapes=[pltpu.VMEM((ROW_TILE, d), F32)],
        args=(dxo, x, y, a, gpre, gpost, w1g, w2g), exchange=exchange)


def weight_grad(a, b, chunked, bk, bn, relu2, name, exchange=None):
    t = a.shape[0]
    a_on = chunked == "a"
    rows = min(t, WGRAD_ROWS)
    n_steps = t // rows

    def body(a_ref, b_ref, o_ref, acc):
        s = pl.program_id(1)
        av = a_ref[...]
        if relu2:
            av = jnp.square(jnp.maximum(av.astype(F32), 0.0))
        _accumulate(acc, _dot_tn(av.astype(BF16), b_ref[...].astype(BF16)), s == 0)

        @pl.when(s == n_steps - 1)
        def _():
            o_ref[...] = acc[...].astype(BF16)

    res = _call(
        body, name=name, grid=(N_CHIPS, n_steps),
        in_specs=[pl.BlockSpec((rows, bk), (lambda c, s: (s, c)) if a_on else (lambda c, s: (s, 0))),
                  pl.BlockSpec((rows, bn), (lambda c, s: (s, 0)) if a_on else (lambda c, s: (s, c)))],
        out_specs=[pl.BlockSpec((None, bk, bn), lambda c, s: (c, 0, 0))],
        out_shape=[jax.ShapeDtypeStruct((N_CHIPS, bk, bn), BF16)],
        scratch_shapes=[pltpu.VMEM((bk, bn), F32)],
        args=(a, b), exchange=exchange)
    return res[0] if exchange is None else res


def weight_grad_stacked(a, b3, bn, name):
    t, bk = a.shape
    width = b3.shape[-1]
    piece = math.gcd(bn, width)
    rows = min(t, WGRAD_ROWS)
    n_steps = t // rows

    def body(a_ref, b_ref, o_hbm, acc, staged, sem):
        s, c = pl.program_id(0), pl.program_id(1)
        av = a_ref[...].astype(BF16)
        for chunk in range(N_CHIPS):
            @pl.when(c == chunk)
            def _(chunk=chunk):
                cols = [divmod(chunk * bn + k * piece, width) for k in range(bn // piece)]
                b = jnp.concatenate([b_ref[p, :, lo:lo + piece] for p, lo in cols], axis=1).astype(BF16)
                _accumulate(acc.at[chunk], _dot_tn(av, b), s == 0)

                @pl.when(s == n_steps - 1)
                def _():
                    staged[...] = acc[chunk].astype(BF16)
                    copy = pltpu.make_async_copy(staged, o_hbm.at[chunk], sem)
                    copy.start()
                    copy.wait()

    return pl.pallas_call(
        body, name=name, grid=(n_steps, N_CHIPS),
        in_specs=[pl.BlockSpec((rows, bk), lambda s, c: (s, 0)),
                  pl.BlockSpec((b3.shape[0], rows, width), lambda s, c: (0, s, 0))],
        out_specs=ANY,
        out_shape=jax.ShapeDtypeStruct((N_CHIPS, bk, bn), BF16),
        scratch_shapes=[pltpu.VMEM((N_CHIPS, bk, bn), F32), pltpu.VMEM((bk, bn), BF16), pltpu.SemaphoreType.DMA],
        compiler_params=_params(("arbitrary", "arbitrary")),
    )(a, b3)


def _hgrn2_chunk(st, qs, fls, ivs, gls, l0, l1, l2, ng):
    nsub = len(qs)
    mx = jnp.maximum(jnp.maximum(l0, l1), l2)
    e0, e1, e2 = jnp.exp(l0 - mx), jnp.exp(l1 - mx), jnp.exp(l2 - mx)
    lb = e0 / (e0 + e1 + e2)
    rows = lax.broadcasted_iota(jnp.int32, (A_SUB, A_SUB), 0)
    cols = lax.broadcasted_iota(jnp.int32, (A_SUB, A_SUB), 1)
    tri = (rows >= cols).astype(F32)
    keep = (lax.broadcasted_iota(jnp.int32, (A_SUB, A_SUB, A_DK), 0)
            >= lax.broadcasted_iota(jnp.int32, (A_SUB, A_SUB, A_DK), 1))
    base = jnp.zeros_like(l0)
    bases, gs, ks, qfs = [], [], [], []
    for i in range(nsub):
        f = lb + (1.0 - lb) * jax.nn.sigmoid(fls[i])
        logf = jnp.log(f)
        bases.append(base)
        gs.append(base + jnp.dot(tri, logf, precision=lax.Precision.HIGHEST, preferred_element_type=F32))
        base = base + jnp.sum(logf, axis=0, keepdims=True)
        ks.append(1.0 - f)
        qfs.append(jax.nn.silu(qs[i]))
    g_last = base
    stb = st.astype(BF16)
    outs = []
    for i in range(nsub):
        o = _dot_nt((qfs[i] * jnp.exp(gs[i])).astype(BF16), stb)
        if i > 0:
            qt = (qfs[i] * jnp.exp(gs[i] - bases[i])).astype(BF16)
            kk = jnp.concatenate([ks[j] * jnp.exp(bases[i] - gs[j]) for j in range(i)], axis=0).astype(BF16)
            vv = jnp.concatenate(ivs[:i], axis=0).astype(BF16)
            o = o + _dot(_dot_nt(qt, kk).astype(BF16), vv)
        dec = jnp.exp(jnp.where(keep, gs[i][:, None, :] - gs[i][None, :, :], NEG_BIG))
        s_diag = jnp.sum(qfs[i][:, None, :] * ks[i][None, :, :] * dec, axis=-1)
        o = o + _dot(s_diag.astype(BF16), ivs[i].astype(BF16))
        o = o * lax.rsqrt(jnp.mean(o * o, axis=-1, keepdims=True) + EPS) * ng
        outs.append(o * jax.nn.silu(gls[i]))
    kdec = jnp.concatenate([ks[j] * jnp.exp(g_last - gs[j]) for j in range(nsub)], axis=0).astype(BF16)
    vall = jnp.concatenate(ivs, axis=0).astype(BF16)
    new_st = st * jnp.exp(g_last) + _dot_tn(vall, kdec)
    return new_st, outs


A_MAX_LOG_DECAY = 60.0


def _half_sums(logf):
    n = logf.shape[0]
    first = lax.broadcasted_iota(jnp.int32, logf.shape, 0) < n // 2
    return (jnp.sum(jnp.where(first, logf, 0.0), axis=0, keepdims=True),
            jnp.sum(jnp.where(first, 0.0, logf), axis=0, keepdims=True))


def _split3(x):
    hi = x.astype(BF16)
    r1 = x - hi.astype(F32)
    mid = r1.astype(BF16)
    return hi, mid, (r1 - mid.astype(F32)).astype(BF16)


def _tri_matmul(x, transpose):
    n = x.shape[0]
    r = lax.broadcasted_iota(jnp.int32, (n, n), 0)
    c = lax.broadcasted_iota(jnp.int32, (n, n), 1)
    tri = ((r <= c) if transpose else (r >= c)).astype(BF16)
    hi, mid, lo = _split3(x)
    return (_dot(tri, lo) + _dot(tri, mid)) + _dot(tri, hi)


@jax.custom_vjp
def _cumsum_rows(x):
    return _tri_matmul(x, False)


def _cumsum_rows_fwd(x):
    return _tri_matmul(x, False), None


def _cumsum_rows_bwd(_, dy):
    return (_tri_matmul(dy, True),)


_cumsum_rows.defvjp(_cumsum_rows_fwd, _cumsum_rows_bwd)


def _lower_bound(l0, l1, l2):
    mx = jnp.maximum(jnp.maximum(l0, l1), l2)
    e0, e1, e2 = jnp.exp(l0 - mx), jnp.exp(l1 - mx), jnp.exp(l2 - mx)
    return e0 / (e0 + e1 + e2)


def _b(x):
    return x.astype(BF16)


@jax.custom_vjp
def _mm(a, b):
    return _dot(_b(a), _b(b))


_mm.defvjp(lambda a, b: (_mm(a, b), (a, b)),
           lambda res, d: (_dot_nt(_b(d), _b(res[1])), _dot_tn(_b(res[0]), _b(d))))


@jax.custom_vjp
def _mm_nt(a, b):
    return _dot_nt(_b(a), _b(b))


_mm_nt.defvjp(lambda a, b: (_mm_nt(a, b), (a, b)),
              lambda res, d: (_dot(_b(d), _b(res[1])), _dot_tn(_b(d), _b(res[0]))))


def _dot_split(dot, a, b):
    ah, bh = _b(a), _b(b)
    al, bl = _b(a - ah.astype(F32)), _b(b - bh.astype(F32))
    return (dot(ah, bl) + dot(al, bh)) + dot(ah, bh)


@jax.custom_vjp
def _mm_scores(a, b):
    return _dot_nt(_b(a), _b(b))


_mm_scores.defvjp(lambda a, b: (_mm_scores(a, b), (a, b)),
                  lambda res, d: (_dot_split(_dot, d, res[1]), _dot_split(_dot_tn, d, res[0])))


@jax.custom_vjp
def _mm_tn(a, b):
    return _dot_tn(_b(a), _b(b))


_mm_tn.defvjp(lambda a, b: (_mm_tn(a, b), (a, b)),
              lambda res, d: (_dot_nt(_b(res[1]), _b(d)), _dot(_b(res[0]), _b(d))))


@jax.custom_vjp
def _split_heads(x):
    return tuple(x[:, h * A_DK:(h + 1) * A_DK] for h in range(A_HEADS))


def _split_heads_fwd(x):
    return _split_heads(x), None


def _split_heads_bwd(_, parts):
    return (jnp.concatenate(parts, axis=1),)


_split_heads.defvjp(_split_heads_fwd, _split_heads_bwd)


def _hgrn2_chunk_fast(sts, q, fl, iv, gl, l0, l1, l2, ng):
    lb = _lower_bound(l0, l1, l2)
    f = lb + (1.0 - lb) * jax.nn.sigmoid(fl)
    return _hgrn2_fast_core(sts, q, f, jnp.log(f), iv, gl, ng)


def _hgrn2_fast_core(sts, q, f, logf, iv, gl, ng):
    g = _cumsum_rows(logf)
    g_mid, g_last = _half_sums(logf)
    g_last = g_mid + g_last
    k = 1.0 - f
    qf = jax.nn.silu(q)
    qms = _split_heads(qf * jnp.exp(g - g_mid))
    kms = _split_heads(k * jnp.exp(g_mid - g))
    qgs = _split_heads(qf * jnp.exp(g))
    kds = _split_heads(k * jnp.exp(g_last - g))
    ivs = _split_heads(iv)
    decays = _split_heads(jnp.exp(g_last))
    n = q.shape[0]
    causal = lax.broadcasted_iota(jnp.int32, (n, n), 0) >= lax.broadcasted_iota(jnp.int32, (n, n), 1)
    raw = [_mm_scores(qm, km) for qm, km in zip(qms, kms)]
    inter = [_mm_nt(qg, st) for qg, st in zip(qgs, sts)]
    scores = [jnp.where(causal, s, 0.0) for s in raw]
    os = [a + _mm(s, v) for a, s, v in zip(inter, scores, ivs)]
    new_sts = [st * d + _mm_tn(v, kd) for st, d, v, kd in zip(sts, decays, ivs, kds)]
    os = [o * lax.rsqrt(jnp.mean(o * o, axis=-1, keepdims=True) + EPS) for o in os]
    return new_sts, jnp.concatenate(os, axis=1) * ng * jax.nn.silu(gl)


A_STEP_CHUNKS = 4


def _chunk_rows(j):
    return pl.ds(pl.multiple_of(j * A_CHUNK, A_CHUNK), A_CHUNK)


def _sub_rows(j, i):
    return pl.ds(pl.multiple_of(j * A_CHUNK + i * A_SUB, A_SUB), A_SUB)


def _sub_blocks(ref, head, j):
    lanes = slice(head * A_DK, (head + 1) * A_DK)
    return [ref[_sub_rows(j, i), lanes] for i in range(A_CHUNK // A_SUB)]


def hgrn2_fwd(proj, lb_table, a_norm, batch, name, exchange=None):
    t = proj.shape[0]
    n_steps = t // batch // (A_CHUNK * A_STEP_CHUNKS)
    rows = A_CHUNK * A_STEP_CHUNKS

    def body(q_ref, f_ref, i_ref, g_ref, lb_ref, ng_ref, o_ref, st_ref, dec_ref, st):
        @pl.when(pl.program_id(1) == 0)
        def _():
            st[...] = jnp.zeros_like(st)

        def chunk(j, carry):
            r = _chunk_rows(j)
            st_ref[j] = st[...]
            lb = _lower_bound(lb_ref[0:1, :], lb_ref[1:2, :], lb_ref[2:3, :])
            f = lb + (1.0 - lb) * jax.nn.sigmoid(f_ref[r, :])
            logf = jnp.log(f)
            decay = jnp.minimum(*_half_sums(logf))
            dec_ref[j] = decay
            mild = jnp.min(decay) >= -A_MAX_LOG_DECAY

            @pl.when(mild)
            def _():
                new_sts, o = _hgrn2_fast_core([st[h] for h in range(A_HEADS)], q_ref[r, :], f, logf,
                                              i_ref[r, :], g_ref[r, :], ng_ref[...])
                for h in range(A_HEADS):
                    st[h] = new_sts[h]
                o_ref[r, :] = o.astype(BF16)

            @pl.when(jnp.logical_not(mild))
            def _():
                for h in range(A_HEADS):
                    lanes = slice(h * A_DK, (h + 1) * A_DK)
                    new_st, outs = _hgrn2_chunk(
                        st[h], _sub_blocks(q_ref, h, j), _sub_blocks(f_ref, h, j), _sub_blocks(i_ref, h, j),
                        _sub_blocks(g_ref, h, j), lb_ref[0:1, lanes], lb_ref[1:2, lanes], lb_ref[2:3, lanes],
                        ng_ref[:, lanes])
                    st[h] = new_st
                    for i, o in enumerate(outs):
                        o_ref[_sub_rows(j, i), lanes] = o.astype(BF16)

            return carry

        lax.fori_loop(0, A_STEP_CHUNKS, chunk, 0)

    def part(k):
        return pl.BlockSpec((rows, A_WIDTH), lambda b, n: (b * n_steps + n, k))

    return _call(
        body, name=name, grid=(batch, n_steps),
        in_specs=[part(0), part(1), part(2), part(3),
                  pl.BlockSpec((3, A_WIDTH), lambda b, n: (0, 0)), pl.BlockSpec((1, A_WIDTH), lambda b, n: (0, 0))],
        out_specs=[part(0),
                   pl.BlockSpec((A_STEP_CHUNKS, A_HEADS, A_DK, A_DK), lambda b, n: (b * n_steps + n, 0, 0, 0)),
                   pl.BlockSpec((A_STEP_CHUNKS, 1, A_WIDTH), lambda b, n: (b * n_steps + n, 0, 0))],
        out_shape=[jax.ShapeDtypeStruct((t, A_WIDTH), BF16),
                   jax.ShapeDtypeStruct((t // A_CHUNK, A_HEADS, A_DK, A_DK), F32),
                   jax.ShapeDtypeStruct((t // A_CHUNK, 1, A_WIDTH), F32)],
        scratch_shapes=[pltpu.VMEM((A_HEADS, A_DK, A_DK), F32)],
        args=(proj, proj, proj, proj, lb_table, a_norm), exchange=exchange)


def hgrn2_bwd(proj, states, decays, lb_table, a_norm, do, batch, name, exchange=None):
    t = proj.shape[0]
    n_steps = t // batch // (A_CHUNK * A_STEP_CHUNKS)
    rows = A_CHUNK * A_STEP_CHUNKS

    def body(q_ref, f_ref, i_ref, g_ref, st_ref, dec_ref, lb_ref, ng_ref, do_ref, dp_ref, dlb_ref, dng_ref, dst):
        @pl.when(jnp.logical_and(pl.program_id(0) == 0, pl.program_id(1) == 0))
        def _():
            dlb_ref[...] = jnp.zeros_like(dlb_ref)
            dng_ref[...] = jnp.zeros_like(dng_ref)

        @pl.when(pl.program_id(1) == 0)
        def _():
            dst[...] = jnp.zeros_like(dst)

        def chunk(jj, carry):
            j = A_STEP_CHUNKS - 1 - jj
            r = _chunk_rows(j)
            mild = jnp.min(dec_ref[j]) >= -A_MAX_LOG_DECAY

            @pl.when(mild)
            def _():
                _, vjp = jax.vjp(
                    _hgrn2_chunk_fast, [st_ref[j, h] for h in range(A_HEADS)], q_ref[r, :], f_ref[r, :],
                    i_ref[r, :], g_ref[r, :], lb_ref[0:1, :], lb_ref[1:2, :], lb_ref[2:3, :], ng_ref[...])
                d_sts, dq, df, di, dg, dl0, dl1, dl2, dng = vjp(
                    ([dst[h] for h in range(A_HEADS)], do_ref[r, :].astype(F32)))
                for h in range(A_HEADS):
                    dst[h] = d_sts[h]
                for k, part in enumerate((dq, df, di, dg)):
                    dp_ref[r, k * A_WIDTH:(k + 1) * A_WIDTH] = part
                for row, val in enumerate((dl0, dl1, dl2)):
                    dlb_ref[row:row + 1, :] += val
                dng_ref[...] += dng

            @pl.when(jnp.logical_not(mild))
            def _():
                for h in range(A_HEADS):
                    lanes = slice(h * A_DK, (h + 1) * A_DK)
                    _, vjp = jax.vjp(
                        _hgrn2_chunk, st_ref[j, h], _sub_blocks(q_ref, h, j), _sub_blocks(f_ref, h, j),
                        _sub_blocks(i_ref, h, j), _sub_blocks(g_ref, h, j), lb_ref[0:1, lanes], lb_ref[1:2, lanes],
                        lb_ref[2:3, lanes], ng_ref[:, lanes])
                    douts = [x.astype(F32) for x in _sub_blocks(do_ref, h, j)]
                    d_st, dqs, dfs, dis, dgs, dl0, dl1, dl2, dng = vjp((dst[h], douts))
                    dst[h] = d_st
                    for k, parts in enumerate((dqs, dfs, dis, dgs)):
                        for i in range(A_CHUNK // A_SUB):
                            dp_ref[_sub_rows(j, i), k * A_WIDTH + h * A_DK:k * A_WIDTH + (h + 1) * A_DK] = parts[i]
                    for row, val in enumerate((dl0, dl1, dl2)):
                        dlb_ref[row:row + 1, lanes] += val
                    dng_ref[:, lanes] += dng

            return carry

        lax.fori_loop(0, A_STEP_CHUNKS, chunk, 0)

    def rev(b, n):
        return b * n_steps + (n_steps - 1 - n)

    def part(k):
        return pl.BlockSpec((rows, A_WIDTH), lambda b, n: (rev(b, n), k))

    const3 = pl.BlockSpec((3, A_WIDTH), lambda b, n: (0, 0))
    const1 = pl.BlockSpec((1, A_WIDTH), lambda b, n: (0, 0))
    return _call(
        body, name=name, grid=(batch, n_steps),
        in_specs=[part(0), part(1), part(2), part(3),
                  pl.BlockSpec((A_STEP_CHUNKS, A_HEADS, A_DK, A_DK), lambda b, n: (rev(b, n), 0, 0, 0)),
                  pl.BlockSpec((A_STEP_CHUNKS, 1, A_WIDTH), lambda b, n: (rev(b, n), 0, 0)),
                  const3, const1, part(0)],
        out_specs=[pl.BlockSpec((rows, 4 * A_WIDTH), lambda b, n: (rev(b, n), 0)), const3, const1],
        out_shape=[jax.ShapeDtypeStruct((t, 4 * A_WIDTH + 2 * B_WIDTH), F32),
                   jax.ShapeDtypeStruct((3, A_WIDTH), F32), jax.ShapeDtypeStruct((1, A_WIDTH), F32)],
        scratch_shapes=[pltpu.VMEM((A_HEADS, A_DK, A_DK), F32)],
        args=(proj, proj, proj, proj, states, decays, lb_table, a_norm, do), exchange=exchange)


B_GDIM = B_WIDTH // B_GROUPS
B_ROWS = 512


def _gmlp_chunk(ubs, vbs, lngs, lnbs, ws, bcols):
    vs = [jax.nn.gelu(v) for v in vbs]
    mu = sum(jnp.sum(v, axis=-1, keepdims=True) for v in vs) * (1.0 / B_WIDTH)
    var = sum(jnp.sum(jnp.square(v - mu), axis=-1, keepdims=True) for v in vs) * (1.0 / B_WIDTH)
    rstd = lax.rsqrt(var + EPS)
    tril = (lax.broadcasted_iota(jnp.int32, (B_CHUNK, B_CHUNK), 0)
            >= lax.broadcasted_iota(jnp.int32, (B_CHUNK, B_CHUNK), 1))
    outs = []
    for g in range(B_GROUPS):
        vn = (vs[g] - mu) * rstd * lngs[g] + lnbs[g]
        w = jnp.where(tril, ws[g], 0.0).astype(BF16)
        outs.append(jax.nn.gelu(ubs[g]) * (_dot(w, vn.astype(BF16)) + bcols[g]))
    return outs


def _gmlp_args(u_ref, v_ref, lng_ref, lnb_ref, w_ref, bt_ref, rows):
    def groups(ref):
        return [ref[rows, g * B_GDIM:(g + 1) * B_GDIM] for g in range(B_GROUPS)]

    def vec(ref):
        return [ref[:, g * B_GDIM:(g + 1) * B_GDIM] for g in range(B_GROUPS)]

    return (groups(u_ref), groups(v_ref), vec(lng_ref), vec(lnb_ref),
            [w_ref[g] for g in range(B_GROUPS)], [bt_ref[:, g:g + 1] for g in range(B_GROUPS)])


def gmlp_fwd(proj, oa, ln_g, ln_b, w, bias_t, name, exchange=None):
    t = proj.shape[0]

    def body(u_ref, v_ref, oa_ref, lng_ref, lnb_ref, w_ref, bt_ref, o_ref):
        o_ref[:, 0:A_WIDTH] = oa_ref[...]
        for n in range(B_ROWS // B_CHUNK):
            rows = slice(n * B_CHUNK, (n + 1) * B_CHUNK)
            outs = _gmlp_chunk(*_gmlp_args(u_ref, v_ref, lng_ref, lnb_ref, w_ref, bt_ref, rows))
            for g, o in enumerate(outs):
                o_ref[rows, A_WIDTH + g * B_GDIM:A_WIDTH + (g + 1) * B_GDIM] = o.astype(BF16)

    vec = pl.BlockSpec((1, B_WIDTH), lambda i: (0, 0))
    return _call(
        body, name=name, grid=(t // B_ROWS,),
        in_specs=[pl.BlockSpec((B_ROWS, B_WIDTH), lambda i: (i, 4)), pl.BlockSpec((B_ROWS, B_WIDTH), lambda i: (i, 5)),
                  pl.BlockSpec((B_ROWS, A_WIDTH), lambda i: (i, 0)), vec, vec,
                  pl.BlockSpec((B_GROUPS, B_CHUNK, B_CHUNK), lambda i: (0, 0, 0)),
                  pl.BlockSpec((B_CHUNK, B_GROUPS), lambda i: (0, 0))],
        out_specs=[pl.BlockSpec((B_ROWS, A_WIDTH + B_WIDTH), lambda i: (i, 0))],
        out_shape=[jax.ShapeDtypeStruct((t, A_WIDTH + B_WIDTH), BF16)],
        args=(proj, proj, oa, ln_g, ln_b, w, bias_t), exchange=exchange)


def gmlp_bwd(proj, dmixin, ln_g, ln_b, w, bias_t, dproj, name, exchange=None):
    t = proj.shape[0]

    def body(u_ref, v_ref, do_ref, lng_ref, lnb_ref, w_ref, bt_ref, dp_in_ref,
             dp_ref, dlng_ref, dlnb_ref, dw_ref, dbt_ref):
        del dp_in_ref

        @pl.when(pl.program_id(0) == 0)
        def _():
            for ref in (dlng_ref, dlnb_ref, dw_ref, dbt_ref):
                ref[...] = jnp.zeros_like(ref)

        for n in range(B_ROWS // B_CHUNK):
            rows = slice(n * B_CHUNK, (n + 1) * B_CHUNK)
            _, vjp = jax.vjp(_gmlp_chunk, *_gmlp_args(u_ref, v_ref, lng_ref, lnb_ref, w_ref, bt_ref, rows))
            douts = [do_ref[rows, g * B_GDIM:(g + 1) * B_GDIM] for g in range(B_GROUPS)]
            dus, dvs, dlngs, dlnbs, dws, dbs = vjp(douts)
            for g in range(B_GROUPS):
                lanes = slice(g * B_GDIM, (g + 1) * B_GDIM)
                dp_ref[rows, lanes] = dus[g]
                dp_ref[rows, B_WIDTH + g * B_GDIM:B_WIDTH + (g + 1) * B_GDIM] = dvs[g]
                dlng_ref[:, lanes] += dlngs[g]
                dlnb_ref[:, lanes] += dlnbs[g]
                dw_ref[g] += dws[g]
                dbt_ref[:, g:g + 1] += dbs[g]

    vec = pl.BlockSpec((1, B_WIDTH), lambda i: (0, 0))
    wspec = pl.BlockSpec((B_GROUPS, B_CHUNK, B_CHUNK), lambda i: (0, 0, 0))
    bspec = pl.BlockSpec((B_CHUNK, B_GROUPS), lambda i: (0, 0))
    return _call(
        body, name=name, grid=(t // B_ROWS,),
        in_specs=[pl.BlockSpec((B_ROWS, B_WIDTH), lambda i: (i, 4)), pl.BlockSpec((B_ROWS, B_WIDTH), lambda i: (i, 5)),
                  pl.BlockSpec((B_ROWS, B_WIDTH), lambda i: (i, 1)), vec, vec, wspec, bspec,
                  pl.BlockSpec(memory_space=pl.ANY)],
        out_specs=[pl.BlockSpec((B_ROWS, 2 * B_WIDTH), lambda i: (i, 2)), vec, vec, wspec, bspec],
        out_shape=[jax.ShapeDtypeStruct(dproj.shape, F32), jax.ShapeDtypeStruct((1, B_WIDTH), F32),
                   jax.ShapeDtypeStruct((1, B_WIDTH), F32), jax.ShapeDtypeStruct((B_GROUPS, B_CHUNK, B_CHUNK), F32),
                   jax.ShapeDtypeStruct((B_CHUNK, B_GROUPS), F32)],
        aliases={7: 0}, args=(proj, proj, dmixin, ln_g, ln_b, w, bias_t, dproj), exchange=exchange)


C_FWD_BLOCKS = 8
C_BWD_BLOCKS = 8
C_PAIR = 2 * C_HEAD_DIM
C_PAIRS = C_HEADS // 2
C_SCALE = 1.0 / math.sqrt(C_HEAD_DIM)
C_ROT_DIM = 2 * C_ROT_HALF


def rope_tables(pos_col, name, exchange=None):
    t = pos_col.shape[0]
    per = SEQ // C_RES

    def body(p_ref, c_ref, a_ref, b_ref):
        lane = jnp.bitwise_and(lax.broadcasted_iota(jnp.int32, (1, C_PAIR), 1), C_HEAD_DIM - 1)
        j = jnp.bitwise_and(lane, C_ROT_HALF - 1).astype(F32)
        inv = jnp.exp(j * (-math.log(ROPE_THETA) / C_ROT_HALF))
        for r in range(C_RES):
            rows = slice(r * per, (r + 1) * per)
            ang = p_ref[pl.ds(r, per, stride=C_RES), :].astype(F32) * inv
            cos, sin = jnp.cos(ang), jnp.sin(ang)
            c_ref[rows, :] = jnp.where(lane < C_ROT_DIM, cos, 1.0)
            a_ref[rows, :] = jnp.where(lane < C_ROT_HALF, -sin, 0.0)
            b_ref[rows, :] = jnp.where(jnp.logical_and(lane >= C_ROT_HALF, lane < C_ROT_DIM), sin, 0.0)

    tab = pl.BlockSpec((SEQ, C_PAIR), lambda i: (i, 0))
    return _call(
        body, name=name, grid=(t // SEQ,),
        in_specs=[pl.BlockSpec((SEQ, 1), lambda i: (i, 0))],
        out_specs=[tab, tab, tab],
        out_shape=[jax.ShapeDtypeStruct((t, C_PAIR), F32)] * 3,
        args=(pos_col,), exchange=exchange)


def _rope(x, c, a, b):
    return x * c + pltpu.roll(x, C_PAIR - C_ROT_HALF, 1) * a + pltpu.roll(x, C_ROT_HALF, 1) * b


def _rope_t(d, c, a, b):
    return d * c + pltpu.roll(d * a, C_ROT_HALF, 1) + pltpu.roll(d * b, C_PAIR - C_ROT_HALF, 1)


C_RES = 16


def _residue_major(a, batch):
    return a.reshape(batch, SEQ // C_RES, C_RES, -1).transpose(0, 2, 1, 3).reshape(a.shape)


def _sequence_order(a, batch):
    return a.reshape(batch, C_RES, SEQ // C_RES, -1).transpose(0, 2, 1, 3).reshape(a.shape)


def _block_pieces(idx, dil):
    nblk = SEQ // dil // C_BLOCK
    r, n = idx // nblk, idx % nblk
    per = C_RES // dil
    size = C_BLOCK // per

    def pieces(blk):
        return [((dil * a + r) * (SEQ // C_RES) + size * blk, size) for a in range(per)]

    return pieces(n), pieces(jnp.maximum(n - 1, 0)), n > 0


def _get_rows(ref, pieces):
    return jnp.concatenate([ref[pl.ds(pl.multiple_of(start, 8), size), :] for start, size in pieces], axis=0)


def _set_rows(ref, pieces, val, add=False):
    for k, (start, size) in enumerate(pieces):
        rows = pl.ds(pl.multiple_of(start, 8), size)
        part = val[k * size:(k + 1) * size]
        ref[rows, :] = ref[rows, :] + part if add else part


def _head_masks():
    low = lax.broadcasted_iota(jnp.int32, (1, C_PAIR), 1) < C_HEAD_DIM
    return low, jnp.logical_not(low)


def _attn_mask(has_prev, dil):
    per = C_RES // dil
    size = C_BLOCK // per

    def position(x):
        x = jnp.bitwise_and(x, C_BLOCK - 1)
        return per * jnp.bitwise_and(x, size - 1) + x // size

    j = lax.broadcasted_iota(jnp.int32, (2 * C_BLOCK, 2 * C_BLOCK), 1)
    pi = position(lax.broadcasted_iota(jnp.int32, (2 * C_BLOCK, 2 * C_BLOCK), 0))
    pj = position(j)
    own = j < C_BLOCK
    return jnp.logical_or(jnp.logical_and(own, pj <= pi),
                          jnp.logical_and(jnp.logical_and(jnp.logical_not(own), pj >= pi), has_prev))


def _stack_heads(x):
    low, high = _head_masks()
    return jnp.concatenate([jnp.where(low, x, 0.0), jnp.where(high, x, 0.0)], axis=0)


def _unstack_heads(x):
    low, _ = _head_masks()
    return jnp.where(low, x[:C_BLOCK], x[C_BLOCK:])


def attn_fwd(qkv, cos_t, sin_a, sin_b, batch, name, exchange=None):
    t = qkv.shape[0]
    nbr = len(C_DILATIONS)

    def body(q_ref, k_ref, v_ref, c_ref, a_ref, b_ref, o_ref, l_ref, qr_ref, kr_ref, qs, ks, *stats):
        acc, mm, dd = stats[0:nbr], stats[nbr:2 * nbr], stats[2 * nbr:3 * nbr]
        c, a, b = c_ref[...], a_ref[...], b_ref[...]
        qs[...] = _rope(q_ref[...], c, a, b) * C_SCALE
        ks[...] = _rope(k_ref[...], c, a, b)
        qr_ref[...] = qs[...].astype(BF16)
        kr_ref[...] = ks[...].astype(BF16)

        def load(idx, dil):
            own, prev, has_prev = _block_pieces(idx, dil)
            return own, (has_prev, _get_rows(qs, own), _get_rows(ks, own), _get_rows(ks, prev),
                         _get_rows(v_ref, own), _get_rows(v_ref, prev))

        def scores(dil, has_prev, q, k_own, k_prev, v_own, v_prev):
            k_cat = jnp.concatenate([k_own, k_prev], axis=0).astype(BF16)
            return jnp.where(_attn_mask(has_prev, dil), _dot_nt(_stack_heads(q).astype(BF16), k_cat), NEG_BIG)

        def softmax(s):
            m = jnp.max(s, axis=-1, keepdims=True)
            p = jnp.exp(s - m)
            return p.astype(BF16), m, jnp.sum(p, axis=-1, keepdims=True)

        def values(pb, has_prev, q, k_own, k_prev, v_own, v_prev):
            low, high = _head_masks()
            v_cat = jnp.concatenate([v_own, v_prev], axis=0)
            p_wide = jnp.concatenate([pb[:C_BLOCK], pb[C_BLOCK:]], axis=1)
            v_tall = jnp.concatenate([jnp.where(low, v_cat, 0.0), jnp.where(high, v_cat, 0.0)], axis=0).astype(BF16)
            return _dot(p_wide, v_tall)

        for bi, dil in enumerate(C_DILATIONS):
            def pair(i, carry, bi=bi, dil=dil):
                low, _ = _head_masks()
                loaded = [load(C_FWD_BLOCKS * i + k, dil) for k in range(C_FWD_BLOCKS)]
                ss = [scores(dil, *ops) for _, ops in loaded]
                sm = [softmax(s) for s in ss]
                pvs = [values(pb, *ops) for (pb, _, _), (_, ops) in zip(sm, loaded)]
                for (own, _), (_, m, den), pv in zip(loaded, sm, pvs):
                    _set_rows(acc[bi], own, pv)
                    _set_rows(mm[bi], own, jnp.where(low, m[:C_BLOCK], m[C_BLOCK:]))
                    _set_rows(dd[bi], own, jnp.where(low, den[:C_BLOCK], den[C_BLOCK:]))
                return carry

            lax.fori_loop(0, SEQ // C_BLOCK // C_FWD_BLOCKS, pair, 0)
        step = 2 * C_BLOCK
        for r0 in range(0, SEQ, step):
            rr = slice(r0, r0 + step)
            ms = [mm[g][rr, :] for g in range(nbr)]
            m_all = functools.reduce(jnp.maximum, ms)
            ws = [jnp.exp(m - m_all) for m in ms]
            num = sum(acc[g][rr, :] * ws[g] for g in range(nbr))
            den = sum(dd[g][rr, :] * ws[g] for g in range(nbr))
            o_ref[rr, :] = (num / den).astype(BF16)
            l_ref[rr, :] = m_all + jnp.log(den)

    def col(k):
        return pl.BlockSpec((SEQ, C_PAIR), lambda b, p: (b, k * C_PAIRS + p))

    tab = pl.BlockSpec((SEQ, C_PAIR), lambda b, p: (b, 0))
    return _call(
        body, name=name, grid=(batch, C_PAIRS),
        in_specs=[col(0), col(1), col(2), tab, tab, tab],
        out_specs=[col(0), col(0), col(0), col(0)],
        out_shape=[jax.ShapeDtypeStruct((t, D_MODEL), BF16), jax.ShapeDtypeStruct((t, D_MODEL), F32),
                   jax.ShapeDtypeStruct((t, D_MODEL), BF16), jax.ShapeDtypeStruct((t, D_MODEL), BF16)],
        scratch_shapes=[pltpu.VMEM((SEQ, C_PAIR), F32)] * (2 + 3 * nbr),
        args=(qkv, qkv, qkv, cos_t, sin_a, sin_b), exchange=exchange)


def attn_bwd(qr, kr, qkv, cos_t, sin_a, sin_b, o, lse, do, batch, name, exchange=None):
    t = qkv.shape[0]

    def body(q_ref, k_ref, v_ref, c_ref, a_ref, b_ref, o_ref, l_ref, do_ref, dqkv_ref, qs, ks, dqs, dks, dvs, dlt):
        low, _ = _head_masks()
        c, a, b = c_ref[...], a_ref[...], b_ref[...]
        qs[...] = q_ref[...].astype(F32)
        ks[...] = k_ref[...].astype(F32)
        prod = do_ref[...] * o_ref[...].astype(F32)
        s_low = jnp.sum(jnp.where(low, prod, 0.0), axis=-1, keepdims=True)
        s_all = jnp.sum(prod, axis=-1, keepdims=True)
        dlt[...] = jnp.where(low, s_low, s_all - s_low)
        dqs[...] = jnp.zeros_like(dqs)
        dks[...] = jnp.zeros_like(dks)
        dvs[...] = jnp.zeros_like(dvs)

        def load(idx, dil):
            own, prev, has_prev = _block_pieces(idx, dil)
            return (own, prev), (has_prev, _get_rows(qs, own), _get_rows(do_ref, own), _get_rows(ks, own),
                                 _get_rows(ks, prev), _get_rows(v_ref, own), _get_rows(v_ref, prev),
                                 _get_rows(l_ref, own), _get_rows(dlt, own))

        def operands(dil, has_prev, q, do, k_own, k_prev, v_own, v_prev, l_full, d_full):
            lcol = jnp.concatenate([l_full[:, 0:1], l_full[:, C_HEAD_DIM:C_HEAD_DIM + 1]], axis=0)
            dcol = jnp.concatenate([d_full[:, 0:1], d_full[:, C_HEAD_DIM:C_HEAD_DIM + 1]], axis=0)
            return (_stack_heads(q).astype(BF16), _stack_heads(do).astype(BF16),
                    jnp.concatenate([k_own, k_prev], axis=0).astype(BF16),
                    jnp.concatenate([v_own, v_prev], axis=0).astype(BF16), lcol, dcol, _attn_mask(has_prev, dil))

        for dil in C_DILATIONS:
            def pair(i, carry, dil=dil):
                loaded = [load(C_BWD_BLOCKS * i + k, dil) for k in range(C_BWD_BLOCKS)]
                ops = [operands(dil, *o) for _, o in loaded]
                ss = [_dot_nt(q_stack, k_cat) for q_stack, _, k_cat, _, _, _, _ in ops]
                dps = [_dot_nt(do_stack, v_cat) for _, do_stack, _, v_cat, _, _, _ in ops]
                ps = [jnp.exp(jnp.where(o[6], s, NEG_BIG) - o[4]) for s, o in zip(ss, ops)]
                dss = [(p * (dp - o[5])).astype(BF16) for p, dp, o in zip(ps, dps, ops)]
                dvs_ = [_dot_tn(p.astype(BF16), o[1]) for p, o in zip(ps, ops)]
                dks_ = [_dot_tn(ds, o[0]) for ds, o in zip(dss, ops)]
                dqs_ = [_unstack_heads(_dot(ds, o[2])) for ds, o in zip(dss, ops)]
                for ((own, prev), _), dq, dk_cat, dv_cat in zip(loaded, dqs_, dks_, dvs_):
                    _set_rows(dqs, own, dq, add=True)
                    _set_rows(dks, own, dk_cat[:C_BLOCK], add=True)
                    _set_rows(dvs, own, dv_cat[:C_BLOCK], add=True)
                    _set_rows(dks, prev, dk_cat[C_BLOCK:], add=True)
                    _set_rows(dvs, prev, dv_cat[C_BLOCK:], add=True)
                return carry

            lax.fori_loop(0, SEQ // C_BLOCK // C_BWD_BLOCKS, pair, 0)
        dqkv_ref[0] = _rope_t(dqs[...] * C_SCALE, c, a, b).astype(BF16)
        dqkv_ref[1] = _rope_t(dks[...], c, a, b).astype(BF16)
        dqkv_ref[2] = dvs[...].astype(BF16)

    def col(k):
        return pl.BlockSpec((SEQ, C_PAIR), lambda b, p: (b, k * C_PAIRS + p))

    tab = pl.BlockSpec((SEQ, C_PAIR), lambda b, p: (b, 0))
    return _call(
        body, name=name, grid=(batch, C_PAIRS),
        in_specs=[col(0), col(0), col(2), tab, tab, tab, col(0), col(0), col(0)],
        out_specs=[pl.BlockSpec((3, SEQ, C_PAIR), lambda b, p: (0, b, p))],
        out_shape=[jax.ShapeDtypeStruct((3, t, D_MODEL), BF16)],
        scratch_shapes=[pltpu.VMEM((SEQ, C_PAIR), F32)] * 6,
        args=(qr, kr, qkv, cos_t, sin_a, sin_b, o, lse, do), exchange=exchange)


def allreduce_small(slab, name):
    rows, lanes = slab.shape

    def body(x_ref, out_ref, gath, send_sems, recv_sems, local_sem):
        x, y, c, chips = _place()
        me, sibling = (x, y, c), (x, y, 1 - c)

        def slot(px, py, pc):
            return gath.at[4 * px + 2 * py + pc]

        def copy(k, block, to, src=None):
            return pltpu.make_async_remote_copy(
                src_ref=slot(*block) if src is None else src, dst_ref=slot(*block),
                send_sem=send_sems.at[k], recv_sem=recv_sems.at[k], device_id=to, device_id_type=MESH)

        mine = pltpu.make_async_copy(x_ref, slot(*me), local_sem)
        mine.start()
        first = [copy(0, me, sibling, src=x_ref)]
        first += [copy(1 + j, me, (*chip, c), src=x_ref) for j, chip in enumerate(chips)]
        for cp in first:
            cp.start()
        passed = [copy(4 + j, (*chip, c), sibling) for j, chip in enumerate(chips)]
        for j, chip in enumerate(chips):
            copy(1 + j, (*chip, c), me).wait_recv()
            passed[j].start()
        copy(0, sibling, me).wait_recv()
        for j, chip in enumerate(chips):
            copy(4 + j, (*chip, 1 - c), me).wait_recv()
        for cp in first + passed:
            cp.wait_send()
        mine.wait()
        total = gath[0]
        for d in range(1, N_DEV):
            total = total + gath[d]
        out_ref[...] = total

    return pl.pallas_call(
        body, name=name,
        in_specs=[pl.BlockSpec(memory_space=pltpu.VMEM)],
        out_specs=pl.BlockSpec(memory_space=pltpu.VMEM),
        out_shape=jax.ShapeDtypeStruct((rows, lanes), F32),
        scratch_shapes=[pltpu.VMEM((N_DEV, rows, lanes), F32),
                        pltpu.SemaphoreType.DMA((7,)), pltpu.SemaphoreType.DMA((7,)), pltpu.SemaphoreType.DMA],
    )(slab)


ELT_ROWS = 512


def reduce_slabs(r, name):
    r = r.reshape(N_CHIPS, -1, r.shape[-1])
    _, rows, cols = r.shape
    br = min(rows, ELT_ROWS)

    def body(r_ref, o_ref):
        o_ref[...] = ((r_ref[3].astype(F32) + r_ref[0].astype(F32)) + r_ref[1].astype(F32)) + r_ref[2].astype(F32)

    return pl.pallas_call(
        body, name=name, grid=(rows // br,),
        in_specs=[pl.BlockSpec((N_CHIPS, br, cols), lambda i: (0, i, 0))],
        out_specs=pl.BlockSpec((br, cols), lambda i: (i, 0)),
        out_shape=jax.ShapeDtypeStruct((rows, cols), F32),
        compiler_params=_params(("arbitrary",)),
    )(r)


def _adamw(w, g, m, v):
    m = ADAM_B1 * m + (1.0 - ADAM_B1) * g
    v = ADAM_B2 * v + (1.0 - ADAM_B2) * jnp.square(g)
    m_hat = m / (1.0 - ADAM_B1 ** ADAM_STEP)
    v_hat = v / (1.0 - ADAM_B2 ** ADAM_STEP)
    delta = -ADAM_LR * (m_hat / (jnp.sqrt(v_hat) + ADAM_EPS) + ADAM_WD * w)
    return delta, m, v


def adamw_big(w, s_mine, s_sibling, m, v, name):
    rows, cols = w.shape
    parts = len(s_mine)
    br = min(rows // parts, ELT_ROWS)
    nb = rows // parts // br

    def body(w_ref, m_ref, v_ref, *rest):
        sums, (g_out, d_out, m_out, v_out) = rest[:2 * parts], rest[2 * parts:]
        p = pl.program_id(0)
        g = sums[0][...] + sums[parts][...]
        for k in range(1, parts):
            g = jnp.where(p == k, sums[k][...] + sums[parts + k][...], g)
        g_out[...] = g
        d_out[...], m_out[...], v_out[...] = _adamw(w_ref[...], g, m_ref[...], v_ref[...])

    def part_spec(k):
        return pl.BlockSpec((br, cols), lambda p, i: (jnp.where(p == k, i, jnp.where(p < k, 0, nb - 1)), 0))

    blk = pl.BlockSpec((br, cols), lambda p, i: (p * nb + i, 0))
    out = jax.ShapeDtypeStruct((rows, cols), F32)
    return pl.pallas_call(
        body, name=name, grid=(parts, nb),
        in_specs=[blk] * 3 + [part_spec(k) for k in range(parts)] * 2, out_specs=[blk] * 4, out_shape=[out] * 4,
        compiler_params=_params(("arbitrary", "arbitrary")),
    )(w, m, v, *s_mine, *s_sibling)


def adamw_small(ws, gs, ms, vs, name):
    n = len(ws)

    def body(*refs):
        w_refs, g_refs, m_refs, v_refs = (refs[k * n:(k + 1) * n] for k in range(4))
        d_out, m_out, v_out = (refs[(4 + k) * n:(5 + k) * n] for k in range(3))
        for i in range(n):
            d_out[i][...], m_out[i][...], v_out[i][...] = _adamw(
                w_refs[i][...], g_refs[i][...], m_refs[i][...], v_refs[i][...])

    outs = [jax.ShapeDtypeStruct(w.shape, F32) for w in ws]
    res = pl.pallas_call(body, name=name, out_shape=outs * 3)(*ws, *gs, *ms, *vs)
    return res[:n], res[n:2 * n], res[2 * n:]


SLAB_LANES = 128
SLAB_ROW_ALIGN = 8


def _pack(parts):
    flat = jnp.concatenate([p.reshape(-1) for p in parts])
    rows = -(-flat.shape[0] // (SLAB_LANES * SLAB_ROW_ALIGN)) * SLAB_ROW_ALIGN
    flat = jnp.pad(flat, (0, rows * SLAB_LANES - flat.shape[0]))
    return flat.reshape(rows, SLAB_LANES)


def _unpack(slab, shapes):
    flat = slab.reshape(-1)
    out, pos = [], 0
    for s in shapes:
        size = math.prod(s)
        out.append(flat[pos:pos + size].reshape(s))
        pos += size
    return out


def kernel(x, positions, norm_mix_pre, norm_mix_post, norm_ffn_pre, norm_ffn_post, w_in_even, lb_table, a_norm, b_ln_g, b_ln_b, b_ws, b_bias, w_out_even, w_in_odd, w_out_odd, w_ff1, w_ff2, loss_target, m_norm_mix_pre, m_norm_mix_post, m_norm_ffn_pre, m_norm_ffn_post, m_w_in_even, m_lb_table, m_a_norm, m_b_ln_g, m_b_ln_b, m_b_ws, m_b_bias, m_w_out_even, m_w_in_odd, m_w_out_odd, m_w_ff1, m_w_ff2, v_norm_mix_pre, v_norm_mix_post, v_norm_ffn_pre, v_norm_ffn_post, v_w_in_even, v_lb_table, v_a_norm, v_b_ln_g, v_b_ln_b, v_b_ws, v_b_bias, v_w_out_even, v_w_in_odd, v_w_out_odd, v_w_ff1, v_w_ff2):
    batch = x.shape[0]
    t = batch * SEQ
    d = D_MODEL
    x0 = x.reshape(t, d)
    target = loss_target.reshape(t, d)

    def gain(p, layer):
        return p[layer:layer + 1]

    def gather(*shards):
        return _Exchange("gather", [w.astype(BF16) for w in shards])

    def scatter(*grads):
        return _Exchange("scatter", grads)

    cos_t, sin_a, sin_b, win_e = rope_tables(positions.reshape(t, 1), "rope_tables", exchange=gather(w_in_even[0]))
    bias_t = b_bias[0].T
    proj, h0, w1_0 = norm_matmul(x0, gain(norm_mix_pre, 0), win_e, "in_proj_even", exchange=gather(w_ff1[0]))
    oa, states, decays, w2_0 = hgrn2_fwd(proj, lb_table, a_norm, batch, "hgrn2_fwd", exchange=gather(w_ff2[0]))
    mixin, wout_e = gmlp_fwd(proj, oa, b_ln_g, b_ln_b, b_ws[0], bias_t, "gmlp_fwd", exchange=gather(w_out_even[0]))
    mix0, x1 = out_proj(mixin, wout_e, x0, gain(norm_mix_post, 0), "out_proj_even")
    x2, hf0, a0, y0, win_o, wout_o = ffn_fwd(x1, gain(norm_ffn_pre, 0), w1_0, w2_0, gain(norm_ffn_post, 0),
                                             "ffn_fwd_0", exchange=gather(w_in_odd[0], w_out_odd[0]))
    x2p = _residue_major(x2, batch)
    qkv, h1 = norm_matmul(x2p, gain(norm_mix_pre, 1), win_o, "in_proj_odd")
    ao, lse, q_rot, k_rot, w1_1, w2_1 = attn_fwd(qkv, cos_t, sin_a, sin_b, batch, "attn_fwd",
                                                 exchange=gather(w_ff1[1], w_ff2[1]))
    mix1, x3 = out_proj(ao, wout_o, x2p, gain(norm_mix_post, 1), "out_proj_odd")
    dx4, hf1, a1, y1, loss_part = ffn_fwd(x3, gain(norm_ffn_pre, 1), w1_1, w2_1, gain(norm_ffn_post, 1),
                                          "ffn_fwd_1", target=_residue_major(target, batch))

    hc = D_FF // N_CHIPS
    dx3, dy1, da1, dg_fpre1, dg_fpost1 = ffn_bwd(
        dx4, x3, y1, a1, gain(norm_ffn_pre, 1), gain(norm_ffn_post, 1), w1_1, w2_1, "ffn_bwd_1")
    g_w1_1 = weight_grad(hf1, da1, "b", d, hc, False, "wgrad_ff1_1")
    g_w2_1 = weight_grad(a1, dy1, "a", hc, d, True, "wgrad_ff2_1")
    dmix1, dao, dg_mpost1 = out_proj_bwd(dx3, mix1, gain(norm_mix_post, 1), wout_o, "out_proj_bwd_odd")
    g_wout_o = weight_grad(ao, dmix1, "a", d // N_CHIPS, d, False, "wgrad_out_odd")
    dqkv, r_w1_1, r_w2_1, r_wout_o = attn_bwd(q_rot, k_rot, qkv, cos_t, sin_a, sin_b, ao, lse, dao, batch, "attn_bwd",
                                              exchange=scatter(g_w1_1, g_w2_1, g_wout_o))
    dx2p, dg_mpre1 = norm_matmul_bwd(dqkv, win_o, x2p, gain(norm_mix_pre, 1), dx3, "in_proj_bwd_odd")
    dx2 = _sequence_order(dx2p, batch)
    g_win_o = weight_grad_stacked(h1, dqkv, 3 * d // N_CHIPS, "wgrad_in_odd")
    s_w1_1, s_w2_1, s_wout_o = (reduce_slabs(r, n) for r, n in (
        (r_w1_1, "reduce_ff1_1"), (r_w2_1, "reduce_ff2_1"), (r_wout_o, "reduce_out_odd")))
    dx1, dy0, da0, dg_fpre0, dg_fpost0, r_win_o, t_w1_1, t_w2_1, t_wout_o = ffn_bwd(
        dx2, x1, y0, a0, gain(norm_ffn_pre, 0), gain(norm_ffn_post, 0), w1_0, w2_0, "ffn_bwd_0",
        exchange=_Both(scatter(g_win_o), _Swap([s_w1_1, s_w2_1, s_wout_o])))
    g_w1_0 = weight_grad(hf0, da0, "b", d, hc, False, "wgrad_ff1_0")
    g_w2_0 = weight_grad(a0, dy0, "a", hc, d, True, "wgrad_ff2_0")
    dmix0, dmixin, dg_mpost0 = out_proj_bwd(dx1, mix0, gain(norm_mix_post, 0), wout_e, "out_proj_bwd_even")
    g_wout_e = weight_grad(mixin, dmix0, "a", d // N_CHIPS, d, False, "wgrad_out_even")
    s_win_o = reduce_slabs(r_win_o, "reduce_in_odd")
    dproj, d_lb, d_anorm, r_w1_0, t_win_o = hgrn2_bwd(
        proj, states, decays, lb_table, a_norm, dmixin, batch, "hgrn2_bwd",
        exchange=_Both(scatter(g_w1_0), _Swap([s_win_o])))
    s_w1_0 = reduce_slabs(r_w1_0, "reduce_ff1_0")
    dproj, d_lng, d_lnb, d_ws, d_bias_t, r_w2_0, t_w1_0 = gmlp_bwd(
        proj, dmixin, b_ln_g, b_ln_b, b_ws[0], bias_t, dproj, "gmlp_bwd",
        exchange=_Both(scatter(g_w2_0), _Swap([s_w1_0])))
    s_w2_0 = reduce_slabs(r_w2_0, "reduce_ff2_0")
    g_win_e, r_wout_e, t_w2_0 = weight_grad(h0, dproj, "b", d, 3 * d // N_CHIPS, False, "wgrad_in_even",
                                            exchange=_Both(scatter(g_wout_e), _Swap([s_w2_0])))
    s_wout_e = reduce_slabs(r_wout_e, "reduce_out_even")
    dx0, dg_mpre0, r_win_e, t_wout_e = norm_matmul_bwd(
        dproj, win_e, x0, gain(norm_mix_pre, 0), dx1, "in_proj_bwd_even",
        exchange=_Both(scatter(g_win_e), _Swap([s_wout_e])))
    grad_x = dx0.reshape(x.shape)
    s_win_e = reduce_slabs(r_win_e, "reduce_in_even")
    (t_win_e,) = exchange_alone(_Swap([s_win_e]), "sibling_swap")

    big_w = [w_in_even, w_out_even, w_in_odd, w_out_odd, w_ff1, w_ff2]
    big_m = [m_w_in_even, m_w_out_even, m_w_in_odd, m_w_out_odd, m_w_ff1, m_w_ff2]
    big_v = [v_w_in_even, v_w_out_even, v_w_in_odd, v_w_out_odd, v_w_ff1, v_w_ff2]
    mine = [[s_win_e], [s_wout_e], [s_win_o], [s_wout_o], [s_w1_0, s_w1_1], [s_w2_0, s_w2_1]]
    theirs = [[t_win_e], [t_wout_e], [t_win_o], [t_wout_o], [t_w1_0, t_w1_1], [t_w2_0, t_w2_1]]
    big = []
    for i, (w, m, v) in enumerate(zip(big_w, big_m, big_v)):
        two_d = (-1, w.shape[-1])
        res = adamw_big(w.reshape(two_d), mine[i], theirs[i], m.reshape(two_d), v.reshape(two_d), "adamw_big_%d" % i)
        big.append([r.reshape(w.shape) for r in res])

    small_w = [norm_mix_pre, norm_mix_post, norm_ffn_pre, norm_ffn_post, lb_table, a_norm, b_ln_g, b_ln_b, b_ws, b_bias]
    small_m = [m_norm_mix_pre, m_norm_mix_post, m_norm_ffn_pre, m_norm_ffn_post, m_lb_table, m_a_norm, m_b_ln_g,
               m_b_ln_b, m_b_ws, m_b_bias]
    small_v = [v_norm_mix_pre, v_norm_mix_post, v_norm_ffn_pre, v_norm_ffn_post, v_lb_table, v_a_norm, v_b_ln_g,
               v_b_ln_b, v_b_ws, v_b_bias]
    partial = [jnp.concatenate([dg_mpre0, dg_mpre1]), jnp.concatenate([dg_mpost0, dg_mpost1]),
               jnp.concatenate([dg_fpre0, dg_fpre1]), jnp.concatenate([dg_fpost0, dg_fpost1]),
               d_lb, d_anorm, d_lng, d_lnb, d_ws[None], d_bias_t.T[None]]
    *small_g, loss = _unpack(allreduce_small(_pack(partial + [loss_part]), "allreduce_small"),
                             [w.shape for w in small_w] + [()])
    small_d, small_nm, small_nv = adamw_small(small_w, small_g, small_m, small_v, "adamw_small")

    order = ["norm_mix_pre", "norm_mix_post", "norm_ffn_pre", "norm_ffn_post", "w_in_even", "lb_table", "a_norm",
             "b_ln_g", "b_ln_b", "b_ws", "b_bias", "w_out_even", "w_in_odd", "w_out_odd", "w_ff1", "w_ff2"]
    small_names = ["norm_mix_pre", "norm_mix_post", "norm_ffn_pre", "norm_ffn_post", "lb_table", "a_norm",
                   "b_ln_g", "b_ln_b", "b_ws", "b_bias"]
    big_names = ["w_in_even", "w_out_even", "w_in_odd", "w_out_odd", "w_ff1", "w_ff2"]
    grads, deltas, new_m, new_v = {}, {}, {}, {}
    for i, nm in enumerate(small_names):
        grads[nm], deltas[nm], new_m[nm], new_v[nm] = small_g[i], small_d[i], small_nm[i], small_nv[i]
    for i, nm in enumerate(big_names):
        grads[nm], deltas[nm], new_m[nm], new_v[nm] = big[i]
    return (loss, grad_x, *[grads[n] for n in order], *[deltas[n] for n in order],
            *[new_m[n] for n in order], *[new_v[n] for n in order])
```

```python
import functools
import math

import jax
import jax.numpy as jnp
from jax import lax
from jax.experimental import pallas as pl
from jax.experimental.pallas import tpu as pltpu

F32 = jnp.float32
BF16 = jnp.bfloat16
MESH = pl.DeviceIdType.MESH

D_MODEL = 1024
SEQ = 2048
D_FF = 4096
N_CHIPS = 4
A_WIDTH = 512
A_HEADS = 4
A_DK = 128
A_CHUNK = 64
A_SUB = 16
B_WIDTH = 512
B_GROUPS = 4
B_CHUNK = 128
C_HEADS = 16
C_HEAD_DIM = 64
C_ROT_HALF = 8
C_BLOCK = 128
C_DILATIONS = (1, 4, 16)
ROPE_THETA = 500000.0
EPS = 1e-6
ADAM_LR = 0.001
ADAM_B1 = 0.9
ADAM_B2 = 0.999
ADAM_EPS = 1e-08
ADAM_WD = 0.01
ADAM_STEP = 10

ROW_TILE = 512
FFN_ROWS = 1024
WGRAD_ROWS = 2048
VMEM_LIMIT = 56 * 1024 * 1024
NEG_BIG = -1e30


def _params(sem=None):
    return pltpu.CompilerParams(dimension_semantics=sem, vmem_limit_bytes=VMEM_LIMIT)


def _dot(a, b):
    return jnp.dot(a, b, preferred_element_type=F32)


def _dot_nt(a, b):
    return lax.dot_general(a, b, (((1,), (1,)), ((), ())), preferred_element_type=F32)


def _dot_tn(a, b):
    return lax.dot_general(a, b, (((0,), (0,)), ((), ())), preferred_element_type=F32)


def _rms(x, g):
    r = lax.rsqrt(jnp.mean(x * x, axis=-1, keepdims=True) + EPS)
    return x * r * g


def _rms_bwd(x, g, dy):
    r = lax.rsqrt(jnp.mean(x * x, axis=-1, keepdims=True) + EPS)
    xh = x * r
    dg = jnp.sum(dy * xh, axis=0, keepdims=True)
    dxh = dy * g
    dx = r * (dxh - xh * jnp.mean(dxh * xh, axis=-1, keepdims=True))
    return dx, dg


def _accumulate(ref, val, first):
    @pl.when(first)
    def _():
        ref[...] = val

    @pl.when(jnp.logical_not(first))
    def _():
        ref[...] += val


N_DEV = 8
ANY = pl.BlockSpec(memory_space=pl.ANY)


def _place():
    x, y, c = lax.axis_index("x"), lax.axis_index("y"), lax.axis_index("c")
    return x, y, c, [(1 - x, y), (x, 1 - y), (1 - x, 1 - y)]


class _Exchange:
    def __init__(self, kind, arrays):
        self.kind, self.arrays, self.n = kind, list(arrays), len(arrays)
        per_peer = pltpu.SemaphoreType.DMA((3 * self.n,))
        if kind == "gather":
            self.out_shape = [jax.ShapeDtypeStruct((N_CHIPS,) + a.shape, a.dtype) for a in self.arrays]
            self.scratch = [per_peer, per_peer, pltpu.SemaphoreType.DMA((self.n,)), per_peer, per_peer]
        else:
            self.out_shape = [jax.ShapeDtypeStruct(a.shape, a.dtype) for a in self.arrays]
            self.scratch = [per_peer, per_peer, pltpu.SemaphoreType.DMA((self.n,))]

    def _copies(self, ins, outs, sems):
        send_sems, recv_sems, local_sems = sems[:3]
        x, y, c, chips = _place()
        me = 2 * x + y
        local, remote = [], []
        for a in range(self.n):
            if self.kind == "gather":
                local.append(pltpu.make_async_copy(ins[a], outs[a].at[me], local_sems.at[a]))
                half = self.arrays[a].shape[0] // 2

                def rows(ref, core, half=half):
                    return ref.at[pl.ds(core * half, half)]
            else:
                local.append(pltpu.make_async_copy(ins[a].at[me], outs[a].at[3], local_sems.at[a]))
            for j, (px, py) in enumerate(chips):
                k = 3 * a + j
                peer = 2 * px + py

                def copy(src, dst, to, send_sem=send_sems.at[k], recv_sem=recv_sems.at[k]):
                    return pltpu.make_async_remote_copy(src_ref=src, dst_ref=dst, send_sem=send_sem, recv_sem=recv_sem,
                                                        device_id=to, device_id_type=MESH)

                if self.kind == "gather":
                    sent = copy(rows(ins[a], c), rows(outs[a].at[me], c), (px, py, c))
                    landed = copy(rows(ins[a], c), rows(outs[a].at[peer], c), (px, py, c))
                    on = dict(send_sem=sems[3].at[k], recv_sem=sems[4].at[k])
                    passed = copy(rows(outs[a].at[peer], c), rows(outs[a].at[peer], c), (x, y, 1 - c), **on)
                    handed = copy(rows(outs[a].at[peer], c), rows(outs[a].at[peer], 1 - c), (x, y, 1 - c), **on)
                    remote.append((sent, landed, passed, handed))
                else:
                    sent = copy(ins[a].at[peer], outs[a].at[j], (px, py, c))
                    remote.append((sent, sent, None, None))
        return local, remote

    def start(self, ins, outs, sems):
        local, remote = self._copies(ins, outs, sems)
        for cp in local:
            cp.start()
        for sent, _, _, _ in remote:
            sent.start()

    def finish(self, ins, outs, sems):
        local, remote = self._copies(ins, outs, sems)
        for _, landed, passed, _ in remote:
            landed.wait_recv()
            if passed is not None:
                passed.start()
        for sent, _, passed, handed in remote:
            if passed is not None:
                handed.wait_recv()
                passed.wait_send()
            sent.wait_send()
        for cp in local:
            cp.wait()


class _Swap:
    def __init__(self, arrays):
        self.arrays, self.n = list(arrays), len(arrays)
        self.out_shape = [jax.ShapeDtypeStruct(a.shape, a.dtype) for a in self.arrays]
        self.scratch = [pltpu.SemaphoreType.DMA((self.n,)), pltpu.SemaphoreType.DMA((self.n,))]

    def _copies(self, ins, outs, sems):
        x, y, c, _ = _place()
        return [pltpu.make_async_remote_copy(src_ref=ins[a], dst_ref=outs[a], send_sem=sems[0].at[a],
                                             recv_sem=sems[1].at[a], device_id=(x, y, 1 - c), device_id_type=MESH)
                for a in range(self.n)]

    def start(self, ins, outs, sems):
        for cp in self._copies(ins, outs, sems):
            cp.start()

    def finish(self, ins, outs, sems):
        for cp in self._copies(ins, outs, sems):
            cp.wait_recv()
            cp.wait_send()


class _Both:
    def __init__(self, first, second):
        self.parts = (first, second)
        self.arrays, self.n = first.arrays + second.arrays, first.n + second.n
        self.out_shape = first.out_shape + second.out_shape
        self.scratch = first.scratch + second.scratch

    def _split(self, ins, outs, sems):
        a, b = self.parts
        return ((a, ins[:a.n], outs[:a.n], sems[:len(a.scratch)]),
                (b, ins[a.n:], outs[a.n:], sems[len(a.scratch):]))

    def start(self, ins, outs, sems):
        for ex, i, o, s in self._split(ins, outs, sems):
            ex.start(i, o, s)

    def finish(self, ins, outs, sems):
        for ex, i, o, s in self._split(ins, outs, sems):
            ex.finish(i, o, s)


def _call(body, *, name, grid, in_specs, out_specs, out_shape, args, scratch_shapes=(), aliases=None, exchange=None):
    if exchange is None:
        return pl.pallas_call(
            body, name=name, grid=grid, in_specs=in_specs, out_specs=out_specs, out_shape=out_shape,
            scratch_shapes=list(scratch_shapes), input_output_aliases=aliases or {},
            compiler_params=_params(("arbitrary",) * len(grid)))(*args)
    n_in, n_out, n_scr, n_ex = len(in_specs), len(out_specs), len(scratch_shapes), exchange.n
    steps = grid

    def wrapped(*refs):
        ins, refs = refs[:n_in], refs[n_in:]
        ex_in, refs = refs[:n_ex], refs[n_ex:]
        outs, refs = refs[:n_out], refs[n_out:]
        ex_out, refs = refs[:n_ex], refs[n_ex:]
        scr, sems = refs[:n_scr], refs[n_scr:]
        first = functools.reduce(jnp.logical_and, [pl.program_id(k) == 0 for k in range(len(steps))])
        last = functools.reduce(jnp.logical_and, [pl.program_id(k) == steps[k] - 1 for k in range(len(steps))])

        @pl.when(first)
        def _():
            exchange.start(ex_in, ex_out, sems)

        body(*ins, *outs, *scr)

        @pl.when(last)
        def _():
            exchange.finish(ex_in, ex_out, sems)

    return pl.pallas_call(
        wrapped, name=name, grid=grid,
        in_specs=list(in_specs) + [ANY] * n_ex, out_specs=list(out_specs) + [ANY] * n_ex,
        out_shape=list(out_shape) + exchange.out_shape,
        scratch_shapes=list(scratch_shapes) + exchange.scratch, input_output_aliases=aliases or {},
        compiler_params=_params(("arbitrary",) * len(grid)))(*args, *exchange.arrays)


def exchange_alone(exchange, name):
    def body(*refs):
        n = exchange.n
        exchange.start(refs[:n], refs[n:2 * n], refs[2 * n:])
        exchange.finish(refs[:n], refs[n:2 * n], refs[2 * n:])

    return pl.pallas_call(
        body, name=name, in_specs=[ANY] * exchange.n, out_specs=[ANY] * exchange.n,
        out_shape=exchange.out_shape, scratch_shapes=exchange.scratch)(*exchange.arrays)


def norm_matmul(x, g, wg, name, exchange=None):
    t, d = x.shape
    nl = wg.shape[2]

    def body(x_ref, g_ref, w_ref, o_ref, h_ref):
        h = _rms(x_ref[...], g_ref[...]).astype(BF16)
        h_ref[...] = h
        for c in range(N_CHIPS):
            o_ref[:, c * nl:(c + 1) * nl] = _dot(h, w_ref[c])

    return _call(
        body, name=name, grid=(t // ROW_TILE,),
        in_specs=[pl.BlockSpec((ROW_TILE, d), lambda i: (i, 0)),
                  pl.BlockSpec((1, d), lambda i: (0, 0)),
                  pl.BlockSpec((N_CHIPS, d, nl), lambda i: (0, 0, 0))],
        out_specs=[pl.BlockSpec((ROW_TILE, N_CHIPS * nl), lambda i: (i, 0)),
                   pl.BlockSpec((ROW_TILE, d), lambda i: (i, 0))],
        out_shape=[jax.ShapeDtypeStruct((t, N_CHIPS * nl), F32), jax.ShapeDtypeStruct((t, d), BF16)],
        args=(x, g, wg), exchange=exchange)


def norm_matmul_bwd(dproj, wg, x, g, dres, name, exchange=None):
    t, d = x.shape
    nl = wg.shape[2]
    stacked = dproj.ndim == 3
    piece = math.gcd(nl, dproj.shape[-1])

    def body(dp_ref, w_ref, x_ref, g_ref, dres_ref, dx_ref, dg_ref):
        dh = None
        for j in range(N_CHIPS * nl // piece):
            c, off = divmod(j * piece, nl)
            if stacked:
                p, lo = divmod(j * piece, dproj.shape[-1])
                lhs = dp_ref[p, :, lo:lo + piece]
            else:
                lhs = dp_ref[:, j * piece:(j + 1) * piece]
            part = _dot_nt(lhs.astype(BF16), w_ref[c, :, off:off + piece])
            dh = part if dh is None else dh + part
        dx, dg = _rms_bwd(x_ref[...], g_ref[...], dh)
        dx_ref[...] = dres_ref[...] + dx
        _accumulate(dg_ref, dg, pl.program_id(0) == 0)

    row = pl.BlockSpec((ROW_TILE, d), lambda i: (i, 0))
    vec = pl.BlockSpec((1, d), lambda i: (0, 0))
    if stacked:
        dp_spec = pl.BlockSpec((dproj.shape[0], ROW_TILE, dproj.shape[-1]), lambda i: (0, i, 0))
    else:
        dp_spec = pl.BlockSpec((ROW_TILE, N_CHIPS * nl), lambda i: (i, 0))
    return _call(
        body, name=name, grid=(t // ROW_TILE,),
        in_specs=[dp_spec, pl.BlockSpec((N_CHIPS, d, nl), lambda i: (0, 0, 0)), row, vec, row],
        out_specs=[row, vec],
        out_shape=[jax.ShapeDtypeStruct((t, d), F32), jax.ShapeDtypeStruct((1, d), F32)],
        args=(dproj, wg, x, g, dres), exchange=exchange)


def out_proj(a, wg, x, g, name):
    t, d = x.shape
    kl = wg.shape[1]

    def body(a_ref, w_ref, x_ref, g_ref, mix_ref, xo_ref):
        acc = _dot(a_ref[:, 0:kl], w_ref[0])
        for c in range(1, N_CHIPS):
            acc += _dot(a_ref[:, c * kl:(c + 1) * kl], w_ref[c])
        mix_ref[...] = acc
        xo_ref[...] = x_ref[...] + _rms(acc, g_ref[...])

    row = pl.BlockSpec((ROW_TILE, d), lambda i: (i, 0))
    return pl.pallas_call(
        body, name=name, grid=(t // ROW_TILE,),
        in_specs=[row, pl.BlockSpec((N_CHIPS, kl, d), lambda i: (0, 0, 0)), row,
                  pl.BlockSpec((1, d), lambda i: (0, 0))],
        out_specs=[row, row],
        out_shape=[jax.ShapeDtypeStruct((t, d), F32), jax.ShapeDtypeStruct((t, d), F32)],
        compiler_params=_params(("arbitrary",)),
    )(a, wg, x, g)


def out_proj_bwd(dxo, mix, g, wg, name):
    t, d = mix.shape
    kl = wg.shape[1]

    def body(dxo_ref, mix_ref, g_ref, w_ref, dmix_ref, da_ref, dg_ref):
        dmix, dg = _rms_bwd(mix_ref[...], g_ref[...], dxo_ref[...])
        dmb = dmix.astype(BF16)
        dmix_ref[...] = dmb
        for c in range(N_CHIPS):
            da_ref[:, c * kl:(c + 1) * kl] = _dot_nt(dmb, w_ref[c])
        _accumulate(dg_ref, dg, pl.program_id(0) == 0)

    row = pl.BlockSpec((ROW_TILE, d), lambda i: (i, 0))
    vec = pl.BlockSpec((1, d), lambda i: (0, 0))
    return pl.pallas_call(
        body, name=name, grid=(t // ROW_TILE,),
        in_specs=[row, row, vec, pl.BlockSpec((N_CHIPS, kl, d), lambda i: (0, 0, 0))],
        out_specs=[row, row, vec],
        out_shape=[jax.ShapeDtypeStruct((t, d), BF16), jax.ShapeDtypeStruct((t, d), F32),
                   jax.ShapeDtypeStruct((1, d), F32)],
        compiler_params=_params(("arbitrary",)),
    )(dxo, mix, g, wg)


def ffn_fwd(x, gpre, w1g, w2g, gpost, name, exchange=None, target=None):
    t, d = x.shape
    hc = w1g.shape[2]
    with_loss = target is not None

    def body(x_ref, gpre_ref, w1_ref, w2_ref, gpost_ref, *rest):
        if with_loss:
            t_ref, xo_ref, h_ref, a_ref, y_ref, l_ref, acc = rest
        else:
            xo_ref, h_ref, a_ref, y_ref, acc = rest
        i, c = pl.program_id(0), pl.program_id(1)

        @pl.when(c == 0)
        def _():
            h_ref[...] = _rms(x_ref[...], gpre_ref[...]).astype(BF16)

        a = _dot(h_ref[...], w1_ref[...])
        a_ref[...] = a.astype(BF16)
        r = jnp.square(jnp.maximum(a, 0.0)).astype(BF16)
        _accumulate(acc, _dot(r, w2_ref[...]), c == 0)

        @pl.when(c == N_CHIPS - 1)
        def _():
            y = acc[...]
            y_ref[...] = y
            xo = x_ref[...] + _rms(y, gpost_ref[...])
            if with_loss:
                e = xo - t_ref[...]
                xo_ref[...] = e * (1.0 / d)
                part = jnp.sum(jnp.sum(e * e, axis=-1, keepdims=True), axis=0, keepdims=True) * (0.5 / d)
                _accumulate(l_ref, part, i == 0)
            else:
                xo_ref[...] = xo

    row = pl.BlockSpec((FFN_ROWS, d), lambda i, c: (i, 0))
    vec = pl.BlockSpec((1, d), lambda i, c: (0, 0))
    one = pl.BlockSpec((1, 1), lambda i, c: (0, 0))
    return _call(
        body, name=name, grid=(t // FFN_ROWS, N_CHIPS),
        in_specs=[row, vec,
                  pl.BlockSpec((None, d, hc), lambda i, c: (c, 0, 0)),
                  pl.BlockSpec((None, hc, d), lambda i, c: (c, 0, 0)), vec] + ([row] if with_loss else []),
        out_specs=[row, row, pl.BlockSpec((FFN_ROWS, hc), lambda i, c: (i, c)), row] + ([one] if with_loss else []),
        out_shape=[jax.ShapeDtypeStruct((t, d), F32), jax.ShapeDtypeStruct((t, d), BF16),
                   jax.ShapeDtypeStruct((t, N_CHIPS * hc), BF16), jax.ShapeDtypeStruct((t, d), F32)]
        + ([jax.ShapeDtypeStruct((1, 1), F32)] if with_loss else []),
        scratch_shapes=[pltpu.VMEM((FFN_ROWS, d), F32)],
        args=(x, gpre, w1g, w2g, gpost) + ((target,) if with_loss else ()), exchange=exchange)


def ffn_bwd(dxo, x, y, a, gpre, gpost, w1g, w2g, name, exchange=None):
    t, d = x.shape
    hc = w1g.shape[2]

    def body(dxo_ref, x_ref, y_ref, a_ref, gpre_ref, gpost_ref, w1_ref, w2_ref,
             dxi_ref, dy_ref, da_ref, dgpre_ref, dgpost_ref, acc):
        i, c = pl.program_id(0), pl.program_id(1)

        @pl.when(c == 0)
        def _():
            dy, dg = _rms_bwd(y_ref[...], gpost_ref[...], dxo_ref[...])
            dy_ref[...] = dy.astype(BF16)
            _accumulate(dgpost_ref, dg, i == 0)

        dr = _dot_nt(dy_ref[...], w2_ref[...])
        da = (dr * (2.0 * jnp.maximum(a_ref[...].astype(F32), 0.0))).astype(BF16)
        da_ref[...] = da
        _accumulate(acc, _dot_nt(da, w1_ref[...]), c == 0)

        @pl.when(c == N_CHIPS - 1)
        def _():
            dx, dg = _rms_bwd(x_ref[...], gpre_ref[...], acc[...])
            dxi_ref[...] = dxo_ref[...] + dx
            _accumulate(dgpre_ref, dg, i == 0)

    row = pl.BlockSpec((ROW_TILE, d), lambda i, c: (i, 0))
    vec = pl.BlockSpec((1, d), lambda i, c: (0, 0))
    hid = pl.BlockSpec((ROW_TILE, hc), lambda i, c: (i, c))
    return _call(
        body, name=name, grid=(t // ROW_TILE, N_CHIPS),
        in_specs=[row, row, row, hid, vec, vec,
                  pl.BlockSpec((None, d, hc), lambda i, c: (c, 0, 0)),
                  pl.BlockSpec((None, hc, d), lambda i, c: (c, 0, 0))],
        out_specs=[row, row, hid, vec, vec],
        out_shape=[jax.ShapeDtypeStruct((t, d), F32), jax.ShapeDtypeStruct((t, d), BF16),
                   jax.ShapeDtypeStruct((t, N_CHIPS * hc), BF16),
                   jax.ShapeDtypeStruct((1, d), F32), jax.ShapeDtypeStruct((1, d), F32)],
        scratch_shapes=[pltpu.VMEM((ROW_TILE, d), F32)],
        args=(dxo, x, y, a, gpre, gpost, w1g, w2g), exchange=exchange)


def weight_grad(a, b, chunked, bk, bn, relu2, name, exchange=None):
    t = a.shape[0]
    a_on = chunked == "a"
    rows = min(t, WGRAD_ROWS)
    n_steps = t // rows

    def body(a_ref, b_ref, o_ref, acc):
        s = pl.program_id(1)
        av = a_ref[...]
        if relu2:
            av = jnp.square(jnp.maximum(av.astype(F32), 0.0))
        _accumulate(acc, _dot_tn(av.astype(BF16), b_ref[...].astype(BF16)), s == 0)

        @pl.when(s == n_steps - 1)
        def _():
            o_ref[...] = acc[...].astype(BF16)

    res = _call(
        body, name=name, grid=(N_CHIPS, n_steps),
        in_specs=[pl.BlockSpec((rows, bk), (lambda c, s: (s, c)) if a_on else (lambda c, s: (s, 0))),
                  pl.BlockSpec((rows, bn), (lambda c, s: (s, 0)) if a_on else (lambda c, s: (s, c)))],
        out_specs=[pl.BlockSpec((None, bk, bn), lambda c, s: (c, 0, 0))],
        out_shape=[jax.ShapeDtypeStruct((N_CHIPS, bk, bn), BF16)],
        scratch_shapes=[pltpu.VMEM((bk, bn), F32)],
        args=(a, b), exchange=exchange)
    return res[0] if exchange is None else res


def weight_grad_stacked(a, b3, bn, name):
    t, bk = a.shape
    width = b3.shape[-1]
    piece = math.gcd(bn, width)
    rows = min(t, WGRAD_ROWS)
    n_steps = t // rows

    def body(a_ref, b_ref, o_hbm, acc, staged, sem):
        s, c = pl.program_id(0), pl.program_id(1)
        av = a_ref[...].astype(BF16)
        for chunk in range(N_CHIPS):
            @pl.when(c == chunk)
            def _(chunk=chunk):
                cols = [divmod(chunk * bn + k * piece, width) for k in range(bn // piece)]
                b = jnp.concatenate([b_ref[p, :, lo:lo + piece] for p, lo in cols], axis=1).astype(BF16)
                _accumulate(acc.at[chunk], _dot_tn(av, b), s == 0)

                @pl.when(s == n_steps - 1)
                def _():
                    staged[...] = acc[chunk].astype(BF16)
                    copy = pltpu.make_async_copy(staged, o_hbm.at[chunk], sem)
                    copy.start()
                    copy.wait()

    return pl.pallas_call(
        body, name=name, grid=(n_steps, N_CHIPS),
        in_specs=[pl.BlockSpec((rows, bk), lambda s, c: (s, 0)),
                  pl.BlockSpec((b3.shape[0], rows, width), lambda s, c: (0, s, 0))],
        out_specs=ANY,
        out_shape=jax.ShapeDtypeStruct((N_CHIPS, bk, bn), BF16),
        scratch_shapes=[pltpu.VMEM((N_CHIPS, bk, bn), F32), pltpu.VMEM((bk, bn), BF16), pltpu.SemaphoreType.DMA],
        compiler_params=_params(("arbitrary", "arbitrary")),
    )(a, b3)


def _hgrn2_chunk(st, qs, fls, ivs, gls, l0, l1, l2, ng):
    nsub = len(qs)
    mx = jnp.maximum(jnp.maximum(l0, l1), l2)
    e0, e1, e2 = jnp.exp(l0 - mx), jnp.exp(l1 - mx), jnp.exp(l2 - mx)
    lb = e0 / (e0 + e1 + e2)
    rows = lax.broadcasted_iota(jnp.int32, (A_SUB, A_SUB), 0)
    cols = lax.broadcasted_iota(jnp.int32, (A_SUB, A_SUB), 1)
    tri = (rows >= cols).astype(F32)
    keep = (lax.broadcasted_iota(jnp.int32, (A_SUB, A_SUB, A_DK), 0)
            >= lax.broadcasted_iota(jnp.int32, (A_SUB, A_SUB, A_DK), 1))
    base = jnp.zeros_like(l0)
    bases, gs, ks, qfs = [], [], [], []
    for i in range(nsub):
        f = lb + (1.0 - lb) * jax.nn.sigmoid(fls[i])
        logf = jnp.log(f)
        bases.append(base)
        gs.append(base + jnp.dot(tri, logf, precision=lax.Precision.HIGHEST, preferred_element_type=F32))
        base = base + jnp.sum(logf, axis=0, keepdims=True)
        ks.append(1.0 - f)
        qfs.append(jax.nn.silu(qs[i]))
    g_last = base
    stb = st.astype(BF16)
    outs = []
    for i in range(nsub):
        o = _dot_nt((qfs[i] * jnp.exp(gs[i])).astype(BF16), stb)
        if i > 0:
            qt = (qfs[i] * jnp.exp(gs[i] - bases[i])).astype(BF16)
            kk = jnp.concatenate([ks[j] * jnp.exp(bases[i] - gs[j]) for j in range(i)], axis=0).astype(BF16)
            vv = jnp.concatenate(ivs[:i], axis=0).astype(BF16)
            o = o + _dot(_dot_nt(qt, kk).astype(BF16), vv)
        dec = jnp.exp(jnp.where(keep, gs[i][:, None, :] - gs[i][None, :, :], NEG_BIG))
        s_diag = jnp.sum(qfs[i][:, None, :] * ks[i][None, :, :] * dec, axis=-1)
        o = o + _dot(s_diag.astype(BF16), ivs[i].astype(BF16))
        o = o * lax.rsqrt(jnp.mean(o * o, axis=-1, keepdims=True) + EPS) * ng
        outs.append(o * jax.nn.silu(gls[i]))
    kdec = jnp.concatenate([ks[j] * jnp.exp(g_last - gs[j]) for j in range(nsub)], axis=0).astype(BF16)
    vall = jnp.concatenate(ivs, axis=0).astype(BF16)
    new_st = st * jnp.exp(g_last) + _dot_tn(vall, kdec)
    return new_st, outs


A_MAX_LOG_DECAY = 60.0


def _half_sums(logf):
    n = logf.shape[0]
    first = lax.broadcasted_iota(jnp.int32, logf.shape, 0) < n // 2
    return (jnp.sum(jnp.where(first, logf, 0.0), axis=0, keepdims=True),
            jnp.sum(jnp.where(first, 0.0, logf), axis=0, keepdims=True))


def _split3(x):
    hi = x.astype(BF16)
    r1 = x - hi.astype(F32)
    mid = r1.astype(BF16)
    return hi, mid, (r1 - mid.astype(F32)).astype(BF16)


def _tri_matmul(x, transpose):
    n = x.shape[0]
    r = lax.broadcasted_iota(jnp.int32, (n, n), 0)
    c = lax.broadcasted_iota(jnp.int32, (n, n), 1)
    tri = ((r <= c) if transpose else (r >= c)).astype(BF16)
    hi, mid, lo = _split3(x)
    return (_dot(tri, lo) + _dot(tri, mid)) + _dot(tri, hi)


@jax.custom_vjp
def _cumsum_rows(x):
    return _tri_matmul(x, False)


def _cumsum_rows_fwd(x):
    return _tri_matmul(x, False), None


def _cumsum_rows_bwd(_, dy):
    return (_tri_matmul(dy, True),)


_cumsum_rows.defvjp(_cumsum_rows_fwd, _cumsum_rows_bwd)


def _lower_bound(l0, l1, l2):
    mx = jnp.maximum(jnp.maximum(l0, l1), l2)
    e0, e1, e2 = jnp.exp(l0 - mx), jnp.exp(l1 - mx), jnp.exp(l2 - mx)
    return e0 / (e0 + e1 + e2)


def _b(x):
    return x.astype(BF16)


@jax.custom_vjp
def _mm(a, b):
    return _dot(_b(a), _b(b))


_mm.defvjp(lambda a, b: (_mm(a, b), (a, b)),
           lambda res, d: (_dot_nt(_b(d), _b(res[1])), _dot_tn(_b(res[0]), _b(d))))


@jax.custom_vjp
def _mm_nt(a, b):
    return _dot_nt(_b(a), _b(b))


_mm_nt.defvjp(lambda a, b: (_mm_nt(a, b), (a, b)),
              lambda res, d: (_dot(_b(d), _b(res[1])), _dot_tn(_b(d), _b(res[0]))))


def _dot_split(dot, a, b):
    ah, bh = _b(a), _b(b)
    al, bl = _b(a - ah.astype(F32)), _b(b - bh.astype(F32))
    return (dot(ah, bl) + dot(al, bh)) + dot(ah, bh)


@jax.custom_vjp
def _mm_scores(a, b):
    return _dot_nt(_b(a), _b(b))


_mm_scores.defvjp(lambda a, b: (_mm_scores(a, b), (a, b)),
                  lambda res, d: (_dot_split(_dot, d, res[1]), _dot_split(_dot_tn, d, res[0])))


@jax.custom_vjp
def _mm_tn(a, b):
    return _dot_tn(_b(a), _b(b))


_mm_tn.defvjp(lambda a, b: (_mm_tn(a, b), (a, b)),
              lambda res, d: (_dot_nt(_b(res[1]), _b(d)), _dot(_b(res[0]), _b(d))))


@jax.custom_vjp
def _split_heads(x):
    return tuple(x[:, h * A_DK:(h + 1) * A_DK] for h in range(A_HEADS))


def _split_heads_fwd(x):
    return _split_heads(x), None


def _split_heads_bwd(_, parts):
    return (jnp.concatenate(parts, axis=1),)


_split_heads.defvjp(_split_heads_fwd, _split_heads_bwd)


def _hgrn2_chunk_fast(sts, q, fl, iv, gl, l0, l1, l2, ng):
    lb = _lower_bound(l0, l1, l2)
    f = lb + (1.0 - lb) * jax.nn.sigmoid(fl)
    return _hgrn2_fast_core(sts, q, f, jnp.log(f), iv, gl, ng)


def _hgrn2_fast_core(sts, q, f, logf, iv, gl, ng):
    g = _cumsum_rows(logf)
    g_mid, g_last = _half_sums(logf)
    g_last = g_mid + g_last
    k = 1.0 - f
    qf = jax.nn.silu(q)
    qms = _split_heads(qf * jnp.exp(g - g_mid))
    kms = _split_heads(k * jnp.exp(g_mid - g))
    qgs = _split_heads(qf * jnp.exp(g))
    kds = _split_heads(k * jnp.exp(g_last - g))
    ivs = _split_heads(iv)
    decays = _split_heads(jnp.exp(g_last))
    n = q.shape[0]
    causal = lax.broadcasted_iota(jnp.int32, (n, n), 0) >= lax.broadcasted_iota(jnp.int32, (n, n), 1)
    raw = [_mm_scores(qm, km) for qm, km in zip(qms, kms)]
    inter = [_mm_nt(qg, st) for qg, st in zip(qgs, sts)]
    scores = [jnp.where(causal, s, 0.0) for s in raw]
    os = [a + _mm(s, v) for a, s, v in zip(inter, scores, ivs)]
    new_sts = [st * d + _mm_tn(v, kd) for st, d, v, kd in zip(sts, decays, ivs, kds)]
    os = [o * lax.rsqrt(jnp.mean(o * o, axis=-1, keepdims=True) + EPS) for o in os]
    return new_sts, jnp.concatenate(os, axis=1) * ng * jax.nn.silu(gl)


A_STEP_CHUNKS = 4


def _chunk_rows(j):
    return pl.ds(pl.multiple_of(j * A_CHUNK, A_CHUNK), A_CHUNK)


def _sub_rows(j, i):
    return pl.ds(pl.multiple_of(j * A_CHUNK + i * A_SUB, A_SUB), A_SUB)


def _sub_blocks(ref, head, j):
    lanes = slice(head * A_DK, (head + 1) * A_DK)
    return [ref[_sub_rows(j, i), lanes] for i in range(A_CHUNK // A_SUB)]


def hgrn2_fwd(proj, lb_table, a_norm, batch, name, exchange=None):
    t = proj.shape[0]
    n_steps = t // batch // (A_CHUNK * A_STEP_CHUNKS)
    rows = A_CHUNK * A_STEP_CHUNKS

    def body(q_ref, f_ref, i_ref, g_ref, lb_ref, ng_ref, o_ref, st_ref, dec_ref, st):
        @pl.when(pl.program_id(1) == 0)
        def _():
            st[...] = jnp.zeros_like(st)

        def chunk(j, carry):
            r = _chunk_rows(j)
            st_ref[j] = st[...]
            lb = _lower_bound(lb_ref[0:1, :], lb_ref[1:2, :], lb_ref[2:3, :])
            f = lb + (1.0 - lb) * jax.nn.sigmoid(f_ref[r, :])
            logf = jnp.log(f)
            decay = jnp.minimum(*_half_sums(logf))
            dec_ref[j] = decay
            mild = jnp.min(decay) >= -A_MAX_LOG_DECAY

            @pl.when(mild)
            def _():
                new_sts, o = _hgrn2_fast_core([st[h] for h in range(A_HEADS)], q_ref[r, :], f, logf,
                                              i_ref[r, :], g_ref[r, :], ng_ref[...])
                for h in range(A_HEADS):
                    st[h] = new_sts[h]
                o_ref[r, :] = o.astype(BF16)

            @pl.when(jnp.logical_not(mild))
            def _():
                for h in range(A_HEADS):
                    lanes = slice(h * A_DK, (h + 1) * A_DK)
                    new_st, outs = _hgrn2_chunk(
                        st[h], _sub_blocks(q_ref, h, j), _sub_blocks(f_ref, h, j), _sub_blocks(i_ref, h, j),
                        _sub_blocks(g_ref, h, j), lb_ref[0:1, lanes], lb_ref[1:2, lanes], lb_ref[2:3, lanes],
                        ng_ref[:, lanes])
                    st[h] = new_st
                    for i, o in enumerate(outs):
                        o_ref[_sub_rows(j, i), lanes] = o.astype(BF16)

            return carry

        lax.fori_loop(0, A_STEP_CHUNKS, chunk, 0)

    def part(k):
        return pl.BlockSpec((rows, A_WIDTH), lambda b, n: (b * n_steps + n, k))

    return _call(
        body, name=name, grid=(batch, n_steps),
        in_specs=[part(0), part(1), part(2), part(3),
                  pl.BlockSpec((3, A_WIDTH), lambda b, n: (0, 0)), pl.BlockSpec((1, A_WIDTH), lambda b, n: (0, 0))],
        out_specs=[part(0),
                   pl.BlockSpec((A_STEP_CHUNKS, A_HEADS, A_DK, A_DK), lambda b, n: (b * n_steps + n, 0, 0, 0)),
                   pl.BlockSpec((A_STEP_CHUNKS, 1, A_WIDTH), lambda b, n: (b * n_steps + n, 0, 0))],
        out_shape=[jax.ShapeDtypeStruct((t, A_WIDTH), BF16),
                   jax.ShapeDtypeStruct((t // A_CHUNK, A_HEADS, A_DK, A_DK), F32),
                   jax.ShapeDtypeStruct((t // A_CHUNK, 1, A_WIDTH), F32)],
        scratch_shapes=[pltpu.VMEM((A_HEADS, A_DK, A_DK), F32)],
        args=(proj, proj, proj, proj, lb_table, a_norm), exchange=exchange)


def hgrn2_bwd(proj, states, decays, lb_table, a_norm, do, batch, name, exchange=None):
    t = proj.shape[0]
    n_steps = t // batch // (A_CHUNK * A_STEP_CHUNKS)
    rows = A_CHUNK * A_STEP_CHUNKS

    def body(q_ref, f_ref, i_ref, g_ref, st_ref, dec_ref, lb_ref, ng_ref, do_ref, dp_ref, dlb_ref, dng_ref, dst):
        @pl.when(jnp.logical_and(pl.program_id(0) == 0, pl.program_id(1) == 0))
        def _():
            dlb_ref[...] = jnp.zeros_like(dlb_ref)
            dng_ref[...] = jnp.zeros_like(dng_ref)

        @pl.when(pl.program_id(1) == 0)
        def _():
            dst[...] = jnp.zeros_like(dst)

        def chunk(jj, carry):
            j = A_STEP_CHUNKS - 1 - jj
            r = _chunk_rows(j)
            mild = jnp.min(dec_ref[j]) >= -A_MAX_LOG_DECAY

            @pl.when(mild)
            def _():
                _, vjp = jax.vjp(
                    _hgrn2_chunk_fast, [st_ref[j, h] for h in range(A_HEADS)], q_ref[r, :], f_ref[r, :],
                    i_ref[r, :], g_ref[r, :], lb_ref[0:1, :], lb_ref[1:2, :], lb_ref[2:3, :], ng_ref[...])
                d_sts, dq, df, di, dg, dl0, dl1, dl2, dng = vjp(
                    ([dst[h] for h in range(A_HEADS)], do_ref[r, :].astype(F32)))
                for h in range(A_HEADS):
                    dst[h] = d_sts[h]
                for k, part in enumerate((dq, df, di, dg)):
                    dp_ref[r, k * A_WIDTH:(k + 1) * A_WIDTH] = part
                for row, val in enumerate((dl0, dl1, dl2)):
                    dlb_ref[row:row + 1, :] += val
                dng_ref[...] += dng

            @pl.when(jnp.logical_not(mild))
            def _():
                for h in range(A_HEADS):
                    lanes = slice(h * A_DK, (h + 1) * A_DK)
                    _, vjp = jax.vjp(
                        _hgrn2_chunk, st_ref[j, h], _sub_blocks(q_ref, h, j), _sub_blocks(f_ref, h, j),
                        _sub_blocks(i_ref, h, j), _sub_blocks(g_ref, h, j), lb_ref[0:1, lanes], lb_ref[1:2, lanes],
                        lb_ref[2:3, lanes], ng_ref[:, lanes])
                    douts = [x.astype(F32) for x in _sub_blocks(do_ref, h, j)]
                    d_st, dqs, dfs, dis, dgs, dl0, dl1, dl2, dng = vjp((dst[h], douts))
                    dst[h] = d_st
                    for k, parts in enumerate((dqs, dfs, dis, dgs)):
                        for i in range(A_CHUNK // A_SUB):
                            dp_ref[_sub_rows(j, i), k * A_WIDTH + h * A_DK:k * A_WIDTH + (h + 1) * A_DK] = parts[i]
                    for row, val in enumerate((dl0, dl1, dl2)):
                        dlb_ref[row:row + 1, lanes] += val
                    dng_ref[:, lanes] += dng

            return carry

        lax.fori_loop(0, A_STEP_CHUNKS, chunk, 0)

    def rev(b, n):
        return b * n_steps + (n_steps - 1 - n)

    def part(k):
        return pl.BlockSpec((rows, A_WIDTH), lambda b, n: (rev(b, n), k))

    const3 = pl.BlockSpec((3, A_WIDTH), lambda b, n: (0, 0))
    const1 = pl.BlockSpec((1, A_WIDTH), lambda b, n: (0, 0))
    return _call(
        body, name=name, grid=(batch, n_steps),
        in_specs=[part(0), part(1), part(2), part(3),
                  pl.BlockSpec((A_STEP_CHUNKS, A_HEADS, A_DK, A_DK), lambda b, n: (rev(b, n), 0, 0, 0)),
                  pl.BlockSpec((A_STEP_CHUNKS, 1, A_WIDTH), lambda b, n: (rev(b, n), 0, 0)),
                  const3, const1, part(0)],
        out_specs=[pl.BlockSpec((rows, 4 * A_WIDTH), lambda b, n: (rev(b, n), 0)), const3, const1],
        out_shape=[jax.ShapeDtypeStruct((t, 4 * A_WIDTH + 2 * B_WIDTH), F32),
                   jax.ShapeDtypeStruct((3, A_WIDTH), F32), jax.ShapeDtypeStruct((1, A_WIDTH), F32)],
        scratch_shapes=[pltpu.VMEM((A_HEADS, A_DK, A_DK), F32)],
        args=(proj, proj, proj, proj, states, decays, lb_table, a_norm, do), exchange=exchange)


B_GDIM = B_WIDTH // B_GROUPS
B_ROWS = 512


def _gmlp_chunk(ubs, vbs, lngs, lnbs, ws, bcols):
    vs = [jax.nn.gelu(v) for v in vbs]
    mu = sum(jnp.sum(v, axis=-1, keepdims=True) for v in vs) * (1.0 / B_WIDTH)
    var = sum(jnp.sum(jnp.square(v - mu), axis=-1, keepdims=True) for v in vs) * (1.0 / B_WIDTH)
    rstd = lax.rsqrt(var + EPS)
    tril = (lax.broadcasted_iota(jnp.int32, (B_CHUNK, B_CHUNK), 0)
            >= lax.broadcasted_iota(jnp.int32, (B_CHUNK, B_CHUNK), 1))
    outs = []
    for g in range(B_GROUPS):
        vn = (vs[g] - mu) * rstd * lngs[g] + lnbs[g]
        w = jnp.where(tril, ws[g], 0.0).astype(BF16)
        outs.append(jax.nn.gelu(ubs[g]) * (_dot(w, vn.astype(BF16)) + bcols[g]))
    return outs


def _gmlp_args(u_ref, v_ref, lng_ref, lnb_ref, w_ref, bt_ref, rows):
    def groups(ref):
        return [ref[rows, g * B_GDIM:(g + 1) * B_GDIM] for g in range(B_GROUPS)]

    def vec(ref):
        return [ref[:, g * B_GDIM:(g + 1) * B_GDIM] for g in range(B_GROUPS)]

    return (groups(u_ref), groups(v_ref), vec(lng_ref), vec(lnb_ref),
            [w_ref[g] for g in range(B_GROUPS)], [bt_ref[:, g:g + 1] for g in range(B_GROUPS)])


def gmlp_fwd(proj, oa, ln_g, ln_b, w, bias_t, name, exchange=None):
    t = proj.shape[0]

    def body(u_ref, v_ref, oa_ref, lng_ref, lnb_ref, w_ref, bt_ref, o_ref):
        o_ref[:, 0:A_WIDTH] = oa_ref[...]
        for n in range(B_ROWS // B_CHUNK):
            rows = slice(n * B_CHUNK, (n + 1) * B_CHUNK)
            outs = _gmlp_chunk(*_gmlp_args(u_ref, v_ref, lng_ref, lnb_ref, w_ref, bt_ref, rows))
            for g, o in enumerate(outs):
                o_ref[rows, A_WIDTH + g * B_GDIM:A_WIDTH + (g + 1) * B_GDIM] = o.astype(BF16)

    vec = pl.BlockSpec((1, B_WIDTH), lambda i: (0, 0))
    return _call(
        body, name=name, grid=(t // B_ROWS,),
        in_specs=[pl.BlockSpec((B_ROWS, B_WIDTH), lambda i: (i, 4)), pl.BlockSpec((B_ROWS, B_WIDTH), lambda i: (i, 5)),
                  pl.BlockSpec((B_ROWS, A_WIDTH), lambda i: (i, 0)), vec, vec,
                  pl.BlockSpec((B_GROUPS, B_CHUNK, B_CHUNK), lambda i: (0, 0, 0)),
                  pl.BlockSpec((B_CHUNK, B_GROUPS), lambda i: (0, 0))],
        out_specs=[pl.BlockSpec((B_ROWS, A_WIDTH + B_WIDTH), lambda i: (i, 0))],
        out_shape=[jax.ShapeDtypeStruct((t, A_WIDTH + B_WIDTH), BF16)],
        args=(proj, proj, oa, ln_g, ln_b, w, bias_t), exchange=exchange)


def gmlp_bwd(proj, dmixin, ln_g, ln_b, w, bias_t, dproj, name, exchange=None):
    t = proj.shape[0]

    def body(u_ref, v_ref, do_ref, lng_ref, lnb_ref, w_ref, bt_ref, dp_in_ref,
             dp_ref, dlng_ref, dlnb_ref, dw_ref, dbt_ref):
        del dp_in_ref

        @pl.when(pl.program_id(0) == 0)
        def _():
            for ref in (dlng_ref, dlnb_ref, dw_ref, dbt_ref):
                ref[...] = jnp.zeros_like(ref)

        for n in range(B_ROWS // B_CHUNK):
            rows = slice(n * B_CHUNK, (n + 1) * B_CHUNK)
            _, vjp = jax.vjp(_gmlp_chunk, *_gmlp_args(u_ref, v_ref, lng_ref, lnb_ref, w_ref, bt_ref, rows))
            douts = [do_ref[rows, g * B_GDIM:(g + 1) * B_GDIM] for g in range(B_GROUPS)]
            dus, dvs, dlngs, dlnbs, dws, dbs = vjp(douts)
            for g in range(B_GROUPS):
                lanes = slice(g * B_GDIM, (g + 1) * B_GDIM)
                dp_ref[rows, lanes] = dus[g]
                dp_ref[rows, B_WIDTH + g * B_GDIM:B_WIDTH + (g + 1) * B_GDIM] = dvs[g]
                dlng_ref[:, lanes] += dlngs[g]
                dlnb_ref[:, lanes] += dlnbs[g]
                dw_ref[g] += dws[g]
                dbt_ref[:, g:g + 1] += dbs[g]

    vec = pl.BlockSpec((1, B_WIDTH), lambda i: (0, 0))
    wspec = pl.BlockSpec((B_GROUPS, B_CHUNK, B_CHUNK), lambda i: (0, 0, 0))
    bspec = pl.BlockSpec((B_CHUNK, B_GROUPS), lambda i: (0, 0))
    return _call(
        body, name=name, grid=(t // B_ROWS,),
        in_specs=[pl.BlockSpec((B_ROWS, B_WIDTH), lambda i: (i, 4)), pl.BlockSpec((B_ROWS, B_WIDTH), lambda i: (i, 5)),
                  pl.BlockSpec((B_ROWS, B_WIDTH), lambda i: (i, 1)), vec, vec, wspec, bspec,
                  pl.BlockSpec(memory_space=pl.ANY)],
        out_specs=[pl.BlockSpec((B_ROWS, 2 * B_WIDTH), lambda i: (i, 2)), vec, vec, wspec, bspec],
        out_shape=[jax.ShapeDtypeStruct(dproj.shape, F32), jax.ShapeDtypeStruct((1, B_WIDTH), F32),
                   jax.ShapeDtypeStruct((1, B_WIDTH), F32), jax.ShapeDtypeStruct((B_GROUPS, B_CHUNK, B_CHUNK), F32),
                   jax.ShapeDtypeStruct((B_CHUNK, B_GROUPS), F32)],
        aliases={7: 0}, args=(proj, proj, dmixin, ln_g, ln_b, w, bias_t, dproj), exchange=exchange)


C_FWD_BLOCKS = 16
C_BWD_BLOCKS = 8
C_PAIR = 2 * C_HEAD_DIM
C_PAIRS = C_HEADS // 2
C_SCALE = 1.0 / math.sqrt(C_HEAD_DIM)
C_ROT_DIM = 2 * C_ROT_HALF
ROPE_ROWS = 1024


def rope_tables(pos_col, name):
    t = pos_col.shape[0]

    def body(p_ref, c_ref, a_ref, b_ref):
        lane = jnp.bitwise_and(lax.broadcasted_iota(jnp.int32, (1, C_PAIR), 1), C_HEAD_DIM - 1)
        j = jnp.bitwise_and(lane, C_ROT_HALF - 1).astype(F32)
        inv = jnp.exp(j * (-math.log(ROPE_THETA) / C_ROT_HALF))
        ang = p_ref[...].astype(F32) * inv
        cos, sin = jnp.cos(ang), jnp.sin(ang)
        c_ref[...] = jnp.where(lane < C_ROT_DIM, cos, 1.0)
        a_ref[...] = jnp.where(lane < C_ROT_HALF, -sin, 0.0)
        b_ref[...] = jnp.where(jnp.logical_and(lane >= C_ROT_HALF, lane < C_ROT_DIM), sin, 0.0)

    tab = pl.BlockSpec((ROPE_ROWS, C_PAIR), lambda i: (i, 0))
    return pl.pallas_call(
        body, name=name, grid=(t // ROPE_ROWS,),
        in_specs=[pl.BlockSpec((ROPE_ROWS, 1), lambda i: (i, 0))],
        out_specs=[tab, tab, tab],
        out_shape=[jax.ShapeDtypeStruct((t, C_PAIR), F32)] * 3,
        compiler_params=_params(("arbitrary",)),
    )(pos_col)


def _rope(x, c, a, b):
    return x * c + pltpu.roll(x, C_PAIR - C_ROT_HALF, 1) * a + pltpu.roll(x, C_ROT_HALF, 1) * b


def _rope_t(d, c, a, b):
    return d * c + pltpu.roll(d * a, C_ROT_HALF, 1) + pltpu.roll(d * b, C_PAIR - C_ROT_HALF, 1)


C_RES = 16


def _residue_major(a, batch):
    return a.reshape(batch, SEQ // C_RES, C_RES, -1).transpose(0, 2, 1, 3).reshape(a.shape)


def _sequence_order(a, batch):
    return a.reshape(batch, C_RES, SEQ // C_RES, -1).transpose(0, 2, 1, 3).reshape(a.shape)


def _block_pieces(idx, dil):
    nblk = SEQ // dil // C_BLOCK
    r, n = idx // nblk, idx % nblk
    per = C_RES // dil
    size = C_BLOCK // per

    def pieces(blk):
        return [((dil * a + r) * (SEQ // C_RES) + size * blk, size) for a in range(per)]

    return pieces(n), pieces(jnp.maximum(n - 1, 0)), n > 0


def _get_rows(ref, pieces):
    return jnp.concatenate([ref[pl.ds(pl.multiple_of(start, 8), size), :] for start, size in pieces], axis=0)


def _set_rows(ref, pieces, val, add=False):
    for k, (start, size) in enumerate(pieces):
        rows = pl.ds(pl.multiple_of(start, 8), size)
        part = val[k * size:(k + 1) * size]
        ref[rows, :] = ref[rows, :] + part if add else part


def _head_masks():
    low = lax.broadcasted_iota(jnp.int32, (1, C_PAIR), 1) < C_HEAD_DIM
    return low, jnp.logical_not(low)


def _attn_mask(has_prev, dil):
    per = C_RES // dil
    size = C_BLOCK // per

    def position(x):
        x = jnp.bitwise_and(x, C_BLOCK - 1)
        return per * jnp.bitwise_and(x, size - 1) + x // size

    j = lax.broadcasted_iota(jnp.int32, (2 * C_BLOCK, 2 * C_BLOCK), 1)
    pi = position(lax.broadcasted_iota(jnp.int32, (2 * C_BLOCK, 2 * C_BLOCK), 0))
    pj = position(j)
    own = j < C_BLOCK
    return jnp.logical_or(jnp.logical_and(own, pj <= pi),
                          jnp.logical_and(jnp.logical_and(jnp.logical_not(own), pj >= pi), has_prev))


def _stack_heads(x):
    low, high = _head_masks()
    return jnp.concatenate([jnp.where(low, x, 0.0), jnp.where(high, x, 0.0)], axis=0)


def _unstack_heads(x):
    low, _ = _head_masks()
    return jnp.where(low, x[:C_BLOCK], x[C_BLOCK:])


def attn_fwd(qkv, cos_t, sin_a, sin_b, batch, name, exchange=None):
    t = qkv.shape[0]
    nbr = len(C_DILATIONS)

    def body(q_ref, k_ref, v_ref, c_ref, a_ref, b_ref, o_ref, l_ref, qr_ref, kr_ref, qs, ks, *stats):
        acc, mm, dd = stats[0:nbr], stats[nbr:2 * nbr], stats[2 * nbr:3 * nbr]
        c, a, b = c_ref[...], a_ref[...], b_ref[...]
        qs[...] = _rope(q_ref[...], c, a, b) * C_SCALE
        ks[...] = _rope(k_ref[...], c, a, b)
        qr_ref[...] = qs[...].astype(BF16)
        kr_ref[...] = ks[...].astype(BF16)

        def load(idx, dil):
            own, prev, has_prev = _block_pieces(idx, dil)
            return own, (has_prev, _get_rows(qs, own), _get_rows(ks, own), _get_rows(ks, prev),
                         _get_rows(v_ref, own), _get_rows(v_ref, prev))

        def scores(dil, has_prev, q, k_own, k_prev, v_own, v_prev):
            k_cat = jnp.concatenate([k_own, k_prev], axis=0).astype(BF16)
            return jnp.where(_attn_mask(has_prev, dil), _dot_nt(_stack_heads(q).astype(BF16), k_cat), NEG_BIG)

        def softmax(s):
            m = jnp.max(s, axis=-1, keepdims=True)
            p = jnp.exp(s - m)
            return p.astype(BF16), m, jnp.sum(p, axis=-1, keepdims=True)

        def values(pb, has_prev, q, k_own, k_prev, v_own, v_prev):
            low, high = _head_masks()
            v_cat = jnp.concatenate([v_own, v_prev], axis=0)
            p_wide = jnp.concatenate([pb[:C_BLOCK], pb[C_BLOCK:]], axis=1)
            v_tall = jnp.concatenate([jnp.where(low, v_cat, 0.0), jnp.where(high, v_cat, 0.0)], axis=0).astype(BF16)
            return _dot(p_wide, v_tall)

        for bi, dil in enumerate(C_DILATIONS):
            def pair(i, carry, bi=bi, dil=dil):
                low, _ = _head_masks()
                loaded = [load(C_FWD_BLOCKS * i + k, dil) for k in range(C_FWD_BLOCKS)]
                ss = [scores(dil, *ops) for _, ops in loaded]
                sm = [softmax(s) for s in ss]
                pvs = [values(pb, *ops) for (pb, _, _), (_, ops) in zip(sm, loaded)]
                for (own, _), (_, m, den), pv in zip(loaded, sm, pvs):
                    _set_rows(acc[bi], own, pv)
                    _set_rows(mm[bi], own, jnp.where(low, m[:C_BLOCK], m[C_BLOCK:]))
                    _set_rows(dd[bi], own, jnp.where(low, den[:C_BLOCK], den[C_BLOCK:]))
                return carry

            lax.fori_loop(0, SEQ // C_BLOCK // C_FWD_BLOCKS, pair, 0)
        step = 2 * C_BLOCK
        for r0 in range(0, SEQ, step):
            rr = slice(r0, r0 + step)
            ms = [mm[g][rr, :] for g in range(nbr)]
            m_all = functools.reduce(jnp.maximum, ms)
            ws = [jnp.exp(m - m_all) for m in ms]
            num = sum(acc[g][rr, :] * ws[g] for g in range(nbr))
            den = sum(dd[g][rr, :] * ws[g] for g in range(nbr))
            o_ref[rr, :] = (num / den).astype(BF16)
            l_ref[rr, :] = m_all + jnp.log(den)

    def col(k):
        return pl.BlockSpec((SEQ, C_PAIR), lambda b, p: (b, k * C_PAIRS + p))

    tab = pl.BlockSpec((SEQ, C_PAIR), lambda b, p: (b, 0))
    return _call(
        body, name=name, grid=(batch, C_PAIRS),
        in_specs=[col(0), col(1), col(2), tab, tab, tab],
        out_specs=[col(0), col(0), col(0), col(0)],
        out_shape=[jax.ShapeDtypeStruct((t, D_MODEL), BF16), jax.ShapeDtypeStruct((t, D_MODEL), F32),
                   jax.ShapeDtypeStruct((t, D_MODEL), BF16), jax.ShapeDtypeStruct((t, D_MODEL), BF16)],
        scratch_shapes=[pltpu.VMEM((SEQ, C_PAIR), F32)] * (2 + 3 * nbr),
        args=(qkv, qkv, qkv, cos_t, sin_a, sin_b), exchange=exchange)


def attn_bwd(qr, kr, qkv, cos_t, sin_a, sin_b, o, lse, do, batch, name, exchange=None):
    t = qkv.shape[0]

    def body(q_ref, k_ref, v_ref, c_ref, a_ref, b_ref, o_ref, l_ref, do_ref, dqkv_ref, qs, ks, dqs, dks, dvs, dlt):
        low, _ = _head_masks()
        c, a, b = c_ref[...], a_ref[...], b_ref[...]
        qs[...] = q_ref[...].astype(F32)
        ks[...] = k_ref[...].astype(F32)
        prod = do_ref[...] * o_ref[...].astype(F32)
        s_low = jnp.sum(jnp.where(low, prod, 0.0), axis=-1, keepdims=True)
        s_all = jnp.sum(prod, axis=-1, keepdims=True)
        dlt[...] = jnp.where(low, s_low, s_all - s_low)
        dqs[...] = jnp.zeros_like(dqs)
        dks[...] = jnp.zeros_like(dks)
        dvs[...] = jnp.zeros_like(dvs)

        def load(idx, dil):
            own, prev, has_prev = _block_pieces(idx, dil)
            return (own, prev), (has_prev, _get_rows(qs, own), _get_rows(do_ref, own), _get_rows(ks, own),
                                 _get_rows(ks, prev), _get_rows(v_ref, own), _get_rows(v_ref, prev),
                                 _get_rows(l_ref, own), _get_rows(dlt, own))

        def operands(dil, has_prev, q, do, k_own, k_prev, v_own, v_prev, l_full, d_full):
            lcol = jnp.concatenate([l_full[:, 0:1], l_full[:, C_HEAD_DIM:C_HEAD_DIM + 1]], axis=0)
            dcol = jnp.concatenate([d_full[:, 0:1], d_full[:, C_HEAD_DIM:C_HEAD_DIM + 1]], axis=0)
            return (_stack_heads(q).astype(BF16), _stack_heads(do).astype(BF16),
                    jnp.concatenate([k_own, k_prev], axis=0).astype(BF16),
                    jnp.concatenate([v_own, v_prev], axis=0).astype(BF16), lcol, dcol, _attn_mask(has_prev, dil))

        for dil in C_DILATIONS:
            def pair(i, carry, dil=dil):
                loaded = [load(C_BWD_BLOCKS * i + k, dil) for k in range(C_BWD_BLOCKS)]
                ops = [operands(dil, *o) for _, o in loaded]
                ss = [_dot_nt(q_stack, k_cat) for q_stack, _, k_cat, _, _, _, _ in ops]
                dps = [_dot_nt(do_stack, v_cat) for _, do_stack, _, v_cat, _, _, _ in ops]
                ps = [jnp.exp(jnp.where(o[6], s, NEG_BIG) - o[4]) for s, o in zip(ss, ops)]
                dss = [(p * (dp - o[5])).astype(BF16) for p, dp, o in zip(ps, dps, ops)]
                dvs_ = [_dot_tn(p.astype(BF16), o[1]) for p, o in zip(ps, ops)]
                dks_ = [_dot_tn(ds, o[0]) for ds, o in zip(dss, ops)]
                dqs_ = [_unstack_heads(_dot(ds, o[2])) for ds, o in zip(dss, ops)]
                for ((own, prev), _), dq, dk_cat, dv_cat in zip(loaded, dqs_, dks_, dvs_):
                    _set_rows(dqs, own, dq, add=True)
                    _set_rows(dks, own, dk_cat[:C_BLOCK], add=True)
                    _set_rows(dvs, own, dv_cat[:C_BLOCK], add=True)
                    _set_rows(dks, prev, dk_cat[C_BLOCK:], add=True)
                    _set_rows(dvs, prev, dv_cat[C_BLOCK:], add=True)
                return carry

            lax.fori_loop(0, SEQ // C_BLOCK // C_BWD_BLOCKS, pair, 0)
        dqkv_ref[0] = _rope_t(dqs[...] * C_SCALE, c, a, b).astype(BF16)
        dqkv_ref[1] = _rope_t(dks[...], c, a, b).astype(BF16)
        dqkv_ref[2] = dvs[...].astype(BF16)

    def col(k):
        return pl.BlockSpec((SEQ, C_PAIR), lambda b, p: (b, k * C_PAIRS + p))

    tab = pl.BlockSpec((SEQ, C_PAIR), lambda b, p: (b, 0))
    return _call(
        body, name=name, grid=(batch, C_PAIRS),
        in_specs=[col(0), col(0), col(2), tab, tab, tab, col(0), col(0), col(0)],
        out_specs=[pl.BlockSpec((3, SEQ, C_PAIR), lambda b, p: (0, b, p))],
        out_shape=[jax.ShapeDtypeStruct((3, t, D_MODEL), BF16)],
        scratch_shapes=[pltpu.VMEM((SEQ, C_PAIR), F32)] * 6,
        args=(qr, kr, qkv, cos_t, sin_a, sin_b, o, lse, do), exchange=exchange)


def allreduce_small(slab, name):
    rows, lanes = slab.shape

    def body(x_ref, out_ref, gath, send_sems, recv_sems, local_sem):
        x, y, c, chips = _place()
        me, sibling = (x, y, c), (x, y, 1 - c)

        def slot(px, py, pc):
            return gath.at[4 * px + 2 * py + pc]

        def copy(k, block, to, src=None):
            return pltpu.make_async_remote_copy(
                src_ref=slot(*block) if src is None else src, dst_ref=slot(*block),
                send_sem=send_sems.at[k], recv_sem=recv_sems.at[k], device_id=to, device_id_type=MESH)

        mine = pltpu.make_async_copy(x_ref, slot(*me), local_sem)
        mine.start()
        first = [copy(0, me, sibling, src=x_ref)]
        first += [copy(1 + j, me, (*chip, c), src=x_ref) for j, chip in enumerate(chips)]
        for cp in first:
            cp.start()
        passed = [copy(4 + j, (*chip, c), sibling) for j, chip in enumerate(chips)]
        for j, chip in enumerate(chips):
            copy(1 + j, (*chip, c), me).wait_recv()
            passed[j].start()
        copy(0, sibling, me).wait_recv()
        for j, chip in enumerate(chips):
            copy(4 + j, (*chip, 1 - c), me).wait_recv()
        for cp in first + passed:
            cp.wait_send()
        mine.wait()
        total = gath[0]
        for d in range(1, N_DEV):
            total = total + gath[d]
        out_ref[...] = total

    return pl.pallas_call(
        body, name=name,
        in_specs=[pl.BlockSpec(memory_space=pltpu.VMEM)],
        out_specs=pl.BlockSpec(memory_space=pltpu.VMEM),
        out_shape=jax.ShapeDtypeStruct((rows, lanes), F32),
        scratch_shapes=[pltpu.VMEM((N_DEV, rows, lanes), F32),
                        pltpu.SemaphoreType.DMA((7,)), pltpu.SemaphoreType.DMA((7,)), pltpu.SemaphoreType.DMA],
    )(slab)


ELT_ROWS = 512


def reduce_slabs(r, name):
    r = r.reshape(N_CHIPS, -1, r.shape[-1])
    _, rows, cols = r.shape
    br = min(rows, ELT_ROWS)

    def body(r_ref, o_ref):
        o_ref[...] = ((r_ref[3].astype(F32) + r_ref[0].astype(F32)) + r_ref[1].astype(F32)) + r_ref[2].astype(F32)

    return pl.pallas_call(
        body, name=name, grid=(rows // br,),
        in_specs=[pl.BlockSpec((N_CHIPS, br, cols), lambda i: (0, i, 0))],
        out_specs=pl.BlockSpec((br, cols), lambda i: (i, 0)),
        out_shape=jax.ShapeDtypeStruct((rows, cols), F32),
        compiler_params=_params(("arbitrary",)),
    )(r)


def _adamw(w, g, m, v):
    m = ADAM_B1 * m + (1.0 - ADAM_B1) * g
    v = ADAM_B2 * v + (1.0 - ADAM_B2) * jnp.square(g)
    m_hat = m / (1.0 - ADAM_B1 ** ADAM_STEP)
    v_hat = v / (1.0 - ADAM_B2 ** ADAM_STEP)
    delta = -ADAM_LR * (m_hat / (jnp.sqrt(v_hat) + ADAM_EPS) + ADAM_WD * w)
    return delta, m, v


def adamw_big(w, s_mine, s_sibling, m, v, name):
    rows, cols = w.shape
    parts = len(s_mine)
    br = min(rows // parts, ELT_ROWS)
    nb = rows // parts // br

    def body(w_ref, m_ref, v_ref, *rest):
        sums, (g_out, d_out, m_out, v_out) = rest[:2 * parts], rest[2 * parts:]
        p = pl.program_id(0)
        g = sums[0][...] + sums[parts][...]
        for k in range(1, parts):
            g = jnp.where(p == k, sums[k][...] + sums[parts + k][...], g)
        g_out[...] = g
        d_out[...], m_out[...], v_out[...] = _adamw(w_ref[...], g, m_ref[...], v_ref[...])

    def part_spec(k):
        return pl.BlockSpec((br, cols), lambda p, i: (jnp.where(p == k, i, jnp.where(p < k, 0, nb - 1)), 0))

    blk = pl.BlockSpec((br, cols), lambda p, i: (p * nb + i, 0))
    out = jax.ShapeDtypeStruct((rows, cols), F32)
    return pl.pallas_call(
        body, name=name, grid=(parts, nb),
        in_specs=[blk] * 3 + [part_spec(k) for k in range(parts)] * 2, out_specs=[blk] * 4, out_shape=[out] * 4,
        compiler_params=_params(("arbitrary", "arbitrary")),
    )(w, m, v, *s_mine, *s_sibling)


def adamw_small(ws, gs, ms, vs, name):
    n = len(ws)

    def body(*refs):
        w_refs, g_refs, m_refs, v_refs = (refs[k * n:(k + 1) * n] for k in range(4))
        d_out, m_out, v_out = (refs[(4 + k) * n:(5 + k) * n] for k in range(3))
        for i in range(n):
            d_out[i][...], m_out[i][...], v_out[i][...] = _adamw(
                w_refs[i][...], g_refs[i][...], m_refs[i][...], v_refs[i][...])

    outs = [jax.ShapeDtypeStruct(w.shape, F32) for w in ws]
    res = pl.pallas_call(body, name=name, out_shape=outs * 3)(*ws, *gs, *ms, *vs)
    return res[:n], res[n:2 * n], res[2 * n:]


SLAB_LANES = 128
SLAB_ROW_ALIGN = 8


def _pack(parts):
    flat = jnp.concatenate([p.reshape(-1) for p in parts])
    rows = -(-flat.shape[0] // (SLAB_LANES * SLAB_ROW_ALIGN)) * SLAB_ROW_ALIGN
    flat = jnp.pad(flat, (0, rows * SLAB_LANES - flat.shape[0]))
    return flat.reshape(rows, SLAB_LANES)


def _unpack(slab, shapes):
    flat = slab.reshape(-1)
    out, pos = [], 0
    for s in shapes:
        size = math.prod(s)
        out.append(flat[pos:pos + size].reshape(s))
        pos += size
    return out


def kernel(x, positions, norm_mix_pre, norm_mix_post, norm_ffn_pre, norm_ffn_post, w_in_even, lb_table, a_norm, b_ln_g, b_ln_b, b_ws, b_bias, w_out_even, w_in_odd, w_out_odd, w_ff1, w_ff2, loss_target, m_norm_mix_pre, m_norm_mix_post, m_norm_ffn_pre, m_norm_ffn_post, m_w_in_even, m_lb_table, m_a_norm, m_b_ln_g, m_b_ln_b, m_b_ws, m_b_bias, m_w_out_even, m_w_in_odd, m_w_out_odd, m_w_ff1, m_w_ff2, v_norm_mix_pre, v_norm_mix_post, v_norm_ffn_pre, v_norm_ffn_post, v_w_in_even, v_lb_table, v_a_norm, v_b_ln_g, v_b_ln_b, v_b_ws, v_b_bias, v_w_out_even, v_w_in_odd, v_w_out_odd, v_w_ff1, v_w_ff2):
    batch = x.shape[0]
    t = batch * SEQ
    d = D_MODEL
    x0 = x.reshape(t, d)
    target = loss_target.reshape(t, d)

    def gain(p, layer):
        return p[layer:layer + 1]

    def gather(*shards):
        return _Exchange("gather", [w.astype(BF16) for w in shards])

    def scatter(*grads):
        return _Exchange("scatter", grads)

    (win_e,) = exchange_alone(gather(w_in_even[0]), "gather_in_even")
    bias_t = b_bias[0].T
    proj, h0, w1_0 = norm_matmul(x0, gain(norm_mix_pre, 0), win_e, "in_proj_even", exchange=gather(w_ff1[0]))
    oa, states, decays, w2_0 = hgrn2_fwd(proj, lb_table, a_norm, batch, "hgrn2_fwd", exchange=gather(w_ff2[0]))
    mixin, wout_e = gmlp_fwd(proj, oa, b_ln_g, b_ln_b, b_ws[0], bias_t, "gmlp_fwd", exchange=gather(w_out_even[0]))
    mix0, x1 = out_proj(mixin, wout_e, x0, gain(norm_mix_post, 0), "out_proj_even")
    x2, hf0, a0, y0, win_o, wout_o = ffn_fwd(x1, gain(norm_ffn_pre, 0), w1_0, w2_0, gain(norm_ffn_post, 0),
                                             "ffn_fwd_0", exchange=gather(w_in_odd[0], w_out_odd[0]))
    x2p = _residue_major(x2, batch)
    qkv, h1 = norm_matmul(x2p, gain(norm_mix_pre, 1), win_o, "in_proj_odd")
    cos_t, sin_a, sin_b = rope_tables(_residue_major(positions.reshape(t, 1), batch), "rope_tables")
    ao, lse, q_rot, k_rot, w1_1, w2_1 = attn_fwd(qkv, cos_t, sin_a, sin_b, batch, "attn_fwd",
                                                 exchange=gather(w_ff1[1], w_ff2[1]))
    mix1, x3 = out_proj(ao, wout_o, x2p, gain(norm_mix_post, 1), "out_proj_odd")
    dx4, hf1, a1, y1, loss_part = ffn_fwd(x3, gain(norm_ffn_pre, 1), w1_1, w2_1, gain(norm_ffn_post, 1),
                                          "ffn_fwd_1", target=_residue_major(target, batch))

    hc = D_FF // N_CHIPS
    dx3, dy1, da1, dg_fpre1, dg_fpost1 = ffn_bwd(
        dx4, x3, y1, a1, gain(norm_ffn_pre, 1), gain(norm_ffn_post, 1), w1_1, w2_1, "ffn_bwd_1")
    g_w1_1 = weight_grad(hf1, da1, "b", d, hc, False, "wgrad_ff1_1")
    g_w2_1 = weight_grad(a1, dy1, "a", hc, d, True, "wgrad_ff2_1")
    dmix1, dao, dg_mpost1 = out_proj_bwd(dx3, mix1, gain(norm_mix_post, 1), wout_o, "out_proj_bwd_odd")
    g_wout_o = weight_grad(ao, dmix1, "a", d // N_CHIPS, d, False, "wgrad_out_odd")
    dqkv, r_w1_1, r_w2_1, r_wout_o = attn_bwd(q_rot, k_rot, qkv, cos_t, sin_a, sin_b, ao, lse, dao, batch, "attn_bwd",
                                              exchange=scatter(g_w1_1, g_w2_1, g_wout_o))
    dx2p, dg_mpre1 = norm_matmul_bwd(dqkv, win_o, x2p, gain(norm_mix_pre, 1), dx3, "in_proj_bwd_odd")
    dx2 = _sequence_order(dx2p, batch)
    g_win_o = weight_grad_stacked(h1, dqkv, 3 * d // N_CHIPS, "wgrad_in_odd")
    s_w1_1, s_w2_1, s_wout_o = (reduce_slabs(r, n) for r, n in (
        (r_w1_1, "reduce_ff1_1"), (r_w2_1, "reduce_ff2_1"), (r_wout_o, "reduce_out_odd")))
    dx1, dy0, da0, dg_fpre0, dg_fpost0, r_win_o, t_w1_1, t_w2_1, t_wout_o = ffn_bwd(
        dx2, x1, y0, a0, gain(norm_ffn_pre, 0), gain(norm_ffn_post, 0), w1_0, w2_0, "ffn_bwd_0",
        exchange=_Both(scatter(g_win_o), _Swap([s_w1_1, s_w2_1, s_wout_o])))
    g_w1_0 = weight_grad(hf0, da0, "b", d, hc, False, "wgrad_ff1_0")
    g_w2_0 = weight_grad(a0, dy0, "a", hc, d, True, "wgrad_ff2_0")
    dmix0, dmixin, dg_mpost0 = out_proj_bwd(dx1, mix0, gain(norm_mix_post, 0), wout_e, "out_proj_bwd_even")
    g_wout_e = weight_grad(mixin, dmix0, "a", d // N_CHIPS, d, False, "wgrad_out_even")
    s_win_o = reduce_slabs(r_win_o, "reduce_in_odd")
    dproj, d_lb, d_anorm, r_w1_0, t_win_o = hgrn2_bwd(
        proj, states, decays, lb_table, a_norm, dmixin, batch, "hgrn2_bwd",
        exchange=_Both(scatter(g_w1_0), _Swap([s_win_o])))
    s_w1_0 = reduce_slabs(r_w1_0, "reduce_ff1_0")
    dproj, d_lng, d_lnb, d_ws, d_bias_t, r_w2_0, t_w1_0 = gmlp_bwd(
        proj, dmixin, b_ln_g, b_ln_b, b_ws[0], bias_t, dproj, "gmlp_bwd",
        exchange=_Both(scatter(g_w2_0), _Swap([s_w1_0])))
    s_w2_0 = reduce_slabs(r_w2_0, "reduce_ff2_0")
    g_win_e, r_wout_e, t_w2_0 = weight_grad(h0, dproj, "b", d, 3 * d // N_CHIPS, False, "wgrad_in_even",
                                            exchange=_Both(scatter(g_wout_e), _Swap([s_w2_0])))
    s_wout_e = reduce_slabs(r_wout_e, "reduce_out_even")
    dx0, dg_mpre0, r_win_e, t_wout_e = norm_matmul_bwd(
        dproj, win_e, x0, gain(norm_mix_pre, 0), dx1, "in_proj_bwd_even",
        exchange=_Both(scatter(g_win_e), _Swap([s_wout_e])))
    grad_x = dx0.reshape(x.shape)
    s_win_e = reduce_slabs(r_win_e, "reduce_in_even")
    (t_win_e,) = exchange_alone(_Swap([s_win_e]), "sibling_swap")

    big_w = [w_in_even, w_out_even, w_in_odd, w_out_odd, w_ff1, w_ff2]
    big_m = [m_w_in_even, m_w_out_even, m_w_in_odd, m_w_out_odd, m_w_ff1, m_w_ff2]
    big_v = [v_w_in_even, v_w_out_even, v_w_in_odd, v_w_out_odd, v_w_ff1, v_w_ff2]
    mine = [[s_win_e], [s_wout_e], [s_win_o], [s_wout_o], [s_w1_0, s_w1_1], [s_w2_0, s_w2_1]]
    theirs = [[t_win_e], [t_wout_e], [t_win_o], [t_wout_o], [t_w1_0, t_w1_1], [t_w2_0, t_w2_1]]
    big = []
    for i, (w, m, v) in enumerate(zip(big_w, big_m, big_v)):
        two_d = (-1, w.shape[-1])
        res = adamw_big(w.reshape(two_d), mine[i], theirs[i], m.reshape(two_d), v.reshape(two_d), "adamw_big_%d" % i)
        big.append([r.reshape(w.shape) for r in res])

    small_w = [norm_mix_pre, norm_mix_post, norm_ffn_pre, norm_ffn_post, lb_table, a_norm, b_ln_g, b_ln_b, b_ws, b_bias]
    small_m = [m_norm_mix_pre, m_norm_mix_post, m_norm_ffn_pre, m_norm_ffn_post, m_lb_table, m_a_norm, m_b_ln_g,
               m_b_ln_b, m_b_ws, m_b_bias]
    small_v = [v_norm_mix_pre, v_norm_mix_post, v_norm_ffn_pre, v_norm_ffn_post, v_lb_table, v_a_norm, v_b_ln_g,
               v_b_ln_b, v_b_ws, v_b_bias]
    partial = [jnp.concatenate([dg_mpre0, dg_mpre1]), jnp.concatenate([dg_mpost0, dg_mpost1]),
               jnp.concatenate([dg_fpre0, dg_fpre1]), jnp.concatenate([dg_fpost0, dg_fpost1]),
               d_lb, d_anorm, d_lng, d_lnb, d_ws[None], d_bias_t.T[None]]
    *small_g, loss = _unpack(allreduce_small(_pack(partial + [loss_part]), "allreduce_small"),
                             [w.shape for w in small_w] + [()])
    small_d, small_nm, small_nv = adamw_small(small_w, small_g, small_m, small_v, "adamw_small")

    order = ["norm_mix_pre", "norm_mix_post", "norm_ffn_pre", "norm_ffn_post", "w_in_even", "lb_table", "a_norm",
             "b_ln_g", "b_ln_b", "b_ws", "b_bias", "w_out_even", "w_in_odd", "w_out_odd", "w_ff1", "w_ff2"]
    small_names = ["norm_mix_pre", "norm_mix_post", "norm_ffn_pre", "norm_ffn_post", "lb_table", "a_norm",
                   "b_ln_g", "b_ln_b", "b_ws", "b_bias"]
    big_names = ["w_in_even", "w_out_even", "w_in_odd", "w_out_odd", "w_ff1", "w_ff2"]
    grads, deltas, new_m, new_v = {}, {}, {}, {}
    for i, nm in enumerate(small_names):
        grads[nm], deltas[nm], new_m[nm], new_v[nm] = small_g[i], small_d[i], small_nm[i], small_nv[i]
    for i, nm in enumerate(big_names):
        grads[nm], deltas[nm], new_m[nm], new_v[nm] = big[i]
    return (loss, grad_x, *[grads[n] for n in order], *[deltas[n] for n in order],
            *[new_m[n] for n in order], *[new_v[n] for n in order])
```

```python
import functools
import math

import jax
import jax.numpy as jnp
from jax import lax
from jax.experimental import pallas as pl
from jax.experimental.pallas import tpu as pltpu

F32 = jnp.float32
BF16 = jnp.bfloat16
MESH = pl.DeviceIdType.MESH

D_MODEL = 1024
SEQ = 2048
D_FF = 4096
N_CHIPS = 4
A_WIDTH = 512
A_HEADS = 4
A_DK = 128
A_CHUNK = 64
A_SUB = 16
B_WIDTH = 512
B_GROUPS = 4
B_CHUNK = 128
C_HEADS = 16
C_HEAD_DIM = 64
C_ROT_HALF = 8
C_BLOCK = 128
C_DILATIONS = (1, 4, 16)
ROPE_THETA = 500000.0
EPS = 1e-6
ADAM_LR = 0.001
ADAM_B1 = 0.9
ADAM_B2 = 0.999
ADAM_EPS = 1e-08
ADAM_WD = 0.01
ADAM_STEP = 10

ROW_TILE = 512
FFN_ROWS = 1024
WGRAD_ROWS = 2048
VMEM_LIMIT = 56 * 1024 * 1024
NEG_BIG = -1e30


def _params(sem=None):
    return pltpu.CompilerParams(dimension_semantics=sem, vmem_limit_bytes=VMEM_LIMIT)


def _dot(a, b):
    return jnp.dot(a, b, preferred_element_type=F32)


def _dot_nt(a, b):
    return lax.dot_general(a, b, (((1,), (1,)), ((), ())), preferred_element_type=F32)


def _dot_tn(a, b):
    return lax.dot_general(a, b, (((0,), (0,)), ((), ())), preferred_element_type=F32)


def _rms(x, g):
    r = lax.rsqrt(jnp.mean(x * x, axis=-1, keepdims=True) + EPS)
    return x * r * g


def _rms_bwd(x, g, dy):
    r = lax.rsqrt(jnp.mean(x * x, axis=-1, keepdims=True) + EPS)
    xh = x * r
    dg = jnp.sum(dy * xh, axis=0, keepdims=True)
    dxh = dy * g
    dx = r * (dxh - xh * jnp.mean(dxh * xh, axis=-1, keepdims=True))
    return dx, dg


def _accumulate(ref, val, first):
    @pl.when(first)
    def _():
        ref[...] = val

    @pl.when(jnp.logical_not(first))
    def _():
        ref[...] += val


N_DEV = 8
ANY = pl.BlockSpec(memory_space=pl.ANY)


def _place():
    x, y, c = lax.axis_index("x"), lax.axis_index("y"), lax.axis_index("c")
    return x, y, c, [(1 - x, y), (x, 1 - y), (1 - x, 1 - y)]


class _Exchange:
    def __init__(self, kind, arrays):
        self.kind, self.arrays, self.n = kind, list(arrays), len(arrays)
        per_peer = pltpu.SemaphoreType.DMA((3 * self.n,))
        if kind == "gather":
            self.out_shape = [jax.ShapeDtypeStruct((N_CHIPS,) + a.shape, a.dtype) for a in self.arrays]
            self.scratch = [per_peer, per_peer, pltpu.SemaphoreType.DMA((self.n,)), per_peer, per_peer]
        else:
            self.out_shape = [jax.ShapeDtypeStruct(a.shape, a.dtype) for a in self.arrays]
            self.scratch = [per_peer, per_peer, pltpu.SemaphoreType.DMA((self.n,))]

    def _copies(self, ins, outs, sems):
        send_sems, recv_sems, local_sems = sems[:3]
        x, y, c, chips = _place()
        me = 2 * x + y
        local, remote = [], []
        for a in range(self.n):
            if self.kind == "gather":
                local.append(pltpu.make_async_copy(ins[a], outs[a].at[me], local_sems.at[a]))
                half = self.arrays[a].shape[0] // 2

                def rows(ref, core, half=half):
                    return ref.at[pl.ds(core * half, half)]
            else:
                local.append(pltpu.make_async_copy(ins[a].at[me], outs[a].at[3], local_sems.at[a]))
            for j, (px, py) in enumerate(chips):
                k = 3 * a + j
                peer = 2 * px + py

                def copy(src, dst, to, send_sem=send_sems.at[k], recv_sem=recv_sems.at[k]):
                    return pltpu.make_async_remote_copy(src_ref=src, dst_ref=dst, send_sem=send_sem, recv_sem=recv_sem,
                                                        device_id=to, device_id_type=MESH)

                if self.kind == "gather":
                    sent = copy(rows(ins[a], c), rows(outs[a].at[me], c), (px, py, c))
                    landed = copy(rows(ins[a], c), rows(outs[a].at[peer], c), (px, py, c))
                    on = dict(send_sem=sems[3].at[k], recv_sem=sems[4].at[k])
                    passed = copy(rows(outs[a].at[peer], c), rows(outs[a].at[peer], c), (x, y, 1 - c), **on)
                    handed = copy(rows(outs[a].at[peer], c), rows(outs[a].at[peer], 1 - c), (x, y, 1 - c), **on)
                    remote.append((sent, landed, passed, handed))
                else:
                    sent = copy(ins[a].at[peer], outs[a].at[j], (px, py, c))
                    remote.append((sent, sent, None, None))
        return local, remote

    def start(self, ins, outs, sems):
        local, remote = self._copies(ins, outs, sems)
        for cp in local:
            cp.start()
        for sent, _, _, _ in remote:
            sent.start()

    def finish(self, ins, outs, sems):
        local, remote = self._copies(ins, outs, sems)
        for _, landed, passed, _ in remote:
            landed.wait_recv()
            if passed is not None:
                passed.start()
        for sent, _, passed, handed in remote:
            if passed is not None:
                handed.wait_recv()
                passed.wait_send()
            sent.wait_send()
        for cp in local:
            cp.wait()


class _Swap:
    def __init__(self, arrays):
        self.arrays, self.n = list(arrays), len(arrays)
        self.out_shape = [jax.ShapeDtypeStruct(a.shape, a.dtype) for a in self.arrays]
        self.scratch = [pltpu.SemaphoreType.DMA((self.n,)), pltpu.SemaphoreType.DMA((self.n,))]

    def _copies(self, ins, outs, sems):
        x, y, c, _ = _place()
        return [pltpu.make_async_remote_copy(src_ref=ins[a], dst_ref=outs[a], send_sem=sems[0].at[a],
                                             recv_sem=sems[1].at[a], device_id=(x, y, 1 - c), device_id_type=MESH)
                for a in range(self.n)]

    def start(self, ins, outs, sems):
        for cp in self._copies(ins, outs, sems):
            cp.start()

    def finish(self, ins, outs, sems):
        for cp in self._copies(ins, outs, sems):
            cp.wait_recv()
            cp.wait_send()


class _Both:
    def __init__(self, first, second):
        self.parts = (first, second)
        self.arrays, self.n = first.arrays + second.arrays, first.n + second.n
        self.out_shape = first.out_shape + second.out_shape
        self.scratch = first.scratch + second.scratch

    def _split(self, ins, outs, sems):
        a, b = self.parts
        return ((a, ins[:a.n], outs[:a.n], sems[:len(a.scratch)]),
                (b, ins[a.n:], outs[a.n:], sems[len(a.scratch):]))

    def start(self, ins, outs, sems):
        for ex, i, o, s in self._split(ins, outs, sems):
            ex.start(i, o, s)

    def finish(self, ins, outs, sems):
        for ex, i, o, s in self._split(ins, outs, sems):
            ex.finish(i, o, s)


def _call(body, *, name, grid, in_specs, out_specs, out_shape, args, scratch_shapes=(), aliases=None, exchange=None):
    if exchange is None:
        return pl.pallas_call(
            body, name=name, grid=grid, in_specs=in_specs, out_specs=out_specs, out_shape=out_shape,
            scratch_shapes=list(scratch_shapes), input_output_aliases=aliases or {},
            compiler_params=_params(("arbitrary",) * len(grid)))(*args)
    n_in, n_out, n_scr, n_ex = len(in_specs), len(out_specs), len(scratch_shapes), exchange.n
    steps = grid

    def wrapped(*refs):
        ins, refs = refs[:n_in], refs[n_in:]
        ex_in, refs = refs[:n_ex], refs[n_ex:]
        outs, refs = refs[:n_out], refs[n_out:]
        ex_out, refs = refs[:n_ex], refs[n_ex:]
        scr, sems = refs[:n_scr], refs[n_scr:]
        first = functools.reduce(jnp.logical_and, [pl.program_id(k) == 0 for k in range(len(steps))])
        last = functools.reduce(jnp.logical_and, [pl.program_id(k) == steps[k] - 1 for k in range(len(steps))])

        @pl.when(first)
        def _():
            exchange.start(ex_in, ex_out, sems)

        body(*ins, *outs, *scr)

        @pl.when(last)
        def _():
            exchange.finish(ex_in, ex_out, sems)

    return pl.pallas_call(
        wrapped, name=name, grid=grid,
        in_specs=list(in_specs) + [ANY] * n_ex, out_specs=list(out_specs) + [ANY] * n_ex,
        out_shape=list(out_shape) + exchange.out_shape,
        scratch_shapes=list(scratch_shapes) + exchange.scratch, input_output_aliases=aliases or {},
        compiler_params=_params(("arbitrary",) * len(grid)))(*args, *exchange.arrays)


def exchange_alone(exchange, name):
    def body(*refs):
        n = exchange.n
        exchange.start(refs[:n], refs[n:2 * n], refs[2 * n:])
        exchange.finish(refs[:n], refs[n:2 * n], refs[2 * n:])

    return pl.pallas_call(
        body, name=name, in_specs=[ANY] * exchange.n, out_specs=[ANY] * exchange.n,
        out_shape=exchange.out_shape, scratch_shapes=exchange.scratch)(*exchange.arrays)


def norm_matmul(x, g, wg, name, exchange=None):
    t, d = x.shape
    nl = wg.shape[2]

    def body(x_ref, g_ref, w_ref, o_ref, h_ref):
        h = _rms(x_ref[...], g_ref[...]).astype(BF16)
        h_ref[...] = h
        for c in range(N_CHIPS):
            o_ref[:, c * nl:(c + 1) * nl] = _dot(h, w_ref[c])

    return _call(
        body, name=name, grid=(t // ROW_TILE,),
        in_specs=[pl.BlockSpec((ROW_TILE, d), lambda i: (i, 0)),
                  pl.BlockSpec((1, d), lambda i: (0, 0)),
                  pl.BlockSpec((N_CHIPS, d, nl), lambda i: (0, 0, 0))],
        out_specs=[pl.BlockSpec((ROW_TILE, N_CHIPS * nl), lambda i: (i, 0)),
                   pl.BlockSpec((ROW_TILE, d), lambda i: (i, 0))],
        out_shape=[jax.ShapeDtypeStruct((t, N_CHIPS * nl), F32), jax.ShapeDtypeStruct((t, d), BF16)],
        args=(x, g, wg), exchange=exchange)


def norm_matmul_bwd(dproj, wg, x, g, dres, name, exchange=None):
    t, d = x.shape
    nl = wg.shape[2]
    stacked = dproj.ndim == 3
    piece = math.gcd(nl, dproj.shape[-1])

    def body(dp_ref, w_ref, x_ref, g_ref, dres_ref, dx_ref, dg_ref):
        dh = None
        for j in range(N_CHIPS * nl // piece):
            c, off = divmod(j * piece, nl)
            if stacked:
                p, lo = divmod(j * piece, dproj.shape[-1])
                lhs = dp_ref[p, :, lo:lo + piece]
            else:
                lhs = dp_ref[:, j * piece:(j + 1) * piece]
            part = _dot_nt(lhs.astype(BF16), w_ref[c, :, off:off + piece])
            dh = part if dh is None else dh + part
        dx, dg = _rms_bwd(x_ref[...], g_ref[...], dh)
        dx_ref[...] = dres_ref[...] + dx
        _accumulate(dg_ref, dg, pl.program_id(0) == 0)

    row = pl.BlockSpec((ROW_TILE, d), lambda i: (i, 0))
    vec = pl.BlockSpec((1, d), lambda i: (0, 0))
    if stacked:
        dp_spec = pl.BlockSpec((dproj.shape[0], ROW_TILE, dproj.shape[-1]), lambda i: (0, i, 0))
    else:
        dp_spec = pl.BlockSpec((ROW_TILE, N_CHIPS * nl), lambda i: (i, 0))
    return _call(
        body, name=name, grid=(t // ROW_TILE,),
        in_specs=[dp_spec, pl.BlockSpec((N_CHIPS, d, nl), lambda i: (0, 0, 0)), row, vec, row],
        out_specs=[row, vec],
        out_shape=[jax.ShapeDtypeStruct((t, d), F32), jax.ShapeDtypeStruct((1, d), F32)],
        args=(dproj, wg, x, g, dres), exchange=exchange)


def out_proj(a, wg, x, g, name):
    t, d = x.shape
    kl = wg.shape[1]

    def body(a_ref, w_ref, x_ref, g_ref, mix_ref, xo_ref):
        acc = _dot(a_ref[:, 0:kl], w_ref[0])
        for c in range(1, N_CHIPS):
            acc += _dot(a_ref[:, c * kl:(c + 1) * kl], w_ref[c])
        mix_ref[...] = acc
        xo_ref[...] = x_ref[...] + _rms(acc, g_ref[...])

    row = pl.BlockSpec((ROW_TILE, d), lambda i: (i, 0))
    return pl.pallas_call(
        body, name=name, grid=(t // ROW_TILE,),
        in_specs=[row, pl.BlockSpec((N_CHIPS, kl, d), lambda i: (0, 0, 0)), row,
                  pl.BlockSpec((1, d), lambda i: (0, 0))],
        out_specs=[row, row],
        out_shape=[jax.ShapeDtypeStruct((t, d), F32), jax.ShapeDtypeStruct((t, d), F32)],
        compiler_params=_params(("arbitrary",)),
    )(a, wg, x, g)


def out_proj_bwd(dxo, mix, g, wg, name):
    t, d = mix.shape
    kl = wg.shape[1]

    def body(dxo_ref, mix_ref, g_ref, w_ref, dmix_ref, da_ref, dg_ref):
        dmix, dg = _rms_bwd(mix_ref[...], g_ref[...], dxo_ref[...])
        dmb = dmix.astype(BF16)
        dmix_ref[...] = dmb
        for c in range(N_CHIPS):
            da_ref[:, c * kl:(c + 1) * kl] = _dot_nt(dmb, w_ref[c])
        _accumulate(dg_ref, dg, pl.program_id(0) == 0)

    row = pl.BlockSpec((ROW_TILE, d), lambda i: (i, 0))
    vec = pl.BlockSpec((1, d), lambda i: (0, 0))
    return pl.pallas_call(
        body, name=name, grid=(t // ROW_TILE,),
        in_specs=[row, row, vec, pl.BlockSpec((N_CHIPS, kl, d), lambda i: (0, 0, 0))],
        out_specs=[row, row, vec],
        out_shape=[jax.ShapeDtypeStruct((t, d), BF16), jax.ShapeDtypeStruct((t, d), F32),
                   jax.ShapeDtypeStruct((1, d), F32)],
        compiler_params=_params(("arbitrary",)),
    )(dxo, mix, g, wg)


def ffn_fwd(x, gpre, w1g, w2g, gpost, name, exchange=None, target=None):
    t, d = x.shape
    hc = w1g.shape[2]
    with_loss = target is not None

    def body(x_ref, gpre_ref, w1_ref, w2_ref, gpost_ref, *rest):
        if with_loss:
            t_ref, xo_ref, h_ref, a_ref, y_ref, l_ref, acc = rest
        else:
            xo_ref, h_ref, a_ref, y_ref, acc = rest
        i, c = pl.program_id(0), pl.program_id(1)

        @pl.when(c == 0)
        def _():
            h_ref[...] = _rms(x_ref[...], gpre_ref[...]).astype(BF16)

        a = _dot(h_ref[...], w1_ref[...])
        a_ref[...] = a.astype(BF16)
        r = jnp.square(jnp.maximum(a, 0.0)).astype(BF16)
        _accumulate(acc, _dot(r, w2_ref[...]), c == 0)

        @pl.when(c == N_CHIPS - 1)
        def _():
            y = acc[...]
            y_ref[...] = y
            xo = x_ref[...] + _rms(y, gpost_ref[...])
            if with_loss:
                e = xo - t_ref[...]
                xo_ref[...] = e * (1.0 / d)
                part = jnp.sum(jnp.sum(e * e, axis=-1, keepdims=True), axis=0, keepdims=True) * (0.5 / d)
                _accumulate(l_ref, part, i == 0)
            else:
                xo_ref[...] = xo

    row = pl.BlockSpec((FFN_ROWS, d), lambda i, c: (i, 0))
    vec = pl.BlockSpec((1, d), lambda i, c: (0, 0))
    one = pl.BlockSpec((1, 1), lambda i, c: (0, 0))
    return _call(
        body, name=name, grid=(t // FFN_ROWS, N_CHIPS),
        in_specs=[row, vec,
                  pl.BlockSpec((None, d, hc), lambda i, c: (c, 0, 0)),
                  pl.BlockSpec((None, hc, d), lambda i, c: (c, 0, 0)), vec] + ([row] if with_loss else []),
        out_specs=[row, row, pl.BlockSpec((FFN_ROWS, hc), lambda i, c: (i, c)), row] + ([one] if with_loss else []),
        out_shape=[jax.ShapeDtypeStruct((t, d), F32), jax.ShapeDtypeStruct((t, d), BF16),
                   jax.ShapeDtypeStruct((t, N_CHIPS * hc), BF16), jax.ShapeDtypeStruct((t, d), F32)]
        + ([jax.ShapeDtypeStruct((1, 1), F32)] if with_loss else []),
        scratch_shapes=[pltpu.VMEM((FFN_ROWS, d), F32)],
        args=(x, gpre, w1g, w2g, gpost) + ((target,) if with_loss else ()), exchange=exchange)


def ffn_bwd(dxo, x, y, a, gpre, gpost, w1g, w2g, name, exchange=None):
    t, d = x.shape
    hc = w1g.shape[2]

    def body(dxo_ref, x_ref, y_ref, a_ref, gpre_ref, gpost_ref, w1_ref, w2_ref,
             dxi_ref, dy_ref, da_ref, dgpre_ref, dgpost_ref, acc):
        i, c = pl.program_id(0), pl.program_id(1)

        @pl.when(c == 0)
        def _():
            dy, dg = _rms_bwd(y_ref[...], gpost_ref[...], dxo_ref[...])
            dy_ref[...] = dy.astype(BF16)
            _accumulate(dgpost_ref, dg, i == 0)

        dr = _dot_nt(dy_ref[...], w2_ref[...])
        da = (dr * (2.0 * jnp.maximum(a_ref[...].astype(F32), 0.0))).astype(BF16)
        da_ref[...] = da
        _accumulate(acc, _dot_nt(da, w1_ref[...]), c == 0)

        @pl.when(c == N_CHIPS - 1)
        def _():
            dx, dg = _rms_bwd(x_ref[...], gpre_ref[...], acc[...])
            dxi_ref[...] = dxo_ref[...] + dx
            _accumulate(dgpre_ref, dg, i == 0)

    row = pl.BlockSpec((ROW_TILE, d), lambda i, c: (i, 0))
    vec = pl.BlockSpec((1, d), lambda i, c: (0, 0))
    hid = pl.BlockSpec((ROW_TILE, hc), lambda i, c: (i, c))
    return _call(
        body, name=name, grid=(t // ROW_TILE, N_CHIPS),
        in_specs=[row, row, row, hid, vec, vec,
                  pl.BlockSpec((None, d, hc), lambda i, c: (c, 0, 0)),
                  pl.BlockSpec((None, hc, d), lambda i, c: (c, 0, 0))],
        out_specs=[row, row, hid, vec, vec],
        out_shape=[jax.ShapeDtypeStruct((t, d), F32), jax.ShapeDtypeStruct((t, d), BF16),
                   jax.ShapeDtypeStruct((t, N_CHIPS * hc), BF16),
                   jax.ShapeDtypeStruct((1, d), F32), jax.ShapeDtypeStruct((1, d), F32)],
        scratch_shapes=[pltpu.VMEM((ROW_TILE, d), F32)],
        args=(dxo, x, y, a, gpre, gpost, w1g, w2g), exchange=exchange)


def weight_grad(a, b, chunked, bk, bn, relu2, name, exchange=None):
    t = a.shape[0]
    a_on = chunked == "a"
    rows = min(t, WGRAD_ROWS)
    n_steps = t // rows

    def body(a_ref, b_ref, o_ref, acc):
        s = pl.program_id(1)
        av = a_ref[...]
        if relu2:
            av = jnp.square(jnp.maximum(av.astype(F32), 0.0))
        _accumulate(acc, _dot_tn(av.astype(BF16), b_ref[...].astype(BF16)), s == 0)

        @pl.when(s == n_steps - 1)
        def _():
            o_ref[...] = acc[...].astype(BF16)

    res = _call(
        body, name=name, grid=(N_CHIPS, n_steps),
        in_specs=[pl.BlockSpec((rows, bk), (lambda c, s: (s, c)) if a_on else (lambda c, s: (s, 0))),
                  pl.BlockSpec((rows, bn), (lambda c, s: (s, 0)) if a_on else (lambda c, s: (s, c)))],
        out_specs=[pl.BlockSpec((None, bk, bn), lambda c, s: (c, 0, 0))],
        out_shape=[jax.ShapeDtypeStruct((N_CHIPS, bk, bn), BF16)],
        scratch_shapes=[pltpu.VMEM((bk, bn), F32)],
        args=(a, b), exchange=exchange)
    return res[0] if exchange is None else res


def weight_grad_stacked(a, b3, bn, name):
    t, bk = a.shape
    width = b3.shape[-1]
    piece = math.gcd(bn, width)
    rows = min(t, WGRAD_ROWS)
    n_steps = t // rows

    def body(a_ref, b_ref, o_hbm, acc, staged, sem):
        s, c = pl.program_id(0), pl.program_id(1)
        av = a_ref[...].astype(BF16)
        for chunk in range(N_CHIPS):
            @pl.when(c == chunk)
            def _(chunk=chunk):
                cols = [divmod(chunk * bn + k * piece, width) for k in range(bn // piece)]
                b = jnp.concatenate([b_ref[p, :, lo:lo + piece] for p, lo in cols], axis=1).astype(BF16)
                _accumulate(acc.at[chunk], _dot_tn(av, b), s == 0)

                @pl.when(s == n_steps - 1)
                def _():
                    staged[...] = acc[chunk].astype(BF16)
                    copy = pltpu.make_async_copy(staged, o_hbm.at[chunk], sem)
                    copy.start()
                    copy.wait()

    return pl.pallas_call(
        body, name=name, grid=(n_steps, N_CHIPS),
        in_specs=[pl.BlockSpec((rows, bk), lambda s, c: (s, 0)),
                  pl.BlockSpec((b3.shape[0], rows, width), lambda s, c: (0, s, 0))],
        out_specs=ANY,
        out_shape=jax.ShapeDtypeStruct((N_CHIPS, bk, bn), BF16),
        scratch_shapes=[pltpu.VMEM((N_CHIPS, bk, bn), F32), pltpu.VMEM((bk, bn), BF16), pltpu.SemaphoreType.DMA],
        compiler_params=_params(("arbitrary", "arbitrary")),
    )(a, b3)


def _hgrn2_chunk(st, qs, fls, ivs, gls, l0, l1, l2, ng):
    nsub = len(qs)
    mx = jnp.maximum(jnp.maximum(l0, l1), l2)
    e0, e1, e2 = jnp.exp(l0 - mx), jnp.exp(l1 - mx), jnp.exp(l2 - mx)
    lb = e0 / (e0 + e1 + e2)
    rows = lax.broadcasted_iota(jnp.int32, (A_SUB, A_SUB), 0)
    cols = lax.broadcasted_iota(jnp.int32, (A_SUB, A_SUB), 1)
    tri = (rows >= cols).astype(F32)
    keep = (lax.broadcasted_iota(jnp.int32, (A_SUB, A_SUB, A_DK), 0)
            >= lax.broadcasted_iota(jnp.int32, (A_SUB, A_SUB, A_DK), 1))
    base = jnp.zeros_like(l0)
    bases, gs, ks, qfs = [], [], [], []
    for i in range(nsub):
        f = lb + (1.0 - lb) * jax.nn.sigmoid(fls[i])
        logf = jnp.log(f)
        bases.append(base)
        gs.append(base + jnp.dot(tri, logf, precision=lax.Precision.HIGHEST, preferred_element_type=F32))
        base = base + jnp.sum(logf, axis=0, keepdims=True)
        ks.append(1.0 - f)
        qfs.append(jax.nn.silu(qs[i]))
    g_last = base
    stb = st.astype(BF16)
    outs = []
    for i in range(nsub):
        o = _dot_nt((qfs[i] * jnp.exp(gs[i])).astype(BF16), stb)
        if i > 0:
            qt = (qfs[i] * jnp.exp(gs[i] - bases[i])).astype(BF16)
            kk = jnp.concatenate([ks[j] * jnp.exp(bases[i] - gs[j]) for j in range(i)], axis=0).astype(BF16)
            vv = jnp.concatenate(ivs[:i], axis=0).astype(BF16)
            o = o + _dot(_dot_nt(qt, kk).astype(BF16), vv)
        dec = jnp.exp(jnp.where(keep, gs[i][:, None, :] - gs[i][None, :, :], NEG_BIG))
        s_diag = jnp.sum(qfs[i][:, None, :] * ks[i][None, :, :] * dec, axis=-1)
        o = o + _dot(s_diag.astype(BF16), ivs[i].astype(BF16))
        o = o * lax.rsqrt(jnp.mean(o * o, axis=-1, keepdims=True) + EPS) * ng
        outs.append(o * jax.nn.silu(gls[i]))
    kdec = jnp.concatenate([ks[j] * jnp.exp(g_last - gs[j]) for j in range(nsub)], axis=0).astype(BF16)
    vall = jnp.concatenate(ivs, axis=0).astype(BF16)
    new_st = st * jnp.exp(g_last) + _dot_tn(vall, kdec)
    return new_st, outs


A_MAX_LOG_DECAY = 60.0


def _half_sums(logf):
    n = logf.shape[0]
    first = lax.broadcasted_iota(jnp.int32, logf.shape, 0) < n // 2
    return (jnp.sum(jnp.where(first, logf, 0.0), axis=0, keepdims=True),
            jnp.sum(jnp.where(first, 0.0, logf), axis=0, keepdims=True))


def _split3(x):
    hi = x.astype(BF16)
    r1 = x - hi.astype(F32)
    mid = r1.astype(BF16)
    return hi, mid, (r1 - mid.astype(F32)).astype(BF16)


def _tri_matmul(x, transpose):
    n = x.shape[0]
    r = lax.broadcasted_iota(jnp.int32, (n, n), 0)
    c = lax.broadcasted_iota(jnp.int32, (n, n), 1)
    tri = ((r <= c) if transpose else (r >= c)).astype(BF16)
    hi, mid, lo = _split3(x)
    return (_dot(tri, lo) + _dot(tri, mid)) + _dot(tri, hi)


@jax.custom_vjp
def _cumsum_rows(x):
    return _tri_matmul(x, False)


def _cumsum_rows_fwd(x):
    return _tri_matmul(x, False), None


def _cumsum_rows_bwd(_, dy):
    return (_tri_matmul(dy, True),)


_cumsum_rows.defvjp(_cumsum_rows_fwd, _cumsum_rows_bwd)


def _lower_bound(l0, l1, l2):
    mx = jnp.maximum(jnp.maximum(l0, l1), l2)
    e0, e1, e2 = jnp.exp(l0 - mx), jnp.exp(l1 - mx), jnp.exp(l2 - mx)
    return e0 / (e0 + e1 + e2)


def _b(x):
    return x.astype(BF16)


@jax.custom_vjp
def _mm(a, b):
    return _dot(_b(a), _b(b))


_mm.defvjp(lambda a, b: (_mm(a, b), (a, b)),
           lambda res, d: (_dot_nt(_b(d), _b(res[1])), _dot_tn(_b(res[0]), _b(d))))


@jax.custom_vjp
def _mm_nt(a, b):
    return _dot_nt(_b(a), _b(b))


_mm_nt.defvjp(lambda a, b: (_mm_nt(a, b), (a, b)),
              lambda res, d: (_dot(_b(d), _b(res[1])), _dot_tn(_b(d), _b(res[0]))))


def _dot_split(dot, a, b):
    ah, bh = _b(a), _b(b)
    al, bl = _b(a - ah.astype(F32)), _b(b - bh.astype(F32))
    return (dot(ah, bl) + dot(al, bh)) + dot(ah, bh)


@jax.custom_vjp
def _mm_scores(a, b):
    return _dot_nt(_b(a), _b(b))


_mm_scores.defvjp(lambda a, b: (_mm_scores(a, b), (a, b)),
                  lambda res, d: (_dot_split(_dot, d, res[1]), _dot_split(_dot_tn, d, res[0])))


@jax.custom_vjp
def _mm_tn(a, b):
    return _dot_tn(_b(a), _b(b))


_mm_tn.defvjp(lambda a, b: (_mm_tn(a, b), (a, b)),
              lambda res, d: (_dot_nt(_b(res[1]), _b(d)), _dot(_b(res[0]), _b(d))))


@jax.custom_vjp
def _split_heads(x):
    return tuple(x[:, h * A_DK:(h + 1) * A_DK] for h in range(A_HEADS))


def _split_heads_fwd(x):
    return _split_heads(x), None


def _split_heads_bwd(_, parts):
    return (jnp.concatenate(parts, axis=1),)


_split_heads.defvjp(_split_heads_fwd, _split_heads_bwd)


def _hgrn2_chunk_fast(sts, q, fl, iv, gl, l0, l1, l2, ng):
    lb = _lower_bound(l0, l1, l2)
    f = lb + (1.0 - lb) * jax.nn.sigmoid(fl)
    return _hgrn2_fast_core(sts, q, f, jnp.log(f), iv, gl, ng)


def _hgrn2_fast_core(sts, q, f, logf, iv, gl, ng):
    g = _cumsum_rows(logf)
    g_mid, g_last = _half_sums(logf)
    g_last = g_mid + g_last
    k = 1.0 - f
    qf = jax.nn.silu(q)
    qms = _split_heads(qf * jnp.exp(g - g_mid))
    kms = _split_heads(k * jnp.exp(g_mid - g))
    qgs = _split_heads(qf * jnp.exp(g))
    kds = _split_heads(k * jnp.exp(g_last - g))
    ivs = _split_heads(iv)
    decays = _split_heads(jnp.exp(g_last))
    n = q.shape[0]
    causal = lax.broadcasted_iota(jnp.int32, (n, n), 0) >= lax.broadcasted_iota(jnp.int32, (n, n), 1)
    raw = [_mm_scores(qm, km) for qm, km in zip(qms, kms)]
    inter = [_mm_nt(qg, st) for qg, st in zip(qgs, sts)]
    scores = [jnp.where(causal, s, 0.0) for s in raw]
    os = [a + _mm(s, v) for a, s, v in zip(inter, scores, ivs)]
    new_sts = [st * d + _mm_tn(v, kd) for st, d, v, kd in zip(sts, decays, ivs, kds)]
    os = [o * lax.rsqrt(jnp.mean(o * o, axis=-1, keepdims=True) + EPS) for o in os]
    return new_sts, jnp.concatenate(os, axis=1) * ng * jax.nn.silu(gl)


A_STEP_CHUNKS = 4


def _chunk_rows(j):
    return pl.ds(pl.multiple_of(j * A_CHUNK, A_CHUNK), A_CHUNK)


def _sub_rows(j, i):
    return pl.ds(pl.multiple_of(j * A_CHUNK + i * A_SUB, A_SUB), A_SUB)


def _sub_blocks(ref, head, j):
    lanes = slice(head * A_DK, (head + 1) * A_DK)
    return [ref[_sub_rows(j, i), lanes] for i in range(A_CHUNK // A_SUB)]


def hgrn2_fwd(proj, lb_table, a_norm, batch, name, exchange=None):
    t = proj.shape[0]
    n_steps = t // batch // (A_CHUNK * A_STEP_CHUNKS)
    rows = A_CHUNK * A_STEP_CHUNKS

    def body(q_ref, f_ref, i_ref, g_ref, lb_ref, ng_ref, o_ref, st_ref, dec_ref, st):
        @pl.when(pl.program_id(1) == 0)
        def _():
            st[...] = jnp.zeros_like(st)

        def chunk(j, carry):
            r = _chunk_rows(j)
            st_ref[j] = st[...]
            lb = _lower_bound(lb_ref[0:1, :], lb_ref[1:2, :], lb_ref[2:3, :])
            f = lb + (1.0 - lb) * jax.nn.sigmoid(f_ref[r, :])
            logf = jnp.log(f)
            decay = jnp.minimum(*_half_sums(logf))
            dec_ref[j] = decay
            mild = jnp.min(decay) >= -A_MAX_LOG_DECAY

            @pl.when(mild)
            def _():
                new_sts, o = _hgrn2_fast_core([st[h] for h in range(A_HEADS)], q_ref[r, :], f, logf,
                                              i_ref[r, :], g_ref[r, :], ng_ref[...])
                for h in range(A_HEADS):
                    st[h] = new_sts[h]
                o_ref[r, :] = o.astype(BF16)

            @pl.when(jnp.logical_not(mild))
            def _():
                for h in range(A_HEADS):
                    lanes = slice(h * A_DK, (h + 1) * A_DK)
                    new_st, outs = _hgrn2_chunk(
                        st[h], _sub_blocks(q_ref, h, j), _sub_blocks(f_ref, h, j), _sub_blocks(i_ref, h, j),
                        _sub_blocks(g_ref, h, j), lb_ref[0:1, lanes], lb_ref[1:2, lanes], lb_ref[2:3, lanes],
                        ng_ref[:, lanes])
                    st[h] = new_st
                    for i, o in enumerate(outs):
                        o_ref[_sub_rows(j, i), lanes] = o.astype(BF16)

            return carry

        lax.fori_loop(0, A_STEP_CHUNKS, chunk, 0)

    def part(k):
        return pl.BlockSpec((rows, A_WIDTH), lambda b, n: (b * n_steps + n, k))

    return _call(
        body, name=name, grid=(batch, n_steps),
        in_specs=[part(0), part(1), part(2), part(3),
                  pl.BlockSpec((3, A_WIDTH), lambda b, n: (0, 0)), pl.BlockSpec((1, A_WIDTH), lambda b, n: (0, 0))],
        out_specs=[part(0),
                   pl.BlockSpec((A_STEP_CHUNKS, A_HEADS, A_DK, A_DK), lambda b, n: (b * n_steps + n, 0, 0, 0)),
                   pl.BlockSpec((A_STEP_CHUNKS, 1, A_WIDTH), lambda b, n: (b * n_steps + n, 0, 0))],
        out_shape=[jax.ShapeDtypeStruct((t, A_WIDTH), BF16),
                   jax.ShapeDtypeStruct((t // A_CHUNK, A_HEADS, A_DK, A_DK), F32),
                   jax.ShapeDtypeStruct((t // A_CHUNK, 1, A_WIDTH), F32)],
        scratch_shapes=[pltpu.VMEM((A_HEADS, A_DK, A_DK), F32)],
        args=(proj, proj, proj, proj, lb_table, a_norm), exchange=exchange)


def hgrn2_bwd(proj, states, decays, lb_table, a_norm, do, batch, name, exchange=None):
    t = proj.shape[0]
    n_steps = t // batch // (A_CHUNK * A_STEP_CHUNKS)
    rows = A_CHUNK * A_STEP_CHUNKS

    def body(q_ref, f_ref, i_ref, g_ref, st_ref, dec_ref, lb_ref, ng_ref, do_ref, dp_ref, dlb_ref, dng_ref, dst):
        @pl.when(jnp.logical_and(pl.program_id(0) == 0, pl.program_id(1) == 0))
        def _():
            dlb_ref[...] = jnp.zeros_like(dlb_ref)
            dng_ref[...] = jnp.zeros_like(dng_ref)

        @pl.when(pl.program_id(1) == 0)
        def _():
            dst[...] = jnp.zeros_like(dst)

        def chunk(jj, carry):
            j = A_STEP_CHUNKS - 1 - jj
            r = _chunk_rows(j)
            mild = jnp.min(dec_ref[j]) >= -A_MAX_LOG_DECAY

            @pl.when(mild)
            def _():
                _, vjp = jax.vjp(
                    _hgrn2_chunk_fast, [st_ref[j, h] for h in range(A_HEADS)], q_ref[r, :], f_ref[r, :],
                    i_ref[r, :], g_ref[r, :], lb_ref[0:1, :], lb_ref[1:2, :], lb_ref[2:3, :], ng_ref[...])
                d_sts, dq, df, di, dg, dl0, dl1, dl2, dng = vjp(
                    ([dst[h] for h in range(A_HEADS)], do_ref[r, :].astype(F32)))
                for h in range(A_HEADS):
                    dst[h] = d_sts[h]
                for k, part in enumerate((dq, df, di, dg)):
                    dp_ref[r, k * A_WIDTH:(k + 1) * A_WIDTH] = part
                for row, val in enumerate((dl0, dl1, dl2)):
                    dlb_ref[row:row + 1, :] += val
                dng_ref[...] += dng

            @pl.when(jnp.logical_not(mild))
            def _():
                for h in range(A_HEADS):
                    lanes = slice(h * A_DK, (h + 1) * A_DK)
                    _, vjp = jax.vjp(
                        _hgrn2_chunk, st_ref[j, h], _sub_blocks(q_ref, h, j), _sub_blocks(f_ref, h, j),
                        _sub_blocks(i_ref, h, j), _sub_blocks(g_ref, h, j), lb_ref[0:1, lanes], lb_ref[1:2, lanes],
                        lb_ref[2:3, lanes], ng_ref[:, lanes])
                    douts = [x.astype(F32) for x in _sub_blocks(do_ref, h, j)]
                    d_st, dqs, dfs, dis, dgs, dl0, dl1, dl2, dng = vjp((dst[h], douts))
                    dst[h] = d_st
                    for k, parts in enumerate((dqs, dfs, dis, dgs)):
                        for i in range(A_CHUNK // A_SUB):
                            dp_ref[_sub_rows(j, i), k * A_WIDTH + h * A_DK:k * A_WIDTH + (h + 1) * A_DK] = parts[i]
                    for row, val in enumerate((dl0, dl1, dl2)):
                        dlb_ref[row:row + 1, lanes] += val
                    dng_ref[:, lanes] += dng

            return carry

        lax.fori_loop(0, A_STEP_CHUNKS, chunk, 0)

    def rev(b, n):
        return b * n_steps + (n_steps - 1 - n)

    def part(k):
        return pl.BlockSpec((rows, A_WIDTH), lambda b, n: (rev(b, n), k))

    const3 = pl.BlockSpec((3, A_WIDTH), lambda b, n: (0, 0))
    const1 = pl.BlockSpec((1, A_WIDTH), lambda b, n: (0, 0))
    return _call(
        body, name=name, grid=(batch, n_steps),
        in_specs=[part(0), part(1), part(2), part(3),
                  pl.BlockSpec((A_STEP_CHUNKS, A_HEADS, A_DK, A_DK), lambda b, n: (rev(b, n), 0, 0, 0)),
                  pl.BlockSpec((A_STEP_CHUNKS, 1, A_WIDTH), lambda b, n: (rev(b, n), 0, 0)),
                  const3, const1, part(0)],
        out_specs=[pl.BlockSpec((rows, 4 * A_WIDTH), lambda b, n: (rev(b, n), 0)), const3, const1],
        out_shape=[jax.ShapeDtypeStruct((t, 4 * A_WIDTH + 2 * B_WIDTH), F32),
                   jax.ShapeDtypeStruct((3, A_WIDTH), F32), jax.ShapeDtypeStruct((1, A_WIDTH), F32)],
        scratch_shapes=[pltpu.VMEM((A_HEADS, A_DK, A_DK), F32)],
        args=(proj, proj, proj, proj, states, decays, lb_table, a_norm, do), exchange=exchange)


B_GDIM = B_WIDTH // B_GROUPS
B_ROWS = 512


def _gmlp_chunk(ubs, vbs, lngs, lnbs, ws, bcols):
    vs = [jax.nn.gelu(v) for v in vbs]
    mu = sum(jnp.sum(v, axis=-1, keepdims=True) for v in vs) * (1.0 / B_WIDTH)
    var = sum(jnp.sum(jnp.square(v - mu), axis=-1, keepdims=True) for v in vs) * (1.0 / B_WIDTH)
    rstd = lax.rsqrt(var + EPS)
    tril = (lax.broadcasted_iota(jnp.int32, (B_CHUNK, B_CHUNK), 0)
            >= lax.broadcasted_iota(jnp.int32, (B_CHUNK, B_CHUNK), 1))
    outs = []
    for g in range(B_GROUPS):
        vn = (vs[g] - mu) * rstd * lngs[g] + lnbs[g]
        w = jnp.where(tril, ws[g], 0.0).astype(BF16)
        outs.append(jax.nn.gelu(ubs[g]) * (_dot(w, vn.astype(BF16)) + bcols[g]))
    return outs


def _gmlp_args(u_ref, v_ref, lng_ref, lnb_ref, w_ref, bt_ref, rows):
    def groups(ref):
        return [ref[rows, g * B_GDIM:(g + 1) * B_GDIM] for g in range(B_GROUPS)]

    def vec(ref):
        return [ref[:, g * B_GDIM:(g + 1) * B_GDIM] for g in range(B_GROUPS)]

    return (groups(u_ref), groups(v_ref), vec(lng_ref), vec(lnb_ref),
            [w_ref[g] for g in range(B_GROUPS)], [bt_ref[:, g:g + 1] for g in range(B_GROUPS)])


def gmlp_fwd(proj, oa, ln_g, ln_b, w, bias_t, name, exchange=None):
    t = proj.shape[0]

    def body(u_ref, v_ref, oa_ref, lng_ref, lnb_ref, w_ref, bt_ref, o_ref):
        o_ref[:, 0:A_WIDTH] = oa_ref[...]
        for n in range(B_ROWS // B_CHUNK):
            rows = slice(n * B_CHUNK, (n + 1) * B_CHUNK)
            outs = _gmlp_chunk(*_gmlp_args(u_ref, v_ref, lng_ref, lnb_ref, w_ref, bt_ref, rows))
            for g, o in enumerate(outs):
                o_ref[rows, A_WIDTH + g * B_GDIM:A_WIDTH + (g + 1) * B_GDIM] = o.astype(BF16)

    vec = pl.BlockSpec((1, B_WIDTH), lambda i: (0, 0))
    return _call(
        body, name=name, grid=(t // B_ROWS,),
        in_specs=[pl.BlockSpec((B_ROWS, B_WIDTH), lambda i: (i, 4)), pl.BlockSpec((B_ROWS, B_WIDTH), lambda i: (i, 5)),
                  pl.BlockSpec((B_ROWS, A_WIDTH), lambda i: (i, 0)), vec, vec,
                  pl.BlockSpec((B_GROUPS, B_CHUNK, B_CHUNK), lambda i: (0, 0, 0)),
                  pl.BlockSpec((B_CHUNK, B_GROUPS), lambda i: (0, 0))],
        out_specs=[pl.BlockSpec((B_ROWS, A_WIDTH + B_WIDTH), lambda i: (i, 0))],
        out_shape=[jax.ShapeDtypeStruct((t, A_WIDTH + B_WIDTH), BF16)],
        args=(proj, proj, oa, ln_g, ln_b, w, bias_t), exchange=exchange)


def gmlp_bwd(proj, dmixin, ln_g, ln_b, w, bias_t, dproj, name, exchange=None):
    t = proj.shape[0]

    def body(u_ref, v_ref, do_ref, lng_ref, lnb_ref, w_ref, bt_ref, dp_in_ref,
             dp_ref, dlng_ref, dlnb_ref, dw_ref, dbt_ref):
        del dp_in_ref

        @pl.when(pl.program_id(0) == 0)
        def _():
            for ref in (dlng_ref, dlnb_ref, dw_ref, dbt_ref):
                ref[...] = jnp.zeros_like(ref)

        for n in range(B_ROWS // B_CHUNK):
            rows = slice(n * B_CHUNK, (n + 1) * B_CHUNK)
            _, vjp = jax.vjp(_gmlp_chunk, *_gmlp_args(u_ref, v_ref, lng_ref, lnb_ref, w_ref, bt_ref, rows))
            douts = [do_ref[rows, g * B_GDIM:(g + 1) * B_GDIM] for g in range(B_GROUPS)]
            dus, dvs, dlngs, dlnbs, dws, dbs = vjp(douts)
            for g in range(B_GROUPS):
                lanes = slice(g * B_GDIM, (g + 1) * B_GDIM)
                dp_ref[rows, lanes] = dus[g]
                dp_ref[rows, B_WIDTH + g * B_GDIM:B_WIDTH + (g + 1) * B_GDIM] = dvs[g]
                dlng_ref[:, lanes] += dlngs[g]
                dlnb_ref[:, lanes] += dlnbs[g]
                dw_ref[g] += dws[g]
                dbt_ref[:, g:g + 1] += dbs[g]

    vec = pl.BlockSpec((1, B_WIDTH), lambda i: (0, 0))
    wspec = pl.BlockSpec((B_GROUPS, B_CHUNK, B_CHUNK), lambda i: (0, 0, 0))
    bspec = pl.BlockSpec((B_CHUNK, B_GROUPS), lambda i: (0, 0))
    return _call(
        body, name=name, grid=(t // B_ROWS,),
        in_specs=[pl.BlockSpec((B_ROWS, B_WIDTH), lambda i: (i, 4)), pl.BlockSpec((B_ROWS, B_WIDTH), lambda i: (i, 5)),
                  pl.BlockSpec((B_ROWS, B_WIDTH), lambda i: (i, 1)), vec, vec, wspec, bspec,
                  pl.BlockSpec(memory_space=pl.ANY)],
        out_specs=[pl.BlockSpec((B_ROWS, 2 * B_WIDTH), lambda i: (i, 2)), vec, vec, wspec, bspec],
        out_shape=[jax.ShapeDtypeStruct(dproj.shape, F32), jax.ShapeDtypeStruct((1, B_WIDTH), F32),
                   jax.ShapeDtypeStruct((1, B_WIDTH), F32), jax.ShapeDtypeStruct((B_GROUPS, B_CHUNK, B_CHUNK), F32),
                   jax.ShapeDtypeStruct((B_CHUNK, B_GROUPS), F32)],
        aliases={7: 0}, args=(proj, proj, dmixin, ln_g, ln_b, w, bias_t, dproj), exchange=exchange)


C_FWD_BLOCKS = 16
C_BWD_BLOCKS = 16
C_PAIR = 2 * C_HEAD_DIM
C_PAIRS = C_HEADS // 2
C_SCALE = 1.0 / math.sqrt(C_HEAD_DIM)
C_ROT_DIM = 2 * C_ROT_HALF
ROPE_ROWS = 1024


def rope_tables(pos_col, name):
    t = pos_col.shape[0]

    def body(p_ref, c_ref, a_ref, b_ref):
        lane = jnp.bitwise_and(lax.broadcasted_iota(jnp.int32, (1, C_PAIR), 1), C_HEAD_DIM - 1)
        j = jnp.bitwise_and(lane, C_ROT_HALF - 1).astype(F32)
        inv = jnp.exp(j * (-math.log(ROPE_THETA) / C_ROT_HALF))
        ang = p_ref[...].astype(F32) * inv
        cos, sin = jnp.cos(ang), jnp.sin(ang)
        c_ref[...] = jnp.where(lane < C_ROT_DIM, cos, 1.0)
        a_ref[...] = jnp.where(lane < C_ROT_HALF, -sin, 0.0)
        b_ref[...] = jnp.where(jnp.logical_and(lane >= C_ROT_HALF, lane < C_ROT_DIM), sin, 0.0)

    tab = pl.BlockSpec((ROPE_ROWS, C_PAIR), lambda i: (i, 0))
    return pl.pallas_call(
        body, name=name, grid=(t // ROPE_ROWS,),
        in_specs=[pl.BlockSpec((ROPE_ROWS, 1), lambda i: (i, 0))],
        out_specs=[tab, tab, tab],
        out_shape=[jax.ShapeDtypeStruct((t, C_PAIR), F32)] * 3,
        compiler_params=_params(("arbitrary",)),
    )(pos_col)


def _rope(x, c, a, b):
    return x * c + pltpu.roll(x, C_PAIR - C_ROT_HALF, 1) * a + pltpu.roll(x, C_ROT_HALF, 1) * b


def _rope_t(d, c, a, b):
    return d * c + pltpu.roll(d * a, C_ROT_HALF, 1) + pltpu.roll(d * b, C_PAIR - C_ROT_HALF, 1)


C_RES = 16


def _residue_major(a, batch):
    return a.reshape(batch, SEQ // C_RES, C_RES, -1).transpose(0, 2, 1, 3).reshape(a.shape)


def _sequence_order(a, batch):
    return a.reshape(batch, C_RES, SEQ // C_RES, -1).transpose(0, 2, 1, 3).reshape(a.shape)


def _block_pieces(idx, dil):
    nblk = SEQ // dil // C_BLOCK
    r, n = idx // nblk, idx % nblk
    per = C_RES // dil
    size = C_BLOCK // per

    def pieces(blk):
        return [((dil * a + r) * (SEQ // C_RES) + size * blk, size) for a in range(per)]

    return pieces(n), pieces(jnp.maximum(n - 1, 0)), n > 0


def _get_rows(ref, pieces):
    return jnp.concatenate([ref[pl.ds(pl.multiple_of(start, 8), size), :] for start, size in pieces], axis=0)


def _set_rows(ref, pieces, val, add=False):
    for k, (start, size) in enumerate(pieces):
        rows = pl.ds(pl.multiple_of(start, 8), size)
        part = val[k * size:(k + 1) * size]
        ref[rows, :] = ref[rows, :] + part if add else part


def _head_masks():
    low = lax.broadcasted_iota(jnp.int32, (1, C_PAIR), 1) < C_HEAD_DIM
    return low, jnp.logical_not(low)


def _attn_mask(has_prev, dil):
    per = C_RES // dil
    size = C_BLOCK // per

    def position(x):
        x = jnp.bitwise_and(x, C_BLOCK - 1)
        return per * jnp.bitwise_and(x, size - 1) + x // size

    j = lax.broadcasted_iota(jnp.int32, (2 * C_BLOCK, 2 * C_BLOCK), 1)
    pi = position(lax.broadcasted_iota(jnp.int32, (2 * C_BLOCK, 2 * C_BLOCK), 0))
    pj = position(j)
    own = j < C_BLOCK
    return jnp.logical_or(jnp.logical_and(own, pj <= pi),
                          jnp.logical_and(jnp.logical_and(jnp.logical_not(own), pj >= pi), has_prev))


def _stack_heads(x):
    low, high = _head_masks()
    return jnp.concatenate([jnp.where(low, x, 0.0), jnp.where(high, x, 0.0)], axis=0)


def _unstack_heads(x):
    low, _ = _head_masks()
    return jnp.where(low, x[:C_BLOCK], x[C_BLOCK:])


def attn_fwd(qkv, cos_t, sin_a, sin_b, batch, name, exchange=None):
    t = qkv.shape[0]
    nbr = len(C_DILATIONS)

    def body(q_ref, k_ref, v_ref, c_ref, a_ref, b_ref, o_ref, l_ref, qr_ref, kr_ref, qs, ks, *stats):
        acc, mm, dd = stats[0:nbr], stats[nbr:2 * nbr], stats[2 * nbr:3 * nbr]
        c, a, b = c_ref[...], a_ref[...], b_ref[...]
        qs[...] = _rope(q_ref[...], c, a, b) * C_SCALE
        ks[...] = _rope(k_ref[...], c, a, b)
        qr_ref[...] = qs[...].astype(BF16)
        kr_ref[...] = ks[...].astype(BF16)

        def load(idx, dil):
            own, prev, has_prev = _block_pieces(idx, dil)
            return own, (has_prev, _get_rows(qs, own), _get_rows(ks, own), _get_rows(ks, prev),
                         _get_rows(v_ref, own), _get_rows(v_ref, prev))

        def scores(dil, has_prev, q, k_own, k_prev, v_own, v_prev):
            k_cat = jnp.concatenate([k_own, k_prev], axis=0).astype(BF16)
            return jnp.where(_attn_mask(has_prev, dil), _dot_nt(_stack_heads(q).astype(BF16), k_cat), NEG_BIG)

        def softmax(s):
            m = jnp.max(s, axis=-1, keepdims=True)
            p = jnp.exp(s - m)
            return p.astype(BF16), m, jnp.sum(p, axis=-1, keepdims=True)

        def values(pb, has_prev, q, k_own, k_prev, v_own, v_prev):
            low, high = _head_masks()
            v_cat = jnp.concatenate([v_own, v_prev], axis=0)
            p_wide = jnp.concatenate([pb[:C_BLOCK], pb[C_BLOCK:]], axis=1)
            v_tall = jnp.concatenate([jnp.where(low, v_cat, 0.0), jnp.where(high, v_cat, 0.0)], axis=0).astype(BF16)
            return _dot(p_wide, v_tall)

        for bi, dil in enumerate(C_DILATIONS):
            def pair(i, carry, bi=bi, dil=dil):
                low, _ = _head_masks()
                loaded = [load(C_FWD_BLOCKS * i + k, dil) for k in range(C_FWD_BLOCKS)]
                ss = [scores(dil, *ops) for _, ops in loaded]
                sm = [softmax(s) for s in ss]
                pvs = [values(pb, *ops) for (pb, _, _), (_, ops) in zip(sm, loaded)]
                for (own, _), (_, m, den), pv in zip(loaded, sm, pvs):
                    _set_rows(acc[bi], own, pv)
                    _set_rows(mm[bi], own, jnp.where(low, m[:C_BLOCK], m[C_BLOCK:]))
                    _set_rows(dd[bi], own, jnp.where(low, den[:C_BLOCK], den[C_BLOCK:]))
                return carry

            lax.fori_loop(0, SEQ // C_BLOCK // C_FWD_BLOCKS, pair, 0)
        step = 2 * C_BLOCK
        for r0 in range(0, SEQ, step):
            rr = slice(r0, r0 + step)
            ms = [mm[g][rr, :] for g in range(nbr)]
            m_all = functools.reduce(jnp.maximum, ms)
            ws = [jnp.exp(m - m_all) for m in ms]
            num = sum(acc[g][rr, :] * ws[g] for g in range(nbr))
            den = sum(dd[g][rr, :] * ws[g] for g in range(nbr))
            o_ref[rr, :] = (num / den).astype(BF16)
            l_ref[rr, :] = m_all + jnp.log(den)

    def col(k):
        return pl.BlockSpec((SEQ, C_PAIR), lambda b, p: (b, k * C_PAIRS + p))

    tab = pl.BlockSpec((SEQ, C_PAIR), lambda b, p: (b, 0))
    return _call(
        body, name=name, grid=(batch, C_PAIRS),
        in_specs=[col(0), col(1), col(2), tab, tab, tab],
        out_specs=[col(0), col(0), col(0), col(0)],
        out_shape=[jax.ShapeDtypeStruct((t, D_MODEL), BF16), jax.ShapeDtypeStruct((t, D_MODEL), F32),
                   jax.ShapeDtypeStruct((t, D_MODEL), BF16), jax.ShapeDtypeStruct((t, D_MODEL), BF16)],
        scratch_shapes=[pltpu.VMEM((SEQ, C_PAIR), F32)] * (2 + 3 * nbr),
        args=(qkv, qkv, qkv, cos_t, sin_a, sin_b), exchange=exchange)


def attn_bwd(qr, kr, qkv, cos_t, sin_a, sin_b, o, lse, do, batch, name, exchange=None):
    t = qkv.shape[0]

    def body(q_ref, k_ref, v_ref, c_ref, a_ref, b_ref, o_ref, l_ref, do_ref, dqkv_ref, qs, ks, dqs, dks, dvs, dlt):
        low, _ = _head_masks()
        c, a, b = c_ref[...], a_ref[...], b_ref[...]
        qs[...] = q_ref[...].astype(F32)
        ks[...] = k_ref[...].astype(F32)
        prod = do_ref[...] * o_ref[...].astype(F32)
        s_low = jnp.sum(jnp.where(low, prod, 0.0), axis=-1, keepdims=True)
        s_all = jnp.sum(prod, axis=-1, keepdims=True)
        dlt[...] = jnp.where(low, s_low, s_all - s_low)
        dqs[...] = jnp.zeros_like(dqs)
        dks[...] = jnp.zeros_like(dks)
        dvs[...] = jnp.zeros_like(dvs)

        def load(idx, dil):
            own, prev, has_prev = _block_pieces(idx, dil)
            return (own, prev), (has_prev, _get_rows(qs, own), _get_rows(do_ref, own), _get_rows(ks, own),
                                 _get_rows(ks, prev), _get_rows(v_ref, own), _get_rows(v_ref, prev),
                                 _get_rows(l_ref, own), _get_rows(dlt, own))

        def operands(dil, has_prev, q, do, k_own, k_prev, v_own, v_prev, l_full, d_full):
            lcol = jnp.concatenate([l_full[:, 0:1], l_full[:, C_HEAD_DIM:C_HEAD_DIM + 1]], axis=0)
            dcol = jnp.concatenate([d_full[:, 0:1], d_full[:, C_HEAD_DIM:C_HEAD_DIM + 1]], axis=0)
            return (_stack_heads(q).astype(BF16), _stack_heads(do).astype(BF16),
                    jnp.concatenate([k_own, k_prev], axis=0).astype(BF16),
                    jnp.concatenate([v_own, v_prev], axis=0).astype(BF16), lcol, dcol, _attn_mask(has_prev, dil))

        for dil in C_DILATIONS:
            def pair(i, carry, dil=dil):
                loaded = [load(C_BWD_BLOCKS * i + k, dil) for k in range(C_BWD_BLOCKS)]
                ops = [operands(dil, *o) for _, o in loaded]
                ss = [_dot_nt(q_stack, k_cat) for q_stack, _, k_cat, _, _, _, _ in ops]
                dps = [_dot_nt(do_stack, v_cat) for _, do_stack, _, v_cat, _, _, _ in ops]
                ps = [jnp.exp(jnp.where(o[6], s, NEG_BIG) - o[4]) for s, o in zip(ss, ops)]
                dss = [(p * (dp - o[5])).astype(BF16) for p, dp, o in zip(ps, dps, ops)]
                dvs_ = [_dot_tn(p.astype(BF16), o[1]) for p, o in zip(ps, ops)]
                dks_ = [_dot_tn(ds, o[0]) for ds, o in zip(dss, ops)]
                dqs_ = [_unstack_heads(_dot(ds, o[2])) for ds, o in zip(dss, ops)]
                for ((own, prev), _), dq, dk_cat, dv_cat in zip(loaded, dqs_, dks_, dvs_):
                    _set_rows(dqs, own, dq, add=True)
                    _set_rows(dks, own, dk_cat[:C_BLOCK], add=True)
                    _set_rows(dvs, own, dv_cat[:C_BLOCK], add=True)
                    _set_rows(dks, prev, dk_cat[C_BLOCK:], add=True)
                    _set_rows(dvs, prev, dv_cat[C_BLOCK:], add=True)
                return carry

            lax.fori_loop(0, SEQ // C_BLOCK // C_BWD_BLOCKS, pair, 0)
        dqkv_ref[0] = _rope_t(dqs[...] * C_SCALE, c, a, b).astype(BF16)
        dqkv_ref[1] = _rope_t(dks[...], c, a, b).astype(BF16)
        dqkv_ref[2] = dvs[...].astype(BF16)

    def col(k):
        return pl.BlockSpec((SEQ, C_PAIR), lambda b, p: (b, k * C_PAIRS + p))

    tab = pl.BlockSpec((SEQ, C_PAIR), lambda b, p: (b, 0))
    return _call(
        body, name=name, grid=(batch, C_PAIRS),
        in_specs=[col(0), col(0), col(2), tab, tab, tab, col(0), col(0), col(0)],
        out_specs=[pl.BlockSpec((3, SEQ, C_PAIR), lambda b, p: (0, b, p))],
        out_shape=[jax.ShapeDtypeStruct((3, t, D_MODEL), BF16)],
        scratch_shapes=[pltpu.VMEM((SEQ, C_PAIR), F32)] * 6,
        args=(qr, kr, qkv, cos_t, sin_a, sin_b, o, lse, do), exchange=exchange)


def allreduce_small(slab, name):
    rows, lanes = slab.shape

    def body(x_ref, out_ref, gath, send_sems, recv_sems, local_sem):
        x, y, c, chips = _place()
        me, sibling = (x, y, c), (x, y, 1 - c)

        def slot(px, py, pc):
            return gath.at[4 * px + 2 * py + pc]

        def copy(k, block, to, src=None):
            return pltpu.make_async_remote_copy(
                src_ref=slot(*block) if src is None else src, dst_ref=slot(*block),
                send_sem=send_sems.at[k], recv_sem=recv_sems.at[k], device_id=to, device_id_type=MESH)

        mine = pltpu.make_async_copy(x_ref, slot(*me), local_sem)
        mine.start()
        first = [copy(0, me, sibling, src=x_ref)]
        first += [copy(1 + j, me, (*chip, c), src=x_ref) for j, chip in enumerate(chips)]
        for cp in first:
            cp.start()
        passed = [copy(4 + j, (*chip, c), sibling) for j, chip in enumerate(chips)]
        for j, chip in enumerate(chips):
            copy(1 + j, (*chip, c), me).wait_recv()
            passed[j].start()
        copy(0, sibling, me).wait_recv()
        for j, chip in enumerate(chips):
            copy(4 + j, (*chip, 1 - c), me).wait_recv()
        for cp in first + passed:
            cp.wait_send()
        mine.wait()
        total = gath[0]
        for d in range(1, N_DEV):
            total = total + gath[d]
        out_ref[...] = total

    return pl.pallas_call(
        body, name=name,
        in_specs=[pl.BlockSpec(memory_space=pltpu.VMEM)],
        out_specs=pl.BlockSpec(memory_space=pltpu.VMEM),
        out_shape=jax.ShapeDtypeStruct((rows, lanes), F32),
        scratch_shapes=[pltpu.VMEM((N_DEV, rows, lanes), F32),
                        pltpu.SemaphoreType.DMA((7,)), pltpu.SemaphoreType.DMA((7,)), pltpu.SemaphoreType.DMA],
    )(slab)


ELT_ROWS = 512


def reduce_slabs(r, name):
    r = r.reshape(N_CHIPS, -1, r.shape[-1])
    _, rows, cols = r.shape
    br = min(rows, ELT_ROWS)

    def body(r_ref, o_ref):
        o_ref[...] = ((r_ref[3].astype(F32) + r_ref[0].astype(F32)) + r_ref[1].astype(F32)) + r_ref[2].astype(F32)

    return pl.pallas_call(
        body, name=name, grid=(rows // br,),
        in_specs=[pl.BlockSpec((N_CHIPS, br, cols), lambda i: (0, i, 0))],
        out_specs=pl.BlockSpec((br, cols), lambda i: (i, 0)),
        out_shape=jax.ShapeDtypeStruct((rows, cols), F32),
        compiler_params=_params(("arbitrary",)),
    )(r)


def _adamw(w, g, m, v):
    m = ADAM_B1 * m + (1.0 - ADAM_B1) * g
    v = ADAM_B2 * v + (1.0 - ADAM_B2) * jnp.square(g)
    m_hat = m / (1.0 - ADAM_B1 ** ADAM_STEP)
    v_hat = v / (1.0 - ADAM_B2 ** ADAM_STEP)
    delta = -ADAM_LR * (m_hat / (jnp.sqrt(v_hat) + ADAM_EPS) + ADAM_WD * w)
    return delta, m, v


def adamw_big(w, s_mine, s_sibling, m, v, name):
    rows, cols = w.shape
    parts = len(s_mine)
    br = min(rows // parts, ELT_ROWS)
    nb = rows // parts // br

    def body(w_ref, m_ref, v_ref, *rest):
        sums, (g_out, d_out, m_out, v_out) = rest[:2 * parts], rest[2 * parts:]
        p = pl.program_id(0)
        g = sums[0][...] + sums[parts][...]
        for k in range(1, parts):
            g = jnp.where(p == k, sums[k][...] + sums[parts + k][...], g)
        g_out[...] = g
        d_out[...], m_out[...], v_out[...] = _adamw(w_ref[...], g, m_ref[...], v_ref[...])

    def part_spec(k):
        return pl.BlockSpec((br, cols), lambda p, i: (jnp.where(p == k, i, jnp.where(p < k, 0, nb - 1)), 0))

    blk = pl.BlockSpec((br, cols), lambda p, i: (p * nb + i, 0))
    out = jax.ShapeDtypeStruct((rows, cols), F32)
    return pl.pallas_call(
        body, name=name, grid=(parts, nb),
        in_specs=[blk] * 3 + [part_spec(k) for k in range(parts)] * 2, out_specs=[blk] * 4, out_shape=[out] * 4,
        compiler_params=_params(("arbitrary", "arbitrary")),
    )(w, m, v, *s_mine, *s_sibling)


def adamw_small(ws, gs, ms, vs, name):
    n = len(ws)

    def body(*refs):
        w_refs, g_refs, m_refs, v_refs = (refs[k * n:(k + 1) * n] for k in range(4))
        d_out, m_out, v_out = (refs[(4 + k) * n:(5 + k) * n] for k in range(3))
        for i in range(n):
            d_out[i][...], m_out[i][...], v_out[i][...] = _adamw(
                w_refs[i][...], g_refs[i][...], m_refs[i][...], v_refs[i][...])

    outs = [jax.ShapeDtypeStruct(w.shape, F32) for w in ws]
    res = pl.pallas_call(body, name=name, out_shape=outs * 3)(*ws, *gs, *ms, *vs)
    return res[:n], res[n:2 * n], res[2 * n:]


SLAB_LANES = 128
SLAB_ROW_ALIGN = 8


def _pack(parts):
    flat = jnp.concatenate([p.reshape(-1) for p in parts])
    rows = -(-flat.shape[0] // (SLAB_LANES * SLAB_ROW_ALIGN)) * SLAB_ROW_ALIGN
    flat = jnp.pad(flat, (0, rows * SLAB_LANES - flat.shape[0]))
    return flat.reshape(rows, SLAB_LANES)


def _unpack(slab, shapes):
    flat = slab.reshape(-1)
    out, pos = [], 0
    for s in shapes:
        size = math.prod(s)
        out.append(flat[pos:pos + size].reshape(s))
        pos += size
    return out


def kernel(x, positions, norm_mix_pre, norm_mix_post, norm_ffn_pre, norm_ffn_post, w_in_even, lb_table, a_norm, b_ln_g, b_ln_b, b_ws, b_bias, w_out_even, w_in_odd, w_out_odd, w_ff1, w_ff2, loss_target, m_norm_mix_pre, m_norm_mix_post, m_norm_ffn_pre, m_norm_ffn_post, m_w_in_even, m_lb_table, m_a_norm, m_b_ln_g, m_b_ln_b, m_b_ws, m_b_bias, m_w_out_even, m_w_in_odd, m_w_out_odd, m_w_ff1, m_w_ff2, v_norm_mix_pre, v_norm_mix_post, v_norm_ffn_pre, v_norm_ffn_post, v_w_in_even, v_lb_table, v_a_norm, v_b_ln_g, v_b_ln_b, v_b_ws, v_b_bias, v_w_out_even, v_w_in_odd, v_w_out_odd, v_w_ff1, v_w_ff2):
    batch = x.shape[0]
    t = batch * SEQ
    d = D_MODEL
    x0 = x.reshape(t, d)
    target = loss_target.reshape(t, d)

    def gain(p, layer):
        return p[layer:layer + 1]

    def gather(*shards):
        return _Exchange("gather", [w.astype(BF16) for w in shards])

    def scatter(*grads):
        return _Exchange("scatter", grads)

    (win_e,) = exchange_alone(gather(w_in_even[0]), "gather_in_even")
    bias_t = b_bias[0].T
    proj, h0, w1_0 = norm_matmul(x0, gain(norm_mix_pre, 0), win_e, "in_proj_even", exchange=gather(w_ff1[0]))
    oa, states, decays, w2_0 = hgrn2_fwd(proj, lb_table, a_norm, batch, "hgrn2_fwd", exchange=gather(w_ff2[0]))
    mixin, wout_e = gmlp_fwd(proj, oa, b_ln_g, b_ln_b, b_ws[0], bias_t, "gmlp_fwd", exchange=gather(w_out_even[0]))
    mix0, x1 = out_proj(mixin, wout_e, x0, gain(norm_mix_post, 0), "out_proj_even")
    x2, hf0, a0, y0, win_o, wout_o = ffn_fwd(x1, gain(norm_ffn_pre, 0), w1_0, w2_0, gain(norm_ffn_post, 0),
                                             "ffn_fwd_0", exchange=gather(w_in_odd[0], w_out_odd[0]))
    x2p = _residue_major(x2, batch)
    qkv, h1 = norm_matmul(x2p, gain(norm_mix_pre, 1), win_o, "in_proj_odd")
    cos_t, sin_a, sin_b = rope_tables(_residue_major(positions.reshape(t, 1), batch), "rope_tables")
    ao, lse, q_rot, k_rot, w1_1, w2_1 = attn_fwd(qkv, cos_t, sin_a, sin_b, batch, "attn_fwd",
                                                 exchange=gather(w_ff1[1], w_ff2[1]))
    mix1, x3 = out_proj(ao, wout_o, x2p, gain(norm_mix_post, 1), "out_proj_odd")
    dx4, hf1, a1, y1, loss_part = ffn_fwd(x3, gain(norm_ffn_pre, 1), w1_1, w2_1, gain(norm_ffn_post, 1),
                                          "ffn_fwd_1", target=_residue_major(target, batch))

    hc = D_FF // N_CHIPS
    dx3, dy1, da1, dg_fpre1, dg_fpost1 = ffn_bwd(
        dx4, x3, y1, a1, gain(norm_ffn_pre, 1), gain(norm_ffn_post, 1), w1_1, w2_1, "ffn_bwd_1")
    g_w1_1 = weight_grad(hf1, da1, "b", d, hc, False, "wgrad_ff1_1")
    g_w2_1 = weight_grad(a1, dy1, "a", hc, d, True, "wgrad_ff2_1")
    dmix1, dao, dg_mpost1 = out_proj_bwd(dx3, mix1, gain(norm_mix_post, 1), wout_o, "out_proj_bwd_odd")
    g_wout_o = weight_grad(ao, dmix1, "a", d // N_CHIPS, d, False, "wgrad_out_odd")
    dqkv, r_w1_1, r_w2_1, r_wout_o = attn_bwd(q_rot, k_rot, qkv, cos_t, sin_a, sin_b, ao, lse, dao, batch, "attn_bwd",
                                              exchange=scatter(g_w1_1, g_w2_1, g_wout_o))
    dx2p, dg_mpre1 = norm_matmul_bwd(dqkv, win_o, x2p, gain(norm_mix_pre, 1), dx3, "in_proj_bwd_odd")
    dx2 = _sequence_order(dx2p, batch)
    g_win_o = weight_grad_stacked(h1, dqkv, 3 * d // N_CHIPS, "wgrad_in_odd")
    s_w1_1, s_w2_1, s_wout_o = (reduce_slabs(r, n) for r, n in (
        (r_w1_1, "reduce_ff1_1"), (r_w2_1, "reduce_ff2_1"), (r_wout_o, "reduce_out_odd")))
    dx1, dy0, da0, dg_fpre0, dg_fpost0, r_win_o, t_w1_1, t_w2_1, t_wout_o = ffn_bwd(
        dx2, x1, y0, a0, gain(norm_ffn_pre, 0), gain(norm_ffn_post, 0), w1_0, w2_0, "ffn_bwd_0",
        exchange=_Both(scatter(g_win_o), _Swap([s_w1_1, s_w2_1, s_wout_o])))
    g_w1_0 = weight_grad(hf0, da0, "b", d, hc, False, "wgrad_ff1_0")
    g_w2_0 = weight_grad(a0, dy0, "a", hc, d, True, "wgrad_ff2_0")
    dmix0, dmixin, dg_mpost0 = out_proj_bwd(dx1, mix0, gain(norm_mix_post, 0), wout_e, "out_proj_bwd_even")
    g_wout_e = weight_grad(mixin, dmix0, "a", d // N_CHIPS, d, False, "wgrad_out_even")
    s_win_o = reduce_slabs(r_win_o, "reduce_in_odd")
    dproj, d_lb, d_anorm, r_w1_0, t_win_o = hgrn2_bwd(
        proj, states, decays, lb_table, a_norm, dmixin, batch, "hgrn2_bwd",
        exchange=_Both(scatter(g_w1_0), _Swap([s_win_o])))
    s_w1_0 = reduce_slabs(r_w1_0, "reduce_ff1_0")
    dproj, d_lng, d_lnb, d_ws, d_bias_t, r_w2_0, t_w1_0 = gmlp_bwd(
        proj, dmixin, b_ln_g, b_ln_b, b_ws[0], bias_t, dproj, "gmlp_bwd",
        exchange=_Both(scatter(g_w2_0), _Swap([s_w1_0])))
    s_w2_0 = reduce_slabs(r_w2_0, "reduce_ff2_0")
    g_win_e, r_wout_e, t_w2_0 = weight_grad(h0, dproj, "b", d, 3 * d // N_CHIPS, False, "wgrad_in_even",
                                            exchange=_Both(scatter(g_wout_e), _Swap([s_w2_0])))
    s_wout_e = reduce_slabs(r_wout_e, "reduce_out_even")
    dx0, dg_mpre0, r_win_e, t_wout_e = norm_matmul_bwd(
        dproj, win_e, x0, gain(norm_mix_pre, 0), dx1, "in_proj_bwd_even",
        exchange=_Both(scatter(g_win_e), _Swap([s_wout_e])))
    grad_x = dx0.reshape(x.shape)
    s_win_e = reduce_slabs(r_win_e, "reduce_in_even")
    (t_win_e,) = exchange_alone(_Swap([s_win_e]), "sibling_swap")

    big_w = [w_in_even, w_out_even, w_in_odd, w_out_odd, w_ff1, w_ff2]
    big_m = [m_w_in_even, m_w_out_even, m_w_in_odd, m_w_out_odd, m_w_ff1, m_w_ff2]
    big_v = [v_w_in_even, v_w_out_even, v_w_in_odd, v_w_out_odd, v_w_ff1, v_w_ff2]
    mine = [[s_win_e], [s_wout_e], [s_win_o], [s_wout_o], [s_w1_0, s_w1_1], [s_w2_0, s_w2_1]]
    theirs = [[t_win_e], [t_wout_e], [t_win_o], [t_wout_o], [t_w1_0, t_w1_1], [t_w2_0, t_w2_1]]
    big = []
    for i, (w, m, v) in enumerate(zip(big_w, big_m, big_v)):
        two_d = (-1, w.shape[-1])
        res = adamw_big(w.reshape(two_d), mine[i], theirs[i], m.reshape(two_d), v.reshape(two_d), "adamw_big_%d" % i)
        big.append([r.reshape(w.shape) for r in res])

    small_w = [norm_mix_pre, norm_mix_post, norm_ffn_pre, norm_ffn_post, lb_table, a_norm, b_ln_g, b_ln_b, b_ws, b_bias]
    small_m = [m_norm_mix_pre, m_norm_mix_post, m_norm_ffn_pre, m_norm_ffn_post, m_lb_table, m_a_norm, m_b_ln_g,
               m_b_ln_b, m_b_ws, m_b_bias]
    small_v = [v_norm_mix_pre, v_norm_mix_post, v_norm_ffn_pre, v_norm_ffn_post, v_lb_table, v_a_norm, v_b_ln_g,
               v_b_ln_b, v_b_ws, v_b_bias]
    partial = [jnp.concatenate([dg_mpre0, dg_mpre1]), jnp.concatenate([dg_mpost0, dg_mpost1]),
               jnp.concatenate([dg_fpre0, dg_fpre1]), jnp.concatenate([dg_fpost0, dg_fpost1]),
               d_lb, d_anorm, d_lng, d_lnb, d_ws[None], d_bias_t.T[None]]
    *small_g, loss = _unpack(allreduce_small(_pack(partial + [loss_part]), "allreduce_small"),
                             [w.shape for w in small_w] + [()])
    small_d, small_nm, small_nv = adamw_small(small_w, small_g, small_m, small_v, "adamw_small")

    order = ["norm_mix_pre", "norm_mix_post", "norm_ffn_pre", "norm_ffn_post", "w_in_even", "lb_table", "a_norm",
             "b_ln_g", "b_ln_b", "b_ws", "b_bias", "w_out_even", "w_in_odd", "w_out_odd", "w_ff1", "w_ff2"]
    small_names = ["norm_mix_pre", "norm_mix_post", "norm_ffn_pre", "norm_ffn_post", "lb_table", "a_norm",
                   "b_ln_g", "b_ln_b", "b_ws", "b_bias"]
    big_names = ["w_in_even", "w_out_even", "w_in_odd", "w_out_odd", "w_ff1", "w_ff2"]
    grads, deltas, new_m, new_v = {}, {}, {}, {}
    for i, nm in enumerate(small_names):
        grads[nm], deltas[nm], new_m[nm], new_v[nm] = small_g[i], small_d[i], small_nm[i], small_nv[i]
    for i, nm in enumerate(big_names):
        grads[nm], deltas[nm], new_m[nm], new_v[nm] = big[i]
    return (loss, grad_x, *[grads[n] for n in order], *[deltas[n] for n in order],
            *[new_m[n] for n in order], *[new_v[n] for n in order])
```

```python
import functools
import math

import jax
import jax.numpy as jnp
from jax import lax
from jax.experimental import pallas as pl
from jax.experimental.pallas import tpu as pltpu

F32 = jnp.float32
BF16 = jnp.bfloat16
MESH = pl.DeviceIdType.MESH

D_MODEL = 1024
SEQ = 2048
D_FF = 4096
N_CHIPS = 4
A_WIDTH = 512
A_HEADS = 4
A_DK = 128
A_CHUNK = 64
A_SUB = 16
B_WIDTH = 512
B_GROUPS = 4
B_CHUNK = 128
C_HEADS = 16
C_HEAD_DIM = 64
C_ROT_HALF = 8
C_BLOCK = 128
C_DILATIONS = (1, 4, 16)
ROPE_THETA = 500000.0
EPS = 1e-6
ADAM_LR = 0.001
ADAM_B1 = 0.9
ADAM_B2 = 0.999
ADAM_EPS = 1e-08
ADAM_WD = 0.01
ADAM_STEP = 10

ROW_TILE = 512
FFN_ROWS = 1024
WGRAD_ROWS = 2048
VMEM_LIMIT = 56 * 1024 * 1024
NEG_BIG = -1e30


def _params(sem=None):
    return pltpu.CompilerParams(dimension_semantics=sem, vmem_limit_bytes=VMEM_LIMIT)


def _dot(a, b):
    return jnp.dot(a, b, preferred_element_type=F32)


def _dot_nt(a, b):
    return lax.dot_general(a, b, (((1,), (1,)), ((), ())), preferred_element_type=F32)


def _dot_tn(a, b):
    return lax.dot_general(a, b, (((0,), (0,)), ((), ())), preferred_element_type=F32)


def _rms(x, g):
    r = lax.rsqrt(jnp.mean(x * x, axis=-1, keepdims=True) + EPS)
    return x * r * g


def _rms_bwd(x, g, dy):
    r = lax.rsqrt(jnp.mean(x * x, axis=-1, keepdims=True) + EPS)
    xh = x * r
    dg = jnp.sum(dy * xh, axis=0, keepdims=True)
    dxh = dy * g
    dx = r * (dxh - xh * jnp.mean(dxh * xh, axis=-1, keepdims=True))
    return dx, dg


def _accumulate(ref, val, first):
    @pl.when(first)
    def _():
        ref[...] = val

    @pl.when(jnp.logical_not(first))
    def _():
        ref[...] += val


N_DEV = 8
ANY = pl.BlockSpec(memory_space=pl.ANY)


def _place():
    x, y, c = lax.axis_index("x"), lax.axis_index("y"), lax.axis_index("c")
    return x, y, c, [(1 - x, y), (x, 1 - y), (1 - x, 1 - y)]


class _Exchange:
    def __init__(self, kind, arrays):
        self.kind, self.arrays, self.n = kind, list(arrays), len(arrays)
        per_peer = pltpu.SemaphoreType.DMA((3 * self.n,))
        if kind == "gather":
            self.out_shape = [jax.ShapeDtypeStruct((N_CHIPS,) + a.shape, a.dtype) for a in self.arrays]
            self.scratch = [per_peer, per_peer, pltpu.SemaphoreType.DMA((self.n,)), per_peer, per_peer]
        else:
            self.out_shape = [jax.ShapeDtypeStruct(a.shape, a.dtype) for a in self.arrays]
            self.scratch = [per_peer, per_peer, pltpu.SemaphoreType.DMA((self.n,))]

    def _copies(self, ins, outs, sems):
        send_sems, recv_sems, local_sems = sems[:3]
        x, y, c, chips = _place()
        me = 2 * x + y
        local, remote = [], []
        for a in range(self.n):
            if self.kind == "gather":
                local.append(pltpu.make_async_copy(ins[a], outs[a].at[me], local_sems.at[a]))
                half = self.arrays[a].shape[0] // 2

                def rows(ref, core, half=half):
                    return ref.at[pl.ds(core * half, half)]
            else:
                local.append(pltpu.make_async_copy(ins[a].at[me], outs[a].at[3], local_sems.at[a]))
            for j, (px, py) in enumerate(chips):
                k = 3 * a + j
                peer = 2 * px + py

                def copy(src, dst, to, send_sem=send_sems.at[k], recv_sem=recv_sems.at[k]):
                    return pltpu.make_async_remote_copy(src_ref=src, dst_ref=dst, send_sem=send_sem, recv_sem=recv_sem,
                                                        device_id=to, device_id_type=MESH)

                if self.kind == "gather":
                    sent = copy(rows(ins[a], c), rows(outs[a].at[me], c), (px, py, c))
                    landed = copy(rows(ins[a], c), rows(outs[a].at[peer], c), (px, py, c))
                    on = dict(send_sem=sems[3].at[k], recv_sem=sems[4].at[k])
                    passed = copy(rows(outs[a].at[peer], c), rows(outs[a].at[peer], c), (x, y, 1 - c), **on)
                    handed = copy(rows(outs[a].at[peer], c), rows(outs[a].at[peer], 1 - c), (x, y, 1 - c), **on)
                    remote.append((sent, landed, passed, handed))
                else:
                    sent = copy(ins[a].at[peer], outs[a].at[j], (px, py, c))
                    remote.append((sent, sent, None, None))
        return local, remote

    def start(self, ins, outs, sems):
        local, remote = self._copies(ins, outs, sems)
        for cp in local:
            cp.start()
        for sent, _, _, _ in remote:
            sent.start()

    def finish(self, ins, outs, sems):
        local, remote = self._copies(ins, outs, sems)
        for _, landed, passed, _ in remote:
            landed.wait_recv()
            if passed is not None:
                passed.start()
        for sent, _, passed, handed in remote:
            if passed is not None:
                handed.wait_recv()
                passed.wait_send()
            sent.wait_send()
        for cp in local:
            cp.wait()


class _Swap:
    def __init__(self, arrays):
        self.arrays, self.n = list(arrays), len(arrays)
        self.out_shape = [jax.ShapeDtypeStruct(a.shape, a.dtype) for a in self.arrays]
        self.scratch = [pltpu.SemaphoreType.DMA((self.n,)), pltpu.SemaphoreType.DMA((self.n,))]

    def _copies(self, ins, outs, sems):
        x, y, c, _ = _place()
        return [pltpu.make_async_remote_copy(src_ref=ins[a], dst_ref=outs[a], send_sem=sems[0].at[a],
                                             recv_sem=sems[1].at[a], device_id=(x, y, 1 - c), device_id_type=MESH)
                for a in range(self.n)]

    def start(self, ins, outs, sems):
        for cp in self._copies(ins, outs, sems):
            cp.start()

    def finish(self, ins, outs, sems):
        for cp in self._copies(ins, outs, sems):
            cp.wait_recv()
            cp.wait_send()


class _Both:
    def __init__(self, first, second):
        self.parts = (first, second)
        self.arrays, self.n = first.arrays + second.arrays, first.n + second.n
        self.out_shape = first.out_shape + second.out_shape
        self.scratch = first.scratch + second.scratch

    def _split(self, ins, outs, sems):
        a, b = self.parts
        return ((a, ins[:a.n], outs[:a.n], sems[:len(a.scratch)]),
                (b, ins[a.n:], outs[a.n:], sems[len(a.scratch):]))

    def start(self, ins, outs, sems):
        for ex, i, o, s in self._split(ins, outs, sems):
            ex.start(i, o, s)

    def finish(self, ins, outs, sems):
        for ex, i, o, s in self._split(ins, outs, sems):
            ex.finish(i, o, s)


def _call(body, *, name, grid, in_specs, out_specs, out_shape, args, scratch_shapes=(), aliases=None, exchange=None):
    if exchange is None:
        return pl.pallas_call(
            body, name=name, grid=grid, in_specs=in_specs, out_specs=out_specs, out_shape=out_shape,
            scratch_shapes=list(scratch_shapes), input_output_aliases=aliases or {},
            compiler_params=_params(("arbitrary",) * len(grid)))(*args)
    n_in, n_out, n_scr, n_ex = len(in_specs), len(out_specs), len(scratch_shapes), exchange.n
    steps = grid

    def wrapped(*refs):
        ins, refs = refs[:n_in], refs[n_in:]
        ex_in, refs = refs[:n_ex], refs[n_ex:]
        outs, refs = refs[:n_out], refs[n_out:]
        ex_out, refs = refs[:n_ex], refs[n_ex:]
        scr, sems = refs[:n_scr], refs[n_scr:]
        first = functools.reduce(jnp.logical_and, [pl.program_id(k) == 0 for k in range(len(steps))])
        last = functools.reduce(jnp.logical_and, [pl.program_id(k) == steps[k] - 1 for k in range(len(steps))])

        @pl.when(first)
        def _():
            exchange.start(ex_in, ex_out, sems)

        body(*ins, *outs, *scr)

        @pl.when(last)
        def _():
            exchange.finish(ex_in, ex_out, sems)

    return pl.pallas_call(
        wrapped, name=name, grid=grid,
        in_specs=list(in_specs) + [ANY] * n_ex, out_specs=list(out_specs) + [ANY] * n_ex,
        out_shape=list(out_shape) + exchange.out_shape,
        scratch_shapes=list(scratch_shapes) + exchange.scratch, input_output_aliases=aliases or {},
        compiler_params=_params(("arbitrary",) * len(grid)))(*args, *exchange.arrays)


def exchange_alone(exchange, name):
    def body(*refs):
        n = exchange.n
        exchange.start(refs[:n], refs[n:2 * n], refs[2 * n:])
        exchange.finish(refs[:n], refs[n:2 * n], refs[2 * n:])

    return pl.pallas_call(
        body, name=name, in_specs=[ANY] * exchange.n, out_specs=[ANY] * exchange.n,
        out_shape=exchange.out_shape, scratch_shapes=exchange.scratch)(*exchange.arrays)


def norm_matmul(x, g, wg, name, exchange=None):
    t, d = x.shape
    nl = wg.shape[2]

    def body(x_ref, g_ref, w_ref, o_ref, h_ref):
        h = _rms(x_ref[...], g_ref[...]).astype(BF16)
        h_ref[...] = h
        for c in range(N_CHIPS):
            o_ref[:, c * nl:(c + 1) * nl] = _dot(h, w_ref[c])

    return _call(
        body, name=name, grid=(t // ROW_TILE,),
        in_specs=[pl.BlockSpec((ROW_TILE, d), lambda i: (i, 0)),
                  pl.BlockSpec((1, d), lambda i: (0, 0)),
                  pl.BlockSpec((N_CHIPS, d, nl), lambda i: (0, 0, 0))],
        out_specs=[pl.BlockSpec((ROW_TILE, N_CHIPS * nl), lambda i: (i, 0)),
                   pl.BlockSpec((ROW_TILE, d), lambda i: (i, 0))],
        out_shape=[jax.ShapeDtypeStruct((t, N_CHIPS * nl), F32), jax.ShapeDtypeStruct((t, d), BF16)],
        args=(x, g, wg), exchange=exchange)


def norm_matmul_bwd(dproj, wg, x, g, dres, name, exchange=None):
    t, d = x.shape
    nl = wg.shape[2]
    stacked = dproj.ndim == 3
    piece = math.gcd(nl, dproj.shape[-1])

    def body(dp_ref, w_ref, x_ref, g_ref, dres_ref, dx_ref, dg_ref):
        dh = None
        for j in range(N_CHIPS * nl // piece):
            c, off = divmod(j * piece, nl)
            if stacked:
                p, lo = divmod(j * piece, dproj.shape[-1])
                lhs = dp_ref[p, :, lo:lo + piece]
            else:
                lhs = dp_ref[:, j * piece:(j + 1) * piece]
            part = _dot_nt(lhs.astype(BF16), w_ref[c, :, off:off + piece])
            dh = part if dh is None else dh + part
        dx, dg = _rms_bwd(x_ref[...], g_ref[...], dh)
        dx_ref[...] = dres_ref[...] + dx
        _accumulate(dg_ref, dg, pl.program_id(0) == 0)

    row = pl.BlockSpec((ROW_TILE, d), lambda i: (i, 0))
    vec = pl.BlockSpec((1, d), lambda i: (0, 0))
    if stacked:
        dp_spec = pl.BlockSpec((dproj.shape[0], ROW_TILE, dproj.shape[-1]), lambda i: (0, i, 0))
    else:
        dp_spec = pl.BlockSpec((ROW_TILE, N_CHIPS * nl), lambda i: (i, 0))
    return _call(
        body, name=name, grid=(t // ROW_TILE,),
        in_specs=[dp_spec, pl.BlockSpec((N_CHIPS, d, nl), lambda i: (0, 0, 0)), row, vec, row],
        out_specs=[row, vec],
        out_shape=[jax.ShapeDtypeStruct((t, d), F32), jax.ShapeDtypeStruct((1, d), F32)],
        args=(dproj, wg, x, g, dres), exchange=exchange)


def out_proj(a, wg, x, g, name):
    t, d = x.shape
    kl = wg.shape[1]

    def body(a_ref, w_ref, x_ref, g_ref, mix_ref, xo_ref):
        acc = _dot(a_ref[:, 0:kl], w_ref[0])
        for c in range(1, N_CHIPS):
            acc += _dot(a_ref[:, c * kl:(c + 1) * kl], w_ref[c])
        mix_ref[...] = acc
        xo_ref[...] = x_ref[...] + _rms(acc, g_ref[...])

    row = pl.BlockSpec((ROW_TILE, d), lambda i: (i, 0))
    return pl.pallas_call(
        body, name=name, grid=(t // ROW_TILE,),
        in_specs=[row, pl.BlockSpec((N_CHIPS, kl, d), lambda i: (0, 0, 0)), row,
                  pl.BlockSpec((1, d), lambda i: (0, 0))],
        out_specs=[row, row],
        out_shape=[jax.ShapeDtypeStruct((t, d), F32), jax.ShapeDtypeStruct((t, d), F32)],
        compiler_params=_params(("arbitrary",)),
    )(a, wg, x, g)


def out_proj_bwd(dxo, mix, g, wg, name):
    t, d = mix.shape
    kl = wg.shape[1]

    def body(dxo_ref, mix_ref, g_ref, w_ref, dmix_ref, da_ref, dg_ref):
        dmix, dg = _rms_bwd(mix_ref[...], g_ref[...], dxo_ref[...])
        dmb = dmix.astype(BF16)
        dmix_ref[...] = dmb
        for c in range(N_CHIPS):
            da_ref[:, c * kl:(c + 1) * kl] = _dot_nt(dmb, w_ref[c])
        _accumulate(dg_ref, dg, pl.program_id(0) == 0)

    row = pl.BlockSpec((ROW_TILE, d), lambda i: (i, 0))
    vec = pl.BlockSpec((1, d), lambda i: (0, 0))
    return pl.pallas_call(
        body, name=name, grid=(t // ROW_TILE,),
        in_specs=[row, row, vec, pl.BlockSpec((N_CHIPS, kl, d), lambda i: (0, 0, 0))],
        out_specs=[row, row, vec],
        out_shape=[jax.ShapeDtypeStruct((t, d), BF16), jax.ShapeDtypeStruct((t, d), F32),
                   jax.ShapeDtypeStruct((1, d), F32)],
        compiler_params=_params(("arbitrary",)),
    )(dxo, mix, g, wg)


def ffn_fwd(x, gpre, w1g, w2g, gpost, name, exchange=None, target=None):
    t, d = x.shape
    hc = w1g.shape[2]
    with_loss = target is not None

    def body(x_ref, gpre_ref, w1_ref, w2_ref, gpost_ref, *rest):
        if with_loss:
            t_ref, xo_ref, h_ref, a_ref, y_ref, l_ref, acc = rest
        else:
            xo_ref, h_ref, a_ref, y_ref, acc = rest
        i, c = pl.program_id(0), pl.program_id(1)

        @pl.when(c == 0)
        def _():
            h_ref[...] = _rms(x_ref[...], gpre_ref[...]).astype(BF16)

        a = _dot(h_ref[...], w1_ref[...])
        a_ref[...] = a.astype(BF16)
        r = jnp.square(jnp.maximum(a, 0.0)).astype(BF16)
        _accumulate(acc, _dot(r, w2_ref[...]), c == 0)

        @pl.when(c == N_CHIPS - 1)
        def _():
            y = acc[...]
            y_ref[...] = y
            xo = x_ref[...] + _rms(y, gpost_ref[...])
            if with_loss:
                e = xo - t_ref[...]
                xo_ref[...] = e * (1.0 / d)
                part = jnp.sum(jnp.sum(e * e, axis=-1, keepdims=True), axis=0, keepdims=True) * (0.5 / d)
                _accumulate(l_ref, part, i == 0)
            else:
                xo_ref[...] = xo

    row = pl.BlockSpec((FFN_ROWS, d), lambda i, c: (i, 0))
    vec = pl.BlockSpec((1, d), lambda i, c: (0, 0))
    one = pl.BlockSpec((1, 1), lambda i, c: (0, 0))
    return _call(
        body, name=name, grid=(t // FFN_ROWS, N_CHIPS),
        in_specs=[row, vec,
                  pl.BlockSpec((None, d, hc), lambda i, c: (c, 0, 0)),
                  pl.BlockSpec((None, hc, d), lambda i, c: (c, 0, 0)), vec] + ([row] if with_loss else []),
        out_specs=[row, row, pl.BlockSpec((FFN_ROWS, hc), lambda i, c: (i, c)), row] + ([one] if with_loss else []),
        out_shape=[jax.ShapeDtypeStruct((t, d), F32), jax.ShapeDtypeStruct((t, d), BF16),
                   jax.ShapeDtypeStruct((t, N_CHIPS * hc), BF16), jax.ShapeDtypeStruct((t, d), F32)]
        + ([jax.ShapeDtypeStruct((1, 1), F32)] if with_loss else []),
        scratch_shapes=[pltpu.VMEM((FFN_ROWS, d), F32)],
        args=(x, gpre, w1g, w2g, gpost) + ((target,) if with_loss else ()), exchange=exchange)


def ffn_bwd(dxo, x, y, a, gpre, gpost, w1g, w2g, name, exchange=None):
    t, d = x.shape
    hc = w1g.shape[2]

    def body(dxo_ref, x_ref, y_ref, a_ref, gpre_ref, gpost_ref, w1_ref, w2_ref,
             dxi_ref, dy_ref, da_ref, dgpre_ref, dgpost_ref, acc):
        i, c = pl.program_id(0), pl.program_id(1)

        @pl.when(c == 0)
        def _():
            dy, dg = _rms_bwd(y_ref[...], gpost_ref[...], dxo_ref[...])
            dy_ref[...] = dy.astype(BF16)
            _accumulate(dgpost_ref, dg, i == 0)

        dr = _dot_nt(dy_ref[...], w2_ref[...])
        da = (dr * (2.0 * jnp.maximum(a_ref[...].astype(F32), 0.0))).astype(BF16)
        da_ref[...] = da
        _accumulate(acc, _dot_nt(da, w1_ref[...]), c == 0)

        @pl.when(c == N_CHIPS - 1)
        def _():
            dx, dg = _rms_bwd(x_ref[...], gpre_ref[...], acc[...])
            dxi_ref[...] = dxo_ref[...] + dx
            _accumulate(dgpre_ref, dg, i == 0)

    row = pl.BlockSpec((ROW_TILE, d), lambda i, c: (i, 0))
    vec = pl.BlockSpec((1, d), lambda i, c: (0, 0))
    hid = pl.BlockSpec((ROW_TILE, hc), lambda i, c: (i, c))
    return _call(
        body, name=name, grid=(t // ROW_TILE, N_CHIPS),
        in_specs=[row, row, row, hid, vec, vec,
                  pl.BlockSpec((None, d, hc), lambda i, c: (c, 0, 0)),
                  pl.BlockSpec((None, hc, d), lambda i, c: (c, 0, 0))],
        out_specs=[row, row, hid, vec, vec],
        out_shape=[jax.ShapeDtypeStruct((t, d), F32), jax.ShapeDtypeStruct((t, d), BF16),
                   jax.ShapeDtypeStruct((t, N_CHIPS * hc), BF16),
                   jax.ShapeDtypeStruct((1, d), F32), jax.ShapeDtypeStruct((1, d), F32)],
        scratch_shapes=[pltpu.VMEM((ROW_TILE, d), F32)],
        args=(dxo, x, y, a, gpre, gpost, w1g, w2g), exchange=exchange)


def weight_grad(a, b, chunked, bk, bn, relu2, name, exchange=None):
    t = a.shape[0]
    a_on = chunked == "a"
    rows = min(t, WGRAD_ROWS)
    n_steps = t // rows

    def body(a_ref, b_ref, o_ref, acc):
        s = pl.program_id(1)
        av = a_ref[...]
        if relu2:
            av = jnp.square(jnp.maximum(av.astype(F32), 0.0))
        _accumulate(acc, _dot_tn(av.astype(BF16), b_ref[...].astype(BF16)), s == 0)

        @pl.when(s == n_steps - 1)
        def _():
            o_ref[...] = acc[...].astype(BF16)

    res = _call(
        body, name=name, grid=(N_CHIPS, n_steps),
        in_specs=[pl.BlockSpec((rows, bk), (lambda c, s: (s, c)) if a_on else (lambda c, s: (s, 0))),
                  pl.BlockSpec((rows, bn), (lambda c, s: (s, 0)) if a_on else (lambda c, s: (s, c)))],
        out_specs=[pl.BlockSpec((None, bk, bn), lambda c, s: (c, 0, 0))],
        out_shape=[jax.ShapeDtypeStruct((N_CHIPS, bk, bn), BF16)],
        scratch_shapes=[pltpu.VMEM((bk, bn), F32)],
        args=(a, b), exchange=exchange)
    return res[0] if exchange is None else res


def weight_grad_stacked(a, b3, bn, name):
    t, bk = a.shape
    width = b3.shape[-1]
    piece = math.gcd(bn, width)
    rows = min(t, WGRAD_ROWS)
    n_steps = t // rows

    def body(a_ref, b_ref, o_hbm, acc, staged, sem):
        s, c = pl.program_id(0), pl.program_id(1)
        av = a_ref[...].astype(BF16)
        for chunk in range(N_CHIPS):
            @pl.when(c == chunk)
            def _(chunk=chunk):
                cols = [divmod(chunk * bn + k * piece, width) for k in range(bn // piece)]
                b = jnp.concatenate([b_ref[p, :, lo:lo + piece] for p, lo in cols], axis=1).astype(BF16)
                _accumulate(acc.at[chunk], _dot_tn(av, b), s == 0)

                @pl.when(s == n_steps - 1)
                def _():
                    staged[...] = acc[chunk].astype(BF16)
                    copy = pltpu.make_async_copy(staged, o_hbm.at[chunk], sem)
                    copy.start()
                    copy.wait()

    return pl.pallas_call(
        body, name=name, grid=(n_steps, N_CHIPS),
        in_specs=[pl.BlockSpec((rows, bk), lambda s, c: (s, 0)),
                  pl.BlockSpec((b3.shape[0], rows, width), lambda s, c: (0, s, 0))],
        out_specs=ANY,
        out_shape=jax.ShapeDtypeStruct((N_CHIPS, bk, bn), BF16),
        scratch_shapes=[pltpu.VMEM((N_CHIPS, bk, bn), F32), pltpu.VMEM((bk, bn), BF16), pltpu.SemaphoreType.DMA],
        compiler_params=_params(("arbitrary", "arbitrary")),
    )(a, b3)


def _hgrn2_chunk(st, qs, fls, ivs, gls, l0, l1, l2, ng):
    nsub = len(qs)
    mx = jnp.maximum(jnp.maximum(l0, l1), l2)
    e0, e1, e2 = jnp.exp(l0 - mx), jnp.exp(l1 - mx), jnp.exp(l2 - mx)
    lb = e0 / (e0 + e1 + e2)
    rows = lax.broadcasted_iota(jnp.int32, (A_SUB, A_SUB), 0)
    cols = lax.broadcasted_iota(jnp.int32, (A_SUB, A_SUB), 1)
    tri = (rows >= cols).astype(F32)
    keep = (lax.broadcasted_iota(jnp.int32, (A_SUB, A_SUB, A_DK), 0)
            >= lax.broadcasted_iota(jnp.int32, (A_SUB, A_SUB, A_DK), 1))
    base = jnp.zeros_like(l0)
    bases, gs, ks, qfs = [], [], [], []
    for i in range(nsub):
        f = lb + (1.0 - lb) * jax.nn.sigmoid(fls[i])
        logf = jnp.log(f)
        bases.append(base)
        gs.append(base + jnp.dot(tri, logf, precision=lax.Precision.HIGHEST, preferred_element_type=F32))
        base = base + jnp.sum(logf, axis=0, keepdims=True)
        ks.append(1.0 - f)
        qfs.append(jax.nn.silu(qs[i]))
    g_last = base
    stb = st.astype(BF16)
    outs = []
    for i in range(nsub):
        o = _dot_nt((qfs[i] * jnp.exp(gs[i])).astype(BF16), stb)
        if i > 0:
            qt = (qfs[i] * jnp.exp(gs[i] - bases[i])).astype(BF16)
            kk = jnp.concatenate([ks[j] * jnp.exp(bases[i] - gs[j]) for j in range(i)], axis=0).astype(BF16)
            vv = jnp.concatenate(ivs[:i], axis=0).astype(BF16)
            o = o + _dot(_dot_nt(qt, kk).astype(BF16), vv)
        dec = jnp.exp(jnp.where(keep, gs[i][:, None, :] - gs[i][None, :, :], NEG_BIG))
        s_diag = jnp.sum(qfs[i][:, None, :] * ks[i][None, :, :] * dec, axis=-1)
        o = o + _dot(s_diag.astype(BF16), ivs[i].astype(BF16))
        o = o * lax.rsqrt(jnp.mean(o * o, axis=-1, keepdims=True) + EPS) * ng
        outs.append(o * jax.nn.silu(gls[i]))
    kdec = jnp.concatenate([ks[j] * jnp.exp(g_last - gs[j]) for j in range(nsub)], axis=0).astype(BF16)
    vall = jnp.concatenate(ivs, axis=0).astype(BF16)
    new_st = st * jnp.exp(g_last) + _dot_tn(vall, kdec)
    return new_st, outs


A_MAX_LOG_DECAY = 60.0


def _half_sums(logf):
    n = logf.shape[0]
    first = lax.broadcasted_iota(jnp.int32, logf.shape, 0) < n // 2
    return (jnp.sum(jnp.where(first, logf, 0.0), axis=0, keepdims=True),
            jnp.sum(jnp.where(first, 0.0, logf), axis=0, keepdims=True))


def _split3(x):
    hi = x.astype(BF16)
    r1 = x - hi.astype(F32)
    mid = r1.astype(BF16)
    return hi, mid, (r1 - mid.astype(F32)).astype(BF16)


def _tri_matmul(x, transpose):
    n = x.shape[0]
    r = lax.broadcasted_iota(jnp.int32, (n, n), 0)
    c = lax.broadcasted_iota(jnp.int32, (n, n), 1)
    tri = ((r <= c) if transpose else (r >= c)).astype(BF16)
    hi, mid, lo = _split3(x)
    return (_dot(tri, lo) + _dot(tri, mid)) + _dot(tri, hi)


@jax.custom_vjp
def _cumsum_rows(x):
    return _tri_matmul(x, False)


def _cumsum_rows_fwd(x):
    return _tri_matmul(x, False), None


def _cumsum_rows_bwd(_, dy):
    return (_tri_matmul(dy, True),)


_cumsum_rows.defvjp(_cumsum_rows_fwd, _cumsum_rows_bwd)


def _lower_bound(l0, l1, l2):
    mx = jnp.maximum(jnp.maximum(l0, l1), l2)
    e0, e1, e2 = jnp.exp(l0 - mx), jnp.exp(l1 - mx), jnp.exp(l2 - mx)
    return e0 / (e0 + e1 + e2)


def _b(x):
    return x.astype(BF16)


@jax.custom_vjp
def _mm(a, b):
    return _dot(_b(a), _b(b))


_mm.defvjp(lambda a, b: (_mm(a, b), (a, b)),
           lambda res, d: (_dot_nt(_b(d), _b(res[1])), _dot_tn(_b(res[0]), _b(d))))


@jax.custom_vjp
def _mm_nt(a, b):
    return _dot_nt(_b(a), _b(b))


_mm_nt.defvjp(lambda a, b: (_mm_nt(a, b), (a, b)),
              lambda res, d: (_dot(_b(d), _b(res[1])), _dot_tn(_b(d), _b(res[0]))))


def _dot_split(dot, a, b):
    ah, bh = _b(a), _b(b)
    al, bl = _b(a - ah.astype(F32)), _b(b - bh.astype(F32))
    return (dot(ah, bl) + dot(al, bh)) + dot(ah, bh)


@jax.custom_vjp
def _mm_scores(a, b):
    return _dot_nt(_b(a), _b(b))


_mm_scores.defvjp(lambda a, b: (_mm_scores(a, b), (a, b)),
                  lambda res, d: (_dot_split(_dot, d, res[1]), _dot_split(_dot_tn, d, res[0])))


@jax.custom_vjp
def _mm_tn(a, b):
    return _dot_tn(_b(a), _b(b))


_mm_tn.defvjp(lambda a, b: (_mm_tn(a, b), (a, b)),
              lambda res, d: (_dot_nt(_b(res[1]), _b(d)), _dot(_b(res[0]), _b(d))))


@jax.custom_vjp
def _split_heads(x):
    return tuple(x[:, h * A_DK:(h + 1) * A_DK] for h in range(A_HEADS))


def _split_heads_fwd(x):
    return _split_heads(x), None


def _split_heads_bwd(_, parts):
    return (jnp.concatenate(parts, axis=1),)


_split_heads.defvjp(_split_heads_fwd, _split_heads_bwd)


def _hgrn2_chunk_fast(sts, q, fl, iv, gl, l0, l1, l2, ng):
    lb = _lower_bound(l0, l1, l2)
    f = lb + (1.0 - lb) * jax.nn.sigmoid(fl)
    return _hgrn2_fast_core(sts, q, f, jnp.log(f), iv, gl, ng)


def _hgrn2_fast_core(sts, q, f, logf, iv, gl, ng):
    g = _cumsum_rows(logf)
    g_mid, g_last = _half_sums(logf)
    g_last = g_mid + g_last
    k = 1.0 - f
    qf = jax.nn.silu(q)
    qms = _split_heads(qf * jnp.exp(g - g_mid))
    kms = _split_heads(k * jnp.exp(g_mid - g))
    qgs = _split_heads(qf * jnp.exp(g))
    kds = _split_heads(k * jnp.exp(g_last - g))
    ivs = _split_heads(iv)
    decays = _split_heads(jnp.exp(g_last))
    n = q.shape[0]
    causal = lax.broadcasted_iota(jnp.int32, (n, n), 0) >= lax.broadcasted_iota(jnp.int32, (n, n), 1)
    raw = [_mm_scores(qm, km) for qm, km in zip(qms, kms)]
    inter = [_mm_nt(qg, st) for qg, st in zip(qgs, sts)]
    scores = [jnp.where(causal, s, 0.0) for s in raw]
    os = [a + _mm(s, v) for a, s, v in zip(inter, scores, ivs)]
    new_sts = [st * d + _mm_tn(v, kd) for st, d, v, kd in zip(sts, decays, ivs, kds)]
    os = [o * lax.rsqrt(jnp.mean(o * o, axis=-1, keepdims=True) + EPS) for o in os]
    return new_sts, jnp.concatenate(os, axis=1) * ng * jax.nn.silu(gl)


A_STEP_CHUNKS = 4


def _chunk_rows(j):
    return pl.ds(pl.multiple_of(j * A_CHUNK, A_CHUNK), A_CHUNK)


def _sub_rows(j, i):
    return pl.ds(pl.multiple_of(j * A_CHUNK + i * A_SUB, A_SUB), A_SUB)


def _sub_blocks(ref, head, j):
    lanes = slice(head * A_DK, (head + 1) * A_DK)
    return [ref[_sub_rows(j, i), lanes] for i in range(A_CHUNK // A_SUB)]


def hgrn2_fwd(proj, lb_table, a_norm, batch, name, exchange=None):
    t = proj.shape[0]
    n_steps = t // batch // (A_CHUNK * A_STEP_CHUNKS)
    rows = A_CHUNK * A_STEP_CHUNKS

    def body(q_ref, f_ref, i_ref, g_ref, lb_ref, ng_ref, o_ref, st_ref, dec_ref, st):
        @pl.when(pl.program_id(1) == 0)
        def _():
            st[...] = jnp.zeros_like(st)

        def chunk(j, carry):
            r = _chunk_rows(j)
            st_ref[j] = st[...]
            lb = _lower_bound(lb_ref[0:1, :], lb_ref[1:2, :], lb_ref[2:3, :])
            f = lb + (1.0 - lb) * jax.nn.sigmoid(f_ref[r, :])
            logf = jnp.log(f)
            decay = jnp.minimum(*_half_sums(logf))
            dec_ref[j] = decay
            mild = jnp.min(decay) >= -A_MAX_LOG_DECAY

            @pl.when(mild)
            def _():
                new_sts, o = _hgrn2_fast_core([st[h] for h in range(A_HEADS)], q_ref[r, :], f, logf,
                                              i_ref[r, :], g_ref[r, :], ng_ref[...])
                for h in range(A_HEADS):
                    st[h] = new_sts[h]
                o_ref[r, :] = o.astype(BF16)

            @pl.when(jnp.logical_not(mild))
            def _():
                for h in range(A_HEADS):
                    lanes = slice(h * A_DK, (h + 1) * A_DK)
                    new_st, outs = _hgrn2_chunk(
                        st[h], _sub_blocks(q_ref, h, j), _sub_blocks(f_ref, h, j), _sub_blocks(i_ref, h, j),
                        _sub_blocks(g_ref, h, j), lb_ref[0:1, lanes], lb_ref[1:2, lanes], lb_ref[2:3, lanes],
                        ng_ref[:, lanes])
                    st[h] = new_st
                    for i, o in enumerate(outs):
                        o_ref[_sub_rows(j, i), lanes] = o.astype(BF16)

            return carry

        lax.fori_loop(0, A_STEP_CHUNKS, chunk, 0)

    def part(k):
        return pl.BlockSpec((rows, A_WIDTH), lambda b, n: (b * n_steps + n, k))

    return _call(
        body, name=name, grid=(batch, n_steps),
        in_specs=[part(0), part(1), part(2), part(3),
                  pl.BlockSpec((3, A_WIDTH), lambda b, n: (0, 0)), pl.BlockSpec((1, A_WIDTH), lambda b, n: (0, 0))],
        out_specs=[part(0),
                   pl.BlockSpec((A_STEP_CHUNKS, A_HEADS, A_DK, A_DK), lambda b, n: (b * n_steps + n, 0, 0, 0)),
                   pl.BlockSpec((A_STEP_CHUNKS, 1, A_WIDTH), lambda b, n: (b * n_steps + n, 0, 0))],
        out_shape=[jax.ShapeDtypeStruct((t, A_WIDTH), BF16),
                   jax.ShapeDtypeStruct((t // A_CHUNK, A_HEADS, A_DK, A_DK), F32),
                   jax.ShapeDtypeStruct((t // A_CHUNK, 1, A_WIDTH), F32)],
        scratch_shapes=[pltpu.VMEM((A_HEADS, A_DK, A_DK), F32)],
        args=(proj, proj, proj, proj, lb_table, a_norm), exchange=exchange)


def hgrn2_bwd(proj, states, decays, lb_table, a_norm, do, batch, name, exchange=None):
    t = proj.shape[0]
    n_steps = t // batch // (A_CHUNK * A_STEP_CHUNKS)
    rows = A_CHUNK * A_STEP_CHUNKS

    def body(q_ref, f_ref, i_ref, g_ref, st_ref, dec_ref, lb_ref, ng_ref, do_ref, dp_ref, dlb_ref, dng_ref, dst):
        @pl.when(jnp.logical_and(pl.program_id(0) == 0, pl.program_id(1) == 0))
        def _():
            dlb_ref[...] = jnp.zeros_like(dlb_ref)
            dng_ref[...] = jnp.zeros_like(dng_ref)

        @pl.when(pl.program_id(1) == 0)
        def _():
            dst[...] = jnp.zeros_like(dst)

        def chunk(jj, carry):
            j = A_STEP_CHUNKS - 1 - jj
            r = _chunk_rows(j)
            mild = jnp.min(dec_ref[j]) >= -A_MAX_LOG_DECAY

            @pl.when(mild)
            def _():
                _, vjp = jax.vjp(
                    _hgrn2_chunk_fast, [st_ref[j, h] for h in range(A_HEADS)], q_ref[r, :], f_ref[r, :],
                    i_ref[r, :], g_ref[r, :], lb_ref[0:1, :], lb_ref[1:2, :], lb_ref[2:3, :], ng_ref[...])
                d_sts, dq, df, di, dg, dl0, dl1, dl2, dng = vjp(
                    ([dst[h] for h in range(A_HEADS)], do_ref[r, :].astype(F32)))
                for h in range(A_HEADS):
                    dst[h] = d_sts[h]
                for k, part in enumerate((dq, df, di, dg)):
                    dp_ref[r, k * A_WIDTH:(k + 1) * A_WIDTH] = part
                for row, val in enumerate((dl0, dl1, dl2)):
                    dlb_ref[row:row + 1, :] += val
                dng_ref[...] += dng

            @pl.when(jnp.logical_not(mild))
            def _():
                for h in range(A_HEADS):
                    lanes = slice(h * A_DK, (h + 1) * A_DK)
                    _, vjp = jax.vjp(
                        _hgrn2_chunk, st_ref[j, h], _sub_blocks(q_ref, h, j), _sub_blocks(f_ref, h, j),
                        _sub_blocks(i_ref, h, j), _sub_blocks(g_ref, h, j), lb_ref[0:1, lanes], lb_ref[1:2, lanes],
                        lb_ref[2:3, lanes], ng_ref[:, lanes])
                    douts = [x.astype(F32) for x in _sub_blocks(do_ref, h, j)]
                    d_st, dqs, dfs, dis, dgs, dl0, dl1, dl2, dng = vjp((dst[h], douts))
                    dst[h] = d_st
                    for k, parts in enumerate((dqs, dfs, dis, dgs)):
                        for i in range(A_CHUNK // A_SUB):
                            dp_ref[_sub_rows(j, i), k * A_WIDTH + h * A_DK:k * A_WIDTH + (h + 1) * A_DK] = parts[i]
                    for row, val in enumerate((dl0, dl1, dl2)):
                        dlb_ref[row:row + 1, lanes] += val
                    dng_ref[:, lanes] += dng

            return carry

        lax.fori_loop(0, A_STEP_CHUNKS, chunk, 0)

    def rev(b, n):
        return b * n_steps + (n_steps - 1 - n)

    def part(k):
        return pl.BlockSpec((rows, A_WIDTH), lambda b, n: (rev(b, n), k))

    const3 = pl.BlockSpec((3, A_WIDTH), lambda b, n: (0, 0))
    const1 = pl.BlockSpec((1, A_WIDTH), lambda b, n: (0, 0))
    return _call(
        body, name=name, grid=(batch, n_steps),
        in_specs=[part(0), part(1), part(2), part(3),
                  pl.BlockSpec((A_STEP_CHUNKS, A_HEADS, A_DK, A_DK), lambda b, n: (rev(b, n), 0, 0, 0)),
                  pl.BlockSpec((A_STEP_CHUNKS, 1, A_WIDTH), lambda b, n: (rev(b, n), 0, 0)),
                  const3, const1, part(0)],
        out_specs=[pl.BlockSpec((rows, 4 * A_WIDTH), lambda b, n: (rev(b, n), 0)), const3, const1],
        out_shape=[jax.ShapeDtypeStruct((t, 4 * A_WIDTH + 2 * B_WIDTH), F32),
                   jax.ShapeDtypeStruct((3, A_WIDTH), F32), jax.ShapeDtypeStruct((1, A_WIDTH), F32)],
        scratch_shapes=[pltpu.VMEM((A_HEADS, A_DK, A_DK), F32)],
        args=(proj, proj, proj, proj, states, decays, lb_table, a_norm, do), exchange=exchange)


B_GDIM = B_WIDTH // B_GROUPS
B_ROWS = 512


def _gmlp_chunk(ubs, vbs, lngs, lnbs, ws, bcols):
    vs = [jax.nn.gelu(v) for v in vbs]
    mu = sum(jnp.sum(v, axis=-1, keepdims=True) for v in vs) * (1.0 / B_WIDTH)
    var = sum(jnp.sum(jnp.square(v - mu), axis=-1, keepdims=True) for v in vs) * (1.0 / B_WIDTH)
    rstd = lax.rsqrt(var + EPS)
    tril = (lax.broadcasted_iota(jnp.int32, (B_CHUNK, B_CHUNK), 0)
            >= lax.broadcasted_iota(jnp.int32, (B_CHUNK, B_CHUNK), 1))
    outs = []
    for g in range(B_GROUPS):
        vn = (vs[g] - mu) * rstd * lngs[g] + lnbs[g]
        w = jnp.where(tril, ws[g], 0.0).astype(BF16)
        outs.append(jax.nn.gelu(ubs[g]) * (_dot(w, vn.astype(BF16)) + bcols[g]))
    return outs


def _gmlp_args(u_ref, v_ref, lng_ref, lnb_ref, w_ref, bt_ref, rows):
    def groups(ref):
        return [ref[rows, g * B_GDIM:(g + 1) * B_GDIM] for g in range(B_GROUPS)]

    def vec(ref):
        return [ref[:, g * B_GDIM:(g + 1) * B_GDIM] for g in range(B_GROUPS)]

    return (groups(u_ref), groups(v_ref), vec(lng_ref), vec(lnb_ref),
            [w_ref[g] for g in range(B_GROUPS)], [bt_ref[:, g:g + 1] for g in range(B_GROUPS)])


def gmlp_fwd(proj, oa, ln_g, ln_b, w, bias_t, name, exchange=None):
    t = proj.shape[0]

    def body(u_ref, v_ref, oa_ref, lng_ref, lnb_ref, w_ref, bt_ref, o_ref):
        o_ref[:, 0:A_WIDTH] = oa_ref[...]
        for n in range(B_ROWS // B_CHUNK):
            rows = slice(n * B_CHUNK, (n + 1) * B_CHUNK)
            outs = _gmlp_chunk(*_gmlp_args(u_ref, v_ref, lng_ref, lnb_ref, w_ref, bt_ref, rows))
            for g, o in enumerate(outs):
                o_ref[rows, A_WIDTH + g * B_GDIM:A_WIDTH + (g + 1) * B_GDIM] = o.astype(BF16)

    vec = pl.BlockSpec((1, B_WIDTH), lambda i: (0, 0))
    return _call(
        body, name=name, grid=(t // B_ROWS,),
        in_specs=[pl.BlockSpec((B_ROWS, B_WIDTH), lambda i: (i, 4)), pl.BlockSpec((B_ROWS, B_WIDTH), lambda i: (i, 5)),
                  pl.BlockSpec((B_ROWS, A_WIDTH), lambda i: (i, 0)), vec, vec,
                  pl.BlockSpec((B_GROUPS, B_CHUNK, B_CHUNK), lambda i: (0, 0, 0)),
                  pl.BlockSpec((B_CHUNK, B_GROUPS), lambda i: (0, 0))],
        out_specs=[pl.BlockSpec((B_ROWS, A_WIDTH + B_WIDTH), lambda i: (i, 0))],
        out_shape=[jax.ShapeDtypeStruct((t, A_WIDTH + B_WIDTH), BF16)],
        args=(proj, proj, oa, ln_g, ln_b, w, bias_t), exchange=exchange)


def gmlp_bwd(proj, dmixin, ln_g, ln_b, w, bias_t, dproj, name, exchange=None):
    t = proj.shape[0]

    def body(u_ref, v_ref, do_ref, lng_ref, lnb_ref, w_ref, bt_ref, dp_in_ref,
             dp_ref, dlng_ref, dlnb_ref, dw_ref, dbt_ref):
        del dp_in_ref

        @pl.when(pl.program_id(0) == 0)
        def _():
            for ref in (dlng_ref, dlnb_ref, dw_ref, dbt_ref):
                ref[...] = jnp.zeros_like(ref)

        for n in range(B_ROWS // B_CHUNK):
            rows = slice(n * B_CHUNK, (n + 1) * B_CHUNK)
            _, vjp = jax.vjp(_gmlp_chunk, *_gmlp_args(u_ref, v_ref, lng_ref, lnb_ref, w_ref, bt_ref, rows))
            douts = [do_ref[rows, g * B_GDIM:(g + 1) * B_GDIM] for g in range(B_GROUPS)]
            dus, dvs, dlngs, dlnbs, dws, dbs = vjp(douts)
            for g in range(B_GROUPS):
                lanes = slice(g * B_GDIM, (g + 1) * B_GDIM)
                dp_ref[rows, lanes] = dus[g]
                dp_ref[rows, B_WIDTH + g * B_GDIM:B_WIDTH + (g + 1) * B_GDIM] = dvs[g]
                dlng_ref[:, lanes] += dlngs[g]
                dlnb_ref[:, lanes] += dlnbs[g]
                dw_ref[g] += dws[g]
                dbt_ref[:, g:g + 1] += dbs[g]

    vec = pl.BlockSpec((1, B_WIDTH), lambda i: (0, 0))
    wspec = pl.BlockSpec((B_GROUPS, B_CHUNK, B_CHUNK), lambda i: (0, 0, 0))
    bspec = pl.BlockSpec((B_CHUNK, B_GROUPS), lambda i: (0, 0))
    return _call(
        body, name=name, grid=(t // B_ROWS,),
        in_specs=[pl.BlockSpec((B_ROWS, B_WIDTH), lambda i: (i, 4)), pl.BlockSpec((B_ROWS, B_WIDTH), lambda i: (i, 5)),
                  pl.BlockSpec((B_ROWS, B_WIDTH), lambda i: (i, 1)), vec, vec, wspec, bspec,
                  pl.BlockSpec(memory_space=pl.ANY)],
        out_specs=[pl.BlockSpec((B_ROWS, 2 * B_WIDTH), lambda i: (i, 2)), vec, vec, wspec, bspec],
        out_shape=[jax.ShapeDtypeStruct(dproj.shape, F32), jax.ShapeDtypeStruct((1, B_WIDTH), F32),
                   jax.ShapeDtypeStruct((1, B_WIDTH), F32), jax.ShapeDtypeStruct((B_GROUPS, B_CHUNK, B_CHUNK), F32),
                   jax.ShapeDtypeStruct((B_CHUNK, B_GROUPS), F32)],
        aliases={7: 0}, args=(proj, proj, dmixin, ln_g, ln_b, w, bias_t, dproj), exchange=exchange)


C_FWD_BLOCKS = 16
C_BWD_BLOCKS = 16
C_PAIR = 2 * C_HEAD_DIM
C_PAIRS = C_HEADS // 2
C_SCALE = 1.0 / math.sqrt(C_HEAD_DIM)
C_ROT_DIM = 2 * C_ROT_HALF
ROPE_ROWS = 1024


def rope_tables(pos_col, name):
    t = pos_col.shape[0]

    def body(p_ref, c_ref, a_ref, b_ref):
        lane = jnp.bitwise_and(lax.broadcasted_iota(jnp.int32, (1, C_PAIR), 1), C_HEAD_DIM - 1)
        j = jnp.bitwise_and(lane, C_ROT_HALF - 1).astype(F32)
        inv = jnp.exp(j * (-math.log(ROPE_THETA) / C_ROT_HALF))
        ang = p_ref[...].astype(F32) * inv
        cos, sin = jnp.cos(ang), jnp.sin(ang)
        c_ref[...] = jnp.where(lane < C_ROT_DIM, cos, 1.0)
        a_ref[...] = jnp.where(lane < C_ROT_HALF, -sin, 0.0)
        b_ref[...] = jnp.where(jnp.logical_and(lane >= C_ROT_HALF, lane < C_ROT_DIM), sin, 0.0)

    tab = pl.BlockSpec((ROPE_ROWS, C_PAIR), lambda i: (i, 0))
    return pl.pallas_call(
        body, name=name, grid=(t // ROPE_ROWS,),
        in_specs=[pl.BlockSpec((ROPE_ROWS, 1), lambda i: (i, 0))],
        out_specs=[tab, tab, tab],
        out_shape=[jax.ShapeDtypeStruct((t, C_PAIR), F32)] * 3,
        compiler_params=_params(("arbitrary",)),
    )(pos_col)


def _rope(x, c, a, b):
    return x * c + pltpu.roll(x, C_PAIR - C_ROT_HALF, 1) * a + pltpu.roll(x, C_ROT_HALF, 1) * b


def _rope_t(d, c, a, b):
    return d * c + pltpu.roll(d * a, C_ROT_HALF, 1) + pltpu.roll(d * b, C_PAIR - C_ROT_HALF, 1)


C_RES = 16


def _residue_major(a, batch):
    return a.reshape(batch, SEQ // C_RES, C_RES, -1).transpose(0, 2, 1, 3).reshape(a.shape)


def _sequence_order(a, batch):
    return a.reshape(batch, C_RES, SEQ // C_RES, -1).transpose(0, 2, 1, 3).reshape(a.shape)


def _block_pieces(idx, dil):
    nblk = SEQ // dil // C_BLOCK
    r, n = idx // nblk, idx % nblk
    per = C_RES // dil
    size = C_BLOCK // per

    def pieces(blk):
        return [((dil * a + r) * (SEQ // C_RES) + size * blk, size) for a in range(per)]

    return pieces(n), pieces(jnp.maximum(n - 1, 0)), n > 0


def _get_rows(ref, pieces):
    return jnp.concatenate([ref[pl.ds(pl.multiple_of(start, 8), size), :] for start, size in pieces], axis=0)


def _set_rows(ref, pieces, val, add=False):
    for k, (start, size) in enumerate(pieces):
        rows = pl.ds(pl.multiple_of(start, 8), size)
        part = val[k * size:(k + 1) * size]
        ref[rows, :] = ref[rows, :] + part if add else part


def _head_masks():
    low = lax.broadcasted_iota(jnp.int32, (1, C_PAIR), 1) < C_HEAD_DIM
    return low, jnp.logical_not(low)


def _attn_mask(has_prev, dil):
    per = C_RES // dil
    size = C_BLOCK // per

    def position(x):
        x = jnp.bitwise_and(x, C_BLOCK - 1)
        return per * jnp.bitwise_and(x, size - 1) + x // size

    j = lax.broadcasted_iota(jnp.int32, (2 * C_BLOCK, 2 * C_BLOCK), 1)
    pi = position(lax.broadcasted_iota(jnp.int32, (2 * C_BLOCK, 2 * C_BLOCK), 0))
    pj = position(j)
    own = j < C_BLOCK
    return jnp.logical_or(jnp.logical_and(own, pj <= pi),
                          jnp.logical_and(jnp.logical_and(jnp.logical_not(own), pj >= pi), has_prev))


def _stack_heads(x):
    low, high = _head_masks()
    return jnp.concatenate([jnp.where(low, x, 0.0), jnp.where(high, x, 0.0)], axis=0)


def _unstack_heads(x):
    low, _ = _head_masks()
    return jnp.where(low, x[:C_BLOCK], x[C_BLOCK:])


def attn_fwd(qkv, cos_t, sin_a, sin_b, batch, name, exchange=None):
    t = qkv.shape[0]
    nbr = len(C_DILATIONS)

    def body(q_ref, k_ref, v_ref, c_ref, a_ref, b_ref, o_ref, l_ref, qr_ref, kr_ref, qs, ks, *stats):
        acc, mm, dd = stats[0:nbr], stats[nbr:2 * nbr], stats[2 * nbr:3 * nbr]
        c, a, b = c_ref[...], a_ref[...], b_ref[...]
        qs[...] = _rope(q_ref[...], c, a, b) * C_SCALE
        ks[...] = _rope(k_ref[...], c, a, b)
        qr_ref[...] = qs[...].astype(BF16)
        kr_ref[...] = ks[...].astype(BF16)

        def load(idx, dil):
            own, prev, has_prev = _block_pieces(idx, dil)
            return own, (has_prev, _get_rows(qs, own), _get_rows(ks, own), _get_rows(ks, prev),
                         _get_rows(v_ref, own), _get_rows(v_ref, prev))

        def scores(dil, has_prev, q, k_own, k_prev, v_own, v_prev):
            k_cat = jnp.concatenate([k_own, k_prev], axis=0).astype(BF16)
            return jnp.where(_attn_mask(has_prev, dil), _dot_nt(_stack_heads(q).astype(BF16), k_cat), NEG_BIG)

        def softmax(s):
            m = jnp.max(s, axis=-1, keepdims=True)
            p = jnp.exp(s - m)
            return p.astype(BF16), m, jnp.sum(p, axis=-1, keepdims=True)

        def values(pb, has_prev, q, k_own, k_prev, v_own, v_prev):
            low, high = _head_masks()
            v_cat = jnp.concatenate([v_own, v_prev], axis=0)
            p_wide = jnp.concatenate([pb[:C_BLOCK], pb[C_BLOCK:]], axis=1)
            v_tall = jnp.concatenate([jnp.where(low, v_cat, 0.0), jnp.where(high, v_cat, 0.0)], axis=0).astype(BF16)
            return _dot(p_wide, v_tall)

        for bi, dil in enumerate(C_DILATIONS):
            def pair(i, carry, bi=bi, dil=dil):
                low, _ = _head_masks()
                loaded = [load(C_FWD_BLOCKS * i + k, dil) for k in range(C_FWD_BLOCKS)]
                ss = [scores(dil, *ops) for _, ops in loaded]
                sm = [softmax(s) for s in ss]
                pvs = [values(pb, *ops) for (pb, _, _), (_, ops) in zip(sm, loaded)]
                for (own, _), (_, m, den), pv in zip(loaded, sm, pvs):
                    _set_rows(acc[bi], own, pv)
                    _set_rows(mm[bi], own, jnp.where(low, m[:C_BLOCK], m[C_BLOCK:]))
                    _set_rows(dd[bi], own, jnp.where(low, den[:C_BLOCK], den[C_BLOCK:]))
                return carry

            lax.fori_loop(0, SEQ // C_BLOCK // C_FWD_BLOCKS, pair, 0)
        step = 2 * C_BLOCK
        for r0 in range(0, SEQ, step):
            rr = slice(r0, r0 + step)
            ms = [mm[g][rr, :] for g in range(nbr)]
            m_all = functools.reduce(jnp.maximum, ms)
            ws = [jnp.exp(m - m_all) for m in ms]
            num = sum(acc[g][rr, :] * ws[g] for g in range(nbr))
            den = sum(dd[g][rr, :] * ws[g] for g in range(nbr))
            o_ref[rr, :] = (num / den).astype(BF16)
            l_ref[rr, :] = m_all + jnp.log(den)

    def col(k):
        return pl.BlockSpec((SEQ, C_PAIR), lambda b, p: (b, k * C_PAIRS + p))

    tab = pl.BlockSpec((SEQ, C_PAIR), lambda b, p: (b, 0))
    return _call(
        body, name=name, grid=(batch, C_PAIRS),
        in_specs=[col(0), col(1), col(2), tab, tab, tab],
        out_specs=[col(0), col(0), col(0), col(0)],
        out_shape=[jax.ShapeDtypeStruct((t, D_MODEL), BF16), jax.ShapeDtypeStruct((t, D_MODEL), F32),
                   jax.ShapeDtypeStruct((t, D_MODEL), BF16), jax.ShapeDtypeStruct((t, D_MODEL), BF16)],
        scratch_shapes=[pltpu.VMEM((SEQ, C_PAIR), F32)] * (2 + 3 * nbr),
        args=(qkv, qkv, qkv, cos_t, sin_a, sin_b), exchange=exchange)


def attn_bwd(qr, kr, qkv, cos_t, sin_a, sin_b, o, lse, do, batch, name, exchange=None):
    t = qkv.shape[0]

    def body(q_ref, k_ref, v_ref, c_ref, a_ref, b_ref, o_ref, l_ref, do_ref, dqkv_ref, qs, ks, dqs, dks, dvs, dlt):
        low, _ = _head_masks()
        c, a, b = c_ref[...], a_ref[...], b_ref[...]
        qs[...] = q_ref[...].astype(F32)
        ks[...] = k_ref[...].astype(F32)
        prod = do_ref[...] * o_ref[...].astype(F32)
        s_low = jnp.sum(jnp.where(low, prod, 0.0), axis=-1, keepdims=True)
        s_all = jnp.sum(prod, axis=-1, keepdims=True)
        dlt[...] = jnp.where(low, s_low, s_all - s_low)
        dqs[...] = jnp.zeros_like(dqs)
        dks[...] = jnp.zeros_like(dks)
        dvs[...] = jnp.zeros_like(dvs)

        def load(idx, dil):
            own, prev, has_prev = _block_pieces(idx, dil)
            return (own, prev), (has_prev, _get_rows(qs, own), _get_rows(do_ref, own), _get_rows(ks, own),
                                 _get_rows(ks, prev), _get_rows(v_ref, own), _get_rows(v_ref, prev),
                                 _get_rows(l_ref, own), _get_rows(dlt, own))

        def operands(dil, has_prev, q, do, k_own, k_prev, v_own, v_prev, l_full, d_full):
            lcol = jnp.concatenate([l_full[:, 0:1], l_full[:, C_HEAD_DIM:C_HEAD_DIM + 1]], axis=0)
            dcol = jnp.concatenate([d_full[:, 0:1], d_full[:, C_HEAD_DIM:C_HEAD_DIM + 1]], axis=0)
            return (_stack_heads(q).astype(BF16), _stack_heads(do).astype(BF16),
                    jnp.concatenate([k_own, k_prev], axis=0).astype(BF16),
                    jnp.concatenate([v_own, v_prev], axis=0).astype(BF16), lcol, dcol, _attn_mask(has_prev, dil))

        for dil in C_DILATIONS:
            def pair(i, carry, dil=dil):
                loaded = [load(C_BWD_BLOCKS * i + k, dil) for k in range(C_BWD_BLOCKS)]
                ops = [operands(dil, *o) for _, o in loaded]
                ss = [_dot_nt(q_stack, k_cat) for q_stack, _, k_cat, _, _, _, _ in ops]
                dps = [_dot_nt(do_stack, v_cat) for _, do_stack, _, v_cat, _, _, _ in ops]
                ps = [jnp.exp(jnp.where(o[6], s, NEG_BIG) - o[4]) for s, o in zip(ss, ops)]
                dss = [(p * (dp - o[5])).astype(BF16) for p, dp, o in zip(ps, dps, ops)]
                dvs_ = [_dot_tn(p.astype(BF16), o[1]) for p, o in zip(ps, ops)]
                dks_ = [_dot_tn(ds, o[0]) for ds, o in zip(dss, ops)]
                dqs_ = [_unstack_heads(_dot(ds, o[2])) for ds, o in zip(dss, ops)]
                for ((own, prev), _), dq, dk_cat, dv_cat in zip(loaded, dqs_, dks_, dvs_):
                    _set_rows(dqs, own, dq, add=True)
                    _set_rows(dks, own, dk_cat[:C_BLOCK], add=True)
                    _set_rows(dvs, own, dv_cat[:C_BLOCK], add=True)
                    _set_rows(dks, prev, dk_cat[C_BLOCK:], add=True)
                    _set_rows(dvs, prev, dv_cat[C_BLOCK:], add=True)
                return carry

            lax.fori_loop(0, SEQ // C_BLOCK // C_BWD_BLOCKS, pair, 0)
        dqkv_ref[0] = _rope_t(dqs[...] * C_SCALE, c, a, b).astype(BF16)
        dqkv_ref[1] = _rope_t(dks[...], c, a, b).astype(BF16)
        dqkv_ref[2] = dvs[...].astype(BF16)

    def col(k):
        return pl.BlockSpec((SEQ, C_PAIR), lambda b, p: (b, k * C_PAIRS + p))

    tab = pl.BlockSpec((SEQ, C_PAIR), lambda b, p: (b, 0))
    return _call(
        body, name=name, grid=(batch, C_PAIRS),
        in_specs=[col(0), col(0), col(2), tab, tab, tab, col(0), col(0), col(0)],
        out_specs=[pl.BlockSpec((3, SEQ, C_PAIR), lambda b, p: (0, b, p))],
        out_shape=[jax.ShapeDtypeStruct((3, t, D_MODEL), BF16)],
        scratch_shapes=[pltpu.VMEM((SEQ, C_PAIR), F32)] * 6,
        args=(qr, kr, qkv, cos_t, sin_a, sin_b, o, lse, do), exchange=exchange)


def allreduce_small(slab, name):
    rows, lanes = slab.shape

    def body(x_ref, out_ref, gath, send_sems, recv_sems, local_sem):
        x, y, c, chips = _place()
        me, sibling = (x, y, c), (x, y, 1 - c)

        def slot(px, py, pc):
            return gath.at[4 * px + 2 * py + pc]

        def copy(k, block, to, src=None):
            return pltpu.make_async_remote_copy(
                src_ref=slot(*block) if src is None else src, dst_ref=slot(*block),
                send_sem=send_sems.at[k], recv_sem=recv_sems.at[k], device_id=to, device_id_type=MESH)

        mine = pltpu.make_async_copy(x_ref, slot(*me), local_sem)
        mine.start()
        first = [copy(0, me, sibling, src=x_ref)]
        first += [copy(1 + j, me, (*chip, c), src=x_ref) for j, chip in enumerate(chips)]
        for cp in first:
            cp.start()
        passed = [copy(4 + j, (*chip, c), sibling) for j, chip in enumerate(chips)]
        for j, chip in enumerate(chips):
            copy(1 + j, (*chip, c), me).wait_recv()
            passed[j].start()
        copy(0, sibling, me).wait_recv()
        for j, chip in enumerate(chips):
            copy(4 + j, (*chip, 1 - c), me).wait_recv()
        for cp in first + passed:
            cp.wait_send()
        mine.wait()
        total = gath[0]
        for d in range(1, N_DEV):
            total = total + gath[d]
        out_ref[...] = total

    return pl.pallas_call(
        body, name=name,
        in_specs=[pl.BlockSpec(memory_space=pltpu.VMEM)],
        out_specs=pl.BlockSpec(memory_space=pltpu.VMEM),
        out_shape=jax.ShapeDtypeStruct((rows, lanes), F32),
        scratch_shapes=[pltpu.VMEM((N_DEV, rows, lanes), F32),
                        pltpu.SemaphoreType.DMA((7,)), pltpu.SemaphoreType.DMA((7,)), pltpu.SemaphoreType.DMA],
    )(slab)


ELT_ROWS = 256


def reduce_slabs(r, name):
    r = r.reshape(N_CHIPS, -1, r.shape[-1])
    _, rows, cols = r.shape
    br = min(rows, ELT_ROWS)

    def body(r_ref, o_ref):
        o_ref[...] = ((r_ref[3].astype(F32) + r_ref[0].astype(F32)) + r_ref[1].astype(F32)) + r_ref[2].astype(F32)

    return pl.pallas_call(
        body, name=name, grid=(rows // br,),
        in_specs=[pl.BlockSpec((N_CHIPS, br, cols), lambda i: (0, i, 0))],
        out_specs=pl.BlockSpec((br, cols), lambda i: (i, 0)),
        out_shape=jax.ShapeDtypeStruct((rows, cols), F32),
        compiler_params=_params(("arbitrary",)),
    )(r)


def _adamw(w, g, m, v):
    m = ADAM_B1 * m + (1.0 - ADAM_B1) * g
    v = ADAM_B2 * v + (1.0 - ADAM_B2) * jnp.square(g)
    m_hat = m / (1.0 - ADAM_B1 ** ADAM_STEP)
    v_hat = v / (1.0 - ADAM_B2 ** ADAM_STEP)
    delta = -ADAM_LR * (m_hat / (jnp.sqrt(v_hat) + ADAM_EPS) + ADAM_WD * w)
    return delta, m, v


def adamw_big(w, s_mine, s_sibling, m, v, name):
    rows, cols = w.shape
    parts = len(s_mine)
    br = min(rows // parts, ELT_ROWS)
    nb = rows // parts // br

    def body(w_ref, m_ref, v_ref, *rest):
        sums, (g_out, d_out, m_out, v_out) = rest[:2 * parts], rest[2 * parts:]
        p = pl.program_id(0)
        g = sums[0][...] + sums[parts][...]
        for k in range(1, parts):
            g = jnp.where(p == k, sums[k][...] + sums[parts + k][...], g)
        g_out[...] = g
        d_out[...], m_out[...], v_out[...] = _adamw(w_ref[...], g, m_ref[...], v_ref[...])

    def part_spec(k):
        return pl.BlockSpec((br, cols), lambda p, i: (jnp.where(p == k, i, jnp.where(p < k, 0, nb - 1)), 0))

    blk = pl.BlockSpec((br, cols), lambda p, i: (p * nb + i, 0))
    out = jax.ShapeDtypeStruct((rows, cols), F32)
    return pl.pallas_call(
        body, name=name, grid=(parts, nb),
        in_specs=[blk] * 3 + [part_spec(k) for k in range(parts)] * 2, out_specs=[blk] * 4, out_shape=[out] * 4,
        compiler_params=_params(("arbitrary", "arbitrary")),
    )(w, m, v, *s_mine, *s_sibling)


def adamw_small(ws, gs, ms, vs, name):
    n = len(ws)

    def body(*refs):
        w_refs, g_refs, m_refs, v_refs = (refs[k * n:(k + 1) * n] for k in range(4))
        d_out, m_out, v_out = (refs[(4 + k) * n:(5 + k) * n] for k in range(3))
        for i in range(n):
            d_out[i][...], m_out[i][...], v_out[i][...] = _adamw(
                w_refs[i][...], g_refs[i][...], m_refs[i][...], v_refs[i][...])

    outs = [jax.ShapeDtypeStruct(w.shape, F32) for w in ws]
    res = pl.pallas_call(body, name=name, out_shape=outs * 3)(*ws, *gs, *ms, *vs)
    return res[:n], res[n:2 * n], res[2 * n:]


SLAB_LANES = 128
SLAB_ROW_ALIGN = 8


def _pack(parts):
    flat = jnp.concatenate([p.reshape(-1) for p in parts])
    rows = -(-flat.shape[0] // (SLAB_LANES * SLAB_ROW_ALIGN)) * SLAB_ROW_ALIGN
    flat = jnp.pad(flat, (0, rows * SLAB_LANES - flat.shape[0]))
    return flat.reshape(rows, SLAB_LANES)


def _unpack(slab, shapes):
    flat = slab.reshape(-1)
    out, pos = [], 0
    for s in shapes:
        size = math.prod(s)
        out.append(flat[pos:pos + size].reshape(s))
        pos += size
    return out


def kernel(x, positions, norm_mix_pre, norm_mix_post, norm_ffn_pre, norm_ffn_post, w_in_even, lb_table, a_norm, b_ln_g, b_ln_b, b_ws, b_bias, w_out_even, w_in_odd, w_out_odd, w_ff1, w_ff2, loss_target, m_norm_mix_pre, m_norm_mix_post, m_norm_ffn_pre, m_norm_ffn_post, m_w_in_even, m_lb_table, m_a_norm, m_b_ln_g, m_b_ln_b, m_b_ws, m_b_bias, m_w_out_even, m_w_in_odd, m_w_out_odd, m_w_ff1, m_w_ff2, v_norm_mix_pre, v_norm_mix_post, v_norm_ffn_pre, v_norm_ffn_post, v_w_in_even, v_lb_table, v_a_norm, v_b_ln_g, v_b_ln_b, v_b_ws, v_b_bias, v_w_out_even, v_w_in_odd, v_w_out_odd, v_w_ff1, v_w_ff2):
    batch = x.shape[0]
    t = batch * SEQ
    d = D_MODEL
    x0 = x.reshape(t, d)
    target = loss_target.reshape(t, d)

    def gain(p, layer):
        return p[layer:layer + 1]

    def gather(*shards):
        return _Exchange("gather", [w.astype(BF16) for w in shards])

    def scatter(*grads):
        return _Exchange("scatter", grads)

    (win_e,) = exchange_alone(gather(w_in_even[0]), "gather_in_even")
    bias_t = b_bias[0].T
    proj, h0, w1_0 = norm_matmul(x0, gain(norm_mix_pre, 0), win_e, "in_proj_even", exchange=gather(w_ff1[0]))
    oa, states, decays, w2_0 = hgrn2_fwd(proj, lb_table, a_norm, batch, "hgrn2_fwd", exchange=gather(w_ff2[0]))
    mixin, wout_e = gmlp_fwd(proj, oa, b_ln_g, b_ln_b, b_ws[0], bias_t, "gmlp_fwd", exchange=gather(w_out_even[0]))
    mix0, x1 = out_proj(mixin, wout_e, x0, gain(norm_mix_post, 0), "out_proj_even")
    x2, hf0, a0, y0, win_o, wout_o = ffn_fwd(x1, gain(norm_ffn_pre, 0), w1_0, w2_0, gain(norm_ffn_post, 0),
                                             "ffn_fwd_0", exchange=gather(w_in_odd[0], w_out_odd[0]))
    x2p = _residue_major(x2, batch)
    qkv, h1 = norm_matmul(x2p, gain(norm_mix_pre, 1), win_o, "in_proj_odd")
    cos_t, sin_a, sin_b = rope_tables(_residue_major(positions.reshape(t, 1), batch), "rope_tables")
    ao, lse, q_rot, k_rot, w1_1, w2_1 = attn_fwd(qkv, cos_t, sin_a, sin_b, batch, "attn_fwd",
                                                 exchange=gather(w_ff1[1], w_ff2[1]))
    mix1, x3 = out_proj(ao, wout_o, x2p, gain(norm_mix_post, 1), "out_proj_odd")
    dx4, hf1, a1, y1, loss_part = ffn_fwd(x3, gain(norm_ffn_pre, 1), w1_1, w2_1, gain(norm_ffn_post, 1),
                                          "ffn_fwd_1", target=_residue_major(target, batch))

    hc = D_FF // N_CHIPS
    dx3, dy1, da1, dg_fpre1, dg_fpost1 = ffn_bwd(
        dx4, x3, y1, a1, gain(norm_ffn_pre, 1), gain(norm_ffn_post, 1), w1_1, w2_1, "ffn_bwd_1")
    g_w1_1 = weight_grad(hf1, da1, "b", d, hc, False, "wgrad_ff1_1")
    g_w2_1 = weight_grad(a1, dy1, "a", hc, d, True, "wgrad_ff2_1")
    dmix1, dao, dg_mpost1 = out_proj_bwd(dx3, mix1, gain(norm_mix_post, 1), wout_o, "out_proj_bwd_odd")
    g_wout_o = weight_grad(ao, dmix1, "a", d // N_CHIPS, d, False, "wgrad_out_odd")
    dqkv, r_w1_1, r_w2_1, r_wout_o = attn_bwd(q_rot, k_rot, qkv, cos_t, sin_a, sin_b, ao, lse, dao, batch, "attn_bwd",
                                              exchange=scatter(g_w1_1, g_w2_1, g_wout_o))
    dx2p, dg_mpre1 = norm_matmul_bwd(dqkv, win_o, x2p, gain(norm_mix_pre, 1), dx3, "in_proj_bwd_odd")
    dx2 = _sequence_order(dx2p, batch)
    g_win_o = weight_grad_stacked(h1, dqkv, 3 * d // N_CHIPS, "wgrad_in_odd")
    s_w1_1, s_w2_1, s_wout_o = (reduce_slabs(r, n) for r, n in (
        (r_w1_1, "reduce_ff1_1"), (r_w2_1, "reduce_ff2_1"), (r_wout_o, "reduce_out_odd")))
    dx1, dy0, da0, dg_fpre0, dg_fpost0, r_win_o, t_w1_1, t_w2_1, t_wout_o = ffn_bwd(
        dx2, x1, y0, a0, gain(norm_ffn_pre, 0), gain(norm_ffn_post, 0), w1_0, w2_0, "ffn_bwd_0",
        exchange=_Both(scatter(g_win_o), _Swap([s_w1_1, s_w2_1, s_wout_o])))
    g_w1_0 = weight_grad(hf0, da0, "b", d, hc, False, "wgrad_ff1_0")
    g_w2_0 = weight_grad(a0, dy0, "a", hc, d, True, "wgrad_ff2_0")
    dmix0, dmixin, dg_mpost0 = out_proj_bwd(dx1, mix0, gain(norm_mix_post, 0), wout_e, "out_proj_bwd_even")
    g_wout_e = weight_grad(mixin, dmix0, "a", d // N_CHIPS, d, False, "wgrad_out_even")
    s_win_o = reduce_slabs(r_win_o, "reduce_in_odd")
    dproj, d_lb, d_anorm, r_w1_0, t_win_o = hgrn2_bwd(
        proj, states, decays, lb_table, a_norm, dmixin, batch, "hgrn2_bwd",
        exchange=_Both(scatter(g_w1_0), _Swap([s_win_o])))
    s_w1_0 = reduce_slabs(r_w1_0, "reduce_ff1_0")
    dproj, d_lng, d_lnb, d_ws, d_bias_t, r_w2_0, t_w1_0 = gmlp_bwd(
        proj, dmixin, b_ln_g, b_ln_b, b_ws[0], bias_t, dproj, "gmlp_bwd",
        exchange=_Both(scatter(g_w2_0), _Swap([s_w1_0])))
    s_w2_0 = reduce_slabs(r_w2_0, "reduce_ff2_0")
    g_win_e, r_wout_e, t_w2_0 = weight_grad(h0, dproj, "b", d, 3 * d // N_CHIPS, False, "wgrad_in_even",
                                            exchange=_Both(scatter(g_wout_e), _Swap([s_w2_0])))
    s_wout_e = reduce_slabs(r_wout_e, "reduce_out_even")
    dx0, dg_mpre0, r_win_e, t_wout_e = norm_matmul_bwd(
        dproj, win_e, x0, gain(norm_mix_pre, 0), dx1, "in_proj_bwd_even",
        exchange=_Both(scatter(g_win_e), _Swap([s_wout_e])))
    grad_x = dx0.reshape(x.shape)
    s_win_e = reduce_slabs(r_win_e, "reduce_in_even")
    (t_win_e,) = exchange_alone(_Swap([s_win_e]), "sibling_swap")

    big_w = [w_in_even, w_out_even, w_in_odd, w_out_odd, w_ff1, w_ff2]
    big_m = [m_w_in_even, m_w_out_even, m_w_in_odd, m_w_out_odd, m_w_ff1, m_w_ff2]
    big_v = [v_w_in_even, v_w_out_even, v_w_in_odd, v_w_out_odd, v_w_ff1, v_w_ff2]
    mine = [[s_win_e], [s_wout_e], [s_win_o], [s_wout_o], [s_w1_0, s_w1_1], [s_w2_0, s_w2_1]]
    theirs = [[t_win_e], [t_wout_e], [t_win_o], [t_wout_o], [t_w1_0, t_w1_1], [t_w2_0, t_w2_1]]
    big = []
    for i, (w, m, v) in enumerate(zip(big_w, big_m, big_v)):
        two_d = (-1, w.shape[-1])
        res = adamw_big(w.reshape(two_d), mine[i], theirs[i], m.reshape(two_d), v.reshape(two_d), "adamw_big_%d" % i)
        big.append([r.reshape(w.shape) for r in res])

    small_w = [norm_mix_pre, norm_mix_post, norm_ffn_pre, norm_ffn_post, lb_table, a_norm, b_ln_g, b_ln_b, b_ws, b_bias]
    small_m = [m_norm_mix_pre, m_norm_mix_post, m_norm_ffn_pre, m_norm_ffn_post, m_lb_table, m_a_norm, m_b_ln_g,
               m_b_ln_b, m_b_ws, m_b_bias]
    small_v = [v_norm_mix_pre, v_norm_mix_post, v_norm_ffn_pre, v_norm_ffn_post, v_lb_table, v_a_norm, v_b_ln_g,
               v_b_ln_b, v_b_ws, v_b_bias]
    partial = [jnp.concatenate([dg_mpre0, dg_mpre1]), jnp.concatenate([dg_mpost0, dg_mpost1]),
               jnp.concatenate([dg_fpre0, dg_fpre1]), jnp.concatenate([dg_fpost0, dg_fpost1]),
               d_lb, d_anorm, d_lng, d_lnb, d_ws[None], d_bias_t.T[None]]
    *small_g, loss = _unpack(allreduce_small(_pack(partial + [loss_part]), "allreduce_small"),
                             [w.shape for w in small_w] + [()])
    small_d, small_nm, small_nv = adamw_small(small_w, small_g, small_m, small_v, "adamw_small")

    order = ["norm_mix_pre", "norm_mix_post", "norm_ffn_pre", "norm_ffn_post", "w_in_even", "lb_table", "a_norm",
             "b_ln_g", "b_ln_b", "b_ws", "b_bias", "w_out_even", "w_in_odd", "w_out_odd", "w_ff1", "w_ff2"]
    small_names = ["norm_mix_pre", "norm_mix_post", "norm_ffn_pre", "norm_ffn_post", "lb_table", "a_norm",
                   "b_ln_g", "b_ln_b", "b_ws", "b_bias"]
    big_names = ["w_in_even", "w_out_even", "w_in_odd", "w_out_odd", "w_ff1", "w_ff2"]
    grads, deltas, new_m, new_v = {}, {}, {}, {}
    for i, nm in enumerate(small_names):
        grads[nm], deltas[nm], new_m[nm], new_v[nm] = small_g[i], small_d[i], small_nm[i], small_nv[i]
    for i, nm in enumerate(big_names):
        grads[nm], deltas[nm], new_m[nm], new_v[nm] = big[i]
    return (loss, grad_x, *[grads[n] for n in order], *[deltas[n] for n in order],
            *[new_m[n] for n in order], *[new_v[n] for n in order])
```

```python
import functools
import math

import jax
import jax.numpy as jnp
from jax import lax
from jax.experimental import pallas as pl
from jax.experimental.pallas import tpu as pltpu

F32 = jnp.float32
BF16 = jnp.bfloat16
MESH = pl.DeviceIdType.MESH

D_MODEL = 1024
SEQ = 2048
D_FF = 4096
N_CHIPS = 4
A_WIDTH = 512
A_HEADS = 4
A_DK = 128
A_CHUNK = 64
A_SUB = 16
B_WIDTH = 512
B_GROUPS = 4
B_CHUNK = 128
C_HEADS = 16
C_HEAD_DIM = 64
C_ROT_HALF = 8
C_BLOCK = 128
C_DILATIONS = (1, 4, 16)
ROPE_THETA = 500000.0
EPS = 1e-6
ADAM_LR = 0.001
ADAM_B1 = 0.9
ADAM_B2 = 0.999
ADAM_EPS = 1e-08
ADAM_WD = 0.01
ADAM_STEP = 10

ROW_TILE = 512
FFN_ROWS = 1024
WGRAD_ROWS = 2048
VMEM_LIMIT = 56 * 1024 * 1024
NEG_BIG = -1e30


def _params(sem=None):
    return pltpu.CompilerParams(dimension_semantics=sem, vmem_limit_bytes=VMEM_LIMIT)


def _dot(a, b):
    return jnp.dot(a, b, preferred_element_type=F32)


def _dot_nt(a, b):
    return lax.dot_general(a, b, (((1,), (1,)), ((), ())), preferred_element_type=F32)


def _dot_tn(a, b):
    return lax.dot_general(a, b, (((0,), (0,)), ((), ())), preferred_element_type=F32)


def _rms(x, g):
    r = lax.rsqrt(jnp.mean(x * x, axis=-1, keepdims=True) + EPS)
    return x * r * g


def _rms_bwd(x, g, dy):
    r = lax.rsqrt(jnp.mean(x * x, axis=-1, keepdims=True) + EPS)
    xh = x * r
    dg = jnp.sum(dy * xh, axis=0, keepdims=True)
    dxh = dy * g
    dx = r * (dxh - xh * jnp.mean(dxh * xh, axis=-1, keepdims=True))
    return dx, dg


def _accumulate(ref, val, first):
    @pl.when(first)
    def _():
        ref[...] = val

    @pl.when(jnp.logical_not(first))
    def _():
        ref[...] += val


N_DEV = 8
ANY = pl.BlockSpec(memory_space=pl.ANY)


def _place():
    x, y, c = lax.axis_index("x"), lax.axis_index("y"), lax.axis_index("c")
    return x, y, c, [(1 - x, y), (x, 1 - y), (1 - x, 1 - y)]


class _Exchange:
    def __init__(self, kind, arrays):
        self.kind, self.arrays, self.n = kind, list(arrays), len(arrays)
        per_peer = pltpu.SemaphoreType.DMA((3 * self.n,))
        if kind == "gather":
            self.out_shape = [jax.ShapeDtypeStruct((N_CHIPS,) + a.shape, a.dtype) for a in self.arrays]
            self.scratch = [per_peer, per_peer, pltpu.SemaphoreType.DMA((self.n,)), per_peer, per_peer]
        else:
            self.out_shape = [jax.ShapeDtypeStruct(a.shape, a.dtype) for a in self.arrays]
            self.scratch = [per_peer, per_peer, pltpu.SemaphoreType.DMA((self.n,))]

    def _copies(self, ins, outs, sems):
        send_sems, recv_sems, local_sems = sems[:3]
        x, y, c, chips = _place()
        me = 2 * x + y
        local, remote = [], []
        for a in range(self.n):
            if self.kind == "gather":
                local.append(pltpu.make_async_copy(ins[a], outs[a].at[me], local_sems.at[a]))
                half = self.arrays[a].shape[0] // 2

                def rows(ref, core, half=half):
                    return ref.at[pl.ds(core * half, half)]
            else:
                local.append(pltpu.make_async_copy(ins[a].at[me], outs[a].at[3], local_sems.at[a]))
            for j, (px, py) in enumerate(chips):
                k = 3 * a + j
                peer = 2 * px + py

                def copy(src, dst, to, send_sem=send_sems.at[k], recv_sem=recv_sems.at[k]):
                    return pltpu.make_async_remote_copy(src_ref=src, dst_ref=dst, send_sem=send_sem, recv_sem=recv_sem,
                                                        device_id=to, device_id_type=MESH)

                if self.kind == "gather":
                    sent = copy(rows(ins[a], c), rows(outs[a].at[me], c), (px, py, c))
                    landed = copy(rows(ins[a], c), rows(outs[a].at[peer], c), (px, py, c))
                    on = dict(send_sem=sems[3].at[k], recv_sem=sems[4].at[k])
                    passed = copy(rows(outs[a].at[peer], c), rows(outs[a].at[peer], c), (x, y, 1 - c), **on)
                    handed = copy(rows(outs[a].at[peer], c), rows(outs[a].at[peer], 1 - c), (x, y, 1 - c), **on)
                    remote.append((sent, landed, passed, handed))
                else:
                    sent = copy(ins[a].at[peer], outs[a].at[j], (px, py, c))
                    remote.append((sent, sent, None, None))
        return local, remote

    def start(self, ins, outs, sems):
        local, remote = self._copies(ins, outs, sems)
        for cp in local:
            cp.start()
        for sent, _, _, _ in remote:
            sent.start()

    def finish(self, ins, outs, sems):
        local, remote = self._copies(ins, outs, sems)
        for _, landed, passed, _ in remote:
            landed.wait_recv()
            if passed is not None:
                passed.start()
        for sent, _, passed, handed in remote:
            if passed is not None:
                handed.wait_recv()
                passed.wait_send()
            sent.wait_send()
        for cp in local:
            cp.wait()


class _Swap:
    def __init__(self, arrays):
        self.arrays, self.n = list(arrays), len(arrays)
        self.out_shape = [jax.ShapeDtypeStruct(a.shape, a.dtype) for a in self.arrays]
        self.scratch = [pltpu.SemaphoreType.DMA((self.n,)), pltpu.SemaphoreType.DMA((self.n,))]

    def _copies(self, ins, outs, sems):
        x, y, c, _ = _place()
        return [pltpu.make_async_remote_copy(src_ref=ins[a], dst_ref=outs[a], send_sem=sems[0].at[a],
                                             recv_sem=sems[1].at[a], device_id=(x, y, 1 - c), device_id_type=MESH)
                for a in range(self.n)]

    def start(self, ins, outs, sems):
        for cp in self._copies(ins, outs, sems):
            cp.start()

    def finish(self, ins, outs, sems):
        for cp in self._copies(ins, outs, sems):
            cp.wait_recv()
            cp.wait_send()


class _Both:
    def __init__(self, first, second):
        self.parts = (first, second)
        self.arrays, self.n = first.arrays + second.arrays, first.n + second.n
        self.out_shape = first.out_shape + second.out_shape
        self.scratch = first.scratch + second.scratch

    def _split(self, ins, outs, sems):
        a, b = self.parts
        return ((a, ins[:a.n], outs[:a.n], sems[:len(a.scratch)]),
                (b, ins[a.n:], outs[a.n:], sems[len(a.scratch):]))

    def start(self, ins, outs, sems):
        for ex, i, o, s in self._split(ins, outs, sems):
            ex.start(i, o, s)

    def finish(self, ins, outs, sems):
        for ex, i, o, s in self._split(ins, outs, sems):
            ex.finish(i, o, s)


def _call(body, *, name, grid, in_specs, out_specs, out_shape, args, scratch_shapes=(), aliases=None, exchange=None):
    if exchange is None:
        return pl.pallas_call(
            body, name=name, grid=grid, in_specs=in_specs, out_specs=out_specs, out_shape=out_shape,
            scratch_shapes=list(scratch_shapes), input_output_aliases=aliases or {},
            compiler_params=_params(("arbitrary",) * len(grid)))(*args)
    n_in, n_out, n_scr, n_ex = len(in_specs), len(out_specs), len(scratch_shapes), exchange.n
    steps = grid

    def wrapped(*refs):
        ins, refs = refs[:n_in], refs[n_in:]
        ex_in, refs = refs[:n_ex], refs[n_ex:]
        outs, refs = refs[:n_out], refs[n_out:]
        ex_out, refs = refs[:n_ex], refs[n_ex:]
        scr, sems = refs[:n_scr], refs[n_scr:]
        first = functools.reduce(jnp.logical_and, [pl.program_id(k) == 0 for k in range(len(steps))])
        last = functools.reduce(jnp.logical_and, [pl.program_id(k) == steps[k] - 1 for k in range(len(steps))])

        @pl.when(first)
        def _():
            exchange.start(ex_in, ex_out, sems)

        body(*ins, *outs, *scr)

        @pl.when(last)
        def _():
            exchange.finish(ex_in, ex_out, sems)

    return pl.pallas_call(
        wrapped, name=name, grid=grid,
        in_specs=list(in_specs) + [ANY] * n_ex, out_specs=list(out_specs) + [ANY] * n_ex,
        out_shape=list(out_shape) + exchange.out_shape,
        scratch_shapes=list(scratch_shapes) + exchange.scratch, input_output_aliases=aliases or {},
        compiler_params=_params(("arbitrary",) * len(grid)))(*args, *exchange.arrays)


def exchange_alone(exchange, name):
    def body(*refs):
        n = exchange.n
        exchange.start(refs[:n], refs[n:2 * n], refs[2 * n:])
        exchange.finish(refs[:n], refs[n:2 * n], refs[2 * n:])

    return pl.pallas_call(
        body, name=name, in_specs=[ANY] * exchange.n, out_specs=[ANY] * exchange.n,
        out_shape=exchange.out_shape, scratch_shapes=exchange.scratch)(*exchange.arrays)


def norm_matmul(x, g, wg, name, exchange=None):
    t, d = x.shape
    nl = wg.shape[2]

    def body(x_ref, g_ref, w_ref, o_ref, h_ref):
        h = _rms(x_ref[...], g_ref[...]).astype(BF16)
        h_ref[...] = h
        for c in range(N_CHIPS):
            o_ref[:, c * nl:(c + 1) * nl] = _dot(h, w_ref[c])

    return _call(
        body, name=name, grid=(t // ROW_TILE,),
        in_specs=[pl.BlockSpec((ROW_TILE, d), lambda i: (i, 0)),
                  pl.BlockSpec((1, d), lambda i: (0, 0)),
                  pl.BlockSpec((N_CHIPS, d, nl), lambda i: (0, 0, 0))],
        out_specs=[pl.BlockSpec((ROW_TILE, N_CHIPS * nl), lambda i: (i, 0)),
                   pl.BlockSpec((ROW_TILE, d), lambda i: (i, 0))],
        out_shape=[jax.ShapeDtypeStruct((t, N_CHIPS * nl), F32), jax.ShapeDtypeStruct((t, d), BF16)],
        args=(x, g, wg), exchange=exchange)


def norm_matmul_bwd(dproj, wg, x, g, dres, name, exchange=None):
    t, d = x.shape
    nl = wg.shape[2]
    stacked = dproj.ndim == 3
    piece = math.gcd(nl, dproj.shape[-1])

    def body(dp_ref, w_ref, x_ref, g_ref, dres_ref, dx_ref, dg_ref):
        dh = None
        for j in range(N_CHIPS * nl // piece):
            c, off = divmod(j * piece, nl)
            if stacked:
                p, lo = divmod(j * piece, dproj.shape[-1])
                lhs = dp_ref[p, :, lo:lo + piece]
            else:
                lhs = dp_ref[:, j * piece:(j + 1) * piece]
            part = _dot_nt(lhs.astype(BF16), w_ref[c, :, off:off + piece])
            dh = part if dh is None else dh + part
        dx, dg = _rms_bwd(x_ref[...], g_ref[...], dh)
        dx_ref[...] = dres_ref[...] + dx
        _accumulate(dg_ref, dg, pl.program_id(0) == 0)

    row = pl.BlockSpec((ROW_TILE, d), lambda i: (i, 0))
    vec = pl.BlockSpec((1, d), lambda i: (0, 0))
    if stacked:
        dp_spec = pl.BlockSpec((dproj.shape[0], ROW_TILE, dproj.shape[-1]), lambda i: (0, i, 0))
    else:
        dp_spec = pl.BlockSpec((ROW_TILE, N_CHIPS * nl), lambda i: (i, 0))
    return _call(
        body, name=name, grid=(t // ROW_TILE,),
        in_specs=[dp_spec, pl.BlockSpec((N_CHIPS, d, nl), lambda i: (0, 0, 0)), row, vec, row],
        out_specs=[row, vec],
        out_shape=[jax.ShapeDtypeStruct((t, d), F32), jax.ShapeDtypeStruct((1, d), F32)],
        args=(dproj, wg, x, g, dres), exchange=exchange)


def out_proj(a, wg, x, g, name):
    t, d = x.shape
    kl = wg.shape[1]

    def body(a_ref, w_ref, x_ref, g_ref, mix_ref, xo_ref):
        acc = _dot(a_ref[:, 0:kl], w_ref[0])
        for c in range(1, N_CHIPS):
            acc += _dot(a_ref[:, c * kl:(c + 1) * kl], w_ref[c])
        mix_ref[...] = acc
        xo_ref[...] = x_ref[...] + _rms(acc, g_ref[...])

    row = pl.BlockSpec((ROW_TILE, d), lambda i: (i, 0))
    return pl.pallas_call(
        body, name=name, grid=(t // ROW_TILE,),
        in_specs=[row, pl.BlockSpec((N_CHIPS, kl, d), lambda i: (0, 0, 0)), row,
                  pl.BlockSpec((1, d), lambda i: (0, 0))],
        out_specs=[row, row],
        out_shape=[jax.ShapeDtypeStruct((t, d), F32), jax.ShapeDtypeStruct((t, d), F32)],
        compiler_params=_params(("arbitrary",)),
    )(a, wg, x, g)


def out_proj_bwd(dxo, mix, g, wg, name):
    t, d = mix.shape
    kl = wg.shape[1]

    def body(dxo_ref, mix_ref, g_ref, w_ref, dmix_ref, da_ref, dg_ref):
        dmix, dg = _rms_bwd(mix_ref[...], g_ref[...], dxo_ref[...])
        dmb = dmix.astype(BF16)
        dmix_ref[...] = dmb
        for c in range(N_CHIPS):
            da_ref[:, c * kl:(c + 1) * kl] = _dot_nt(dmb, w_ref[c])
        _accumulate(dg_ref, dg, pl.program_id(0) == 0)

    row = pl.BlockSpec((ROW_TILE, d), lambda i: (i, 0))
    vec = pl.BlockSpec((1, d), lambda i: (0, 0))
    return pl.pallas_call(
        body, name=name, grid=(t // ROW_TILE,),
        in_specs=[row, row, vec, pl.BlockSpec((N_CHIPS, kl, d), lambda i: (0, 0, 0))],
        out_specs=[row, row, vec],
        out_shape=[jax.ShapeDtypeStruct((t, d), BF16), jax.ShapeDtypeStruct((t, d), F32),
                   jax.ShapeDtypeStruct((1, d), F32)],
        compiler_params=_params(("arbitrary",)),
    )(dxo, mix, g, wg)


def ffn_fwd(x, gpre, w1g, w2g, gpost, name, exchange=None, target=None):
    t, d = x.shape
    hc = w1g.shape[2]
    with_loss = target is not None

    def body(x_ref, gpre_ref, w1_ref, w2_ref, gpost_ref, *rest):
        if with_loss:
            t_ref, xo_ref, h_ref, a_ref, y_ref, l_ref, acc = rest
        else:
            xo_ref, h_ref, a_ref, y_ref, acc = rest
        i, c = pl.program_id(0), pl.program_id(1)

        @pl.when(c == 0)
        def _():
            h_ref[...] = _rms(x_ref[...], gpre_ref[...]).astype(BF16)

        a = _dot(h_ref[...], w1_ref[...])
        a_ref[...] = a.astype(BF16)
        r = jnp.square(jnp.maximum(a, 0.0)).astype(BF16)
        _accumulate(acc, _dot(r, w2_ref[...]), c == 0)

        @pl.when(c == N_CHIPS - 1)
        def _():
            y = acc[...]
            y_ref[...] = y
            xo = x_ref[...] + _rms(y, gpost_ref[...])
            if with_loss:
                e = xo - t_ref[...]
                xo_ref[...] = e * (1.0 / d)
                part = jnp.sum(jnp.sum(e * e, axis=-1, keepdims=True), axis=0, keepdims=True) * (0.5 / d)
                _accumulate(l_ref, part, i == 0)
            else:
                xo_ref[...] = xo

    row = pl.BlockSpec((FFN_ROWS, d), lambda i, c: (i, 0))
    vec = pl.BlockSpec((1, d), lambda i, c: (0, 0))
    one = pl.BlockSpec((1, 1), lambda i, c: (0, 0))
    return _call(
        body, name=name, grid=(t // FFN_ROWS, N_CHIPS),
        in_specs=[row, vec,
                  pl.BlockSpec((None, d, hc), lambda i, c: (c, 0, 0)),
                  pl.BlockSpec((None, hc, d), lambda i, c: (c, 0, 0)), vec] + ([row] if with_loss else []),
        out_specs=[row, row, pl.BlockSpec((FFN_ROWS, hc), lambda i, c: (i, c)), row] + ([one] if with_loss else []),
        out_shape=[jax.ShapeDtypeStruct((t, d), F32), jax.ShapeDtypeStruct((t, d), BF16),
                   jax.ShapeDtypeStruct((t, N_CHIPS * hc), BF16), jax.ShapeDtypeStruct((t, d), F32)]
        + ([jax.ShapeDtypeStruct((1, 1), F32)] if with_loss else []),
        scratch_shapes=[pltpu.VMEM((FFN_ROWS, d), F32)],
        args=(x, gpre, w1g, w2g, gpost) + ((target,) if with_loss else ()), exchange=exchange)


def ffn_bwd(dxo, x, y, a, gpre, gpost, w1g, w2g, name, exchange=None):
    t, d = x.shape
    hc = w1g.shape[2]

    def body(dxo_ref, x_ref, y_ref, a_ref, gpre_ref, gpost_ref, w1_ref, w2_ref,
             dxi_ref, dy_ref, da_ref, dgpre_ref, dgpost_ref, acc):
        i, c = pl.program_id(0), pl.program_id(1)

        @pl.when(c == 0)
        def _():
            dy, dg = _rms_bwd(y_ref[...], gpost_ref[...], dxo_ref[...])
            dy_ref[...] = dy.astype(BF16)
            _accumulate(dgpost_ref, dg, i == 0)

        dr = _dot_nt(dy_ref[...], w2_ref[...])
        da = (dr * (2.0 * jnp.maximum(a_ref[...].astype(F32), 0.0))).astype(BF16)
        da_ref[...] = da
        _accumulate(acc, _dot_nt(da, w1_ref[...]), c == 0)

        @pl.when(c == N_CHIPS - 1)
        def _():
            dx, dg = _rms_bwd(x_ref[...], gpre_ref[...], acc[...])
            dxi_ref[...] = dxo_ref[...] + dx
            _accumulate(dgpre_ref, dg, i == 0)

    row = pl.BlockSpec((ROW_TILE, d), lambda i, c: (i, 0))
    vec = pl.BlockSpec((1, d), lambda i, c: (0, 0))
    hid = pl.BlockSpec((ROW_TILE, hc), lambda i, c: (i, c))
    return _call(
        body, name=name, grid=(t // ROW_TILE, N_CHIPS),
        in_specs=[row, row, row, hid, vec, vec,
                  pl.BlockSpec((None, d, hc), lambda i, c: (c, 0, 0)),
                  pl.BlockSpec((None, hc, d), lambda i, c: (c, 0, 0))],
        out_specs=[row, row, hid, vec, vec],
        out_shape=[jax.ShapeDtypeStruct((t, d), F32), jax.ShapeDtypeStruct((t, d), BF16),
                   jax.ShapeDtypeStruct((t, N_CHIPS * hc), BF16),
                   jax.ShapeDtypeStruct((1, d), F32), jax.ShapeDtypeStruct((1, d), F32)],
        scratch_shapes=[pltpu.VMEM((ROW_TILE, d), F32)],
        args=(dxo, x, y, a, gpre, gpost, w1g, w2g), exchange=exchange)


def weight_grad(a, b, chunked, bk, bn, relu2, name, exchange=None):
    t = a.shape[0]
    a_on = chunked == "a"
    rows = min(t, WGRAD_ROWS)
    n_steps = t // rows

    def body(a_ref, b_ref, o_ref, acc):
        s = pl.program_id(1)
        av = a_ref[...]
        if relu2:
            av = jnp.square(jnp.maximum(av.astype(F32), 0.0))
        _accumulate(acc, _dot_tn(av.astype(BF16), b_ref[...].astype(BF16)), s == 0)

        @pl.when(s == n_steps - 1)
        def _():
            o_ref[...] = acc[...].astype(BF16)

    res = _call(
        body, name=name, grid=(N_CHIPS, n_steps),
        in_specs=[pl.BlockSpec((rows, bk), (lambda c, s: (s, c)) if a_on else (lambda c, s: (s, 0))),
                  pl.BlockSpec((rows, bn), (lambda c, s: (s, 0)) if a_on else (lambda c, s: (s, c)))],
        out_specs=[pl.BlockSpec((None, bk, bn), lambda c, s: (c, 0, 0))],
        out_shape=[jax.ShapeDtypeStruct((N_CHIPS, bk, bn), BF16)],
        scratch_shapes=[pltpu.VMEM((bk, bn), F32)],
        args=(a, b), exchange=exchange)
    return res[0] if exchange is None else res


def weight_grad_stacked(a, b3, bn, name):
    t, bk = a.shape
    width = b3.shape[-1]
    piece = math.gcd(bn, width)
    rows = min(t, WGRAD_ROWS)
    n_steps = t // rows

    def body(a_ref, b_ref, o_hbm, acc, staged, sem):
        s, c = pl.program_id(0), pl.program_id(1)
        av = a_ref[...].astype(BF16)
        for chunk in range(N_CHIPS):
            @pl.when(c == chunk)
            def _(chunk=chunk):
                cols = [divmod(chunk * bn + k * piece, width) for k in range(bn // piece)]
                b = jnp.concatenate([b_ref[p, :, lo:lo + piece] for p, lo in cols], axis=1).astype(BF16)
                _accumulate(acc.at[chunk], _dot_tn(av, b), s == 0)

                @pl.when(s == n_steps - 1)
                def _():
                    staged[...] = acc[chunk].astype(BF16)
                    copy = pltpu.make_async_copy(staged, o_hbm.at[chunk], sem)
                    copy.start()
                    copy.wait()

    return pl.pallas_call(
        body, name=name, grid=(n_steps, N_CHIPS),
        in_specs=[pl.BlockSpec((rows, bk), lambda s, c: (s, 0)),
                  pl.BlockSpec((b3.shape[0], rows, width), lambda s, c: (0, s, 0))],
        out_specs=ANY,
        out_shape=jax.ShapeDtypeStruct((N_CHIPS, bk, bn), BF16),
        scratch_shapes=[pltpu.VMEM((N_CHIPS, bk, bn), F32), pltpu.VMEM((bk, bn), BF16), pltpu.SemaphoreType.DMA],
        compiler_params=_params(("arbitrary", "arbitrary")),
    )(a, b3)


def _hgrn2_chunk(st, qs, fls, ivs, gls, l0, l1, l2, ng):
    nsub = len(qs)
    mx = jnp.maximum(jnp.maximum(l0, l1), l2)
    e0, e1, e2 = jnp.exp(l0 - mx), jnp.exp(l1 - mx), jnp.exp(l2 - mx)
    lb = e0 / (e0 + e1 + e2)
    rows = lax.broadcasted_iota(jnp.int32, (A_SUB, A_SUB), 0)
    cols = lax.broadcasted_iota(jnp.int32, (A_SUB, A_SUB), 1)
    tri = (rows >= cols).astype(F32)
    keep = (lax.broadcasted_iota(jnp.int32, (A_SUB, A_SUB, A_DK), 0)
            >= lax.broadcasted_iota(jnp.int32, (A_SUB, A_SUB, A_DK), 1))
    base = jnp.zeros_like(l0)
    bases, gs, ks, qfs = [], [], [], []
    for i in range(nsub):
        f = lb + (1.0 - lb) * jax.nn.sigmoid(fls[i])
        logf = jnp.log(f)
        bases.append(base)
        gs.append(base + jnp.dot(tri, logf, precision=lax.Precision.HIGHEST, preferred_element_type=F32))
        base = base + jnp.sum(logf, axis=0, keepdims=True)
        ks.append(1.0 - f)
        qfs.append(jax.nn.silu(qs[i]))
    g_last = base
    stb = st.astype(BF16)
    outs = []
    for i in range(nsub):
        o = _dot_nt((qfs[i] * jnp.exp(gs[i])).astype(BF16), stb)
        if i > 0:
            qt = (qfs[i] * jnp.exp(gs[i] - bases[i])).astype(BF16)
            kk = jnp.concatenate([ks[j] * jnp.exp(bases[i] - gs[j]) for j in range(i)], axis=0).astype(BF16)
            vv = jnp.concatenate(ivs[:i], axis=0).astype(BF16)
            o = o + _dot(_dot_nt(qt, kk).astype(BF16), vv)
        dec = jnp.exp(jnp.where(keep, gs[i][:, None, :] - gs[i][None, :, :], NEG_BIG))
        s_diag = jnp.sum(qfs[i][:, None, :] * ks[i][None, :, :] * dec, axis=-1)
        o = o + _dot(s_diag.astype(BF16), ivs[i].astype(BF16))
        o = o * lax.rsqrt(jnp.mean(o * o, axis=-1, keepdims=True) + EPS) * ng
        outs.append(o * jax.nn.silu(gls[i]))
    kdec = jnp.concatenate([ks[j] * jnp.exp(g_last - gs[j]) for j in range(nsub)], axis=0).astype(BF16)
    vall = jnp.concatenate(ivs, axis=0).astype(BF16)
    new_st = st * jnp.exp(g_last) + _dot_tn(vall, kdec)
    return new_st, outs


A_MAX_LOG_DECAY = 60.0


def _half_sums(logf):
    n = logf.shape[0]
    first = lax.broadcasted_iota(jnp.int32, logf.shape, 0) < n // 2
    return (jnp.sum(jnp.where(first, logf, 0.0), axis=0, keepdims=True),
            jnp.sum(jnp.where(first, 0.0, logf), axis=0, keepdims=True))


def _split3(x):
    hi = x.astype(BF16)
    r1 = x - hi.astype(F32)
    mid = r1.astype(BF16)
    return hi, mid, (r1 - mid.astype(F32)).astype(BF16)


def _tri_matmul(x, transpose):
    n = x.shape[0]
    r = lax.broadcasted_iota(jnp.int32, (n, n), 0)
    c = lax.broadcasted_iota(jnp.int32, (n, n), 1)
    tri = ((r <= c) if transpose else (r >= c)).astype(BF16)
    hi, mid, lo = _split3(x)
    return (_dot(tri, lo) + _dot(tri, mid)) + _dot(tri, hi)


@jax.custom_vjp
def _cumsum_rows(x):
    return _tri_matmul(x, False)


def _cumsum_rows_fwd(x):
    return _tri_matmul(x, False), None


def _cumsum_rows_bwd(_, dy):
    return (_tri_matmul(dy, True),)


_cumsum_rows.defvjp(_cumsum_rows_fwd, _cumsum_rows_bwd)


def _lower_bound(l0, l1, l2):
    mx = jnp.maximum(jnp.maximum(l0, l1), l2)
    e0, e1, e2 = jnp.exp(l0 - mx), jnp.exp(l1 - mx), jnp.exp(l2 - mx)
    return e0 / (e0 + e1 + e2)


def _b(x):
    return x.astype(BF16)


@jax.custom_vjp
def _mm(a, b):
    return _dot(_b(a), _b(b))


_mm.defvjp(lambda a, b: (_mm(a, b), (a, b)),
           lambda res, d: (_dot_nt(_b(d), _b(res[1])), _dot_tn(_b(res[0]), _b(d))))


@jax.custom_vjp
def _mm_nt(a, b):
    return _dot_nt(_b(a), _b(b))


_mm_nt.defvjp(lambda a, b: (_mm_nt(a, b), (a, b)),
              lambda res, d: (_dot(_b(d), _b(res[1])), _dot_tn(_b(d), _b(res[0]))))


def _dot_split(dot, a, b):
    ah, bh = _b(a), _b(b)
    al, bl = _b(a - ah.astype(F32)), _b(b - bh.astype(F32))
    return (dot(ah, bl) + dot(al, bh)) + dot(ah, bh)


@jax.custom_vjp
def _mm_scores(a, b):
    return _dot_nt(_b(a), _b(b))


_mm_scores.defvjp(lambda a, b: (_mm_scores(a, b), (a, b)),
                  lambda res, d: (_dot_split(_dot, d, res[1]), _dot_split(_dot_tn, d, res[0])))


@jax.custom_vjp
def _mm_tn(a, b):
    return _dot_tn(_b(a), _b(b))


_mm_tn.defvjp(lambda a, b: (_mm_tn(a, b), (a, b)),
              lambda res, d: (_dot_nt(_b(res[1]), _b(d)), _dot(_b(res[0]), _b(d))))


@jax.custom_vjp
def _split_heads(x):
    return tuple(x[:, h * A_DK:(h + 1) * A_DK] for h in range(A_HEADS))


def _split_heads_fwd(x):
    return _split_heads(x), None


def _split_heads_bwd(_, parts):
    return (jnp.concatenate(parts, axis=1),)


_split_heads.defvjp(_split_heads_fwd, _split_heads_bwd)


def _hgrn2_chunk_fast(sts, q, fl, iv, gl, l0, l1, l2, ng):
    lb = _lower_bound(l0, l1, l2)
    f = lb + (1.0 - lb) * jax.nn.sigmoid(fl)
    return _hgrn2_fast_core(sts, q, f, jnp.log(f), iv, gl, ng)


def _hgrn2_fast_core(sts, q, f, logf, iv, gl, ng):
    g = _cumsum_rows(logf)
    g_mid, g_last = _half_sums(logf)
    g_last = g_mid + g_last
    k = 1.0 - f
    qf = jax.nn.silu(q)
    qms = _split_heads(qf * jnp.exp(g - g_mid))
    kms = _split_heads(k * jnp.exp(g_mid - g))
    qgs = _split_heads(qf * jnp.exp(g))
    kds = _split_heads(k * jnp.exp(g_last - g))
    ivs = _split_heads(iv)
    decays = _split_heads(jnp.exp(g_last))
    n = q.shape[0]
    causal = lax.broadcasted_iota(jnp.int32, (n, n), 0) >= lax.broadcasted_iota(jnp.int32, (n, n), 1)
    raw = [_mm_scores(qm, km) for qm, km in zip(qms, kms)]
    inter = [_mm_nt(qg, st) for qg, st in zip(qgs, sts)]
    scores = [jnp.where(causal, s, 0.0) for s in raw]
    os = [a + _mm(s, v) for a, s, v in zip(inter, scores, ivs)]
    new_sts = [st * d + _mm_tn(v, kd) for st, d, v, kd in zip(sts, decays, ivs, kds)]
    os = [o * lax.rsqrt(jnp.mean(o * o, axis=-1, keepdims=True) + EPS) for o in os]
    return new_sts, jnp.concatenate(os, axis=1) * ng * jax.nn.silu(gl)


A_STEP_CHUNKS = 4


def _chunk_rows(j):
    return pl.ds(pl.multiple_of(j * A_CHUNK, A_CHUNK), A_CHUNK)


def _sub_rows(j, i):
    return pl.ds(pl.multiple_of(j * A_CHUNK + i * A_SUB, A_SUB), A_SUB)


def _sub_blocks(ref, head, j):
    lanes = slice(head * A_DK, (head + 1) * A_DK)
    return [ref[_sub_rows(j, i), lanes] for i in range(A_CHUNK // A_SUB)]


def hgrn2_fwd(proj, lb_table, a_norm, batch, name, exchange=None):
    t = proj.shape[0]
    n_steps = t // batch // (A_CHUNK * A_STEP_CHUNKS)
    rows = A_CHUNK * A_STEP_CHUNKS

    def body(q_ref, f_ref, i_ref, g_ref, lb_ref, ng_ref, o_ref, st_ref, dec_ref, st):
        @pl.when(pl.program_id(1) == 0)
        def _():
            st[...] = jnp.zeros_like(st)

        def chunk(j, carry):
            r = _chunk_rows(j)
            st_ref[j] = st[...]
            lb = _lower_bound(lb_ref[0:1, :], lb_ref[1:2, :], lb_ref[2:3, :])
            f = lb + (1.0 - lb) * jax.nn.sigmoid(f_ref[r, :])
            logf = jnp.log(f)
            decay = jnp.minimum(*_half_sums(logf))
            dec_ref[j] = decay
            mild = jnp.min(decay) >= -A_MAX_LOG_DECAY

            @pl.when(mild)
            def _():
                new_sts, o = _hgrn2_fast_core([st[h] for h in range(A_HEADS)], q_ref[r, :], f, logf,
                                              i_ref[r, :], g_ref[r, :], ng_ref[...])
                for h in range(A_HEADS):
                    st[h] = new_sts[h]
                o_ref[r, :] = o.astype(BF16)

            @pl.when(jnp.logical_not(mild))
            def _():
                for h in range(A_HEADS):
                    lanes = slice(h * A_DK, (h + 1) * A_DK)
                    new_st, outs = _hgrn2_chunk(
                        st[h], _sub_blocks(q_ref, h, j), _sub_blocks(f_ref, h, j), _sub_blocks(i_ref, h, j),
                        _sub_blocks(g_ref, h, j), lb_ref[0:1, lanes], lb_ref[1:2, lanes], lb_ref[2:3, lanes],
                        ng_ref[:, lanes])
                    st[h] = new_st
                    for i, o in enumerate(outs):
                        o_ref[_sub_rows(j, i), lanes] = o.astype(BF16)

            return carry

        lax.fori_loop(0, A_STEP_CHUNKS, chunk, 0)

    def part(k):
        return pl.BlockSpec((rows, A_WIDTH), lambda b, n: (b * n_steps + n, k))

    return _call(
        body, name=name, grid=(batch, n_steps),
        in_specs=[part(0), part(1), part(2), part(3),
                  pl.BlockSpec((3, A_WIDTH), lambda b, n: (0, 0)), pl.BlockSpec((1, A_WIDTH), lambda b, n: (0, 0))],
        out_specs=[part(0),
                   pl.BlockSpec((A_STEP_CHUNKS, A_HEADS, A_DK, A_DK), lambda b, n: (b * n_steps + n, 0, 0, 0)),
                   pl.BlockSpec((A_STEP_CHUNKS, 1, A_WIDTH), lambda b, n: (b * n_steps + n, 0, 0))],
        out_shape=[jax.ShapeDtypeStruct((t, A_WIDTH), BF16),
                   jax.ShapeDtypeStruct((t // A_CHUNK, A_HEADS, A_DK, A_DK), F32),
                   jax.ShapeDtypeStruct((t // A_CHUNK, 1, A_WIDTH), F32)],
        scratch_shapes=[pltpu.VMEM((A_HEADS, A_DK, A_DK), F32)],
        args=(proj, proj, proj, proj, lb_table, a_norm), exchange=exchange)


def hgrn2_bwd(proj, states, decays, lb_table, a_norm, do, batch, name, exchange=None):
    t = proj.shape[0]
    n_steps = t // batch // (A_CHUNK * A_STEP_CHUNKS)
    rows = A_CHUNK * A_STEP_CHUNKS

    def body(q_ref, f_ref, i_ref, g_ref, st_ref, dec_ref, lb_ref, ng_ref, do_ref, dp_ref, dlb_ref, dng_ref, dst):
        @pl.when(jnp.logical_and(pl.program_id(0) == 0, pl.program_id(1) == 0))
        def _():
            dlb_ref[...] = jnp.zeros_like(dlb_ref)
            dng_ref[...] = jnp.zeros_like(dng_ref)

        @pl.when(pl.program_id(1) == 0)
        def _():
            dst[...] = jnp.zeros_like(dst)

        def chunk(jj, carry):
            j = A_STEP_CHUNKS - 1 - jj
            r = _chunk_rows(j)
            mild = jnp.min(dec_ref[j]) >= -A_MAX_LOG_DECAY

            @pl.when(mild)
            def _():
                _, vjp = jax.vjp(
                    _hgrn2_chunk_fast, [st_ref[j, h] for h in range(A_HEADS)], q_ref[r, :], f_ref[r, :],
                    i_ref[r, :], g_ref[r, :], lb_ref[0:1, :], lb_ref[1:2, :], lb_ref[2:3, :], ng_ref[...])
                d_sts, dq, df, di, dg, dl0, dl1, dl2, dng = vjp(
                    ([dst[h] for h in range(A_HEADS)], do_ref[r, :].astype(F32)))
                for h in range(A_HEADS):
                    dst[h] = d_sts[h]
                for k, part in enumerate((dq, df, di, dg)):
                    dp_ref[r, k * A_WIDTH:(k + 1) * A_WIDTH] = part
                for row, val in enumerate((dl0, dl1, dl2)):
                    dlb_ref[row:row + 1, :] += val
                dng_ref[...] += dng

            @pl.when(jnp.logical_not(mild))
            def _():
                for h in range(A_HEADS):
                    lanes = slice(h * A_DK, (h + 1) * A_DK)
                    _, vjp = jax.vjp(
                        _hgrn2_chunk, st_ref[j, h], _sub_blocks(q_ref, h, j), _sub_blocks(f_ref, h, j),
                        _sub_blocks(i_ref, h, j), _sub_blocks(g_ref, h, j), lb_ref[0:1, lanes], lb_ref[1:2, lanes],
                        lb_ref[2:3, lanes], ng_ref[:, lanes])
                    douts = [x.astype(F32) for x in _sub_blocks(do_ref, h, j)]
                    d_st, dqs, dfs, dis, dgs, dl0, dl1, dl2, dng = vjp((dst[h], douts))
                    dst[h] = d_st
                    for k, parts in enumerate((dqs, dfs, dis, dgs)):
                        for i in range(A_CHUNK // A_SUB):
                            dp_ref[_sub_rows(j, i), k * A_WIDTH + h * A_DK:k * A_WIDTH + (h + 1) * A_DK] = parts[i]
                    for row, val in enumerate((dl0, dl1, dl2)):
                        dlb_ref[row:row + 1, lanes] += val
                    dng_ref[:, lanes] += dng

            return carry

        lax.fori_loop(0, A_STEP_CHUNKS, chunk, 0)

    def rev(b, n):
        return b * n_steps + (n_steps - 1 - n)

    def part(k):
        return pl.BlockSpec((rows, A_WIDTH), lambda b, n: (rev(b, n), k))

    const3 = pl.BlockSpec((3, A_WIDTH), lambda b, n: (0, 0))
    const1 = pl.BlockSpec((1, A_WIDTH), lambda b, n: (0, 0))
    return _call(
        body, name=name, grid=(batch, n_steps),
        in_specs=[part(0), part(1), part(2), part(3),
                  pl.BlockSpec((A_STEP_CHUNKS, A_HEADS, A_DK, A_DK), lambda b, n: (rev(b, n), 0, 0, 0)),
                  pl.BlockSpec((A_STEP_CHUNKS, 1, A_WIDTH), lambda b, n: (rev(b, n), 0, 0)),
                  const3, const1, part(0)],
        out_specs=[pl.BlockSpec((rows, 4 * A_WIDTH), lambda b, n: (rev(b, n), 0)), const3, const1],
        out_shape=[jax.ShapeDtypeStruct((t, 4 * A_WIDTH + 2 * B_WIDTH), F32),
                   jax.ShapeDtypeStruct((3, A_WIDTH), F32), jax.ShapeDtypeStruct((1, A_WIDTH), F32)],
        scratch_shapes=[pltpu.VMEM((A_HEADS, A_DK, A_DK), F32)],
        args=(proj, proj, proj, proj, states, decays, lb_table, a_norm, do), exchange=exchange)


B_GDIM = B_WIDTH // B_GROUPS
B_ROWS = 512


def _gmlp_chunk(ubs, vbs, lngs, lnbs, ws, bcols):
    vs = [jax.nn.gelu(v) for v in vbs]
    mu = sum(jnp.sum(v, axis=-1, keepdims=True) for v in vs) * (1.0 / B_WIDTH)
    var = sum(jnp.sum(jnp.square(v - mu), axis=-1, keepdims=True) for v in vs) * (1.0 / B_WIDTH)
    rstd = lax.rsqrt(var + EPS)
    tril = (lax.broadcasted_iota(jnp.int32, (B_CHUNK, B_CHUNK), 0)
            >= lax.broadcasted_iota(jnp.int32, (B_CHUNK, B_CHUNK), 1))
    outs = []
    for g in range(B_GROUPS):
        vn = (vs[g] - mu) * rstd * lngs[g] + lnbs[g]
        w = jnp.where(tril, ws[g], 0.0).astype(BF16)
        outs.append(jax.nn.gelu(ubs[g]) * (_dot(w, vn.astype(BF16)) + bcols[g]))
    return outs


def _gmlp_args(u_ref, v_ref, lng_ref, lnb_ref, w_ref, bt_ref, rows):
    def groups(ref):
        return [ref[rows, g * B_GDIM:(g + 1) * B_GDIM] for g in range(B_GROUPS)]

    def vec(ref):
        return [ref[:, g * B_GDIM:(g + 1) * B_GDIM] for g in range(B_GROUPS)]

    return (groups(u_ref), groups(v_ref), vec(lng_ref), vec(lnb_ref),
            [w_ref[g] for g in range(B_GROUPS)], [bt_ref[:, g:g + 1] for g in range(B_GROUPS)])


def gmlp_fwd(proj, oa, ln_g, ln_b, w, bias_t, name, exchange=None):
    t = proj.shape[0]

    def body(u_ref, v_ref, oa_ref, lng_ref, lnb_ref, w_ref, bt_ref, o_ref):
        o_ref[:, 0:A_WIDTH] = oa_ref[...]
        for n in range(B_ROWS // B_CHUNK):
            rows = slice(n * B_CHUNK, (n + 1) * B_CHUNK)
            outs = _gmlp_chunk(*_gmlp_args(u_ref, v_ref, lng_ref, lnb_ref, w_ref, bt_ref, rows))
            for g, o in enumerate(outs):
                o_ref[rows, A_WIDTH + g * B_GDIM:A_WIDTH + (g + 1) * B_GDIM] = o.astype(BF16)

    vec = pl.BlockSpec((1, B_WIDTH), lambda i: (0, 0))
    return _call(
        body, name=name, grid=(t // B_ROWS,),
        in_specs=[pl.BlockSpec((B_ROWS, B_WIDTH), lambda i: (i, 4)), pl.BlockSpec((B_ROWS, B_WIDTH), lambda i: (i, 5)),
                  pl.BlockSpec((B_ROWS, A_WIDTH), lambda i: (i, 0)), vec, vec,
                  pl.BlockSpec((B_GROUPS, B_CHUNK, B_CHUNK), lambda i: (0, 0, 0)),
                  pl.BlockSpec((B_CHUNK, B_GROUPS), lambda i: (0, 0))],
        out_specs=[pl.BlockSpec((B_ROWS, A_WIDTH + B_WIDTH), lambda i: (i, 0))],
        out_shape=[jax.ShapeDtypeStruct((t, A_WIDTH + B_WIDTH), BF16)],
        args=(proj, proj, oa, ln_g, ln_b, w, bias_t), exchange=exchange)


def gmlp_bwd(proj, dmixin, ln_g, ln_b, w, bias_t, dproj, name, exchange=None):
    t = proj.shape[0]

    def body(u_ref, v_ref, do_ref, lng_ref, lnb_ref, w_ref, bt_ref, dp_in_ref,
             dp_ref, dlng_ref, dlnb_ref, dw_ref, dbt_ref):
        del dp_in_ref

        @pl.when(pl.program_id(0) == 0)
        def _():
            for ref in (dlng_ref, dlnb_ref, dw_ref, dbt_ref):
                ref[...] = jnp.zeros_like(ref)

        for n in range(B_ROWS // B_CHUNK):
            rows = slice(n * B_CHUNK, (n + 1) * B_CHUNK)
            _, vjp = jax.vjp(_gmlp_chunk, *_gmlp_args(u_ref, v_ref, lng_ref, lnb_ref, w_ref, bt_ref, rows))
            douts = [do_ref[rows, g * B_GDIM:(g + 1) * B_GDIM] for g in range(B_GROUPS)]
            dus, dvs, dlngs, dlnbs, dws, dbs = vjp(douts)
            for g in range(B_GROUPS):
                lanes = slice(g * B_GDIM, (g + 1) * B_GDIM)
                dp_ref[rows, lanes] = dus[g]
                dp_ref[rows, B_WIDTH + g * B_GDIM:B_WIDTH + (g + 1) * B_GDIM] = dvs[g]
                dlng_ref[:, lanes] += dlngs[g]
                dlnb_ref[:, lanes] += dlnbs[g]
                dw_ref[g] += dws[g]
                dbt_ref[:, g:g + 1] += dbs[g]

    vec = pl.BlockSpec((1, B_WIDTH), lambda i: (0, 0))
    wspec = pl.BlockSpec((B_GROUPS, B_CHUNK, B_CHUNK), lambda i: (0, 0, 0))
    bspec = pl.BlockSpec((B_CHUNK, B_GROUPS), lambda i: (0, 0))
    return _call(
        body, name=name, grid=(t // B_ROWS,),
        in_specs=[pl.BlockSpec((B_ROWS, B_WIDTH), lambda i: (i, 4)), pl.BlockSpec((B_ROWS, B_WIDTH), lambda i: (i, 5)),
                  pl.BlockSpec((B_ROWS, B_WIDTH), lambda i: (i, 1)), vec, vec, wspec, bspec,
                  pl.BlockSpec(memory_space=pl.ANY)],
        out_specs=[pl.BlockSpec((B_ROWS, 2 * B_WIDTH), lambda i: (i, 2)), vec, vec, wspec, bspec],
        out_shape=[jax.ShapeDtypeStruct(dproj.shape, F32), jax.ShapeDtypeStruct((1, B_WIDTH), F32),
                   jax.ShapeDtypeStruct((1, B_WIDTH), F32), jax.ShapeDtypeStruct((B_GROUPS, B_CHUNK, B_CHUNK), F32),
                   jax.ShapeDtypeStruct((B_CHUNK, B_GROUPS), F32)],
        aliases={7: 0}, args=(proj, proj, dmixin, ln_g, ln_b, w, bias_t, dproj), exchange=exchange)


C_FWD_BLOCKS = 16
C_BWD_BLOCKS = 16
C_PAIR = 2 * C_HEAD_DIM
C_PAIRS = C_HEADS // 2
C_SCALE = 1.0 / math.sqrt(C_HEAD_DIM)
C_ROT_DIM = 2 * C_ROT_HALF
ROPE_ROWS = 1024


def rope_tables(pos_col, name):
    t = pos_col.shape[0]

    def body(p_ref, c_ref, a_ref, b_ref):
        lane = jnp.bitwise_and(lax.broadcasted_iota(jnp.int32, (1, C_PAIR), 1), C_HEAD_DIM - 1)
        j = jnp.bitwise_and(lane, C_ROT_HALF - 1).astype(F32)
        inv = jnp.exp(j * (-math.log(ROPE_THETA) / C_ROT_HALF))
        ang = p_ref[...].astype(F32) * inv
        cos, sin = jnp.cos(ang), jnp.sin(ang)
        c_ref[...] = jnp.where(lane < C_ROT_DIM, cos, 1.0)
        a_ref[...] = jnp.where(lane < C_ROT_HALF, -sin, 0.0)
        b_ref[...] = jnp.where(jnp.logical_and(lane >= C_ROT_HALF, lane < C_ROT_DIM), sin, 0.0)

    tab = pl.BlockSpec((ROPE_ROWS, C_PAIR), lambda i: (i, 0))
    return pl.pallas_call(
        body, name=name, grid=(t // ROPE_ROWS,),
        in_specs=[pl.BlockSpec((ROPE_ROWS, 1), lambda i: (i, 0))],
        out_specs=[tab, tab, tab],
        out_shape=[jax.ShapeDtypeStruct((t, C_PAIR), F32)] * 3,
        compiler_params=_params(("arbitrary",)),
    )(pos_col)


def _rope(x, c, a, b):
    return x * c + pltpu.roll(x, C_PAIR - C_ROT_HALF, 1) * a + pltpu.roll(x, C_ROT_HALF, 1) * b


def _rope_t(d, c, a, b):
    return d * c + pltpu.roll(d * a, C_ROT_HALF, 1) + pltpu.roll(d * b, C_PAIR - C_ROT_HALF, 1)


C_RES = 16


def _residue_major(a, batch):
    return a.reshape(batch, SEQ // C_RES, C_RES, -1).transpose(0, 2, 1, 3).reshape(a.shape)


def _sequence_order(a, batch):
    return a.reshape(batch, C_RES, SEQ // C_RES, -1).transpose(0, 2, 1, 3).reshape(a.shape)


def _block_pieces(idx, dil):
    nblk = SEQ // dil // C_BLOCK
    r, n = idx // nblk, idx % nblk
    per = C_RES // dil
    size = C_BLOCK // per

    def pieces(blk):
        return [((dil * a + r) * (SEQ // C_RES) + size * blk, size) for a in range(per)]

    return pieces(n), pieces(jnp.maximum(n - 1, 0)), n > 0


def _get_rows(ref, pieces):
    return jnp.concatenate([ref[pl.ds(pl.multiple_of(start, 8), size), :] for start, size in pieces], axis=0)


def _set_rows(ref, pieces, val, add=False):
    for k, (start, size) in enumerate(pieces):
        rows = pl.ds(pl.multiple_of(start, 8), size)
        part = val[k * size:(k + 1) * size]
        ref[rows, :] = ref[rows, :] + part if add else part


def _head_masks():
    low = lax.broadcasted_iota(jnp.int32, (1, C_PAIR), 1) < C_HEAD_DIM
    return low, jnp.logical_not(low)


def _attn_mask(has_prev, dil):
    per = C_RES // dil
    size = C_BLOCK // per

    def position(x):
        x = jnp.bitwise_and(x, C_BLOCK - 1)
        return per * jnp.bitwise_and(x, size - 1) + x // size

    j = lax.broadcasted_iota(jnp.int32, (2 * C_BLOCK, 2 * C_BLOCK), 1)
    pi = position(lax.broadcasted_iota(jnp.int32, (2 * C_BLOCK, 2 * C_BLOCK), 0))
    pj = position(j)
    own = j < C_BLOCK
    return jnp.logical_or(jnp.logical_and(own, pj <= pi),
                          jnp.logical_and(jnp.logical_and(jnp.logical_not(own), pj >= pi), has_prev))


def _stack_heads(x):
    low, high = _head_masks()
    return jnp.concatenate([jnp.where(low, x, 0.0), jnp.where(high, x, 0.0)], axis=0)


def _unstack_heads(x):
    low, _ = _head_masks()
    return jnp.where(low, x[:C_BLOCK], x[C_BLOCK:])


def attn_fwd(qkv, cos_t, sin_a, sin_b, batch, name, exchange=None):
    t = qkv.shape[0]
    nbr = len(C_DILATIONS)

    def body(q_ref, k_ref, v_ref, c_ref, a_ref, b_ref, o_ref, l_ref, qr_ref, kr_ref, qs, ks, *stats):
        acc, mm, dd = stats[0:nbr], stats[nbr:2 * nbr], stats[2 * nbr:3 * nbr]
        c, a, b = c_ref[...], a_ref[...], b_ref[...]
        qs[...] = _rope(q_ref[...], c, a, b) * C_SCALE
        ks[...] = _rope(k_ref[...], c, a, b)
        qr_ref[...] = qs[...].astype(BF16)
        kr_ref[...] = ks[...].astype(BF16)

        def load(idx, dil):
            own, prev, has_prev = _block_pieces(idx, dil)
            return own, (has_prev, _get_rows(qs, own), _get_rows(ks, own), _get_rows(ks, prev),
                         _get_rows(v_ref, own), _get_rows(v_ref, prev))

        def scores(dil, has_prev, q, k_own, k_prev, v_own, v_prev):
            k_cat = jnp.concatenate([k_own, k_prev], axis=0).astype(BF16)
            return jnp.where(_attn_mask(has_prev, dil), _dot_nt(_stack_heads(q).astype(BF16), k_cat), NEG_BIG)

        def softmax(s):
            m = jnp.max(s, axis=-1, keepdims=True)
            p = jnp.exp(s - m)
            return p.astype(BF16), m, jnp.sum(p, axis=-1, keepdims=True)

        def values(pb, has_prev, q, k_own, k_prev, v_own, v_prev):
            low, high = _head_masks()
            v_cat = jnp.concatenate([v_own, v_prev], axis=0)
            p_wide = jnp.concatenate([pb[:C_BLOCK], pb[C_BLOCK:]], axis=1)
            v_tall = jnp.concatenate([jnp.where(low, v_cat, 0.0), jnp.where(high, v_cat, 0.0)], axis=0).astype(BF16)
            return _dot(p_wide, v_tall)

        for bi, dil in enumerate(C_DILATIONS):
            def pair(i, carry, bi=bi, dil=dil):
                low, _ = _head_masks()
                loaded = [load(C_FWD_BLOCKS * i + k, dil) for k in range(C_FWD_BLOCKS)]
                ss = [scores(dil, *ops) for _, ops in loaded]
                sm = [softmax(s) for s in ss]
                pvs = [values(pb, *ops) for (pb, _, _), (_, ops) in zip(sm, loaded)]
                for (own, _), (_, m, den), pv in zip(loaded, sm, pvs):
                    _set_rows(acc[bi], own, pv)
                    _set_rows(mm[bi], own, jnp.where(low, m[:C_BLOCK], m[C_BLOCK:]))
                    _set_rows(dd[bi], own, jnp.where(low, den[:C_BLOCK], den[C_BLOCK:]))
                return carry

            lax.fori_loop(0, SEQ // C_BLOCK // C_FWD_BLOCKS, pair, 0)
        step = 2 * C_BLOCK
        for r0 in range(0, SEQ, step):
            rr = slice(r0, r0 + step)
            ms = [mm[g][rr, :] for g in range(nbr)]
            m_all = functools.reduce(jnp.maximum, ms)
            ws = [jnp.exp(m - m_all) for m in ms]
            num = sum(acc[g][rr, :] * ws[g] for g in range(nbr))
            den = sum(dd[g][rr, :] * ws[g] for g in range(nbr))
            o_ref[rr, :] = (num / den).astype(BF16)
            l_ref[rr, :] = m_all + jnp.log(den)

    def col(k):
        return pl.BlockSpec((SEQ, C_PAIR), lambda b, p: (b, k * C_PAIRS + p))

    tab = pl.BlockSpec((SEQ, C_PAIR), lambda b, p: (b, 0))
    return _call(
        body, name=name, grid=(batch, C_PAIRS),
        in_specs=[col(0), col(1), col(2), tab, tab, tab],
        out_specs=[col(0), col(0), col(0), col(0)],
        out_shape=[jax.ShapeDtypeStruct((t, D_MODEL), BF16), jax.ShapeDtypeStruct((t, D_MODEL), F32),
                   jax.ShapeDtypeStruct((t, D_MODEL), BF16), jax.ShapeDtypeStruct((t, D_MODEL), BF16)],
        scratch_shapes=[pltpu.VMEM((SEQ, C_PAIR), F32)] * (2 + 3 * nbr),
        args=(qkv, qkv, qkv, cos_t, sin_a, sin_b), exchange=exchange)


def attn_bwd(qr, kr, qkv, cos_t, sin_a, sin_b, o, lse, do, batch, name, exchange=None):
    t = qkv.shape[0]

    def body(q_ref, k_ref, v_ref, c_ref, a_ref, b_ref, o_ref, l_ref, do_ref, dqkv_ref, qs, ks, dqs, dks, dvs, dlt):
        low, _ = _head_masks()
        c, a, b = c_ref[...], a_ref[...], b_ref[...]
        qs[...] = q_ref[...].astype(F32)
        ks[...] = k_ref[...].astype(F32)
        prod = do_ref[...] * o_ref[...].astype(F32)
        s_low = jnp.sum(jnp.where(low, prod, 0.0), axis=-1, keepdims=True)
        s_all = jnp.sum(prod, axis=-1, keepdims=True)
        dlt[...] = jnp.where(low, s_low, s_all - s_low)
        dqs[...] = jnp.zeros_like(dqs)
        dks[...] = jnp.zeros_like(dks)
        dvs[...] = jnp.zeros_like(dvs)

        def load(idx, dil):
            own, prev, has_prev = _block_pieces(idx, dil)
            return (own, prev), (has_prev, _get_rows(qs, own), _get_rows(do_ref, own), _get_rows(ks, own),
                                 _get_rows(ks, prev), _get_rows(v_ref, own), _get_rows(v_ref, prev),
                                 _get_rows(l_ref, own), _get_rows(dlt, own))

        def operands(dil, has_prev, q, do, k_own, k_prev, v_own, v_prev, l_full, d_full):
            lcol = jnp.concatenate([l_full[:, 0:1], l_full[:, C_HEAD_DIM:C_HEAD_DIM + 1]], axis=0)
            dcol = jnp.concatenate([d_full[:, 0:1], d_full[:, C_HEAD_DIM:C_HEAD_DIM + 1]], axis=0)
            return (_stack_heads(q).astype(BF16), _stack_heads(do).astype(BF16),
                    jnp.concatenate([k_own, k_prev], axis=0).astype(BF16),
                    jnp.concatenate([v_own, v_prev], axis=0).astype(BF16), lcol, dcol, _attn_mask(has_prev, dil))

        for dil in C_DILATIONS:
            def pair(i, carry, dil=dil):
                loaded = [load(C_BWD_BLOCKS * i + k, dil) for k in range(C_BWD_BLOCKS)]
                ops = [operands(dil, *o) for _, o in loaded]
                ss = [_dot_nt(q_stack, k_cat) for q_stack, _, k_cat, _, _, _, _ in ops]
                dps = [_dot_nt(do_stack, v_cat) for _, do_stack, _, v_cat, _, _, _ in ops]
                ps = [jnp.exp(jnp.where(o[6], s, NEG_BIG) - o[4]) for s, o in zip(ss, ops)]
                dss = [(p * (dp - o[5])).astype(BF16) for p, dp, o in zip(ps, dps, ops)]
                dvs_ = [_dot_tn(p.astype(BF16), o[1]) for p, o in zip(ps, ops)]
                dks_ = [_dot_tn(ds, o[0]) for ds, o in zip(dss, ops)]
                dqs_ = [_unstack_heads(_dot(ds, o[2])) for ds, o in zip(dss, ops)]
                for ((own, prev), _), dq, dk_cat, dv_cat in zip(loaded, dqs_, dks_, dvs_):
                    _set_rows(dqs, own, dq, add=True)
                    _set_rows(dks, own, dk_cat[:C_BLOCK], add=True)
                    _set_rows(dvs, own, dv_cat[:C_BLOCK], add=True)
                    _set_rows(dks, prev, dk_cat[C_BLOCK:], add=True)
                    _set_rows(dvs, prev, dv_cat[C_BLOCK:], add=True)
                return carry

            lax.fori_loop(0, SEQ // C_BLOCK // C_BWD_BLOCKS, pair, 0)
        dqkv_ref[0] = _rope_t(dqs[...] * C_SCALE, c, a, b).astype(BF16)
        dqkv_ref[1] = _rope_t(dks[...], c, a, b).astype(BF16)
        dqkv_ref[2] = dvs[...].astype(BF16)

    def col(k):
        return pl.BlockSpec((SEQ, C_PAIR), lambda b, p: (b, k * C_PAIRS + p))

    tab = pl.BlockSpec((SEQ, C_PAIR), lambda b, p: (b, 0))
    return _call(
        body, name=name, grid=(batch, C_PAIRS),
        in_specs=[col(0), col(0), col(2), tab, tab, tab, col(0), col(0), col(0)],
        out_specs=[pl.BlockSpec((3, SEQ, C_PAIR), lambda b, p: (0, b, p))],
        out_shape=[jax.ShapeDtypeStruct((3, t, D_MODEL), BF16)],
        scratch_shapes=[pltpu.VMEM((SEQ, C_PAIR), F32)] * 6,
        args=(qr, kr, qkv, cos_t, sin_a, sin_b, o, lse, do), exchange=exchange)


def allreduce_small(slab, name):
    rows, lanes = slab.shape

    def body(x_ref, out_ref, gath, send_sems, recv_sems, local_sem):
        x, y, c, chips = _place()
        me, sibling = (x, y, c), (x, y, 1 - c)

        def slot(px, py, pc):
            return gath.at[4 * px + 2 * py + pc]

        def copy(k, block, to, src=None):
            return pltpu.make_async_remote_copy(
                src_ref=slot(*block) if src is None else src, dst_ref=slot(*block),
                send_sem=send_sems.at[k], recv_sem=recv_sems.at[k], device_id=to, device_id_type=MESH)

        mine = pltpu.make_async_copy(x_ref, slot(*me), local_sem)
        mine.start()
        first = [copy(0, me, sibling, src=x_ref)]
        first += [copy(1 + j, me, (*chip, c), src=x_ref) for j, chip in enumerate(chips)]
        for cp in first:
            cp.start()
        passed = [copy(4 + j, (*chip, c), sibling) for j, chip in enumerate(chips)]
        for j, chip in enumerate(chips):
            copy(1 + j, (*chip, c), me).wait_recv()
            passed[j].start()
        copy(0, sibling, me).wait_recv()
        for j, chip in enumerate(chips):
            copy(4 + j, (*chip, 1 - c), me).wait_recv()
        for cp in first + passed:
            cp.wait_send()
        mine.wait()
        total = gath[0]
        for d in range(1, N_DEV):
            total = total + gath[d]
        out_ref[...] = total

    return pl.pallas_call(
        body, name=name,
        in_specs=[pl.BlockSpec(memory_space=pltpu.VMEM)],
        out_specs=pl.BlockSpec(memory_space=pltpu.VMEM),
        out_shape=jax.ShapeDtypeStruct((rows, lanes), F32),
        scratch_shapes=[pltpu.VMEM((N_DEV, rows, lanes), F32),
                        pltpu.SemaphoreType.DMA((7,)), pltpu.SemaphoreType.DMA((7,)), pltpu.SemaphoreType.DMA],
    )(slab)


ELT_ROWS = 512
SUM_ROWS = 1024


def reduce_slabs(r, name):
    r = r.reshape(N_CHIPS, -1, r.shape[-1])
    _, rows, cols = r.shape
    br = min(rows, SUM_ROWS)

    def body(r_ref, o_ref):
        o_ref[...] = ((r_ref[3].astype(F32) + r_ref[0].astype(F32)) + r_ref[1].astype(F32)) + r_ref[2].astype(F32)

    return pl.pallas_call(
        body, name=name, grid=(rows // br,),
        in_specs=[pl.BlockSpec((N_CHIPS, br, cols), lambda i: (0, i, 0))],
        out_specs=pl.BlockSpec((br, cols), lambda i: (i, 0)),
        out_shape=jax.ShapeDtypeStruct((rows, cols), F32),
        compiler_params=_params(("arbitrary",)),
    )(r)


def _adamw(w, g, m, v):
    m = ADAM_B1 * m + (1.0 - ADAM_B1) * g
    v = ADAM_B2 * v + (1.0 - ADAM_B2) * jnp.square(g)
    m_hat = m / (1.0 - ADAM_B1 ** ADAM_STEP)
    v_hat = v / (1.0 - ADAM_B2 ** ADAM_STEP)
    delta = -ADAM_LR * (m_hat / (jnp.sqrt(v_hat) + ADAM_EPS) + ADAM_WD * w)
    return delta, m, v


def adamw_big(w, s_mine, s_sibling, m, v, name):
    rows, cols = w.shape
    parts = len(s_mine)
    br = min(rows // parts, ELT_ROWS)
    nb = rows // parts // br

    def body(w_ref, m_ref, v_ref, *rest):
        sums, (g_out, d_out, m_out, v_out) = rest[:2 * parts], rest[2 * parts:]
        p = pl.program_id(0)
        g = sums[0][...] + sums[parts][...]
        for k in range(1, parts):
            g = jnp.where(p == k, sums[k][...] + sums[parts + k][...], g)
        g_out[...] = g
        d_out[...], m_out[...], v_out[...] = _adamw(w_ref[...], g, m_ref[...], v_ref[...])

    def part_spec(k):
        return pl.BlockSpec((br, cols), lambda p, i: (jnp.where(p == k, i, jnp.where(p < k, 0, nb - 1)), 0))

    blk = pl.BlockSpec((br, cols), lambda p, i: (p * nb + i, 0))
    out = jax.ShapeDtypeStruct((rows, cols), F32)
    return pl.pallas_call(
        body, name=name, grid=(parts, nb),
        in_specs=[blk] * 3 + [part_spec(k) for k in range(parts)] * 2, out_specs=[blk] * 4, out_shape=[out] * 4,
        compiler_params=_params(("arbitrary", "arbitrary")),
    )(w, m, v, *s_mine, *s_sibling)


def adamw_small(ws, gs, ms, vs, name):
    n = len(ws)

    def body(*refs):
        w_refs, g_refs, m_refs, v_refs = (refs[k * n:(k + 1) * n] for k in range(4))
        d_out, m_out, v_out = (refs[(4 + k) * n:(5 + k) * n] for k in range(3))
        for i in range(n):
            d_out[i][...], m_out[i][...], v_out[i][...] = _adamw(
                w_refs[i][...], g_refs[i][...], m_refs[i][...], v_refs[i][...])

    outs = [jax.ShapeDtypeStruct(w.shape, F32) for w in ws]
    res = pl.pallas_call(body, name=name, out_shape=outs * 3)(*ws, *gs, *ms, *vs)
    return res[:n], res[n:2 * n], res[2 * n:]


SLAB_LANES = 128
SLAB_ROW_ALIGN = 8


def _pack(parts):
    flat = jnp.concatenate([p.reshape(-1) for p in parts])
    rows = -(-flat.shape[0] // (SLAB_LANES * SLAB_ROW_ALIGN)) * SLAB_ROW_ALIGN
    flat = jnp.pad(flat, (0, rows * SLAB_LANES - flat.shape[0]))
    return flat.reshape(rows, SLAB_LANES)


def _unpack(slab, shapes):
    flat = slab.reshape(-1)
    out, pos = [], 0
    for s in shapes:
        size = math.prod(s)
        out.append(flat[pos:pos + size].reshape(s))
        pos += size
    return out


def kernel(x, positions, norm_mix_pre, norm_mix_post, norm_ffn_pre, norm_ffn_post, w_in_even, lb_table, a_norm, b_ln_g, b_ln_b, b_ws, b_bias, w_out_even, w_in_odd, w_out_odd, w_ff1, w_ff2, loss_target, m_norm_mix_pre, m_norm_mix_post, m_norm_ffn_pre, m_norm_ffn_post, m_w_in_even, m_lb_table, m_a_norm, m_b_ln_g, m_b_ln_b, m_b_ws, m_b_bias, m_w_out_even, m_w_in_odd, m_w_out_odd, m_w_ff1, m_w_ff2, v_norm_mix_pre, v_norm_mix_post, v_norm_ffn_pre, v_norm_ffn_post, v_w_in_even, v_lb_table, v_a_norm, v_b_ln_g, v_b_ln_b, v_b_ws, v_b_bias, v_w_out_even, v_w_in_odd, v_w_out_odd, v_w_ff1, v_w_ff2):
    batch = x.shape[0]
    t = batch * SEQ
    d = D_MODEL
    x0 = x.reshape(t, d)
    target = loss_target.reshape(t, d)

    def gain(p, layer):
        return p[layer:layer + 1]

    def gather(*shards):
        return _Exchange("gather", [w.astype(BF16) for w in shards])

    def scatter(*grads):
        return _Exchange("scatter", grads)

    (win_e,) = exchange_alone(gather(w_in_even[0]), "gather_in_even")
    bias_t = b_bias[0].T
    proj, h0, w1_0 = norm_matmul(x0, gain(norm_mix_pre, 0), win_e, "in_proj_even", exchange=gather(w_ff1[0]))
    oa, states, decays, w2_0 = hgrn2_fwd(proj, lb_table, a_norm, batch, "hgrn2_fwd", exchange=gather(w_ff2[0]))
    mixin, wout_e = gmlp_fwd(proj, oa, b_ln_g, b_ln_b, b_ws[0], bias_t, "gmlp_fwd", exchange=gather(w_out_even[0]))
    mix0, x1 = out_proj(mixin, wout_e, x0, gain(norm_mix_post, 0), "out_proj_even")
    x2, hf0, a0, y0, win_o, wout_o = ffn_fwd(x1, gain(norm_ffn_pre, 0), w1_0, w2_0, gain(norm_ffn_post, 0),
                                             "ffn_fwd_0", exchange=gather(w_in_odd[0], w_out_odd[0]))
    x2p = _residue_major(x2, batch)
    qkv, h1 = norm_matmul(x2p, gain(norm_mix_pre, 1), win_o, "in_proj_odd")
    cos_t, sin_a, sin_b = rope_tables(_residue_major(positions.reshape(t, 1), batch), "rope_tables")
    ao, lse, q_rot, k_rot, w1_1, w2_1 = attn_fwd(qkv, cos_t, sin_a, sin_b, batch, "attn_fwd",
                                                 exchange=gather(w_ff1[1], w_ff2[1]))
    mix1, x3 = out_proj(ao, wout_o, x2p, gain(norm_mix_post, 1), "out_proj_odd")
    dx4, hf1, a1, y1, loss_part = ffn_fwd(x3, gain(norm_ffn_pre, 1), w1_1, w2_1, gain(norm_ffn_post, 1),
                                          "ffn_fwd_1", target=_residue_major(target, batch))

    hc = D_FF // N_CHIPS
    dx3, dy1, da1, dg_fpre1, dg_fpost1 = ffn_bwd(
        dx4, x3, y1, a1, gain(norm_ffn_pre, 1), gain(norm_ffn_post, 1), w1_1, w2_1, "ffn_bwd_1")
    g_w1_1 = weight_grad(hf1, da1, "b", d, hc, False, "wgrad_ff1_1")
    g_w2_1 = weight_grad(a1, dy1, "a", hc, d, True, "wgrad_ff2_1")
    dmix1, dao, dg_mpost1 = out_proj_bwd(dx3, mix1, gain(norm_mix_post, 1), wout_o, "out_proj_bwd_odd")
    g_wout_o = weight_grad(ao, dmix1, "a", d // N_CHIPS, d, False, "wgrad_out_odd")
    dqkv, r_w1_1, r_w2_1, r_wout_o = attn_bwd(q_rot, k_rot, qkv, cos_t, sin_a, sin_b, ao, lse, dao, batch, "attn_bwd",
                                              exchange=scatter(g_w1_1, g_w2_1, g_wout_o))
    dx2p, dg_mpre1 = norm_matmul_bwd(dqkv, win_o, x2p, gain(norm_mix_pre, 1), dx3, "in_proj_bwd_odd")
    dx2 = _sequence_order(dx2p, batch)
    g_win_o = weight_grad_stacked(h1, dqkv, 3 * d // N_CHIPS, "wgrad_in_odd")
    s_w1_1, s_w2_1, s_wout_o = (reduce_slabs(r, n) for r, n in (
        (r_w1_1, "reduce_ff1_1"), (r_w2_1, "reduce_ff2_1"), (r_wout_o, "reduce_out_odd")))
    dx1, dy0, da0, dg_fpre0, dg_fpost0, r_win_o, t_w1_1, t_w2_1, t_wout_o = ffn_bwd(
        dx2, x1, y0, a0, gain(norm_ffn_pre, 0), gain(norm_ffn_post, 0), w1_0, w2_0, "ffn_bwd_0",
        exchange=_Both(scatter(g_win_o), _Swap([s_w1_1, s_w2_1, s_wout_o])))
    g_w1_0 = weight_grad(hf0, da0, "b", d, hc, False, "wgrad_ff1_0")
    g_w2_0 = weight_grad(a0, dy0, "a", hc, d, True, "wgrad_ff2_0")
    dmix0, dmixin, dg_mpost0 = out_proj_bwd(dx1, mix0, gain(norm_mix_post, 0), wout_e, "out_proj_bwd_even")
    g_wout_e = weight_grad(mixin, dmix0, "a", d // N_CHIPS, d, False, "wgrad_out_even")
    s_win_o = reduce_slabs(r_win_o, "reduce_in_odd")
    dproj, d_lb, d_anorm, r_w1_0, t_win_o = hgrn2_bwd(
        proj, states, decays, lb_table, a_norm, dmixin, batch, "hgrn2_bwd",
        exchange=_Both(scatter(g_w1_0), _Swap([s_win_o])))
    s_w1_0 = reduce_slabs(r_w1_0, "reduce_ff1_0")
    dproj, d_lng, d_lnb, d_ws, d_bias_t, r_w2_0, t_w1_0 = gmlp_bwd(
        proj, dmixin, b_ln_g, b_ln_b, b_ws[0], bias_t, dproj, "gmlp_bwd",
        exchange=_Both(scatter(g_w2_0), _Swap([s_w1_0])))
    s_w2_0 = reduce_slabs(r_w2_0, "reduce_ff2_0")
    g_win_e, r_wout_e, t_w2_0 = weight_grad(h0, dproj, "b", d, 3 * d // N_CHIPS, False, "wgrad_in_even",
                                            exchange=_Both(scatter(g_wout_e), _Swap([s_w2_0])))
    s_wout_e = reduce_slabs(r_wout_e, "reduce_out_even")
    dx0, dg_mpre0, r_win_e, t_wout_e = norm_matmul_bwd(
        dproj, win_e, x0, gain(norm_mix_pre, 0), dx1, "in_proj_bwd_even",
        exchange=_Both(scatter(g_win_e), _Swap([s_wout_e])))
    grad_x = dx0.reshape(x.shape)
    s_win_e = reduce_slabs(r_win_e, "reduce_in_even")
    (t_win_e,) = exchange_alone(_Swap([s_win_e]), "sibling_swap")

    big_w = [w_in_even, w_out_even, w_in_odd, w_out_odd, w_ff1, w_ff2]
    big_m = [m_w_in_even, m_w_out_even, m_w_in_odd, m_w_out_odd, m_w_ff1, m_w_ff2]
    big_v = [v_w_in_even, v_w_out_even, v_w_in_odd, v_w_out_odd, v_w_ff1, v_w_ff2]
    mine = [[s_win_e], [s_wout_e], [s_win_o], [s_wout_o], [s_w1_0, s_w1_1], [s_w2_0, s_w2_1]]
    theirs = [[t_win_e], [t_wout_e], [t_win_o], [t_wout_o], [t_w1_0, t_w1_1], [t_w2_0, t_w2_1]]
    big = []
    for i, (w, m, v) in enumerate(zip(big_w, big_m, big_v)):
        two_d = (-1, w.shape[-1])
        res = adamw_big(w.reshape(two_d), mine[i], theirs[i], m.reshape(two_d), v.reshape(two_d), "adamw_big_%d" % i)
        big.append([r.reshape(w.shape) for r in res])

    small_w = [norm_mix_pre, norm_mix_post, norm_ffn_pre, norm_ffn_post, lb_table, a_norm, b_ln_g, b_ln_b, b_ws, b_bias]
    small_m = [m_norm_mix_pre, m_norm_mix_post, m_norm_ffn_pre, m_norm_ffn_post, m_lb_table, m_a_norm, m_b_ln_g,
               m_b_ln_b, m_b_ws, m_b_bias]
    small_v = [v_norm_mix_pre, v_norm_mix_post, v_norm_ffn_pre, v_norm_ffn_post, v_lb_table, v_a_norm, v_b_ln_g,
               v_b_ln_b, v_b_ws, v_b_bias]
    partial = [jnp.concatenate([dg_mpre0, dg_mpre1]), jnp.concatenate([dg_mpost0, dg_mpost1]),
               jnp.concatenate([dg_fpre0, dg_fpre1]), jnp.concatenate([dg_fpost0, dg_fpost1]),
               d_lb, d_anorm, d_lng, d_lnb, d_ws[None], d_bias_t.T[None]]
    *small_g, loss = _unpack(allreduce_small(_pack(partial + [loss_part]), "allreduce_small"),
                             [w.shape for w in small_w] + [()])
    small_d, small_nm, small_nv = adamw_small(small_w, small_g, small_m, small_v, "adamw_small")

    order = ["norm_mix_pre", "norm_mix_post", "norm_ffn_pre", "norm_ffn_post", "w_in_even", "lb_table", "a_norm",
             "b_ln_g", "b_ln_b", "b_ws", "b_bias", "w_out_even", "w_in_odd", "w_out_odd", "w_ff1", "w_ff2"]
    small_names = ["norm_mix_pre", "norm_mix_post", "norm_ffn_pre", "norm_ffn_post", "lb_table", "a_norm",
                   "b_ln_g", "b_ln_b", "b_ws", "b_bias"]
    big_names = ["w_in_even", "w_out_even", "w_in_odd", "w_out_odd", "w_ff1", "w_ff2"]
    grads, deltas, new_m, new_v = {}, {}, {}, {}
    for i, nm in enumerate(small_names):
        grads[nm], deltas[nm], new_m[nm], new_v[nm] = small_g[i], small_d[i], small_nm[i], small_nv[i]
    for i, nm in enumerate(big_names):
        grads[nm], deltas[nm], new_m[nm], new_v[nm] = big[i]
    return (loss, grad_x, *[grads[n] for n in order], *[deltas[n] for n in order],
            *[new_m[n] for n in order], *[new_v[n] for n in order])
```

```python
import functools
import math

import jax
import jax.numpy as jnp
from jax import lax
from jax.experimental import pallas as pl
from jax.experimental.pallas import tpu as pltpu

F32 = jnp.float32
BF16 = jnp.bfloat16
MESH = pl.DeviceIdType.MESH

D_MODEL = 1024
SEQ = 2048
D_FF = 4096
N_CHIPS = 4
A_WIDTH = 512
A_HEADS = 4
A_DK = 128
A_CHUNK = 64
A_SUB = 16
B_WIDTH = 512
B_GROUPS = 4
B_CHUNK = 128
C_HEADS = 16
C_HEAD_DIM = 64
C_ROT_HALF = 8
C_BLOCK = 128
C_DILATIONS = (1, 4, 16)
ROPE_THETA = 500000.0
EPS = 1e-6
ADAM_LR = 0.001
ADAM_B1 = 0.9
ADAM_B2 = 0.999
ADAM_EPS = 1e-08
ADAM_WD = 0.01
ADAM_STEP = 10

ROW_TILE = 512
FFN_ROWS = 1024
WGRAD_ROWS = 2048
VMEM_LIMIT = 56 * 1024 * 1024
NEG_BIG = -1e30


def _params(sem=None):
    return pltpu.CompilerParams(dimension_semantics=sem, vmem_limit_bytes=VMEM_LIMIT)


def _dot(a, b):
    return jnp.dot(a, b, preferred_element_type=F32)


def _dot_nt(a, b):
    return lax.dot_general(a, b, (((1,), (1,)), ((), ())), preferred_element_type=F32)


def _dot_tn(a, b):
    return lax.dot_general(a, b, (((0,), (0,)), ((), ())), preferred_element_type=F32)


def _rms(x, g):
    r = lax.rsqrt(jnp.mean(x * x, axis=-1, keepdims=True) + EPS)
    return x * r * g


def _rms_bwd(x, g, dy):
    r = lax.rsqrt(jnp.mean(x * x, axis=-1, keepdims=True) + EPS)
    xh = x * r
    dg = jnp.sum(dy * xh, axis=0, keepdims=True)
    dxh = dy * g
    dx = r * (dxh - xh * jnp.mean(dxh * xh, axis=-1, keepdims=True))
    return dx, dg


def _accumulate(ref, val, first):
    @pl.when(first)
    def _():
        ref[...] = val

    @pl.when(jnp.logical_not(first))
    def _():
        ref[...] += val


N_DEV = 8
ANY = pl.BlockSpec(memory_space=pl.ANY)


def _place():
    x, y, c = lax.axis_index("x"), lax.axis_index("y"), lax.axis_index("c")
    return x, y, c, [(1 - x, y), (x, 1 - y), (1 - x, 1 - y)]


class _Exchange:
    def __init__(self, kind, arrays):
        self.kind, self.arrays, self.n = kind, list(arrays), len(arrays)
        per_peer = pltpu.SemaphoreType.DMA((3 * self.n,))
        if kind == "gather":
            self.out_shape = [jax.ShapeDtypeStruct((N_CHIPS,) + a.shape, a.dtype) for a in self.arrays]
            self.scratch = [per_peer, per_peer, pltpu.SemaphoreType.DMA((self.n,)), per_peer, per_peer]
        else:
            self.out_shape = [jax.ShapeDtypeStruct(a.shape, a.dtype) for a in self.arrays]
            self.scratch = [per_peer, per_peer, pltpu.SemaphoreType.DMA((self.n,))]

    def _copies(self, ins, outs, sems):
        send_sems, recv_sems, local_sems = sems[:3]
        x, y, c, chips = _place()
        me = 2 * x + y
        local, remote = [], []
        for a in range(self.n):
            if self.kind == "gather":
                local.append(pltpu.make_async_copy(ins[a], outs[a].at[me], local_sems.at[a]))
                half = self.arrays[a].shape[0] // 2

                def rows(ref, core, half=half):
                    return ref.at[pl.ds(core * half, half)]
            else:
                local.append(pltpu.make_async_copy(ins[a].at[me], outs[a].at[3], local_sems.at[a]))
            for j, (px, py) in enumerate(chips):
                k = 3 * a + j
                peer = 2 * px + py

                def copy(src, dst, to, send_sem=send_sems.at[k], recv_sem=recv_sems.at[k]):
                    return pltpu.make_async_remote_copy(src_ref=src, dst_ref=dst, send_sem=send_sem, recv_sem=recv_sem,
                                                        device_id=to, device_id_type=MESH)

                if self.kind == "gather":
                    sent = copy(rows(ins[a], c), rows(outs[a].at[me], c), (px, py, c))
                    landed = copy(rows(ins[a], c), rows(outs[a].at[peer], c), (px, py, c))
                    on = dict(send_sem=sems[3].at[k], recv_sem=sems[4].at[k])
                    passed = copy(rows(outs[a].at[peer], c), rows(outs[a].at[peer], c), (x, y, 1 - c), **on)
                    handed = copy(rows(outs[a].at[peer], c), rows(outs[a].at[peer], 1 - c), (x, y, 1 - c), **on)
                    remote.append((sent, landed, passed, handed))
                else:
                    sent = copy(ins[a].at[peer], outs[a].at[j], (px, py, c))
                    remote.append((sent, sent, None, None))
        return local, remote

    def start(self, ins, outs, sems):
        local, remote = self._copies(ins, outs, sems)
        for cp in local:
            cp.start()
        for sent, _, _, _ in remote:
            sent.start()

    def finish(self, ins, outs, sems):
        local, remote = self._copies(ins, outs, sems)
        for _, landed, passed, _ in remote:
            landed.wait_recv()
            if passed is not None:
                passed.start()
        for sent, _, passed, handed in remote:
            if passed is not None:
                handed.wait_recv()
                passed.wait_send()
            sent.wait_send()
        for cp in local:
            cp.wait()


class _Swap:
    def __init__(self, arrays):
        self.arrays, self.n = list(arrays), len(arrays)
        self.out_shape = [jax.ShapeDtypeStruct(a.shape, a.dtype) for a in self.arrays]
        self.scratch = [pltpu.SemaphoreType.DMA((self.n,)), pltpu.SemaphoreType.DMA((self.n,))]

    def _copies(self, ins, outs, sems):
        x, y, c, _ = _place()
        return [pltpu.make_async_remote_copy(src_ref=ins[a], dst_ref=outs[a], send_sem=sems[0].at[a],
                                             recv_sem=sems[1].at[a], device_id=(x, y, 1 - c), device_id_type=MESH)
                for a in range(self.n)]

    def start(self, ins, outs, sems):
        for cp in self._copies(ins, outs, sems):
            cp.start()

    def finish(self, ins, outs, sems):
        for cp in self._copies(ins, outs, sems):
            cp.wait_recv()
            cp.wait_send()


class _Both:
    def __init__(self, first, second):
        self.parts = (first, second)
        self.arrays, self.n = first.arrays + second.arrays, first.n + second.n
        self.out_shape = first.out_shape + second.out_shape
        self.scratch = first.scratch + second.scratch

    def _split(self, ins, outs, sems):
        a, b = self.parts
        return ((a, ins[:a.n], outs[:a.n], sems[:len(a.scratch)]),
                (b, ins[a.n:], outs[a.n:], sems[len(a.scratch):]))

    def start(self, ins, outs, sems):
        for ex, i, o, s in self._split(ins, outs, sems):
            ex.start(i, o, s)

    def finish(self, ins, outs, sems):
        for ex, i, o, s in self._split(ins, outs, sems):
            ex.finish(i, o, s)


def _call(body, *, name, grid, in_specs, out_specs, out_shape, args, scratch_shapes=(), aliases=None, exchange=None):
    if exchange is None:
        return pl.pallas_call(
            body, name=name, grid=grid, in_specs=in_specs, out_specs=out_specs, out_shape=out_shape,
            scratch_shapes=list(scratch_shapes), input_output_aliases=aliases or {},
            compiler_params=_params(("arbitrary",) * len(grid)))(*args)
    n_in, n_out, n_scr, n_ex = len(in_specs), len(out_specs), len(scratch_shapes), exchange.n
    steps = grid

    def wrapped(*refs):
        ins, refs = refs[:n_in], refs[n_in:]
        ex_in, refs = refs[:n_ex], refs[n_ex:]
        outs, refs = refs[:n_out], refs[n_out:]
        ex_out, refs = refs[:n_ex], refs[n_ex:]
        scr, sems = refs[:n_scr], refs[n_scr:]
        first = functools.reduce(jnp.logical_and, [pl.program_id(k) == 0 for k in range(len(steps))])
        last = functools.reduce(jnp.logical_and, [pl.program_id(k) == steps[k] - 1 for k in range(len(steps))])

        @pl.when(first)
        def _():
            exchange.start(ex_in, ex_out, sems)

        body(*ins, *outs, *scr)

        @pl.when(last)
        def _():
            exchange.finish(ex_in, ex_out, sems)

    return pl.pallas_call(
        wrapped, name=name, grid=grid,
        in_specs=list(in_specs) + [ANY] * n_ex, out_specs=list(out_specs) + [ANY] * n_ex,
        out_shape=list(out_shape) + exchange.out_shape,
        scratch_shapes=list(scratch_shapes) + exchange.scratch, input_output_aliases=aliases or {},
        compiler_params=_params(("arbitrary",) * len(grid)))(*args, *exchange.arrays)


def exchange_alone(exchange, name):
    def body(*refs):
        n = exchange.n
        exchange.start(refs[:n], refs[n:2 * n], refs[2 * n:])
        exchange.finish(refs[:n], refs[n:2 * n], refs[2 * n:])

    return pl.pallas_call(
        body, name=name, in_specs=[ANY] * exchange.n, out_specs=[ANY] * exchange.n,
        out_shape=exchange.out_shape, scratch_shapes=exchange.scratch)(*exchange.arrays)


def norm_matmul(x, g, wg, name, exchange=None):
    t, d = x.shape
    nl = wg.shape[2]

    def body(x_ref, g_ref, w_ref, o_ref, h_ref):
        h = _rms(x_ref[...], g_ref[...]).astype(BF16)
        h_ref[...] = h
        for c in range(N_CHIPS):
            o_ref[:, c * nl:(c + 1) * nl] = _dot(h, w_ref[c])

    return _call(
        body, name=name, grid=(t // ROW_TILE,),
        in_specs=[pl.BlockSpec((ROW_TILE, d), lambda i: (i, 0)),
                  pl.BlockSpec((1, d), lambda i: (0, 0)),
                  pl.BlockSpec((N_CHIPS, d, nl), lambda i: (0, 0, 0))],
        out_specs=[pl.BlockSpec((ROW_TILE, N_CHIPS * nl), lambda i: (i, 0)),
                   pl.BlockSpec((ROW_TILE, d), lambda i: (i, 0))],
        out_shape=[jax.ShapeDtypeStruct((t, N_CHIPS * nl), F32), jax.ShapeDtypeStruct((t, d), BF16)],
        args=(x, g, wg), exchange=exchange)


def norm_matmul_bwd(dproj, wg, x, g, dres, name, exchange=None):
    t, d = x.shape
    nl = wg.shape[2]
    stacked = dproj.ndim == 3
    piece = math.gcd(nl, dproj.shape[-1])

    def body(dp_ref, w_ref, x_ref, g_ref, dres_ref, dx_ref, dg_ref):
        dh = None
        for j in range(N_CHIPS * nl // piece):
            c, off = divmod(j * piece, nl)
            if stacked:
                p, lo = divmod(j * piece, dproj.shape[-1])
                lhs = dp_ref[p, :, lo:lo + piece]
            else:
                lhs = dp_ref[:, j * piece:(j + 1) * piece]
            part = _dot_nt(lhs.astype(BF16), w_ref[c, :, off:off + piece])
            dh = part if dh is None else dh + part
        dx, dg = _rms_bwd(x_ref[...], g_ref[...], dh)
        dx_ref[...] = dres_ref[...] + dx
        _accumulate(dg_ref, dg, pl.program_id(0) == 0)

    row = pl.BlockSpec((ROW_TILE, d), lambda i: (i, 0))
    vec = pl.BlockSpec((1, d), lambda i: (0, 0))
    if stacked:
        dp_spec = pl.BlockSpec((dproj.shape[0], ROW_TILE, dproj.shape[-1]), lambda i: (0, i, 0))
    else:
        dp_spec = pl.BlockSpec((ROW_TILE, N_CHIPS * nl), lambda i: (i, 0))
    return _call(
        body, name=name, grid=(t // ROW_TILE,),
        in_specs=[dp_spec, pl.BlockSpec((N_CHIPS, d, nl), lambda i: (0, 0, 0)), row, vec, row],
        out_specs=[row, vec],
        out_shape=[jax.ShapeDtypeStruct((t, d), F32), jax.ShapeDtypeStruct((1, d), F32)],
        args=(dproj, wg, x, g, dres), exchange=exchange)


def out_proj(a, wg, x, g, name):
    t, d = x.shape
    kl = wg.shape[1]

    def body(a_ref, w_ref, x_ref, g_ref, mix_ref, xo_ref):
        acc = _dot(a_ref[:, 0:kl], w_ref[0])
        for c in range(1, N_CHIPS):
            acc += _dot(a_ref[:, c * kl:(c + 1) * kl], w_ref[c])
        mix_ref[...] = acc
        xo_ref[...] = x_ref[...] + _rms(acc, g_ref[...])

    row = pl.BlockSpec((ROW_TILE, d), lambda i: (i, 0))
    return pl.pallas_call(
        body, name=name, grid=(t // ROW_TILE,),
        in_specs=[row, pl.BlockSpec((N_CHIPS, kl, d), lambda i: (0, 0, 0)), row,
                  pl.BlockSpec((1, d), lambda i: (0, 0))],
        out_specs=[row, row],
        out_shape=[jax.ShapeDtypeStruct((t, d), F32), jax.ShapeDtypeStruct((t, d), F32)],
        compiler_params=_params(("arbitrary",)),
    )(a, wg, x, g)


def out_proj_bwd(dxo, mix, g, wg, name):
    t, d = mix.shape
    kl = wg.shape[1]

    def body(dxo_ref, mix_ref, g_ref, w_ref, dmix_ref, da_ref, dg_ref):
        dmix, dg = _rms_bwd(mix_ref[...], g_ref[...], dxo_ref[...])
        dmb = dmix.astype(BF16)
        dmix_ref[...] = dmb
        for c in range(N_CHIPS):
            da_ref[:, c * kl:(c + 1) * kl] = _dot_nt(dmb, w_ref[c])
        _accumulate(dg_ref, dg, pl.program_id(0) == 0)

    row = pl.BlockSpec((ROW_TILE, d), lambda i: (i, 0))
    vec = pl.BlockSpec((1, d), lambda i: (0, 0))
    return pl.pallas_call(
        body, name=name, grid=(t // ROW_TILE,),
        in_specs=[row, row, vec, pl.BlockSpec((N_CHIPS, kl, d), lambda i: (0, 0, 0))],
        out_specs=[row, row, vec],
        out_shape=[jax.ShapeDtypeStruct((t, d), BF16), jax.ShapeDtypeStruct((t, d), F32),
                   jax.ShapeDtypeStruct((1, d), F32)],
        compiler_params=_params(("arbitrary",)),
    )(dxo, mix, g, wg)


def ffn_fwd(x, gpre, w1g, w2g, gpost, name, exchange=None, target=None):
    t, d = x.shape
    hc = w1g.shape[2]
    with_loss = target is not None

    def body(x_ref, gpre_ref, w1_ref, w2_ref, gpost_ref, *rest):
        if with_loss:
            t_ref, xo_ref, h_ref, a_ref, y_ref, l_ref, acc = rest
        else:
            xo_ref, h_ref, a_ref, y_ref, acc = rest
        i, c = pl.program_id(0), pl.program_id(1)

        @pl.when(c == 0)
        def _():
            h_ref[...] = _rms(x_ref[...], gpre_ref[...]).astype(BF16)

        a = _dot(h_ref[...], w1_ref[...])
        a_ref[...] = a.astype(BF16)
        r = jnp.square(jnp.maximum(a, 0.0)).astype(BF16)
        _accumulate(acc, _dot(r, w2_ref[...]), c == 0)

        @pl.when(c == N_CHIPS - 1)
        def _():
            y = acc[...]
            y_ref[...] = y
            xo = x_ref[...] + _rms(y, gpost_ref[...])
            if with_loss:
                e = xo - t_ref[...]
                xo_ref[...] = e * (1.0 / d)
                part = jnp.sum(jnp.sum(e * e, axis=-1, keepdims=True), axis=0, keepdims=True) * (0.5 / d)
                _accumulate(l_ref, part, i == 0)
            else:
                xo_ref[...] = xo

    row = pl.BlockSpec((FFN_ROWS, d), lambda i, c: (i, 0))
    vec = pl.BlockSpec((1, d), lambda i, c: (0, 0))
    one = pl.BlockSpec((1, 1), lambda i, c: (0, 0))
    return _call(
        body, name=name, grid=(t // FFN_ROWS, N_CHIPS),
        in_specs=[row, vec,
                  pl.BlockSpec((None, d, hc), lambda i, c: (c, 0, 0)),
                  pl.BlockSpec((None, hc, d), lambda i, c: (c, 0, 0)), vec] + ([row] if with_loss else []),
        out_specs=[row, row, pl.BlockSpec((FFN_ROWS, hc), lambda i, c: (i, c)), row] + ([one] if with_loss else []),
        out_shape=[jax.ShapeDtypeStruct((t, d), F32), jax.ShapeDtypeStruct((t, d), BF16),
                   jax.ShapeDtypeStruct((t, N_CHIPS * hc), BF16), jax.ShapeDtypeStruct((t, d), F32)]
        + ([jax.ShapeDtypeStruct((1, 1), F32)] if with_loss else []),
        scratch_shapes=[pltpu.VMEM((FFN_ROWS, d), F32)],
        args=(x, gpre, w1g, w2g, gpost) + ((target,) if with_loss else ()), exchange=exchange)


def ffn_bwd(dxo, x, y, a, gpre, gpost, w1g, w2g, name, exchange=None):
    t, d = x.shape
    hc = w1g.shape[2]

    def body(dxo_ref, x_ref, y_ref, a_ref, gpre_ref, gpost_ref, w1_ref, w2_ref,
             dxi_ref, dy_ref, da_ref, dgpre_ref, dgpost_ref, acc):
        i, c = pl.program_id(0), pl.program_id(1)

        @pl.when(c == 0)
        def _():
            dy, dg = _rms_bwd(y_ref[...], gpost_ref[...], dxo_ref[...])
            dy_ref[...] = dy.astype(BF16)
            _accumulate(dgpost_ref, dg, i == 0)

        dr = _dot_nt(dy_ref[...], w2_ref[...])
        da = (dr * (2.0 * jnp.maximum(a_ref[...].astype(F32), 0.0))).astype(BF16)
        da_ref[...] = da
        _accumulate(acc, _dot_nt(da, w1_ref[...]), c == 0)

        @pl.when(c == N_CHIPS - 1)
        def _():
            dx, dg = _rms_bwd(x_ref[...], gpre_ref[...], acc[...])
            dxi_ref[...] = dxo_ref[...] + dx
            _accumulate(dgpre_ref, dg, i == 0)

    row = pl.BlockSpec((ROW_TILE, d), lambda i, c: (i, 0))
    vec = pl.BlockSpec((1, d), lambda i, c: (0, 0))
    hid = pl.BlockSpec((ROW_TILE, hc), lambda i, c: (i, c))
    return _call(
        body, name=name, grid=(t // ROW_TILE, N_CHIPS),
        in_specs=[row, row, row, hid, vec, vec,
                  pl.BlockSpec((None, d, hc), lambda i, c: (c, 0, 0)),
                  pl.BlockSpec((None, hc, d), lambda i, c: (c, 0, 0))],
        out_specs=[row, row, hid, vec, vec],
        out_shape=[jax.ShapeDtypeStruct((t, d), F32), jax.ShapeDtypeStruct((t, d), BF16),
                   jax.ShapeDtypeStruct((t, N_CHIPS * hc), BF16),
                   jax.ShapeDtypeStruct((1, d), F32), jax.ShapeDtypeStruct((1, d), F32)],
        scratch_shapes=[pltpu.VMEM((ROW_TILE, d), F32)],
        args=(dxo, x, y, a, gpre, gpost, w1g, w2g), exchange=exchange)


def weight_grad(a, b, chunked, bk, bn, relu2, name, exchange=None):
    t = a.shape[0]
    a_on = chunked == "a"
    rows = min(t, WGRAD_ROWS)
    n_steps = t // rows

    def body(a_ref, b_ref, o_ref, acc):
        s = pl.program_id(1)
        av = a_ref[...]
        if relu2:
            av = jnp.square(jnp.maximum(av.astype(F32), 0.0))
        _accumulate(acc, _dot_tn(av.astype(BF16), b_ref[...].astype(BF16)), s == 0)

        @pl.when(s == n_steps - 1)
        def _():
            o_ref[...] = acc[...].astype(BF16)

    res = _call(
        body, name=name, grid=(N_CHIPS, n_steps),
        in_specs=[pl.BlockSpec((rows, bk), (lambda c, s: (s, c)) if a_on else (lambda c, s: (s, 0))),
                  pl.BlockSpec((rows, bn), (lambda c, s: (s, 0)) if a_on else (lambda c, s: (s, c)))],
        out_specs=[pl.BlockSpec((None, bk, bn), lambda c, s: (c, 0, 0))],
        out_shape=[jax.ShapeDtypeStruct((N_CHIPS, bk, bn), BF16)],
        scratch_shapes=[pltpu.VMEM((bk, bn), F32)],
        args=(a, b), exchange=exchange)
    return res[0] if exchange is None else res


def weight_grad_stacked(a, b3, bn, name):
    t, bk = a.shape
    width = b3.shape[-1]
    piece = math.gcd(bn, width)
    rows = min(t, WGRAD_ROWS)
    n_steps = t // rows

    def body(a_ref, b_ref, o_hbm, acc, staged, sem):
        s, c = pl.program_id(0), pl.program_id(1)
        av = a_ref[...].astype(BF16)
        for chunk in range(N_CHIPS):
            @pl.when(c == chunk)
            def _(chunk=chunk):
                cols = [divmod(chunk * bn + k * piece, width) for k in range(bn // piece)]
                b = jnp.concatenate([b_ref[p, :, lo:lo + piece] for p, lo in cols], axis=1).astype(BF16)
                _accumulate(acc.at[chunk], _dot_tn(av, b), s == 0)

                @pl.when(s == n_steps - 1)
                def _():
                    staged[...] = acc[chunk].astype(BF16)
                    copy = pltpu.make_async_copy(staged, o_hbm.at[chunk], sem)
                    copy.start()
                    copy.wait()

    return pl.pallas_call(
        body, name=name, grid=(n_steps, N_CHIPS),
        in_specs=[pl.BlockSpec((rows, bk), lambda s, c: (s, 0)),
                  pl.BlockSpec((b3.shape[0], rows, width), lambda s, c: (0, s, 0))],
        out_specs=ANY,
        out_shape=jax.ShapeDtypeStruct((N_CHIPS, bk, bn), BF16),
        scratch_shapes=[pltpu.VMEM((N_CHIPS, bk, bn), F32), pltpu.VMEM((bk, bn), BF16), pltpu.SemaphoreType.DMA],
        compiler_params=_params(("arbitrary", "arbitrary")),
    )(a, b3)


def _hgrn2_chunk(st, qs, fls, ivs, gls, l0, l1, l2, ng):
    nsub = len(qs)
    mx = jnp.maximum(jnp.maximum(l0, l1), l2)
    e0, e1, e2 = jnp.exp(l0 - mx), jnp.exp(l1 - mx), jnp.exp(l2 - mx)
    lb = e0 / (e0 + e1 + e2)
    rows = lax.broadcasted_iota(jnp.int32, (A_SUB, A_SUB), 0)
    cols = lax.broadcasted_iota(jnp.int32, (A_SUB, A_SUB), 1)
    tri = (rows >= cols).astype(F32)
    keep = (lax.broadcasted_iota(jnp.int32, (A_SUB, A_SUB, A_DK), 0)
            >= lax.broadcasted_iota(jnp.int32, (A_SUB, A_SUB, A_DK), 1))
    base = jnp.zeros_like(l0)
    bases, gs, ks, qfs = [], [], [], []
    for i in range(nsub):
        f = lb + (1.0 - lb) * jax.nn.sigmoid(fls[i])
        logf = jnp.log(f)
        bases.append(base)
        gs.append(base + jnp.dot(tri, logf, precision=lax.Precision.HIGHEST, preferred_element_type=F32))
        base = base + jnp.sum(logf, axis=0, keepdims=True)
        ks.append(1.0 - f)
        qfs.append(jax.nn.silu(qs[i]))
    g_last = base
    stb = st.astype(BF16)
    outs = []
    for i in range(nsub):
        o = _dot_nt((qfs[i] * jnp.exp(gs[i])).astype(BF16), stb)
        if i > 0:
            qt = (qfs[i] * jnp.exp(gs[i] - bases[i])).astype(BF16)
            kk = jnp.concatenate([ks[j] * jnp.exp(bases[i] - gs[j]) for j in range(i)], axis=0).astype(BF16)
            vv = jnp.concatenate(ivs[:i], axis=0).astype(BF16)
            o = o + _dot(_dot_nt(qt, kk).astype(BF16), vv)
        dec = jnp.exp(jnp.where(keep, gs[i][:, None, :] - gs[i][None, :, :], NEG_BIG))
        s_diag = jnp.sum(qfs[i][:, None, :] * ks[i][None, :, :] * dec, axis=-1)
        o = o + _dot(s_diag.astype(BF16), ivs[i].astype(BF16))
        o = o * lax.rsqrt(jnp.mean(o * o, axis=-1, keepdims=True) + EPS) * ng
        outs.append(o * jax.nn.silu(gls[i]))
    kdec = jnp.concatenate([ks[j] * jnp.exp(g_last - gs[j]) for j in range(nsub)], axis=0).astype(BF16)
    vall = jnp.concatenate(ivs, axis=0).astype(BF16)
    new_st = st * jnp.exp(g_last) + _dot_tn(vall, kdec)
    return new_st, outs


A_MAX_LOG_DECAY = 60.0


def _half_sums(logf):
    n = logf.shape[0]
    first = lax.broadcasted_iota(jnp.int32, logf.shape, 0) < n // 2
    return (jnp.sum(jnp.where(first, logf, 0.0), axis=0, keepdims=True),
            jnp.sum(jnp.where(first, 0.0, logf), axis=0, keepdims=True))


def _split3(x):
    hi = x.astype(BF16)
    r1 = x - hi.astype(F32)
    mid = r1.astype(BF16)
    return hi, mid, (r1 - mid.astype(F32)).astype(BF16)


def _tri_matmul(x, transpose):
    n = x.shape[0]
    r = lax.broadcasted_iota(jnp.int32, (n, n), 0)
    c = lax.broadcasted_iota(jnp.int32, (n, n), 1)
    tri = ((r <= c) if transpose else (r >= c)).astype(BF16)
    hi, mid, lo = _split3(x)
    return (_dot(tri, lo) + _dot(tri, mid)) + _dot(tri, hi)


@jax.custom_vjp
def _cumsum_rows(x):
    return _tri_matmul(x, False)


def _cumsum_rows_fwd(x):
    return _tri_matmul(x, False), None


def _cumsum_rows_bwd(_, dy):
    return (_tri_matmul(dy, True),)


_cumsum_rows.defvjp(_cumsum_rows_fwd, _cumsum_rows_bwd)


def _lower_bound(l0, l1, l2):
    mx = jnp.maximum(jnp.maximum(l0, l1), l2)
    e0, e1, e2 = jnp.exp(l0 - mx), jnp.exp(l1 - mx), jnp.exp(l2 - mx)
    return e0 / (e0 + e1 + e2)


def _b(x):
    return x.astype(BF16)


@jax.custom_vjp
def _mm(a, b):
    return _dot(_b(a), _b(b))


_mm.defvjp(lambda a, b: (_mm(a, b), (a, b)),
           lambda res, d: (_dot_nt(_b(d), _b(res[1])), _dot_tn(_b(res[0]), _b(d))))


@jax.custom_vjp
def _mm_nt(a, b):
    return _dot_nt(_b(a), _b(b))


_mm_nt.defvjp(lambda a, b: (_mm_nt(a, b), (a, b)),
              lambda res, d: (_dot(_b(d), _b(res[1])), _dot_tn(_b(d), _b(res[0]))))


def _dot_split(dot, a, b):
    ah, bh = _b(a), _b(b)
    al, bl = _b(a - ah.astype(F32)), _b(b - bh.astype(F32))
    return (dot(ah, bl) + dot(al, bh)) + dot(ah, bh)


@jax.custom_vjp
def _mm_scores(a, b):
    return _dot_nt(_b(a), _b(b))


_mm_scores.defvjp(lambda a, b: (_mm_scores(a, b), (a, b)),
                  lambda res, d: (_dot_split(_dot, d, res[1]), _dot_split(_dot_tn, d, res[0])))


@jax.custom_vjp
def _mm_tn(a, b):
    return _dot_tn(_b(a), _b(b))


_mm_tn.defvjp(lambda a, b: (_mm_tn(a, b), (a, b)),
              lambda res, d: (_dot_nt(_b(res[1]), _b(d)), _dot(_b(res[0]), _b(d))))


@jax.custom_vjp
def _split_heads(x):
    return tuple(x[:, h * A_DK:(h + 1) * A_DK] for h in range(A_HEADS))


def _split_heads_fwd(x):
    return _split_heads(x), None


def _split_heads_bwd(_, parts):
    return (jnp.concatenate(parts, axis=1),)


_split_heads.defvjp(_split_heads_fwd, _split_heads_bwd)


def _hgrn2_chunk_fast(sts, q, fl, iv, gl, l0, l1, l2, ng):
    lb = _lower_bound(l0, l1, l2)
    f = lb + (1.0 - lb) * jax.nn.sigmoid(fl)
    return _hgrn2_fast_core(sts, q, f, jnp.log(f), iv, gl, ng)


def _hgrn2_fast_core(sts, q, f, logf, iv, gl, ng):
    g = _cumsum_rows(logf)
    g_mid, g_last = _half_sums(logf)
    g_last = g_mid + g_last
    k = 1.0 - f
    qf = jax.nn.silu(q)
    qms = _split_heads(qf * jnp.exp(g - g_mid))
    kms = _split_heads(k * jnp.exp(g_mid - g))
    qgs = _split_heads(qf * jnp.exp(g))
    kds = _split_heads(k * jnp.exp(g_last - g))
    ivs = _split_heads(iv)
    decays = _split_heads(jnp.exp(g_last))
    n = q.shape[0]
    causal = lax.broadcasted_iota(jnp.int32, (n, n), 0) >= lax.broadcasted_iota(jnp.int32, (n, n), 1)
    raw = [_mm_scores(qm, km) for qm, km in zip(qms, kms)]
    inter = [_mm_nt(qg, st) for qg, st in zip(qgs, sts)]
    scores = [jnp.where(causal, s, 0.0) for s in raw]
    os = [a + _mm(s, v) for a, s, v in zip(inter, scores, ivs)]
    new_sts = [st * d + _mm_tn(v, kd) for st, d, v, kd in zip(sts, decays, ivs, kds)]
    os = [o * lax.rsqrt(jnp.mean(o * o, axis=-1, keepdims=True) + EPS) for o in os]
    return new_sts, jnp.concatenate(os, axis=1) * ng * jax.nn.silu(gl)


A_STEP_CHUNKS = 4


def _chunk_rows(j):
    return pl.ds(pl.multiple_of(j * A_CHUNK, A_CHUNK), A_CHUNK)


def _sub_rows(j, i):
    return pl.ds(pl.multiple_of(j * A_CHUNK + i * A_SUB, A_SUB), A_SUB)


def _sub_blocks(ref, head, j):
    lanes = slice(head * A_DK, (head + 1) * A_DK)
    return [ref[_sub_rows(j, i), lanes] for i in range(A_CHUNK // A_SUB)]


def hgrn2_fwd(proj, lb_table, a_norm, batch, name, exchange=None):
    t = proj.shape[0]
    n_steps = t // batch // (A_CHUNK * A_STEP_CHUNKS)
    rows = A_CHUNK * A_STEP_CHUNKS

    def body(q_ref, f_ref, i_ref, g_ref, lb_ref, ng_ref, o_ref, st_ref, dec_ref, st):
        @pl.when(pl.program_id(1) == 0)
        def _():
            st[...] = jnp.zeros_like(st)

        def chunk(j, carry):
            r = _chunk_rows(j)
            st_ref[j] = st[...]
            lb = _lower_bound(lb_ref[0:1, :], lb_ref[1:2, :], lb_ref[2:3, :])
            f = lb + (1.0 - lb) * jax.nn.sigmoid(f_ref[r, :])
            logf = jnp.log(f)
            decay = jnp.minimum(*_half_sums(logf))
            dec_ref[j] = decay
            mild = jnp.min(decay) >= -A_MAX_LOG_DECAY

            @pl.when(mild)
            def _():
                new_sts, o = _hgrn2_fast_core([st[h] for h in range(A_HEADS)], q_ref[r, :], f, logf,
                                              i_ref[r, :], g_ref[r, :], ng_ref[...])
                for h in range(A_HEADS):
                    st[h] = new_sts[h]
                o_ref[r, :] = o.astype(BF16)

            @pl.when(jnp.logical_not(mild))
            def _():
                for h in range(A_HEADS):
                    lanes = slice(h * A_DK, (h + 1) * A_DK)
                    new_st, outs = _hgrn2_chunk(
                        st[h], _sub_blocks(q_ref, h, j), _sub_blocks(f_ref, h, j), _sub_blocks(i_ref, h, j),
                        _sub_blocks(g_ref, h, j), lb_ref[0:1, lanes], lb_ref[1:2, lanes], lb_ref[2:3, lanes],
                        ng_ref[:, lanes])
                    st[h] = new_st
                    for i, o in enumerate(outs):
                        o_ref[_sub_rows(j, i), lanes] = o.astype(BF16)

            return carry

        lax.fori_loop(0, A_STEP_CHUNKS, chunk, 0)

    def part(k):
        return pl.BlockSpec((rows, A_WIDTH), lambda b, n: (b * n_steps + n, k))

    return _call(
        body, name=name, grid=(batch, n_steps),
        in_specs=[part(0), part(1), part(2), part(3),
                  pl.BlockSpec((3, A_WIDTH), lambda b, n: (0, 0)), pl.BlockSpec((1, A_WIDTH), lambda b, n: (0, 0))],
        out_specs=[part(0),
                   pl.BlockSpec((A_STEP_CHUNKS, A_HEADS, A_DK, A_DK), lambda b, n: (b * n_steps + n, 0, 0, 0)),
                   pl.BlockSpec((A_STEP_CHUNKS, 1, A_WIDTH), lambda b, n: (b * n_steps + n, 0, 0))],
        out_shape=[jax.ShapeDtypeStruct((t, A_WIDTH), BF16),
                   jax.ShapeDtypeStruct((t // A_CHUNK, A_HEADS, A_DK, A_DK), F32),
                   jax.ShapeDtypeStruct((t // A_CHUNK, 1, A_WIDTH), F32)],
        scratch_shapes=[pltpu.VMEM((A_HEADS, A_DK, A_DK), F32)],
        args=(proj, proj, proj, proj, lb_table, a_norm), exchange=exchange)


def hgrn2_bwd(proj, states, decays, lb_table, a_norm, do, batch, name, exchange=None):
    t = proj.shape[0]
    n_steps = t // batch // (A_CHUNK * A_STEP_CHUNKS)
    rows = A_CHUNK * A_STEP_CHUNKS

    def body(q_ref, f_ref, i_ref, g_ref, st_ref, dec_ref, lb_ref, ng_ref, do_ref, dp_ref, dlb_ref, dng_ref, dst):
        @pl.when(jnp.logical_and(pl.program_id(0) == 0, pl.program_id(1) == 0))
        def _():
            dlb_ref[...] = jnp.zeros_like(dlb_ref)
            dng_ref[...] = jnp.zeros_like(dng_ref)

        @pl.when(pl.program_id(1) == 0)
        def _():
            dst[...] = jnp.zeros_like(dst)

        def chunk(jj, carry):
            j = A_STEP_CHUNKS - 1 - jj
            r = _chunk_rows(j)
            mild = jnp.min(dec_ref[j]) >= -A_MAX_LOG_DECAY

            @pl.when(mild)
            def _():
                _, vjp = jax.vjp(
                    _hgrn2_chunk_fast, [st_ref[j, h] for h in range(A_HEADS)], q_ref[r, :], f_ref[r, :],
                    i_ref[r, :], g_ref[r, :], lb_ref[0:1, :], lb_ref[1:2, :], lb_ref[2:3, :], ng_ref[...])
                d_sts, dq, df, di, dg, dl0, dl1, dl2, dng = vjp(
                    ([dst[h] for h in range(A_HEADS)], do_ref[r, :].astype(F32)))
                for h in range(A_HEADS):
                    dst[h] = d_sts[h]
                for k, part in enumerate((dq, df, di, dg)):
                    dp_ref[r, k * A_WIDTH:(k + 1) * A_WIDTH] = part
                for row, val in enumerate((dl0, dl1, dl2)):
                    dlb_ref[row:row + 1, :] += val
                dng_ref[...] += dng

            @pl.when(jnp.logical_not(mild))
            def _():
                for h in range(A_HEADS):
                    lanes = slice(h * A_DK, (h + 1) * A_DK)
                    _, vjp = jax.vjp(
                        _hgrn2_chunk, st_ref[j, h], _sub_blocks(q_ref, h, j), _sub_blocks(f_ref, h, j),
                        _sub_blocks(i_ref, h, j), _sub_blocks(g_ref, h, j), lb_ref[0:1, lanes], lb_ref[1:2, lanes],
                        lb_ref[2:3, lanes], ng_ref[:, lanes])
                    douts = [x.astype(F32) for x in _sub_blocks(do_ref, h, j)]
                    d_st, dqs, dfs, dis, dgs, dl0, dl1, dl2, dng = vjp((dst[h], douts))
                    dst[h] = d_st
                    for k, parts in enumerate((dqs, dfs, dis, dgs)):
                        for i in range(A_CHUNK // A_SUB):
                            dp_ref[_sub_rows(j, i), k * A_WIDTH + h * A_DK:k * A_WIDTH + (h + 1) * A_DK] = parts[i]
                    for row, val in enumerate((dl0, dl1, dl2)):
                        dlb_ref[row:row + 1, lanes] += val
                    dng_ref[:, lanes] += dng

            return carry

        lax.fori_loop(0, A_STEP_CHUNKS, chunk, 0)

    def rev(b, n):
        return b * n_steps + (n_steps - 1 - n)

    def part(k):
        return pl.BlockSpec((rows, A_WIDTH), lambda b, n: (rev(b, n), k))

    const3 = pl.BlockSpec((3, A_WIDTH), lambda b, n: (0, 0))
    const1 = pl.BlockSpec((1, A_WIDTH), lambda b, n: (0, 0))
    return _call(
        body, name=name, grid=(batch, n_steps),
        in_specs=[part(0), part(1), part(2), part(3),
                  pl.BlockSpec((A_STEP_CHUNKS, A_HEADS, A_DK, A_DK), lambda b, n: (rev(b, n), 0, 0, 0)),
                  pl.BlockSpec((A_STEP_CHUNKS, 1, A_WIDTH), lambda b, n: (rev(b, n), 0, 0)),
                  const3, const1, part(0)],
        out_specs=[pl.BlockSpec((rows, 4 * A_WIDTH), lambda b, n: (rev(b, n), 0)), const3, const1],
        out_shape=[jax.ShapeDtypeStruct((t, 4 * A_WIDTH + 2 * B_WIDTH), F32),
                   jax.ShapeDtypeStruct((3, A_WIDTH), F32), jax.ShapeDtypeStruct((1, A_WIDTH), F32)],
        scratch_shapes=[pltpu.VMEM((A_HEADS, A_DK, A_DK), F32)],
        args=(proj, proj, proj, proj, states, decays, lb_table, a_norm, do), exchange=exchange)


B_GDIM = B_WIDTH // B_GROUPS
B_ROWS = 512


def _gmlp_chunk(ubs, vbs, lngs, lnbs, ws, bcols):
    vs = [jax.nn.gelu(v) for v in vbs]
    mu = sum(jnp.sum(v, axis=-1, keepdims=True) for v in vs) * (1.0 / B_WIDTH)
    var = sum(jnp.sum(jnp.square(v - mu), axis=-1, keepdims=True) for v in vs) * (1.0 / B_WIDTH)
    rstd = lax.rsqrt(var + EPS)
    tril = (lax.broadcasted_iota(jnp.int32, (B_CHUNK, B_CHUNK), 0)
            >= lax.broadcasted_iota(jnp.int32, (B_CHUNK, B_CHUNK), 1))
    outs = []
    for g in range(B_GROUPS):
        vn = (vs[g] - mu) * rstd * lngs[g] + lnbs[g]
        w = jnp.where(tril, ws[g], 0.0).astype(BF16)
        outs.append(jax.nn.gelu(ubs[g]) * (_dot(w, vn.astype(BF16)) + bcols[g]))
    return outs


def _gmlp_args(u_ref, v_ref, lng_ref, lnb_ref, w_ref, bt_ref, rows):
    def groups(ref):
        return [ref[rows, g * B_GDIM:(g + 1) * B_GDIM] for g in range(B_GROUPS)]

    def vec(ref):
        return [ref[:, g * B_GDIM:(g + 1) * B_GDIM] for g in range(B_GROUPS)]

    return (groups(u_ref), groups(v_ref), vec(lng_ref), vec(lnb_ref),
            [w_ref[g] for g in range(B_GROUPS)], [bt_ref[:, g:g + 1] for g in range(B_GROUPS)])


def gmlp_fwd(proj, oa, ln_g, ln_b, w, bias_t, name, exchange=None):
    t = proj.shape[0]

    def body(u_ref, v_ref, oa_ref, lng_ref, lnb_ref, w_ref, bt_ref, o_ref):
        o_ref[:, 0:A_WIDTH] = oa_ref[...]
        for n in range(B_ROWS // B_CHUNK):
            rows = slice(n * B_CHUNK, (n + 1) * B_CHUNK)
            outs = _gmlp_chunk(*_gmlp_args(u_ref, v_ref, lng_ref, lnb_ref, w_ref, bt_ref, rows))
            for g, o in enumerate(outs):
                o_ref[rows, A_WIDTH + g * B_GDIM:A_WIDTH + (g + 1) * B_GDIM] = o.astype(BF16)

    vec = pl.BlockSpec((1, B_WIDTH), lambda i: (0, 0))
    return _call(
        body, name=name, grid=(t // B_ROWS,),
        in_specs=[pl.BlockSpec((B_ROWS, B_WIDTH), lambda i: (i, 4)), pl.BlockSpec((B_ROWS, B_WIDTH), lambda i: (i, 5)),
                  pl.BlockSpec((B_ROWS, A_WIDTH), lambda i: (i, 0)), vec, vec,
                  pl.BlockSpec((B_GROUPS, B_CHUNK, B_CHUNK), lambda i: (0, 0, 0)),
                  pl.BlockSpec((B_CHUNK, B_GROUPS), lambda i: (0, 0))],
        out_specs=[pl.BlockSpec((B_ROWS, A_WIDTH + B_WIDTH), lambda i: (i, 0))],
        out_shape=[jax.ShapeDtypeStruct((t, A_WIDTH + B_WIDTH), BF16)],
        args=(proj, proj, oa, ln_g, ln_b, w, bias_t), exchange=exchange)


def gmlp_bwd(proj, dmixin, ln_g, ln_b, w, bias_t, dproj, name, exchange=None):
    t = proj.shape[0]

    def body(u_ref, v_ref, do_ref, lng_ref, lnb_ref, w_ref, bt_ref, dp_in_ref,
             dp_ref, dlng_ref, dlnb_ref, dw_ref, dbt_ref):
        del dp_in_ref

        @pl.when(pl.program_id(0) == 0)
        def _():
            for ref in (dlng_ref, dlnb_ref, dw_ref, dbt_ref):
                ref[...] = jnp.zeros_like(ref)

        for n in range(B_ROWS // B_CHUNK):
            rows = slice(n * B_CHUNK, (n + 1) * B_CHUNK)
            _, vjp = jax.vjp(_gmlp_chunk, *_gmlp_args(u_ref, v_ref, lng_ref, lnb_ref, w_ref, bt_ref, rows))
            douts = [do_ref[rows, g * B_GDIM:(g + 1) * B_GDIM] for g in range(B_GROUPS)]
            dus, dvs, dlngs, dlnbs, dws, dbs = vjp(douts)
            for g in range(B_GROUPS):
                lanes = slice(g * B_GDIM, (g + 1) * B_GDIM)
                dp_ref[rows, lanes] = dus[g]
                dp_ref[rows, B_WIDTH + g * B_GDIM:B_WIDTH + (g + 1) * B_GDIM] = dvs[g]
                dlng_ref[:, lanes] += dlngs[g]
                dlnb_ref[:, lanes] += dlnbs[g]
                dw_ref[g] += dws[g]
                dbt_ref[:, g:g + 1] += dbs[g]

    vec = pl.BlockSpec((1, B_WIDTH), lambda i: (0, 0))
    wspec = pl.BlockSpec((B_GROUPS, B_CHUNK, B_CHUNK), lambda i: (0, 0, 0))
    bspec = pl.BlockSpec((B_CHUNK, B_GROUPS), lambda i: (0, 0))
    return _call(
        body, name=name, grid=(t // B_ROWS,),
        in_specs=[pl.BlockSpec((B_ROWS, B_WIDTH), lambda i: (i, 4)), pl.BlockSpec((B_ROWS, B_WIDTH), lambda i: (i, 5)),
                  pl.BlockSpec((B_ROWS, B_WIDTH), lambda i: (i, 1)), vec, vec, wspec, bspec,
                  pl.BlockSpec(memory_space=pl.ANY)],
        out_specs=[pl.BlockSpec((B_ROWS, 2 * B_WIDTH), lambda i: (i, 2)), vec, vec, wspec, bspec],
        out_shape=[jax.ShapeDtypeStruct(dproj.shape, F32), jax.ShapeDtypeStruct((1, B_WIDTH), F32),
                   jax.ShapeDtypeStruct((1, B_WIDTH), F32), jax.ShapeDtypeStruct((B_GROUPS, B_CHUNK, B_CHUNK), F32),
                   jax.ShapeDtypeStruct((B_CHUNK, B_GROUPS), F32)],
        aliases={7: 0}, args=(proj, proj, dmixin, ln_g, ln_b, w, bias_t, dproj), exchange=exchange)


C_FWD_BLOCKS = 16
C_BWD_BLOCKS = 16
C_PAIR = 2 * C_HEAD_DIM
C_PAIRS = C_HEADS // 2
C_SCALE = 1.0 / math.sqrt(C_HEAD_DIM)
C_ROT_DIM = 2 * C_ROT_HALF
ROPE_ROWS = 1024


def rope_tables(pos_col, name):
    t = pos_col.shape[0]

    def body(p_ref, c_ref, a_ref, b_ref):
        lane = jnp.bitwise_and(lax.broadcasted_iota(jnp.int32, (1, C_PAIR), 1), C_HEAD_DIM - 1)
        j = jnp.bitwise_and(lane, C_ROT_HALF - 1).astype(F32)
        inv = jnp.exp(j * (-math.log(ROPE_THETA) / C_ROT_HALF))
        ang = p_ref[...].astype(F32) * inv
        cos, sin = jnp.cos(ang), jnp.sin(ang)
        c_ref[...] = jnp.where(lane < C_ROT_DIM, cos, 1.0)
        a_ref[...] = jnp.where(lane < C_ROT_HALF, -sin, 0.0)
        b_ref[...] = jnp.where(jnp.logical_and(lane >= C_ROT_HALF, lane < C_ROT_DIM), sin, 0.0)

    tab = pl.BlockSpec((ROPE_ROWS, C_PAIR), lambda i: (i, 0))
    return pl.pallas_call(
        body, name=name, grid=(t // ROPE_ROWS,),
        in_specs=[pl.BlockSpec((ROPE_ROWS, 1), lambda i: (i, 0))],
        out_specs=[tab, tab, tab],
        out_shape=[jax.ShapeDtypeStruct((t, C_PAIR), F32)] * 3,
        compiler_params=_params(("arbitrary",)),
    )(pos_col)


def _rope(x, c, a, b):
    return x * c + pltpu.roll(x, C_PAIR - C_ROT_HALF, 1) * a + pltpu.roll(x, C_ROT_HALF, 1) * b


def _rope_t(d, c, a, b):
    return d * c + pltpu.roll(d * a, C_ROT_HALF, 1) + pltpu.roll(d * b, C_PAIR - C_ROT_HALF, 1)


C_RES = 16


def _residue_major(a, batch):
    return a.reshape(batch, SEQ // C_RES, C_RES, -1).transpose(0, 2, 1, 3).reshape(a.shape)


def _sequence_order(a, batch):
    return a.reshape(batch, C_RES, SEQ // C_RES, -1).transpose(0, 2, 1, 3).reshape(a.shape)


def _block_pieces(idx, dil):
    nblk = SEQ // dil // C_BLOCK
    r, n = idx // nblk, idx % nblk
    per = C_RES // dil
    size = C_BLOCK // per

    def pieces(blk):
        return [((dil * a + r) * (SEQ // C_RES) + size * blk, size) for a in range(per)]

    return pieces(n), pieces(jnp.maximum(n - 1, 0)), n > 0


def _get_rows(ref, pieces):
    return jnp.concatenate([ref[pl.ds(pl.multiple_of(start, 8), size), :] for start, size in pieces], axis=0)


def _set_rows(ref, pieces, val, add=False):
    for k, (start, size) in enumerate(pieces):
        rows = pl.ds(pl.multiple_of(start, 8), size)
        part = val[k * size:(k + 1) * size]
        ref[rows, :] = ref[rows, :] + part if add else part


def _head_masks():
    low = lax.broadcasted_iota(jnp.int32, (1, C_PAIR), 1) < C_HEAD_DIM
    return low, jnp.logical_not(low)


def _attn_mask(has_prev, dil):
    per = C_RES // dil
    size = C_BLOCK // per

    def position(x):
        x = jnp.bitwise_and(x, C_BLOCK - 1)
        return per * jnp.bitwise_and(x, size - 1) + x // size

    j = lax.broadcasted_iota(jnp.int32, (2 * C_BLOCK, 2 * C_BLOCK), 1)
    pi = position(lax.broadcasted_iota(jnp.int32, (2 * C_BLOCK, 2 * C_BLOCK), 0))
    pj = position(j)
    own = j < C_BLOCK
    return jnp.logical_or(jnp.logical_and(own, pj <= pi),
                          jnp.logical_and(jnp.logical_and(jnp.logical_not(own), pj >= pi), has_prev))


def _stack_heads(x):
    low, high = _head_masks()
    return jnp.concatenate([jnp.where(low, x, 0.0), jnp.where(high, x, 0.0)], axis=0)


def _unstack_heads(x):
    low, _ = _head_masks()
    return jnp.where(low, x[:C_BLOCK], x[C_BLOCK:])


def attn_fwd(qkv, cos_t, sin_a, sin_b, batch, name, exchange=None):
    t = qkv.shape[0]
    nbr = len(C_DILATIONS)

    def body(q_ref, k_ref, v_ref, c_ref, a_ref, b_ref, o_ref, l_ref, qr_ref, kr_ref, qs, ks, *stats):
        acc, mm, dd = stats[0:nbr], stats[nbr:2 * nbr], stats[2 * nbr:3 * nbr]
        c, a, b = c_ref[...], a_ref[...], b_ref[...]
        qs[...] = _rope(q_ref[...], c, a, b) * C_SCALE
        ks[...] = _rope(k_ref[...], c, a, b)
        qr_ref[...] = qs[...].astype(BF16)
        kr_ref[...] = ks[...].astype(BF16)

        def load(idx, dil):
            own, prev, has_prev = _block_pieces(idx, dil)
            return own, (has_prev, _get_rows(qs, own), _get_rows(ks, own), _get_rows(ks, prev),
                         _get_rows(v_ref, own), _get_rows(v_ref, prev))

        def scores(dil, has_prev, q, k_own, k_prev, v_own, v_prev):
            k_cat = jnp.concatenate([k_own, k_prev], axis=0).astype(BF16)
            return jnp.where(_attn_mask(has_prev, dil), _dot_nt(_stack_heads(q).astype(BF16), k_cat), NEG_BIG)

        def softmax(s):
            m = jnp.max(s, axis=-1, keepdims=True)
            p = jnp.exp(s - m)
            return p.astype(BF16), m, jnp.sum(p, axis=-1, keepdims=True)

        def values(pb, has_prev, q, k_own, k_prev, v_own, v_prev):
            low, high = _head_masks()
            v_cat = jnp.concatenate([v_own, v_prev], axis=0)
            p_wide = jnp.concatenate([pb[:C_BLOCK], pb[C_BLOCK:]], axis=1)
            v_tall = jnp.concatenate([jnp.where(low, v_cat, 0.0), jnp.where(high, v_cat, 0.0)], axis=0).astype(BF16)
            return _dot(p_wide, v_tall)

        for bi, dil in enumerate(C_DILATIONS):
            def pair(i, carry, bi=bi, dil=dil):
                low, _ = _head_masks()
                loaded = [load(C_FWD_BLOCKS * i + k, dil) for k in range(C_FWD_BLOCKS)]
                ss = [scores(dil, *ops) for _, ops in loaded]
                sm = [softmax(s) for s in ss]
                pvs = [values(pb, *ops) for (pb, _, _), (_, ops) in zip(sm, loaded)]
                for (own, _), (_, m, den), pv in zip(loaded, sm, pvs):
                    _set_rows(acc[bi], own, pv)
                    _set_rows(mm[bi], own, jnp.where(low, m[:C_BLOCK], m[C_BLOCK:]))
                    _set_rows(dd[bi], own, jnp.where(low, den[:C_BLOCK], den[C_BLOCK:]))
                return carry

            lax.fori_loop(0, SEQ // C_BLOCK // C_FWD_BLOCKS, pair, 0)
        step = 2 * C_BLOCK
        for r0 in range(0, SEQ, step):
            rr = slice(r0, r0 + step)
            ms = [mm[g][rr, :] for g in range(nbr)]
            m_all = functools.reduce(jnp.maximum, ms)
            ws = [jnp.exp(m - m_all) for m in ms]
            num = sum(acc[g][rr, :] * ws[g] for g in range(nbr))
            den = sum(dd[g][rr, :] * ws[g] for g in range(nbr))
            o_ref[rr, :] = (num / den).astype(BF16)
            l_ref[rr, :] = m_all + jnp.log(den)

    def col(k):
        return pl.BlockSpec((SEQ, C_PAIR), lambda b, p: (b, k * C_PAIRS + p))

    tab = pl.BlockSpec((SEQ, C_PAIR), lambda b, p: (b, 0))
    return _call(
        body, name=name, grid=(batch, C_PAIRS),
        in_specs=[col(0), col(1), col(2), tab, tab, tab],
        out_specs=[col(0), col(0), col(0), col(0)],
        out_shape=[jax.ShapeDtypeStruct((t, D_MODEL), BF16), jax.ShapeDtypeStruct((t, D_MODEL), F32),
                   jax.ShapeDtypeStruct((t, D_MODEL), BF16), jax.ShapeDtypeStruct((t, D_MODEL), BF16)],
        scratch_shapes=[pltpu.VMEM((SEQ, C_PAIR), F32)] * (2 + 3 * nbr),
        args=(qkv, qkv, qkv, cos_t, sin_a, sin_b), exchange=exchange)


def attn_bwd(qr, kr, qkv, cos_t, sin_a, sin_b, o, lse, do, batch, name, exchange=None):
    t = qkv.shape[0]

    def body(q_ref, k_ref, v_ref, c_ref, a_ref, b_ref, o_ref, l_ref, do_ref, dqkv_ref, qs, ks, dqs, dks, dvs, dlt):
        low, _ = _head_masks()
        c, a, b = c_ref[...], a_ref[...], b_ref[...]
        qs[...] = q_ref[...].astype(F32)
        ks[...] = k_ref[...].astype(F32)
        prod = do_ref[...] * o_ref[...].astype(F32)
        s_low = jnp.sum(jnp.where(low, prod, 0.0), axis=-1, keepdims=True)
        s_all = jnp.sum(prod, axis=-1, keepdims=True)
        dlt[...] = jnp.where(low, s_low, s_all - s_low)
        dqs[...] = jnp.zeros_like(dqs)
        dks[...] = jnp.zeros_like(dks)
        dvs[...] = jnp.zeros_like(dvs)

        def load(idx, dil):
            own, prev, has_prev = _block_pieces(idx, dil)
            return (own, prev), (has_prev, _get_rows(qs, own), _get_rows(do_ref, own), _get_rows(ks, own),
                                 _get_rows(ks, prev), _get_rows(v_ref, own), _get_rows(v_ref, prev),
                                 _get_rows(l_ref, own), _get_rows(dlt, own))

        def operands(dil, has_prev, q, do, k_own, k_prev, v_own, v_prev, l_full, d_full):
            lcol = jnp.concatenate([l_full[:, 0:1], l_full[:, C_HEAD_DIM:C_HEAD_DIM + 1]], axis=0)
            dcol = jnp.concatenate([d_full[:, 0:1], d_full[:, C_HEAD_DIM:C_HEAD_DIM + 1]], axis=0)
            return (_stack_heads(q).astype(BF16), _stack_heads(do).astype(BF16),
                    jnp.concatenate([k_own, k_prev], axis=0).astype(BF16),
                    jnp.concatenate([v_own, v_prev], axis=0).astype(BF16), lcol, dcol, _attn_mask(has_prev, dil))

        for dil in C_DILATIONS:
            def pair(i, carry, dil=dil):
                loaded = [load(C_BWD_BLOCKS * i + k, dil) for k in range(C_BWD_BLOCKS)]
                ops = [operands(dil, *o) for _, o in loaded]
                ss = [_dot_nt(q_stack, k_cat) for q_stack, _, k_cat, _, _, _, _ in ops]
                dps = [_dot_nt(do_stack, v_cat) for _, do_stack, _, v_cat, _, _, _ in ops]
                ps = [jnp.exp(jnp.where(o[6], s, NEG_BIG) - o[4]) for s, o in zip(ss, ops)]
                dss = [(p * (dp - o[5])).astype(BF16) for p, dp, o in zip(ps, dps, ops)]
                dvs_ = [_dot_tn(p.astype(BF16), o[1]) for p, o in zip(ps, ops)]
                dks_ = [_dot_tn(ds, o[0]) for ds, o in zip(dss, ops)]
                dqs_ = [_unstack_heads(_dot(ds, o[2])) for ds, o in zip(dss, ops)]
                for ((own, prev), _), dq, dk_cat, dv_cat in zip(loaded, dqs_, dks_, dvs_):
                    _set_rows(dqs, own, dq, add=True)
                    _set_rows(dks, own, dk_cat[:C_BLOCK], add=True)
                    _set_rows(dvs, own, dv_cat[:C_BLOCK], add=True)
                    _set_rows(dks, prev, dk_cat[C_BLOCK:], add=True)
                    _set_rows(dvs, prev, dv_cat[C_BLOCK:], add=True)
                return carry

            lax.fori_loop(0, SEQ // C_BLOCK // C_BWD_BLOCKS, pair, 0)
        dqkv_ref[0] = _rope_t(dqs[...] * C_SCALE, c, a, b).astype(BF16)
        dqkv_ref[1] = _rope_t(dks[...], c, a, b).astype(BF16)
        dqkv_ref[2] = dvs[...].astype(BF16)

    def col(k):
        return pl.BlockSpec((SEQ, C_PAIR), lambda b, p: (b, k * C_PAIRS + p))

    tab = pl.BlockSpec((SEQ, C_PAIR), lambda b, p: (b, 0))
    return _call(
        body, name=name, grid=(batch, C_PAIRS),
        in_specs=[col(0), col(0), col(2), tab, tab, tab, col(0), col(0), col(0)],
        out_specs=[pl.BlockSpec((3, SEQ, C_PAIR), lambda b, p: (0, b, p))],
        out_shape=[jax.ShapeDtypeStruct((3, t, D_MODEL), BF16)],
        scratch_shapes=[pltpu.VMEM((SEQ, C_PAIR), F32)] * 6,
        args=(qr, kr, qkv, cos_t, sin_a, sin_b, o, lse, do), exchange=exchange)


def allreduce_small(slab, name):
    rows, lanes = slab.shape

    def body(x_ref, out_ref, gath, send_sems, recv_sems, local_sem):
        x, y, c, chips = _place()
        me, sibling = (x, y, c), (x, y, 1 - c)

        def slot(px, py, pc):
            return gath.at[4 * px + 2 * py + pc]

        def copy(k, block, to, src=None):
            return pltpu.make_async_remote_copy(
                src_ref=slot(*block) if src is None else src, dst_ref=slot(*block),
                send_sem=send_sems.at[k], recv_sem=recv_sems.at[k], device_id=to, device_id_type=MESH)

        mine = pltpu.make_async_copy(x_ref, slot(*me), local_sem)
        mine.start()
        first = [copy(0, me, sibling, src=x_ref)]
        first += [copy(1 + j, me, (*chip, c), src=x_ref) for j, chip in enumerate(chips)]
        for cp in first:
            cp.start()
        passed = [copy(4 + j, (*chip, c), sibling) for j, chip in enumerate(chips)]
        for j, chip in enumerate(chips):
            copy(1 + j, (*chip, c), me).wait_recv()
            passed[j].start()
        copy(0, sibling, me).wait_recv()
        for j, chip in enumerate(chips):
            copy(4 + j, (*chip, 1 - c), me).wait_recv()
        for cp in first + passed:
            cp.wait_send()
        mine.wait()
        total = gath[0]
        for d in range(1, N_DEV):
            total = total + gath[d]
        out_ref[...] = total

    return pl.pallas_call(
        body, name=name,
        in_specs=[pl.BlockSpec(memory_space=pltpu.VMEM)],
        out_specs=pl.BlockSpec(memory_space=pltpu.VMEM),
        out_shape=jax.ShapeDtypeStruct((rows, lanes), F32),
        scratch_shapes=[pltpu.VMEM((N_DEV, rows, lanes), F32),
                        pltpu.SemaphoreType.DMA((7,)), pltpu.SemaphoreType.DMA((7,)), pltpu.SemaphoreType.DMA],
    )(slab)


ELT_ROWS = 512


def reduce_slabs(r, name):
    r = r.reshape(N_CHIPS, -1, r.shape[-1])
    _, rows, cols = r.shape
    br = min(rows, ELT_ROWS)

    def body(r_ref, o_ref):
        o_ref[...] = ((r_ref[3].astype(F32) + r_ref[0].astype(F32)) + r_ref[1].astype(F32)) + r_ref[2].astype(F32)

    return pl.pallas_call(
        body, name=name, grid=(rows // br,),
        in_specs=[pl.BlockSpec((N_CHIPS, br, cols), lambda i: (0, i, 0))],
        out_specs=pl.BlockSpec((br, cols), lambda i: (i, 0)),
        out_shape=jax.ShapeDtypeStruct((rows, cols), F32),
        compiler_params=_params(("arbitrary",)),
    )(r)


def _adamw(w, g, m, v):
    m = ADAM_B1 * m + (1.0 - ADAM_B1) * g
    v = ADAM_B2 * v + (1.0 - ADAM_B2) * jnp.square(g)
    m_hat = m / (1.0 - ADAM_B1 ** ADAM_STEP)
    v_hat = v / (1.0 - ADAM_B2 ** ADAM_STEP)
    delta = -ADAM_LR * (m_hat / (jnp.sqrt(v_hat) + ADAM_EPS) + ADAM_WD * w)
    return delta, m, v


def adamw_big(w, s_mine, s_sibling, m, v, name):
    rows, cols = w.shape
    parts = len(s_mine)
    br = min(rows // parts, ELT_ROWS)
    nb = rows // parts // br

    def body(w_ref, m_ref, v_ref, *rest):
        sums, (g_out, d_out, m_out, v_out) = rest[:2 * parts], rest[2 * parts:]
        p = pl.program_id(0)
        g = sums[0][...] + sums[parts][...]
        for k in range(1, parts):
            g = jnp.where(p == k, sums[k][...] + sums[parts + k][...], g)
        g_out[...] = g
        d_out[...], m_out[...], v_out[...] = _adamw(w_ref[...], g, m_ref[...], v_ref[...])

    def part_spec(k):
        return pl.BlockSpec((br, cols), lambda p, i: (jnp.where(p == k, i, jnp.where(p < k, 0, nb - 1)), 0))

    blk = pl.BlockSpec((br, cols), lambda p, i: (p * nb + i, 0))
    out = jax.ShapeDtypeStruct((rows, cols), F32)
    return pl.pallas_call(
        body, name=name, grid=(parts, nb),
        in_specs=[blk] * 3 + [part_spec(k) for k in range(parts)] * 2, out_specs=[blk] * 4, out_shape=[out] * 4,
        compiler_params=_params(("arbitrary", "arbitrary")),
    )(w, m, v, *s_mine, *s_sibling)


def adamw_small(ws, gs, ms, vs, name):
    n = len(ws)

    def body(*refs):
        w_refs, g_refs, m_refs, v_refs = (refs[k * n:(k + 1) * n] for k in range(4))
        d_out, m_out, v_out = (refs[(4 + k) * n:(5 + k) * n] for k in range(3))
        for i in range(n):
            d_out[i][...], m_out[i][...], v_out[i][...] = _adamw(
                w_refs[i][...], g_refs[i][...], m_refs[i][...], v_refs[i][...])

    outs = [jax.ShapeDtypeStruct(w.shape, F32) for w in ws]
    res = pl.pallas_call(body, name=name, out_shape=outs * 3)(*ws, *gs, *ms, *vs)
    return res[:n], res[n:2 * n], res[2 * n:]


SLAB_LANES = 128
SLAB_ROW_ALIGN = 8


def _pack(parts):
    flat = jnp.concatenate([p.reshape(-1) for p in parts])
    rows = -(-flat.shape[0] // (SLAB_LANES * SLAB_ROW_ALIGN)) * SLAB_ROW_ALIGN
    flat = jnp.pad(flat, (0, rows * SLAB_LANES - flat.shape[0]))
    return flat.reshape(rows, SLAB_LANES)


def _unpack(slab, shapes):
    flat = slab.reshape(-1)
    out, pos = [], 0
    for s in shapes:
        size = math.prod(s)
        out.append(flat[pos:pos + size].reshape(s))
        pos += size
    return out


def kernel(x, positions, norm_mix_pre, norm_mix_post, norm_ffn_pre, norm_ffn_post, w_in_even, lb_table, a_norm, b_ln_g, b_ln_b, b_ws, b_bias, w_out_even, w_in_odd, w_out_odd, w_ff1, w_ff2, loss_target, m_norm_mix_pre, m_norm_mix_post, m_norm_ffn_pre, m_norm_ffn_post, m_w_in_even, m_lb_table, m_a_norm, m_b_ln_g, m_b_ln_b, m_b_ws, m_b_bias, m_w_out_even, m_w_in_odd, m_w_out_odd, m_w_ff1, m_w_ff2, v_norm_mix_pre, v_norm_mix_post, v_norm_ffn_pre, v_norm_ffn_post, v_w_in_even, v_lb_table, v_a_norm, v_b_ln_g, v_b_ln_b, v_b_ws, v_b_bias, v_w_out_even, v_w_in_odd, v_w_out_odd, v_w_ff1, v_w_ff2):
    batch = x.shape[0]
    t = batch * SEQ
    d = D_MODEL
    x0 = x.reshape(t, d)
    target = loss_target.reshape(t, d)

    def gain(p, layer):
        return p[layer:layer + 1]

    def gather(*shards):
        return _Exchange("gather", [w.astype(BF16) for w in shards])

    def scatter(*grads):
        return _Exchange("scatter", grads)

    (win_e,) = exchange_alone(gather(w_in_even[0]), "gather_in_even")
    bias_t = b_bias[0].T
    proj, h0, w1_0 = norm_matmul(x0, gain(norm_mix_pre, 0), win_e, "in_proj_even", exchange=gather(w_ff1[0]))
    oa, states, decays, w2_0 = hgrn2_fwd(proj, lb_table, a_norm, batch, "hgrn2_fwd", exchange=gather(w_ff2[0]))
    mixin, wout_e = gmlp_fwd(proj, oa, b_ln_g, b_ln_b, b_ws[0], bias_t, "gmlp_fwd", exchange=gather(w_out_even[0]))
    mix0, x1 = out_proj(mixin, wout_e, x0, gain(norm_mix_post, 0), "out_proj_even")
    x2, hf0, a0, y0, win_o, wout_o = ffn_fwd(x1, gain(norm_ffn_pre, 0), w1_0, w2_0, gain(norm_ffn_post, 0),
                                             "ffn_fwd_0", exchange=gather(w_in_odd[0], w_out_odd[0]))
    x2p = _residue_major(x2, batch)
    qkv, h1 = norm_matmul(x2p, gain(norm_mix_pre, 1), win_o, "in_proj_odd")
    cos_t, sin_a, sin_b = rope_tables(_residue_major(positions.reshape(t, 1), batch), "rope_tables")
    ao, lse, q_rot, k_rot, w1_1, w2_1 = attn_fwd(qkv, cos_t, sin_a, sin_b, batch, "attn_fwd",
                                                 exchange=gather(w_ff1[1], w_ff2[1]))
    mix1, x3 = out_proj(ao, wout_o, x2p, gain(norm_mix_post, 1), "out_proj_odd")
    dx4, hf1, a1, y1, loss_part = ffn_fwd(x3, gain(norm_ffn_pre, 1), w1_1, w2_1, gain(norm_ffn_post, 1),
                                          "ffn_fwd_1", target=_residue_major(target, batch))

    hc = D_FF // N_CHIPS
    dx3, dy1, da1, dg_fpre1, dg_fpost1 = ffn_bwd(
        dx4, x3, y1, a1, gain(norm_ffn_pre, 1), gain(norm_ffn_post, 1), w1_1, w2_1, "ffn_bwd_1")
    g_w1_1 = weight_grad(hf1, da1, "b", d, hc, False, "wgrad_ff1_1")
    g_w2_1 = weight_grad(a1, dy1, "a", hc, d, True, "wgrad_ff2_1")
    dmix1, dao, dg_mpost1 = out_proj_bwd(dx3, mix1, gain(norm_mix_post, 1), wout_o, "out_proj_bwd_odd")
    g_wout_o = weight_grad(ao, dmix1, "a", d // N_CHIPS, d, False, "wgrad_out_odd")
    dqkv, r_w1_1, r_w2_1, r_wout_o = attn_bwd(q_rot, k_rot, qkv, cos_t, sin_a, sin_b, ao, lse, dao, batch, "attn_bwd",
                                              exchange=scatter(g_w1_1, g_w2_1, g_wout_o))
    dx2p, dg_mpre1 = norm_matmul_bwd(dqkv, win_o, x2p, gain(norm_mix_pre, 1), dx3, "in_proj_bwd_odd")
    dx2 = _sequence_order(dx2p, batch)
    g_win_o = weight_grad_stacked(h1, dqkv, 3 * d // N_CHIPS, "wgrad_in_odd")
    s_w1_1, s_w2_1, s_wout_o = (reduce_slabs(r, n) for r, n in (
        (r_w1_1, "reduce_ff1_1"), (r_w2_1, "reduce_ff2_1"), (r_wout_o, "reduce_out_odd")))
    dx1, dy0, da0, dg_fpre0, dg_fpost0, r_win_o, t_w1_1, t_w2_1, t_wout_o = ffn_bwd(
        dx2, x1, y0, a0, gain(norm_ffn_pre, 0), gain(norm_ffn_post, 0), w1_0, w2_0, "ffn_bwd_0",
        exchange=_Both(scatter(g_win_o), _Swap([s_w1_1, s_w2_1, s_wout_o])))
    g_w1_0 = weight_grad(hf0, da0, "b", d, hc, False, "wgrad_ff1_0")
    g_w2_0, r_w1_0a = weight_grad(a0, dy0, "a", hc, d, True, "wgrad_ff2_0", exchange=scatter(g_w1_0[:, :d // 2]))
    dmix0, dmixin, dg_mpost0 = out_proj_bwd(dx1, mix0, gain(norm_mix_post, 0), wout_e, "out_proj_bwd_even")
    g_wout_e = weight_grad(mixin, dmix0, "a", d // N_CHIPS, d, False, "wgrad_out_even")
    s_win_o = reduce_slabs(r_win_o, "reduce_in_odd")
    dproj, d_lb, d_anorm, r_w1_0b, r_w2_0a, t_win_o = hgrn2_bwd(
        proj, states, decays, lb_table, a_norm, dmixin, batch, "hgrn2_bwd",
        exchange=_Both(scatter(g_w1_0[:, d // 2:], g_w2_0[:, :hc // 2]), _Swap([s_win_o])))
    s_w1_0 = jnp.concatenate([reduce_slabs(r_w1_0a, "reduce_ff1_0a"), reduce_slabs(r_w1_0b, "reduce_ff1_0b")])
    dproj, d_lng, d_lnb, d_ws, d_bias_t, r_w2_0b, t_w1_0 = gmlp_bwd(
        proj, dmixin, b_ln_g, b_ln_b, b_ws[0], bias_t, dproj, "gmlp_bwd",
        exchange=_Both(scatter(g_w2_0[:, hc // 2:]), _Swap([s_w1_0])))
    s_w2_0 = jnp.concatenate([reduce_slabs(r_w2_0a, "reduce_ff2_0a"), reduce_slabs(r_w2_0b, "reduce_ff2_0b")])
    g_win_e, r_wout_e, t_w2_0 = weight_grad(h0, dproj, "b", d, 3 * d // N_CHIPS, False, "wgrad_in_even",
                                            exchange=_Both(scatter(g_wout_e), _Swap([s_w2_0])))
    s_wout_e = reduce_slabs(r_wout_e, "reduce_out_even")
    dx0, dg_mpre0, r_win_e, t_wout_e = norm_matmul_bwd(
        dproj, win_e, x0, gain(norm_mix_pre, 0), dx1, "in_proj_bwd_even",
        exchange=_Both(scatter(g_win_e), _Swap([s_wout_e])))
    grad_x = dx0.reshape(x.shape)
    s_win_e = reduce_slabs(r_win_e, "reduce_in_even")
    (t_win_e,) = exchange_alone(_Swap([s_win_e]), "sibling_swap")

    big_w = [w_in_even, w_out_even, w_in_odd, w_out_odd, w_ff1, w_ff2]
    big_m = [m_w_in_even, m_w_out_even, m_w_in_odd, m_w_out_odd, m_w_ff1, m_w_ff2]
    big_v = [v_w_in_even, v_w_out_even, v_w_in_odd, v_w_out_odd, v_w_ff1, v_w_ff2]
    mine = [[s_win_e], [s_wout_e], [s_win_o], [s_wout_o], [s_w1_0, s_w1_1], [s_w2_0, s_w2_1]]
    theirs = [[t_win_e], [t_wout_e], [t_win_o], [t_wout_o], [t_w1_0, t_w1_1], [t_w2_0, t_w2_1]]
    big = []
    for i, (w, m, v) in enumerate(zip(big_w, big_m, big_v)):
        two_d = (-1, w.shape[-1])
        res = adamw_big(w.reshape(two_d), mine[i], theirs[i], m.reshape(two_d), v.reshape(two_d), "adamw_big_%d" % i)
        big.append([r.reshape(w.shape) for r in res])

    small_w = [norm_mix_pre, norm_mix_post, norm_ffn_pre, norm_ffn_post, lb_table, a_norm, b_ln_g, b_ln_b, b_ws, b_bias]
    small_m = [m_norm_mix_pre, m_norm_mix_post, m_norm_ffn_pre, m_norm_ffn_post, m_lb_table, m_a_norm, m_b_ln_g,
               m_b_ln_b, m_b_ws, m_b_bias]
    small_v = [v_norm_mix_pre, v_norm_mix_post, v_norm_ffn_pre, v_norm_ffn_post, v_lb_table, v_a_norm, v_b_ln_g,
               v_b_ln_b, v_b_ws, v_b_bias]
    partial = [jnp.concatenate([dg_mpre0, dg_mpre1]), jnp.concatenate([dg_mpost0, dg_mpost1]),
               jnp.concatenate([dg_fpre0, dg_fpre1]), jnp.concatenate([dg_fpost0, dg_fpost1]),
               d_lb, d_anorm, d_lng, d_lnb, d_ws[None], d_bias_t.T[None]]
    *small_g, loss = _unpack(allreduce_small(_pack(partial + [loss_part]), "allreduce_small"),
                             [w.shape for w in small_w] + [()])
    small_d, small_nm, small_nv = adamw_small(small_w, small_g, small_m, small_v, "adamw_small")

    order = ["norm_mix_pre", "norm_mix_post", "norm_ffn_pre", "norm_ffn_post", "w_in_even", "lb_table", "a_norm",
             "b_ln_g", "b_ln_b", "b_ws", "b_bias", "w_out_even", "w_in_odd", "w_out_odd", "w_ff1", "w_ff2"]
    small_names = ["norm_mix_pre", "norm_mix_post", "norm_ffn_pre", "norm_ffn_post", "lb_table", "a_norm",
                   "b_ln_g", "b_ln_b", "b_ws", "b_bias"]
    big_names = ["w_in_even", "w_out_even", "w_in_odd", "w_out_odd", "w_ff1", "w_ff2"]
    grads, deltas, new_m, new_v = {}, {}, {}, {}
    for i, nm in enumerate(small_names):
        grads[nm], deltas[nm], new_m[nm], new_v[nm] = small_g[i], small_d[i], small_nm[i], small_nv[i]
    for i, nm in enumerate(big_names):
        grads[nm], deltas[nm], new_m[nm], new_v[nm] = big[i]
    return (loss, grad_x, *[grads[n] for n in order], *[deltas[n] for n in order],
            *[new_m[n] for n in order], *[new_v[n] for n in order])
```

```python
import functools
import math

import jax
import jax.numpy as jnp
from jax import lax
from jax.experimental import pallas as pl
from jax.experimental.pallas import tpu as pltpu

F32 = jnp.float32
BF16 = jnp.bfloat16
MESH = pl.DeviceIdType.MESH

D_MODEL = 1024
SEQ = 2048
D_FF = 4096
N_CHIPS = 4
A_WIDTH = 512
A_HEADS = 4
A_DK = 128
A_CHUNK = 64
A_SUB = 16
B_WIDTH = 512
B_GROUPS = 4
B_CHUNK = 128
C_HEADS = 16
C_HEAD_DIM = 64
C_ROT_HALF = 8
C_BLOCK = 128
C_DILATIONS = (1, 4, 16)
ROPE_THETA = 500000.0
EPS = 1e-6
ADAM_LR = 0.001
ADAM_B1 = 0.9
ADAM_B2 = 0.999
ADAM_EPS = 1e-08
ADAM_WD = 0.01
ADAM_STEP = 10

ROW_TILE = 512
FFN_ROWS = 1024
WGRAD_ROWS = 2048
VMEM_LIMIT = 56 * 1024 * 1024
NEG_BIG = -1e30


def _params(sem=None):
    return pltpu.CompilerParams(dimension_semantics=sem, vmem_limit_bytes=VMEM_LIMIT)


def _dot(a, b):
    return jnp.dot(a, b, preferred_element_type=F32)


def _dot_nt(a, b):
    return lax.dot_general(a, b, (((1,), (1,)), ((), ())), preferred_element_type=F32)


def _dot_tn(a, b):
    return lax.dot_general(a, b, (((0,), (0,)), ((), ())), preferred_element_type=F32)


def _rms(x, g):
    r = lax.rsqrt(jnp.mean(x * x, axis=-1, keepdims=True) + EPS)
    return x * r * g


def _rms_bwd(x, g, dy):
    r = lax.rsqrt(jnp.mean(x * x, axis=-1, keepdims=True) + EPS)
    xh = x * r
    dg = jnp.sum(dy * xh, axis=0, keepdims=True)
    dxh = dy * g
    dx = r * (dxh - xh * jnp.mean(dxh * xh, axis=-1, keepdims=True))
    return dx, dg


def _accumulate(ref, val, first):
    @pl.when(first)
    def _():
        ref[...] = val

    @pl.when(jnp.logical_not(first))
    def _():
        ref[...] += val


N_DEV = 8
ANY = pl.BlockSpec(memory_space=pl.ANY)


def _place():
    x, y, c = lax.axis_index("x"), lax.axis_index("y"), lax.axis_index("c")
    return x, y, c, [(1 - x, y), (x, 1 - y), (1 - x, 1 - y)]


class _Exchange:
    def __init__(self, kind, arrays):
        self.kind, self.arrays, self.n = kind, list(arrays), len(arrays)
        per_peer = pltpu.SemaphoreType.DMA((3 * self.n,))
        if kind == "gather":
            self.out_shape = [jax.ShapeDtypeStruct((N_CHIPS,) + a.shape, a.dtype) for a in self.arrays]
            self.scratch = [per_peer, per_peer, pltpu.SemaphoreType.DMA((self.n,)), per_peer, per_peer]
        else:
            self.out_shape = [jax.ShapeDtypeStruct(a.shape, a.dtype) for a in self.arrays]
            self.scratch = [per_peer, per_peer, pltpu.SemaphoreType.DMA((self.n,))]

    def _copies(self, ins, outs, sems):
        send_sems, recv_sems, local_sems = sems[:3]
        x, y, c, chips = _place()
        me = 2 * x + y
        local, remote = [], []
        for a in range(self.n):
            if self.kind == "gather":
                local.append(pltpu.make_async_copy(ins[a], outs[a].at[me], local_sems.at[a]))
                half = self.arrays[a].shape[0] // 2

                def rows(ref, core, half=half):
                    return ref.at[pl.ds(core * half, half)]
            else:
                local.append(pltpu.make_async_copy(ins[a].at[me], outs[a].at[3], local_sems.at[a]))
            for j, (px, py) in enumerate(chips):
                k = 3 * a + j
                peer = 2 * px + py

                def copy(src, dst, to, send_sem=send_sems.at[k], recv_sem=recv_sems.at[k]):
                    return pltpu.make_async_remote_copy(src_ref=src, dst_ref=dst, send_sem=send_sem, recv_sem=recv_sem,
                                                        device_id=to, device_id_type=MESH)

                if self.kind == "gather":
                    sent = copy(rows(ins[a], c), rows(outs[a].at[me], c), (px, py, c))
                    landed = copy(rows(ins[a], c), rows(outs[a].at[peer], c), (px, py, c))
                    on = dict(send_sem=sems[3].at[k], recv_sem=sems[4].at[k])
                    passed = copy(rows(outs[a].at[peer], c), rows(outs[a].at[peer], c), (x, y, 1 - c), **on)
                    handed = copy(rows(outs[a].at[peer], c), rows(outs[a].at[peer], 1 - c), (x, y, 1 - c), **on)
                    remote.append((sent, landed, passed, handed))
                else:
                    sent = copy(ins[a].at[peer], outs[a].at[j], (px, py, c))
                    remote.append((sent, sent, None, None))
        return local, remote

    def start(self, ins, outs, sems):
        local, remote = self._copies(ins, outs, sems)
        for cp in local:
            cp.start()
        for sent, _, _, _ in remote:
            sent.start()

    def finish(self, ins, outs, sems):
        local, remote = self._copies(ins, outs, sems)
        for _, landed, passed, _ in remote:
            landed.wait_recv()
            if passed is not None:
                passed.start()
        for sent, _, passed, handed in remote:
            if passed is not None:
                handed.wait_recv()
                passed.wait_send()
            sent.wait_send()
        for cp in local:
            cp.wait()


class _Swap:
    def __init__(self, arrays):
        self.arrays, self.n = list(arrays), len(arrays)
        self.out_shape = [jax.ShapeDtypeStruct(a.shape, a.dtype) for a in self.arrays]
        self.scratch = [pltpu.SemaphoreType.DMA((self.n,)), pltpu.SemaphoreType.DMA((self.n,))]

    def _copies(self, ins, outs, sems):
        x, y, c, _ = _place()
        return [pltpu.make_async_remote_copy(src_ref=ins[a], dst_ref=outs[a], send_sem=sems[0].at[a],
                                             recv_sem=sems[1].at[a], device_id=(x, y, 1 - c), device_id_type=MESH)
                for a in range(self.n)]

    def start(self, ins, outs, sems):
        for cp in self._copies(ins, outs, sems):
            cp.start()

    def finish(self, ins, outs, sems):
        for cp in self._copies(ins, outs, sems):
            cp.wait_recv()
            cp.wait_send()


class _Both:
    def __init__(self, first, second):
        self.parts = (first, second)
        self.arrays, self.n = first.arrays + second.arrays, first.n + second.n
        self.out_shape = first.out_shape + second.out_shape
        self.scratch = first.scratch + second.scratch

    def _split(self, ins, outs, sems):
        a, b = self.parts
        return ((a, ins[:a.n], outs[:a.n], sems[:len(a.scratch)]),
                (b, ins[a.n:], outs[a.n:], sems[len(a.scratch):]))

    def start(self, ins, outs, sems):
        for ex, i, o, s in self._split(ins, outs, sems):
            ex.start(i, o, s)

    def finish(self, ins, outs, sems):
        for ex, i, o, s in self._split(ins, outs, sems):
            ex.finish(i, o, s)


def _call(body, *, name, grid, in_specs, out_specs, out_shape, args, scratch_shapes=(), aliases=None, exchange=None):
    if exchange is None:
        return pl.pallas_call(
            body, name=name, grid=grid, in_specs=in_specs, out_specs=out_specs, out_shape=out_shape,
            scratch_shapes=list(scratch_shapes), input_output_aliases=aliases or {},
            compiler_params=_params(("arbitrary",) * len(grid)))(*args)
    n_in, n_out, n_scr, n_ex = len(in_specs), len(out_specs), len(scratch_shapes), exchange.n
    steps = grid

    def wrapped(*refs):
        ins, refs = refs[:n_in], refs[n_in:]
        ex_in, refs = refs[:n_ex], refs[n_ex:]
        outs, refs = refs[:n_out], refs[n_out:]
        ex_out, refs = refs[:n_ex], refs[n_ex:]
        scr, sems = refs[:n_scr], refs[n_scr:]
        first = functools.reduce(jnp.logical_and, [pl.program_id(k) == 0 for k in range(len(steps))])
        last = functools.reduce(jnp.logical_and, [pl.program_id(k) == steps[k] - 1 for k in range(len(steps))])

        @pl.when(first)
        def _():
            exchange.start(ex_in, ex_out, sems)

        body(*ins, *outs, *scr)

        @pl.when(last)
        def _():
            exchange.finish(ex_in, ex_out, sems)

    return pl.pallas_call(
        wrapped, name=name, grid=grid,
        in_specs=list(in_specs) + [ANY] * n_ex, out_specs=list(out_specs) + [ANY] * n_ex,
        out_shape=list(out_shape) + exchange.out_shape,
        scratch_shapes=list(scratch_shapes) + exchange.scratch, input_output_aliases=aliases or {},
        compiler_params=_params(("arbitrary",) * len(grid)))(*args, *exchange.arrays)


def exchange_alone(exchange, name):
    def body(*refs):
        n = exchange.n
        exchange.start(refs[:n], refs[n:2 * n], refs[2 * n:])
        exchange.finish(refs[:n], refs[n:2 * n], refs[2 * n:])

    return pl.pallas_call(
        body, name=name, in_specs=[ANY] * exchange.n, out_specs=[ANY] * exchange.n,
        out_shape=exchange.out_shape, scratch_shapes=exchange.scratch)(*exchange.arrays)


def norm_matmul(x, g, wg, name, exchange=None):
    t, d = x.shape
    nl = wg.shape[2]

    def body(x_ref, g_ref, w_ref, o_ref, h_ref):
        h = _rms(x_ref[...], g_ref[...]).astype(BF16)
        h_ref[...] = h
        for c in range(N_CHIPS):
            o_ref[:, c * nl:(c + 1) * nl] = _dot(h, w_ref[c])

    return _call(
        body, name=name, grid=(t // ROW_TILE,),
        in_specs=[pl.BlockSpec((ROW_TILE, d), lambda i: (i, 0)),
                  pl.BlockSpec((1, d), lambda i: (0, 0)),
                  pl.BlockSpec((N_CHIPS, d, nl), lambda i: (0, 0, 0))],
        out_specs=[pl.BlockSpec((ROW_TILE, N_CHIPS * nl), lambda i: (i, 0)),
                   pl.BlockSpec((ROW_TILE, d), lambda i: (i, 0))],
        out_shape=[jax.ShapeDtypeStruct((t, N_CHIPS * nl), F32), jax.ShapeDtypeStruct((t, d), BF16)],
        args=(x, g, wg), exchange=exchange)


def norm_matmul_bwd(dproj, wg, x, g, dres, name, exchange=None):
    t, d = x.shape
    nl = wg.shape[2]
    stacked = dproj.ndim == 3
    piece = math.gcd(nl, dproj.shape[-1])

    def body(dp_ref, w_ref, x_ref, g_ref, dres_ref, dx_ref, dg_ref):
        dh = None
        for j in range(N_CHIPS * nl // piece):
            c, off = divmod(j * piece, nl)
            if stacked:
                p, lo = divmod(j * piece, dproj.shape[-1])
                lhs = dp_ref[p, :, lo:lo + piece]
            else:
                lhs = dp_ref[:, j * piece:(j + 1) * piece]
            part = _dot_nt(lhs.astype(BF16), w_ref[c, :, off:off + piece])
            dh = part if dh is None else dh + part
        dx, dg = _rms_bwd(x_ref[...], g_ref[...], dh)
        dx_ref[...] = dres_ref[...] + dx
        _accumulate(dg_ref, dg, pl.program_id(0) == 0)

    row = pl.BlockSpec((ROW_TILE, d), lambda i: (i, 0))
    vec = pl.BlockSpec((1, d), lambda i: (0, 0))
    if stacked:
        dp_spec = pl.BlockSpec((dproj.shape[0], ROW_TILE, dproj.shape[-1]), lambda i: (0, i, 0))
    else:
        dp_spec = pl.BlockSpec((ROW_TILE, N_CHIPS * nl), lambda i: (i, 0))
    return _call(
        body, name=name, grid=(t // ROW_TILE,),
        in_specs=[dp_spec, pl.BlockSpec((N_CHIPS, d, nl), lambda i: (0, 0, 0)), row, vec, row],
        out_specs=[row, vec],
        out_shape=[jax.ShapeDtypeStruct((t, d), F32), jax.ShapeDtypeStruct((1, d), F32)],
        args=(dproj, wg, x, g, dres), exchange=exchange)


def out_proj(a, wg, x, g, name):
    t, d = x.shape
    kl = wg.shape[1]

    def body(a_ref, w_ref, x_ref, g_ref, mix_ref, xo_ref):
        acc = _dot(a_ref[:, 0:kl], w_ref[0])
        for c in range(1, N_CHIPS):
            acc += _dot(a_ref[:, c * kl:(c + 1) * kl], w_ref[c])
        mix_ref[...] = acc
        xo_ref[...] = x_ref[...] + _rms(acc, g_ref[...])

    row = pl.BlockSpec((ROW_TILE, d), lambda i: (i, 0))
    return pl.pallas_call(
        body, name=name, grid=(t // ROW_TILE,),
        in_specs=[row, pl.BlockSpec((N_CHIPS, kl, d), lambda i: (0, 0, 0)), row,
                  pl.BlockSpec((1, d), lambda i: (0, 0))],
        out_specs=[row, row],
        out_shape=[jax.ShapeDtypeStruct((t, d), F32), jax.ShapeDtypeStruct((t, d), F32)],
        compiler_params=_params(("arbitrary",)),
    )(a, wg, x, g)


def out_proj_bwd(dxo, mix, g, wg, name):
    t, d = mix.shape
    kl = wg.shape[1]

    def body(dxo_ref, mix_ref, g_ref, w_ref, dmix_ref, da_ref, dg_ref):
        dmix, dg = _rms_bwd(mix_ref[...], g_ref[...], dxo_ref[...])
        dmb = dmix.astype(BF16)
        dmix_ref[...] = dmb
        for c in range(N_CHIPS):
            da_ref[:, c * kl:(c + 1) * kl] = _dot_nt(dmb, w_ref[c])
        _accumulate(dg_ref, dg, pl.program_id(0) == 0)

    row = pl.BlockSpec((ROW_TILE, d), lambda i: (i, 0))
    vec = pl.BlockSpec((1, d), lambda i: (0, 0))
    return pl.pallas_call(
        body, name=name, grid=(t // ROW_TILE,),
        in_specs=[row, row, vec, pl.BlockSpec((N_CHIPS, kl, d), lambda i: (0, 0, 0))],
        out_specs=[row, row, vec],
        out_shape=[jax.ShapeDtypeStruct((t, d), BF16), jax.ShapeDtypeStruct((t, d), F32),
                   jax.ShapeDtypeStruct((1, d), F32)],
        compiler_params=_params(("arbitrary",)),
    )(dxo, mix, g, wg)


def ffn_fwd(x, gpre, w1g, w2g, gpost, name, exchange=None, target=None):
    t, d = x.shape
    hc = w1g.shape[2]
    with_loss = target is not None

    def body(x_ref, gpre_ref, w1_ref, w2_ref, gpost_ref, *rest):
        if with_loss:
            t_ref, xo_ref, h_ref, a_ref, y_ref, l_ref, acc = rest
        else:
            xo_ref, h_ref, a_ref, y_ref, acc = rest
        i, c = pl.program_id(0), pl.program_id(1)

        @pl.when(c == 0)
        def _():
            h_ref[...] = _rms(x_ref[...], gpre_ref[...]).astype(BF16)

        a = _dot(h_ref[...], w1_ref[...])
        a_ref[...] = a.astype(BF16)
        r = jnp.square(jnp.maximum(a, 0.0)).astype(BF16)
        _accumulate(acc, _dot(r, w2_ref[...]), c == 0)

        @pl.when(c == N_CHIPS - 1)
        def _():
            y = acc[...]
            y_ref[...] = y
            xo = x_ref[...] + _rms(y, gpost_ref[...])
            if with_loss:
                e = xo - t_ref[...]
                xo_ref[...] = e * (1.0 / d)
                part = jnp.sum(jnp.sum(e * e, axis=-1, keepdims=True), axis=0, keepdims=True) * (0.5 / d)
                _accumulate(l_ref, part, i == 0)
            else:
                xo_ref[...] = xo

    row = pl.BlockSpec((FFN_ROWS, d), lambda i, c: (i, 0))
    vec = pl.BlockSpec((1, d), lambda i, c: (0, 0))
    one = pl.BlockSpec((1, 1), lambda i, c: (0, 0))
    return _call(
        body, name=name, grid=(t // FFN_ROWS, N_CHIPS),
        in_specs=[row, vec,
                  pl.BlockSpec((None, d, hc), lambda i, c: (c, 0, 0)),
                  pl.BlockSpec((None, hc, d), lambda i, c: (c, 0, 0)), vec] + ([row] if with_loss else []),
        out_specs=[row, row, pl.BlockSpec((FFN_ROWS, hc), lambda i, c: (i, c)), row] + ([one] if with_loss else []),
        out_shape=[jax.ShapeDtypeStruct((t, d), F32), jax.ShapeDtypeStruct((t, d), BF16),
                   jax.ShapeDtypeStruct((t, N_CHIPS * hc), BF16), jax.ShapeDtypeStruct((t, d), F32)]
        + ([jax.ShapeDtypeStruct((1, 1), F32)] if with_loss else []),
        scratch_shapes=[pltpu.VMEM((FFN_ROWS, d), F32)],
        args=(x, gpre, w1g, w2g, gpost) + ((target,) if with_loss else ()), exchange=exchange)


def ffn_bwd(dxo, x, y, a, gpre, gpost, w1g, w2g, name, exchange=None):
    t, d = x.shape
    hc = w1g.shape[2]

    def body(dxo_ref, x_ref, y_ref, a_ref, gpre_ref, gpost_ref, w1_ref, w2_ref,
             dxi_ref, dy_ref, da_ref, dgpre_ref, dgpost_ref, acc):
        i, c = pl.program_id(0), pl.program_id(1)

        @pl.when(c == 0)
        def _():
            dy, dg = _rms_bwd(y_ref[...], gpost_ref[...], dxo_ref[...])
            dy_ref[...] = dy.astype(BF16)
            _accumulate(dgpost_ref, dg, i == 0)

        dr = _dot_nt(dy_ref[...], w2_ref[...])
        da = (dr * (2.0 * jnp.maximum(a_ref[...].astype(F32), 0.0))).astype(BF16)
        da_ref[...] = da
        _accumulate(acc, _dot_nt(da, w1_ref[...]), c == 0)

        @pl.when(c == N_CHIPS - 1)
        def _():
            dx, dg = _rms_bwd(x_ref[...], gpre_ref[...], acc[...])
            dxi_ref[...] = dxo_ref[...] + dx
            _accumulate(dgpre_ref, dg, i == 0)

    row = pl.BlockSpec((ROW_TILE, d), lambda i, c: (i, 0))
    vec = pl.BlockSpec((1, d), lambda i, c: (0, 0))
    hid = pl.BlockSpec((ROW_TILE, hc), lambda i, c: (i, c))
    return _call(
        body, name=name, grid=(t // ROW_TILE, N_CHIPS),
        in_specs=[row, row, row, hid, vec, vec,
                  pl.BlockSpec((None, d, hc), lambda i, c: (c, 0, 0)),
                  pl.BlockSpec((None, hc, d), lambda i, c: (c, 0, 0))],
        out_specs=[row, row, hid, vec, vec],
        out_shape=[jax.ShapeDtypeStruct((t, d), F32), jax.ShapeDtypeStruct((t, d), BF16),
                   jax.ShapeDtypeStruct((t, N_CHIPS * hc), BF16),
                   jax.ShapeDtypeStruct((1, d), F32), jax.ShapeDtypeStruct((1, d), F32)],
        scratch_shapes=[pltpu.VMEM((ROW_TILE, d), F32)],
        args=(dxo, x, y, a, gpre, gpost, w1g, w2g), exchange=exchange)


def weight_grad(a, b, chunked, bk, bn, relu2, name, exchange=None, parts=1):
    t = a.shape[0]
    a_on = chunked == "a"
    rows = min(t, WGRAD_ROWS)
    n_steps = t // rows
    part_rows = bk // parts

    def body(a_ref, b_ref, *rest):
        o_refs, acc = rest[:-1], rest[-1]
        s = pl.program_id(1)
        av = a_ref[...]
        if relu2:
            av = jnp.square(jnp.maximum(av.astype(F32), 0.0))
        _accumulate(acc, _dot_tn(av.astype(BF16), b_ref[...].astype(BF16)), s == 0)

        @pl.when(s == n_steps - 1)
        def _():
            for k, o_ref in enumerate(o_refs):
                o_ref[...] = acc[k * part_rows:(k + 1) * part_rows, :].astype(BF16)

    res = _call(
        body, name=name, grid=(N_CHIPS, n_steps),
        in_specs=[pl.BlockSpec((rows, bk), (lambda c, s: (s, c)) if a_on else (lambda c, s: (s, 0))),
                  pl.BlockSpec((rows, bn), (lambda c, s: (s, 0)) if a_on else (lambda c, s: (s, c)))],
        out_specs=[pl.BlockSpec((None, part_rows, bn), lambda c, s: (c, 0, 0))] * parts,
        out_shape=[jax.ShapeDtypeStruct((N_CHIPS, part_rows, bn), BF16)] * parts,
        scratch_shapes=[pltpu.VMEM((bk, bn), F32)],
        args=(a, b), exchange=exchange)
    return res[0] if exchange is None and parts == 1 else res


def weight_grad_stacked(a, b3, bn, name):
    t, bk = a.shape
    width = b3.shape[-1]
    piece = math.gcd(bn, width)
    rows = min(t, WGRAD_ROWS)
    n_steps = t // rows

    def body(a_ref, b_ref, o_hbm, acc, staged, sem):
        s, c = pl.program_id(0), pl.program_id(1)
        av = a_ref[...].astype(BF16)
        for chunk in range(N_CHIPS):
            @pl.when(c == chunk)
            def _(chunk=chunk):
                cols = [divmod(chunk * bn + k * piece, width) for k in range(bn // piece)]
                b = jnp.concatenate([b_ref[p, :, lo:lo + piece] for p, lo in cols], axis=1).astype(BF16)
                _accumulate(acc.at[chunk], _dot_tn(av, b), s == 0)

                @pl.when(s == n_steps - 1)
                def _():
                    staged[...] = acc[chunk].astype(BF16)
                    copy = pltpu.make_async_copy(staged, o_hbm.at[chunk], sem)
                    copy.start()
                    copy.wait()

    return pl.pallas_call(
        body, name=name, grid=(n_steps, N_CHIPS),
        in_specs=[pl.BlockSpec((rows, bk), lambda s, c: (s, 0)),
                  pl.BlockSpec((b3.shape[0], rows, width), lambda s, c: (0, s, 0))],
        out_specs=ANY,
        out_shape=jax.ShapeDtypeStruct((N_CHIPS, bk, bn), BF16),
        scratch_shapes=[pltpu.VMEM((N_CHIPS, bk, bn), F32), pltpu.VMEM((bk, bn), BF16), pltpu.SemaphoreType.DMA],
        compiler_params=_params(("arbitrary", "arbitrary")),
    )(a, b3)


def _hgrn2_chunk(st, qs, fls, ivs, gls, l0, l1, l2, ng):
    nsub = len(qs)
    mx = jnp.maximum(jnp.maximum(l0, l1), l2)
    e0, e1, e2 = jnp.exp(l0 - mx), jnp.exp(l1 - mx), jnp.exp(l2 - mx)
    lb = e0 / (e0 + e1 + e2)
    rows = lax.broadcasted_iota(jnp.int32, (A_SUB, A_SUB), 0)
    cols = lax.broadcasted_iota(jnp.int32, (A_SUB, A_SUB), 1)
    tri = (rows >= cols).astype(F32)
    keep = (lax.broadcasted_iota(jnp.int32, (A_SUB, A_SUB, A_DK), 0)
            >= lax.broadcasted_iota(jnp.int32, (A_SUB, A_SUB, A_DK), 1))
    base = jnp.zeros_like(l0)
    bases, gs, ks, qfs = [], [], [], []
    for i in range(nsub):
        f = lb + (1.0 - lb) * jax.nn.sigmoid(fls[i])
        logf = jnp.log(f)
        bases.append(base)
        gs.append(base + jnp.dot(tri, logf, precision=lax.Precision.HIGHEST, preferred_element_type=F32))
        base = base + jnp.sum(logf, axis=0, keepdims=True)
        ks.append(1.0 - f)
        qfs.append(jax.nn.silu(qs[i]))
    g_last = base
    stb = st.astype(BF16)
    outs = []
    for i in range(nsub):
        o = _dot_nt((qfs[i] * jnp.exp(gs[i])).astype(BF16), stb)
        if i > 0:
            qt = (qfs[i] * jnp.exp(gs[i] - bases[i])).astype(BF16)
            kk = jnp.concatenate([ks[j] * jnp.exp(bases[i] - gs[j]) for j in range(i)], axis=0).astype(BF16)
            vv = jnp.concatenate(ivs[:i], axis=0).astype(BF16)
            o = o + _dot(_dot_nt(qt, kk).astype(BF16), vv)
        dec = jnp.exp(jnp.where(keep, gs[i][:, None, :] - gs[i][None, :, :], NEG_BIG))
        s_diag = jnp.sum(qfs[i][:, None, :] * ks[i][None, :, :] * dec, axis=-1)
        o = o + _dot(s_diag.astype(BF16), ivs[i].astype(BF16))
        o = o * lax.rsqrt(jnp.mean(o * o, axis=-1, keepdims=True) + EPS) * ng
        outs.append(o * jax.nn.silu(gls[i]))
    kdec = jnp.concatenate([ks[j] * jnp.exp(g_last - gs[j]) for j in range(nsub)], axis=0).astype(BF16)
    vall = jnp.concatenate(ivs, axis=0).astype(BF16)
    new_st = st * jnp.exp(g_last) + _dot_tn(vall, kdec)
    return new_st, outs


A_MAX_LOG_DECAY = 60.0


def _half_sums(logf):
    n = logf.shape[0]
    first = lax.broadcasted_iota(jnp.int32, logf.shape, 0) < n // 2
    return (jnp.sum(jnp.where(first, logf, 0.0), axis=0, keepdims=True),
            jnp.sum(jnp.where(first, 0.0, logf), axis=0, keepdims=True))


def _split3(x):
    hi = x.astype(BF16)
    r1 = x - hi.astype(F32)
    mid = r1.astype(BF16)
    return hi, mid, (r1 - mid.astype(F32)).astype(BF16)


def _tri_matmul(x, transpose):
    n = x.shape[0]
    r = lax.broadcasted_iota(jnp.int32, (n, n), 0)
    c = lax.broadcasted_iota(jnp.int32, (n, n), 1)
    tri = ((r <= c) if transpose else (r >= c)).astype(BF16)
    hi, mid, lo = _split3(x)
    return (_dot(tri, lo) + _dot(tri, mid)) + _dot(tri, hi)


@jax.custom_vjp
def _cumsum_rows(x):
    return _tri_matmul(x, False)


def _cumsum_rows_fwd(x):
    return _tri_matmul(x, False), None


def _cumsum_rows_bwd(_, dy):
    return (_tri_matmul(dy, True),)


_cumsum_rows.defvjp(_cumsum_rows_fwd, _cumsum_rows_bwd)


def _lower_bound(l0, l1, l2):
    mx = jnp.maximum(jnp.maximum(l0, l1), l2)
    e0, e1, e2 = jnp.exp(l0 - mx), jnp.exp(l1 - mx), jnp.exp(l2 - mx)
    return e0 / (e0 + e1 + e2)


def _b(x):
    return x.astype(BF16)


@jax.custom_vjp
def _mm(a, b):
    return _dot(_b(a), _b(b))


_mm.defvjp(lambda a, b: (_mm(a, b), (a, b)),
           lambda res, d: (_dot_nt(_b(d), _b(res[1])), _dot_tn(_b(res[0]), _b(d))))


@jax.custom_vjp
def _mm_nt(a, b):
    return _dot_nt(_b(a), _b(b))


_mm_nt.defvjp(lambda a, b: (_mm_nt(a, b), (a, b)),
              lambda res, d: (_dot(_b(d), _b(res[1])), _dot_tn(_b(d), _b(res[0]))))


def _dot_split(dot, a, b):
    ah, bh = _b(a), _b(b)
    al, bl = _b(a - ah.astype(F32)), _b(b - bh.astype(F32))
    return (dot(ah, bl) + dot(al, bh)) + dot(ah, bh)


@jax.custom_vjp
def _mm_scores(a, b):
    return _dot_nt(_b(a), _b(b))


_mm_scores.defvjp(lambda a, b: (_mm_scores(a, b), (a, b)),
                  lambda res, d: (_dot_split(_dot, d, res[1]), _dot_split(_dot_tn, d, res[0])))


@jax.custom_vjp
def _mm_tn(a, b):
    return _dot_tn(_b(a), _b(b))


_mm_tn.defvjp(lambda a, b: (_mm_tn(a, b), (a, b)),
              lambda res, d: (_dot_nt(_b(res[1]), _b(d)), _dot(_b(res[0]), _b(d))))


@jax.custom_vjp
def _split_heads(x):
    return tuple(x[:, h * A_DK:(h + 1) * A_DK] for h in range(A_HEADS))


def _split_heads_fwd(x):
    return _split_heads(x), None


def _split_heads_bwd(_, parts):
    return (jnp.concatenate(parts, axis=1),)


_split_heads.defvjp(_split_heads_fwd, _split_heads_bwd)


def _hgrn2_chunk_fast(sts, q, fl, iv, gl, l0, l1, l2, ng):
    lb = _lower_bound(l0, l1, l2)
    f = lb + (1.0 - lb) * jax.nn.sigmoid(fl)
    return _hgrn2_fast_core(sts, q, f, jnp.log(f), iv, gl, ng)


def _hgrn2_fast_core(sts, q, f, logf, iv, gl, ng):
    g = _cumsum_rows(logf)
    g_mid, g_last = _half_sums(logf)
    g_last = g_mid + g_last
    k = 1.0 - f
    qf = jax.nn.silu(q)
    qms = _split_heads(qf * jnp.exp(g - g_mid))
    kms = _split_heads(k * jnp.exp(g_mid - g))
    qgs = _split_heads(qf * jnp.exp(g))
    kds = _split_heads(k * jnp.exp(g_last - g))
    ivs = _split_heads(iv)
    decays = _split_heads(jnp.exp(g_last))
    n = q.shape[0]
    causal = lax.broadcasted_iota(jnp.int32, (n, n), 0) >= lax.broadcasted_iota(jnp.int32, (n, n), 1)
    raw = [_mm_scores(qm, km) for qm, km in zip(qms, kms)]
    inter = [_mm_nt(qg, st) for qg, st in zip(qgs, sts)]
    scores = [jnp.where(causal, s, 0.0) for s in raw]
    os = [a + _mm(s, v) for a, s, v in zip(inter, scores, ivs)]
    new_sts = [st * d + _mm_tn(v, kd) for st, d, v, kd in zip(sts, decays, ivs, kds)]
    os = [o * lax.rsqrt(jnp.mean(o * o, axis=-1, keepdims=True) + EPS) for o in os]
    return new_sts, jnp.concatenate(os, axis=1) * ng * jax.nn.silu(gl)


A_STEP_CHUNKS = 4


def _chunk_rows(j):
    return pl.ds(pl.multiple_of(j * A_CHUNK, A_CHUNK), A_CHUNK)


def _sub_rows(j, i):
    return pl.ds(pl.multiple_of(j * A_CHUNK + i * A_SUB, A_SUB), A_SUB)


def _sub_blocks(ref, head, j):
    lanes = slice(head * A_DK, (head + 1) * A_DK)
    return [ref[_sub_rows(j, i), lanes] for i in range(A_CHUNK // A_SUB)]


def hgrn2_fwd(proj, lb_table, a_norm, batch, name, exchange=None):
    t = proj.shape[0]
    n_steps = t // batch // (A_CHUNK * A_STEP_CHUNKS)
    rows = A_CHUNK * A_STEP_CHUNKS

    def body(q_ref, f_ref, i_ref, g_ref, lb_ref, ng_ref, o_ref, st_ref, dec_ref, st):
        @pl.when(pl.program_id(1) == 0)
        def _():
            st[...] = jnp.zeros_like(st)

        def chunk(j, carry):
            r = _chunk_rows(j)
            st_ref[j] = st[...]
            lb = _lower_bound(lb_ref[0:1, :], lb_ref[1:2, :], lb_ref[2:3, :])
            f = lb + (1.0 - lb) * jax.nn.sigmoid(f_ref[r, :])
            logf = jnp.log(f)
            decay = jnp.minimum(*_half_sums(logf))
            dec_ref[j] = decay
            mild = jnp.min(decay) >= -A_MAX_LOG_DECAY

            @pl.when(mild)
            def _():
                new_sts, o = _hgrn2_fast_core([st[h] for h in range(A_HEADS)], q_ref[r, :], f, logf,
                                              i_ref[r, :], g_ref[r, :], ng_ref[...])
                for h in range(A_HEADS):
                    st[h] = new_sts[h]
                o_ref[r, :] = o.astype(BF16)

            @pl.when(jnp.logical_not(mild))
            def _():
                for h in range(A_HEADS):
                    lanes = slice(h * A_DK, (h + 1) * A_DK)
                    new_st, outs = _hgrn2_chunk(
                        st[h], _sub_blocks(q_ref, h, j), _sub_blocks(f_ref, h, j), _sub_blocks(i_ref, h, j),
                        _sub_blocks(g_ref, h, j), lb_ref[0:1, lanes], lb_ref[1:2, lanes], lb_ref[2:3, lanes],
                        ng_ref[:, lanes])
                    st[h] = new_st
                    for i, o in enumerate(outs):
                        o_ref[_sub_rows(j, i), lanes] = o.astype(BF16)

            return carry

        lax.fori_loop(0, A_STEP_CHUNKS, chunk, 0)

    def part(k):
        return pl.BlockSpec((rows, A_WIDTH), lambda b, n: (b * n_steps + n, k))

    return _call(
        body, name=name, grid=(batch, n_steps),
        in_specs=[part(0), part(1), part(2), part(3),
                  pl.BlockSpec((3, A_WIDTH), lambda b, n: (0, 0)), pl.BlockSpec((1, A_WIDTH), lambda b, n: (0, 0))],
        out_specs=[part(0),
                   pl.BlockSpec((A_STEP_CHUNKS, A_HEADS, A_DK, A_DK), lambda b, n: (b * n_steps + n, 0, 0, 0)),
                   pl.BlockSpec((A_STEP_CHUNKS, 1, A_WIDTH), lambda b, n: (b * n_steps + n, 0, 0))],
        out_shape=[jax.ShapeDtypeStruct((t, A_WIDTH), BF16),
                   jax.ShapeDtypeStruct((t // A_CHUNK, A_HEADS, A_DK, A_DK), F32),
                   jax.ShapeDtypeStruct((t // A_CHUNK, 1, A_WIDTH), F32)],
        scratch_shapes=[pltpu.VMEM((A_HEADS, A_DK, A_DK), F32)],
        args=(proj, proj, proj, proj, lb_table, a_norm), exchange=exchange)


def hgrn2_bwd(proj, states, decays, lb_table, a_norm, do, batch, name, exchange=None):
    t = proj.shape[0]
    n_steps = t // batch // (A_CHUNK * A_STEP_CHUNKS)
    rows = A_CHUNK * A_STEP_CHUNKS

    def body(q_ref, f_ref, i_ref, g_ref, st_ref, dec_ref, lb_ref, ng_ref, do_ref, dp_ref, dlb_ref, dng_ref, dst):
        @pl.when(jnp.logical_and(pl.program_id(0) == 0, pl.program_id(1) == 0))
        def _():
            dlb_ref[...] = jnp.zeros_like(dlb_ref)
            dng_ref[...] = jnp.zeros_like(dng_ref)

        @pl.when(pl.program_id(1) == 0)
        def _():
            dst[...] = jnp.zeros_like(dst)

        def chunk(jj, carry):
            j = A_STEP_CHUNKS - 1 - jj
            r = _chunk_rows(j)
            mild = jnp.min(dec_ref[j]) >= -A_MAX_LOG_DECAY

            @pl.when(mild)
            def _():
                _, vjp = jax.vjp(
                    _hgrn2_chunk_fast, [st_ref[j, h] for h in range(A_HEADS)], q_ref[r, :], f_ref[r, :],
                    i_ref[r, :], g_ref[r, :], lb_ref[0:1, :], lb_ref[1:2, :], lb_ref[2:3, :], ng_ref[...])
                d_sts, dq, df, di, dg, dl0, dl1, dl2, dng = vjp(
                    ([dst[h] for h in range(A_HEADS)], do_ref[r, :].astype(F32)))
                for h in range(A_HEADS):
                    dst[h] = d_sts[h]
                for k, part in enumerate((dq, df, di, dg)):
                    dp_ref[r, k * A_WIDTH:(k + 1) * A_WIDTH] = part
                for row, val in enumerate((dl0, dl1, dl2)):
                    dlb_ref[row:row + 1, :] += val
                dng_ref[...] += dng

            @pl.when(jnp.logical_not(mild))
            def _():
                for h in range(A_HEADS):
                    lanes = slice(h * A_DK, (h + 1) * A_DK)
                    _, vjp = jax.vjp(
                        _hgrn2_chunk, st_ref[j, h], _sub_blocks(q_ref, h, j), _sub_blocks(f_ref, h, j),
                        _sub_blocks(i_ref, h, j), _sub_blocks(g_ref, h, j), lb_ref[0:1, lanes], lb_ref[1:2, lanes],
                        lb_ref[2:3, lanes], ng_ref[:, lanes])
                    douts = [x.astype(F32) for x in _sub_blocks(do_ref, h, j)]
                    d_st, dqs, dfs, dis, dgs, dl0, dl1, dl2, dng = vjp((dst[h], douts))
                    dst[h] = d_st
                    for k, parts in enumerate((dqs, dfs, dis, dgs)):
                        for i in range(A_CHUNK // A_SUB):
                            dp_ref[_sub_rows(j, i), k * A_WIDTH + h * A_DK:k * A_WIDTH + (h + 1) * A_DK] = parts[i]
                    for row, val in enumerate((dl0, dl1, dl2)):
                        dlb_ref[row:row + 1, lanes] += val
                    dng_ref[:, lanes] += dng

            return carry

        lax.fori_loop(0, A_STEP_CHUNKS, chunk, 0)

    def rev(b, n):
        return b * n_steps + (n_steps - 1 - n)

    def part(k):
        return pl.BlockSpec((rows, A_WIDTH), lambda b, n: (rev(b, n), k))

    const3 = pl.BlockSpec((3, A_WIDTH), lambda b, n: (0, 0))
    const1 = pl.BlockSpec((1, A_WIDTH), lambda b, n: (0, 0))
    return _call(
        body, name=name, grid=(batch, n_steps),
        in_specs=[part(0), part(1), part(2), part(3),
                  pl.BlockSpec((A_STEP_CHUNKS, A_HEADS, A_DK, A_DK), lambda b, n: (rev(b, n), 0, 0, 0)),
                  pl.BlockSpec((A_STEP_CHUNKS, 1, A_WIDTH), lambda b, n: (rev(b, n), 0, 0)),
                  const3, const1, part(0)],
        out_specs=[pl.BlockSpec((rows, 4 * A_WIDTH), lambda b, n: (rev(b, n), 0)), const3, const1],
        out_shape=[jax.ShapeDtypeStruct((t, 4 * A_WIDTH + 2 * B_WIDTH), F32),
                   jax.ShapeDtypeStruct((3, A_WIDTH), F32), jax.ShapeDtypeStruct((1, A_WIDTH), F32)],
        scratch_shapes=[pltpu.VMEM((A_HEADS, A_DK, A_DK), F32)],
        args=(proj, proj, proj, proj, states, decays, lb_table, a_norm, do), exchange=exchange)


B_GDIM = B_WIDTH // B_GROUPS
B_ROWS = 512


def _gmlp_chunk(ubs, vbs, lngs, lnbs, ws, bcols):
    vs = [jax.nn.gelu(v) for v in vbs]
    mu = sum(jnp.sum(v, axis=-1, keepdims=True) for v in vs) * (1.0 / B_WIDTH)
    var = sum(jnp.sum(jnp.square(v - mu), axis=-1, keepdims=True) for v in vs) * (1.0 / B_WIDTH)
    rstd = lax.rsqrt(var + EPS)
    tril = (lax.broadcasted_iota(jnp.int32, (B_CHUNK, B_CHUNK), 0)
            >= lax.broadcasted_iota(jnp.int32, (B_CHUNK, B_CHUNK), 1))
    outs = []
    for g in range(B_GROUPS):
        vn = (vs[g] - mu) * rstd * lngs[g] + lnbs[g]
        w = jnp.where(tril, ws[g], 0.0).astype(BF16)
        outs.append(jax.nn.gelu(ubs[g]) * (_dot(w, vn.astype(BF16)) + bcols[g]))
    return outs


def _gmlp_args(u_ref, v_ref, lng_ref, lnb_ref, w_ref, bt_ref, rows):
    def groups(ref):
        return [ref[rows, g * B_GDIM:(g + 1) * B_GDIM] for g in range(B_GROUPS)]

    def vec(ref):
        return [ref[:, g * B_GDIM:(g + 1) * B_GDIM] for g in range(B_GROUPS)]

    return (groups(u_ref), groups(v_ref), vec(lng_ref), vec(lnb_ref),
            [w_ref[g] for g in range(B_GROUPS)], [bt_ref[:, g:g + 1] for g in range(B_GROUPS)])


def gmlp_fwd(proj, oa, ln_g, ln_b, w, bias_t, name, exchange=None):
    t = proj.shape[0]

    def body(u_ref, v_ref, oa_ref, lng_ref, lnb_ref, w_ref, bt_ref, o_ref):
        o_ref[:, 0:A_WIDTH] = oa_ref[...]
        for n in range(B_ROWS // B_CHUNK):
            rows = slice(n * B_CHUNK, (n + 1) * B_CHUNK)
            outs = _gmlp_chunk(*_gmlp_args(u_ref, v_ref, lng_ref, lnb_ref, w_ref, bt_ref, rows))
            for g, o in enumerate(outs):
                o_ref[rows, A_WIDTH + g * B_GDIM:A_WIDTH + (g + 1) * B_GDIM] = o.astype(BF16)

    vec = pl.BlockSpec((1, B_WIDTH), lambda i: (0, 0))
    return _call(
        body, name=name, grid=(t // B_ROWS,),
        in_specs=[pl.BlockSpec((B_ROWS, B_WIDTH), lambda i: (i, 4)), pl.BlockSpec((B_ROWS, B_WIDTH), lambda i: (i, 5)),
                  pl.BlockSpec((B_ROWS, A_WIDTH), lambda i: (i, 0)), vec, vec,
                  pl.BlockSpec((B_GROUPS, B_CHUNK, B_CHUNK), lambda i: (0, 0, 0)),
                  pl.BlockSpec((B_CHUNK, B_GROUPS), lambda i: (0, 0))],
        out_specs=[pl.BlockSpec((B_ROWS, A_WIDTH + B_WIDTH), lambda i: (i, 0))],
        out_shape=[jax.ShapeDtypeStruct((t, A_WIDTH + B_WIDTH), BF16)],
        args=(proj, proj, oa, ln_g, ln_b, w, bias_t), exchange=exchange)


def gmlp_bwd(proj, dmixin, ln_g, ln_b, w, bias_t, dproj, name, exchange=None):
    t = proj.shape[0]

    def body(u_ref, v_ref, do_ref, lng_ref, lnb_ref, w_ref, bt_ref, dp_in_ref,
             dp_ref, dlng_ref, dlnb_ref, dw_ref, dbt_ref):
        del dp_in_ref

        @pl.when(pl.program_id(0) == 0)
        def _():
            for ref in (dlng_ref, dlnb_ref, dw_ref, dbt_ref):
                ref[...] = jnp.zeros_like(ref)

        for n in range(B_ROWS // B_CHUNK):
            rows = slice(n * B_CHUNK, (n + 1) * B_CHUNK)
            _, vjp = jax.vjp(_gmlp_chunk, *_gmlp_args(u_ref, v_ref, lng_ref, lnb_ref, w_ref, bt_ref, rows))
            douts = [do_ref[rows, g * B_GDIM:(g + 1) * B_GDIM] for g in range(B_GROUPS)]
            dus, dvs, dlngs, dlnbs, dws, dbs = vjp(douts)
            for g in range(B_GROUPS):
                lanes = slice(g * B_GDIM, (g + 1) * B_GDIM)
                dp_ref[rows, lanes] = dus[g]
                dp_ref[rows, B_WIDTH + g * B_GDIM:B_WIDTH + (g + 1) * B_GDIM] = dvs[g]
                dlng_ref[:, lanes] += dlngs[g]
                dlnb_ref[:, lanes] += dlnbs[g]
                dw_ref[g] += dws[g]
                dbt_ref[:, g:g + 1] += dbs[g]

    vec = pl.BlockSpec((1, B_WIDTH), lambda i: (0, 0))
    wspec = pl.BlockSpec((B_GROUPS, B_CHUNK, B_CHUNK), lambda i: (0, 0, 0))
    bspec = pl.BlockSpec((B_CHUNK, B_GROUPS), lambda i: (0, 0))
    return _call(
        body, name=name, grid=(t // B_ROWS,),
        in_specs=[pl.BlockSpec((B_ROWS, B_WIDTH), lambda i: (i, 4)), pl.BlockSpec((B_ROWS, B_WIDTH), lambda i: (i, 5)),
                  pl.BlockSpec((B_ROWS, B_WIDTH), lambda i: (i, 1)), vec, vec, wspec, bspec,
                  pl.BlockSpec(memory_space=pl.ANY)],
        out_specs=[pl.BlockSpec((B_ROWS, 2 * B_WIDTH), lambda i: (i, 2)), vec, vec, wspec, bspec],
        out_shape=[jax.ShapeDtypeStruct(dproj.shape, F32), jax.ShapeDtypeStruct((1, B_WIDTH), F32),
                   jax.ShapeDtypeStruct((1, B_WIDTH), F32), jax.ShapeDtypeStruct((B_GROUPS, B_CHUNK, B_CHUNK), F32),
                   jax.ShapeDtypeStruct((B_CHUNK, B_GROUPS), F32)],
        aliases={7: 0}, args=(proj, proj, dmixin, ln_g, ln_b, w, bias_t, dproj), exchange=exchange)


C_FWD_BLOCKS = 16
C_BWD_BLOCKS = 16
C_PAIR = 2 * C_HEAD_DIM
C_PAIRS = C_HEADS // 2
C_SCALE = 1.0 / math.sqrt(C_HEAD_DIM)
C_ROT_DIM = 2 * C_ROT_HALF
ROPE_ROWS = 1024


def rope_tables(pos_col, name):
    t = pos_col.shape[0]

    def body(p_ref, c_ref, a_ref, b_ref):
        lane = jnp.bitwise_and(lax.broadcasted_iota(jnp.int32, (1, C_PAIR), 1), C_HEAD_DIM - 1)
        j = jnp.bitwise_and(lane, C_ROT_HALF - 1).astype(F32)
        inv = jnp.exp(j * (-math.log(ROPE_THETA) / C_ROT_HALF))
        ang = p_ref[...].astype(F32) * inv
        cos, sin = jnp.cos(ang), jnp.sin(ang)
        c_ref[...] = jnp.where(lane < C_ROT_DIM, cos, 1.0)
        a_ref[...] = jnp.where(lane < C_ROT_HALF, -sin, 0.0)
        b_ref[...] = jnp.where(jnp.logical_and(lane >= C_ROT_HALF, lane < C_ROT_DIM), sin, 0.0)

    tab = pl.BlockSpec((ROPE_ROWS, C_PAIR), lambda i: (i, 0))
    return pl.pallas_call(
        body, name=name, grid=(t // ROPE_ROWS,),
        in_specs=[pl.BlockSpec((ROPE_ROWS, 1), lambda i: (i, 0))],
        out_specs=[tab, tab, tab],
        out_shape=[jax.ShapeDtypeStruct((t, C_PAIR), F32)] * 3,
        compiler_params=_params(("arbitrary",)),
    )(pos_col)


def _rope(x, c, a, b):
    return x * c + pltpu.roll(x, C_PAIR - C_ROT_HALF, 1) * a + pltpu.roll(x, C_ROT_HALF, 1) * b


def _rope_t(d, c, a, b):
    return d * c + pltpu.roll(d * a, C_ROT_HALF, 1) + pltpu.roll(d * b, C_PAIR - C_ROT_HALF, 1)


C_RES = 16


def _residue_major(a, batch):
    return a.reshape(batch, SEQ // C_RES, C_RES, -1).transpose(0, 2, 1, 3).reshape(a.shape)


def _sequence_order(a, batch):
    return a.reshape(batch, C_RES, SEQ // C_RES, -1).transpose(0, 2, 1, 3).reshape(a.shape)


def _block_pieces(idx, dil):
    nblk = SEQ // dil // C_BLOCK
    r, n = idx // nblk, idx % nblk
    per = C_RES // dil
    size = C_BLOCK // per

    def pieces(blk):
        return [((dil * a + r) * (SEQ // C_RES) + size * blk, size) for a in range(per)]

    return pieces(n), pieces(jnp.maximum(n - 1, 0)), n > 0


def _get_rows(ref, pieces):
    return jnp.concatenate([ref[pl.ds(pl.multiple_of(start, 8), size), :] for start, size in pieces], axis=0)


def _set_rows(ref, pieces, val, add=False):
    for k, (start, size) in enumerate(pieces):
        rows = pl.ds(pl.multiple_of(start, 8), size)
        part = val[k * size:(k + 1) * size]
        ref[rows, :] = ref[rows, :] + part if add else part


def _head_masks():
    low = lax.broadcasted_iota(jnp.int32, (1, C_PAIR), 1) < C_HEAD_DIM
    return low, jnp.logical_not(low)


def _attn_mask(has_prev, dil):
    per = C_RES // dil
    size = C_BLOCK // per

    def position(x):
        x = jnp.bitwise_and(x, C_BLOCK - 1)
        return per * jnp.bitwise_and(x, size - 1) + x // size

    j = lax.broadcasted_iota(jnp.int32, (2 * C_BLOCK, 2 * C_BLOCK), 1)
    pi = position(lax.broadcasted_iota(jnp.int32, (2 * C_BLOCK, 2 * C_BLOCK), 0))
    pj = position(j)
    own = j < C_BLOCK
    return jnp.logical_or(jnp.logical_and(own, pj <= pi),
                          jnp.logical_and(jnp.logical_and(jnp.logical_not(own), pj >= pi), has_prev))


def _stack_heads(x):
    low, high = _head_masks()
    return jnp.concatenate([jnp.where(low, x, 0.0), jnp.where(high, x, 0.0)], axis=0)


def _unstack_heads(x):
    low, _ = _head_masks()
    return jnp.where(low, x[:C_BLOCK], x[C_BLOCK:])


def attn_fwd(qkv, cos_t, sin_a, sin_b, batch, name, exchange=None):
    t = qkv.shape[0]
    nbr = len(C_DILATIONS)

    def body(q_ref, k_ref, v_ref, c_ref, a_ref, b_ref, o_ref, l_ref, qr_ref, kr_ref, qs, ks, *stats):
        acc, mm, dd = stats[0:nbr], stats[nbr:2 * nbr], stats[2 * nbr:3 * nbr]
        c, a, b = c_ref[...], a_ref[...], b_ref[...]
        qs[...] = _rope(q_ref[...], c, a, b) * C_SCALE
        ks[...] = _rope(k_ref[...], c, a, b)
        qr_ref[...] = qs[...].astype(BF16)
        kr_ref[...] = ks[...].astype(BF16)

        def load(idx, dil):
            own, prev, has_prev = _block_pieces(idx, dil)
            return own, (has_prev, _get_rows(qs, own), _get_rows(ks, own), _get_rows(ks, prev),
                         _get_rows(v_ref, own), _get_rows(v_ref, prev))

        def scores(dil, has_prev, q, k_own, k_prev, v_own, v_prev):
            k_cat = jnp.concatenate([k_own, k_prev], axis=0).astype(BF16)
            return jnp.where(_attn_mask(has_prev, dil), _dot_nt(_stack_heads(q).astype(BF16), k_cat), NEG_BIG)

        def softmax(s):
            m = jnp.max(s, axis=-1, keepdims=True)
            p = jnp.exp(s - m)
            return p.astype(BF16), m, jnp.sum(p, axis=-1, keepdims=True)

        def values(pb, has_prev, q, k_own, k_prev, v_own, v_prev):
            low, high = _head_masks()
            v_cat = jnp.concatenate([v_own, v_prev], axis=0)
            p_wide = jnp.concatenate([pb[:C_BLOCK], pb[C_BLOCK:]], axis=1)
            v_tall = jnp.concatenate([jnp.where(low, v_cat, 0.0), jnp.where(high, v_cat, 0.0)], axis=0).astype(BF16)
            return _dot(p_wide, v_tall)

        for bi, dil in enumerate(C_DILATIONS):
            def pair(i, carry, bi=bi, dil=dil):
                low, _ = _head_masks()
                loaded = [load(C_FWD_BLOCKS * i + k, dil) for k in range(C_FWD_BLOCKS)]
                ss = [scores(dil, *ops) for _, ops in loaded]
                sm = [softmax(s) for s in ss]
                pvs = [values(pb, *ops) for (pb, _, _), (_, ops) in zip(sm, loaded)]
                for (own, _), (_, m, den), pv in zip(loaded, sm, pvs):
                    _set_rows(acc[bi], own, pv)
                    _set_rows(mm[bi], own, jnp.where(low, m[:C_BLOCK], m[C_BLOCK:]))
                    _set_rows(dd[bi], own, jnp.where(low, den[:C_BLOCK], den[C_BLOCK:]))
                return carry

            lax.fori_loop(0, SEQ // C_BLOCK // C_FWD_BLOCKS, pair, 0)
        step = 2 * C_BLOCK
        for r0 in range(0, SEQ, step):
            rr = slice(r0, r0 + step)
            ms = [mm[g][rr, :] for g in range(nbr)]
            m_all = functools.reduce(jnp.maximum, ms)
            ws = [jnp.exp(m - m_all) for m in ms]
            num = sum(acc[g][rr, :] * ws[g] for g in range(nbr))
            den = sum(dd[g][rr, :] * ws[g] for g in range(nbr))
            o_ref[rr, :] = (num / den).astype(BF16)
            l_ref[rr, :] = m_all + jnp.log(den)

    def col(k):
        return pl.BlockSpec((SEQ, C_PAIR), lambda b, p: (b, k * C_PAIRS + p))

    tab = pl.BlockSpec((SEQ, C_PAIR), lambda b, p: (b, 0))
    return _call(
        body, name=name, grid=(batch, C_PAIRS),
        in_specs=[col(0), col(1), col(2), tab, tab, tab],
        out_specs=[col(0), col(0), col(0), col(0)],
        out_shape=[jax.ShapeDtypeStruct((t, D_MODEL), BF16), jax.ShapeDtypeStruct((t, D_MODEL), F32),
                   jax.ShapeDtypeStruct((t, D_MODEL), BF16), jax.ShapeDtypeStruct((t, D_MODEL), BF16)],
        scratch_shapes=[pltpu.VMEM((SEQ, C_PAIR), F32)] * (2 + 3 * nbr),
        args=(qkv, qkv, qkv, cos_t, sin_a, sin_b), exchange=exchange)


def attn_bwd(qr, kr, qkv, cos_t, sin_a, sin_b, o, lse, do, batch, name, exchange=None):
    t = qkv.shape[0]

    def body(q_ref, k_ref, v_ref, c_ref, a_ref, b_ref, o_ref, l_ref, do_ref, dqkv_ref, qs, ks, dqs, dks, dvs, dlt):
        low, _ = _head_masks()
        c, a, b = c_ref[...], a_ref[...], b_ref[...]
        qs[...] = q_ref[...].astype(F32)
        ks[...] = k_ref[...].astype(F32)
        prod = do_ref[...] * o_ref[...].astype(F32)
        s_low = jnp.sum(jnp.where(low, prod, 0.0), axis=-1, keepdims=True)
        s_all = jnp.sum(prod, axis=-1, keepdims=True)
        dlt[...] = jnp.where(low, s_low, s_all - s_low)
        dqs[...] = jnp.zeros_like(dqs)
        dks[...] = jnp.zeros_like(dks)
        dvs[...] = jnp.zeros_like(dvs)

        def load(idx, dil):
            own, prev, has_prev = _block_pieces(idx, dil)
            return (own, prev), (has_prev, _get_rows(qs, own), _get_rows(do_ref, own), _get_rows(ks, own),
                                 _get_rows(ks, prev), _get_rows(v_ref, own), _get_rows(v_ref, prev),
                                 _get_rows(l_ref, own), _get_rows(dlt, own))

        def operands(dil, has_prev, q, do, k_own, k_prev, v_own, v_prev, l_full, d_full):
            lcol = jnp.concatenate([l_full[:, 0:1], l_full[:, C_HEAD_DIM:C_HEAD_DIM + 1]], axis=0)
            dcol = jnp.concatenate([d_full[:, 0:1], d_full[:, C_HEAD_DIM:C_HEAD_DIM + 1]], axis=0)
            return (_stack_heads(q).astype(BF16), _stack_heads(do).astype(BF16),
                    jnp.concatenate([k_own, k_prev], axis=0).astype(BF16),
                    jnp.concatenate([v_own, v_prev], axis=0).astype(BF16), lcol, dcol, _attn_mask(has_prev, dil))

        for dil in C_DILATIONS:
            def pair(i, carry, dil=dil):
                loaded = [load(C_BWD_BLOCKS * i + k, dil) for k in range(C_BWD_BLOCKS)]
                ops = [operands(dil, *o) for _, o in loaded]
                ss = [_dot_nt(q_stack, k_cat) for q_stack, _, k_cat, _, _, _, _ in ops]
                dps = [_dot_nt(do_stack, v_cat) for _, do_stack, _, v_cat, _, _, _ in ops]
                ps = [jnp.exp(jnp.where(o[6], s, NEG_BIG) - o[4]) for s, o in zip(ss, ops)]
                dss = [(p * (dp - o[5])).astype(BF16) for p, dp, o in zip(ps, dps, ops)]
                dvs_ = [_dot_tn(p.astype(BF16), o[1]) for p, o in zip(ps, ops)]
                dks_ = [_dot_tn(ds, o[0]) for ds, o in zip(dss, ops)]
                dqs_ = [_unstack_heads(_dot(ds, o[2])) for ds, o in zip(dss, ops)]
                for ((own, prev), _), dq, dk_cat, dv_cat in zip(loaded, dqs_, dks_, dvs_):
                    _set_rows(dqs, own, dq, add=True)
                    _set_rows(dks, own, dk_cat[:C_BLOCK], add=True)
                    _set_rows(dvs, own, dv_cat[:C_BLOCK], add=True)
                    _set_rows(dks, prev, dk_cat[C_BLOCK:], add=True)
                    _set_rows(dvs, prev, dv_cat[C_BLOCK:], add=True)
                return carry

            lax.fori_loop(0, SEQ // C_BLOCK // C_BWD_BLOCKS, pair, 0)
        dqkv_ref[0] = _rope_t(dqs[...] * C_SCALE, c, a, b).astype(BF16)
        dqkv_ref[1] = _rope_t(dks[...], c, a, b).astype(BF16)
        dqkv_ref[2] = dvs[...].astype(BF16)

    def col(k):
        return pl.BlockSpec((SEQ, C_PAIR), lambda b, p: (b, k * C_PAIRS + p))

    tab = pl.BlockSpec((SEQ, C_PAIR), lambda b, p: (b, 0))
    return _call(
        body, name=name, grid=(batch, C_PAIRS),
        in_specs=[col(0), col(0), col(2), tab, tab, tab, col(0), col(0), col(0)],
        out_specs=[pl.BlockSpec((3, SEQ, C_PAIR), lambda b, p: (0, b, p))],
        out_shape=[jax.ShapeDtypeStruct((3, t, D_MODEL), BF16)],
        scratch_shapes=[pltpu.VMEM((SEQ, C_PAIR), F32)] * 6,
        args=(qr, kr, qkv, cos_t, sin_a, sin_b, o, lse, do), exchange=exchange)


def allreduce_small(slab, name):
    rows, lanes = slab.shape

    def body(x_ref, out_ref, gath, send_sems, recv_sems, local_sem):
        x, y, c, chips = _place()
        me, sibling = (x, y, c), (x, y, 1 - c)

        def slot(px, py, pc):
            return gath.at[4 * px + 2 * py + pc]

        def copy(k, block, to, src=None):
            return pltpu.make_async_remote_copy(
                src_ref=slot(*block) if src is None else src, dst_ref=slot(*block),
                send_sem=send_sems.at[k], recv_sem=recv_sems.at[k], device_id=to, device_id_type=MESH)

        mine = pltpu.make_async_copy(x_ref, slot(*me), local_sem)
        mine.start()
        first = [copy(0, me, sibling, src=x_ref)]
        first += [copy(1 + j, me, (*chip, c), src=x_ref) for j, chip in enumerate(chips)]
        for cp in first:
            cp.start()
        passed = [copy(4 + j, (*chip, c), sibling) for j, chip in enumerate(chips)]
        for j, chip in enumerate(chips):
            copy(1 + j, (*chip, c), me).wait_recv()
            passed[j].start()
        copy(0, sibling, me).wait_recv()
        for j, chip in enumerate(chips):
            copy(4 + j, (*chip, 1 - c), me).wait_recv()
        for cp in first + passed:
            cp.wait_send()
        mine.wait()
        total = gath[0]
        for d in range(1, N_DEV):
            total = total + gath[d]
        out_ref[...] = total

    return pl.pallas_call(
        body, name=name,
        in_specs=[pl.BlockSpec(memory_space=pltpu.VMEM)],
        out_specs=pl.BlockSpec(memory_space=pltpu.VMEM),
        out_shape=jax.ShapeDtypeStruct((rows, lanes), F32),
        scratch_shapes=[pltpu.VMEM((N_DEV, rows, lanes), F32),
                        pltpu.SemaphoreType.DMA((7,)), pltpu.SemaphoreType.DMA((7,)), pltpu.SemaphoreType.DMA],
    )(slab)


ELT_ROWS = 512


def reduce_slabs(r, name):
    rs = [p.reshape(N_CHIPS, -1, p.shape[-1]) for p in (r if isinstance(r, (list, tuple)) else [r])]
    parts = len(rs)
    _, rows, cols = rs[0].shape
    br = min(rows, ELT_ROWS)
    nb = rows // br

    def total(r_ref):
        return ((r_ref[3].astype(F32) + r_ref[0].astype(F32)) + r_ref[1].astype(F32)) + r_ref[2].astype(F32)

    def body(*refs):
        r_refs, o_ref = refs[:-1], refs[-1]
        p = pl.program_id(0)
        out = total(r_refs[0])
        for k in range(1, parts):
            out = jnp.where(p == k, total(r_refs[k]), out)
        o_ref[...] = out

    def part_spec(k):
        return pl.BlockSpec((N_CHIPS, br, cols), lambda p, i: (0, jnp.where(p == k, i, jnp.where(p < k, 0, nb - 1)), 0))

    return pl.pallas_call(
        body, name=name, grid=(parts, nb),
        in_specs=[part_spec(k) for k in range(parts)],
        out_specs=pl.BlockSpec((br, cols), lambda p, i: (p * nb + i, 0)),
        out_shape=jax.ShapeDtypeStruct((parts * rows, cols), F32),
        compiler_params=_params(("arbitrary", "arbitrary")),
    )(*rs)


def _adamw(w, g, m, v):
    m = ADAM_B1 * m + (1.0 - ADAM_B1) * g
    v = ADAM_B2 * v + (1.0 - ADAM_B2) * jnp.square(g)
    m_hat = m / (1.0 - ADAM_B1 ** ADAM_STEP)
    v_hat = v / (1.0 - ADAM_B2 ** ADAM_STEP)
    delta = -ADAM_LR * (m_hat / (jnp.sqrt(v_hat) + ADAM_EPS) + ADAM_WD * w)
    return delta, m, v


def adamw_big(w, s_mine, s_sibling, m, v, name):
    rows, cols = w.shape
    parts = len(s_mine)
    br = min(rows // parts, ELT_ROWS)
    nb = rows // parts // br

    def body(w_ref, m_ref, v_ref, *rest):
        sums, (g_out, d_out, m_out, v_out) = rest[:2 * parts], rest[2 * parts:]
        p = pl.program_id(0)
        g = sums[0][...] + sums[parts][...]
        for k in range(1, parts):
            g = jnp.where(p == k, sums[k][...] + sums[parts + k][...], g)
        g_out[...] = g
        d_out[...], m_out[...], v_out[...] = _adamw(w_ref[...], g, m_ref[...], v_ref[...])

    def part_spec(k):
        return pl.BlockSpec((br, cols), lambda p, i: (jnp.where(p == k, i, jnp.where(p < k, 0, nb - 1)), 0))

    blk = pl.BlockSpec((br, cols), lambda p, i: (p * nb + i, 0))
    out = jax.ShapeDtypeStruct((rows, cols), F32)
    return pl.pallas_call(
        body, name=name, grid=(parts, nb),
        in_specs=[blk] * 3 + [part_spec(k) for k in range(parts)] * 2, out_specs=[blk] * 4, out_shape=[out] * 4,
        compiler_params=_params(("arbitrary", "arbitrary")),
    )(w, m, v, *s_mine, *s_sibling)


def adamw_small(ws, gs, ms, vs, name):
    n = len(ws)

    def body(*refs):
        w_refs, g_refs, m_refs, v_refs = (refs[k * n:(k + 1) * n] for k in range(4))
        d_out, m_out, v_out = (refs[(4 + k) * n:(5 + k) * n] for k in range(3))
        for i in range(n):
            d_out[i][...], m_out[i][...], v_out[i][...] = _adamw(
                w_refs[i][...], g_refs[i][...], m_refs[i][...], v_refs[i][...])

    outs = [jax.ShapeDtypeStruct(w.shape, F32) for w in ws]
    res = pl.pallas_call(body, name=name, out_shape=outs * 3)(*ws, *gs, *ms, *vs)
    return res[:n], res[n:2 * n], res[2 * n:]


SLAB_LANES = 128
SLAB_ROW_ALIGN = 8


def _pack(parts):
    flat = jnp.concatenate([p.reshape(-1) for p in parts])
    rows = -(-flat.shape[0] // (SLAB_LANES * SLAB_ROW_ALIGN)) * SLAB_ROW_ALIGN
    flat = jnp.pad(flat, (0, rows * SLAB_LANES - flat.shape[0]))
    return flat.reshape(rows, SLAB_LANES)


def _unpack(slab, shapes):
    flat = slab.reshape(-1)
    out, pos = [], 0
    for s in shapes:
        size = math.prod(s)
        out.append(flat[pos:pos + size].reshape(s))
        pos += size
    return out


def kernel(x, positions, norm_mix_pre, norm_mix_post, norm_ffn_pre, norm_ffn_post, w_in_even, lb_table, a_norm, b_ln_g, b_ln_b, b_ws, b_bias, w_out_even, w_in_odd, w_out_odd, w_ff1, w_ff2, loss_target, m_norm_mix_pre, m_norm_mix_post, m_norm_ffn_pre, m_norm_ffn_post, m_w_in_even, m_lb_table, m_a_norm, m_b_ln_g, m_b_ln_b, m_b_ws, m_b_bias, m_w_out_even, m_w_in_odd, m_w_out_odd, m_w_ff1, m_w_ff2, v_norm_mix_pre, v_norm_mix_post, v_norm_ffn_pre, v_norm_ffn_post, v_w_in_even, v_lb_table, v_a_norm, v_b_ln_g, v_b_ln_b, v_b_ws, v_b_bias, v_w_out_even, v_w_in_odd, v_w_out_odd, v_w_ff1, v_w_ff2):
    batch = x.shape[0]
    t = batch * SEQ
    d = D_MODEL
    x0 = x.reshape(t, d)
    target = loss_target.reshape(t, d)

    def gain(p, layer):
        return p[layer:layer + 1]

    def gather(*shards):
        return _Exchange("gather", [w.astype(BF16) for w in shards])

    def scatter(*grads):
        return _Exchange("scatter", grads)

    (win_e,) = exchange_alone(gather(w_in_even[0]), "gather_in_even")
    bias_t = b_bias[0].T
    proj, h0, w1_0 = norm_matmul(x0, gain(norm_mix_pre, 0), win_e, "in_proj_even", exchange=gather(w_ff1[0]))
    oa, states, decays, w2_0 = hgrn2_fwd(proj, lb_table, a_norm, batch, "hgrn2_fwd", exchange=gather(w_ff2[0]))
    mixin, wout_e = gmlp_fwd(proj, oa, b_ln_g, b_ln_b, b_ws[0], bias_t, "gmlp_fwd", exchange=gather(w_out_even[0]))
    mix0, x1 = out_proj(mixin, wout_e, x0, gain(norm_mix_post, 0), "out_proj_even")
    x2, hf0, a0, y0, win_o, wout_o = ffn_fwd(x1, gain(norm_ffn_pre, 0), w1_0, w2_0, gain(norm_ffn_post, 0),
                                             "ffn_fwd_0", exchange=gather(w_in_odd[0], w_out_odd[0]))
    x2p = _residue_major(x2, batch)
    qkv, h1 = norm_matmul(x2p, gain(norm_mix_pre, 1), win_o, "in_proj_odd")
    cos_t, sin_a, sin_b = rope_tables(_residue_major(positions.reshape(t, 1), batch), "rope_tables")
    ao, lse, q_rot, k_rot, w1_1, w2_1 = attn_fwd(qkv, cos_t, sin_a, sin_b, batch, "attn_fwd",
                                                 exchange=gather(w_ff1[1], w_ff2[1]))
    mix1, x3 = out_proj(ao, wout_o, x2p, gain(norm_mix_post, 1), "out_proj_odd")
    dx4, hf1, a1, y1, loss_part = ffn_fwd(x3, gain(norm_ffn_pre, 1), w1_1, w2_1, gain(norm_ffn_post, 1),
                                          "ffn_fwd_1", target=_residue_major(target, batch))

    hc = D_FF // N_CHIPS
    dx3, dy1, da1, dg_fpre1, dg_fpost1 = ffn_bwd(
        dx4, x3, y1, a1, gain(norm_ffn_pre, 1), gain(norm_ffn_post, 1), w1_1, w2_1, "ffn_bwd_1")
    g_w1_1 = weight_grad(hf1, da1, "b", d, hc, False, "wgrad_ff1_1")
    g_w2_1 = weight_grad(a1, dy1, "a", hc, d, True, "wgrad_ff2_1")
    dmix1, dao, dg_mpost1 = out_proj_bwd(dx3, mix1, gain(norm_mix_post, 1), wout_o, "out_proj_bwd_odd")
    g_wout_o = weight_grad(ao, dmix1, "a", d // N_CHIPS, d, False, "wgrad_out_odd")
    dqkv, r_w1_1, r_w2_1, r_wout_o = attn_bwd(q_rot, k_rot, qkv, cos_t, sin_a, sin_b, ao, lse, dao, batch, "attn_bwd",
                                              exchange=scatter(g_w1_1, g_w2_1, g_wout_o))
    dx2p, dg_mpre1 = norm_matmul_bwd(dqkv, win_o, x2p, gain(norm_mix_pre, 1), dx3, "in_proj_bwd_odd")
    dx2 = _sequence_order(dx2p, batch)
    g_win_o = weight_grad_stacked(h1, dqkv, 3 * d // N_CHIPS, "wgrad_in_odd")
    s_w1_1, s_w2_1, s_wout_o = (reduce_slabs(r, n) for r, n in (
        (r_w1_1, "reduce_ff1_1"), (r_w2_1, "reduce_ff2_1"), (r_wout_o, "reduce_out_odd")))
    dx1, dy0, da0, dg_fpre0, dg_fpost0, r_win_o, t_w1_1, t_w2_1, t_wout_o = ffn_bwd(
        dx2, x1, y0, a0, gain(norm_ffn_pre, 0), gain(norm_ffn_post, 0), w1_0, w2_0, "ffn_bwd_0",
        exchange=_Both(scatter(g_win_o), _Swap([s_w1_1, s_w2_1, s_wout_o])))
    g_w1_0a, g_w1_0b = weight_grad(hf0, da0, "b", d, hc, False, "wgrad_ff1_0", parts=2)
    g_w2_0a, g_w2_0b, r_w1_0a = weight_grad(a0, dy0, "a", hc, d, True, "wgrad_ff2_0", exchange=scatter(g_w1_0a), parts=2)
    dmix0, dmixin, dg_mpost0 = out_proj_bwd(dx1, mix0, gain(norm_mix_post, 0), wout_e, "out_proj_bwd_even")
    g_wout_e = weight_grad(mixin, dmix0, "a", d // N_CHIPS, d, False, "wgrad_out_even")
    s_win_o = reduce_slabs(r_win_o, "reduce_in_odd")
    dproj, d_lb, d_anorm, r_w1_0b, r_w2_0a, t_win_o = hgrn2_bwd(
        proj, states, decays, lb_table, a_norm, dmixin, batch, "hgrn2_bwd",
        exchange=_Both(scatter(g_w1_0b, g_w2_0a), _Swap([s_win_o])))
    s_w1_0 = reduce_slabs([r_w1_0a, r_w1_0b], "reduce_ff1_0")
    dproj, d_lng, d_lnb, d_ws, d_bias_t, r_w2_0b, t_w1_0 = gmlp_bwd(
        proj, dmixin, b_ln_g, b_ln_b, b_ws[0], bias_t, dproj, "gmlp_bwd",
        exchange=_Both(scatter(g_w2_0b), _Swap([s_w1_0])))
    s_w2_0 = reduce_slabs([r_w2_0a, r_w2_0b], "reduce_ff2_0")
    g_win_e, r_wout_e, t_w2_0 = weight_grad(h0, dproj, "b", d, 3 * d // N_CHIPS, False, "wgrad_in_even",
                                            exchange=_Both(scatter(g_wout_e), _Swap([s_w2_0])))
    s_wout_e = reduce_slabs(r_wout_e, "reduce_out_even")
    dx0, dg_mpre0, r_win_e, t_wout_e = norm_matmul_bwd(
        dproj, win_e, x0, gain(norm_mix_pre, 0), dx1, "in_proj_bwd_even",
        exchange=_Both(scatter(g_win_e), _Swap([s_wout_e])))
    grad_x = dx0.reshape(x.shape)
    s_win_e = reduce_slabs(r_win_e, "reduce_in_even")
    (t_win_e,) = exchange_alone(_Swap([s_win_e]), "sibling_swap")

    big_w = [w_in_even, w_out_even, w_in_odd, w_out_odd, w_ff1, w_ff2]
    big_m = [m_w_in_even, m_w_out_even, m_w_in_odd, m_w_out_odd, m_w_ff1, m_w_ff2]
    big_v = [v_w_in_even, v_w_out_even, v_w_in_odd, v_w_out_odd, v_w_ff1, v_w_ff2]
    mine = [[s_win_e], [s_wout_e], [s_win_o], [s_wout_o], [s_w1_0, s_w1_1], [s_w2_0, s_w2_1]]
    theirs = [[t_win_e], [t_wout_e], [t_win_o], [t_wout_o], [t_w1_0, t_w1_1], [t_w2_0, t_w2_1]]
    big = []
    for i, (w, m, v) in enumerate(zip(big_w, big_m, big_v)):
        two_d = (-1, w.shape[-1])
        res = adamw_big(w.reshape(two_d), mine[i], theirs[i], m.reshape(two_d), v.reshape(two_d), "adamw_big_%d" % i)
        big.append([r.reshape(w.shape) for r in res])

    small_w = [norm_mix_pre, norm_mix_post, norm_ffn_pre, norm_ffn_post, lb_table, a_norm, b_ln_g, b_ln_b, b_ws, b_bias]
    small_m = [m_norm_mix_pre, m_norm_mix_post, m_norm_ffn_pre, m_norm_ffn_post, m_lb_table, m_a_norm, m_b_ln_g,
               m_b_ln_b, m_b_ws, m_b_bias]
    small_v = [v_norm_mix_pre, v_norm_mix_post, v_norm_ffn_pre, v_norm_ffn_post, v_lb_table, v_a_norm, v_b_ln_g,
               v_b_ln_b, v_b_ws, v_b_bias]
    partial = [jnp.concatenate([dg_mpre0, dg_mpre1]), jnp.concatenate([dg_mpost0, dg_mpost1]),
               jnp.concatenate([dg_fpre0, dg_fpre1]), jnp.concatenate([dg_fpost0, dg_fpost1]),
               d_lb, d_anorm, d_lng, d_lnb, d_ws[None], d_bias_t.T[None]]
    *small_g, loss = _unpack(allreduce_small(_pack(partial + [loss_part]), "allreduce_small"),
                             [w.shape for w in small_w] + [()])
    small_d, small_nm, small_nv = adamw_small(small_w, small_g, small_m, small_v, "adamw_small")

    order = ["norm_mix_pre", "norm_mix_post", "norm_ffn_pre", "norm_ffn_post", "w_in_even", "lb_table", "a_norm",
             "b_ln_g", "b_ln_b", "b_ws", "b_bias", "w_out_even", "w_in_odd", "w_out_odd", "w_ff1", "w_ff2"]
    small_names = ["norm_mix_pre", "norm_mix_post", "norm_ffn_pre", "norm_ffn_post", "lb_table", "a_norm",
                   "b_ln_g", "b_ln_b", "b_ws", "b_bias"]
    big_names = ["w_in_even", "w_out_even", "w_in_odd", "w_out_odd", "w_ff1", "w_ff2"]
    grads, deltas, new_m, new_v = {}, {}, {}, {}
    for i, nm in enumerate(small_names):
        grads[nm], deltas[nm], new_m[nm], new_v[nm] = small_g[i], small_d[i], small_nm[i], small_nv[i]
    for i, nm in enumerate(big_names):
        grads[nm], deltas[nm], new_m[nm], new_v[nm] = big[i]
    return (loss, grad_x, *[grads[n] for n in order], *[deltas[n] for n in order],
            *[new_m[n] for n in order], *[new_v[n] for n in order])
```

```python
import functools
import math

import jax
import jax.numpy as jnp
from jax import lax
from jax.experimental import pallas as pl
from jax.experimental.pallas import tpu as pltpu

F32 = jnp.float32
BF16 = jnp.bfloat16
MESH = pl.DeviceIdType.MESH

D_MODEL = 1024
SEQ = 2048
D_FF = 4096
N_CHIPS = 4
A_WIDTH = 512
A_HEADS = 4
A_DK = 128
A_CHUNK = 64
A_SUB = 16
B_WIDTH = 512
B_GROUPS = 4
B_CHUNK = 128
C_HEADS = 16
C_HEAD_DIM = 64
C_ROT_HALF = 8
C_BLOCK = 128
C_DILATIONS = (1, 4, 16)
ROPE_THETA = 500000.0
EPS = 1e-6
ADAM_LR = 0.001
ADAM_B1 = 0.9
ADAM_B2 = 0.999
ADAM_EPS = 1e-08
ADAM_WD = 0.01
ADAM_STEP = 10

ROW_TILE = 512
FFN_ROWS = 1024
WGRAD_ROWS = 2048
VMEM_LIMIT = 56 * 1024 * 1024
NEG_BIG = -1e30


def _params(sem=None):
    return pltpu.CompilerParams(dimension_semantics=sem, vmem_limit_bytes=VMEM_LIMIT)


def _dot(a, b):
    return jnp.dot(a, b, preferred_element_type=F32)


def _dot_nt(a, b):
    return lax.dot_general(a, b, (((1,), (1,)), ((), ())), preferred_element_type=F32)


def _dot_tn(a, b):
    return lax.dot_general(a, b, (((0,), (0,)), ((), ())), preferred_element_type=F32)


def _rms(x, g):
    r = lax.rsqrt(jnp.mean(x * x, axis=-1, keepdims=True) + EPS)
    return x * r * g


def _rms_bwd(x, g, dy):
    r = lax.rsqrt(jnp.mean(x * x, axis=-1, keepdims=True) + EPS)
    xh = x * r
    dg = jnp.sum(dy * xh, axis=0, keepdims=True)
    dxh = dy * g
    dx = r * (dxh - xh * jnp.mean(dxh * xh, axis=-1, keepdims=True))
    return dx, dg


def _accumulate(ref, val, first):
    @pl.when(first)
    def _():
        ref[...] = val

    @pl.when(jnp.logical_not(first))
    def _():
        ref[...] += val


N_DEV = 8
ANY = pl.BlockSpec(memory_space=pl.ANY)


def _place():
    x, y, c = lax.axis_index("x"), lax.axis_index("y"), lax.axis_index("c")
    return x, y, c, [(1 - x, y), (x, 1 - y), (1 - x, 1 - y)]


class _Exchange:
    def __init__(self, kind, arrays):
        self.kind, self.arrays, self.n = kind, list(arrays), len(arrays)
        per_peer = pltpu.SemaphoreType.DMA((3 * self.n,))
        if kind == "gather":
            self.out_shape = [jax.ShapeDtypeStruct((N_CHIPS,) + a.shape, a.dtype) for a in self.arrays]
            self.scratch = [per_peer, per_peer, pltpu.SemaphoreType.DMA((self.n,)), per_peer, per_peer]
        else:
            self.out_shape = [jax.ShapeDtypeStruct(a.shape, a.dtype) for a in self.arrays]
            self.scratch = [per_peer, per_peer, pltpu.SemaphoreType.DMA((self.n,))]

    def _copies(self, ins, outs, sems):
        send_sems, recv_sems, local_sems = sems[:3]
        x, y, c, chips = _place()
        me = 2 * x + y
        local, remote = [], []
        for a in range(self.n):
            if self.kind == "gather":
                local.append(pltpu.make_async_copy(ins[a], outs[a].at[me], local_sems.at[a]))
                half = self.arrays[a].shape[0] // 2

                def rows(ref, core, half=half):
                    return ref.at[pl.ds(core * half, half)]
            else:
                local.append(pltpu.make_async_copy(ins[a].at[me], outs[a].at[3], local_sems.at[a]))
            for j, (px, py) in enumerate(chips):
                k = 3 * a + j
                peer = 2 * px + py

                def copy(src, dst, to, send_sem=send_sems.at[k], recv_sem=recv_sems.at[k]):
                    return pltpu.make_async_remote_copy(src_ref=src, dst_ref=dst, send_sem=send_sem, recv_sem=recv_sem,
                                                        device_id=to, device_id_type=MESH)

                if self.kind == "gather":
                    sent = copy(rows(ins[a], c), rows(outs[a].at[me], c), (px, py, c))
                    landed = copy(rows(ins[a], c), rows(outs[a].at[peer], c), (px, py, c))
                    on = dict(send_sem=sems[3].at[k], recv_sem=sems[4].at[k])
                    passed = copy(rows(outs[a].at[peer], c), rows(outs[a].at[peer], c), (x, y, 1 - c), **on)
                    handed = copy(rows(outs[a].at[peer], c), rows(outs[a].at[peer], 1 - c), (x, y, 1 - c), **on)
                    remote.append((sent, landed, passed, handed))
                else:
                    sent = copy(ins[a].at[peer], outs[a].at[j], (px, py, c))
                    remote.append((sent, sent, None, None))
        return local, remote

    def start(self, ins, outs, sems):
        local, remote = self._copies(ins, outs, sems)
        for cp in local:
            cp.start()
        for sent, _, _, _ in remote:
            sent.start()

    def finish(self, ins, outs, sems):
        local, remote = self._copies(ins, outs, sems)
        for _, landed, passed, _ in remote:
            landed.wait_recv()
            if passed is not None:
                passed.start()
        for sent, _, passed, handed in remote:
            if passed is not None:
                handed.wait_recv()
                passed.wait_send()
            sent.wait_send()
        for cp in local:
            cp.wait()


class _Swap:
    def __init__(self, arrays):
        self.arrays, self.n = list(arrays), len(arrays)
        self.out_shape = [jax.ShapeDtypeStruct(a.shape, a.dtype) for a in self.arrays]
        self.scratch = [pltpu.SemaphoreType.DMA((self.n,)), pltpu.SemaphoreType.DMA((self.n,))]

    def _copies(self, ins, outs, sems):
        x, y, c, _ = _place()
        return [pltpu.make_async_remote_copy(src_ref=ins[a], dst_ref=outs[a], send_sem=sems[0].at[a],
                                             recv_sem=sems[1].at[a], device_id=(x, y, 1 - c), device_id_type=MESH)
                for a in range(self.n)]

    def start(self, ins, outs, sems):
        for cp in self._copies(ins, outs, sems):
            cp.start()

    def finish(self, ins, outs, sems):
        for cp in self._copies(ins, outs, sems):
            cp.wait_recv()
            cp.wait_send()


class _Both:
    def __init__(self, first, second):
        self.parts = (first, second)
        self.arrays, self.n = first.arrays + second.arrays, first.n + second.n
        self.out_shape = first.out_shape + second.out_shape
        self.scratch = first.scratch + second.scratch

    def _split(self, ins, outs, sems):
        a, b = self.parts
        return ((a, ins[:a.n], outs[:a.n], sems[:len(a.scratch)]),
                (b, ins[a.n:], outs[a.n:], sems[len(a.scratch):]))

    def start(self, ins, outs, sems):
        for ex, i, o, s in self._split(ins, outs, sems):
            ex.start(i, o, s)

    def finish(self, ins, outs, sems):
        for ex, i, o, s in self._split(ins, outs, sems):
            ex.finish(i, o, s)


def _call(body, *, name, grid, in_specs, out_specs, out_shape, args, scratch_shapes=(), aliases=None, exchange=None):
    if exchange is None:
        return pl.pallas_call(
            body, name=name, grid=grid, in_specs=in_specs, out_specs=out_specs, out_shape=out_shape,
            scratch_shapes=list(scratch_shapes), input_output_aliases=aliases or {},
            compiler_params=_params(("arbitrary",) * len(grid)))(*args)
    n_in, n_out, n_scr, n_ex = len(in_specs), len(out_specs), len(scratch_shapes), exchange.n
    steps = grid

    def wrapped(*refs):
        ins, refs = refs[:n_in], refs[n_in:]
        ex_in, refs = refs[:n_ex], refs[n_ex:]
        outs, refs = refs[:n_out], refs[n_out:]
        ex_out, refs = refs[:n_ex], refs[n_ex:]
        scr, sems = refs[:n_scr], refs[n_scr:]
        first = functools.reduce(jnp.logical_and, [pl.program_id(k) == 0 for k in range(len(steps))])
        last = functools.reduce(jnp.logical_and, [pl.program_id(k) == steps[k] - 1 for k in range(len(steps))])

        @pl.when(first)
        def _():
            exchange.start(ex_in, ex_out, sems)

        body(*ins, *outs, *scr)

        @pl.when(last)
        def _():
            exchange.finish(ex_in, ex_out, sems)

    return pl.pallas_call(
        wrapped, name=name, grid=grid,
        in_specs=list(in_specs) + [ANY] * n_ex, out_specs=list(out_specs) + [ANY] * n_ex,
        out_shape=list(out_shape) + exchange.out_shape,
        scratch_shapes=list(scratch_shapes) + exchange.scratch, input_output_aliases=aliases or {},
        compiler_params=_params(("arbitrary",) * len(grid)))(*args, *exchange.arrays)


def exchange_alone(exchange, name):
    def body(*refs):
        n = exchange.n
        exchange.start(refs[:n], refs[n:2 * n], refs[2 * n:])
        exchange.finish(refs[:n], refs[n:2 * n], refs[2 * n:])

    return pl.pallas_call(
        body, name=name, in_specs=[ANY] * exchange.n, out_specs=[ANY] * exchange.n,
        out_shape=exchange.out_shape, scratch_shapes=exchange.scratch)(*exchange.arrays)


def norm_matmul(x, g, wg, name, exchange=None):
    t, d = x.shape
    nl = wg.shape[2]

    def body(x_ref, g_ref, w_ref, o_ref, h_ref):
        h = _rms(x_ref[...], g_ref[...]).astype(BF16)
        h_ref[...] = h
        for c in range(N_CHIPS):
            o_ref[:, c * nl:(c + 1) * nl] = _dot(h, w_ref[c])

    return _call(
        body, name=name, grid=(t // ROW_TILE,),
        in_specs=[pl.BlockSpec((ROW_TILE, d), lambda i: (i, 0)),
                  pl.BlockSpec((1, d), lambda i: (0, 0)),
                  pl.BlockSpec((N_CHIPS, d, nl), lambda i: (0, 0, 0))],
        out_specs=[pl.BlockSpec((ROW_TILE, N_CHIPS * nl), lambda i: (i, 0)),
                   pl.BlockSpec((ROW_TILE, d), lambda i: (i, 0))],
        out_shape=[jax.ShapeDtypeStruct((t, N_CHIPS * nl), F32), jax.ShapeDtypeStruct((t, d), BF16)],
        args=(x, g, wg), exchange=exchange)


def norm_matmul_bwd(dproj, wg, x, g, dres, name, exchange=None):
    t, d = x.shape
    nl = wg.shape[2]
    stacked = dproj.ndim == 3
    piece = math.gcd(nl, dproj.shape[-1])

    def body(dp_ref, w_ref, x_ref, g_ref, dres_ref, dx_ref, dg_ref):
        dh = None
        for j in range(N_CHIPS * nl // piece):
            c, off = divmod(j * piece, nl)
            if stacked:
                p, lo = divmod(j * piece, dproj.shape[-1])
                lhs = dp_ref[p, :, lo:lo + piece]
            else:
                lhs = dp_ref[:, j * piece:(j + 1) * piece]
            part = _dot_nt(lhs.astype(BF16), w_ref[c, :, off:off + piece])
            dh = part if dh is None else dh + part
        dx, dg = _rms_bwd(x_ref[...], g_ref[...], dh)
        dx_ref[...] = dres_ref[...] + dx
        _accumulate(dg_ref, dg, pl.program_id(0) == 0)

    row = pl.BlockSpec((ROW_TILE, d), lambda i: (i, 0))
    vec = pl.BlockSpec((1, d), lambda i: (0, 0))
    if stacked:
        dp_spec = pl.BlockSpec((dproj.shape[0], ROW_TILE, dproj.shape[-1]), lambda i: (0, i, 0))
    else:
        dp_spec = pl.BlockSpec((ROW_TILE, N_CHIPS * nl), lambda i: (i, 0))
    return _call(
        body, name=name, grid=(t // ROW_TILE,),
        in_specs=[dp_spec, pl.BlockSpec((N_CHIPS, d, nl), lambda i: (0, 0, 0)), row, vec, row],
        out_specs=[row, vec],
        out_shape=[jax.ShapeDtypeStruct((t, d), F32), jax.ShapeDtypeStruct((1, d), F32)],
        args=(dproj, wg, x, g, dres), exchange=exchange)


def out_proj(a, wg, x, g, name):
    t, d = x.shape
    kl = wg.shape[1]

    def body(a_ref, w_ref, x_ref, g_ref, mix_ref, xo_ref):
        acc = _dot(a_ref[:, 0:kl], w_ref[0])
        for c in range(1, N_CHIPS):
            acc += _dot(a_ref[:, c * kl:(c + 1) * kl], w_ref[c])
        mix_ref[...] = acc
        xo_ref[...] = x_ref[...] + _rms(acc, g_ref[...])

    row = pl.BlockSpec((ROW_TILE, d), lambda i: (i, 0))
    return pl.pallas_call(
        body, name=name, grid=(t // ROW_TILE,),
        in_specs=[row, pl.BlockSpec((N_CHIPS, kl, d), lambda i: (0, 0, 0)), row,
                  pl.BlockSpec((1, d), lambda i: (0, 0))],
        out_specs=[row, row],
        out_shape=[jax.ShapeDtypeStruct((t, d), F32), jax.ShapeDtypeStruct((t, d), F32)],
        compiler_params=_params(("arbitrary",)),
    )(a, wg, x, g)


def out_proj_bwd(dxo, mix, g, wg, name):
    t, d = mix.shape
    kl = wg.shape[1]

    def body(dxo_ref, mix_ref, g_ref, w_ref, dmix_ref, da_ref, dg_ref):
        dmix, dg = _rms_bwd(mix_ref[...], g_ref[...], dxo_ref[...])
        dmb = dmix.astype(BF16)
        dmix_ref[...] = dmb
        for c in range(N_CHIPS):
            da_ref[:, c * kl:(c + 1) * kl] = _dot_nt(dmb, w_ref[c])
        _accumulate(dg_ref, dg, pl.program_id(0) == 0)

    row = pl.BlockSpec((ROW_TILE, d), lambda i: (i, 0))
    vec = pl.BlockSpec((1, d), lambda i: (0, 0))
    return pl.pallas_call(
        body, name=name, grid=(t // ROW_TILE,),
        in_specs=[row, row, vec, pl.BlockSpec((N_CHIPS, kl, d), lambda i: (0, 0, 0))],
        out_specs=[row, row, vec],
        out_shape=[jax.ShapeDtypeStruct((t, d), BF16), jax.ShapeDtypeStruct((t, d), F32),
                   jax.ShapeDtypeStruct((1, d), F32)],
        compiler_params=_params(("arbitrary",)),
    )(dxo, mix, g, wg)


def ffn_fwd(x, gpre, w1g, w2g, gpost, name, exchange=None, target=None):
    t, d = x.shape
    hc = w1g.shape[2]
    with_loss = target is not None

    def body(x_ref, gpre_ref, w1_ref, w2_ref, gpost_ref, *rest):
        if with_loss:
            t_ref, xo_ref, h_ref, a_ref, y_ref, l_ref, acc = rest
        else:
            xo_ref, h_ref, a_ref, y_ref, acc = rest
        i, c = pl.program_id(0), pl.program_id(1)

        @pl.when(c == 0)
        def _():
            h_ref[...] = _rms(x_ref[...], gpre_ref[...]).astype(BF16)

        a = _dot(h_ref[...], w1_ref[...])
        a_ref[...] = a.astype(BF16)
        r = jnp.square(jnp.maximum(a, 0.0)).astype(BF16)
        _accumulate(acc, _dot(r, w2_ref[...]), c == 0)

        @pl.when(c == N_CHIPS - 1)
        def _():
            y = acc[...]
            y_ref[...] = y
            xo = x_ref[...] + _rms(y, gpost_ref[...])
            if with_loss:
                e = xo - t_ref[...]
                xo_ref[...] = e * (1.0 / d)
                part = jnp.sum(jnp.sum(e * e, axis=-1, keepdims=True), axis=0, keepdims=True) * (0.5 / d)
                _accumulate(l_ref, part, i == 0)
            else:
                xo_ref[...] = xo

    row = pl.BlockSpec((FFN_ROWS, d), lambda i, c: (i, 0))
    vec = pl.BlockSpec((1, d), lambda i, c: (0, 0))
    one = pl.BlockSpec((1, 1), lambda i, c: (0, 0))
    return _call(
        body, name=name, grid=(t // FFN_ROWS, N_CHIPS),
        in_specs=[row, vec,
                  pl.BlockSpec((None, d, hc), lambda i, c: (c, 0, 0)),
                  pl.BlockSpec((None, hc, d), lambda i, c: (c, 0, 0)), vec] + ([row] if with_loss else []),
        out_specs=[row, row, pl.BlockSpec((FFN_ROWS, hc), lambda i, c: (i, c)), row] + ([one] if with_loss else []),
        out_shape=[jax.ShapeDtypeStruct((t, d), F32), jax.ShapeDtypeStruct((t, d), BF16),
                   jax.ShapeDtypeStruct((t, N_CHIPS * hc), BF16), jax.ShapeDtypeStruct((t, d), F32)]
        + ([jax.ShapeDtypeStruct((1, 1), F32)] if with_loss else []),
        scratch_shapes=[pltpu.VMEM((FFN_ROWS, d), F32)],
        args=(x, gpre, w1g, w2g, gpost) + ((target,) if with_loss else ()), exchange=exchange)


def ffn_bwd(dxo, x, y, a, gpre, gpost, w1g, w2g, name, exchange=None):
    t, d = x.shape
    hc = w1g.shape[2]

    def body(dxo_ref, x_ref, y_ref, a_ref, gpre_ref, gpost_ref, w1_ref, w2_ref,
             dxi_ref, dy_ref, da_ref, dgpre_ref, dgpost_ref, acc):
        i, c = pl.program_id(0), pl.program_id(1)

        @pl.when(c == 0)
        def _():
            dy, dg = _rms_bwd(y_ref[...], gpost_ref[...], dxo_ref[...])
            dy_ref[...] = dy.astype(BF16)
            _accumulate(dgpost_ref, dg, i == 0)

        dr = _dot_nt(dy_ref[...], w2_ref[...])
        da = (dr * (2.0 * jnp.maximum(a_ref[...].astype(F32), 0.0))).astype(BF16)
        da_ref[...] = da
        _accumulate(acc, _dot_nt(da, w1_ref[...]), c == 0)

        @pl.when(c == N_CHIPS - 1)
        def _():
            dx, dg = _rms_bwd(x_ref[...], gpre_ref[...], acc[...])
            dxi_ref[...] = dxo_ref[...] + dx
            _accumulate(dgpre_ref, dg, i == 0)

    row = pl.BlockSpec((ROW_TILE, d), lambda i, c: (i, 0))
    vec = pl.BlockSpec((1, d), lambda i, c: (0, 0))
    hid = pl.BlockSpec((ROW_TILE, hc), lambda i, c: (i, c))
    return _call(
        body, name=name, grid=(t // ROW_TILE, N_CHIPS),
        in_specs=[row, row, row, hid, vec, vec,
                  pl.BlockSpec((None, d, hc), lambda i, c: (c, 0, 0)),
                  pl.BlockSpec((None, hc, d), lambda i, c: (c, 0, 0))],
        out_specs=[row, row, hid, vec, vec],
        out_shape=[jax.ShapeDtypeStruct((t, d), F32), jax.ShapeDtypeStruct((t, d), BF16),
                   jax.ShapeDtypeStruct((t, N_CHIPS * hc), BF16),
                   jax.ShapeDtypeStruct((1, d), F32), jax.ShapeDtypeStruct((1, d), F32)],
        scratch_shapes=[pltpu.VMEM((ROW_TILE, d), F32)],
        args=(dxo, x, y, a, gpre, gpost, w1g, w2g), exchange=exchange)


def weight_grad(a, b, chunked, bk, bn, relu2, name, exchange=None, parts=1):
    t = a.shape[0]
    a_on = chunked == "a"
    rows = min(t, WGRAD_ROWS)
    n_steps = t // rows
    part_rows = bk // parts

    def body(a_ref, b_ref, *rest):
        o_refs, acc = rest[:-1], rest[-1]
        s = pl.program_id(1)
        av = a_ref[...]
        if relu2:
            av = jnp.square(jnp.maximum(av.astype(F32), 0.0))
        _accumulate(acc, _dot_tn(av.astype(BF16), b_ref[...].astype(BF16)), s == 0)

        @pl.when(s == n_steps - 1)
        def _():
            for k, o_ref in enumerate(o_refs):
                o_ref[...] = acc[k * part_rows:(k + 1) * part_rows, :].astype(BF16)

    res = _call(
        body, name=name, grid=(N_CHIPS, n_steps),
        in_specs=[pl.BlockSpec((rows, bk), (lambda c, s: (s, c)) if a_on else (lambda c, s: (s, 0))),
                  pl.BlockSpec((rows, bn), (lambda c, s: (s, 0)) if a_on else (lambda c, s: (s, c)))],
        out_specs=[pl.BlockSpec((None, part_rows, bn), lambda c, s: (c, 0, 0))] * parts,
        out_shape=[jax.ShapeDtypeStruct((N_CHIPS, part_rows, bn), BF16)] * parts,
        scratch_shapes=[pltpu.VMEM((bk, bn), F32)],
        args=(a, b), exchange=exchange)
    return res[0] if exchange is None and parts == 1 else res


def weight_grad_stacked(a, b3, bn, name):
    t, bk = a.shape
    width = b3.shape[-1]
    piece = math.gcd(bn, width)
    rows = min(t, WGRAD_ROWS)
    n_steps = t // rows

    def body(a_ref, b_ref, o_hbm, acc, staged, sem):
        s, c = pl.program_id(0), pl.program_id(1)
        av = a_ref[...].astype(BF16)
        for chunk in range(N_CHIPS):
            @pl.when(c == chunk)
            def _(chunk=chunk):
                cols = [divmod(chunk * bn + k * piece, width) for k in range(bn // piece)]
                b = jnp.concatenate([b_ref[p, :, lo:lo + piece] for p, lo in cols], axis=1).astype(BF16)
                _accumulate(acc.at[chunk], _dot_tn(av, b), s == 0)

                @pl.when(s == n_steps - 1)
                def _():
                    staged[...] = acc[chunk].astype(BF16)
                    copy = pltpu.make_async_copy(staged, o_hbm.at[chunk], sem)
                    copy.start()
                    copy.wait()

    return pl.pallas_call(
        body, name=name, grid=(n_steps, N_CHIPS),
        in_specs=[pl.BlockSpec((rows, bk), lambda s, c: (s, 0)),
                  pl.BlockSpec((b3.shape[0], rows, width), lambda s, c: (0, s, 0))],
        out_specs=ANY,
        out_shape=jax.ShapeDtypeStruct((N_CHIPS, bk, bn), BF16),
        scratch_shapes=[pltpu.VMEM((N_CHIPS, bk, bn), F32), pltpu.VMEM((bk, bn), BF16), pltpu.SemaphoreType.DMA],
        compiler_params=_params(("arbitrary", "arbitrary")),
    )(a, b3)


def _hgrn2_chunk(st, qs, fls, ivs, gls, l0, l1, l2, ng):
    nsub = len(qs)
    mx = jnp.maximum(jnp.maximum(l0, l1), l2)
    e0, e1, e2 = jnp.exp(l0 - mx), jnp.exp(l1 - mx), jnp.exp(l2 - mx)
    lb = e0 / (e0 + e1 + e2)
    rows = lax.broadcasted_iota(jnp.int32, (A_SUB, A_SUB), 0)
    cols = lax.broadcasted_iota(jnp.int32, (A_SUB, A_SUB), 1)
    tri = (rows >= cols).astype(F32)
    keep = (lax.broadcasted_iota(jnp.int32, (A_SUB, A_SUB, A_DK), 0)
            >= lax.broadcasted_iota(jnp.int32, (A_SUB, A_SUB, A_DK), 1))
    base = jnp.zeros_like(l0)
    bases, gs, ks, qfs = [], [], [], []
    for i in range(nsub):
        f = lb + (1.0 - lb) * jax.nn.sigmoid(fls[i])
        logf = jnp.log(f)
        bases.append(base)
        gs.append(base + jnp.dot(tri, logf, precision=lax.Precision.HIGHEST, preferred_element_type=F32))
        base = base + jnp.sum(logf, axis=0, keepdims=True)
        ks.append(1.0 - f)
        qfs.append(jax.nn.silu(qs[i]))
    g_last = base
    stb = st.astype(BF16)
    outs = []
    for i in range(nsub):
        o = _dot_nt((qfs[i] * jnp.exp(gs[i])).astype(BF16), stb)
        if i > 0:
            qt = (qfs[i] * jnp.exp(gs[i] - bases[i])).astype(BF16)
            kk = jnp.concatenate([ks[j] * jnp.exp(bases[i] - gs[j]) for j in range(i)], axis=0).astype(BF16)
            vv = jnp.concatenate(ivs[:i], axis=0).astype(BF16)
            o = o + _dot(_dot_nt(qt, kk).astype(BF16), vv)
        dec = jnp.exp(jnp.where(keep, gs[i][:, None, :] - gs[i][None, :, :], NEG_BIG))
        s_diag = jnp.sum(qfs[i][:, None, :] * ks[i][None, :, :] * dec, axis=-1)
        o = o + _dot(s_diag.astype(BF16), ivs[i].astype(BF16))
        o = o * lax.rsqrt(jnp.mean(o * o, axis=-1, keepdims=True) + EPS) * ng
        outs.append(o * jax.nn.silu(gls[i]))
    kdec = jnp.concatenate([ks[j] * jnp.exp(g_last - gs[j]) for j in range(nsub)], axis=0).astype(BF16)
    vall = jnp.concatenate(ivs, axis=0).astype(BF16)
    new_st = st * jnp.exp(g_last) + _dot_tn(vall, kdec)
    return new_st, outs


A_MAX_LOG_DECAY = 60.0


def _half_sums(logf):
    n = logf.shape[0]
    first = lax.broadcasted_iota(jnp.int32, logf.shape, 0) < n // 2
    return (jnp.sum(jnp.where(first, logf, 0.0), axis=0, keepdims=True),
            jnp.sum(jnp.where(first, 0.0, logf), axis=0, keepdims=True))


def _split3(x):
    hi = x.astype(BF16)
    r1 = x - hi.astype(F32)
    mid = r1.astype(BF16)
    return hi, mid, (r1 - mid.astype(F32)).astype(BF16)


def _tri_matmul(x, transpose):
    n = x.shape[0]
    r = lax.broadcasted_iota(jnp.int32, (n, n), 0)
    c = lax.broadcasted_iota(jnp.int32, (n, n), 1)
    tri = ((r <= c) if transpose else (r >= c)).astype(BF16)
    hi, mid, lo = _split3(x)
    return (_dot(tri, lo) + _dot(tri, mid)) + _dot(tri, hi)


@jax.custom_vjp
def _cumsum_rows(x):
    return _tri_matmul(x, False)


def _cumsum_rows_fwd(x):
    return _tri_matmul(x, False), None


def _cumsum_rows_bwd(_, dy):
    return (_tri_matmul(dy, True),)


_cumsum_rows.defvjp(_cumsum_rows_fwd, _cumsum_rows_bwd)


def _lower_bound(l0, l1, l2):
    mx = jnp.maximum(jnp.maximum(l0, l1), l2)
    e0, e1, e2 = jnp.exp(l0 - mx), jnp.exp(l1 - mx), jnp.exp(l2 - mx)
    return e0 / (e0 + e1 + e2)


def _b(x):
    return x.astype(BF16)


@jax.custom_vjp
def _mm(a, b):
    return _dot(_b(a), _b(b))


_mm.defvjp(lambda a, b: (_mm(a, b), (a, b)),
           lambda res, d: (_dot_nt(_b(d), _b(res[1])), _dot_tn(_b(res[0]), _b(d))))


@jax.custom_vjp
def _mm_nt(a, b):
    return _dot_nt(_b(a), _b(b))


_mm_nt.defvjp(lambda a, b: (_mm_nt(a, b), (a, b)),
              lambda res, d: (_dot(_b(d), _b(res[1])), _dot_tn(_b(d), _b(res[0]))))


def _dot_split(dot, a, b):
    ah, bh = _b(a), _b(b)
    al, bl = _b(a - ah.astype(F32)), _b(b - bh.astype(F32))
    return (dot(ah, bl) + dot(al, bh)) + dot(ah, bh)


@jax.custom_vjp
def _mm_scores(a, b):
    return _dot_nt(_b(a), _b(b))


_mm_scores.defvjp(lambda a, b: (_mm_scores(a, b), (a, b)),
                  lambda res, d: (_dot_split(_dot, d, res[1]), _dot_split(_dot_tn, d, res[0])))


@jax.custom_vjp
def _mm_tn(a, b):
    return _dot_tn(_b(a), _b(b))


_mm_tn.defvjp(lambda a, b: (_mm_tn(a, b), (a, b)),
              lambda res, d: (_dot_nt(_b(res[1]), _b(d)), _dot(_b(res[0]), _b(d))))


@jax.custom_vjp
def _split_heads(x):
    return tuple(x[:, h * A_DK:(h + 1) * A_DK] for h in range(A_HEADS))


def _split_heads_fwd(x):
    return _split_heads(x), None


def _split_heads_bwd(_, parts):
    return (jnp.concatenate(parts, axis=1),)


_split_heads.defvjp(_split_heads_fwd, _split_heads_bwd)


def _hgrn2_chunk_fast(sts, q, fl, iv, gl, l0, l1, l2, ng):
    lb = _lower_bound(l0, l1, l2)
    f = lb + (1.0 - lb) * jax.nn.sigmoid(fl)
    return _hgrn2_fast_core(sts, q, f, jnp.log(f), iv, gl, ng)


def _hgrn2_fast_core(sts, q, f, logf, iv, gl, ng):
    g = _cumsum_rows(logf)
    g_mid, g_last = _half_sums(logf)
    g_last = g_mid + g_last
    k = 1.0 - f
    qf = jax.nn.silu(q)
    qms = _split_heads(qf * jnp.exp(g - g_mid))
    kms = _split_heads(k * jnp.exp(g_mid - g))
    qgs = _split_heads(qf * jnp.exp(g))
    kds = _split_heads(k * jnp.exp(g_last - g))
    ivs = _split_heads(iv)
    decays = _split_heads(jnp.exp(g_last))
    n = q.shape[0]
    causal = lax.broadcasted_iota(jnp.int32, (n, n), 0) >= lax.broadcasted_iota(jnp.int32, (n, n), 1)
    raw = [_mm_scores(qm, km) for qm, km in zip(qms, kms)]
    inter = [_mm_nt(qg, st) for qg, st in zip(qgs, sts)]
    scores = [jnp.where(causal, s, 0.0) for s in raw]
    os = [a + _mm(s, v) for a, s, v in zip(inter, scores, ivs)]
    new_sts = [st * d + _mm_tn(v, kd) for st, d, v, kd in zip(sts, decays, ivs, kds)]
    os = [o * lax.rsqrt(jnp.mean(o * o, axis=-1, keepdims=True) + EPS) for o in os]
    return new_sts, jnp.concatenate(os, axis=1) * ng * jax.nn.silu(gl)


A_STEP_CHUNKS = 4


def _chunk_rows(j):
    return pl.ds(pl.multiple_of(j * A_CHUNK, A_CHUNK), A_CHUNK)


def _sub_rows(j, i):
    return pl.ds(pl.multiple_of(j * A_CHUNK + i * A_SUB, A_SUB), A_SUB)


def _sub_blocks(ref, head, j):
    lanes = slice(head * A_DK, (head + 1) * A_DK)
    return [ref[_sub_rows(j, i), lanes] for i in range(A_CHUNK // A_SUB)]


def hgrn2_fwd(proj, lb_table, a_norm, batch, name, exchange=None):
    t = proj.shape[0]
    n_steps = t // batch // (A_CHUNK * A_STEP_CHUNKS)
    rows = A_CHUNK * A_STEP_CHUNKS

    def body(q_ref, f_ref, i_ref, g_ref, lb_ref, ng_ref, o_ref, st_ref, dec_ref, st):
        @pl.when(pl.program_id(1) == 0)
        def _():
            st[...] = jnp.zeros_like(st)

        def chunk(j, carry):
            r = _chunk_rows(j)
            st_ref[j] = st[...]
            lb = _lower_bound(lb_ref[0:1, :], lb_ref[1:2, :], lb_ref[2:3, :])
            f = lb + (1.0 - lb) * jax.nn.sigmoid(f_ref[r, :])
            logf = jnp.log(f)
            decay = jnp.minimum(*_half_sums(logf))
            dec_ref[j] = decay
            mild = jnp.min(decay) >= -A_MAX_LOG_DECAY

            @pl.when(mild)
            def _():
                new_sts, o = _hgrn2_fast_core([st[h] for h in range(A_HEADS)], q_ref[r, :], f, logf,
                                              i_ref[r, :], g_ref[r, :], ng_ref[...])
                for h in range(A_HEADS):
                    st[h] = new_sts[h]
                o_ref[r, :] = o.astype(BF16)

            @pl.when(jnp.logical_not(mild))
            def _():
                for h in range(A_HEADS):
                    lanes = slice(h * A_DK, (h + 1) * A_DK)
                    new_st, outs = _hgrn2_chunk(
                        st[h], _sub_blocks(q_ref, h, j), _sub_blocks(f_ref, h, j), _sub_blocks(i_ref, h, j),
                        _sub_blocks(g_ref, h, j), lb_ref[0:1, lanes], lb_ref[1:2, lanes], lb_ref[2:3, lanes],
                        ng_ref[:, lanes])
                    st[h] = new_st
                    for i, o in enumerate(outs):
                        o_ref[_sub_rows(j, i), lanes] = o.astype(BF16)

            return carry

        lax.fori_loop(0, A_STEP_CHUNKS, chunk, 0)

    def part(k):
        return pl.BlockSpec((rows, A_WIDTH), lambda b, n: (b * n_steps + n, k))

    return _call(
        body, name=name, grid=(batch, n_steps),
        in_specs=[part(0), part(1), part(2), part(3),
                  pl.BlockSpec((3, A_WIDTH), lambda b, n: (0, 0)), pl.BlockSpec((1, A_WIDTH), lambda b, n: (0, 0))],
        out_specs=[part(0),
                   pl.BlockSpec((A_STEP_CHUNKS, A_HEADS, A_DK, A_DK), lambda b, n: (b * n_steps + n, 0, 0, 0)),
                   pl.BlockSpec((A_STEP_CHUNKS, 1, A_WIDTH), lambda b, n: (b * n_steps + n, 0, 0))],
        out_shape=[jax.ShapeDtypeStruct((t, A_WIDTH), BF16),
                   jax.ShapeDtypeStruct((t // A_CHUNK, A_HEADS, A_DK, A_DK), F32),
                   jax.ShapeDtypeStruct((t // A_CHUNK, 1, A_WIDTH), F32)],
        scratch_shapes=[pltpu.VMEM((A_HEADS, A_DK, A_DK), F32)],
        args=(proj, proj, proj, proj, lb_table, a_norm), exchange=exchange)


def hgrn2_bwd(proj, states, decays, lb_table, a_norm, do, batch, name, exchange=None):
    t = proj.shape[0]
    n_steps = t // batch // (A_CHUNK * A_STEP_CHUNKS)
    rows = A_CHUNK * A_STEP_CHUNKS

    def body(q_ref, f_ref, i_ref, g_ref, st_ref, dec_ref, lb_ref, ng_ref, do_ref, dp_ref, dlb_ref, dng_ref, dst):
        @pl.when(jnp.logical_and(pl.program_id(0) == 0, pl.program_id(1) == 0))
        def _():
            dlb_ref[...] = jnp.zeros_like(dlb_ref)
            dng_ref[...] = jnp.zeros_like(dng_ref)

        @pl.when(pl.program_id(1) == 0)
        def _():
            dst[...] = jnp.zeros_like(dst)

        def chunk(jj, carry):
            j = A_STEP_CHUNKS - 1 - jj
            r = _chunk_rows(j)
            mild = jnp.min(dec_ref[j]) >= -A_MAX_LOG_DECAY

            @pl.when(mild)
            def _():
                _, vjp = jax.vjp(
                    _hgrn2_chunk_fast, [st_ref[j, h] for h in range(A_HEADS)], q_ref[r, :], f_ref[r, :],
                    i_ref[r, :], g_ref[r, :], lb_ref[0:1, :], lb_ref[1:2, :], lb_ref[2:3, :], ng_ref[...])
                d_sts, dq, df, di, dg, dl0, dl1, dl2, dng = vjp(
                    ([dst[h] for h in range(A_HEADS)], do_ref[r, :].astype(F32)))
                for h in range(A_HEADS):
                    dst[h] = d_sts[h]
                for k, part in enumerate((dq, df, di, dg)):
                    dp_ref[r, k * A_WIDTH:(k + 1) * A_WIDTH] = part
                for row, val in enumerate((dl0, dl1, dl2)):
                    dlb_ref[row:row + 1, :] += val
                dng_ref[...] += dng

            @pl.when(jnp.logical_not(mild))
            def _():
                for h in range(A_HEADS):
                    lanes = slice(h * A_DK, (h + 1) * A_DK)
                    _, vjp = jax.vjp(
                        _hgrn2_chunk, st_ref[j, h], _sub_blocks(q_ref, h, j), _sub_blocks(f_ref, h, j),
                        _sub_blocks(i_ref, h, j), _sub_blocks(g_ref, h, j), lb_ref[0:1, lanes], lb_ref[1:2, lanes],
                        lb_ref[2:3, lanes], ng_ref[:, lanes])
                    douts = [x.astype(F32) for x in _sub_blocks(do_ref, h, j)]
                    d_st, dqs, dfs, dis, dgs, dl0, dl1, dl2, dng = vjp((dst[h], douts))
                    dst[h] = d_st
                    for k, parts in enumerate((dqs, dfs, dis, dgs)):
                        for i in range(A_CHUNK // A_SUB):
                            dp_ref[_sub_rows(j, i), k * A_WIDTH + h * A_DK:k * A_WIDTH + (h + 1) * A_DK] = parts[i]
                    for row, val in enumerate((dl0, dl1, dl2)):
                        dlb_ref[row:row + 1, lanes] += val
                    dng_ref[:, lanes] += dng

            return carry

        lax.fori_loop(0, A_STEP_CHUNKS, chunk, 0)

    def rev(b, n):
        return b * n_steps + (n_steps - 1 - n)

    def part(k):
        return pl.BlockSpec((rows, A_WIDTH), lambda b, n: (rev(b, n), k))

    const3 = pl.BlockSpec((3, A_WIDTH), lambda b, n: (0, 0))
    const1 = pl.BlockSpec((1, A_WIDTH), lambda b, n: (0, 0))
    return _call(
        body, name=name, grid=(batch, n_steps),
        in_specs=[part(0), part(1), part(2), part(3),
                  pl.BlockSpec((A_STEP_CHUNKS, A_HEADS, A_DK, A_DK), lambda b, n: (rev(b, n), 0, 0, 0)),
                  pl.BlockSpec((A_STEP_CHUNKS, 1, A_WIDTH), lambda b, n: (rev(b, n), 0, 0)),
                  const3, const1, part(0)],
        out_specs=[pl.BlockSpec((rows, 4 * A_WIDTH), lambda b, n: (rev(b, n), 0)), const3, const1],
        out_shape=[jax.ShapeDtypeStruct((t, 4 * A_WIDTH + 2 * B_WIDTH), F32),
                   jax.ShapeDtypeStruct((3, A_WIDTH), F32), jax.ShapeDtypeStruct((1, A_WIDTH), F32)],
        scratch_shapes=[pltpu.VMEM((A_HEADS, A_DK, A_DK), F32)],
        args=(proj, proj, proj, proj, states, decays, lb_table, a_norm, do), exchange=exchange)


B_GDIM = B_WIDTH // B_GROUPS
B_ROWS = 512


def _gmlp_chunk(ubs, vbs, lngs, lnbs, ws, bcols):
    vs = [jax.nn.gelu(v) for v in vbs]
    mu = sum(jnp.sum(v, axis=-1, keepdims=True) for v in vs) * (1.0 / B_WIDTH)
    var = sum(jnp.sum(jnp.square(v - mu), axis=-1, keepdims=True) for v in vs) * (1.0 / B_WIDTH)
    rstd = lax.rsqrt(var + EPS)
    tril = (lax.broadcasted_iota(jnp.int32, (B_CHUNK, B_CHUNK), 0)
            >= lax.broadcasted_iota(jnp.int32, (B_CHUNK, B_CHUNK), 1))
    outs = []
    for g in range(B_GROUPS):
        vn = (vs[g] - mu) * rstd * lngs[g] + lnbs[g]
        w = jnp.where(tril, ws[g], 0.0).astype(BF16)
        outs.append(jax.nn.gelu(ubs[g]) * (_dot(w, vn.astype(BF16)) + bcols[g]))
    return outs


def _gmlp_args(u_ref, v_ref, lng_ref, lnb_ref, w_ref, bt_ref, rows):
    def groups(ref):
        return [ref[rows, g * B_GDIM:(g + 1) * B_GDIM] for g in range(B_GROUPS)]

    def vec(ref):
        return [ref[:, g * B_GDIM:(g + 1) * B_GDIM] for g in range(B_GROUPS)]

    return (groups(u_ref), groups(v_ref), vec(lng_ref), vec(lnb_ref),
            [w_ref[g] for g in range(B_GROUPS)], [bt_ref[:, g:g + 1] for g in range(B_GROUPS)])


def gmlp_fwd(proj, oa, ln_g, ln_b, w, bias_t, name, exchange=None):
    t = proj.shape[0]

    def body(u_ref, v_ref, oa_ref, lng_ref, lnb_ref, w_ref, bt_ref, o_ref):
        o_ref[:, 0:A_WIDTH] = oa_ref[...]
        for n in range(B_ROWS // B_CHUNK):
            rows = slice(n * B_CHUNK, (n + 1) * B_CHUNK)
            outs = _gmlp_chunk(*_gmlp_args(u_ref, v_ref, lng_ref, lnb_ref, w_ref, bt_ref, rows))
            for g, o in enumerate(outs):
                o_ref[rows, A_WIDTH + g * B_GDIM:A_WIDTH + (g + 1) * B_GDIM] = o.astype(BF16)

    vec = pl.BlockSpec((1, B_WIDTH), lambda i: (0, 0))
    return _call(
        body, name=name, grid=(t // B_ROWS,),
        in_specs=[pl.BlockSpec((B_ROWS, B_WIDTH), lambda i: (i, 4)), pl.BlockSpec((B_ROWS, B_WIDTH), lambda i: (i, 5)),
                  pl.BlockSpec((B_ROWS, A_WIDTH), lambda i: (i, 0)), vec, vec,
                  pl.BlockSpec((B_GROUPS, B_CHUNK, B_CHUNK), lambda i: (0, 0, 0)),
                  pl.BlockSpec((B_CHUNK, B_GROUPS), lambda i: (0, 0))],
        out_specs=[pl.BlockSpec((B_ROWS, A_WIDTH + B_WIDTH), lambda i: (i, 0))],
        out_shape=[jax.ShapeDtypeStruct((t, A_WIDTH + B_WIDTH), BF16)],
        args=(proj, proj, oa, ln_g, ln_b, w, bias_t), exchange=exchange)


def gmlp_bwd(proj, dmixin, ln_g, ln_b, w, bias_t, dproj, name, exchange=None):
    t = proj.shape[0]

    def body(u_ref, v_ref, do_ref, lng_ref, lnb_ref, w_ref, bt_ref, dp_in_ref,
             dp_ref, dlng_ref, dlnb_ref, dw_ref, dbt_ref):
        del dp_in_ref

        @pl.when(pl.program_id(0) == 0)
        def _():
            for ref in (dlng_ref, dlnb_ref, dw_ref, dbt_ref):
                ref[...] = jnp.zeros_like(ref)

        for n in range(B_ROWS // B_CHUNK):
            rows = slice(n * B_CHUNK, (n + 1) * B_CHUNK)
            _, vjp = jax.vjp(_gmlp_chunk, *_gmlp_args(u_ref, v_ref, lng_ref, lnb_ref, w_ref, bt_ref, rows))
            douts = [do_ref[rows, g * B_GDIM:(g + 1) * B_GDIM] for g in range(B_GROUPS)]
            dus, dvs, dlngs, dlnbs, dws, dbs = vjp(douts)
            for g in range(B_GROUPS):
                lanes = slice(g * B_GDIM, (g + 1) * B_GDIM)
                dp_ref[rows, lanes] = dus[g]
                dp_ref[rows, B_WIDTH + g * B_GDIM:B_WIDTH + (g + 1) * B_GDIM] = dvs[g]
                dlng_ref[:, lanes] += dlngs[g]
                dlnb_ref[:, lanes] += dlnbs[g]
                dw_ref[g] += dws[g]
                dbt_ref[:, g:g + 1] += dbs[g]

    vec = pl.BlockSpec((1, B_WIDTH), lambda i: (0, 0))
    wspec = pl.BlockSpec((B_GROUPS, B_CHUNK, B_CHUNK), lambda i: (0, 0, 0))
    bspec = pl.BlockSpec((B_CHUNK, B_GROUPS), lambda i: (0, 0))
    return _call(
        body, name=name, grid=(t // B_ROWS,),
        in_specs=[pl.BlockSpec((B_ROWS, B_WIDTH), lambda i: (i, 4)), pl.BlockSpec((B_ROWS, B_WIDTH), lambda i: (i, 5)),
                  pl.BlockSpec((B_ROWS, B_WIDTH), lambda i: (i, 1)), vec, vec, wspec, bspec,
                  pl.BlockSpec(memory_space=pl.ANY)],
        out_specs=[pl.BlockSpec((B_ROWS, 2 * B_WIDTH), lambda i: (i, 2)), vec, vec, wspec, bspec],
        out_shape=[jax.ShapeDtypeStruct(dproj.shape, F32), jax.ShapeDtypeStruct((1, B_WIDTH), F32),
                   jax.ShapeDtypeStruct((1, B_WIDTH), F32), jax.ShapeDtypeStruct((B_GROUPS, B_CHUNK, B_CHUNK), F32),
                   jax.ShapeDtypeStruct((B_CHUNK, B_GROUPS), F32)],
        aliases={7: 0}, args=(proj, proj, dmixin, ln_g, ln_b, w, bias_t, dproj), exchange=exchange)


C_FWD_BLOCKS = 16
C_BWD_BLOCKS = 16
C_PAIR = 2 * C_HEAD_DIM
C_PAIRS = C_HEADS // 2
C_SCALE = 1.0 / math.sqrt(C_HEAD_DIM)
C_ROT_DIM = 2 * C_ROT_HALF
ROPE_ROWS = 1024


def rope_tables(pos_col, name):
    t = pos_col.shape[0]

    def body(p_ref, c_ref, a_ref, b_ref):
        lane = jnp.bitwise_and(lax.broadcasted_iota(jnp.int32, (1, C_PAIR), 1), C_HEAD_DIM - 1)
        j = jnp.bitwise_and(lane, C_ROT_HALF - 1).astype(F32)
        inv = jnp.exp(j * (-math.log(ROPE_THETA) / C_ROT_HALF))
        ang = p_ref[...].astype(F32) * inv
        cos, sin = jnp.cos(ang), jnp.sin(ang)
        c_ref[...] = jnp.where(lane < C_ROT_DIM, cos, 1.0)
        a_ref[...] = jnp.where(lane < C_ROT_HALF, -sin, 0.0)
        b_ref[...] = jnp.where(jnp.logical_and(lane >= C_ROT_HALF, lane < C_ROT_DIM), sin, 0.0)

    tab = pl.BlockSpec((ROPE_ROWS, C_PAIR), lambda i: (i, 0))
    return pl.pallas_call(
        body, name=name, grid=(t // ROPE_ROWS,),
        in_specs=[pl.BlockSpec((ROPE_ROWS, 1), lambda i: (i, 0))],
        out_specs=[tab, tab, tab],
        out_shape=[jax.ShapeDtypeStruct((t, C_PAIR), F32)] * 3,
        compiler_params=_params(("arbitrary",)),
    )(pos_col)


def _rope(x, c, a, b):
    return x * c + pltpu.roll(x, C_PAIR - C_ROT_HALF, 1) * a + pltpu.roll(x, C_ROT_HALF, 1) * b


def _rope_t(d, c, a, b):
    return d * c + pltpu.roll(d * a, C_ROT_HALF, 1) + pltpu.roll(d * b, C_PAIR - C_ROT_HALF, 1)


C_RES = 16


def _residue_major(a, batch):
    return a.reshape(batch, SEQ // C_RES, C_RES, -1).transpose(0, 2, 1, 3).reshape(a.shape)


def _sequence_order(a, batch):
    return a.reshape(batch, C_RES, SEQ // C_RES, -1).transpose(0, 2, 1, 3).reshape(a.shape)


def _block_pieces(idx, dil):
    nblk = SEQ // dil // C_BLOCK
    r, n = idx // nblk, idx % nblk
    per = C_RES // dil
    size = C_BLOCK // per

    def pieces(blk):
        return [((dil * a + r) * (SEQ // C_RES) + size * blk, size) for a in range(per)]

    return pieces(n), pieces(jnp.maximum(n - 1, 0)), n > 0


def _get_rows(ref, pieces):
    return jnp.concatenate([ref[pl.ds(pl.multiple_of(start, 8), size), :] for start, size in pieces], axis=0)


def _set_rows(ref, pieces, val, add=False):
    for k, (start, size) in enumerate(pieces):
        rows = pl.ds(pl.multiple_of(start, 8), size)
        part = val[k * size:(k + 1) * size]
        ref[rows, :] = ref[rows, :] + part if add else part


def _head_masks():
    low = lax.broadcasted_iota(jnp.int32, (1, C_PAIR), 1) < C_HEAD_DIM
    return low, jnp.logical_not(low)


def _attn_mask(has_prev, dil):
    per = C_RES // dil
    size = C_BLOCK // per

    def position(x):
        x = jnp.bitwise_and(x, C_BLOCK - 1)
        return per * jnp.bitwise_and(x, size - 1) + x // size

    j = lax.broadcasted_iota(jnp.int32, (2 * C_BLOCK, 2 * C_BLOCK), 1)
    pi = position(lax.broadcasted_iota(jnp.int32, (2 * C_BLOCK, 2 * C_BLOCK), 0))
    pj = position(j)
    own = j < C_BLOCK
    return jnp.logical_or(jnp.logical_and(own, pj <= pi),
                          jnp.logical_and(jnp.logical_and(jnp.logical_not(own), pj >= pi), has_prev))


def _stack_heads(x):
    low, high = _head_masks()
    return jnp.concatenate([jnp.where(low, x, 0.0), jnp.where(high, x, 0.0)], axis=0)


def _unstack_heads(x):
    low, _ = _head_masks()
    return jnp.where(low, x[:C_BLOCK], x[C_BLOCK:])


def attn_fwd(qkv, cos_t, sin_a, sin_b, batch, name, exchange=None):
    t = qkv.shape[0]
    nbr = len(C_DILATIONS)

    def body(q_ref, k_ref, v_ref, c_ref, a_ref, b_ref, o_ref, l_ref, qr_ref, kr_ref, qs, ks, *stats):
        acc, mm, dd = stats[0:nbr], stats[nbr:2 * nbr], stats[2 * nbr:3 * nbr]
        c, a, b = c_ref[...], a_ref[...], b_ref[...]
        qs[...] = _rope(q_ref[...], c, a, b) * C_SCALE
        ks[...] = _rope(k_ref[...], c, a, b)
        qr_ref[...] = qs[...].astype(BF16)
        kr_ref[...] = ks[...].astype(BF16)

        def load(idx, dil):
            own, prev, has_prev = _block_pieces(idx, dil)
            return own, (has_prev, _get_rows(qs, own), _get_rows(ks, own), _get_rows(ks, prev),
                         _get_rows(v_ref, own), _get_rows(v_ref, prev))

        def scores(dil, has_prev, q, k_own, k_prev, v_own, v_prev):
            k_cat = jnp.concatenate([k_own, k_prev], axis=0).astype(BF16)
            return jnp.where(_attn_mask(has_prev, dil), _dot_nt(_stack_heads(q).astype(BF16), k_cat), NEG_BIG)

        def softmax(s):
            m = jnp.max(s, axis=-1, keepdims=True)
            p = jnp.exp(s - m)
            return p.astype(BF16), m, jnp.sum(p, axis=-1, keepdims=True)

        def values(pb, has_prev, q, k_own, k_prev, v_own, v_prev):
            low, high = _head_masks()
            v_cat = jnp.concatenate([v_own, v_prev], axis=0)
            p_wide = jnp.concatenate([pb[:C_BLOCK], pb[C_BLOCK:]], axis=1)
            v_tall = jnp.concatenate([jnp.where(low, v_cat, 0.0), jnp.where(high, v_cat, 0.0)], axis=0).astype(BF16)
            return _dot(p_wide, v_tall)

        for bi, dil in enumerate(C_DILATIONS):
            def pair(i, carry, bi=bi, dil=dil):
                low, _ = _head_masks()
                loaded = [load(C_FWD_BLOCKS * i + k, dil) for k in range(C_FWD_BLOCKS)]
                ss = [scores(dil, *ops) for _, ops in loaded]
                sm = [softmax(s) for s in ss]
                pvs = [values(pb, *ops) for (pb, _, _), (_, ops) in zip(sm, loaded)]
                for (own, _), (_, m, den), pv in zip(loaded, sm, pvs):
                    _set_rows(acc[bi], own, pv)
                    _set_rows(mm[bi], own, jnp.where(low, m[:C_BLOCK], m[C_BLOCK:]))
                    _set_rows(dd[bi], own, jnp.where(low, den[:C_BLOCK], den[C_BLOCK:]))
                return carry

            lax.fori_loop(0, SEQ // C_BLOCK // C_FWD_BLOCKS, pair, 0)
        step = 2 * C_BLOCK
        for r0 in range(0, SEQ, step):
            rr = slice(r0, r0 + step)
            ms = [mm[g][rr, :] for g in range(nbr)]
            m_all = functools.reduce(jnp.maximum, ms)
            ws = [jnp.exp(m - m_all) for m in ms]
            num = sum(acc[g][rr, :] * ws[g] for g in range(nbr))
            den = sum(dd[g][rr, :] * ws[g] for g in range(nbr))
            o_ref[rr, :] = (num / den).astype(BF16)
            l_ref[rr, :] = m_all + jnp.log(den)

    def col(k):
        return pl.BlockSpec((SEQ, C_PAIR), lambda b, p: (b, k * C_PAIRS + p))

    tab = pl.BlockSpec((SEQ, C_PAIR), lambda b, p: (b, 0))
    return _call(
        body, name=name, grid=(batch, C_PAIRS),
        in_specs=[col(0), col(1), col(2), tab, tab, tab],
        out_specs=[col(0), col(0), col(0), col(0)],
        out_shape=[jax.ShapeDtypeStruct((t, D_MODEL), BF16), jax.ShapeDtypeStruct((t, D_MODEL), F32),
                   jax.ShapeDtypeStruct((t, D_MODEL), BF16), jax.ShapeDtypeStruct((t, D_MODEL), BF16)],
        scratch_shapes=[pltpu.VMEM((SEQ, C_PAIR), F32)] * (2 + 3 * nbr),
        args=(qkv, qkv, qkv, cos_t, sin_a, sin_b), exchange=exchange)


def attn_bwd(qr, kr, qkv, cos_t, sin_a, sin_b, o, lse, do, batch, name, exchange=None):
    t = qkv.shape[0]

    def body(q_ref, k_ref, v_ref, c_ref, a_ref, b_ref, o_ref, l_ref, do_ref, dqkv_ref, qs, ks, dqs, dks, dvs, dlt):
        low, _ = _head_masks()
        c, a, b = c_ref[...], a_ref[...], b_ref[...]
        qs[...] = q_ref[...].astype(F32)
        ks[...] = k_ref[...].astype(F32)
        prod = do_ref[...] * o_ref[...].astype(F32)
        s_low = jnp.sum(jnp.where(low, prod, 0.0), axis=-1, keepdims=True)
        s_all = jnp.sum(prod, axis=-1, keepdims=True)
        dlt[...] = jnp.where(low, s_low, s_all - s_low)
        dqs[...] = jnp.zeros_like(dqs)
        dks[...] = jnp.zeros_like(dks)
        dvs[...] = jnp.zeros_like(dvs)

        def load(idx, dil):
            own, prev, has_prev = _block_pieces(idx, dil)
            return (own, prev), (has_prev, _get_rows(qs, own), _get_rows(do_ref, own), _get_rows(ks, own),
                                 _get_rows(ks, prev), _get_rows(v_ref, own), _get_rows(v_ref, prev),
                                 _get_rows(l_ref, own), _get_rows(dlt, own))

        def operands(dil, has_prev, q, do, k_own, k_prev, v_own, v_prev, l_full, d_full):
            lcol = jnp.concatenate([l_full[:, 0:1], l_full[:, C_HEAD_DIM:C_HEAD_DIM + 1]], axis=0)
            dcol = jnp.concatenate([d_full[:, 0:1], d_full[:, C_HEAD_DIM:C_HEAD_DIM + 1]], axis=0)
            return (_stack_heads(q).astype(BF16), _stack_heads(do).astype(BF16),
                    jnp.concatenate([k_own, k_prev], axis=0).astype(BF16),
                    jnp.concatenate([v_own, v_prev], axis=0).astype(BF16), lcol, dcol, _attn_mask(has_prev, dil))

        for dil in C_DILATIONS:
            def pair(i, carry, dil=dil):
                loaded = [load(C_BWD_BLOCKS * i + k, dil) for k in range(C_BWD_BLOCKS)]
                ops = [operands(dil, *o) for _, o in loaded]
                ss = [_dot_nt(q_stack, k_cat) for q_stack, _, k_cat, _, _, _, _ in ops]
                dps = [_dot_nt(do_stack, v_cat) for _, do_stack, _, v_cat, _, _, _ in ops]
                ps = [jnp.exp(jnp.where(o[6], s, NEG_BIG) - o[4]) for s, o in zip(ss, ops)]
                dss = [(p * (dp - o[5])).astype(BF16) for p, dp, o in zip(ps, dps, ops)]
                dvs_ = [_dot_tn(p.astype(BF16), o[1]) for p, o in zip(ps, ops)]
                dks_ = [_dot_tn(ds, o[0]) for ds, o in zip(dss, ops)]
                dqs_ = [_unstack_heads(_dot(ds, o[2])) for ds, o in zip(dss, ops)]
                for ((own, prev), _), dq, dk_cat, dv_cat in zip(loaded, dqs_, dks_, dvs_):
                    _set_rows(dqs, own, dq, add=True)
                    _set_rows(dks, own, dk_cat[:C_BLOCK], add=True)
                    _set_rows(dvs, own, dv_cat[:C_BLOCK], add=True)
                    _set_rows(dks, prev, dk_cat[C_BLOCK:], add=True)
                    _set_rows(dvs, prev, dv_cat[C_BLOCK:], add=True)
                return carry

            lax.fori_loop(0, SEQ // C_BLOCK // C_BWD_BLOCKS, pair, 0)
        dqkv_ref[0] = _rope_t(dqs[...] * C_SCALE, c, a, b).astype(BF16)
        dqkv_ref[1] = _rope_t(dks[...], c, a, b).astype(BF16)
        dqkv_ref[2] = dvs[...].astype(BF16)

    def col(k):
        return pl.BlockSpec((SEQ, C_PAIR), lambda b, p: (b, k * C_PAIRS + p))

    tab = pl.BlockSpec((SEQ, C_PAIR), lambda b, p: (b, 0))
    return _call(
        body, name=name, grid=(batch, C_PAIRS),
        in_specs=[col(0), col(0), col(2), tab, tab, tab, col(0), col(0), col(0)],
        out_specs=[pl.BlockSpec((3, SEQ, C_PAIR), lambda b, p: (0, b, p))],
        out_shape=[jax.ShapeDtypeStruct((3, t, D_MODEL), BF16)],
        scratch_shapes=[pltpu.VMEM((SEQ, C_PAIR), F32)] * 6,
        args=(qr, kr, qkv, cos_t, sin_a, sin_b, o, lse, do), exchange=exchange)


def allreduce_small(slab, name):
    rows, lanes = slab.shape

    def body(x_ref, out_ref, gath, send_sems, recv_sems, local_sem):
        x, y, c, chips = _place()
        me, sibling = (x, y, c), (x, y, 1 - c)

        def slot(px, py, pc):
            return gath.at[4 * px + 2 * py + pc]

        def copy(k, block, to, src=None):
            return pltpu.make_async_remote_copy(
                src_ref=slot(*block) if src is None else src, dst_ref=slot(*block),
                send_sem=send_sems.at[k], recv_sem=recv_sems.at[k], device_id=to, device_id_type=MESH)

        mine = pltpu.make_async_copy(x_ref, slot(*me), local_sem)
        mine.start()
        first = [copy(0, me, sibling, src=x_ref)]
        first += [copy(1 + j, me, (*chip, c), src=x_ref) for j, chip in enumerate(chips)]
        for cp in first:
            cp.start()
        passed = [copy(4 + j, (*chip, c), sibling) for j, chip in enumerate(chips)]
        for j, chip in enumerate(chips):
            copy(1 + j, (*chip, c), me).wait_recv()
            passed[j].start()
        copy(0, sibling, me).wait_recv()
        for j, chip in enumerate(chips):
            copy(4 + j, (*chip, 1 - c), me).wait_recv()
        for cp in first + passed:
            cp.wait_send()
        mine.wait()
        total = gath[0]
        for d in range(1, N_DEV):
            total = total + gath[d]
        out_ref[...] = total

    return pl.pallas_call(
        body, name=name,
        in_specs=[pl.BlockSpec(memory_space=pltpu.VMEM)],
        out_specs=pl.BlockSpec(memory_space=pltpu.VMEM),
        out_shape=jax.ShapeDtypeStruct((rows, lanes), F32),
        scratch_shapes=[pltpu.VMEM((N_DEV, rows, lanes), F32),
                        pltpu.SemaphoreType.DMA((7,)), pltpu.SemaphoreType.DMA((7,)), pltpu.SemaphoreType.DMA],
    )(slab)


ELT_ROWS = 512


def reduce_slabs(r, name):
    rs = [p.reshape(N_CHIPS, -1, p.shape[-1]) for p in (r if isinstance(r, (list, tuple)) else [r])]
    parts = len(rs)
    _, rows, cols = rs[0].shape
    br = min(rows, ELT_ROWS)
    nb = rows // br

    def total(r_ref):
        return ((r_ref[3].astype(F32) + r_ref[0].astype(F32)) + r_ref[1].astype(F32)) + r_ref[2].astype(F32)

    def body(*refs):
        r_refs, o_ref = refs[:-1], refs[-1]
        p = pl.program_id(0)
        out = total(r_refs[0])
        for k in range(1, parts):
            out = jnp.where(p == k, total(r_refs[k]), out)
        o_ref[...] = out

    def part_spec(k):
        return pl.BlockSpec((N_CHIPS, br, cols), lambda p, i: (0, jnp.where(p == k, i, jnp.where(p < k, 0, nb - 1)), 0))

    return pl.pallas_call(
        body, name=name, grid=(parts, nb),
        in_specs=[part_spec(k) for k in range(parts)],
        out_specs=pl.BlockSpec((br, cols), lambda p, i: (p * nb + i, 0)),
        out_shape=jax.ShapeDtypeStruct((parts * rows, cols), F32),
        compiler_params=_params(("arbitrary", "arbitrary")),
    )(*rs)


def _adamw(w, g, m, v):
    m = ADAM_B1 * m + (1.0 - ADAM_B1) * g
    v = ADAM_B2 * v + (1.0 - ADAM_B2) * jnp.square(g)
    m_hat = m / (1.0 - ADAM_B1 ** ADAM_STEP)
    v_hat = v / (1.0 - ADAM_B2 ** ADAM_STEP)
    delta = -ADAM_LR * (m_hat / (jnp.sqrt(v_hat) + ADAM_EPS) + ADAM_WD * w)
    return delta, m, v


def adamw_big(w, s_mine, s_sibling, m, v, name, exchange=None):
    rows, cols = w.shape
    parts = len(s_mine)
    br = min(rows // parts, ELT_ROWS)
    nb = rows // parts // br

    def body(w_ref, m_ref, v_ref, *rest):
        sums, (g_out, d_out, m_out, v_out) = rest[:2 * parts], rest[2 * parts:]
        p = pl.program_id(0)
        g = sums[0][...] + sums[parts][...]
        for k in range(1, parts):
            g = jnp.where(p == k, sums[k][...] + sums[parts + k][...], g)
        g_out[...] = g
        d_out[...], m_out[...], v_out[...] = _adamw(w_ref[...], g, m_ref[...], v_ref[...])

    def part_spec(k):
        return pl.BlockSpec((br, cols), lambda p, i: (jnp.where(p == k, i, jnp.where(p < k, 0, nb - 1)), 0))

    blk = pl.BlockSpec((br, cols), lambda p, i: (p * nb + i, 0))
    out = jax.ShapeDtypeStruct((rows, cols), F32)
    return _call(
        body, name=name, grid=(parts, nb),
        in_specs=[blk] * 3 + [part_spec(k) for k in range(parts)] * 2, out_specs=[blk] * 4, out_shape=[out] * 4,
        args=(w, m, v, *s_mine, *s_sibling), exchange=exchange)


def adamw_small(ws, gs, ms, vs, name):
    n = len(ws)

    def body(*refs):
        w_refs, g_refs, m_refs, v_refs = (refs[k * n:(k + 1) * n] for k in range(4))
        d_out, m_out, v_out = (refs[(4 + k) * n:(5 + k) * n] for k in range(3))
        for i in range(n):
            d_out[i][...], m_out[i][...], v_out[i][...] = _adamw(
                w_refs[i][...], g_refs[i][...], m_refs[i][...], v_refs[i][...])

    outs = [jax.ShapeDtypeStruct(w.shape, F32) for w in ws]
    res = pl.pallas_call(body, name=name, out_shape=outs * 3)(*ws, *gs, *ms, *vs)
    return res[:n], res[n:2 * n], res[2 * n:]


SLAB_LANES = 128
SLAB_ROW_ALIGN = 8


def _pack(parts):
    flat = jnp.concatenate([p.reshape(-1) for p in parts])
    rows = -(-flat.shape[0] // (SLAB_LANES * SLAB_ROW_ALIGN)) * SLAB_ROW_ALIGN
    flat = jnp.pad(flat, (0, rows * SLAB_LANES - flat.shape[0]))
    return flat.reshape(rows, SLAB_LANES)


def _unpack(slab, shapes):
    flat = slab.reshape(-1)
    out, pos = [], 0
    for s in shapes:
        size = math.prod(s)
        out.append(flat[pos:pos + size].reshape(s))
        pos += size
    return out


def kernel(x, positions, norm_mix_pre, norm_mix_post, norm_ffn_pre, norm_ffn_post, w_in_even, lb_table, a_norm, b_ln_g, b_ln_b, b_ws, b_bias, w_out_even, w_in_odd, w_out_odd, w_ff1, w_ff2, loss_target, m_norm_mix_pre, m_norm_mix_post, m_norm_ffn_pre, m_norm_ffn_post, m_w_in_even, m_lb_table, m_a_norm, m_b_ln_g, m_b_ln_b, m_b_ws, m_b_bias, m_w_out_even, m_w_in_odd, m_w_out_odd, m_w_ff1, m_w_ff2, v_norm_mix_pre, v_norm_mix_post, v_norm_ffn_pre, v_norm_ffn_post, v_w_in_even, v_lb_table, v_a_norm, v_b_ln_g, v_b_ln_b, v_b_ws, v_b_bias, v_w_out_even, v_w_in_odd, v_w_out_odd, v_w_ff1, v_w_ff2):
    batch = x.shape[0]
    t = batch * SEQ
    d = D_MODEL
    x0 = x.reshape(t, d)
    target = loss_target.reshape(t, d)

    def gain(p, layer):
        return p[layer:layer + 1]

    def gather(*shards):
        return _Exchange("gather", [w.astype(BF16) for w in shards])

    def scatter(*grads):
        return _Exchange("scatter", grads)

    (win_e,) = exchange_alone(gather(w_in_even[0]), "gather_in_even")
    bias_t = b_bias[0].T
    proj, h0, w1_0 = norm_matmul(x0, gain(norm_mix_pre, 0), win_e, "in_proj_even", exchange=gather(w_ff1[0]))
    oa, states, decays, w2_0 = hgrn2_fwd(proj, lb_table, a_norm, batch, "hgrn2_fwd", exchange=gather(w_ff2[0]))
    mixin, wout_e = gmlp_fwd(proj, oa, b_ln_g, b_ln_b, b_ws[0], bias_t, "gmlp_fwd", exchange=gather(w_out_even[0]))
    mix0, x1 = out_proj(mixin, wout_e, x0, gain(norm_mix_post, 0), "out_proj_even")
    x2, hf0, a0, y0, win_o, wout_o = ffn_fwd(x1, gain(norm_ffn_pre, 0), w1_0, w2_0, gain(norm_ffn_post, 0),
                                             "ffn_fwd_0", exchange=gather(w_in_odd[0], w_out_odd[0]))
    x2p = _residue_major(x2, batch)
    qkv, h1 = norm_matmul(x2p, gain(norm_mix_pre, 1), win_o, "in_proj_odd")
    cos_t, sin_a, sin_b = rope_tables(_residue_major(positions.reshape(t, 1), batch), "rope_tables")
    ao, lse, q_rot, k_rot, w1_1, w2_1 = attn_fwd(qkv, cos_t, sin_a, sin_b, batch, "attn_fwd",
                                                 exchange=gather(w_ff1[1], w_ff2[1]))
    mix1, x3 = out_proj(ao, wout_o, x2p, gain(norm_mix_post, 1), "out_proj_odd")
    dx4, hf1, a1, y1, loss_part = ffn_fwd(x3, gain(norm_ffn_pre, 1), w1_1, w2_1, gain(norm_ffn_post, 1),
                                          "ffn_fwd_1", target=_residue_major(target, batch))

    hc = D_FF // N_CHIPS
    dx3, dy1, da1, dg_fpre1, dg_fpost1 = ffn_bwd(
        dx4, x3, y1, a1, gain(norm_ffn_pre, 1), gain(norm_ffn_post, 1), w1_1, w2_1, "ffn_bwd_1")
    g_w1_1 = weight_grad(hf1, da1, "b", d, hc, False, "wgrad_ff1_1")
    g_w2_1 = weight_grad(a1, dy1, "a", hc, d, True, "wgrad_ff2_1")
    dmix1, dao, dg_mpost1 = out_proj_bwd(dx3, mix1, gain(norm_mix_post, 1), wout_o, "out_proj_bwd_odd")
    g_wout_o = weight_grad(ao, dmix1, "a", d // N_CHIPS, d, False, "wgrad_out_odd")
    dqkv, r_w1_1, r_w2_1, r_wout_o = attn_bwd(q_rot, k_rot, qkv, cos_t, sin_a, sin_b, ao, lse, dao, batch, "attn_bwd",
                                              exchange=scatter(g_w1_1, g_w2_1, g_wout_o))
    dx2p, dg_mpre1 = norm_matmul_bwd(dqkv, win_o, x2p, gain(norm_mix_pre, 1), dx3, "in_proj_bwd_odd")
    dx2 = _sequence_order(dx2p, batch)
    g_win_o = weight_grad_stacked(h1, dqkv, 3 * d // N_CHIPS, "wgrad_in_odd")
    s_w1_1, s_w2_1, s_wout_o = (reduce_slabs(r, n) for r, n in (
        (r_w1_1, "reduce_ff1_1"), (r_w2_1, "reduce_ff2_1"), (r_wout_o, "reduce_out_odd")))
    dx1, dy0, da0, dg_fpre0, dg_fpost0, r_win_o, t_w1_1, t_w2_1, t_wout_o = ffn_bwd(
        dx2, x1, y0, a0, gain(norm_ffn_pre, 0), gain(norm_ffn_post, 0), w1_0, w2_0, "ffn_bwd_0",
        exchange=_Both(scatter(g_win_o), _Swap([s_w1_1, s_w2_1, s_wout_o])))
    g_w1_0a, g_w1_0b = weight_grad(hf0, da0, "b", d, hc, False, "wgrad_ff1_0", parts=2)
    g_w2_0a, g_w2_0b, r_w1_0a = weight_grad(a0, dy0, "a", hc, d, True, "wgrad_ff2_0", exchange=scatter(g_w1_0a), parts=2)
    dmix0, dmixin, dg_mpost0 = out_proj_bwd(dx1, mix0, gain(norm_mix_post, 0), wout_e, "out_proj_bwd_even")
    g_wout_e = weight_grad(mixin, dmix0, "a", d // N_CHIPS, d, False, "wgrad_out_even")
    s_win_o = reduce_slabs(r_win_o, "reduce_in_odd")
    dproj, d_lb, d_anorm, r_w1_0b, r_w2_0a, t_win_o = hgrn2_bwd(
        proj, states, decays, lb_table, a_norm, dmixin, batch, "hgrn2_bwd",
        exchange=_Both(scatter(g_w1_0b, g_w2_0a), _Swap([s_win_o])))
    s_w1_0 = reduce_slabs([r_w1_0a, r_w1_0b], "reduce_ff1_0")
    dproj, d_lng, d_lnb, d_ws, d_bias_t, r_w2_0b, t_w1_0 = gmlp_bwd(
        proj, dmixin, b_ln_g, b_ln_b, b_ws[0], bias_t, dproj, "gmlp_bwd",
        exchange=_Both(scatter(g_w2_0b), _Swap([s_w1_0])))
    s_w2_0 = reduce_slabs([r_w2_0a, r_w2_0b], "reduce_ff2_0")
    g_win_ea, g_win_eb, r_wout_e, t_w2_0 = weight_grad(
        h0, dproj, "b", d, 3 * d // N_CHIPS, False, "wgrad_in_even",
        exchange=_Both(scatter(g_wout_e), _Swap([s_w2_0])), parts=2)
    s_wout_e = reduce_slabs(r_wout_e, "reduce_out_even")
    dx0, dg_mpre0, r_win_ea, t_wout_e = norm_matmul_bwd(
        dproj, win_e, x0, gain(norm_mix_pre, 0), dx1, "in_proj_bwd_even",
        exchange=_Both(scatter(g_win_ea), _Swap([s_wout_e])))
    grad_x = dx0.reshape(x.shape)

    big_w = [w_in_even, w_out_even, w_in_odd, w_out_odd, w_ff1, w_ff2]
    big_m = [m_w_in_even, m_w_out_even, m_w_in_odd, m_w_out_odd, m_w_ff1, m_w_ff2]
    big_v = [v_w_in_even, v_w_out_even, v_w_in_odd, v_w_out_odd, v_w_ff1, v_w_ff2]
    mine = [None, [s_wout_e], [s_win_o], [s_wout_o], [s_w1_0, s_w1_1], [s_w2_0, s_w2_1]]
    theirs = [None, [t_wout_e], [t_win_o], [t_wout_o], [t_w1_0, t_w1_1], [t_w2_0, t_w2_1]]
    big = [None] * len(big_w)

    def adamw_step(i, exchange=None):
        w, m, v = big_w[i], big_m[i], big_v[i]
        two_d = (-1, w.shape[-1])
        res = adamw_big(w.reshape(two_d), mine[i], theirs[i], m.reshape(two_d), v.reshape(two_d), "adamw_big_%d" % i,
                        exchange=exchange)
        big[i] = [r.reshape(w.shape) for r in res[:4]]
        return res[4:]

    (r_win_eb,) = adamw_step(4, exchange=scatter(g_win_eb))
    s_win_e = reduce_slabs([r_win_ea, r_win_eb], "reduce_in_even")
    (t_win_e,) = exchange_alone(_Swap([s_win_e]), "sibling_swap")
    mine[0], theirs[0] = [s_win_e], [t_win_e]
    for i in (5, 1, 2, 3, 0):
        adamw_step(i)

    small_w = [norm_mix_pre, norm_mix_post, norm_ffn_pre, norm_ffn_post, lb_table, a_norm, b_ln_g, b_ln_b, b_ws, b_bias]
    small_m = [m_norm_mix_pre, m_norm_mix_post, m_norm_ffn_pre, m_norm_ffn_post, m_lb_table, m_a_norm, m_b_ln_g,
               m_b_ln_b, m_b_ws, m_b_bias]
    small_v = [v_norm_mix_pre, v_norm_mix_post, v_norm_ffn_pre, v_norm_ffn_post, v_lb_table, v_a_norm, v_b_ln_g,
               v_b_ln_b, v_b_ws, v_b_bias]
    partial = [jnp.concatenate([dg_mpre0, dg_mpre1]), jnp.concatenate([dg_mpost0, dg_mpost1]),
               jnp.concatenate([dg_fpre0, dg_fpre1]), jnp.concatenate([dg_fpost0, dg_fpost1]),
               d_lb, d_anorm, d_lng, d_lnb, d_ws[None], d_bias_t.T[None]]
    *small_g, loss = _unpack(allreduce_small(_pack(partial + [loss_part]), "allreduce_small"),
                             [w.shape for w in small_w] + [()])
    small_d, small_nm, small_nv = adamw_small(small_w, small_g, small_m, small_v, "adamw_small")

    order = ["norm_mix_pre", "norm_mix_post", "norm_ffn_pre", "norm_ffn_post", "w_in_even", "lb_table", "a_norm",
             "b_ln_g", "b_ln_b", "b_ws", "b_bias", "w_out_even", "w_in_odd", "w_out_odd", "w_ff1", "w_ff2"]
    small_names = ["norm_mix_pre", "norm_mix_post", "norm_ffn_pre", "norm_ffn_post", "lb_table", "a_norm",
                   "b_ln_g", "b_ln_b", "b_ws", "b_bias"]
    big_names = ["w_in_even", "w_out_even", "w_in_odd", "w_out_odd", "w_ff1", "w_ff2"]
    grads, deltas, new_m, new_v = {}, {}, {}, {}
    for i, nm in enumerate(small_names):
        grads[nm], deltas[nm], new_m[nm], new_v[nm] = small_g[i], small_d[i], small_nm[i], small_nv[i]
    for i, nm in enumerate(big_names):
        grads[nm], deltas[nm], new_m[nm], new_v[nm] = big[i]
    return (loss, grad_x, *[grads[n] for n in order], *[deltas[n] for n in order],
            *[new_m[n] for n in order], *[new_v[n] for n in order])
```
